```python
import jax, jax.numpy as jnp
from jax import lax
import numpy as np

D_MODEL = 2048
BATCH = 8
SEQ = 2048
DEPTH = 2

EPS = 1e-6
NEG = -1e30
HEAD_DIM = 128
ROT_DIM = HEAD_DIM // 4
ROPE_THETA = 500000.0
DILATED_PATTERNS = ((128, 1), (512, 4), (2048, 16))
ATT_GROUPS = len(DILATED_PATTERNS)
HEADS_PER_GROUP = D_MODEL // (2 * HEAD_DIM)
ATT_QKV = ATT_GROUPS * HEADS_PER_GROUP * HEAD_DIM
ATT_OUT = HEADS_PER_GROUP * HEAD_DIM
POOL_WINDOWS = (2, 4, 8, 16)
POOL_GROUPS = len(POOL_WINDOWS)
POOL_WIDTH = D_MODEL // 2
POOL_CH = POOL_WIDTH // POOL_GROUPS
SGU_WIDTH = D_MODEL // 2
SGU_GROUPS = 4
SGU_CH = SGU_WIDTH // SGU_GROUPS
CHUNK = 128
CONV_WIDTH = D_MODEL // 2
CONV_K = 31
EVEN_COLS = (POOL_WIDTH, POOL_WIDTH, ATT_QKV, ATT_QKV, ATT_QKV, ATT_OUT)
ODD_COLS = (SGU_WIDTH, SGU_WIDTH, SGU_WIDTH, CONV_WIDTH, CONV_WIDTH, CONV_WIDTH)
EVEN_IN = sum(EVEN_COLS)
ODD_IN = sum(ODD_COLS)
MIX_OUT = POOL_WIDTH + ATT_OUT
N_EVEN = (DEPTH + 1) // 2
N_ODD = DEPTH // 2

kernel_name = "hybrid_pool_dilattn_sgu_conv"


def _split_points(cols):
    return [int(c) for c in np.cumsum(cols)[:-1]]


def rmsnorm(x, g):
    xf = x.astype(jnp.float32)
    y = xf * lax.rsqrt(jnp.mean(xf * xf, axis=-1, keepdims=True) + EPS) * g.astype(jnp.float32)
    return y.astype(x.dtype)


def layernorm(x, g, b):
    xf = x.astype(jnp.float32)
    mu = jnp.mean(xf, axis=-1, keepdims=True)
    var = jnp.mean(jnp.square(xf - mu), axis=-1, keepdims=True)
    y = (xf - mu) * lax.rsqrt(var + EPS) * g.astype(jnp.float32) + b.astype(jnp.float32)
    return y.astype(x.dtype)


def partial_rope(t, cos, sin):
    tf = t.astype(jnp.float32)
    half = ROT_DIM // 2
    t1, t2 = tf[..., :half], tf[..., half:ROT_DIM]
    c, s = cos[None, :, None, :], sin[None, :, None, :]
    out = jnp.concatenate([t1 * c - t2 * s, t2 * c + t1 * s, tf[..., ROT_DIM:]], axis=-1)
    return out.astype(t.dtype)


def causal_pool_mixer(xa, pool_w, pool_scale):
    B, S, _ = xa.shape
    xg = xa.reshape(B, S, POOL_GROUPS, POOL_CH).astype(jnp.float32)
    csp = jnp.concatenate([jnp.zeros((B, 1, POOL_GROUPS, POOL_CH), jnp.float32),
                           jnp.cumsum(xg, axis=1)], axis=1)
    t = jnp.arange(S)
    outs = []
    for g, w in enumerate(POOL_WINDOWS):
        upper = csp[:, 1:, g]
        lower = jnp.concatenate([jnp.zeros((B, w - 1, POOL_CH), jnp.float32),
                                 csp[:, :S + 1 - w, g]], axis=1)
        count = jnp.minimum(t + 1, w).astype(jnp.float32)[None, :, None]
        outs.append((upper - lower) / count - xg[:, :, g])
    pooled = jnp.stack(outs, axis=2).astype(xa.dtype)
    mixed = jnp.einsum('bsgc,gcd->bsgd', pooled, pool_w)
    return mixed.reshape(B, S, POOL_WIDTH) * pool_scale


def dilated_group(q, k, v, dilation, span):
    B, S, H, E = q.shape
    L = S // dilation
    nb = -(-L // span)
    Lp = nb * span

    def to_blocks(t):
        t = t.reshape(B, L, dilation, H, E)
        t = jnp.pad(t, ((0, 0), (0, Lp - L), (0, 0), (0, 0), (0, 0)))
        return t.reshape(B, nb, span, dilation, H, E)

    def with_prev(t):
        prev = jnp.pad(t, ((0, 0), (1, 0), (0, 0), (0, 0), (0, 0), (0, 0)))[:, :-1]
        return jnp.concatenate([prev, t], axis=2)

    qb = to_blocks(q)
    kk = with_prev(to_blocks(k))
    vv = with_prev(to_blocks(v))
    s = jnp.einsum('bnqrhe,bnkrhe->bnrhqk', qb, kk,
                   preferred_element_type=jnp.float32) * (HEAD_DIM ** -0.5)
    qi = jnp.arange(span)[:, None]
    kj = jnp.arange(2 * span)[None, :] - span
    dist = qi - kj
    blk = jnp.arange(nb)[:, None, None]
    valid = (dist >= 0)[None] & (dist <= span)[None] & (blk * span + kj[None] >= 0)
    s = jnp.where(valid[None, :, None, None], s, NEG)
    m = jnp.max(s, axis=-1, keepdims=True)
    p = jnp.exp(s - m)
    den = jnp.sum(p, axis=-1)
    o = jnp.einsum('bnrhqk,bnkrhe->bnqrhe', p.astype(vv.dtype), vv,
                   preferred_element_type=jnp.float32)
    den_t = jnp.transpose(den, (0, 1, 4, 2, 3))
    o = o / den_t[..., None]
    lse = jnp.transpose(m[..., 0], (0, 1, 4, 2, 3)) + jnp.log(den_t)
    o = o.reshape(B, Lp, dilation, H, E)[:, :L].reshape(B, S, H, E)
    lse = lse.reshape(B, Lp, dilation, H)[:, :L].reshape(B, S, H)
    return o, lse


def dilated_attention(q, k, v, cos, sin):
    B, S, _ = q.shape
    shp = (B, S, ATT_GROUPS * HEADS_PER_GROUP, HEAD_DIM)
    q = partial_rope(q.reshape(shp), cos, sin).reshape(B, S, ATT_GROUPS, HEADS_PER_GROUP, HEAD_DIM)
    k = partial_rope(k.reshape(shp), cos, sin).reshape(B, S, ATT_GROUPS, HEADS_PER_GROUP, HEAD_DIM)
    v = v.reshape(B, S, ATT_GROUPS, HEADS_PER_GROUP, HEAD_DIM)
    outs, lses = [], []
    for g, (window, dilation) in enumerate(DILATED_PATTERNS):
        o_g, lse_g = dilated_group(q[:, :, g], k[:, :, g], v[:, :, g], dilation, window // dilation)
        outs.append(o_g)
        lses.append(lse_g)
    wts = jax.nn.softmax(jnp.stack(lses, axis=0), axis=0)
    o = jnp.sum(wts[..., None] * jnp.stack(outs, axis=0), axis=0)
    return o.reshape(B, S, ATT_OUT).astype(q.dtype)


def chunked_sgu(u, v, g, b, w_s, b_s):
    B, S, _ = v.shape
    vn = layernorm(v, g, b).reshape(B, S // CHUNK, CHUNK, SGU_GROUPS, SGU_CH)
    mask = jnp.tril(jnp.ones((CHUNK, CHUNK), w_s.dtype))
    s = jnp.einsum('hij,bnjhc->bnihc', w_s * mask[None], vn) + b_s.T[None, None, :, :, None]
    return u * s.reshape(B, S, SGU_WIDTH)


def causal_depthwise_conv(x, w, b):
    out = lax.conv_general_dilated(x, w[:, None, :].astype(x.dtype), window_strides=(1,),
                                   padding=[(CONV_K - 1, 0)],
                                   dimension_numbers=('NWC', 'WIO', 'NWC'),
                                   feature_group_count=x.shape[-1])
    return out + b


def even_mixer(h, w_in, pool_w, pool_scale, w_out, cos, sin):
    z = h @ w_in
    a_in, a_gate, q, k, v, b_gate = jnp.split(z, _split_points(EVEN_COLS), axis=-1)
    ya = causal_pool_mixer(a_in, pool_w, pool_scale) * jax.nn.silu(a_gate)
    yb = dilated_attention(q, k, v, cos, sin) * jax.nn.silu(b_gate)
    return jnp.concatenate([ya, yb], axis=-1) @ w_out


def odd_mixer(h, w_in, sgu_g, sgu_b, sgu_w, sgu_bias, conv_w, conv_b, cn_g, cn_b, w_out):
    z = h @ w_in
    u, v, c_gate, d_val, d_glu, d_gate = jnp.split(z, _split_points(ODD_COLS), axis=-1)
    yc = chunked_sgu(u, v, sgu_g, sgu_b, sgu_w, sgu_bias) * jax.nn.silu(c_gate)
    d = d_val * jax.nn.sigmoid(d_glu)
    d = causal_depthwise_conv(d, conv_w, conv_b)
    d = jax.nn.silu(layernorm(d, cn_g, cn_b))
    yd = d * jax.nn.silu(d_gate)
    return jnp.concatenate([yc, yd], axis=-1) @ w_out


def _fwd_setup_inputs(seed: int = 0) -> dict:
    key = jax.random.key(seed)
    ks = jax.random.split(key, 20)
    f32 = jnp.float32

    def nrm(k, shape, scale):
        return jax.random.normal(k, shape, f32) * scale

    def gain(k, shape):
        return 1.0 + 0.05 * jax.random.normal(k, shape, f32)

    return {
        "x": jax.random.normal(ks[0], (BATCH, SEQ, D_MODEL), f32),
        "e_pre_norm": gain(ks[1], (N_EVEN, D_MODEL)),
        "e_w_in": nrm(ks[2], (N_EVEN, D_MODEL, EVEN_IN), D_MODEL ** -0.5),
        "e_pool_w": nrm(ks[3], (N_EVEN, POOL_GROUPS, POOL_CH, POOL_CH), POOL_CH ** -0.5),
        "e_pool_scale": gain(ks[4], (N_EVEN, POOL_WIDTH)),
        "e_w_out": nrm(ks[5], (N_EVEN, MIX_OUT, D_MODEL), MIX_OUT ** -0.5),
        "e_post_norm": gain(ks[6], (N_EVEN, D_MODEL)),
        "o_pre_norm": gain(ks[7], (N_ODD, D_MODEL)),
        "o_w_in": nrm(ks[8], (N_ODD, D_MODEL, ODD_IN), D_MODEL ** -0.5),
        "o_sgu_norm_g": gain(ks[9], (N_ODD, SGU_WIDTH)),
        "o_sgu_norm_b": nrm(ks[10], (N_ODD, SGU_WIDTH), 0.02),
        "o_sgu_w": nrm(ks[11], (N_ODD, SGU_GROUPS, CHUNK, CHUNK), CHUNK ** -0.5),
        "o_sgu_b": gain(ks[12], (N_ODD, SGU_GROUPS, CHUNK)),
        "o_conv_w": nrm(ks[13], (N_ODD, CONV_K, CONV_WIDTH), CONV_K ** -0.5),
        "o_conv_b": nrm(ks[14], (N_ODD, CONV_WIDTH), 0.02),
        "o_conv_norm_g": gain(ks[15], (N_ODD, CONV_WIDTH)),
        "o_conv_norm_b": nrm(ks[16], (N_ODD, CONV_WIDTH), 0.02),
        "o_w_out": nrm(ks[17], (N_ODD, MIX_OUT, D_MODEL), MIX_OUT ** -0.5),
        "o_post_norm": gain(ks[18], (N_ODD, D_MODEL)),
    }


def _fwd_reference(x, e_pre_norm, e_w_in, e_pool_w, e_pool_scale, e_w_out, e_post_norm,
              o_pre_norm, o_w_in, o_sgu_norm_g, o_sgu_norm_b, o_sgu_w, o_sgu_b,
              o_conv_w, o_conv_b, o_conv_norm_g, o_conv_norm_b, o_w_out, o_post_norm):
    S = x.shape[1]
    pos = jnp.arange(S, dtype=jnp.float32)
    inv_freq = jnp.power(ROPE_THETA, -jnp.arange(0, ROT_DIM, 2, dtype=jnp.float32) / ROT_DIM)
    ang = pos[:, None] * inv_freq[None, :]
    cos, sin = jnp.cos(ang), jnp.sin(ang)
    for i in range(DEPTH):
        j = i // 2
        if i % 2 == 0:
            h = rmsnorm(x, e_pre_norm[j])
            y = even_mixer(h, e_w_in[j], e_pool_w[j], e_pool_scale[j], e_w_out[j], cos, sin)
            x = x + rmsnorm(y, e_post_norm[j])
        else:
            h = rmsnorm(x, o_pre_norm[j])
            y = odd_mixer(h, o_w_in[j], o_sgu_norm_g[j], o_sgu_norm_b[j], o_sgu_w[j], o_sgu_b[j],
                          o_conv_w[j], o_conv_b[j], o_conv_norm_g[j], o_conv_norm_b[j], o_w_out[j])
            x = x + rmsnorm(y, o_post_norm[j])
    return x


import jax as _jax
import jax.numpy as _jnp

TWIN_FORMAT = 'train_step'
FWD_PARAMS = ['x', 'e_pre_norm', 'e_w_in', 'e_pool_w', 'e_pool_scale', 'e_w_out', 'e_post_norm', 'o_pre_norm', 'o_w_in', 'o_sgu_norm_g', 'o_sgu_norm_b', 'o_sgu_w', 'o_sgu_b', 'o_conv_w', 'o_conv_b', 'o_conv_norm_g', 'o_conv_norm_b', 'o_w_out', 'o_post_norm']
TWIN_WEIGHTS = ['e_pre_norm', 'e_w_in', 'e_pool_w', 'e_pool_scale', 'e_w_out', 'e_post_norm', 'o_pre_norm', 'o_w_in', 'o_sgu_norm_g', 'o_sgu_norm_b', 'o_sgu_w', 'o_sgu_b', 'o_conv_w', 'o_conv_b', 'o_conv_norm_g', 'o_conv_norm_b', 'o_w_out', 'o_post_norm']
TWIN_DIFF_INPUT = 'x'
TWIN_INPUTS = ['x', 'e_pre_norm', 'e_w_in', 'e_pool_w', 'e_pool_scale', 'e_w_out', 'e_post_norm', 'o_pre_norm', 'o_w_in', 'o_sgu_norm_g', 'o_sgu_norm_b', 'o_sgu_w', 'o_sgu_b', 'o_conv_w', 'o_conv_b', 'o_conv_norm_g', 'o_conv_norm_b', 'o_w_out', 'o_post_norm', 'loss_target', 'm_e_pre_norm', 'm_e_w_in', 'm_e_pool_w', 'm_e_pool_scale', 'm_e_w_out', 'm_e_post_norm', 'm_o_pre_norm', 'm_o_w_in', 'm_o_sgu_norm_g', 'm_o_sgu_norm_b', 'm_o_sgu_w', 'm_o_sgu_b', 'm_o_conv_w', 'm_o_conv_b', 'm_o_conv_norm_g', 'm_o_conv_norm_b', 'm_o_w_out', 'm_o_post_norm', 'v_e_pre_norm', 'v_e_w_in', 'v_e_pool_w', 'v_e_pool_scale', 'v_e_w_out', 'v_e_post_norm', 'v_o_pre_norm', 'v_o_w_in', 'v_o_sgu_norm_g', 'v_o_sgu_norm_b', 'v_o_sgu_w', 'v_o_sgu_b', 'v_o_conv_w', 'v_o_conv_b', 'v_o_conv_norm_g', 'v_o_conv_norm_b', 'v_o_w_out', 'v_o_post_norm']
TWIN_OUTPUTS = ['loss', 'grad_x', 'grad_e_pre_norm', 'grad_e_w_in', 'grad_e_pool_w', 'grad_e_pool_scale', 'grad_e_w_out', 'grad_e_post_norm', 'grad_o_pre_norm', 'grad_o_w_in', 'grad_o_sgu_norm_g', 'grad_o_sgu_norm_b', 'grad_o_sgu_w', 'grad_o_sgu_b', 'grad_o_conv_w', 'grad_o_conv_b', 'grad_o_conv_norm_g', 'grad_o_conv_norm_b', 'grad_o_w_out', 'grad_o_post_norm', 'delta_e_pre_norm', 'delta_e_w_in', 'delta_e_pool_w', 'delta_e_pool_scale', 'delta_e_w_out', 'delta_e_post_norm', 'delta_o_pre_norm', 'delta_o_w_in', 'delta_o_sgu_norm_g', 'delta_o_sgu_norm_b', 'delta_o_sgu_w', 'delta_o_sgu_b', 'delta_o_conv_w', 'delta_o_conv_b', 'delta_o_conv_norm_g', 'delta_o_conv_norm_b', 'delta_o_w_out', 'delta_o_post_norm', 'new_m_e_pre_norm', 'new_m_e_w_in', 'new_m_e_pool_w', 'new_m_e_pool_scale', 'new_m_e_w_out', 'new_m_e_post_norm', 'new_m_o_pre_norm', 'new_m_o_w_in', 'new_m_o_sgu_norm_g', 'new_m_o_sgu_norm_b', 'new_m_o_sgu_w', 'new_m_o_sgu_b', 'new_m_o_conv_w', 'new_m_o_conv_b', 'new_m_o_conv_norm_g', 'new_m_o_conv_norm_b', 'new_m_o_w_out', 'new_m_o_post_norm', 'new_v_e_pre_norm', 'new_v_e_w_in', 'new_v_e_pool_w', 'new_v_e_pool_scale', 'new_v_e_w_out', 'new_v_e_post_norm', 'new_v_o_pre_norm', 'new_v_o_w_in', 'new_v_o_sgu_norm_g', 'new_v_o_sgu_norm_b', 'new_v_o_sgu_w', 'new_v_o_sgu_b', 'new_v_o_conv_w', 'new_v_o_conv_b', 'new_v_o_conv_norm_g', 'new_v_o_conv_norm_b', 'new_v_o_w_out', 'new_v_o_post_norm']
TWIN_LEAF_KINDS = {'loss': 'loss', 'grad_x': 'grad_x', 'grad_e_pre_norm': 'grad_w', 'grad_e_w_in': 'grad_w', 'grad_e_pool_w': 'grad_w', 'grad_e_pool_scale': 'grad_w', 'grad_e_w_out': 'grad_w', 'grad_e_post_norm': 'grad_w', 'grad_o_pre_norm': 'grad_w', 'grad_o_w_in': 'grad_w', 'grad_o_sgu_norm_g': 'grad_w', 'grad_o_sgu_norm_b': 'grad_w', 'grad_o_sgu_w': 'grad_w', 'grad_o_sgu_b': 'grad_w', 'grad_o_conv_w': 'grad_w', 'grad_o_conv_b': 'grad_w', 'grad_o_conv_norm_g': 'grad_w', 'grad_o_conv_norm_b': 'grad_w', 'grad_o_w_out': 'grad_w', 'grad_o_post_norm': 'grad_w', 'delta_e_pre_norm': 'delta_w', 'delta_e_w_in': 'delta_w', 'delta_e_pool_w': 'delta_w', 'delta_e_pool_scale': 'delta_w', 'delta_e_w_out': 'delta_w', 'delta_e_post_norm': 'delta_w', 'delta_o_pre_norm': 'delta_w', 'delta_o_w_in': 'delta_w', 'delta_o_sgu_norm_g': 'delta_w', 'delta_o_sgu_norm_b': 'delta_w', 'delta_o_sgu_w': 'delta_w', 'delta_o_sgu_b': 'delta_w', 'delta_o_conv_w': 'delta_w', 'delta_o_conv_b': 'delta_w', 'delta_o_conv_norm_g': 'delta_w', 'delta_o_conv_norm_b': 'delta_w', 'delta_o_w_out': 'delta_w', 'delta_o_post_norm': 'delta_w', 'new_m_e_pre_norm': 'new_m', 'new_m_e_w_in': 'new_m', 'new_m_e_pool_w': 'new_m', 'new_m_e_pool_scale': 'new_m', 'new_m_e_w_out': 'new_m', 'new_m_e_post_norm': 'new_m', 'new_m_o_pre_norm': 'new_m', 'new_m_o_w_in': 'new_m', 'new_m_o_sgu_norm_g': 'new_m', 'new_m_o_sgu_norm_b': 'new_m', 'new_m_o_sgu_w': 'new_m', 'new_m_o_sgu_b': 'new_m', 'new_m_o_conv_w': 'new_m', 'new_m_o_conv_b': 'new_m', 'new_m_o_conv_norm_g': 'new_m', 'new_m_o_conv_norm_b': 'new_m', 'new_m_o_w_out': 'new_m', 'new_m_o_post_norm': 'new_m', 'new_v_e_pre_norm': 'new_v', 'new_v_e_w_in': 'new_v', 'new_v_e_pool_w': 'new_v', 'new_v_e_pool_scale': 'new_v', 'new_v_e_w_out': 'new_v', 'new_v_e_post_norm': 'new_v', 'new_v_o_pre_norm': 'new_v', 'new_v_o_w_in': 'new_v', 'new_v_o_sgu_norm_g': 'new_v', 'new_v_o_sgu_norm_b': 'new_v', 'new_v_o_sgu_w': 'new_v', 'new_v_o_sgu_b': 'new_v', 'new_v_o_conv_w': 'new_v', 'new_v_o_conv_b': 'new_v', 'new_v_o_conv_norm_g': 'new_v', 'new_v_o_conv_norm_b': 'new_v', 'new_v_o_w_out': 'new_v', 'new_v_o_post_norm': 'new_v'}


def _forward(args):
    return _fwd_reference(*[args[k] for k in FWD_PARAMS])


def _output_shape():
    out = _jax.eval_shape(lambda: _forward(_fwd_setup_inputs(0)))
    return out.shape, out.dtype

N_MICROBATCH = 1
ADAM_LR = 0.001
ADAM_B1 = 0.9
ADAM_B2 = 0.999
ADAM_EPS = 1e-08
ADAM_WD = 0.01
ADAM_STEP = 10
PER_EXAMPLE_BATCH_AXIS = {'x': 0, 'loss_target': 0}
SHARED_INPUTS = []
_WEIGHT_DTYPES = {'e_pre_norm': _jnp.float32, 'e_w_in': _jnp.float32, 'e_pool_w': _jnp.float32, 'e_pool_scale': _jnp.float32, 'e_w_out': _jnp.float32, 'e_post_norm': _jnp.float32, 'o_pre_norm': _jnp.float32, 'o_w_in': _jnp.float32, 'o_sgu_norm_g': _jnp.float32, 'o_sgu_norm_b': _jnp.float32, 'o_sgu_w': _jnp.float32, 'o_sgu_b': _jnp.float32, 'o_conv_w': _jnp.float32, 'o_conv_b': _jnp.float32, 'o_conv_norm_g': _jnp.float32, 'o_conv_norm_b': _jnp.float32, 'o_w_out': _jnp.float32, 'o_post_norm': _jnp.float32}
MOMENT_SCALE = {'e_pre_norm': 2.360882e-01, 'e_w_in': 9.685879e-02, 'e_pool_w': 2.350030e-01, 'e_pool_scale': 2.434856e-01, 'e_w_out': 1.740431e-01, 'e_post_norm': 8.004332e+00, 'o_pre_norm': 1.746552e-01, 'o_w_in': 1.035646e-01, 'o_sgu_norm_g': 7.760079e-02, 'o_sgu_norm_b': 7.918607e-02, 'o_sgu_w': 1.098343e-01, 'o_sgu_b': 1.493345e-01, 'o_conv_w': 7.304759e-02, 'o_conv_b': 2.167971e-01, 'o_conv_norm_g': 1.134688e-01, 'o_conv_norm_b': 1.299460e-01, 'o_w_out': 1.260855e-01, 'o_post_norm': 7.996222e+00}


def _to_microbatches(a, axis):
    t = _jnp.moveaxis(a, axis, 0)
    t = t.reshape((N_MICROBATCH, t.shape[0] // N_MICROBATCH) + t.shape[1:])
    return _jnp.moveaxis(t, 1, axis + 1)


def setup_inputs(seed: int = 0) -> dict:
    inp = _fwd_setup_inputs(seed)
    key = _jax.random.fold_in(_jax.random.key(seed), 7919)
    shape, _ = _output_shape()
    out = dict(inp)
    out["loss_target"] = _jax.random.normal(_jax.random.fold_in(key, 0), shape, _jnp.float32)
    for i, name in enumerate(TWIN_WEIGHTS):
        w = inp[name].astype(_jnp.float32)
        if MOMENT_SCALE is None:
            s = _jnp.sqrt(_jnp.mean(_jnp.square(w)) + 1e-30)
        else:
            s = MOMENT_SCALE[name]
        km, kv = _jax.random.split(_jax.random.fold_in(key, i + 1))
        out[name] = w
        out["m_" + name] = s * _jax.random.normal(km, w.shape, _jnp.float32)
        out["v_" + name] = (s * s) * _jax.random.uniform(kv, w.shape, _jnp.float32, 0.5, 1.5)
    if N_MICROBATCH > 1:
        for name, axis in PER_EXAMPLE_BATCH_AXIS.items():
            out[name] = _to_microbatches(out[name], axis)
    return {'x': out['x'], 'e_pre_norm': out['e_pre_norm'], 'e_w_in': out['e_w_in'], 'e_pool_w': out['e_pool_w'], 'e_pool_scale': out['e_pool_scale'], 'e_w_out': out['e_w_out'], 'e_post_norm': out['e_post_norm'], 'o_pre_norm': out['o_pre_norm'], 'o_w_in': out['o_w_in'], 'o_sgu_norm_g': out['o_sgu_norm_g'], 'o_sgu_norm_b': out['o_sgu_norm_b'], 'o_sgu_w': out['o_sgu_w'], 'o_sgu_b': out['o_sgu_b'], 'o_conv_w': out['o_conv_w'], 'o_conv_b': out['o_conv_b'], 'o_conv_norm_g': out['o_conv_norm_g'], 'o_conv_norm_b': out['o_conv_norm_b'], 'o_w_out': out['o_w_out'], 'o_post_norm': out['o_post_norm'], 'loss_target': out['loss_target'], 'm_e_pre_norm': out['m_e_pre_norm'], 'm_e_w_in': out['m_e_w_in'], 'm_e_pool_w': out['m_e_pool_w'], 'm_e_pool_scale': out['m_e_pool_scale'], 'm_e_w_out': out['m_e_w_out'], 'm_e_post_norm': out['m_e_post_norm'], 'm_o_pre_norm': out['m_o_pre_norm'], 'm_o_w_in': out['m_o_w_in'], 'm_o_sgu_norm_g': out['m_o_sgu_norm_g'], 'm_o_sgu_norm_b': out['m_o_sgu_norm_b'], 'm_o_sgu_w': out['m_o_sgu_w'], 'm_o_sgu_b': out['m_o_sgu_b'], 'm_o_conv_w': out['m_o_conv_w'], 'm_o_conv_b': out['m_o_conv_b'], 'm_o_conv_norm_g': out['m_o_conv_norm_g'], 'm_o_conv_norm_b': out['m_o_conv_norm_b'], 'm_o_w_out': out['m_o_w_out'], 'm_o_post_norm': out['m_o_post_norm'], 'v_e_pre_norm': out['v_e_pre_norm'], 'v_e_w_in': out['v_e_w_in'], 'v_e_pool_w': out['v_e_pool_w'], 'v_e_pool_scale': out['v_e_pool_scale'], 'v_e_w_out': out['v_e_w_out'], 'v_e_post_norm': out['v_e_post_norm'], 'v_o_pre_norm': out['v_o_pre_norm'], 'v_o_w_in': out['v_o_w_in'], 'v_o_sgu_norm_g': out['v_o_sgu_norm_g'], 'v_o_sgu_norm_b': out['v_o_sgu_norm_b'], 'v_o_sgu_w': out['v_o_sgu_w'], 'v_o_sgu_b': out['v_o_sgu_b'], 'v_o_conv_w': out['v_o_conv_w'], 'v_o_conv_b': out['v_o_conv_b'], 'v_o_conv_norm_g': out['v_o_conv_norm_g'], 'v_o_conv_norm_b': out['v_o_conv_norm_b'], 'v_o_w_out': out['v_o_w_out'], 'v_o_post_norm': out['v_o_post_norm']}


def _loss(weights, diff, rest, loss_target):
    with _jax.named_scope("forward"):
        args = {**rest, TWIN_DIFF_INPUT: diff, **{k: w.astype(_WEIGHT_DTYPES[k]) for k, w in weights.items()}}
        y = _forward(args)
    with _jax.named_scope("loss_head"):
        err = _jnp.square(y.astype(_jnp.float32) - loss_target)
        return 0.5 * _jnp.sum(_jnp.mean(err, axis=-1)) if err.ndim else 0.5 * err


def _adamw(w, g, m, v):
    m = ADAM_B1 * m + (1.0 - ADAM_B1) * g
    v = ADAM_B2 * v + (1.0 - ADAM_B2) * _jnp.square(g)
    m_hat = m / (1.0 - ADAM_B1 ** ADAM_STEP)
    v_hat = v / (1.0 - ADAM_B2 ** ADAM_STEP)
    delta = -ADAM_LR * (m_hat / (_jnp.sqrt(v_hat) + ADAM_EPS) + ADAM_WD * w)
    return delta, m, v


def reference(x, e_pre_norm, e_w_in, e_pool_w, e_pool_scale, e_w_out, e_post_norm, o_pre_norm, o_w_in, o_sgu_norm_g, o_sgu_norm_b, o_sgu_w, o_sgu_b, o_conv_w, o_conv_b, o_conv_norm_g, o_conv_norm_b, o_w_out, o_post_norm, loss_target, m_e_pre_norm, m_e_w_in, m_e_pool_w, m_e_pool_scale, m_e_w_out, m_e_post_norm, m_o_pre_norm, m_o_w_in, m_o_sgu_norm_g, m_o_sgu_norm_b, m_o_sgu_w, m_o_sgu_b, m_o_conv_w, m_o_conv_b, m_o_conv_norm_g, m_o_conv_norm_b, m_o_w_out, m_o_post_norm, v_e_pre_norm, v_e_w_in, v_e_pool_w, v_e_pool_scale, v_e_w_out, v_e_post_norm, v_o_pre_norm, v_o_w_in, v_o_sgu_norm_g, v_o_sgu_norm_b, v_o_sgu_w, v_o_sgu_b, v_o_conv_w, v_o_conv_b, v_o_conv_norm_g, v_o_conv_norm_b, v_o_w_out, v_o_post_norm):
    given = dict(x=x, e_pre_norm=e_pre_norm, e_w_in=e_w_in, e_pool_w=e_pool_w, e_pool_scale=e_pool_scale, e_w_out=e_w_out, e_post_norm=e_post_norm, o_pre_norm=o_pre_norm, o_w_in=o_w_in, o_sgu_norm_g=o_sgu_norm_g, o_sgu_norm_b=o_sgu_norm_b, o_sgu_w=o_sgu_w, o_sgu_b=o_sgu_b, o_conv_w=o_conv_w, o_conv_b=o_conv_b, o_conv_norm_g=o_conv_norm_g, o_conv_norm_b=o_conv_norm_b, o_w_out=o_w_out, o_post_norm=o_post_norm, loss_target=loss_target, m_e_pre_norm=m_e_pre_norm, m_e_w_in=m_e_w_in, m_e_pool_w=m_e_pool_w, m_e_pool_scale=m_e_pool_scale, m_e_w_out=m_e_w_out, m_e_post_norm=m_e_post_norm, m_o_pre_norm=m_o_pre_norm, m_o_w_in=m_o_w_in, m_o_sgu_norm_g=m_o_sgu_norm_g, m_o_sgu_norm_b=m_o_sgu_norm_b, m_o_sgu_w=m_o_sgu_w, m_o_sgu_b=m_o_sgu_b, m_o_conv_w=m_o_conv_w, m_o_conv_b=m_o_conv_b, m_o_conv_norm_g=m_o_conv_norm_g, m_o_conv_norm_b=m_o_conv_norm_b, m_o_w_out=m_o_w_out, m_o_post_norm=m_o_post_norm, v_e_pre_norm=v_e_pre_norm, v_e_w_in=v_e_w_in, v_e_pool_w=v_e_pool_w, v_e_pool_scale=v_e_pool_scale, v_e_w_out=v_e_w_out, v_e_post_norm=v_e_post_norm, v_o_pre_norm=v_o_pre_norm, v_o_w_in=v_o_w_in, v_o_sgu_norm_g=v_o_sgu_norm_g, v_o_sgu_norm_b=v_o_sgu_norm_b, v_o_sgu_w=v_o_sgu_w, v_o_sgu_b=v_o_sgu_b, v_o_conv_w=v_o_conv_w, v_o_conv_b=v_o_conv_b, v_o_conv_norm_g=v_o_conv_norm_g, v_o_conv_norm_b=v_o_conv_norm_b, v_o_w_out=v_o_w_out, v_o_post_norm=v_o_post_norm)
    weights = {n: given[n] for n in TWIN_WEIGHTS}
    shared = {n: given[n] for n in SHARED_INPUTS}
    per_example = {n: given[n] for n in ['x']}
    grad_fn = _jax.value_and_grad(_loss, argnums=(0, 1))

    def one_microbatch(ex, loss_target):
        ex = dict(ex)
        diff = ex.pop(TWIN_DIFF_INPUT)
        return grad_fn(weights, diff, {**shared, **ex}, loss_target)

    if N_MICROBATCH == 1:
        loss, (grad_w, grad_x) = one_microbatch(per_example, given["loss_target"])
    else:
        def body(carry, xs):
            loss_sum, grad_sum = carry
            l_k, (gw_k, gx_k) = one_microbatch(xs[0], xs[1])
            with _jax.named_scope("update"):
                return (loss_sum + l_k, _jax.tree.map(_jnp.add, grad_sum, gw_k)), gx_k

        init = (_jnp.zeros((), _jnp.float32), _jax.tree.map(_jnp.zeros_like, weights))
        (loss, grad_w), grad_x = _jax.lax.scan(body, init, (per_example, given["loss_target"]))
    with _jax.named_scope("update"):
        delta_w, new_m, new_v = {}, {}, {}
        for n in TWIN_WEIGHTS:
            delta_w[n], new_m[n], new_v[n] = _adamw(weights[n], grad_w[n], given["m_" + n], given["v_" + n])
    return (loss, grad_x, *[grad_w[n] for n in TWIN_WEIGHTS], *[delta_w[n] for n in TWIN_WEIGHTS],
            *[new_m[n] for n in TWIN_WEIGHTS], *[new_v[n] for n in TWIN_WEIGHTS])
```

```python
import jax
import jax.numpy as jnp
from jax import lax
from jax.experimental import pallas as pl
from jax.experimental.pallas import tpu as pltpu

f32 = jnp.float32
bf16 = jnp.bfloat16
SDS = jax.ShapeDtypeStruct

SEQ = 2048
D_MODEL = 2048
EPS = 1e-6
NEG = -1e30
HEAD_DIM = 128
ROT_HALF = 16
ROPE_THETA = 500000.0
DILATIONS = (1, 4, 16)
SPAN = 128
N_HEADS = 8
HALF = 1024
POOL_CH = 256
CONV_K = 31
CONV_PAD = 32
CHUNK = 128
N_CHIPS = 4
LANES = 256
ANY = pl.BlockSpec(memory_space=pl.ANY)
MESH = pl.DeviceIdType.MESH

ADAM_LR = 0.001
ADAM_B1 = 0.9
ADAM_B2 = 0.999
ADAM_EPS = 1e-08
ADAM_WD = 0.01
ADAM_STEP = 10


def _dot(a, b):
    return jnp.dot(a, b, preferred_element_type=f32)


def _dot_nt(a, b):
    return lax.dot_general(a, b, (((1,), (1,)), ((), ())), preferred_element_type=f32)


def _dot_tn(a, b):
    return lax.dot_general(a, b, (((0,), (0,)), ((), ())), preferred_element_type=f32)


def _sigmoid(x):
    return 1.0 / (1.0 + jnp.exp(-x))


def _silu_and_grad(x):
    s = _sigmoid(x)
    return x * s, s * (1.0 + x * (1.0 - s))


def _rms_fwd(x, g):
    r = lax.rsqrt(jnp.mean(x * x, axis=-1, keepdims=True) + EPS)
    return x * r * g


def _rms_bwd(x, g, dout):
    r = lax.rsqrt(jnp.mean(x * x, axis=-1, keepdims=True) + EPS)
    xh = x * r
    dg = jnp.sum(dout * xh, axis=0, keepdims=True)
    dxh = dout * g
    dx = r * (dxh - xh * jnp.mean(dxh * xh, axis=-1, keepdims=True))
    return dx, dg


def _ln_stats(x):
    mu = jnp.mean(x, axis=-1, keepdims=True)
    xc = x - mu
    rstd = lax.rsqrt(jnp.mean(xc * xc, axis=-1, keepdims=True) + EPS)
    return xc * rstd, rstd


def _ln_bwd(xh, rstd, g, dout):
    dg = jnp.sum(dout * xh, axis=0, keepdims=True)
    db = jnp.sum(dout, axis=0, keepdims=True)
    dxh = dout * g
    dx = rstd * (dxh - jnp.mean(dxh, axis=-1, keepdims=True) - xh * jnp.mean(dxh * xh, axis=-1, keepdims=True))
    return dx, dg, db


def _accumulate(ref, value, first):
    @pl.when(first)
    def _():
        ref[...] = value

    @pl.when(jnp.logical_not(first))
    def _():
        ref[...] += value


def _col_tile(ns):
    for t in (1024, 768, 512, 256):
        if ns % t == 0:
            return t
    raise ValueError(ns)


def _mm_nn(a, w, out_dtype, name):
    m, k = a.shape
    j, _, ns = w.shape
    tm, tn = 1024, _col_tile(ns)
    nb = ns // tn

    def body(a_ref, w_ref, o_ref):
        o_ref[...] = _dot(a_ref[...], w_ref[...]).astype(o_ref.dtype)

    return pl.pallas_call(
        body, name=name, grid=(j * nb, m // tm),
        in_specs=[pl.BlockSpec((tm, k), lambda n, i: (i, 0)),
                  pl.BlockSpec((None, k, tn), lambda n, i: (n // nb, 0, n % nb))],
        out_specs=pl.BlockSpec((tm, tn), lambda n, i: (i, n)),
        out_shape=SDS((m, j * ns), out_dtype),
    )(a, w)


def _mm_nt(dz, w, out_dtype, name):
    m, _ = dz.shape
    j, k, ns = w.shape
    tm, tk, tn = 1024, 1024, _col_tile(ns)
    nb = ns // tn
    steps = j * nb

    def body(dz_ref, w_ref, o_ref, acc_ref):
        r = pl.program_id(2)
        _accumulate(acc_ref, _dot_nt(dz_ref[...], w_ref[...]), r == 0)

        @pl.when(r == steps - 1)
        def _():
            o_ref[...] = acc_ref[...].astype(o_ref.dtype)

    return pl.pallas_call(
        body, name=name, grid=(m // tm, k // tk, steps),
        in_specs=[pl.BlockSpec((tm, tn), lambda i, kk, r: (i, r)),
                  pl.BlockSpec((None, tk, tn), lambda i, kk, r: (r // nb, kk, r % nb))],
        out_specs=pl.BlockSpec((tm, tk), lambda i, kk, r: (i, kk)),
        out_shape=SDS((m, k), out_dtype),
        scratch_shapes=[pltpu.VMEM((tm, tk), f32)],
    )(dz, w)


def _mm_tn(a, dz, j, name):
    m, k = a.shape
    ns = dz.shape[1] // j
    tk, tn = 1024, _col_tile(ns)
    nb = ns // tn

    def body(a_ref, dz_ref, o_ref):
        o_ref[...] = _dot_tn(a_ref[...], dz_ref[...]).astype(o_ref.dtype)

    return pl.pallas_call(
        body, name=name, grid=(k // tk, j * nb),
        in_specs=[pl.BlockSpec((m, tk), lambda kk, n: (0, kk)),
                  pl.BlockSpec((m, tn), lambda kk, n: (0, n))],
        out_specs=pl.BlockSpec((None, tk, tn), lambda kk, n: (n // nb, kk, n % nb)),
        out_shape=SDS((j, k, ns), bf16),
    )(a, dz)


ROWS = 256


def _row_spec(width=D_MODEL, col=0):
    return pl.BlockSpec((ROWS, width), lambda i: (i, col))


def _vec_spec(width=D_MODEL):
    return pl.BlockSpec((1, width), lambda i: (0, 0))


def _pre_norm(x, g):
    def body(x_ref, g_ref, h_ref):
        h_ref[...] = _rms_fwd(x_ref[...], g_ref[...]).astype(bf16)

    return pl.pallas_call(
        body, name="pre_norm", grid=(SEQ // ROWS,), in_specs=[_row_spec(), _vec_spec()],
        out_specs=_row_spec(), out_shape=SDS((SEQ, D_MODEL), bf16))(x, g)


def _mid_norm(x, y, g_post, g_pre):
    def body(x_ref, y_ref, gpost_ref, gpre_ref, x1_ref, h1_ref):
        x1 = x_ref[...] + _rms_fwd(y_ref[...], gpost_ref[...])
        x1_ref[...] = x1
        h1_ref[...] = _rms_fwd(x1, gpre_ref[...]).astype(bf16)

    return pl.pallas_call(
        body, name="mid_norm", grid=(SEQ // ROWS,),
        in_specs=[_row_spec(), _row_spec(), _vec_spec(), _vec_spec()],
        out_specs=[_row_spec(), _row_spec()],
        out_shape=[SDS((SEQ, D_MODEL), f32), SDS((SEQ, D_MODEL), bf16)])(x, y, g_post, g_pre)


def _final_norm_loss(x1, y, g_post, target):
    def body(x1_ref, y_ref, g_ref, t_ref, loss_ref, dx2_ref, dy_ref, dg_ref):
        first = pl.program_id(0) == 0
        y = y_ref[...]
        g = g_ref[...]
        err = x1_ref[...] + _rms_fwd(y, g) - t_ref[...]
        sq = jnp.sum(jnp.sum(err * err, axis=1, keepdims=True), axis=0, keepdims=True)
        _accumulate(loss_ref, sq * (0.5 / D_MODEL), first)
        dx2 = err * (1.0 / D_MODEL)
        dx2_ref[...] = dx2
        dy, dg = _rms_bwd(y, g, dx2)
        dy_ref[...] = dy.astype(bf16)
        _accumulate(dg_ref, dg, first)

    return pl.pallas_call(
        body, name="final_norm_loss", grid=(SEQ // ROWS,),
        in_specs=[_row_spec(), _row_spec(), _vec_spec(), _row_spec()],
        out_specs=[pl.BlockSpec((1, 1), lambda i: (0, 0)), _row_spec(), _row_spec(), _vec_spec()],
        out_shape=[SDS((1, 1), f32), SDS((SEQ, D_MODEL), f32), SDS((SEQ, D_MODEL), bf16), SDS((1, D_MODEL), f32)],
    )(x1, y, g_post, target)


def _mid_norm_bwd(dx2, dh1, x1, y0, g_pre, g_post):
    def body(dx2_ref, dh1_ref, x1_ref, y0_ref, gpre_ref, gpost_ref, dx1_ref, dy0_ref, dgpre_ref, dgpost_ref):
        first = pl.program_id(0) == 0
        d_in, dgpre = _rms_bwd(x1_ref[...], gpre_ref[...], dh1_ref[...])
        dx1 = dx2_ref[...] + d_in
        dx1_ref[...] = dx1
        dy0, dgpost = _rms_bwd(y0_ref[...], gpost_ref[...], dx1)
        dy0_ref[...] = dy0.astype(bf16)
        _accumulate(dgpre_ref, dgpre, first)
        _accumulate(dgpost_ref, dgpost, first)

    return pl.pallas_call(
        body, name="mid_norm_bwd", grid=(SEQ // ROWS,),
        in_specs=[_row_spec(), _row_spec(), _row_spec(), _row_spec(), _vec_spec(), _vec_spec()],
        out_specs=[_row_spec(), _row_spec(), _vec_spec(), _vec_spec()],
        out_shape=[SDS((SEQ, D_MODEL), f32), SDS((SEQ, D_MODEL), bf16), SDS((1, D_MODEL), f32), SDS((1, D_MODEL), f32)],
    )(dx2, dh1, x1, y0, g_pre, g_post)


def _pre_norm_bwd(dx1, dh0, x, g):
    def body(dx1_ref, dh0_ref, x_ref, g_ref, dx_ref, dg_ref):
        d_in, dg = _rms_bwd(x_ref[...], g_ref[...], dh0_ref[...])
        dx_ref[...] = dx1_ref[...] + d_in
        _accumulate(dg_ref, dg, pl.program_id(0) == 0)

    return pl.pallas_call(
        body, name="pre_norm_bwd", grid=(SEQ // ROWS,),
        in_specs=[_row_spec(), _row_spec(), _row_spec(), _vec_spec()],
        out_specs=[_row_spec(), _vec_spec()],
        out_shape=[SDS((SEQ, D_MODEL), f32), SDS((1, D_MODEL), f32)])(dx1, dh0, x, g)


def _pool_count(g):
    row = lax.broadcasted_iota(jnp.int32, (SEQ, 1), 0)
    width = jnp.left_shift(2, g)
    return row, width, jnp.minimum(row + 1, width).astype(f32)


def _trailing_sum(x, row, width):
    s = x
    for k in (1, 2, 4, 8):
        shifted = jnp.where(row >= k, pltpu.roll(s, k, 0), 0.0)
        s = jnp.where(width > k, s + shifted, s)
    return s


def _leading_sum(x, row, width):
    s = x
    for k in (1, 2, 4, 8):
        shifted = jnp.where(row < SEQ - k, pltpu.roll(s, SEQ - k, 0), 0.0)
        s = jnp.where(width > k, s + shifted, s)
    return s


def _pool_specs():
    a_in = pl.BlockSpec((SEQ, POOL_CH), lambda g: (0, g))
    a_gate = pl.BlockSpec((SEQ, POOL_CH), lambda g: (0, 4 + g))
    w = pl.BlockSpec((None, POOL_CH, POOL_CH), lambda g: (g, 0, 0))
    scale = pl.BlockSpec((1, POOL_CH), lambda g: (0, g))
    return a_in, a_gate, w, scale


def _pool_fwd(z0, pool_w, pool_scale):
    def body(a_ref, gate_ref, w_ref, scale_ref, ya_ref):
        row, width, count = _pool_count(pl.program_id(0))
        a = a_ref[...]
        pooled = _trailing_sum(a, row, width) / count - a
        mixed = _dot(pooled.astype(bf16), w_ref[...]) * scale_ref[...]
        gate = gate_ref[...]
        ya_ref[...] = (mixed * gate * _sigmoid(gate)).astype(bf16)

    return pl.pallas_call(
        body, name="pool_fwd", grid=(4,), in_specs=list(_pool_specs()),
        out_specs=pl.BlockSpec((SEQ, POOL_CH), lambda g: (0, g)),
        out_shape=SDS((SEQ, HALF), bf16))(z0, z0, pool_w, pool_scale)


def _pool_bwd(z0, dcat, pool_w, pool_scale):
    def body(a_ref, gate_ref, w_ref, scale_ref, dya_ref, da_ref, dgate_ref, dw_ref, dscale_ref):
        row, width, count = _pool_count(pl.program_id(0))
        a = a_ref[...]
        pooled = (_trailing_sum(a, row, width) / count - a).astype(bf16)
        w = w_ref[...]
        scale = scale_ref[...]
        mixed = _dot(pooled, w)
        silu, dsilu = _silu_and_grad(gate_ref[...])
        dya = dya_ref[...]
        dgate_ref[...] = (dya * mixed * scale * dsilu).astype(bf16)
        dms = dya * silu
        dscale_ref[...] = jnp.sum(dms * mixed, axis=0, keepdims=True)
        dmixed = (dms * scale).astype(bf16)
        dw_ref[...] = _dot_tn(pooled, dmixed)
        dpooled = _dot_nt(dmixed, w)
        da_ref[...] = (_leading_sum(dpooled / count, row, width) - dpooled).astype(bf16)

    a_in, a_gate, w, scale = _pool_specs()
    col = pl.BlockSpec((SEQ, POOL_CH), lambda g: (0, g))
    return pl.pallas_call(
        body, name="pool_bwd", grid=(4,), in_specs=[a_in, a_gate, w, scale, col],
        out_specs=[col, col, w, scale],
        out_shape=[SDS((SEQ, HALF), bf16), SDS((SEQ, HALF), bf16), SDS((4, POOL_CH, POOL_CH), f32), SDS((1, HALF), f32)],
    )(z0, z0, pool_w, pool_scale, dcat)


Q_COL, K_COL, V_COL, BGATE_COL = 16, 40, 64, 88


def _rope_tables():
    pos = jnp.arange(SEQ, dtype=f32)
    inv_freq = jnp.power(ROPE_THETA, -jnp.arange(0, 2 * ROT_HALF, 2, dtype=f32) / (2 * ROT_HALF))
    ang = pos[:, None] * inv_freq[None, :]
    cos, sin = jnp.cos(ang), jnp.sin(ang)
    zeros = jnp.zeros((SEQ, HEAD_DIM - 2 * ROT_HALF), f32)
    zero_half = jnp.zeros((SEQ, ROT_HALF), f32)
    cos_t = jnp.concatenate([cos, cos, zeros + 1.0], axis=1)
    sin_a = jnp.concatenate([-sin, zero_half, zeros], axis=1)
    sin_b = jnp.concatenate([zero_half, sin, zeros], axis=1)
    return cos_t, sin_a, sin_b


def _rope(t, cos_t, sin_a, sin_b):
    return t * cos_t + pltpu.roll(t, HEAD_DIM - ROT_HALF, 1) * sin_a + pltpu.roll(t, ROT_HALF, 1) * sin_b


def _rope_transposed(d, cos_t, sin_a, sin_b):
    return d * cos_t + pltpu.roll(d * sin_a, ROT_HALF, 1) + pltpu.roll(d * sin_b, HEAD_DIM - ROT_HALF, 1)


def _by_residue(dst_ref, src_ref, dilation, dtype):
    if dilation == 1:
        dst_ref[...] = src_ref[...].astype(dtype)
        return
    length = SEQ // dilation
    for r in range(dilation):
        dst_ref[r * length:(r + 1) * length, :] = src_ref[pl.ds(r, length, stride=dilation), :].astype(dtype)


def _by_position(dst_ref, src_ref, dilation):
    if dilation == 1:
        dst_ref[...] = src_ref[...]
        return
    length = SEQ // dilation
    for r in range(dilation):
        dst_ref[pl.ds(r, length, stride=dilation), :] = src_ref[r * length:(r + 1) * length, :]


def _band_masks():
    qi = lax.broadcasted_iota(jnp.int32, (SPAN, SPAN), 0)
    kj = lax.broadcasted_iota(jnp.int32, (SPAN, SPAN), 1)
    return kj <= qi, kj - qi


def _head_spec(col):
    return pl.BlockSpec((SEQ, HEAD_DIM), lambda h: (0, col + h))


def _table_spec():
    return pl.BlockSpec((SEQ, HEAD_DIM), lambda h: (0, 0))


def _attn_fwd(z0, tables):
    scale = HEAD_DIM ** -0.5

    def body(*refs):
        qkv = refs[0:9]
        bg_ref, cos_ref, sa_ref, sb_ref = refs[9:13]
        yb_ref, att_ref, lse_ref = refs[13:16]
        qd, kd, vd, tmp, o_res, l_res, o_pos, l_pos = refs[16:24]
        own_mask, prev_off = _band_masks()
        cos_t, sin_a, sin_b = cos_ref[...], sa_ref[...], sb_ref[...]
        for g, dilation in enumerate(DILATIONS):
            q_ref, k_ref, v_ref = qkv[3 * g:3 * g + 3]
            blocks_per_residue = SEQ // dilation // SPAN
            tmp[...] = _rope(q_ref[...], cos_t, sin_a, sin_b)
            _by_residue(qd, tmp, dilation, bf16)
            tmp[...] = _rope(k_ref[...], cos_t, sin_a, sin_b)
            _by_residue(kd, tmp, dilation, bf16)
            _by_residue(vd, v_ref, dilation, bf16)

            def block(c, carry, dilation=dilation, blocks_per_residue=blocks_per_residue):
                r0 = pl.multiple_of(c * SPAN, SPAN)
                q = qd[pl.ds(r0, SPAN), :]
                s = jnp.where(own_mask, _dot_nt(q, kd[pl.ds(r0, SPAN), :]) * scale, NEG)
                m = jnp.max(s, axis=1, keepdims=True)
                if blocks_per_residue > 1:
                    p0 = pl.multiple_of(jnp.maximum(c - 1, 0) * SPAN, SPAN)
                    first_off = jnp.where(c % blocks_per_residue > 0, 0, SPAN)
                    sp = jnp.where(prev_off >= first_off, _dot_nt(q, kd[pl.ds(p0, SPAN), :]) * scale, NEG)
                    m = jnp.maximum(m, jnp.max(sp, axis=1, keepdims=True))
                    pp = jnp.exp(sp - m)
                p = jnp.exp(s - m)
                den = jnp.sum(p, axis=1, keepdims=True)
                o = _dot(p.astype(bf16), vd[pl.ds(r0, SPAN), :])
                if blocks_per_residue > 1:
                    den = den + jnp.sum(pp, axis=1, keepdims=True)
                    o = o + _dot(pp.astype(bf16), vd[pl.ds(p0, SPAN), :])
                o_res[pl.ds(r0, SPAN), :] = o / den
                l_res[pl.ds(r0, SPAN), :] = jnp.broadcast_to(m + jnp.log(den), (SPAN, HEAD_DIM))
                return carry

            lax.fori_loop(0, SEQ // SPAN, block, 0)
            _by_position(o_pos.at[g], o_res, dilation)
            _by_position(l_pos.at[g], l_res, dilation)

        l0, l1, l2 = l_pos[0], l_pos[1], l_pos[2]
        top = jnp.maximum(jnp.maximum(l0, l1), l2)
        total = top + jnp.log(jnp.exp(l0 - top) + jnp.exp(l1 - top) + jnp.exp(l2 - top))
        att = jnp.exp(l0 - total) * o_pos[0] + jnp.exp(l1 - total) * o_pos[1] + jnp.exp(l2 - total) * o_pos[2]
        att_ref[...] = att
        lse_ref[...] = total
        gate = bg_ref[...]
        yb_ref[...] = (att * gate * _sigmoid(gate)).astype(bf16)

    in_specs = []
    for g in range(3):
        in_specs += [_head_spec(Q_COL + 8 * g), _head_spec(K_COL + 8 * g), _head_spec(V_COL + 8 * g)]
    in_specs += [_head_spec(BGATE_COL), _table_spec(), _table_spec(), _table_spec()]
    out_spec = pl.BlockSpec((SEQ, HEAD_DIM), lambda h: (0, h))
    vm = lambda dt: pltpu.VMEM((SEQ, HEAD_DIM), dt)
    return pl.pallas_call(
        body, name="attn_fwd", grid=(N_HEADS,), in_specs=in_specs, out_specs=[out_spec] * 3,
        out_shape=[SDS((SEQ, HALF), bf16), SDS((SEQ, HALF), f32), SDS((SEQ, HALF), f32)],
        scratch_shapes=[vm(bf16), vm(bf16), vm(bf16), vm(f32), vm(f32), vm(f32),
                        pltpu.VMEM((3, SEQ, HEAD_DIM), f32), pltpu.VMEM((3, SEQ, HEAD_DIM), f32)],
    )(*([z0] * 10), *tables)


def _attn_bwd_group(g, z0, att, lse, dcat, tables):
    scale = HEAD_DIM ** -0.5
    dilation = DILATIONS[g]
    blocks_per_residue = SEQ // dilation // SPAN
    with_gate = g == 0

    def body(*refs):
        q_ref, k_ref, v_ref, bg_ref, att_ref, lse_ref, dyb_ref, cos_ref, sa_ref, sb_ref = refs[0:10]
        n_out = 4 if with_gate else 3
        dq_ref, dk_ref, dv_ref = refs[10:13]
        qd, kd, vd, dod, ld, dd, tmp, aq, ak, av = refs[10 + n_out:20 + n_out]
        own_mask, prev_off = _band_masks()
        cos_t, sin_a, sin_b = cos_ref[...], sa_ref[...], sb_ref[...]
        gate = bg_ref[...]
        silu, dsilu = _silu_and_grad(gate)
        att_v = att_ref[...]
        dyb = dyb_ref[...]
        if with_gate:
            refs[13][...] = (dyb * att_v * dsilu).astype(bf16)
        datt = dyb * silu
        tmp[...] = datt
        _by_residue(dod, tmp, dilation, bf16)
        tmp[...] = jnp.broadcast_to(jnp.sum(datt * att_v, axis=1, keepdims=True), (SEQ, HEAD_DIM))
        _by_residue(dd, tmp, dilation, f32)
        _by_residue(ld, lse_ref, dilation, f32)
        tmp[...] = _rope(q_ref[...], cos_t, sin_a, sin_b)
        _by_residue(qd, tmp, dilation, bf16)
        tmp[...] = _rope(k_ref[...], cos_t, sin_a, sin_b)
        _by_residue(kd, tmp, dilation, bf16)
        _by_residue(vd, v_ref, dilation, bf16)
        ak[...] = jnp.zeros((SEQ, HEAD_DIM), f32)
        av[...] = jnp.zeros((SEQ, HEAD_DIM), f32)

        def block(c, carry):
            r0 = pl.multiple_of(c * SPAN, SPAN)
            rows = pl.ds(r0, SPAN)
            q, k, v, do = qd[rows, :], kd[rows, :], vd[rows, :], dod[rows, :]
            lse_q, delta = ld[rows, :], dd[rows, :]
            p = jnp.where(own_mask, jnp.exp(_dot_nt(q, k) * scale - lse_q), 0.0)
            ds = (p * (_dot_nt(do, v) - delta) * scale).astype(bf16)
            av[rows, :] += _dot_tn(p.astype(bf16), do)
            ak[rows, :] += _dot_tn(ds, q)
            dq = _dot(ds, k)
            if blocks_per_residue > 1:
                p0 = pl.multiple_of(jnp.maximum(c - 1, 0) * SPAN, SPAN)
                prev = pl.ds(p0, SPAN)
                kp, vp = kd[prev, :], vd[prev, :]
                first_off = jnp.where(c % blocks_per_residue > 0, 0, SPAN)
                pp = jnp.where(prev_off >= first_off, jnp.exp(_dot_nt(q, kp) * scale - lse_q), 0.0)
                dsp = (pp * (_dot_nt(do, vp) - delta) * scale).astype(bf16)
                av[prev, :] += _dot_tn(pp.astype(bf16), do)
                ak[prev, :] += _dot_tn(dsp, q)
                dq = dq + _dot(dsp, kp)
            aq[rows, :] = dq
            return carry

        lax.fori_loop(0, SEQ // SPAN, block, 0)
        _by_position(tmp, aq, dilation)
        dq_ref[...] = _rope_transposed(tmp[...], cos_t, sin_a, sin_b).astype(bf16)
        _by_position(tmp, ak, dilation)
        dk_ref[...] = _rope_transposed(tmp[...], cos_t, sin_a, sin_b).astype(bf16)
        _by_position(tmp, av, dilation)
        dv_ref[...] = tmp[...].astype(bf16)

    head = pl.BlockSpec((SEQ, HEAD_DIM), lambda h: (0, h))
    in_specs = [_head_spec(Q_COL + 8 * g), _head_spec(K_COL + 8 * g), _head_spec(V_COL + 8 * g), _head_spec(BGATE_COL),
                head, head, _head_spec(8), _table_spec(), _table_spec(), _table_spec()]
    n_out = 4 if with_gate else 3
    vm = lambda dt: pltpu.VMEM((SEQ, HEAD_DIM), dt)
    return pl.pallas_call(
        body, name=f"attn_bwd_g{g}", grid=(N_HEADS,), in_specs=in_specs, out_specs=[head] * n_out,
        out_shape=[SDS((SEQ, HALF), bf16)] * n_out,
        scratch_shapes=[vm(bf16), vm(bf16), vm(bf16), vm(bf16), vm(f32), vm(f32), vm(f32), vm(f32), vm(f32), vm(f32)],
    )(z0, z0, z0, z0, att, lse, dcat, *tables)


def _sgu_specs():
    chunk = lambda col: pl.BlockSpec((CHUNK, HALF), lambda n: (n, col))
    vec = pl.BlockSpec((1, HALF), lambda n: (0, 0))
    w = pl.BlockSpec((4, CHUNK, CHUNK), lambda n: (0, 0, 0))
    bias = pl.BlockSpec((CHUNK, CHUNK), lambda n: (0, 0))
    return chunk, vec, w, bias


def _sgu_weights(w_ref):
    tril = lax.broadcasted_iota(jnp.int32, (CHUNK, CHUNK), 1) <= lax.broadcasted_iota(jnp.int32, (CHUNK, CHUNK), 0)
    return tril, [jnp.where(tril, w_ref[h], 0.0).astype(bf16) for h in range(4)]


def _sgu_fwd(z1, ln_g, ln_b, sgu_w, bias_t):
    def body(u_ref, v_ref, cg_ref, g_ref, b_ref, w_ref, bias_ref, yc_ref):
        _, ws = _sgu_weights(w_ref)
        xh, _ = _ln_stats(v_ref[...])
        vn = (xh * g_ref[...] + b_ref[...]).astype(bf16)
        for h in range(4):
            cols = slice(h * POOL_CH, (h + 1) * POOL_CH)
            s = _dot(ws[h], vn[:, cols]) + bias_ref[:, h:h + 1]
            gate = cg_ref[:, cols]
            yc_ref[:, cols] = (u_ref[:, cols] * s * gate * _sigmoid(gate)).astype(bf16)

    chunk, vec, w, bias = _sgu_specs()
    return pl.pallas_call(
        body, name="sgu_fwd", grid=(SEQ // CHUNK,),
        in_specs=[chunk(0), chunk(1), chunk(2), vec, vec, w, bias], out_specs=chunk(0),
        out_shape=SDS((SEQ, HALF), bf16))(z1, z1, z1, ln_g, ln_b, sgu_w, bias_t)


def _sgu_bwd(z1, dcat, ln_g, ln_b, sgu_w, bias_t):
    def body(u_ref, v_ref, cg_ref, dyc_ref, g_ref, b_ref, w_ref, bias_ref,
             du_ref, dv_ref, dcg_ref, dw_ref, dbias_ref, dg_ref, db_ref, dvn_ref):
        first = pl.program_id(0) == 0
        tril, ws = _sgu_weights(w_ref)
        xh, rstd = _ln_stats(v_ref[...])
        g = g_ref[...]
        vn = (xh * g + b_ref[...]).astype(bf16)

        @pl.when(first)
        def _():
            dbias_ref[...] = jnp.zeros((CHUNK, CHUNK), f32)

        for h in range(4):
            cols = slice(h * POOL_CH, (h + 1) * POOL_CH)
            vn_h = vn[:, cols]
            s = _dot(ws[h], vn_h) + bias_ref[:, h:h + 1]
            silu, dsilu = _silu_and_grad(cg_ref[:, cols])
            dyc = dyc_ref[:, cols]
            u = u_ref[:, cols]
            du_ref[:, cols] = (dyc * s * silu).astype(bf16)
            dcg_ref[:, cols] = (dyc * u * s * dsilu).astype(bf16)
            ds = dyc * u * silu
            dbias_ref[:, h:h + 1] += jnp.sum(ds, axis=1, keepdims=True)
            ds = ds.astype(bf16)
            _accumulate(dw_ref.at[h], jnp.where(tril, _dot_nt(ds, vn_h), 0.0), first)
            dvn_ref[:, cols] = _dot_tn(ws[h], ds)
        dv, dg, db = _ln_bwd(xh, rstd, g, dvn_ref[...])
        dv_ref[...] = dv.astype(bf16)
        _accumulate(dg_ref, dg, first)
        _accumulate(db_ref, db, first)

    chunk, vec, w, bias = _sgu_specs()
    return pl.pallas_call(
        body, name="sgu_bwd", grid=(SEQ // CHUNK,),
        in_specs=[chunk(0), chunk(1), chunk(2), chunk(0), vec, vec, w, bias],
        out_specs=[chunk(0), chunk(0), chunk(0), w, bias, vec, vec],
        out_shape=[SDS((SEQ, HALF), bf16)] * 3 + [SDS((4, CHUNK, CHUNK), f32), SDS((CHUNK, CHUNK), f32),
                                                   SDS((1, HALF), f32), SDS((1, HALF), f32)],
        scratch_shapes=[pltpu.VMEM((CHUNK, HALF), f32)],
    )(z1, z1, z1, dcat, ln_g, ln_b, sgu_w, bias_t)


CONV_TILE = 128
DVAL_COL, DGLU_COL = 12, 16


def _conv_specs():
    val = pl.BlockSpec((SEQ, POOL_CH), lambda j: (0, DVAL_COL + j))
    glu = pl.BlockSpec((SEQ, POOL_CH), lambda j: (0, DGLU_COL + j))
    w = pl.BlockSpec((CONV_K, POOL_CH), lambda j: (0, j))
    col = pl.BlockSpec((SEQ, POOL_CH), lambda j: (0, j))
    vec = pl.BlockSpec((1, POOL_CH), lambda j: (0, j))
    return val, glu, w, col, vec


def _conv_fwd(z1, conv_w, conv_b):
    def body(val_ref, glu_ref, w_ref, b_ref, out_ref, xpad):
        xpad[0:CONV_PAD, :] = jnp.zeros((CONV_PAD, POOL_CH), f32)
        xpad[CONV_PAD:, :] = val_ref[...] * _sigmoid(glu_ref[...])
        w = w_ref[...]
        bias = b_ref[...]

        def tile(i, carry):
            t0 = pl.multiple_of(i * CONV_TILE, CONV_TILE)
            window = xpad[pl.ds(t0, CONV_TILE + CONV_PAD), :]
            acc = jnp.broadcast_to(bias, (CONV_TILE, POOL_CH))
            for k in range(CONV_K):
                shift = CONV_PAD - (CONV_K - 1) + k
                acc = acc + w[k:k + 1, :] * pltpu.roll(window, CONV_TILE + CONV_PAD - shift, 0)[0:CONV_TILE]
            out_ref[pl.ds(t0, CONV_TILE), :] = acc
            return carry

        lax.fori_loop(0, SEQ // CONV_TILE, tile, 0)

    val, glu, w, col, vec = _conv_specs()
    return pl.pallas_call(
        body, name="conv_fwd", grid=(4,), in_specs=[val, glu, w, vec], out_specs=col,
        out_shape=SDS((SEQ, HALF), f32), scratch_shapes=[pltpu.VMEM((SEQ + CONV_PAD, POOL_CH), f32)],
    )(z1, z1, conv_w, conv_b)


def _conv_bwd(z1, dconv, conv_w):
    def body(val_ref, glu_ref, w_ref, dout_ref, dval_ref, dglu_ref, dw_ref, db_ref, xpad, dpad, dx_ref):
        val = val_ref[...]
        sig = _sigmoid(glu_ref[...])
        xpad[0:CONV_PAD, :] = jnp.zeros((CONV_PAD, POOL_CH), f32)
        xpad[CONV_PAD:, :] = val * sig
        dout = dout_ref[...]
        dpad[0:SEQ, :] = dout
        dpad[SEQ:, :] = jnp.zeros((CONV_PAD, POOL_CH), f32)
        db_ref[...] = jnp.sum(dout, axis=0, keepdims=True)
        dw_ref[...] = jnp.zeros((CONV_K, POOL_CH), f32)
        w = w_ref[...]

        def tile(i, carry):
            t0 = pl.multiple_of(i * CONV_TILE, CONV_TILE)
            x_win = xpad[pl.ds(t0, CONV_TILE + CONV_PAD), :]
            d_win = dpad[pl.ds(t0, CONV_TILE + CONV_PAD), :]
            d_own = d_win[0:CONV_TILE]
            acc = jnp.zeros((CONV_TILE, POOL_CH), f32)
            for k in range(CONV_K):
                shift = CONV_PAD - (CONV_K - 1) + k
                x_k = pltpu.roll(x_win, CONV_TILE + CONV_PAD - shift, 0)[0:CONV_TILE]
                dw_ref[k:k + 1, :] += jnp.sum(d_own * x_k, axis=0, keepdims=True)
                back = CONV_K - 1 - k
                d_k = d_own if back == 0 else pltpu.roll(d_win, CONV_TILE + CONV_PAD - back, 0)[0:CONV_TILE]
                acc = acc + w[k:k + 1, :] * d_k
            dx_ref[pl.ds(t0, CONV_TILE), :] = acc
            return carry

        lax.fori_loop(0, SEQ // CONV_TILE, tile, 0)
        dx = dx_ref[...]
        dval_ref[...] = (dx * sig).astype(bf16)
        dglu_ref[...] = (dx * val * sig * (1.0 - sig)).astype(bf16)

    val, glu, w, col, vec = _conv_specs()
    pad = pltpu.VMEM((SEQ + CONV_PAD, POOL_CH), f32)
    return pl.pallas_call(
        body, name="conv_bwd", grid=(4,), in_specs=[val, glu, w, col], out_specs=[col, col, w, vec],
        out_shape=[SDS((SEQ, HALF), bf16), SDS((SEQ, HALF), bf16), SDS((CONV_K, HALF), f32), SDS((1, HALF), f32)],
        scratch_shapes=[pad, pad, pltpu.VMEM((SEQ, POOL_CH), f32)],
    )(z1, z1, conv_w, dconv)


DGATE_COL = 5


def _conv_norm_fwd(conv, z1, g, b):
    def body(c_ref, gate_ref, g_ref, b_ref, yd_ref):
        xh, _ = _ln_stats(c_ref[...])
        n = xh * g_ref[...] + b_ref[...]
        gate = gate_ref[...]
        yd_ref[...] = (n * _sigmoid(n) * gate * _sigmoid(gate)).astype(bf16)

    return pl.pallas_call(
        body, name="conv_norm_fwd", grid=(SEQ // ROWS,),
        in_specs=[_row_spec(HALF), _row_spec(HALF, DGATE_COL), _vec_spec(HALF), _vec_spec(HALF)],
        out_specs=_row_spec(HALF), out_shape=SDS((SEQ, HALF), bf16))(conv, z1, g, b)


def _conv_norm_bwd(conv, z1, dcat, g, b):
    def body(c_ref, gate_ref, dyd_ref, g_ref, b_ref, dconv_ref, dgate_ref, dg_ref, db_ref):
        first = pl.program_id(0) == 0
        xh, rstd = _ln_stats(c_ref[...])
        g = g_ref[...]
        n_silu, n_dsilu = _silu_and_grad(xh * g + b_ref[...])
        gate_silu, gate_dsilu = _silu_and_grad(gate_ref[...])
        dyd = dyd_ref[...]
        dgate_ref[...] = (dyd * n_silu * gate_dsilu).astype(bf16)
        dconv, dg, db = _ln_bwd(xh, rstd, g, dyd * gate_silu * n_dsilu)
        dconv_ref[...] = dconv
        _accumulate(dg_ref, dg, first)
        _accumulate(db_ref, db, first)

    return pl.pallas_call(
        body, name="conv_norm_bwd", grid=(SEQ // ROWS,),
        in_specs=[_row_spec(HALF), _row_spec(HALF, DGATE_COL), _row_spec(HALF, 1), _vec_spec(HALF), _vec_spec(HALF)],
        out_specs=[_row_spec(HALF), _row_spec(HALF), _vec_spec(HALF), _vec_spec(HALF)],
        out_shape=[SDS((SEQ, HALF), f32), SDS((SEQ, HALF), bf16), SDS((1, HALF), f32), SDS((1, HALF), f32)],
    )(conv, z1, dcat, g, b)


def _local_step(x, target, p):
    tables = _rope_tables()
    pool_w_bf = p["e_pool_w"].astype(bf16)
    bias_t = jnp.pad(p["o_sgu_b"].T, ((0, 0), (0, CHUNK - 4)))

    h0 = _pre_norm(x, p["e_pre_norm"])
    z0 = _mm_nn(h0, p["e_w_in"], f32, "e_in")
    ya = _pool_fwd(z0, pool_w_bf, p["e_pool_scale"])
    yb, att, lse = _attn_fwd(z0, tables)
    cat0 = jnp.concatenate([ya, yb], axis=1)
    y0 = _mm_nn(cat0, p["e_w_out"], f32, "e_out")
    x1, h1 = _mid_norm(x, y0, p["e_post_norm"], p["o_pre_norm"])
    z1 = _mm_nn(h1, p["o_w_in"], f32, "o_in")
    yc = _sgu_fwd(z1, p["o_sgu_norm_g"], p["o_sgu_norm_b"], p["o_sgu_w"], bias_t)
    conv = _conv_fwd(z1, p["o_conv_w"], p["o_conv_b"])
    yd = _conv_norm_fwd(conv, z1, p["o_conv_norm_g"], p["o_conv_norm_b"])
    cat1 = jnp.concatenate([yc, yd], axis=1)
    y1 = _mm_nn(cat1, p["o_w_out"], f32, "o_out")
    loss, dx2, dy1, g_o_post = _final_norm_loss(x1, y1, p["o_post_norm"], target)

    g_o_w_out = _mm_tn(cat1, dy1, 1, "o_out_dw")
    dcat1 = _mm_nt(dy1, p["o_w_out"], f32, "o_out_dx")
    du, dv, dcg, g_sgu_w, g_bias_t, g_sgu_g, g_sgu_b = _sgu_bwd(
        z1, dcat1, p["o_sgu_norm_g"], p["o_sgu_norm_b"], p["o_sgu_w"], bias_t)
    dconv, ddgate, g_cn_g, g_cn_b = _conv_norm_bwd(conv, z1, dcat1, p["o_conv_norm_g"], p["o_conv_norm_b"])
    ddval, ddglu, g_conv_w, g_conv_b = _conv_bwd(z1, dconv, p["o_conv_w"])
    dz1 = jnp.concatenate([du, dv, dcg, ddval, ddglu, ddgate], axis=1)
    g_o_w_in = _mm_tn(h1, dz1, N_CHIPS, "o_in_dw")
    dh1 = _mm_nt(dz1, p["o_w_in"], f32, "o_in_dx")
    dx1, dy0, g_o_pre, g_e_post = _mid_norm_bwd(dx2, dh1, x1, y0, p["o_pre_norm"], p["e_post_norm"])

    g_e_w_out = _mm_tn(cat0, dy0, 1, "e_out_dw")
    dcat0 = _mm_nt(dy0, p["e_w_out"], f32, "e_out_dx")
    da, dagate, g_pool_w, g_pool_scale = _pool_bwd(z0, dcat0, pool_w_bf, p["e_pool_scale"])
    dq0, dk0, dv0, dbgate = _attn_bwd_group(0, z0, att, lse, dcat0, tables)
    dq1, dk1, dv1 = _attn_bwd_group(1, z0, att, lse, dcat0, tables)
    dq2, dk2, dv2 = _attn_bwd_group(2, z0, att, lse, dcat0, tables)
    dz0 = jnp.concatenate([da, dagate, dq0, dq1, dq2, dk0, dk1, dk2, dv0, dv1, dv2, dbgate], axis=1)
    g_e_w_in = _mm_tn(h0, dz0, N_CHIPS, "e_in_dw")
    dh0 = _mm_nt(dz0, p["e_w_in"], f32, "e_in_dx")
    grad_x, g_e_pre = _pre_norm_bwd(dx1, dh0, x, p["e_pre_norm"])

    big = {"e_w_in": g_e_w_in, "e_w_out": g_e_w_out.reshape(N_CHIPS, HALF // 2, D_MODEL),
           "o_w_in": g_o_w_in, "o_w_out": g_o_w_out.reshape(N_CHIPS, HALF // 2, D_MODEL)}
    small = {"e_pre_norm": g_e_pre, "e_pool_w": g_pool_w, "e_pool_scale": g_pool_scale, "e_post_norm": g_e_post,
             "o_pre_norm": g_o_pre, "o_sgu_norm_g": g_sgu_g, "o_sgu_norm_b": g_sgu_b, "o_sgu_w": g_sgu_w,
             "o_sgu_b": g_bias_t[:, 0:4].T, "o_conv_w": g_conv_w, "o_conv_b": g_conv_b,
             "o_conv_norm_g": g_cn_g, "o_conv_norm_b": g_cn_b, "o_post_norm": g_o_post}
    return loss, grad_x, big, small


def _place():
    x, y, c = lax.axis_index("x"), lax.axis_index("y"), lax.axis_index("c")
    others = [(1 - x, y), (x, 1 - y), (1 - x, 1 - y)]
    return x, y, c, 2 * x + y, others


def _all_gather(shards, name):
    n = len(shards)

    def body(*refs):
        ins, outs = refs[:n], refs[n:2 * n]
        send_sems, recv_sems, local_sems = refs[2 * n:]
        x, y, c, me, others = _place()
        sibling = (x, y, 1 - c)

        def half(a, chip, core):
            rows = ins[a].shape[0] // 2
            return outs[a].at[chip, pl.ds(core * rows, rows), :]

        def copy(a, k, src, dst, to):
            return pltpu.make_async_remote_copy(src_ref=src, dst_ref=dst, send_sem=send_sems.at[6 * a + k],
                                                recv_sem=recv_sems.at[6 * a + k], device_id=to, device_id_type=MESH)

        local = [pltpu.make_async_copy(ins[a], outs[a].at[me], local_sems.at[a]) for a in range(n)]
        for cp in local:
            cp.start()
        sent = []
        for a in range(n):
            rows = ins[a].shape[0] // 2
            mine = ins[a].at[pl.ds(c * rows, rows), :]
            for k, (ox, oy) in enumerate(others):
                sent.append(copy(a, k, mine, half(a, me, c), (ox, oy, c)))
                sent[-1].start()
        for k, (ox, oy) in enumerate(others):
            chip = 2 * ox + oy
            for a in range(n):
                landed = half(a, chip, c)
                copy(a, k, landed, landed, (ox, oy, c)).wait_recv()
                sent.append(copy(a, 3 + k, landed, landed, sibling))
                sent[-1].start()
        for k, (ox, oy) in enumerate(others):
            chip = 2 * ox + oy
            for a in range(n):
                theirs = half(a, chip, 1 - c)
                copy(a, 3 + k, theirs, theirs, sibling).wait_recv()
        for cp in sent:
            cp.wait_send()
        for cp in local:
            cp.wait()

    return pl.pallas_call(
        body, name=name, in_specs=[ANY] * n, out_specs=[ANY] * n,
        out_shape=[SDS((N_CHIPS,) + s.shape, s.dtype) for s in shards],
        scratch_shapes=[pltpu.SemaphoreType.DMA((6 * n,)), pltpu.SemaphoreType.DMA((6 * n,)), pltpu.SemaphoreType.DMA((n,))],
    )(*shards)


def _swap_halves(parts, name):
    n = len(parts)

    def body(*refs):
        ins, own, theirs = refs[:n], refs[n:2 * n], refs[2 * n:3 * n]
        send_sems, recv_sems, local_sems = refs[3 * n:]
        x, y, c, _, _ = _place()
        sibling = (x, y, 1 - c)
        copies = []
        for a in range(n):
            rows = ins[a].shape[1] // 2
            keep = pltpu.make_async_copy(ins[a].at[:, pl.ds(c * rows, rows), :], own[a], local_sems.at[a])
            give = pltpu.make_async_remote_copy(
                src_ref=ins[a].at[:, pl.ds((1 - c) * rows, rows), :], dst_ref=theirs[a], send_sem=send_sems.at[a],
                recv_sem=recv_sems.at[a], device_id=sibling, device_id_type=MESH)
            keep.start()
            give.start()
            copies += [keep, give]
        for cp in copies:
            cp.wait()

    half = [SDS((N_CHIPS, s.shape[1] // 2, s.shape[2]), s.dtype) for s in parts]
    out = pl.pallas_call(
        body, name=name, in_specs=[ANY] * n, out_specs=[ANY] * (2 * n), out_shape=half + half,
        scratch_shapes=[pltpu.SemaphoreType.DMA((n,)), pltpu.SemaphoreType.DMA((n,)), pltpu.SemaphoreType.DMA((n,))],
    )(*parts)
    return out[:n], out[n:]


def _scatter_chips(parts, name):
    n = len(parts)

    def body(*refs):
        ins, outs = refs[:n], refs[n:2 * n]
        send_sems, recv_sems, local_sems = refs[2 * n:]
        x, y, c, me, others = _place()

        def copy(a, k, slot_from, slot_to, chip_xy):
            return pltpu.make_async_remote_copy(
                src_ref=ins[a].at[slot_from], dst_ref=outs[a].at[slot_to], send_sem=send_sems.at[3 * a + k],
                recv_sem=recv_sems.at[3 * a + k], device_id=(chip_xy[0], chip_xy[1], c), device_id_type=MESH)

        keeps, gives = [], []
        for a in range(n):
            keeps.append(pltpu.make_async_copy(ins[a].at[me], outs[a].at[me], local_sems.at[a]))
            keeps[-1].start()
            for k, (ox, oy) in enumerate(others):
                gives.append(copy(a, k, 2 * ox + oy, me, (ox, oy)))
                gives[-1].start()
        for a in range(n):
            for k, (ox, oy) in enumerate(others):
                copy(a, k, me, 2 * ox + oy, (ox, oy)).wait_recv()
        for cp in gives:
            cp.wait_send()
        for cp in keeps:
            cp.wait()

    return pl.pallas_call(
        body, name=name, in_specs=[ANY] * n, out_specs=[ANY] * n, out_shape=[SDS(s.shape, s.dtype) for s in parts],
        scratch_shapes=[pltpu.SemaphoreType.DMA((3 * n,)), pltpu.SemaphoreType.DMA((3 * n,)), pltpu.SemaphoreType.DMA((n,))],
    )(*parts)


def _join_halves(halves, name):
    n = len(halves)

    def body(*refs):
        ins, outs = refs[:n], refs[n:2 * n]
        send_sems, recv_sems, local_sems = refs[2 * n:]
        x, y, c, _, _ = _place()

        def copy(a, core):
            rows = ins[a].shape[0]
            return pltpu.make_async_remote_copy(
                src_ref=ins[a], dst_ref=outs[a].at[pl.ds(core * rows, rows), :], send_sem=send_sems.at[a],
                recv_sem=recv_sems.at[a], device_id=(x, y, 1 - c), device_id_type=MESH)

        keeps, gives = [], []
        for a in range(n):
            rows = ins[a].shape[0]
            keeps.append(pltpu.make_async_copy(ins[a], outs[a].at[pl.ds(c * rows, rows), :], local_sems.at[a]))
            gives.append(copy(a, c))
            keeps[-1].start()
            gives[-1].start()
        for a in range(n):
            copy(a, 1 - c).wait_recv()
        for cp in gives:
            cp.wait_send()
        for cp in keeps:
            cp.wait()

    return pl.pallas_call(
        body, name=name, in_specs=[ANY] * n, out_specs=[ANY] * n,
        out_shape=[SDS((2 * s.shape[0], s.shape[1]), s.dtype) for s in halves],
        scratch_shapes=[pltpu.SemaphoreType.DMA((n,)), pltpu.SemaphoreType.DMA((n,)), pltpu.SemaphoreType.DMA((n,))],
    )(*halves)


def _add_pair(a, b, name):
    _, r, c = a.shape
    tr = 256 if r % 256 == 0 else r // 2 if r > 512 else r

    def body(a_ref, b_ref, o_ref):
        o_ref[...] = (a_ref[...].astype(f32) + b_ref[...].astype(f32)).astype(o_ref.dtype)

    spec = pl.BlockSpec((None, tr, c), lambda j, i: (j, i, 0))
    return pl.pallas_call(body, name=name, grid=(N_CHIPS, r // tr), in_specs=[spec, spec], out_specs=spec,
                          out_shape=SDS(a.shape, a.dtype))(a, b)


def _add_chips(u, name):
    _, r, c = u.shape
    tr = 256 if r % 256 == 0 else r

    def body(u_ref, o_ref):
        o_ref[...] = ((u_ref[0].astype(f32) + u_ref[1].astype(f32)) + u_ref[2].astype(f32)) + u_ref[3].astype(f32)

    return pl.pallas_call(
        body, name=name, grid=(r // tr,), in_specs=[pl.BlockSpec((N_CHIPS, tr, c), lambda i: (0, i, 0))],
        out_specs=pl.BlockSpec((tr, c), lambda i: (i, 0)), out_shape=SDS((r, c), f32))(u)


def _reduce_scatter(parts, tag):
    own, theirs = _swap_halves(parts, f"swap_halves_{tag}")
    chip_sums = [_add_pair(o, t, f"add_cores_{tag}{i}") for i, (o, t) in enumerate(zip(own, theirs))]
    gathered = _scatter_chips(chip_sums, f"scatter_chips_{tag}")
    halves = [_add_chips(u, f"add_chips_{tag}{i}") for i, u in enumerate(gathered)]
    return _join_halves(halves, f"join_halves_{tag}")


def _adamw(w, g, m, v, name):
    r, c = w.shape
    tr = 128 if r % 128 == 0 else r
    b1c = 1.0 - ADAM_B1 ** ADAM_STEP
    b2c = 1.0 - ADAM_B2 ** ADAM_STEP

    def body(w_ref, g_ref, m_ref, v_ref, d_ref, nm_ref, nv_ref):
        g = g_ref[...]
        nm = ADAM_B1 * m_ref[...] + (1.0 - ADAM_B1) * g
        nv = ADAM_B2 * v_ref[...] + (1.0 - ADAM_B2) * (g * g)
        nm_ref[...] = nm
        nv_ref[...] = nv
        d_ref[...] = -ADAM_LR * ((nm / b1c) / (jnp.sqrt(nv / b2c) + ADAM_EPS) + ADAM_WD * w_ref[...])

    spec = pl.BlockSpec((tr, c), lambda i: (i, 0))
    return pl.pallas_call(body, name=name, grid=(r // tr,), in_specs=[spec] * 4, out_specs=[spec] * 3,
                          out_shape=[SDS((r, c), f32)] * 3)(w, g, m, v)


def _pack(arrays, total_rows=None):
    parts = []
    rows = 0
    for a in arrays:
        flat = a.reshape(-1, LANES)
        pad = -flat.shape[0] % 8
        parts.append(jnp.pad(flat, ((0, pad), (0, 0))))
        rows += flat.shape[0] + pad
    if total_rows is not None:
        parts.append(jnp.zeros((total_rows - rows, LANES), arrays[0].dtype))
    return jnp.concatenate(parts, axis=0)


def _unpack(buf, shapes):
    out = []
    row = 0
    lead = buf.shape[:-2]
    for shape in shapes:
        size = 1
        for s in shape:
            size *= s
        rows = size // LANES
        out.append(buf[..., row:row + rows, :].reshape(lead + tuple(shape)))
        row += rows + (-rows % 8)
    return out


BIG = ("e_w_in", "e_w_out", "o_w_in", "o_w_out")
SHARDED_SMALL = {
    "e_pool_w": ((4, 64, 256), 1), "o_pre_norm": ((512,), 0), "o_sgu_norm_g": ((256,), 0), "o_sgu_norm_b": ((256,), 0),
    "o_conv_w": ((31, 256), 1), "o_conv_b": ((256,), 0), "o_conv_norm_g": ((256,), 0), "o_conv_norm_b": ((256,), 0),
    "o_post_norm": ((512,), 0),
}
REPLICATED_SMALL = {"e_pre_norm": (2048,), "e_pool_scale": (1024,), "e_post_norm": (2048,),
                    "o_sgu_w": (4, 128, 128), "o_sgu_b": (4, 128)}
SMALL_ORDER = ("e_pre_norm", "e_pool_w", "e_pool_scale", "e_post_norm", "o_pre_norm", "o_sgu_norm_g", "o_sgu_norm_b",
               "o_sgu_w", "o_sgu_b", "o_conv_w", "o_conv_b", "o_conv_norm_g", "o_conv_norm_b", "o_post_norm")
ALL_ORDER = ("e_pre_norm", "e_w_in", "e_pool_w", "e_pool_scale", "e_w_out", "e_post_norm", "o_pre_norm", "o_w_in",
             "o_sgu_norm_g", "o_sgu_norm_b", "o_sgu_w", "o_sgu_b", "o_conv_w", "o_conv_b", "o_conv_norm_g",
             "o_conv_norm_b", "o_w_out", "o_post_norm")


def _full_shape(name):
    shape, axis = SHARDED_SMALL[name]
    return tuple(s * N_CHIPS if i == axis else s for i, s in enumerate(shape))


def _from_chips(name, stacked):
    shape, axis = SHARDED_SMALL[name]
    return jnp.moveaxis(stacked, 0, axis).reshape(_full_shape(name))


def _my_shard(name, full, chip):
    shape, axis = SHARDED_SMALL[name]
    return lax.dynamic_slice_in_dim(full, chip * shape[axis], shape[axis], axis)


def kernel(x, e_pre_norm, e_w_in, e_pool_w, e_pool_scale, e_w_out, e_post_norm, o_pre_norm, o_w_in, o_sgu_norm_g, o_sgu_norm_b, o_sgu_w, o_sgu_b, o_conv_w, o_conv_b, o_conv_norm_g, o_conv_norm_b, o_w_out, o_post_norm, loss_target, m_e_pre_norm, m_e_w_in, m_e_pool_w, m_e_pool_scale, m_e_w_out, m_e_post_norm, m_o_pre_norm, m_o_w_in, m_o_sgu_norm_g, m_o_sgu_norm_b, m_o_sgu_w, m_o_sgu_b, m_o_conv_w, m_o_conv_b, m_o_conv_norm_g, m_o_conv_norm_b, m_o_w_out, m_o_post_norm, v_e_pre_norm, v_e_w_in, v_e_pool_w, v_e_pool_scale, v_e_w_out, v_e_post_norm, v_o_pre_norm, v_o_w_in, v_o_sgu_norm_g, v_o_sgu_norm_b, v_o_sgu_w, v_o_sgu_b, v_o_conv_w, v_o_conv_b, v_o_conv_norm_g, v_o_conv_norm_b, v_o_w_out, v_o_post_norm):
    w = dict(e_pre_norm=e_pre_norm, e_w_in=e_w_in, e_pool_w=e_pool_w, e_pool_scale=e_pool_scale, e_w_out=e_w_out,
             e_post_norm=e_post_norm, o_pre_norm=o_pre_norm, o_w_in=o_w_in, o_sgu_norm_g=o_sgu_norm_g,
             o_sgu_norm_b=o_sgu_norm_b, o_sgu_w=o_sgu_w, o_sgu_b=o_sgu_b, o_conv_w=o_conv_w, o_conv_b=o_conv_b,
             o_conv_norm_g=o_conv_norm_g, o_conv_norm_b=o_conv_norm_b, o_w_out=o_w_out, o_post_norm=o_post_norm)
    m = dict(e_pre_norm=m_e_pre_norm, e_w_in=m_e_w_in, e_pool_w=m_e_pool_w, e_pool_scale=m_e_pool_scale,
             e_w_out=m_e_w_out, e_post_norm=m_e_post_norm, o_pre_norm=m_o_pre_norm, o_w_in=m_o_w_in,
             o_sgu_norm_g=m_o_sgu_norm_g, o_sgu_norm_b=m_o_sgu_norm_b, o_sgu_w=m_o_sgu_w, o_sgu_b=m_o_sgu_b,
             o_conv_w=m_o_conv_w, o_conv_b=m_o_conv_b, o_conv_norm_g=m_o_conv_norm_g, o_conv_norm_b=m_o_conv_norm_b,
             o_w_out=m_o_w_out, o_post_norm=m_o_post_norm)
    v = dict(e_pre_norm=v_e_pre_norm, e_w_in=v_e_w_in, e_pool_w=v_e_pool_w, e_pool_scale=v_e_pool_scale,
             e_w_out=v_e_w_out, e_post_norm=v_e_post_norm, o_pre_norm=v_o_pre_norm, o_w_in=v_o_w_in,
             o_sgu_norm_g=v_o_sgu_norm_g, o_sgu_norm_b=v_o_sgu_norm_b, o_sgu_w=v_o_sgu_w, o_sgu_b=v_o_sgu_b,
             o_conv_w=v_o_conv_w, o_conv_b=v_o_conv_b, o_conv_norm_g=v_o_conv_norm_g, o_conv_norm_b=v_o_conv_norm_b,
             o_w_out=v_o_w_out, o_post_norm=v_o_post_norm)
    w, m, v = ({k: a[0] for k, a in d.items()} for d in (w, m, v))
    chip = 2 * lax.axis_index("x") + lax.axis_index("y")

    sharded_names = list(SHARDED_SMALL)
    small_shard = _pack([w[k] for k in sharded_names], total_rows=352)
    gathered = _all_gather([w[k].astype(bf16) for k in BIG] + [small_shard], "gather_weights")
    full = {k: a for k, a in zip(BIG, gathered[:4])}
    full["e_w_out"] = full["e_w_out"].reshape(1, D_MODEL, D_MODEL)
    full["o_w_out"] = full["o_w_out"].reshape(1, D_MODEL, D_MODEL)
    stacked = _unpack(gathered[4], [SHARDED_SMALL[k][0] for k in sharded_names])
    for k, a in zip(sharded_names, stacked):
        full[k] = _from_chips(k, a)
    for k in REPLICATED_SMALL:
        full[k] = w[k]
    for k in ("e_pre_norm", "e_pool_scale", "e_post_norm", "o_pre_norm", "o_sgu_norm_g", "o_sgu_norm_b", "o_conv_b",
              "o_conv_norm_g", "o_conv_norm_b", "o_post_norm"):
        full[k] = full[k].reshape(1, -1)

    loss, grad_x, big, small = _local_step(x[0], loss_target[0], full)

    small_full_shapes = {k: (_full_shape(k) if k in SHARDED_SMALL else REPLICATED_SMALL[k]) for k in SMALL_ORDER}
    small_parts = _pack([small[k].reshape(small_full_shapes[k]) for k in SMALL_ORDER], total_rows=1536)
    reduced = _reduce_scatter([big[k] for k in BIG] + [small_parts.reshape(N_CHIPS, 384, LANES)], "grads")
    grads = {k: a for k, a in zip(BIG, reduced[:4])}
    small_sum = _all_gather([reduced[4]], "gather_small_grads")[0].reshape(1536, LANES)
    for k, a in zip(SMALL_ORDER, _unpack(small_sum, [small_full_shapes[k] for k in SMALL_ORDER])):
        grads[k] = _my_shard(k, a, chip) if k in SHARDED_SMALL else a
    loss = lax.psum(loss[0, 0], ("x", "y", "c"))

    delta, new_m, new_v = {}, {}, {}
    for k in BIG:
        delta[k], new_m[k], new_v[k] = _adamw(w[k], grads[k], m[k], v[k], f"adamw_{k}")
    local_shapes = [w[k].shape for k in SMALL_ORDER]
    packed = [_pack([d[k] for k in SMALL_ORDER]) for d in (w, grads, m, v)]
    for d, buf in zip((delta, new_m, new_v), _adamw(*packed, "adamw_small")):
        for k, a in zip(SMALL_ORDER, _unpack(buf, local_shapes)):
            d[k] = a

    outs = [loss, grad_x[None]]
    for d in (grads, delta, new_m, new_v):
        outs += [d[k][None] for k in ALL_ORDER]
    return tuple(outs)
```

```python
import jax
import jax.numpy as jnp
from jax import lax
from jax.experimental import pallas as pl
from jax.experimental.pallas import tpu as pltpu

f32 = jnp.float32
bf16 = jnp.bfloat16
SDS = jax.ShapeDtypeStruct

SEQ = 2048
D_MODEL = 2048
EPS = 1e-6
NEG = -1e30
HEAD_DIM = 128
ROT_HALF = 16
ROPE_THETA = 500000.0
DILATIONS = (1, 4, 16)
SPAN = 128
N_HEADS = 8
HALF = 1024
POOL_CH = 256
CONV_K = 31
CONV_PAD = 32
CHUNK = 128
N_CHIPS = 4
LANES = 256
ANY = pl.BlockSpec(memory_space=pl.ANY)
MESH = pl.DeviceIdType.MESH

ADAM_LR = 0.001
ADAM_B1 = 0.9
ADAM_B2 = 0.999
ADAM_EPS = 1e-08
ADAM_WD = 0.01
ADAM_STEP = 10


def _dot(a, b):
    return jnp.dot(a, b, preferred_element_type=f32)


def _dot_nt(a, b):
    return lax.dot_general(a, b, (((1,), (1,)), ((), ())), preferred_element_type=f32)


def _dot_tn(a, b):
    return lax.dot_general(a, b, (((0,), (0,)), ((), ())), preferred_element_type=f32)


def _sigmoid(x):
    return 1.0 / (1.0 + jnp.exp(-x))


def _silu_and_grad(x):
    s = _sigmoid(x)
    return x * s, s * (1.0 + x * (1.0 - s))


def _rms_fwd(x, g):
    r = lax.rsqrt(jnp.mean(x * x, axis=-1, keepdims=True) + EPS)
    return x * r * g


def _rms_bwd(x, g, dout):
    r = lax.rsqrt(jnp.mean(x * x, axis=-1, keepdims=True) + EPS)
    xh = x * r
    dg = jnp.sum(dout * xh, axis=0, keepdims=True)
    dxh = dout * g
    dx = r * (dxh - xh * jnp.mean(dxh * xh, axis=-1, keepdims=True))
    return dx, dg


def _ln_stats(x):
    mu = jnp.mean(x, axis=-1, keepdims=True)
    xc = x - mu
    rstd = lax.rsqrt(jnp.mean(xc * xc, axis=-1, keepdims=True) + EPS)
    return xc * rstd, rstd


def _ln_bwd(xh, rstd, g, dout):
    dg = jnp.sum(dout * xh, axis=0, keepdims=True)
    db = jnp.sum(dout, axis=0, keepdims=True)
    dxh = dout * g
    dx = rstd * (dxh - jnp.mean(dxh, axis=-1, keepdims=True) - xh * jnp.mean(dxh * xh, axis=-1, keepdims=True))
    return dx, dg, db


def _accumulate(ref, value, first):
    @pl.when(first)
    def _():
        ref[...] = value

    @pl.when(jnp.logical_not(first))
    def _():
        ref[...] += value


def _col_tile(ns):
    for t in (1024, 768, 512, 256):
        if ns % t == 0:
            return t
    raise ValueError(ns)


def _mm_nn(a, w, out_dtype, name):
    m, k = a.shape
    j, _, ns = w.shape
    tm, tn = 1024, _col_tile(ns)
    nb = ns // tn

    def body(a_ref, w_ref, o_ref):
        o_ref[...] = _dot(a_ref[...], w_ref[...]).astype(o_ref.dtype)

    return pl.pallas_call(
        body, name=name, grid=(j * nb, m // tm),
        in_specs=[pl.BlockSpec((tm, k), lambda n, i: (i, 0)),
                  pl.BlockSpec((None, k, tn), lambda n, i: (n // nb, 0, n % nb))],
        out_specs=pl.BlockSpec((tm, tn), lambda n, i: (i, n)),
        out_shape=SDS((m, j * ns), out_dtype),
    )(a, w)


def _mm_nt(dz, w, out_dtype, name):
    m, _ = dz.shape
    j, k, ns = w.shape
    tm, tk, tn = 1024, 1024, _col_tile(ns)
    nb = ns // tn
    steps = j * nb

    def body(dz_ref, w_ref, o_ref, acc_ref):
        r = pl.program_id(2)
        _accumulate(acc_ref, _dot_nt(dz_ref[...], w_ref[...]), r == 0)

        @pl.when(r == steps - 1)
        def _():
            o_ref[...] = acc_ref[...].astype(o_ref.dtype)

    return pl.pallas_call(
        body, name=name, grid=(m // tm, k // tk, steps),
        in_specs=[pl.BlockSpec((tm, tn), lambda i, kk, r: (i, r)),
                  pl.BlockSpec((None, tk, tn), lambda i, kk, r: (r // nb, kk, r % nb))],
        out_specs=pl.BlockSpec((tm, tk), lambda i, kk, r: (i, kk)),
        out_shape=SDS((m, k), out_dtype),
        scratch_shapes=[pltpu.VMEM((tm, tk), f32)],
    )(dz, w)


def _mm_tn(a, dz, j, name):
    m, k = a.shape
    ns = dz.shape[1] // j
    tk, tn = 1024, _col_tile(ns)
    nb = ns // tn

    def body(a_ref, dz_ref, o_ref):
        o_ref[...] = _dot_tn(a_ref[...], dz_ref[...]).astype(o_ref.dtype)

    return pl.pallas_call(
        body, name=name, grid=(k // tk, j * nb),
        in_specs=[pl.BlockSpec((m, tk), lambda kk, n: (0, kk)),
                  pl.BlockSpec((m, tn), lambda kk, n: (0, n))],
        out_specs=pl.BlockSpec((None, tk, tn), lambda kk, n: (n // nb, kk, n % nb)),
        out_shape=SDS((j, k, ns), bf16),
    )(a, dz)


ROWS = 256


def _row_spec(width=D_MODEL, col=0):
    return pl.BlockSpec((ROWS, width), lambda i: (i, col))


def _vec_spec(width=D_MODEL):
    return pl.BlockSpec((1, width), lambda i: (0, 0))


def _pre_norm(x, g):
    def body(x_ref, g_ref, h_ref):
        h_ref[...] = _rms_fwd(x_ref[...], g_ref[...]).astype(bf16)

    return pl.pallas_call(
        body, name="pre_norm", grid=(SEQ // ROWS,), in_specs=[_row_spec(), _vec_spec()],
        out_specs=_row_spec(), out_shape=SDS((SEQ, D_MODEL), bf16))(x, g)


def _mid_norm(x, y, g_post, g_pre):
    def body(x_ref, y_ref, gpost_ref, gpre_ref, x1_ref, h1_ref):
        x1 = x_ref[...] + _rms_fwd(y_ref[...], gpost_ref[...])
        x1_ref[...] = x1
        h1_ref[...] = _rms_fwd(x1, gpre_ref[...]).astype(bf16)

    return pl.pallas_call(
        body, name="mid_norm", grid=(SEQ // ROWS,),
        in_specs=[_row_spec(), _row_spec(), _vec_spec(), _vec_spec()],
        out_specs=[_row_spec(), _row_spec()],
        out_shape=[SDS((SEQ, D_MODEL), f32), SDS((SEQ, D_MODEL), bf16)])(x, y, g_post, g_pre)


def _final_norm_loss(x1, y, g_post, target):
    def body(x1_ref, y_ref, g_ref, t_ref, loss_ref, dx2_ref, dy_ref, dg_ref):
        first = pl.program_id(0) == 0
        y = y_ref[...]
        g = g_ref[...]
        err = x1_ref[...] + _rms_fwd(y, g) - t_ref[...]
        sq = jnp.sum(jnp.sum(err * err, axis=1, keepdims=True), axis=0, keepdims=True)
        _accumulate(loss_ref, sq * (0.5 / D_MODEL), first)
        dx2 = err * (1.0 / D_MODEL)
        dx2_ref[...] = dx2
        dy, dg = _rms_bwd(y, g, dx2)
        dy_ref[...] = dy.astype(bf16)
        _accumulate(dg_ref, dg, first)

    return pl.pallas_call(
        body, name="final_norm_loss", grid=(SEQ // ROWS,),
        in_specs=[_row_spec(), _row_spec(), _vec_spec(), _row_spec()],
        out_specs=[pl.BlockSpec((1, 1), lambda i: (0, 0)), _row_spec(), _row_spec(), _vec_spec()],
        out_shape=[SDS((1, 1), f32), SDS((SEQ, D_MODEL), f32), SDS((SEQ, D_MODEL), bf16), SDS((1, D_MODEL), f32)],
    )(x1, y, g_post, target)


def _mid_norm_bwd(dx2, dh1, x1, y0, g_pre, g_post):
    def body(dx2_ref, dh1_ref, x1_ref, y0_ref, gpre_ref, gpost_ref, dx1_ref, dy0_ref, dgpre_ref, dgpost_ref):
        first = pl.program_id(0) == 0
        d_in, dgpre = _rms_bwd(x1_ref[...], gpre_ref[...], dh1_ref[...])
        dx1 = dx2_ref[...] + d_in
        dx1_ref[...] = dx1
        dy0, dgpost = _rms_bwd(y0_ref[...], gpost_ref[...], dx1)
        dy0_ref[...] = dy0.astype(bf16)
        _accumulate(dgpre_ref, dgpre, first)
        _accumulate(dgpost_ref, dgpost, first)

    return pl.pallas_call(
        body, name="mid_norm_bwd", grid=(SEQ // ROWS,),
        in_specs=[_row_spec(), _row_spec(), _row_spec(), _row_spec(), _vec_spec(), _vec_spec()],
        out_specs=[_row_spec(), _row_spec(), _vec_spec(), _vec_spec()],
        out_shape=[SDS((SEQ, D_MODEL), f32), SDS((SEQ, D_MODEL), bf16), SDS((1, D_MODEL), f32), SDS((1, D_MODEL), f32)],
    )(dx2, dh1, x1, y0, g_pre, g_post)


def _pre_norm_bwd(dx1, dh0, x, g):
    def body(dx1_ref, dh0_ref, x_ref, g_ref, dx_ref, dg_ref):
        d_in, dg = _rms_bwd(x_ref[...], g_ref[...], dh0_ref[...])
        dx_ref[...] = dx1_ref[...] + d_in
        _accumulate(dg_ref, dg, pl.program_id(0) == 0)

    return pl.pallas_call(
        body, name="pre_norm_bwd", grid=(SEQ // ROWS,),
        in_specs=[_row_spec(), _row_spec(), _row_spec(), _vec_spec()],
        out_specs=[_row_spec(), _vec_spec()],
        out_shape=[SDS((SEQ, D_MODEL), f32), SDS((1, D_MODEL), f32)])(dx1, dh0, x, g)


def _pool_count(g):
    row = lax.broadcasted_iota(jnp.int32, (SEQ, 1), 0)
    width = jnp.left_shift(2, g)
    return row, width, jnp.minimum(row + 1, width).astype(f32)


def _trailing_sum(x, row, width):
    s = x
    for k in (1, 2, 4, 8):
        shifted = jnp.where(row >= k, pltpu.roll(s, k, 0), 0.0)
        s = jnp.where(width > k, s + shifted, s)
    return s


def _leading_sum(x, row, width):
    s = x
    for k in (1, 2, 4, 8):
        shifted = jnp.where(row < SEQ - k, pltpu.roll(s, SEQ - k, 0), 0.0)
        s = jnp.where(width > k, s + shifted, s)
    return s


def _pool_specs():
    a_in = pl.BlockSpec((SEQ, POOL_CH), lambda g: (0, g))
    a_gate = pl.BlockSpec((SEQ, POOL_CH), lambda g: (0, 4 + g))
    w = pl.BlockSpec((None, POOL_CH, POOL_CH), lambda g: (g, 0, 0))
    scale = pl.BlockSpec((1, POOL_CH), lambda g: (0, g))
    return a_in, a_gate, w, scale


def _pool_fwd(z0, pool_w, pool_scale):
    def body(a_ref, gate_ref, w_ref, scale_ref, ya_ref):
        row, width, count = _pool_count(pl.program_id(0))
        a = a_ref[...]
        pooled = _trailing_sum(a, row, width) / count - a
        mixed = _dot(pooled.astype(bf16), w_ref[...]) * scale_ref[...]
        gate = gate_ref[...]
        ya_ref[...] = (mixed * gate * _sigmoid(gate)).astype(bf16)

    return pl.pallas_call(
        body, name="pool_fwd", grid=(4,), in_specs=list(_pool_specs()),
        out_specs=pl.BlockSpec((SEQ, POOL_CH), lambda g: (0, g)),
        out_shape=SDS((SEQ, HALF), bf16))(z0, z0, pool_w, pool_scale)


def _pool_bwd(z0, dcat, pool_w, pool_scale):
    def body(a_ref, gate_ref, w_ref, scale_ref, dya_ref, da_ref, dgate_ref, dw_ref, dscale_ref):
        row, width, count = _pool_count(pl.program_id(0))
        a = a_ref[...]
        pooled = (_trailing_sum(a, row, width) / count - a).astype(bf16)
        w = w_ref[...]
        scale = scale_ref[...]
        mixed = _dot(pooled, w)
        silu, dsilu = _silu_and_grad(gate_ref[...])
        dya = dya_ref[...]
        dgate_ref[...] = (dya * mixed * scale * dsilu).astype(bf16)
        dms = dya * silu
        dscale_ref[...] = jnp.sum(dms * mixed, axis=0, keepdims=True)
        dmixed = (dms * scale).astype(bf16)
        dw_ref[...] = _dot_tn(pooled, dmixed)
        dpooled = _dot_nt(dmixed, w)
        da_ref[...] = (_leading_sum(dpooled / count, row, width) - dpooled).astype(bf16)

    a_in, a_gate, w, scale = _pool_specs()
    col = pl.BlockSpec((SEQ, POOL_CH), lambda g: (0, g))
    return pl.pallas_call(
        body, name="pool_bwd", grid=(4,), in_specs=[a_in, a_gate, w, scale, col],
        out_specs=[col, col, w, scale],
        out_shape=[SDS((SEQ, HALF), bf16), SDS((SEQ, HALF), bf16), SDS((4, POOL_CH, POOL_CH), f32), SDS((1, HALF), f32)],
    )(z0, z0, pool_w, pool_scale, dcat)


Q_COL, K_COL, V_COL, BGATE_COL = 16, 40, 64, 88


def _rope_tables():
    pos = jnp.arange(SEQ, dtype=f32)
    inv_freq = jnp.power(ROPE_THETA, -jnp.arange(0, 2 * ROT_HALF, 2, dtype=f32) / (2 * ROT_HALF))
    ang = pos[:, None] * inv_freq[None, :]
    cos, sin = jnp.cos(ang), jnp.sin(ang)
    zeros = jnp.zeros((SEQ, HEAD_DIM - 2 * ROT_HALF), f32)
    zero_half = jnp.zeros((SEQ, ROT_HALF), f32)
    cos_t = jnp.concatenate([cos, cos, zeros + 1.0], axis=1)
    sin_a = jnp.concatenate([-sin, zero_half, zeros], axis=1)
    sin_b = jnp.concatenate([zero_half, sin, zeros], axis=1)
    return cos_t, sin_a, sin_b


def _rope(t, cos_t, sin_a, sin_b):
    return t * cos_t + pltpu.roll(t, HEAD_DIM - ROT_HALF, 1) * sin_a + pltpu.roll(t, ROT_HALF, 1) * sin_b


def _rope_transposed(d, cos_t, sin_a, sin_b):
    return d * cos_t + pltpu.roll(d * sin_a, ROT_HALF, 1) + pltpu.roll(d * sin_b, HEAD_DIM - ROT_HALF, 1)


def _by_residue(dst_ref, src_ref, dilation, dtype):
    if dilation == 1:
        dst_ref[...] = src_ref[...].astype(dtype)
        return
    length = SEQ // dilation
    for r in range(dilation):
        dst_ref[r * length:(r + 1) * length, :] = src_ref[pl.ds(r, length, stride=dilation), :].astype(dtype)


def _by_position(dst_ref, src_ref, dilation):
    if dilation == 1:
        dst_ref[...] = src_ref[...]
        return
    length = SEQ // dilation
    for r in range(dilation):
        dst_ref[pl.ds(r, length, stride=dilation), :] = src_ref[r * length:(r + 1) * length, :]


def _band_masks():
    qi = lax.broadcasted_iota(jnp.int32, (SPAN, SPAN), 0)
    kj = lax.broadcasted_iota(jnp.int32, (SPAN, SPAN), 1)
    return kj <= qi, kj - qi


def _head_spec(col):
    return pl.BlockSpec((SEQ, HEAD_DIM), lambda h: (0, col + h))


def _table_spec():
    return pl.BlockSpec((SEQ, HEAD_DIM), lambda h: (0, 0))


def _attn_fwd(z0, tables):
    scale = HEAD_DIM ** -0.5

    def body(*refs):
        qkv = refs[0:9]
        bg_ref, cos_ref, sa_ref, sb_ref = refs[9:13]
        yb_ref, att_ref, lse_ref = refs[13:16]
        qd, kd, vd, tmp, o_res, l_res, o_pos, l_pos = refs[16:24]
        own_mask, prev_off = _band_masks()
        cos_t, sin_a, sin_b = cos_ref[...], sa_ref[...], sb_ref[...]
        for g, dilation in enumerate(DILATIONS):
            q_ref, k_ref, v_ref = qkv[3 * g:3 * g + 3]
            blocks_per_residue = SEQ // dilation // SPAN
            tmp[...] = _rope(q_ref[...], cos_t, sin_a, sin_b)
            _by_residue(qd, tmp, dilation, bf16)
            tmp[...] = _rope(k_ref[...], cos_t, sin_a, sin_b)
            _by_residue(kd, tmp, dilation, bf16)
            _by_residue(vd, v_ref, dilation, bf16)

            def block(c, carry, dilation=dilation, blocks_per_residue=blocks_per_residue):
                r0 = pl.multiple_of(c * SPAN, SPAN)
                q = qd[pl.ds(r0, SPAN), :]
                s = jnp.where(own_mask, _dot_nt(q, kd[pl.ds(r0, SPAN), :]) * scale, NEG)
                m = jnp.max(s, axis=1, keepdims=True)
                if blocks_per_residue > 1:
                    p0 = pl.multiple_of(jnp.maximum(c - 1, 0) * SPAN, SPAN)
                    first_off = jnp.where(c % blocks_per_residue > 0, 0, SPAN)
                    sp = jnp.where(prev_off >= first_off, _dot_nt(q, kd[pl.ds(p0, SPAN), :]) * scale, NEG)
                    m = jnp.maximum(m, jnp.max(sp, axis=1, keepdims=True))
                    pp = jnp.exp(sp - m)
                p = jnp.exp(s - m)
                den = jnp.sum(p, axis=1, keepdims=True)
                o = _dot(p.astype(bf16), vd[pl.ds(r0, SPAN), :])
                if blocks_per_residue > 1:
                    den = den + jnp.sum(pp, axis=1, keepdims=True)
                    o = o + _dot(pp.astype(bf16), vd[pl.ds(p0, SPAN), :])
                o_res[pl.ds(r0, SPAN), :] = o / den
                l_res[pl.ds(r0, SPAN), :] = jnp.broadcast_to(m + jnp.log(den), (SPAN, HEAD_DIM))
                return carry

            lax.fori_loop(0, SEQ // SPAN, block, 0)
            _by_position(o_pos.at[g], o_res, dilation)
            _by_position(l_pos.at[g], l_res, dilation)

        l0, l1, l2 = l_pos[0], l_pos[1], l_pos[2]
        top = jnp.maximum(jnp.maximum(l0, l1), l2)
        total = top + jnp.log(jnp.exp(l0 - top) + jnp.exp(l1 - top) + jnp.exp(l2 - top))
        att = jnp.exp(l0 - total) * o_pos[0] + jnp.exp(l1 - total) * o_pos[1] + jnp.exp(l2 - total) * o_pos[2]
        att_ref[...] = att
        lse_ref[...] = total
        gate = bg_ref[...]
        yb_ref[...] = (att * gate * _sigmoid(gate)).astype(bf16)

    in_specs = []
    for g in range(3):
        in_specs += [_head_spec(Q_COL + 8 * g), _head_spec(K_COL + 8 * g), _head_spec(V_COL + 8 * g)]
    in_specs += [_head_spec(BGATE_COL), _table_spec(), _table_spec(), _table_spec()]
    out_spec = pl.BlockSpec((SEQ, HEAD_DIM), lambda h: (0, h))
    vm = lambda dt: pltpu.VMEM((SEQ, HEAD_DIM), dt)
    return pl.pallas_call(
        body, name="attn_fwd", grid=(N_HEADS,), in_specs=in_specs, out_specs=[out_spec] * 3,
        out_shape=[SDS((SEQ, HALF), bf16), SDS((SEQ, HALF), f32), SDS((SEQ, HALF), f32)],
        scratch_shapes=[vm(bf16), vm(bf16), vm(bf16), vm(f32), vm(f32), vm(f32),
                        pltpu.VMEM((3, SEQ, HEAD_DIM), f32), pltpu.VMEM((3, SEQ, HEAD_DIM), f32)],
    )(*([z0] * 10), *tables)


def _attn_bwd_group(g, z0, att, lse, dcat, tables):
    scale = HEAD_DIM ** -0.5
    dilation = DILATIONS[g]
    blocks_per_residue = SEQ // dilation // SPAN
    with_gate = g == 0

    def body(*refs):
        q_ref, k_ref, v_ref, bg_ref, att_ref, lse_ref, dyb_ref, cos_ref, sa_ref, sb_ref = refs[0:10]
        n_out = 4 if with_gate else 3
        dq_ref, dk_ref, dv_ref = refs[10:13]
        qd, kd, vd, dod, ld, dd, tmp, aq, ak, av = refs[10 + n_out:20 + n_out]
        own_mask, prev_off = _band_masks()
        cos_t, sin_a, sin_b = cos_ref[...], sa_ref[...], sb_ref[...]
        gate = bg_ref[...]
        silu, dsilu = _silu_and_grad(gate)
        att_v = att_ref[...]
        dyb = dyb_ref[...]
        if with_gate:
            refs[13][...] = (dyb * att_v * dsilu).astype(bf16)
        datt = dyb * silu
        tmp[...] = datt
        _by_residue(dod, tmp, dilation, bf16)
        tmp[...] = jnp.broadcast_to(jnp.sum(datt * att_v, axis=1, keepdims=True), (SEQ, HEAD_DIM))
        _by_residue(dd, tmp, dilation, f32)
        _by_residue(ld, lse_ref, dilation, f32)
        tmp[...] = _rope(q_ref[...], cos_t, sin_a, sin_b)
        _by_residue(qd, tmp, dilation, bf16)
        tmp[...] = _rope(k_ref[...], cos_t, sin_a, sin_b)
        _by_residue(kd, tmp, dilation, bf16)
        _by_residue(vd, v_ref, dilation, bf16)
        ak[...] = jnp.zeros((SEQ, HEAD_DIM), f32)
        av[...] = jnp.zeros((SEQ, HEAD_DIM), f32)

        def block(c, carry):
            r0 = pl.multiple_of(c * SPAN, SPAN)
            rows = pl.ds(r0, SPAN)
            q, k, v, do = qd[rows, :], kd[rows, :], vd[rows, :], dod[rows, :]
            lse_q, delta = ld[rows, :], dd[rows, :]
            p = jnp.where(own_mask, jnp.exp(_dot_nt(q, k) * scale - lse_q), 0.0)
            ds = (p * (_dot_nt(do, v) - delta) * scale).astype(bf16)
            av[rows, :] += _dot_tn(p.astype(bf16), do)
            ak[rows, :] += _dot_tn(ds, q)
            dq = _dot(ds, k)
            if blocks_per_residue > 1:
                p0 = pl.multiple_of(jnp.maximum(c - 1, 0) * SPAN, SPAN)
                prev = pl.ds(p0, SPAN)
                kp, vp = kd[prev, :], vd[prev, :]
                first_off = jnp.where(c % blocks_per_residue > 0, 0, SPAN)
                pp = jnp.where(prev_off >= first_off, jnp.exp(_dot_nt(q, kp) * scale - lse_q), 0.0)
                dsp = (pp * (_dot_nt(do, vp) - delta) * scale).astype(bf16)
                av[prev, :] += _dot_tn(pp.astype(bf16), do)
                ak[prev, :] += _dot_tn(dsp, q)
                dq = dq + _dot(dsp, kp)
            aq[rows, :] = dq
            return carry

        lax.fori_loop(0, SEQ // SPAN, block, 0)
        _by_position(tmp, aq, dilation)
        dq_ref[...] = _rope_transposed(tmp[...], cos_t, sin_a, sin_b).astype(bf16)
        _by_position(tmp, ak, dilation)
        dk_ref[...] = _rope_transposed(tmp[...], cos_t, sin_a, sin_b).astype(bf16)
        _by_position(tmp, av, dilation)
        dv_ref[...] = tmp[...].astype(bf16)

    head = pl.BlockSpec((SEQ, HEAD_DIM), lambda h: (0, h))
    in_specs = [_head_spec(Q_COL + 8 * g), _head_spec(K_COL + 8 * g), _head_spec(V_COL + 8 * g), _head_spec(BGATE_COL),
                head, head, _head_spec(8), _table_spec(), _table_spec(), _table_spec()]
    n_out = 4 if with_gate else 3
    vm = lambda dt: pltpu.VMEM((SEQ, HEAD_DIM), dt)
    return pl.pallas_call(
        body, name=f"attn_bwd_g{g}", grid=(N_HEADS,), in_specs=in_specs, out_specs=[head] * n_out,
        out_shape=[SDS((SEQ, HALF), bf16)] * n_out,
        scratch_shapes=[vm(bf16), vm(bf16), vm(bf16), vm(bf16), vm(f32), vm(f32), vm(f32), vm(f32), vm(f32), vm(f32)],
    )(z0, z0, z0, z0, att, lse, dcat, *tables)


def _sgu_specs():
    chunk = lambda col: pl.BlockSpec((CHUNK, HALF), lambda n: (n, col))
    vec = pl.BlockSpec((1, HALF), lambda n: (0, 0))
    w = pl.BlockSpec((4, CHUNK, CHUNK), lambda n: (0, 0, 0))
    bias = pl.BlockSpec((CHUNK, CHUNK), lambda n: (0, 0))
    return chunk, vec, w, bias


def _sgu_weights(w_ref):
    tril = lax.broadcasted_iota(jnp.int32, (CHUNK, CHUNK), 1) <= lax.broadcasted_iota(jnp.int32, (CHUNK, CHUNK), 0)
    return tril, [jnp.where(tril, w_ref[h], 0.0).astype(bf16) for h in range(4)]


def _sgu_fwd(z1, ln_g, ln_b, sgu_w, bias_t):
    def body(u_ref, v_ref, cg_ref, g_ref, b_ref, w_ref, bias_ref, yc_ref):
        _, ws = _sgu_weights(w_ref)
        xh, _ = _ln_stats(v_ref[...])
        vn = (xh * g_ref[...] + b_ref[...]).astype(bf16)
        for h in range(4):
            cols = slice(h * POOL_CH, (h + 1) * POOL_CH)
            s = _dot(ws[h], vn[:, cols]) + bias_ref[:, h:h + 1]
            gate = cg_ref[:, cols]
            yc_ref[:, cols] = (u_ref[:, cols] * s * gate * _sigmoid(gate)).astype(bf16)

    chunk, vec, w, bias = _sgu_specs()
    return pl.pallas_call(
        body, name="sgu_fwd", grid=(SEQ // CHUNK,),
        in_specs=[chunk(0), chunk(1), chunk(2), vec, vec, w, bias], out_specs=chunk(0),
        out_shape=SDS((SEQ, HALF), bf16))(z1, z1, z1, ln_g, ln_b, sgu_w, bias_t)


def _sgu_bwd(z1, dcat, ln_g, ln_b, sgu_w, bias_t):
    def body(u_ref, v_ref, cg_ref, dyc_ref, g_ref, b_ref, w_ref, bias_ref,
             du_ref, dv_ref, dcg_ref, dw_ref, dbias_ref, dg_ref, db_ref, dvn_ref):
        first = pl.program_id(0) == 0
        tril, ws = _sgu_weights(w_ref)
        xh, rstd = _ln_stats(v_ref[...])
        g = g_ref[...]
        vn = (xh * g + b_ref[...]).astype(bf16)

        @pl.when(first)
        def _():
            dbias_ref[...] = jnp.zeros((CHUNK, CHUNK), f32)

        for h in range(4):
            cols = slice(h * POOL_CH, (h + 1) * POOL_CH)
            vn_h = vn[:, cols]
            s = _dot(ws[h], vn_h) + bias_ref[:, h:h + 1]
            silu, dsilu = _silu_and_grad(cg_ref[:, cols])
            dyc = dyc_ref[:, cols]
            u = u_ref[:, cols]
            du_ref[:, cols] = (dyc * s * silu).astype(bf16)
            dcg_ref[:, cols] = (dyc * u * s * dsilu).astype(bf16)
            ds = dyc * u * silu
            dbias_ref[:, h:h + 1] += jnp.sum(ds, axis=1, keepdims=True)
            ds = ds.astype(bf16)
            _accumulate(dw_ref.at[h], jnp.where(tril, _dot_nt(ds, vn_h), 0.0), first)
            dvn_ref[:, cols] = _dot_tn(ws[h], ds)
        dv, dg, db = _ln_bwd(xh, rstd, g, dvn_ref[...])
        dv_ref[...] = dv.astype(bf16)
        _accumulate(dg_ref, dg, first)
        _accumulate(db_ref, db, first)

    chunk, vec, w, bias = _sgu_specs()
    return pl.pallas_call(
        body, name="sgu_bwd", grid=(SEQ // CHUNK,),
        in_specs=[chunk(0), chunk(1), chunk(2), chunk(0), vec, vec, w, bias],
        out_specs=[chunk(0), chunk(0), chunk(0), w, bias, vec, vec],
        out_shape=[SDS((SEQ, HALF), bf16)] * 3 + [SDS((4, CHUNK, CHUNK), f32), SDS((CHUNK, CHUNK), f32),
                                                   SDS((1, HALF), f32), SDS((1, HALF), f32)],
        scratch_shapes=[pltpu.VMEM((CHUNK, HALF), f32)],
    )(z1, z1, z1, dcat, ln_g, ln_b, sgu_w, bias_t)


CONV_TILE = 128
DVAL_COL, DGLU_COL = 12, 16


def _conv_specs():
    val = pl.BlockSpec((SEQ, POOL_CH), lambda j: (0, DVAL_COL + j))
    glu = pl.BlockSpec((SEQ, POOL_CH), lambda j: (0, DGLU_COL + j))
    w = pl.BlockSpec((CONV_K, POOL_CH), lambda j: (0, j))
    col = pl.BlockSpec((SEQ, POOL_CH), lambda j: (0, j))
    vec = pl.BlockSpec((1, POOL_CH), lambda j: (0, j))
    return val, glu, w, col, vec


def _conv_fwd(z1, conv_w, conv_b):
    def body(val_ref, glu_ref, w_ref, b_ref, out_ref, xpad):
        xpad[0:CONV_PAD, :] = jnp.zeros((CONV_PAD, POOL_CH), f32)
        xpad[CONV_PAD:, :] = val_ref[...] * _sigmoid(glu_ref[...])
        w = w_ref[...]
        bias = b_ref[...]

        def tile(i, carry):
            t0 = pl.multiple_of(i * CONV_TILE, CONV_TILE)
            window = xpad[pl.ds(t0, CONV_TILE + CONV_PAD), :]
            acc = jnp.broadcast_to(bias, (CONV_TILE, POOL_CH))
            for k in range(CONV_K):
                shift = CONV_PAD - (CONV_K - 1) + k
                acc = acc + w[k:k + 1, :] * pltpu.roll(window, CONV_TILE + CONV_PAD - shift, 0)[0:CONV_TILE]
            out_ref[pl.ds(t0, CONV_TILE), :] = acc
            return carry

        lax.fori_loop(0, SEQ // CONV_TILE, tile, 0)

    val, glu, w, col, vec = _conv_specs()
    return pl.pallas_call(
        body, name="conv_fwd", grid=(4,), in_specs=[val, glu, w, vec], out_specs=col,
        out_shape=SDS((SEQ, HALF), f32), scratch_shapes=[pltpu.VMEM((SEQ + CONV_PAD, POOL_CH), f32)],
    )(z1, z1, conv_w, conv_b)


def _conv_bwd(z1, dconv, conv_w):
    def body(val_ref, glu_ref, w_ref, dout_ref, dval_ref, dglu_ref, dw_ref, db_ref, xpad, dpad, dx_ref):
        val = val_ref[...]
        sig = _sigmoid(glu_ref[...])
        xpad[0:CONV_PAD, :] = jnp.zeros((CONV_PAD, POOL_CH), f32)
        xpad[CONV_PAD:, :] = val * sig
        dout = dout_ref[...]
        dpad[0:SEQ, :] = dout
        dpad[SEQ:, :] = jnp.zeros((CONV_PAD, POOL_CH), f32)
        db_ref[...] = jnp.sum(dout, axis=0, keepdims=True)
        dw_ref[...] = jnp.zeros((CONV_K, POOL_CH), f32)
        w = w_ref[...]

        def tile(i, carry):
            t0 = pl.multiple_of(i * CONV_TILE, CONV_TILE)
            x_win = xpad[pl.ds(t0, CONV_TILE + CONV_PAD), :]
            d_win = dpad[pl.ds(t0, CONV_TILE + CONV_PAD), :]
            d_own = d_win[0:CONV_TILE]
            acc = jnp.zeros((CONV_TILE, POOL_CH), f32)
            for k in range(CONV_K):
                shift = CONV_PAD - (CONV_K - 1) + k
                x_k = pltpu.roll(x_win, CONV_TILE + CONV_PAD - shift, 0)[0:CONV_TILE]
                dw_ref[k:k + 1, :] += jnp.sum(d_own * x_k, axis=0, keepdims=True)
                back = CONV_K - 1 - k
                d_k = d_own if back == 0 else pltpu.roll(d_win, CONV_TILE + CONV_PAD - back, 0)[0:CONV_TILE]
                acc = acc + w[k:k + 1, :] * d_k
            dx_ref[pl.ds(t0, CONV_TILE), :] = acc
            return carry

        lax.fori_loop(0, SEQ // CONV_TILE, tile, 0)
        dx = dx_ref[...]
        dval_ref[...] = (dx * sig).astype(bf16)
        dglu_ref[...] = (dx * val * sig * (1.0 - sig)).astype(bf16)

    val, glu, w, col, vec = _conv_specs()
    pad = pltpu.VMEM((SEQ + CONV_PAD, POOL_CH), f32)
    return pl.pallas_call(
        body, name="conv_bwd", grid=(4,), in_specs=[val, glu, w, col], out_specs=[col, col, w, vec],
        out_shape=[SDS((SEQ, HALF), bf16), SDS((SEQ, HALF), bf16), SDS((CONV_K, HALF), f32), SDS((1, HALF), f32)],
        scratch_shapes=[pad, pad, pltpu.VMEM((SEQ, POOL_CH), f32)],
    )(z1, z1, conv_w, dconv)


DGATE_COL = 5


def _conv_norm_fwd(conv, z1, g, b):
    def body(c_ref, gate_ref, g_ref, b_ref, yd_ref):
        xh, _ = _ln_stats(c_ref[...])
        n = xh * g_ref[...] + b_ref[...]
        gate = gate_ref[...]
        yd_ref[...] = (n * _sigmoid(n) * gate * _sigmoid(gate)).astype(bf16)

    return pl.pallas_call(
        body, name="conv_norm_fwd", grid=(SEQ // ROWS,),
        in_specs=[_row_spec(HALF), _row_spec(HALF, DGATE_COL), _vec_spec(HALF), _vec_spec(HALF)],
        out_specs=_row_spec(HALF), out_shape=SDS((SEQ, HALF), bf16))(conv, z1, g, b)


def _conv_norm_bwd(conv, z1, dcat, g, b):
    def body(c_ref, gate_ref, dyd_ref, g_ref, b_ref, dconv_ref, dgate_ref, dg_ref, db_ref):
        first = pl.program_id(0) == 0
        xh, rstd = _ln_stats(c_ref[...])
        g = g_ref[...]
        n_silu, n_dsilu = _silu_and_grad(xh * g + b_ref[...])
        gate_silu, gate_dsilu = _silu_and_grad(gate_ref[...])
        dyd = dyd_ref[...]
        dgate_ref[...] = (dyd * n_silu * gate_dsilu).astype(bf16)
        dconv, dg, db = _ln_bwd(xh, rstd, g, dyd * gate_silu * n_dsilu)
        dconv_ref[...] = dconv
        _accumulate(dg_ref, dg, first)
        _accumulate(db_ref, db, first)

    return pl.pallas_call(
        body, name="conv_norm_bwd", grid=(SEQ // ROWS,),
        in_specs=[_row_spec(HALF), _row_spec(HALF, DGATE_COL), _row_spec(HALF, 1), _vec_spec(HALF), _vec_spec(HALF)],
        out_specs=[_row_spec(HALF), _row_spec(HALF), _vec_spec(HALF), _vec_spec(HALF)],
        out_shape=[SDS((SEQ, HALF), f32), SDS((SEQ, HALF), bf16), SDS((1, HALF), f32), SDS((1, HALF), f32)],
    )(conv, z1, dcat, g, b)


def _local_step(x, target, p):
    tables = _rope_tables()
    pool_w_bf = p["e_pool_w"].astype(bf16)
    bias_t = jnp.pad(p["o_sgu_b"].T, ((0, 0), (0, CHUNK - 4)))

    h0 = _pre_norm(x, p["e_pre_norm"])
    z0 = _mm_nn(h0, p["e_w_in"], f32, "e_in")
    ya = _pool_fwd(z0, pool_w_bf, p["e_pool_scale"])
    yb, att, lse = _attn_fwd(z0, tables)
    cat0 = jnp.concatenate([ya, yb], axis=1)
    y0 = _mm_nn(cat0, p["e_w_out"], f32, "e_out")
    x1, h1 = _mid_norm(x, y0, p["e_post_norm"], p["o_pre_norm"])
    z1 = _mm_nn(h1, p["o_w_in"], f32, "o_in")
    yc = _sgu_fwd(z1, p["o_sgu_norm_g"], p["o_sgu_norm_b"], p["o_sgu_w"], bias_t)
    conv = _conv_fwd(z1, p["o_conv_w"], p["o_conv_b"])
    yd = _conv_norm_fwd(conv, z1, p["o_conv_norm_g"], p["o_conv_norm_b"])
    cat1 = jnp.concatenate([yc, yd], axis=1)
    y1 = _mm_nn(cat1, p["o_w_out"], f32, "o_out")
    loss, dx2, dy1, g_o_post = _final_norm_loss(x1, y1, p["o_post_norm"], target)

    g_o_w_out = _mm_tn(cat1, dy1, 1, "o_out_dw")
    dcat1 = _mm_nt(dy1, p["o_w_out"], f32, "o_out_dx")
    du, dv, dcg, g_sgu_w, g_bias_t, g_sgu_g, g_sgu_b = _sgu_bwd(
        z1, dcat1, p["o_sgu_norm_g"], p["o_sgu_norm_b"], p["o_sgu_w"], bias_t)
    dconv, ddgate, g_cn_g, g_cn_b = _conv_norm_bwd(conv, z1, dcat1, p["o_conv_norm_g"], p["o_conv_norm_b"])
    ddval, ddglu, g_conv_w, g_conv_b = _conv_bwd(z1, dconv, p["o_conv_w"])
    dz1 = jnp.concatenate([du, dv, dcg, ddval, ddglu, ddgate], axis=1)
    g_o_w_in = _mm_tn(h1, dz1, N_CHIPS, "o_in_dw")
    dh1 = _mm_nt(dz1, p["o_w_in"], f32, "o_in_dx")
    dx1, dy0, g_o_pre, g_e_post = _mid_norm_bwd(dx2, dh1, x1, y0, p["o_pre_norm"], p["e_post_norm"])

    g_e_w_out = _mm_tn(cat0, dy0, 1, "e_out_dw")
    dcat0 = _mm_nt(dy0, p["e_w_out"], f32, "e_out_dx")
    da, dagate, g_pool_w, g_pool_scale = _pool_bwd(z0, dcat0, pool_w_bf, p["e_pool_scale"])
    dq0, dk0, dv0, dbgate = _attn_bwd_group(0, z0, att, lse, dcat0, tables)
    dq1, dk1, dv1 = _attn_bwd_group(1, z0, att, lse, dcat0, tables)
    dq2, dk2, dv2 = _attn_bwd_group(2, z0, att, lse, dcat0, tables)
    dz0 = jnp.concatenate([da, dagate, dq0, dq1, dq2, dk0, dk1, dk2, dv0, dv1, dv2, dbgate], axis=1)
    g_e_w_in = _mm_tn(h0, dz0, N_CHIPS, "e_in_dw")
    dh0 = _mm_nt(dz0, p["e_w_in"], f32, "e_in_dx")
    grad_x, g_e_pre = _pre_norm_bwd(dx1, dh0, x, p["e_pre_norm"])

    big = {"e_w_in": g_e_w_in, "e_w_out": g_e_w_out.reshape(N_CHIPS, HALF // 2, D_MODEL),
           "o_w_in": g_o_w_in, "o_w_out": g_o_w_out.reshape(N_CHIPS, HALF // 2, D_MODEL)}
    small = {"e_pre_norm": g_e_pre, "e_pool_w": g_pool_w, "e_pool_scale": g_pool_scale, "e_post_norm": g_e_post,
             "o_pre_norm": g_o_pre, "o_sgu_norm_g": g_sgu_g, "o_sgu_norm_b": g_sgu_b, "o_sgu_w": g_sgu_w,
             "o_sgu_b": g_bias_t[:, 0:4].T, "o_conv_w": g_conv_w, "o_conv_b": g_conv_b,
             "o_conv_norm_g": g_cn_g, "o_conv_norm_b": g_cn_b, "o_post_norm": g_o_post}
    return loss, grad_x, big, small


def _place():
    x, y, c = lax.axis_index("x"), lax.axis_index("y"), lax.axis_index("c")
    others = [(1 - x, y), (x, 1 - y), (1 - x, 1 - y)]
    return x, y, c, 2 * x + y, others


def _all_gather(shards, name):
    n = len(shards)

    def body(*refs):
        ins, outs = refs[:n], refs[n:2 * n]
        send_sems, recv_sems, local_sems = refs[2 * n:]
        x, y, c, me, others = _place()
        sibling = (x, y, 1 - c)

        def half(a, chip, core):
            rows = ins[a].shape[0] // 2
            return outs[a].at[chip, pl.ds(core * rows, rows), :]

        def copy(a, k, src, dst, to):
            return pltpu.make_async_remote_copy(src_ref=src, dst_ref=dst, send_sem=send_sems.at[6 * a + k],
                                                recv_sem=recv_sems.at[6 * a + k], device_id=to, device_id_type=MESH)

        local = [pltpu.make_async_copy(ins[a], outs[a].at[me], local_sems.at[a]) for a in range(n)]
        for cp in local:
            cp.start()
        sent = []
        for a in range(n):
            rows = ins[a].shape[0] // 2
            mine = ins[a].at[pl.ds(c * rows, rows), :]
            for k, (ox, oy) in enumerate(others):
                sent.append(copy(a, k, mine, half(a, me, c), (ox, oy, c)))
                sent[-1].start()
        for a in range(n):
            for k, (ox, oy) in enumerate(others):
                landed = half(a, 2 * ox + oy, c)
                copy(a, k, landed, landed, (ox, oy, c)).wait_recv()
                sent.append(copy(a, 3 + k, landed, landed, sibling))
                sent[-1].start()
        for k, (ox, oy) in enumerate(others):
            chip = 2 * ox + oy
            for a in range(n):
                theirs = half(a, chip, 1 - c)
                copy(a, 3 + k, theirs, theirs, sibling).wait_recv()
        for cp in sent:
            cp.wait_send()
        for cp in local:
            cp.wait()

    return pl.pallas_call(
        body, name=name, in_specs=[ANY] * n, out_specs=[ANY] * n,
        out_shape=[SDS((N_CHIPS,) + s.shape, s.dtype) for s in shards],
        scratch_shapes=[pltpu.SemaphoreType.DMA((6 * n,)), pltpu.SemaphoreType.DMA((6 * n,)), pltpu.SemaphoreType.DMA((n,))],
    )(*shards)


def _swap_halves(parts, name):
    n = len(parts)

    def body(*refs):
        ins, own, theirs = refs[:n], refs[n:2 * n], refs[2 * n:3 * n]
        send_sems, recv_sems, local_sems = refs[3 * n:]
        x, y, c, _, _ = _place()
        sibling = (x, y, 1 - c)
        copies = []
        for a in range(n):
            rows = ins[a].shape[1] // 2
            keep = pltpu.make_async_copy(ins[a].at[:, pl.ds(c * rows, rows), :], own[a], local_sems.at[a])
            give = pltpu.make_async_remote_copy(
                src_ref=ins[a].at[:, pl.ds((1 - c) * rows, rows), :], dst_ref=theirs[a], send_sem=send_sems.at[a],
                recv_sem=recv_sems.at[a], device_id=sibling, device_id_type=MESH)
            keep.start()
            give.start()
            copies += [keep, give]
        for cp in copies:
            cp.wait()

    half = [SDS((N_CHIPS, s.shape[1] // 2, s.shape[2]), s.dtype) for s in parts]
    out = pl.pallas_call(
        body, name=name, in_specs=[ANY] * n, out_specs=[ANY] * (2 * n), out_shape=half + half,
        scratch_shapes=[pltpu.SemaphoreType.DMA((n,)), pltpu.SemaphoreType.DMA((n,)), pltpu.SemaphoreType.DMA((n,))],
    )(*parts)
    return out[:n], out[n:]


def _scatter_chips(parts, name):
    n = len(parts)

    def body(*refs):
        ins, outs = refs[:n], refs[n:2 * n]
        send_sems, recv_sems, local_sems = refs[2 * n:]
        x, y, c, me, others = _place()

        def copy(a, k, slot_from, slot_to, chip_xy):
            return pltpu.make_async_remote_copy(
                src_ref=ins[a].at[slot_from], dst_ref=outs[a].at[slot_to], send_sem=send_sems.at[3 * a + k],
                recv_sem=recv_sems.at[3 * a + k], device_id=(chip_xy[0], chip_xy[1], c), device_id_type=MESH)

        keeps, gives = [], []
        for a in range(n):
            keeps.append(pltpu.make_async_copy(ins[a].at[me], outs[a].at[me], local_sems.at[a]))
            keeps[-1].start()
            for k, (ox, oy) in enumerate(others):
                gives.append(copy(a, k, 2 * ox + oy, me, (ox, oy)))
                gives[-1].start()
        for a in range(n):
            for k, (ox, oy) in enumerate(others):
                copy(a, k, me, 2 * ox + oy, (ox, oy)).wait_recv()
        for cp in gives:
            cp.wait_send()
        for cp in keeps:
            cp.wait()

    return pl.pallas_call(
        body, name=name, in_specs=[ANY] * n, out_specs=[ANY] * n, out_shape=[SDS(s.shape, s.dtype) for s in parts],
        scratch_shapes=[pltpu.SemaphoreType.DMA((3 * n,)), pltpu.SemaphoreType.DMA((3 * n,)), pltpu.SemaphoreType.DMA((n,))],
    )(*parts)


def _join_halves(halves, name):
    n = len(halves)

    def body(*refs):
        ins, outs = refs[:n], refs[n:2 * n]
        send_sems, recv_sems, local_sems = refs[2 * n:]
        x, y, c, _, _ = _place()

        def copy(a, core):
            rows = ins[a].shape[0]
            return pltpu.make_async_remote_copy(
                src_ref=ins[a], dst_ref=outs[a].at[pl.ds(core * rows, rows), :], send_sem=send_sems.at[a],
                recv_sem=recv_sems.at[a], device_id=(x, y, 1 - c), device_id_type=MESH)

        keeps, gives = [], []
        for a in range(n):
            rows = ins[a].shape[0]
            keeps.append(pltpu.make_async_copy(ins[a], outs[a].at[pl.ds(c * rows, rows), :], local_sems.at[a]))
            gives.append(copy(a, c))
            keeps[-1].start()
            gives[-1].start()
        for a in range(n):
            copy(a, 1 - c).wait_recv()
        for cp in gives:
            cp.wait_send()
        for cp in keeps:
            cp.wait()

    return pl.pallas_call(
        body, name=name, in_specs=[ANY] * n, out_specs=[ANY] * n,
        out_shape=[SDS((2 * s.shape[0], s.shape[1]), s.dtype) for s in halves],
        scratch_shapes=[pltpu.SemaphoreType.DMA((n,)), pltpu.SemaphoreType.DMA((n,)), pltpu.SemaphoreType.DMA((n,))],
    )(*halves)


def _add_pair(a, b, name):
    _, r, c = a.shape
    tr = 256 if r % 256 == 0 else r // 2 if r > 512 else r

    def body(a_ref, b_ref, o_ref):
        o_ref[...] = (a_ref[...].astype(f32) + b_ref[...].astype(f32)).astype(o_ref.dtype)

    spec = pl.BlockSpec((None, tr, c), lambda j, i: (j, i, 0))
    return pl.pallas_call(body, name=name, grid=(N_CHIPS, r // tr), in_specs=[spec, spec], out_specs=spec,
                          out_shape=SDS(a.shape, a.dtype))(a, b)


def _add_chips(u, name):
    _, r, c = u.shape
    tr = 256 if r % 256 == 0 else r

    def body(u_ref, o_ref):
        o_ref[...] = ((u_ref[0].astype(f32) + u_ref[1].astype(f32)) + u_ref[2].astype(f32)) + u_ref[3].astype(f32)

    return pl.pallas_call(
        body, name=name, grid=(r // tr,), in_specs=[pl.BlockSpec((N_CHIPS, tr, c), lambda i: (0, i, 0))],
        out_specs=pl.BlockSpec((tr, c), lambda i: (i, 0)), out_shape=SDS((r, c), f32))(u)


SWAP_ROWS = 256


def _swap_add(g, name):
    chips, r, c = g.shape
    half = r // 2
    nb = half // SWAP_ROWS
    steps = chips * nb

    def body(core_ref, mine_ref, theirs_ref, out_ref, landing, send_sems, recv_sems, free_sems):
        i = pl.program_id(0)
        slot = i % 2
        x, y, core, _, _ = _place()
        sibling = (x, y, 1 - core)

        @pl.when(i >= 2)
        def _():
            pl.semaphore_wait(free_sems.at[slot], 1)

        send = pltpu.make_async_remote_copy(src_ref=theirs_ref, dst_ref=landing.at[slot], send_sem=send_sems.at[slot],
                                            recv_sem=recv_sems.at[slot], device_id=sibling, device_id_type=MESH)
        send.start()
        send.wait_recv()
        out_ref[...] = (mine_ref[...].astype(f32) + landing[slot].astype(f32)).astype(out_ref.dtype)

        @pl.when(i + 2 < steps)
        def _():
            pl.semaphore_signal(free_sems.at[slot], 1, device_id=sibling, device_id_type=MESH)

        send.wait_send()

    block = (SWAP_ROWS, c)
    grid_spec = pltpu.PrefetchScalarGridSpec(
        num_scalar_prefetch=1, grid=(steps,),
        in_specs=[pl.BlockSpec(block, lambda i, core: ((2 * (i // nb) + core[0]) * nb + i % nb, 0)),
                  pl.BlockSpec(block, lambda i, core: ((2 * (i // nb) + 1 - core[0]) * nb + i % nb, 0))],
        out_specs=pl.BlockSpec(block, lambda i, core: (i, 0)),
        scratch_shapes=[pltpu.VMEM((2, SWAP_ROWS, c), g.dtype), pltpu.SemaphoreType.DMA((2,)),
                        pltpu.SemaphoreType.DMA((2,)), pltpu.SemaphoreType.REGULAR((2,))])
    core = lax.axis_index("c").astype(jnp.int32).reshape(1)
    rows = g.reshape(chips * r, c)
    out = pl.pallas_call(body, name=name, grid_spec=grid_spec, out_shape=SDS((chips * half, c), g.dtype))(core, rows, rows)
    return out.reshape(chips, half, c)


def _add_chips_join(u, name):
    chips, rh, c = u.shape
    nb = rh // SWAP_ROWS

    def body(u_ref, out_hbm, buf, send_sems, recv_sem, local_sems):
        i = pl.program_id(0)
        slot = i % 2
        x, y, core, _, _ = _place()
        sibling = (x, y, 1 - core)

        def copies(s, step):
            rows = pl.ds(pl.multiple_of((core * nb + step) * SWAP_ROWS, SWAP_ROWS), SWAP_ROWS)
            keep = pltpu.make_async_copy(buf.at[s], out_hbm.at[rows, :], local_sems.at[s])
            give = pltpu.make_async_remote_copy(src_ref=buf.at[s], dst_ref=out_hbm.at[rows, :], send_sem=send_sems.at[s],
                                                recv_sem=recv_sem.at[0], device_id=sibling, device_id_type=MESH)
            return keep, give

        def drain(s, step):
            keep, give = copies(s, step)
            keep.wait()
            give.wait_send()

        @pl.when(i >= 2)
        def _():
            drain(slot, i - 2)

        buf[slot] = ((u_ref[0].astype(f32) + u_ref[1].astype(f32)) + u_ref[2].astype(f32)) + u_ref[3].astype(f32)
        keep, give = copies(slot, i)
        keep.start()
        give.start()

        @pl.when(i == nb - 1)
        def _():
            drain(slot, i)
            if nb > 1:
                drain(1 - slot, i - 1)
            theirs = out_hbm.at[pl.ds((1 - core) * rh, rh), :]
            pltpu.make_async_remote_copy(src_ref=theirs, dst_ref=theirs, send_sem=send_sems.at[0], recv_sem=recv_sem.at[0],
                                         device_id=sibling, device_id_type=MESH).wait_recv()

    return pl.pallas_call(
        body, name=name, grid=(nb,), in_specs=[pl.BlockSpec((chips, SWAP_ROWS, c), lambda i: (0, i, 0))],
        out_specs=ANY, out_shape=SDS((2 * rh, c), f32),
        scratch_shapes=[pltpu.VMEM((2, SWAP_ROWS, c), f32), pltpu.SemaphoreType.DMA((2,)),
                        pltpu.SemaphoreType.DMA((1,)), pltpu.SemaphoreType.DMA((2,))],
    )(u)


def _reduce_scatter(parts, tag):
    staged = [p.shape[1] % (2 * SWAP_ROWS) == 0 for p in parts]
    plain = [p for p, s in zip(parts, staged) if not s]
    chip_sums = [None] * len(parts)
    if plain:
        own, theirs = _swap_halves(plain, f"swap_halves_{tag}")
        sums = iter([_add_pair(o, t, f"add_cores_{tag}{i}") for i, (o, t) in enumerate(zip(own, theirs))])
    for i, (p, s) in enumerate(zip(parts, staged)):
        chip_sums[i] = _swap_add(p, f"swap_add_{tag}{i}") if s else next(sums)
    gathered = _scatter_chips(chip_sums, f"scatter_chips_{tag}")
    out = [None] * len(parts)
    if plain:
        halves = [_add_chips(u, f"add_chips_{tag}{i}") for i, (u, s) in enumerate(zip(gathered, staged)) if not s]
        joined = iter(_join_halves(halves, f"join_halves_{tag}"))
    for i, (u, s) in enumerate(zip(gathered, staged)):
        out[i] = _add_chips_join(u, f"add_chips_join_{tag}{i}") if s else next(joined)
    return out


def _adamw(w, g, m, v, name):
    r, c = w.shape
    tr = 128 if r % 128 == 0 else r
    b1c = 1.0 - ADAM_B1 ** ADAM_STEP
    b2c = 1.0 - ADAM_B2 ** ADAM_STEP

    def body(w_ref, g_ref, m_ref, v_ref, d_ref, nm_ref, nv_ref):
        g = g_ref[...]
        nm = ADAM_B1 * m_ref[...] + (1.0 - ADAM_B1) * g
        nv = ADAM_B2 * v_ref[...] + (1.0 - ADAM_B2) * (g * g)
        nm_ref[...] = nm
        nv_ref[...] = nv
        d_ref[...] = -ADAM_LR * ((nm / b1c) / (jnp.sqrt(nv / b2c) + ADAM_EPS) + ADAM_WD * w_ref[...])

    spec = pl.BlockSpec((tr, c), lambda i: (i, 0))
    return pl.pallas_call(body, name=name, grid=(r // tr,), in_specs=[spec] * 4, out_specs=[spec] * 3,
                          out_shape=[SDS((r, c), f32)] * 3)(w, g, m, v)


def _pack(arrays, total_rows=None):
    parts = []
    rows = 0
    for a in arrays:
        flat = a.reshape(-1, LANES)
        pad = -flat.shape[0] % 8
        parts.append(jnp.pad(flat, ((0, pad), (0, 0))))
        rows += flat.shape[0] + pad
    if total_rows is not None:
        parts.append(jnp.zeros((total_rows - rows, LANES), arrays[0].dtype))
    return jnp.concatenate(parts, axis=0)


def _unpack(buf, shapes):
    out = []
    row = 0
    lead = buf.shape[:-2]
    for shape in shapes:
        size = 1
        for s in shape:
            size *= s
        rows = size // LANES
        out.append(buf[..., row:row + rows, :].reshape(lead + tuple(shape)))
        row += rows + (-rows % 8)
    return out


BIG = ("e_w_in", "e_w_out", "o_w_in", "o_w_out")
SHARDED_SMALL = {
    "e_pool_w": ((4, 64, 256), 1), "o_pre_norm": ((512,), 0), "o_sgu_norm_g": ((256,), 0), "o_sgu_norm_b": ((256,), 0),
    "o_conv_w": ((31, 256), 1), "o_conv_b": ((256,), 0), "o_conv_norm_g": ((256,), 0), "o_conv_norm_b": ((256,), 0),
    "o_post_norm": ((512,), 0),
}
REPLICATED_SMALL = {"e_pre_norm": (2048,), "e_pool_scale": (1024,), "e_post_norm": (2048,),
                    "o_sgu_w": (4, 128, 128), "o_sgu_b": (4, 128)}
SMALL_ORDER = ("e_pre_norm", "e_pool_w", "e_pool_scale", "e_post_norm", "o_pre_norm", "o_sgu_norm_g", "o_sgu_norm_b",
               "o_sgu_w", "o_sgu_b", "o_conv_w", "o_conv_b", "o_conv_norm_g", "o_conv_norm_b", "o_post_norm")
ALL_ORDER = ("e_pre_norm", "e_w_in", "e_pool_w", "e_pool_scale", "e_w_out", "e_post_norm", "o_pre_norm", "o_w_in",
             "o_sgu_norm_g", "o_sgu_norm_b", "o_sgu_w", "o_sgu_b", "o_conv_w", "o_conv_b", "o_conv_norm_g",
             "o_conv_norm_b", "o_w_out", "o_post_norm")


def _full_shape(name):
    shape, axis = SHARDED_SMALL[name]
    return tuple(s * N_CHIPS if i == axis else s for i, s in enumerate(shape))


def _from_chips(name, stacked):
    shape, axis = SHARDED_SMALL[name]
    return jnp.moveaxis(stacked, 0, axis).reshape(_full_shape(name))


def _my_shard(name, full, chip):
    shape, axis = SHARDED_SMALL[name]
    return lax.dynamic_slice_in_dim(full, chip * shape[axis], shape[axis], axis)


def kernel(x, e_pre_norm, e_w_in, e_pool_w, e_pool_scale, e_w_out, e_post_norm, o_pre_norm, o_w_in, o_sgu_norm_g, o_sgu_norm_b, o_sgu_w, o_sgu_b, o_conv_w, o_conv_b, o_conv_norm_g, o_conv_norm_b, o_w_out, o_post_norm, loss_target, m_e_pre_norm, m_e_w_in, m_e_pool_w, m_e_pool_scale, m_e_w_out, m_e_post_norm, m_o_pre_norm, m_o_w_in, m_o_sgu_norm_g, m_o_sgu_norm_b, m_o_sgu_w, m_o_sgu_b, m_o_conv_w, m_o_conv_b, m_o_conv_norm_g, m_o_conv_norm_b, m_o_w_out, m_o_post_norm, v_e_pre_norm, v_e_w_in, v_e_pool_w, v_e_pool_scale, v_e_w_out, v_e_post_norm, v_o_pre_norm, v_o_w_in, v_o_sgu_norm_g, v_o_sgu_norm_b, v_o_sgu_w, v_o_sgu_b, v_o_conv_w, v_o_conv_b, v_o_conv_norm_g, v_o_conv_norm_b, v_o_w_out, v_o_post_norm):
    w = dict(e_pre_norm=e_pre_norm, e_w_in=e_w_in, e_pool_w=e_pool_w, e_pool_scale=e_pool_scale, e_w_out=e_w_out,
             e_post_norm=e_post_norm, o_pre_norm=o_pre_norm, o_w_in=o_w_in, o_sgu_norm_g=o_sgu_norm_g,
             o_sgu_norm_b=o_sgu_norm_b, o_sgu_w=o_sgu_w, o_sgu_b=o_sgu_b, o_conv_w=o_conv_w, o_conv_b=o_conv_b,
             o_conv_norm_g=o_conv_norm_g, o_conv_norm_b=o_conv_norm_b, o_w_out=o_w_out, o_post_norm=o_post_norm)
    m = dict(e_pre_norm=m_e_pre_norm, e_w_in=m_e_w_in, e_pool_w=m_e_pool_w, e_pool_scale=m_e_pool_scale,
             e_w_out=m_e_w_out, e_post_norm=m_e_post_norm, o_pre_norm=m_o_pre_norm, o_w_in=m_o_w_in,
             o_sgu_norm_g=m_o_sgu_norm_g, o_sgu_norm_b=m_o_sgu_norm_b, o_sgu_w=m_o_sgu_w, o_sgu_b=m_o_sgu_b,
             o_conv_w=m_o_conv_w, o_conv_b=m_o_conv_b, o_conv_norm_g=m_o_conv_norm_g, o_conv_norm_b=m_o_conv_norm_b,
             o_w_out=m_o_w_out, o_post_norm=m_o_post_norm)
    v = dict(e_pre_norm=v_e_pre_norm, e_w_in=v_e_w_in, e_pool_w=v_e_pool_w, e_pool_scale=v_e_pool_scale,
             e_w_out=v_e_w_out, e_post_norm=v_e_post_norm, o_pre_norm=v_o_pre_norm, o_w_in=v_o_w_in,
             o_sgu_norm_g=v_o_sgu_norm_g, o_sgu_norm_b=v_o_sgu_norm_b, o_sgu_w=v_o_sgu_w, o_sgu_b=v_o_sgu_b,
             o_conv_w=v_o_conv_w, o_conv_b=v_o_conv_b, o_conv_norm_g=v_o_conv_norm_g, o_conv_norm_b=v_o_conv_norm_b,
             o_w_out=v_o_w_out, o_post_norm=v_o_post_norm)
    w, m, v = ({k: a[0] for k, a in d.items()} for d in (w, m, v))
    chip = 2 * lax.axis_index("x") + lax.axis_index("y")

    sharded_names = list(SHARDED_SMALL)
    small_shard = _pack([w[k] for k in sharded_names], total_rows=352)
    gathered = _all_gather([w[k].astype(bf16) for k in BIG] + [small_shard], "gather_weights")
    full = {k: a for k, a in zip(BIG, gathered[:4])}
    full["e_w_out"] = full["e_w_out"].reshape(1, D_MODEL, D_MODEL)
    full["o_w_out"] = full["o_w_out"].reshape(1, D_MODEL, D_MODEL)
    stacked = _unpack(gathered[4], [SHARDED_SMALL[k][0] for k in sharded_names])
    for k, a in zip(sharded_names, stacked):
        full[k] = _from_chips(k, a)
    for k in REPLICATED_SMALL:
        full[k] = w[k]
    for k in ("e_pre_norm", "e_pool_scale", "e_post_norm", "o_pre_norm", "o_sgu_norm_g", "o_sgu_norm_b", "o_conv_b",
              "o_conv_norm_g", "o_conv_norm_b", "o_post_norm"):
        full[k] = full[k].reshape(1, -1)

    loss, grad_x, big, small = _local_step(x[0], loss_target[0], full)

    small_full_shapes = {k: (_full_shape(k) if k in SHARDED_SMALL else REPLICATED_SMALL[k]) for k in SMALL_ORDER}
    small_parts = _pack([small[k].reshape(small_full_shapes[k]) for k in SMALL_ORDER], total_rows=1536)
    reduced = _reduce_scatter([big[k] for k in BIG] + [small_parts.reshape(N_CHIPS, 384, LANES)], "grads")
    grads = {k: a for k, a in zip(BIG, reduced[:4])}
    small_sum = _all_gather([reduced[4]], "gather_small_grads")[0].reshape(1536, LANES)
    for k, a in zip(SMALL_ORDER, _unpack(small_sum, [small_full_shapes[k] for k in SMALL_ORDER])):
        grads[k] = _my_shard(k, a, chip) if k in SHARDED_SMALL else a
    loss = lax.psum(loss[0, 0], ("x", "y", "c"))

    delta, new_m, new_v = {}, {}, {}
    for k in BIG:
        delta[k], new_m[k], new_v[k] = _adamw(w[k], grads[k], m[k], v[k], f"adamw_{k}")
    local_shapes = [w[k].shape for k in SMALL_ORDER]
    packed = [_pack([d[k] for k in SMALL_ORDER]) for d in (w, grads, m, v)]
    for d, buf in zip((delta, new_m, new_v), _adamw(*packed, "adamw_small")):
        for k, a in zip(SMALL_ORDER, _unpack(buf, local_shapes)):
            d[k] = a

    outs = [loss, grad_x[None]]
    for d in (grads, delta, new_m, new_v):
        outs += [d[k][None] for k in ALL_ORDER]
    return tuple(outs)
```

```python
import jax
import jax.numpy as jnp
from jax import lax
from jax.experimental import pallas as pl
from jax.experimental.pallas import tpu as pltpu

f32 = jnp.float32
bf16 = jnp.bfloat16
SDS = jax.ShapeDtypeStruct

SEQ = 2048
D_MODEL = 2048
EPS = 1e-6
NEG = -1e30
HEAD_DIM = 128
ROT_HALF = 16
ROPE_THETA = 500000.0
DILATIONS = (1, 4, 16)
SPAN = 128
N_HEADS = 8
HALF = 1024
POOL_CH = 256
CONV_K = 31
CONV_PAD = 32
CHUNK = 128
N_CHIPS = 4
LANES = 256
SMALL_SHARD_ROWS = 352
SMALL_GRAD_ROWS = 1536
ANY = pl.BlockSpec(memory_space=pl.ANY)
MESH = pl.DeviceIdType.MESH

ADAM_LR = 0.001
ADAM_B1 = 0.9
ADAM_B2 = 0.999
ADAM_EPS = 1e-08
ADAM_WD = 0.01
ADAM_STEP = 10


def _dot(a, b):
    return jnp.dot(a, b, preferred_element_type=f32)


def _dot_nt(a, b):
    return lax.dot_general(a, b, (((1,), (1,)), ((), ())), preferred_element_type=f32)


def _dot_tn(a, b):
    return lax.dot_general(a, b, (((0,), (0,)), ((), ())), preferred_element_type=f32)


def _sigmoid(x):
    return 1.0 / (1.0 + jnp.exp(-x))


def _silu_and_grad(x):
    s = _sigmoid(x)
    return x * s, s * (1.0 + x * (1.0 - s))


def _rms_fwd(x, g):
    r = lax.rsqrt(jnp.mean(x * x, axis=-1, keepdims=True) + EPS)
    return x * r * g


def _rms_bwd(x, g, dout):
    r = lax.rsqrt(jnp.mean(x * x, axis=-1, keepdims=True) + EPS)
    xh = x * r
    dg = jnp.sum(dout * xh, axis=0, keepdims=True)
    dxh = dout * g
    dx = r * (dxh - xh * jnp.mean(dxh * xh, axis=-1, keepdims=True))
    return dx, dg


def _ln_stats(x):
    mu = jnp.mean(x, axis=-1, keepdims=True)
    xc = x - mu
    rstd = lax.rsqrt(jnp.mean(xc * xc, axis=-1, keepdims=True) + EPS)
    return xc * rstd, rstd


def _ln_bwd(xh, rstd, g, dout):
    dg = jnp.sum(dout * xh, axis=0, keepdims=True)
    db = jnp.sum(dout, axis=0, keepdims=True)
    dxh = dout * g
    dx = rstd * (dxh - jnp.mean(dxh, axis=-1, keepdims=True) - xh * jnp.mean(dxh * xh, axis=-1, keepdims=True))
    return dx, dg, db


def _accumulate(ref, value, first):
    @pl.when(first)
    def _():
        ref[...] = value

    @pl.when(jnp.logical_not(first))
    def _():
        ref[...] += value


def _col_tile(ns):
    for t in (1024, 768, 512, 256):
        if ns % t == 0:
            return t
    raise ValueError(ns)


def _mm_nn(a, w, out_dtype, name):
    m, k = a.shape
    j, _, ns = w.shape
    tm, tn = 1024, _col_tile(ns)
    nb = ns // tn

    def body(a_ref, w_ref, o_ref):
        o_ref[...] = _dot(a_ref[...], w_ref[...]).astype(o_ref.dtype)

    return pl.pallas_call(
        body, name=name, grid=(j * nb, m // tm),
        in_specs=[pl.BlockSpec((tm, k), lambda n, i: (i, 0)),
                  pl.BlockSpec((None, k, tn), lambda n, i: (n // nb, 0, n % nb))],
        out_specs=pl.BlockSpec((tm, tn), lambda n, i: (i, n)),
        out_shape=SDS((m, j * ns), out_dtype),
    )(a, w)


def _mm_nt(dz, w, out_dtype, name, after):
    m, _ = dz.shape
    j, k, ns = w.shape
    tm, tk, tn = 1024, 1024, _col_tile(ns)
    nb = ns // tn
    steps = j * nb

    def body(dz_ref, w_ref, after_ref, o_ref, acc_ref):
        r = pl.program_id(2)
        _accumulate(acc_ref, _dot_nt(dz_ref[...], w_ref[...]), r == 0)

        @pl.when(r == steps - 1)
        def _():
            o_ref[...] = acc_ref[...].astype(o_ref.dtype)

    return pl.pallas_call(
        body, name=name, grid=(m // tm, k // tk, steps),
        in_specs=[pl.BlockSpec((tm, tn), lambda i, kk, r: (i, r)),
                  pl.BlockSpec((None, tk, tn), lambda i, kk, r: (r // nb, kk, r % nb)), ANY],
        out_specs=pl.BlockSpec((tm, tk), lambda i, kk, r: (i, kk)),
        out_shape=SDS((m, k), out_dtype),
        scratch_shapes=[pltpu.VMEM((tm, tk), f32)],
    )(dz, w, after)


def _mm_tn(a, dz, j, name):
    m, k = a.shape
    ns = dz.shape[1] // j
    tk, tn = 1024, _col_tile(ns)
    nb = ns // tn

    def body(a_ref, dz_ref, o_ref):
        o_ref[...] = _dot_tn(a_ref[...], dz_ref[...]).astype(o_ref.dtype)

    return pl.pallas_call(
        body, name=name, grid=(k // tk, j * nb),
        in_specs=[pl.BlockSpec((m, tk), lambda kk, n: (0, kk)),
                  pl.BlockSpec((m, tn), lambda kk, n: (0, n))],
        out_specs=pl.BlockSpec((None, tk, tn), lambda kk, n: (n // nb, kk, n % nb)),
        out_shape=SDS((j, k, ns), bf16),
    )(a, dz)


ROWS = 256


def _row_spec(width=D_MODEL, col=0):
    return pl.BlockSpec((ROWS, width), lambda i: (i, col))


def _vec_spec(width=D_MODEL):
    return pl.BlockSpec((1, width), lambda i: (0, 0))


def _pre_norm(x, g):
    def body(x_ref, g_ref, h_ref):
        h_ref[...] = _rms_fwd(x_ref[...], g_ref[...]).astype(bf16)

    return pl.pallas_call(
        body, name="pre_norm", grid=(SEQ // ROWS,), in_specs=[_row_spec(), _vec_spec()],
        out_specs=_row_spec(), out_shape=SDS((SEQ, D_MODEL), bf16))(x, g)


def _mid_norm(x, y, g_post, g_pre):
    def body(x_ref, y_ref, gpost_ref, gpre_ref, x1_ref, h1_ref):
        x1 = x_ref[...] + _rms_fwd(y_ref[...], gpost_ref[...])
        x1_ref[...] = x1
        h1_ref[...] = _rms_fwd(x1, gpre_ref[...]).astype(bf16)

    return pl.pallas_call(
        body, name="mid_norm", grid=(SEQ // ROWS,),
        in_specs=[_row_spec(), _row_spec(), _vec_spec(), _vec_spec()],
        out_specs=[_row_spec(), _row_spec()],
        out_shape=[SDS((SEQ, D_MODEL), f32), SDS((SEQ, D_MODEL), bf16)])(x, y, g_post, g_pre)


def _final_norm_loss(x1, y, g_post, target):
    def body(x1_ref, y_ref, g_ref, t_ref, loss_ref, dx2_ref, dy_ref, dg_ref):
        first = pl.program_id(0) == 0
        y = y_ref[...]
        g = g_ref[...]
        err = x1_ref[...] + _rms_fwd(y, g) - t_ref[...]
        sq = jnp.sum(jnp.sum(err * err, axis=1, keepdims=True), axis=0, keepdims=True)
        _accumulate(loss_ref, sq * (0.5 / D_MODEL), first)
        dx2 = err * (1.0 / D_MODEL)
        dx2_ref[...] = dx2
        dy, dg = _rms_bwd(y, g, dx2)
        dy_ref[...] = dy.astype(bf16)
        _accumulate(dg_ref, dg, first)

    return pl.pallas_call(
        body, name="final_norm_loss", grid=(SEQ // ROWS,),
        in_specs=[_row_spec(), _row_spec(), _vec_spec(), _row_spec()],
        out_specs=[pl.BlockSpec((1, 1), lambda i: (0, 0)), _row_spec(), _row_spec(), _vec_spec()],
        out_shape=[SDS((1, 1), f32), SDS((SEQ, D_MODEL), f32), SDS((SEQ, D_MODEL), bf16), SDS((1, D_MODEL), f32)],
    )(x1, y, g_post, target)


def _mid_norm_bwd(dx2, dh1, x1, y0, g_pre, g_post):
    def body(dx2_ref, dh1_ref, x1_ref, y0_ref, gpre_ref, gpost_ref, dx1_ref, dy0_ref, dgpre_ref, dgpost_ref):
        first = pl.program_id(0) == 0
        d_in, dgpre = _rms_bwd(x1_ref[...], gpre_ref[...], dh1_ref[...])
        dx1 = dx2_ref[...] + d_in
        dx1_ref[...] = dx1
        dy0, dgpost = _rms_bwd(y0_ref[...], gpost_ref[...], dx1)
        dy0_ref[...] = dy0.astype(bf16)
        _accumulate(dgpre_ref, dgpre, first)
        _accumulate(dgpost_ref, dgpost, first)

    return pl.pallas_call(
        body, name="mid_norm_bwd", grid=(SEQ // ROWS,),
        in_specs=[_row_spec(), _row_spec(), _row_spec(), _row_spec(), _vec_spec(), _vec_spec()],
        out_specs=[_row_spec(), _row_spec(), _vec_spec(), _vec_spec()],
        out_shape=[SDS((SEQ, D_MODEL), f32), SDS((SEQ, D_MODEL), bf16), SDS((1, D_MODEL), f32), SDS((1, D_MODEL), f32)],
    )(dx2, dh1, x1, y0, g_pre, g_post)


def _pre_norm_bwd(dx1, dh0, x, g):
    def body(dx1_ref, dh0_ref, x_ref, g_ref, dx_ref, dg_ref):
        d_in, dg = _rms_bwd(x_ref[...], g_ref[...], dh0_ref[...])
        dx_ref[...] = dx1_ref[...] + d_in
        _accumulate(dg_ref, dg, pl.program_id(0) == 0)

    return pl.pallas_call(
        body, name="pre_norm_bwd", grid=(SEQ // ROWS,),
        in_specs=[_row_spec(), _row_spec(), _row_spec(), _vec_spec()],
        out_specs=[_row_spec(), _vec_spec()],
        out_shape=[SDS((SEQ, D_MODEL), f32), SDS((1, D_MODEL), f32)])(dx1, dh0, x, g)


def _pool_count(g):
    row = lax.broadcasted_iota(jnp.int32, (SEQ, 1), 0)
    width = jnp.left_shift(2, g)
    return row, width, jnp.minimum(row + 1, width).astype(f32)


def _trailing_sum(x, row, width):
    s = x
    for k in (1, 2, 4, 8):
        shifted = jnp.where(row >= k, pltpu.roll(s, k, 0), 0.0)
        s = jnp.where(width > k, s + shifted, s)
    return s


def _leading_sum(x, row, width):
    s = x
    for k in (1, 2, 4, 8):
        shifted = jnp.where(row < SEQ - k, pltpu.roll(s, SEQ - k, 0), 0.0)
        s = jnp.where(width > k, s + shifted, s)
    return s


def _pool_specs():
    a_in = pl.BlockSpec((SEQ, POOL_CH), lambda g: (0, g))
    a_gate = pl.BlockSpec((SEQ, POOL_CH), lambda g: (0, 4 + g))
    w = pl.BlockSpec((None, POOL_CH, POOL_CH), lambda g: (g, 0, 0))
    scale = pl.BlockSpec((1, POOL_CH), lambda g: (0, g))
    return a_in, a_gate, w, scale


def _pool_fwd(z0, pool_w, pool_scale):
    def body(a_ref, gate_ref, w_ref, scale_ref, ya_ref):
        row, width, count = _pool_count(pl.program_id(0))
        a = a_ref[...]
        pooled = _trailing_sum(a, row, width) / count - a
        mixed = _dot(pooled.astype(bf16), w_ref[...]) * scale_ref[...]
        gate = gate_ref[...]
        ya_ref[...] = (mixed * gate * _sigmoid(gate)).astype(bf16)

    return pl.pallas_call(
        body, name="pool_fwd", grid=(4,), in_specs=list(_pool_specs()),
        out_specs=pl.BlockSpec((SEQ, POOL_CH), lambda g: (0, g)),
        out_shape=SDS((SEQ, HALF), bf16))(z0, z0, pool_w, pool_scale)


def _pool_bwd(z0, dcat, pool_w, pool_scale):
    def body(a_ref, gate_ref, w_ref, scale_ref, dya_ref, da_ref, dgate_ref, dw_ref, dscale_ref):
        row, width, count = _pool_count(pl.program_id(0))
        a = a_ref[...]
        pooled = (_trailing_sum(a, row, width) / count - a).astype(bf16)
        w = w_ref[...]
        scale = scale_ref[...]
        mixed = _dot(pooled, w)
        silu, dsilu = _silu_and_grad(gate_ref[...])
        dya = dya_ref[...]
        dgate_ref[...] = (dya * mixed * scale * dsilu).astype(bf16)
        dms = dya * silu
        dscale_ref[...] = jnp.sum(dms * mixed, axis=0, keepdims=True)
        dmixed = (dms * scale).astype(bf16)
        dw_ref[...] = _dot_tn(pooled, dmixed)
        dpooled = _dot_nt(dmixed, w)
        da_ref[...] = (_leading_sum(dpooled / count, row, width) - dpooled).astype(bf16)

    a_in, a_gate, w, scale = _pool_specs()
    col = pl.BlockSpec((SEQ, POOL_CH), lambda g: (0, g))
    return pl.pallas_call(
        body, name="pool_bwd", grid=(4,), in_specs=[a_in, a_gate, w, scale, col],
        out_specs=[col, col, w, scale],
        out_shape=[SDS((SEQ, HALF), bf16), SDS((SEQ, HALF), bf16), SDS((4, POOL_CH, POOL_CH), f32), SDS((1, HALF), f32)],
    )(z0, z0, pool_w, pool_scale, dcat)


Q_COL, K_COL, V_COL, BGATE_COL = 16, 40, 64, 88


def _rope_tables():
    pos = jnp.arange(SEQ, dtype=f32)
    inv_freq = jnp.power(ROPE_THETA, -jnp.arange(0, 2 * ROT_HALF, 2, dtype=f32) / (2 * ROT_HALF))
    ang = pos[:, None] * inv_freq[None, :]
    cos, sin = jnp.cos(ang), jnp.sin(ang)
    zeros = jnp.zeros((SEQ, HEAD_DIM - 2 * ROT_HALF), f32)
    zero_half = jnp.zeros((SEQ, ROT_HALF), f32)
    cos_t = jnp.concatenate([cos, cos, zeros + 1.0], axis=1)
    sin_a = jnp.concatenate([-sin, zero_half, zeros], axis=1)
    sin_b = jnp.concatenate([zero_half, sin, zeros], axis=1)
    return cos_t, sin_a, sin_b


def _rope(t, cos_t, sin_a, sin_b):
    return t * cos_t + pltpu.roll(t, HEAD_DIM - ROT_HALF, 1) * sin_a + pltpu.roll(t, ROT_HALF, 1) * sin_b


def _rope_transposed(d, cos_t, sin_a, sin_b):
    return d * cos_t + pltpu.roll(d * sin_a, ROT_HALF, 1) + pltpu.roll(d * sin_b, HEAD_DIM - ROT_HALF, 1)


def _by_residue(dst_ref, src_ref, dilation, dtype):
    if dilation == 1:
        dst_ref[...] = src_ref[...].astype(dtype)
        return
    length = SEQ // dilation
    for r in range(dilation):
        dst_ref[r * length:(r + 1) * length, :] = src_ref[pl.ds(r, length, stride=dilation), :].astype(dtype)


def _by_position(dst_ref, src_ref, dilation):
    if dilation == 1:
        dst_ref[...] = src_ref[...]
        return
    length = SEQ // dilation
    for r in range(dilation):
        dst_ref[pl.ds(r, length, stride=dilation), :] = src_ref[r * length:(r + 1) * length, :]


def _band_masks():
    qi = lax.broadcasted_iota(jnp.int32, (SPAN, SPAN), 0)
    kj = lax.broadcasted_iota(jnp.int32, (SPAN, SPAN), 1)
    return kj <= qi, kj - qi


def _head_spec(col):
    return pl.BlockSpec((SEQ, HEAD_DIM), lambda h: (0, col + h))


def _table_spec():
    return pl.BlockSpec((SEQ, HEAD_DIM), lambda h: (0, 0))


def _attn_fwd(z0, tables):
    scale = HEAD_DIM ** -0.5

    def body(*refs):
        qkv = refs[0:9]
        bg_ref, cos_ref, sa_ref, sb_ref = refs[9:13]
        yb_ref, att_ref, lse_ref = refs[13:16]
        qd, kd, vd, tmp, o_res, l_res, o_pos, l_pos = refs[16:24]
        own_mask, prev_off = _band_masks()
        cos_t, sin_a, sin_b = cos_ref[...], sa_ref[...], sb_ref[...]
        for g, dilation in enumerate(DILATIONS):
            q_ref, k_ref, v_ref = qkv[3 * g:3 * g + 3]
            blocks_per_residue = SEQ // dilation // SPAN
            tmp[...] = _rope(q_ref[...], cos_t, sin_a, sin_b)
            _by_residue(qd, tmp, dilation, bf16)
            tmp[...] = _rope(k_ref[...], cos_t, sin_a, sin_b)
            _by_residue(kd, tmp, dilation, bf16)
            _by_residue(vd, v_ref, dilation, bf16)

            def block(c, carry, dilation=dilation, blocks_per_residue=blocks_per_residue):
                r0 = pl.multiple_of(c * SPAN, SPAN)
                q = qd[pl.ds(r0, SPAN), :]
                s = jnp.where(own_mask, _dot_nt(q, kd[pl.ds(r0, SPAN), :]) * scale, NEG)
                m = jnp.max(s, axis=1, keepdims=True)
                if blocks_per_residue > 1:
                    p0 = pl.multiple_of(jnp.maximum(c - 1, 0) * SPAN, SPAN)
                    first_off = jnp.where(c % blocks_per_residue > 0, 0, SPAN)
                    sp = jnp.where(prev_off >= first_off, _dot_nt(q, kd[pl.ds(p0, SPAN), :]) * scale, NEG)
                    m = jnp.maximum(m, jnp.max(sp, axis=1, keepdims=True))
                    pp = jnp.exp(sp - m)
                p = jnp.exp(s - m)
                den = jnp.sum(p, axis=1, keepdims=True)
                o = _dot(p.astype(bf16), vd[pl.ds(r0, SPAN), :])
                if blocks_per_residue > 1:
                    den = den + jnp.sum(pp, axis=1, keepdims=True)
                    o = o + _dot(pp.astype(bf16), vd[pl.ds(p0, SPAN), :])
                o_res[pl.ds(r0, SPAN), :] = o / den
                l_res[pl.ds(r0, SPAN), :] = jnp.broadcast_to(m + jnp.log(den), (SPAN, HEAD_DIM))
                return carry

            lax.fori_loop(0, SEQ // SPAN, block, 0)
            _by_position(o_pos.at[g], o_res, dilation)
            _by_position(l_pos.at[g], l_res, dilation)

        l0, l1, l2 = l_pos[0], l_pos[1], l_pos[2]
        top = jnp.maximum(jnp.maximum(l0, l1), l2)
        total = top + jnp.log(jnp.exp(l0 - top) + jnp.exp(l1 - top) + jnp.exp(l2 - top))
        att = jnp.exp(l0 - total) * o_pos[0] + jnp.exp(l1 - total) * o_pos[1] + jnp.exp(l2 - total) * o_pos[2]
        att_ref[...] = att
        lse_ref[...] = total
        gate = bg_ref[...]
        yb_ref[...] = (att * gate * _sigmoid(gate)).astype(bf16)

    in_specs = []
    for g in range(3):
        in_specs += [_head_spec(Q_COL + 8 * g), _head_spec(K_COL + 8 * g), _head_spec(V_COL + 8 * g)]
    in_specs += [_head_spec(BGATE_COL), _table_spec(), _table_spec(), _table_spec()]
    out_spec = pl.BlockSpec((SEQ, HEAD_DIM), lambda h: (0, h))
    vm = lambda dt: pltpu.VMEM((SEQ, HEAD_DIM), dt)
    return pl.pallas_call(
        body, name="attn_fwd", grid=(N_HEADS,), in_specs=in_specs, out_specs=[out_spec] * 3,
        out_shape=[SDS((SEQ, HALF), bf16), SDS((SEQ, HALF), f32), SDS((SEQ, HALF), f32)],
        scratch_shapes=[vm(bf16), vm(bf16), vm(bf16), vm(f32), vm(f32), vm(f32),
                        pltpu.VMEM((3, SEQ, HEAD_DIM), f32), pltpu.VMEM((3, SEQ, HEAD_DIM), f32)],
    )(*([z0] * 10), *tables)


def _attn_bwd_group(g, z0, att, lse, dcat, tables):
    scale = HEAD_DIM ** -0.5
    dilation = DILATIONS[g]
    blocks_per_residue = SEQ // dilation // SPAN
    with_gate = g == 0

    def body(*refs):
        q_ref, k_ref, v_ref, bg_ref, att_ref, lse_ref, dyb_ref, cos_ref, sa_ref, sb_ref = refs[0:10]
        n_out = 4 if with_gate else 3
        dq_ref, dk_ref, dv_ref = refs[10:13]
        qd, kd, vd, dod, ld, dd, tmp, aq, ak, av = refs[10 + n_out:20 + n_out]
        own_mask, prev_off = _band_masks()
        cos_t, sin_a, sin_b = cos_ref[...], sa_ref[...], sb_ref[...]
        gate = bg_ref[...]
        silu, dsilu = _silu_and_grad(gate)
        att_v = att_ref[...]
        dyb = dyb_ref[...]
        if with_gate:
            refs[13][...] = (dyb * att_v * dsilu).astype(bf16)
        datt = dyb * silu
        tmp[...] = datt
        _by_residue(dod, tmp, dilation, bf16)
        tmp[...] = jnp.broadcast_to(jnp.sum(datt * att_v, axis=1, keepdims=True), (SEQ, HEAD_DIM))
        _by_residue(dd, tmp, dilation, f32)
        _by_residue(ld, lse_ref, dilation, f32)
        tmp[...] = _rope(q_ref[...], cos_t, sin_a, sin_b)
        _by_residue(qd, tmp, dilation, bf16)
        tmp[...] = _rope(k_ref[...], cos_t, sin_a, sin_b)
        _by_residue(kd, tmp, dilation, bf16)
        _by_residue(vd, v_ref, dilation, bf16)
        ak[...] = jnp.zeros((SEQ, HEAD_DIM), f32)
        av[...] = jnp.zeros((SEQ, HEAD_DIM), f32)

        def block(c, carry):
            r0 = pl.multiple_of(c * SPAN, SPAN)
            rows = pl.ds(r0, SPAN)
            q, k, v, do = qd[rows, :], kd[rows, :], vd[rows, :], dod[rows, :]
            lse_q, delta = ld[rows, :], dd[rows, :]
            p = jnp.where(own_mask, jnp.exp(_dot_nt(q, k) * scale - lse_q), 0.0)
            ds = (p * (_dot_nt(do, v) - delta) * scale).astype(bf16)
            av[rows, :] += _dot_tn(p.astype(bf16), do)
            ak[rows, :] += _dot_tn(ds, q)
            dq = _dot(ds, k)
            if blocks_per_residue > 1:
                p0 = pl.multiple_of(jnp.maximum(c - 1, 0) * SPAN, SPAN)
                prev = pl.ds(p0, SPAN)
                kp, vp = kd[prev, :], vd[prev, :]
                first_off = jnp.where(c % blocks_per_residue > 0, 0, SPAN)
                pp = jnp.where(prev_off >= first_off, jnp.exp(_dot_nt(q, kp) * scale - lse_q), 0.0)
                dsp = (pp * (_dot_nt(do, vp) - delta) * scale).astype(bf16)
                av[prev, :] += _dot_tn(pp.astype(bf16), do)
                ak[prev, :] += _dot_tn(dsp, q)
                dq = dq + _dot(dsp, kp)
            aq[rows, :] = dq
            return carry

        lax.fori_loop(0, SEQ // SPAN, block, 0)
        _by_position(tmp, aq, dilation)
        dq_ref[...] = _rope_transposed(tmp[...], cos_t, sin_a, sin_b).astype(bf16)
        _by_position(tmp, ak, dilation)
        dk_ref[...] = _rope_transposed(tmp[...], cos_t, sin_a, sin_b).astype(bf16)
        _by_position(tmp, av, dilation)
        dv_ref[...] = tmp[...].astype(bf16)

    head = pl.BlockSpec((SEQ, HEAD_DIM), lambda h: (0, h))
    in_specs = [_head_spec(Q_COL + 8 * g), _head_spec(K_COL + 8 * g), _head_spec(V_COL + 8 * g), _head_spec(BGATE_COL),
                head, head, _head_spec(8), _table_spec(), _table_spec(), _table_spec()]
    n_out = 4 if with_gate else 3
    vm = lambda dt: pltpu.VMEM((SEQ, HEAD_DIM), dt)
    return pl.pallas_call(
        body, name=f"attn_bwd_g{g}", grid=(N_HEADS,), in_specs=in_specs, out_specs=[head] * n_out,
        out_shape=[SDS((SEQ, HALF), bf16)] * n_out,
        scratch_shapes=[vm(bf16), vm(bf16), vm(bf16), vm(bf16), vm(f32), vm(f32), vm(f32), vm(f32), vm(f32), vm(f32)],
    )(z0, z0, z0, z0, att, lse, dcat, *tables)


def _sgu_specs():
    chunk = lambda col: pl.BlockSpec((CHUNK, HALF), lambda n: (n, col))
    vec = pl.BlockSpec((1, HALF), lambda n: (0, 0))
    w = pl.BlockSpec((4, CHUNK, CHUNK), lambda n: (0, 0, 0))
    bias = pl.BlockSpec((CHUNK, CHUNK), lambda n: (0, 0))
    return chunk, vec, w, bias


def _sgu_weights(w_ref):
    tril = lax.broadcasted_iota(jnp.int32, (CHUNK, CHUNK), 1) <= lax.broadcasted_iota(jnp.int32, (CHUNK, CHUNK), 0)
    return tril, [jnp.where(tril, w_ref[h], 0.0).astype(bf16) for h in range(4)]


def _sgu_fwd(z1, ln_g, ln_b, sgu_w, bias_t):
    def body(u_ref, v_ref, cg_ref, g_ref, b_ref, w_ref, bias_ref, yc_ref):
        _, ws = _sgu_weights(w_ref)
        xh, _ = _ln_stats(v_ref[...])
        vn = (xh * g_ref[...] + b_ref[...]).astype(bf16)
        for h in range(4):
            cols = slice(h * POOL_CH, (h + 1) * POOL_CH)
            s = _dot(ws[h], vn[:, cols]) + bias_ref[:, h:h + 1]
            gate = cg_ref[:, cols]
            yc_ref[:, cols] = (u_ref[:, cols] * s * gate * _sigmoid(gate)).astype(bf16)

    chunk, vec, w, bias = _sgu_specs()
    return pl.pallas_call(
        body, name="sgu_fwd", grid=(SEQ // CHUNK,),
        in_specs=[chunk(0), chunk(1), chunk(2), vec, vec, w, bias], out_specs=chunk(0),
        out_shape=SDS((SEQ, HALF), bf16))(z1, z1, z1, ln_g, ln_b, sgu_w, bias_t)


def _sgu_bwd(z1, dcat, ln_g, ln_b, sgu_w, bias_t):
    def body(u_ref, v_ref, cg_ref, dyc_ref, g_ref, b_ref, w_ref, bias_ref,
             du_ref, dv_ref, dcg_ref, dw_ref, dbias_ref, dg_ref, db_ref, dvn_ref):
        first = pl.program_id(0) == 0
        tril, ws = _sgu_weights(w_ref)
        xh, rstd = _ln_stats(v_ref[...])
        g = g_ref[...]
        vn = (xh * g + b_ref[...]).astype(bf16)

        @pl.when(first)
        def _():
            dbias_ref[...] = jnp.zeros((CHUNK, CHUNK), f32)

        for h in range(4):
            cols = slice(h * POOL_CH, (h + 1) * POOL_CH)
            vn_h = vn[:, cols]
            s = _dot(ws[h], vn_h) + bias_ref[:, h:h + 1]
            silu, dsilu = _silu_and_grad(cg_ref[:, cols])
            dyc = dyc_ref[:, cols]
            u = u_ref[:, cols]
            du_ref[:, cols] = (dyc * s * silu).astype(bf16)
            dcg_ref[:, cols] = (dyc * u * s * dsilu).astype(bf16)
            ds = dyc * u * silu
            dbias_ref[:, h:h + 1] += jnp.sum(ds, axis=1, keepdims=True)
            ds = ds.astype(bf16)
            _accumulate(dw_ref.at[h], jnp.where(tril, _dot_nt(ds, vn_h), 0.0), first)
            dvn_ref[:, cols] = _dot_tn(ws[h], ds)
        dv, dg, db = _ln_bwd(xh, rstd, g, dvn_ref[...])
        dv_ref[...] = dv.astype(bf16)
        _accumulate(dg_ref, dg, first)
        _accumulate(db_ref, db, first)

    chunk, vec, w, bias = _sgu_specs()
    return pl.pallas_call(
        body, name="sgu_bwd", grid=(SEQ // CHUNK,),
        in_specs=[chunk(0), chunk(1), chunk(2), chunk(0), vec, vec, w, bias],
        out_specs=[chunk(0), chunk(0), chunk(0), w, bias, vec, vec],
        out_shape=[SDS((SEQ, HALF), bf16)] * 3 + [SDS((4, CHUNK, CHUNK), f32), SDS((CHUNK, CHUNK), f32),
                                                   SDS((1, HALF), f32), SDS((1, HALF), f32)],
        scratch_shapes=[pltpu.VMEM((CHUNK, HALF), f32)],
    )(z1, z1, z1, dcat, ln_g, ln_b, sgu_w, bias_t)


CONV_TILE = 128
DVAL_COL, DGLU_COL = 12, 16


def _conv_specs():
    val = pl.BlockSpec((SEQ, POOL_CH), lambda j: (0, DVAL_COL + j))
    glu = pl.BlockSpec((SEQ, POOL_CH), lambda j: (0, DGLU_COL + j))
    w = pl.BlockSpec((CONV_K, POOL_CH), lambda j: (0, j))
    col = pl.BlockSpec((SEQ, POOL_CH), lambda j: (0, j))
    vec = pl.BlockSpec((1, POOL_CH), lambda j: (0, j))
    return val, glu, w, col, vec


def _conv_fwd(z1, conv_w, conv_b):
    def body(val_ref, glu_ref, w_ref, b_ref, out_ref, xpad):
        xpad[0:CONV_PAD, :] = jnp.zeros((CONV_PAD, POOL_CH), f32)
        xpad[CONV_PAD:, :] = val_ref[...] * _sigmoid(glu_ref[...])
        w = w_ref[...]
        bias = b_ref[...]

        def tile(i, carry):
            t0 = pl.multiple_of(i * CONV_TILE, CONV_TILE)
            window = xpad[pl.ds(t0, CONV_TILE + CONV_PAD), :]
            acc = jnp.broadcast_to(bias, (CONV_TILE, POOL_CH))
            for k in range(CONV_K):
                shift = CONV_PAD - (CONV_K - 1) + k
                acc = acc + w[k:k + 1, :] * pltpu.roll(window, CONV_TILE + CONV_PAD - shift, 0)[0:CONV_TILE]
            out_ref[pl.ds(t0, CONV_TILE), :] = acc
            return carry

        lax.fori_loop(0, SEQ // CONV_TILE, tile, 0)

    val, glu, w, col, vec = _conv_specs()
    return pl.pallas_call(
        body, name="conv_fwd", grid=(4,), in_specs=[val, glu, w, vec], out_specs=col,
        out_shape=SDS((SEQ, HALF), f32), scratch_shapes=[pltpu.VMEM((SEQ + CONV_PAD, POOL_CH), f32)],
    )(z1, z1, conv_w, conv_b)


def _conv_bwd(z1, dconv, conv_w):
    def body(val_ref, glu_ref, w_ref, dout_ref, dval_ref, dglu_ref, dw_ref, db_ref, xpad, dpad, dx_ref):
        val = val_ref[...]
        sig = _sigmoid(glu_ref[...])
        xpad[0:CONV_PAD, :] = jnp.zeros((CONV_PAD, POOL_CH), f32)
        xpad[CONV_PAD:, :] = val * sig
        dout = dout_ref[...]
        dpad[0:SEQ, :] = dout
        dpad[SEQ:, :] = jnp.zeros((CONV_PAD, POOL_CH), f32)
        db_ref[...] = jnp.sum(dout, axis=0, keepdims=True)
        dw_ref[...] = jnp.zeros((CONV_K, POOL_CH), f32)
        w = w_ref[...]

        def tile(i, carry):
            t0 = pl.multiple_of(i * CONV_TILE, CONV_TILE)
            x_win = xpad[pl.ds(t0, CONV_TILE + CONV_PAD), :]
            d_win = dpad[pl.ds(t0, CONV_TILE + CONV_PAD), :]
            d_own = d_win[0:CONV_TILE]
            acc = jnp.zeros((CONV_TILE, POOL_CH), f32)
            for k in range(CONV_K):
                shift = CONV_PAD - (CONV_K - 1) + k
                x_k = pltpu.roll(x_win, CONV_TILE + CONV_PAD - shift, 0)[0:CONV_TILE]
                dw_ref[k:k + 1, :] += jnp.sum(d_own * x_k, axis=0, keepdims=True)
                back = CONV_K - 1 - k
                d_k = d_own if back == 0 else pltpu.roll(d_win, CONV_TILE + CONV_PAD - back, 0)[0:CONV_TILE]
                acc = acc + w[k:k + 1, :] * d_k
            dx_ref[pl.ds(t0, CONV_TILE), :] = acc
            return carry

        lax.fori_loop(0, SEQ // CONV_TILE, tile, 0)
        dx = dx_ref[...]
        dval_ref[...] = (dx * sig).astype(bf16)
        dglu_ref[...] = (dx * val * sig * (1.0 - sig)).astype(bf16)

    val, glu, w, col, vec = _conv_specs()
    pad = pltpu.VMEM((SEQ + CONV_PAD, POOL_CH), f32)
    return pl.pallas_call(
        body, name="conv_bwd", grid=(4,), in_specs=[val, glu, w, col], out_specs=[col, col, w, vec],
        out_shape=[SDS((SEQ, HALF), bf16), SDS((SEQ, HALF), bf16), SDS((CONV_K, HALF), f32), SDS((1, HALF), f32)],
        scratch_shapes=[pad, pad, pltpu.VMEM((SEQ, POOL_CH), f32)],
    )(z1, z1, conv_w, dconv)


DGATE_COL = 5


def _conv_norm_fwd(conv, z1, g, b):
    def body(c_ref, gate_ref, g_ref, b_ref, yd_ref):
        xh, _ = _ln_stats(c_ref[...])
        n = xh * g_ref[...] + b_ref[...]
        gate = gate_ref[...]
        yd_ref[...] = (n * _sigmoid(n) * gate * _sigmoid(gate)).astype(bf16)

    return pl.pallas_call(
        body, name="conv_norm_fwd", grid=(SEQ // ROWS,),
        in_specs=[_row_spec(HALF), _row_spec(HALF, DGATE_COL), _vec_spec(HALF), _vec_spec(HALF)],
        out_specs=_row_spec(HALF), out_shape=SDS((SEQ, HALF), bf16))(conv, z1, g, b)


def _conv_norm_bwd(conv, z1, dcat, g, b):
    def body(c_ref, gate_ref, dyd_ref, g_ref, b_ref, dconv_ref, dgate_ref, dg_ref, db_ref):
        first = pl.program_id(0) == 0
        xh, rstd = _ln_stats(c_ref[...])
        g = g_ref[...]
        n_silu, n_dsilu = _silu_and_grad(xh * g + b_ref[...])
        gate_silu, gate_dsilu = _silu_and_grad(gate_ref[...])
        dyd = dyd_ref[...]
        dgate_ref[...] = (dyd * n_silu * gate_dsilu).astype(bf16)
        dconv, dg, db = _ln_bwd(xh, rstd, g, dyd * gate_silu * n_dsilu)
        dconv_ref[...] = dconv
        _accumulate(dg_ref, dg, first)
        _accumulate(db_ref, db, first)

    return pl.pallas_call(
        body, name="conv_norm_bwd", grid=(SEQ // ROWS,),
        in_specs=[_row_spec(HALF), _row_spec(HALF, DGATE_COL), _row_spec(HALF, 1), _vec_spec(HALF), _vec_spec(HALF)],
        out_specs=[_row_spec(HALF), _row_spec(HALF), _vec_spec(HALF), _vec_spec(HALF)],
        out_shape=[SDS((SEQ, HALF), f32), SDS((SEQ, HALF), bf16), SDS((1, HALF), f32), SDS((1, HALF), f32)],
    )(conv, z1, dcat, g, b)


def _step(x, target, w, chip):
    chip_vec = chip.astype(jnp.int32).reshape(1)
    sharded_names = list(SHARDED_SMALL)
    small_shard = _pack([w[k] for k in sharded_names], total_rows=SMALL_SHARD_ROWS)
    small_slot = lax.dynamic_update_slice(jnp.zeros((N_CHIPS, SMALL_SHARD_ROWS, LANES), f32), small_shard[None], (chip, 0, 0))
    slots = [small_slot] + [_cast_into_slot(w[k], chip_vec, f"cast_{k}") for k in BIG]
    sems, bufs, token = _gather_start(slots)
    tables = _rope_tables()

    def vec(k):
        return w[k].reshape(1, -1)

    h0 = _pre_norm(x, vec("e_pre_norm") + token[0, 0])
    small_full, e_w_in = _forward_halves(_gather_wait(bufs[0:2], sems[0:2], h0, "gather_wait_first"), "forward_first")
    p = {k: _from_chips(k, a) for k, a in zip(sharded_names, _unpack(small_full, [SHARDED_SMALL[k][0] for k in sharded_names]))}
    for k in ("o_pre_norm", "o_sgu_norm_g", "o_sgu_norm_b", "o_conv_b", "o_conv_norm_g", "o_conv_norm_b", "o_post_norm"):
        p[k] = p[k].reshape(1, -1)
    pool_w_bf = p["e_pool_w"].astype(bf16)
    bias_t = jnp.pad(w["o_sgu_b"].T, ((0, 0), (0, CHUNK - 4)))

    z0 = _mm_nn(h0, e_w_in, f32, "e_in")
    ya = _pool_fwd(z0, pool_w_bf, vec("e_pool_scale"))
    yb, att, lse = _attn_fwd(z0, tables)
    e_w_out, o_w_in, o_w_out = _forward_halves(_gather_wait(bufs[2:5], sems[2:5], att, "gather_wait_rest"), "forward_rest")
    e_w_out = e_w_out.reshape(1, D_MODEL, D_MODEL)
    o_w_out = o_w_out.reshape(1, D_MODEL, D_MODEL)
    cat0 = jnp.concatenate([ya, yb], axis=1)
    y0 = _mm_nn(cat0, e_w_out, f32, "e_out")
    x1, h1 = _mid_norm(x, y0, vec("e_post_norm"), p["o_pre_norm"])
    z1 = _mm_nn(h1, o_w_in, f32, "o_in")
    yc = _sgu_fwd(z1, p["o_sgu_norm_g"], p["o_sgu_norm_b"], w["o_sgu_w"], bias_t)
    conv = _conv_fwd(z1, p["o_conv_w"], p["o_conv_b"])
    yd = _conv_norm_fwd(conv, z1, p["o_conv_norm_g"], p["o_conv_norm_b"])
    cat1 = jnp.concatenate([yc, yd], axis=1)
    y1 = _mm_nn(cat1, o_w_out, f32, "o_out")
    loss, dx2, dy1, g_o_post = _final_norm_loss(x1, y1, p["o_post_norm"], target)

    in_flight = {}

    def send_off(name, grad):
        sem, sums, land, tok = _scatter_start(_swap_add(grad, f"swap_add_{name}"), f"scatter_start_{name}")
        in_flight[name] = (sem, sums, land)
        return tok

    tok = send_off("o_w_out", _mm_tn(cat1, dy1, 1, "o_out_dw").reshape(N_CHIPS, HALF // 2, D_MODEL))
    dcat1 = _mm_nt(dy1, o_w_out, f32, "o_out_dx", tok)
    du, dv, dcg, g_sgu_w, g_bias_t, g_sgu_g, g_sgu_b = _sgu_bwd(
        z1, dcat1, p["o_sgu_norm_g"] + tok[0, 0], p["o_sgu_norm_b"], w["o_sgu_w"], bias_t)
    dconv, ddgate, g_cn_g, g_cn_b = _conv_norm_bwd(conv, z1, dcat1, p["o_conv_norm_g"], p["o_conv_norm_b"])
    ddval, ddglu, g_conv_w, g_conv_b = _conv_bwd(z1, dconv, p["o_conv_w"])
    dz1 = jnp.concatenate([du, dv, dcg, ddval, ddglu, ddgate], axis=1)
    tok = send_off("o_w_in", _mm_tn(h1, dz1, N_CHIPS, "o_in_dw"))
    dh1 = _mm_nt(dz1, o_w_in, f32, "o_in_dx", tok)
    dx1, dy0, g_o_pre, g_e_post = _mid_norm_bwd(dx2, dh1, x1, y0, p["o_pre_norm"] + tok[0, 0], vec("e_post_norm"))

    tok = send_off("e_w_out", _mm_tn(cat0, dy0, 1, "e_out_dw").reshape(N_CHIPS, HALF // 2, D_MODEL))
    dcat0 = _mm_nt(dy0, e_w_out, f32, "e_out_dx", tok)
    da, dagate, g_pool_w, g_pool_scale = _pool_bwd(z0, dcat0, pool_w_bf, vec("e_pool_scale") + tok[0, 0])
    dq0, dk0, dv0, dbgate = _attn_bwd_group(0, z0, att, lse, dcat0, tables)
    dq1, dk1, dv1 = _attn_bwd_group(1, z0, att, lse, dcat0, tables)
    dq2, dk2, dv2 = _attn_bwd_group(2, z0, att, lse, dcat0, tables)
    dz0 = jnp.concatenate([da, dagate, dq0, dq1, dq2, dk0, dk1, dk2, dv0, dv1, dv2, dbgate], axis=1)
    tok = send_off("e_w_in", _mm_tn(h0, dz0, N_CHIPS, "e_in_dw"))
    dh0 = _mm_nt(dz0, e_w_in, f32, "e_in_dx", tok)
    grad_x, g_e_pre = _pre_norm_bwd(dx1, dh0, x, vec("e_pre_norm") + tok[0, 0])

    small = {"e_pre_norm": g_e_pre, "e_pool_w": g_pool_w, "e_pool_scale": g_pool_scale, "e_post_norm": g_e_post,
             "o_pre_norm": g_o_pre, "o_sgu_norm_g": g_sgu_g, "o_sgu_norm_b": g_sgu_b, "o_sgu_w": g_sgu_w,
             "o_sgu_b": g_bias_t[:, 0:4].T, "o_conv_w": g_conv_w, "o_conv_b": g_conv_b,
             "o_conv_norm_g": g_cn_g, "o_conv_norm_b": g_cn_b, "o_post_norm": g_o_post}
    return loss, grad_x, in_flight, small


def _land(in_flight, name, chip, after):
    sems, sums, land = in_flight[name]
    sums, land = _scatter_wait(sems, sums, land, after, f"scatter_wait_{name}")
    return _add_landed_join(sums, land, chip.astype(jnp.int32).reshape(1), f"add_landed_{name}")


def _place():
    x, y, c = lax.axis_index("x"), lax.axis_index("y"), lax.axis_index("c")
    others = [(1 - x, y), (x, 1 - y), (1 - x, 1 - y)]
    return x, y, c, 2 * x + y, others


def _all_gather(shards, name):
    n = len(shards)

    def body(*refs):
        ins, outs = refs[:n], refs[n:2 * n]
        send_sems, recv_sems, local_sems = refs[2 * n:]
        x, y, c, me, others = _place()
        sibling = (x, y, 1 - c)

        def half(a, chip, core):
            rows = ins[a].shape[0] // 2
            return outs[a].at[chip, pl.ds(core * rows, rows), :]

        def copy(a, k, src, dst, to):
            return pltpu.make_async_remote_copy(src_ref=src, dst_ref=dst, send_sem=send_sems.at[6 * a + k],
                                                recv_sem=recv_sems.at[6 * a + k], device_id=to, device_id_type=MESH)

        local = [pltpu.make_async_copy(ins[a], outs[a].at[me], local_sems.at[a]) for a in range(n)]
        for cp in local:
            cp.start()
        sent = []
        for a in range(n):
            rows = ins[a].shape[0] // 2
            mine = ins[a].at[pl.ds(c * rows, rows), :]
            for k, (ox, oy) in enumerate(others):
                sent.append(copy(a, k, mine, half(a, me, c), (ox, oy, c)))
                sent[-1].start()
        for a in range(n):
            for k, (ox, oy) in enumerate(others):
                landed = half(a, 2 * ox + oy, c)
                copy(a, k, landed, landed, (ox, oy, c)).wait_recv()
                sent.append(copy(a, 3 + k, landed, landed, sibling))
                sent[-1].start()
        for k, (ox, oy) in enumerate(others):
            chip = 2 * ox + oy
            for a in range(n):
                theirs = half(a, chip, 1 - c)
                copy(a, 3 + k, theirs, theirs, sibling).wait_recv()
        for cp in sent:
            cp.wait_send()
        for cp in local:
            cp.wait()

    return pl.pallas_call(
        body, name=name, in_specs=[ANY] * n, out_specs=[ANY] * n,
        out_shape=[SDS((N_CHIPS,) + s.shape, s.dtype) for s in shards],
        scratch_shapes=[pltpu.SemaphoreType.DMA((6 * n,)), pltpu.SemaphoreType.DMA((6 * n,)), pltpu.SemaphoreType.DMA((n,))],
    )(*shards)


def _swap_halves(parts, name):
    n = len(parts)

    def body(*refs):
        ins, own, theirs = refs[:n], refs[n:2 * n], refs[2 * n:3 * n]
        send_sems, recv_sems, local_sems = refs[3 * n:]
        x, y, c, _, _ = _place()
        sibling = (x, y, 1 - c)
        copies = []
        for a in range(n):
            rows = ins[a].shape[1] // 2
            keep = pltpu.make_async_copy(ins[a].at[:, pl.ds(c * rows, rows), :], own[a], local_sems.at[a])
            give = pltpu.make_async_remote_copy(
                src_ref=ins[a].at[:, pl.ds((1 - c) * rows, rows), :], dst_ref=theirs[a], send_sem=send_sems.at[a],
                recv_sem=recv_sems.at[a], device_id=sibling, device_id_type=MESH)
            keep.start()
            give.start()
            copies += [keep, give]
        for cp in copies:
            cp.wait()

    half = [SDS((N_CHIPS, s.shape[1] // 2, s.shape[2]), s.dtype) for s in parts]
    out = pl.pallas_call(
        body, name=name, in_specs=[ANY] * n, out_specs=[ANY] * (2 * n), out_shape=half + half,
        scratch_shapes=[pltpu.SemaphoreType.DMA((n,)), pltpu.SemaphoreType.DMA((n,)), pltpu.SemaphoreType.DMA((n,))],
    )(*parts)
    return out[:n], out[n:]


def _scatter_chips(parts, name):
    n = len(parts)

    def body(*refs):
        ins, outs = refs[:n], refs[n:2 * n]
        send_sems, recv_sems, local_sems = refs[2 * n:]
        x, y, c, me, others = _place()

        def copy(a, k, slot_from, slot_to, chip_xy):
            return pltpu.make_async_remote_copy(
                src_ref=ins[a].at[slot_from], dst_ref=outs[a].at[slot_to], send_sem=send_sems.at[3 * a + k],
                recv_sem=recv_sems.at[3 * a + k], device_id=(chip_xy[0], chip_xy[1], c), device_id_type=MESH)

        keeps, gives = [], []
        for a in range(n):
            keeps.append(pltpu.make_async_copy(ins[a].at[me], outs[a].at[me], local_sems.at[a]))
            keeps[-1].start()
            for k, (ox, oy) in enumerate(others):
                gives.append(copy(a, k, 2 * ox + oy, me, (ox, oy)))
                gives[-1].start()
        for a in range(n):
            for k, (ox, oy) in enumerate(others):
                copy(a, k, me, 2 * ox + oy, (ox, oy)).wait_recv()
        for cp in gives:
            cp.wait_send()
        for cp in keeps:
            cp.wait()

    return pl.pallas_call(
        body, name=name, in_specs=[ANY] * n, out_specs=[ANY] * n, out_shape=[SDS(s.shape, s.dtype) for s in parts],
        scratch_shapes=[pltpu.SemaphoreType.DMA((3 * n,)), pltpu.SemaphoreType.DMA((3 * n,)), pltpu.SemaphoreType.DMA((n,))],
    )(*parts)


def _join_halves(halves, name):
    n = len(halves)

    def body(*refs):
        ins, outs = refs[:n], refs[n:2 * n]
        send_sems, recv_sems, local_sems = refs[2 * n:]
        x, y, c, _, _ = _place()

        def copy(a, core):
            rows = ins[a].shape[0]
            return pltpu.make_async_remote_copy(
                src_ref=ins[a], dst_ref=outs[a].at[pl.ds(core * rows, rows), :], send_sem=send_sems.at[a],
                recv_sem=recv_sems.at[a], device_id=(x, y, 1 - c), device_id_type=MESH)

        keeps, gives = [], []
        for a in range(n):
            rows = ins[a].shape[0]
            keeps.append(pltpu.make_async_copy(ins[a], outs[a].at[pl.ds(c * rows, rows), :], local_sems.at[a]))
            gives.append(copy(a, c))
            keeps[-1].start()
            gives[-1].start()
        for a in range(n):
            copy(a, 1 - c).wait_recv()
        for cp in gives:
            cp.wait_send()
        for cp in keeps:
            cp.wait()

    return pl.pallas_call(
        body, name=name, in_specs=[ANY] * n, out_specs=[ANY] * n,
        out_shape=[SDS((2 * s.shape[0], s.shape[1]), s.dtype) for s in halves],
        scratch_shapes=[pltpu.SemaphoreType.DMA((n,)), pltpu.SemaphoreType.DMA((n,)), pltpu.SemaphoreType.DMA((n,))],
    )(*halves)


def _add_pair(a, b, name):
    _, r, c = a.shape
    tr = 256 if r % 256 == 0 else r // 2 if r > 512 else r

    def body(a_ref, b_ref, o_ref):
        o_ref[...] = (a_ref[...].astype(f32) + b_ref[...].astype(f32)).astype(o_ref.dtype)

    spec = pl.BlockSpec((None, tr, c), lambda j, i: (j, i, 0))
    return pl.pallas_call(body, name=name, grid=(N_CHIPS, r // tr), in_specs=[spec, spec], out_specs=spec,
                          out_shape=SDS(a.shape, a.dtype))(a, b)


def _add_chips(u, name):
    _, r, c = u.shape
    tr = 256 if r % 256 == 0 else r

    def body(u_ref, o_ref):
        o_ref[...] = ((u_ref[0].astype(f32) + u_ref[1].astype(f32)) + u_ref[2].astype(f32)) + u_ref[3].astype(f32)

    return pl.pallas_call(
        body, name=name, grid=(r // tr,), in_specs=[pl.BlockSpec((N_CHIPS, tr, c), lambda i: (0, i, 0))],
        out_specs=pl.BlockSpec((tr, c), lambda i: (i, 0)), out_shape=SDS((r, c), f32))(u)


SWAP_ROWS = 256


def _swap_add(g, name):
    chips, r, c = g.shape
    half = r // 2
    nb = half // SWAP_ROWS
    steps = chips * nb

    def body(core_ref, mine_ref, theirs_ref, out_ref, landing, send_sems, recv_sems, free_sems):
        i = pl.program_id(0)
        slot = i % 2
        x, y, core, _, _ = _place()
        sibling = (x, y, 1 - core)

        @pl.when(i >= 2)
        def _():
            pl.semaphore_wait(free_sems.at[slot], 1)

        send = pltpu.make_async_remote_copy(src_ref=theirs_ref, dst_ref=landing.at[slot], send_sem=send_sems.at[slot],
                                            recv_sem=recv_sems.at[slot], device_id=sibling, device_id_type=MESH)
        send.start()
        send.wait_recv()
        out_ref[...] = (mine_ref[...].astype(f32) + landing[slot].astype(f32)).astype(out_ref.dtype)

        @pl.when(i + 2 < steps)
        def _():
            pl.semaphore_signal(free_sems.at[slot], 1, device_id=sibling, device_id_type=MESH)

        send.wait_send()

    block = (SWAP_ROWS, c)
    grid_spec = pltpu.PrefetchScalarGridSpec(
        num_scalar_prefetch=1, grid=(steps,),
        in_specs=[pl.BlockSpec(block, lambda i, core: ((2 * (i // nb) + core[0]) * nb + i % nb, 0)),
                  pl.BlockSpec(block, lambda i, core: ((2 * (i // nb) + 1 - core[0]) * nb + i % nb, 0))],
        out_specs=pl.BlockSpec(block, lambda i, core: (i, 0)),
        scratch_shapes=[pltpu.VMEM((2, SWAP_ROWS, c), g.dtype), pltpu.SemaphoreType.DMA((2,)),
                        pltpu.SemaphoreType.DMA((2,)), pltpu.SemaphoreType.REGULAR((2,))])
    core = lax.axis_index("c").astype(jnp.int32).reshape(1)
    rows = g.reshape(chips * r, c)
    out = pl.pallas_call(body, name=name, grid_spec=grid_spec, out_shape=SDS((chips * half, c), g.dtype))(core, rows, rows)
    return out.reshape(chips, half, c)


def _add_chips_join(u, name):
    chips, rh, c = u.shape
    nb = rh // SWAP_ROWS

    def body(u_ref, out_hbm, buf, send_sems, recv_sem, local_sems):
        i = pl.program_id(0)
        slot = i % 2
        x, y, core, _, _ = _place()
        sibling = (x, y, 1 - core)

        def copies(s, step):
            rows = pl.ds(pl.multiple_of((core * nb + step) * SWAP_ROWS, SWAP_ROWS), SWAP_ROWS)
            keep = pltpu.make_async_copy(buf.at[s], out_hbm.at[rows, :], local_sems.at[s])
            give = pltpu.make_async_remote_copy(src_ref=buf.at[s], dst_ref=out_hbm.at[rows, :], send_sem=send_sems.at[s],
                                                recv_sem=recv_sem.at[0], device_id=sibling, device_id_type=MESH)
            return keep, give

        def drain(s, step):
            keep, give = copies(s, step)
            keep.wait()
            give.wait_send()

        @pl.when(i >= 2)
        def _():
            drain(slot, i - 2)

        buf[slot] = ((u_ref[0].astype(f32) + u_ref[1].astype(f32)) + u_ref[2].astype(f32)) + u_ref[3].astype(f32)
        keep, give = copies(slot, i)
        keep.start()
        give.start()

        @pl.when(i == nb - 1)
        def _():
            drain(slot, i)
            if nb > 1:
                drain(1 - slot, i - 1)
            theirs = out_hbm.at[pl.ds((1 - core) * rh, rh), :]
            pltpu.make_async_remote_copy(src_ref=theirs, dst_ref=theirs, send_sem=send_sems.at[0], recv_sem=recv_sem.at[0],
                                         device_id=sibling, device_id_type=MESH).wait_recv()

    return pl.pallas_call(
        body, name=name, grid=(nb,), in_specs=[pl.BlockSpec((chips, SWAP_ROWS, c), lambda i: (0, i, 0))],
        out_specs=ANY, out_shape=SDS((2 * rh, c), f32),
        scratch_shapes=[pltpu.VMEM((2, SWAP_ROWS, c), f32), pltpu.SemaphoreType.DMA((2,)),
                        pltpu.SemaphoreType.DMA((1,)), pltpu.SemaphoreType.DMA((2,))],
    )(u)


def _reduce_scatter(parts, tag):
    staged = [p.shape[1] % (2 * SWAP_ROWS) == 0 for p in parts]
    plain = [p for p, s in zip(parts, staged) if not s]
    chip_sums = [None] * len(parts)
    if plain:
        own, theirs = _swap_halves(plain, f"swap_halves_{tag}")
        sums = iter([_add_pair(o, t, f"add_cores_{tag}{i}") for i, (o, t) in enumerate(zip(own, theirs))])
    for i, (p, s) in enumerate(zip(parts, staged)):
        chip_sums[i] = _swap_add(p, f"swap_add_{tag}{i}") if s else next(sums)
    gathered = _scatter_chips(chip_sums, f"scatter_chips_{tag}")
    out = [None] * len(parts)
    if plain:
        halves = [_add_chips(u, f"add_chips_{tag}{i}") for i, (u, s) in enumerate(zip(gathered, staged)) if not s]
        joined = iter(_join_halves(halves, f"join_halves_{tag}"))
    for i, (u, s) in enumerate(zip(gathered, staged)):
        out[i] = _add_chips_join(u, f"add_chips_join_{tag}{i}") if s else next(joined)
    return out


HBM = pl.BlockSpec(memory_space=pltpu.HBM)
SEM = pl.BlockSpec(memory_space=pltpu.SEMAPHORE)
EFFECT = pltpu.SideEffectType.DATAFLOW_SIDE_EFFECTING


def _in_hbm(a):
    return pltpu.with_memory_space_constraint(a, pltpu.HBM)


def _cast_into_slot(w, chip, name):
    r, c = w.shape
    nb = r // SWAP_ROWS

    def body(chip_ref, w_ref, o_ref):
        o_ref[...] = w_ref[...].astype(bf16)

    grid_spec = pltpu.PrefetchScalarGridSpec(
        num_scalar_prefetch=1, grid=(nb,),
        in_specs=[pl.BlockSpec((SWAP_ROWS, c), lambda i, chip: (i, 0))],
        out_specs=pl.BlockSpec((SWAP_ROWS, c), lambda i, chip: (chip[0] * nb + i, 0)))
    out = pl.pallas_call(body, name=name, grid_spec=grid_spec, out_shape=SDS((N_CHIPS * r, c), bf16))(chip, w)
    return out.reshape(N_CHIPS, r, c)


def _gather_start(bufs):
    n = len(bufs)

    def body(*refs):
        ins, sems, token = refs[:n], refs[n:3 * n], refs[4 * n]
        x, y, c, me, others = _place()
        for a in range(n):
            rows = ins[a].shape[1] // 2
            mine = ins[a].at[me, pl.ds(c * rows, rows), :]
            for k, (ox, oy) in enumerate(others):
                pltpu.make_async_remote_copy(src_ref=mine, dst_ref=mine, send_sem=sems[2 * a].at[k],
                                             recv_sem=sems[2 * a + 1].at[k], device_id=(ox, oy, c),
                                             device_id_type=MESH).start()
        token[...] = jnp.zeros_like(token)

    out = pl.pallas_call(
        body, name="gather_start", in_specs=[HBM] * n,
        out_shape=(*[pltpu.SemaphoreType.DMA((3,))] * (2 * n), *[pltpu.HBM(b.shape, b.dtype) for b in bufs],
                   SDS((8, 128), f32)),
        out_specs=(*[SEM] * (2 * n), *[HBM] * n, pl.BlockSpec(memory_space=pltpu.VMEM)),
        input_output_aliases={a: 2 * n + a for a in range(n)},
        compiler_params=pltpu.CompilerParams(has_side_effects=EFFECT),
    )(*[_in_hbm(b) for b in bufs])
    return [(out[2 * a], out[2 * a + 1]) for a in range(n)], list(out[2 * n:3 * n]), out[3 * n]


def _gather_wait(bufs, sems, after, name):
    n = len(bufs)

    def body(*refs):
        ins, sem_refs = refs[:n], refs[n:3 * n]
        x, y, c, me, others = _place()
        for a in range(n):
            rows = ins[a].shape[1] // 2
            mine = ins[a].at[me, pl.ds(c * rows, rows), :]
            for k, (ox, oy) in enumerate(others):
                landed = ins[a].at[2 * ox + oy, pl.ds(c * rows, rows), :]
                copy = pltpu.make_async_remote_copy(src_ref=mine, dst_ref=landed, send_sem=sem_refs[2 * a].at[k],
                                                    recv_sem=sem_refs[2 * a + 1].at[k], device_id=(ox, oy, c),
                                                    device_id_type=MESH)
                copy.wait_send()
                copy.wait_recv()

    flat_sems = [s for pair in sems for s in pair]
    out = pl.pallas_call(
        body, name=name, in_specs=[HBM] * n + [SEM] * (2 * n) + [ANY],
        out_shape=tuple(pltpu.HBM(b.shape, b.dtype) for b in bufs), out_specs=tuple([HBM] * n),
        input_output_aliases={a: a for a in range(n)},
        compiler_params=pltpu.CompilerParams(has_side_effects=EFFECT),
    )(*bufs, *flat_sems, after)
    return list(out)


def _forward_halves(bufs, name):
    n = len(bufs)
    blocks = []
    for b in bufs:
        half = b.shape[1] // 2
        tr = SWAP_ROWS if half % SWAP_ROWS == 0 else half
        blocks.append((half, tr))
    work = [(a, k, b) for a in range(n) for k in range(3) for b in range(blocks[a][0] // blocks[a][1])]

    def body(*refs):
        outs, stages = refs[n:2 * n], refs[2 * n:3 * n]
        load_sems, send_sems, recv_sems = refs[3 * n:]
        x, y, c, me, others = _place()
        sibling = (x, y, 1 - c)

        def rows(item):
            a, k, b = item
            half, tr = blocks[a]
            ox, oy = others[k]
            return outs[a].at[2 * ox + oy, pl.ds(c * half + b * tr, tr), :]

        def load(s, item):
            return pltpu.make_async_copy(rows(item), stages[item[0]].at[s], load_sems.at[s])

        def send(s, item):
            return pltpu.make_async_remote_copy(src_ref=stages[item[0]].at[s], dst_ref=rows(item), send_sem=send_sems.at[s],
                                                recv_sem=recv_sems.at[item[0]], device_id=sibling, device_id_type=MESH)

        load(0, work[0]).start()
        for t, item in enumerate(work):
            s = t % 2
            load(s, item).wait()
            send(s, item).start()
            if t + 1 < len(work):
                if t >= 1:
                    send(1 - s, work[t - 1]).wait_send()
                load(1 - s, work[t + 1]).start()
        if len(work) > 1:
            send(len(work) % 2, work[-2]).wait_send()
        send((len(work) - 1) % 2, work[-1]).wait_send()
        for a in range(n):
            theirs = outs[a].at[pl.ds(0, 3), pl.ds(0, blocks[a][0]), :]
            pltpu.make_async_remote_copy(src_ref=theirs, dst_ref=theirs, send_sem=send_sems.at[0], recv_sem=recv_sems.at[a],
                                         device_id=sibling, device_id_type=MESH).wait_recv()

    out = pl.pallas_call(
        body, name=name, in_specs=[ANY] * n, out_specs=[ANY] * n, out_shape=[SDS(b.shape, b.dtype) for b in bufs],
        input_output_aliases={a: a for a in range(n)},
        scratch_shapes=[pltpu.VMEM((2, blocks[a][1], bufs[a].shape[2]), bufs[a].dtype) for a in range(n)]
        + [pltpu.SemaphoreType.DMA((2,)), pltpu.SemaphoreType.DMA((2,)), pltpu.SemaphoreType.DMA((n,))],
    )(*bufs)
    return list(out)


def _scatter_start(chip_sums, name):
    def body(a_ref, land_ref, send_sems, recv_sems, a_thru, land_thru, token):
        x, y, c, me, others = _place()
        for k, (ox, oy) in enumerate(others):
            pltpu.make_async_remote_copy(src_ref=a_ref.at[2 * ox + oy], dst_ref=land_ref.at[me], send_sem=send_sems.at[k],
                                         recv_sem=recv_sems.at[k], device_id=(ox, oy, c), device_id_type=MESH).start()
        token[...] = jnp.zeros_like(token)

    shape = pltpu.HBM(chip_sums.shape, chip_sums.dtype)
    send, recv, a_thru, land, token = pl.pallas_call(
        body, name=name, in_specs=[HBM, HBM],
        out_shape=(pltpu.SemaphoreType.DMA((3,)), pltpu.SemaphoreType.DMA((3,)), shape, shape, SDS((8, 128), f32)),
        out_specs=(SEM, SEM, HBM, HBM, pl.BlockSpec(memory_space=pltpu.VMEM)), input_output_aliases={0: 2, 1: 3},
        compiler_params=pltpu.CompilerParams(has_side_effects=EFFECT),
    )(_in_hbm(chip_sums), _in_hbm(lax.empty(chip_sums.shape, chip_sums.dtype)))
    return (send, recv), a_thru, land, token


def _scatter_wait(sems, chip_sums, land, after, name):
    def body(a_ref, land_ref, send_sems, recv_sems, after_ref, a_out, land_out):
        x, y, c, me, others = _place()
        for k, (ox, oy) in enumerate(others):
            copy = pltpu.make_async_remote_copy(
                src_ref=a_ref.at[2 * ox + oy], dst_ref=land_ref.at[2 * ox + oy], send_sem=send_sems.at[k],
                recv_sem=recv_sems.at[k], device_id=(ox, oy, c), device_id_type=MESH)
            copy.wait_send()
            copy.wait_recv()

    shape = pltpu.HBM(chip_sums.shape, chip_sums.dtype)
    return pl.pallas_call(
        body, name=name, in_specs=[HBM, HBM, SEM, SEM, ANY], out_shape=(shape, shape), out_specs=(HBM, HBM),
        input_output_aliases={0: 0, 1: 1}, compiler_params=pltpu.CompilerParams(has_side_effects=EFFECT),
    )(chip_sums, land, sems[0], sems[1], after)


def _add_landed_join(chip_sums, land, chip, name):
    chips, rh, c = chip_sums.shape
    nb = rh // SWAP_ROWS

    def body(chip_ref, own_ref, l1_ref, l2_ref, l3_ref, out_hbm, buf, send_sems, recv_sem, local_sems):
        i = pl.program_id(0)
        slot = i % 2
        x, y, core, _, _ = _place()
        sibling = (x, y, 1 - core)

        def copies(s, step):
            rows = pl.ds(pl.multiple_of((core * nb + step) * SWAP_ROWS, SWAP_ROWS), SWAP_ROWS)
            keep = pltpu.make_async_copy(buf.at[s], out_hbm.at[rows, :], local_sems.at[s])
            give = pltpu.make_async_remote_copy(src_ref=buf.at[s], dst_ref=out_hbm.at[rows, :], send_sem=send_sems.at[s],
                                                recv_sem=recv_sem.at[0], device_id=sibling, device_id_type=MESH)
            return keep, give

        def drain(s, step):
            keep, give = copies(s, step)
            keep.wait()
            give.wait_send()

        @pl.when(i >= 2)
        def _():
            drain(slot, i - 2)

        buf[slot] = ((own_ref[...].astype(f32) + l1_ref[...].astype(f32)) + l2_ref[...].astype(f32)) + l3_ref[...].astype(f32)
        keep, give = copies(slot, i)
        keep.start()
        give.start()

        @pl.when(i == nb - 1)
        def _():
            drain(slot, i)
            if nb > 1:
                drain(1 - slot, i - 1)
            theirs = out_hbm.at[pl.ds((1 - core) * rh, rh), :]
            pltpu.make_async_remote_copy(src_ref=theirs, dst_ref=theirs, send_sem=send_sems.at[0], recv_sem=recv_sem.at[0],
                                         device_id=sibling, device_id_type=MESH).wait_recv()

    block = (SWAP_ROWS, c)
    from_slot = lambda d: pl.BlockSpec(block, lambda i, chip: (((chip[0] + d) % chips) * nb + i, 0))
    grid_spec = pltpu.PrefetchScalarGridSpec(
        num_scalar_prefetch=1, grid=(nb,), in_specs=[from_slot(0), from_slot(1), from_slot(2), from_slot(3)],
        out_specs=ANY,
        scratch_shapes=[pltpu.VMEM((2, SWAP_ROWS, c), f32), pltpu.SemaphoreType.DMA((2,)),
                        pltpu.SemaphoreType.DMA((1,)), pltpu.SemaphoreType.DMA((2,))])
    land_rows = land.reshape(chips * rh, c)
    return pl.pallas_call(body, name=name, grid_spec=grid_spec, out_shape=SDS((2 * rh, c), f32))(
        chip, chip_sums.reshape(chips * rh, c), land_rows, land_rows, land_rows)


def _adamw(w, g, m, v, name):
    r, c = w.shape
    tr = 128 if r % 128 == 0 else r
    b1c = 1.0 - ADAM_B1 ** ADAM_STEP
    b2c = 1.0 - ADAM_B2 ** ADAM_STEP

    def body(w_ref, g_ref, m_ref, v_ref, d_ref, nm_ref, nv_ref):
        g = g_ref[...]
        nm = ADAM_B1 * m_ref[...] + (1.0 - ADAM_B1) * g
        nv = ADAM_B2 * v_ref[...] + (1.0 - ADAM_B2) * (g * g)
        nm_ref[...] = nm
        nv_ref[...] = nv
        d_ref[...] = -ADAM_LR * ((nm / b1c) / (jnp.sqrt(nv / b2c) + ADAM_EPS) + ADAM_WD * w_ref[...])

    spec = pl.BlockSpec((tr, c), lambda i: (i, 0))
    return pl.pallas_call(body, name=name, grid=(r // tr,), in_specs=[spec] * 4, out_specs=[spec] * 3,
                          out_shape=[SDS((r, c), f32)] * 3)(w, g, m, v)


def _pack(arrays, total_rows=None):
    parts = []
    rows = 0
    for a in arrays:
        flat = a.reshape(-1, LANES)
        pad = -flat.shape[0] % 8
        parts.append(jnp.pad(flat, ((0, pad), (0, 0))))
        rows += flat.shape[0] + pad
    if total_rows is not None:
        parts.append(jnp.zeros((total_rows - rows, LANES), arrays[0].dtype))
    return jnp.concatenate(parts, axis=0)


def _unpack(buf, shapes):
    out = []
    row = 0
    lead = buf.shape[:-2]
    for shape in shapes:
        size = 1
        for s in shape:
            size *= s
        rows = size // LANES
        out.append(buf[..., row:row + rows, :].reshape(lead + tuple(shape)))
        row += rows + (-rows % 8)
    return out


BIG = ("e_w_in", "e_w_out", "o_w_in", "o_w_out")
SHARDED_SMALL = {
    "e_pool_w": ((4, 64, 256), 1), "o_pre_norm": ((512,), 0), "o_sgu_norm_g": ((256,), 0), "o_sgu_norm_b": ((256,), 0),
    "o_conv_w": ((31, 256), 1), "o_conv_b": ((256,), 0), "o_conv_norm_g": ((256,), 0), "o_conv_norm_b": ((256,), 0),
    "o_post_norm": ((512,), 0),
}
REPLICATED_SMALL = {"e_pre_norm": (2048,), "e_pool_scale": (1024,), "e_post_norm": (2048,),
                    "o_sgu_w": (4, 128, 128), "o_sgu_b": (4, 128)}
SMALL_ORDER = ("e_pre_norm", "e_pool_w", "e_pool_scale", "e_post_norm", "o_pre_norm", "o_sgu_norm_g", "o_sgu_norm_b",
               "o_sgu_w", "o_sgu_b", "o_conv_w", "o_conv_b", "o_conv_norm_g", "o_conv_norm_b", "o_post_norm")
ALL_ORDER = ("e_pre_norm", "e_w_in", "e_pool_w", "e_pool_scale", "e_w_out", "e_post_norm", "o_pre_norm", "o_w_in",
             "o_sgu_norm_g", "o_sgu_norm_b", "o_sgu_w", "o_sgu_b", "o_conv_w", "o_conv_b", "o_conv_norm_g",
             "o_conv_norm_b", "o_w_out", "o_post_norm")


def _full_shape(name):
    shape, axis = SHARDED_SMALL[name]
    return tuple(s * N_CHIPS if i == axis else s for i, s in enumerate(shape))


def _from_chips(name, stacked):
    shape, axis = SHARDED_SMALL[name]
    return jnp.moveaxis(stacked, 0, axis).reshape(_full_shape(name))


def _my_shard(name, full, chip):
    shape, axis = SHARDED_SMALL[name]
    return lax.dynamic_slice_in_dim(full, chip * shape[axis], shape[axis], axis)


def kernel(x, e_pre_norm, e_w_in, e_pool_w, e_pool_scale, e_w_out, e_post_norm, o_pre_norm, o_w_in, o_sgu_norm_g, o_sgu_norm_b, o_sgu_w, o_sgu_b, o_conv_w, o_conv_b, o_conv_norm_g, o_conv_norm_b, o_w_out, o_post_norm, loss_target, m_e_pre_norm, m_e_w_in, m_e_pool_w, m_e_pool_scale, m_e_w_out, m_e_post_norm, m_o_pre_norm, m_o_w_in, m_o_sgu_norm_g, m_o_sgu_norm_b, m_o_sgu_w, m_o_sgu_b, m_o_conv_w, m_o_conv_b, m_o_conv_norm_g, m_o_conv_norm_b, m_o_w_out, m_o_post_norm, v_e_pre_norm, v_e_w_in, v_e_pool_w, v_e_pool_scale, v_e_w_out, v_e_post_norm, v_o_pre_norm, v_o_w_in, v_o_sgu_norm_g, v_o_sgu_norm_b, v_o_sgu_w, v_o_sgu_b, v_o_conv_w, v_o_conv_b, v_o_conv_norm_g, v_o_conv_norm_b, v_o_w_out, v_o_post_norm):
    w = dict(e_pre_norm=e_pre_norm, e_w_in=e_w_in, e_pool_w=e_pool_w, e_pool_scale=e_pool_scale, e_w_out=e_w_out,
             e_post_norm=e_post_norm, o_pre_norm=o_pre_norm, o_w_in=o_w_in, o_sgu_norm_g=o_sgu_norm_g,
             o_sgu_norm_b=o_sgu_norm_b, o_sgu_w=o_sgu_w, o_sgu_b=o_sgu_b, o_conv_w=o_conv_w, o_conv_b=o_conv_b,
             o_conv_norm_g=o_conv_norm_g, o_conv_norm_b=o_conv_norm_b, o_w_out=o_w_out, o_post_norm=o_post_norm)
    m = dict(e_pre_norm=m_e_pre_norm, e_w_in=m_e_w_in, e_pool_w=m_e_pool_w, e_pool_scale=m_e_pool_scale,
             e_w_out=m_e_w_out, e_post_norm=m_e_post_norm, o_pre_norm=m_o_pre_norm, o_w_in=m_o_w_in,
             o_sgu_norm_g=m_o_sgu_norm_g, o_sgu_norm_b=m_o_sgu_norm_b, o_sgu_w=m_o_sgu_w, o_sgu_b=m_o_sgu_b,
             o_conv_w=m_o_conv_w, o_conv_b=m_o_conv_b, o_conv_norm_g=m_o_conv_norm_g, o_conv_norm_b=m_o_conv_norm_b,
             o_w_out=m_o_w_out, o_post_norm=m_o_post_norm)
    v = dict(e_pre_norm=v_e_pre_norm, e_w_in=v_e_w_in, e_pool_w=v_e_pool_w, e_pool_scale=v_e_pool_scale,
             e_w_out=v_e_w_out, e_post_norm=v_e_post_norm, o_pre_norm=v_o_pre_norm, o_w_in=v_o_w_in,
             o_sgu_norm_g=v_o_sgu_norm_g, o_sgu_norm_b=v_o_sgu_norm_b, o_sgu_w=v_o_sgu_w, o_sgu_b=v_o_sgu_b,
             o_conv_w=v_o_conv_w, o_conv_b=v_o_conv_b, o_conv_norm_g=v_o_conv_norm_g, o_conv_norm_b=v_o_conv_norm_b,
             o_w_out=v_o_w_out, o_post_norm=v_o_post_norm)
    w, m, v = ({k: a[0] for k, a in d.items()} for d in (w, m, v))
    chip = 2 * lax.axis_index("x") + lax.axis_index("y")

    loss, grad_x, in_flight, small = _step(x[0], loss_target[0], w, chip)

    small_full_shapes = {k: (_full_shape(k) if k in SHARDED_SMALL else REPLICATED_SMALL[k]) for k in SMALL_ORDER}
    small_parts = _pack([small[k].reshape(small_full_shapes[k]) for k in SMALL_ORDER], total_rows=SMALL_GRAD_ROWS)
    reduced = _reduce_scatter([small_parts.reshape(N_CHIPS, SMALL_GRAD_ROWS // N_CHIPS, LANES)], "small")
    small_sum = _all_gather(reduced, "gather_small_grads")[0].reshape(SMALL_GRAD_ROWS, LANES)
    grads = {}
    for k, a in zip(SMALL_ORDER, _unpack(small_sum, [small_full_shapes[k] for k in SMALL_ORDER])):
        grads[k] = _my_shard(k, a, chip) if k in SHARDED_SMALL else a
    loss = lax.psum(loss[0, 0], ("x", "y", "c"))

    delta, new_m, new_v = {}, {}, {}
    after = small_sum
    for k in ("o_w_out", "o_w_in", "e_w_out", "e_w_in"):
        grads[k] = _land(in_flight, k, chip, after)
        delta[k], new_m[k], new_v[k] = _adamw(w[k], grads[k], m[k], v[k], f"adamw_{k}")
        after = delta[k]
    local_shapes = [w[k].shape for k in SMALL_ORDER]
    packed = [_pack([d[k] for k in SMALL_ORDER]) for d in (w, grads, m, v)]
    for d, buf in zip((delta, new_m, new_v), _adamw(*packed, "adamw_small")):
        for k, a in zip(SMALL_ORDER, _unpack(buf, local_shapes)):
            d[k] = a

    outs = [loss, grad_x[None]]
    for d in (grads, delta, new_m, new_v):
        outs += [d[k][None] for k in ALL_ORDER]
    return tuple(outs)
```

```python
import jax
import jax.numpy as jnp
from jax import lax
from jax.experimental import pallas as pl
from jax.experimental.pallas import tpu as pltpu

f32 = jnp.float32
bf16 = jnp.bfloat16
SDS = jax.ShapeDtypeStruct

SEQ = 2048
D_MODEL = 2048
EPS = 1e-6
NEG = -1e30
HEAD_DIM = 128
ROT_HALF = 16
ROPE_THETA = 500000.0
DILATIONS = (1, 4, 16)
SPAN = 128
N_HEADS = 8
HALF = 1024
POOL_CH = 256
CONV_K = 31
CONV_PAD = 32
CHUNK = 128
N_CHIPS = 4
LANES = 256
SMALL_SHARD_ROWS = 352
SMALL_GRAD_ROWS = 1536
ANY = pl.BlockSpec(memory_space=pl.ANY)
MESH = pl.DeviceIdType.MESH

ADAM_LR = 0.001
ADAM_B1 = 0.9
ADAM_B2 = 0.999
ADAM_EPS = 1e-08
ADAM_WD = 0.01
ADAM_STEP = 10


def _dot(a, b):
    return jnp.dot(a, b, preferred_element_type=f32)


def _dot_nt(a, b):
    return lax.dot_general(a, b, (((1,), (1,)), ((), ())), preferred_element_type=f32)


def _dot_tn(a, b):
    return lax.dot_general(a, b, (((0,), (0,)), ((), ())), preferred_element_type=f32)


def _sigmoid(x):
    return 1.0 / (1.0 + jnp.exp(-x))


def _silu_and_grad(x):
    s = _sigmoid(x)
    return x * s, s * (1.0 + x * (1.0 - s))


def _rms_fwd(x, g):
    r = lax.rsqrt(jnp.mean(x * x, axis=-1, keepdims=True) + EPS)
    return x * r * g


def _rms_bwd(x, g, dout):
    r = lax.rsqrt(jnp.mean(x * x, axis=-1, keepdims=True) + EPS)
    xh = x * r
    dg = jnp.sum(dout * xh, axis=0, keepdims=True)
    dxh = dout * g
    dx = r * (dxh - xh * jnp.mean(dxh * xh, axis=-1, keepdims=True))
    return dx, dg


def _ln_stats(x):
    mu = jnp.mean(x, axis=-1, keepdims=True)
    xc = x - mu
    rstd = lax.rsqrt(jnp.mean(xc * xc, axis=-1, keepdims=True) + EPS)
    return xc * rstd, rstd


def _ln_bwd(xh, rstd, g, dout):
    dg = jnp.sum(dout * xh, axis=0, keepdims=True)
    db = jnp.sum(dout, axis=0, keepdims=True)
    dxh = dout * g
    dx = rstd * (dxh - jnp.mean(dxh, axis=-1, keepdims=True) - xh * jnp.mean(dxh * xh, axis=-1, keepdims=True))
    return dx, dg, db


def _accumulate(ref, value, first):
    @pl.when(first)
    def _():
        ref[...] = value

    @pl.when(jnp.logical_not(first))
    def _():
        ref[...] += value


def _col_tile(ns):
    for t in (1024, 768, 512, 256):
        if ns % t == 0:
            return t
    raise ValueError(ns)


def _mm_nn(a, w, out_dtype, name):
    m, k = a.shape
    j, _, ns = w.shape
    tm, tn = 1024, _col_tile(ns)
    nb = ns // tn

    def body(a_ref, w_ref, o_ref):
        o_ref[...] = _dot(a_ref[...], w_ref[...]).astype(o_ref.dtype)

    return pl.pallas_call(
        body, name=name, grid=(j * nb, m // tm),
        in_specs=[pl.BlockSpec((tm, k), lambda n, i: (i, 0)),
                  pl.BlockSpec((None, k, tn), lambda n, i: (n // nb, 0, n % nb))],
        out_specs=pl.BlockSpec((tm, tn), lambda n, i: (i, n)),
        out_shape=SDS((m, j * ns), out_dtype),
    )(a, w)


def _mm_nt(dz, w, out_dtype, name, after):
    m, _ = dz.shape
    j, k, ns = w.shape
    tm, tk, tn = 1024, 1024, _col_tile(ns)
    nb = ns // tn
    steps = j * nb

    def body(dz_ref, w_ref, after_ref, o_ref, acc_ref):
        r = pl.program_id(2)
        _accumulate(acc_ref, _dot_nt(dz_ref[...], w_ref[...]), r == 0)

        @pl.when(r == steps - 1)
        def _():
            o_ref[...] = acc_ref[...].astype(o_ref.dtype)

    return pl.pallas_call(
        body, name=name, grid=(m // tm, k // tk, steps),
        in_specs=[pl.BlockSpec((tm, tn), lambda i, kk, r: (i, r)),
                  pl.BlockSpec((None, tk, tn), lambda i, kk, r: (r // nb, kk, r % nb)), ANY],
        out_specs=pl.BlockSpec((tm, tk), lambda i, kk, r: (i, kk)),
        out_shape=SDS((m, k), out_dtype),
        scratch_shapes=[pltpu.VMEM((tm, tk), f32)],
    )(dz, w, after)


def _mm_tn(a, dz, j, name):
    m, k = a.shape
    ns = dz.shape[1] // j
    tk, tn = 1024, _col_tile(ns)
    nb = ns // tn

    def body(a_ref, dz_ref, o_ref):
        o_ref[...] = _dot_tn(a_ref[...], dz_ref[...]).astype(o_ref.dtype)

    return pl.pallas_call(
        body, name=name, grid=(k // tk, j * nb),
        in_specs=[pl.BlockSpec((m, tk), lambda kk, n: (0, kk)),
                  pl.BlockSpec((m, tn), lambda kk, n: (0, n))],
        out_specs=pl.BlockSpec((None, tk, tn), lambda kk, n: (n // nb, kk, n % nb)),
        out_shape=SDS((j, k, ns), bf16),
    )(a, dz)


ROWS = 256


def _row_spec(width=D_MODEL, col=0):
    return pl.BlockSpec((ROWS, width), lambda i: (i, col))


def _vec_spec(width=D_MODEL):
    return pl.BlockSpec((1, width), lambda i: (0, 0))


def _pre_norm(x, g):
    def body(x_ref, g_ref, h_ref):
        h_ref[...] = _rms_fwd(x_ref[...], g_ref[...]).astype(bf16)

    return pl.pallas_call(
        body, name="pre_norm", grid=(SEQ // ROWS,), in_specs=[_row_spec(), _vec_spec()],
        out_specs=_row_spec(), out_shape=SDS((SEQ, D_MODEL), bf16))(x, g)


def _mid_norm(x, y, g_post, g_pre):
    def body(x_ref, y_ref, gpost_ref, gpre_ref, x1_ref, h1_ref):
        x1 = x_ref[...] + _rms_fwd(y_ref[...], gpost_ref[...])
        x1_ref[...] = x1
        h1_ref[...] = _rms_fwd(x1, gpre_ref[...]).astype(bf16)

    return pl.pallas_call(
        body, name="mid_norm", grid=(SEQ // ROWS,),
        in_specs=[_row_spec(), _row_spec(), _vec_spec(), _vec_spec()],
        out_specs=[_row_spec(), _row_spec()],
        out_shape=[SDS((SEQ, D_MODEL), f32), SDS((SEQ, D_MODEL), bf16)])(x, y, g_post, g_pre)


def _final_norm_loss(x1, y, g_post, target):
    def body(x1_ref, y_ref, g_ref, t_ref, loss_ref, dx2_ref, dy_ref, dg_ref):
        first = pl.program_id(0) == 0
        y = y_ref[...]
        g = g_ref[...]
        err = x1_ref[...] + _rms_fwd(y, g) - t_ref[...]
        sq = jnp.sum(jnp.sum(err * err, axis=1, keepdims=True), axis=0, keepdims=True)
        _accumulate(loss_ref, sq * (0.5 / D_MODEL), first)
        dx2 = err * (1.0 / D_MODEL)
        dx2_ref[...] = dx2
        dy, dg = _rms_bwd(y, g, dx2)
        dy_ref[...] = dy.astype(bf16)
        _accumulate(dg_ref, dg, first)

    return pl.pallas_call(
        body, name="final_norm_loss", grid=(SEQ // ROWS,),
        in_specs=[_row_spec(), _row_spec(), _vec_spec(), _row_spec()],
        out_specs=[pl.BlockSpec((1, 1), lambda i: (0, 0)), _row_spec(), _row_spec(), _vec_spec()],
        out_shape=[SDS((1, 1), f32), SDS((SEQ, D_MODEL), f32), SDS((SEQ, D_MODEL), bf16), SDS((1, D_MODEL), f32)],
    )(x1, y, g_post, target)


def _mid_norm_bwd(dx2, dh1, x1, y0, g_pre, g_post):
    def body(dx2_ref, dh1_ref, x1_ref, y0_ref, gpre_ref, gpost_ref, dx1_ref, dy0_ref, dgpre_ref, dgpost_ref):
        first = pl.program_id(0) == 0
        d_in, dgpre = _rms_bwd(x1_ref[...], gpre_ref[...], dh1_ref[...])
        dx1 = dx2_ref[...] + d_in
        dx1_ref[...] = dx1
        dy0, dgpost = _rms_bwd(y0_ref[...], gpost_ref[...], dx1)
        dy0_ref[...] = dy0.astype(bf16)
        _accumulate(dgpre_ref, dgpre, first)
        _accumulate(dgpost_ref, dgpost, first)

    return pl.pallas_call(
        body, name="mid_norm_bwd", grid=(SEQ // ROWS,),
        in_specs=[_row_spec(), _row_spec(), _row_spec(), _row_spec(), _vec_spec(), _vec_spec()],
        out_specs=[_row_spec(), _row_spec(), _vec_spec(), _vec_spec()],
        out_shape=[SDS((SEQ, D_MODEL), f32), SDS((SEQ, D_MODEL), bf16), SDS((1, D_MODEL), f32), SDS((1, D_MODEL), f32)],
    )(dx2, dh1, x1, y0, g_pre, g_post)


def _pre_norm_bwd(dx1, dh0, x, g):
    def body(dx1_ref, dh0_ref, x_ref, g_ref, dx_ref, dg_ref):
        d_in, dg = _rms_bwd(x_ref[...], g_ref[...], dh0_ref[...])
        dx_ref[...] = dx1_ref[...] + d_in
        _accumulate(dg_ref, dg, pl.program_id(0) == 0)

    return pl.pallas_call(
        body, name="pre_norm_bwd", grid=(SEQ // ROWS,),
        in_specs=[_row_spec(), _row_spec(), _row_spec(), _vec_spec()],
        out_specs=[_row_spec(), _vec_spec()],
        out_shape=[SDS((SEQ, D_MODEL), f32), SDS((1, D_MODEL), f32)])(dx1, dh0, x, g)


def _pool_count(g):
    row = lax.broadcasted_iota(jnp.int32, (SEQ, 1), 0)
    width = jnp.left_shift(2, g)
    return row, width, jnp.minimum(row + 1, width).astype(f32)


def _trailing_sum(x, row, width):
    s = x
    for k in (1, 2, 4, 8):
        shifted = jnp.where(row >= k, pltpu.roll(s, k, 0), 0.0)
        s = jnp.where(width > k, s + shifted, s)
    return s


def _leading_sum(x, row, width):
    s = x
    for k in (1, 2, 4, 8):
        shifted = jnp.where(row < SEQ - k, pltpu.roll(s, SEQ - k, 0), 0.0)
        s = jnp.where(width > k, s + shifted, s)
    return s


def _pool_specs():
    a_in = pl.BlockSpec((SEQ, POOL_CH), lambda g: (0, g))
    a_gate = pl.BlockSpec((SEQ, POOL_CH), lambda g: (0, 4 + g))
    w = pl.BlockSpec((None, POOL_CH, POOL_CH), lambda g: (g, 0, 0))
    scale = pl.BlockSpec((1, POOL_CH), lambda g: (0, g))
    return a_in, a_gate, w, scale


def _pool_fwd(z0, pool_w, pool_scale):
    def body(a_ref, gate_ref, w_ref, scale_ref, ya_ref):
        row, width, count = _pool_count(pl.program_id(0))
        a = a_ref[...]
        pooled = _trailing_sum(a, row, width) / count - a
        mixed = _dot(pooled.astype(bf16), w_ref[...]) * scale_ref[...]
        gate = gate_ref[...]
        ya_ref[...] = (mixed * gate * _sigmoid(gate)).astype(bf16)

    return pl.pallas_call(
        body, name="pool_fwd", grid=(4,), in_specs=list(_pool_specs()),
        out_specs=pl.BlockSpec((SEQ, POOL_CH), lambda g: (0, g)),
        out_shape=SDS((SEQ, HALF), bf16))(z0, z0, pool_w, pool_scale)


def _pool_bwd(z0, dcat, pool_w, pool_scale):
    def body(a_ref, gate_ref, w_ref, scale_ref, dya_ref, da_ref, dgate_ref, dw_ref, dscale_ref):
        row, width, count = _pool_count(pl.program_id(0))
        a = a_ref[...]
        pooled = (_trailing_sum(a, row, width) / count - a).astype(bf16)
        w = w_ref[...]
        scale = scale_ref[...]
        mixed = _dot(pooled, w)
        silu, dsilu = _silu_and_grad(gate_ref[...])
        dya = dya_ref[...]
        dgate_ref[...] = (dya * mixed * scale * dsilu).astype(bf16)
        dms = dya * silu
        dscale_ref[...] = jnp.sum(dms * mixed, axis=0, keepdims=True)
        dmixed = (dms * scale).astype(bf16)
        dw_ref[...] = _dot_tn(pooled, dmixed)
        dpooled = _dot_nt(dmixed, w)
        da_ref[...] = (_leading_sum(dpooled / count, row, width) - dpooled).astype(bf16)

    a_in, a_gate, w, scale = _pool_specs()
    col = pl.BlockSpec((SEQ, POOL_CH), lambda g: (0, g))
    return pl.pallas_call(
        body, name="pool_bwd", grid=(4,), in_specs=[a_in, a_gate, w, scale, col],
        out_specs=[col, col, w, scale],
        out_shape=[SDS((SEQ, HALF), bf16), SDS((SEQ, HALF), bf16), SDS((4, POOL_CH, POOL_CH), f32), SDS((1, HALF), f32)],
    )(z0, z0, pool_w, pool_scale, dcat)


Q_COL, K_COL, V_COL, BGATE_COL = 16, 40, 64, 88


def _rope_tables():
    pos = jnp.arange(SEQ, dtype=f32)
    inv_freq = jnp.power(ROPE_THETA, -jnp.arange(0, 2 * ROT_HALF, 2, dtype=f32) / (2 * ROT_HALF))
    ang = pos[:, None] * inv_freq[None, :]
    cos, sin = jnp.cos(ang), jnp.sin(ang)
    zeros = jnp.zeros((SEQ, HEAD_DIM - 2 * ROT_HALF), f32)
    zero_half = jnp.zeros((SEQ, ROT_HALF), f32)
    cos_t = jnp.concatenate([cos, cos, zeros + 1.0], axis=1)
    sin_a = jnp.concatenate([-sin, zero_half, zeros], axis=1)
    sin_b = jnp.concatenate([zero_half, sin, zeros], axis=1)
    return cos_t, sin_a, sin_b


def _rope(t, cos_t, sin_a, sin_b):
    return t * cos_t + pltpu.roll(t, HEAD_DIM - ROT_HALF, 1) * sin_a + pltpu.roll(t, ROT_HALF, 1) * sin_b


def _rope_transposed(d, cos_t, sin_a, sin_b):
    return d * cos_t + pltpu.roll(d * sin_a, ROT_HALF, 1) + pltpu.roll(d * sin_b, HEAD_DIM - ROT_HALF, 1)


ROW_CHUNK = 256


def _chunks(fn):
    def step(i, carry):
        fn(pl.multiple_of(i * ROW_CHUNK, ROW_CHUNK))
        return carry

    lax.fori_loop(0, SEQ // ROW_CHUNK, step, 0)


def _pieces(dilation):
    length = SEQ // dilation
    n = min(length, ROW_CHUNK)
    return [(r, l0, n) for r in range(dilation) for l0 in range(0, length, n)]


def _by_residue(dst_ref, src_ref, dilation, dtype, dst_off=0):
    length = SEQ // dilation
    for r, l0, n in _pieces(dilation):
        src = src_ref[l0:l0 + n, :] if dilation == 1 else src_ref[pl.ds(r + dilation * l0, n, stride=dilation), :]
        start = dst_off + r * length + l0
        dst_ref[start:start + n, :] = src.astype(dtype)


def _by_position(dst_ref, src_ref, dilation):
    length = SEQ // dilation
    for r, l0, n in _pieces(dilation):
        src = src_ref[r * length + l0:r * length + l0 + n, :]
        if dilation == 1:
            dst_ref[l0:l0 + n, :] = src
        else:
            dst_ref[pl.ds(r + dilation * l0, n, stride=dilation), :] = src


def _window_rule():
    qi = lax.broadcasted_iota(jnp.int32, (SPAN, 2 * SPAN), 0)
    kj = lax.broadcasted_iota(jnp.int32, (SPAN, 2 * SPAN), 1)
    prev = kj < SPAN
    return jnp.where(prev, qi - kj, kj - SPAN - qi), prev


def _own_mask():
    return lax.broadcasted_iota(jnp.int32, (SPAN, SPAN), 1) <= lax.broadcasted_iota(jnp.int32, (SPAN, SPAN), 0)


def _head_spec(col):
    return pl.BlockSpec((SEQ, HEAD_DIM), lambda h: (0, col + h))


def _table_spec():
    return pl.BlockSpec((SEQ, HEAD_DIM), lambda h: (0, 0))


def _attn_fwd(z0, tables):
    scale = HEAD_DIM ** -0.5

    def body(*refs):
        qkv = refs[0:9]
        bg_ref, cos_ref, sa_ref, sb_ref = refs[9:13]
        yb_ref, att_ref, lse_ref = refs[13:16]
        qd, kd, vd, tmp, o_res, l_res, o_pos, l_pos = refs[16:24]
        rule, prev_part = _window_rule()
        own_mask = _own_mask()
        kd[0:SPAN, :] = jnp.zeros((SPAN, HEAD_DIM), bf16)
        vd[0:SPAN, :] = jnp.zeros((SPAN, HEAD_DIM), bf16)

        def rope_into_tmp(src_ref):
            def rows(start):
                r = pl.ds(start, ROW_CHUNK)
                tmp[r, :] = _rope(src_ref[r, :], cos_ref[r, :], sa_ref[r, :], sb_ref[r, :])

            _chunks(rows)

        for g, dilation in enumerate(DILATIONS):
            q_ref, k_ref, v_ref = qkv[3 * g:3 * g + 3]
            blocks_per_residue = SEQ // dilation // SPAN
            rope_into_tmp(q_ref)
            _by_residue(qd, tmp, dilation, bf16)
            rope_into_tmp(k_ref)
            _by_residue(kd, tmp, dilation, bf16, SPAN)
            _by_residue(vd, v_ref, dilation, bf16, SPAN)

            def block(c, carry, blocks_per_residue=blocks_per_residue):
                r0 = pl.multiple_of(c * SPAN, SPAN)
                q = qd[pl.ds(r0, SPAN), :]
                if blocks_per_residue > 1:
                    keys = pl.ds(r0, 2 * SPAN)
                    ok = rule <= jnp.where(prev_part, jnp.where(c % blocks_per_residue > 0, 0, -SPAN), 0)
                else:
                    keys = pl.ds(pl.multiple_of(c * SPAN + SPAN, SPAN), SPAN)
                    ok = own_mask
                s = jnp.where(ok, _dot_nt(q, kd[keys, :]) * scale, NEG)
                m = jnp.max(s, axis=1, keepdims=True)
                p = jnp.exp(s - m)
                den = jnp.sum(p, axis=1, keepdims=True)
                o_res[pl.ds(r0, SPAN), :] = _dot(p.astype(bf16), vd[keys, :]) / den
                l_res[pl.ds(r0, SPAN), :] = jnp.broadcast_to(m + jnp.log(den), (SPAN, HEAD_DIM))
                return carry

            lax.fori_loop(0, SEQ // SPAN, block, 0, unroll=4)
            _by_position(o_pos.at[g], o_res, dilation)
            _by_position(l_pos.at[g], l_res, dilation)

        def merge(start):
            r = pl.ds(start, ROW_CHUNK)
            l0, l1, l2 = l_pos[0, r, :], l_pos[1, r, :], l_pos[2, r, :]
            top = jnp.maximum(jnp.maximum(l0, l1), l2)
            total = top + jnp.log(jnp.exp(l0 - top) + jnp.exp(l1 - top) + jnp.exp(l2 - top))
            att = (jnp.exp(l0 - total) * o_pos[0, r, :] + jnp.exp(l1 - total) * o_pos[1, r, :]
                   + jnp.exp(l2 - total) * o_pos[2, r, :])
            att_ref[r, :] = att
            lse_ref[r, :] = total
            gate = bg_ref[r, :]
            yb_ref[r, :] = (att * gate * _sigmoid(gate)).astype(bf16)

        _chunks(merge)

    in_specs = []
    for g in range(3):
        in_specs += [_head_spec(Q_COL + 8 * g), _head_spec(K_COL + 8 * g), _head_spec(V_COL + 8 * g)]
    in_specs += [_head_spec(BGATE_COL), _table_spec(), _table_spec(), _table_spec()]
    out_spec = pl.BlockSpec((SEQ, HEAD_DIM), lambda h: (0, h))
    vm = lambda dt: pltpu.VMEM((SEQ, HEAD_DIM), dt)
    padded = pltpu.VMEM((SEQ + SPAN, HEAD_DIM), bf16)
    return pl.pallas_call(
        body, name="attn_fwd", grid=(N_HEADS,), in_specs=in_specs, out_specs=[out_spec] * 3,
        out_shape=[SDS((SEQ, HALF), bf16), SDS((SEQ, HALF), f32), SDS((SEQ, HALF), f32)],
        scratch_shapes=[vm(bf16), padded, padded, vm(f32), vm(f32), vm(f32),
                        pltpu.VMEM((3, SEQ, HEAD_DIM), f32), pltpu.VMEM((3, SEQ, HEAD_DIM), f32)],
    )(*([z0] * 10), *tables)


def _attn_bwd_group(g, z0, att, lse, dcat, tables):
    scale = HEAD_DIM ** -0.5
    dilation = DILATIONS[g]
    blocks_per_residue = SEQ // dilation // SPAN
    with_gate = g == 0

    def body(*refs):
        q_ref, k_ref, v_ref, bg_ref, att_ref, lse_ref, dyb_ref, cos_ref, sa_ref, sb_ref = refs[0:10]
        n_out = 4 if with_gate else 3
        dq_ref, dk_ref, dv_ref = refs[10:13]
        qd, kd, vd, dod, ld, dd, tmp, aq, ak, av, ak_prev, av_prev = refs[10 + n_out:22 + n_out]
        rule, prev_part = _window_rule()
        own_mask = _own_mask()
        kd[0:SPAN, :] = jnp.zeros((SPAN, HEAD_DIM), bf16)
        vd[0:SPAN, :] = jnp.zeros((SPAN, HEAD_DIM), bf16)
        ak_prev[SEQ:SEQ + SPAN, :] = jnp.zeros((SPAN, HEAD_DIM), f32)
        av_prev[SEQ:SEQ + SPAN, :] = jnp.zeros((SPAN, HEAD_DIM), f32)

        def gate_rows(start):
            r = pl.ds(start, ROW_CHUNK)
            silu, dsilu = _silu_and_grad(bg_ref[r, :])
            att_v = att_ref[r, :]
            dyb = dyb_ref[r, :]
            if with_gate:
                refs[13][r, :] = (dyb * att_v * dsilu).astype(bf16)
            datt = dyb * silu
            tmp[r, :] = datt
            aq[r, :] = jnp.broadcast_to(jnp.sum(datt * att_v, axis=1, keepdims=True), (ROW_CHUNK, HEAD_DIM))

        def rope_into_tmp(src_ref):
            def rows(start):
                r = pl.ds(start, ROW_CHUNK)
                tmp[r, :] = _rope(src_ref[r, :], cos_ref[r, :], sa_ref[r, :], sb_ref[r, :])

            _chunks(rows)

        _chunks(gate_rows)
        _by_residue(dod, tmp, dilation, bf16)
        _by_residue(dd, aq, dilation, f32)
        _by_residue(ld, lse_ref, dilation, f32)
        rope_into_tmp(q_ref)
        _by_residue(qd, tmp, dilation, bf16)
        rope_into_tmp(k_ref)
        _by_residue(kd, tmp, dilation, bf16, SPAN)
        _by_residue(vd, v_ref, dilation, bf16, SPAN)

        def block(c, carry):
            r0 = pl.multiple_of(c * SPAN, SPAN)
            rows = pl.ds(r0, SPAN)
            q, do, lse_q, delta = qd[rows, :], dod[rows, :], ld[rows, :], dd[rows, :]
            if blocks_per_residue > 1:
                keys = pl.ds(r0, 2 * SPAN)
                ok = rule <= jnp.where(prev_part, jnp.where(c % blocks_per_residue > 0, 0, -SPAN), 0)
                lse_q = jnp.concatenate([lse_q, lse_q], axis=1)
                delta = jnp.concatenate([delta, delta], axis=1)
            else:
                keys = pl.ds(pl.multiple_of(c * SPAN + SPAN, SPAN), SPAN)
                ok = own_mask
            k, v = kd[keys, :], vd[keys, :]
            p = jnp.where(ok, jnp.exp(_dot_nt(q, k) * scale - lse_q), 0.0)
            ds = (p * (_dot_nt(do, v) - delta) * scale).astype(bf16)
            dv = _dot_tn(p.astype(bf16), do)
            dk = _dot_tn(ds, q)
            aq[rows, :] = _dot(ds, k)
            if blocks_per_residue > 1:
                av_prev[rows, :] = dv[0:SPAN]
                ak_prev[rows, :] = dk[0:SPAN]
                av[rows, :] = dv[SPAN:]
                ak[rows, :] = dk[SPAN:]
            else:
                av[rows, :] = dv
                ak[rows, :] = dk
            return carry

        lax.fori_loop(0, SEQ // SPAN, block, 0, unroll=4)

        def finish(out_ref, acc, acc_prev, roped):
            if blocks_per_residue > 1 and acc_prev is not None:
                def add(start):
                    r = pl.ds(start, ROW_CHUNK)
                    acc[r, :] = acc[r, :] + acc_prev[pl.ds(pl.multiple_of(start + SPAN, SPAN), ROW_CHUNK), :]

                _chunks(add)
            _by_position(tmp, acc, dilation)

            def rows(start):
                r = pl.ds(start, ROW_CHUNK)
                d = tmp[r, :]
                if roped:
                    d = _rope_transposed(d, cos_ref[r, :], sa_ref[r, :], sb_ref[r, :])
                out_ref[r, :] = d.astype(bf16)

            _chunks(rows)

        finish(dq_ref, aq, None, True)
        finish(dk_ref, ak, ak_prev, True)
        finish(dv_ref, av, av_prev, False)

    head = pl.BlockSpec((SEQ, HEAD_DIM), lambda h: (0, h))
    in_specs = [_head_spec(Q_COL + 8 * g), _head_spec(K_COL + 8 * g), _head_spec(V_COL + 8 * g), _head_spec(BGATE_COL),
                head, head, _head_spec(8), _table_spec(), _table_spec(), _table_spec()]
    n_out = 4 if with_gate else 3
    vm = lambda dt: pltpu.VMEM((SEQ, HEAD_DIM), dt)
    padded = lambda dt: pltpu.VMEM((SEQ + SPAN, HEAD_DIM), dt)
    return pl.pallas_call(
        body, name=f"attn_bwd_g{g}", grid=(N_HEADS,), in_specs=in_specs, out_specs=[head] * n_out,
        out_shape=[SDS((SEQ, HALF), bf16)] * n_out,
        scratch_shapes=[vm(bf16), padded(bf16), padded(bf16), vm(bf16), vm(f32), vm(f32), vm(f32), vm(f32), vm(f32),
                        vm(f32), padded(f32), padded(f32)],
    )(z0, z0, z0, z0, att, lse, dcat, *tables)


def _sgu_specs():
    chunk = lambda col: pl.BlockSpec((CHUNK, HALF), lambda n: (n, col))
    vec = pl.BlockSpec((1, HALF), lambda n: (0, 0))
    w = pl.BlockSpec((4, CHUNK, CHUNK), lambda n: (0, 0, 0))
    bias = pl.BlockSpec((CHUNK, CHUNK), lambda n: (0, 0))
    return chunk, vec, w, bias


def _sgu_weights(w_ref):
    tril = lax.broadcasted_iota(jnp.int32, (CHUNK, CHUNK), 1) <= lax.broadcasted_iota(jnp.int32, (CHUNK, CHUNK), 0)
    return tril, [jnp.where(tril, w_ref[h], 0.0).astype(bf16) for h in range(4)]


def _sgu_fwd(z1, ln_g, ln_b, sgu_w, bias_t):
    def body(u_ref, v_ref, cg_ref, g_ref, b_ref, w_ref, bias_ref, yc_ref):
        _, ws = _sgu_weights(w_ref)
        xh, _ = _ln_stats(v_ref[...])
        vn = (xh * g_ref[...] + b_ref[...]).astype(bf16)
        for h in range(4):
            cols = slice(h * POOL_CH, (h + 1) * POOL_CH)
            s = _dot(ws[h], vn[:, cols]) + bias_ref[:, h:h + 1]
            gate = cg_ref[:, cols]
            yc_ref[:, cols] = (u_ref[:, cols] * s * gate * _sigmoid(gate)).astype(bf16)

    chunk, vec, w, bias = _sgu_specs()
    return pl.pallas_call(
        body, name="sgu_fwd", grid=(SEQ // CHUNK,),
        in_specs=[chunk(0), chunk(1), chunk(2), vec, vec, w, bias], out_specs=chunk(0),
        out_shape=SDS((SEQ, HALF), bf16))(z1, z1, z1, ln_g, ln_b, sgu_w, bias_t)


def _sgu_bwd(z1, dcat, ln_g, ln_b, sgu_w, bias_t):
    def body(u_ref, v_ref, cg_ref, dyc_ref, g_ref, b_ref, w_ref, bias_ref,
             du_ref, dv_ref, dcg_ref, dw_ref, dbias_ref, dg_ref, db_ref, dvn_ref):
        first = pl.program_id(0) == 0
        tril, ws = _sgu_weights(w_ref)
        xh, rstd = _ln_stats(v_ref[...])
        g = g_ref[...]
        vn = (xh * g + b_ref[...]).astype(bf16)

        @pl.when(first)
        def _():
            dbias_ref[...] = jnp.zeros((CHUNK, CHUNK), f32)

        for h in range(4):
            cols = slice(h * POOL_CH, (h + 1) * POOL_CH)
            vn_h = vn[:, cols]
            s = _dot(ws[h], vn_h) + bias_ref[:, h:h + 1]
            silu, dsilu = _silu_and_grad(cg_ref[:, cols])
            dyc = dyc_ref[:, cols]
            u = u_ref[:, cols]
            du_ref[:, cols] = (dyc * s * silu).astype(bf16)
            dcg_ref[:, cols] = (dyc * u * s * dsilu).astype(bf16)
            ds = dyc * u * silu
            dbias_ref[:, h:h + 1] += jnp.sum(ds, axis=1, keepdims=True)
            ds = ds.astype(bf16)
            _accumulate(dw_ref.at[h], jnp.where(tril, _dot_nt(ds, vn_h), 0.0), first)
            dvn_ref[:, cols] = _dot_tn(ws[h], ds)
        dv, dg, db = _ln_bwd(xh, rstd, g, dvn_ref[...])
        dv_ref[...] = dv.astype(bf16)
        _accumulate(dg_ref, dg, first)
        _accumulate(db_ref, db, first)

    chunk, vec, w, bias = _sgu_specs()
    return pl.pallas_call(
        body, name="sgu_bwd", grid=(SEQ // CHUNK,),
        in_specs=[chunk(0), chunk(1), chunk(2), chunk(0), vec, vec, w, bias],
        out_specs=[chunk(0), chunk(0), chunk(0), w, bias, vec, vec],
        out_shape=[SDS((SEQ, HALF), bf16)] * 3 + [SDS((4, CHUNK, CHUNK), f32), SDS((CHUNK, CHUNK), f32),
                                                   SDS((1, HALF), f32), SDS((1, HALF), f32)],
        scratch_shapes=[pltpu.VMEM((CHUNK, HALF), f32)],
    )(z1, z1, z1, dcat, ln_g, ln_b, sgu_w, bias_t)


CONV_TILE = 128
DVAL_COL, DGLU_COL = 12, 16


def _conv_specs():
    val = pl.BlockSpec((SEQ, POOL_CH), lambda j: (0, DVAL_COL + j))
    glu = pl.BlockSpec((SEQ, POOL_CH), lambda j: (0, DGLU_COL + j))
    w = pl.BlockSpec((CONV_K, POOL_CH), lambda j: (0, j))
    col = pl.BlockSpec((SEQ, POOL_CH), lambda j: (0, j))
    vec = pl.BlockSpec((1, POOL_CH), lambda j: (0, j))
    return val, glu, w, col, vec


def _conv_fwd(z1, conv_w, conv_b):
    def body(val_ref, glu_ref, w_ref, b_ref, out_ref, xpad):
        xpad[0:CONV_PAD, :] = jnp.zeros((CONV_PAD, POOL_CH), f32)
        xpad[CONV_PAD:, :] = val_ref[...] * _sigmoid(glu_ref[...])
        w = w_ref[...]
        bias = b_ref[...]

        def tile(i, carry):
            t0 = pl.multiple_of(i * CONV_TILE, CONV_TILE)
            window = xpad[pl.ds(t0, CONV_TILE + CONV_PAD), :]
            acc = jnp.broadcast_to(bias, (CONV_TILE, POOL_CH))
            for k in range(CONV_K):
                shift = CONV_PAD - (CONV_K - 1) + k
                acc = acc + w[k:k + 1, :] * pltpu.roll(window, CONV_TILE + CONV_PAD - shift, 0)[0:CONV_TILE]
            out_ref[pl.ds(t0, CONV_TILE), :] = acc
            return carry

        lax.fori_loop(0, SEQ // CONV_TILE, tile, 0)

    val, glu, w, col, vec = _conv_specs()
    return pl.pallas_call(
        body, name="conv_fwd", grid=(4,), in_specs=[val, glu, w, vec], out_specs=col,
        out_shape=SDS((SEQ, HALF), f32), scratch_shapes=[pltpu.VMEM((SEQ + CONV_PAD, POOL_CH), f32)],
    )(z1, z1, conv_w, conv_b)


def _conv_bwd(z1, dconv, conv_w):
    def body(val_ref, glu_ref, w_ref, dout_ref, dval_ref, dglu_ref, dw_ref, db_ref, xpad, dpad, dx_ref):
        val = val_ref[...]
        sig = _sigmoid(glu_ref[...])
        xpad[0:CONV_PAD, :] = jnp.zeros((CONV_PAD, POOL_CH), f32)
        xpad[CONV_PAD:, :] = val * sig
        dout = dout_ref[...]
        dpad[0:SEQ, :] = dout
        dpad[SEQ:, :] = jnp.zeros((CONV_PAD, POOL_CH), f32)
        db_ref[...] = jnp.sum(dout, axis=0, keepdims=True)
        dw_ref[...] = jnp.zeros((CONV_K, POOL_CH), f32)
        w = w_ref[...]

        def tile(i, carry):
            t0 = pl.multiple_of(i * CONV_TILE, CONV_TILE)
            x_win = xpad[pl.ds(t0, CONV_TILE + CONV_PAD), :]
            d_win = dpad[pl.ds(t0, CONV_TILE + CONV_PAD), :]
            d_own = d_win[0:CONV_TILE]
            acc = jnp.zeros((CONV_TILE, POOL_CH), f32)
            for k in range(CONV_K):
                shift = CONV_PAD - (CONV_K - 1) + k
                x_k = pltpu.roll(x_win, CONV_TILE + CONV_PAD - shift, 0)[0:CONV_TILE]
                dw_ref[k:k + 1, :] += jnp.sum(d_own * x_k, axis=0, keepdims=True)
                back = CONV_K - 1 - k
                d_k = d_own if back == 0 else pltpu.roll(d_win, CONV_TILE + CONV_PAD - back, 0)[0:CONV_TILE]
                acc = acc + w[k:k + 1, :] * d_k
            dx_ref[pl.ds(t0, CONV_TILE), :] = acc
            return carry

        lax.fori_loop(0, SEQ // CONV_TILE, tile, 0)
        dx = dx_ref[...]
        dval_ref[...] = (dx * sig).astype(bf16)
        dglu_ref[...] = (dx * val * sig * (1.0 - sig)).astype(bf16)

    val, glu, w, col, vec = _conv_specs()
    pad = pltpu.VMEM((SEQ + CONV_PAD, POOL_CH), f32)
    return pl.pallas_call(
        body, name="conv_bwd", grid=(4,), in_specs=[val, glu, w, col], out_specs=[col, col, w, vec],
        out_shape=[SDS((SEQ, HALF), bf16), SDS((SEQ, HALF), bf16), SDS((CONV_K, HALF), f32), SDS((1, HALF), f32)],
        scratch_shapes=[pad, pad, pltpu.VMEM((SEQ, POOL_CH), f32)],
    )(z1, z1, conv_w, dconv)


DGATE_COL = 5


def _conv_norm_fwd(conv, z1, g, b):
    def body(c_ref, gate_ref, g_ref, b_ref, yd_ref):
        xh, _ = _ln_stats(c_ref[...])
        n = xh * g_ref[...] + b_ref[...]
        gate = gate_ref[...]
        yd_ref[...] = (n * _sigmoid(n) * gate * _sigmoid(gate)).astype(bf16)

    return pl.pallas_call(
        body, name="conv_norm_fwd", grid=(SEQ // ROWS,),
        in_specs=[_row_spec(HALF), _row_spec(HALF, DGATE_COL), _vec_spec(HALF), _vec_spec(HALF)],
        out_specs=_row_spec(HALF), out_shape=SDS((SEQ, HALF), bf16))(conv, z1, g, b)


def _conv_norm_bwd(conv, z1, dcat, g, b):
    def body(c_ref, gate_ref, dyd_ref, g_ref, b_ref, dconv_ref, dgate_ref, dg_ref, db_ref):
        first = pl.program_id(0) == 0
        xh, rstd = _ln_stats(c_ref[...])
        g = g_ref[...]
        n_silu, n_dsilu = _silu_and_grad(xh * g + b_ref[...])
        gate_silu, gate_dsilu = _silu_and_grad(gate_ref[...])
        dyd = dyd_ref[...]
        dgate_ref[...] = (dyd * n_silu * gate_dsilu).astype(bf16)
        dconv, dg, db = _ln_bwd(xh, rstd, g, dyd * gate_silu * n_dsilu)
        dconv_ref[...] = dconv
        _accumulate(dg_ref, dg, first)
        _accumulate(db_ref, db, first)

    return pl.pallas_call(
        body, name="conv_norm_bwd", grid=(SEQ // ROWS,),
        in_specs=[_row_spec(HALF), _row_spec(HALF, DGATE_COL), _row_spec(HALF, 1), _vec_spec(HALF), _vec_spec(HALF)],
        out_specs=[_row_spec(HALF), _row_spec(HALF), _vec_spec(HALF), _vec_spec(HALF)],
        out_shape=[SDS((SEQ, HALF), f32), SDS((SEQ, HALF), bf16), SDS((1, HALF), f32), SDS((1, HALF), f32)],
    )(conv, z1, dcat, g, b)


def _step(x, target, w, chip):
    chip_vec = chip.astype(jnp.int32).reshape(1)
    sharded_names = list(SHARDED_SMALL)
    small_shard = _pack([w[k] for k in sharded_names], total_rows=SMALL_SHARD_ROWS)
    small_slot = lax.dynamic_update_slice(jnp.zeros((N_CHIPS, SMALL_SHARD_ROWS, LANES), f32), small_shard[None], (chip, 0, 0))
    slots = [small_slot] + [_cast_into_slot(w[k], chip_vec, f"cast_{k}") for k in BIG]
    sems, bufs, token = _gather_start(slots)
    tables = _rope_tables()

    def vec(k):
        return w[k].reshape(1, -1)

    h0 = _pre_norm(x, vec("e_pre_norm") + token[0, 0])
    small_full, e_w_in = _forward_halves(_gather_wait(bufs[0:2], sems[0:2], h0, "gather_wait_first"), "forward_first")
    p = {k: _from_chips(k, a) for k, a in zip(sharded_names, _unpack(small_full, [SHARDED_SMALL[k][0] for k in sharded_names]))}
    for k in ("o_pre_norm", "o_sgu_norm_g", "o_sgu_norm_b", "o_conv_b", "o_conv_norm_g", "o_conv_norm_b", "o_post_norm"):
        p[k] = p[k].reshape(1, -1)
    pool_w_bf = p["e_pool_w"].astype(bf16)
    bias_t = jnp.pad(w["o_sgu_b"].T, ((0, 0), (0, CHUNK - 4)))

    z0 = _mm_nn(h0, e_w_in, f32, "e_in")
    ya = _pool_fwd(z0, pool_w_bf, vec("e_pool_scale"))
    yb, att, lse = _attn_fwd(z0, tables)
    e_w_out, o_w_in, o_w_out = _forward_halves(_gather_wait(bufs[2:5], sems[2:5], att, "gather_wait_rest"), "forward_rest")
    e_w_out = e_w_out.reshape(1, D_MODEL, D_MODEL)
    o_w_out = o_w_out.reshape(1, D_MODEL, D_MODEL)
    cat0 = jnp.concatenate([ya, yb], axis=1)
    y0 = _mm_nn(cat0, e_w_out, f32, "e_out")
    x1, h1 = _mid_norm(x, y0, vec("e_post_norm"), p["o_pre_norm"])
    z1 = _mm_nn(h1, o_w_in, f32, "o_in")
    yc = _sgu_fwd(z1, p["o_sgu_norm_g"], p["o_sgu_norm_b"], w["o_sgu_w"], bias_t)
    conv = _conv_fwd(z1, p["o_conv_w"], p["o_conv_b"])
    yd = _conv_norm_fwd(conv, z1, p["o_conv_norm_g"], p["o_conv_norm_b"])
    cat1 = jnp.concatenate([yc, yd], axis=1)
    y1 = _mm_nn(cat1, o_w_out, f32, "o_out")
    loss, dx2, dy1, g_o_post = _final_norm_loss(x1, y1, p["o_post_norm"], target)

    in_flight = {}

    def send_off(name, grad):
        sem, sums, land, tok = _scatter_start(_swap_add(grad, f"swap_add_{name}"), f"scatter_start_{name}")
        in_flight[name] = (sem, sums, land)
        return tok

    tok = send_off("o_w_out", _mm_tn(cat1, dy1, 1, "o_out_dw").reshape(N_CHIPS, HALF // 2, D_MODEL))
    dcat1 = _mm_nt(dy1, o_w_out, f32, "o_out_dx", tok)
    du, dv, dcg, g_sgu_w, g_bias_t, g_sgu_g, g_sgu_b = _sgu_bwd(
        z1, dcat1, p["o_sgu_norm_g"] + tok[0, 0], p["o_sgu_norm_b"], w["o_sgu_w"], bias_t)
    dconv, ddgate, g_cn_g, g_cn_b = _conv_norm_bwd(conv, z1, dcat1, p["o_conv_norm_g"], p["o_conv_norm_b"])
    ddval, ddglu, g_conv_w, g_conv_b = _conv_bwd(z1, dconv, p["o_conv_w"])
    dz1 = jnp.concatenate([du, dv, dcg, ddval, ddglu, ddgate], axis=1)
    tok = send_off("o_w_in", _mm_tn(h1, dz1, N_CHIPS, "o_in_dw"))
    dh1 = _mm_nt(dz1, o_w_in, f32, "o_in_dx", tok)
    dx1, dy0, g_o_pre, g_e_post = _mid_norm_bwd(dx2, dh1, x1, y0, p["o_pre_norm"] + tok[0, 0], vec("e_post_norm"))

    tok = send_off("e_w_out", _mm_tn(cat0, dy0, 1, "e_out_dw").reshape(N_CHIPS, HALF // 2, D_MODEL))
    dcat0 = _mm_nt(dy0, e_w_out, f32, "e_out_dx", tok)
    da, dagate, g_pool_w, g_pool_scale = _pool_bwd(z0, dcat0, pool_w_bf, vec("e_pool_scale") + tok[0, 0])
    dq0, dk0, dv0, dbgate = _attn_bwd_group(0, z0, att, lse, dcat0, tables)
    dq1, dk1, dv1 = _attn_bwd_group(1, z0, att, lse, dcat0, tables)
    dq2, dk2, dv2 = _attn_bwd_group(2, z0, att, lse, dcat0, tables)
    dz0 = jnp.concatenate([da, dagate, dq0, dq1, dq2, dk0, dk1, dk2, dv0, dv1, dv2, dbgate], axis=1)
    tok = send_off("e_w_in", _mm_tn(h0, dz0, N_CHIPS, "e_in_dw"))
    dh0 = _mm_nt(dz0, e_w_in, f32, "e_in_dx", tok)
    grad_x, g_e_pre = _pre_norm_bwd(dx1, dh0, x, vec("e_pre_norm") + tok[0, 0])

    small = {"e_pre_norm": g_e_pre, "e_pool_w": g_pool_w, "e_pool_scale": g_pool_scale, "e_post_norm": g_e_post,
             "o_pre_norm": g_o_pre, "o_sgu_norm_g": g_sgu_g, "o_sgu_norm_b": g_sgu_b, "o_sgu_w": g_sgu_w,
             "o_sgu_b": g_bias_t[:, 0:4].T, "o_conv_w": g_conv_w, "o_conv_b": g_conv_b,
             "o_conv_norm_g": g_cn_g, "o_conv_norm_b": g_cn_b, "o_post_norm": g_o_post}
    return loss, grad_x, in_flight, small


def _land(in_flight, name, chip, after):
    sems, sums, land = in_flight[name]
    sums, land = _scatter_wait(sems, sums, land, after, f"scatter_wait_{name}")
    return _add_landed_join(sums, land, chip.astype(jnp.int32).reshape(1), f"add_landed_{name}")


def _place():
    x, y, c = lax.axis_index("x"), lax.axis_index("y"), lax.axis_index("c")
    others = [(1 - x, y), (x, 1 - y), (1 - x, 1 - y)]
    return x, y, c, 2 * x + y, others


def _all_gather(shards, name):
    n = len(shards)

    def body(*refs):
        ins, outs = refs[:n], refs[n:2 * n]
        send_sems, recv_sems, local_sems = refs[2 * n:]
        x, y, c, me, others = _place()
        sibling = (x, y, 1 - c)

        def half(a, chip, core):
            rows = ins[a].shape[0] // 2
            return outs[a].at[chip, pl.ds(core * rows, rows), :]

        def copy(a, k, src, dst, to):
            return pltpu.make_async_remote_copy(src_ref=src, dst_ref=dst, send_sem=send_sems.at[6 * a + k],
                                                recv_sem=recv_sems.at[6 * a + k], device_id=to, device_id_type=MESH)

        local = [pltpu.make_async_copy(ins[a], outs[a].at[me], local_sems.at[a]) for a in range(n)]
        for cp in local:
            cp.start()
        sent = []
        for a in range(n):
            rows = ins[a].shape[0] // 2
            mine = ins[a].at[pl.ds(c * rows, rows), :]
            for k, (ox, oy) in enumerate(others):
                sent.append(copy(a, k, mine, half(a, me, c), (ox, oy, c)))
                sent[-1].start()
        for a in range(n):
            for k, (ox, oy) in enumerate(others):
                landed = half(a, 2 * ox + oy, c)
                copy(a, k, landed, landed, (ox, oy, c)).wait_recv()
                sent.append(copy(a, 3 + k, landed, landed, sibling))
                sent[-1].start()
        for k, (ox, oy) in enumerate(others):
            chip = 2 * ox + oy
            for a in range(n):
                theirs = half(a, chip, 1 - c)
                copy(a, 3 + k, theirs, theirs, sibling).wait_recv()
        for cp in sent:
            cp.wait_send()
        for cp in local:
            cp.wait()

    return pl.pallas_call(
        body, name=name, in_specs=[ANY] * n, out_specs=[ANY] * n,
        out_shape=[SDS((N_CHIPS,) + s.shape, s.dtype) for s in shards],
        scratch_shapes=[pltpu.SemaphoreType.DMA((6 * n,)), pltpu.SemaphoreType.DMA((6 * n,)), pltpu.SemaphoreType.DMA((n,))],
    )(*shards)


def _swap_halves(parts, name):
    n = len(parts)

    def body(*refs):
        ins, own, theirs = refs[:n], refs[n:2 * n], refs[2 * n:3 * n]
        send_sems, recv_sems, local_sems = refs[3 * n:]
        x, y, c, _, _ = _place()
        sibling = (x, y, 1 - c)
        copies = []
        for a in range(n):
            rows = ins[a].shape[1] // 2
            keep = pltpu.make_async_copy(ins[a].at[:, pl.ds(c * rows, rows), :], own[a], local_sems.at[a])
            give = pltpu.make_async_remote_copy(
                src_ref=ins[a].at[:, pl.ds((1 - c) * rows, rows), :], dst_ref=theirs[a], send_sem=send_sems.at[a],
                recv_sem=recv_sems.at[a], device_id=sibling, device_id_type=MESH)
            keep.start()
            give.start()
            copies += [keep, give]
        for cp in copies:
            cp.wait()

    half = [SDS((N_CHIPS, s.shape[1] // 2, s.shape[2]), s.dtype) for s in parts]
    out = pl.pallas_call(
        body, name=name, in_specs=[ANY] * n, out_specs=[ANY] * (2 * n), out_shape=half + half,
        scratch_shapes=[pltpu.SemaphoreType.DMA((n,)), pltpu.SemaphoreType.DMA((n,)), pltpu.SemaphoreType.DMA((n,))],
    )(*parts)
    return out[:n], out[n:]


def _scatter_chips(parts, name):
    n = len(parts)

    def body(*refs):
        ins, outs = refs[:n], refs[n:2 * n]
        send_sems, recv_sems, local_sems = refs[2 * n:]
        x, y, c, me, others = _place()

        def copy(a, k, slot_from, slot_to, chip_xy):
            return pltpu.make_async_remote_copy(
                src_ref=ins[a].at[slot_from], dst_ref=outs[a].at[slot_to], send_sem=send_sems.at[3 * a + k],
                recv_sem=recv_sems.at[3 * a + k], device_id=(chip_xy[0], chip_xy[1], c), device_id_type=MESH)

        keeps, gives = [], []
        for a in range(n):
            keeps.append(pltpu.make_async_copy(ins[a].at[me], outs[a].at[me], local_sems.at[a]))
            keeps[-1].start()
            for k, (ox, oy) in enumerate(others):
                gives.append(copy(a, k, 2 * ox + oy, me, (ox, oy)))
                gives[-1].start()
        for a in range(n):
            for k, (ox, oy) in enumerate(others):
                copy(a, k, me, 2 * ox + oy, (ox, oy)).wait_recv()
        for cp in gives:
            cp.wait_send()
        for cp in keeps:
            cp.wait()

    return pl.pallas_call(
        body, name=name, in_specs=[ANY] * n, out_specs=[ANY] * n, out_shape=[SDS(s.shape, s.dtype) for s in parts],
        scratch_shapes=[pltpu.SemaphoreType.DMA((3 * n,)), pltpu.SemaphoreType.DMA((3 * n,)), pltpu.SemaphoreType.DMA((n,))],
    )(*parts)


def _join_halves(halves, name):
    n = len(halves)

    def body(*refs):
        ins, outs = refs[:n], refs[n:2 * n]
        send_sems, recv_sems, local_sems = refs[2 * n:]
        x, y, c, _, _ = _place()

        def copy(a, core):
            rows = ins[a].shape[0]
            return pltpu.make_async_remote_copy(
                src_ref=ins[a], dst_ref=outs[a].at[pl.ds(core * rows, rows), :], send_sem=send_sems.at[a],
                recv_sem=recv_sems.at[a], device_id=(x, y, 1 - c), device_id_type=MESH)

        keeps, gives = [], []
        for a in range(n):
            rows = ins[a].shape[0]
            keeps.append(pltpu.make_async_copy(ins[a], outs[a].at[pl.ds(c * rows, rows), :], local_sems.at[a]))
            gives.append(copy(a, c))
            keeps[-1].start()
            gives[-1].start()
        for a in range(n):
            copy(a, 1 - c).wait_recv()
        for cp in gives:
            cp.wait_send()
        for cp in keeps:
            cp.wait()

    return pl.pallas_call(
        body, name=name, in_specs=[ANY] * n, out_specs=[ANY] * n,
        out_shape=[SDS((2 * s.shape[0], s.shape[1]), s.dtype) for s in halves],
        scratch_shapes=[pltpu.SemaphoreType.DMA((n,)), pltpu.SemaphoreType.DMA((n,)), pltpu.SemaphoreType.DMA((n,))],
    )(*halves)


def _add_pair(a, b, name):
    _, r, c = a.shape
    tr = 256 if r % 256 == 0 else r // 2 if r > 512 else r

    def body(a_ref, b_ref, o_ref):
        o_ref[...] = (a_ref[...].astype(f32) + b_ref[...].astype(f32)).astype(o_ref.dtype)

    spec = pl.BlockSpec((None, tr, c), lambda j, i: (j, i, 0))
    return pl.pallas_call(body, name=name, grid=(N_CHIPS, r // tr), in_specs=[spec, spec], out_specs=spec,
                          out_shape=SDS(a.shape, a.dtype))(a, b)


def _add_chips(u, name):
    _, r, c = u.shape
    tr = 256 if r % 256 == 0 else r

    def body(u_ref, o_ref):
        o_ref[...] = ((u_ref[0].astype(f32) + u_ref[1].astype(f32)) + u_ref[2].astype(f32)) + u_ref[3].astype(f32)

    return pl.pallas_call(
        body, name=name, grid=(r // tr,), in_specs=[pl.BlockSpec((N_CHIPS, tr, c), lambda i: (0, i, 0))],
        out_specs=pl.BlockSpec((tr, c), lambda i: (i, 0)), out_shape=SDS((r, c), f32))(u)


SWAP_ROWS = 256


def _swap_add(g, name):
    chips, r, c = g.shape
    half = r // 2
    nb = half // SWAP_ROWS
    steps = chips * nb

    def body(core_ref, mine_ref, theirs_ref, out_ref, landing, send_sems, recv_sems, free_sems):
        i = pl.program_id(0)
        slot = i % 2
        x, y, core, _, _ = _place()
        sibling = (x, y, 1 - core)

        @pl.when(i >= 2)
        def _():
            pl.semaphore_wait(free_sems.at[slot], 1)

        send = pltpu.make_async_remote_copy(src_ref=theirs_ref, dst_ref=landing.at[slot], send_sem=send_sems.at[slot],
                                            recv_sem=recv_sems.at[slot], device_id=sibling, device_id_type=MESH)
        send.start()
        send.wait_recv()
        out_ref[...] = (mine_ref[...].astype(f32) + landing[slot].astype(f32)).astype(out_ref.dtype)

        @pl.when(i + 2 < steps)
        def _():
            pl.semaphore_signal(free_sems.at[slot], 1, device_id=sibling, device_id_type=MESH)

        send.wait_send()

    block = (SWAP_ROWS, c)
    grid_spec = pltpu.PrefetchScalarGridSpec(
        num_scalar_prefetch=1, grid=(steps,),
        in_specs=[pl.BlockSpec(block, lambda i, core: ((2 * (i // nb) + core[0]) * nb + i % nb, 0)),
                  pl.BlockSpec(block, lambda i, core: ((2 * (i // nb) + 1 - core[0]) * nb + i % nb, 0))],
        out_specs=pl.BlockSpec(block, lambda i, core: (i, 0)),
        scratch_shapes=[pltpu.VMEM((2, SWAP_ROWS, c), g.dtype), pltpu.SemaphoreType.DMA((2,)),
                        pltpu.SemaphoreType.DMA((2,)), pltpu.SemaphoreType.REGULAR((2,))])
    core = lax.axis_index("c").astype(jnp.int32).reshape(1)
    rows = g.reshape(chips * r, c)
    out = pl.pallas_call(body, name=name, grid_spec=grid_spec, out_shape=SDS((chips * half, c), g.dtype))(core, rows, rows)
    return out.reshape(chips, half, c)


def _reduce_scatter(parts, tag):
    own, theirs = _swap_halves(parts, f"swap_halves_{tag}")
    chip_sums = [_add_pair(o, t, f"add_cores_{tag}{i}") for i, (o, t) in enumerate(zip(own, theirs))]
    gathered = _scatter_chips(chip_sums, f"scatter_chips_{tag}")
    halves = [_add_chips(u, f"add_chips_{tag}{i}") for i, u in enumerate(gathered)]
    return _join_halves(halves, f"join_halves_{tag}")


HBM = pl.BlockSpec(memory_space=pltpu.HBM)
SEM = pl.BlockSpec(memory_space=pltpu.SEMAPHORE)
EFFECT = pltpu.SideEffectType.DATAFLOW_SIDE_EFFECTING


def _in_hbm(a):
    return pltpu.with_memory_space_constraint(a, pltpu.HBM)


def _cast_into_slot(w, chip, name):
    r, c = w.shape
    nb = r // SWAP_ROWS

    def body(chip_ref, w_ref, o_ref):
        o_ref[...] = w_ref[...].astype(bf16)

    grid_spec = pltpu.PrefetchScalarGridSpec(
        num_scalar_prefetch=1, grid=(nb,),
        in_specs=[pl.BlockSpec((SWAP_ROWS, c), lambda i, chip: (i, 0))],
        out_specs=pl.BlockSpec((SWAP_ROWS, c), lambda i, chip: (chip[0] * nb + i, 0)))
    out = pl.pallas_call(body, name=name, grid_spec=grid_spec, out_shape=SDS((N_CHIPS * r, c), bf16))(chip, w)
    return out.reshape(N_CHIPS, r, c)


def _gather_start(bufs):
    n = len(bufs)

    def body(*refs):
        ins, sems, token = refs[:n], refs[n:3 * n], refs[4 * n]
        x, y, c, me, others = _place()
        for a in range(n):
            rows = ins[a].shape[1] // 2
            mine = ins[a].at[me, pl.ds(c * rows, rows), :]
            for k, (ox, oy) in enumerate(others):
                pltpu.make_async_remote_copy(src_ref=mine, dst_ref=mine, send_sem=sems[2 * a].at[k],
                                             recv_sem=sems[2 * a + 1].at[k], device_id=(ox, oy, c),
                                             device_id_type=MESH).start()
        token[...] = jnp.zeros_like(token)

    out = pl.pallas_call(
        body, name="gather_start", in_specs=[HBM] * n,
        out_shape=(*[pltpu.SemaphoreType.DMA((3,))] * (2 * n), *[pltpu.HBM(b.shape, b.dtype) for b in bufs],
                   SDS((8, 128), f32)),
        out_specs=(*[SEM] * (2 * n), *[HBM] * n, pl.BlockSpec(memory_space=pltpu.VMEM)),
        input_output_aliases={a: 2 * n + a for a in range(n)},
        compiler_params=pltpu.CompilerParams(has_side_effects=EFFECT),
    )(*[_in_hbm(b) for b in bufs])
    return [(out[2 * a], out[2 * a + 1]) for a in range(n)], list(out[2 * n:3 * n]), out[3 * n]


def _gather_wait(bufs, sems, after, name):
    n = len(bufs)

    def body(*refs):
        ins, sem_refs = refs[:n], refs[n:3 * n]
        x, y, c, me, others = _place()
        for a in range(n):
            rows = ins[a].shape[1] // 2
            mine = ins[a].at[me, pl.ds(c * rows, rows), :]
            for k, (ox, oy) in enumerate(others):
                landed = ins[a].at[2 * ox + oy, pl.ds(c * rows, rows), :]
                copy = pltpu.make_async_remote_copy(src_ref=mine, dst_ref=landed, send_sem=sem_refs[2 * a].at[k],
                                                    recv_sem=sem_refs[2 * a + 1].at[k], device_id=(ox, oy, c),
                                                    device_id_type=MESH)
                copy.wait_send()
                copy.wait_recv()

    flat_sems = [s for pair in sems for s in pair]
    out = pl.pallas_call(
        body, name=name, in_specs=[HBM] * n + [SEM] * (2 * n) + [ANY],
        out_shape=tuple(pltpu.HBM(b.shape, b.dtype) for b in bufs), out_specs=tuple([HBM] * n),
        input_output_aliases={a: a for a in range(n)},
        compiler_params=pltpu.CompilerParams(has_side_effects=EFFECT),
    )(*bufs, *flat_sems, after)
    return list(out)


def _forward_halves(bufs, name):
    n = len(bufs)
    blocks = []
    for b in bufs:
        half = b.shape[1] // 2
        tr = SWAP_ROWS if half % SWAP_ROWS == 0 else half
        blocks.append((half, tr))
    work = [(a, k, b) for a in range(n) for k in range(3) for b in range(blocks[a][0] // blocks[a][1])]

    def body(*refs):
        outs, stages = refs[n:2 * n], refs[2 * n:3 * n]
        load_sems, send_sems, recv_sems = refs[3 * n:]
        x, y, c, me, others = _place()
        sibling = (x, y, 1 - c)

        def rows(item):
            a, k, b = item
            half, tr = blocks[a]
            ox, oy = others[k]
            return outs[a].at[2 * ox + oy, pl.ds(c * half + b * tr, tr), :]

        def load(s, item):
            return pltpu.make_async_copy(rows(item), stages[item[0]].at[s], load_sems.at[s])

        def send(s, item):
            return pltpu.make_async_remote_copy(src_ref=stages[item[0]].at[s], dst_ref=rows(item), send_sem=send_sems.at[s],
                                                recv_sem=recv_sems.at[item[0]], device_id=sibling, device_id_type=MESH)

        load(0, work[0]).start()
        for t, item in enumerate(work):
            s = t % 2
            load(s, item).wait()
            send(s, item).start()
            if t + 1 < len(work):
                if t >= 1:
                    send(1 - s, work[t - 1]).wait_send()
                load(1 - s, work[t + 1]).start()
        if len(work) > 1:
            send(len(work) % 2, work[-2]).wait_send()
        send((len(work) - 1) % 2, work[-1]).wait_send()
        for a in range(n):
            theirs = outs[a].at[pl.ds(0, 3), pl.ds(0, blocks[a][0]), :]
            pltpu.make_async_remote_copy(src_ref=theirs, dst_ref=theirs, send_sem=send_sems.at[0], recv_sem=recv_sems.at[a],
                                         device_id=sibling, device_id_type=MESH).wait_recv()

    out = pl.pallas_call(
        body, name=name, in_specs=[ANY] * n, out_specs=[ANY] * n, out_shape=[SDS(b.shape, b.dtype) for b in bufs],
        input_output_aliases={a: a for a in range(n)},
        scratch_shapes=[pltpu.VMEM((2, blocks[a][1], bufs[a].shape[2]), bufs[a].dtype) for a in range(n)]
        + [pltpu.SemaphoreType.DMA((2,)), pltpu.SemaphoreType.DMA((2,)), pltpu.SemaphoreType.DMA((n,))],
    )(*bufs)
    return list(out)


def _scatter_start(chip_sums, name):
    def body(a_ref, land_ref, send_sems, recv_sems, a_thru, land_thru, token):
        x, y, c, me, others = _place()
        for k, (ox, oy) in enumerate(others):
            pltpu.make_async_remote_copy(src_ref=a_ref.at[2 * ox + oy], dst_ref=land_ref.at[me], send_sem=send_sems.at[k],
                                         recv_sem=recv_sems.at[k], device_id=(ox, oy, c), device_id_type=MESH).start()
        token[...] = jnp.zeros_like(token)

    shape = pltpu.HBM(chip_sums.shape, chip_sums.dtype)
    send, recv, a_thru, land, token = pl.pallas_call(
        body, name=name, in_specs=[HBM, HBM],
        out_shape=(pltpu.SemaphoreType.DMA((3,)), pltpu.SemaphoreType.DMA((3,)), shape, shape, SDS((8, 128), f32)),
        out_specs=(SEM, SEM, HBM, HBM, pl.BlockSpec(memory_space=pltpu.VMEM)), input_output_aliases={0: 2, 1: 3},
        compiler_params=pltpu.CompilerParams(has_side_effects=EFFECT),
    )(_in_hbm(chip_sums), _in_hbm(lax.empty(chip_sums.shape, chip_sums.dtype)))
    return (send, recv), a_thru, land, token


def _scatter_wait(sems, chip_sums, land, after, name):
    def body(a_ref, land_ref, send_sems, recv_sems, after_ref, a_out, land_out):
        x, y, c, me, others = _place()
        for k, (ox, oy) in enumerate(others):
            copy = pltpu.make_async_remote_copy(
                src_ref=a_ref.at[2 * ox + oy], dst_ref=land_ref.at[2 * ox + oy], send_sem=send_sems.at[k],
                recv_sem=recv_sems.at[k], device_id=(ox, oy, c), device_id_type=MESH)
            copy.wait_send()
            copy.wait_recv()

    shape = pltpu.HBM(chip_sums.shape, chip_sums.dtype)
    return pl.pallas_call(
        body, name=name, in_specs=[HBM, HBM, SEM, SEM, ANY], out_shape=(shape, shape), out_specs=(HBM, HBM),
        input_output_aliases={0: 0, 1: 1}, compiler_params=pltpu.CompilerParams(has_side_effects=EFFECT),
    )(chip_sums, land, sems[0], sems[1], after)


def _add_landed_join(chip_sums, land, chip, name):
    chips, rh, c = chip_sums.shape
    nb = rh // SWAP_ROWS

    def body(chip_ref, own_ref, l1_ref, l2_ref, l3_ref, out_hbm, buf, send_sems, recv_sem, local_sems):
        i = pl.program_id(0)
        slot = i % 2
        x, y, core, _, _ = _place()
        sibling = (x, y, 1 - core)

        def copies(s, step):
            rows = pl.ds(pl.multiple_of((core * nb + step) * SWAP_ROWS, SWAP_ROWS), SWAP_ROWS)
            keep = pltpu.make_async_copy(buf.at[s], out_hbm.at[rows, :], local_sems.at[s])
            give = pltpu.make_async_remote_copy(src_ref=buf.at[s], dst_ref=out_hbm.at[rows, :], send_sem=send_sems.at[s],
                                                recv_sem=recv_sem.at[0], device_id=sibling, device_id_type=MESH)
            return keep, give

        def drain(s, step):
            keep, give = copies(s, step)
            keep.wait()
            give.wait_send()

        @pl.when(i >= 2)
        def _():
            drain(slot, i - 2)

        buf[slot] = ((own_ref[...].astype(f32) + l1_ref[...].astype(f32)) + l2_ref[...].astype(f32)) + l3_ref[...].astype(f32)
        keep, give = copies(slot, i)
        keep.start()
        give.start()

        @pl.when(i == nb - 1)
        def _():
            drain(slot, i)
            if nb > 1:
                drain(1 - slot, i - 1)
            theirs = out_hbm.at[pl.ds((1 - core) * rh, rh), :]
            pltpu.make_async_remote_copy(src_ref=theirs, dst_ref=theirs, send_sem=send_sems.at[0], recv_sem=recv_sem.at[0],
                                         device_id=sibling, device_id_type=MESH).wait_recv()

    block = (SWAP_ROWS, c)
    from_slot = lambda d: pl.BlockSpec(block, lambda i, chip: (((chip[0] + d) % chips) * nb + i, 0))
    grid_spec = pltpu.PrefetchScalarGridSpec(
        num_scalar_prefetch=1, grid=(nb,), in_specs=[from_slot(0), from_slot(1), from_slot(2), from_slot(3)],
        out_specs=ANY,
        scratch_shapes=[pltpu.VMEM((2, SWAP_ROWS, c), f32), pltpu.SemaphoreType.DMA((2,)),
                        pltpu.SemaphoreType.DMA((1,)), pltpu.SemaphoreType.DMA((2,))])
    land_rows = land.reshape(chips * rh, c)
    return pl.pallas_call(body, name=name, grid_spec=grid_spec, out_shape=SDS((2 * rh, c), f32))(
        chip, chip_sums.reshape(chips * rh, c), land_rows, land_rows, land_rows)


def _adamw(w, g, m, v, name):
    r, c = w.shape
    tr = 128 if r % 128 == 0 else r
    b1c = 1.0 - ADAM_B1 ** ADAM_STEP
    b2c = 1.0 - ADAM_B2 ** ADAM_STEP

    def body(w_ref, g_ref, m_ref, v_ref, d_ref, nm_ref, nv_ref):
        g = g_ref[...]
        nm = ADAM_B1 * m_ref[...] + (1.0 - ADAM_B1) * g
        nv = ADAM_B2 * v_ref[...] + (1.0 - ADAM_B2) * (g * g)
        nm_ref[...] = nm
        nv_ref[...] = nv
        d_ref[...] = -ADAM_LR * ((nm / b1c) / (jnp.sqrt(nv / b2c) + ADAM_EPS) + ADAM_WD * w_ref[...])

    spec = pl.BlockSpec((tr, c), lambda i: (i, 0))
    return pl.pallas_call(body, name=name, grid=(r // tr,), in_specs=[spec] * 4, out_specs=[spec] * 3,
                          out_shape=[SDS((r, c), f32)] * 3)(w, g, m, v)


def _pack(arrays, total_rows=None):
    parts = []
    rows = 0
    for a in arrays:
        flat = a.reshape(-1, LANES)
        pad = -flat.shape[0] % 8
        parts.append(jnp.pad(flat, ((0, pad), (0, 0))))
        rows += flat.shape[0] + pad
    if total_rows is not None:
        parts.append(jnp.zeros((total_rows - rows, LANES), arrays[0].dtype))
    return jnp.concatenate(parts, axis=0)


def _unpack(buf, shapes):
    out = []
    row = 0
    lead = buf.shape[:-2]
    for shape in shapes:
        size = 1
        for s in shape:
            size *= s
        rows = size // LANES
        out.append(buf[..., row:row + rows, :].reshape(lead + tuple(shape)))
        row += rows + (-rows % 8)
    return out


BIG = ("e_w_in", "e_w_out", "o_w_in", "o_w_out")
SHARDED_SMALL = {
    "e_pool_w": ((4, 64, 256), 1), "o_pre_norm": ((512,), 0), "o_sgu_norm_g": ((256,), 0), "o_sgu_norm_b": ((256,), 0),
    "o_conv_w": ((31, 256), 1), "o_conv_b": ((256,), 0), "o_conv_norm_g": ((256,), 0), "o_conv_norm_b": ((256,), 0),
    "o_post_norm": ((512,), 0),
}
REPLICATED_SMALL = {"e_pre_norm": (2048,), "e_pool_scale": (1024,), "e_post_norm": (2048,),
                    "o_sgu_w": (4, 128, 128), "o_sgu_b": (4, 128)}
SMALL_ORDER = ("e_pre_norm", "e_pool_w", "e_pool_scale", "e_post_norm", "o_pre_norm", "o_sgu_norm_g", "o_sgu_norm_b",
               "o_sgu_w", "o_sgu_b", "o_conv_w", "o_conv_b", "o_conv_norm_g", "o_conv_norm_b", "o_post_norm")
ALL_ORDER = ("e_pre_norm", "e_w_in", "e_pool_w", "e_pool_scale", "e_w_out", "e_post_norm", "o_pre_norm", "o_w_in",
             "o_sgu_norm_g", "o_sgu_norm_b", "o_sgu_w", "o_sgu_b", "o_conv_w", "o_conv_b", "o_conv_norm_g",
             "o_conv_norm_b", "o_w_out", "o_post_norm")


def _full_shape(name):
    shape, axis = SHARDED_SMALL[name]
    return tuple(s * N_CHIPS if i == axis else s for i, s in enumerate(shape))


def _from_chips(name, stacked):
    shape, axis = SHARDED_SMALL[name]
    return jnp.moveaxis(stacked, 0, axis).reshape(_full_shape(name))


def _my_shard(name, full, chip):
    shape, axis = SHARDED_SMALL[name]
    return lax.dynamic_slice_in_dim(full, chip * shape[axis], shape[axis], axis)


def kernel(x, e_pre_norm, e_w_in, e_pool_w, e_pool_scale, e_w_out, e_post_norm, o_pre_norm, o_w_in, o_sgu_norm_g, o_sgu_norm_b, o_sgu_w, o_sgu_b, o_conv_w, o_conv_b, o_conv_norm_g, o_conv_norm_b, o_w_out, o_post_norm, loss_target, m_e_pre_norm, m_e_w_in, m_e_pool_w, m_e_pool_scale, m_e_w_out, m_e_post_norm, m_o_pre_norm, m_o_w_in, m_o_sgu_norm_g, m_o_sgu_norm_b, m_o_sgu_w, m_o_sgu_b, m_o_conv_w, m_o_conv_b, m_o_conv_norm_g, m_o_conv_norm_b, m_o_w_out, m_o_post_norm, v_e_pre_norm, v_e_w_in, v_e_pool_w, v_e_pool_scale, v_e_w_out, v_e_post_norm, v_o_pre_norm, v_o_w_in, v_o_sgu_norm_g, v_o_sgu_norm_b, v_o_sgu_w, v_o_sgu_b, v_o_conv_w, v_o_conv_b, v_o_conv_norm_g, v_o_conv_norm_b, v_o_w_out, v_o_post_norm):
    w = dict(e_pre_norm=e_pre_norm, e_w_in=e_w_in, e_pool_w=e_pool_w, e_pool_scale=e_pool_scale, e_w_out=e_w_out,
             e_post_norm=e_post_norm, o_pre_norm=o_pre_norm, o_w_in=o_w_in, o_sgu_norm_g=o_sgu_norm_g,
             o_sgu_norm_b=o_sgu_norm_b, o_sgu_w=o_sgu_w, o_sgu_b=o_sgu_b, o_conv_w=o_conv_w, o_conv_b=o_conv_b,
             o_conv_norm_g=o_conv_norm_g, o_conv_norm_b=o_conv_norm_b, o_w_out=o_w_out, o_post_norm=o_post_norm)
    m = dict(e_pre_norm=m_e_pre_norm, e_w_in=m_e_w_in, e_pool_w=m_e_pool_w, e_pool_scale=m_e_pool_scale,
             e_w_out=m_e_w_out, e_post_norm=m_e_post_norm, o_pre_norm=m_o_pre_norm, o_w_in=m_o_w_in,
             o_sgu_norm_g=m_o_sgu_norm_g, o_sgu_norm_b=m_o_sgu_norm_b, o_sgu_w=m_o_sgu_w, o_sgu_b=m_o_sgu_b,
             o_conv_w=m_o_conv_w, o_conv_b=m_o_conv_b, o_conv_norm_g=m_o_conv_norm_g, o_conv_norm_b=m_o_conv_norm_b,
             o_w_out=m_o_w_out, o_post_norm=m_o_post_norm)
    v = dict(e_pre_norm=v_e_pre_norm, e_w_in=v_e_w_in, e_pool_w=v_e_pool_w, e_pool_scale=v_e_pool_scale,
             e_w_out=v_e_w_out, e_post_norm=v_e_post_norm, o_pre_norm=v_o_pre_norm, o_w_in=v_o_w_in,
             o_sgu_norm_g=v_o_sgu_norm_g, o_sgu_norm_b=v_o_sgu_norm_b, o_sgu_w=v_o_sgu_w, o_sgu_b=v_o_sgu_b,
             o_conv_w=v_o_conv_w, o_conv_b=v_o_conv_b, o_conv_norm_g=v_o_conv_norm_g, o_conv_norm_b=v_o_conv_norm_b,
             o_w_out=v_o_w_out, o_post_norm=v_o_post_norm)
    w, m, v = ({k: a[0] for k, a in d.items()} for d in (w, m, v))
    chip = 2 * lax.axis_index("x") + lax.axis_index("y")

    loss, grad_x, in_flight, small = _step(x[0], loss_target[0], w, chip)

    grads, delta, new_m, new_v = {}, {}, {}, {}
    after = grad_x
    for k in ("o_w_out", "o_w_in", "e_w_out", "e_w_in"):
        grads[k] = _land(in_flight, k, chip, after)
        delta[k], new_m[k], new_v[k] = _adamw(w[k], grads[k], m[k], v[k], f"adamw_{k}")
        after = delta[k]

    small_full_shapes = {k: (_full_shape(k) if k in SHARDED_SMALL else REPLICATED_SMALL[k]) for k in SMALL_ORDER}
    small_parts = _pack([small[k].reshape(small_full_shapes[k]) for k in SMALL_ORDER], total_rows=SMALL_GRAD_ROWS)
    small_parts = small_parts + 0.0 * after[0, 0]
    reduced = _reduce_scatter([small_parts.reshape(N_CHIPS, SMALL_GRAD_ROWS // N_CHIPS, LANES)], "small")
    small_sum = _all_gather(reduced, "gather_small_grads")[0].reshape(SMALL_GRAD_ROWS, LANES)
    for k, a in zip(SMALL_ORDER, _unpack(small_sum, [small_full_shapes[k] for k in SMALL_ORDER])):
        grads[k] = _my_shard(k, a, chip) if k in SHARDED_SMALL else a
    loss = lax.psum(loss[0, 0], ("x", "y", "c"))
    local_shapes = [w[k].shape for k in SMALL_ORDER]
    packed = [_pack([d[k] for k in SMALL_ORDER]) for d in (w, grads, m, v)]
    for d, buf in zip((delta, new_m, new_v), _adamw(*packed, "adamw_small")):
        for k, a in zip(SMALL_ORDER, _unpack(buf, local_shapes)):
            d[k] = a

    outs = [loss, grad_x[None]]
    for d in (grads, delta, new_m, new_v):
        outs += [d[k][None] for k in ALL_ORDER]
    return tuple(outs)
```

```python
import jax
import jax.numpy as jnp
from jax import lax
from jax.experimental import pallas as pl
from jax.experimental.pallas import tpu as pltpu

f32 = jnp.float32
bf16 = jnp.bfloat16
SDS = jax.ShapeDtypeStruct

SEQ = 2048
D_MODEL = 2048
EPS = 1e-6
NEG = -1e30
HEAD_DIM = 128
ROT_HALF = 16
ROPE_THETA = 500000.0
DILATIONS = (1, 4, 16)
SPAN = 128
N_HEADS = 8
HALF = 1024
POOL_CH = 256
CONV_K = 31
CONV_PAD = 32
CHUNK = 128
N_CHIPS = 4
LANES = 256
SMALL_SHARD_ROWS = 352
SMALL_GRAD_ROWS = 1536
ANY = pl.BlockSpec(memory_space=pl.ANY)
MESH = pl.DeviceIdType.MESH

ADAM_LR = 0.001
ADAM_B1 = 0.9
ADAM_B2 = 0.999
ADAM_EPS = 1e-08
ADAM_WD = 0.01
ADAM_STEP = 10


def _dot(a, b):
    return jnp.dot(a, b, preferred_element_type=f32)


def _dot_nt(a, b):
    return lax.dot_general(a, b, (((1,), (1,)), ((), ())), preferred_element_type=f32)


def _dot_tn(a, b):
    return lax.dot_general(a, b, (((0,), (0,)), ((), ())), preferred_element_type=f32)


def _sigmoid(x):
    return 1.0 / (1.0 + jnp.exp(-x))


def _silu_and_grad(x):
    s = _sigmoid(x)
    return x * s, s * (1.0 + x * (1.0 - s))


def _rms_fwd(x, g):
    r = lax.rsqrt(jnp.mean(x * x, axis=-1, keepdims=True) + EPS)
    return x * r * g


def _rms_bwd(x, g, dout):
    r = lax.rsqrt(jnp.mean(x * x, axis=-1, keepdims=True) + EPS)
    xh = x * r
    dg = jnp.sum(dout * xh, axis=0, keepdims=True)
    dxh = dout * g
    dx = r * (dxh - xh * jnp.mean(dxh * xh, axis=-1, keepdims=True))
    return dx, dg


def _ln_stats(x):
    mu = jnp.mean(x, axis=-1, keepdims=True)
    xc = x - mu
    rstd = lax.rsqrt(jnp.mean(xc * xc, axis=-1, keepdims=True) + EPS)
    return xc * rstd, rstd


def _ln_bwd(xh, rstd, g, dout):
    dg = jnp.sum(dout * xh, axis=0, keepdims=True)
    db = jnp.sum(dout, axis=0, keepdims=True)
    dxh = dout * g
    dx = rstd * (dxh - jnp.mean(dxh, axis=-1, keepdims=True) - xh * jnp.mean(dxh * xh, axis=-1, keepdims=True))
    return dx, dg, db


def _accumulate(ref, value, first):
    @pl.when(first)
    def _():
        ref[...] = value

    @pl.when(jnp.logical_not(first))
    def _():
        ref[...] += value


def _col_tile(ns):
    for t in (1024, 768, 512, 256):
        if ns % t == 0:
            return t
    raise ValueError(ns)


def _mm_nn(a, w, out_dtype, name):
    m, k = a.shape
    j, _, ns = w.shape
    tm, tn = 1024, _col_tile(ns)
    nb = ns // tn

    def body(a_ref, w_ref, o_ref):
        o_ref[...] = _dot(a_ref[...], w_ref[...]).astype(o_ref.dtype)

    return pl.pallas_call(
        body, name=name, grid=(j * nb, m // tm),
        in_specs=[pl.BlockSpec((tm, k), lambda n, i: (i, 0)),
                  pl.BlockSpec((None, k, tn), lambda n, i: (n // nb, 0, n % nb))],
        out_specs=pl.BlockSpec((tm, tn), lambda n, i: (i, n)),
        out_shape=SDS((m, j * ns), out_dtype),
    )(a, w)


def _mm_nt(dz, w, out_dtype, name, after):
    m, _ = dz.shape
    j, k, ns = w.shape
    tm, tk, tn = 1024, 1024, _col_tile(ns)
    nb = ns // tn
    steps = j * nb

    def body(dz_ref, w_ref, after_ref, o_ref, acc_ref):
        r = pl.program_id(2)
        _accumulate(acc_ref, _dot_nt(dz_ref[...], w_ref[...]), r == 0)

        @pl.when(r == steps - 1)
        def _():
            o_ref[...] = acc_ref[...].astype(o_ref.dtype)

    return pl.pallas_call(
        body, name=name, grid=(m // tm, k // tk, steps),
        in_specs=[pl.BlockSpec((tm, tn), lambda i, kk, r: (i, r)),
                  pl.BlockSpec((None, tk, tn), lambda i, kk, r: (r // nb, kk, r % nb)), ANY],
        out_specs=pl.BlockSpec((tm, tk), lambda i, kk, r: (i, kk)),
        out_shape=SDS((m, k), out_dtype),
        scratch_shapes=[pltpu.VMEM((tm, tk), f32)],
    )(dz, w, after)


def _mm_tn(a, dz, j, name):
    m, k = a.shape
    ns = dz.shape[1] // j
    tk, tn = 1024, _col_tile(ns)
    nb = ns // tn

    def body(a_ref, dz_ref, o_ref):
        o_ref[...] = _dot_tn(a_ref[...], dz_ref[...]).astype(o_ref.dtype)

    return pl.pallas_call(
        body, name=name, grid=(k // tk, j * nb),
        in_specs=[pl.BlockSpec((m, tk), lambda kk, n: (0, kk)),
                  pl.BlockSpec((m, tn), lambda kk, n: (0, n))],
        out_specs=pl.BlockSpec((None, tk, tn), lambda kk, n: (n // nb, kk, n % nb)),
        out_shape=SDS((j, k, ns), bf16),
    )(a, dz)


ROWS = 256


def _row_spec(width=D_MODEL, col=0):
    return pl.BlockSpec((ROWS, width), lambda i: (i, col))


def _vec_spec(width=D_MODEL):
    return pl.BlockSpec((1, width), lambda i: (0, 0))


def _pre_norm(x, g):
    def body(x_ref, g_ref, h_ref):
        h_ref[...] = _rms_fwd(x_ref[...], g_ref[...]).astype(bf16)

    return pl.pallas_call(
        body, name="pre_norm", grid=(SEQ // ROWS,), in_specs=[_row_spec(), _vec_spec()],
        out_specs=_row_spec(), out_shape=SDS((SEQ, D_MODEL), bf16))(x, g)


def _mid_norm(x, y, g_post, g_pre):
    def body(x_ref, y_ref, gpost_ref, gpre_ref, x1_ref, h1_ref):
        x1 = x_ref[...] + _rms_fwd(y_ref[...], gpost_ref[...])
        x1_ref[...] = x1
        h1_ref[...] = _rms_fwd(x1, gpre_ref[...]).astype(bf16)

    return pl.pallas_call(
        body, name="mid_norm", grid=(SEQ // ROWS,),
        in_specs=[_row_spec(), _row_spec(), _vec_spec(), _vec_spec()],
        out_specs=[_row_spec(), _row_spec()],
        out_shape=[SDS((SEQ, D_MODEL), f32), SDS((SEQ, D_MODEL), bf16)])(x, y, g_post, g_pre)


def _final_norm_loss(x1, y, g_post, target):
    def body(x1_ref, y_ref, g_ref, t_ref, loss_ref, dx2_ref, dy_ref, dg_ref):
        first = pl.program_id(0) == 0
        y = y_ref[...]
        g = g_ref[...]
        err = x1_ref[...] + _rms_fwd(y, g) - t_ref[...]
        sq = jnp.sum(jnp.sum(err * err, axis=1, keepdims=True), axis=0, keepdims=True)
        _accumulate(loss_ref, sq * (0.5 / D_MODEL), first)
        dx2 = err * (1.0 / D_MODEL)
        dx2_ref[...] = dx2
        dy, dg = _rms_bwd(y, g, dx2)
        dy_ref[...] = dy.astype(bf16)
        _accumulate(dg_ref, dg, first)

    return pl.pallas_call(
        body, name="final_norm_loss", grid=(SEQ // ROWS,),
        in_specs=[_row_spec(), _row_spec(), _vec_spec(), _row_spec()],
        out_specs=[pl.BlockSpec((1, 1), lambda i: (0, 0)), _row_spec(), _row_spec(), _vec_spec()],
        out_shape=[SDS((1, 1), f32), SDS((SEQ, D_MODEL), f32), SDS((SEQ, D_MODEL), bf16), SDS((1, D_MODEL), f32)],
    )(x1, y, g_post, target)


def _mid_norm_bwd(dx2, dh1, x1, y0, g_pre, g_post):
    def body(dx2_ref, dh1_ref, x1_ref, y0_ref, gpre_ref, gpost_ref, dx1_ref, dy0_ref, dgpre_ref, dgpost_ref):
        first = pl.program_id(0) == 0
        d_in, dgpre = _rms_bwd(x1_ref[...], gpre_ref[...], dh1_ref[...])
        dx1 = dx2_ref[...] + d_in
        dx1_ref[...] = dx1
        dy0, dgpost = _rms_bwd(y0_ref[...], gpost_ref[...], dx1)
        dy0_ref[...] = dy0.astype(bf16)
        _accumulate(dgpre_ref, dgpre, first)
        _accumulate(dgpost_ref, dgpost, first)

    return pl.pallas_call(
        body, name="mid_norm_bwd", grid=(SEQ // ROWS,),
        in_specs=[_row_spec(), _row_spec(), _row_spec(), _row_spec(), _vec_spec(), _vec_spec()],
        out_specs=[_row_spec(), _row_spec(), _vec_spec(), _vec_spec()],
        out_shape=[SDS((SEQ, D_MODEL), f32), SDS((SEQ, D_MODEL), bf16), SDS((1, D_MODEL), f32), SDS((1, D_MODEL), f32)],
    )(dx2, dh1, x1, y0, g_pre, g_post)


def _pre_norm_bwd(dx1, dh0, x, g):
    def body(dx1_ref, dh0_ref, x_ref, g_ref, dx_ref, dg_ref):
        d_in, dg = _rms_bwd(x_ref[...], g_ref[...], dh0_ref[...])
        dx_ref[...] = dx1_ref[...] + d_in
        _accumulate(dg_ref, dg, pl.program_id(0) == 0)

    return pl.pallas_call(
        body, name="pre_norm_bwd", grid=(SEQ // ROWS,),
        in_specs=[_row_spec(), _row_spec(), _row_spec(), _vec_spec()],
        out_specs=[_row_spec(), _vec_spec()],
        out_shape=[SDS((SEQ, D_MODEL), f32), SDS((1, D_MODEL), f32)])(dx1, dh0, x, g)


def _pool_count(g):
    row = lax.broadcasted_iota(jnp.int32, (SEQ, 1), 0)
    width = jnp.left_shift(2, g)
    return row, width, jnp.minimum(row + 1, width).astype(f32)


def _trailing_sum(x, row, width):
    s = x
    for k in (1, 2, 4, 8):
        shifted = jnp.where(row >= k, pltpu.roll(s, k, 0), 0.0)
        s = jnp.where(width > k, s + shifted, s)
    return s


def _leading_sum(x, row, width):
    s = x
    for k in (1, 2, 4, 8):
        shifted = jnp.where(row < SEQ - k, pltpu.roll(s, SEQ - k, 0), 0.0)
        s = jnp.where(width > k, s + shifted, s)
    return s


def _pool_specs():
    a_in = pl.BlockSpec((SEQ, POOL_CH), lambda g: (0, g))
    a_gate = pl.BlockSpec((SEQ, POOL_CH), lambda g: (0, 4 + g))
    w = pl.BlockSpec((None, POOL_CH, POOL_CH), lambda g: (g, 0, 0))
    scale = pl.BlockSpec((1, POOL_CH), lambda g: (0, g))
    return a_in, a_gate, w, scale


def _pool_fwd(z0, pool_w, pool_scale):
    def body(a_ref, gate_ref, w_ref, scale_ref, ya_ref):
        row, width, count = _pool_count(pl.program_id(0))
        a = a_ref[...]
        pooled = _trailing_sum(a, row, width) / count - a
        mixed = _dot(pooled.astype(bf16), w_ref[...]) * scale_ref[...]
        gate = gate_ref[...]
        ya_ref[...] = (mixed * gate * _sigmoid(gate)).astype(bf16)

    return pl.pallas_call(
        body, name="pool_fwd", grid=(4,), in_specs=list(_pool_specs()),
        out_specs=pl.BlockSpec((SEQ, POOL_CH), lambda g: (0, g)),
        out_shape=SDS((SEQ, HALF), bf16))(z0, z0, pool_w, pool_scale)


def _pool_bwd(z0, dcat, pool_w, pool_scale):
    def body(a_ref, gate_ref, w_ref, scale_ref, dya_ref, da_ref, dgate_ref, dw_ref, dscale_ref):
        row, width, count = _pool_count(pl.program_id(0))
        a = a_ref[...]
        pooled = (_trailing_sum(a, row, width) / count - a).astype(bf16)
        w = w_ref[...]
        scale = scale_ref[...]
        mixed = _dot(pooled, w)
        silu, dsilu = _silu_and_grad(gate_ref[...])
        dya = dya_ref[...]
        dgate_ref[...] = (dya * mixed * scale * dsilu).astype(bf16)
        dms = dya * silu
        dscale_ref[...] = jnp.sum(dms * mixed, axis=0, keepdims=True)
        dmixed = (dms * scale).astype(bf16)
        dw_ref[...] = _dot_tn(pooled, dmixed)
        dpooled = _dot_nt(dmixed, w)
        da_ref[...] = (_leading_sum(dpooled / count, row, width) - dpooled).astype(bf16)

    a_in, a_gate, w, scale = _pool_specs()
    col = pl.BlockSpec((SEQ, POOL_CH), lambda g: (0, g))
    return pl.pallas_call(
        body, name="pool_bwd", grid=(4,), in_specs=[a_in, a_gate, w, scale, col],
        out_specs=[col, col, w, scale],
        out_shape=[SDS((SEQ, HALF), bf16), SDS((SEQ, HALF), bf16), SDS((4, POOL_CH, POOL_CH), f32), SDS((1, HALF), f32)],
    )(z0, z0, pool_w, pool_scale, dcat)


Q_COL, K_COL, V_COL, BGATE_COL = 16, 40, 64, 88


def _rope_tables():
    pos = jnp.arange(SEQ, dtype=f32)
    inv_freq = jnp.power(ROPE_THETA, -jnp.arange(0, 2 * ROT_HALF, 2, dtype=f32) / (2 * ROT_HALF))
    ang = pos[:, None] * inv_freq[None, :]
    cos, sin = jnp.cos(ang), jnp.sin(ang)
    zeros = jnp.zeros((SEQ, HEAD_DIM - 2 * ROT_HALF), f32)
    cos_t = jnp.concatenate([cos, cos, zeros + 1.0], axis=1)
    sin_t = jnp.concatenate([sin, sin, zeros], axis=1)
    j = jnp.arange(HEAD_DIM)[:, None]
    i = jnp.arange(HEAD_DIM)[None, :]
    rot = jnp.where((i < ROT_HALF) & (j == i + ROT_HALF), -1.0, 0.0) + jnp.where(
        (i >= ROT_HALF) & (i < 2 * ROT_HALF) & (j == i - ROT_HALF), 1.0, 0.0)
    return cos_t, sin_t, rot.astype(bf16), rot.T.astype(bf16)


def _exact_dot(t, m):
    hi = t.astype(bf16)
    lo = (t - hi.astype(f32)).astype(bf16)
    return _dot(hi, m) + _dot(lo, m)


def _rope(t, cos_t, sin_t, rot):
    return t * cos_t + _exact_dot(t, rot) * sin_t


def _rope_transposed(d, cos_t, sin_t, rot_t):
    return d * cos_t + _exact_dot(d * sin_t, rot_t)


ROW_CHUNK = 256


def _chunks(fn):
    def step(i, carry):
        fn(pl.multiple_of(i * ROW_CHUNK, ROW_CHUNK))
        return carry

    lax.fori_loop(0, SEQ // ROW_CHUNK, step, 0)


def _pieces(dilation):
    length = SEQ // dilation
    n = min(length, ROW_CHUNK)
    return [(r, l0, n) for r in range(dilation) for l0 in range(0, length, n)]


def _by_residue(dst_ref, src_ref, dilation, dtype):
    length = SEQ // dilation
    for r, l0, n in _pieces(dilation):
        src = src_ref[l0:l0 + n, :] if dilation == 1 else src_ref[pl.ds(r + dilation * l0, n, stride=dilation), :]
        start = r * length + l0
        dst_ref[start:start + n, :] = src.astype(dtype)


def _by_position(dst_ref, src_ref, dilation):
    length = SEQ // dilation
    for r, l0, n in _pieces(dilation):
        src = src_ref[r * length + l0:r * length + l0 + n, :]
        if dilation == 1:
            dst_ref[l0:l0 + n, :] = src
        else:
            dst_ref[pl.ds(r + dilation * l0, n, stride=dilation), :] = src


def _attn_masks():
    qi = lax.broadcasted_iota(jnp.int32, (SPAN, 2 * SPAN), 0)
    kj = lax.broadcasted_iota(jnp.int32, (SPAN, 2 * SPAN), 1)
    window = ((kj < SPAN) & (kj >= qi)) | ((kj >= SPAN) & (kj - SPAN <= qi))
    own = lax.broadcasted_iota(jnp.int32, (SPAN, SPAN), 1) <= lax.broadcasted_iota(jnp.int32, (SPAN, SPAN), 0)
    return window, own


def _attn_blocks(dilation):
    per_residue = SEQ // dilation // SPAN
    blocks = [(c, c % per_residue != 0) for c in range(SEQ // SPAN)]
    return [blocks[i:i + 4] for i in range(0, len(blocks), 4)]


def _block_keys(c, has_prev):
    return slice((c - 1) * SPAN if has_prev else c * SPAN, (c + 1) * SPAN)


def _head_spec(col):
    return pl.BlockSpec((SEQ, HEAD_DIM), lambda h: (0, col + h))


def _table_spec():
    return pl.BlockSpec((SEQ, HEAD_DIM), lambda h: (0, 0))


def _attn_fwd(z0, tables):
    scale = HEAD_DIM ** -0.5

    def body(*refs):
        qkv = refs[0:9]
        bg_ref, cos_ref, sin_ref, rot_ref = refs[9:13]
        yb_ref, att_ref, lse_ref = refs[13:16]
        saved = refs[16:25]
        tmp_q, tmp_k, v_ones, o_res, l_res, o_nat, l_nat = refs[25:32]
        window_mask, own_mask = _attn_masks()
        rot = rot_ref[...]

        @pl.when(pl.program_id(0) == 0)
        def _():
            v_ones[:, HEAD_DIM:] = jnp.ones((SEQ, HEAD_DIM), bf16)

        for g, dilation in enumerate(DILATIONS):
            q_ref, k_ref, v_ref = qkv[3 * g:3 * g + 3]
            qd, kd, vd = saved[3 * g:3 * g + 3]

            def rope_rows(start, q_ref=q_ref, k_ref=k_ref):
                r = pl.ds(start, ROW_CHUNK)
                cos_t, sin_t = cos_ref[r, :], sin_ref[r, :]
                tmp_q[r, :] = _rope(q_ref[r, :], cos_t, sin_t, rot) * scale
                tmp_k[r, :] = _rope(k_ref[r, :], cos_t, sin_t, rot)

            _chunks(rope_rows)
            _by_residue(qd, tmp_q, dilation, bf16)
            _by_residue(kd, tmp_k, dilation, bf16)
            _by_residue(vd, v_ref, dilation, bf16)
            for l0 in range(0, SEQ, ROW_CHUNK):
                v_ones[l0:l0 + ROW_CHUNK, 0:HEAD_DIM] = vd[l0:l0 + ROW_CHUNK, :]

            for four in _attn_blocks(dilation):
                scores = [_dot_nt(qd[c * SPAN:(c + 1) * SPAN, :], kd[_block_keys(c, prev), :]) for c, prev in four]
                tops, probs = [], []
                for (c, prev), s in zip(four, scores):
                    s = jnp.where(window_mask if prev else own_mask, s, NEG)
                    tops.append(jnp.max(s, axis=1, keepdims=True))
                    probs.append(jnp.exp(s - tops[-1]).astype(bf16))
                sums = [_dot(p, v_ones[_block_keys(c, prev), :]) for (c, prev), p in zip(four, probs)]
                for (c, prev), m, o in zip(four, tops, sums):
                    den = o[:, HEAD_DIM:]
                    o_res[c * SPAN:(c + 1) * SPAN, :] = o[:, :HEAD_DIM] / den
                    l_res[c * SPAN:(c + 1) * SPAN, :] = m + jnp.log(den)

            if dilation > 1:
                _by_position(o_nat, o_res, dilation)
                _by_position(l_nat, l_res, dilation)
            o_g, l_g = (o_res, l_res) if dilation == 1 else (o_nat, l_nat)

            def merge(start, g=g, o_g=o_g, l_g=l_g):
                r = pl.ds(start, ROW_CHUNK)
                if g == 0:
                    att, total = o_g[r, :], l_g[r, :]
                else:
                    l_old, l_new = lse_ref[r, :], l_g[r, :]
                    top = jnp.maximum(l_old, l_new)
                    total = top + jnp.log(jnp.exp(l_old - top) + jnp.exp(l_new - top))
                    att = att_ref[r, :] * jnp.exp(l_old - total) + o_g[r, :] * jnp.exp(l_new - total)
                att_ref[r, :] = att
                lse_ref[r, :] = total
                if g == len(DILATIONS) - 1:
                    gate = bg_ref[r, :]
                    yb_ref[r, :] = (att * gate * _sigmoid(gate)).astype(bf16)

            _chunks(merge)

    in_specs = []
    for g in range(3):
        in_specs += [_head_spec(Q_COL + 8 * g), _head_spec(K_COL + 8 * g), _head_spec(V_COL + 8 * g)]
    in_specs += [_head_spec(BGATE_COL), _table_spec(), _table_spec(), pl.BlockSpec((HEAD_DIM, HEAD_DIM), lambda h: (0, 0))]
    out_spec = pl.BlockSpec((SEQ, HEAD_DIM), lambda h: (0, h))
    vm = lambda dt: pltpu.VMEM((SEQ, HEAD_DIM), dt)
    cos_t, sin_t, rot, _ = tables
    out = pl.pallas_call(
        body, name="attn_fwd", grid=(N_HEADS,), in_specs=in_specs, out_specs=[out_spec] * 12,
        out_shape=[SDS((SEQ, HALF), bf16), SDS((SEQ, HALF), f32), SDS((SEQ, HALF), f32)] + [SDS((SEQ, HALF), bf16)] * 9,
        scratch_shapes=[vm(f32), vm(f32), pltpu.VMEM((SEQ, 2 * HEAD_DIM), bf16), vm(f32), vm(f32), vm(f32), vm(f32)],
    )(*([z0] * 10), cos_t, sin_t, rot)
    return out[0], out[1], out[2], [tuple(out[3 + 3 * g:6 + 3 * g]) for g in range(3)]


def _attn_bwd_group(g, saved, z0, att, lse, dcat, tables):
    scale = HEAD_DIM ** -0.5
    dilation = DILATIONS[g]
    with_gate = g == 0

    def body(*refs):
        qd, kd, vd, bg_ref, att_ref, lse_ref, dyb_ref, cos_ref, sin_ref, rot_t_ref = refs[0:10]
        n_out = 4 if with_gate else 3
        dq_ref, dk_ref, dv_ref = refs[10:13]
        dod, ld, dd, tmp, aq, ak, av = refs[10 + n_out:17 + n_out]
        window_mask, own_mask = _attn_masks()
        rot_t = rot_t_ref[...]

        def gate_rows(start):
            r = pl.ds(start, ROW_CHUNK)
            silu, dsilu = _silu_and_grad(bg_ref[r, :])
            att_v = att_ref[r, :]
            dyb = dyb_ref[r, :]
            if with_gate:
                refs[13][r, :] = (dyb * att_v * dsilu).astype(bf16)
            datt = dyb * silu
            tmp[r, :] = datt
            aq[r, :] = jnp.broadcast_to(jnp.sum(datt * att_v, axis=1, keepdims=True), (ROW_CHUNK, HEAD_DIM))

        _chunks(gate_rows)
        _by_residue(dod, tmp, dilation, bf16)
        _by_residue(dd, aq, dilation, f32)
        _by_residue(ld, lse_ref, dilation, f32)

        for four in _attn_blocks(dilation):
            rows = [slice(c * SPAN, (c + 1) * SPAN) for c, _ in four]
            keys = [_block_keys(c, prev) for c, prev in four]
            scores = [_dot_nt(qd[r, :], kd[k, :]) for r, k in zip(rows, keys)]
            dprobs = [_dot_nt(dod[r, :], vd[k, :]) for r, k in zip(rows, keys)]
            probs, dscores = [], []
            for (c, prev), r, s, dp in zip(four, rows, scores, dprobs):
                lse_q, delta = ld[r, :], dd[r, :]
                if prev:
                    lse_q = jnp.concatenate([lse_q, lse_q], axis=1)
                    delta = jnp.concatenate([delta, delta], axis=1)
                p = jnp.where(window_mask if prev else own_mask, jnp.exp(s - lse_q), 0.0)
                probs.append(p.astype(bf16))
                dscores.append((p * (dp - delta)).astype(bf16))
            dvs = [_dot_tn(p, dod[r, :]) for p, r in zip(probs, rows)]
            dks = [_dot_tn(ds, qd[r, :]) for ds, r in zip(dscores, rows)]
            dqs = [_dot(ds, kd[k, :]) for ds, k in zip(dscores, keys)]
            for (c, prev), r, dv, dk, dq in zip(four, rows, dvs, dks, dqs):
                aq[r, :] = dq
                if prev:
                    before = slice((c - 1) * SPAN, c * SPAN)
                    av[before, :] += dv[0:SPAN]
                    ak[before, :] += dk[0:SPAN]
                    av[r, :] = dv[SPAN:]
                    ak[r, :] = dk[SPAN:]
                else:
                    av[r, :] = dv
                    ak[r, :] = dk

        def finish(out_ref, acc, factor, roped):
            if dilation > 1:
                _by_position(tmp, acc, dilation)
            src = acc if dilation == 1 else tmp

            def rows(start):
                r = pl.ds(start, ROW_CHUNK)
                d = src[r, :]
                if factor != 1.0:
                    d = d * factor
                if roped:
                    d = _rope_transposed(d, cos_ref[r, :], sin_ref[r, :], rot_t)
                out_ref[r, :] = d.astype(bf16)

            _chunks(rows)

        finish(dq_ref, aq, scale, True)
        finish(dk_ref, ak, 1.0, True)
        finish(dv_ref, av, 1.0, False)

    head = pl.BlockSpec((SEQ, HEAD_DIM), lambda h: (0, h))
    in_specs = [head, head, head, _head_spec(BGATE_COL), head, head, _head_spec(8), _table_spec(), _table_spec(),
                pl.BlockSpec((HEAD_DIM, HEAD_DIM), lambda h: (0, 0))]
    n_out = 4 if with_gate else 3
    vm = lambda dt: pltpu.VMEM((SEQ, HEAD_DIM), dt)
    cos_t, sin_t, _, rot_t = tables
    return pl.pallas_call(
        body, name=f"attn_bwd_g{g}", grid=(N_HEADS,), in_specs=in_specs, out_specs=[head] * n_out,
        out_shape=[SDS((SEQ, HALF), bf16)] * n_out,
        scratch_shapes=[vm(bf16), vm(f32), vm(f32), vm(f32), vm(f32), vm(f32), vm(f32)],
    )(*saved, z0, att, lse, dcat, cos_t, sin_t, rot_t)


def _sgu_specs():
    chunk = lambda col: pl.BlockSpec((CHUNK, HALF), lambda n: (n, col))
    vec = pl.BlockSpec((1, HALF), lambda n: (0, 0))
    w = pl.BlockSpec((4, CHUNK, CHUNK), lambda n: (0, 0, 0))
    bias = pl.BlockSpec((CHUNK, CHUNK), lambda n: (0, 0))
    return chunk, vec, w, bias


def _sgu_weights(w_ref):
    tril = lax.broadcasted_iota(jnp.int32, (CHUNK, CHUNK), 1) <= lax.broadcasted_iota(jnp.int32, (CHUNK, CHUNK), 0)
    return tril, [jnp.where(tril, w_ref[h], 0.0).astype(bf16) for h in range(4)]


def _sgu_fwd(z1, ln_g, ln_b, sgu_w, bias_t):
    def body(u_ref, v_ref, cg_ref, g_ref, b_ref, w_ref, bias_ref, yc_ref):
        _, ws = _sgu_weights(w_ref)
        xh, _ = _ln_stats(v_ref[...])
        vn = (xh * g_ref[...] + b_ref[...]).astype(bf16)
        for h in range(4):
            cols = slice(h * POOL_CH, (h + 1) * POOL_CH)
            s = _dot(ws[h], vn[:, cols]) + bias_ref[:, h:h + 1]
            gate = cg_ref[:, cols]
            yc_ref[:, cols] = (u_ref[:, cols] * s * gate * _sigmoid(gate)).astype(bf16)

    chunk, vec, w, bias = _sgu_specs()
    return pl.pallas_call(
        body, name="sgu_fwd", grid=(SEQ // CHUNK,),
        in_specs=[chunk(0), chunk(1), chunk(2), vec, vec, w, bias], out_specs=chunk(0),
        out_shape=SDS((SEQ, HALF), bf16))(z1, z1, z1, ln_g, ln_b, sgu_w, bias_t)


def _sgu_bwd(z1, dcat, ln_g, ln_b, sgu_w, bias_t):
    def body(u_ref, v_ref, cg_ref, dyc_ref, g_ref, b_ref, w_ref, bias_ref,
             du_ref, dv_ref, dcg_ref, dw_ref, dbias_ref, dg_ref, db_ref, dvn_ref):
        first = pl.program_id(0) == 0
        tril, ws = _sgu_weights(w_ref)
        xh, rstd = _ln_stats(v_ref[...])
        g = g_ref[...]
        vn = (xh * g + b_ref[...]).astype(bf16)

        @pl.when(first)
        def _():
            dbias_ref[...] = jnp.zeros((CHUNK, CHUNK), f32)

        for h in range(4):
            cols = slice(h * POOL_CH, (h + 1) * POOL_CH)
            vn_h = vn[:, cols]
            s = _dot(ws[h], vn_h) + bias_ref[:, h:h + 1]
            silu, dsilu = _silu_and_grad(cg_ref[:, cols])
            dyc = dyc_ref[:, cols]
            u = u_ref[:, cols]
            du_ref[:, cols] = (dyc * s * silu).astype(bf16)
            dcg_ref[:, cols] = (dyc * u * s * dsilu).astype(bf16)
            ds = dyc * u * silu
            dbias_ref[:, h:h + 1] += jnp.sum(ds, axis=1, keepdims=True)
            ds = ds.astype(bf16)
            _accumulate(dw_ref.at[h], jnp.where(tril, _dot_nt(ds, vn_h), 0.0), first)
            dvn_ref[:, cols] = _dot_tn(ws[h], ds)
        dv, dg, db = _ln_bwd(xh, rstd, g, dvn_ref[...])
        dv_ref[...] = dv.astype(bf16)
        _accumulate(dg_ref, dg, first)
        _accumulate(db_ref, db, first)

    chunk, vec, w, bias = _sgu_specs()
    return pl.pallas_call(
        body, name="sgu_bwd", grid=(SEQ // CHUNK,),
        in_specs=[chunk(0), chunk(1), chunk(2), chunk(0), vec, vec, w, bias],
        out_specs=[chunk(0), chunk(0), chunk(0), w, bias, vec, vec],
        out_shape=[SDS((SEQ, HALF), bf16)] * 3 + [SDS((4, CHUNK, CHUNK), f32), SDS((CHUNK, CHUNK), f32),
                                                   SDS((1, HALF), f32), SDS((1, HALF), f32)],
        scratch_shapes=[pltpu.VMEM((CHUNK, HALF), f32)],
    )(z1, z1, z1, dcat, ln_g, ln_b, sgu_w, bias_t)


CONV_TILE = 128
DVAL_COL, DGLU_COL = 12, 16


def _conv_specs():
    val = pl.BlockSpec((SEQ, POOL_CH), lambda j: (0, DVAL_COL + j))
    glu = pl.BlockSpec((SEQ, POOL_CH), lambda j: (0, DGLU_COL + j))
    w = pl.BlockSpec((CONV_K, POOL_CH), lambda j: (0, j))
    col = pl.BlockSpec((SEQ, POOL_CH), lambda j: (0, j))
    vec = pl.BlockSpec((1, POOL_CH), lambda j: (0, j))
    return val, glu, w, col, vec


def _conv_fwd(z1, conv_w, conv_b):
    def body(val_ref, glu_ref, w_ref, b_ref, out_ref, xpad):
        xpad[0:CONV_PAD, :] = jnp.zeros((CONV_PAD, POOL_CH), f32)
        xpad[CONV_PAD:, :] = val_ref[...] * _sigmoid(glu_ref[...])
        w = w_ref[...]
        bias = b_ref[...]

        def tile(i, carry):
            t0 = pl.multiple_of(i * CONV_TILE, CONV_TILE)
            window = xpad[pl.ds(t0, CONV_TILE + CONV_PAD), :]
            acc = jnp.broadcast_to(bias, (CONV_TILE, POOL_CH))
            for k in range(CONV_K):
                shift = CONV_PAD - (CONV_K - 1) + k
                acc = acc + w[k:k + 1, :] * pltpu.roll(window, CONV_TILE + CONV_PAD - shift, 0)[0:CONV_TILE]
            out_ref[pl.ds(t0, CONV_TILE), :] = acc
            return carry

        lax.fori_loop(0, SEQ // CONV_TILE, tile, 0)

    val, glu, w, col, vec = _conv_specs()
    return pl.pallas_call(
        body, name="conv_fwd", grid=(4,), in_specs=[val, glu, w, vec], out_specs=col,
        out_shape=SDS((SEQ, HALF), f32), scratch_shapes=[pltpu.VMEM((SEQ + CONV_PAD, POOL_CH), f32)],
    )(z1, z1, conv_w, conv_b)


def _conv_bwd(z1, dconv, conv_w):
    def body(val_ref, glu_ref, w_ref, dout_ref, dval_ref, dglu_ref, dw_ref, db_ref, xpad, dpad, dx_ref):
        val = val_ref[...]
        sig = _sigmoid(glu_ref[...])
        xpad[0:CONV_PAD, :] = jnp.zeros((CONV_PAD, POOL_CH), f32)
        xpad[CONV_PAD:, :] = val * sig
        dout = dout_ref[...]
        dpad[0:SEQ, :] = dout
        dpad[SEQ:, :] = jnp.zeros((CONV_PAD, POOL_CH), f32)
        db_ref[...] = jnp.sum(dout, axis=0, keepdims=True)
        dw_ref[...] = jnp.zeros((CONV_K, POOL_CH), f32)
        w = w_ref[...]

        def tile(i, carry):
            t0 = pl.multiple_of(i * CONV_TILE, CONV_TILE)
            x_win = xpad[pl.ds(t0, CONV_TILE + CONV_PAD), :]
            d_win = dpad[pl.ds(t0, CONV_TILE + CONV_PAD), :]
            d_own = d_win[0:CONV_TILE]
            acc = jnp.zeros((CONV_TILE, POOL_CH), f32)
            for k in range(CONV_K):
                shift = CONV_PAD - (CONV_K - 1) + k
                x_k = pltpu.roll(x_win, CONV_TILE + CONV_PAD - shift, 0)[0:CONV_TILE]
                dw_ref[k:k + 1, :] += jnp.sum(d_own * x_k, axis=0, keepdims=True)
                back = CONV_K - 1 - k
                d_k = d_own if back == 0 else pltpu.roll(d_win, CONV_TILE + CONV_PAD - back, 0)[0:CONV_TILE]
                acc = acc + w[k:k + 1, :] * d_k
            dx_ref[pl.ds(t0, CONV_TILE), :] = acc
            return carry

        lax.fori_loop(0, SEQ // CONV_TILE, tile, 0)
        dx = dx_ref[...]
        dval_ref[...] = (dx * sig).astype(bf16)
        dglu_ref[...] = (dx * val * sig * (1.0 - sig)).astype(bf16)

    val, glu, w, col, vec = _conv_specs()
    pad = pltpu.VMEM((SEQ + CONV_PAD, POOL_CH), f32)
    return pl.pallas_call(
        body, name="conv_bwd", grid=(4,), in_specs=[val, glu, w, col], out_specs=[col, col, w, vec],
        out_shape=[SDS((SEQ, HALF), bf16), SDS((SEQ, HALF), bf16), SDS((CONV_K, HALF), f32), SDS((1, HALF), f32)],
        scratch_shapes=[pad, pad, pltpu.VMEM((SEQ, POOL_CH), f32)],
    )(z1, z1, conv_w, dconv)


DGATE_COL = 5


def _conv_norm_fwd(conv, z1, g, b):
    def body(c_ref, gate_ref, g_ref, b_ref, yd_ref):
        xh, _ = _ln_stats(c_ref[...])
        n = xh * g_ref[...] + b_ref[...]
        gate = gate_ref[...]
        yd_ref[...] = (n * _sigmoid(n) * gate * _sigmoid(gate)).astype(bf16)

    return pl.pallas_call(
        body, name="conv_norm_fwd", grid=(SEQ // ROWS,),
        in_specs=[_row_spec(HALF), _row_spec(HALF, DGATE_COL), _vec_spec(HALF), _vec_spec(HALF)],
        out_specs=_row_spec(HALF), out_shape=SDS((SEQ, HALF), bf16))(conv, z1, g, b)


def _conv_norm_bwd(conv, z1, dcat, g, b):
    def body(c_ref, gate_ref, dyd_ref, g_ref, b_ref, dconv_ref, dgate_ref, dg_ref, db_ref):
        first = pl.program_id(0) == 0
        xh, rstd = _ln_stats(c_ref[...])
        g = g_ref[...]
        n_silu, n_dsilu = _silu_and_grad(xh * g + b_ref[...])
        gate_silu, gate_dsilu = _silu_and_grad(gate_ref[...])
        dyd = dyd_ref[...]
        dgate_ref[...] = (dyd * n_silu * gate_dsilu).astype(bf16)
        dconv, dg, db = _ln_bwd(xh, rstd, g, dyd * gate_silu * n_dsilu)
        dconv_ref[...] = dconv
        _accumulate(dg_ref, dg, first)
        _accumulate(db_ref, db, first)

    return pl.pallas_call(
        body, name="conv_norm_bwd", grid=(SEQ // ROWS,),
        in_specs=[_row_spec(HALF), _row_spec(HALF, DGATE_COL), _row_spec(HALF, 1), _vec_spec(HALF), _vec_spec(HALF)],
        out_specs=[_row_spec(HALF), _row_spec(HALF), _vec_spec(HALF), _vec_spec(HALF)],
        out_shape=[SDS((SEQ, HALF), f32), SDS((SEQ, HALF), bf16), SDS((1, HALF), f32), SDS((1, HALF), f32)],
    )(conv, z1, dcat, g, b)


def _step(x, target, w, chip):
    chip_vec = chip.astype(jnp.int32).reshape(1)
    sharded_names = list(SHARDED_SMALL)
    small_shard = _pack([w[k] for k in sharded_names], total_rows=SMALL_SHARD_ROWS)
    small_slot = lax.dynamic_update_slice(jnp.zeros((N_CHIPS, SMALL_SHARD_ROWS, LANES), f32), small_shard[None], (chip, 0, 0))
    slots = [small_slot] + [_cast_into_slot(w[k], chip_vec, f"cast_{k}") for k in BIG]
    sems, bufs, token = _gather_start(slots)
    tables = _rope_tables()

    def vec(k):
        return w[k].reshape(1, -1)

    h0 = _pre_norm(x, vec("e_pre_norm") + token[0, 0])
    small_full, e_w_in = _forward_halves(_gather_wait(bufs[0:2], sems[0:2], h0, "gather_wait_first"), "forward_first")
    p = {k: _from_chips(k, a) for k, a in zip(sharded_names, _unpack(small_full, [SHARDED_SMALL[k][0] for k in sharded_names]))}
    for k in ("o_pre_norm", "o_sgu_norm_g", "o_sgu_norm_b", "o_conv_b", "o_conv_norm_g", "o_conv_norm_b", "o_post_norm"):
        p[k] = p[k].reshape(1, -1)
    pool_w_bf = p["e_pool_w"].astype(bf16)
    bias_t = jnp.pad(w["o_sgu_b"].T, ((0, 0), (0, CHUNK - 4)))

    z0 = _mm_nn(h0, e_w_in, f32, "e_in")
    ya = _pool_fwd(z0, pool_w_bf, vec("e_pool_scale"))
    yb, att, lse, qkv_by_residue = _attn_fwd(z0, tables)
    e_w_out, o_w_in, o_w_out = _forward_halves(_gather_wait(bufs[2:5], sems[2:5], att, "gather_wait_rest"), "forward_rest")
    e_w_out = e_w_out.reshape(1, D_MODEL, D_MODEL)
    o_w_out = o_w_out.reshape(1, D_MODEL, D_MODEL)
    cat0 = jnp.concatenate([ya, yb], axis=1)
    y0 = _mm_nn(cat0, e_w_out, f32, "e_out")
    x1, h1 = _mid_norm(x, y0, vec("e_post_norm"), p["o_pre_norm"])
    z1 = _mm_nn(h1, o_w_in, f32, "o_in")
    yc = _sgu_fwd(z1, p["o_sgu_norm_g"], p["o_sgu_norm_b"], w["o_sgu_w"], bias_t)
    conv = _conv_fwd(z1, p["o_conv_w"], p["o_conv_b"])
    yd = _conv_norm_fwd(conv, z1, p["o_conv_norm_g"], p["o_conv_norm_b"])
    cat1 = jnp.concatenate([yc, yd], axis=1)
    y1 = _mm_nn(cat1, o_w_out, f32, "o_out")
    loss, dx2, dy1, g_o_post = _final_norm_loss(x1, y1, p["o_post_norm"], target)

    in_flight = {}

    def send_off(name, grad):
        sem, sums, land, tok = _scatter_start(_swap_add(grad, f"swap_add_{name}"), f"scatter_start_{name}")
        in_flight[name] = (sem, sums, land)
        return tok

    tok = send_off("o_w_out", _mm_tn(cat1, dy1, 1, "o_out_dw").reshape(N_CHIPS, HALF // 2, D_MODEL))
    dcat1 = _mm_nt(dy1, o_w_out, f32, "o_out_dx", tok)
    du, dv, dcg, g_sgu_w, g_bias_t, g_sgu_g, g_sgu_b = _sgu_bwd(
        z1, dcat1, p["o_sgu_norm_g"] + tok[0, 0], p["o_sgu_norm_b"], w["o_sgu_w"], bias_t)
    dconv, ddgate, g_cn_g, g_cn_b = _conv_norm_bwd(conv, z1, dcat1, p["o_conv_norm_g"], p["o_conv_norm_b"])
    ddval, ddglu, g_conv_w, g_conv_b = _conv_bwd(z1, dconv, p["o_conv_w"])
    dz1 = jnp.concatenate([du, dv, dcg, ddval, ddglu, ddgate], axis=1)
    tok = send_off("o_w_in", _mm_tn(h1, dz1, N_CHIPS, "o_in_dw"))
    dh1 = _mm_nt(dz1, o_w_in, f32, "o_in_dx", tok)
    dx1, dy0, g_o_pre, g_e_post = _mid_norm_bwd(dx2, dh1, x1, y0, p["o_pre_norm"] + tok[0, 0], vec("e_post_norm"))

    tok = send_off("e_w_out", _mm_tn(cat0, dy0, 1, "e_out_dw").reshape(N_CHIPS, HALF // 2, D_MODEL))
    dcat0 = _mm_nt(dy0, e_w_out, f32, "e_out_dx", tok)
    da, dagate, g_pool_w, g_pool_scale = _pool_bwd(z0, dcat0, pool_w_bf, vec("e_pool_scale") + tok[0, 0])
    dq0, dk0, dv0, dbgate = _attn_bwd_group(0, qkv_by_residue[0], z0, att, lse, dcat0, tables)
    dq1, dk1, dv1 = _attn_bwd_group(1, qkv_by_residue[1], z0, att, lse, dcat0, tables)
    dq2, dk2, dv2 = _attn_bwd_group(2, qkv_by_residue[2], z0, att, lse, dcat0, tables)
    dz0 = jnp.concatenate([da, dagate, dq0, dq1, dq2, dk0, dk1, dk2, dv0, dv1, dv2, dbgate], axis=1)
    tok = send_off("e_w_in", _mm_tn(h0, dz0, N_CHIPS, "e_in_dw"))
    dh0 = _mm_nt(dz0, e_w_in, f32, "e_in_dx", tok)
    grad_x, g_e_pre = _pre_norm_bwd(dx1, dh0, x, vec("e_pre_norm") + tok[0, 0])

    small = {"e_pre_norm": g_e_pre, "e_pool_w": g_pool_w, "e_pool_scale": g_pool_scale, "e_post_norm": g_e_post,
             "o_pre_norm": g_o_pre, "o_sgu_norm_g": g_sgu_g, "o_sgu_norm_b": g_sgu_b, "o_sgu_w": g_sgu_w,
             "o_sgu_b": g_bias_t[:, 0:4].T, "o_conv_w": g_conv_w, "o_conv_b": g_conv_b,
             "o_conv_norm_g": g_cn_g, "o_conv_norm_b": g_cn_b, "o_post_norm": g_o_post}
    return loss, grad_x, in_flight, small


def _land(in_flight, name, chip, after):
    sems, sums, land = in_flight[name]
    sums, land = _scatter_wait(sems, sums, land, after, f"scatter_wait_{name}")
    return _add_landed_join(sums, land, chip.astype(jnp.int32).reshape(1), f"add_landed_{name}")


def _place():
    x, y, c = lax.axis_index("x"), lax.axis_index("y"), lax.axis_index("c")
    others = [(1 - x, y), (x, 1 - y), (1 - x, 1 - y)]
    return x, y, c, 2 * x + y, others


def _all_gather(shards, name):
    n = len(shards)

    def body(*refs):
        ins, outs = refs[:n], refs[n:2 * n]
        send_sems, recv_sems, local_sems = refs[2 * n:]
        x, y, c, me, others = _place()
        sibling = (x, y, 1 - c)

        def half(a, chip, core):
            rows = ins[a].shape[0] // 2
            return outs[a].at[chip, pl.ds(core * rows, rows), :]

        def copy(a, k, src, dst, to):
            return pltpu.make_async_remote_copy(src_ref=src, dst_ref=dst, send_sem=send_sems.at[6 * a + k],
                                                recv_sem=recv_sems.at[6 * a + k], device_id=to, device_id_type=MESH)

        local = [pltpu.make_async_copy(ins[a], outs[a].at[me], local_sems.at[a]) for a in range(n)]
        for cp in local:
            cp.start()
        sent = []
        for a in range(n):
            rows = ins[a].shape[0] // 2
            mine = ins[a].at[pl.ds(c * rows, rows), :]
            for k, (ox, oy) in enumerate(others):
                sent.append(copy(a, k, mine, half(a, me, c), (ox, oy, c)))
                sent[-1].start()
        for a in range(n):
            for k, (ox, oy) in enumerate(others):
                landed = half(a, 2 * ox + oy, c)
                copy(a, k, landed, landed, (ox, oy, c)).wait_recv()
                sent.append(copy(a, 3 + k, landed, landed, sibling))
                sent[-1].start()
        for k, (ox, oy) in enumerate(others):
            chip = 2 * ox + oy
            for a in range(n):
                theirs = half(a, chip, 1 - c)
                copy(a, 3 + k, theirs, theirs, sibling).wait_recv()
        for cp in sent:
            cp.wait_send()
        for cp in local:
            cp.wait()

    return pl.pallas_call(
        body, name=name, in_specs=[ANY] * n, out_specs=[ANY] * n,
        out_shape=[SDS((N_CHIPS,) + s.shape, s.dtype) for s in shards],
        scratch_shapes=[pltpu.SemaphoreType.DMA((6 * n,)), pltpu.SemaphoreType.DMA((6 * n,)), pltpu.SemaphoreType.DMA((n,))],
    )(*shards)


def _swap_halves(parts, name):
    n = len(parts)

    def body(*refs):
        ins, own, theirs = refs[:n], refs[n:2 * n], refs[2 * n:3 * n]
        send_sems, recv_sems, local_sems = refs[3 * n:]
        x, y, c, _, _ = _place()
        sibling = (x, y, 1 - c)
        copies = []
        for a in range(n):
            rows = ins[a].shape[1] // 2
            keep = pltpu.make_async_copy(ins[a].at[:, pl.ds(c * rows, rows), :], own[a], local_sems.at[a])
            give = pltpu.make_async_remote_copy(
                src_ref=ins[a].at[:, pl.ds((1 - c) * rows, rows), :], dst_ref=theirs[a], send_sem=send_sems.at[a],
                recv_sem=recv_sems.at[a], device_id=sibling, device_id_type=MESH)
            keep.start()
            give.start()
            copies += [keep, give]
        for cp in copies:
            cp.wait()

    half = [SDS((N_CHIPS, s.shape[1] // 2, s.shape[2]), s.dtype) for s in parts]
    out = pl.pallas_call(
        body, name=name, in_specs=[ANY] * n, out_specs=[ANY] * (2 * n), out_shape=half + half,
        scratch_shapes=[pltpu.SemaphoreType.DMA((n,)), pltpu.SemaphoreType.DMA((n,)), pltpu.SemaphoreType.DMA((n,))],
    )(*parts)
    return out[:n], out[n:]


def _scatter_chips(parts, name):
    n = len(parts)

    def body(*refs):
        ins, outs = refs[:n], refs[n:2 * n]
        send_sems, recv_sems, local_sems = refs[2 * n:]
        x, y, c, me, others = _place()

        def copy(a, k, slot_from, slot_to, chip_xy):
            return pltpu.make_async_remote_copy(
                src_ref=ins[a].at[slot_from], dst_ref=outs[a].at[slot_to], send_sem=send_sems.at[3 * a + k],
                recv_sem=recv_sems.at[3 * a + k], device_id=(chip_xy[0], chip_xy[1], c), device_id_type=MESH)

        keeps, gives = [], []
        for a in range(n):
            keeps.append(pltpu.make_async_copy(ins[a].at[me], outs[a].at[me], local_sems.at[a]))
            keeps[-1].start()
            for k, (ox, oy) in enumerate(others):
                gives.append(copy(a, k, 2 * ox + oy, me, (ox, oy)))
                gives[-1].start()
        for a in range(n):
            for k, (ox, oy) in enumerate(others):
                copy(a, k, me, 2 * ox + oy, (ox, oy)).wait_recv()
        for cp in gives:
            cp.wait_send()
        for cp in keeps:
            cp.wait()

    return pl.pallas_call(
        body, name=name, in_specs=[ANY] * n, out_specs=[ANY] * n, out_shape=[SDS(s.shape, s.dtype) for s in parts],
        scratch_shapes=[pltpu.SemaphoreType.DMA((3 * n,)), pltpu.SemaphoreType.DMA((3 * n,)), pltpu.SemaphoreType.DMA((n,))],
    )(*parts)


def _join_halves(halves, name):
    n = len(halves)

    def body(*refs):
        ins, outs = refs[:n], refs[n:2 * n]
        send_sems, recv_sems, local_sems = refs[2 * n:]
        x, y, c, _, _ = _place()

        def copy(a, core):
            rows = ins[a].shape[0]
            return pltpu.make_async_remote_copy(
                src_ref=ins[a], dst_ref=outs[a].at[pl.ds(core * rows, rows), :], send_sem=send_sems.at[a],
                recv_sem=recv_sems.at[a], device_id=(x, y, 1 - c), device_id_type=MESH)

        keeps, gives = [], []
        for a in range(n):
            rows = ins[a].shape[0]
            keeps.append(pltpu.make_async_copy(ins[a], outs[a].at[pl.ds(c * rows, rows), :], local_sems.at[a]))
            gives.append(copy(a, c))
            keeps[-1].start()
            gives[-1].start()
        for a in range(n):
            copy(a, 1 - c).wait_recv()
        for cp in gives:
            cp.wait_send()
        for cp in keeps:
            cp.wait()

    return pl.pallas_call(
        body, name=name, in_specs=[ANY] * n, out_specs=[ANY] * n,
        out_shape=[SDS((2 * s.shape[0], s.shape[1]), s.dtype) for s in halves],
        scratch_shapes=[pltpu.SemaphoreType.DMA((n,)), pltpu.SemaphoreType.DMA((n,)), pltpu.SemaphoreType.DMA((n,))],
    )(*halves)


def _add_pair(a, b, name):
    _, r, c = a.shape
    tr = 256 if r % 256 == 0 else r // 2 if r > 512 else r

    def body(a_ref, b_ref, o_ref):
        o_ref[...] = (a_ref[...].astype(f32) + b_ref[...].astype(f32)).astype(o_ref.dtype)

    spec = pl.BlockSpec((None, tr, c), lambda j, i: (j, i, 0))
    return pl.pallas_call(body, name=name, grid=(N_CHIPS, r // tr), in_specs=[spec, spec], out_specs=spec,
                          out_shape=SDS(a.shape, a.dtype))(a, b)


def _add_chips(u, name):
    _, r, c = u.shape
    tr = 256 if r % 256 == 0 else r

    def body(u_ref, o_ref):
        o_ref[...] = ((u_ref[0].astype(f32) + u_ref[1].astype(f32)) + u_ref[2].astype(f32)) + u_ref[3].astype(f32)

    return pl.pallas_call(
        body, name=name, grid=(r // tr,), in_specs=[pl.BlockSpec((N_CHIPS, tr, c), lambda i: (0, i, 0))],
        out_specs=pl.BlockSpec((tr, c), lambda i: (i, 0)), out_shape=SDS((r, c), f32))(u)


SWAP_ROWS = 256


def _swap_add(g, name):
    chips, r, c = g.shape
    half = r // 2
    nb = half // SWAP_ROWS
    steps = chips * nb

    def body(core_ref, mine_ref, theirs_ref, out_ref, landing, send_sems, recv_sems, free_sems):
        i = pl.program_id(0)
        slot = i % 2
        x, y, core, _, _ = _place()
        sibling = (x, y, 1 - core)

        @pl.when(i >= 2)
        def _():
            pl.semaphore_wait(free_sems.at[slot], 1)

        send = pltpu.make_async_remote_copy(src_ref=theirs_ref, dst_ref=landing.at[slot], send_sem=send_sems.at[slot],
                                            recv_sem=recv_sems.at[slot], device_id=sibling, device_id_type=MESH)
        send.start()
        send.wait_recv()
        out_ref[...] = (mine_ref[...].astype(f32) + landing[slot].astype(f32)).astype(out_ref.dtype)

        @pl.when(i + 2 < steps)
        def _():
            pl.semaphore_signal(free_sems.at[slot], 1, device_id=sibling, device_id_type=MESH)

        send.wait_send()

    block = (SWAP_ROWS, c)
    grid_spec = pltpu.PrefetchScalarGridSpec(
        num_scalar_prefetch=1, grid=(steps,),
        in_specs=[pl.BlockSpec(block, lambda i, core: ((2 * (i // nb) + core[0]) * nb + i % nb, 0)),
                  pl.BlockSpec(block, lambda i, core: ((2 * (i // nb) + 1 - core[0]) * nb + i % nb, 0))],
        out_specs=pl.BlockSpec(block, lambda i, core: (i, 0)),
        scratch_shapes=[pltpu.VMEM((2, SWAP_ROWS, c), g.dtype), pltpu.SemaphoreType.DMA((2,)),
                        pltpu.SemaphoreType.DMA((2,)), pltpu.SemaphoreType.REGULAR((2,))])
    core = lax.axis_index("c").astype(jnp.int32).reshape(1)
    rows = g.reshape(chips * r, c)
    out = pl.pallas_call(body, name=name, grid_spec=grid_spec, out_shape=SDS((chips * half, c), g.dtype))(core, rows, rows)
    return out.reshape(chips, half, c)


def _reduce_scatter(parts, tag):
    own, theirs = _swap_halves(parts, f"swap_halves_{tag}")
    chip_sums = [_add_pair(o, t, f"add_cores_{tag}{i}") for i, (o, t) in enumerate(zip(own, theirs))]
    gathered = _scatter_chips(chip_sums, f"scatter_chips_{tag}")
    halves = [_add_chips(u, f"add_chips_{tag}{i}") for i, u in enumerate(gathered)]
    return _join_halves(halves, f"join_halves_{tag}")


HBM = pl.BlockSpec(memory_space=pltpu.HBM)
SEM = pl.BlockSpec(memory_space=pltpu.SEMAPHORE)
EFFECT = pltpu.SideEffectType.DATAFLOW_SIDE_EFFECTING


def _in_hbm(a):
    return pltpu.with_memory_space_constraint(a, pltpu.HBM)


def _cast_into_slot(w, chip, name):
    r, c = w.shape
    nb = r // SWAP_ROWS

    def body(chip_ref, w_ref, o_ref):
        o_ref[...] = w_ref[...].astype(bf16)

    grid_spec = pltpu.PrefetchScalarGridSpec(
        num_scalar_prefetch=1, grid=(nb,),
        in_specs=[pl.BlockSpec((SWAP_ROWS, c), lambda i, chip: (i, 0))],
        out_specs=pl.BlockSpec((SWAP_ROWS, c), lambda i, chip: (chip[0] * nb + i, 0)))
    out = pl.pallas_call(body, name=name, grid_spec=grid_spec, out_shape=SDS((N_CHIPS * r, c), bf16))(chip, w)
    return out.reshape(N_CHIPS, r, c)


def _gather_start(bufs):
    n = len(bufs)

    def body(*refs):
        ins, sems, token = refs[:n], refs[n:3 * n], refs[4 * n]
        x, y, c, me, others = _place()
        for a in range(n):
            rows = ins[a].shape[1] // 2
            mine = ins[a].at[me, pl.ds(c * rows, rows), :]
            for k, (ox, oy) in enumerate(others):
                pltpu.make_async_remote_copy(src_ref=mine, dst_ref=mine, send_sem=sems[2 * a].at[k],
                                             recv_sem=sems[2 * a + 1].at[k], device_id=(ox, oy, c),
                                             device_id_type=MESH).start()
        token[...] = jnp.zeros_like(token)

    out = pl.pallas_call(
        body, name="gather_start", in_specs=[HBM] * n,
        out_shape=(*[pltpu.SemaphoreType.DMA((3,))] * (2 * n), *[pltpu.HBM(b.shape, b.dtype) for b in bufs],
                   SDS((8, 128), f32)),
        out_specs=(*[SEM] * (2 * n), *[HBM] * n, pl.BlockSpec(memory_space=pltpu.VMEM)),
        input_output_aliases={a: 2 * n + a for a in range(n)},
        compiler_params=pltpu.CompilerParams(has_side_effects=EFFECT),
    )(*[_in_hbm(b) for b in bufs])
    return [(out[2 * a], out[2 * a + 1]) for a in range(n)], list(out[2 * n:3 * n]), out[3 * n]


def _gather_wait(bufs, sems, after, name):
    n = len(bufs)

    def body(*refs):
        ins, sem_refs = refs[:n], refs[n:3 * n]
        x, y, c, me, others = _place()
        for a in range(n):
            rows = ins[a].shape[1] // 2
            mine = ins[a].at[me, pl.ds(c * rows, rows), :]
            for k, (ox, oy) in enumerate(others):
                landed = ins[a].at[2 * ox + oy, pl.ds(c * rows, rows), :]
                copy = pltpu.make_async_remote_copy(src_ref=mine, dst_ref=landed, send_sem=sem_refs[2 * a].at[k],
                                                    recv_sem=sem_refs[2 * a + 1].at[k], device_id=(ox, oy, c),
                                                    device_id_type=MESH)
                copy.wait_send()
                copy.wait_recv()

    flat_sems = [s for pair in sems for s in pair]
    out = pl.pallas_call(
        body, name=name, in_specs=[HBM] * n + [SEM] * (2 * n) + [ANY],
        out_shape=tuple(pltpu.HBM(b.shape, b.dtype) for b in bufs), out_specs=tuple([HBM] * n),
        input_output_aliases={a: a for a in range(n)},
        compiler_params=pltpu.CompilerParams(has_side_effects=EFFECT),
    )(*bufs, *flat_sems, after)
    return list(out)


def _forward_halves(bufs, name):
    n = len(bufs)
    blocks = []
    for b in bufs:
        half = b.shape[1] // 2
        tr = SWAP_ROWS if half % SWAP_ROWS == 0 else half
        blocks.append((half, tr))
    work = [(a, k, b) for a in range(n) for k in range(3) for b in range(blocks[a][0] // blocks[a][1])]

    def body(*refs):
        outs, stages = refs[n:2 * n], refs[2 * n:3 * n]
        load_sems, send_sems, recv_sems = refs[3 * n:]
        x, y, c, me, others = _place()
        sibling = (x, y, 1 - c)

        def rows(item):
            a, k, b = item
            half, tr = blocks[a]
            ox, oy = others[k]
            return outs[a].at[2 * ox + oy, pl.ds(c * half + b * tr, tr), :]

        def load(s, item):
            return pltpu.make_async_copy(rows(item), stages[item[0]].at[s], load_sems.at[s])

        def send(s, item):
            return pltpu.make_async_remote_copy(src_ref=stages[item[0]].at[s], dst_ref=rows(item), send_sem=send_sems.at[s],
                                                recv_sem=recv_sems.at[item[0]], device_id=sibling, device_id_type=MESH)

        load(0, work[0]).start()
        for t, item in enumerate(work):
            s = t % 2
            load(s, item).wait()
            send(s, item).start()
            if t + 1 < len(work):
                if t >= 1:
                    send(1 - s, work[t - 1]).wait_send()
                load(1 - s, work[t + 1]).start()
        if len(work) > 1:
            send(len(work) % 2, work[-2]).wait_send()
        send((len(work) - 1) % 2, work[-1]).wait_send()
        for a in range(n):
            theirs = outs[a].at[pl.ds(0, 3), pl.ds(0, blocks[a][0]), :]
            pltpu.make_async_remote_copy(src_ref=theirs, dst_ref=theirs, send_sem=send_sems.at[0], recv_sem=recv_sems.at[a],
                                         device_id=sibling, device_id_type=MESH).wait_recv()

    out = pl.pallas_call(
        body, name=name, in_specs=[ANY] * n, out_specs=[ANY] * n, out_shape=[SDS(b.shape, b.dtype) for b in bufs],
        input_output_aliases={a: a for a in range(n)},
        scratch_shapes=[pltpu.VMEM((2, blocks[a][1], bufs[a].shape[2]), bufs[a].dtype) for a in range(n)]
        + [pltpu.SemaphoreType.DMA((2,)), pltpu.SemaphoreType.DMA((2,)), pltpu.SemaphoreType.DMA((n,))],
    )(*bufs)
    return list(out)


def _scatter_start(chip_sums, name):
    def body(a_ref, land_ref, send_sems, recv_sems, a_thru, land_thru, token):
        x, y, c, me, others = _place()
        for k, (ox, oy) in enumerate(others):
            pltpu.make_async_remote_copy(src_ref=a_ref.at[2 * ox + oy], dst_ref=land_ref.at[me], send_sem=send_sems.at[k],
                                         recv_sem=recv_sems.at[k], device_id=(ox, oy, c), device_id_type=MESH).start()
        token[...] = jnp.zeros_like(token)

    shape = pltpu.HBM(chip_sums.shape, chip_sums.dtype)
    send, recv, a_thru, land, token = pl.pallas_call(
        body, name=name, in_specs=[HBM, HBM],
        out_shape=(pltpu.SemaphoreType.DMA((3,)), pltpu.SemaphoreType.DMA((3,)), shape, shape, SDS((8, 128), f32)),
        out_specs=(SEM, SEM, HBM, HBM, pl.BlockSpec(memory_space=pltpu.VMEM)), input_output_aliases={0: 2, 1: 3},
        compiler_params=pltpu.CompilerParams(has_side_effects=EFFECT),
    )(_in_hbm(chip_sums), _in_hbm(lax.empty(chip_sums.shape, chip_sums.dtype)))
    return (send, recv), a_thru, land, token


def _scatter_wait(sems, chip_sums, land, after, name):
    def body(a_ref, land_ref, send_sems, recv_sems, after_ref, a_out, land_out):
        x, y, c, me, others = _place()
        for k, (ox, oy) in enumerate(others):
            copy = pltpu.make_async_remote_copy(
                src_ref=a_ref.at[2 * ox + oy], dst_ref=land_ref.at[2 * ox + oy], send_sem=send_sems.at[k],
                recv_sem=recv_sems.at[k], device_id=(ox, oy, c), device_id_type=MESH)
            copy.wait_send()
            copy.wait_recv()

    shape = pltpu.HBM(chip_sums.shape, chip_sums.dtype)
    return pl.pallas_call(
        body, name=name, in_specs=[HBM, HBM, SEM, SEM, ANY], out_shape=(shape, shape), out_specs=(HBM, HBM),
        input_output_aliases={0: 0, 1: 1}, compiler_params=pltpu.CompilerParams(has_side_effects=EFFECT),
    )(chip_sums, land, sems[0], sems[1], after)


def _add_landed_join(chip_sums, land, chip, name):
    chips, rh, c = chip_sums.shape
    nb = rh // SWAP_ROWS

    def body(chip_ref, own_ref, l1_ref, l2_ref, l3_ref, out_hbm, buf, send_sems, recv_sem, local_sems):
        i = pl.program_id(0)
        slot = i % 2
        x, y, core, _, _ = _place()
        sibling = (x, y, 1 - core)

        def copies(s, step):
            rows = pl.ds(pl.multiple_of((core * nb + step) * SWAP_ROWS, SWAP_ROWS), SWAP_ROWS)
            keep = pltpu.make_async_copy(buf.at[s], out_hbm.at[rows, :], local_sems.at[s])
            give = pltpu.make_async_remote_copy(src_ref=buf.at[s], dst_ref=out_hbm.at[rows, :], send_sem=send_sems.at[s],
                                                recv_sem=recv_sem.at[0], device_id=sibling, device_id_type=MESH)
            return keep, give

        def drain(s, step):
            keep, give = copies(s, step)
            keep.wait()
            give.wait_send()

        @pl.when(i >= 2)
        def _():
            drain(slot, i - 2)

        buf[slot] = ((own_ref[...].astype(f32) + l1_ref[...].astype(f32)) + l2_ref[...].astype(f32)) + l3_ref[...].astype(f32)
        keep, give = copies(slot, i)
        keep.start()
        give.start()

        @pl.when(i == nb - 1)
        def _():
            drain(slot, i)
            if nb > 1:
                drain(1 - slot, i - 1)
            theirs = out_hbm.at[pl.ds((1 - core) * rh, rh), :]
            pltpu.make_async_remote_copy(src_ref=theirs, dst_ref=theirs, send_sem=send_sems.at[0], recv_sem=recv_sem.at[0],
                                         device_id=sibling, device_id_type=MESH).wait_recv()

    block = (SWAP_ROWS, c)
    from_slot = lambda d: pl.BlockSpec(block, lambda i, chip: (((chip[0] + d) % chips) * nb + i, 0))
    grid_spec = pltpu.PrefetchScalarGridSpec(
        num_scalar_prefetch=1, grid=(nb,), in_specs=[from_slot(0), from_slot(1), from_slot(2), from_slot(3)],
        out_specs=ANY,
        scratch_shapes=[pltpu.VMEM((2, SWAP_ROWS, c), f32), pltpu.SemaphoreType.DMA((2,)),
                        pltpu.SemaphoreType.DMA((1,)), pltpu.SemaphoreType.DMA((2,))])
    land_rows = land.reshape(chips * rh, c)
    return pl.pallas_call(body, name=name, grid_spec=grid_spec, out_shape=SDS((2 * rh, c), f32))(
        chip, chip_sums.reshape(chips * rh, c), land_rows, land_rows, land_rows)


def _adamw(w, g, m, v, name):
    r, c = w.shape
    tr = 128 if r % 128 == 0 else r
    b1c = 1.0 - ADAM_B1 ** ADAM_STEP
    b2c = 1.0 - ADAM_B2 ** ADAM_STEP

    def body(w_ref, g_ref, m_ref, v_ref, d_ref, nm_ref, nv_ref):
        g = g_ref[...]
        nm = ADAM_B1 * m_ref[...] + (1.0 - ADAM_B1) * g
        nv = ADAM_B2 * v_ref[...] + (1.0 - ADAM_B2) * (g * g)
        nm_ref[...] = nm
        nv_ref[...] = nv
        d_ref[...] = -ADAM_LR * ((nm / b1c) / (jnp.sqrt(nv / b2c) + ADAM_EPS) + ADAM_WD * w_ref[...])

    spec = pl.BlockSpec((tr, c), lambda i: (i, 0))
    return pl.pallas_call(body, name=name, grid=(r // tr,), in_specs=[spec] * 4, out_specs=[spec] * 3,
                          out_shape=[SDS((r, c), f32)] * 3)(w, g, m, v)


def _pack(arrays, total_rows=None):
    parts = []
    rows = 0
    for a in arrays:
        flat = a.reshape(-1, LANES)
        pad = -flat.shape[0] % 8
        parts.append(jnp.pad(flat, ((0, pad), (0, 0))))
        rows += flat.shape[0] + pad
    if total_rows is not None:
        parts.append(jnp.zeros((total_rows - rows, LANES), arrays[0].dtype))
    return jnp.concatenate(parts, axis=0)


def _unpack(buf, shapes):
    out = []
    row = 0
    lead = buf.shape[:-2]
    for shape in shapes:
        size = 1
        for s in shape:
            size *= s
        rows = size // LANES
        out.append(buf[..., row:row + rows, :].reshape(lead + tuple(shape)))
        row += rows + (-rows % 8)
    return out


BIG = ("e_w_in", "e_w_out", "o_w_in", "o_w_out")
SHARDED_SMALL = {
    "e_pool_w": ((4, 64, 256), 1), "o_pre_norm": ((512,), 0), "o_sgu_norm_g": ((256,), 0), "o_sgu_norm_b": ((256,), 0),
    "o_conv_w": ((31, 256), 1), "o_conv_b": ((256,), 0), "o_conv_norm_g": ((256,), 0), "o_conv_norm_b": ((256,), 0),
    "o_post_norm": ((512,), 0),
}
REPLICATED_SMALL = {"e_pre_norm": (2048,), "e_pool_scale": (1024,), "e_post_norm": (2048,),
                    "o_sgu_w": (4, 128, 128), "o_sgu_b": (4, 128)}
SMALL_ORDER = ("e_pre_norm", "e_pool_w", "e_pool_scale", "e_post_norm", "o_pre_norm", "o_sgu_norm_g", "o_sgu_norm_b",
               "o_sgu_w", "o_sgu_b", "o_conv_w", "o_conv_b", "o_conv_norm_g", "o_conv_norm_b", "o_post_norm")
ALL_ORDER = ("e_pre_norm", "e_w_in", "e_pool_w", "e_pool_scale", "e_w_out", "e_post_norm", "o_pre_norm", "o_w_in",
             "o_sgu_norm_g", "o_sgu_norm_b", "o_sgu_w", "o_sgu_b", "o_conv_w", "o_conv_b", "o_conv_norm_g",
             "o_conv_norm_b", "o_w_out", "o_post_norm")


def _full_shape(name):
    shape, axis = SHARDED_SMALL[name]
    return tuple(s * N_CHIPS if i == axis else s for i, s in enumerate(shape))


def _from_chips(name, stacked):
    shape, axis = SHARDED_SMALL[name]
    return jnp.moveaxis(stacked, 0, axis).reshape(_full_shape(name))


def _my_shard(name, full, chip):
    shape, axis = SHARDED_SMALL[name]
    return lax.dynamic_slice_in_dim(full, chip * shape[axis], shape[axis], axis)


def kernel(x, e_pre_norm, e_w_in, e_pool_w, e_pool_scale, e_w_out, e_post_norm, o_pre_norm, o_w_in, o_sgu_norm_g, o_sgu_norm_b, o_sgu_w, o_sgu_b, o_conv_w, o_conv_b, o_conv_norm_g, o_conv_norm_b, o_w_out, o_post_norm, loss_target, m_e_pre_norm, m_e_w_in, m_e_pool_w, m_e_pool_scale, m_e_w_out, m_e_post_norm, m_o_pre_norm, m_o_w_in, m_o_sgu_norm_g, m_o_sgu_norm_b, m_o_sgu_w, m_o_sgu_b, m_o_conv_w, m_o_conv_b, m_o_conv_norm_g, m_o_conv_norm_b, m_o_w_out, m_o_post_norm, v_e_pre_norm, v_e_w_in, v_e_pool_w, v_e_pool_scale, v_e_w_out, v_e_post_norm, v_o_pre_norm, v_o_w_in, v_o_sgu_norm_g, v_o_sgu_norm_b, v_o_sgu_w, v_o_sgu_b, v_o_conv_w, v_o_conv_b, v_o_conv_norm_g, v_o_conv_norm_b, v_o_w_out, v_o_post_norm):
    w = dict(e_pre_norm=e_pre_norm, e_w_in=e_w_in, e_pool_w=e_pool_w, e_pool_scale=e_pool_scale, e_w_out=e_w_out,
             e_post_norm=e_post_norm, o_pre_norm=o_pre_norm, o_w_in=o_w_in, o_sgu_norm_g=o_sgu_norm_g,
             o_sgu_norm_b=o_sgu_norm_b, o_sgu_w=o_sgu_w, o_sgu_b=o_sgu_b, o_conv_w=o_conv_w, o_conv_b=o_conv_b,
             o_conv_norm_g=o_conv_norm_g, o_conv_norm_b=o_conv_norm_b, o_w_out=o_w_out, o_post_norm=o_post_norm)
    m = dict(e_pre_norm=m_e_pre_norm, e_w_in=m_e_w_in, e_pool_w=m_e_pool_w, e_pool_scale=m_e_pool_scale,
             e_w_out=m_e_w_out, e_post_norm=m_e_post_norm, o_pre_norm=m_o_pre_norm, o_w_in=m_o_w_in,
             o_sgu_norm_g=m_o_sgu_norm_g, o_sgu_norm_b=m_o_sgu_norm_b, o_sgu_w=m_o_sgu_w, o_sgu_b=m_o_sgu_b,
             o_conv_w=m_o_conv_w, o_conv_b=m_o_conv_b, o_conv_norm_g=m_o_conv_norm_g, o_conv_norm_b=m_o_conv_norm_b,
             o_w_out=m_o_w_out, o_post_norm=m_o_post_norm)
    v = dict(e_pre_norm=v_e_pre_norm, e_w_in=v_e_w_in, e_pool_w=v_e_pool_w, e_pool_scale=v_e_pool_scale,
             e_w_out=v_e_w_out, e_post_norm=v_e_post_norm, o_pre_norm=v_o_pre_norm, o_w_in=v_o_w_in,
             o_sgu_norm_g=v_o_sgu_norm_g, o_sgu_norm_b=v_o_sgu_norm_b, o_sgu_w=v_o_sgu_w, o_sgu_b=v_o_sgu_b,
             o_conv_w=v_o_conv_w, o_conv_b=v_o_conv_b, o_conv_norm_g=v_o_conv_norm_g, o_conv_norm_b=v_o_conv_norm_b,
             o_w_out=v_o_w_out, o_post_norm=v_o_post_norm)
    w, m, v = ({k: a[0] for k, a in d.items()} for d in (w, m, v))
    chip = 2 * lax.axis_index("x") + lax.axis_index("y")

    loss, grad_x, in_flight, small = _step(x[0], loss_target[0], w, chip)

    grads, delta, new_m, new_v = {}, {}, {}, {}
    after = grad_x
    for k in ("o_w_out", "o_w_in", "e_w_out", "e_w_in"):
        grads[k] = _land(in_flight, k, chip, after)
        delta[k], new_m[k], new_v[k] = _adamw(w[k], grads[k], m[k], v[k], f"adamw_{k}")
        after = delta[k]

    small_full_shapes = {k: (_full_shape(k) if k in SHARDED_SMALL else REPLICATED_SMALL[k]) for k in SMALL_ORDER}
    small_parts = _pack([small[k].reshape(small_full_shapes[k]) for k in SMALL_ORDER], total_rows=SMALL_GRAD_ROWS)
    small_parts = small_parts + 0.0 * after[0, 0]
    reduced = _reduce_scatter([small_parts.reshape(N_CHIPS, SMALL_GRAD_ROWS // N_CHIPS, LANES)], "small")
    small_sum = _all_gather(reduced, "gather_small_grads")[0].reshape(SMALL_GRAD_ROWS, LANES)
    for k, a in zip(SMALL_ORDER, _unpack(small_sum, [small_full_shapes[k] for k in SMALL_ORDER])):
        grads[k] = _my_shard(k, a, chip) if k in SHARDED_SMALL else a
    loss = lax.psum(loss[0, 0], ("x", "y", "c"))
    local_shapes = [w[k].shape for k in SMALL_ORDER]
    packed = [_pack([d[k] for k in SMALL_ORDER]) for d in (w, grads, m, v)]
    for d, buf in zip((delta, new_m, new_v), _adamw(*packed, "adamw_small")):
        for k, a in zip(SMALL_ORDER, _unpack(buf, local_shapes)):
            d[k] = a

    outs = [loss, grad_x[None]]
    for d in (grads, delta, new_m, new_v):
        outs += [d[k][None] for k in ALL_ORDER]
    return tuple(outs)
```

```python
import functools

import jax
import jax.numpy as jnp
from jax import lax
from jax.experimental import pallas as pl
from jax.experimental.pallas import tpu as pltpu

f32 = jnp.float32
bf16 = jnp.bfloat16
SDS = jax.ShapeDtypeStruct

SEQ = 2048
D_MODEL = 2048
EPS = 1e-6
NEG = -1e30
HEAD_DIM = 128
ROT_HALF = 16
ROPE_THETA = 500000.0
DILATIONS = (1, 4, 16)
SPAN = 128
N_HEADS = 8
HALF = 1024
POOL_CH = 256
CONV_K = 31
CONV_PAD = 32
CHUNK = 128
N_CHIPS = 4
LANES = 256
SMALL_SHARD_ROWS = 352
SMALL_GRAD_ROWS = 1536
ANY = pl.BlockSpec(memory_space=pl.ANY)
MESH = pl.DeviceIdType.MESH

ADAM_LR = 0.001
ADAM_B1 = 0.9
ADAM_B2 = 0.999
ADAM_EPS = 1e-08
ADAM_WD = 0.01
ADAM_STEP = 10


def _dot(a, b):
    return jnp.dot(a, b, preferred_element_type=f32)


def _dot_nt(a, b):
    return lax.dot_general(a, b, (((1,), (1,)), ((), ())), preferred_element_type=f32)


def _dot_tn(a, b):
    return lax.dot_general(a, b, (((0,), (0,)), ((), ())), preferred_element_type=f32)


def _sigmoid(x):
    return 1.0 / (1.0 + jnp.exp(-x))


def _silu_and_grad(x):
    s = _sigmoid(x)
    return x * s, s * (1.0 + x * (1.0 - s))


def _rms_fwd(x, g):
    r = lax.rsqrt(jnp.mean(x * x, axis=-1, keepdims=True) + EPS)
    return x * r * g


def _rms_bwd(x, g, dout):
    r = lax.rsqrt(jnp.mean(x * x, axis=-1, keepdims=True) + EPS)
    xh = x * r
    dg = jnp.sum(dout * xh, axis=0, keepdims=True)
    dxh = dout * g
    dx = r * (dxh - xh * jnp.mean(dxh * xh, axis=-1, keepdims=True))
    return dx, dg


def _ln_stats(x):
    mu = jnp.mean(x, axis=-1, keepdims=True)
    xc = x - mu
    rstd = lax.rsqrt(jnp.mean(xc * xc, axis=-1, keepdims=True) + EPS)
    return xc * rstd, rstd


def _ln_bwd(xh, rstd, g, dout):
    dg = jnp.sum(dout * xh, axis=0, keepdims=True)
    db = jnp.sum(dout, axis=0, keepdims=True)
    dxh = dout * g
    dx = rstd * (dxh - jnp.mean(dxh, axis=-1, keepdims=True) - xh * jnp.mean(dxh * xh, axis=-1, keepdims=True))
    return dx, dg, db


def _accumulate(ref, value, first):
    @pl.when(first)
    def _():
        ref[...] = value

    @pl.when(jnp.logical_not(first))
    def _():
        ref[...] += value


def _col_tile(ns):
    for t in (1024, 768, 512, 256):
        if ns % t == 0:
            return t
    raise ValueError(ns)


def _mm_nn(a, w, out_dtype, name):
    m, k = a.shape
    j, _, ns = w.shape
    tm, tn = 1024, _col_tile(ns)
    nb = ns // tn

    def body(a_ref, w_ref, o_ref):
        o_ref[...] = _dot(a_ref[...], w_ref[...]).astype(o_ref.dtype)

    return pl.pallas_call(
        body, name=name, grid=(j * nb, m // tm),
        in_specs=[pl.BlockSpec((tm, k), lambda n, i: (i, 0)),
                  pl.BlockSpec((None, k, tn), lambda n, i: (n // nb, 0, n % nb))],
        out_specs=pl.BlockSpec((tm, tn), lambda n, i: (i, n)),
        out_shape=SDS((m, j * ns), out_dtype),
    )(a, w)


def _mm_nt(dz, w, name, after):
    m, _ = dz.shape
    j, k, ns = w.shape
    tm, tk = 1024, 1024
    tn = next(t for t in (2048, 1536, 1024, 768, 512) if ns % t == 0)
    nb = ns // tn

    def body(dz_ref, w_ref, after_ref, o_ref):
        _accumulate(o_ref, _dot_nt(dz_ref[...], w_ref[...]), pl.program_id(2) == 0)

    return pl.pallas_call(
        body, name=name, grid=(m // tm, k // tk, j * nb),
        in_specs=[pl.BlockSpec((tm, tn), lambda i, kk, r: (i, r)),
                  pl.BlockSpec((None, tk, tn), lambda i, kk, r: (r // nb, kk, r % nb)), ANY],
        out_specs=pl.BlockSpec((tm, tk), lambda i, kk, r: (i, kk)),
        out_shape=SDS((m, k), f32),
    )(dz, w, after)


def _mm_tn(a, dz, j, name):
    m, k = a.shape
    ns = dz.shape[1] // j
    tk, tn = 1024, _col_tile(ns)
    nb = ns // tn

    def body(a_ref, dz_ref, o_ref):
        o_ref[...] = _dot_tn(a_ref[...], dz_ref[...]).astype(o_ref.dtype)

    return pl.pallas_call(
        body, name=name, grid=(k // tk, j * nb),
        in_specs=[pl.BlockSpec((m, tk), lambda kk, n: (0, kk)),
                  pl.BlockSpec((m, tn), lambda kk, n: (0, n))],
        out_specs=pl.BlockSpec((None, tk, tn), lambda kk, n: (n // nb, kk, n % nb)),
        out_shape=SDS((j, k, ns), bf16),
    )(a, dz)


ROWS = 256


def _row_spec(width=D_MODEL, col=0):
    return pl.BlockSpec((ROWS, width), lambda i: (i, col))


def _vec_spec(width=D_MODEL):
    return pl.BlockSpec((1, width), lambda i: (0, 0))


def _pre_norm(x, g):
    def body(x_ref, g_ref, h_ref):
        h_ref[...] = _rms_fwd(x_ref[...], g_ref[...]).astype(bf16)

    return pl.pallas_call(
        body, name="pre_norm", grid=(SEQ // ROWS,), in_specs=[_row_spec(), _vec_spec()],
        out_specs=_row_spec(), out_shape=SDS((SEQ, D_MODEL), bf16))(x, g)


def _mid_norm(x, y, g_post, g_pre):
    def body(x_ref, y_ref, gpost_ref, gpre_ref, x1_ref, h1_ref):
        x1 = x_ref[...] + _rms_fwd(y_ref[...], gpost_ref[...])
        x1_ref[...] = x1
        h1_ref[...] = _rms_fwd(x1, gpre_ref[...]).astype(bf16)

    return pl.pallas_call(
        body, name="mid_norm", grid=(SEQ // ROWS,),
        in_specs=[_row_spec(), _row_spec(), _vec_spec(), _vec_spec()],
        out_specs=[_row_spec(), _row_spec()],
        out_shape=[SDS((SEQ, D_MODEL), f32), SDS((SEQ, D_MODEL), bf16)])(x, y, g_post, g_pre)


def _final_norm_loss(x1, y, g_post, target):
    def body(x1_ref, y_ref, g_ref, t_ref, loss_ref, dx2_ref, dy_ref, dg_ref):
        first = pl.program_id(0) == 0
        y = y_ref[...]
        g = g_ref[...]
        err = x1_ref[...] + _rms_fwd(y, g) - t_ref[...]
        sq = jnp.sum(jnp.sum(err * err, axis=1, keepdims=True), axis=0, keepdims=True)
        _accumulate(loss_ref, sq * (0.5 / D_MODEL), first)
        dx2 = err * (1.0 / D_MODEL)
        dx2_ref[...] = dx2
        dy, dg = _rms_bwd(y, g, dx2)
        dy_ref[...] = dy.astype(bf16)
        _accumulate(dg_ref, dg, first)

    return pl.pallas_call(
        body, name="final_norm_loss", grid=(SEQ // ROWS,),
        in_specs=[_row_spec(), _row_spec(), _vec_spec(), _row_spec()],
        out_specs=[pl.BlockSpec((1, 1), lambda i: (0, 0)), _row_spec(), _row_spec(), _vec_spec()],
        out_shape=[SDS((1, 1), f32), SDS((SEQ, D_MODEL), f32), SDS((SEQ, D_MODEL), bf16), SDS((1, D_MODEL), f32)],
    )(x1, y, g_post, target)


def _mid_norm_bwd(dx2, dh1, x1, y0, g_pre, g_post):
    def body(dx2_ref, dh1_ref, x1_ref, y0_ref, gpre_ref, gpost_ref, dx1_ref, dy0_ref, dgpre_ref, dgpost_ref):
        first = pl.program_id(0) == 0
        d_in, dgpre = _rms_bwd(x1_ref[...], gpre_ref[...], dh1_ref[...])
        dx1 = dx2_ref[...] + d_in
        dx1_ref[...] = dx1
        dy0, dgpost = _rms_bwd(y0_ref[...], gpost_ref[...], dx1)
        dy0_ref[...] = dy0.astype(bf16)
        _accumulate(dgpre_ref, dgpre, first)
        _accumulate(dgpost_ref, dgpost, first)

    return pl.pallas_call(
        body, name="mid_norm_bwd", grid=(SEQ // ROWS,),
        in_specs=[_row_spec(), _row_spec(), _row_spec(), _row_spec(), _vec_spec(), _vec_spec()],
        out_specs=[_row_spec(), _row_spec(), _vec_spec(), _vec_spec()],
        out_shape=[SDS((SEQ, D_MODEL), f32), SDS((SEQ, D_MODEL), bf16), SDS((1, D_MODEL), f32), SDS((1, D_MODEL), f32)],
    )(dx2, dh1, x1, y0, g_pre, g_post)


def _pre_norm_bwd(dx1, dh0, x, g):
    def body(dx1_ref, dh0_ref, x_ref, g_ref, dx_ref, dg_ref):
        d_in, dg = _rms_bwd(x_ref[...], g_ref[...], dh0_ref[...])
        dx_ref[...] = dx1_ref[...] + d_in
        _accumulate(dg_ref, dg, pl.program_id(0) == 0)

    return pl.pallas_call(
        body, name="pre_norm_bwd", grid=(SEQ // ROWS,),
        in_specs=[_row_spec(), _row_spec(), _row_spec(), _vec_spec()],
        out_specs=[_row_spec(), _vec_spec()],
        out_shape=[SDS((SEQ, D_MODEL), f32), SDS((1, D_MODEL), f32)])(dx1, dh0, x, g)


def _pool_count(g):
    row = lax.broadcasted_iota(jnp.int32, (SEQ, 1), 0)
    width = jnp.left_shift(2, g)
    return row, width, jnp.minimum(row + 1, width).astype(f32)


def _trailing_sum(x, row, width):
    s = x
    for k in (1, 2, 4, 8):
        shifted = jnp.where(row >= k, pltpu.roll(s, k, 0), 0.0)
        s = jnp.where(width > k, s + shifted, s)
    return s


def _leading_sum(x, row, width):
    s = x
    for k in (1, 2, 4, 8):
        shifted = jnp.where(row < SEQ - k, pltpu.roll(s, SEQ - k, 0), 0.0)
        s = jnp.where(width > k, s + shifted, s)
    return s


def _pool_specs():
    a_in = pl.BlockSpec((SEQ, POOL_CH), lambda g: (0, g))
    a_gate = pl.BlockSpec((SEQ, POOL_CH), lambda g: (0, 4 + g))
    w = pl.BlockSpec((None, POOL_CH, POOL_CH), lambda g: (g, 0, 0))
    scale = pl.BlockSpec((1, POOL_CH), lambda g: (0, g))
    return a_in, a_gate, w, scale


def _pool_fwd(z0, pool_w, pool_scale):
    def body(a_ref, gate_ref, w_ref, scale_ref, ya_ref):
        row, width, count = _pool_count(pl.program_id(0))
        a = a_ref[...]
        pooled = _trailing_sum(a, row, width) / count - a
        mixed = _dot(pooled.astype(bf16), w_ref[...]) * scale_ref[...]
        gate = gate_ref[...]
        ya_ref[...] = (mixed * gate * _sigmoid(gate)).astype(bf16)

    return pl.pallas_call(
        body, name="pool_fwd", grid=(4,), in_specs=list(_pool_specs()),
        out_specs=pl.BlockSpec((SEQ, POOL_CH), lambda g: (0, g)),
        out_shape=SDS((SEQ, HALF), bf16))(z0, z0, pool_w, pool_scale)


def _pool_bwd(z0, dcat, pool_w, pool_scale):
    def body(a_ref, gate_ref, w_ref, scale_ref, dya_ref, da_ref, dgate_ref, dw_ref, dscale_ref):
        row, width, count = _pool_count(pl.program_id(0))
        a = a_ref[...]
        pooled = (_trailing_sum(a, row, width) / count - a).astype(bf16)
        w = w_ref[...]
        scale = scale_ref[...]
        mixed = _dot(pooled, w)
        silu, dsilu = _silu_and_grad(gate_ref[...])
        dya = dya_ref[...]
        dgate_ref[...] = (dya * mixed * scale * dsilu).astype(bf16)
        dms = dya * silu
        dscale_ref[...] = jnp.sum(dms * mixed, axis=0, keepdims=True)
        dmixed = (dms * scale).astype(bf16)
        dw_ref[...] = _dot_tn(pooled, dmixed)
        dpooled = _dot_nt(dmixed, w)
        da_ref[...] = (_leading_sum(dpooled / count, row, width) - dpooled).astype(bf16)

    a_in, a_gate, w, scale = _pool_specs()
    col = pl.BlockSpec((SEQ, POOL_CH), lambda g: (0, g))
    return pl.pallas_call(
        body, name="pool_bwd", grid=(4,), in_specs=[a_in, a_gate, w, scale, col],
        out_specs=[col, col, w, scale],
        out_shape=[SDS((SEQ, HALF), bf16), SDS((SEQ, HALF), bf16), SDS((4, POOL_CH, POOL_CH), f32), SDS((1, HALF), f32)],
    )(z0, z0, pool_w, pool_scale, dcat)


Q_COL, K_COL, V_COL, BGATE_COL = 16, 40, 64, 88


def _rope_tables():
    pos = jnp.arange(SEQ, dtype=f32)
    inv_freq = jnp.power(ROPE_THETA, -jnp.arange(0, 2 * ROT_HALF, 2, dtype=f32) / (2 * ROT_HALF))
    ang = pos[:, None] * inv_freq[None, :]
    cos, sin = jnp.cos(ang), jnp.sin(ang)
    zeros = jnp.zeros((SEQ, HEAD_DIM - 2 * ROT_HALF), f32)
    cos_t = jnp.concatenate([cos, cos, zeros + 1.0], axis=1)
    sin_t = jnp.concatenate([sin, sin, zeros], axis=1)
    j = jnp.arange(HEAD_DIM)[:, None]
    i = jnp.arange(HEAD_DIM)[None, :]
    rot = jnp.where((i < ROT_HALF) & (j == i + ROT_HALF), -1.0, 0.0) + jnp.where(
        (i >= ROT_HALF) & (i < 2 * ROT_HALF) & (j == i - ROT_HALF), 1.0, 0.0)
    return cos_t, sin_t, rot.astype(bf16), rot.T.astype(bf16)


def _exact_dot(t, m):
    hi = t.astype(bf16)
    lo = (t - hi.astype(f32)).astype(bf16)
    return _dot(hi, m) + _dot(lo, m)


def _rope(t, cos_t, sin_t, rot):
    return t * cos_t + _exact_dot(t, rot) * sin_t


def _rope_transposed(d, cos_t, sin_t, rot_t):
    return d * cos_t + _exact_dot(d * sin_t, rot_t)


ROW_CHUNK = 256


def _chunks(fn):
    def step(i, carry):
        fn(pl.multiple_of(i * ROW_CHUNK, ROW_CHUNK))
        return carry

    lax.fori_loop(0, SEQ // ROW_CHUNK, step, 0)


def _pieces(dilation):
    length = SEQ // dilation
    n = min(length, ROW_CHUNK)
    return [(r, l0, n) for r in range(dilation) for l0 in range(0, length, n)]


def _by_residue(dst_ref, src_ref, dilation, dtype):
    length = SEQ // dilation
    for r, l0, n in _pieces(dilation):
        src = src_ref[l0:l0 + n, :] if dilation == 1 else src_ref[pl.ds(r + dilation * l0, n, stride=dilation), :]
        start = r * length + l0
        dst_ref[start:start + n, :] = src.astype(dtype)


def _by_position(dst_ref, src_ref, dilation):
    length = SEQ // dilation
    for r, l0, n in _pieces(dilation):
        src = src_ref[r * length + l0:r * length + l0 + n, :]
        if dilation == 1:
            dst_ref[l0:l0 + n, :] = src
        else:
            dst_ref[pl.ds(r + dilation * l0, n, stride=dilation), :] = src


def _attn_masks():
    qi = lax.broadcasted_iota(jnp.int32, (SPAN, 2 * SPAN), 0)
    kj = lax.broadcasted_iota(jnp.int32, (SPAN, 2 * SPAN), 1)
    window = ((kj < SPAN) & (kj >= qi)) | ((kj >= SPAN) & (kj - SPAN <= qi))
    own = lax.broadcasted_iota(jnp.int32, (SPAN, SPAN), 1) <= lax.broadcasted_iota(jnp.int32, (SPAN, SPAN), 0)
    return window, own


def _attn_blocks(dilation):
    per_residue = SEQ // dilation // SPAN
    blocks = [(c, c % per_residue != 0) for c in range(SEQ // SPAN)]
    return [blocks[i:i + 4] for i in range(0, len(blocks), 4)]


def _block_keys(c, has_prev):
    return slice((c - 1) * SPAN if has_prev else c * SPAN, (c + 1) * SPAN)


def _head_spec(col):
    return pl.BlockSpec((SEQ, HEAD_DIM), lambda h: (0, col + h))


def _table_spec():
    return pl.BlockSpec((SEQ, HEAD_DIM), lambda h: (0, 0))


def _attn_fwd(z0, tables):
    scale = HEAD_DIM ** -0.5

    def body(*refs):
        qkv = refs[0:9]
        bg_ref, cos_ref, sin_ref, rot_ref = refs[9:13]
        yb_ref, att_ref, lse_ref = refs[13:16]
        saved = refs[16:25]
        tmp_q, tmp_k, v_ones, o_res, l_res, o_nat, l_nat = refs[25:32]
        window_mask, own_mask = _attn_masks()
        rot = rot_ref[...]

        @pl.when(pl.program_id(0) == 0)
        def _():
            v_ones[:, HEAD_DIM:] = jnp.ones((SEQ, HEAD_DIM), bf16)

        for g, dilation in enumerate(DILATIONS):
            q_ref, k_ref, v_ref = qkv[3 * g:3 * g + 3]
            qd, kd, vd = saved[3 * g:3 * g + 3]

            def rope_rows(start, q_ref=q_ref, k_ref=k_ref):
                r = pl.ds(start, ROW_CHUNK)
                cos_t, sin_t = cos_ref[r, :], sin_ref[r, :]
                tmp_q[r, :] = _rope(q_ref[r, :], cos_t, sin_t, rot) * scale
                tmp_k[r, :] = _rope(k_ref[r, :], cos_t, sin_t, rot)

            _chunks(rope_rows)
            _by_residue(qd, tmp_q, dilation, bf16)
            _by_residue(kd, tmp_k, dilation, bf16)
            _by_residue(vd, v_ref, dilation, bf16)
            for l0 in range(0, SEQ, ROW_CHUNK):
                v_ones[l0:l0 + ROW_CHUNK, 0:HEAD_DIM] = vd[l0:l0 + ROW_CHUNK, :]

            for four in _attn_blocks(dilation):
                scores = [_dot_nt(qd[c * SPAN:(c + 1) * SPAN, :], kd[_block_keys(c, prev), :]) for c, prev in four]
                tops, probs = [], []
                for (c, prev), s in zip(four, scores):
                    s = jnp.where(window_mask if prev else own_mask, s, NEG)
                    tops.append(jnp.max(s, axis=1, keepdims=True))
                    probs.append(jnp.exp(s - tops[-1]).astype(bf16))
                sums = [_dot(p, v_ones[_block_keys(c, prev), :]) for (c, prev), p in zip(four, probs)]
                for (c, prev), m, o in zip(four, tops, sums):
                    den = o[:, HEAD_DIM:]
                    o_res[c * SPAN:(c + 1) * SPAN, :] = o[:, :HEAD_DIM] / den
                    l_res[c * SPAN:(c + 1) * SPAN, :] = m + jnp.log(den)

            if dilation > 1:
                _by_position(o_nat, o_res, dilation)
                _by_position(l_nat, l_res, dilation)
            o_g, l_g = (o_res, l_res) if dilation == 1 else (o_nat, l_nat)

            def merge(start, g=g, o_g=o_g, l_g=l_g):
                r = pl.ds(start, ROW_CHUNK)
                if g == 0:
                    att, total = o_g[r, :], l_g[r, :]
                else:
                    l_old, l_new = lse_ref[r, :], l_g[r, :]
                    top = jnp.maximum(l_old, l_new)
                    total = top + jnp.log(jnp.exp(l_old - top) + jnp.exp(l_new - top))
                    att = att_ref[r, :] * jnp.exp(l_old - total) + o_g[r, :] * jnp.exp(l_new - total)
                att_ref[r, :] = att
                lse_ref[r, :] = total
                if g == len(DILATIONS) - 1:
                    gate = bg_ref[r, :]
                    yb_ref[r, :] = (att * gate * _sigmoid(gate)).astype(bf16)

            _chunks(merge)

    in_specs = []
    for g in range(3):
        in_specs += [_head_spec(Q_COL + 8 * g), _head_spec(K_COL + 8 * g), _head_spec(V_COL + 8 * g)]
    in_specs += [_head_spec(BGATE_COL), _table_spec(), _table_spec(), pl.BlockSpec((HEAD_DIM, HEAD_DIM), lambda h: (0, 0))]
    out_spec = pl.BlockSpec((SEQ, HEAD_DIM), lambda h: (0, h))
    vm = lambda dt: pltpu.VMEM((SEQ, HEAD_DIM), dt)
    cos_t, sin_t, rot, _ = tables
    out = pl.pallas_call(
        body, name="attn_fwd", grid=(N_HEADS,), in_specs=in_specs, out_specs=[out_spec] * 12,
        out_shape=[SDS((SEQ, HALF), bf16), SDS((SEQ, HALF), f32), SDS((SEQ, HALF), f32)] + [SDS((SEQ, HALF), bf16)] * 9,
        scratch_shapes=[vm(f32), vm(f32), pltpu.VMEM((SEQ, 2 * HEAD_DIM), bf16), vm(f32), vm(f32), vm(f32), vm(f32)],
    )(*([z0] * 10), cos_t, sin_t, rot)
    return out[0], out[1], out[2], [tuple(out[3 + 3 * g:6 + 3 * g]) for g in range(3)]


def _attn_bwd_group(g, saved, z0, att, lse, dcat, tables):
    scale = HEAD_DIM ** -0.5
    dilation = DILATIONS[g]
    with_gate = g == 0

    def body(*refs):
        qd, kd, vd, bg_ref, att_ref, lse_ref, dyb_ref, cos_ref, sin_ref, rot_t_ref = refs[0:10]
        n_out = 4 if with_gate else 3
        dq_ref, dk_ref, dv_ref = refs[10:13]
        dod, ld, dd, tmp, aq, ak, av = refs[10 + n_out:17 + n_out]
        window_mask, own_mask = _attn_masks()
        rot_t = rot_t_ref[...]

        def gate_rows(start):
            r = pl.ds(start, ROW_CHUNK)
            silu, dsilu = _silu_and_grad(bg_ref[r, :])
            att_v = att_ref[r, :]
            dyb = dyb_ref[r, :]
            if with_gate:
                refs[13][r, :] = (dyb * att_v * dsilu).astype(bf16)
            datt = dyb * silu
            tmp[r, :] = datt
            aq[r, :] = jnp.broadcast_to(jnp.sum(datt * att_v, axis=1, keepdims=True), (ROW_CHUNK, HEAD_DIM))

        _chunks(gate_rows)
        _by_residue(dod, tmp, dilation, bf16)
        _by_residue(dd, aq, dilation, f32)
        _by_residue(ld, lse_ref, dilation, f32)

        for four in _attn_blocks(dilation):
            rows = [slice(c * SPAN, (c + 1) * SPAN) for c, _ in four]
            keys = [_block_keys(c, prev) for c, prev in four]
            scores = [_dot_nt(qd[r, :], kd[k, :]) for r, k in zip(rows, keys)]
            dprobs = [_dot_nt(dod[r, :], vd[k, :]) for r, k in zip(rows, keys)]
            probs, dscores = [], []
            for (c, prev), r, s, dp in zip(four, rows, scores, dprobs):
                lse_q, delta = ld[r, :], dd[r, :]
                if prev:
                    lse_q = jnp.concatenate([lse_q, lse_q], axis=1)
                    delta = jnp.concatenate([delta, delta], axis=1)
                p = jnp.where(window_mask if prev else own_mask, jnp.exp(s - lse_q), 0.0)
                probs.append(p.astype(bf16))
                dscores.append((p * (dp - delta)).astype(bf16))
            dvs = [_dot_tn(p, dod[r, :]) for p, r in zip(probs, rows)]
            dks = [_dot_tn(ds, qd[r, :]) for ds, r in zip(dscores, rows)]
            dqs = [_dot(ds, kd[k, :]) for ds, k in zip(dscores, keys)]
            for (c, prev), r, dv, dk, dq in zip(four, rows, dvs, dks, dqs):
                aq[r, :] = dq
                if prev:
                    before = slice((c - 1) * SPAN, c * SPAN)
                    av[before, :] += dv[0:SPAN]
                    ak[before, :] += dk[0:SPAN]
                    av[r, :] = dv[SPAN:]
                    ak[r, :] = dk[SPAN:]
                else:
                    av[r, :] = dv
                    ak[r, :] = dk

        def finish(out_ref, acc, factor, roped):
            if dilation > 1:
                _by_position(tmp, acc, dilation)
            src = acc if dilation == 1 else tmp

            def rows(start):
                r = pl.ds(start, ROW_CHUNK)
                d = src[r, :]
                if factor != 1.0:
                    d = d * factor
                if roped:
                    d = _rope_transposed(d, cos_ref[r, :], sin_ref[r, :], rot_t)
                out_ref[r, :] = d.astype(bf16)

            _chunks(rows)

        finish(dq_ref, aq, scale, True)
        finish(dk_ref, ak, 1.0, True)
        finish(dv_ref, av, 1.0, False)

    head = pl.BlockSpec((SEQ, HEAD_DIM), lambda h: (0, h))
    in_specs = [head, head, head, _head_spec(BGATE_COL), head, head, _head_spec(8), _table_spec(), _table_spec(),
                pl.BlockSpec((HEAD_DIM, HEAD_DIM), lambda h: (0, 0))]
    n_out = 4 if with_gate else 3
    vm = lambda dt: pltpu.VMEM((SEQ, HEAD_DIM), dt)
    cos_t, sin_t, _, rot_t = tables
    return pl.pallas_call(
        body, name=f"attn_bwd_g{g}", grid=(N_HEADS,), in_specs=in_specs, out_specs=[head] * n_out,
        out_shape=[SDS((SEQ, HALF), bf16)] * n_out,
        scratch_shapes=[vm(bf16), vm(f32), vm(f32), vm(f32), vm(f32), vm(f32), vm(f32)],
    )(*saved, z0, att, lse, dcat, cos_t, sin_t, rot_t)


def _sgu_specs():
    chunk = lambda col: pl.BlockSpec((CHUNK, HALF), lambda n: (n, col))
    vec = pl.BlockSpec((1, HALF), lambda n: (0, 0))
    w = pl.BlockSpec((4, CHUNK, CHUNK), lambda n: (0, 0, 0))
    bias = pl.BlockSpec((CHUNK, CHUNK), lambda n: (0, 0))
    return chunk, vec, w, bias


def _sgu_weights(w_ref):
    tril = lax.broadcasted_iota(jnp.int32, (CHUNK, CHUNK), 1) <= lax.broadcasted_iota(jnp.int32, (CHUNK, CHUNK), 0)
    return tril, [jnp.where(tril, w_ref[h], 0.0).astype(bf16) for h in range(4)]


def _sgu_fwd(z1, ln_g, ln_b, sgu_w, bias_t):
    def body(u_ref, v_ref, cg_ref, g_ref, b_ref, w_ref, bias_ref, yc_ref):
        _, ws = _sgu_weights(w_ref)
        xh, _ = _ln_stats(v_ref[...])
        vn = (xh * g_ref[...] + b_ref[...]).astype(bf16)
        for h in range(4):
            cols = slice(h * POOL_CH, (h + 1) * POOL_CH)
            s = _dot(ws[h], vn[:, cols]) + bias_ref[:, h:h + 1]
            gate = cg_ref[:, cols]
            yc_ref[:, cols] = (u_ref[:, cols] * s * gate * _sigmoid(gate)).astype(bf16)

    chunk, vec, w, bias = _sgu_specs()
    return pl.pallas_call(
        body, name="sgu_fwd", grid=(SEQ // CHUNK,),
        in_specs=[chunk(0), chunk(1), chunk(2), vec, vec, w, bias], out_specs=chunk(0),
        out_shape=SDS((SEQ, HALF), bf16))(z1, z1, z1, ln_g, ln_b, sgu_w, bias_t)


def _sgu_bwd(z1, dcat, ln_g, ln_b, sgu_w, bias_t):
    def body(u_ref, v_ref, cg_ref, dyc_ref, g_ref, b_ref, w_ref, bias_ref,
             du_ref, dv_ref, dcg_ref, dw_ref, dbias_ref, dg_ref, db_ref, dvn_ref):
        first = pl.program_id(0) == 0
        tril, ws = _sgu_weights(w_ref)
        xh, rstd = _ln_stats(v_ref[...])
        g = g_ref[...]
        vn = (xh * g + b_ref[...]).astype(bf16)

        @pl.when(first)
        def _():
            dbias_ref[...] = jnp.zeros((CHUNK, CHUNK), f32)

        for h in range(4):
            cols = slice(h * POOL_CH, (h + 1) * POOL_CH)
            vn_h = vn[:, cols]
            s = _dot(ws[h], vn_h) + bias_ref[:, h:h + 1]
            silu, dsilu = _silu_and_grad(cg_ref[:, cols])
            dyc = dyc_ref[:, cols]
            u = u_ref[:, cols]
            du_ref[:, cols] = (dyc * s * silu).astype(bf16)
            dcg_ref[:, cols] = (dyc * u * s * dsilu).astype(bf16)
            ds = dyc * u * silu
            dbias_ref[:, h:h + 1] += jnp.sum(ds, axis=1, keepdims=True)
            ds = ds.astype(bf16)
            _accumulate(dw_ref.at[h], jnp.where(tril, _dot_nt(ds, vn_h), 0.0), first)
            dvn_ref[:, cols] = _dot_tn(ws[h], ds)
        dv, dg, db = _ln_bwd(xh, rstd, g, dvn_ref[...])
        dv_ref[...] = dv.astype(bf16)
        _accumulate(dg_ref, dg, first)
        _accumulate(db_ref, db, first)

    chunk, vec, w, bias = _sgu_specs()
    return pl.pallas_call(
        body, name="sgu_bwd", grid=(SEQ // CHUNK,),
        in_specs=[chunk(0), chunk(1), chunk(2), chunk(0), vec, vec, w, bias],
        out_specs=[chunk(0), chunk(0), chunk(0), w, bias, vec, vec],
        out_shape=[SDS((SEQ, HALF), bf16)] * 3 + [SDS((4, CHUNK, CHUNK), f32), SDS((CHUNK, CHUNK), f32),
                                                   SDS((1, HALF), f32), SDS((1, HALF), f32)],
        scratch_shapes=[pltpu.VMEM((CHUNK, HALF), f32)],
    )(z1, z1, z1, dcat, ln_g, ln_b, sgu_w, bias_t)


CONV_TILE = 128
DVAL_COL, DGLU_COL = 12, 16


def _conv_specs():
    val = pl.BlockSpec((SEQ, POOL_CH), lambda j: (0, DVAL_COL + j))
    glu = pl.BlockSpec((SEQ, POOL_CH), lambda j: (0, DGLU_COL + j))
    w = pl.BlockSpec((CONV_K, POOL_CH), lambda j: (0, j))
    col = pl.BlockSpec((SEQ, POOL_CH), lambda j: (0, j))
    vec = pl.BlockSpec((1, POOL_CH), lambda j: (0, j))
    return val, glu, w, col, vec


def _conv_fwd(z1, conv_w, conv_b):
    def body(val_ref, glu_ref, w_ref, b_ref, out_ref, xpad):
        xpad[0:CONV_PAD, :] = jnp.zeros((CONV_PAD, POOL_CH), f32)
        xpad[CONV_PAD:, :] = val_ref[...] * _sigmoid(glu_ref[...])
        w = w_ref[...]
        bias = b_ref[...]

        def tile(i, carry):
            t0 = pl.multiple_of(i * CONV_TILE, CONV_TILE)
            window = xpad[pl.ds(t0, CONV_TILE + CONV_PAD), :]
            acc = jnp.broadcast_to(bias, (CONV_TILE, POOL_CH))
            for k in range(CONV_K):
                shift = CONV_PAD - (CONV_K - 1) + k
                acc = acc + w[k:k + 1, :] * pltpu.roll(window, CONV_TILE + CONV_PAD - shift, 0)[0:CONV_TILE]
            out_ref[pl.ds(t0, CONV_TILE), :] = acc
            return carry

        lax.fori_loop(0, SEQ // CONV_TILE, tile, 0)

    val, glu, w, col, vec = _conv_specs()
    return pl.pallas_call(
        body, name="conv_fwd", grid=(4,), in_specs=[val, glu, w, vec], out_specs=col,
        out_shape=SDS((SEQ, HALF), f32), scratch_shapes=[pltpu.VMEM((SEQ + CONV_PAD, POOL_CH), f32)],
    )(z1, z1, conv_w, conv_b)


def _conv_bwd(z1, dconv, conv_w):
    def body(val_ref, glu_ref, w_ref, dout_ref, dval_ref, dglu_ref, dw_ref, db_ref, xpad, dpad, dx_ref):
        val = val_ref[...]
        sig = _sigmoid(glu_ref[...])
        xpad[0:CONV_PAD, :] = jnp.zeros((CONV_PAD, POOL_CH), f32)
        xpad[CONV_PAD:, :] = val * sig
        dout = dout_ref[...]
        dpad[0:SEQ, :] = dout
        dpad[SEQ:, :] = jnp.zeros((CONV_PAD, POOL_CH), f32)
        db_ref[...] = jnp.sum(dout, axis=0, keepdims=True)
        dw_ref[...] = jnp.zeros((CONV_K, POOL_CH), f32)
        w = w_ref[...]

        def tile(i, carry):
            t0 = pl.multiple_of(i * CONV_TILE, CONV_TILE)
            x_win = xpad[pl.ds(t0, CONV_TILE + CONV_PAD), :]
            d_win = dpad[pl.ds(t0, CONV_TILE + CONV_PAD), :]
            d_own = d_win[0:CONV_TILE]
            acc = jnp.zeros((CONV_TILE, POOL_CH), f32)
            for k in range(CONV_K):
                shift = CONV_PAD - (CONV_K - 1) + k
                x_k = pltpu.roll(x_win, CONV_TILE + CONV_PAD - shift, 0)[0:CONV_TILE]
                dw_ref[k:k + 1, :] += jnp.sum(d_own * x_k, axis=0, keepdims=True)
                back = CONV_K - 1 - k
                d_k = d_own if back == 0 else pltpu.roll(d_win, CONV_TILE + CONV_PAD - back, 0)[0:CONV_TILE]
                acc = acc + w[k:k + 1, :] * d_k
            dx_ref[pl.ds(t0, CONV_TILE), :] = acc
            return carry

        lax.fori_loop(0, SEQ // CONV_TILE, tile, 0)
        dx = dx_ref[...]
        dval_ref[...] = (dx * sig).astype(bf16)
        dglu_ref[...] = (dx * val * sig * (1.0 - sig)).astype(bf16)

    val, glu, w, col, vec = _conv_specs()
    pad = pltpu.VMEM((SEQ + CONV_PAD, POOL_CH), f32)
    return pl.pallas_call(
        body, name="conv_bwd", grid=(4,), in_specs=[val, glu, w, col], out_specs=[col, col, w, vec],
        out_shape=[SDS((SEQ, HALF), bf16), SDS((SEQ, HALF), bf16), SDS((CONV_K, HALF), f32), SDS((1, HALF), f32)],
        scratch_shapes=[pad, pad, pltpu.VMEM((SEQ, POOL_CH), f32)],
    )(z1, z1, conv_w, dconv)


DGATE_COL = 5


def _conv_norm_fwd(conv, z1, g, b):
    def body(c_ref, gate_ref, g_ref, b_ref, yd_ref):
        xh, _ = _ln_stats(c_ref[...])
        n = xh * g_ref[...] + b_ref[...]
        gate = gate_ref[...]
        yd_ref[...] = (n * _sigmoid(n) * gate * _sigmoid(gate)).astype(bf16)

    return pl.pallas_call(
        body, name="conv_norm_fwd", grid=(SEQ // ROWS,),
        in_specs=[_row_spec(HALF), _row_spec(HALF, DGATE_COL), _vec_spec(HALF), _vec_spec(HALF)],
        out_specs=_row_spec(HALF), out_shape=SDS((SEQ, HALF), bf16))(conv, z1, g, b)


def _conv_norm_bwd(conv, z1, dcat, g, b):
    def body(c_ref, gate_ref, dyd_ref, g_ref, b_ref, dconv_ref, dgate_ref, dg_ref, db_ref):
        first = pl.program_id(0) == 0
        xh, rstd = _ln_stats(c_ref[...])
        g = g_ref[...]
        n_silu, n_dsilu = _silu_and_grad(xh * g + b_ref[...])
        gate_silu, gate_dsilu = _silu_and_grad(gate_ref[...])
        dyd = dyd_ref[...]
        dgate_ref[...] = (dyd * n_silu * gate_dsilu).astype(bf16)
        dconv, dg, db = _ln_bwd(xh, rstd, g, dyd * gate_silu * n_dsilu)
        dconv_ref[...] = dconv
        _accumulate(dg_ref, dg, first)
        _accumulate(db_ref, db, first)

    return pl.pallas_call(
        body, name="conv_norm_bwd", grid=(SEQ // ROWS,),
        in_specs=[_row_spec(HALF), _row_spec(HALF, DGATE_COL), _row_spec(HALF, 1), _vec_spec(HALF), _vec_spec(HALF)],
        out_specs=[_row_spec(HALF), _row_spec(HALF), _vec_spec(HALF), _vec_spec(HALF)],
        out_shape=[SDS((SEQ, HALF), f32), SDS((SEQ, HALF), bf16), SDS((1, HALF), f32), SDS((1, HALF), f32)],
    )(conv, z1, dcat, g, b)


def _step(x, target, w, chip):
    chip_vec = chip.astype(jnp.int32).reshape(1)
    sharded_names = list(SHARDED_SMALL)
    small_shard = _pack([w[k] for k in sharded_names], total_rows=SMALL_SHARD_ROWS)
    small_slot = lax.dynamic_update_slice(jnp.zeros((N_CHIPS, SMALL_SHARD_ROWS, LANES), f32), small_shard[None], (chip, 0, 0))
    slots = [small_slot] + [_cast_into_slot(w[k], chip_vec, f"cast_{k}") for k in BIG]
    sems, bufs, token = _gather_start(slots)
    tables = _rope_tables()

    def vec(k):
        return w[k].reshape(1, -1)

    h0 = _pre_norm(x, vec("e_pre_norm") + token[0, 0])
    small_full, e_w_in = _forward_halves(_gather_wait(bufs[0:2], sems[0:2], h0, "gather_wait_first"), "forward_first")
    p = {k: _from_chips(k, a) for k, a in zip(sharded_names, _unpack(small_full, [SHARDED_SMALL[k][0] for k in sharded_names]))}
    for k in ("o_pre_norm", "o_sgu_norm_g", "o_sgu_norm_b", "o_conv_b", "o_conv_norm_g", "o_conv_norm_b", "o_post_norm"):
        p[k] = p[k].reshape(1, -1)
    pool_w_bf = p["e_pool_w"].astype(bf16)
    bias_t = jnp.pad(w["o_sgu_b"].T, ((0, 0), (0, CHUNK - 4)))

    z0 = _mm_nn(h0, e_w_in, f32, "e_in")
    ya = _pool_fwd(z0, pool_w_bf, vec("e_pool_scale"))
    yb, att, lse, qkv_by_residue = _attn_fwd(z0, tables)
    e_w_out, o_w_in, o_w_out = _forward_halves(_gather_wait(bufs[2:5], sems[2:5], att, "gather_wait_rest"), "forward_rest")
    e_w_out = e_w_out.reshape(1, D_MODEL, D_MODEL)
    o_w_out = o_w_out.reshape(1, D_MODEL, D_MODEL)
    cat0 = jnp.concatenate([ya, yb], axis=1)
    y0 = _mm_nn(cat0, e_w_out, f32, "e_out")
    x1, h1 = _mid_norm(x, y0, vec("e_post_norm"), p["o_pre_norm"])
    z1 = _mm_nn(h1, o_w_in, f32, "o_in")
    yc = _sgu_fwd(z1, p["o_sgu_norm_g"], p["o_sgu_norm_b"], w["o_sgu_w"], bias_t)
    conv = _conv_fwd(z1, p["o_conv_w"], p["o_conv_b"])
    yd = _conv_norm_fwd(conv, z1, p["o_conv_norm_g"], p["o_conv_norm_b"])
    cat1 = jnp.concatenate([yc, yd], axis=1)
    y1 = _mm_nn(cat1, o_w_out, f32, "o_out")
    loss, dx2, dy1, g_o_post = _final_norm_loss(x1, y1, p["o_post_norm"], target)

    in_flight = {}

    def send_off(name, grad):
        sem, sums, land, tok = _scatter_start(_swap_add(grad, f"swap_add_{name}"), f"scatter_start_{name}")
        in_flight[name] = (sem, sums, land)
        return tok

    tok = send_off("o_w_out", _mm_tn(cat1, dy1, 1, "o_out_dw").reshape(N_CHIPS, HALF // 2, D_MODEL))
    dcat1 = _mm_nt(dy1, o_w_out, "o_out_dx", tok)
    du, dv, dcg, g_sgu_w, g_bias_t, g_sgu_g, g_sgu_b = _sgu_bwd(
        z1, dcat1, p["o_sgu_norm_g"] + tok[0, 0], p["o_sgu_norm_b"], w["o_sgu_w"], bias_t)
    dconv, ddgate, g_cn_g, g_cn_b = _conv_norm_bwd(conv, z1, dcat1, p["o_conv_norm_g"], p["o_conv_norm_b"])
    ddval, ddglu, g_conv_w, g_conv_b = _conv_bwd(z1, dconv, p["o_conv_w"])
    dz1 = jnp.concatenate([du, dv, dcg, ddval, ddglu, ddgate], axis=1)
    tok = send_off("o_w_in", _mm_tn(h1, dz1, N_CHIPS, "o_in_dw"))
    dh1 = _mm_nt(dz1, o_w_in, "o_in_dx", tok)
    dx1, dy0, g_o_pre, g_e_post = _mid_norm_bwd(dx2, dh1, x1, y0, p["o_pre_norm"] + tok[0, 0], vec("e_post_norm"))

    tok = send_off("e_w_out", _mm_tn(cat0, dy0, 1, "e_out_dw").reshape(N_CHIPS, HALF // 2, D_MODEL))
    dcat0 = _mm_nt(dy0, e_w_out, "e_out_dx", tok)
    da, dagate, g_pool_w, g_pool_scale = _pool_bwd(z0, dcat0, pool_w_bf, vec("e_pool_scale") + tok[0, 0])
    dq0, dk0, dv0, dbgate = _attn_bwd_group(0, qkv_by_residue[0], z0, att, lse, dcat0, tables)
    dq1, dk1, dv1 = _attn_bwd_group(1, qkv_by_residue[1], z0, att, lse, dcat0, tables)
    dq2, dk2, dv2 = _attn_bwd_group(2, qkv_by_residue[2], z0, att, lse, dcat0, tables)
    dz0 = jnp.concatenate([da, dagate, dq0, dq1, dq2, dk0, dk1, dk2, dv0, dv1, dv2, dbgate], axis=1)
    tok = send_off("e_w_in", _mm_tn(h0, dz0, N_CHIPS, "e_in_dw"))
    dh0 = _mm_nt(dz0, e_w_in, "e_in_dx", tok)
    grad_x, g_e_pre = _pre_norm_bwd(dx1, dh0, x, vec("e_pre_norm") + tok[0, 0])

    small = {"e_pre_norm": g_e_pre, "e_pool_w": g_pool_w, "e_pool_scale": g_pool_scale, "e_post_norm": g_e_post,
             "o_pre_norm": g_o_pre, "o_sgu_norm_g": g_sgu_g, "o_sgu_norm_b": g_sgu_b, "o_sgu_w": g_sgu_w,
             "o_sgu_b": g_bias_t[:, 0:4].T, "o_conv_w": g_conv_w, "o_conv_b": g_conv_b,
             "o_conv_norm_g": g_cn_g, "o_conv_norm_b": g_cn_b, "o_post_norm": g_o_post}
    return loss, grad_x, in_flight, small


def _land(in_flight, name, chip, after):
    sems, sums, land = in_flight[name]
    sums, land = _scatter_wait(sems, sums, land, after, f"scatter_wait_{name}")
    return _add_landed_join(sums, land, chip.astype(jnp.int32).reshape(1), f"add_landed_{name}")


def _place():
    x, y, c = lax.axis_index("x"), lax.axis_index("y"), lax.axis_index("c")
    others = [(1 - x, y), (x, 1 - y), (1 - x, 1 - y)]
    return x, y, c, 2 * x + y, others


def _all_gather(shards, name):
    n = len(shards)

    def body(*refs):
        ins, outs = refs[:n], refs[n:2 * n]
        send_sems, recv_sems, local_sems = refs[2 * n:]
        x, y, c, me, others = _place()
        sibling = (x, y, 1 - c)

        def half(a, chip, core):
            rows = ins[a].shape[0] // 2
            return outs[a].at[chip, pl.ds(core * rows, rows), :]

        def copy(a, k, src, dst, to):
            return pltpu.make_async_remote_copy(src_ref=src, dst_ref=dst, send_sem=send_sems.at[6 * a + k],
                                                recv_sem=recv_sems.at[6 * a + k], device_id=to, device_id_type=MESH)

        local = [pltpu.make_async_copy(ins[a], outs[a].at[me], local_sems.at[a]) for a in range(n)]
        for cp in local:
            cp.start()
        sent = []
        for a in range(n):
            rows = ins[a].shape[0] // 2
            mine = ins[a].at[pl.ds(c * rows, rows), :]
            for k, (ox, oy) in enumerate(others):
                sent.append(copy(a, k, mine, half(a, me, c), (ox, oy, c)))
                sent[-1].start()
        for a in range(n):
            for k, (ox, oy) in enumerate(others):
                landed = half(a, 2 * ox + oy, c)
                copy(a, k, landed, landed, (ox, oy, c)).wait_recv()
                sent.append(copy(a, 3 + k, landed, landed, sibling))
                sent[-1].start()
        for k, (ox, oy) in enumerate(others):
            chip = 2 * ox + oy
            for a in range(n):
                theirs = half(a, chip, 1 - c)
                copy(a, 3 + k, theirs, theirs, sibling).wait_recv()
        for cp in sent:
            cp.wait_send()
        for cp in local:
            cp.wait()

    return pl.pallas_call(
        body, name=name, in_specs=[ANY] * n, out_specs=[ANY] * n,
        out_shape=[SDS((N_CHIPS,) + s.shape, s.dtype) for s in shards],
        scratch_shapes=[pltpu.SemaphoreType.DMA((6 * n,)), pltpu.SemaphoreType.DMA((6 * n,)), pltpu.SemaphoreType.DMA((n,))],
    )(*shards)


def _swap_halves(parts, name):
    n = len(parts)

    def body(*refs):
        ins, own, theirs = refs[:n], refs[n:2 * n], refs[2 * n:3 * n]
        send_sems, recv_sems, local_sems = refs[3 * n:]
        x, y, c, _, _ = _place()
        sibling = (x, y, 1 - c)
        copies = []
        for a in range(n):
            rows = ins[a].shape[1] // 2
            keep = pltpu.make_async_copy(ins[a].at[:, pl.ds(c * rows, rows), :], own[a], local_sems.at[a])
            give = pltpu.make_async_remote_copy(
                src_ref=ins[a].at[:, pl.ds((1 - c) * rows, rows), :], dst_ref=theirs[a], send_sem=send_sems.at[a],
                recv_sem=recv_sems.at[a], device_id=sibling, device_id_type=MESH)
            keep.start()
            give.start()
            copies += [keep, give]
        for cp in copies:
            cp.wait()

    half = [SDS((N_CHIPS, s.shape[1] // 2, s.shape[2]), s.dtype) for s in parts]
    out = pl.pallas_call(
        body, name=name, in_specs=[ANY] * n, out_specs=[ANY] * (2 * n), out_shape=half + half,
        scratch_shapes=[pltpu.SemaphoreType.DMA((n,)), pltpu.SemaphoreType.DMA((n,)), pltpu.SemaphoreType.DMA((n,))],
    )(*parts)
    return out[:n], out[n:]


def _scatter_chips(parts, name):
    n = len(parts)

    def body(*refs):
        ins, outs = refs[:n], refs[n:2 * n]
        send_sems, recv_sems, local_sems = refs[2 * n:]
        x, y, c, me, others = _place()

        def copy(a, k, slot_from, slot_to, chip_xy):
            return pltpu.make_async_remote_copy(
                src_ref=ins[a].at[slot_from], dst_ref=outs[a].at[slot_to], send_sem=send_sems.at[3 * a + k],
                recv_sem=recv_sems.at[3 * a + k], device_id=(chip_xy[0], chip_xy[1], c), device_id_type=MESH)

        keeps, gives = [], []
        for a in range(n):
            keeps.append(pltpu.make_async_copy(ins[a].at[me], outs[a].at[me], local_sems.at[a]))
            keeps[-1].start()
            for k, (ox, oy) in enumerate(others):
                gives.append(copy(a, k, 2 * ox + oy, me, (ox, oy)))
                gives[-1].start()
        for a in range(n):
            for k, (ox, oy) in enumerate(others):
                copy(a, k, me, 2 * ox + oy, (ox, oy)).wait_recv()
        for cp in gives:
            cp.wait_send()
        for cp in keeps:
            cp.wait()

    return pl.pallas_call(
        body, name=name, in_specs=[ANY] * n, out_specs=[ANY] * n, out_shape=[SDS(s.shape, s.dtype) for s in parts],
        scratch_shapes=[pltpu.SemaphoreType.DMA((3 * n,)), pltpu.SemaphoreType.DMA((3 * n,)), pltpu.SemaphoreType.DMA((n,))],
    )(*parts)


def _join_halves(halves, name):
    n = len(halves)

    def body(*refs):
        ins, outs = refs[:n], refs[n:2 * n]
        send_sems, recv_sems, local_sems = refs[2 * n:]
        x, y, c, _, _ = _place()

        def copy(a, core):
            rows = ins[a].shape[0]
            return pltpu.make_async_remote_copy(
                src_ref=ins[a], dst_ref=outs[a].at[pl.ds(core * rows, rows), :], send_sem=send_sems.at[a],
                recv_sem=recv_sems.at[a], device_id=(x, y, 1 - c), device_id_type=MESH)

        keeps, gives = [], []
        for a in range(n):
            rows = ins[a].shape[0]
            keeps.append(pltpu.make_async_copy(ins[a], outs[a].at[pl.ds(c * rows, rows), :], local_sems.at[a]))
            gives.append(copy(a, c))
            keeps[-1].start()
            gives[-1].start()
        for a in range(n):
            copy(a, 1 - c).wait_recv()
        for cp in gives:
            cp.wait_send()
        for cp in keeps:
            cp.wait()

    return pl.pallas_call(
        body, name=name, in_specs=[ANY] * n, out_specs=[ANY] * n,
        out_shape=[SDS((2 * s.shape[0], s.shape[1]), s.dtype) for s in halves],
        scratch_shapes=[pltpu.SemaphoreType.DMA((n,)), pltpu.SemaphoreType.DMA((n,)), pltpu.SemaphoreType.DMA((n,))],
    )(*halves)


def _add_pair(a, b, name):
    _, r, c = a.shape
    tr = 256 if r % 256 == 0 else r // 2 if r > 512 else r

    def body(a_ref, b_ref, o_ref):
        o_ref[...] = (a_ref[...].astype(f32) + b_ref[...].astype(f32)).astype(o_ref.dtype)

    spec = pl.BlockSpec((None, tr, c), lambda j, i: (j, i, 0))
    return pl.pallas_call(body, name=name, grid=(N_CHIPS, r // tr), in_specs=[spec, spec], out_specs=spec,
                          out_shape=SDS(a.shape, a.dtype))(a, b)


def _add_chips(u, name):
    _, r, c = u.shape
    tr = 256 if r % 256 == 0 else r

    def body(u_ref, o_ref):
        o_ref[...] = ((u_ref[0].astype(f32) + u_ref[1].astype(f32)) + u_ref[2].astype(f32)) + u_ref[3].astype(f32)

    return pl.pallas_call(
        body, name=name, grid=(r // tr,), in_specs=[pl.BlockSpec((N_CHIPS, tr, c), lambda i: (0, i, 0))],
        out_specs=pl.BlockSpec((tr, c), lambda i: (i, 0)), out_shape=SDS((r, c), f32))(u)


SWAP_ROWS = 256


def _swap_add(g, name):
    chips, r, c = g.shape
    half = r // 2
    rows_per_step = 2 * SWAP_ROWS if half % (2 * SWAP_ROWS) == 0 else SWAP_ROWS
    nb = half // rows_per_step
    steps = chips * nb

    def body(core_ref, mine_ref, theirs_ref, out_ref, landing, send_sems, recv_sems, free_sems):
        i = pl.program_id(0)
        slot = i % 2
        x, y, core, _, _ = _place()
        sibling = (x, y, 1 - core)

        @pl.when(i >= 2)
        def _():
            pl.semaphore_wait(free_sems.at[slot], 1)

        send = pltpu.make_async_remote_copy(src_ref=theirs_ref, dst_ref=landing.at[slot], send_sem=send_sems.at[slot],
                                            recv_sem=recv_sems.at[slot], device_id=sibling, device_id_type=MESH)
        send.start()
        send.wait_recv()
        out_ref[...] = (mine_ref[...].astype(f32) + landing[slot].astype(f32)).astype(out_ref.dtype)

        @pl.when(i + 2 < steps)
        def _():
            pl.semaphore_signal(free_sems.at[slot], 1, device_id=sibling, device_id_type=MESH)

        send.wait_send()

    block = (rows_per_step, c)
    grid_spec = pltpu.PrefetchScalarGridSpec(
        num_scalar_prefetch=1, grid=(steps,),
        in_specs=[pl.BlockSpec(block, lambda i, core: ((2 * (i // nb) + core[0]) * nb + i % nb, 0)),
                  pl.BlockSpec(block, lambda i, core: ((2 * (i // nb) + 1 - core[0]) * nb + i % nb, 0))],
        out_specs=pl.BlockSpec(block, lambda i, core: (i, 0)),
        scratch_shapes=[pltpu.VMEM((2, rows_per_step, c), g.dtype), pltpu.SemaphoreType.DMA((2,)),
                        pltpu.SemaphoreType.DMA((2,)), pltpu.SemaphoreType.REGULAR((2,))])
    core = lax.axis_index("c").astype(jnp.int32).reshape(1)
    rows = g.reshape(chips * r, c)
    out = pl.pallas_call(body, name=name, grid_spec=grid_spec, out_shape=SDS((chips * half, c), g.dtype))(core, rows, rows)
    return out.reshape(chips, half, c)


def _reduce_scatter(parts, tag):
    own, theirs = _swap_halves(parts, f"swap_halves_{tag}")
    chip_sums = [_add_pair(o, t, f"add_cores_{tag}{i}") for i, (o, t) in enumerate(zip(own, theirs))]
    gathered = _scatter_chips(chip_sums, f"scatter_chips_{tag}")
    halves = [_add_chips(u, f"add_chips_{tag}{i}") for i, u in enumerate(gathered)]
    return _join_halves(halves, f"join_halves_{tag}")


HBM = pl.BlockSpec(memory_space=pltpu.HBM)
SEM = pl.BlockSpec(memory_space=pltpu.SEMAPHORE)
EFFECT = pltpu.SideEffectType.DATAFLOW_SIDE_EFFECTING


def _in_hbm(a):
    return pltpu.with_memory_space_constraint(a, pltpu.HBM)


def _cast_into_slot(w, chip, name):
    r, c = w.shape
    nb = r // SWAP_ROWS

    def body(chip_ref, w_ref, o_ref):
        o_ref[...] = w_ref[...].astype(bf16)

    grid_spec = pltpu.PrefetchScalarGridSpec(
        num_scalar_prefetch=1, grid=(nb,),
        in_specs=[pl.BlockSpec((SWAP_ROWS, c), lambda i, chip: (i, 0))],
        out_specs=pl.BlockSpec((SWAP_ROWS, c), lambda i, chip: (chip[0] * nb + i, 0)))
    out = pl.pallas_call(body, name=name, grid_spec=grid_spec, out_shape=SDS((N_CHIPS * r, c), bf16))(chip, w)
    return out.reshape(N_CHIPS, r, c)


def _gather_start(bufs):
    n = len(bufs)

    def body(*refs):
        ins, sems, token = refs[:n], refs[n:3 * n], refs[4 * n]
        x, y, c, me, others = _place()
        for a in range(n):
            rows = ins[a].shape[1] // 2
            mine = ins[a].at[me, pl.ds(c * rows, rows), :]
            for k, (ox, oy) in enumerate(others):
                pltpu.make_async_remote_copy(src_ref=mine, dst_ref=mine, send_sem=sems[2 * a].at[k],
                                             recv_sem=sems[2 * a + 1].at[k], device_id=(ox, oy, c),
                                             device_id_type=MESH).start()
        token[...] = jnp.zeros_like(token)

    out = pl.pallas_call(
        body, name="gather_start", in_specs=[HBM] * n,
        out_shape=(*[pltpu.SemaphoreType.DMA((3,))] * (2 * n), *[pltpu.HBM(b.shape, b.dtype) for b in bufs],
                   SDS((8, 128), f32)),
        out_specs=(*[SEM] * (2 * n), *[HBM] * n, pl.BlockSpec(memory_space=pltpu.VMEM)),
        input_output_aliases={a: 2 * n + a for a in range(n)},
        compiler_params=pltpu.CompilerParams(has_side_effects=EFFECT),
    )(*[_in_hbm(b) for b in bufs])
    return [(out[2 * a], out[2 * a + 1]) for a in range(n)], list(out[2 * n:3 * n]), out[3 * n]


def _gather_wait(bufs, sems, after, name):
    n = len(bufs)

    def body(*refs):
        ins, sem_refs = refs[:n], refs[n:3 * n]
        x, y, c, me, others = _place()
        for a in range(n):
            rows = ins[a].shape[1] // 2
            mine = ins[a].at[me, pl.ds(c * rows, rows), :]
            for k, (ox, oy) in enumerate(others):
                landed = ins[a].at[2 * ox + oy, pl.ds(c * rows, rows), :]
                copy = pltpu.make_async_remote_copy(src_ref=mine, dst_ref=landed, send_sem=sem_refs[2 * a].at[k],
                                                    recv_sem=sem_refs[2 * a + 1].at[k], device_id=(ox, oy, c),
                                                    device_id_type=MESH)
                copy.wait_send()
                copy.wait_recv()

    flat_sems = [s for pair in sems for s in pair]
    out = pl.pallas_call(
        body, name=name, in_specs=[HBM] * n + [SEM] * (2 * n) + [ANY],
        out_shape=tuple(pltpu.HBM(b.shape, b.dtype) for b in bufs), out_specs=tuple([HBM] * n),
        input_output_aliases={a: a for a in range(n)},
        compiler_params=pltpu.CompilerParams(has_side_effects=EFFECT),
    )(*bufs, *flat_sems, after)
    return list(out)


def _forward_halves(bufs, name):
    n = len(bufs)
    blocks = []
    for b in bufs:
        half = b.shape[1] // 2
        tr = SWAP_ROWS if half % SWAP_ROWS == 0 else half
        blocks.append((half, tr))
    work = [(a, k, b) for a in range(n) for k in range(3) for b in range(blocks[a][0] // blocks[a][1])]

    def body(*refs):
        outs, stages = refs[n:2 * n], refs[2 * n:3 * n]
        load_sems, send_sems, recv_sems = refs[3 * n:]
        x, y, c, me, others = _place()
        sibling = (x, y, 1 - c)

        def rows(item):
            a, k, b = item
            half, tr = blocks[a]
            ox, oy = others[k]
            return outs[a].at[2 * ox + oy, pl.ds(c * half + b * tr, tr), :]

        def load(s, item):
            return pltpu.make_async_copy(rows(item), stages[item[0]].at[s], load_sems.at[s])

        def send(s, item):
            return pltpu.make_async_remote_copy(src_ref=stages[item[0]].at[s], dst_ref=rows(item), send_sem=send_sems.at[s],
                                                recv_sem=recv_sems.at[item[0]], device_id=sibling, device_id_type=MESH)

        load(0, work[0]).start()
        for t, item in enumerate(work):
            s = t % 2
            load(s, item).wait()
            send(s, item).start()
            if t + 1 < len(work):
                if t >= 1:
                    send(1 - s, work[t - 1]).wait_send()
                load(1 - s, work[t + 1]).start()
        if len(work) > 1:
            send(len(work) % 2, work[-2]).wait_send()
        send((len(work) - 1) % 2, work[-1]).wait_send()
        for a in range(n):
            theirs = outs[a].at[pl.ds(0, 3), pl.ds(0, blocks[a][0]), :]
            pltpu.make_async_remote_copy(src_ref=theirs, dst_ref=theirs, send_sem=send_sems.at[0], recv_sem=recv_sems.at[a],
                                         device_id=sibling, device_id_type=MESH).wait_recv()

    out = pl.pallas_call(
        body, name=name, in_specs=[ANY] * n, out_specs=[ANY] * n, out_shape=[SDS(b.shape, b.dtype) for b in bufs],
        input_output_aliases={a: a for a in range(n)},
        scratch_shapes=[pltpu.VMEM((2, blocks[a][1], bufs[a].shape[2]), bufs[a].dtype) for a in range(n)]
        + [pltpu.SemaphoreType.DMA((2,)), pltpu.SemaphoreType.DMA((2,)), pltpu.SemaphoreType.DMA((n,))],
    )(*bufs)
    return list(out)


def _scatter_start(chip_sums, name):
    def body(a_ref, land_ref, send_sems, recv_sems, a_thru, land_thru, token):
        x, y, c, me, others = _place()
        for k, (ox, oy) in enumerate(others):
            pltpu.make_async_remote_copy(src_ref=a_ref.at[2 * ox + oy], dst_ref=land_ref.at[me], send_sem=send_sems.at[k],
                                         recv_sem=recv_sems.at[k], device_id=(ox, oy, c), device_id_type=MESH).start()
        token[...] = jnp.zeros_like(token)

    shape = pltpu.HBM(chip_sums.shape, chip_sums.dtype)
    send, recv, a_thru, land, token = pl.pallas_call(
        body, name=name, in_specs=[HBM, HBM],
        out_shape=(pltpu.SemaphoreType.DMA((3,)), pltpu.SemaphoreType.DMA((3,)), shape, shape, SDS((8, 128), f32)),
        out_specs=(SEM, SEM, HBM, HBM, pl.BlockSpec(memory_space=pltpu.VMEM)), input_output_aliases={0: 2, 1: 3},
        compiler_params=pltpu.CompilerParams(has_side_effects=EFFECT),
    )(_in_hbm(chip_sums), _in_hbm(lax.empty(chip_sums.shape, chip_sums.dtype)))
    return (send, recv), a_thru, land, token


def _scatter_wait(sems, chip_sums, land, after, name):
    def body(a_ref, land_ref, send_sems, recv_sems, after_ref, a_out, land_out):
        x, y, c, me, others = _place()
        for k, (ox, oy) in enumerate(others):
            copy = pltpu.make_async_remote_copy(
                src_ref=a_ref.at[2 * ox + oy], dst_ref=land_ref.at[2 * ox + oy], send_sem=send_sems.at[k],
                recv_sem=recv_sems.at[k], device_id=(ox, oy, c), device_id_type=MESH)
            copy.wait_send()
            copy.wait_recv()

    shape = pltpu.HBM(chip_sums.shape, chip_sums.dtype)
    return pl.pallas_call(
        body, name=name, in_specs=[HBM, HBM, SEM, SEM, ANY], out_shape=(shape, shape), out_specs=(HBM, HBM),
        input_output_aliases={0: 0, 1: 1}, compiler_params=pltpu.CompilerParams(has_side_effects=EFFECT),
    )(chip_sums, land, sems[0], sems[1], after)


def _add_landed_join(chip_sums, land, chip, name):
    chips, rh, c = chip_sums.shape
    nb = rh // SWAP_ROWS

    def body(chip_ref, own_ref, l1_ref, l2_ref, l3_ref, out_hbm, buf, send_sems, recv_sem, local_sems):
        i = pl.program_id(0)
        slot = i % 2
        x, y, core, _, _ = _place()
        sibling = (x, y, 1 - core)

        def copies(s, step):
            rows = pl.ds(pl.multiple_of((core * nb + step) * SWAP_ROWS, SWAP_ROWS), SWAP_ROWS)
            keep = pltpu.make_async_copy(buf.at[s], out_hbm.at[rows, :], local_sems.at[s])
            give = pltpu.make_async_remote_copy(src_ref=buf.at[s], dst_ref=out_hbm.at[rows, :], send_sem=send_sems.at[s],
                                                recv_sem=recv_sem.at[0], device_id=sibling, device_id_type=MESH)
            return keep, give

        def drain(s, step):
            keep, give = copies(s, step)
            keep.wait()
            give.wait_send()

        @pl.when(i >= 2)
        def _():
            drain(slot, i - 2)

        buf[slot] = ((own_ref[...].astype(f32) + l1_ref[...].astype(f32)) + l2_ref[...].astype(f32)) + l3_ref[...].astype(f32)
        keep, give = copies(slot, i)
        keep.start()
        give.start()

        @pl.when(i == nb - 1)
        def _():
            drain(slot, i)
            if nb > 1:
                drain(1 - slot, i - 1)
            theirs = out_hbm.at[pl.ds((1 - core) * rh, rh), :]
            pltpu.make_async_remote_copy(src_ref=theirs, dst_ref=theirs, send_sem=send_sems.at[0], recv_sem=recv_sem.at[0],
                                         device_id=sibling, device_id_type=MESH).wait_recv()

    block = (SWAP_ROWS, c)
    from_slot = lambda d: pl.BlockSpec(block, lambda i, chip: (((chip[0] + d) % chips) * nb + i, 0))
    grid_spec = pltpu.PrefetchScalarGridSpec(
        num_scalar_prefetch=1, grid=(nb,), in_specs=[from_slot(0), from_slot(1), from_slot(2), from_slot(3)],
        out_specs=ANY,
        scratch_shapes=[pltpu.VMEM((2, SWAP_ROWS, c), f32), pltpu.SemaphoreType.DMA((2,)),
                        pltpu.SemaphoreType.DMA((1,)), pltpu.SemaphoreType.DMA((2,))])
    land_rows = land.reshape(chips * rh, c)
    return pl.pallas_call(body, name=name, grid_spec=grid_spec, out_shape=SDS((2 * rh, c), f32))(
        chip, chip_sums.reshape(chips * rh, c), land_rows, land_rows, land_rows)


def _adamw_update(w_ref, g_ref, m_ref, v_ref, d_ref, nm_ref, nv_ref):
    g = g_ref[...]
    nm = ADAM_B1 * m_ref[...] + (1.0 - ADAM_B1) * g
    nv = ADAM_B2 * v_ref[...] + (1.0 - ADAM_B2) * (g * g)
    nm_ref[...] = nm
    nv_ref[...] = nv
    m_hat = nm / (1.0 - ADAM_B1 ** ADAM_STEP)
    v_hat = nv / (1.0 - ADAM_B2 ** ADAM_STEP)
    d_ref[...] = -ADAM_LR * (m_hat / (jnp.sqrt(v_hat) + ADAM_EPS) + ADAM_WD * w_ref[...])


def _adamw(w, g, m, v, name):
    r, c = w.shape
    tr = 128 if r % 128 == 0 else r
    spec = pl.BlockSpec((tr, c), lambda i: (i, 0))
    return pl.pallas_call(functools.partial(_adamw_update), name=name, grid=(r // tr,), in_specs=[spec] * 4,
                          out_specs=[spec] * 3, out_shape=[SDS((r, c), f32)] * 3)(w, g, m, v)


def _adamw_small(ws, gs, ms, vs):
    n = len(ws)

    def body(*refs):
        for i in range(n):
            _adamw_update(*refs[i:7 * n:n])

    whole = pl.BlockSpec(memory_space=pltpu.VMEM)
    out = pl.pallas_call(body, name="adamw_small", in_specs=[whole] * (4 * n), out_specs=[whole] * (3 * n),
                         out_shape=[SDS(a.shape, f32) for a in ws] * 3)(*ws, *gs, *ms, *vs)
    return out[:n], out[n:2 * n], out[2 * n:]


def _pack(arrays, total_rows=None):
    parts = []
    rows = 0
    for a in arrays:
        flat = a.reshape(-1, LANES)
        pad = -flat.shape[0] % 8
        parts.append(jnp.pad(flat, ((0, pad), (0, 0))))
        rows += flat.shape[0] + pad
    if total_rows is not None:
        parts.append(jnp.zeros((total_rows - rows, LANES), arrays[0].dtype))
    return jnp.concatenate(parts, axis=0)


def _unpack(buf, shapes):
    out = []
    row = 0
    lead = buf.shape[:-2]
    for shape in shapes:
        size = 1
        for s in shape:
            size *= s
        rows = size // LANES
        out.append(buf[..., row:row + rows, :].reshape(lead + tuple(shape)))
        row += rows + (-rows % 8)
    return out


BIG = ("e_w_in", "e_w_out", "o_w_in", "o_w_out")
SHARDED_SMALL = {
    "e_pool_w": ((4, 64, 256), 1), "o_pre_norm": ((512,), 0), "o_sgu_norm_g": ((256,), 0), "o_sgu_norm_b": ((256,), 0),
    "o_conv_w": ((31, 256), 1), "o_conv_b": ((256,), 0), "o_conv_norm_g": ((256,), 0), "o_conv_norm_b": ((256,), 0),
    "o_post_norm": ((512,), 0),
}
REPLICATED_SMALL = {"e_pre_norm": (2048,), "e_pool_scale": (1024,), "e_post_norm": (2048,),
                    "o_sgu_w": (4, 128, 128), "o_sgu_b": (4, 128)}
SMALL_ORDER = ("e_pre_norm", "e_pool_w", "e_pool_scale", "e_post_norm", "o_pre_norm", "o_sgu_norm_g", "o_sgu_norm_b",
               "o_sgu_w", "o_sgu_b", "o_conv_w", "o_conv_b", "o_conv_norm_g", "o_conv_norm_b", "o_post_norm")
ALL_ORDER = ("e_pre_norm", "e_w_in", "e_pool_w", "e_pool_scale", "e_w_out", "e_post_norm", "o_pre_norm", "o_w_in",
             "o_sgu_norm_g", "o_sgu_norm_b", "o_sgu_w", "o_sgu_b", "o_conv_w", "o_conv_b", "o_conv_norm_g",
             "o_conv_norm_b", "o_w_out", "o_post_norm")


def _full_shape(name):
    shape, axis = SHARDED_SMALL[name]
    return tuple(s * N_CHIPS if i == axis else s for i, s in enumerate(shape))


def _from_chips(name, stacked):
    shape, axis = SHARDED_SMALL[name]
    return jnp.moveaxis(stacked, 0, axis).reshape(_full_shape(name))


def _my_shard(name, full, chip):
    shape, axis = SHARDED_SMALL[name]
    return lax.dynamic_slice_in_dim(full, chip * shape[axis], shape[axis], axis)


def kernel(x, e_pre_norm, e_w_in, e_pool_w, e_pool_scale, e_w_out, e_post_norm, o_pre_norm, o_w_in, o_sgu_norm_g, o_sgu_norm_b, o_sgu_w, o_sgu_b, o_conv_w, o_conv_b, o_conv_norm_g, o_conv_norm_b, o_w_out, o_post_norm, loss_target, m_e_pre_norm, m_e_w_in, m_e_pool_w, m_e_pool_scale, m_e_w_out, m_e_post_norm, m_o_pre_norm, m_o_w_in, m_o_sgu_norm_g, m_o_sgu_norm_b, m_o_sgu_w, m_o_sgu_b, m_o_conv_w, m_o_conv_b, m_o_conv_norm_g, m_o_conv_norm_b, m_o_w_out, m_o_post_norm, v_e_pre_norm, v_e_w_in, v_e_pool_w, v_e_pool_scale, v_e_w_out, v_e_post_norm, v_o_pre_norm, v_o_w_in, v_o_sgu_norm_g, v_o_sgu_norm_b, v_o_sgu_w, v_o_sgu_b, v_o_conv_w, v_o_conv_b, v_o_conv_norm_g, v_o_conv_norm_b, v_o_w_out, v_o_post_norm):
    w = dict(e_pre_norm=e_pre_norm, e_w_in=e_w_in, e_pool_w=e_pool_w, e_pool_scale=e_pool_scale, e_w_out=e_w_out,
             e_post_norm=e_post_norm, o_pre_norm=o_pre_norm, o_w_in=o_w_in, o_sgu_norm_g=o_sgu_norm_g,
             o_sgu_norm_b=o_sgu_norm_b, o_sgu_w=o_sgu_w, o_sgu_b=o_sgu_b, o_conv_w=o_conv_w, o_conv_b=o_conv_b,
             o_conv_norm_g=o_conv_norm_g, o_conv_norm_b=o_conv_norm_b, o_w_out=o_w_out, o_post_norm=o_post_norm)
    m = dict(e_pre_norm=m_e_pre_norm, e_w_in=m_e_w_in, e_pool_w=m_e_pool_w, e_pool_scale=m_e_pool_scale,
             e_w_out=m_e_w_out, e_post_norm=m_e_post_norm, o_pre_norm=m_o_pre_norm, o_w_in=m_o_w_in,
             o_sgu_norm_g=m_o_sgu_norm_g, o_sgu_norm_b=m_o_sgu_norm_b, o_sgu_w=m_o_sgu_w, o_sgu_b=m_o_sgu_b,
             o_conv_w=m_o_conv_w, o_conv_b=m_o_conv_b, o_conv_norm_g=m_o_conv_norm_g, o_conv_norm_b=m_o_conv_norm_b,
             o_w_out=m_o_w_out, o_post_norm=m_o_post_norm)
    v = dict(e_pre_norm=v_e_pre_norm, e_w_in=v_e_w_in, e_pool_w=v_e_pool_w, e_pool_scale=v_e_pool_scale,
             e_w_out=v_e_w_out, e_post_norm=v_e_post_norm, o_pre_norm=v_o_pre_norm, o_w_in=v_o_w_in,
             o_sgu_norm_g=v_o_sgu_norm_g, o_sgu_norm_b=v_o_sgu_norm_b, o_sgu_w=v_o_sgu_w, o_sgu_b=v_o_sgu_b,
             o_conv_w=v_o_conv_w, o_conv_b=v_o_conv_b, o_conv_norm_g=v_o_conv_norm_g, o_conv_norm_b=v_o_conv_norm_b,
             o_w_out=v_o_w_out, o_post_norm=v_o_post_norm)
    w, m, v = ({k: a[0] for k, a in d.items()} for d in (w, m, v))
    chip = 2 * lax.axis_index("x") + lax.axis_index("y")

    loss, grad_x, in_flight, small = _step(x[0], loss_target[0], w, chip)

    grads, delta, new_m, new_v = {}, {}, {}, {}
    after = grad_x
    for k in ("o_w_out", "o_w_in", "e_w_out", "e_w_in"):
        grads[k] = _land(in_flight, k, chip, after)
        delta[k], new_m[k], new_v[k] = _adamw(w[k], grads[k], m[k], v[k], f"adamw_{k}")
        after = delta[k]

    small_full_shapes = {k: (_full_shape(k) if k in SHARDED_SMALL else REPLICATED_SMALL[k]) for k in SMALL_ORDER}
    small_parts = _pack([small[k].reshape(small_full_shapes[k]) for k in SMALL_ORDER], total_rows=SMALL_GRAD_ROWS)
    small_parts = small_parts + 0.0 * after[0, 0]
    reduced = _reduce_scatter([small_parts.reshape(N_CHIPS, SMALL_GRAD_ROWS // N_CHIPS, LANES)], "small")
    small_sum = _all_gather(reduced, "gather_small_grads")[0].reshape(SMALL_GRAD_ROWS, LANES)
    for k, a in zip(SMALL_ORDER, _unpack(small_sum, [small_full_shapes[k] for k in SMALL_ORDER])):
        grads[k] = _my_shard(k, a, chip) if k in SHARDED_SMALL else a
    loss = lax.psum(loss[0, 0], ("x", "y", "c"))

    def rows_of(a):
        return a.reshape(-1, a.shape[-1])

    updates = _adamw_small(*[[rows_of(d[k]) for k in SMALL_ORDER] for d in (w, grads, m, v)])
    for d, arrays in zip((delta, new_m, new_v), updates):
        for k, a in zip(SMALL_ORDER, arrays):
            d[k] = a.reshape(w[k].shape)

    outs = [loss, grad_x[None]]
    for d in (grads, delta, new_m, new_v):
        outs += [d[k][None] for k in ALL_ORDER]
    return tuple(outs)
```

```python
import functools

import jax
import jax.numpy as jnp
from jax import lax
from jax.experimental import pallas as pl
from jax.experimental.pallas import tpu as pltpu

f32 = jnp.float32
bf16 = jnp.bfloat16
SDS = jax.ShapeDtypeStruct

SEQ = 2048
D_MODEL = 2048
EPS = 1e-6
NEG = -1e30
HEAD_DIM = 128
ROT_HALF = 16
ROPE_THETA = 500000.0
DILATIONS = (1, 4, 16)
SPAN = 128
N_HEADS = 8
HALF = 1024
POOL_CH = 256
CONV_K = 31
CONV_PAD = 32
CHUNK = 128
N_CHIPS = 4
LANES = 256
E_IN_PIECES = 3
SMALL_SHARD_ROWS = 352
SMALL_GRAD_ROWS = 1536
ANY = pl.BlockSpec(memory_space=pl.ANY)
MESH = pl.DeviceIdType.MESH

ADAM_LR = 0.001
ADAM_B1 = 0.9
ADAM_B2 = 0.999
ADAM_EPS = 1e-08
ADAM_WD = 0.01
ADAM_STEP = 10


def _dot(a, b):
    return jnp.dot(a, b, preferred_element_type=f32)


def _dot_nt(a, b):
    return lax.dot_general(a, b, (((1,), (1,)), ((), ())), preferred_element_type=f32)


def _dot_tn(a, b):
    return lax.dot_general(a, b, (((0,), (0,)), ((), ())), preferred_element_type=f32)


def _sigmoid(x):
    return 1.0 / (1.0 + jnp.exp(-x))


def _silu_and_grad(x):
    s = _sigmoid(x)
    return x * s, s * (1.0 + x * (1.0 - s))


def _rms_fwd(x, g):
    r = lax.rsqrt(jnp.mean(x * x, axis=-1, keepdims=True) + EPS)
    return x * r * g


def _rms_bwd(x, g, dout):
    r = lax.rsqrt(jnp.mean(x * x, axis=-1, keepdims=True) + EPS)
    xh = x * r
    dg = jnp.sum(dout * xh, axis=0, keepdims=True)
    dxh = dout * g
    dx = r * (dxh - xh * jnp.mean(dxh * xh, axis=-1, keepdims=True))
    return dx, dg


def _ln_stats(x):
    mu = jnp.mean(x, axis=-1, keepdims=True)
    xc = x - mu
    rstd = lax.rsqrt(jnp.mean(xc * xc, axis=-1, keepdims=True) + EPS)
    return xc * rstd, rstd


def _ln_bwd(xh, rstd, g, dout):
    dg = jnp.sum(dout * xh, axis=0, keepdims=True)
    db = jnp.sum(dout, axis=0, keepdims=True)
    dxh = dout * g
    dx = rstd * (dxh - jnp.mean(dxh, axis=-1, keepdims=True) - xh * jnp.mean(dxh * xh, axis=-1, keepdims=True))
    return dx, dg, db


def _accumulate(ref, value, first):
    @pl.when(first)
    def _():
        ref[...] = value

    @pl.when(jnp.logical_not(first))
    def _():
        ref[...] += value


def _col_tile(ns):
    for t in (1024, 768, 512, 256):
        if ns % t == 0:
            return t
    raise ValueError(ns)


def _mm_nn(a, w, out_dtype, name, piece=0, pieces=1, into=None):
    m, k = a.shape
    j, _, ns = w.shape
    tm, tn = 1024, _col_tile(ns)
    nb = ns // tn

    def body(a_ref, w_ref, *rest):
        rest[-1][...] = _dot(a_ref[...], w_ref[...]).astype(rest[-1].dtype)

    return pl.pallas_call(
        body, name=name, grid=(j * nb, m // tm),
        in_specs=[pl.BlockSpec((tm, k), lambda n, i: (i, 0)),
                  pl.BlockSpec((None, k, tn), lambda n, i: (n // nb, 0, n % nb))] + ([] if into is None else [ANY]),
        out_specs=pl.BlockSpec((tm, tn), lambda n, i: (i, ((n // nb) * pieces + piece) * nb + n % nb)),
        out_shape=SDS((m, j * ns * pieces), out_dtype),
        input_output_aliases={} if into is None else {2: 0},
    )(a, w, *([] if into is None else [into]))


def _mm_nt(dz, w, name, after):
    m, _ = dz.shape
    j, k, ns = w.shape
    tm, tk = 1024, 1024
    tn = next(t for t in (2048, 1536, 1024, 768, 512) if ns % t == 0)
    nb = ns // tn

    def body(dz_ref, w_ref, after_ref, o_ref):
        _accumulate(o_ref, _dot_nt(dz_ref[...], w_ref[...]), pl.program_id(2) == 0)

    return pl.pallas_call(
        body, name=name, grid=(m // tm, k // tk, j * nb),
        in_specs=[pl.BlockSpec((tm, tn), lambda i, kk, r: (i, r)),
                  pl.BlockSpec((None, tk, tn), lambda i, kk, r: (r // nb, kk, r % nb)), ANY],
        out_specs=pl.BlockSpec((tm, tk), lambda i, kk, r: (i, kk)),
        out_shape=SDS((m, k), f32),
    )(dz, w, after)


def _mm_tn(a, dz, j, name):
    m, k = a.shape
    ns = dz.shape[1] // j
    tk, tn = 1024, _col_tile(ns)
    nb = ns // tn

    def body(a_ref, dz_ref, o_ref):
        o_ref[...] = _dot_tn(a_ref[...], dz_ref[...]).astype(o_ref.dtype)

    return pl.pallas_call(
        body, name=name, grid=(k // tk, j * nb),
        in_specs=[pl.BlockSpec((m, tk), lambda kk, n: (0, kk)),
                  pl.BlockSpec((m, tn), lambda kk, n: (0, n))],
        out_specs=pl.BlockSpec((None, tk, tn), lambda kk, n: (n // nb, kk, n % nb)),
        out_shape=SDS((j, k, ns), bf16),
    )(a, dz)


ROWS = 256


def _row_spec(width=D_MODEL, col=0):
    return pl.BlockSpec((ROWS, width), lambda i: (i, col))


def _vec_spec(width=D_MODEL):
    return pl.BlockSpec((1, width), lambda i: (0, 0))


def _pre_norm(x, g):
    def body(x_ref, g_ref, h_ref):
        h_ref[...] = _rms_fwd(x_ref[...], g_ref[...]).astype(bf16)

    return pl.pallas_call(
        body, name="pre_norm", grid=(SEQ // ROWS,), in_specs=[_row_spec(), _vec_spec()],
        out_specs=_row_spec(), out_shape=SDS((SEQ, D_MODEL), bf16))(x, g)


def _mid_norm(x, y, g_post, g_pre):
    def body(x_ref, y_ref, gpost_ref, gpre_ref, x1_ref, h1_ref):
        x1 = x_ref[...] + _rms_fwd(y_ref[...], gpost_ref[...])
        x1_ref[...] = x1
        h1_ref[...] = _rms_fwd(x1, gpre_ref[...]).astype(bf16)

    return pl.pallas_call(
        body, name="mid_norm", grid=(SEQ // ROWS,),
        in_specs=[_row_spec(), _row_spec(), _vec_spec(), _vec_spec()],
        out_specs=[_row_spec(), _row_spec()],
        out_shape=[SDS((SEQ, D_MODEL), f32), SDS((SEQ, D_MODEL), bf16)])(x, y, g_post, g_pre)


def _final_norm_loss(x1, y, g_post, target):
    def body(x1_ref, y_ref, g_ref, t_ref, loss_ref, dx2_ref, dy_ref, dg_ref):
        first = pl.program_id(0) == 0
        y = y_ref[...]
        g = g_ref[...]
        err = x1_ref[...] + _rms_fwd(y, g) - t_ref[...]
        sq = jnp.sum(jnp.sum(err * err, axis=1, keepdims=True), axis=0, keepdims=True)
        _accumulate(loss_ref, sq * (0.5 / D_MODEL), first)
        dx2 = err * (1.0 / D_MODEL)
        dx2_ref[...] = dx2
        dy, dg = _rms_bwd(y, g, dx2)
        dy_ref[...] = dy.astype(bf16)
        _accumulate(dg_ref, dg, first)

    return pl.pallas_call(
        body, name="final_norm_loss", grid=(SEQ // ROWS,),
        in_specs=[_row_spec(), _row_spec(), _vec_spec(), _row_spec()],
        out_specs=[pl.BlockSpec((1, 1), lambda i: (0, 0)), _row_spec(), _row_spec(), _vec_spec()],
        out_shape=[SDS((1, 1), f32), SDS((SEQ, D_MODEL), f32), SDS((SEQ, D_MODEL), bf16), SDS((1, D_MODEL), f32)],
    )(x1, y, g_post, target)


def _mid_norm_bwd(dx2, dh1, x1, y0, g_pre, g_post):
    def body(dx2_ref, dh1_ref, x1_ref, y0_ref, gpre_ref, gpost_ref, dx1_ref, dy0_ref, dgpre_ref, dgpost_ref):
        first = pl.program_id(0) == 0
        d_in, dgpre = _rms_bwd(x1_ref[...], gpre_ref[...], dh1_ref[...])
        dx1 = dx2_ref[...] + d_in
        dx1_ref[...] = dx1
        dy0, dgpost = _rms_bwd(y0_ref[...], gpost_ref[...], dx1)
        dy0_ref[...] = dy0.astype(bf16)
        _accumulate(dgpre_ref, dgpre, first)
        _accumulate(dgpost_ref, dgpost, first)

    return pl.pallas_call(
        body, name="mid_norm_bwd", grid=(SEQ // ROWS,),
        in_specs=[_row_spec(), _row_spec(), _row_spec(), _row_spec(), _vec_spec(), _vec_spec()],
        out_specs=[_row_spec(), _row_spec(), _vec_spec(), _vec_spec()],
        out_shape=[SDS((SEQ, D_MODEL), f32), SDS((SEQ, D_MODEL), bf16), SDS((1, D_MODEL), f32), SDS((1, D_MODEL), f32)],
    )(dx2, dh1, x1, y0, g_pre, g_post)


def _pre_norm_bwd(dx1, dh0, x, g):
    def body(dx1_ref, dh0_ref, x_ref, g_ref, dx_ref, dg_ref):
        d_in, dg = _rms_bwd(x_ref[...], g_ref[...], dh0_ref[...])
        dx_ref[...] = dx1_ref[...] + d_in
        _accumulate(dg_ref, dg, pl.program_id(0) == 0)

    return pl.pallas_call(
        body, name="pre_norm_bwd", grid=(SEQ // ROWS,),
        in_specs=[_row_spec(), _row_spec(), _row_spec(), _vec_spec()],
        out_specs=[_row_spec(), _vec_spec()],
        out_shape=[SDS((SEQ, D_MODEL), f32), SDS((1, D_MODEL), f32)])(dx1, dh0, x, g)


def _pool_count(g):
    row = lax.broadcasted_iota(jnp.int32, (SEQ, 1), 0)
    width = jnp.left_shift(2, g)
    return row, width, jnp.minimum(row + 1, width).astype(f32)


def _trailing_sum(x, row, width):
    s = x
    for k in (1, 2, 4, 8):
        shifted = jnp.where(row >= k, pltpu.roll(s, k, 0), 0.0)
        s = jnp.where(width > k, s + shifted, s)
    return s


def _leading_sum(x, row, width):
    s = x
    for k in (1, 2, 4, 8):
        shifted = jnp.where(row < SEQ - k, pltpu.roll(s, SEQ - k, 0), 0.0)
        s = jnp.where(width > k, s + shifted, s)
    return s


def _pool_specs():
    a_in = pl.BlockSpec((SEQ, POOL_CH), lambda g: (0, g))
    a_gate = pl.BlockSpec((SEQ, POOL_CH), lambda g: (0, 4 + g))
    w = pl.BlockSpec((None, POOL_CH, POOL_CH), lambda g: (g, 0, 0))
    scale = pl.BlockSpec((1, POOL_CH), lambda g: (0, g))
    return a_in, a_gate, w, scale


def _pool_fwd(z0, pool_w, pool_scale):
    def body(a_ref, gate_ref, w_ref, scale_ref, ya_ref):
        row, width, count = _pool_count(pl.program_id(0))
        a = a_ref[...]
        pooled = _trailing_sum(a, row, width) / count - a
        mixed = _dot(pooled.astype(bf16), w_ref[...]) * scale_ref[...]
        gate = gate_ref[...]
        ya_ref[...] = (mixed * gate * _sigmoid(gate)).astype(bf16)

    return pl.pallas_call(
        body, name="pool_fwd", grid=(4,), in_specs=list(_pool_specs()),
        out_specs=pl.BlockSpec((SEQ, POOL_CH), lambda g: (0, g)),
        out_shape=SDS((SEQ, HALF), bf16))(z0, z0, pool_w, pool_scale)


def _pool_bwd(z0, dcat, pool_w, pool_scale):
    def body(a_ref, gate_ref, w_ref, scale_ref, dya_ref, da_ref, dgate_ref, dw_ref, dscale_ref):
        row, width, count = _pool_count(pl.program_id(0))
        a = a_ref[...]
        pooled = (_trailing_sum(a, row, width) / count - a).astype(bf16)
        w = w_ref[...]
        scale = scale_ref[...]
        mixed = _dot(pooled, w)
        silu, dsilu = _silu_and_grad(gate_ref[...])
        dya = dya_ref[...]
        dgate_ref[...] = (dya * mixed * scale * dsilu).astype(bf16)
        dms = dya * silu
        dscale_ref[...] = jnp.sum(dms * mixed, axis=0, keepdims=True)
        dmixed = (dms * scale).astype(bf16)
        dw_ref[...] = _dot_tn(pooled, dmixed)
        dpooled = _dot_nt(dmixed, w)
        da_ref[...] = (_leading_sum(dpooled / count, row, width) - dpooled).astype(bf16)

    a_in, a_gate, w, scale = _pool_specs()
    col = pl.BlockSpec((SEQ, POOL_CH), lambda g: (0, g))
    return pl.pallas_call(
        body, name="pool_bwd", grid=(4,), in_specs=[a_in, a_gate, w, scale, col],
        out_specs=[col, col, w, scale],
        out_shape=[SDS((SEQ, HALF), bf16), SDS((SEQ, HALF), bf16), SDS((4, POOL_CH, POOL_CH), f32), SDS((1, HALF), f32)],
    )(z0, z0, pool_w, pool_scale, dcat)


Q_COL, K_COL, V_COL, BGATE_COL = 16, 40, 64, 88


def _rope_tables():
    pos = jnp.arange(SEQ, dtype=f32)
    inv_freq = jnp.power(ROPE_THETA, -jnp.arange(0, 2 * ROT_HALF, 2, dtype=f32) / (2 * ROT_HALF))
    ang = pos[:, None] * inv_freq[None, :]
    cos, sin = jnp.cos(ang), jnp.sin(ang)
    zeros = jnp.zeros((SEQ, HEAD_DIM - 2 * ROT_HALF), f32)
    cos_t = jnp.concatenate([cos, cos, zeros + 1.0], axis=1)
    sin_t = jnp.concatenate([sin, sin, zeros], axis=1)
    j = jnp.arange(HEAD_DIM)[:, None]
    i = jnp.arange(HEAD_DIM)[None, :]
    rot = jnp.where((i < ROT_HALF) & (j == i + ROT_HALF), -1.0, 0.0) + jnp.where(
        (i >= ROT_HALF) & (i < 2 * ROT_HALF) & (j == i - ROT_HALF), 1.0, 0.0)
    return cos_t, sin_t, rot.astype(bf16), rot.T.astype(bf16)


def _exact_dot(t, m):
    hi = t.astype(bf16)
    lo = (t - hi.astype(f32)).astype(bf16)
    return _dot(hi, m) + _dot(lo, m)


def _rope(t, cos_t, sin_t, rot):
    return t * cos_t + _exact_dot(t, rot) * sin_t


def _rope_transposed(d, cos_t, sin_t, rot_t):
    return d * cos_t + _exact_dot(d * sin_t, rot_t)


ROW_CHUNK = 256


def _chunks(fn):
    def step(i, carry):
        fn(pl.multiple_of(i * ROW_CHUNK, ROW_CHUNK))
        return carry

    lax.fori_loop(0, SEQ // ROW_CHUNK, step, 0)


def _pieces(dilation):
    length = SEQ // dilation
    n = min(length, ROW_CHUNK)
    return [(r, l0, n) for r in range(dilation) for l0 in range(0, length, n)]


def _by_residue(dst_ref, src_ref, dilation, dtype):
    length = SEQ // dilation
    for r, l0, n in _pieces(dilation):
        src = src_ref[l0:l0 + n, :] if dilation == 1 else src_ref[pl.ds(r + dilation * l0, n, stride=dilation), :]
        start = r * length + l0
        dst_ref[start:start + n, :] = src.astype(dtype)


def _by_position(dst_ref, src_ref, dilation):
    length = SEQ // dilation
    for r, l0, n in _pieces(dilation):
        src = src_ref[r * length + l0:r * length + l0 + n, :]
        if dilation == 1:
            dst_ref[l0:l0 + n, :] = src
        else:
            dst_ref[pl.ds(r + dilation * l0, n, stride=dilation), :] = src


def _attn_masks():
    qi = lax.broadcasted_iota(jnp.int32, (SPAN, 2 * SPAN), 0)
    kj = lax.broadcasted_iota(jnp.int32, (SPAN, 2 * SPAN), 1)
    window = ((kj < SPAN) & (kj >= qi)) | ((kj >= SPAN) & (kj - SPAN <= qi))
    own = lax.broadcasted_iota(jnp.int32, (SPAN, SPAN), 1) <= lax.broadcasted_iota(jnp.int32, (SPAN, SPAN), 0)
    return window, own


def _attn_blocks(dilation):
    per_residue = SEQ // dilation // SPAN
    blocks = [(c, c % per_residue != 0) for c in range(SEQ // SPAN)]
    return [blocks[i:i + 4] for i in range(0, len(blocks), 4)]


def _block_keys(c, has_prev):
    return slice((c - 1) * SPAN if has_prev else c * SPAN, (c + 1) * SPAN)


def _head_spec(col):
    return pl.BlockSpec((SEQ, HEAD_DIM), lambda h: (0, col + h))


def _table_spec():
    return pl.BlockSpec((SEQ, HEAD_DIM), lambda h: (0, 0))


def _attn_fwd(z0, tables):
    scale = HEAD_DIM ** -0.5

    def body(*refs):
        qkv = refs[0:9]
        bg_ref, cos_ref, sin_ref, rot_ref = refs[9:13]
        yb_ref, att_ref, lse_ref = refs[13:16]
        saved = refs[16:25]
        tmp_q, tmp_k, v_ones, o_res, l_res, o_nat, l_nat = refs[25:32]
        window_mask, own_mask = _attn_masks()
        rot = rot_ref[...]

        @pl.when(pl.program_id(0) == 0)
        def _():
            v_ones[:, HEAD_DIM:] = jnp.ones((SEQ, HEAD_DIM), bf16)

        for g, dilation in enumerate(DILATIONS):
            q_ref, k_ref, v_ref = qkv[3 * g:3 * g + 3]
            qd, kd, vd = saved[3 * g:3 * g + 3]

            def rope_rows(start, q_ref=q_ref, k_ref=k_ref):
                r = pl.ds(start, ROW_CHUNK)
                cos_t, sin_t = cos_ref[r, :], sin_ref[r, :]
                tmp_q[r, :] = _rope(q_ref[r, :], cos_t, sin_t, rot) * scale
                tmp_k[r, :] = _rope(k_ref[r, :], cos_t, sin_t, rot)

            _chunks(rope_rows)
            _by_residue(qd, tmp_q, dilation, bf16)
            _by_residue(kd, tmp_k, dilation, bf16)
            _by_residue(vd, v_ref, dilation, bf16)
            for l0 in range(0, SEQ, ROW_CHUNK):
                v_ones[l0:l0 + ROW_CHUNK, 0:HEAD_DIM] = vd[l0:l0 + ROW_CHUNK, :]

            for four in _attn_blocks(dilation):
                scores = [_dot_nt(qd[c * SPAN:(c + 1) * SPAN, :], kd[_block_keys(c, prev), :]) for c, prev in four]
                tops, probs = [], []
                for (c, prev), s in zip(four, scores):
                    s = jnp.where(window_mask if prev else own_mask, s, NEG)
                    tops.append(jnp.max(s, axis=1, keepdims=True))
                    probs.append(jnp.exp(s - tops[-1]).astype(bf16))
                sums = [_dot(p, v_ones[_block_keys(c, prev), :]) for (c, prev), p in zip(four, probs)]
                for (c, prev), m, o in zip(four, tops, sums):
                    den = o[:, HEAD_DIM:]
                    o_res[c * SPAN:(c + 1) * SPAN, :] = o[:, :HEAD_DIM] / den
                    l_res[c * SPAN:(c + 1) * SPAN, :] = m + jnp.log(den)

            if dilation > 1:
                _by_position(o_nat, o_res, dilation)
                _by_position(l_nat, l_res, dilation)
            o_g, l_g = (o_res, l_res) if dilation == 1 else (o_nat, l_nat)

            def merge(start, g=g, o_g=o_g, l_g=l_g):
                r = pl.ds(start, ROW_CHUNK)
                if g == 0:
                    att, total = o_g[r, :], l_g[r, :]
                else:
                    l_old, l_new = lse_ref[r, :], l_g[r, :]
                    top = jnp.maximum(l_old, l_new)
                    total = top + jnp.log(jnp.exp(l_old - top) + jnp.exp(l_new - top))
                    att = att_ref[r, :] * jnp.exp(l_old - total) + o_g[r, :] * jnp.exp(l_new - total)
                att_ref[r, :] = att
                lse_ref[r, :] = total
                if g == len(DILATIONS) - 1:
                    gate = bg_ref[r, :]
                    yb_ref[r, :] = (att * gate * _sigmoid(gate)).astype(bf16)

            _chunks(merge)

    in_specs = []
    for g in range(3):
        in_specs += [_head_spec(Q_COL + 8 * g), _head_spec(K_COL + 8 * g), _head_spec(V_COL + 8 * g)]
    in_specs += [_head_spec(BGATE_COL), _table_spec(), _table_spec(), pl.BlockSpec((HEAD_DIM, HEAD_DIM), lambda h: (0, 0))]
    out_spec = pl.BlockSpec((SEQ, HEAD_DIM), lambda h: (0, h))
    vm = lambda dt: pltpu.VMEM((SEQ, HEAD_DIM), dt)
    cos_t, sin_t, rot, _ = tables
    out = pl.pallas_call(
        body, name="attn_fwd", grid=(N_HEADS,), in_specs=in_specs, out_specs=[out_spec] * 12,
        out_shape=[SDS((SEQ, HALF), bf16), SDS((SEQ, HALF), f32), SDS((SEQ, HALF), f32)] + [SDS((SEQ, HALF), bf16)] * 9,
        scratch_shapes=[vm(f32), vm(f32), pltpu.VMEM((SEQ, 2 * HEAD_DIM), bf16), vm(f32), vm(f32), vm(f32), vm(f32)],
    )(*([z0] * 10), cos_t, sin_t, rot)
    return out[0], out[1], out[2], [tuple(out[3 + 3 * g:6 + 3 * g]) for g in range(3)]


def _attn_bwd_group(g, saved, z0, att, lse, dcat, tables):
    scale = HEAD_DIM ** -0.5
    dilation = DILATIONS[g]
    with_gate = g == 0

    def body(*refs):
        qd, kd, vd, bg_ref, att_ref, lse_ref, dyb_ref, cos_ref, sin_ref, rot_t_ref = refs[0:10]
        n_out = 4 if with_gate else 3
        dq_ref, dk_ref, dv_ref = refs[10:13]
        dod, ld, dd, tmp, aq, ak, av = refs[10 + n_out:17 + n_out]
        window_mask, own_mask = _attn_masks()
        rot_t = rot_t_ref[...]

        def gate_rows(start):
            r = pl.ds(start, ROW_CHUNK)
            silu, dsilu = _silu_and_grad(bg_ref[r, :])
            att_v = att_ref[r, :]
            dyb = dyb_ref[r, :]
            if with_gate:
                refs[13][r, :] = (dyb * att_v * dsilu).astype(bf16)
            datt = dyb * silu
            tmp[r, :] = datt
            aq[r, :] = jnp.broadcast_to(jnp.sum(datt * att_v, axis=1, keepdims=True), (ROW_CHUNK, HEAD_DIM))

        _chunks(gate_rows)
        _by_residue(dod, tmp, dilation, bf16)
        _by_residue(dd, aq, dilation, f32)
        _by_residue(ld, lse_ref, dilation, f32)

        for four in _attn_blocks(dilation):
            rows = [slice(c * SPAN, (c + 1) * SPAN) for c, _ in four]
            keys = [_block_keys(c, prev) for c, prev in four]
            scores = [_dot_nt(qd[r, :], kd[k, :]) for r, k in zip(rows, keys)]
            dprobs = [_dot_nt(dod[r, :], vd[k, :]) for r, k in zip(rows, keys)]
            probs, dscores = [], []
            for (c, prev), r, s, dp in zip(four, rows, scores, dprobs):
                lse_q, delta = ld[r, :], dd[r, :]
                if prev:
                    lse_q = jnp.concatenate([lse_q, lse_q], axis=1)
                    delta = jnp.concatenate([delta, delta], axis=1)
                p = jnp.where(window_mask if prev else own_mask, jnp.exp(s - lse_q), 0.0)
                probs.append(p.astype(bf16))
                dscores.append((p * (dp - delta)).astype(bf16))
            dvs = [_dot_tn(p, dod[r, :]) for p, r in zip(probs, rows)]
            dks = [_dot_tn(ds, qd[r, :]) for ds, r in zip(dscores, rows)]
            dqs = [_dot(ds, kd[k, :]) for ds, k in zip(dscores, keys)]
            for (c, prev), r, dv, dk, dq in zip(four, rows, dvs, dks, dqs):
                aq[r, :] = dq
                if prev:
                    before = slice((c - 1) * SPAN, c * SPAN)
                    av[before, :] += dv[0:SPAN]
                    ak[before, :] += dk[0:SPAN]
                    av[r, :] = dv[SPAN:]
                    ak[r, :] = dk[SPAN:]
                else:
                    av[r, :] = dv
                    ak[r, :] = dk

        def finish(out_ref, acc, factor, roped):
            if dilation > 1:
                _by_position(tmp, acc, dilation)
            src = acc if dilation == 1 else tmp

            def rows(start):
                r = pl.ds(start, ROW_CHUNK)
                d = src[r, :]
                if factor != 1.0:
                    d = d * factor
                if roped:
                    d = _rope_transposed(d, cos_ref[r, :], sin_ref[r, :], rot_t)
                out_ref[r, :] = d.astype(bf16)

            _chunks(rows)

        finish(dq_ref, aq, scale, True)
        finish(dk_ref, ak, 1.0, True)
        finish(dv_ref, av, 1.0, False)

    head = pl.BlockSpec((SEQ, HEAD_DIM), lambda h: (0, h))
    in_specs = [head, head, head, _head_spec(BGATE_COL), head, head, _head_spec(8), _table_spec(), _table_spec(),
                pl.BlockSpec((HEAD_DIM, HEAD_DIM), lambda h: (0, 0))]
    n_out = 4 if with_gate else 3
    vm = lambda dt: pltpu.VMEM((SEQ, HEAD_DIM), dt)
    cos_t, sin_t, _, rot_t = tables
    return pl.pallas_call(
        body, name=f"attn_bwd_g{g}", grid=(N_HEADS,), in_specs=in_specs, out_specs=[head] * n_out,
        out_shape=[SDS((SEQ, HALF), bf16)] * n_out,
        scratch_shapes=[vm(bf16), vm(f32), vm(f32), vm(f32), vm(f32), vm(f32), vm(f32)],
    )(*saved, z0, att, lse, dcat, cos_t, sin_t, rot_t)


def _sgu_specs():
    chunk = lambda col: pl.BlockSpec((CHUNK, HALF), lambda n: (n, col))
    vec = pl.BlockSpec((1, HALF), lambda n: (0, 0))
    w = pl.BlockSpec((4, CHUNK, CHUNK), lambda n: (0, 0, 0))
    bias = pl.BlockSpec((CHUNK, CHUNK), lambda n: (0, 0))
    return chunk, vec, w, bias


def _sgu_weights(w_ref):
    tril = lax.broadcasted_iota(jnp.int32, (CHUNK, CHUNK), 1) <= lax.broadcasted_iota(jnp.int32, (CHUNK, CHUNK), 0)
    return tril, [jnp.where(tril, w_ref[h], 0.0).astype(bf16) for h in range(4)]


def _sgu_fwd(z1, ln_g, ln_b, sgu_w, bias_t):
    def body(u_ref, v_ref, cg_ref, g_ref, b_ref, w_ref, bias_ref, yc_ref):
        _, ws = _sgu_weights(w_ref)
        xh, _ = _ln_stats(v_ref[...])
        vn = (xh * g_ref[...] + b_ref[...]).astype(bf16)
        for h in range(4):
            cols = slice(h * POOL_CH, (h + 1) * POOL_CH)
            s = _dot(ws[h], vn[:, cols]) + bias_ref[:, h:h + 1]
            gate = cg_ref[:, cols]
            yc_ref[:, cols] = (u_ref[:, cols] * s * gate * _sigmoid(gate)).astype(bf16)

    chunk, vec, w, bias = _sgu_specs()
    return pl.pallas_call(
        body, name="sgu_fwd", grid=(SEQ // CHUNK,),
        in_specs=[chunk(0), chunk(1), chunk(2), vec, vec, w, bias], out_specs=chunk(0),
        out_shape=SDS((SEQ, HALF), bf16))(z1, z1, z1, ln_g, ln_b, sgu_w, bias_t)


def _sgu_bwd(z1, dcat, ln_g, ln_b, sgu_w, bias_t):
    def body(u_ref, v_ref, cg_ref, dyc_ref, g_ref, b_ref, w_ref, bias_ref,
             du_ref, dv_ref, dcg_ref, dw_ref, dbias_ref, dg_ref, db_ref, dvn_ref):
        first = pl.program_id(0) == 0
        tril, ws = _sgu_weights(w_ref)
        xh, rstd = _ln_stats(v_ref[...])
        g = g_ref[...]
        vn = (xh * g + b_ref[...]).astype(bf16)

        @pl.when(first)
        def _():
            dbias_ref[...] = jnp.zeros((CHUNK, CHUNK), f32)

        for h in range(4):
            cols = slice(h * POOL_CH, (h + 1) * POOL_CH)
            vn_h = vn[:, cols]
            s = _dot(ws[h], vn_h) + bias_ref[:, h:h + 1]
            silu, dsilu = _silu_and_grad(cg_ref[:, cols])
            dyc = dyc_ref[:, cols]
            u = u_ref[:, cols]
            du_ref[:, cols] = (dyc * s * silu).astype(bf16)
            dcg_ref[:, cols] = (dyc * u * s * dsilu).astype(bf16)
            ds = dyc * u * silu
            dbias_ref[:, h:h + 1] += jnp.sum(ds, axis=1, keepdims=True)
            ds = ds.astype(bf16)
            _accumulate(dw_ref.at[h], jnp.where(tril, _dot_nt(ds, vn_h), 0.0), first)
            dvn_ref[:, cols] = _dot_tn(ws[h], ds)
        dv, dg, db = _ln_bwd(xh, rstd, g, dvn_ref[...])
        dv_ref[...] = dv.astype(bf16)
        _accumulate(dg_ref, dg, first)
        _accumulate(db_ref, db, first)

    chunk, vec, w, bias = _sgu_specs()
    return pl.pallas_call(
        body, name="sgu_bwd", grid=(SEQ // CHUNK,),
        in_specs=[chunk(0), chunk(1), chunk(2), chunk(0), vec, vec, w, bias],
        out_specs=[chunk(0), chunk(0), chunk(0), w, bias, vec, vec],
        out_shape=[SDS((SEQ, HALF), bf16)] * 3 + [SDS((4, CHUNK, CHUNK), f32), SDS((CHUNK, CHUNK), f32),
                                                   SDS((1, HALF), f32), SDS((1, HALF), f32)],
        scratch_shapes=[pltpu.VMEM((CHUNK, HALF), f32)],
    )(z1, z1, z1, dcat, ln_g, ln_b, sgu_w, bias_t)


CONV_TILE = 128
DVAL_COL, DGLU_COL = 12, 16


def _conv_specs():
    val = pl.BlockSpec((SEQ, POOL_CH), lambda j: (0, DVAL_COL + j))
    glu = pl.BlockSpec((SEQ, POOL_CH), lambda j: (0, DGLU_COL + j))
    w = pl.BlockSpec((CONV_K, POOL_CH), lambda j: (0, j))
    col = pl.BlockSpec((SEQ, POOL_CH), lambda j: (0, j))
    vec = pl.BlockSpec((1, POOL_CH), lambda j: (0, j))
    return val, glu, w, col, vec


def _conv_fwd(z1, conv_w, conv_b):
    def body(val_ref, glu_ref, w_ref, b_ref, out_ref, xpad):
        xpad[0:CONV_PAD, :] = jnp.zeros((CONV_PAD, POOL_CH), f32)
        xpad[CONV_PAD:, :] = val_ref[...] * _sigmoid(glu_ref[...])
        w = w_ref[...]
        bias = b_ref[...]

        def tile(i, carry):
            t0 = pl.multiple_of(i * CONV_TILE, CONV_TILE)
            window = xpad[pl.ds(t0, CONV_TILE + CONV_PAD), :]
            acc = jnp.broadcast_to(bias, (CONV_TILE, POOL_CH))
            for k in range(CONV_K):
                shift = CONV_PAD - (CONV_K - 1) + k
                acc = acc + w[k:k + 1, :] * pltpu.roll(window, CONV_TILE + CONV_PAD - shift, 0)[0:CONV_TILE]
            out_ref[pl.ds(t0, CONV_TILE), :] = acc
            return carry

        lax.fori_loop(0, SEQ // CONV_TILE, tile, 0)

    val, glu, w, col, vec = _conv_specs()
    return pl.pallas_call(
        body, name="conv_fwd", grid=(4,), in_specs=[val, glu, w, vec], out_specs=col,
        out_shape=SDS((SEQ, HALF), f32), scratch_shapes=[pltpu.VMEM((SEQ + CONV_PAD, POOL_CH), f32)],
    )(z1, z1, conv_w, conv_b)


def _conv_bwd(z1, dconv, conv_w):
    def body(val_ref, glu_ref, w_ref, dout_ref, dval_ref, dglu_ref, dw_ref, db_ref, xpad, dpad, dx_ref):
        val = val_ref[...]
        sig = _sigmoid(glu_ref[...])
        xpad[0:CONV_PAD, :] = jnp.zeros((CONV_PAD, POOL_CH), f32)
        xpad[CONV_PAD:, :] = val * sig
        dout = dout_ref[...]
        dpad[0:SEQ, :] = dout
        dpad[SEQ:, :] = jnp.zeros((CONV_PAD, POOL_CH), f32)
        db_ref[...] = jnp.sum(dout, axis=0, keepdims=True)
        dw_ref[...] = jnp.zeros((CONV_K, POOL_CH), f32)
        w = w_ref[...]

        def tile(i, carry):
            t0 = pl.multiple_of(i * CONV_TILE, CONV_TILE)
            x_win = xpad[pl.ds(t0, CONV_TILE + CONV_PAD), :]
            d_win = dpad[pl.ds(t0, CONV_TILE + CONV_PAD), :]
            d_own = d_win[0:CONV_TILE]
            acc = jnp.zeros((CONV_TILE, POOL_CH), f32)
            for k in range(CONV_K):
                shift = CONV_PAD - (CONV_K - 1) + k
                x_k = pltpu.roll(x_win, CONV_TILE + CONV_PAD - shift, 0)[0:CONV_TILE]
                dw_ref[k:k + 1, :] += jnp.sum(d_own * x_k, axis=0, keepdims=True)
                back = CONV_K - 1 - k
                d_k = d_own if back == 0 else pltpu.roll(d_win, CONV_TILE + CONV_PAD - back, 0)[0:CONV_TILE]
                acc = acc + w[k:k + 1, :] * d_k
            dx_ref[pl.ds(t0, CONV_TILE), :] = acc
            return carry

        lax.fori_loop(0, SEQ // CONV_TILE, tile, 0)
        dx = dx_ref[...]
        dval_ref[...] = (dx * sig).astype(bf16)
        dglu_ref[...] = (dx * val * sig * (1.0 - sig)).astype(bf16)

    val, glu, w, col, vec = _conv_specs()
    pad = pltpu.VMEM((SEQ + CONV_PAD, POOL_CH), f32)
    return pl.pallas_call(
        body, name="conv_bwd", grid=(4,), in_specs=[val, glu, w, col], out_specs=[col, col, w, vec],
        out_shape=[SDS((SEQ, HALF), bf16), SDS((SEQ, HALF), bf16), SDS((CONV_K, HALF), f32), SDS((1, HALF), f32)],
        scratch_shapes=[pad, pad, pltpu.VMEM((SEQ, POOL_CH), f32)],
    )(z1, z1, conv_w, dconv)


DGATE_COL = 5


def _conv_norm_fwd(conv, z1, g, b):
    def body(c_ref, gate_ref, g_ref, b_ref, yd_ref):
        xh, _ = _ln_stats(c_ref[...])
        n = xh * g_ref[...] + b_ref[...]
        gate = gate_ref[...]
        yd_ref[...] = (n * _sigmoid(n) * gate * _sigmoid(gate)).astype(bf16)

    return pl.pallas_call(
        body, name="conv_norm_fwd", grid=(SEQ // ROWS,),
        in_specs=[_row_spec(HALF), _row_spec(HALF, DGATE_COL), _vec_spec(HALF), _vec_spec(HALF)],
        out_specs=_row_spec(HALF), out_shape=SDS((SEQ, HALF), bf16))(conv, z1, g, b)


def _conv_norm_bwd(conv, z1, dcat, g, b):
    def body(c_ref, gate_ref, dyd_ref, g_ref, b_ref, dconv_ref, dgate_ref, dg_ref, db_ref):
        first = pl.program_id(0) == 0
        xh, rstd = _ln_stats(c_ref[...])
        g = g_ref[...]
        n_silu, n_dsilu = _silu_and_grad(xh * g + b_ref[...])
        gate_silu, gate_dsilu = _silu_and_grad(gate_ref[...])
        dyd = dyd_ref[...]
        dgate_ref[...] = (dyd * n_silu * gate_dsilu).astype(bf16)
        dconv, dg, db = _ln_bwd(xh, rstd, g, dyd * gate_silu * n_dsilu)
        dconv_ref[...] = dconv
        _accumulate(dg_ref, dg, first)
        _accumulate(db_ref, db, first)

    return pl.pallas_call(
        body, name="conv_norm_bwd", grid=(SEQ // ROWS,),
        in_specs=[_row_spec(HALF), _row_spec(HALF, DGATE_COL), _row_spec(HALF, 1), _vec_spec(HALF), _vec_spec(HALF)],
        out_specs=[_row_spec(HALF), _row_spec(HALF), _vec_spec(HALF), _vec_spec(HALF)],
        out_shape=[SDS((SEQ, HALF), f32), SDS((SEQ, HALF), bf16), SDS((1, HALF), f32), SDS((1, HALF), f32)],
    )(conv, z1, dcat, g, b)


def _step(x, target, w, chip):
    chip_vec = chip.astype(jnp.int32).reshape(1)
    sharded_names = list(SHARDED_SMALL)
    small_shard = _pack([w[k] for k in sharded_names], total_rows=SMALL_SHARD_ROWS)
    small_slot = lax.dynamic_update_slice(jnp.zeros((N_CHIPS, SMALL_SHARD_ROWS, LANES), f32), small_shard[None], (chip, 0, 0))
    slots = [small_slot] + [_cast_into_slot(w["e_w_in"], chip_vec, f"cast_e_w_in{i}", i, E_IN_PIECES) for i in range(E_IN_PIECES)]
    slots += [_cast_into_slot(w[k], chip_vec, f"cast_{k}") for k in BIG[1:]]
    sems, bufs, token = _gather_start(slots)
    tables = _rope_tables()

    def vec(k):
        return w[k].reshape(1, -1)

    h0 = _pre_norm(x, vec("e_pre_norm") + token[0, 0])
    after, z0, e_w_in = h0, None, []
    for i in range(E_IN_PIECES):
        group = slice(0, 2) if i == 0 else slice(1 + i, 2 + i)
        landed = _forward_halves(_gather_wait(bufs[group], sems[group], after, f"gather_wait_{i}"), f"forward_{i}")
        if i == 0:
            small_full = landed[0]
        e_w_in.append(landed[-1])
        z0 = _mm_nn(h0, landed[-1], f32, f"e_in{i}", i, E_IN_PIECES, z0)
        after = z0
    e_w_in = jnp.concatenate(e_w_in, axis=2)
    p = {k: _from_chips(k, a) for k, a in zip(sharded_names, _unpack(small_full, [SHARDED_SMALL[k][0] for k in sharded_names]))}
    for k in ("o_pre_norm", "o_sgu_norm_g", "o_sgu_norm_b", "o_conv_b", "o_conv_norm_g", "o_conv_norm_b", "o_post_norm"):
        p[k] = p[k].reshape(1, -1)
    pool_w_bf = p["e_pool_w"].astype(bf16)
    bias_t = jnp.pad(w["o_sgu_b"].T, ((0, 0), (0, CHUNK - 4)))

    ya = _pool_fwd(z0, pool_w_bf, vec("e_pool_scale"))
    yb, att, lse, qkv_by_residue = _attn_fwd(z0, tables)
    rest = slice(1 + E_IN_PIECES, 4 + E_IN_PIECES)
    e_w_out, o_w_in, o_w_out = _forward_halves(_gather_wait(bufs[rest], sems[rest], att, "gather_wait_rest"), "forward_rest")
    e_w_out = e_w_out.reshape(1, D_MODEL, D_MODEL)
    o_w_out = o_w_out.reshape(1, D_MODEL, D_MODEL)
    cat0 = jnp.concatenate([ya, yb], axis=1)
    y0 = _mm_nn(cat0, e_w_out, f32, "e_out")
    x1, h1 = _mid_norm(x, y0, vec("e_post_norm"), p["o_pre_norm"])
    z1 = _mm_nn(h1, o_w_in, f32, "o_in")
    yc = _sgu_fwd(z1, p["o_sgu_norm_g"], p["o_sgu_norm_b"], w["o_sgu_w"], bias_t)
    conv = _conv_fwd(z1, p["o_conv_w"], p["o_conv_b"])
    yd = _conv_norm_fwd(conv, z1, p["o_conv_norm_g"], p["o_conv_norm_b"])
    cat1 = jnp.concatenate([yc, yd], axis=1)
    y1 = _mm_nn(cat1, o_w_out, f32, "o_out")
    loss, dx2, dy1, g_o_post = _final_norm_loss(x1, y1, p["o_post_norm"], target)

    in_flight = {}

    def send_off(name, grad):
        sem, sums, land, tok = _scatter_start(_swap_add(grad, f"swap_add_{name}"), f"scatter_start_{name}")
        in_flight[name] = (sem, sums, land)
        return tok

    tok = send_off("o_w_out", _mm_tn(cat1, dy1, 1, "o_out_dw").reshape(N_CHIPS, HALF // 2, D_MODEL))
    dcat1 = _mm_nt(dy1, o_w_out, "o_out_dx", tok)
    du, dv, dcg, g_sgu_w, g_bias_t, g_sgu_g, g_sgu_b = _sgu_bwd(
        z1, dcat1, p["o_sgu_norm_g"] + tok[0, 0], p["o_sgu_norm_b"], w["o_sgu_w"], bias_t)
    dconv, ddgate, g_cn_g, g_cn_b = _conv_norm_bwd(conv, z1, dcat1, p["o_conv_norm_g"], p["o_conv_norm_b"])
    ddval, ddglu, g_conv_w, g_conv_b = _conv_bwd(z1, dconv, p["o_conv_w"])
    dz1 = jnp.concatenate([du, dv, dcg, ddval, ddglu, ddgate], axis=1)
    tok = send_off("o_w_in", _mm_tn(h1, dz1, N_CHIPS, "o_in_dw"))
    dh1 = _mm_nt(dz1, o_w_in, "o_in_dx", tok)
    dx1, dy0, g_o_pre, g_e_post = _mid_norm_bwd(dx2, dh1, x1, y0, p["o_pre_norm"] + tok[0, 0], vec("e_post_norm"))

    tok = send_off("e_w_out", _mm_tn(cat0, dy0, 1, "e_out_dw").reshape(N_CHIPS, HALF // 2, D_MODEL))
    dcat0 = _mm_nt(dy0, e_w_out, "e_out_dx", tok)
    da, dagate, g_pool_w, g_pool_scale = _pool_bwd(z0, dcat0, pool_w_bf, vec("e_pool_scale") + tok[0, 0])
    dq0, dk0, dv0, dbgate = _attn_bwd_group(0, qkv_by_residue[0], z0, att, lse, dcat0, tables)
    dq1, dk1, dv1 = _attn_bwd_group(1, qkv_by_residue[1], z0, att, lse, dcat0, tables)
    dq2, dk2, dv2 = _attn_bwd_group(2, qkv_by_residue[2], z0, att, lse, dcat0, tables)
    dz0 = jnp.concatenate([da, dagate, dq0, dq1, dq2, dk0, dk1, dk2, dv0, dv1, dv2, dbgate], axis=1)
    tok = send_off("e_w_in", _mm_tn(h0, dz0, N_CHIPS, "e_in_dw"))
    dh0 = _mm_nt(dz0, e_w_in, "e_in_dx", tok)
    grad_x, g_e_pre = _pre_norm_bwd(dx1, dh0, x, vec("e_pre_norm") + tok[0, 0])

    small = {"e_pre_norm": g_e_pre, "e_pool_w": g_pool_w, "e_pool_scale": g_pool_scale, "e_post_norm": g_e_post,
             "o_pre_norm": g_o_pre, "o_sgu_norm_g": g_sgu_g, "o_sgu_norm_b": g_sgu_b, "o_sgu_w": g_sgu_w,
             "o_sgu_b": g_bias_t[:, 0:4].T, "o_conv_w": g_conv_w, "o_conv_b": g_conv_b,
             "o_conv_norm_g": g_cn_g, "o_conv_norm_b": g_cn_b, "o_post_norm": g_o_post}
    return loss, grad_x, in_flight, small


def _land(in_flight, name, chip, after):
    sems, sums, land = in_flight[name]
    sums, land = _scatter_wait(sems, sums, land, after, f"scatter_wait_{name}")
    return _add_landed_join(sums, land, chip.astype(jnp.int32).reshape(1), f"add_landed_{name}")


def _place():
    x, y, c = lax.axis_index("x"), lax.axis_index("y"), lax.axis_index("c")
    others = [(1 - x, y), (x, 1 - y), (1 - x, 1 - y)]
    return x, y, c, 2 * x + y, others


def _all_gather(shards, name):
    n = len(shards)

    def body(*refs):
        ins, outs = refs[:n], refs[n:2 * n]
        send_sems, recv_sems, local_sems = refs[2 * n:]
        x, y, c, me, others = _place()
        sibling = (x, y, 1 - c)

        def half(a, chip, core):
            rows = ins[a].shape[0] // 2
            return outs[a].at[chip, pl.ds(core * rows, rows), :]

        def copy(a, k, src, dst, to):
            return pltpu.make_async_remote_copy(src_ref=src, dst_ref=dst, send_sem=send_sems.at[6 * a + k],
                                                recv_sem=recv_sems.at[6 * a + k], device_id=to, device_id_type=MESH)

        local = [pltpu.make_async_copy(ins[a], outs[a].at[me], local_sems.at[a]) for a in range(n)]
        for cp in local:
            cp.start()
        sent = []
        for a in range(n):
            rows = ins[a].shape[0] // 2
            mine = ins[a].at[pl.ds(c * rows, rows), :]
            for k, (ox, oy) in enumerate(others):
                sent.append(copy(a, k, mine, half(a, me, c), (ox, oy, c)))
                sent[-1].start()
        for a in range(n):
            for k, (ox, oy) in enumerate(others):
                landed = half(a, 2 * ox + oy, c)
                copy(a, k, landed, landed, (ox, oy, c)).wait_recv()
                sent.append(copy(a, 3 + k, landed, landed, sibling))
                sent[-1].start()
        for k, (ox, oy) in enumerate(others):
            chip = 2 * ox + oy
            for a in range(n):
                theirs = half(a, chip, 1 - c)
                copy(a, 3 + k, theirs, theirs, sibling).wait_recv()
        for cp in sent:
            cp.wait_send()
        for cp in local:
            cp.wait()

    return pl.pallas_call(
        body, name=name, in_specs=[ANY] * n, out_specs=[ANY] * n,
        out_shape=[SDS((N_CHIPS,) + s.shape, s.dtype) for s in shards],
        scratch_shapes=[pltpu.SemaphoreType.DMA((6 * n,)), pltpu.SemaphoreType.DMA((6 * n,)), pltpu.SemaphoreType.DMA((n,))],
    )(*shards)


def _swap_halves(parts, name):
    n = len(parts)

    def body(*refs):
        ins, own, theirs = refs[:n], refs[n:2 * n], refs[2 * n:3 * n]
        send_sems, recv_sems, local_sems = refs[3 * n:]
        x, y, c, _, _ = _place()
        sibling = (x, y, 1 - c)
        copies = []
        for a in range(n):
            rows = ins[a].shape[1] // 2
            keep = pltpu.make_async_copy(ins[a].at[:, pl.ds(c * rows, rows), :], own[a], local_sems.at[a])
            give = pltpu.make_async_remote_copy(
                src_ref=ins[a].at[:, pl.ds((1 - c) * rows, rows), :], dst_ref=theirs[a], send_sem=send_sems.at[a],
                recv_sem=recv_sems.at[a], device_id=sibling, device_id_type=MESH)
            keep.start()
            give.start()
            copies += [keep, give]
        for cp in copies:
            cp.wait()

    half = [SDS((N_CHIPS, s.shape[1] // 2, s.shape[2]), s.dtype) for s in parts]
    out = pl.pallas_call(
        body, name=name, in_specs=[ANY] * n, out_specs=[ANY] * (2 * n), out_shape=half + half,
        scratch_shapes=[pltpu.SemaphoreType.DMA((n,)), pltpu.SemaphoreType.DMA((n,)), pltpu.SemaphoreType.DMA((n,))],
    )(*parts)
    return out[:n], out[n:]


def _scatter_chips(parts, name):
    n = len(parts)

    def body(*refs):
        ins, outs = refs[:n], refs[n:2 * n]
        send_sems, recv_sems, local_sems = refs[2 * n:]
        x, y, c, me, others = _place()

        def copy(a, k, slot_from, slot_to, chip_xy):
            return pltpu.make_async_remote_copy(
                src_ref=ins[a].at[slot_from], dst_ref=outs[a].at[slot_to], send_sem=send_sems.at[3 * a + k],
                recv_sem=recv_sems.at[3 * a + k], device_id=(chip_xy[0], chip_xy[1], c), device_id_type=MESH)

        keeps, gives = [], []
        for a in range(n):
            keeps.append(pltpu.make_async_copy(ins[a].at[me], outs[a].at[me], local_sems.at[a]))
            keeps[-1].start()
            for k, (ox, oy) in enumerate(others):
                gives.append(copy(a, k, 2 * ox + oy, me, (ox, oy)))
                gives[-1].start()
        for a in range(n):
            for k, (ox, oy) in enumerate(others):
                copy(a, k, me, 2 * ox + oy, (ox, oy)).wait_recv()
        for cp in gives:
            cp.wait_send()
        for cp in keeps:
            cp.wait()

    return pl.pallas_call(
        body, name=name, in_specs=[ANY] * n, out_specs=[ANY] * n, out_shape=[SDS(s.shape, s.dtype) for s in parts],
        scratch_shapes=[pltpu.SemaphoreType.DMA((3 * n,)), pltpu.SemaphoreType.DMA((3 * n,)), pltpu.SemaphoreType.DMA((n,))],
    )(*parts)


def _join_halves(halves, name):
    n = len(halves)

    def body(*refs):
        ins, outs = refs[:n], refs[n:2 * n]
        send_sems, recv_sems, local_sems = refs[2 * n:]
        x, y, c, _, _ = _place()

        def copy(a, core):
            rows = ins[a].shape[0]
            return pltpu.make_async_remote_copy(
                src_ref=ins[a], dst_ref=outs[a].at[pl.ds(core * rows, rows), :], send_sem=send_sems.at[a],
                recv_sem=recv_sems.at[a], device_id=(x, y, 1 - c), device_id_type=MESH)

        keeps, gives = [], []
        for a in range(n):
            rows = ins[a].shape[0]
            keeps.append(pltpu.make_async_copy(ins[a], outs[a].at[pl.ds(c * rows, rows), :], local_sems.at[a]))
            gives.append(copy(a, c))
            keeps[-1].start()
            gives[-1].start()
        for a in range(n):
            copy(a, 1 - c).wait_recv()
        for cp in gives:
            cp.wait_send()
        for cp in keeps:
            cp.wait()

    return pl.pallas_call(
        body, name=name, in_specs=[ANY] * n, out_specs=[ANY] * n,
        out_shape=[SDS((2 * s.shape[0], s.shape[1]), s.dtype) for s in halves],
        scratch_shapes=[pltpu.SemaphoreType.DMA((n,)), pltpu.SemaphoreType.DMA((n,)), pltpu.SemaphoreType.DMA((n,))],
    )(*halves)


def _add_pair(a, b, name):
    _, r, c = a.shape
    tr = 256 if r % 256 == 0 else r // 2 if r > 512 else r

    def body(a_ref, b_ref, o_ref):
        o_ref[...] = (a_ref[...].astype(f32) + b_ref[...].astype(f32)).astype(o_ref.dtype)

    spec = pl.BlockSpec((None, tr, c), lambda j, i: (j, i, 0))
    return pl.pallas_call(body, name=name, grid=(N_CHIPS, r // tr), in_specs=[spec, spec], out_specs=spec,
                          out_shape=SDS(a.shape, a.dtype))(a, b)


def _add_chips(u, name):
    _, r, c = u.shape
    tr = 256 if r % 256 == 0 else r

    def body(u_ref, o_ref):
        o_ref[...] = ((u_ref[0].astype(f32) + u_ref[1].astype(f32)) + u_ref[2].astype(f32)) + u_ref[3].astype(f32)

    return pl.pallas_call(
        body, name=name, grid=(r // tr,), in_specs=[pl.BlockSpec((N_CHIPS, tr, c), lambda i: (0, i, 0))],
        out_specs=pl.BlockSpec((tr, c), lambda i: (i, 0)), out_shape=SDS((r, c), f32))(u)


SWAP_ROWS = 256


def _swap_add(g, name):
    chips, r, c = g.shape
    half = r // 2
    rows_per_step = 2 * SWAP_ROWS if half % (2 * SWAP_ROWS) == 0 else SWAP_ROWS
    nb = half // rows_per_step
    steps = chips * nb

    def body(core_ref, mine_ref, theirs_ref, out_ref, landing, send_sems, recv_sems, free_sems):
        i = pl.program_id(0)
        slot = i % 2
        x, y, core, _, _ = _place()
        sibling = (x, y, 1 - core)

        @pl.when(i >= 2)
        def _():
            pl.semaphore_wait(free_sems.at[slot], 1)

        send = pltpu.make_async_remote_copy(src_ref=theirs_ref, dst_ref=landing.at[slot], send_sem=send_sems.at[slot],
                                            recv_sem=recv_sems.at[slot], device_id=sibling, device_id_type=MESH)
        send.start()
        send.wait_recv()
        out_ref[...] = (mine_ref[...].astype(f32) + landing[slot].astype(f32)).astype(out_ref.dtype)

        @pl.when(i + 2 < steps)
        def _():
            pl.semaphore_signal(free_sems.at[slot], 1, device_id=sibling, device_id_type=MESH)

        send.wait_send()

    block = (rows_per_step, c)
    grid_spec = pltpu.PrefetchScalarGridSpec(
        num_scalar_prefetch=1, grid=(steps,),
        in_specs=[pl.BlockSpec(block, lambda i, core: ((2 * (i // nb) + core[0]) * nb + i % nb, 0)),
                  pl.BlockSpec(block, lambda i, core: ((2 * (i // nb) + 1 - core[0]) * nb + i % nb, 0))],
        out_specs=pl.BlockSpec(block, lambda i, core: (i, 0)),
        scratch_shapes=[pltpu.VMEM((2, rows_per_step, c), g.dtype), pltpu.SemaphoreType.DMA((2,)),
                        pltpu.SemaphoreType.DMA((2,)), pltpu.SemaphoreType.REGULAR((2,))])
    core = lax.axis_index("c").astype(jnp.int32).reshape(1)
    rows = g.reshape(chips * r, c)
    out = pl.pallas_call(body, name=name, grid_spec=grid_spec, out_shape=SDS((chips * half, c), g.dtype))(core, rows, rows)
    return out.reshape(chips, half, c)


def _reduce_scatter(parts, tag):
    own, theirs = _swap_halves(parts, f"swap_halves_{tag}")
    chip_sums = [_add_pair(o, t, f"add_cores_{tag}{i}") for i, (o, t) in enumerate(zip(own, theirs))]
    gathered = _scatter_chips(chip_sums, f"scatter_chips_{tag}")
    halves = [_add_chips(u, f"add_chips_{tag}{i}") for i, u in enumerate(gathered)]
    return _join_halves(halves, f"join_halves_{tag}")


HBM = pl.BlockSpec(memory_space=pltpu.HBM)
SEM = pl.BlockSpec(memory_space=pltpu.SEMAPHORE)
EFFECT = pltpu.SideEffectType.DATAFLOW_SIDE_EFFECTING


def _in_hbm(a):
    return pltpu.with_memory_space_constraint(a, pltpu.HBM)


def _cast_into_slot(w, chip, name, piece=0, pieces=1):
    r, c = w.shape
    c = c // pieces
    nb = r // SWAP_ROWS

    def body(chip_ref, w_ref, o_ref):
        o_ref[...] = w_ref[...].astype(bf16)

    grid_spec = pltpu.PrefetchScalarGridSpec(
        num_scalar_prefetch=1, grid=(nb,),
        in_specs=[pl.BlockSpec((SWAP_ROWS, c), lambda i, chip: (i, piece))],
        out_specs=pl.BlockSpec((SWAP_ROWS, c), lambda i, chip: (chip[0] * nb + i, 0)))
    out = pl.pallas_call(body, name=name, grid_spec=grid_spec, out_shape=SDS((N_CHIPS * r, c), bf16))(chip, w)
    return out.reshape(N_CHIPS, r, c)


def _gather_start(bufs):
    n = len(bufs)

    def body(*refs):
        ins, sems, token = refs[:n], refs[n:3 * n], refs[4 * n]
        x, y, c, me, others = _place()
        for a in range(n):
            rows = ins[a].shape[1] // 2
            mine = ins[a].at[me, pl.ds(c * rows, rows), :]
            for k, (ox, oy) in enumerate(others):
                pltpu.make_async_remote_copy(src_ref=mine, dst_ref=mine, send_sem=sems[2 * a].at[k],
                                             recv_sem=sems[2 * a + 1].at[k], device_id=(ox, oy, c),
                                             device_id_type=MESH).start()
        token[...] = jnp.zeros_like(token)

    out = pl.pallas_call(
        body, name="gather_start", in_specs=[HBM] * n,
        out_shape=(*[pltpu.SemaphoreType.DMA((3,))] * (2 * n), *[pltpu.HBM(b.shape, b.dtype) for b in bufs],
                   SDS((8, 128), f32)),
        out_specs=(*[SEM] * (2 * n), *[HBM] * n, pl.BlockSpec(memory_space=pltpu.VMEM)),
        input_output_aliases={a: 2 * n + a for a in range(n)},
        compiler_params=pltpu.CompilerParams(has_side_effects=EFFECT),
    )(*[_in_hbm(b) for b in bufs])
    return [(out[2 * a], out[2 * a + 1]) for a in range(n)], list(out[2 * n:3 * n]), out[3 * n]


def _gather_wait(bufs, sems, after, name):
    n = len(bufs)

    def body(*refs):
        ins, sem_refs = refs[:n], refs[n:3 * n]
        x, y, c, me, others = _place()
        for a in range(n):
            rows = ins[a].shape[1] // 2
            mine = ins[a].at[me, pl.ds(c * rows, rows), :]
            for k, (ox, oy) in enumerate(others):
                landed = ins[a].at[2 * ox + oy, pl.ds(c * rows, rows), :]
                copy = pltpu.make_async_remote_copy(src_ref=mine, dst_ref=landed, send_sem=sem_refs[2 * a].at[k],
                                                    recv_sem=sem_refs[2 * a + 1].at[k], device_id=(ox, oy, c),
                                                    device_id_type=MESH)
                copy.wait_send()
                copy.wait_recv()

    flat_sems = [s for pair in sems for s in pair]
    out = pl.pallas_call(
        body, name=name, in_specs=[HBM] * n + [SEM] * (2 * n) + [ANY],
        out_shape=tuple(pltpu.HBM(b.shape, b.dtype) for b in bufs), out_specs=tuple([HBM] * n),
        input_output_aliases={a: a for a in range(n)},
        compiler_params=pltpu.CompilerParams(has_side_effects=EFFECT),
    )(*bufs, *flat_sems, after)
    return list(out)


def _forward_halves(bufs, name):
    n = len(bufs)
    blocks = []
    for b in bufs:
        half = b.shape[1] // 2
        tr = SWAP_ROWS if half % SWAP_ROWS == 0 else half
        blocks.append((half, tr))
    work = [(a, k, b) for a in range(n) for k in range(3) for b in range(blocks[a][0] // blocks[a][1])]

    def body(*refs):
        outs, stages = refs[n:2 * n], refs[2 * n:3 * n]
        load_sems, send_sems, recv_sems = refs[3 * n:]
        x, y, c, me, others = _place()
        sibling = (x, y, 1 - c)

        def rows(item):
            a, k, b = item
            half, tr = blocks[a]
            ox, oy = others[k]
            return outs[a].at[2 * ox + oy, pl.ds(c * half + b * tr, tr), :]

        def load(s, item):
            return pltpu.make_async_copy(rows(item), stages[item[0]].at[s], load_sems.at[s])

        def send(s, item):
            return pltpu.make_async_remote_copy(src_ref=stages[item[0]].at[s], dst_ref=rows(item), send_sem=send_sems.at[s],
                                                recv_sem=recv_sems.at[item[0]], device_id=sibling, device_id_type=MESH)

        load(0, work[0]).start()
        for t, item in enumerate(work):
            s = t % 2
            load(s, item).wait()
            send(s, item).start()
            if t + 1 < len(work):
                if t >= 1:
                    send(1 - s, work[t - 1]).wait_send()
                load(1 - s, work[t + 1]).start()
        if len(work) > 1:
            send(len(work) % 2, work[-2]).wait_send()
        send((len(work) - 1) % 2, work[-1]).wait_send()
        for a in range(n):
            theirs = outs[a].at[pl.ds(0, 3), pl.ds(0, blocks[a][0]), :]
            pltpu.make_async_remote_copy(src_ref=theirs, dst_ref=theirs, send_sem=send_sems.at[0], recv_sem=recv_sems.at[a],
                                         device_id=sibling, device_id_type=MESH).wait_recv()

    out = pl.pallas_call(
        body, name=name, in_specs=[ANY] * n, out_specs=[ANY] * n, out_shape=[SDS(b.shape, b.dtype) for b in bufs],
        input_output_aliases={a: a for a in range(n)},
        scratch_shapes=[pltpu.VMEM((2, blocks[a][1], bufs[a].shape[2]), bufs[a].dtype) for a in range(n)]
        + [pltpu.SemaphoreType.DMA((2,)), pltpu.SemaphoreType.DMA((2,)), pltpu.SemaphoreType.DMA((n,))],
    )(*bufs)
    return list(out)


def _scatter_start(chip_sums, name):
    def body(a_ref, land_ref, send_sems, recv_sems, a_thru, land_thru, token):
        x, y, c, me, others = _place()
        for k, (ox, oy) in enumerate(others):
            pltpu.make_async_remote_copy(src_ref=a_ref.at[2 * ox + oy], dst_ref=land_ref.at[me], send_sem=send_sems.at[k],
                                         recv_sem=recv_sems.at[k], device_id=(ox, oy, c), device_id_type=MESH).start()
        token[...] = jnp.zeros_like(token)

    shape = pltpu.HBM(chip_sums.shape, chip_sums.dtype)
    send, recv, a_thru, land, token = pl.pallas_call(
        body, name=name, in_specs=[HBM, HBM],
        out_shape=(pltpu.SemaphoreType.DMA((3,)), pltpu.SemaphoreType.DMA((3,)), shape, shape, SDS((8, 128), f32)),
        out_specs=(SEM, SEM, HBM, HBM, pl.BlockSpec(memory_space=pltpu.VMEM)), input_output_aliases={0: 2, 1: 3},
        compiler_params=pltpu.CompilerParams(has_side_effects=EFFECT),
    )(_in_hbm(chip_sums), _in_hbm(lax.empty(chip_sums.shape, chip_sums.dtype)))
    return (send, recv), a_thru, land, token


def _scatter_wait(sems, chip_sums, land, after, name):
    def body(a_ref, land_ref, send_sems, recv_sems, after_ref, a_out, land_out):
        x, y, c, me, others = _place()
        for k, (ox, oy) in enumerate(others):
            copy = pltpu.make_async_remote_copy(
                src_ref=a_ref.at[2 * ox + oy], dst_ref=land_ref.at[2 * ox + oy], send_sem=send_sems.at[k],
                recv_sem=recv_sems.at[k], device_id=(ox, oy, c), device_id_type=MESH)
            copy.wait_send()
            copy.wait_recv()

    shape = pltpu.HBM(chip_sums.shape, chip_sums.dtype)
    return pl.pallas_call(
        body, name=name, in_specs=[HBM, HBM, SEM, SEM, ANY], out_shape=(shape, shape), out_specs=(HBM, HBM),
        input_output_aliases={0: 0, 1: 1}, compiler_params=pltpu.CompilerParams(has_side_effects=EFFECT),
    )(chip_sums, land, sems[0], sems[1], after)


def _add_landed_join(chip_sums, land, chip, name):
    chips, rh, c = chip_sums.shape
    nb = rh // SWAP_ROWS

    def body(chip_ref, own_ref, l1_ref, l2_ref, l3_ref, out_hbm, buf, send_sems, recv_sem, local_sems):
        i = pl.program_id(0)
        slot = i % 2
        x, y, core, _, _ = _place()
        sibling = (x, y, 1 - core)

        def copies(s, step):
            rows = pl.ds(pl.multiple_of((core * nb + step) * SWAP_ROWS, SWAP_ROWS), SWAP_ROWS)
            keep = pltpu.make_async_copy(buf.at[s], out_hbm.at[rows, :], local_sems.at[s])
            give = pltpu.make_async_remote_copy(src_ref=buf.at[s], dst_ref=out_hbm.at[rows, :], send_sem=send_sems.at[s],
                                                recv_sem=recv_sem.at[0], device_id=sibling, device_id_type=MESH)
            return keep, give

        def drain(s, step):
            keep, give = copies(s, step)
            keep.wait()
            give.wait_send()

        @pl.when(i >= 2)
        def _():
            drain(slot, i - 2)

        buf[slot] = ((own_ref[...].astype(f32) + l1_ref[...].astype(f32)) + l2_ref[...].astype(f32)) + l3_ref[...].astype(f32)
        keep, give = copies(slot, i)
        keep.start()
        give.start()

        @pl.when(i == nb - 1)
        def _():
            drain(slot, i)
            if nb > 1:
                drain(1 - slot, i - 1)
            theirs = out_hbm.at[pl.ds((1 - core) * rh, rh), :]
            pltpu.make_async_remote_copy(src_ref=theirs, dst_ref=theirs, send_sem=send_sems.at[0], recv_sem=recv_sem.at[0],
                                         device_id=sibling, device_id_type=MESH).wait_recv()

    block = (SWAP_ROWS, c)
    from_slot = lambda d: pl.BlockSpec(block, lambda i, chip: (((chip[0] + d) % chips) * nb + i, 0))
    grid_spec = pltpu.PrefetchScalarGridSpec(
        num_scalar_prefetch=1, grid=(nb,), in_specs=[from_slot(0), from_slot(1), from_slot(2), from_slot(3)],
        out_specs=ANY,
        scratch_shapes=[pltpu.VMEM((2, SWAP_ROWS, c), f32), pltpu.SemaphoreType.DMA((2,)),
                        pltpu.SemaphoreType.DMA((1,)), pltpu.SemaphoreType.DMA((2,))])
    land_rows = land.reshape(chips * rh, c)
    return pl.pallas_call(body, name=name, grid_spec=grid_spec, out_shape=SDS((2 * rh, c), f32))(
        chip, chip_sums.reshape(chips * rh, c), land_rows, land_rows, land_rows)


def _adamw_update(w_ref, g_ref, m_ref, v_ref, d_ref, nm_ref, nv_ref):
    g = g_ref[...]
    nm = ADAM_B1 * m_ref[...] + (1.0 - ADAM_B1) * g
    nv = ADAM_B2 * v_ref[...] + (1.0 - ADAM_B2) * (g * g)
    nm_ref[...] = nm
    nv_ref[...] = nv
    m_hat = nm / (1.0 - ADAM_B1 ** ADAM_STEP)
    v_hat = nv / (1.0 - ADAM_B2 ** ADAM_STEP)
    d_ref[...] = -ADAM_LR * (m_hat / (jnp.sqrt(v_hat) + ADAM_EPS) + ADAM_WD * w_ref[...])


def _adamw(w, g, m, v, name):
    r, c = w.shape
    tr = 128 if r % 128 == 0 else r
    spec = pl.BlockSpec((tr, c), lambda i: (i, 0))
    return pl.pallas_call(functools.partial(_adamw_update), name=name, grid=(r // tr,), in_specs=[spec] * 4,
                          out_specs=[spec] * 3, out_shape=[SDS((r, c), f32)] * 3)(w, g, m, v)


def _adamw_small(ws, gs, ms, vs):
    n = len(ws)

    def body(*refs):
        for i in range(n):
            _adamw_update(*refs[i:7 * n:n])

    whole = pl.BlockSpec(memory_space=pltpu.VMEM)
    out = pl.pallas_call(body, name="adamw_small", in_specs=[whole] * (4 * n), out_specs=[whole] * (3 * n),
                         out_shape=[SDS(a.shape, f32) for a in ws] * 3)(*ws, *gs, *ms, *vs)
    return out[:n], out[n:2 * n], out[2 * n:]


def _pack(arrays, total_rows=None):
    parts = []
    rows = 0
    for a in arrays:
        flat = a.reshape(-1, LANES)
        pad = -flat.shape[0] % 8
        parts.append(jnp.pad(flat, ((0, pad), (0, 0))))
        rows += flat.shape[0] + pad
    if total_rows is not None:
        parts.append(jnp.zeros((total_rows - rows, LANES), arrays[0].dtype))
    return jnp.concatenate(parts, axis=0)


def _unpack(buf, shapes):
    out = []
    row = 0
    lead = buf.shape[:-2]
    for shape in shapes:
        size = 1
        for s in shape:
            size *= s
        rows = size // LANES
        out.append(buf[..., row:row + rows, :].reshape(lead + tuple(shape)))
        row += rows + (-rows % 8)
    return out


BIG = ("e_w_in", "e_w_out", "o_w_in", "o_w_out")
SHARDED_SMALL = {
    "e_pool_w": ((4, 64, 256), 1), "o_pre_norm": ((512,), 0), "o_sgu_norm_g": ((256,), 0), "o_sgu_norm_b": ((256,), 0),
    "o_conv_w": ((31, 256), 1), "o_conv_b": ((256,), 0), "o_conv_norm_g": ((256,), 0), "o_conv_norm_b": ((256,), 0),
    "o_post_norm": ((512,), 0),
}
REPLICATED_SMALL = {"e_pre_norm": (2048,), "e_pool_scale": (1024,), "e_post_norm": (2048,),
                    "o_sgu_w": (4, 128, 128), "o_sgu_b": (4, 128)}
SMALL_ORDER = ("e_pre_norm", "e_pool_w", "e_pool_scale", "e_post_norm", "o_pre_norm", "o_sgu_norm_g", "o_sgu_norm_b",
               "o_sgu_w", "o_sgu_b", "o_conv_w", "o_conv_b", "o_conv_norm_g", "o_conv_norm_b", "o_post_norm")
ALL_ORDER = ("e_pre_norm", "e_w_in", "e_pool_w", "e_pool_scale", "e_w_out", "e_post_norm", "o_pre_norm", "o_w_in",
             "o_sgu_norm_g", "o_sgu_norm_b", "o_sgu_w", "o_sgu_b", "o_conv_w", "o_conv_b", "o_conv_norm_g",
             "o_conv_norm_b", "o_w_out", "o_post_norm")


def _full_shape(name):
    shape, axis = SHARDED_SMALL[name]
    return tuple(s * N_CHIPS if i == axis else s for i, s in enumerate(shape))


def _from_chips(name, stacked):
    shape, axis = SHARDED_SMALL[name]
    return jnp.moveaxis(stacked, 0, axis).reshape(_full_shape(name))


def _my_shard(name, full, chip):
    shape, axis = SHARDED_SMALL[name]
    return lax.dynamic_slice_in_dim(full, chip * shape[axis], shape[axis], axis)


def kernel(x, e_pre_norm, e_w_in, e_pool_w, e_pool_scale, e_w_out, e_post_norm, o_pre_norm, o_w_in, o_sgu_norm_g, o_sgu_norm_b, o_sgu_w, o_sgu_b, o_conv_w, o_conv_b, o_conv_norm_g, o_conv_norm_b, o_w_out, o_post_norm, loss_target, m_e_pre_norm, m_e_w_in, m_e_pool_w, m_e_pool_scale, m_e_w_out, m_e_post_norm, m_o_pre_norm, m_o_w_in, m_o_sgu_norm_g, m_o_sgu_norm_b, m_o_sgu_w, m_o_sgu_b, m_o_conv_w, m_o_conv_b, m_o_conv_norm_g, m_o_conv_norm_b, m_o_w_out, m_o_post_norm, v_e_pre_norm, v_e_w_in, v_e_pool_w, v_e_pool_scale, v_e_w_out, v_e_post_norm, v_o_pre_norm, v_o_w_in, v_o_sgu_norm_g, v_o_sgu_norm_b, v_o_sgu_w, v_o_sgu_b, v_o_conv_w, v_o_conv_b, v_o_conv_norm_g, v_o_conv_norm_b, v_o_w_out, v_o_post_norm):
    w = dict(e_pre_norm=e_pre_norm, e_w_in=e_w_in, e_pool_w=e_pool_w, e_pool_scale=e_pool_scale, e_w_out=e_w_out,
             e_post_norm=e_post_norm, o_pre_norm=o_pre_norm, o_w_in=o_w_in, o_sgu_norm_g=o_sgu_norm_g,
             o_sgu_norm_b=o_sgu_norm_b, o_sgu_w=o_sgu_w, o_sgu_b=o_sgu_b, o_conv_w=o_conv_w, o_conv_b=o_conv_b,
             o_conv_norm_g=o_conv_norm_g, o_conv_norm_b=o_conv_norm_b, o_w_out=o_w_out, o_post_norm=o_post_norm)
    m = dict(e_pre_norm=m_e_pre_norm, e_w_in=m_e_w_in, e_pool_w=m_e_pool_w, e_pool_scale=m_e_pool_scale,
             e_w_out=m_e_w_out, e_post_norm=m_e_post_norm, o_pre_norm=m_o_pre_norm, o_w_in=m_o_w_in,
             o_sgu_norm_g=m_o_sgu_norm_g, o_sgu_norm_b=m_o_sgu_norm_b, o_sgu_w=m_o_sgu_w, o_sgu_b=m_o_sgu_b,
             o_conv_w=m_o_conv_w, o_conv_b=m_o_conv_b, o_conv_norm_g=m_o_conv_norm_g, o_conv_norm_b=m_o_conv_norm_b,
             o_w_out=m_o_w_out, o_post_norm=m_o_post_norm)
    v = dict(e_pre_norm=v_e_pre_norm, e_w_in=v_e_w_in, e_pool_w=v_e_pool_w, e_pool_scale=v_e_pool_scale,
             e_w_out=v_e_w_out, e_post_norm=v_e_post_norm, o_pre_norm=v_o_pre_norm, o_w_in=v_o_w_in,
             o_sgu_norm_g=v_o_sgu_norm_g, o_sgu_norm_b=v_o_sgu_norm_b, o_sgu_w=v_o_sgu_w, o_sgu_b=v_o_sgu_b,
             o_conv_w=v_o_conv_w, o_conv_b=v_o_conv_b, o_conv_norm_g=v_o_conv_norm_g, o_conv_norm_b=v_o_conv_norm_b,
             o_w_out=v_o_w_out, o_post_norm=v_o_post_norm)
    w, m, v = ({k: a[0] for k, a in d.items()} for d in (w, m, v))
    chip = 2 * lax.axis_index("x") + lax.axis_index("y")

    loss, grad_x, in_flight, small = _step(x[0], loss_target[0], w, chip)

    grads, delta, new_m, new_v = {}, {}, {}, {}
    after = grad_x
    for k in ("o_w_out", "o_w_in", "e_w_out", "e_w_in"):
        grads[k] = _land(in_flight, k, chip, after)
        delta[k], new_m[k], new_v[k] = _adamw(w[k], grads[k], m[k], v[k], f"adamw_{k}")
        after = delta[k]

    small_full_shapes = {k: (_full_shape(k) if k in SHARDED_SMALL else REPLICATED_SMALL[k]) for k in SMALL_ORDER}
    small_parts = _pack([small[k].reshape(small_full_shapes[k]) for k in SMALL_ORDER], total_rows=SMALL_GRAD_ROWS)
    small_parts = small_parts + 0.0 * after[0, 0]
    reduced = _reduce_scatter([small_parts.reshape(N_CHIPS, SMALL_GRAD_ROWS // N_CHIPS, LANES)], "small")
    small_sum = _all_gather(reduced, "gather_small_grads")[0].reshape(SMALL_GRAD_ROWS, LANES)
    for k, a in zip(SMALL_ORDER, _unpack(small_sum, [small_full_shapes[k] for k in SMALL_ORDER])):
        grads[k] = _my_shard(k, a, chip) if k in SHARDED_SMALL else a
    loss = lax.psum(loss[0, 0], ("x", "y", "c"))

    def rows_of(a):
        return a.reshape(-1, a.shape[-1])

    updates = _adamw_small(*[[rows_of(d[k]) for k in SMALL_ORDER] for d in (w, grads, m, v)])
    for d, arrays in zip((delta, new_m, new_v), updates):
        for k, a in zip(SMALL_ORDER, arrays):
            d[k] = a.reshape(w[k].shape)

    outs = [loss, grad_x[None]]
    for d in (grads, delta, new_m, new_v):
        outs += [d[k][None] for k in ALL_ORDER]
    return tuple(outs)
```

```python
import functools

import jax
import jax.numpy as jnp
from jax import lax
from jax.experimental import pallas as pl
from jax.experimental.pallas import tpu as pltpu

f32 = jnp.float32
bf16 = jnp.bfloat16
SDS = jax.ShapeDtypeStruct

SEQ = 2048
D_MODEL = 2048
EPS = 1e-6
NEG = -1e30
HEAD_DIM = 128
ROT_HALF = 16
ROPE_THETA = 500000.0
DILATIONS = (1, 4, 16)
SPAN = 128
N_HEADS = 8
HALF = 1024
POOL_CH = 256
CONV_K = 31
CONV_PAD = 32
CHUNK = 128
N_CHIPS = 4
LANES = 256
E_IN_PIECES = 3
SMALL_SHARD_ROWS = 352
SMALL_GRAD_ROWS = 1536
ANY = pl.BlockSpec(memory_space=pl.ANY)
MESH = pl.DeviceIdType.MESH

ADAM_LR = 0.001
ADAM_B1 = 0.9
ADAM_B2 = 0.999
ADAM_EPS = 1e-08
ADAM_WD = 0.01
ADAM_STEP = 10


def _dot(a, b):
    return jnp.dot(a, b, preferred_element_type=f32)


def _dot_nt(a, b):
    return lax.dot_general(a, b, (((1,), (1,)), ((), ())), preferred_element_type=f32)


def _dot_tn(a, b):
    return lax.dot_general(a, b, (((0,), (0,)), ((), ())), preferred_element_type=f32)


def _sigmoid(x):
    return 1.0 / (1.0 + jnp.exp(-x))


def _silu_and_grad(x):
    s = _sigmoid(x)
    return x * s, s * (1.0 + x * (1.0 - s))


def _rms_fwd(x, g):
    r = lax.rsqrt(jnp.mean(x * x, axis=-1, keepdims=True) + EPS)
    return x * r * g


def _rms_bwd(x, g, dout):
    r = lax.rsqrt(jnp.mean(x * x, axis=-1, keepdims=True) + EPS)
    xh = x * r
    dg = jnp.sum(dout * xh, axis=0, keepdims=True)
    dxh = dout * g
    dx = r * (dxh - xh * jnp.mean(dxh * xh, axis=-1, keepdims=True))
    return dx, dg


def _ln_stats(x):
    mu = jnp.mean(x, axis=-1, keepdims=True)
    xc = x - mu
    rstd = lax.rsqrt(jnp.mean(xc * xc, axis=-1, keepdims=True) + EPS)
    return xc * rstd, rstd


def _ln_bwd(xh, rstd, g, dout):
    dg = jnp.sum(dout * xh, axis=0, keepdims=True)
    db = jnp.sum(dout, axis=0, keepdims=True)
    dxh = dout * g
    dx = rstd * (dxh - jnp.mean(dxh, axis=-1, keepdims=True) - xh * jnp.mean(dxh * xh, axis=-1, keepdims=True))
    return dx, dg, db


def _accumulate(ref, value, first):
    @pl.when(first)
    def _():
        ref[...] = value

    @pl.when(jnp.logical_not(first))
    def _():
        ref[...] += value


def _col_tile(ns):
    for t in (1024, 768, 512, 256):
        if ns % t == 0:
            return t
    raise ValueError(ns)


def _mm_nn(a, w, out_dtype, name, piece=0, pieces=1, into=None):
    m, k = a.shape
    j, _, ns = w.shape
    tm, tn = 1024, _col_tile(ns)
    nb = ns // tn

    def body(a_ref, w_ref, *rest):
        rest[-1][...] = _dot(a_ref[...], w_ref[...]).astype(rest[-1].dtype)

    return pl.pallas_call(
        body, name=name, grid=(j * nb, m // tm),
        in_specs=[pl.BlockSpec((tm, k), lambda n, i: (i, 0)),
                  pl.BlockSpec((None, k, tn), lambda n, i: (n // nb, 0, n % nb))] + ([] if into is None else [ANY]),
        out_specs=pl.BlockSpec((tm, tn), lambda n, i: (i, ((n // nb) * pieces + piece) * nb + n % nb)),
        out_shape=SDS((m, j * ns * pieces), out_dtype),
        input_output_aliases={} if into is None else {2: 0},
    )(a, w, *([] if into is None else [into]))


def _mm_nt(dz, ws, name, after):
    m, _ = dz.shape
    pieces = len(ws)
    j, k, ns = ws[0].shape
    tm, tk = 1024, 1024
    tn = next(t for t in (2048, 1536, 1024, 768, 512) if ns % t == 0)
    nb = ns // tn
    turn = j * nb

    def body(dz_ref, *rest):
        w_refs, o_ref = rest[:pieces], rest[-1]
        r = pl.program_id(2)
        for q in range(pieces):
            @pl.when(r // turn == q)
            def _(q=q):
                _accumulate(o_ref, _dot_nt(dz_ref[...], w_refs[q][...]), r == 0)

    def w_spec(q):
        def index(i, kk, r):
            rr = jnp.clip(r - q * turn, 0, turn - 1)
            return rr // nb, kk, rr % nb

        return pl.BlockSpec((None, tk, tn), index)

    def dz_index(i, kk, r):
        rr = r % turn
        return i, ((rr // nb) * pieces + r // turn) * nb + rr % nb

    return pl.pallas_call(
        body, name=name, grid=(m // tm, k // tk, pieces * turn),
        in_specs=[pl.BlockSpec((tm, tn), dz_index)] + [w_spec(q) for q in range(pieces)] + [ANY],
        out_specs=pl.BlockSpec((tm, tk), lambda i, kk, r: (i, kk)),
        out_shape=SDS((m, k), f32),
    )(dz, *ws, after)


def _mm_tn(a, dz, j, name):
    m, k = a.shape
    ns = dz.shape[1] // j
    tk, tn = 1024, _col_tile(ns)
    nb = ns // tn

    def body(a_ref, dz_ref, o_ref):
        o_ref[...] = _dot_tn(a_ref[...], dz_ref[...]).astype(o_ref.dtype)

    return pl.pallas_call(
        body, name=name, grid=(k // tk, j * nb),
        in_specs=[pl.BlockSpec((m, tk), lambda kk, n: (0, kk)),
                  pl.BlockSpec((m, tn), lambda kk, n: (0, n))],
        out_specs=pl.BlockSpec((None, tk, tn), lambda kk, n: (n // nb, kk, n % nb)),
        out_shape=SDS((j, k, ns), bf16),
    )(a, dz)


ROWS = 256


def _row_spec(width=D_MODEL, col=0):
    return pl.BlockSpec((ROWS, width), lambda i: (i, col))


def _vec_spec(width=D_MODEL):
    return pl.BlockSpec((1, width), lambda i: (0, 0))


def _pre_norm(x, g):
    def body(x_ref, g_ref, h_ref):
        h_ref[...] = _rms_fwd(x_ref[...], g_ref[...]).astype(bf16)

    return pl.pallas_call(
        body, name="pre_norm", grid=(SEQ // ROWS,), in_specs=[_row_spec(), _vec_spec()],
        out_specs=_row_spec(), out_shape=SDS((SEQ, D_MODEL), bf16))(x, g)


def _mid_norm(x, y, g_post, g_pre):
    def body(x_ref, y_ref, gpost_ref, gpre_ref, x1_ref, h1_ref):
        x1 = x_ref[...] + _rms_fwd(y_ref[...], gpost_ref[...])
        x1_ref[...] = x1
        h1_ref[...] = _rms_fwd(x1, gpre_ref[...]).astype(bf16)

    return pl.pallas_call(
        body, name="mid_norm", grid=(SEQ // ROWS,),
        in_specs=[_row_spec(), _row_spec(), _vec_spec(), _vec_spec()],
        out_specs=[_row_spec(), _row_spec()],
        out_shape=[SDS((SEQ, D_MODEL), f32), SDS((SEQ, D_MODEL), bf16)])(x, y, g_post, g_pre)


def _final_norm_loss(x1, y, g_post, target):
    def body(x1_ref, y_ref, g_ref, t_ref, loss_ref, dx2_ref, dy_ref, dg_ref):
        first = pl.program_id(0) == 0
        y = y_ref[...]
        g = g_ref[...]
        err = x1_ref[...] + _rms_fwd(y, g) - t_ref[...]
        sq = jnp.sum(jnp.sum(err * err, axis=1, keepdims=True), axis=0, keepdims=True)
        _accumulate(loss_ref, sq * (0.5 / D_MODEL), first)
        dx2 = err * (1.0 / D_MODEL)
        dx2_ref[...] = dx2
        dy, dg = _rms_bwd(y, g, dx2)
        dy_ref[...] = dy.astype(bf16)
        _accumulate(dg_ref, dg, first)

    return pl.pallas_call(
        body, name="final_norm_loss", grid=(SEQ // ROWS,),
        in_specs=[_row_spec(), _row_spec(), _vec_spec(), _row_spec()],
        out_specs=[pl.BlockSpec((1, 1), lambda i: (0, 0)), _row_spec(), _row_spec(), _vec_spec()],
        out_shape=[SDS((1, 1), f32), SDS((SEQ, D_MODEL), f32), SDS((SEQ, D_MODEL), bf16), SDS((1, D_MODEL), f32)],
    )(x1, y, g_post, target)


def _mid_norm_bwd(dx2, dh1, x1, y0, g_pre, g_post):
    def body(dx2_ref, dh1_ref, x1_ref, y0_ref, gpre_ref, gpost_ref, dx1_ref, dy0_ref, dgpre_ref, dgpost_ref):
        first = pl.program_id(0) == 0
        d_in, dgpre = _rms_bwd(x1_ref[...], gpre_ref[...], dh1_ref[...])
        dx1 = dx2_ref[...] + d_in
        dx1_ref[...] = dx1
        dy0, dgpost = _rms_bwd(y0_ref[...], gpost_ref[...], dx1)
        dy0_ref[...] = dy0.astype(bf16)
        _accumulate(dgpre_ref, dgpre, first)
        _accumulate(dgpost_ref, dgpost, first)

    return pl.pallas_call(
        body, name="mid_norm_bwd", grid=(SEQ // ROWS,),
        in_specs=[_row_spec(), _row_spec(), _row_spec(), _row_spec(), _vec_spec(), _vec_spec()],
        out_specs=[_row_spec(), _row_spec(), _vec_spec(), _vec_spec()],
        out_shape=[SDS((SEQ, D_MODEL), f32), SDS((SEQ, D_MODEL), bf16), SDS((1, D_MODEL), f32), SDS((1, D_MODEL), f32)],
    )(dx2, dh1, x1, y0, g_pre, g_post)


def _pre_norm_bwd(dx1, dh0, x, g):
    def body(dx1_ref, dh0_ref, x_ref, g_ref, dx_ref, dg_ref):
        d_in, dg = _rms_bwd(x_ref[...], g_ref[...], dh0_ref[...])
        dx_ref[...] = dx1_ref[...] + d_in
        _accumulate(dg_ref, dg, pl.program_id(0) == 0)

    return pl.pallas_call(
        body, name="pre_norm_bwd", grid=(SEQ // ROWS,),
        in_specs=[_row_spec(), _row_spec(), _row_spec(), _vec_spec()],
        out_specs=[_row_spec(), _vec_spec()],
        out_shape=[SDS((SEQ, D_MODEL), f32), SDS((1, D_MODEL), f32)])(dx1, dh0, x, g)


def _pool_count(g):
    row = lax.broadcasted_iota(jnp.int32, (SEQ, 1), 0)
    width = jnp.left_shift(2, g)
    return row, width, jnp.minimum(row + 1, width).astype(f32)


def _trailing_sum(x, row, width):
    s = x
    for k in (1, 2, 4, 8):
        shifted = jnp.where(row >= k, pltpu.roll(s, k, 0), 0.0)
        s = jnp.where(width > k, s + shifted, s)
    return s


def _leading_sum(x, row, width):
    s = x
    for k in (1, 2, 4, 8):
        shifted = jnp.where(row < SEQ - k, pltpu.roll(s, SEQ - k, 0), 0.0)
        s = jnp.where(width > k, s + shifted, s)
    return s


def _pool_specs():
    a_in = pl.BlockSpec((SEQ, POOL_CH), lambda g: (0, g))
    a_gate = pl.BlockSpec((SEQ, POOL_CH), lambda g: (0, 4 + g))
    w = pl.BlockSpec((None, POOL_CH, POOL_CH), lambda g: (g, 0, 0))
    scale = pl.BlockSpec((1, POOL_CH), lambda g: (0, g))
    return a_in, a_gate, w, scale


def _pool_fwd(z0, pool_w, pool_scale):
    def body(a_ref, gate_ref, w_ref, scale_ref, ya_ref):
        row, width, count = _pool_count(pl.program_id(0))
        a = a_ref[...]
        pooled = _trailing_sum(a, row, width) / count - a
        mixed = _dot(pooled.astype(bf16), w_ref[...]) * scale_ref[...]
        gate = gate_ref[...]
        ya_ref[...] = (mixed * gate * _sigmoid(gate)).astype(bf16)

    return pl.pallas_call(
        body, name="pool_fwd", grid=(4,), in_specs=list(_pool_specs()),
        out_specs=pl.BlockSpec((SEQ, POOL_CH), lambda g: (0, g)),
        out_shape=SDS((SEQ, HALF), bf16))(z0, z0, pool_w, pool_scale)


def _pool_bwd(z0, dcat, pool_w, pool_scale):
    def body(a_ref, gate_ref, w_ref, scale_ref, dya_ref, da_ref, dgate_ref, dw_ref, dscale_ref):
        row, width, count = _pool_count(pl.program_id(0))
        a = a_ref[...]
        pooled = (_trailing_sum(a, row, width) / count - a).astype(bf16)
        w = w_ref[...]
        scale = scale_ref[...]
        mixed = _dot(pooled, w)
        silu, dsilu = _silu_and_grad(gate_ref[...])
        dya = dya_ref[...]
        dgate_ref[...] = (dya * mixed * scale * dsilu).astype(bf16)
        dms = dya * silu
        dscale_ref[...] = jnp.sum(dms * mixed, axis=0, keepdims=True)
        dmixed = (dms * scale).astype(bf16)
        dw_ref[...] = _dot_tn(pooled, dmixed)
        dpooled = _dot_nt(dmixed, w)
        da_ref[...] = (_leading_sum(dpooled / count, row, width) - dpooled).astype(bf16)

    a_in, a_gate, w, scale = _pool_specs()
    col = pl.BlockSpec((SEQ, POOL_CH), lambda g: (0, g))
    return pl.pallas_call(
        body, name="pool_bwd", grid=(4,), in_specs=[a_in, a_gate, w, scale, col],
        out_specs=[col, col, w, scale],
        out_shape=[SDS((SEQ, HALF), bf16), SDS((SEQ, HALF), bf16), SDS((4, POOL_CH, POOL_CH), f32), SDS((1, HALF), f32)],
    )(z0, z0, pool_w, pool_scale, dcat)


Q_COL, K_COL, V_COL, BGATE_COL = 16, 40, 64, 88


def _rope_tables():
    pos = jnp.arange(SEQ, dtype=f32)
    inv_freq = jnp.power(ROPE_THETA, -jnp.arange(0, 2 * ROT_HALF, 2, dtype=f32) / (2 * ROT_HALF))
    ang = pos[:, None] * inv_freq[None, :]
    cos, sin = jnp.cos(ang), jnp.sin(ang)
    zeros = jnp.zeros((SEQ, HEAD_DIM - 2 * ROT_HALF), f32)
    cos_t = jnp.concatenate([cos, cos, zeros + 1.0], axis=1)
    sin_t = jnp.concatenate([sin, sin, zeros], axis=1)
    j = jnp.arange(HEAD_DIM)[:, None]
    i = jnp.arange(HEAD_DIM)[None, :]
    rot = jnp.where((i < ROT_HALF) & (j == i + ROT_HALF), -1.0, 0.0) + jnp.where(
        (i >= ROT_HALF) & (i < 2 * ROT_HALF) & (j == i - ROT_HALF), 1.0, 0.0)
    return cos_t, sin_t, rot.astype(bf16), rot.T.astype(bf16)


def _exact_dot(t, m):
    hi = t.astype(bf16)
    lo = (t - hi.astype(f32)).astype(bf16)
    return _dot(hi, m) + _dot(lo, m)


def _rope(t, cos_t, sin_t, rot):
    return t * cos_t + _exact_dot(t, rot) * sin_t


def _rope_transposed(d, cos_t, sin_t, rot_t):
    return d * cos_t + _exact_dot(d * sin_t, rot_t)


ROW_CHUNK = 256


def _chunks(fn):
    def step(i, carry):
        fn(pl.multiple_of(i * ROW_CHUNK, ROW_CHUNK))
        return carry

    lax.fori_loop(0, SEQ // ROW_CHUNK, step, 0)


def _pieces(dilation):
    length = SEQ // dilation
    n = min(length, ROW_CHUNK)
    return [(r, l0, n) for r in range(dilation) for l0 in range(0, length, n)]


def _by_residue(dst_ref, src_ref, dilation, dtype):
    length = SEQ // dilation
    for r, l0, n in _pieces(dilation):
        src = src_ref[l0:l0 + n, :] if dilation == 1 else src_ref[pl.ds(r + dilation * l0, n, stride=dilation), :]
        start = r * length + l0
        dst_ref[start:start + n, :] = src.astype(dtype)


def _by_position(dst_ref, src_ref, dilation):
    length = SEQ // dilation
    for r, l0, n in _pieces(dilation):
        src = src_ref[r * length + l0:r * length + l0 + n, :]
        if dilation == 1:
            dst_ref[l0:l0 + n, :] = src
        else:
            dst_ref[pl.ds(r + dilation * l0, n, stride=dilation), :] = src


def _attn_masks():
    qi = lax.broadcasted_iota(jnp.int32, (SPAN, 2 * SPAN), 0)
    kj = lax.broadcasted_iota(jnp.int32, (SPAN, 2 * SPAN), 1)
    window = ((kj < SPAN) & (kj >= qi)) | ((kj >= SPAN) & (kj - SPAN <= qi))
    own = lax.broadcasted_iota(jnp.int32, (SPAN, SPAN), 1) <= lax.broadcasted_iota(jnp.int32, (SPAN, SPAN), 0)
    return window, own


def _attn_blocks(dilation):
    per_residue = SEQ // dilation // SPAN
    blocks = [(c, c % per_residue != 0) for c in range(SEQ // SPAN)]
    return [blocks[i:i + 4] for i in range(0, len(blocks), 4)]


def _block_keys(c, has_prev):
    return slice((c - 1) * SPAN if has_prev else c * SPAN, (c + 1) * SPAN)


def _head_spec(col):
    return pl.BlockSpec((SEQ, HEAD_DIM), lambda h: (0, col + h))


def _table_spec():
    return pl.BlockSpec((SEQ, HEAD_DIM), lambda h: (0, 0))


def _attn_fwd(z0, tables):
    scale = HEAD_DIM ** -0.5

    def body(*refs):
        qkv = refs[0:9]
        bg_ref, cos_ref, sin_ref, rot_ref = refs[9:13]
        yb_ref, att_ref, lse_ref = refs[13:16]
        saved = refs[16:25]
        tmp_q, tmp_k, v_ones, o_res, l_res, o_nat, l_nat = refs[25:32]
        window_mask, own_mask = _attn_masks()
        rot = rot_ref[...]

        @pl.when(pl.program_id(0) == 0)
        def _():
            v_ones[:, HEAD_DIM:] = jnp.ones((SEQ, HEAD_DIM), bf16)

        for g, dilation in enumerate(DILATIONS):
            q_ref, k_ref, v_ref = qkv[3 * g:3 * g + 3]
            qd, kd, vd = saved[3 * g:3 * g + 3]

            def rope_rows(start, q_ref=q_ref, k_ref=k_ref):
                r = pl.ds(start, ROW_CHUNK)
                cos_t, sin_t = cos_ref[r, :], sin_ref[r, :]
                tmp_q[r, :] = _rope(q_ref[r, :], cos_t, sin_t, rot) * scale
                tmp_k[r, :] = _rope(k_ref[r, :], cos_t, sin_t, rot)

            _chunks(rope_rows)
            _by_residue(qd, tmp_q, dilation, bf16)
            _by_residue(kd, tmp_k, dilation, bf16)
            _by_residue(vd, v_ref, dilation, bf16)
            for l0 in range(0, SEQ, ROW_CHUNK):
                v_ones[l0:l0 + ROW_CHUNK, 0:HEAD_DIM] = vd[l0:l0 + ROW_CHUNK, :]

            for four in _attn_blocks(dilation):
                scores = [_dot_nt(qd[c * SPAN:(c + 1) * SPAN, :], kd[_block_keys(c, prev), :]) for c, prev in four]
                tops, probs = [], []
                for (c, prev), s in zip(four, scores):
                    s = jnp.where(window_mask if prev else own_mask, s, NEG)
                    tops.append(jnp.max(s, axis=1, keepdims=True))
                    probs.append(jnp.exp(s - tops[-1]).astype(bf16))
                sums = [_dot(p, v_ones[_block_keys(c, prev), :]) for (c, prev), p in zip(four, probs)]
                for (c, prev), m, o in zip(four, tops, sums):
                    den = o[:, HEAD_DIM:]
                    o_res[c * SPAN:(c + 1) * SPAN, :] = o[:, :HEAD_DIM] / den
                    l_res[c * SPAN:(c + 1) * SPAN, :] = m + jnp.log(den)

            if dilation > 1:
                _by_position(o_nat, o_res, dilation)
                _by_position(l_nat, l_res, dilation)
            o_g, l_g = (o_res, l_res) if dilation == 1 else (o_nat, l_nat)

            def merge(start, g=g, o_g=o_g, l_g=l_g):
                r = pl.ds(start, ROW_CHUNK)
                if g == 0:
                    att, total = o_g[r, :], l_g[r, :]
                else:
                    l_old, l_new = lse_ref[r, :], l_g[r, :]
                    top = jnp.maximum(l_old, l_new)
                    total = top + jnp.log(jnp.exp(l_old - top) + jnp.exp(l_new - top))
                    att = att_ref[r, :] * jnp.exp(l_old - total) + o_g[r, :] * jnp.exp(l_new - total)
                att_ref[r, :] = att
                lse_ref[r, :] = total
                if g == len(DILATIONS) - 1:
                    gate = bg_ref[r, :]
                    yb_ref[r, :] = (att * gate * _sigmoid(gate)).astype(bf16)

            _chunks(merge)

    in_specs = []
    for g in range(3):
        in_specs += [_head_spec(Q_COL + 8 * g), _head_spec(K_COL + 8 * g), _head_spec(V_COL + 8 * g)]
    in_specs += [_head_spec(BGATE_COL), _table_spec(), _table_spec(), pl.BlockSpec((HEAD_DIM, HEAD_DIM), lambda h: (0, 0))]
    out_spec = pl.BlockSpec((SEQ, HEAD_DIM), lambda h: (0, h))
    vm = lambda dt: pltpu.VMEM((SEQ, HEAD_DIM), dt)
    cos_t, sin_t, rot, _ = tables
    out = pl.pallas_call(
        body, name="attn_fwd", grid=(N_HEADS,), in_specs=in_specs, out_specs=[out_spec] * 12,
        out_shape=[SDS((SEQ, HALF), bf16), SDS((SEQ, HALF), f32), SDS((SEQ, HALF), f32)] + [SDS((SEQ, HALF), bf16)] * 9,
        scratch_shapes=[vm(f32), vm(f32), pltpu.VMEM((SEQ, 2 * HEAD_DIM), bf16), vm(f32), vm(f32), vm(f32), vm(f32)],
    )(*([z0] * 10), cos_t, sin_t, rot)
    return out[0], out[1], out[2], [tuple(out[3 + 3 * g:6 + 3 * g]) for g in range(3)]


def _attn_bwd_group(g, saved, z0, att, lse, dcat, tables):
    scale = HEAD_DIM ** -0.5
    dilation = DILATIONS[g]
    with_gate = g == 0

    def body(*refs):
        qd, kd, vd, bg_ref, att_ref, lse_ref, dyb_ref, cos_ref, sin_ref, rot_t_ref = refs[0:10]
        n_out = 4 if with_gate else 3
        dq_ref, dk_ref, dv_ref = refs[10:13]
        dod, ld, dd, tmp, aq, ak, av = refs[10 + n_out:17 + n_out]
        window_mask, own_mask = _attn_masks()
        rot_t = rot_t_ref[...]

        def gate_rows(start):
            r = pl.ds(start, ROW_CHUNK)
            silu, dsilu = _silu_and_grad(bg_ref[r, :])
            att_v = att_ref[r, :]
            dyb = dyb_ref[r, :]
            if with_gate:
                refs[13][r, :] = (dyb * att_v * dsilu).astype(bf16)
            datt = dyb * silu
            tmp[r, :] = datt
            aq[r, :] = jnp.broadcast_to(jnp.sum(datt * att_v, axis=1, keepdims=True), (ROW_CHUNK, HEAD_DIM))

        _chunks(gate_rows)
        _by_residue(dod, tmp, dilation, bf16)
        _by_residue(dd, aq, dilation, f32)
        _by_residue(ld, lse_ref, dilation, f32)

        for four in _attn_blocks(dilation):
            rows = [slice(c * SPAN, (c + 1) * SPAN) for c, _ in four]
            keys = [_block_keys(c, prev) for c, prev in four]
            scores = [_dot_nt(qd[r, :], kd[k, :]) for r, k in zip(rows, keys)]
            dprobs = [_dot_nt(dod[r, :], vd[k, :]) for r, k in zip(rows, keys)]
            probs, dscores = [], []
            for (c, prev), r, s, dp in zip(four, rows, scores, dprobs):
                lse_q, delta = ld[r, :], dd[r, :]
                if prev:
                    lse_q = jnp.concatenate([lse_q, lse_q], axis=1)
                    delta = jnp.concatenate([delta, delta], axis=1)
                p = jnp.where(window_mask if prev else own_mask, jnp.exp(s - lse_q), 0.0)
                probs.append(p.astype(bf16))
                dscores.append((p * (dp - delta)).astype(bf16))
            dvs = [_dot_tn(p, dod[r, :]) for p, r in zip(probs, rows)]
            dks = [_dot_tn(ds, qd[r, :]) for ds, r in zip(dscores, rows)]
            dqs = [_dot(ds, kd[k, :]) for ds, k in zip(dscores, keys)]
            for (c, prev), r, dv, dk, dq in zip(four, rows, dvs, dks, dqs):
                aq[r, :] = dq
                if prev:
                    before = slice((c - 1) * SPAN, c * SPAN)
                    av[before, :] += dv[0:SPAN]
                    ak[before, :] += dk[0:SPAN]
                    av[r, :] = dv[SPAN:]
                    ak[r, :] = dk[SPAN:]
                else:
                    av[r, :] = dv
                    ak[r, :] = dk

        def finish(out_ref, acc, factor, roped):
            if dilation > 1:
                _by_position(tmp, acc, dilation)
            src = acc if dilation == 1 else tmp

            def rows(start):
                r = pl.ds(start, ROW_CHUNK)
                d = src[r, :]
                if factor != 1.0:
                    d = d * factor
                if roped:
                    d = _rope_transposed(d, cos_ref[r, :], sin_ref[r, :], rot_t)
                out_ref[r, :] = d.astype(bf16)

            _chunks(rows)

        finish(dq_ref, aq, scale, True)
        finish(dk_ref, ak, 1.0, True)
        finish(dv_ref, av, 1.0, False)

    head = pl.BlockSpec((SEQ, HEAD_DIM), lambda h: (0, h))
    in_specs = [head, head, head, _head_spec(BGATE_COL), head, head, _head_spec(8), _table_spec(), _table_spec(),
                pl.BlockSpec((HEAD_DIM, HEAD_DIM), lambda h: (0, 0))]
    n_out = 4 if with_gate else 3
    vm = lambda dt: pltpu.VMEM((SEQ, HEAD_DIM), dt)
    cos_t, sin_t, _, rot_t = tables
    return pl.pallas_call(
        body, name=f"attn_bwd_g{g}", grid=(N_HEADS,), in_specs=in_specs, out_specs=[head] * n_out,
        out_shape=[SDS((SEQ, HALF), bf16)] * n_out,
        scratch_shapes=[vm(bf16), vm(f32), vm(f32), vm(f32), vm(f32), vm(f32), vm(f32)],
    )(*saved, z0, att, lse, dcat, cos_t, sin_t, rot_t)


def _sgu_specs():
    chunk = lambda col: pl.BlockSpec((CHUNK, HALF), lambda n: (n, col))
    vec = pl.BlockSpec((1, HALF), lambda n: (0, 0))
    w = pl.BlockSpec((4, CHUNK, CHUNK), lambda n: (0, 0, 0))
    bias = pl.BlockSpec((CHUNK, CHUNK), lambda n: (0, 0))
    return chunk, vec, w, bias


def _sgu_weights(w_ref):
    tril = lax.broadcasted_iota(jnp.int32, (CHUNK, CHUNK), 1) <= lax.broadcasted_iota(jnp.int32, (CHUNK, CHUNK), 0)
    return tril, [jnp.where(tril, w_ref[h], 0.0).astype(bf16) for h in range(4)]


def _sgu_fwd(z1, ln_g, ln_b, sgu_w, bias_t):
    def body(u_ref, v_ref, cg_ref, g_ref, b_ref, w_ref, bias_ref, yc_ref):
        _, ws = _sgu_weights(w_ref)
        xh, _ = _ln_stats(v_ref[...])
        vn = (xh * g_ref[...] + b_ref[...]).astype(bf16)
        for h in range(4):
            cols = slice(h * POOL_CH, (h + 1) * POOL_CH)
            s = _dot(ws[h], vn[:, cols]) + bias_ref[:, h:h + 1]
            gate = cg_ref[:, cols]
            yc_ref[:, cols] = (u_ref[:, cols] * s * gate * _sigmoid(gate)).astype(bf16)

    chunk, vec, w, bias = _sgu_specs()
    return pl.pallas_call(
        body, name="sgu_fwd", grid=(SEQ // CHUNK,),
        in_specs=[chunk(0), chunk(1), chunk(2), vec, vec, w, bias], out_specs=chunk(0),
        out_shape=SDS((SEQ, HALF), bf16))(z1, z1, z1, ln_g, ln_b, sgu_w, bias_t)


def _sgu_bwd(z1, dcat, ln_g, ln_b, sgu_w, bias_t):
    def body(u_ref, v_ref, cg_ref, dyc_ref, g_ref, b_ref, w_ref, bias_ref,
             du_ref, dv_ref, dcg_ref, dw_ref, dbias_ref, dg_ref, db_ref, dvn_ref):
        first = pl.program_id(0) == 0
        tril, ws = _sgu_weights(w_ref)
        xh, rstd = _ln_stats(v_ref[...])
        g = g_ref[...]
        vn = (xh * g + b_ref[...]).astype(bf16)

        @pl.when(first)
        def _():
            dbias_ref[...] = jnp.zeros((CHUNK, CHUNK), f32)

        for h in range(4):
            cols = slice(h * POOL_CH, (h + 1) * POOL_CH)
            vn_h = vn[:, cols]
            s = _dot(ws[h], vn_h) + bias_ref[:, h:h + 1]
            silu, dsilu = _silu_and_grad(cg_ref[:, cols])
            dyc = dyc_ref[:, cols]
            u = u_ref[:, cols]
            du_ref[:, cols] = (dyc * s * silu).astype(bf16)
            dcg_ref[:, cols] = (dyc * u * s * dsilu).astype(bf16)
            ds = dyc * u * silu
            dbias_ref[:, h:h + 1] += jnp.sum(ds, axis=1, keepdims=True)
            ds = ds.astype(bf16)
            _accumulate(dw_ref.at[h], jnp.where(tril, _dot_nt(ds, vn_h), 0.0), first)
            dvn_ref[:, cols] = _dot_tn(ws[h], ds)
        dv, dg, db = _ln_bwd(xh, rstd, g, dvn_ref[...])
        dv_ref[...] = dv.astype(bf16)
        _accumulate(dg_ref, dg, first)
        _accumulate(db_ref, db, first)

    chunk, vec, w, bias = _sgu_specs()
    return pl.pallas_call(
        body, name="sgu_bwd", grid=(SEQ // CHUNK,),
        in_specs=[chunk(0), chunk(1), chunk(2), chunk(0), vec, vec, w, bias],
        out_specs=[chunk(0), chunk(0), chunk(0), w, bias, vec, vec],
        out_shape=[SDS((SEQ, HALF), bf16)] * 3 + [SDS((4, CHUNK, CHUNK), f32), SDS((CHUNK, CHUNK), f32),
                                                   SDS((1, HALF), f32), SDS((1, HALF), f32)],
        scratch_shapes=[pltpu.VMEM((CHUNK, HALF), f32)],
    )(z1, z1, z1, dcat, ln_g, ln_b, sgu_w, bias_t)


CONV_TILE = 128
DVAL_COL, DGLU_COL = 12, 16


def _conv_specs():
    val = pl.BlockSpec((SEQ, POOL_CH), lambda j: (0, DVAL_COL + j))
    glu = pl.BlockSpec((SEQ, POOL_CH), lambda j: (0, DGLU_COL + j))
    w = pl.BlockSpec((CONV_K, POOL_CH), lambda j: (0, j))
    col = pl.BlockSpec((SEQ, POOL_CH), lambda j: (0, j))
    vec = pl.BlockSpec((1, POOL_CH), lambda j: (0, j))
    return val, glu, w, col, vec


def _conv_fwd(z1, conv_w, conv_b):
    def body(val_ref, glu_ref, w_ref, b_ref, out_ref, xpad):
        xpad[0:CONV_PAD, :] = jnp.zeros((CONV_PAD, POOL_CH), f32)
        xpad[CONV_PAD:, :] = val_ref[...] * _sigmoid(glu_ref[...])
        w = w_ref[...]
        bias = b_ref[...]

        def tile(i, carry):
            t0 = pl.multiple_of(i * CONV_TILE, CONV_TILE)
            window = xpad[pl.ds(t0, CONV_TILE + CONV_PAD), :]
            acc = jnp.broadcast_to(bias, (CONV_TILE, POOL_CH))
            for k in range(CONV_K):
                shift = CONV_PAD - (CONV_K - 1) + k
                acc = acc + w[k:k + 1, :] * pltpu.roll(window, CONV_TILE + CONV_PAD - shift, 0)[0:CONV_TILE]
            out_ref[pl.ds(t0, CONV_TILE), :] = acc
            return carry

        lax.fori_loop(0, SEQ // CONV_TILE, tile, 0)

    val, glu, w, col, vec = _conv_specs()
    return pl.pallas_call(
        body, name="conv_fwd", grid=(4,), in_specs=[val, glu, w, vec], out_specs=col,
        out_shape=SDS((SEQ, HALF), f32), scratch_shapes=[pltpu.VMEM((SEQ + CONV_PAD, POOL_CH), f32)],
    )(z1, z1, conv_w, conv_b)


def _conv_bwd(z1, dconv, conv_w):
    def body(val_ref, glu_ref, w_ref, dout_ref, dval_ref, dglu_ref, dw_ref, db_ref, xpad, dpad, dx_ref):
        val = val_ref[...]
        sig = _sigmoid(glu_ref[...])
        xpad[0:CONV_PAD, :] = jnp.zeros((CONV_PAD, POOL_CH), f32)
        xpad[CONV_PAD:, :] = val * sig
        dout = dout_ref[...]
        dpad[0:SEQ, :] = dout
        dpad[SEQ:, :] = jnp.zeros((CONV_PAD, POOL_CH), f32)
        db_ref[...] = jnp.sum(dout, axis=0, keepdims=True)
        dw_ref[...] = jnp.zeros((CONV_K, POOL_CH), f32)
        w = w_ref[...]

        def tile(i, carry):
            t0 = pl.multiple_of(i * CONV_TILE, CONV_TILE)
            x_win = xpad[pl.ds(t0, CONV_TILE + CONV_PAD), :]
            d_win = dpad[pl.ds(t0, CONV_TILE + CONV_PAD), :]
            d_own = d_win[0:CONV_TILE]
            acc = jnp.zeros((CONV_TILE, POOL_CH), f32)
            for k in range(CONV_K):
                shift = CONV_PAD - (CONV_K - 1) + k
                x_k = pltpu.roll(x_win, CONV_TILE + CONV_PAD - shift, 0)[0:CONV_TILE]
                dw_ref[k:k + 1, :] += jnp.sum(d_own * x_k, axis=0, keepdims=True)
                back = CONV_K - 1 - k
                d_k = d_own if back == 0 else pltpu.roll(d_win, CONV_TILE + CONV_PAD - back, 0)[0:CONV_TILE]
                acc = acc + w[k:k + 1, :] * d_k
            dx_ref[pl.ds(t0, CONV_TILE), :] = acc
            return carry

        lax.fori_loop(0, SEQ // CONV_TILE, tile, 0)
        dx = dx_ref[...]
        dval_ref[...] = (dx * sig).astype(bf16)
        dglu_ref[...] = (dx * val * sig * (1.0 - sig)).astype(bf16)

    val, glu, w, col, vec = _conv_specs()
    pad = pltpu.VMEM((SEQ + CONV_PAD, POOL_CH), f32)
    return pl.pallas_call(
        body, name="conv_bwd", grid=(4,), in_specs=[val, glu, w, col], out_specs=[col, col, w, vec],
        out_shape=[SDS((SEQ, HALF), bf16), SDS((SEQ, HALF), bf16), SDS((CONV_K, HALF), f32), SDS((1, HALF), f32)],
        scratch_shapes=[pad, pad, pltpu.VMEM((SEQ, POOL_CH), f32)],
    )(z1, z1, conv_w, dconv)


DGATE_COL = 5


def _conv_norm_fwd(conv, z1, g, b):
    def body(c_ref, gate_ref, g_ref, b_ref, yd_ref):
        xh, _ = _ln_stats(c_ref[...])
        n = xh * g_ref[...] + b_ref[...]
        gate = gate_ref[...]
        yd_ref[...] = (n * _sigmoid(n) * gate * _sigmoid(gate)).astype(bf16)

    return pl.pallas_call(
        body, name="conv_norm_fwd", grid=(SEQ // ROWS,),
        in_specs=[_row_spec(HALF), _row_spec(HALF, DGATE_COL), _vec_spec(HALF), _vec_spec(HALF)],
        out_specs=_row_spec(HALF), out_shape=SDS((SEQ, HALF), bf16))(conv, z1, g, b)


def _conv_norm_bwd(conv, z1, dcat, g, b):
    def body(c_ref, gate_ref, dyd_ref, g_ref, b_ref, dconv_ref, dgate_ref, dg_ref, db_ref):
        first = pl.program_id(0) == 0
        xh, rstd = _ln_stats(c_ref[...])
        g = g_ref[...]
        n_silu, n_dsilu = _silu_and_grad(xh * g + b_ref[...])
        gate_silu, gate_dsilu = _silu_and_grad(gate_ref[...])
        dyd = dyd_ref[...]
        dgate_ref[...] = (dyd * n_silu * gate_dsilu).astype(bf16)
        dconv, dg, db = _ln_bwd(xh, rstd, g, dyd * gate_silu * n_dsilu)
        dconv_ref[...] = dconv
        _accumulate(dg_ref, dg, first)
        _accumulate(db_ref, db, first)

    return pl.pallas_call(
        body, name="conv_norm_bwd", grid=(SEQ // ROWS,),
        in_specs=[_row_spec(HALF), _row_spec(HALF, DGATE_COL), _row_spec(HALF, 1), _vec_spec(HALF), _vec_spec(HALF)],
        out_specs=[_row_spec(HALF), _row_spec(HALF), _vec_spec(HALF), _vec_spec(HALF)],
        out_shape=[SDS((SEQ, HALF), f32), SDS((SEQ, HALF), bf16), SDS((1, HALF), f32), SDS((1, HALF), f32)],
    )(conv, z1, dcat, g, b)


def _step(x, target, w, chip):
    chip_vec = chip.astype(jnp.int32).reshape(1)
    sharded_names = list(SHARDED_SMALL)
    small_shard = _pack([w[k] for k in sharded_names], total_rows=SMALL_SHARD_ROWS)
    small_slot = lax.dynamic_update_slice(jnp.zeros((N_CHIPS, SMALL_SHARD_ROWS, LANES), f32), small_shard[None], (chip, 0, 0))
    slots = [small_slot] + [_cast_into_slot(w["e_w_in"], chip_vec, f"cast_e_w_in{i}", i, E_IN_PIECES) for i in range(E_IN_PIECES)]
    slots += [_cast_into_slot(w[k], chip_vec, f"cast_{k}") for k in BIG[1:]]
    sems, bufs, token = _gather_start(slots)
    tables = _rope_tables()

    def vec(k):
        return w[k].reshape(1, -1)

    h0 = _pre_norm(x, vec("e_pre_norm") + token[0, 0])
    after, z0, e_w_in = h0, None, []
    for i in range(E_IN_PIECES):
        group = slice(0, 2) if i == 0 else slice(1 + i, 2 + i)
        landed = _forward_halves(_gather_wait(bufs[group], sems[group], after, f"gather_wait_{i}"), f"forward_{i}")
        if i == 0:
            small_full = landed[0]
        e_w_in.append(landed[-1])
        z0 = _mm_nn(h0, landed[-1], f32, f"e_in{i}", i, E_IN_PIECES, z0)
        after = z0
    p = {k: _from_chips(k, a) for k, a in zip(sharded_names, _unpack(small_full, [SHARDED_SMALL[k][0] for k in sharded_names]))}
    for k in ("o_pre_norm", "o_sgu_norm_g", "o_sgu_norm_b", "o_conv_b", "o_conv_norm_g", "o_conv_norm_b", "o_post_norm"):
        p[k] = p[k].reshape(1, -1)
    pool_w_bf = p["e_pool_w"].astype(bf16)
    bias_t = jnp.pad(w["o_sgu_b"].T, ((0, 0), (0, CHUNK - 4)))

    ya = _pool_fwd(z0, pool_w_bf, vec("e_pool_scale"))
    yb, att, lse, qkv_by_residue = _attn_fwd(z0, tables)

    def arrived(index, after, name):
        one = slice(index, index + 1)
        return _forward_halves(_gather_wait(bufs[one], sems[one], after, f"gather_wait_{name}"), f"forward_{name}")[0]

    e_w_out = arrived(1 + E_IN_PIECES, att, "e_w_out").reshape(1, D_MODEL, D_MODEL)
    cat0 = jnp.concatenate([ya, yb], axis=1)
    y0 = _mm_nn(cat0, e_w_out, f32, "e_out")
    x1, h1 = _mid_norm(x, y0, vec("e_post_norm"), p["o_pre_norm"])
    o_w_in = arrived(2 + E_IN_PIECES, h1, "o_w_in")
    z1 = _mm_nn(h1, o_w_in, f32, "o_in")
    yc = _sgu_fwd(z1, p["o_sgu_norm_g"], p["o_sgu_norm_b"], w["o_sgu_w"], bias_t)
    conv = _conv_fwd(z1, p["o_conv_w"], p["o_conv_b"])
    yd = _conv_norm_fwd(conv, z1, p["o_conv_norm_g"], p["o_conv_norm_b"])
    o_w_out = arrived(3 + E_IN_PIECES, yd, "o_w_out").reshape(1, D_MODEL, D_MODEL)
    cat1 = jnp.concatenate([yc, yd], axis=1)
    y1 = _mm_nn(cat1, o_w_out, f32, "o_out")
    loss, dx2, dy1, g_o_post = _final_norm_loss(x1, y1, p["o_post_norm"], target)

    in_flight = {}

    def send_off(name, grad):
        sem, sums, land, tok = _scatter_start(_swap_add(grad, f"swap_add_{name}"), f"scatter_start_{name}")
        in_flight[name] = (sem, sums, land)
        return tok

    tok = send_off("o_w_out", _mm_tn(cat1, dy1, 1, "o_out_dw").reshape(N_CHIPS, HALF // 2, D_MODEL))
    dcat1 = _mm_nt(dy1, [o_w_out], "o_out_dx", tok)
    du, dv, dcg, g_sgu_w, g_bias_t, g_sgu_g, g_sgu_b = _sgu_bwd(
        z1, dcat1, p["o_sgu_norm_g"] + tok[0, 0], p["o_sgu_norm_b"], w["o_sgu_w"], bias_t)
    dconv, ddgate, g_cn_g, g_cn_b = _conv_norm_bwd(conv, z1, dcat1, p["o_conv_norm_g"], p["o_conv_norm_b"])
    ddval, ddglu, g_conv_w, g_conv_b = _conv_bwd(z1, dconv, p["o_conv_w"])
    dz1 = jnp.concatenate([du, dv, dcg, ddval, ddglu, ddgate], axis=1)
    tok = send_off("o_w_in", _mm_tn(h1, dz1, N_CHIPS, "o_in_dw"))
    dh1 = _mm_nt(dz1, [o_w_in], "o_in_dx", tok)
    dx1, dy0, g_o_pre, g_e_post = _mid_norm_bwd(dx2, dh1, x1, y0, p["o_pre_norm"] + tok[0, 0], vec("e_post_norm"))

    tok = send_off("e_w_out", _mm_tn(cat0, dy0, 1, "e_out_dw").reshape(N_CHIPS, HALF // 2, D_MODEL))
    dcat0 = _mm_nt(dy0, [e_w_out], "e_out_dx", tok)
    da, dagate, g_pool_w, g_pool_scale = _pool_bwd(z0, dcat0, pool_w_bf, vec("e_pool_scale") + tok[0, 0])
    dq0, dk0, dv0, dbgate = _attn_bwd_group(0, qkv_by_residue[0], z0, att, lse, dcat0, tables)
    dq1, dk1, dv1 = _attn_bwd_group(1, qkv_by_residue[1], z0, att, lse, dcat0, tables)
    dq2, dk2, dv2 = _attn_bwd_group(2, qkv_by_residue[2], z0, att, lse, dcat0, tables)
    dz0 = jnp.concatenate([da, dagate, dq0, dq1, dq2, dk0, dk1, dk2, dv0, dv1, dv2, dbgate], axis=1)
    tok = send_off("e_w_in", _mm_tn(h0, dz0, N_CHIPS, "e_in_dw"))
    dh0 = _mm_nt(dz0, e_w_in, "e_in_dx", tok)
    grad_x, g_e_pre = _pre_norm_bwd(dx1, dh0, x, vec("e_pre_norm") + tok[0, 0])

    small = {"e_pre_norm": g_e_pre, "e_pool_w": g_pool_w, "e_pool_scale": g_pool_scale, "e_post_norm": g_e_post,
             "o_pre_norm": g_o_pre, "o_sgu_norm_g": g_sgu_g, "o_sgu_norm_b": g_sgu_b, "o_sgu_w": g_sgu_w,
             "o_sgu_b": g_bias_t[:, 0:4].T, "o_conv_w": g_conv_w, "o_conv_b": g_conv_b,
             "o_conv_norm_g": g_cn_g, "o_conv_norm_b": g_cn_b, "o_post_norm": g_o_post}
    return loss, grad_x, in_flight, small


def _land(in_flight, name, chip, after):
    sems, sums, land = in_flight[name]
    sums, land = _scatter_wait(sems, sums, land, after, f"scatter_wait_{name}")
    return _add_landed_join(sums, land, chip.astype(jnp.int32).reshape(1), f"add_landed_{name}")


def _place():
    x, y, c = lax.axis_index("x"), lax.axis_index("y"), lax.axis_index("c")
    others = [(1 - x, y), (x, 1 - y), (1 - x, 1 - y)]
    return x, y, c, 2 * x + y, others


def _all_gather(shards, name):
    n = len(shards)

    def body(*refs):
        ins, outs = refs[:n], refs[n:2 * n]
        send_sems, recv_sems, local_sems = refs[2 * n:]
        x, y, c, me, others = _place()
        sibling = (x, y, 1 - c)

        def half(a, chip, core):
            rows = ins[a].shape[0] // 2
            return outs[a].at[chip, pl.ds(core * rows, rows), :]

        def copy(a, k, src, dst, to):
            return pltpu.make_async_remote_copy(src_ref=src, dst_ref=dst, send_sem=send_sems.at[6 * a + k],
                                                recv_sem=recv_sems.at[6 * a + k], device_id=to, device_id_type=MESH)

        local = [pltpu.make_async_copy(ins[a], outs[a].at[me], local_sems.at[a]) for a in range(n)]
        for cp in local:
            cp.start()
        sent = []
        for a in range(n):
            rows = ins[a].shape[0] // 2
            mine = ins[a].at[pl.ds(c * rows, rows), :]
            for k, (ox, oy) in enumerate(others):
                sent.append(copy(a, k, mine, half(a, me, c), (ox, oy, c)))
                sent[-1].start()
        for a in range(n):
            for k, (ox, oy) in enumerate(others):
                landed = half(a, 2 * ox + oy, c)
                copy(a, k, landed, landed, (ox, oy, c)).wait_recv()
                sent.append(copy(a, 3 + k, landed, landed, sibling))
                sent[-1].start()
        for k, (ox, oy) in enumerate(others):
            chip = 2 * ox + oy
            for a in range(n):
                theirs = half(a, chip, 1 - c)
                copy(a, 3 + k, theirs, theirs, sibling).wait_recv()
        for cp in sent:
            cp.wait_send()
        for cp in local:
            cp.wait()

    return pl.pallas_call(
        body, name=name, in_specs=[ANY] * n, out_specs=[ANY] * n,
        out_shape=[SDS((N_CHIPS,) + s.shape, s.dtype) for s in shards],
        scratch_shapes=[pltpu.SemaphoreType.DMA((6 * n,)), pltpu.SemaphoreType.DMA((6 * n,)), pltpu.SemaphoreType.DMA((n,))],
    )(*shards)


def _swap_halves(parts, name):
    n = len(parts)

    def body(*refs):
        ins, own, theirs = refs[:n], refs[n:2 * n], refs[2 * n:3 * n]
        send_sems, recv_sems, local_sems = refs[3 * n:]
        x, y, c, _, _ = _place()
        sibling = (x, y, 1 - c)
        copies = []
        for a in range(n):
            rows = ins[a].shape[1] // 2
            keep = pltpu.make_async_copy(ins[a].at[:, pl.ds(c * rows, rows), :], own[a], local_sems.at[a])
            give = pltpu.make_async_remote_copy(
                src_ref=ins[a].at[:, pl.ds((1 - c) * rows, rows), :], dst_ref=theirs[a], send_sem=send_sems.at[a],
                recv_sem=recv_sems.at[a], device_id=sibling, device_id_type=MESH)
            keep.start()
            give.start()
            copies += [keep, give]
        for cp in copies:
            cp.wait()

    half = [SDS((N_CHIPS, s.shape[1] // 2, s.shape[2]), s.dtype) for s in parts]
    out = pl.pallas_call(
        body, name=name, in_specs=[ANY] * n, out_specs=[ANY] * (2 * n), out_shape=half + half,
        scratch_shapes=[pltpu.SemaphoreType.DMA((n,)), pltpu.SemaphoreType.DMA((n,)), pltpu.SemaphoreType.DMA((n,))],
    )(*parts)
    return out[:n], out[n:]


def _scatter_chips(parts, name):
    n = len(parts)

    def body(*refs):
        ins, outs = refs[:n], refs[n:2 * n]
        send_sems, recv_sems, local_sems = refs[2 * n:]
        x, y, c, me, others = _place()

        def copy(a, k, slot_from, slot_to, chip_xy):
            return pltpu.make_async_remote_copy(
                src_ref=ins[a].at[slot_from], dst_ref=outs[a].at[slot_to], send_sem=send_sems.at[3 * a + k],
                recv_sem=recv_sems.at[3 * a + k], device_id=(chip_xy[0], chip_xy[1], c), device_id_type=MESH)

        keeps, gives = [], []
        for a in range(n):
            keeps.append(pltpu.make_async_copy(ins[a].at[me], outs[a].at[me], local_sems.at[a]))
            keeps[-1].start()
            for k, (ox, oy) in enumerate(others):
                gives.append(copy(a, k, 2 * ox + oy, me, (ox, oy)))
                gives[-1].start()
        for a in range(n):
            for k, (ox, oy) in enumerate(others):
                copy(a, k, me, 2 * ox + oy, (ox, oy)).wait_recv()
        for cp in gives:
            cp.wait_send()
        for cp in keeps:
            cp.wait()

    return pl.pallas_call(
        body, name=name, in_specs=[ANY] * n, out_specs=[ANY] * n, out_shape=[SDS(s.shape, s.dtype) for s in parts],
        scratch_shapes=[pltpu.SemaphoreType.DMA((3 * n,)), pltpu.SemaphoreType.DMA((3 * n,)), pltpu.SemaphoreType.DMA((n,))],
    )(*parts)


def _join_halves(halves, name):
    n = len(halves)

    def body(*refs):
        ins, outs = refs[:n], refs[n:2 * n]
        send_sems, recv_sems, local_sems = refs[2 * n:]
        x, y, c, _, _ = _place()

        def copy(a, core):
            rows = ins[a].shape[0]
            return pltpu.make_async_remote_copy(
                src_ref=ins[a], dst_ref=outs[a].at[pl.ds(core * rows, rows), :], send_sem=send_sems.at[a],
                recv_sem=recv_sems.at[a], device_id=(x, y, 1 - c), device_id_type=MESH)

        keeps, gives = [], []
        for a in range(n):
            rows = ins[a].shape[0]
            keeps.append(pltpu.make_async_copy(ins[a], outs[a].at[pl.ds(c * rows, rows), :], local_sems.at[a]))
            gives.append(copy(a, c))
            keeps[-1].start()
            gives[-1].start()
        for a in range(n):
            copy(a, 1 - c).wait_recv()
        for cp in gives:
            cp.wait_send()
        for cp in keeps:
            cp.wait()

    return pl.pallas_call(
        body, name=name, in_specs=[ANY] * n, out_specs=[ANY] * n,
        out_shape=[SDS((2 * s.shape[0], s.shape[1]), s.dtype) for s in halves],
        scratch_shapes=[pltpu.SemaphoreType.DMA((n,)), pltpu.SemaphoreType.DMA((n,)), pltpu.SemaphoreType.DMA((n,))],
    )(*halves)


def _add_pair(a, b, name):
    _, r, c = a.shape
    tr = 256 if r % 256 == 0 else r // 2 if r > 512 else r

    def body(a_ref, b_ref, o_ref):
        o_ref[...] = (a_ref[...].astype(f32) + b_ref[...].astype(f32)).astype(o_ref.dtype)

    spec = pl.BlockSpec((None, tr, c), lambda j, i: (j, i, 0))
    return pl.pallas_call(body, name=name, grid=(N_CHIPS, r // tr), in_specs=[spec, spec], out_specs=spec,
                          out_shape=SDS(a.shape, a.dtype))(a, b)


def _add_chips(u, name):
    _, r, c = u.shape
    tr = 256 if r % 256 == 0 else r

    def body(u_ref, o_ref):
        o_ref[...] = ((u_ref[0].astype(f32) + u_ref[1].astype(f32)) + u_ref[2].astype(f32)) + u_ref[3].astype(f32)

    return pl.pallas_call(
        body, name=name, grid=(r // tr,), in_specs=[pl.BlockSpec((N_CHIPS, tr, c), lambda i: (0, i, 0))],
        out_specs=pl.BlockSpec((tr, c), lambda i: (i, 0)), out_shape=SDS((r, c), f32))(u)


SWAP_ROWS = 256


def _swap_add(g, name):
    chips, r, c = g.shape
    half = r // 2
    rows_per_step = 2 * SWAP_ROWS if half % (2 * SWAP_ROWS) == 0 else SWAP_ROWS
    nb = half // rows_per_step
    steps = chips * nb

    def body(core_ref, mine_ref, theirs_ref, out_ref, landing, send_sems, recv_sems, free_sems):
        i = pl.program_id(0)
        slot = i % 2
        x, y, core, _, _ = _place()
        sibling = (x, y, 1 - core)

        @pl.when(i >= 2)
        def _():
            pl.semaphore_wait(free_sems.at[slot], 1)

        send = pltpu.make_async_remote_copy(src_ref=theirs_ref, dst_ref=landing.at[slot], send_sem=send_sems.at[slot],
                                            recv_sem=recv_sems.at[slot], device_id=sibling, device_id_type=MESH)
        send.start()
        send.wait_recv()
        out_ref[...] = (mine_ref[...].astype(f32) + landing[slot].astype(f32)).astype(out_ref.dtype)

        @pl.when(i + 2 < steps)
        def _():
            pl.semaphore_signal(free_sems.at[slot], 1, device_id=sibling, device_id_type=MESH)

        send.wait_send()

    block = (rows_per_step, c)
    grid_spec = pltpu.PrefetchScalarGridSpec(
        num_scalar_prefetch=1, grid=(steps,),
        in_specs=[pl.BlockSpec(block, lambda i, core: ((2 * (i // nb) + core[0]) * nb + i % nb, 0)),
                  pl.BlockSpec(block, lambda i, core: ((2 * (i // nb) + 1 - core[0]) * nb + i % nb, 0))],
        out_specs=pl.BlockSpec(block, lambda i, core: (i, 0)),
        scratch_shapes=[pltpu.VMEM((2, rows_per_step, c), g.dtype), pltpu.SemaphoreType.DMA((2,)),
                        pltpu.SemaphoreType.DMA((2,)), pltpu.SemaphoreType.REGULAR((2,))])
    core = lax.axis_index("c").astype(jnp.int32).reshape(1)
    rows = g.reshape(chips * r, c)
    out = pl.pallas_call(body, name=name, grid_spec=grid_spec, out_shape=SDS((chips * half, c), g.dtype))(core, rows, rows)
    return out.reshape(chips, half, c)


def _reduce_scatter(parts, tag):
    own, theirs = _swap_halves(parts, f"swap_halves_{tag}")
    chip_sums = [_add_pair(o, t, f"add_cores_{tag}{i}") for i, (o, t) in enumerate(zip(own, theirs))]
    gathered = _scatter_chips(chip_sums, f"scatter_chips_{tag}")
    halves = [_add_chips(u, f"add_chips_{tag}{i}") for i, u in enumerate(gathered)]
    return _join_halves(halves, f"join_halves_{tag}")


HBM = pl.BlockSpec(memory_space=pltpu.HBM)
SEM = pl.BlockSpec(memory_space=pltpu.SEMAPHORE)
EFFECT = pltpu.SideEffectType.DATAFLOW_SIDE_EFFECTING


def _in_hbm(a):
    return pltpu.with_memory_space_constraint(a, pltpu.HBM)


def _cast_into_slot(w, chip, name, piece=0, pieces=1):
    r, c = w.shape
    c = c // pieces
    nb = r // SWAP_ROWS

    def body(chip_ref, w_ref, o_ref):
        o_ref[...] = w_ref[...].astype(bf16)

    grid_spec = pltpu.PrefetchScalarGridSpec(
        num_scalar_prefetch=1, grid=(nb,),
        in_specs=[pl.BlockSpec((SWAP_ROWS, c), lambda i, chip: (i, piece))],
        out_specs=pl.BlockSpec((SWAP_ROWS, c), lambda i, chip: (chip[0] * nb + i, 0)))
    out = pl.pallas_call(body, name=name, grid_spec=grid_spec, out_shape=SDS((N_CHIPS * r, c), bf16))(chip, w)
    return out.reshape(N_CHIPS, r, c)


def _gather_start(bufs):
    n = len(bufs)

    def body(*refs):
        ins, sems, token = refs[:n], refs[n:3 * n], refs[4 * n]
        x, y, c, me, others = _place()
        for a in range(n):
            rows = ins[a].shape[1] // 2
            mine = ins[a].at[me, pl.ds(c * rows, rows), :]
            for k, (ox, oy) in enumerate(others):
                pltpu.make_async_remote_copy(src_ref=mine, dst_ref=mine, send_sem=sems[2 * a].at[k],
                                             recv_sem=sems[2 * a + 1].at[k], device_id=(ox, oy, c),
                                             device_id_type=MESH).start()
        token[...] = jnp.zeros_like(token)

    out = pl.pallas_call(
        body, name="gather_start", in_specs=[HBM] * n,
        out_shape=(*[pltpu.SemaphoreType.DMA((3,))] * (2 * n), *[pltpu.HBM(b.shape, b.dtype) for b in bufs],
                   SDS((8, 128), f32)),
        out_specs=(*[SEM] * (2 * n), *[HBM] * n, pl.BlockSpec(memory_space=pltpu.VMEM)),
        input_output_aliases={a: 2 * n + a for a in range(n)},
        compiler_params=pltpu.CompilerParams(has_side_effects=EFFECT),
    )(*[_in_hbm(b) for b in bufs])
    return [(out[2 * a], out[2 * a + 1]) for a in range(n)], list(out[2 * n:3 * n]), out[3 * n]


def _gather_wait(bufs, sems, after, name):
    n = len(bufs)

    def body(*refs):
        ins, sem_refs = refs[:n], refs[n:3 * n]
        x, y, c, me, others = _place()
        for a in range(n):
            rows = ins[a].shape[1] // 2
            mine = ins[a].at[me, pl.ds(c * rows, rows), :]
            for k, (ox, oy) in enumerate(others):
                landed = ins[a].at[2 * ox + oy, pl.ds(c * rows, rows), :]
                copy = pltpu.make_async_remote_copy(src_ref=mine, dst_ref=landed, send_sem=sem_refs[2 * a].at[k],
                                                    recv_sem=sem_refs[2 * a + 1].at[k], device_id=(ox, oy, c),
                                                    device_id_type=MESH)
                copy.wait_send()
                copy.wait_recv()

    flat_sems = [s for pair in sems for s in pair]
    out = pl.pallas_call(
        body, name=name, in_specs=[HBM] * n + [SEM] * (2 * n) + [ANY],
        out_shape=tuple(pltpu.HBM(b.shape, b.dtype) for b in bufs), out_specs=tuple([HBM] * n),
        input_output_aliases={a: a for a in range(n)},
        compiler_params=pltpu.CompilerParams(has_side_effects=EFFECT),
    )(*bufs, *flat_sems, after)
    return list(out)


def _forward_halves(bufs, name):
    n = len(bufs)
    blocks = []
    for b in bufs:
        half = b.shape[1] // 2
        tr = SWAP_ROWS if half % SWAP_ROWS == 0 else half
        blocks.append((half, tr))
    work = [(a, k, b) for a in range(n) for k in range(3) for b in range(blocks[a][0] // blocks[a][1])]

    def body(*refs):
        outs, stages = refs[n:2 * n], refs[2 * n:3 * n]
        load_sems, send_sems, recv_sems = refs[3 * n:]
        x, y, c, me, others = _place()
        sibling = (x, y, 1 - c)

        def rows(item):
            a, k, b = item
            half, tr = blocks[a]
            ox, oy = others[k]
            return outs[a].at[2 * ox + oy, pl.ds(c * half + b * tr, tr), :]

        def load(s, item):
            return pltpu.make_async_copy(rows(item), stages[item[0]].at[s], load_sems.at[s])

        def send(s, item):
            return pltpu.make_async_remote_copy(src_ref=stages[item[0]].at[s], dst_ref=rows(item), send_sem=send_sems.at[s],
                                                recv_sem=recv_sems.at[item[0]], device_id=sibling, device_id_type=MESH)

        load(0, work[0]).start()
        for t, item in enumerate(work):
            s = t % 2
            load(s, item).wait()
            send(s, item).start()
            if t + 1 < len(work):
                if t >= 1:
                    send(1 - s, work[t - 1]).wait_send()
                load(1 - s, work[t + 1]).start()
        if len(work) > 1:
            send(len(work) % 2, work[-2]).wait_send()
        send((len(work) - 1) % 2, work[-1]).wait_send()
        for a in range(n):
            theirs = outs[a].at[pl.ds(0, 3), pl.ds(0, blocks[a][0]), :]
            pltpu.make_async_remote_copy(src_ref=theirs, dst_ref=theirs, send_sem=send_sems.at[0], recv_sem=recv_sems.at[a],
                                         device_id=sibling, device_id_type=MESH).wait_recv()

    out = pl.pallas_call(
        body, name=name, in_specs=[ANY] * n, out_specs=[ANY] * n, out_shape=[SDS(b.shape, b.dtype) for b in bufs],
        input_output_aliases={a: a for a in range(n)},
        scratch_shapes=[pltpu.VMEM((2, blocks[a][1], bufs[a].shape[2]), bufs[a].dtype) for a in range(n)]
        + [pltpu.SemaphoreType.DMA((2,)), pltpu.SemaphoreType.DMA((2,)), pltpu.SemaphoreType.DMA((n,))],
    )(*bufs)
    return list(out)


def _scatter_start(chip_sums, name):
    def body(a_ref, land_ref, send_sems, recv_sems, a_thru, land_thru, token):
        x, y, c, me, others = _place()
        for k, (ox, oy) in enumerate(others):
            pltpu.make_async_remote_copy(src_ref=a_ref.at[2 * ox + oy], dst_ref=land_ref.at[me], send_sem=send_sems.at[k],
                                         recv_sem=recv_sems.at[k], device_id=(ox, oy, c), device_id_type=MESH).start()
        token[...] = jnp.zeros_like(token)

    shape = pltpu.HBM(chip_sums.shape, chip_sums.dtype)
    send, recv, a_thru, land, token = pl.pallas_call(
        body, name=name, in_specs=[HBM, HBM],
        out_shape=(pltpu.SemaphoreType.DMA((3,)), pltpu.SemaphoreType.DMA((3,)), shape, shape, SDS((8, 128), f32)),
        out_specs=(SEM, SEM, HBM, HBM, pl.BlockSpec(memory_space=pltpu.VMEM)), input_output_aliases={0: 2, 1: 3},
        compiler_params=pltpu.CompilerParams(has_side_effects=EFFECT),
    )(_in_hbm(chip_sums), _in_hbm(lax.empty(chip_sums.shape, chip_sums.dtype)))
    return (send, recv), a_thru, land, token


def _scatter_wait(sems, chip_sums, land, after, name):
    def body(a_ref, land_ref, send_sems, recv_sems, after_ref, a_out, land_out):
        x, y, c, me, others = _place()
        for k, (ox, oy) in enumerate(others):
            copy = pltpu.make_async_remote_copy(
                src_ref=a_ref.at[2 * ox + oy], dst_ref=land_ref.at[2 * ox + oy], send_sem=send_sems.at[k],
                recv_sem=recv_sems.at[k], device_id=(ox, oy, c), device_id_type=MESH)
            copy.wait_send()
            copy.wait_recv()

    shape = pltpu.HBM(chip_sums.shape, chip_sums.dtype)
    return pl.pallas_call(
        body, name=name, in_specs=[HBM, HBM, SEM, SEM, ANY], out_shape=(shape, shape), out_specs=(HBM, HBM),
        input_output_aliases={0: 0, 1: 1}, compiler_params=pltpu.CompilerParams(has_side_effects=EFFECT),
    )(chip_sums, land, sems[0], sems[1], after)


def _add_landed_join(chip_sums, land, chip, name):
    chips, rh, c = chip_sums.shape
    nb = rh // SWAP_ROWS

    def body(chip_ref, own_ref, l1_ref, l2_ref, l3_ref, out_hbm, buf, send_sems, recv_sem, local_sems):
        i = pl.program_id(0)
        slot = i % 2
        x, y, core, _, _ = _place()
        sibling = (x, y, 1 - core)

        def copies(s, step):
            rows = pl.ds(pl.multiple_of((core * nb + step) * SWAP_ROWS, SWAP_ROWS), SWAP_ROWS)
            keep = pltpu.make_async_copy(buf.at[s], out_hbm.at[rows, :], local_sems.at[s])
            give = pltpu.make_async_remote_copy(src_ref=buf.at[s], dst_ref=out_hbm.at[rows, :], send_sem=send_sems.at[s],
                                                recv_sem=recv_sem.at[0], device_id=sibling, device_id_type=MESH)
            return keep, give

        def drain(s, step):
            keep, give = copies(s, step)
            keep.wait()
            give.wait_send()

        @pl.when(i >= 2)
        def _():
            drain(slot, i - 2)

        buf[slot] = ((own_ref[...].astype(f32) + l1_ref[...].astype(f32)) + l2_ref[...].astype(f32)) + l3_ref[...].astype(f32)
        keep, give = copies(slot, i)
        keep.start()
        give.start()

        @pl.when(i == nb - 1)
        def _():
            drain(slot, i)
            if nb > 1:
                drain(1 - slot, i - 1)
            theirs = out_hbm.at[pl.ds((1 - core) * rh, rh), :]
            pltpu.make_async_remote_copy(src_ref=theirs, dst_ref=theirs, send_sem=send_sems.at[0], recv_sem=recv_sem.at[0],
                                         device_id=sibling, device_id_type=MESH).wait_recv()

    block = (SWAP_ROWS, c)
    from_slot = lambda d: pl.BlockSpec(block, lambda i, chip: (((chip[0] + d) % chips) * nb + i, 0))
    grid_spec = pltpu.PrefetchScalarGridSpec(
        num_scalar_prefetch=1, grid=(nb,), in_specs=[from_slot(0), from_slot(1), from_slot(2), from_slot(3)],
        out_specs=ANY,
        scratch_shapes=[pltpu.VMEM((2, SWAP_ROWS, c), f32), pltpu.SemaphoreType.DMA((2,)),
                        pltpu.SemaphoreType.DMA((1,)), pltpu.SemaphoreType.DMA((2,))])
    land_rows = land.reshape(chips * rh, c)
    return pl.pallas_call(body, name=name, grid_spec=grid_spec, out_shape=SDS((2 * rh, c), f32))(
        chip, chip_sums.reshape(chips * rh, c), land_rows, land_rows, land_rows)


def _adamw_update(w_ref, g_ref, m_ref, v_ref, d_ref, nm_ref, nv_ref):
    g = g_ref[...]
    nm = ADAM_B1 * m_ref[...] + (1.0 - ADAM_B1) * g
    nv = ADAM_B2 * v_ref[...] + (1.0 - ADAM_B2) * (g * g)
    nm_ref[...] = nm
    nv_ref[...] = nv
    m_hat = nm / (1.0 - ADAM_B1 ** ADAM_STEP)
    v_hat = nv / (1.0 - ADAM_B2 ** ADAM_STEP)
    d_ref[...] = -ADAM_LR * (m_hat / (jnp.sqrt(v_hat) + ADAM_EPS) + ADAM_WD * w_ref[...])


def _adamw(w, g, m, v, name):
    r, c = w.shape
    tr = 128 if r % 128 == 0 else r
    spec = pl.BlockSpec((tr, c), lambda i: (i, 0))
    return pl.pallas_call(functools.partial(_adamw_update), name=name, grid=(r // tr,), in_specs=[spec] * 4,
                          out_specs=[spec] * 3, out_shape=[SDS((r, c), f32)] * 3)(w, g, m, v)


def _adamw_small(ws, gs, ms, vs):
    n = len(ws)

    def body(*refs):
        for i in range(n):
            _adamw_update(*refs[i:7 * n:n])

    whole = pl.BlockSpec(memory_space=pltpu.VMEM)
    out = pl.pallas_call(body, name="adamw_small", in_specs=[whole] * (4 * n), out_specs=[whole] * (3 * n),
                         out_shape=[SDS(a.shape, f32) for a in ws] * 3)(*ws, *gs, *ms, *vs)
    return out[:n], out[n:2 * n], out[2 * n:]


def _pack(arrays, total_rows=None):
    parts = []
    rows = 0
    for a in arrays:
        flat = a.reshape(-1, LANES)
        pad = -flat.shape[0] % 8
        parts.append(jnp.pad(flat, ((0, pad), (0, 0))))
        rows += flat.shape[0] + pad
    if total_rows is not None:
        parts.append(jnp.zeros((total_rows - rows, LANES), arrays[0].dtype))
    return jnp.concatenate(parts, axis=0)


def _unpack(buf, shapes):
    out = []
    row = 0
    lead = buf.shape[:-2]
    for shape in shapes:
        size = 1
        for s in shape:
            size *= s
        rows = size // LANES
        out.append(buf[..., row:row + rows, :].reshape(lead + tuple(shape)))
        row += rows + (-rows % 8)
    return out


BIG = ("e_w_in", "e_w_out", "o_w_in", "o_w_out")
SHARDED_SMALL = {
    "e_pool_w": ((4, 64, 256), 1), "o_pre_norm": ((512,), 0), "o_sgu_norm_g": ((256,), 0), "o_sgu_norm_b": ((256,), 0),
    "o_conv_w": ((31, 256), 1), "o_conv_b": ((256,), 0), "o_conv_norm_g": ((256,), 0), "o_conv_norm_b": ((256,), 0),
    "o_post_norm": ((512,), 0),
}
REPLICATED_SMALL = {"e_pre_norm": (2048,), "e_pool_scale": (1024,), "e_post_norm": (2048,),
                    "o_sgu_w": (4, 128, 128), "o_sgu_b": (4, 128)}
SMALL_ORDER = ("e_pre_norm", "e_pool_w", "e_pool_scale", "e_post_norm", "o_pre_norm", "o_sgu_norm_g", "o_sgu_norm_b",
               "o_sgu_w", "o_sgu_b", "o_conv_w", "o_conv_b", "o_conv_norm_g", "o_conv_norm_b", "o_post_norm")
ALL_ORDER = ("e_pre_norm", "e_w_in", "e_pool_w", "e_pool_scale", "e_w_out", "e_post_norm", "o_pre_norm", "o_w_in",
             "o_sgu_norm_g", "o_sgu_norm_b", "o_sgu_w", "o_sgu_b", "o_conv_w", "o_conv_b", "o_conv_norm_g",
             "o_conv_norm_b", "o_w_out", "o_post_norm")


def _full_shape(name):
    shape, axis = SHARDED_SMALL[name]
    return tuple(s * N_CHIPS if i == axis else s for i, s in enumerate(shape))


def _from_chips(name, stacked):
    shape, axis = SHARDED_SMALL[name]
    return jnp.moveaxis(stacked, 0, axis).reshape(_full_shape(name))


def _my_shard(name, full, chip):
    shape, axis = SHARDED_SMALL[name]
    return lax.dynamic_slice_in_dim(full, chip * shape[axis], shape[axis], axis)


def kernel(x, e_pre_norm, e_w_in, e_pool_w, e_pool_scale, e_w_out, e_post_norm, o_pre_norm, o_w_in, o_sgu_norm_g, o_sgu_norm_b, o_sgu_w, o_sgu_b, o_conv_w, o_conv_b, o_conv_norm_g, o_conv_norm_b, o_w_out, o_post_norm, loss_target, m_e_pre_norm, m_e_w_in, m_e_pool_w, m_e_pool_scale, m_e_w_out, m_e_post_norm, m_o_pre_norm, m_o_w_in, m_o_sgu_norm_g, m_o_sgu_norm_b, m_o_sgu_w, m_o_sgu_b, m_o_conv_w, m_o_conv_b, m_o_conv_norm_g, m_o_conv_norm_b, m_o_w_out, m_o_post_norm, v_e_pre_norm, v_e_w_in, v_e_pool_w, v_e_pool_scale, v_e_w_out, v_e_post_norm, v_o_pre_norm, v_o_w_in, v_o_sgu_norm_g, v_o_sgu_norm_b, v_o_sgu_w, v_o_sgu_b, v_o_conv_w, v_o_conv_b, v_o_conv_norm_g, v_o_conv_norm_b, v_o_w_out, v_o_post_norm):
    w = dict(e_pre_norm=e_pre_norm, e_w_in=e_w_in, e_pool_w=e_pool_w, e_pool_scale=e_pool_scale, e_w_out=e_w_out,
             e_post_norm=e_post_norm, o_pre_norm=o_pre_norm, o_w_in=o_w_in, o_sgu_norm_g=o_sgu_norm_g,
             o_sgu_norm_b=o_sgu_norm_b, o_sgu_w=o_sgu_w, o_sgu_b=o_sgu_b, o_conv_w=o_conv_w, o_conv_b=o_conv_b,
             o_conv_norm_g=o_conv_norm_g, o_conv_norm_b=o_conv_norm_b, o_w_out=o_w_out, o_post_norm=o_post_norm)
    m = dict(e_pre_norm=m_e_pre_norm, e_w_in=m_e_w_in, e_pool_w=m_e_pool_w, e_pool_scale=m_e_pool_scale,
             e_w_out=m_e_w_out, e_post_norm=m_e_post_norm, o_pre_norm=m_o_pre_norm, o_w_in=m_o_w_in,
             o_sgu_norm_g=m_o_sgu_norm_g, o_sgu_norm_b=m_o_sgu_norm_b, o_sgu_w=m_o_sgu_w, o_sgu_b=m_o_sgu_b,
             o_conv_w=m_o_conv_w, o_conv_b=m_o_conv_b, o_conv_norm_g=m_o_conv_norm_g, o_conv_norm_b=m_o_conv_norm_b,
             o_w_out=m_o_w_out, o_post_norm=m_o_post_norm)
    v = dict(e_pre_norm=v_e_pre_norm, e_w_in=v_e_w_in, e_pool_w=v_e_pool_w, e_pool_scale=v_e_pool_scale,
             e_w_out=v_e_w_out, e_post_norm=v_e_post_norm, o_pre_norm=v_o_pre_norm, o_w_in=v_o_w_in,
             o_sgu_norm_g=v_o_sgu_norm_g, o_sgu_norm_b=v_o_sgu_norm_b, o_sgu_w=v_o_sgu_w, o_sgu_b=v_o_sgu_b,
             o_conv_w=v_o_conv_w, o_conv_b=v_o_conv_b, o_conv_norm_g=v_o_conv_norm_g, o_conv_norm_b=v_o_conv_norm_b,
             o_w_out=v_o_w_out, o_post_norm=v_o_post_norm)
    w, m, v = ({k: a[0] for k, a in d.items()} for d in (w, m, v))
    chip = 2 * lax.axis_index("x") + lax.axis_index("y")

    loss, grad_x, in_flight, small = _step(x[0], loss_target[0], w, chip)

    grads, delta, new_m, new_v = {}, {}, {}, {}
    after = grad_x
    for k in ("o_w_out", "o_w_in", "e_w_out", "e_w_in"):
        grads[k] = _land(in_flight, k, chip, after)
        delta[k], new_m[k], new_v[k] = _adamw(w[k], grads[k], m[k], v[k], f"adamw_{k}")
        after = delta[k]

    small_full_shapes = {k: (_full_shape(k) if k in SHARDED_SMALL else REPLICATED_SMALL[k]) for k in SMALL_ORDER}
    small_parts = _pack([small[k].reshape(small_full_shapes[k]) for k in SMALL_ORDER], total_rows=SMALL_GRAD_ROWS)
    small_parts = small_parts + 0.0 * after[0, 0]
    reduced = _reduce_scatter([small_parts.reshape(N_CHIPS, SMALL_GRAD_ROWS // N_CHIPS, LANES)], "small")
    small_sum = _all_gather(reduced, "gather_small_grads")[0].reshape(SMALL_GRAD_ROWS, LANES)
    for k, a in zip(SMALL_ORDER, _unpack(small_sum, [small_full_shapes[k] for k in SMALL_ORDER])):
        grads[k] = _my_shard(k, a, chip) if k in SHARDED_SMALL else a
    loss = lax.psum(loss[0, 0], ("x", "y", "c"))

    def rows_of(a):
        return a.reshape(-1, a.shape[-1])

    updates = _adamw_small(*[[rows_of(d[k]) for k in SMALL_ORDER] for d in (w, grads, m, v)])
    for d, arrays in zip((delta, new_m, new_v), updates):
        for k, a in zip(SMALL_ORDER, arrays):
            d[k] = a.reshape(w[k].shape)

    outs = [loss, grad_x[None]]
    for d in (grads, delta, new_m, new_v):
        outs += [d[k][None] for k in ALL_ORDER]
    return tuple(outs)
```

```python
import jax
import jax.numpy as jnp
from jax import lax
from jax.experimental import pallas as pl
from jax.experimental.pallas import tpu as pltpu

f32 = jnp.float32
bf16 = jnp.bfloat16
SDS = jax.ShapeDtypeStruct

SEQ = 2048
D_MODEL = 2048
EPS = 1e-6
NEG = -1e30
HEAD_DIM = 128
ROT_HALF = 16
ROPE_THETA = 500000.0
DILATIONS = (1, 4, 16)
SPAN = 128
N_HEADS = 8
HALF = 1024
POOL_CH = 256
CONV_K = 31
CONV_PAD = 32
CHUNK = 128
N_CHIPS = 4
LANES = 256
E_IN_PIECES = 3
SMALL_SHARD_ROWS = 352
SMALL_GRAD_ROWS = 1536
ANY = pl.BlockSpec(memory_space=pl.ANY)
MESH = pl.DeviceIdType.MESH

ADAM_LR = 0.001
ADAM_B1 = 0.9
ADAM_B2 = 0.999
ADAM_EPS = 1e-08
ADAM_WD = 0.01
ADAM_STEP = 10


def _dot(a, b):
    return jnp.dot(a, b, preferred_element_type=f32)


def _dot_nt(a, b):
    return lax.dot_general(a, b, (((1,), (1,)), ((), ())), preferred_element_type=f32)


def _dot_tn(a, b):
    return lax.dot_general(a, b, (((0,), (0,)), ((), ())), preferred_element_type=f32)


def _sigmoid(x):
    return 1.0 / (1.0 + jnp.exp(-x))


def _silu_and_grad(x):
    s = _sigmoid(x)
    return x * s, s * (1.0 + x * (1.0 - s))


def _rms_fwd(x, g):
    r = lax.rsqrt(jnp.mean(x * x, axis=-1, keepdims=True) + EPS)
    return x * r * g


def _rms_bwd(x, g, dout):
    r = lax.rsqrt(jnp.mean(x * x, axis=-1, keepdims=True) + EPS)
    xh = x * r
    dg = jnp.sum(dout * xh, axis=0, keepdims=True)
    dxh = dout * g
    dx = r * (dxh - xh * jnp.mean(dxh * xh, axis=-1, keepdims=True))
    return dx, dg


def _ln_stats(x):
    mu = jnp.mean(x, axis=-1, keepdims=True)
    xc = x - mu
    rstd = lax.rsqrt(jnp.mean(xc * xc, axis=-1, keepdims=True) + EPS)
    return xc * rstd, rstd


def _ln_bwd(xh, rstd, g, dout):
    dg = jnp.sum(dout * xh, axis=0, keepdims=True)
    db = jnp.sum(dout, axis=0, keepdims=True)
    dxh = dout * g
    dx = rstd * (dxh - jnp.mean(dxh, axis=-1, keepdims=True) - xh * jnp.mean(dxh * xh, axis=-1, keepdims=True))
    return dx, dg, db


def _accumulate(ref, value, first):
    @pl.when(first)
    def _():
        ref[...] = value

    @pl.when(jnp.logical_not(first))
    def _():
        ref[...] += value


def _col_tile(ns):
    for t in (1024, 768, 512, 256):
        if ns % t == 0:
            return t
    raise ValueError(ns)


def _mm_nn(a, w, out_dtype, name, piece=0, pieces=1, into=None):
    m, k = a.shape
    j, _, ns = w.shape
    tm, tn = 1024, _col_tile(ns)
    nb = ns // tn

    def body(a_ref, w_ref, *rest):
        rest[-1][...] = _dot(a_ref[...], w_ref[...]).astype(rest[-1].dtype)

    return pl.pallas_call(
        body, name=name, grid=(j * nb, m // tm),
        in_specs=[pl.BlockSpec((tm, k), lambda n, i: (i, 0)),
                  pl.BlockSpec((None, k, tn), lambda n, i: (n // nb, 0, n % nb))] + ([] if into is None else [ANY]),
        out_specs=pl.BlockSpec((tm, tn), lambda n, i: (i, ((n // nb) * pieces + piece) * nb + n % nb)),
        out_shape=SDS((m, j * ns * pieces), out_dtype),
        input_output_aliases={} if into is None else {2: 0},
    )(a, w, *([] if into is None else [into]))


def _mm_nt(dz, ws, name, after):
    m, _ = dz.shape
    pieces = len(ws)
    j, k, ns = ws[0].shape
    tm, tk = 1024, 1024

    def body(dz_ref, *rest):
        w_refs, o_ref = rest[:pieces], rest[-1]
        total = _dot_nt(dz_ref[:, 0:ns], w_refs[0][...])
        for q in range(1, pieces):
            total = total + _dot_nt(dz_ref[:, q * ns:(q + 1) * ns], w_refs[q][...])
        _accumulate(o_ref, total, pl.program_id(2) == 0)

    return pl.pallas_call(
        body, name=name, grid=(m // tm, k // tk, j),
        in_specs=[pl.BlockSpec((tm, pieces * ns), lambda i, kk, r: (i, r))]
        + [pl.BlockSpec((None, tk, ns), lambda i, kk, r: (r, kk, 0))] * pieces + [ANY],
        out_specs=pl.BlockSpec((tm, tk), lambda i, kk, r: (i, kk)),
        out_shape=SDS((m, k), f32),
    )(dz, *ws, after)


def _mm_tn(a, dz, j, name):
    m, k = a.shape
    ns = dz.shape[1] // j
    tk, tn = 1024, _col_tile(ns)
    nb = ns // tn

    def body(a_ref, dz_ref, o_ref):
        o_ref[...] = _dot_tn(a_ref[...], dz_ref[...]).astype(o_ref.dtype)

    return pl.pallas_call(
        body, name=name, grid=(k // tk, j * nb),
        in_specs=[pl.BlockSpec((m, tk), lambda kk, n: (0, kk)),
                  pl.BlockSpec((m, tn), lambda kk, n: (0, n))],
        out_specs=pl.BlockSpec((None, tk, tn), lambda kk, n: (n // nb, kk, n % nb)),
        out_shape=SDS((j, k, ns), bf16),
    )(a, dz)


ROWS = 256


def _row_spec(width=D_MODEL, col=0):
    return pl.BlockSpec((ROWS, width), lambda i: (i, col))


def _vec_spec(width=D_MODEL):
    return pl.BlockSpec((1, width), lambda i: (0, 0))


def _pre_norm(x, g):
    def body(x_ref, g_ref, h_ref):
        h_ref[...] = _rms_fwd(x_ref[...], g_ref[...]).astype(bf16)

    return pl.pallas_call(
        body, name="pre_norm", grid=(SEQ // ROWS,), in_specs=[_row_spec(), _vec_spec()],
        out_specs=_row_spec(), out_shape=SDS((SEQ, D_MODEL), bf16))(x, g)


def _mid_norm(x, y, g_post, g_pre):
    def body(x_ref, y_ref, gpost_ref, gpre_ref, x1_ref, h1_ref):
        x1 = x_ref[...] + _rms_fwd(y_ref[...], gpost_ref[...])
        x1_ref[...] = x1
        h1_ref[...] = _rms_fwd(x1, gpre_ref[...]).astype(bf16)

    return pl.pallas_call(
        body, name="mid_norm", grid=(SEQ // ROWS,),
        in_specs=[_row_spec(), _row_spec(), _vec_spec(), _vec_spec()],
        out_specs=[_row_spec(), _row_spec()],
        out_shape=[SDS((SEQ, D_MODEL), f32), SDS((SEQ, D_MODEL), bf16)])(x, y, g_post, g_pre)


def _final_norm_loss(x1, y, g_post, target):
    def body(x1_ref, y_ref, g_ref, t_ref, loss_ref, dx2_ref, dy_ref, dg_ref):
        first = pl.program_id(0) == 0
        y = y_ref[...]
        g = g_ref[...]
        err = x1_ref[...] + _rms_fwd(y, g) - t_ref[...]
        sq = jnp.sum(jnp.sum(err * err, axis=1, keepdims=True), axis=0, keepdims=True)
        _accumulate(loss_ref, sq * (0.5 / D_MODEL), first)
        dx2 = err * (1.0 / D_MODEL)
        dx2_ref[...] = dx2
        dy, dg = _rms_bwd(y, g, dx2)
        dy_ref[...] = dy.astype(bf16)
        _accumulate(dg_ref, dg, first)

    return pl.pallas_call(
        body, name="final_norm_loss", grid=(SEQ // ROWS,),
        in_specs=[_row_spec(), _row_spec(), _vec_spec(), _row_spec()],
        out_specs=[pl.BlockSpec((1, 1), lambda i: (0, 0)), _row_spec(), _row_spec(), _vec_spec()],
        out_shape=[SDS((1, 1), f32), SDS((SEQ, D_MODEL), f32), SDS((SEQ, D_MODEL), bf16), SDS((1, D_MODEL), f32)],
    )(x1, y, g_post, target)


def _mid_norm_bwd(dx2, dh1, x1, y0, g_pre, g_post):
    def body(dx2_ref, dh1_ref, x1_ref, y0_ref, gpre_ref, gpost_ref, dx1_ref, dy0_ref, dgpre_ref, dgpost_ref):
        first = pl.program_id(0) == 0
        d_in, dgpre = _rms_bwd(x1_ref[...], gpre_ref[...], dh1_ref[...])
        dx1 = dx2_ref[...] + d_in
        dx1_ref[...] = dx1
        dy0, dgpost = _rms_bwd(y0_ref[...], gpost_ref[...], dx1)
        dy0_ref[...] = dy0.astype(bf16)
        _accumulate(dgpre_ref, dgpre, first)
        _accumulate(dgpost_ref, dgpost, first)

    return pl.pallas_call(
        body, name="mid_norm_bwd", grid=(SEQ // ROWS,),
        in_specs=[_row_spec(), _row_spec(), _row_spec(), _row_spec(), _vec_spec(), _vec_spec()],
        out_specs=[_row_spec(), _row_spec(), _vec_spec(), _vec_spec()],
        out_shape=[SDS((SEQ, D_MODEL), f32), SDS((SEQ, D_MODEL), bf16), SDS((1, D_MODEL), f32), SDS((1, D_MODEL), f32)],
    )(dx2, dh1, x1, y0, g_pre, g_post)


def _pre_norm_bwd(dx1, dh0, x, g):
    def body(dx1_ref, dh0_ref, x_ref, g_ref, dx_ref, dg_ref):
        d_in, dg = _rms_bwd(x_ref[...], g_ref[...], dh0_ref[...])
        dx_ref[...] = dx1_ref[...] + d_in
        _accumulate(dg_ref, dg, pl.program_id(0) == 0)

    return pl.pallas_call(
        body, name="pre_norm_bwd", grid=(SEQ // ROWS,),
        in_specs=[_row_spec(), _row_spec(), _row_spec(), _vec_spec()],
        out_specs=[_row_spec(), _vec_spec()],
        out_shape=[SDS((SEQ, D_MODEL), f32), SDS((1, D_MODEL), f32)])(dx1, dh0, x, g)


def _pool_count(g):
    row = lax.broadcasted_iota(jnp.int32, (SEQ, 1), 0)
    width = jnp.left_shift(2, g)
    return row, width, jnp.minimum(row + 1, width).astype(f32)


def _trailing_sum(x, row, width):
    s = x
    for k in (1, 2, 4, 8):
        shifted = jnp.where(row >= k, pltpu.roll(s, k, 0), 0.0)
        s = jnp.where(width > k, s + shifted, s)
    return s


def _leading_sum(x, row, width):
    s = x
    for k in (1, 2, 4, 8):
        shifted = jnp.where(row < SEQ - k, pltpu.roll(s, SEQ - k, 0), 0.0)
        s = jnp.where(width > k, s + shifted, s)
    return s


def _pool_specs():
    a_in = pl.BlockSpec((SEQ, POOL_CH), lambda g: (0, g))
    a_gate = pl.BlockSpec((SEQ, POOL_CH), lambda g: (0, 4 + g))
    w = pl.BlockSpec((None, POOL_CH, POOL_CH), lambda g: (g, 0, 0))
    scale = pl.BlockSpec((1, POOL_CH), lambda g: (0, g))
    return a_in, a_gate, w, scale


def _pool_fwd(z0, pool_w, pool_scale):
    def body(a_ref, gate_ref, w_ref, scale_ref, ya_ref):
        row, width, count = _pool_count(pl.program_id(0))
        a = a_ref[...]
        pooled = _trailing_sum(a, row, width) / count - a
        mixed = _dot(pooled.astype(bf16), w_ref[...]) * scale_ref[...]
        gate = gate_ref[...]
        ya_ref[...] = (mixed * gate * _sigmoid(gate)).astype(bf16)

    return pl.pallas_call(
        body, name="pool_fwd", grid=(4,), in_specs=list(_pool_specs()),
        out_specs=pl.BlockSpec((SEQ, POOL_CH), lambda g: (0, g)),
        out_shape=SDS((SEQ, HALF), bf16))(z0, z0, pool_w, pool_scale)


def _pool_bwd(z0, dcat, pool_w, pool_scale):
    def body(a_ref, gate_ref, w_ref, scale_ref, dya_ref, da_ref, dgate_ref, dw_ref, dscale_ref):
        row, width, count = _pool_count(pl.program_id(0))
        a = a_ref[...]
        pooled = (_trailing_sum(a, row, width) / count - a).astype(bf16)
        w = w_ref[...]
        scale = scale_ref[...]
        mixed = _dot(pooled, w)
        silu, dsilu = _silu_and_grad(gate_ref[...])
        dya = dya_ref[...]
        dgate_ref[...] = (dya * mixed * scale * dsilu).astype(bf16)
        dms = dya * silu
        dscale_ref[...] = jnp.sum(dms * mixed, axis=0, keepdims=True)
        dmixed = (dms * scale).astype(bf16)
        dw_ref[...] = _dot_tn(pooled, dmixed)
        dpooled = _dot_nt(dmixed, w)
        da_ref[...] = (_leading_sum(dpooled / count, row, width) - dpooled).astype(bf16)

    a_in, a_gate, w, scale = _pool_specs()
    col = pl.BlockSpec((SEQ, POOL_CH), lambda g: (0, g))
    return pl.pallas_call(
        body, name="pool_bwd", grid=(4,), in_specs=[a_in, a_gate, w, scale, col],
        out_specs=[col, col, w, scale],
        out_shape=[SDS((SEQ, HALF), bf16), SDS((SEQ, HALF), bf16), SDS((4, POOL_CH, POOL_CH), f32), SDS((1, HALF), f32)],
    )(z0, z0, pool_w, pool_scale, dcat)


Q_COL, K_COL, V_COL, BGATE_COL = 16, 40, 64, 88


def _rope_tables():
    pos = jnp.arange(SEQ, dtype=f32)
    inv_freq = jnp.power(ROPE_THETA, -jnp.arange(0, 2 * ROT_HALF, 2, dtype=f32) / (2 * ROT_HALF))
    ang = pos[:, None] * inv_freq[None, :]
    cos, sin = jnp.cos(ang), jnp.sin(ang)
    zeros = jnp.zeros((SEQ, HEAD_DIM - 2 * ROT_HALF), f32)
    cos_t = jnp.concatenate([cos, cos, zeros + 1.0], axis=1)
    sin_t = jnp.concatenate([sin, sin, zeros], axis=1)
    j = jnp.arange(HEAD_DIM)[:, None]
    i = jnp.arange(HEAD_DIM)[None, :]
    rot = jnp.where((i < ROT_HALF) & (j == i + ROT_HALF), -1.0, 0.0) + jnp.where(
        (i >= ROT_HALF) & (i < 2 * ROT_HALF) & (j == i - ROT_HALF), 1.0, 0.0)
    return cos_t, sin_t, rot.astype(bf16), rot.T.astype(bf16)


def _exact_dot(t, m):
    hi = t.astype(bf16)
    lo = (t - hi.astype(f32)).astype(bf16)
    return _dot(hi, m) + _dot(lo, m)


def _rope(t, cos_t, sin_t, rot):
    return t * cos_t + _exact_dot(t, rot) * sin_t


def _rope_transposed(d, cos_t, sin_t, rot_t):
    return d * cos_t + _exact_dot(d * sin_t, rot_t)


ROW_CHUNK = 256


def _chunks(fn):
    def step(i, carry):
        fn(pl.multiple_of(i * ROW_CHUNK, ROW_CHUNK))
        return carry

    lax.fori_loop(0, SEQ // ROW_CHUNK, step, 0, unroll=2)


def _pieces(dilation):
    length = SEQ // dilation
    n = min(length, ROW_CHUNK)
    return [(r, l0, n) for r in range(dilation) for l0 in range(0, length, n)]


def _by_residue(dst_ref, src_ref, dilation, dtype):
    length = SEQ // dilation
    for r, l0, n in _pieces(dilation):
        src = src_ref[l0:l0 + n, :] if dilation == 1 else src_ref[pl.ds(r + dilation * l0, n, stride=dilation), :]
        start = r * length + l0
        dst_ref[start:start + n, :] = src.astype(dtype)


def _by_position(dst_ref, src_ref, dilation):
    length = SEQ // dilation
    for r, l0, n in _pieces(dilation):
        src = src_ref[r * length + l0:r * length + l0 + n, :]
        if dilation == 1:
            dst_ref[l0:l0 + n, :] = src
        else:
            dst_ref[pl.ds(r + dilation * l0, n, stride=dilation), :] = src


def _attn_masks():
    qi = lax.broadcasted_iota(jnp.int32, (SPAN, 2 * SPAN), 0)
    kj = lax.broadcasted_iota(jnp.int32, (SPAN, 2 * SPAN), 1)
    window = ((kj < SPAN) & (kj >= qi)) | ((kj >= SPAN) & (kj - SPAN <= qi))
    own = lax.broadcasted_iota(jnp.int32, (SPAN, SPAN), 1) <= lax.broadcasted_iota(jnp.int32, (SPAN, SPAN), 0)
    return window, own


def _attn_blocks(dilation):
    per_residue = SEQ // dilation // SPAN
    blocks = [(c, c % per_residue != 0) for c in range(SEQ // SPAN)]
    return [blocks[i:i + 4] for i in range(0, len(blocks), 4)]


def _block_keys(c, has_prev):
    return slice((c - 1) * SPAN if has_prev else c * SPAN, (c + 1) * SPAN)


def _head_spec(col):
    return pl.BlockSpec((SEQ, HEAD_DIM), lambda h: (0, col + h))


def _table_spec():
    return pl.BlockSpec((SEQ, HEAD_DIM), lambda h: (0, 0))


def _attn_fwd(z0, tables):
    scale = HEAD_DIM ** -0.5

    def body(*refs):
        qkv = refs[0:9]
        bg_ref, cos_ref, sin_ref, rot_ref = refs[9:13]
        yb_ref, att_ref, lse_ref = refs[13:16]
        saved = refs[16:25]
        tmp_q, tmp_k, v_ones, o_res, l_res, o_nat, l_nat = refs[25:32]
        window_mask, own_mask = _attn_masks()
        rot = rot_ref[...]

        @pl.when(pl.program_id(0) == 0)
        def _():
            v_ones[:, HEAD_DIM:] = jnp.ones((SEQ, HEAD_DIM), bf16)

        for g, dilation in enumerate(DILATIONS):
            q_ref, k_ref, v_ref = qkv[3 * g:3 * g + 3]
            qd, kd, vd = saved[3 * g:3 * g + 3]

            def rope_rows(start, q_ref=q_ref, k_ref=k_ref):
                r = pl.ds(start, ROW_CHUNK)
                cos_t, sin_t = cos_ref[r, :], sin_ref[r, :]
                tmp_q[r, :] = _rope(q_ref[r, :], cos_t, sin_t, rot) * scale
                tmp_k[r, :] = _rope(k_ref[r, :], cos_t, sin_t, rot)

            _chunks(rope_rows)
            _by_residue(qd, tmp_q, dilation, bf16)
            _by_residue(kd, tmp_k, dilation, bf16)
            _by_residue(vd, v_ref, dilation, bf16)
            for l0 in range(0, SEQ, ROW_CHUNK):
                v_ones[l0:l0 + ROW_CHUNK, 0:HEAD_DIM] = vd[l0:l0 + ROW_CHUNK, :]

            for four in _attn_blocks(dilation):
                scores = [_dot_nt(qd[c * SPAN:(c + 1) * SPAN, :], kd[_block_keys(c, prev), :]) for c, prev in four]
                tops, probs = [], []
                for (c, prev), s in zip(four, scores):
                    s = jnp.where(window_mask if prev else own_mask, s, NEG)
                    tops.append(jnp.max(s, axis=1, keepdims=True))
                    probs.append(jnp.exp(s - tops[-1]).astype(bf16))
                sums = [_dot(p, v_ones[_block_keys(c, prev), :]) for (c, prev), p in zip(four, probs)]
                for (c, prev), m, o in zip(four, tops, sums):
                    den = o[:, HEAD_DIM:]
                    o_res[c * SPAN:(c + 1) * SPAN, :] = o[:, :HEAD_DIM] / den
                    l_res[c * SPAN:(c + 1) * SPAN, :] = m + jnp.log(den)

            if dilation > 1:
                _by_position(o_nat, o_res, dilation)
                _by_position(l_nat, l_res, dilation)
            o_g, l_g = (o_res, l_res) if dilation == 1 else (o_nat, l_nat)

            def merge(start, g=g, o_g=o_g, l_g=l_g):
                r = pl.ds(start, ROW_CHUNK)
                if g == 0:
                    att, total = o_g[r, :], l_g[r, :]
                else:
                    l_old, l_new = lse_ref[r, :], l_g[r, :]
                    top = jnp.maximum(l_old, l_new)
                    total = top + jnp.log(jnp.exp(l_old - top) + jnp.exp(l_new - top))
                    att = att_ref[r, :] * jnp.exp(l_old - total) + o_g[r, :] * jnp.exp(l_new - total)
                att_ref[r, :] = att
                lse_ref[r, :] = total
                if g == len(DILATIONS) - 1:
                    gate = bg_ref[r, :]
                    yb_ref[r, :] = (att * gate * _sigmoid(gate)).astype(bf16)

            _chunks(merge)

    in_specs = []
    for g in range(3):
        in_specs += [_head_spec(Q_COL + 8 * g), _head_spec(K_COL + 8 * g), _head_spec(V_COL + 8 * g)]
    in_specs += [_head_spec(BGATE_COL), _table_spec(), _table_spec(), pl.BlockSpec((HEAD_DIM, HEAD_DIM), lambda h: (0, 0))]
    out_spec = pl.BlockSpec((SEQ, HEAD_DIM), lambda h: (0, h))
    vm = lambda dt: pltpu.VMEM((SEQ, HEAD_DIM), dt)
    cos_t, sin_t, rot, _ = tables
    out = pl.pallas_call(
        body, name="attn_fwd", grid=(N_HEADS,), in_specs=in_specs, out_specs=[out_spec] * 12,
        out_shape=[SDS((SEQ, HALF), bf16), SDS((SEQ, HALF), f32), SDS((SEQ, HALF), f32)] + [SDS((SEQ, HALF), bf16)] * 9,
        scratch_shapes=[vm(f32), vm(f32), pltpu.VMEM((SEQ, 2 * HEAD_DIM), bf16), vm(f32), vm(f32), vm(f32), vm(f32)],
    )(*([z0] * 10), cos_t, sin_t, rot)
    return out[0], out[1], out[2], [tuple(out[3 + 3 * g:6 + 3 * g]) for g in range(3)]


def _attn_bwd_group(g, saved, z0, att, lse, dcat, tables):
    scale = HEAD_DIM ** -0.5
    dilation = DILATIONS[g]
    with_gate = g == 0

    def body(*refs):
        qd, kd, vd, bg_ref, att_ref, lse_ref, dyb_ref, cos_ref, sin_ref, rot_t_ref = refs[0:10]
        n_out = 4 if with_gate else 3
        dq_ref, dk_ref, dv_ref = refs[10:13]
        dod, ld, dd, tmp, aq, ak, av = refs[10 + n_out:17 + n_out]
        window_mask, own_mask = _attn_masks()
        rot_t = rot_t_ref[...]

        def gate_rows(start):
            r = pl.ds(start, ROW_CHUNK)
            silu, dsilu = _silu_and_grad(bg_ref[r, :])
            att_v = att_ref[r, :]
            dyb = dyb_ref[r, :]
            if with_gate:
                refs[13][r, :] = (dyb * att_v * dsilu).astype(bf16)
            datt = dyb * silu
            tmp[r, :] = datt
            aq[r, :] = jnp.broadcast_to(jnp.sum(datt * att_v, axis=1, keepdims=True), (ROW_CHUNK, HEAD_DIM))

        _chunks(gate_rows)
        _by_residue(dod, tmp, dilation, bf16)
        _by_residue(dd, aq, dilation, f32)
        _by_residue(ld, lse_ref, dilation, f32)

        for four in _attn_blocks(dilation):
            rows = [slice(c * SPAN, (c + 1) * SPAN) for c, _ in four]
            keys = [_block_keys(c, prev) for c, prev in four]
            scores = [_dot_nt(qd[r, :], kd[k, :]) for r, k in zip(rows, keys)]
            dprobs = [_dot_nt(dod[r, :], vd[k, :]) for r, k in zip(rows, keys)]
            probs, dscores = [], []
            for (c, prev), r, s, dp in zip(four, rows, scores, dprobs):
                lse_q, delta = ld[r, :], dd[r, :]
                if prev:
                    lse_q = jnp.concatenate([lse_q, lse_q], axis=1)
                    delta = jnp.concatenate([delta, delta], axis=1)
                p = jnp.where(window_mask if prev else own_mask, jnp.exp(s - lse_q), 0.0)
                probs.append(p.astype(bf16))
                dscores.append((p * (dp - delta)).astype(bf16))
            dvs = [_dot_tn(p, dod[r, :]) for p, r in zip(probs, rows)]
            dks = [_dot_tn(ds, qd[r, :]) for ds, r in zip(dscores, rows)]
            dqs = [_dot(ds, kd[k, :]) for ds, k in zip(dscores, keys)]
            for (c, prev), r, dv, dk, dq in zip(four, rows, dvs, dks, dqs):
                aq[r, :] = dq
                if prev:
                    before = slice((c - 1) * SPAN, c * SPAN)
                    av[before, :] += dv[0:SPAN]
                    ak[before, :] += dk[0:SPAN]
                    av[r, :] = dv[SPAN:]
                    ak[r, :] = dk[SPAN:]
                else:
                    av[r, :] = dv
                    ak[r, :] = dk

        def finish(out_ref, acc, factor, roped):
            if dilation > 1:
                _by_position(tmp, acc, dilation)
            src = acc if dilation == 1 else tmp

            def rows(start):
                r = pl.ds(start, ROW_CHUNK)
                d = src[r, :]
                if factor != 1.0:
                    d = d * factor
                if roped:
                    d = _rope_transposed(d, cos_ref[r, :], sin_ref[r, :], rot_t)
                out_ref[r, :] = d.astype(bf16)

            _chunks(rows)

        finish(dq_ref, aq, scale, True)
        finish(dk_ref, ak, 1.0, True)
        finish(dv_ref, av, 1.0, False)

    head = pl.BlockSpec((SEQ, HEAD_DIM), lambda h: (0, h))
    in_specs = [head, head, head, _head_spec(BGATE_COL), head, head, _head_spec(8), _table_spec(), _table_spec(),
                pl.BlockSpec((HEAD_DIM, HEAD_DIM), lambda h: (0, 0))]
    n_out = 4 if with_gate else 3
    vm = lambda dt: pltpu.VMEM((SEQ, HEAD_DIM), dt)
    cos_t, sin_t, _, rot_t = tables
    return pl.pallas_call(
        body, name=f"attn_bwd_g{g}", grid=(N_HEADS,), in_specs=in_specs, out_specs=[head] * n_out,
        out_shape=[SDS((SEQ, HALF), bf16)] * n_out,
        scratch_shapes=[vm(bf16), vm(f32), vm(f32), vm(f32), vm(f32), vm(f32), vm(f32)],
    )(*saved, z0, att, lse, dcat, cos_t, sin_t, rot_t)


def _sgu_specs():
    chunk = lambda col: pl.BlockSpec((CHUNK, HALF), lambda n: (n, col))
    vec = pl.BlockSpec((1, HALF), lambda n: (0, 0))
    w = pl.BlockSpec((4, CHUNK, CHUNK), lambda n: (0, 0, 0))
    bias = pl.BlockSpec((CHUNK, CHUNK), lambda n: (0, 0))
    return chunk, vec, w, bias


def _sgu_weights(w_ref):
    tril = lax.broadcasted_iota(jnp.int32, (CHUNK, CHUNK), 1) <= lax.broadcasted_iota(jnp.int32, (CHUNK, CHUNK), 0)
    return tril, [jnp.where(tril, w_ref[h], 0.0).astype(bf16) for h in range(4)]


def _sgu_fwd(z1, ln_g, ln_b, sgu_w, bias_t):
    def body(u_ref, v_ref, cg_ref, g_ref, b_ref, w_ref, bias_ref, yc_ref):
        _, ws = _sgu_weights(w_ref)
        xh, _ = _ln_stats(v_ref[...])
        vn = (xh * g_ref[...] + b_ref[...]).astype(bf16)
        for h in range(4):
            cols = slice(h * POOL_CH, (h + 1) * POOL_CH)
            s = _dot(ws[h], vn[:, cols]) + bias_ref[:, h:h + 1]
            gate = cg_ref[:, cols]
            yc_ref[:, cols] = (u_ref[:, cols] * s * gate * _sigmoid(gate)).astype(bf16)

    chunk, vec, w, bias = _sgu_specs()
    return pl.pallas_call(
        body, name="sgu_fwd", grid=(SEQ // CHUNK,),
        in_specs=[chunk(0), chunk(1), chunk(2), vec, vec, w, bias], out_specs=chunk(0),
        out_shape=SDS((SEQ, HALF), bf16))(z1, z1, z1, ln_g, ln_b, sgu_w, bias_t)


def _sgu_bwd(z1, dcat, ln_g, ln_b, sgu_w, bias_t):
    def body(u_ref, v_ref, cg_ref, dyc_ref, g_ref, b_ref, w_ref, bias_ref,
             du_ref, dv_ref, dcg_ref, dw_ref, dbias_ref, dg_ref, db_ref, dvn_ref):
        first = pl.program_id(0) == 0
        tril, ws = _sgu_weights(w_ref)
        xh, rstd = _ln_stats(v_ref[...])
        g = g_ref[...]
        vn = (xh * g + b_ref[...]).astype(bf16)

        @pl.when(first)
        def _():
            dbias_ref[...] = jnp.zeros((CHUNK, CHUNK), f32)

        for h in range(4):
            cols = slice(h * POOL_CH, (h + 1) * POOL_CH)
            vn_h = vn[:, cols]
            s = _dot(ws[h], vn_h) + bias_ref[:, h:h + 1]
            silu, dsilu = _silu_and_grad(cg_ref[:, cols])
            dyc = dyc_ref[:, cols]
            u = u_ref[:, cols]
            du_ref[:, cols] = (dyc * s * silu).astype(bf16)
            dcg_ref[:, cols] = (dyc * u * s * dsilu).astype(bf16)
            ds = dyc * u * silu
            dbias_ref[:, h:h + 1] += jnp.sum(ds, axis=1, keepdims=True)
            ds = ds.astype(bf16)
            _accumulate(dw_ref.at[h], jnp.where(tril, _dot_nt(ds, vn_h), 0.0), first)
            dvn_ref[:, cols] = _dot_tn(ws[h], ds)
        dv, dg, db = _ln_bwd(xh, rstd, g, dvn_ref[...])
        dv_ref[...] = dv.astype(bf16)
        _accumulate(dg_ref, dg, first)
        _accumulate(db_ref, db, first)

    chunk, vec, w, bias = _sgu_specs()
    return pl.pallas_call(
        body, name="sgu_bwd", grid=(SEQ // CHUNK,),
        in_specs=[chunk(0), chunk(1), chunk(2), chunk(0), vec, vec, w, bias],
        out_specs=[chunk(0), chunk(0), chunk(0), w, bias, vec, vec],
        out_shape=[SDS((SEQ, HALF), bf16)] * 3 + [SDS((4, CHUNK, CHUNK), f32), SDS((CHUNK, CHUNK), f32),
                                                   SDS((1, HALF), f32), SDS((1, HALF), f32)],
        scratch_shapes=[pltpu.VMEM((CHUNK, HALF), f32)],
    )(z1, z1, z1, dcat, ln_g, ln_b, sgu_w, bias_t)


CONV_TILE = 128
DVAL_COL, DGLU_COL = 12, 16


def _conv_specs():
    val = pl.BlockSpec((SEQ, POOL_CH), lambda j: (0, DVAL_COL + j))
    glu = pl.BlockSpec((SEQ, POOL_CH), lambda j: (0, DGLU_COL + j))
    w = pl.BlockSpec((CONV_K, POOL_CH), lambda j: (0, j))
    col = pl.BlockSpec((SEQ, POOL_CH), lambda j: (0, j))
    vec = pl.BlockSpec((1, POOL_CH), lambda j: (0, j))
    return val, glu, w, col, vec


def _conv_fwd(z1, conv_w, conv_b):
    def body(val_ref, glu_ref, w_ref, b_ref, out_ref, xpad):
        xpad[0:CONV_PAD, :] = jnp.zeros((CONV_PAD, POOL_CH), f32)
        xpad[CONV_PAD:, :] = val_ref[...] * _sigmoid(glu_ref[...])
        w = w_ref[...]
        bias = b_ref[...]

        def tile(i, carry):
            t0 = pl.multiple_of(i * CONV_TILE, CONV_TILE)
            window = xpad[pl.ds(t0, CONV_TILE + CONV_PAD), :]
            acc = jnp.broadcast_to(bias, (CONV_TILE, POOL_CH))
            for k in range(CONV_K):
                shift = CONV_PAD - (CONV_K - 1) + k
                acc = acc + w[k:k + 1, :] * pltpu.roll(window, CONV_TILE + CONV_PAD - shift, 0)[0:CONV_TILE]
            out_ref[pl.ds(t0, CONV_TILE), :] = acc
            return carry

        lax.fori_loop(0, SEQ // CONV_TILE, tile, 0)

    val, glu, w, col, vec = _conv_specs()
    return pl.pallas_call(
        body, name="conv_fwd", grid=(4,), in_specs=[val, glu, w, vec], out_specs=col,
        out_shape=SDS((SEQ, HALF), f32), scratch_shapes=[pltpu.VMEM((SEQ + CONV_PAD, POOL_CH), f32)],
    )(z1, z1, conv_w, conv_b)


def _conv_bwd(z1, dconv, conv_w):
    def body(val_ref, glu_ref, w_ref, dout_ref, dval_ref, dglu_ref, dw_ref, db_ref, xpad, dpad, dx_ref):
        val = val_ref[...]
        sig = _sigmoid(glu_ref[...])
        xpad[0:CONV_PAD, :] = jnp.zeros((CONV_PAD, POOL_CH), f32)
        xpad[CONV_PAD:, :] = val * sig
        dout = dout_ref[...]
        dpad[0:SEQ, :] = dout
        dpad[SEQ:, :] = jnp.zeros((CONV_PAD, POOL_CH), f32)
        db_ref[...] = jnp.sum(dout, axis=0, keepdims=True)
        dw_ref[...] = jnp.zeros((CONV_K, POOL_CH), f32)
        w = w_ref[...]

        def tile(i, carry):
            t0 = pl.multiple_of(i * CONV_TILE, CONV_TILE)
            x_win = xpad[pl.ds(t0, CONV_TILE + CONV_PAD), :]
            d_win = dpad[pl.ds(t0, CONV_TILE + CONV_PAD), :]
            d_own = d_win[0:CONV_TILE]
            acc = jnp.zeros((CONV_TILE, POOL_CH), f32)
            for k in range(CONV_K):
                shift = CONV_PAD - (CONV_K - 1) + k
                x_k = pltpu.roll(x_win, CONV_TILE + CONV_PAD - shift, 0)[0:CONV_TILE]
                dw_ref[k:k + 1, :] += jnp.sum(d_own * x_k, axis=0, keepdims=True)
                back = CONV_K - 1 - k
                d_k = d_own if back == 0 else pltpu.roll(d_win, CONV_TILE + CONV_PAD - back, 0)[0:CONV_TILE]
                acc = acc + w[k:k + 1, :] * d_k
            dx_ref[pl.ds(t0, CONV_TILE), :] = acc
            return carry

        lax.fori_loop(0, SEQ // CONV_TILE, tile, 0)
        dx = dx_ref[...]
        dval_ref[...] = (dx * sig).astype(bf16)
        dglu_ref[...] = (dx * val * sig * (1.0 - sig)).astype(bf16)

    val, glu, w, col, vec = _conv_specs()
    pad = pltpu.VMEM((SEQ + CONV_PAD, POOL_CH), f32)
    return pl.pallas_call(
        body, name="conv_bwd", grid=(4,), in_specs=[val, glu, w, col], out_specs=[col, col, w, vec],
        out_shape=[SDS((SEQ, HALF), bf16), SDS((SEQ, HALF), bf16), SDS((CONV_K, HALF), f32), SDS((1, HALF), f32)],
        scratch_shapes=[pad, pad, pltpu.VMEM((SEQ, POOL_CH), f32)],
    )(z1, z1, conv_w, dconv)


DGATE_COL = 5


def _conv_norm_fwd(conv, z1, g, b):
    def body(c_ref, gate_ref, g_ref, b_ref, yd_ref):
        xh, _ = _ln_stats(c_ref[...])
        n = xh * g_ref[...] + b_ref[...]
        gate = gate_ref[...]
        yd_ref[...] = (n * _sigmoid(n) * gate * _sigmoid(gate)).astype(bf16)

    return pl.pallas_call(
        body, name="conv_norm_fwd", grid=(SEQ // ROWS,),
        in_specs=[_row_spec(HALF), _row_spec(HALF, DGATE_COL), _vec_spec(HALF), _vec_spec(HALF)],
        out_specs=_row_spec(HALF), out_shape=SDS((SEQ, HALF), bf16))(conv, z1, g, b)


def _conv_norm_bwd(conv, z1, dcat, g, b):
    def body(c_ref, gate_ref, dyd_ref, g_ref, b_ref, dconv_ref, dgate_ref, dg_ref, db_ref):
        first = pl.program_id(0) == 0
        xh, rstd = _ln_stats(c_ref[...])
        g = g_ref[...]
        n_silu, n_dsilu = _silu_and_grad(xh * g + b_ref[...])
        gate_silu, gate_dsilu = _silu_and_grad(gate_ref[...])
        dyd = dyd_ref[...]
        dgate_ref[...] = (dyd * n_silu * gate_dsilu).astype(bf16)
        dconv, dg, db = _ln_bwd(xh, rstd, g, dyd * gate_silu * n_dsilu)
        dconv_ref[...] = dconv
        _accumulate(dg_ref, dg, first)
        _accumulate(db_ref, db, first)

    return pl.pallas_call(
        body, name="conv_norm_bwd", grid=(SEQ // ROWS,),
        in_specs=[_row_spec(HALF), _row_spec(HALF, DGATE_COL), _row_spec(HALF, 1), _vec_spec(HALF), _vec_spec(HALF)],
        out_specs=[_row_spec(HALF), _row_spec(HALF), _vec_spec(HALF), _vec_spec(HALF)],
        out_shape=[SDS((SEQ, HALF), f32), SDS((SEQ, HALF), bf16), SDS((1, HALF), f32), SDS((1, HALF), f32)],
    )(conv, z1, dcat, g, b)


def _step(x, target, w, chip):
    chip_vec = chip.astype(jnp.int32).reshape(1)
    sharded_names = list(SHARDED_SMALL)
    small_shard = _pack([w[k] for k in sharded_names], total_rows=SMALL_SHARD_ROWS)
    small_slot = lax.dynamic_update_slice(jnp.zeros((N_CHIPS, SMALL_SHARD_ROWS, LANES), f32), small_shard[None], (chip, 0, 0))
    slots = [small_slot] + [_cast_into_slot(w["e_w_in"], chip_vec, f"cast_e_w_in{i}", i, E_IN_PIECES) for i in range(E_IN_PIECES)]
    slots += [_cast_into_slot(w[k], chip_vec, f"cast_{k}") for k in BIG[1:]]
    sems, bufs, token = _gather_start(slots)
    tables = _rope_tables()

    def vec(k):
        return w[k].reshape(1, -1)

    h0 = _pre_norm(x, vec("e_pre_norm") + token[0, 0])
    after, z0, e_w_in = h0, None, []
    for i in range(E_IN_PIECES):
        group = slice(0, 2) if i == 0 else slice(1 + i, 2 + i)
        landed = _forward_halves(_gather_wait(bufs[group], sems[group], after, f"gather_wait_{i}"), f"forward_{i}")
        if i == 0:
            small_full = landed[0]
        e_w_in.append(landed[-1])
        z0 = _mm_nn(h0, landed[-1], f32, f"e_in{i}", i, E_IN_PIECES, z0)
        after = z0
    p = {k: _from_chips(k, a) for k, a in zip(sharded_names, _unpack(small_full, [SHARDED_SMALL[k][0] for k in sharded_names]))}
    for k in ("o_pre_norm", "o_sgu_norm_g", "o_sgu_norm_b", "o_conv_b", "o_conv_norm_g", "o_conv_norm_b", "o_post_norm"):
        p[k] = p[k].reshape(1, -1)
    pool_w_bf = p["e_pool_w"].astype(bf16)
    bias_t = jnp.pad(w["o_sgu_b"].T, ((0, 0), (0, CHUNK - 4)))

    ya = _pool_fwd(z0, pool_w_bf, vec("e_pool_scale"))
    yb, att, lse, qkv_by_residue = _attn_fwd(z0, tables)

    def arrived(index, after, name):
        one = slice(index, index + 1)
        return _forward_halves(_gather_wait(bufs[one], sems[one], after, f"gather_wait_{name}"), f"forward_{name}")[0]

    e_w_out = arrived(1 + E_IN_PIECES, att, "e_w_out").reshape(1, D_MODEL, D_MODEL)
    cat0 = jnp.concatenate([ya, yb], axis=1)
    y0 = _mm_nn(cat0, e_w_out, f32, "e_out")
    x1, h1 = _mid_norm(x, y0, vec("e_post_norm"), p["o_pre_norm"])
    o_w_in = arrived(2 + E_IN_PIECES, h1, "o_w_in")
    z1 = _mm_nn(h1, o_w_in, f32, "o_in")
    yc = _sgu_fwd(z1, p["o_sgu_norm_g"], p["o_sgu_norm_b"], w["o_sgu_w"], bias_t)
    conv = _conv_fwd(z1, p["o_conv_w"], p["o_conv_b"])
    yd = _conv_norm_fwd(conv, z1, p["o_conv_norm_g"], p["o_conv_norm_b"])
    o_w_out = arrived(3 + E_IN_PIECES, yd, "o_w_out").reshape(1, D_MODEL, D_MODEL)
    cat1 = jnp.concatenate([yc, yd], axis=1)
    y1 = _mm_nn(cat1, o_w_out, f32, "o_out")
    loss, dx2, dy1, g_o_post = _final_norm_loss(x1, y1, p["o_post_norm"], target)

    in_flight = {}

    def send_off(name, grad):
        sem, sums, land, tok = _scatter_start(_swap_add(grad, f"swap_add_{name}"), f"scatter_start_{name}")
        in_flight[name] = (sem, sums, land)
        return tok

    tok = send_off("o_w_out", _mm_tn(cat1, dy1, 1, "o_out_dw").reshape(N_CHIPS, HALF // 2, D_MODEL))
    dcat1 = _mm_nt(dy1, [o_w_out], "o_out_dx", tok)
    du, dv, dcg, g_sgu_w, g_bias_t, g_sgu_g, g_sgu_b = _sgu_bwd(
        z1, dcat1, p["o_sgu_norm_g"] + tok[0, 0], p["o_sgu_norm_b"], w["o_sgu_w"], bias_t)
    dconv, ddgate, g_cn_g, g_cn_b = _conv_norm_bwd(conv, z1, dcat1, p["o_conv_norm_g"], p["o_conv_norm_b"])
    ddval, ddglu, g_conv_w, g_conv_b = _conv_bwd(z1, dconv, p["o_conv_w"])
    dz1 = jnp.concatenate([du, dv, dcg, ddval, ddglu, ddgate], axis=1)
    tok = send_off("o_w_in", _mm_tn(h1, dz1, N_CHIPS, "o_in_dw"))
    dh1 = _mm_nt(dz1, [o_w_in], "o_in_dx", tok)
    dx1, dy0, g_o_pre, g_e_post = _mid_norm_bwd(dx2, dh1, x1, y0, p["o_pre_norm"] + tok[0, 0], vec("e_post_norm"))

    tok = send_off("e_w_out", _mm_tn(cat0, dy0, 1, "e_out_dw").reshape(N_CHIPS, HALF // 2, D_MODEL))
    dcat0 = _mm_nt(dy0, [e_w_out], "e_out_dx", tok)
    da, dagate, g_pool_w, g_pool_scale = _pool_bwd(z0, dcat0, pool_w_bf, vec("e_pool_scale") + tok[0, 0])
    dq0, dk0, dv0, dbgate = _attn_bwd_group(0, qkv_by_residue[0], z0, att, lse, dcat0, tables)
    dq1, dk1, dv1 = _attn_bwd_group(1, qkv_by_residue[1], z0, att, lse, dcat0, tables)
    dq2, dk2, dv2 = _attn_bwd_group(2, qkv_by_residue[2], z0, att, lse, dcat0, tables)
    dz0 = jnp.concatenate([da, dagate, dq0, dq1, dq2, dk0, dk1, dk2, dv0, dv1, dv2, dbgate], axis=1)
    tok = send_off("e_w_in", _mm_tn(h0, dz0, N_CHIPS, "e_in_dw"))
    dh0 = _mm_nt(dz0, e_w_in, "e_in_dx", tok)
    grad_x, g_e_pre = _pre_norm_bwd(dx1, dh0, x, vec("e_pre_norm") + tok[0, 0])

    small = {"e_pre_norm": g_e_pre, "e_pool_w": g_pool_w, "e_pool_scale": g_pool_scale, "e_post_norm": g_e_post,
             "o_pre_norm": g_o_pre, "o_sgu_norm_g": g_sgu_g, "o_sgu_norm_b": g_sgu_b, "o_sgu_w": g_sgu_w,
             "o_sgu_b": g_bias_t[:, 0:4].T, "o_conv_w": g_conv_w, "o_conv_b": g_conv_b,
             "o_conv_norm_g": g_cn_g, "o_conv_norm_b": g_cn_b, "o_post_norm": g_o_post}
    return loss, grad_x, in_flight, small


def _land(in_flight, name, chip, after):
    sems, sums, land = in_flight[name]
    sums, land = _scatter_wait(sems, sums, land, after, f"scatter_wait_{name}")
    return _add_landed_join(sums, land, chip.astype(jnp.int32).reshape(1), f"add_landed_{name}")


def _place():
    x, y, c = lax.axis_index("x"), lax.axis_index("y"), lax.axis_index("c")
    others = [(1 - x, y), (x, 1 - y), (1 - x, 1 - y)]
    return x, y, c, 2 * x + y, others


def _all_gather(shards, name):
    n = len(shards)

    def body(*refs):
        ins, outs = refs[:n], refs[n:2 * n]
        send_sems, recv_sems, local_sems = refs[2 * n:]
        x, y, c, me, others = _place()
        sibling = (x, y, 1 - c)

        def half(a, chip, core):
            rows = ins[a].shape[0] // 2
            return outs[a].at[chip, pl.ds(core * rows, rows), :]

        def copy(a, k, src, dst, to):
            return pltpu.make_async_remote_copy(src_ref=src, dst_ref=dst, send_sem=send_sems.at[6 * a + k],
                                                recv_sem=recv_sems.at[6 * a + k], device_id=to, device_id_type=MESH)

        local = [pltpu.make_async_copy(ins[a], outs[a].at[me], local_sems.at[a]) for a in range(n)]
        for cp in local:
            cp.start()
        sent = []
        for a in range(n):
            rows = ins[a].shape[0] // 2
            mine = ins[a].at[pl.ds(c * rows, rows), :]
            for k, (ox, oy) in enumerate(others):
                sent.append(copy(a, k, mine, half(a, me, c), (ox, oy, c)))
                sent[-1].start()
        for a in range(n):
            for k, (ox, oy) in enumerate(others):
                landed = half(a, 2 * ox + oy, c)
                copy(a, k, landed, landed, (ox, oy, c)).wait_recv()
                sent.append(copy(a, 3 + k, landed, landed, sibling))
                sent[-1].start()
        for k, (ox, oy) in enumerate(others):
            chip = 2 * ox + oy
            for a in range(n):
                theirs = half(a, chip, 1 - c)
                copy(a, 3 + k, theirs, theirs, sibling).wait_recv()
        for cp in sent:
            cp.wait_send()
        for cp in local:
            cp.wait()

    return pl.pallas_call(
        body, name=name, in_specs=[ANY] * n, out_specs=[ANY] * n,
        out_shape=[SDS((N_CHIPS,) + s.shape, s.dtype) for s in shards],
        scratch_shapes=[pltpu.SemaphoreType.DMA((6 * n,)), pltpu.SemaphoreType.DMA((6 * n,)), pltpu.SemaphoreType.DMA((n,))],
    )(*shards)


def _swap_halves(parts, name):
    n = len(parts)

    def body(*refs):
        ins, own, theirs = refs[:n], refs[n:2 * n], refs[2 * n:3 * n]
        send_sems, recv_sems, local_sems = refs[3 * n:]
        x, y, c, _, _ = _place()
        sibling = (x, y, 1 - c)
        copies = []
        for a in range(n):
            rows = ins[a].shape[1] // 2
            keep = pltpu.make_async_copy(ins[a].at[:, pl.ds(c * rows, rows), :], own[a], local_sems.at[a])
            give = pltpu.make_async_remote_copy(
                src_ref=ins[a].at[:, pl.ds((1 - c) * rows, rows), :], dst_ref=theirs[a], send_sem=send_sems.at[a],
                recv_sem=recv_sems.at[a], device_id=sibling, device_id_type=MESH)
            keep.start()
            give.start()
            copies += [keep, give]
        for cp in copies:
            cp.wait()

    half = [SDS((N_CHIPS, s.shape[1] // 2, s.shape[2]), s.dtype) for s in parts]
    out = pl.pallas_call(
        body, name=name, in_specs=[ANY] * n, out_specs=[ANY] * (2 * n), out_shape=half + half,
        scratch_shapes=[pltpu.SemaphoreType.DMA((n,)), pltpu.SemaphoreType.DMA((n,)), pltpu.SemaphoreType.DMA((n,))],
    )(*parts)
    return out[:n], out[n:]


def _scatter_chips(parts, name):
    n = len(parts)

    def body(*refs):
        ins, outs = refs[:n], refs[n:2 * n]
        send_sems, recv_sems, local_sems = refs[2 * n:]
        x, y, c, me, others = _place()

        def copy(a, k, slot_from, slot_to, chip_xy):
            return pltpu.make_async_remote_copy(
                src_ref=ins[a].at[slot_from], dst_ref=outs[a].at[slot_to], send_sem=send_sems.at[3 * a + k],
                recv_sem=recv_sems.at[3 * a + k], device_id=(chip_xy[0], chip_xy[1], c), device_id_type=MESH)

        keeps, gives = [], []
        for a in range(n):
            keeps.append(pltpu.make_async_copy(ins[a].at[me], outs[a].at[me], local_sems.at[a]))
            keeps[-1].start()
            for k, (ox, oy) in enumerate(others):
                gives.append(copy(a, k, 2 * ox + oy, me, (ox, oy)))
                gives[-1].start()
        for a in range(n):
            for k, (ox, oy) in enumerate(others):
                copy(a, k, me, 2 * ox + oy, (ox, oy)).wait_recv()
        for cp in gives:
            cp.wait_send()
        for cp in keeps:
            cp.wait()

    return pl.pallas_call(
        body, name=name, in_specs=[ANY] * n, out_specs=[ANY] * n, out_shape=[SDS(s.shape, s.dtype) for s in parts],
        scratch_shapes=[pltpu.SemaphoreType.DMA((3 * n,)), pltpu.SemaphoreType.DMA((3 * n,)), pltpu.SemaphoreType.DMA((n,))],
    )(*parts)


def _join_halves(halves, name):
    n = len(halves)

    def body(*refs):
        ins, outs = refs[:n], refs[n:2 * n]
        send_sems, recv_sems, local_sems = refs[2 * n:]
        x, y, c, _, _ = _place()

        def copy(a, core):
            rows = ins[a].shape[0]
            return pltpu.make_async_remote_copy(
                src_ref=ins[a], dst_ref=outs[a].at[pl.ds(core * rows, rows), :], send_sem=send_sems.at[a],
                recv_sem=recv_sems.at[a], device_id=(x, y, 1 - c), device_id_type=MESH)

        keeps, gives = [], []
        for a in range(n):
            rows = ins[a].shape[0]
            keeps.append(pltpu.make_async_copy(ins[a], outs[a].at[pl.ds(c * rows, rows), :], local_sems.at[a]))
            gives.append(copy(a, c))
            keeps[-1].start()
            gives[-1].start()
        for a in range(n):
            copy(a, 1 - c).wait_recv()
        for cp in gives:
            cp.wait_send()
        for cp in keeps:
            cp.wait()

    return pl.pallas_call(
        body, name=name, in_specs=[ANY] * n, out_specs=[ANY] * n,
        out_shape=[SDS((2 * s.shape[0], s.shape[1]), s.dtype) for s in halves],
        scratch_shapes=[pltpu.SemaphoreType.DMA((n,)), pltpu.SemaphoreType.DMA((n,)), pltpu.SemaphoreType.DMA((n,))],
    )(*halves)


def _add_pair(a, b, name):
    _, r, c = a.shape
    tr = 256 if r % 256 == 0 else r // 2 if r > 512 else r

    def body(a_ref, b_ref, o_ref):
        o_ref[...] = (a_ref[...].astype(f32) + b_ref[...].astype(f32)).astype(o_ref.dtype)

    spec = pl.BlockSpec((None, tr, c), lambda j, i: (j, i, 0))
    return pl.pallas_call(body, name=name, grid=(N_CHIPS, r // tr), in_specs=[spec, spec], out_specs=spec,
                          out_shape=SDS(a.shape, a.dtype))(a, b)


def _add_chips(u, name):
    _, r, c = u.shape
    tr = 256 if r % 256 == 0 else r

    def body(u_ref, o_ref):
        o_ref[...] = ((u_ref[0].astype(f32) + u_ref[1].astype(f32)) + u_ref[2].astype(f32)) + u_ref[3].astype(f32)

    return pl.pallas_call(
        body, name=name, grid=(r // tr,), in_specs=[pl.BlockSpec((N_CHIPS, tr, c), lambda i: (0, i, 0))],
        out_specs=pl.BlockSpec((tr, c), lambda i: (i, 0)), out_shape=SDS((r, c), f32))(u)


SWAP_ROWS = 256


def _swap_add(g, name):
    chips, r, c = g.shape
    half = r // 2
    rows_per_step = 2 * SWAP_ROWS if half % (2 * SWAP_ROWS) == 0 else SWAP_ROWS
    nb = half // rows_per_step
    steps = chips * nb

    def body(core_ref, mine_ref, theirs_ref, out_ref, landing, send_sems, recv_sems, free_sems):
        i = pl.program_id(0)
        slot = i % 2
        x, y, core, _, _ = _place()
        sibling = (x, y, 1 - core)

        @pl.when(i >= 2)
        def _():
            pl.semaphore_wait(free_sems.at[slot], 1)

        send = pltpu.make_async_remote_copy(src_ref=theirs_ref, dst_ref=landing.at[slot], send_sem=send_sems.at[slot],
                                            recv_sem=recv_sems.at[slot], device_id=sibling, device_id_type=MESH)
        send.start()
        send.wait_recv()
        out_ref[...] = (mine_ref[...].astype(f32) + landing[slot].astype(f32)).astype(out_ref.dtype)

        @pl.when(i + 2 < steps)
        def _():
            pl.semaphore_signal(free_sems.at[slot], 1, device_id=sibling, device_id_type=MESH)

        send.wait_send()

    block = (rows_per_step, c)
    grid_spec = pltpu.PrefetchScalarGridSpec(
        num_scalar_prefetch=1, grid=(steps,),
        in_specs=[pl.BlockSpec(block, lambda i, core: ((2 * (i // nb) + core[0]) * nb + i % nb, 0)),
                  pl.BlockSpec(block, lambda i, core: ((2 * (i // nb) + 1 - core[0]) * nb + i % nb, 0))],
        out_specs=pl.BlockSpec(block, lambda i, core: (i, 0)),
        scratch_shapes=[pltpu.VMEM((2, rows_per_step, c), g.dtype), pltpu.SemaphoreType.DMA((2,)),
                        pltpu.SemaphoreType.DMA((2,)), pltpu.SemaphoreType.REGULAR((2,))])
    core = lax.axis_index("c").astype(jnp.int32).reshape(1)
    rows = g.reshape(chips * r, c)
    out = pl.pallas_call(body, name=name, grid_spec=grid_spec, out_shape=SDS((chips * half, c), g.dtype))(core, rows, rows)
    return out.reshape(chips, half, c)


def _reduce_scatter(parts, tag):
    own, theirs = _swap_halves(parts, f"swap_halves_{tag}")
    chip_sums = [_add_pair(o, t, f"add_cores_{tag}{i}") for i, (o, t) in enumerate(zip(own, theirs))]
    gathered = _scatter_chips(chip_sums, f"scatter_chips_{tag}")
    halves = [_add_chips(u, f"add_chips_{tag}{i}") for i, u in enumerate(gathered)]
    return _join_halves(halves, f"join_halves_{tag}")


HBM = pl.BlockSpec(memory_space=pltpu.HBM)
SEM = pl.BlockSpec(memory_space=pltpu.SEMAPHORE)
EFFECT = pltpu.SideEffectType.DATAFLOW_SIDE_EFFECTING


def _in_hbm(a):
    return pltpu.with_memory_space_constraint(a, pltpu.HBM)


def _cast_into_slot(w, chip, name, piece=0, pieces=1):
    r, c = w.shape
    c = c // pieces
    nb = r // SWAP_ROWS

    def body(chip_ref, w_ref, o_ref):
        o_ref[...] = w_ref[...].astype(bf16)

    grid_spec = pltpu.PrefetchScalarGridSpec(
        num_scalar_prefetch=1, grid=(nb,),
        in_specs=[pl.BlockSpec((SWAP_ROWS, c), lambda i, chip: (i, piece))],
        out_specs=pl.BlockSpec((SWAP_ROWS, c), lambda i, chip: (chip[0] * nb + i, 0)))
    out = pl.pallas_call(body, name=name, grid_spec=grid_spec, out_shape=SDS((N_CHIPS * r, c), bf16))(chip, w)
    return out.reshape(N_CHIPS, r, c)


def _gather_start(bufs):
    n = len(bufs)

    def body(*refs):
        ins, sems, token = refs[:n], refs[n:3 * n], refs[4 * n]
        x, y, c, me, others = _place()
        for a in range(n):
            rows = ins[a].shape[1] // 2
            mine = ins[a].at[me, pl.ds(c * rows, rows), :]
            for k, (ox, oy) in enumerate(others):
                pltpu.make_async_remote_copy(src_ref=mine, dst_ref=mine, send_sem=sems[2 * a].at[k],
                                             recv_sem=sems[2 * a + 1].at[k], device_id=(ox, oy, c),
                                             device_id_type=MESH).start()
        token[...] = jnp.zeros_like(token)

    out = pl.pallas_call(
        body, name="gather_start", in_specs=[HBM] * n,
        out_shape=(*[pltpu.SemaphoreType.DMA((3,))] * (2 * n), *[pltpu.HBM(b.shape, b.dtype) for b in bufs],
                   SDS((8, 128), f32)),
        out_specs=(*[SEM] * (2 * n), *[HBM] * n, pl.BlockSpec(memory_space=pltpu.VMEM)),
        input_output_aliases={a: 2 * n + a for a in range(n)},
        compiler_params=pltpu.CompilerParams(has_side_effects=EFFECT),
    )(*[_in_hbm(b) for b in bufs])
    return [(out[2 * a], out[2 * a + 1]) for a in range(n)], list(out[2 * n:3 * n]), out[3 * n]


def _gather_wait(bufs, sems, after, name):
    n = len(bufs)

    def body(*refs):
        ins, sem_refs = refs[:n], refs[n:3 * n]
        x, y, c, me, others = _place()
        for a in range(n):
            rows = ins[a].shape[1] // 2
            mine = ins[a].at[me, pl.ds(c * rows, rows), :]
            for k, (ox, oy) in enumerate(others):
                landed = ins[a].at[2 * ox + oy, pl.ds(c * rows, rows), :]
                copy = pltpu.make_async_remote_copy(src_ref=mine, dst_ref=landed, send_sem=sem_refs[2 * a].at[k],
                                                    recv_sem=sem_refs[2 * a + 1].at[k], device_id=(ox, oy, c),
                                                    device_id_type=MESH)
                copy.wait_send()
                copy.wait_recv()

    flat_sems = [s for pair in sems for s in pair]
    out = pl.pallas_call(
        body, name=name, in_specs=[HBM] * n + [SEM] * (2 * n) + [ANY],
        out_shape=tuple(pltpu.HBM(b.shape, b.dtype) for b in bufs), out_specs=tuple([HBM] * n),
        input_output_aliases={a: a for a in range(n)},
        compiler_params=pltpu.CompilerParams(has_side_effects=EFFECT),
    )(*bufs, *flat_sems, after)
    return list(out)


def _forward_halves(bufs, name):
    n = len(bufs)
    blocks = []
    for b in bufs:
        half = b.shape[1] // 2
        tr = SWAP_ROWS if half % SWAP_ROWS == 0 else half
        blocks.append((half, tr))
    work = [(a, k, b) for a in range(n) for k in range(3) for b in range(blocks[a][0] // blocks[a][1])]

    def body(*refs):
        outs, stages = refs[n:2 * n], refs[2 * n:3 * n]
        load_sems, send_sems, recv_sems = refs[3 * n:]
        x, y, c, me, others = _place()
        sibling = (x, y, 1 - c)

        def rows(item):
            a, k, b = item
            half, tr = blocks[a]
            ox, oy = others[k]
            return outs[a].at[2 * ox + oy, pl.ds(c * half + b * tr, tr), :]

        def load(s, item):
            return pltpu.make_async_copy(rows(item), stages[item[0]].at[s], load_sems.at[s])

        def send(s, item):
            return pltpu.make_async_remote_copy(src_ref=stages[item[0]].at[s], dst_ref=rows(item), send_sem=send_sems.at[s],
                                                recv_sem=recv_sems.at[item[0]], device_id=sibling, device_id_type=MESH)

        load(0, work[0]).start()
        for t, item in enumerate(work):
            s = t % 2
            load(s, item).wait()
            send(s, item).start()
            if t + 1 < len(work):
                if t >= 1:
                    send(1 - s, work[t - 1]).wait_send()
                load(1 - s, work[t + 1]).start()
        if len(work) > 1:
            send(len(work) % 2, work[-2]).wait_send()
        send((len(work) - 1) % 2, work[-1]).wait_send()
        for a in range(n):
            theirs = outs[a].at[pl.ds(0, 3), pl.ds(0, blocks[a][0]), :]
            pltpu.make_async_remote_copy(src_ref=theirs, dst_ref=theirs, send_sem=send_sems.at[0], recv_sem=recv_sems.at[a],
                                         device_id=sibling, device_id_type=MESH).wait_recv()

    out = pl.pallas_call(
        body, name=name, in_specs=[ANY] * n, out_specs=[ANY] * n, out_shape=[SDS(b.shape, b.dtype) for b in bufs],
        input_output_aliases={a: a for a in range(n)},
        scratch_shapes=[pltpu.VMEM((2, blocks[a][1], bufs[a].shape[2]), bufs[a].dtype) for a in range(n)]
        + [pltpu.SemaphoreType.DMA((2,)), pltpu.SemaphoreType.DMA((2,)), pltpu.SemaphoreType.DMA((n,))],
    )(*bufs)
    return list(out)


def _scatter_start(chip_sums, name):
    def body(a_ref, land_ref, send_sems, recv_sems, a_thru, land_thru, token):
        x, y, c, me, others = _place()
        for k, (ox, oy) in enumerate(others):
            pltpu.make_async_remote_copy(src_ref=a_ref.at[2 * ox + oy], dst_ref=land_ref.at[me], send_sem=send_sems.at[k],
                                         recv_sem=recv_sems.at[k], device_id=(ox, oy, c), device_id_type=MESH).start()
        token[...] = jnp.zeros_like(token)

    shape = pltpu.HBM(chip_sums.shape, chip_sums.dtype)
    send, recv, a_thru, land, token = pl.pallas_call(
        body, name=name, in_specs=[HBM, HBM],
        out_shape=(pltpu.SemaphoreType.DMA((3,)), pltpu.SemaphoreType.DMA((3,)), shape, shape, SDS((8, 128), f32)),
        out_specs=(SEM, SEM, HBM, HBM, pl.BlockSpec(memory_space=pltpu.VMEM)), input_output_aliases={0: 2, 1: 3},
        compiler_params=pltpu.CompilerParams(has_side_effects=EFFECT),
    )(_in_hbm(chip_sums), _in_hbm(lax.empty(chip_sums.shape, chip_sums.dtype)))
    return (send, recv), a_thru, land, token


def _scatter_wait(sems, chip_sums, land, after, name):
    def body(a_ref, land_ref, send_sems, recv_sems, after_ref, a_out, land_out):
        x, y, c, me, others = _place()
        for k, (ox, oy) in enumerate(others):
            copy = pltpu.make_async_remote_copy(
                src_ref=a_ref.at[2 * ox + oy], dst_ref=land_ref.at[2 * ox + oy], send_sem=send_sems.at[k],
                recv_sem=recv_sems.at[k], device_id=(ox, oy, c), device_id_type=MESH)
            copy.wait_send()
            copy.wait_recv()

    shape = pltpu.HBM(chip_sums.shape, chip_sums.dtype)
    return pl.pallas_call(
        body, name=name, in_specs=[HBM, HBM, SEM, SEM, ANY], out_shape=(shape, shape), out_specs=(HBM, HBM),
        input_output_aliases={0: 0, 1: 1}, compiler_params=pltpu.CompilerParams(has_side_effects=EFFECT),
    )(chip_sums, land, sems[0], sems[1], after)


def _add_landed_join(chip_sums, land, chip, name):
    chips, rh, c = chip_sums.shape
    nb = rh // SWAP_ROWS

    def body(chip_ref, own_ref, l1_ref, l2_ref, l3_ref, out_hbm, buf, send_sems, recv_sem, local_sems):
        i = pl.program_id(0)
        slot = i % 2
        x, y, core, _, _ = _place()
        sibling = (x, y, 1 - core)

        def copies(s, step):
            rows = pl.ds(pl.multiple_of((core * nb + step) * SWAP_ROWS, SWAP_ROWS), SWAP_ROWS)
            keep = pltpu.make_async_copy(buf.at[s], out_hbm.at[rows, :], local_sems.at[s])
            give = pltpu.make_async_remote_copy(src_ref=buf.at[s], dst_ref=out_hbm.at[rows, :], send_sem=send_sems.at[s],
                                                recv_sem=recv_sem.at[0], device_id=sibling, device_id_type=MESH)
            return keep, give

        def drain(s, step):
            keep, give = copies(s, step)
            keep.wait()
            give.wait_send()

        @pl.when(i >= 2)
        def _():
            drain(slot, i - 2)

        buf[slot] = ((own_ref[...].astype(f32) + l1_ref[...].astype(f32)) + l2_ref[...].astype(f32)) + l3_ref[...].astype(f32)
        keep, give = copies(slot, i)
        keep.start()
        give.start()

        @pl.when(i == nb - 1)
        def _():
            drain(slot, i)
            if nb > 1:
                drain(1 - slot, i - 1)
            theirs = out_hbm.at[pl.ds((1 - core) * rh, rh), :]
            pltpu.make_async_remote_copy(src_ref=theirs, dst_ref=theirs, send_sem=send_sems.at[0], recv_sem=recv_sem.at[0],
                                         device_id=sibling, device_id_type=MESH).wait_recv()

    block = (SWAP_ROWS, c)
    from_slot = lambda d: pl.BlockSpec(block, lambda i, chip: (((chip[0] + d) % chips) * nb + i, 0))
    grid_spec = pltpu.PrefetchScalarGridSpec(
        num_scalar_prefetch=1, grid=(nb,), in_specs=[from_slot(0), from_slot(1), from_slot(2), from_slot(3)],
        out_specs=ANY,
        scratch_shapes=[pltpu.VMEM((2, SWAP_ROWS, c), f32), pltpu.SemaphoreType.DMA((2,)),
                        pltpu.SemaphoreType.DMA((1,)), pltpu.SemaphoreType.DMA((2,))])
    land_rows = land.reshape(chips * rh, c)
    return pl.pallas_call(body, name=name, grid_spec=grid_spec, out_shape=SDS((2 * rh, c), f32))(
        chip, chip_sums.reshape(chips * rh, c), land_rows, land_rows, land_rows)


def _adamw_update(w_ref, g_ref, m_ref, v_ref, d_ref, nm_ref, nv_ref):
    g = g_ref[...]
    nm = ADAM_B1 * m_ref[...] + (1.0 - ADAM_B1) * g
    nv = ADAM_B2 * v_ref[...] + (1.0 - ADAM_B2) * (g * g)
    nm_ref[...] = nm
    nv_ref[...] = nv
    m_hat = nm / (1.0 - ADAM_B1 ** ADAM_STEP)
    v_hat = nv / (1.0 - ADAM_B2 ** ADAM_STEP)
    d_ref[...] = -ADAM_LR * (m_hat / (jnp.sqrt(v_hat) + ADAM_EPS) + ADAM_WD * w_ref[...])


def _adamw(w, g, m, v, name):
    r, c = w.shape
    tr = 128 if r % 128 == 0 else r

    def body(w_ref, g_ref, m_ref, v_ref, g_out_ref, d_ref, nm_ref, nv_ref):
        g_out_ref[...] = g_ref[...]
        _adamw_update(w_ref, g_ref, m_ref, v_ref, d_ref, nm_ref, nv_ref)

    spec = pl.BlockSpec((tr, c), lambda i: (i, 0))
    return pl.pallas_call(body, name=name, grid=(r // tr,), in_specs=[spec] * 4, out_specs=[spec] * 4,
                          out_shape=[SDS((r, c), f32)] * 4)(w, g, m, v)


def _adamw_small(ws, gs, ms, vs):
    n = len(ws)

    def body(*refs):
        for i in range(n):
            _adamw_update(*refs[i:7 * n:n])

    whole = pl.BlockSpec(memory_space=pltpu.VMEM)
    out = pl.pallas_call(body, name="adamw_small", in_specs=[whole] * (4 * n), out_specs=[whole] * (3 * n),
                         out_shape=[SDS(a.shape, f32) for a in ws] * 3)(*ws, *gs, *ms, *vs)
    return out[:n], out[n:2 * n], out[2 * n:]


def _pack(arrays, total_rows=None):
    parts = []
    rows = 0
    for a in arrays:
        flat = a.reshape(-1, LANES)
        pad = -flat.shape[0] % 8
        parts.append(jnp.pad(flat, ((0, pad), (0, 0))))
        rows += flat.shape[0] + pad
    if total_rows is not None:
        parts.append(jnp.zeros((total_rows - rows, LANES), arrays[0].dtype))
    return jnp.concatenate(parts, axis=0)


def _unpack(buf, shapes):
    out = []
    row = 0
    lead = buf.shape[:-2]
    for shape in shapes:
        size = 1
        for s in shape:
            size *= s
        rows = size // LANES
        out.append(buf[..., row:row + rows, :].reshape(lead + tuple(shape)))
        row += rows + (-rows % 8)
    return out


BIG = ("e_w_in", "e_w_out", "o_w_in", "o_w_out")
SHARDED_SMALL = {
    "e_pool_w": ((4, 64, 256), 1), "o_pre_norm": ((512,), 0), "o_sgu_norm_g": ((256,), 0), "o_sgu_norm_b": ((256,), 0),
    "o_conv_w": ((31, 256), 1), "o_conv_b": ((256,), 0), "o_conv_norm_g": ((256,), 0), "o_conv_norm_b": ((256,), 0),
    "o_post_norm": ((512,), 0),
}
REPLICATED_SMALL = {"e_pre_norm": (2048,), "e_pool_scale": (1024,), "e_post_norm": (2048,),
                    "o_sgu_w": (4, 128, 128), "o_sgu_b": (4, 128)}
SMALL_ORDER = ("e_pre_norm", "e_pool_w", "e_pool_scale", "e_post_norm", "o_pre_norm", "o_sgu_norm_g", "o_sgu_norm_b",
               "o_sgu_w", "o_sgu_b", "o_conv_w", "o_conv_b", "o_conv_norm_g", "o_conv_norm_b", "o_post_norm")
ALL_ORDER = ("e_pre_norm", "e_w_in", "e_pool_w", "e_pool_scale", "e_w_out", "e_post_norm", "o_pre_norm", "o_w_in",
             "o_sgu_norm_g", "o_sgu_norm_b", "o_sgu_w", "o_sgu_b", "o_conv_w", "o_conv_b", "o_conv_norm_g",
             "o_conv_norm_b", "o_w_out", "o_post_norm")


def _full_shape(name):
    shape, axis = SHARDED_SMALL[name]
    return tuple(s * N_CHIPS if i == axis else s for i, s in enumerate(shape))


def _from_chips(name, stacked):
    shape, axis = SHARDED_SMALL[name]
    return jnp.moveaxis(stacked, 0, axis).reshape(_full_shape(name))


def _my_shard(name, full, chip):
    shape, axis = SHARDED_SMALL[name]
    return lax.dynamic_slice_in_dim(full, chip * shape[axis], shape[axis], axis)


def kernel(x, e_pre_norm, e_w_in, e_pool_w, e_pool_scale, e_w_out, e_post_norm, o_pre_norm, o_w_in, o_sgu_norm_g, o_sgu_norm_b, o_sgu_w, o_sgu_b, o_conv_w, o_conv_b, o_conv_norm_g, o_conv_norm_b, o_w_out, o_post_norm, loss_target, m_e_pre_norm, m_e_w_in, m_e_pool_w, m_e_pool_scale, m_e_w_out, m_e_post_norm, m_o_pre_norm, m_o_w_in, m_o_sgu_norm_g, m_o_sgu_norm_b, m_o_sgu_w, m_o_sgu_b, m_o_conv_w, m_o_conv_b, m_o_conv_norm_g, m_o_conv_norm_b, m_o_w_out, m_o_post_norm, v_e_pre_norm, v_e_w_in, v_e_pool_w, v_e_pool_scale, v_e_w_out, v_e_post_norm, v_o_pre_norm, v_o_w_in, v_o_sgu_norm_g, v_o_sgu_norm_b, v_o_sgu_w, v_o_sgu_b, v_o_conv_w, v_o_conv_b, v_o_conv_norm_g, v_o_conv_norm_b, v_o_w_out, v_o_post_norm):
    w = dict(e_pre_norm=e_pre_norm, e_w_in=e_w_in, e_pool_w=e_pool_w, e_pool_scale=e_pool_scale, e_w_out=e_w_out,
             e_post_norm=e_post_norm, o_pre_norm=o_pre_norm, o_w_in=o_w_in, o_sgu_norm_g=o_sgu_norm_g,
             o_sgu_norm_b=o_sgu_norm_b, o_sgu_w=o_sgu_w, o_sgu_b=o_sgu_b, o_conv_w=o_conv_w, o_conv_b=o_conv_b,
             o_conv_norm_g=o_conv_norm_g, o_conv_norm_b=o_conv_norm_b, o_w_out=o_w_out, o_post_norm=o_post_norm)
    m = dict(e_pre_norm=m_e_pre_norm, e_w_in=m_e_w_in, e_pool_w=m_e_pool_w, e_pool_scale=m_e_pool_scale,
             e_w_out=m_e_w_out, e_post_norm=m_e_post_norm, o_pre_norm=m_o_pre_norm, o_w_in=m_o_w_in,
             o_sgu_norm_g=m_o_sgu_norm_g, o_sgu_norm_b=m_o_sgu_norm_b, o_sgu_w=m_o_sgu_w, o_sgu_b=m_o_sgu_b,
             o_conv_w=m_o_conv_w, o_conv_b=m_o_conv_b, o_conv_norm_g=m_o_conv_norm_g, o_conv_norm_b=m_o_conv_norm_b,
             o_w_out=m_o_w_out, o_post_norm=m_o_post_norm)
    v = dict(e_pre_norm=v_e_pre_norm, e_w_in=v_e_w_in, e_pool_w=v_e_pool_w, e_pool_scale=v_e_pool_scale,
             e_w_out=v_e_w_out, e_post_norm=v_e_post_norm, o_pre_norm=v_o_pre_norm, o_w_in=v_o_w_in,
             o_sgu_norm_g=v_o_sgu_norm_g, o_sgu_norm_b=v_o_sgu_norm_b, o_sgu_w=v_o_sgu_w, o_sgu_b=v_o_sgu_b,
             o_conv_w=v_o_conv_w, o_conv_b=v_o_conv_b, o_conv_norm_g=v_o_conv_norm_g, o_conv_norm_b=v_o_conv_norm_b,
             o_w_out=v_o_w_out, o_post_norm=v_o_post_norm)
    w, m, v = ({k: a[0] for k, a in d.items()} for d in (w, m, v))
    chip = 2 * lax.axis_index("x") + lax.axis_index("y")

    loss, grad_x, in_flight, small = _step(x[0], loss_target[0], w, chip)

    grads, delta, new_m, new_v = {}, {}, {}, {}
    after = grad_x
    for k in ("o_w_out", "o_w_in", "e_w_out", "e_w_in"):
        grads[k], delta[k], new_m[k], new_v[k] = _adamw(w[k], _land(in_flight, k, chip, after), m[k], v[k], f"adamw_{k}")
        after = delta[k]

    small_full_shapes = {k: (_full_shape(k) if k in SHARDED_SMALL else REPLICATED_SMALL[k]) for k in SMALL_ORDER}
    small_parts = _pack([small[k].reshape(small_full_shapes[k]) for k in SMALL_ORDER], total_rows=SMALL_GRAD_ROWS)
    small_parts = small_parts + 0.0 * after[0, 0]
    reduced = _reduce_scatter([small_parts.reshape(N_CHIPS, SMALL_GRAD_ROWS // N_CHIPS, LANES)], "small")
    small_sum = _all_gather(reduced, "gather_small_grads")[0].reshape(SMALL_GRAD_ROWS, LANES)
    for k, a in zip(SMALL_ORDER, _unpack(small_sum, [small_full_shapes[k] for k in SMALL_ORDER])):
        grads[k] = _my_shard(k, a, chip) if k in SHARDED_SMALL else a
    loss = lax.psum(loss[0, 0], ("x", "y", "c"))

    def rows_of(a):
        return a.reshape(-1, a.shape[-1])

    updates = _adamw_small(*[[rows_of(d[k]) for k in SMALL_ORDER] for d in (w, grads, m, v)])
    for d, arrays in zip((delta, new_m, new_v), updates):
        for k, a in zip(SMALL_ORDER, arrays):
            d[k] = a.reshape(w[k].shape)

    outs = [loss, grad_x[None]]
    for d in (grads, delta, new_m, new_v):
        outs += [d[k][None] for k in ALL_ORDER]
    return tuple(outs)
```

```python
import jax
import jax.numpy as jnp
from jax import lax
from jax.experimental import pallas as pl
from jax.experimental.pallas import tpu as pltpu

f32 = jnp.float32
bf16 = jnp.bfloat16
SDS = jax.ShapeDtypeStruct

SEQ = 2048
D_MODEL = 2048
EPS = 1e-6
NEG = -1e30
HEAD_DIM = 128
ROT_HALF = 16
ROPE_THETA = 500000.0
DILATIONS = (1, 4, 16)
SPAN = 128
N_HEADS = 8
HALF = 1024
POOL_CH = 256
CONV_K = 31
CONV_PAD = 32
CHUNK = 128
N_CHIPS = 4
LANES = 256
E_IN_PIECES = 3
SMALL_SHARD_ROWS = 352
SMALL_GRAD_ROWS = 1536
ANY = pl.BlockSpec(memory_space=pl.ANY)
MESH = pl.DeviceIdType.MESH

ADAM_LR = 0.001
ADAM_B1 = 0.9
ADAM_B2 = 0.999
ADAM_EPS = 1e-08
ADAM_WD = 0.01
ADAM_STEP = 10


def _dot(a, b):
    return jnp.dot(a, b, preferred_element_type=f32)


def _dot_nt(a, b):
    return lax.dot_general(a, b, (((1,), (1,)), ((), ())), preferred_element_type=f32)


def _dot_tn(a, b):
    return lax.dot_general(a, b, (((0,), (0,)), ((), ())), preferred_element_type=f32)


def _sigmoid(x):
    return 1.0 / (1.0 + jnp.exp(-x))


def _silu_and_grad(x):
    s = _sigmoid(x)
    return x * s, s * (1.0 + x * (1.0 - s))


def _rms_fwd(x, g):
    r = lax.rsqrt(jnp.mean(x * x, axis=-1, keepdims=True) + EPS)
    return x * r * g


def _rms_bwd(x, g, dout):
    r = lax.rsqrt(jnp.mean(x * x, axis=-1, keepdims=True) + EPS)
    xh = x * r
    dg = jnp.sum(dout * xh, axis=0, keepdims=True)
    dxh = dout * g
    dx = r * (dxh - xh * jnp.mean(dxh * xh, axis=-1, keepdims=True))
    return dx, dg


def _ln_stats(x):
    mu = jnp.mean(x, axis=-1, keepdims=True)
    xc = x - mu
    rstd = lax.rsqrt(jnp.mean(xc * xc, axis=-1, keepdims=True) + EPS)
    return xc * rstd, rstd


def _ln_bwd(xh, rstd, g, dout):
    dg = jnp.sum(dout * xh, axis=0, keepdims=True)
    db = jnp.sum(dout, axis=0, keepdims=True)
    dxh = dout * g
    dx = rstd * (dxh - jnp.mean(dxh, axis=-1, keepdims=True) - xh * jnp.mean(dxh * xh, axis=-1, keepdims=True))
    return dx, dg, db


def _accumulate(ref, value, first):
    @pl.when(first)
    def _():
        ref[...] = value

    @pl.when(jnp.logical_not(first))
    def _():
        ref[...] += value


def _col_tile(ns):
    for t in (1024, 768, 512, 256):
        if ns % t == 0:
            return t
    raise ValueError(ns)


def _mm_nn(a, w, out_dtype, name, piece=0, pieces=1, into=None):
    m, k = a.shape
    j, _, ns = w.shape
    tm, tn = 1024, _col_tile(ns)
    nb = ns // tn

    def body(a_ref, w_ref, *rest):
        rest[-1][...] = _dot(a_ref[...], w_ref[...]).astype(rest[-1].dtype)

    return pl.pallas_call(
        body, name=name, grid=(j * nb, m // tm),
        in_specs=[pl.BlockSpec((tm, k), lambda n, i: (i, 0)),
                  pl.BlockSpec((None, k, tn), lambda n, i: (n // nb, 0, n % nb))] + ([] if into is None else [ANY]),
        out_specs=pl.BlockSpec((tm, tn), lambda n, i: (i, ((n // nb) * pieces + piece) * nb + n % nb)),
        out_shape=SDS((m, j * ns * pieces), out_dtype),
        input_output_aliases={} if into is None else {2: 0},
    )(a, w, *([] if into is None else [into]))


def _mm_nt(dz, ws, name, after):
    m, _ = dz.shape
    pieces = len(ws)
    j, k, ns = ws[0].shape
    tm, tk = 1024, 1024

    def body(dz_ref, *rest):
        w_refs, o_ref = rest[:pieces], rest[-1]
        total = _dot_nt(dz_ref[:, 0:ns], w_refs[0][...])
        for q in range(1, pieces):
            total = total + _dot_nt(dz_ref[:, q * ns:(q + 1) * ns], w_refs[q][...])
        _accumulate(o_ref, total, pl.program_id(2) == 0)

    return pl.pallas_call(
        body, name=name, grid=(m // tm, k // tk, j),
        in_specs=[pl.BlockSpec((tm, pieces * ns), lambda i, kk, r: (i, r))]
        + [pl.BlockSpec((None, tk, ns), lambda i, kk, r: (r, kk, 0))] * pieces + [ANY],
        out_specs=pl.BlockSpec((tm, tk), lambda i, kk, r: (i, kk)),
        out_shape=SDS((m, k), f32),
    )(dz, *ws, after)


def _mm_tn(a, dz, j, name):
    m, k = a.shape
    ns = dz.shape[1] // j
    tk, tn = 1024, _col_tile(ns)
    nb = ns // tn

    def body(a_ref, dz_ref, o_ref):
        o_ref[...] = _dot_tn(a_ref[...], dz_ref[...]).astype(o_ref.dtype)

    return pl.pallas_call(
        body, name=name, grid=(k // tk, j * nb),
        in_specs=[pl.BlockSpec((m, tk), lambda kk, n: (0, kk)),
                  pl.BlockSpec((m, tn), lambda kk, n: (0, n))],
        out_specs=pl.BlockSpec((None, tk, tn), lambda kk, n: (n // nb, kk, n % nb)),
        out_shape=SDS((j, k, ns), bf16),
    )(a, dz)


ROWS = 256


def _row_spec(width=D_MODEL, col=0):
    return pl.BlockSpec((ROWS, width), lambda i: (i, col))


def _vec_spec(width=D_MODEL):
    return pl.BlockSpec((1, width), lambda i: (0, 0))


def _pre_norm(x, g):
    def body(x_ref, g_ref, h_ref):
        h_ref[...] = _rms_fwd(x_ref[...], g_ref[...]).astype(bf16)

    return pl.pallas_call(
        body, name="pre_norm", grid=(SEQ // ROWS,), in_specs=[_row_spec(), _vec_spec()],
        out_specs=_row_spec(), out_shape=SDS((SEQ, D_MODEL), bf16))(x, g)


def _mid_norm(x, y, g_post, g_pre):
    def body(x_ref, y_ref, gpost_ref, gpre_ref, x1_ref, h1_ref):
        x1 = x_ref[...] + _rms_fwd(y_ref[...], gpost_ref[...])
        x1_ref[...] = x1
        h1_ref[...] = _rms_fwd(x1, gpre_ref[...]).astype(bf16)

    return pl.pallas_call(
        body, name="mid_norm", grid=(SEQ // ROWS,),
        in_specs=[_row_spec(), _row_spec(), _vec_spec(), _vec_spec()],
        out_specs=[_row_spec(), _row_spec()],
        out_shape=[SDS((SEQ, D_MODEL), f32), SDS((SEQ, D_MODEL), bf16)])(x, y, g_post, g_pre)


def _final_norm_loss(x1, y, g_post, target):
    def body(x1_ref, y_ref, g_ref, t_ref, loss_ref, dx2_ref, dy_ref, dg_ref):
        first = pl.program_id(0) == 0
        y = y_ref[...]
        g = g_ref[...]
        err = x1_ref[...] + _rms_fwd(y, g) - t_ref[...]
        sq = jnp.sum(jnp.sum(err * err, axis=1, keepdims=True), axis=0, keepdims=True)
        _accumulate(loss_ref, sq * (0.5 / D_MODEL), first)
        dx2 = err * (1.0 / D_MODEL)
        dx2_ref[...] = dx2
        dy, dg = _rms_bwd(y, g, dx2)
        dy_ref[...] = dy.astype(bf16)
        _accumulate(dg_ref, dg, first)

    return pl.pallas_call(
        body, name="final_norm_loss", grid=(SEQ // ROWS,),
        in_specs=[_row_spec(), _row_spec(), _vec_spec(), _row_spec()],
        out_specs=[pl.BlockSpec((1, 1), lambda i: (0, 0)), _row_spec(), _row_spec(), _vec_spec()],
        out_shape=[SDS((1, 1), f32), SDS((SEQ, D_MODEL), f32), SDS((SEQ, D_MODEL), bf16), SDS((1, D_MODEL), f32)],
    )(x1, y, g_post, target)


def _mid_norm_bwd(dx2, dh1, x1, y0, g_pre, g_post):
    def body(dx2_ref, dh1_ref, x1_ref, y0_ref, gpre_ref, gpost_ref, dx1_ref, dy0_ref, dgpre_ref, dgpost_ref):
        first = pl.program_id(0) == 0
        d_in, dgpre = _rms_bwd(x1_ref[...], gpre_ref[...], dh1_ref[...])
        dx1 = dx2_ref[...] + d_in
        dx1_ref[...] = dx1
        dy0, dgpost = _rms_bwd(y0_ref[...], gpost_ref[...], dx1)
        dy0_ref[...] = dy0.astype(bf16)
        _accumulate(dgpre_ref, dgpre, first)
        _accumulate(dgpost_ref, dgpost, first)

    return pl.pallas_call(
        body, name="mid_norm_bwd", grid=(SEQ // ROWS,),
        in_specs=[_row_spec(), _row_spec(), _row_spec(), _row_spec(), _vec_spec(), _vec_spec()],
        out_specs=[_row_spec(), _row_spec(), _vec_spec(), _vec_spec()],
        out_shape=[SDS((SEQ, D_MODEL), f32), SDS((SEQ, D_MODEL), bf16), SDS((1, D_MODEL), f32), SDS((1, D_MODEL), f32)],
    )(dx2, dh1, x1, y0, g_pre, g_post)


def _pre_norm_bwd(dx1, dh0, x, g):
    def body(dx1_ref, dh0_ref, x_ref, g_ref, dx_ref, dg_ref):
        d_in, dg = _rms_bwd(x_ref[...], g_ref[...], dh0_ref[...])
        dx_ref[...] = dx1_ref[...] + d_in
        _accumulate(dg_ref, dg, pl.program_id(0) == 0)

    return pl.pallas_call(
        body, name="pre_norm_bwd", grid=(SEQ // ROWS,),
        in_specs=[_row_spec(), _row_spec(), _row_spec(), _vec_spec()],
        out_specs=[_row_spec(), _vec_spec()],
        out_shape=[SDS((SEQ, D_MODEL), f32), SDS((1, D_MODEL), f32)])(dx1, dh0, x, g)


def _pool_count(g):
    row = lax.broadcasted_iota(jnp.int32, (SEQ, 1), 0)
    width = jnp.left_shift(2, g)
    return row, width, jnp.minimum(row + 1, width).astype(f32)


def _trailing_sum(x, row, width):
    s = x
    for k in (1, 2, 4, 8):
        shifted = jnp.where(row >= k, pltpu.roll(s, k, 0), 0.0)
        s = jnp.where(width > k, s + shifted, s)
    return s


def _leading_sum(x, row, width):
    s = x
    for k in (1, 2, 4, 8):
        shifted = jnp.where(row < SEQ - k, pltpu.roll(s, SEQ - k, 0), 0.0)
        s = jnp.where(width > k, s + shifted, s)
    return s


def _pool_specs():
    a_in = pl.BlockSpec((SEQ, POOL_CH), lambda g: (0, g))
    a_gate = pl.BlockSpec((SEQ, POOL_CH), lambda g: (0, 4 + g))
    w = pl.BlockSpec((None, POOL_CH, POOL_CH), lambda g: (g, 0, 0))
    scale = pl.BlockSpec((1, POOL_CH), lambda g: (0, g))
    return a_in, a_gate, w, scale


def _pool_fwd(z0, pool_w, pool_scale):
    def body(a_ref, gate_ref, w_ref, scale_ref, ya_ref):
        row, width, count = _pool_count(pl.program_id(0))
        a = a_ref[...]
        pooled = _trailing_sum(a, row, width) / count - a
        mixed = _dot(pooled.astype(bf16), w_ref[...]) * scale_ref[...]
        gate = gate_ref[...]
        ya_ref[...] = (mixed * gate * _sigmoid(gate)).astype(bf16)

    return pl.pallas_call(
        body, name="pool_fwd", grid=(4,), in_specs=list(_pool_specs()),
        out_specs=pl.BlockSpec((SEQ, POOL_CH), lambda g: (0, g)),
        out_shape=SDS((SEQ, HALF), bf16))(z0, z0, pool_w, pool_scale)


def _pool_bwd(z0, dcat, pool_w, pool_scale):
    def body(a_ref, gate_ref, w_ref, scale_ref, dya_ref, da_ref, dgate_ref, dw_ref, dscale_ref):
        row, width, count = _pool_count(pl.program_id(0))
        a = a_ref[...]
        pooled = (_trailing_sum(a, row, width) / count - a).astype(bf16)
        w = w_ref[...]
        scale = scale_ref[...]
        mixed = _dot(pooled, w)
        silu, dsilu = _silu_and_grad(gate_ref[...])
        dya = dya_ref[...]
        dgate_ref[...] = (dya * mixed * scale * dsilu).astype(bf16)
        dms = dya * silu
        dscale_ref[...] = jnp.sum(dms * mixed, axis=0, keepdims=True)
        dmixed = (dms * scale).astype(bf16)
        dw_ref[...] = _dot_tn(pooled, dmixed)
        dpooled = _dot_nt(dmixed, w)
        da_ref[...] = (_leading_sum(dpooled / count, row, width) - dpooled).astype(bf16)

    a_in, a_gate, w, scale = _pool_specs()
    col = pl.BlockSpec((SEQ, POOL_CH), lambda g: (0, g))
    return pl.pallas_call(
        body, name="pool_bwd", grid=(4,), in_specs=[a_in, a_gate, w, scale, col],
        out_specs=[col, col, w, scale],
        out_shape=[SDS((SEQ, HALF), bf16), SDS((SEQ, HALF), bf16), SDS((4, POOL_CH, POOL_CH), f32), SDS((1, HALF), f32)],
    )(z0, z0, pool_w, pool_scale, dcat)


Q_COL, K_COL, V_COL, BGATE_COL = 16, 40, 64, 88


def _rope_tables():
    pos = jnp.arange(SEQ, dtype=f32)
    inv_freq = jnp.power(ROPE_THETA, -jnp.arange(0, 2 * ROT_HALF, 2, dtype=f32) / (2 * ROT_HALF))
    ang = pos[:, None] * inv_freq[None, :]
    cos, sin = jnp.cos(ang), jnp.sin(ang)
    zeros = jnp.zeros((SEQ, HEAD_DIM - 2 * ROT_HALF), f32)
    cos_t = jnp.concatenate([cos, cos, zeros + 1.0], axis=1)
    sin_t = jnp.concatenate([sin, sin, zeros], axis=1)
    j = jnp.arange(HEAD_DIM)[:, None]
    i = jnp.arange(HEAD_DIM)[None, :]
    rot = jnp.where((i < ROT_HALF) & (j == i + ROT_HALF), -1.0, 0.0) + jnp.where(
        (i >= ROT_HALF) & (i < 2 * ROT_HALF) & (j == i - ROT_HALF), 1.0, 0.0)
    return cos_t, sin_t, rot.astype(bf16), rot.T.astype(bf16)


def _exact_dot(t, m):
    hi = t.astype(bf16)
    lo = (t - hi.astype(f32)).astype(bf16)
    return _dot(hi, m) + _dot(lo, m)


def _rope(t, cos_t, sin_t, rot):
    return t * cos_t + _exact_dot(t, rot) * sin_t


def _rope_transposed(d, cos_t, sin_t, rot_t):
    return d * cos_t + _exact_dot(d * sin_t, rot_t)


ROW_CHUNK = 256


def _chunks(fn):
    def step(i, carry):
        fn(pl.multiple_of(i * ROW_CHUNK, ROW_CHUNK))
        return carry

    lax.fori_loop(0, SEQ // ROW_CHUNK, step, 0, unroll=2)


def _pieces(dilation):
    length = SEQ // dilation
    n = min(length, ROW_CHUNK)
    return [(r, l0, n) for r in range(dilation) for l0 in range(0, length, n)]


def _by_residue(dst_ref, src_ref, dilation, dtype):
    length = SEQ // dilation
    for r, l0, n in _pieces(dilation):
        src = src_ref[l0:l0 + n, :] if dilation == 1 else src_ref[pl.ds(r + dilation * l0, n, stride=dilation), :]
        start = r * length + l0
        dst_ref[start:start + n, :] = src.astype(dtype)


def _by_position(dst_ref, src_ref, dilation):
    length = SEQ // dilation
    for r, l0, n in _pieces(dilation):
        src = src_ref[r * length + l0:r * length + l0 + n, :]
        if dilation == 1:
            dst_ref[l0:l0 + n, :] = src
        else:
            dst_ref[pl.ds(r + dilation * l0, n, stride=dilation), :] = src


def _attn_masks():
    qi = lax.broadcasted_iota(jnp.int32, (SPAN, 2 * SPAN), 0)
    kj = lax.broadcasted_iota(jnp.int32, (SPAN, 2 * SPAN), 1)
    window = ((kj < SPAN) & (kj >= qi)) | ((kj >= SPAN) & (kj - SPAN <= qi))
    own = lax.broadcasted_iota(jnp.int32, (SPAN, SPAN), 1) <= lax.broadcasted_iota(jnp.int32, (SPAN, SPAN), 0)
    return window, own


def _attn_blocks(dilation):
    per_residue = SEQ // dilation // SPAN
    blocks = [(c, c % per_residue != 0) for c in range(SEQ // SPAN)]
    return [blocks[i:i + 4] for i in range(0, len(blocks), 4)]


def _block_keys(c, has_prev):
    return slice((c - 1) * SPAN if has_prev else c * SPAN, (c + 1) * SPAN)


def _head_spec(col):
    return pl.BlockSpec((SEQ, HEAD_DIM), lambda h: (0, col + h))


def _table_spec():
    return pl.BlockSpec((SEQ, HEAD_DIM), lambda h: (0, 0))


def _attn_fwd(z0, tables):
    scale = HEAD_DIM ** -0.5

    def body(*refs):
        qkv = refs[0:9]
        bg_ref, cos_ref, sin_ref, rot_ref = refs[9:13]
        yb_ref, att_ref, lse_ref = refs[13:16]
        saved = refs[16:25]
        tmp_q, tmp_k, v_ones, o_res, l_res, o_nat, l_nat = refs[25:32]
        window_mask, own_mask = _attn_masks()
        rot = rot_ref[...]

        @pl.when(pl.program_id(0) == 0)
        def _():
            v_ones[:, HEAD_DIM:] = jnp.ones((SEQ, HEAD_DIM), bf16)

        for g, dilation in enumerate(DILATIONS):
            q_ref, k_ref, v_ref = qkv[3 * g:3 * g + 3]
            qd, kd, vd = saved[3 * g:3 * g + 3]

            def rope_rows(start, q_ref=q_ref, k_ref=k_ref):
                r = pl.ds(start, ROW_CHUNK)
                cos_t, sin_t = cos_ref[r, :], sin_ref[r, :]
                tmp_q[r, :] = _rope(q_ref[r, :], cos_t, sin_t, rot) * scale
                tmp_k[r, :] = _rope(k_ref[r, :], cos_t, sin_t, rot)

            _chunks(rope_rows)
            _by_residue(qd, tmp_q, dilation, bf16)
            _by_residue(kd, tmp_k, dilation, bf16)
            _by_residue(vd, v_ref, dilation, bf16)
            for l0 in range(0, SEQ, ROW_CHUNK):
                v_ones[l0:l0 + ROW_CHUNK, 0:HEAD_DIM] = vd[l0:l0 + ROW_CHUNK, :]

            for four in _attn_blocks(dilation):
                scores = [_dot_nt(qd[c * SPAN:(c + 1) * SPAN, :], kd[_block_keys(c, prev), :]) for c, prev in four]
                tops, probs = [], []
                for (c, prev), s in zip(four, scores):
                    s = jnp.where(window_mask if prev else own_mask, s, NEG)
                    tops.append(jnp.max(s, axis=1, keepdims=True))
                    probs.append(jnp.exp(s - tops[-1]).astype(bf16))
                sums = [_dot(p, v_ones[_block_keys(c, prev), :]) for (c, prev), p in zip(four, probs)]
                for (c, prev), m, o in zip(four, tops, sums):
                    den = o[:, HEAD_DIM:]
                    o_res[c * SPAN:(c + 1) * SPAN, :] = o[:, :HEAD_DIM] / den
                    l_res[c * SPAN:(c + 1) * SPAN, :] = m + jnp.log(den)

            if dilation > 1:
                _by_position(o_nat, o_res, dilation)
                _by_position(l_nat, l_res, dilation)
            o_g, l_g = (o_res, l_res) if dilation == 1 else (o_nat, l_nat)

            def merge(start, g=g, o_g=o_g, l_g=l_g):
                r = pl.ds(start, ROW_CHUNK)
                if g == 0:
                    att, total = o_g[r, :], l_g[r, :]
                else:
                    l_old, l_new = lse_ref[r, :], l_g[r, :]
                    top = jnp.maximum(l_old, l_new)
                    total = top + jnp.log(jnp.exp(l_old - top) + jnp.exp(l_new - top))
                    att = att_ref[r, :] * jnp.exp(l_old - total) + o_g[r, :] * jnp.exp(l_new - total)
                att_ref[r, :] = att
                lse_ref[r, :] = total
                if g == len(DILATIONS) - 1:
                    gate = bg_ref[r, :]
                    yb_ref[r, :] = (att * gate * _sigmoid(gate)).astype(bf16)

            _chunks(merge)

    in_specs = []
    for g in range(3):
        in_specs += [_head_spec(Q_COL + 8 * g), _head_spec(K_COL + 8 * g), _head_spec(V_COL + 8 * g)]
    in_specs += [_head_spec(BGATE_COL), _table_spec(), _table_spec(), pl.BlockSpec((HEAD_DIM, HEAD_DIM), lambda h: (0, 0))]
    out_spec = pl.BlockSpec((SEQ, HEAD_DIM), lambda h: (0, h))
    vm = lambda dt: pltpu.VMEM((SEQ, HEAD_DIM), dt)
    cos_t, sin_t, rot, _ = tables
    out = pl.pallas_call(
        body, name="attn_fwd", grid=(N_HEADS,), in_specs=in_specs, out_specs=[out_spec] * 12,
        out_shape=[SDS((SEQ, HALF), bf16), SDS((SEQ, HALF), f32), SDS((SEQ, HALF), f32)] + [SDS((SEQ, HALF), bf16)] * 9,
        scratch_shapes=[vm(f32), vm(f32), pltpu.VMEM((SEQ, 2 * HEAD_DIM), bf16), vm(f32), vm(f32), vm(f32), vm(f32)],
    )(*([z0] * 10), cos_t, sin_t, rot)
    return out[0], out[1], out[2], [tuple(out[3 + 3 * g:6 + 3 * g]) for g in range(3)]


def _attn_bwd_group(g, saved, z0, att, lse, dcat, tables):
    scale = HEAD_DIM ** -0.5
    dilation = DILATIONS[g]
    with_gate = g == 0

    def body(*refs):
        qd, kd, vd, bg_ref, att_ref, lse_ref, dyb_ref, cos_ref, sin_ref, rot_t_ref = refs[0:10]
        n_out = 4 if with_gate else 3
        dq_ref, dk_ref, dv_ref = refs[10:13]
        dod, ld, dd, tmp, aq, ak, av = refs[10 + n_out:17 + n_out]
        window_mask, own_mask = _attn_masks()
        rot_t = rot_t_ref[...]

        def gate_rows(start):
            r = pl.ds(start, ROW_CHUNK)
            silu, dsilu = _silu_and_grad(bg_ref[r, :])
            att_v = att_ref[r, :]
            dyb = dyb_ref[r, :]
            if with_gate:
                refs[13][r, :] = (dyb * att_v * dsilu).astype(bf16)
            datt = dyb * silu
            tmp[r, :] = datt
            aq[r, :] = jnp.broadcast_to(jnp.sum(datt * att_v, axis=1, keepdims=True), (ROW_CHUNK, HEAD_DIM))

        _chunks(gate_rows)
        _by_residue(dod, tmp, dilation, bf16)
        _by_residue(dd, aq, dilation, f32)
        _by_residue(ld, lse_ref, dilation, f32)

        for four in _attn_blocks(dilation):
            rows = [slice(c * SPAN, (c + 1) * SPAN) for c, _ in four]
            keys = [_block_keys(c, prev) for c, prev in four]
            scores = [_dot_nt(qd[r, :], kd[k, :]) for r, k in zip(rows, keys)]
            dprobs = [_dot_nt(dod[r, :], vd[k, :]) for r, k in zip(rows, keys)]
            probs, dscores = [], []
            for (c, prev), r, s, dp in zip(four, rows, scores, dprobs):
                lse_q, delta = ld[r, :], dd[r, :]
                if prev:
                    lse_q = jnp.concatenate([lse_q, lse_q], axis=1)
                    delta = jnp.concatenate([delta, delta], axis=1)
                p = jnp.where(window_mask if prev else own_mask, jnp.exp(s - lse_q), 0.0)
                probs.append(p.astype(bf16))
                dscores.append((p * (dp - delta)).astype(bf16))
            dvs = [_dot_tn(p, dod[r, :]) for p, r in zip(probs, rows)]
            dks = [_dot_tn(ds, qd[r, :]) for ds, r in zip(dscores, rows)]
            dqs = [_dot(ds, kd[k, :]) for ds, k in zip(dscores, keys)]
            for (c, prev), r, dv, dk, dq in zip(four, rows, dvs, dks, dqs):
                aq[r, :] = dq
                if prev:
                    before = slice((c - 1) * SPAN, c * SPAN)
                    av[before, :] += dv[0:SPAN]
                    ak[before, :] += dk[0:SPAN]
                    av[r, :] = dv[SPAN:]
                    ak[r, :] = dk[SPAN:]
                else:
                    av[r, :] = dv
                    ak[r, :] = dk

        def finish(out_ref, acc, factor, roped):
            if dilation > 1:
                _by_position(tmp, acc, dilation)
            src = acc if dilation == 1 else tmp

            def rows(start):
                r = pl.ds(start, ROW_CHUNK)
                d = src[r, :]
                if factor != 1.0:
                    d = d * factor
                if roped:
                    d = _rope_transposed(d, cos_ref[r, :], sin_ref[r, :], rot_t)
                out_ref[r, :] = d.astype(bf16)

            _chunks(rows)

        finish(dq_ref, aq, scale, True)
        finish(dk_ref, ak, 1.0, True)
        finish(dv_ref, av, 1.0, False)

    head = pl.BlockSpec((SEQ, HEAD_DIM), lambda h: (0, h))
    in_specs = [head, head, head, _head_spec(BGATE_COL), head, head, _head_spec(8), _table_spec(), _table_spec(),
                pl.BlockSpec((HEAD_DIM, HEAD_DIM), lambda h: (0, 0))]
    n_out = 4 if with_gate else 3
    vm = lambda dt: pltpu.VMEM((SEQ, HEAD_DIM), dt)
    cos_t, sin_t, _, rot_t = tables
    return pl.pallas_call(
        body, name=f"attn_bwd_g{g}", grid=(N_HEADS,), in_specs=in_specs, out_specs=[head] * n_out,
        out_shape=[SDS((SEQ, HALF), bf16)] * n_out,
        scratch_shapes=[vm(bf16), vm(f32), vm(f32), vm(f32), vm(f32), vm(f32), vm(f32)],
    )(*saved, z0, att, lse, dcat, cos_t, sin_t, rot_t)


def _sgu_specs():
    chunk = lambda col: pl.BlockSpec((CHUNK, HALF), lambda n: (n, col))
    vec = pl.BlockSpec((1, HALF), lambda n: (0, 0))
    w = pl.BlockSpec((4, CHUNK, CHUNK), lambda n: (0, 0, 0))
    bias = pl.BlockSpec((CHUNK, CHUNK), lambda n: (0, 0))
    return chunk, vec, w, bias


def _sgu_weights(w_ref):
    tril = lax.broadcasted_iota(jnp.int32, (CHUNK, CHUNK), 1) <= lax.broadcasted_iota(jnp.int32, (CHUNK, CHUNK), 0)
    return tril, [jnp.where(tril, w_ref[h], 0.0).astype(bf16) for h in range(4)]


def _sgu_fwd(z1, ln_g, ln_b, sgu_w, bias_t):
    def body(u_ref, v_ref, cg_ref, g_ref, b_ref, w_ref, bias_ref, yc_ref):
        _, ws = _sgu_weights(w_ref)
        xh, _ = _ln_stats(v_ref[...])
        vn = (xh * g_ref[...] + b_ref[...]).astype(bf16)
        for h in range(4):
            cols = slice(h * POOL_CH, (h + 1) * POOL_CH)
            s = _dot(ws[h], vn[:, cols]) + bias_ref[:, h:h + 1]
            gate = cg_ref[:, cols]
            yc_ref[:, cols] = (u_ref[:, cols] * s * gate * _sigmoid(gate)).astype(bf16)

    chunk, vec, w, bias = _sgu_specs()
    return pl.pallas_call(
        body, name="sgu_fwd", grid=(SEQ // CHUNK,),
        in_specs=[chunk(0), chunk(1), chunk(2), vec, vec, w, bias], out_specs=chunk(0),
        out_shape=SDS((SEQ, HALF), bf16))(z1, z1, z1, ln_g, ln_b, sgu_w, bias_t)


def _sgu_bwd(z1, dcat, ln_g, ln_b, sgu_w, bias_t):
    def body(u_ref, v_ref, cg_ref, dyc_ref, g_ref, b_ref, w_ref, bias_ref,
             du_ref, dv_ref, dcg_ref, dw_ref, dbias_ref, dg_ref, db_ref, dvn_ref):
        first = pl.program_id(0) == 0
        tril, ws = _sgu_weights(w_ref)
        xh, rstd = _ln_stats(v_ref[...])
        g = g_ref[...]
        vn = (xh * g + b_ref[...]).astype(bf16)

        @pl.when(first)
        def _():
            dbias_ref[...] = jnp.zeros((CHUNK, CHUNK), f32)

        for h in range(4):
            cols = slice(h * POOL_CH, (h + 1) * POOL_CH)
            vn_h = vn[:, cols]
            s = _dot(ws[h], vn_h) + bias_ref[:, h:h + 1]
            silu, dsilu = _silu_and_grad(cg_ref[:, cols])
            dyc = dyc_ref[:, cols]
            u = u_ref[:, cols]
            du_ref[:, cols] = (dyc * s * silu).astype(bf16)
            dcg_ref[:, cols] = (dyc * u * s * dsilu).astype(bf16)
            ds = dyc * u * silu
            dbias_ref[:, h:h + 1] += jnp.sum(ds, axis=1, keepdims=True)
            ds = ds.astype(bf16)
            _accumulate(dw_ref.at[h], jnp.where(tril, _dot_nt(ds, vn_h), 0.0), first)
            dvn_ref[:, cols] = _dot_tn(ws[h], ds)
        dv, dg, db = _ln_bwd(xh, rstd, g, dvn_ref[...])
        dv_ref[...] = dv.astype(bf16)
        _accumulate(dg_ref, dg, first)
        _accumulate(db_ref, db, first)

    chunk, vec, w, bias = _sgu_specs()
    return pl.pallas_call(
        body, name="sgu_bwd", grid=(SEQ // CHUNK,),
        in_specs=[chunk(0), chunk(1), chunk(2), chunk(0), vec, vec, w, bias],
        out_specs=[chunk(0), chunk(0), chunk(0), w, bias, vec, vec],
        out_shape=[SDS((SEQ, HALF), bf16)] * 3 + [SDS((4, CHUNK, CHUNK), f32), SDS((CHUNK, CHUNK), f32),
                                                   SDS((1, HALF), f32), SDS((1, HALF), f32)],
        scratch_shapes=[pltpu.VMEM((CHUNK, HALF), f32)],
    )(z1, z1, z1, dcat, ln_g, ln_b, sgu_w, bias_t)


CONV_TILE = 128
DVAL_COL, DGLU_COL = 12, 16


def _conv_specs():
    val = pl.BlockSpec((SEQ, POOL_CH), lambda j: (0, DVAL_COL + j))
    glu = pl.BlockSpec((SEQ, POOL_CH), lambda j: (0, DGLU_COL + j))
    w = pl.BlockSpec((CONV_K, POOL_CH), lambda j: (0, j))
    col = pl.BlockSpec((SEQ, POOL_CH), lambda j: (0, j))
    vec = pl.BlockSpec((1, POOL_CH), lambda j: (0, j))
    return val, glu, w, col, vec


def _conv_fwd(z1, conv_w, conv_b):
    def body(val_ref, glu_ref, w_ref, b_ref, out_ref, xpad):
        xpad[0:CONV_PAD, :] = jnp.zeros((CONV_PAD, POOL_CH), f32)
        xpad[CONV_PAD:, :] = val_ref[...] * _sigmoid(glu_ref[...])
        w = w_ref[...]
        bias = b_ref[...]

        def tile(i, carry):
            t0 = pl.multiple_of(i * CONV_TILE, CONV_TILE)
            window = xpad[pl.ds(t0, CONV_TILE + CONV_PAD), :]
            acc = jnp.broadcast_to(bias, (CONV_TILE, POOL_CH))
            for k in range(CONV_K):
                shift = CONV_PAD - (CONV_K - 1) + k
                acc = acc + w[k:k + 1, :] * pltpu.roll(window, CONV_TILE + CONV_PAD - shift, 0)[0:CONV_TILE]
            out_ref[pl.ds(t0, CONV_TILE), :] = acc
            return carry

        lax.fori_loop(0, SEQ // CONV_TILE, tile, 0)

    val, glu, w, col, vec = _conv_specs()
    return pl.pallas_call(
        body, name="conv_fwd", grid=(4,), in_specs=[val, glu, w, vec], out_specs=col,
        out_shape=SDS((SEQ, HALF), f32), scratch_shapes=[pltpu.VMEM((SEQ + CONV_PAD, POOL_CH), f32)],
    )(z1, z1, conv_w, conv_b)


def _conv_bwd(z1, dconv, conv_w):
    def body(val_ref, glu_ref, w_ref, dout_ref, dval_ref, dglu_ref, dw_ref, db_ref, xpad, dpad, dx_ref):
        val = val_ref[...]
        sig = _sigmoid(glu_ref[...])
        xpad[0:CONV_PAD, :] = jnp.zeros((CONV_PAD, POOL_CH), f32)
        xpad[CONV_PAD:, :] = val * sig
        dout = dout_ref[...]
        dpad[0:SEQ, :] = dout
        dpad[SEQ:, :] = jnp.zeros((CONV_PAD, POOL_CH), f32)
        db_ref[...] = jnp.sum(dout, axis=0, keepdims=True)
        dw_ref[...] = jnp.zeros((CONV_K, POOL_CH), f32)
        w = w_ref[...]

        def tile(i, carry):
            t0 = pl.multiple_of(i * CONV_TILE, CONV_TILE)
            x_win = xpad[pl.ds(t0, CONV_TILE + CONV_PAD), :]
            d_win = dpad[pl.ds(t0, CONV_TILE + CONV_PAD), :]
            d_own = d_win[0:CONV_TILE]
            acc = jnp.zeros((CONV_TILE, POOL_CH), f32)
            for k in range(CONV_K):
                shift = CONV_PAD - (CONV_K - 1) + k
                x_k = pltpu.roll(x_win, CONV_TILE + CONV_PAD - shift, 0)[0:CONV_TILE]
                dw_ref[k:k + 1, :] += jnp.sum(d_own * x_k, axis=0, keepdims=True)
                back = CONV_K - 1 - k
                d_k = d_own if back == 0 else pltpu.roll(d_win, CONV_TILE + CONV_PAD - back, 0)[0:CONV_TILE]
                acc = acc + w[k:k + 1, :] * d_k
            dx_ref[pl.ds(t0, CONV_TILE), :] = acc
            return carry

        lax.fori_loop(0, SEQ // CONV_TILE, tile, 0)
        dx = dx_ref[...]
        dval_ref[...] = (dx * sig).astype(bf16)
        dglu_ref[...] = (dx * val * sig * (1.0 - sig)).astype(bf16)

    val, glu, w, col, vec = _conv_specs()
    pad = pltpu.VMEM((SEQ + CONV_PAD, POOL_CH), f32)
    return pl.pallas_call(
        body, name="conv_bwd", grid=(4,), in_specs=[val, glu, w, col], out_specs=[col, col, w, vec],
        out_shape=[SDS((SEQ, HALF), bf16), SDS((SEQ, HALF), bf16), SDS((CONV_K, HALF), f32), SDS((1, HALF), f32)],
        scratch_shapes=[pad, pad, pltpu.VMEM((SEQ, POOL_CH), f32)],
    )(z1, z1, conv_w, dconv)


DGATE_COL = 5


def _conv_norm_fwd(conv, z1, g, b):
    def body(c_ref, gate_ref, g_ref, b_ref, yd_ref):
        xh, _ = _ln_stats(c_ref[...])
        n = xh * g_ref[...] + b_ref[...]
        gate = gate_ref[...]
        yd_ref[...] = (n * _sigmoid(n) * gate * _sigmoid(gate)).astype(bf16)

    return pl.pallas_call(
        body, name="conv_norm_fwd", grid=(SEQ // ROWS,),
        in_specs=[_row_spec(HALF), _row_spec(HALF, DGATE_COL), _vec_spec(HALF), _vec_spec(HALF)],
        out_specs=_row_spec(HALF), out_shape=SDS((SEQ, HALF), bf16))(conv, z1, g, b)


def _conv_norm_bwd(conv, z1, dcat, g, b):
    def body(c_ref, gate_ref, dyd_ref, g_ref, b_ref, dconv_ref, dgate_ref, dg_ref, db_ref):
        first = pl.program_id(0) == 0
        xh, rstd = _ln_stats(c_ref[...])
        g = g_ref[...]
        n_silu, n_dsilu = _silu_and_grad(xh * g + b_ref[...])
        gate_silu, gate_dsilu = _silu_and_grad(gate_ref[...])
        dyd = dyd_ref[...]
        dgate_ref[...] = (dyd * n_silu * gate_dsilu).astype(bf16)
        dconv, dg, db = _ln_bwd(xh, rstd, g, dyd * gate_silu * n_dsilu)
        dconv_ref[...] = dconv
        _accumulate(dg_ref, dg, first)
        _accumulate(db_ref, db, first)

    return pl.pallas_call(
        body, name="conv_norm_bwd", grid=(SEQ // ROWS,),
        in_specs=[_row_spec(HALF), _row_spec(HALF, DGATE_COL), _row_spec(HALF, 1), _vec_spec(HALF), _vec_spec(HALF)],
        out_specs=[_row_spec(HALF), _row_spec(HALF), _vec_spec(HALF), _vec_spec(HALF)],
        out_shape=[SDS((SEQ, HALF), f32), SDS((SEQ, HALF), bf16), SDS((1, HALF), f32), SDS((1, HALF), f32)],
    )(conv, z1, dcat, g, b)


def _step(x, target, w, chip):
    chip_vec = chip.astype(jnp.int32).reshape(1)
    sharded_names = list(SHARDED_SMALL)
    small_shard = _pack([w[k] for k in sharded_names], total_rows=SMALL_SHARD_ROWS)
    small_slot = lax.dynamic_update_slice(jnp.zeros((N_CHIPS, SMALL_SHARD_ROWS, LANES), f32), small_shard[None], (chip, 0, 0))
    slots = [small_slot] + [_cast_into_slot(w["e_w_in"], chip_vec, f"cast_e_w_in{i}", i, E_IN_PIECES) for i in range(E_IN_PIECES)]
    slots += [_cast_into_slot(w[k], chip_vec, f"cast_{k}") for k in BIG[1:]]
    sems, bufs, token = _gather_start(slots)
    tables = _rope_tables()

    def vec(k):
        return w[k].reshape(1, -1)

    h0 = _pre_norm(x, vec("e_pre_norm") + token[0, 0])
    after, z0, e_w_in = h0, None, []
    for i in range(E_IN_PIECES):
        group = slice(0, 2) if i == 0 else slice(1 + i, 2 + i)
        landed = _forward_halves(_gather_wait(bufs[group], sems[group], after, f"gather_wait_{i}"), f"forward_{i}")
        if i == 0:
            small_full = landed[0]
        e_w_in.append(landed[-1])
        z0 = _mm_nn(h0, landed[-1], f32, f"e_in{i}", i, E_IN_PIECES, z0)
        after = z0
    p = {k: _from_chips(k, a) for k, a in zip(sharded_names, _unpack(small_full, [SHARDED_SMALL[k][0] for k in sharded_names]))}
    for k in ("o_pre_norm", "o_sgu_norm_g", "o_sgu_norm_b", "o_conv_b", "o_conv_norm_g", "o_conv_norm_b", "o_post_norm"):
        p[k] = p[k].reshape(1, -1)
    pool_w_bf = p["e_pool_w"].astype(bf16)
    bias_t = jnp.pad(w["o_sgu_b"].T, ((0, 0), (0, CHUNK - 4)))

    ya = _pool_fwd(z0, pool_w_bf, vec("e_pool_scale"))
    yb, att, lse, qkv_by_residue = _attn_fwd(z0, tables)

    def arrived(index, after, name):
        one = slice(index, index + 1)
        return _forward_halves(_gather_wait(bufs[one], sems[one], after, f"gather_wait_{name}"), f"forward_{name}")[0]

    e_w_out = arrived(1 + E_IN_PIECES, att, "e_w_out").reshape(1, D_MODEL, D_MODEL)
    cat0 = jnp.concatenate([ya, yb], axis=1)
    y0 = _mm_nn(cat0, e_w_out, f32, "e_out")
    x1, h1 = _mid_norm(x, y0, vec("e_post_norm"), p["o_pre_norm"])
    o_w_in = arrived(2 + E_IN_PIECES, h1, "o_w_in")
    z1 = _mm_nn(h1, o_w_in, f32, "o_in")
    yc = _sgu_fwd(z1, p["o_sgu_norm_g"], p["o_sgu_norm_b"], w["o_sgu_w"], bias_t)
    conv = _conv_fwd(z1, p["o_conv_w"], p["o_conv_b"])
    yd = _conv_norm_fwd(conv, z1, p["o_conv_norm_g"], p["o_conv_norm_b"])
    o_w_out = arrived(3 + E_IN_PIECES, yd, "o_w_out").reshape(1, D_MODEL, D_MODEL)
    cat1 = jnp.concatenate([yc, yd], axis=1)
    y1 = _mm_nn(cat1, o_w_out, f32, "o_out")
    loss, dx2, dy1, g_o_post = _final_norm_loss(x1, y1, p["o_post_norm"], target)

    in_flight = {}

    def send_off(name, grad):
        sem, sums, land, tok = _scatter_start(_swap_add(grad, f"swap_add_{name}"), f"scatter_start_{name}")
        in_flight[name] = (sem, sums, land)
        return tok

    tok = send_off("o_w_out", _mm_tn(cat1, dy1, 1, "o_out_dw").reshape(N_CHIPS, HALF // 2, D_MODEL))
    dcat1 = _mm_nt(dy1, [o_w_out], "o_out_dx", tok)
    du, dv, dcg, g_sgu_w, g_bias_t, g_sgu_g, g_sgu_b = _sgu_bwd(
        z1, dcat1, p["o_sgu_norm_g"] + tok[0, 0], p["o_sgu_norm_b"], w["o_sgu_w"], bias_t)
    dconv, ddgate, g_cn_g, g_cn_b = _conv_norm_bwd(conv, z1, dcat1, p["o_conv_norm_g"], p["o_conv_norm_b"])
    ddval, ddglu, g_conv_w, g_conv_b = _conv_bwd(z1, dconv, p["o_conv_w"])
    dz1 = jnp.concatenate([du, dv, dcg, ddval, ddglu, ddgate], axis=1)
    tok = send_off("o_w_in", _mm_tn(h1, dz1, N_CHIPS, "o_in_dw"))
    dh1 = _mm_nt(dz1, [o_w_in], "o_in_dx", tok)
    dx1, dy0, g_o_pre, g_e_post = _mid_norm_bwd(dx2, dh1, x1, y0, p["o_pre_norm"] + tok[0, 0], vec("e_post_norm"))

    tok = send_off("e_w_out", _mm_tn(cat0, dy0, 1, "e_out_dw").reshape(N_CHIPS, HALF // 2, D_MODEL))
    dcat0 = _mm_nt(dy0, [e_w_out], "e_out_dx", tok)
    da, dagate, g_pool_w, g_pool_scale = _pool_bwd(z0, dcat0, pool_w_bf, vec("e_pool_scale") + tok[0, 0])
    dq0, dk0, dv0, dbgate = _attn_bwd_group(0, qkv_by_residue[0], z0, att, lse, dcat0, tables)
    dq1, dk1, dv1 = _attn_bwd_group(1, qkv_by_residue[1], z0, att, lse, dcat0, tables)
    dq2, dk2, dv2 = _attn_bwd_group(2, qkv_by_residue[2], z0, att, lse, dcat0, tables)
    dz0 = jnp.concatenate([da, dagate, dq0, dq1, dq2, dk0, dk1, dk2, dv0, dv1, dv2, dbgate], axis=1)
    tok = send_off("e_w_in", _mm_tn(h0, dz0, N_CHIPS, "e_in_dw"))
    dh0 = _mm_nt(dz0, e_w_in, "e_in_dx", tok)
    grad_x, g_e_pre = _pre_norm_bwd(dx1, dh0, x, vec("e_pre_norm") + tok[0, 0])

    small = {"e_pre_norm": g_e_pre, "e_pool_w": g_pool_w, "e_pool_scale": g_pool_scale, "e_post_norm": g_e_post,
             "o_pre_norm": g_o_pre, "o_sgu_norm_g": g_sgu_g, "o_sgu_norm_b": g_sgu_b, "o_sgu_w": g_sgu_w,
             "o_sgu_b": g_bias_t[:, 0:4].T, "o_conv_w": g_conv_w, "o_conv_b": g_conv_b,
             "o_conv_norm_g": g_cn_g, "o_conv_norm_b": g_cn_b, "o_post_norm": g_o_post}
    return loss, grad_x, in_flight, small


def _land(in_flight, name, chip, after):
    sems, sums, land = in_flight[name]
    sums, land = _scatter_wait(sems, sums, land, after, f"scatter_wait_{name}")
    return _add_landed_join(sums, land, chip.astype(jnp.int32).reshape(1), f"add_landed_{name}")


def _place():
    x, y, c = lax.axis_index("x"), lax.axis_index("y"), lax.axis_index("c")
    others = [(1 - x, y), (x, 1 - y), (1 - x, 1 - y)]
    return x, y, c, 2 * x + y, others


def _all_gather(shards, name):
    n = len(shards)

    def body(*refs):
        ins, outs = refs[:n], refs[n:2 * n]
        send_sems, recv_sems, local_sems = refs[2 * n:]
        x, y, c, me, others = _place()
        sibling = (x, y, 1 - c)

        def half(a, chip, core):
            rows = ins[a].shape[0] // 2
            return outs[a].at[chip, pl.ds(core * rows, rows), :]

        def copy(a, k, src, dst, to):
            return pltpu.make_async_remote_copy(src_ref=src, dst_ref=dst, send_sem=send_sems.at[6 * a + k],
                                                recv_sem=recv_sems.at[6 * a + k], device_id=to, device_id_type=MESH)

        local = [pltpu.make_async_copy(ins[a], outs[a].at[me], local_sems.at[a]) for a in range(n)]
        for cp in local:
            cp.start()
        sent = []
        for a in range(n):
            rows = ins[a].shape[0] // 2
            mine = ins[a].at[pl.ds(c * rows, rows), :]
            for k, (ox, oy) in enumerate(others):
                sent.append(copy(a, k, mine, half(a, me, c), (ox, oy, c)))
                sent[-1].start()
        for a in range(n):
            for k, (ox, oy) in enumerate(others):
                landed = half(a, 2 * ox + oy, c)
                copy(a, k, landed, landed, (ox, oy, c)).wait_recv()
                sent.append(copy(a, 3 + k, landed, landed, sibling))
                sent[-1].start()
        for k, (ox, oy) in enumerate(others):
            chip = 2 * ox + oy
            for a in range(n):
                theirs = half(a, chip, 1 - c)
                copy(a, 3 + k, theirs, theirs, sibling).wait_recv()
        for cp in sent:
            cp.wait_send()
        for cp in local:
            cp.wait()

    return pl.pallas_call(
        body, name=name, in_specs=[ANY] * n, out_specs=[ANY] * n,
        out_shape=[SDS((N_CHIPS,) + s.shape, s.dtype) for s in shards],
        scratch_shapes=[pltpu.SemaphoreType.DMA((6 * n,)), pltpu.SemaphoreType.DMA((6 * n,)), pltpu.SemaphoreType.DMA((n,))],
    )(*shards)


def _swap_halves(parts, name):
    n = len(parts)

    def body(*refs):
        ins, own, theirs = refs[:n], refs[n:2 * n], refs[2 * n:3 * n]
        send_sems, recv_sems, local_sems = refs[3 * n:]
        x, y, c, _, _ = _place()
        sibling = (x, y, 1 - c)
        copies = []
        for a in range(n):
            rows = ins[a].shape[1] // 2
            keep = pltpu.make_async_copy(ins[a].at[:, pl.ds(c * rows, rows), :], own[a], local_sems.at[a])
            give = pltpu.make_async_remote_copy(
                src_ref=ins[a].at[:, pl.ds((1 - c) * rows, rows), :], dst_ref=theirs[a], send_sem=send_sems.at[a],
                recv_sem=recv_sems.at[a], device_id=sibling, device_id_type=MESH)
            keep.start()
            give.start()
            copies += [keep, give]
        for cp in copies:
            cp.wait()

    half = [SDS((N_CHIPS, s.shape[1] // 2, s.shape[2]), s.dtype) for s in parts]
    out = pl.pallas_call(
        body, name=name, in_specs=[ANY] * n, out_specs=[ANY] * (2 * n), out_shape=half + half,
        scratch_shapes=[pltpu.SemaphoreType.DMA((n,)), pltpu.SemaphoreType.DMA((n,)), pltpu.SemaphoreType.DMA((n,))],
    )(*parts)
    return out[:n], out[n:]


def _scatter_chips(parts, name):
    n = len(parts)

    def body(*refs):
        ins, outs = refs[:n], refs[n:2 * n]
        send_sems, recv_sems, local_sems = refs[2 * n:]
        x, y, c, me, others = _place()

        def copy(a, k, slot_from, slot_to, chip_xy):
            return pltpu.make_async_remote_copy(
                src_ref=ins[a].at[slot_from], dst_ref=outs[a].at[slot_to], send_sem=send_sems.at[3 * a + k],
                recv_sem=recv_sems.at[3 * a + k], device_id=(chip_xy[0], chip_xy[1], c), device_id_type=MESH)

        keeps, gives = [], []
        for a in range(n):
            keeps.append(pltpu.make_async_copy(ins[a].at[me], outs[a].at[me], local_sems.at[a]))
            keeps[-1].start()
            for k, (ox, oy) in enumerate(others):
                gives.append(copy(a, k, 2 * ox + oy, me, (ox, oy)))
                gives[-1].start()
        for a in range(n):
            for k, (ox, oy) in enumerate(others):
                copy(a, k, me, 2 * ox + oy, (ox, oy)).wait_recv()
        for cp in gives:
            cp.wait_send()
        for cp in keeps:
            cp.wait()

    return pl.pallas_call(
        body, name=name, in_specs=[ANY] * n, out_specs=[ANY] * n, out_shape=[SDS(s.shape, s.dtype) for s in parts],
        scratch_shapes=[pltpu.SemaphoreType.DMA((3 * n,)), pltpu.SemaphoreType.DMA((3 * n,)), pltpu.SemaphoreType.DMA((n,))],
    )(*parts)


def _join_halves(halves, name):
    n = len(halves)

    def body(*refs):
        ins, outs = refs[:n], refs[n:2 * n]
        send_sems, recv_sems, local_sems = refs[2 * n:]
        x, y, c, _, _ = _place()

        def copy(a, core):
            rows = ins[a].shape[0]
            return pltpu.make_async_remote_copy(
                src_ref=ins[a], dst_ref=outs[a].at[pl.ds(core * rows, rows), :], send_sem=send_sems.at[a],
                recv_sem=recv_sems.at[a], device_id=(x, y, 1 - c), device_id_type=MESH)

        keeps, gives = [], []
        for a in range(n):
            rows = ins[a].shape[0]
            keeps.append(pltpu.make_async_copy(ins[a], outs[a].at[pl.ds(c * rows, rows), :], local_sems.at[a]))
            gives.append(copy(a, c))
            keeps[-1].start()
            gives[-1].start()
        for a in range(n):
            copy(a, 1 - c).wait_recv()
        for cp in gives:
            cp.wait_send()
        for cp in keeps:
            cp.wait()

    return pl.pallas_call(
        body, name=name, in_specs=[ANY] * n, out_specs=[ANY] * n,
        out_shape=[SDS((2 * s.shape[0], s.shape[1]), s.dtype) for s in halves],
        scratch_shapes=[pltpu.SemaphoreType.DMA((n,)), pltpu.SemaphoreType.DMA((n,)), pltpu.SemaphoreType.DMA((n,))],
    )(*halves)


def _add_pair(a, b, name):
    _, r, c = a.shape
    tr = 256 if r % 256 == 0 else r // 2 if r > 512 else r

    def body(a_ref, b_ref, o_ref):
        o_ref[...] = (a_ref[...].astype(f32) + b_ref[...].astype(f32)).astype(o_ref.dtype)

    spec = pl.BlockSpec((None, tr, c), lambda j, i: (j, i, 0))
    return pl.pallas_call(body, name=name, grid=(N_CHIPS, r // tr), in_specs=[spec, spec], out_specs=spec,
                          out_shape=SDS(a.shape, a.dtype))(a, b)


def _add_chips(u, name):
    _, r, c = u.shape
    tr = 256 if r % 256 == 0 else r

    def body(u_ref, o_ref):
        o_ref[...] = ((u_ref[0].astype(f32) + u_ref[1].astype(f32)) + u_ref[2].astype(f32)) + u_ref[3].astype(f32)

    return pl.pallas_call(
        body, name=name, grid=(r // tr,), in_specs=[pl.BlockSpec((N_CHIPS, tr, c), lambda i: (0, i, 0))],
        out_specs=pl.BlockSpec((tr, c), lambda i: (i, 0)), out_shape=SDS((r, c), f32))(u)


SWAP_ROWS = 256
FORWARD_STAGE_BYTES = 4 << 20


def _swap_add(g, name):
    chips, r, c = g.shape
    half = r // 2
    rows_per_step = 2 * SWAP_ROWS if half % (2 * SWAP_ROWS) == 0 else SWAP_ROWS
    nb = half // rows_per_step
    steps = chips * nb

    def body(core_ref, mine_ref, theirs_ref, out_ref, landing, send_sems, recv_sems, free_sems):
        i = pl.program_id(0)
        x, y, core, _, _ = _place()
        sibling = (x, y, 1 - core)

        def send(slot):
            return pltpu.make_async_remote_copy(src_ref=theirs_ref, dst_ref=landing.at[slot], send_sem=send_sems.at[slot],
                                                recv_sem=recv_sems.at[slot], device_id=sibling, device_id_type=MESH)

        @pl.when(i < steps)
        def _():
            @pl.when(i >= 2)
            def _():
                pl.semaphore_wait(free_sems.at[i % 2], 1)

            send(i % 2).start()

        @pl.when(i >= 1)
        def _():
            landed = (i - 1) % 2
            send(landed).wait_recv()
            out_ref[...] = (mine_ref[...].astype(f32) + landing[landed].astype(f32)).astype(out_ref.dtype)

            @pl.when(i + 1 < steps)
            def _():
                pl.semaphore_signal(free_sems.at[landed], 1, device_id=sibling, device_id_type=MESH)

        @pl.when(i < steps)
        def _():
            send(i % 2).wait_send()

    def rows_of(b, h):
        return (2 * (b // nb) + h) * nb + b % nb

    block = (rows_per_step, c)
    grid_spec = pltpu.PrefetchScalarGridSpec(
        num_scalar_prefetch=1, grid=(steps + 1,),
        in_specs=[pl.BlockSpec(block, lambda i, core: (rows_of(jnp.maximum(i - 1, 0), core[0]), 0)),
                  pl.BlockSpec(block, lambda i, core: (rows_of(jnp.minimum(i, steps - 1), 1 - core[0]), 0))],
        out_specs=pl.BlockSpec(block, lambda i, core: (jnp.maximum(i - 1, 0), 0)),
        scratch_shapes=[pltpu.VMEM((2, rows_per_step, c), g.dtype), pltpu.SemaphoreType.DMA((2,)),
                        pltpu.SemaphoreType.DMA((2,)), pltpu.SemaphoreType.REGULAR((2,))])
    core = lax.axis_index("c").astype(jnp.int32).reshape(1)
    rows = g.reshape(chips * r, c)
    out = pl.pallas_call(body, name=name, grid_spec=grid_spec, out_shape=SDS((chips * half, c), g.dtype))(core, rows, rows)
    return out.reshape(chips, half, c)


def _reduce_scatter(parts, tag):
    own, theirs = _swap_halves(parts, f"swap_halves_{tag}")
    chip_sums = [_add_pair(o, t, f"add_cores_{tag}{i}") for i, (o, t) in enumerate(zip(own, theirs))]
    gathered = _scatter_chips(chip_sums, f"scatter_chips_{tag}")
    halves = [_add_chips(u, f"add_chips_{tag}{i}") for i, u in enumerate(gathered)]
    return _join_halves(halves, f"join_halves_{tag}")


HBM = pl.BlockSpec(memory_space=pltpu.HBM)
SEM = pl.BlockSpec(memory_space=pltpu.SEMAPHORE)
EFFECT = pltpu.SideEffectType.DATAFLOW_SIDE_EFFECTING


def _in_hbm(a):
    return pltpu.with_memory_space_constraint(a, pltpu.HBM)


def _cast_into_slot(w, chip, name, piece=0, pieces=1):
    r, c = w.shape
    c = c // pieces
    nb = r // SWAP_ROWS

    def body(chip_ref, w_ref, o_ref):
        o_ref[...] = w_ref[...].astype(bf16)

    grid_spec = pltpu.PrefetchScalarGridSpec(
        num_scalar_prefetch=1, grid=(nb,),
        in_specs=[pl.BlockSpec((SWAP_ROWS, c), lambda i, chip: (i, piece))],
        out_specs=pl.BlockSpec((SWAP_ROWS, c), lambda i, chip: (chip[0] * nb + i, 0)))
    out = pl.pallas_call(body, name=name, grid_spec=grid_spec, out_shape=SDS((N_CHIPS * r, c), bf16))(chip, w)
    return out.reshape(N_CHIPS, r, c)


def _gather_start(bufs):
    n = len(bufs)

    def body(*refs):
        ins, sems, token = refs[:n], refs[n:3 * n], refs[4 * n]
        x, y, c, me, others = _place()
        for a in range(n):
            rows = ins[a].shape[1] // 2
            mine = ins[a].at[me, pl.ds(c * rows, rows), :]
            for k, (ox, oy) in enumerate(others):
                pltpu.make_async_remote_copy(src_ref=mine, dst_ref=mine, send_sem=sems[2 * a].at[k],
                                             recv_sem=sems[2 * a + 1].at[k], device_id=(ox, oy, c),
                                             device_id_type=MESH).start()
        token[...] = jnp.zeros_like(token)

    out = pl.pallas_call(
        body, name="gather_start", in_specs=[HBM] * n,
        out_shape=(*[pltpu.SemaphoreType.DMA((3,))] * (2 * n), *[pltpu.HBM(b.shape, b.dtype) for b in bufs],
                   SDS((8, 128), f32)),
        out_specs=(*[SEM] * (2 * n), *[HBM] * n, pl.BlockSpec(memory_space=pltpu.VMEM)),
        input_output_aliases={a: 2 * n + a for a in range(n)},
        compiler_params=pltpu.CompilerParams(has_side_effects=EFFECT),
    )(*[_in_hbm(b) for b in bufs])
    return [(out[2 * a], out[2 * a + 1]) for a in range(n)], list(out[2 * n:3 * n]), out[3 * n]


def _gather_wait(bufs, sems, after, name):
    n = len(bufs)

    def body(*refs):
        ins, sem_refs = refs[:n], refs[n:3 * n]
        x, y, c, me, others = _place()
        for a in range(n):
            rows = ins[a].shape[1] // 2
            mine = ins[a].at[me, pl.ds(c * rows, rows), :]
            for k, (ox, oy) in enumerate(others):
                landed = ins[a].at[2 * ox + oy, pl.ds(c * rows, rows), :]
                copy = pltpu.make_async_remote_copy(src_ref=mine, dst_ref=landed, send_sem=sem_refs[2 * a].at[k],
                                                    recv_sem=sem_refs[2 * a + 1].at[k], device_id=(ox, oy, c),
                                                    device_id_type=MESH)
                copy.wait_send()
                copy.wait_recv()

    flat_sems = [s for pair in sems for s in pair]
    out = pl.pallas_call(
        body, name=name, in_specs=[HBM] * n + [SEM] * (2 * n) + [ANY],
        out_shape=tuple(pltpu.HBM(b.shape, b.dtype) for b in bufs), out_specs=tuple([HBM] * n),
        input_output_aliases={a: a for a in range(n)},
        compiler_params=pltpu.CompilerParams(has_side_effects=EFFECT),
    )(*bufs, *flat_sems, after)
    return list(out)


def _forward_halves(bufs, name):
    n = len(bufs)
    blocks = []
    for b in bufs:
        half = b.shape[1] // 2
        whole = half * b.shape[2] * b.dtype.itemsize <= FORWARD_STAGE_BYTES
        blocks.append((half, half if whole or half % SWAP_ROWS else SWAP_ROWS))
    work = [(a, k, b) for a in range(n) for k in range(3) for b in range(blocks[a][0] // blocks[a][1])]

    def body(*refs):
        outs, stages = refs[n:2 * n], refs[2 * n:3 * n]
        load_sems, send_sems, recv_sems = refs[3 * n:]
        x, y, c, me, others = _place()
        sibling = (x, y, 1 - c)

        def rows(item):
            a, k, b = item
            half, tr = blocks[a]
            ox, oy = others[k]
            return outs[a].at[2 * ox + oy, pl.ds(c * half + b * tr, tr), :]

        def load(s, item):
            return pltpu.make_async_copy(rows(item), stages[item[0]].at[s], load_sems.at[s])

        def send(s, item):
            return pltpu.make_async_remote_copy(src_ref=stages[item[0]].at[s], dst_ref=rows(item), send_sem=send_sems.at[s],
                                                recv_sem=recv_sems.at[item[0]], device_id=sibling, device_id_type=MESH)

        load(0, work[0]).start()
        for t, item in enumerate(work):
            s = t % 2
            load(s, item).wait()
            send(s, item).start()
            if t + 1 < len(work):
                if t >= 1:
                    send(1 - s, work[t - 1]).wait_send()
                load(1 - s, work[t + 1]).start()
        if len(work) > 1:
            send(len(work) % 2, work[-2]).wait_send()
        send((len(work) - 1) % 2, work[-1]).wait_send()
        for a in range(n):
            theirs = outs[a].at[pl.ds(0, 3), pl.ds(0, blocks[a][0]), :]
            pltpu.make_async_remote_copy(src_ref=theirs, dst_ref=theirs, send_sem=send_sems.at[0], recv_sem=recv_sems.at[a],
                                         device_id=sibling, device_id_type=MESH).wait_recv()

    out = pl.pallas_call(
        body, name=name, in_specs=[ANY] * n, out_specs=[ANY] * n, out_shape=[SDS(b.shape, b.dtype) for b in bufs],
        input_output_aliases={a: a for a in range(n)},
        scratch_shapes=[pltpu.VMEM((2, blocks[a][1], bufs[a].shape[2]), bufs[a].dtype) for a in range(n)]
        + [pltpu.SemaphoreType.DMA((2,)), pltpu.SemaphoreType.DMA((2,)), pltpu.SemaphoreType.DMA((n,))],
    )(*bufs)
    return list(out)


def _scatter_start(chip_sums, name):
    def body(a_ref, land_ref, send_sems, recv_sems, a_thru, land_thru, token):
        x, y, c, me, others = _place()
        for k, (ox, oy) in enumerate(others):
            pltpu.make_async_remote_copy(src_ref=a_ref.at[2 * ox + oy], dst_ref=land_ref.at[me], send_sem=send_sems.at[k],
                                         recv_sem=recv_sems.at[k], device_id=(ox, oy, c), device_id_type=MESH).start()
        token[...] = jnp.zeros_like(token)

    shape = pltpu.HBM(chip_sums.shape, chip_sums.dtype)
    send, recv, a_thru, land, token = pl.pallas_call(
        body, name=name, in_specs=[HBM, HBM],
        out_shape=(pltpu.SemaphoreType.DMA((3,)), pltpu.SemaphoreType.DMA((3,)), shape, shape, SDS((8, 128), f32)),
        out_specs=(SEM, SEM, HBM, HBM, pl.BlockSpec(memory_space=pltpu.VMEM)), input_output_aliases={0: 2, 1: 3},
        compiler_params=pltpu.CompilerParams(has_side_effects=EFFECT),
    )(_in_hbm(chip_sums), _in_hbm(lax.empty(chip_sums.shape, chip_sums.dtype)))
    return (send, recv), a_thru, land, token


def _scatter_wait(sems, chip_sums, land, after, name):
    def body(a_ref, land_ref, send_sems, recv_sems, after_ref, a_out, land_out):
        x, y, c, me, others = _place()
        for k, (ox, oy) in enumerate(others):
            copy = pltpu.make_async_remote_copy(
                src_ref=a_ref.at[2 * ox + oy], dst_ref=land_ref.at[2 * ox + oy], send_sem=send_sems.at[k],
                recv_sem=recv_sems.at[k], device_id=(ox, oy, c), device_id_type=MESH)
            copy.wait_send()
            copy.wait_recv()

    shape = pltpu.HBM(chip_sums.shape, chip_sums.dtype)
    return pl.pallas_call(
        body, name=name, in_specs=[HBM, HBM, SEM, SEM, ANY], out_shape=(shape, shape), out_specs=(HBM, HBM),
        input_output_aliases={0: 0, 1: 1}, compiler_params=pltpu.CompilerParams(has_side_effects=EFFECT),
    )(chip_sums, land, sems[0], sems[1], after)


def _add_landed_join(chip_sums, land, chip, name):
    chips, rh, c = chip_sums.shape
    nb = rh // SWAP_ROWS

    def body(chip_ref, own_ref, l1_ref, l2_ref, l3_ref, out_hbm, buf, send_sems, recv_sem, local_sems):
        i = pl.program_id(0)
        slot = i % 2
        x, y, core, _, _ = _place()
        sibling = (x, y, 1 - core)

        def copies(s, step):
            rows = pl.ds(pl.multiple_of((core * nb + step) * SWAP_ROWS, SWAP_ROWS), SWAP_ROWS)
            keep = pltpu.make_async_copy(buf.at[s], out_hbm.at[rows, :], local_sems.at[s])
            give = pltpu.make_async_remote_copy(src_ref=buf.at[s], dst_ref=out_hbm.at[rows, :], send_sem=send_sems.at[s],
                                                recv_sem=recv_sem.at[0], device_id=sibling, device_id_type=MESH)
            return keep, give

        def drain(s, step):
            keep, give = copies(s, step)
            keep.wait()
            give.wait_send()

        @pl.when(i >= 2)
        def _():
            drain(slot, i - 2)

        buf[slot] = ((own_ref[...].astype(f32) + l1_ref[...].astype(f32)) + l2_ref[...].astype(f32)) + l3_ref[...].astype(f32)
        keep, give = copies(slot, i)
        keep.start()
        give.start()

        @pl.when(i == nb - 1)
        def _():
            drain(slot, i)
            if nb > 1:
                drain(1 - slot, i - 1)
            theirs = out_hbm.at[pl.ds((1 - core) * rh, rh), :]
            pltpu.make_async_remote_copy(src_ref=theirs, dst_ref=theirs, send_sem=send_sems.at[0], recv_sem=recv_sem.at[0],
                                         device_id=sibling, device_id_type=MESH).wait_recv()

    block = (SWAP_ROWS, c)
    from_slot = lambda d: pl.BlockSpec(block, lambda i, chip: (((chip[0] + d) % chips) * nb + i, 0))
    grid_spec = pltpu.PrefetchScalarGridSpec(
        num_scalar_prefetch=1, grid=(nb,), in_specs=[from_slot(0), from_slot(1), from_slot(2), from_slot(3)],
        out_specs=ANY,
        scratch_shapes=[pltpu.VMEM((2, SWAP_ROWS, c), f32), pltpu.SemaphoreType.DMA((2,)),
                        pltpu.SemaphoreType.DMA((1,)), pltpu.SemaphoreType.DMA((2,))])
    land_rows = land.reshape(chips * rh, c)
    return pl.pallas_call(body, name=name, grid_spec=grid_spec, out_shape=SDS((2 * rh, c), f32))(
        chip, chip_sums.reshape(chips * rh, c), land_rows, land_rows, land_rows)


def _adamw_update(w_ref, g_ref, m_ref, v_ref, d_ref, nm_ref, nv_ref):
    g = g_ref[...]
    nm = ADAM_B1 * m_ref[...] + (1.0 - ADAM_B1) * g
    nv = ADAM_B2 * v_ref[...] + (1.0 - ADAM_B2) * (g * g)
    nm_ref[...] = nm
    nv_ref[...] = nv
    m_hat = nm / (1.0 - ADAM_B1 ** ADAM_STEP)
    v_hat = nv / (1.0 - ADAM_B2 ** ADAM_STEP)
    d_ref[...] = -ADAM_LR * (m_hat / (jnp.sqrt(v_hat) + ADAM_EPS) + ADAM_WD * w_ref[...])


def _adamw(w, g, m, v, name):
    r, c = w.shape
    tr = 128 if r % 128 == 0 else r

    def body(w_ref, g_ref, m_ref, v_ref, g_out_ref, d_ref, nm_ref, nv_ref):
        g_out_ref[...] = g_ref[...]
        _adamw_update(w_ref, g_ref, m_ref, v_ref, d_ref, nm_ref, nv_ref)

    spec = pl.BlockSpec((tr, c), lambda i: (i, 0))
    return pl.pallas_call(body, name=name, grid=(r // tr,), in_specs=[spec] * 4, out_specs=[spec] * 4,
                          out_shape=[SDS((r, c), f32)] * 4)(w, g, m, v)


def _adamw_small(ws, gs, ms, vs):
    n = len(ws)

    def body(*refs):
        for i in range(n):
            _adamw_update(*refs[i:7 * n:n])

    whole = pl.BlockSpec(memory_space=pltpu.VMEM)
    out = pl.pallas_call(body, name="adamw_small", in_specs=[whole] * (4 * n), out_specs=[whole] * (3 * n),
                         out_shape=[SDS(a.shape, f32) for a in ws] * 3)(*ws, *gs, *ms, *vs)
    return out[:n], out[n:2 * n], out[2 * n:]


def _pack(arrays, total_rows=None):
    parts = []
    rows = 0
    for a in arrays:
        flat = a.reshape(-1, LANES)
        pad = -flat.shape[0] % 8
        parts.append(jnp.pad(flat, ((0, pad), (0, 0))))
        rows += flat.shape[0] + pad
    if total_rows is not None:
        parts.append(jnp.zeros((total_rows - rows, LANES), arrays[0].dtype))
    return jnp.concatenate(parts, axis=0)


def _unpack(buf, shapes):
    out = []
    row = 0
    lead = buf.shape[:-2]
    for shape in shapes:
        size = 1
        for s in shape:
            size *= s
        rows = size // LANES
        out.append(buf[..., row:row + rows, :].reshape(lead + tuple(shape)))
        row += rows + (-rows % 8)
    return out


BIG = ("e_w_in", "e_w_out", "o_w_in", "o_w_out")
SHARDED_SMALL = {
    "e_pool_w": ((4, 64, 256), 1), "o_pre_norm": ((512,), 0), "o_sgu_norm_g": ((256,), 0), "o_sgu_norm_b": ((256,), 0),
    "o_conv_w": ((31, 256), 1), "o_conv_b": ((256,), 0), "o_conv_norm_g": ((256,), 0), "o_conv_norm_b": ((256,), 0),
    "o_post_norm": ((512,), 0),
}
REPLICATED_SMALL = {"e_pre_norm": (2048,), "e_pool_scale": (1024,), "e_post_norm": (2048,),
                    "o_sgu_w": (4, 128, 128), "o_sgu_b": (4, 128)}
SMALL_ORDER = ("e_pre_norm", "e_pool_w", "e_pool_scale", "e_post_norm", "o_pre_norm", "o_sgu_norm_g", "o_sgu_norm_b",
               "o_sgu_w", "o_sgu_b", "o_conv_w", "o_conv_b", "o_conv_norm_g", "o_conv_norm_b", "o_post_norm")
ALL_ORDER = ("e_pre_norm", "e_w_in", "e_pool_w", "e_pool_scale", "e_w_out", "e_post_norm", "o_pre_norm", "o_w_in",
             "o_sgu_norm_g", "o_sgu_norm_b", "o_sgu_w", "o_sgu_b", "o_conv_w", "o_conv_b", "o_conv_norm_g",
             "o_conv_norm_b", "o_w_out", "o_post_norm")


def _full_shape(name):
    shape, axis = SHARDED_SMALL[name]
    return tuple(s * N_CHIPS if i == axis else s for i, s in enumerate(shape))


def _from_chips(name, stacked):
    shape, axis = SHARDED_SMALL[name]
    return jnp.moveaxis(stacked, 0, axis).reshape(_full_shape(name))


def _my_shard(name, full, chip):
    shape, axis = SHARDED_SMALL[name]
    return lax.dynamic_slice_in_dim(full, chip * shape[axis], shape[axis], axis)


def kernel(x, e_pre_norm, e_w_in, e_pool_w, e_pool_scale, e_w_out, e_post_norm, o_pre_norm, o_w_in, o_sgu_norm_g, o_sgu_norm_b, o_sgu_w, o_sgu_b, o_conv_w, o_conv_b, o_conv_norm_g, o_conv_norm_b, o_w_out, o_post_norm, loss_target, m_e_pre_norm, m_e_w_in, m_e_pool_w, m_e_pool_scale, m_e_w_out, m_e_post_norm, m_o_pre_norm, m_o_w_in, m_o_sgu_norm_g, m_o_sgu_norm_b, m_o_sgu_w, m_o_sgu_b, m_o_conv_w, m_o_conv_b, m_o_conv_norm_g, m_o_conv_norm_b, m_o_w_out, m_o_post_norm, v_e_pre_norm, v_e_w_in, v_e_pool_w, v_e_pool_scale, v_e_w_out, v_e_post_norm, v_o_pre_norm, v_o_w_in, v_o_sgu_norm_g, v_o_sgu_norm_b, v_o_sgu_w, v_o_sgu_b, v_o_conv_w, v_o_conv_b, v_o_conv_norm_g, v_o_conv_norm_b, v_o_w_out, v_o_post_norm):
    w = dict(e_pre_norm=e_pre_norm, e_w_in=e_w_in, e_pool_w=e_pool_w, e_pool_scale=e_pool_scale, e_w_out=e_w_out,
             e_post_norm=e_post_norm, o_pre_norm=o_pre_norm, o_w_in=o_w_in, o_sgu_norm_g=o_sgu_norm_g,
             o_sgu_norm_b=o_sgu_norm_b, o_sgu_w=o_sgu_w, o_sgu_b=o_sgu_b, o_conv_w=o_conv_w, o_conv_b=o_conv_b,
             o_conv_norm_g=o_conv_norm_g, o_conv_norm_b=o_conv_norm_b, o_w_out=o_w_out, o_post_norm=o_post_norm)
    m = dict(e_pre_norm=m_e_pre_norm, e_w_in=m_e_w_in, e_pool_w=m_e_pool_w, e_pool_scale=m_e_pool_scale,
             e_w_out=m_e_w_out, e_post_norm=m_e_post_norm, o_pre_norm=m_o_pre_norm, o_w_in=m_o_w_in,
             o_sgu_norm_g=m_o_sgu_norm_g, o_sgu_norm_b=m_o_sgu_norm_b, o_sgu_w=m_o_sgu_w, o_sgu_b=m_o_sgu_b,
             o_conv_w=m_o_conv_w, o_conv_b=m_o_conv_b, o_conv_norm_g=m_o_conv_norm_g, o_conv_norm_b=m_o_conv_norm_b,
             o_w_out=m_o_w_out, o_post_norm=m_o_post_norm)
    v = dict(e_pre_norm=v_e_pre_norm, e_w_in=v_e_w_in, e_pool_w=v_e_pool_w, e_pool_scale=v_e_pool_scale,
             e_w_out=v_e_w_out, e_post_norm=v_e_post_norm, o_pre_norm=v_o_pre_norm, o_w_in=v_o_w_in,
             o_sgu_norm_g=v_o_sgu_norm_g, o_sgu_norm_b=v_o_sgu_norm_b, o_sgu_w=v_o_sgu_w, o_sgu_b=v_o_sgu_b,
             o_conv_w=v_o_conv_w, o_conv_b=v_o_conv_b, o_conv_norm_g=v_o_conv_norm_g, o_conv_norm_b=v_o_conv_norm_b,
             o_w_out=v_o_w_out, o_post_norm=v_o_post_norm)
    w, m, v = ({k: a[0] for k, a in d.items()} for d in (w, m, v))
    chip = 2 * lax.axis_index("x") + lax.axis_index("y")

    loss, grad_x, in_flight, small = _step(x[0], loss_target[0], w, chip)

    grads, delta, new_m, new_v = {}, {}, {}, {}
    after = grad_x
    for k in ("o_w_out", "o_w_in", "e_w_out", "e_w_in"):
        grads[k], delta[k], new_m[k], new_v[k] = _adamw(w[k], _land(in_flight, k, chip, after), m[k], v[k], f"adamw_{k}")
        after = delta[k]

    small_full_shapes = {k: (_full_shape(k) if k in SHARDED_SMALL else REPLICATED_SMALL[k]) for k in SMALL_ORDER}
    small_parts = _pack([small[k].reshape(small_full_shapes[k]) for k in SMALL_ORDER], total_rows=SMALL_GRAD_ROWS)
    small_parts = small_parts + 0.0 * after[0, 0]
    reduced = _reduce_scatter([small_parts.reshape(N_CHIPS, SMALL_GRAD_ROWS // N_CHIPS, LANES)], "small")
    small_sum = _all_gather(reduced, "gather_small_grads")[0].reshape(SMALL_GRAD_ROWS, LANES)
    for k, a in zip(SMALL_ORDER, _unpack(small_sum, [small_full_shapes[k] for k in SMALL_ORDER])):
        grads[k] = _my_shard(k, a, chip) if k in SHARDED_SMALL else a
    loss = lax.psum(loss[0, 0], ("x", "y", "c"))

    def rows_of(a):
        return a.reshape(-1, a.shape[-1])

    updates = _adamw_small(*[[rows_of(d[k]) for k in SMALL_ORDER] for d in (w, grads, m, v)])
    for d, arrays in zip((delta, new_m, new_v), updates):
        for k, a in zip(SMALL_ORDER, arrays):
            d[k] = a.reshape(w[k].shape)

    outs = [loss, grad_x[None]]
    for d in (grads, delta, new_m, new_v):
        outs += [d[k][None] for k in ALL_ORDER]
    return tuple(outs)
```

```python
import jax
import jax.numpy as jnp
from jax import lax
from jax.experimental import pallas as pl
from jax.experimental.pallas import tpu as pltpu

f32 = jnp.float32
bf16 = jnp.bfloat16
SDS = jax.ShapeDtypeStruct

SEQ = 2048
D_MODEL = 2048
EPS = 1e-6
NEG = -1e30
HEAD_DIM = 128
ROT_HALF = 16
ROPE_THETA = 500000.0
DILATIONS = (1, 4, 16)
SPAN = 128
N_HEADS = 8
HALF = 1024
POOL_CH = 256
CONV_K = 31
CONV_PAD = 32
CHUNK = 128
N_CHIPS = 4
LANES = 256
E_IN_PIECES = 3
SMALL_SHARD_ROWS = 352
SMALL_GRAD_ROWS = 1536
ANY = pl.BlockSpec(memory_space=pl.ANY)
MESH = pl.DeviceIdType.MESH

ADAM_LR = 0.001
ADAM_B1 = 0.9
ADAM_B2 = 0.999
ADAM_EPS = 1e-08
ADAM_WD = 0.01
ADAM_STEP = 10


def _dot(a, b):
    return jnp.dot(a, b, preferred_element_type=f32)


def _dot_nt(a, b):
    return lax.dot_general(a, b, (((1,), (1,)), ((), ())), preferred_element_type=f32)


def _dot_tn(a, b):
    return lax.dot_general(a, b, (((0,), (0,)), ((), ())), preferred_element_type=f32)


def _sigmoid(x):
    return 1.0 / (1.0 + jnp.exp(-x))


def _silu_and_grad(x):
    s = _sigmoid(x)
    return x * s, s * (1.0 + x * (1.0 - s))


def _rms_fwd(x, g):
    r = lax.rsqrt(jnp.mean(x * x, axis=-1, keepdims=True) + EPS)
    return x * r * g


def _rms_bwd(x, g, dout):
    r = lax.rsqrt(jnp.mean(x * x, axis=-1, keepdims=True) + EPS)
    xh = x * r
    dg = jnp.sum(dout * xh, axis=0, keepdims=True)
    dxh = dout * g
    dx = r * (dxh - xh * jnp.mean(dxh * xh, axis=-1, keepdims=True))
    return dx, dg


def _ln_stats(x):
    mu = jnp.mean(x, axis=-1, keepdims=True)
    xc = x - mu
    rstd = lax.rsqrt(jnp.mean(xc * xc, axis=-1, keepdims=True) + EPS)
    return xc * rstd, rstd


def _ln_bwd(xh, rstd, g, dout):
    dg = jnp.sum(dout * xh, axis=0, keepdims=True)
    db = jnp.sum(dout, axis=0, keepdims=True)
    dxh = dout * g
    dx = rstd * (dxh - jnp.mean(dxh, axis=-1, keepdims=True) - xh * jnp.mean(dxh * xh, axis=-1, keepdims=True))
    return dx, dg, db


def _accumulate(ref, value, first):
    @pl.when(first)
    def _():
        ref[...] = value

    @pl.when(jnp.logical_not(first))
    def _():
        ref[...] += value


def _col_tile(ns):
    for t in (1024, 768, 512, 256):
        if ns % t == 0:
            return t
    raise ValueError(ns)


def _mm_nn(a, w, out_dtype, name, piece=0, pieces=1, into=None):
    m, k = a.shape
    j, _, ns = w.shape
    tm, tn = m, _col_tile(ns)
    nb = ns // tn

    def body(a_ref, w_ref, *rest):
        rest[-1][...] = _dot(a_ref[...], w_ref[...]).astype(rest[-1].dtype)

    return pl.pallas_call(
        body, name=name, grid=(j * nb, m // tm),
        in_specs=[pl.BlockSpec((tm, k), lambda n, i: (i, 0)),
                  pl.BlockSpec((None, k, tn), lambda n, i: (n // nb, 0, n % nb))] + ([] if into is None else [ANY]),
        out_specs=pl.BlockSpec((tm, tn), lambda n, i: (i, ((n // nb) * pieces + piece) * nb + n % nb)),
        out_shape=SDS((m, j * ns * pieces), out_dtype),
        input_output_aliases={} if into is None else {2: 0},
    )(a, w, *([] if into is None else [into]))


def _mm_nt(dz, ws, name, after):
    m, _ = dz.shape
    pieces = len(ws)
    j, k, ns = ws[0].shape
    tm, tk = 1024, 1024

    def body(dz_ref, *rest):
        w_refs, o_ref = rest[:pieces], rest[-1]
        total = _dot_nt(dz_ref[:, 0:ns], w_refs[0][...])
        for q in range(1, pieces):
            total = total + _dot_nt(dz_ref[:, q * ns:(q + 1) * ns], w_refs[q][...])
        _accumulate(o_ref, total, pl.program_id(2) == 0)

    return pl.pallas_call(
        body, name=name, grid=(m // tm, k // tk, j),
        in_specs=[pl.BlockSpec((tm, pieces * ns), lambda i, kk, r: (i, r))]
        + [pl.BlockSpec((None, tk, ns), lambda i, kk, r: (r, kk, 0))] * pieces + [ANY],
        out_specs=pl.BlockSpec((tm, tk), lambda i, kk, r: (i, kk)),
        out_shape=SDS((m, k), f32),
    )(dz, *ws, after)


def _mm_tn(a, dz, j, name):
    m, k = a.shape
    ns = dz.shape[1] // j
    tk, tn = 1024, _col_tile(ns)
    nb = ns // tn

    def body(a_ref, dz_ref, o_ref):
        o_ref[...] = _dot_tn(a_ref[...], dz_ref[...]).astype(o_ref.dtype)

    return pl.pallas_call(
        body, name=name, grid=(k // tk, j * nb),
        in_specs=[pl.BlockSpec((m, tk), lambda kk, n: (0, kk)),
                  pl.BlockSpec((m, tn), lambda kk, n: (0, n))],
        out_specs=pl.BlockSpec((None, tk, tn), lambda kk, n: (n // nb, kk, n % nb)),
        out_shape=SDS((j, k, ns), bf16),
    )(a, dz)


ROWS = 256


def _row_spec(width=D_MODEL, col=0):
    return pl.BlockSpec((ROWS, width), lambda i: (i, col))


def _vec_spec(width=D_MODEL):
    return pl.BlockSpec((1, width), lambda i: (0, 0))


def _pre_norm(x, g):
    def body(x_ref, g_ref, h_ref):
        h_ref[...] = _rms_fwd(x_ref[...], g_ref[...]).astype(bf16)

    return pl.pallas_call(
        body, name="pre_norm", grid=(SEQ // ROWS,), in_specs=[_row_spec(), _vec_spec()],
        out_specs=_row_spec(), out_shape=SDS((SEQ, D_MODEL), bf16))(x, g)


def _mid_norm(x, y, g_post, g_pre):
    def body(x_ref, y_ref, gpost_ref, gpre_ref, x1_ref, h1_ref):
        x1 = x_ref[...] + _rms_fwd(y_ref[...], gpost_ref[...])
        x1_ref[...] = x1
        h1_ref[...] = _rms_fwd(x1, gpre_ref[...]).astype(bf16)

    return pl.pallas_call(
        body, name="mid_norm", grid=(SEQ // ROWS,),
        in_specs=[_row_spec(), _row_spec(), _vec_spec(), _vec_spec()],
        out_specs=[_row_spec(), _row_spec()],
        out_shape=[SDS((SEQ, D_MODEL), f32), SDS((SEQ, D_MODEL), bf16)])(x, y, g_post, g_pre)


def _final_norm_loss(x1, y, g_post, target):
    def body(x1_ref, y_ref, g_ref, t_ref, loss_ref, dx2_ref, dy_ref, dg_ref):
        first = pl.program_id(0) == 0
        y = y_ref[...]
        g = g_ref[...]
        err = x1_ref[...] + _rms_fwd(y, g) - t_ref[...]
        sq = jnp.sum(jnp.sum(err * err, axis=1, keepdims=True), axis=0, keepdims=True)
        _accumulate(loss_ref, sq * (0.5 / D_MODEL), first)
        dx2 = err * (1.0 / D_MODEL)
        dx2_ref[...] = dx2
        dy, dg = _rms_bwd(y, g, dx2)
        dy_ref[...] = dy.astype(bf16)
        _accumulate(dg_ref, dg, first)

    return pl.pallas_call(
        body, name="final_norm_loss", grid=(SEQ // ROWS,),
        in_specs=[_row_spec(), _row_spec(), _vec_spec(), _row_spec()],
        out_specs=[pl.BlockSpec((1, 1), lambda i: (0, 0)), _row_spec(), _row_spec(), _vec_spec()],
        out_shape=[SDS((1, 1), f32), SDS((SEQ, D_MODEL), f32), SDS((SEQ, D_MODEL), bf16), SDS((1, D_MODEL), f32)],
    )(x1, y, g_post, target)


def _mid_norm_bwd(dx2, dh1, x1, y0, g_pre, g_post):
    def body(dx2_ref, dh1_ref, x1_ref, y0_ref, gpre_ref, gpost_ref, dx1_ref, dy0_ref, dgpre_ref, dgpost_ref):
        first = pl.program_id(0) == 0
        d_in, dgpre = _rms_bwd(x1_ref[...], gpre_ref[...], dh1_ref[...])
        dx1 = dx2_ref[...] + d_in
        dx1_ref[...] = dx1
        dy0, dgpost = _rms_bwd(y0_ref[...], gpost_ref[...], dx1)
        dy0_ref[...] = dy0.astype(bf16)
        _accumulate(dgpre_ref, dgpre, first)
        _accumulate(dgpost_ref, dgpost, first)

    return pl.pallas_call(
        body, name="mid_norm_bwd", grid=(SEQ // ROWS,),
        in_specs=[_row_spec(), _row_spec(), _row_spec(), _row_spec(), _vec_spec(), _vec_spec()],
        out_specs=[_row_spec(), _row_spec(), _vec_spec(), _vec_spec()],
        out_shape=[SDS((SEQ, D_MODEL), f32), SDS((SEQ, D_MODEL), bf16), SDS((1, D_MODEL), f32), SDS((1, D_MODEL), f32)],
    )(dx2, dh1, x1, y0, g_pre, g_post)


def _pre_norm_bwd(dx1, dh0, x, g):
    def body(dx1_ref, dh0_ref, x_ref, g_ref, dx_ref, dg_ref):
        d_in, dg = _rms_bwd(x_ref[...], g_ref[...], dh0_ref[...])
        dx_ref[...] = dx1_ref[...] + d_in
        _accumulate(dg_ref, dg, pl.program_id(0) == 0)

    return pl.pallas_call(
        body, name="pre_norm_bwd", grid=(SEQ // ROWS,),
        in_specs=[_row_spec(), _row_spec(), _row_spec(), _vec_spec()],
        out_specs=[_row_spec(), _vec_spec()],
        out_shape=[SDS((SEQ, D_MODEL), f32), SDS((1, D_MODEL), f32)])(dx1, dh0, x, g)


def _pool_count(g):
    row = lax.broadcasted_iota(jnp.int32, (SEQ, 1), 0)
    width = jnp.left_shift(2, g)
    return row, width, jnp.minimum(row + 1, width).astype(f32)


def _trailing_sum(x, row, width):
    s = x
    for k in (1, 2, 4, 8):
        shifted = jnp.where(row >= k, pltpu.roll(s, k, 0), 0.0)
        s = jnp.where(width > k, s + shifted, s)
    return s


def _leading_sum(x, row, width):
    s = x
    for k in (1, 2, 4, 8):
        shifted = jnp.where(row < SEQ - k, pltpu.roll(s, SEQ - k, 0), 0.0)
        s = jnp.where(width > k, s + shifted, s)
    return s


def _pool_specs():
    a_in = pl.BlockSpec((SEQ, POOL_CH), lambda g: (0, g))
    a_gate = pl.BlockSpec((SEQ, POOL_CH), lambda g: (0, 4 + g))
    w = pl.BlockSpec((None, POOL_CH, POOL_CH), lambda g: (g, 0, 0))
    scale = pl.BlockSpec((1, POOL_CH), lambda g: (0, g))
    return a_in, a_gate, w, scale


def _pool_fwd(z0, pool_w, pool_scale):
    def body(a_ref, gate_ref, w_ref, scale_ref, ya_ref):
        row, width, count = _pool_count(pl.program_id(0))
        a = a_ref[...]
        pooled = _trailing_sum(a, row, width) / count - a
        mixed = _dot(pooled.astype(bf16), w_ref[...]) * scale_ref[...]
        gate = gate_ref[...]
        ya_ref[...] = (mixed * gate * _sigmoid(gate)).astype(bf16)

    return pl.pallas_call(
        body, name="pool_fwd", grid=(4,), in_specs=list(_pool_specs()),
        out_specs=pl.BlockSpec((SEQ, POOL_CH), lambda g: (0, g)),
        out_shape=SDS((SEQ, HALF), bf16))(z0, z0, pool_w, pool_scale)


def _pool_bwd(z0, dcat, pool_w, pool_scale):
    def body(a_ref, gate_ref, w_ref, scale_ref, dya_ref, da_ref, dgate_ref, dw_ref, dscale_ref):
        row, width, count = _pool_count(pl.program_id(0))
        a = a_ref[...]
        pooled = (_trailing_sum(a, row, width) / count - a).astype(bf16)
        w = w_ref[...]
        scale = scale_ref[...]
        mixed = _dot(pooled, w)
        silu, dsilu = _silu_and_grad(gate_ref[...])
        dya = dya_ref[...]
        dgate_ref[...] = (dya * mixed * scale * dsilu).astype(bf16)
        dms = dya * silu
        dscale_ref[...] = jnp.sum(dms * mixed, axis=0, keepdims=True)
        dmixed = (dms * scale).astype(bf16)
        dw_ref[...] = _dot_tn(pooled, dmixed)
        dpooled = _dot_nt(dmixed, w)
        da_ref[...] = (_leading_sum(dpooled / count, row, width) - dpooled).astype(bf16)

    a_in, a_gate, w, scale = _pool_specs()
    col = pl.BlockSpec((SEQ, POOL_CH), lambda g: (0, g))
    return pl.pallas_call(
        body, name="pool_bwd", grid=(4,), in_specs=[a_in, a_gate, w, scale, col],
        out_specs=[col, col, w, scale],
        out_shape=[SDS((SEQ, HALF), bf16), SDS((SEQ, HALF), bf16), SDS((4, POOL_CH, POOL_CH), f32), SDS((1, HALF), f32)],
    )(z0, z0, pool_w, pool_scale, dcat)


Q_COL, K_COL, V_COL, BGATE_COL = 16, 40, 64, 88


def _rope_tables():
    pos = jnp.arange(SEQ, dtype=f32)
    inv_freq = jnp.power(ROPE_THETA, -jnp.arange(0, 2 * ROT_HALF, 2, dtype=f32) / (2 * ROT_HALF))
    ang = pos[:, None] * inv_freq[None, :]
    cos, sin = jnp.cos(ang), jnp.sin(ang)
    zeros = jnp.zeros((SEQ, HEAD_DIM - 2 * ROT_HALF), f32)
    cos_t = jnp.concatenate([cos, cos, zeros + 1.0], axis=1)
    sin_t = jnp.concatenate([sin, sin, zeros], axis=1)
    j = jnp.arange(HEAD_DIM)[:, None]
    i = jnp.arange(HEAD_DIM)[None, :]
    rot = jnp.where((i < ROT_HALF) & (j == i + ROT_HALF), -1.0, 0.0) + jnp.where(
        (i >= ROT_HALF) & (i < 2 * ROT_HALF) & (j == i - ROT_HALF), 1.0, 0.0)
    return cos_t, sin_t, rot.astype(bf16), rot.T.astype(bf16)


def _exact_dot(t, m):
    hi = t.astype(bf16)
    lo = (t - hi.astype(f32)).astype(bf16)
    return _dot(hi, m) + _dot(lo, m)


def _rope(t, cos_t, sin_t, rot):
    return t * cos_t + _exact_dot(t, rot) * sin_t


def _rope_transposed(d, cos_t, sin_t, rot_t):
    return d * cos_t + _exact_dot(d * sin_t, rot_t)


ROW_CHUNK = 256


def _chunks(fn):
    def step(i, carry):
        fn(pl.multiple_of(i * ROW_CHUNK, ROW_CHUNK))
        return carry

    lax.fori_loop(0, SEQ // ROW_CHUNK, step, 0, unroll=2)


def _pieces(dilation):
    length = SEQ // dilation
    n = min(length, ROW_CHUNK)
    return [(r, l0, n) for r in range(dilation) for l0 in range(0, length, n)]


def _by_residue(dst_ref, src_ref, dilation, dtype):
    length = SEQ // dilation
    for r, l0, n in _pieces(dilation):
        src = src_ref[l0:l0 + n, :] if dilation == 1 else src_ref[pl.ds(r + dilation * l0, n, stride=dilation), :]
        start = r * length + l0
        dst_ref[start:start + n, :] = src.astype(dtype)


def _by_position(dst_ref, src_ref, dilation):
    length = SEQ // dilation
    for r, l0, n in _pieces(dilation):
        src = src_ref[r * length + l0:r * length + l0 + n, :]
        if dilation == 1:
            dst_ref[l0:l0 + n, :] = src
        else:
            dst_ref[pl.ds(r + dilation * l0, n, stride=dilation), :] = src


def _attn_masks():
    qi = lax.broadcasted_iota(jnp.int32, (SPAN, 2 * SPAN), 0)
    kj = lax.broadcasted_iota(jnp.int32, (SPAN, 2 * SPAN), 1)
    window = ((kj < SPAN) & (kj >= qi)) | ((kj >= SPAN) & (kj - SPAN <= qi))
    own = lax.broadcasted_iota(jnp.int32, (SPAN, SPAN), 1) <= lax.broadcasted_iota(jnp.int32, (SPAN, SPAN), 0)
    return window, own


def _attn_blocks(dilation):
    per_residue = SEQ // dilation // SPAN
    blocks = [(c, c % per_residue != 0) for c in range(SEQ // SPAN)]
    return [blocks[i:i + 4] for i in range(0, len(blocks), 4)]


def _block_keys(c, has_prev):
    return slice((c - 1) * SPAN if has_prev else c * SPAN, (c + 1) * SPAN)


def _head_spec(col):
    return pl.BlockSpec((SEQ, HEAD_DIM), lambda h: (0, col + h))


def _table_spec():
    return pl.BlockSpec((SEQ, HEAD_DIM), lambda h: (0, 0))


def _attn_fwd(z0, tables):
    scale = HEAD_DIM ** -0.5

    def body(*refs):
        qkv = refs[0:9]
        bg_ref, cos_ref, sin_ref, rot_ref = refs[9:13]
        yb_ref, att_ref, lse_ref = refs[13:16]
        saved = refs[16:25]
        tmp_q, tmp_k, v_ones, o_res, l_res, o_nat, l_nat = refs[25:32]
        window_mask, own_mask = _attn_masks()
        rot = rot_ref[...]

        @pl.when(pl.program_id(0) == 0)
        def _():
            v_ones[:, HEAD_DIM:] = jnp.ones((SEQ, HEAD_DIM), bf16)

        for g, dilation in enumerate(DILATIONS):
            q_ref, k_ref, v_ref = qkv[3 * g:3 * g + 3]
            qd, kd, vd = saved[3 * g:3 * g + 3]

            def rope_rows(start, q_ref=q_ref, k_ref=k_ref):
                r = pl.ds(start, ROW_CHUNK)
                cos_t, sin_t = cos_ref[r, :], sin_ref[r, :]
                tmp_q[r, :] = _rope(q_ref[r, :], cos_t, sin_t, rot) * scale
                tmp_k[r, :] = _rope(k_ref[r, :], cos_t, sin_t, rot)

            _chunks(rope_rows)
            _by_residue(qd, tmp_q, dilation, bf16)
            _by_residue(kd, tmp_k, dilation, bf16)
            _by_residue(vd, v_ref, dilation, bf16)
            for l0 in range(0, SEQ, ROW_CHUNK):
                v_ones[l0:l0 + ROW_CHUNK, 0:HEAD_DIM] = vd[l0:l0 + ROW_CHUNK, :]

            for four in _attn_blocks(dilation):
                scores = [_dot_nt(qd[c * SPAN:(c + 1) * SPAN, :], kd[_block_keys(c, prev), :]) for c, prev in four]
                tops, probs = [], []
                for (c, prev), s in zip(four, scores):
                    s = jnp.where(window_mask if prev else own_mask, s, NEG)
                    tops.append(jnp.max(s, axis=1, keepdims=True))
                    probs.append(jnp.exp(s - tops[-1]).astype(bf16))
                sums = [_dot(p, v_ones[_block_keys(c, prev), :]) for (c, prev), p in zip(four, probs)]
                for (c, prev), m, o in zip(four, tops, sums):
                    den = o[:, HEAD_DIM:]
                    o_res[c * SPAN:(c + 1) * SPAN, :] = o[:, :HEAD_DIM] / den
                    l_res[c * SPAN:(c + 1) * SPAN, :] = m + jnp.log(den)

            if dilation > 1:
                _by_position(o_nat, o_res, dilation)
                _by_position(l_nat, l_res, dilation)
            o_g, l_g = (o_res, l_res) if dilation == 1 else (o_nat, l_nat)

            def merge(start, g=g, o_g=o_g, l_g=l_g):
                r = pl.ds(start, ROW_CHUNK)
                if g == 0:
                    att, total = o_g[r, :], l_g[r, :]
                else:
                    l_old, l_new = lse_ref[r, :], l_g[r, :]
                    top = jnp.maximum(l_old, l_new)
                    total = top + jnp.log(jnp.exp(l_old - top) + jnp.exp(l_new - top))
                    att = att_ref[r, :] * jnp.exp(l_old - total) + o_g[r, :] * jnp.exp(l_new - total)
                att_ref[r, :] = att
                lse_ref[r, :] = total
                if g == len(DILATIONS) - 1:
                    gate = bg_ref[r, :]
                    yb_ref[r, :] = (att * gate * _sigmoid(gate)).astype(bf16)

            _chunks(merge)

    in_specs = []
    for g in range(3):
        in_specs += [_head_spec(Q_COL + 8 * g), _head_spec(K_COL + 8 * g), _head_spec(V_COL + 8 * g)]
    in_specs += [_head_spec(BGATE_COL), _table_spec(), _table_spec(), pl.BlockSpec((HEAD_DIM, HEAD_DIM), lambda h: (0, 0))]
    out_spec = pl.BlockSpec((SEQ, HEAD_DIM), lambda h: (0, h))
    vm = lambda dt: pltpu.VMEM((SEQ, HEAD_DIM), dt)
    cos_t, sin_t, rot, _ = tables
    out = pl.pallas_call(
        body, name="attn_fwd", grid=(N_HEADS,), in_specs=in_specs, out_specs=[out_spec] * 12,
        out_shape=[SDS((SEQ, HALF), bf16), SDS((SEQ, HALF), f32), SDS((SEQ, HALF), f32)] + [SDS((SEQ, HALF), bf16)] * 9,
        scratch_shapes=[vm(f32), vm(f32), pltpu.VMEM((SEQ, 2 * HEAD_DIM), bf16), vm(f32), vm(f32), vm(f32), vm(f32)],
    )(*([z0] * 10), cos_t, sin_t, rot)
    return out[0], out[1], out[2], [tuple(out[3 + 3 * g:6 + 3 * g]) for g in range(3)]


def _attn_bwd_group(g, saved, z0, att, lse, dcat, tables):
    scale = HEAD_DIM ** -0.5
    dilation = DILATIONS[g]
    with_gate = g == 0

    def body(*refs):
        qd, kd, vd, bg_ref, att_ref, lse_ref, dyb_ref, cos_ref, sin_ref, rot_t_ref = refs[0:10]
        n_out = 4 if with_gate else 3
        dq_ref, dk_ref, dv_ref = refs[10:13]
        dod, ld, dd, tmp, aq, ak, av = refs[10 + n_out:17 + n_out]
        window_mask, own_mask = _attn_masks()
        rot_t = rot_t_ref[...]

        def gate_rows(start):
            r = pl.ds(start, ROW_CHUNK)
            silu, dsilu = _silu_and_grad(bg_ref[r, :])
            att_v = att_ref[r, :]
            dyb = dyb_ref[r, :]
            if with_gate:
                refs[13][r, :] = (dyb * att_v * dsilu).astype(bf16)
            datt = dyb * silu
            tmp[r, :] = datt
            aq[r, :] = jnp.broadcast_to(jnp.sum(datt * att_v, axis=1, keepdims=True), (ROW_CHUNK, HEAD_DIM))

        _chunks(gate_rows)
        _by_residue(dod, tmp, dilation, bf16)
        _by_residue(dd, aq, dilation, f32)
        _by_residue(ld, lse_ref, dilation, f32)

        for four in _attn_blocks(dilation):
            rows = [slice(c * SPAN, (c + 1) * SPAN) for c, _ in four]
            keys = [_block_keys(c, prev) for c, prev in four]
            scores = [_dot_nt(qd[r, :], kd[k, :]) for r, k in zip(rows, keys)]
            dprobs = [_dot_nt(dod[r, :], vd[k, :]) for r, k in zip(rows, keys)]
            probs, dscores = [], []
            for (c, prev), r, s, dp in zip(four, rows, scores, dprobs):
                lse_q, delta = ld[r, :], dd[r, :]
                if prev:
                    lse_q = jnp.concatenate([lse_q, lse_q], axis=1)
                    delta = jnp.concatenate([delta, delta], axis=1)
                p = jnp.where(window_mask if prev else own_mask, jnp.exp(s - lse_q), 0.0)
                probs.append(p.astype(bf16))
                dscores.append((p * (dp - delta)).astype(bf16))
            dvs = [_dot_tn(p, dod[r, :]) for p, r in zip(probs, rows)]
            dks = [_dot_tn(ds, qd[r, :]) for ds, r in zip(dscores, rows)]
            dqs = [_dot(ds, kd[k, :]) for ds, k in zip(dscores, keys)]
            for (c, prev), r, dv, dk, dq in zip(four, rows, dvs, dks, dqs):
                aq[r, :] = dq
                if prev:
                    before = slice((c - 1) * SPAN, c * SPAN)
                    av[before, :] += dv[0:SPAN]
                    ak[before, :] += dk[0:SPAN]
                    av[r, :] = dv[SPAN:]
                    ak[r, :] = dk[SPAN:]
                else:
                    av[r, :] = dv
                    ak[r, :] = dk

        def finish(out_ref, acc, factor, roped):
            if dilation > 1:
                _by_position(tmp, acc, dilation)
            src = acc if dilation == 1 else tmp

            def rows(start):
                r = pl.ds(start, ROW_CHUNK)
                d = src[r, :]
                if factor != 1.0:
                    d = d * factor
                if roped:
                    d = _rope_transposed(d, cos_ref[r, :], sin_ref[r, :], rot_t)
                out_ref[r, :] = d.astype(bf16)

            _chunks(rows)

        finish(dq_ref, aq, scale, True)
        finish(dk_ref, ak, 1.0, True)
        finish(dv_ref, av, 1.0, False)

    head = pl.BlockSpec((SEQ, HEAD_DIM), lambda h: (0, h))
    in_specs = [head, head, head, _head_spec(BGATE_COL), head, head, _head_spec(8), _table_spec(), _table_spec(),
                pl.BlockSpec((HEAD_DIM, HEAD_DIM), lambda h: (0, 0))]
    n_out = 4 if with_gate else 3
    vm = lambda dt: pltpu.VMEM((SEQ, HEAD_DIM), dt)
    cos_t, sin_t, _, rot_t = tables
    return pl.pallas_call(
        body, name=f"attn_bwd_g{g}", grid=(N_HEADS,), in_specs=in_specs, out_specs=[head] * n_out,
        out_shape=[SDS((SEQ, HALF), bf16)] * n_out,
        scratch_shapes=[vm(bf16), vm(f32), vm(f32), vm(f32), vm(f32), vm(f32), vm(f32)],
    )(*saved, z0, att, lse, dcat, cos_t, sin_t, rot_t)


def _sgu_specs():
    chunk = lambda col: pl.BlockSpec((CHUNK, HALF), lambda n: (n, col))
    vec = pl.BlockSpec((1, HALF), lambda n: (0, 0))
    w = pl.BlockSpec((4, CHUNK, CHUNK), lambda n: (0, 0, 0))
    bias = pl.BlockSpec((CHUNK, CHUNK), lambda n: (0, 0))
    return chunk, vec, w, bias


def _sgu_weights(w_ref):
    tril = lax.broadcasted_iota(jnp.int32, (CHUNK, CHUNK), 1) <= lax.broadcasted_iota(jnp.int32, (CHUNK, CHUNK), 0)
    return tril, [jnp.where(tril, w_ref[h], 0.0).astype(bf16) for h in range(4)]


def _sgu_fwd(z1, ln_g, ln_b, sgu_w, bias_t):
    def body(u_ref, v_ref, cg_ref, g_ref, b_ref, w_ref, bias_ref, yc_ref):
        _, ws = _sgu_weights(w_ref)
        xh, _ = _ln_stats(v_ref[...])
        vn = (xh * g_ref[...] + b_ref[...]).astype(bf16)
        for h in range(4):
            cols = slice(h * POOL_CH, (h + 1) * POOL_CH)
            s = _dot(ws[h], vn[:, cols]) + bias_ref[:, h:h + 1]
            gate = cg_ref[:, cols]
            yc_ref[:, cols] = (u_ref[:, cols] * s * gate * _sigmoid(gate)).astype(bf16)

    chunk, vec, w, bias = _sgu_specs()
    return pl.pallas_call(
        body, name="sgu_fwd", grid=(SEQ // CHUNK,),
        in_specs=[chunk(0), chunk(1), chunk(2), vec, vec, w, bias], out_specs=chunk(0),
        out_shape=SDS((SEQ, HALF), bf16))(z1, z1, z1, ln_g, ln_b, sgu_w, bias_t)


def _sgu_bwd(z1, dcat, ln_g, ln_b, sgu_w, bias_t):
    def body(u_ref, v_ref, cg_ref, dyc_ref, g_ref, b_ref, w_ref, bias_ref,
             du_ref, dv_ref, dcg_ref, dw_ref, dbias_ref, dg_ref, db_ref, dvn_ref):
        first = pl.program_id(0) == 0
        tril, ws = _sgu_weights(w_ref)
        xh, rstd = _ln_stats(v_ref[...])
        g = g_ref[...]
        vn = (xh * g + b_ref[...]).astype(bf16)

        @pl.when(first)
        def _():
            dbias_ref[...] = jnp.zeros((CHUNK, CHUNK), f32)

        for h in range(4):
            cols = slice(h * POOL_CH, (h + 1) * POOL_CH)
            vn_h = vn[:, cols]
            s = _dot(ws[h], vn_h) + bias_ref[:, h:h + 1]
            silu, dsilu = _silu_and_grad(cg_ref[:, cols])
            dyc = dyc_ref[:, cols]
            u = u_ref[:, cols]
            du_ref[:, cols] = (dyc * s * silu).astype(bf16)
            dcg_ref[:, cols] = (dyc * u * s * dsilu).astype(bf16)
            ds = dyc * u * silu
            dbias_ref[:, h:h + 1] += jnp.sum(ds, axis=1, keepdims=True)
            ds = ds.astype(bf16)
            _accumulate(dw_ref.at[h], jnp.where(tril, _dot_nt(ds, vn_h), 0.0), first)
            dvn_ref[:, cols] = _dot_tn(ws[h], ds)
        dv, dg, db = _ln_bwd(xh, rstd, g, dvn_ref[...])
        dv_ref[...] = dv.astype(bf16)
        _accumulate(dg_ref, dg, first)
        _accumulate(db_ref, db, first)

    chunk, vec, w, bias = _sgu_specs()
    return pl.pallas_call(
        body, name="sgu_bwd", grid=(SEQ // CHUNK,),
        in_specs=[chunk(0), chunk(1), chunk(2), chunk(0), vec, vec, w, bias],
        out_specs=[chunk(0), chunk(0), chunk(0), w, bias, vec, vec],
        out_shape=[SDS((SEQ, HALF), bf16)] * 3 + [SDS((4, CHUNK, CHUNK), f32), SDS((CHUNK, CHUNK), f32),
                                                   SDS((1, HALF), f32), SDS((1, HALF), f32)],
        scratch_shapes=[pltpu.VMEM((CHUNK, HALF), f32)],
    )(z1, z1, z1, dcat, ln_g, ln_b, sgu_w, bias_t)


CONV_TILE = 128
DVAL_COL, DGLU_COL = 12, 16


def _conv_specs():
    val = pl.BlockSpec((SEQ, POOL_CH), lambda j: (0, DVAL_COL + j))
    glu = pl.BlockSpec((SEQ, POOL_CH), lambda j: (0, DGLU_COL + j))
    w = pl.BlockSpec((CONV_K, POOL_CH), lambda j: (0, j))
    col = pl.BlockSpec((SEQ, POOL_CH), lambda j: (0, j))
    vec = pl.BlockSpec((1, POOL_CH), lambda j: (0, j))
    return val, glu, w, col, vec


def _conv_fwd(z1, conv_w, conv_b):
    def body(val_ref, glu_ref, w_ref, b_ref, out_ref, xpad):
        xpad[0:CONV_PAD, :] = jnp.zeros((CONV_PAD, POOL_CH), f32)
        xpad[CONV_PAD:, :] = val_ref[...] * _sigmoid(glu_ref[...])
        w = w_ref[...]
        bias = b_ref[...]

        def tile(i, carry):
            t0 = pl.multiple_of(i * CONV_TILE, CONV_TILE)
            window = xpad[pl.ds(t0, CONV_TILE + CONV_PAD), :]
            acc = jnp.broadcast_to(bias, (CONV_TILE, POOL_CH))
            for k in range(CONV_K):
                shift = CONV_PAD - (CONV_K - 1) + k
                acc = acc + w[k:k + 1, :] * pltpu.roll(window, CONV_TILE + CONV_PAD - shift, 0)[0:CONV_TILE]
            out_ref[pl.ds(t0, CONV_TILE), :] = acc
            return carry

        lax.fori_loop(0, SEQ // CONV_TILE, tile, 0)

    val, glu, w, col, vec = _conv_specs()
    return pl.pallas_call(
        body, name="conv_fwd", grid=(4,), in_specs=[val, glu, w, vec], out_specs=col,
        out_shape=SDS((SEQ, HALF), f32), scratch_shapes=[pltpu.VMEM((SEQ + CONV_PAD, POOL_CH), f32)],
    )(z1, z1, conv_w, conv_b)


def _conv_bwd(z1, dconv, conv_w):
    def body(val_ref, glu_ref, w_ref, dout_ref, dval_ref, dglu_ref, dw_ref, db_ref, xpad, dpad, dx_ref):
        val = val_ref[...]
        sig = _sigmoid(glu_ref[...])
        xpad[0:CONV_PAD, :] = jnp.zeros((CONV_PAD, POOL_CH), f32)
        xpad[CONV_PAD:, :] = val * sig
        dout = dout_ref[...]
        dpad[0:SEQ, :] = dout
        dpad[SEQ:, :] = jnp.zeros((CONV_PAD, POOL_CH), f32)
        db_ref[...] = jnp.sum(dout, axis=0, keepdims=True)
        dw_ref[...] = jnp.zeros((CONV_K, POOL_CH), f32)
        w = w_ref[...]

        def tile(i, carry):
            t0 = pl.multiple_of(i * CONV_TILE, CONV_TILE)
            x_win = xpad[pl.ds(t0, CONV_TILE + CONV_PAD), :]
            d_win = dpad[pl.ds(t0, CONV_TILE + CONV_PAD), :]
            d_own = d_win[0:CONV_TILE]
            acc = jnp.zeros((CONV_TILE, POOL_CH), f32)
            for k in range(CONV_K):
                shift = CONV_PAD - (CONV_K - 1) + k
                x_k = pltpu.roll(x_win, CONV_TILE + CONV_PAD - shift, 0)[0:CONV_TILE]
                dw_ref[k:k + 1, :] += jnp.sum(d_own * x_k, axis=0, keepdims=True)
                back = CONV_K - 1 - k
                d_k = d_own if back == 0 else pltpu.roll(d_win, CONV_TILE + CONV_PAD - back, 0)[0:CONV_TILE]
                acc = acc + w[k:k + 1, :] * d_k
            dx_ref[pl.ds(t0, CONV_TILE), :] = acc
            return carry

        lax.fori_loop(0, SEQ // CONV_TILE, tile, 0)
        dx = dx_ref[...]
        dval_ref[...] = (dx * sig).astype(bf16)
        dglu_ref[...] = (dx * val * sig * (1.0 - sig)).astype(bf16)

    val, glu, w, col, vec = _conv_specs()
    pad = pltpu.VMEM((SEQ + CONV_PAD, POOL_CH), f32)
    return pl.pallas_call(
        body, name="conv_bwd", grid=(4,), in_specs=[val, glu, w, col], out_specs=[col, col, w, vec],
        out_shape=[SDS((SEQ, HALF), bf16), SDS((SEQ, HALF), bf16), SDS((CONV_K, HALF), f32), SDS((1, HALF), f32)],
        scratch_shapes=[pad, pad, pltpu.VMEM((SEQ, POOL_CH), f32)],
    )(z1, z1, conv_w, dconv)


DGATE_COL = 5


def _conv_norm_fwd(conv, z1, g, b):
    def body(c_ref, gate_ref, g_ref, b_ref, yd_ref):
        xh, _ = _ln_stats(c_ref[...])
        n = xh * g_ref[...] + b_ref[...]
        gate = gate_ref[...]
        yd_ref[...] = (n * _sigmoid(n) * gate * _sigmoid(gate)).astype(bf16)

    return pl.pallas_call(
        body, name="conv_norm_fwd", grid=(SEQ // ROWS,),
        in_specs=[_row_spec(HALF), _row_spec(HALF, DGATE_COL), _vec_spec(HALF), _vec_spec(HALF)],
        out_specs=_row_spec(HALF), out_shape=SDS((SEQ, HALF), bf16))(conv, z1, g, b)


def _conv_norm_bwd(conv, z1, dcat, g, b):
    def body(c_ref, gate_ref, dyd_ref, g_ref, b_ref, dconv_ref, dgate_ref, dg_ref, db_ref):
        first = pl.program_id(0) == 0
        xh, rstd = _ln_stats(c_ref[...])
        g = g_ref[...]
        n_silu, n_dsilu = _silu_and_grad(xh * g + b_ref[...])
        gate_silu, gate_dsilu = _silu_and_grad(gate_ref[...])
        dyd = dyd_ref[...]
        dgate_ref[...] = (dyd * n_silu * gate_dsilu).astype(bf16)
        dconv, dg, db = _ln_bwd(xh, rstd, g, dyd * gate_silu * n_dsilu)
        dconv_ref[...] = dconv
        _accumulate(dg_ref, dg, first)
        _accumulate(db_ref, db, first)

    return pl.pallas_call(
        body, name="conv_norm_bwd", grid=(SEQ // ROWS,),
        in_specs=[_row_spec(HALF), _row_spec(HALF, DGATE_COL), _row_spec(HALF, 1), _vec_spec(HALF), _vec_spec(HALF)],
        out_specs=[_row_spec(HALF), _row_spec(HALF), _vec_spec(HALF), _vec_spec(HALF)],
        out_shape=[SDS((SEQ, HALF), f32), SDS((SEQ, HALF), bf16), SDS((1, HALF), f32), SDS((1, HALF), f32)],
    )(conv, z1, dcat, g, b)


def _step(x, target, w, chip):
    chip_vec = chip.astype(jnp.int32).reshape(1)
    sharded_names = list(SHARDED_SMALL)
    small_shard = _pack([w[k] for k in sharded_names], total_rows=SMALL_SHARD_ROWS)
    small_slot = lax.dynamic_update_slice(jnp.zeros((N_CHIPS, SMALL_SHARD_ROWS, LANES), f32), small_shard[None], (chip, 0, 0))
    first = [small_slot, _cast_into_slot(w["e_w_in"], chip_vec, "cast_e_w_in0", small_slot, 0, E_IN_PIECES)]
    sems, bufs, token = _gather_start(first, "gather_start_first")
    more = [_cast_into_slot(w["e_w_in"], chip_vec, f"cast_e_w_in{i}", token, i, E_IN_PIECES) for i in range(1, E_IN_PIECES)]
    more_sems, more_bufs, token = _gather_start(more, "gather_start_pieces")
    rest = [_cast_into_slot(w[k], chip_vec, f"cast_{k}", token) for k in BIG[1:]]
    rest_sems, rest_bufs, token = _gather_start(rest, "gather_start_rest")
    sems, bufs = sems + more_sems + rest_sems, bufs + more_bufs + rest_bufs
    tables = _rope_tables()

    def vec(k):
        return w[k].reshape(1, -1)

    h0 = _pre_norm(x, vec("e_pre_norm") + token[0, 0])
    after, z0, e_w_in = h0, None, []
    for i in range(E_IN_PIECES):
        group = slice(0, 2) if i == 0 else slice(1 + i, 2 + i)
        landed = _forward_halves(_gather_wait(bufs[group], sems[group], after, f"gather_wait_{i}"), f"forward_{i}")
        if i == 0:
            small_full = landed[0]
        e_w_in.append(landed[-1])
        z0 = _mm_nn(h0, landed[-1], f32, f"e_in{i}", i, E_IN_PIECES, z0)
        after = z0
    p = {k: _from_chips(k, a) for k, a in zip(sharded_names, _unpack(small_full, [SHARDED_SMALL[k][0] for k in sharded_names]))}
    for k in ("o_pre_norm", "o_sgu_norm_g", "o_sgu_norm_b", "o_conv_b", "o_conv_norm_g", "o_conv_norm_b", "o_post_norm"):
        p[k] = p[k].reshape(1, -1)
    pool_w_bf = p["e_pool_w"].astype(bf16)
    bias_t = jnp.pad(w["o_sgu_b"].T, ((0, 0), (0, CHUNK - 4)))

    ya = _pool_fwd(z0, pool_w_bf, vec("e_pool_scale"))
    yb, att, lse, qkv_by_residue = _attn_fwd(z0, tables)

    def arrived(index, after, name):
        one = slice(index, index + 1)
        return _forward_halves(_gather_wait(bufs[one], sems[one], after, f"gather_wait_{name}"), f"forward_{name}")[0]

    e_w_out = arrived(1 + E_IN_PIECES, att, "e_w_out").reshape(1, D_MODEL, D_MODEL)
    cat0 = jnp.concatenate([ya, yb], axis=1)
    y0 = _mm_nn(cat0, e_w_out, f32, "e_out")
    x1, h1 = _mid_norm(x, y0, vec("e_post_norm"), p["o_pre_norm"])
    o_w_in = arrived(2 + E_IN_PIECES, h1, "o_w_in")
    z1 = _mm_nn(h1, o_w_in, f32, "o_in")
    yc = _sgu_fwd(z1, p["o_sgu_norm_g"], p["o_sgu_norm_b"], w["o_sgu_w"], bias_t)
    conv = _conv_fwd(z1, p["o_conv_w"], p["o_conv_b"])
    yd = _conv_norm_fwd(conv, z1, p["o_conv_norm_g"], p["o_conv_norm_b"])
    o_w_out = arrived(3 + E_IN_PIECES, yd, "o_w_out").reshape(1, D_MODEL, D_MODEL)
    cat1 = jnp.concatenate([yc, yd], axis=1)
    y1 = _mm_nn(cat1, o_w_out, f32, "o_out")
    loss, dx2, dy1, g_o_post = _final_norm_loss(x1, y1, p["o_post_norm"], target)

    in_flight = {}

    def send_off(name, grad):
        sem, sums, land, tok = _scatter_start(_swap_add(grad, f"swap_add_{name}"), f"scatter_start_{name}")
        in_flight[name] = (sem, sums, land)
        return tok

    tok = send_off("o_w_out", _mm_tn(cat1, dy1, 1, "o_out_dw").reshape(N_CHIPS, HALF // 2, D_MODEL))
    dcat1 = _mm_nt(dy1, [o_w_out], "o_out_dx", tok)
    du, dv, dcg, g_sgu_w, g_bias_t, g_sgu_g, g_sgu_b = _sgu_bwd(
        z1, dcat1, p["o_sgu_norm_g"] + tok[0, 0], p["o_sgu_norm_b"], w["o_sgu_w"], bias_t)
    dconv, ddgate, g_cn_g, g_cn_b = _conv_norm_bwd(conv, z1, dcat1, p["o_conv_norm_g"], p["o_conv_norm_b"])
    ddval, ddglu, g_conv_w, g_conv_b = _conv_bwd(z1, dconv, p["o_conv_w"])
    dz1 = jnp.concatenate([du, dv, dcg, ddval, ddglu, ddgate], axis=1)
    tok = send_off("o_w_in", _mm_tn(h1, dz1, N_CHIPS, "o_in_dw"))
    dh1 = _mm_nt(dz1, [o_w_in], "o_in_dx", tok)
    dx1, dy0, g_o_pre, g_e_post = _mid_norm_bwd(dx2, dh1, x1, y0, p["o_pre_norm"] + tok[0, 0], vec("e_post_norm"))

    tok = send_off("e_w_out", _mm_tn(cat0, dy0, 1, "e_out_dw").reshape(N_CHIPS, HALF // 2, D_MODEL))
    dcat0 = _mm_nt(dy0, [e_w_out], "e_out_dx", tok)
    da, dagate, g_pool_w, g_pool_scale = _pool_bwd(z0, dcat0, pool_w_bf, vec("e_pool_scale") + tok[0, 0])
    dq0, dk0, dv0, dbgate = _attn_bwd_group(0, qkv_by_residue[0], z0, att, lse, dcat0, tables)
    dq1, dk1, dv1 = _attn_bwd_group(1, qkv_by_residue[1], z0, att, lse, dcat0, tables)
    dq2, dk2, dv2 = _attn_bwd_group(2, qkv_by_residue[2], z0, att, lse, dcat0, tables)
    dz0 = jnp.concatenate([da, dagate, dq0, dq1, dq2, dk0, dk1, dk2, dv0, dv1, dv2, dbgate], axis=1)
    tok = send_off("e_w_in", _mm_tn(h0, dz0, N_CHIPS, "e_in_dw"))
    dh0 = _mm_nt(dz0, e_w_in, "e_in_dx", tok)
    grad_x, g_e_pre = _pre_norm_bwd(dx1, dh0, x, vec("e_pre_norm") + tok[0, 0])

    small = {"e_pre_norm": g_e_pre, "e_pool_w": g_pool_w, "e_pool_scale": g_pool_scale, "e_post_norm": g_e_post,
             "o_pre_norm": g_o_pre, "o_sgu_norm_g": g_sgu_g, "o_sgu_norm_b": g_sgu_b, "o_sgu_w": g_sgu_w,
             "o_sgu_b": g_bias_t[:, 0:4].T, "o_conv_w": g_conv_w, "o_conv_b": g_conv_b,
             "o_conv_norm_g": g_cn_g, "o_conv_norm_b": g_cn_b, "o_post_norm": g_o_post}
    return loss, grad_x, in_flight, small


def _land(in_flight, name, chip, after):
    sems, sums, land = in_flight[name]
    sums, land = _scatter_wait(sems, sums, land, after, f"scatter_wait_{name}")
    return _add_landed_join(sums, land, chip.astype(jnp.int32).reshape(1), f"add_landed_{name}")


def _place():
    x, y, c = lax.axis_index("x"), lax.axis_index("y"), lax.axis_index("c")
    others = [(1 - x, y), (x, 1 - y), (1 - x, 1 - y)]
    return x, y, c, 2 * x + y, others


def _all_gather(shards, name):
    n = len(shards)

    def body(*refs):
        ins, outs = refs[:n], refs[n:2 * n]
        send_sems, recv_sems, local_sems = refs[2 * n:]
        x, y, c, me, others = _place()
        sibling = (x, y, 1 - c)

        def half(a, chip, core):
            rows = ins[a].shape[0] // 2
            return outs[a].at[chip, pl.ds(core * rows, rows), :]

        def copy(a, k, src, dst, to):
            return pltpu.make_async_remote_copy(src_ref=src, dst_ref=dst, send_sem=send_sems.at[6 * a + k],
                                                recv_sem=recv_sems.at[6 * a + k], device_id=to, device_id_type=MESH)

        local = [pltpu.make_async_copy(ins[a], outs[a].at[me], local_sems.at[a]) for a in range(n)]
        for cp in local:
            cp.start()
        sent = []
        for a in range(n):
            rows = ins[a].shape[0] // 2
            mine = ins[a].at[pl.ds(c * rows, rows), :]
            for k, (ox, oy) in enumerate(others):
                sent.append(copy(a, k, mine, half(a, me, c), (ox, oy, c)))
                sent[-1].start()
        for a in range(n):
            for k, (ox, oy) in enumerate(others):
                landed = half(a, 2 * ox + oy, c)
                copy(a, k, landed, landed, (ox, oy, c)).wait_recv()
                sent.append(copy(a, 3 + k, landed, landed, sibling))
                sent[-1].start()
        for k, (ox, oy) in enumerate(others):
            chip = 2 * ox + oy
            for a in range(n):
                theirs = half(a, chip, 1 - c)
                copy(a, 3 + k, theirs, theirs, sibling).wait_recv()
        for cp in sent:
            cp.wait_send()
        for cp in local:
            cp.wait()

    return pl.pallas_call(
        body, name=name, in_specs=[ANY] * n, out_specs=[ANY] * n,
        out_shape=[SDS((N_CHIPS,) + s.shape, s.dtype) for s in shards],
        scratch_shapes=[pltpu.SemaphoreType.DMA((6 * n,)), pltpu.SemaphoreType.DMA((6 * n,)), pltpu.SemaphoreType.DMA((n,))],
    )(*shards)


def _swap_halves(parts, name):
    n = len(parts)

    def body(*refs):
        ins, own, theirs = refs[:n], refs[n:2 * n], refs[2 * n:3 * n]
        send_sems, recv_sems, local_sems = refs[3 * n:]
        x, y, c, _, _ = _place()
        sibling = (x, y, 1 - c)
        copies = []
        for a in range(n):
            rows = ins[a].shape[1] // 2
            keep = pltpu.make_async_copy(ins[a].at[:, pl.ds(c * rows, rows), :], own[a], local_sems.at[a])
            give = pltpu.make_async_remote_copy(
                src_ref=ins[a].at[:, pl.ds((1 - c) * rows, rows), :], dst_ref=theirs[a], send_sem=send_sems.at[a],
                recv_sem=recv_sems.at[a], device_id=sibling, device_id_type=MESH)
            keep.start()
            give.start()
            copies += [keep, give]
        for cp in copies:
            cp.wait()

    half = [SDS((N_CHIPS, s.shape[1] // 2, s.shape[2]), s.dtype) for s in parts]
    out = pl.pallas_call(
        body, name=name, in_specs=[ANY] * n, out_specs=[ANY] * (2 * n), out_shape=half + half,
        scratch_shapes=[pltpu.SemaphoreType.DMA((n,)), pltpu.SemaphoreType.DMA((n,)), pltpu.SemaphoreType.DMA((n,))],
    )(*parts)
    return out[:n], out[n:]


def _scatter_chips(parts, name):
    n = len(parts)

    def body(*refs):
        ins, outs = refs[:n], refs[n:2 * n]
        send_sems, recv_sems, local_sems = refs[2 * n:]
        x, y, c, me, others = _place()

        def copy(a, k, slot_from, slot_to, chip_xy):
            return pltpu.make_async_remote_copy(
                src_ref=ins[a].at[slot_from], dst_ref=outs[a].at[slot_to], send_sem=send_sems.at[3 * a + k],
                recv_sem=recv_sems.at[3 * a + k], device_id=(chip_xy[0], chip_xy[1], c), device_id_type=MESH)

        keeps, gives = [], []
        for a in range(n):
            keeps.append(pltpu.make_async_copy(ins[a].at[me], outs[a].at[me], local_sems.at[a]))
            keeps[-1].start()
            for k, (ox, oy) in enumerate(others):
                gives.append(copy(a, k, 2 * ox + oy, me, (ox, oy)))
                gives[-1].start()
        for a in range(n):
            for k, (ox, oy) in enumerate(others):
                copy(a, k, me, 2 * ox + oy, (ox, oy)).wait_recv()
        for cp in gives:
            cp.wait_send()
        for cp in keeps:
            cp.wait()

    return pl.pallas_call(
        body, name=name, in_specs=[ANY] * n, out_specs=[ANY] * n, out_shape=[SDS(s.shape, s.dtype) for s in parts],
        scratch_shapes=[pltpu.SemaphoreType.DMA((3 * n,)), pltpu.SemaphoreType.DMA((3 * n,)), pltpu.SemaphoreType.DMA((n,))],
    )(*parts)


def _join_halves(halves, name):
    n = len(halves)

    def body(*refs):
        ins, outs = refs[:n], refs[n:2 * n]
        send_sems, recv_sems, local_sems = refs[2 * n:]
        x, y, c, _, _ = _place()

        def copy(a, core):
            rows = ins[a].shape[0]
            return pltpu.make_async_remote_copy(
                src_ref=ins[a], dst_ref=outs[a].at[pl.ds(core * rows, rows), :], send_sem=send_sems.at[a],
                recv_sem=recv_sems.at[a], device_id=(x, y, 1 - c), device_id_type=MESH)

        keeps, gives = [], []
        for a in range(n):
            rows = ins[a].shape[0]
            keeps.append(pltpu.make_async_copy(ins[a], outs[a].at[pl.ds(c * rows, rows), :], local_sems.at[a]))
            gives.append(copy(a, c))
            keeps[-1].start()
            gives[-1].start()
        for a in range(n):
            copy(a, 1 - c).wait_recv()
        for cp in gives:
            cp.wait_send()
        for cp in keeps:
            cp.wait()

    return pl.pallas_call(
        body, name=name, in_specs=[ANY] * n, out_specs=[ANY] * n,
        out_shape=[SDS((2 * s.shape[0], s.shape[1]), s.dtype) for s in halves],
        scratch_shapes=[pltpu.SemaphoreType.DMA((n,)), pltpu.SemaphoreType.DMA((n,)), pltpu.SemaphoreType.DMA((n,))],
    )(*halves)


def _add_pair(a, b, name):
    _, r, c = a.shape
    tr = 256 if r % 256 == 0 else r // 2 if r > 512 else r

    def body(a_ref, b_ref, o_ref):
        o_ref[...] = (a_ref[...].astype(f32) + b_ref[...].astype(f32)).astype(o_ref.dtype)

    spec = pl.BlockSpec((None, tr, c), lambda j, i: (j, i, 0))
    return pl.pallas_call(body, name=name, grid=(N_CHIPS, r // tr), in_specs=[spec, spec], out_specs=spec,
                          out_shape=SDS(a.shape, a.dtype))(a, b)


def _add_chips(u, name):
    _, r, c = u.shape
    tr = 256 if r % 256 == 0 else r

    def body(u_ref, o_ref):
        o_ref[...] = ((u_ref[0].astype(f32) + u_ref[1].astype(f32)) + u_ref[2].astype(f32)) + u_ref[3].astype(f32)

    return pl.pallas_call(
        body, name=name, grid=(r // tr,), in_specs=[pl.BlockSpec((N_CHIPS, tr, c), lambda i: (0, i, 0))],
        out_specs=pl.BlockSpec((tr, c), lambda i: (i, 0)), out_shape=SDS((r, c), f32))(u)


SWAP_ROWS = 256
FORWARD_STAGE_BYTES = 4 << 20


def _swap_add(g, name):
    chips, r, c = g.shape
    half = r // 2
    rows_per_step = 2 * SWAP_ROWS if half % (2 * SWAP_ROWS) == 0 else SWAP_ROWS
    nb = half // rows_per_step
    steps = chips * nb

    def body(core_ref, mine_ref, theirs_ref, out_ref, landing, send_sems, recv_sems, free_sems):
        i = pl.program_id(0)
        x, y, core, _, _ = _place()
        sibling = (x, y, 1 - core)

        def send(slot):
            return pltpu.make_async_remote_copy(src_ref=theirs_ref, dst_ref=landing.at[slot], send_sem=send_sems.at[slot],
                                                recv_sem=recv_sems.at[slot], device_id=sibling, device_id_type=MESH)

        @pl.when(i < steps)
        def _():
            @pl.when(i >= 2)
            def _():
                pl.semaphore_wait(free_sems.at[i % 2], 1)

            send(i % 2).start()

        @pl.when(i >= 1)
        def _():
            landed = (i - 1) % 2
            send(landed).wait_recv()
            out_ref[...] = (mine_ref[...].astype(f32) + landing[landed].astype(f32)).astype(out_ref.dtype)

            @pl.when(i + 1 < steps)
            def _():
                pl.semaphore_signal(free_sems.at[landed], 1, device_id=sibling, device_id_type=MESH)

        @pl.when(i < steps)
        def _():
            send(i % 2).wait_send()

    def rows_of(b, h):
        return (2 * (b // nb) + h) * nb + b % nb

    block = (rows_per_step, c)
    grid_spec = pltpu.PrefetchScalarGridSpec(
        num_scalar_prefetch=1, grid=(steps + 1,),
        in_specs=[pl.BlockSpec(block, lambda i, core: (rows_of(jnp.maximum(i - 1, 0), core[0]), 0)),
                  pl.BlockSpec(block, lambda i, core: (rows_of(jnp.minimum(i, steps - 1), 1 - core[0]), 0))],
        out_specs=pl.BlockSpec(block, lambda i, core: (jnp.maximum(i - 1, 0), 0)),
        scratch_shapes=[pltpu.VMEM((2, rows_per_step, c), g.dtype), pltpu.SemaphoreType.DMA((2,)),
                        pltpu.SemaphoreType.DMA((2,)), pltpu.SemaphoreType.REGULAR((2,))])
    core = lax.axis_index("c").astype(jnp.int32).reshape(1)
    rows = g.reshape(chips * r, c)
    out = pl.pallas_call(body, name=name, grid_spec=grid_spec, out_shape=SDS((chips * half, c), g.dtype))(core, rows, rows)
    return out.reshape(chips, half, c)


def _reduce_scatter(parts, tag):
    own, theirs = _swap_halves(parts, f"swap_halves_{tag}")
    chip_sums = [_add_pair(o, t, f"add_cores_{tag}{i}") for i, (o, t) in enumerate(zip(own, theirs))]
    gathered = _scatter_chips(chip_sums, f"scatter_chips_{tag}")
    halves = [_add_chips(u, f"add_chips_{tag}{i}") for i, u in enumerate(gathered)]
    return _join_halves(halves, f"join_halves_{tag}")


HBM = pl.BlockSpec(memory_space=pltpu.HBM)
SEM = pl.BlockSpec(memory_space=pltpu.SEMAPHORE)
EFFECT = pltpu.SideEffectType.DATAFLOW_SIDE_EFFECTING


def _in_hbm(a):
    return pltpu.with_memory_space_constraint(a, pltpu.HBM)


def _cast_into_slot(w, chip, name, after, piece=0, pieces=1):
    r, c = w.shape
    c = c // pieces
    nb = r // SWAP_ROWS

    def body(chip_ref, w_ref, after_ref, o_ref):
        o_ref[...] = w_ref[...].astype(bf16)

    grid_spec = pltpu.PrefetchScalarGridSpec(
        num_scalar_prefetch=1, grid=(nb,),
        in_specs=[pl.BlockSpec((SWAP_ROWS, c), lambda i, chip: (i, piece)), ANY],
        out_specs=pl.BlockSpec((SWAP_ROWS, c), lambda i, chip: (chip[0] * nb + i, 0)))
    out = pl.pallas_call(body, name=name, grid_spec=grid_spec, out_shape=SDS((N_CHIPS * r, c), bf16))(chip, w, after)
    return out.reshape(N_CHIPS, r, c)


def _gather_start(bufs, name):
    n = len(bufs)

    def body(*refs):
        ins, sems, token = refs[:n], refs[n:3 * n], refs[4 * n]
        x, y, c, me, others = _place()
        for a in range(n):
            rows = ins[a].shape[1] // 2
            mine = ins[a].at[me, pl.ds(c * rows, rows), :]
            for k, (ox, oy) in enumerate(others):
                pltpu.make_async_remote_copy(src_ref=mine, dst_ref=mine, send_sem=sems[2 * a].at[k],
                                             recv_sem=sems[2 * a + 1].at[k], device_id=(ox, oy, c),
                                             device_id_type=MESH).start()
        token[...] = jnp.zeros_like(token)

    out = pl.pallas_call(
        body, name=name, in_specs=[HBM] * n,
        out_shape=(*[pltpu.SemaphoreType.DMA((3,))] * (2 * n), *[pltpu.HBM(b.shape, b.dtype) for b in bufs],
                   SDS((8, 128), f32)),
        out_specs=(*[SEM] * (2 * n), *[HBM] * n, pl.BlockSpec(memory_space=pltpu.VMEM)),
        input_output_aliases={a: 2 * n + a for a in range(n)},
        compiler_params=pltpu.CompilerParams(has_side_effects=EFFECT),
    )(*[_in_hbm(b) for b in bufs])
    return [(out[2 * a], out[2 * a + 1]) for a in range(n)], list(out[2 * n:3 * n]), out[3 * n]


def _gather_wait(bufs, sems, after, name):
    n = len(bufs)

    def body(*refs):
        ins, sem_refs = refs[:n], refs[n:3 * n]
        x, y, c, me, others = _place()
        for a in range(n):
            rows = ins[a].shape[1] // 2
            mine = ins[a].at[me, pl.ds(c * rows, rows), :]
            for k, (ox, oy) in enumerate(others):
                landed = ins[a].at[2 * ox + oy, pl.ds(c * rows, rows), :]
                copy = pltpu.make_async_remote_copy(src_ref=mine, dst_ref=landed, send_sem=sem_refs[2 * a].at[k],
                                                    recv_sem=sem_refs[2 * a + 1].at[k], device_id=(ox, oy, c),
                                                    device_id_type=MESH)
                copy.wait_send()
                copy.wait_recv()

    flat_sems = [s for pair in sems for s in pair]
    out = pl.pallas_call(
        body, name=name, in_specs=[HBM] * n + [SEM] * (2 * n) + [ANY],
        out_shape=tuple(pltpu.HBM(b.shape, b.dtype) for b in bufs), out_specs=tuple([HBM] * n),
        input_output_aliases={a: a for a in range(n)},
        compiler_params=pltpu.CompilerParams(has_side_effects=EFFECT),
    )(*bufs, *flat_sems, after)
    return list(out)


def _forward_halves(bufs, name):
    n = len(bufs)
    blocks = []
    for b in bufs:
        half = b.shape[1] // 2
        whole = half * b.shape[2] * b.dtype.itemsize <= FORWARD_STAGE_BYTES
        blocks.append((half, half if whole or half % SWAP_ROWS else SWAP_ROWS))
    work = [(a, k, b) for a in range(n) for k in range(3) for b in range(blocks[a][0] // blocks[a][1])]

    def body(*refs):
        outs, stages = refs[n:2 * n], refs[2 * n:3 * n]
        load_sems, send_sems, recv_sems = refs[3 * n:]
        x, y, c, me, others = _place()
        sibling = (x, y, 1 - c)

        def rows(item):
            a, k, b = item
            half, tr = blocks[a]
            ox, oy = others[k]
            return outs[a].at[2 * ox + oy, pl.ds(c * half + b * tr, tr), :]

        def load(s, item):
            return pltpu.make_async_copy(rows(item), stages[item[0]].at[s], load_sems.at[s])

        def send(s, item):
            return pltpu.make_async_remote_copy(src_ref=stages[item[0]].at[s], dst_ref=rows(item), send_sem=send_sems.at[s],
                                                recv_sem=recv_sems.at[item[0]], device_id=sibling, device_id_type=MESH)

        load(0, work[0]).start()
        for t, item in enumerate(work):
            s = t % 2
            load(s, item).wait()
            send(s, item).start()
            if t + 1 < len(work):
                if t >= 1:
                    send(1 - s, work[t - 1]).wait_send()
                load(1 - s, work[t + 1]).start()
        if len(work) > 1:
            send(len(work) % 2, work[-2]).wait_send()
        send((len(work) - 1) % 2, work[-1]).wait_send()
        for a in range(n):
            theirs = outs[a].at[pl.ds(0, 3), pl.ds(0, blocks[a][0]), :]
            pltpu.make_async_remote_copy(src_ref=theirs, dst_ref=theirs, send_sem=send_sems.at[0], recv_sem=recv_sems.at[a],
                                         device_id=sibling, device_id_type=MESH).wait_recv()

    out = pl.pallas_call(
        body, name=name, in_specs=[ANY] * n, out_specs=[ANY] * n, out_shape=[SDS(b.shape, b.dtype) for b in bufs],
        input_output_aliases={a: a for a in range(n)},
        scratch_shapes=[pltpu.VMEM((2, blocks[a][1], bufs[a].shape[2]), bufs[a].dtype) for a in range(n)]
        + [pltpu.SemaphoreType.DMA((2,)), pltpu.SemaphoreType.DMA((2,)), pltpu.SemaphoreType.DMA((n,))],
    )(*bufs)
    return list(out)


def _scatter_start(chip_sums, name):
    def body(a_ref, land_ref, send_sems, recv_sems, a_thru, land_thru, token):
        x, y, c, me, others = _place()
        for k, (ox, oy) in enumerate(others):
            pltpu.make_async_remote_copy(src_ref=a_ref.at[2 * ox + oy], dst_ref=land_ref.at[me], send_sem=send_sems.at[k],
                                         recv_sem=recv_sems.at[k], device_id=(ox, oy, c), device_id_type=MESH).start()
        token[...] = jnp.zeros_like(token)

    shape = pltpu.HBM(chip_sums.shape, chip_sums.dtype)
    send, recv, a_thru, land, token = pl.pallas_call(
        body, name=name, in_specs=[HBM, HBM],
        out_shape=(pltpu.SemaphoreType.DMA((3,)), pltpu.SemaphoreType.DMA((3,)), shape, shape, SDS((8, 128), f32)),
        out_specs=(SEM, SEM, HBM, HBM, pl.BlockSpec(memory_space=pltpu.VMEM)), input_output_aliases={0: 2, 1: 3},
        compiler_params=pltpu.CompilerParams(has_side_effects=EFFECT),
    )(_in_hbm(chip_sums), _in_hbm(lax.empty(chip_sums.shape, chip_sums.dtype)))
    return (send, recv), a_thru, land, token


def _scatter_wait(sems, chip_sums, land, after, name):
    def body(a_ref, land_ref, send_sems, recv_sems, after_ref, a_out, land_out):
        x, y, c, me, others = _place()
        for k, (ox, oy) in enumerate(others):
            copy = pltpu.make_async_remote_copy(
                src_ref=a_ref.at[2 * ox + oy], dst_ref=land_ref.at[2 * ox + oy], send_sem=send_sems.at[k],
                recv_sem=recv_sems.at[k], device_id=(ox, oy, c), device_id_type=MESH)
            copy.wait_send()
            copy.wait_recv()

    shape = pltpu.HBM(chip_sums.shape, chip_sums.dtype)
    return pl.pallas_call(
        body, name=name, in_specs=[HBM, HBM, SEM, SEM, ANY], out_shape=(shape, shape), out_specs=(HBM, HBM),
        input_output_aliases={0: 0, 1: 1}, compiler_params=pltpu.CompilerParams(has_side_effects=EFFECT),
    )(chip_sums, land, sems[0], sems[1], after)


def _add_landed_join(chip_sums, land, chip, name):
    chips, rh, c = chip_sums.shape
    nb = rh // SWAP_ROWS

    def body(chip_ref, own_ref, l1_ref, l2_ref, l3_ref, out_hbm, buf, send_sems, recv_sem, local_sems):
        i = pl.program_id(0)
        slot = i % 2
        x, y, core, _, _ = _place()
        sibling = (x, y, 1 - core)

        def copies(s, step):
            rows = pl.ds(pl.multiple_of((core * nb + step) * SWAP_ROWS, SWAP_ROWS), SWAP_ROWS)
            keep = pltpu.make_async_copy(buf.at[s], out_hbm.at[rows, :], local_sems.at[s])
            give = pltpu.make_async_remote_copy(src_ref=buf.at[s], dst_ref=out_hbm.at[rows, :], send_sem=send_sems.at[s],
                                                recv_sem=recv_sem.at[0], device_id=sibling, device_id_type=MESH)
            return keep, give

        def drain(s, step):
            keep, give = copies(s, step)
            keep.wait()
            give.wait_send()

        @pl.when(i >= 2)
        def _():
            drain(slot, i - 2)

        buf[slot] = ((own_ref[...].astype(f32) + l1_ref[...].astype(f32)) + l2_ref[...].astype(f32)) + l3_ref[...].astype(f32)
        keep, give = copies(slot, i)
        keep.start()
        give.start()

        @pl.when(i == nb - 1)
        def _():
            drain(slot, i)
            if nb > 1:
                drain(1 - slot, i - 1)
            theirs = out_hbm.at[pl.ds((1 - core) * rh, rh), :]
            pltpu.make_async_remote_copy(src_ref=theirs, dst_ref=theirs, send_sem=send_sems.at[0], recv_sem=recv_sem.at[0],
                                         device_id=sibling, device_id_type=MESH).wait_recv()

    block = (SWAP_ROWS, c)
    from_slot = lambda d: pl.BlockSpec(block, lambda i, chip: (((chip[0] + d) % chips) * nb + i, 0))
    grid_spec = pltpu.PrefetchScalarGridSpec(
        num_scalar_prefetch=1, grid=(nb,), in_specs=[from_slot(0), from_slot(1), from_slot(2), from_slot(3)],
        out_specs=ANY,
        scratch_shapes=[pltpu.VMEM((2, SWAP_ROWS, c), f32), pltpu.SemaphoreType.DMA((2,)),
                        pltpu.SemaphoreType.DMA((1,)), pltpu.SemaphoreType.DMA((2,))])
    land_rows = land.reshape(chips * rh, c)
    return pl.pallas_call(body, name=name, grid_spec=grid_spec, out_shape=SDS((2 * rh, c), f32))(
        chip, chip_sums.reshape(chips * rh, c), land_rows, land_rows, land_rows)


def _adamw_update(w_ref, g_ref, m_ref, v_ref, d_ref, nm_ref, nv_ref):
    g = g_ref[...]
    nm = ADAM_B1 * m_ref[...] + (1.0 - ADAM_B1) * g
    nv = ADAM_B2 * v_ref[...] + (1.0 - ADAM_B2) * (g * g)
    nm_ref[...] = nm
    nv_ref[...] = nv
    m_hat = nm / (1.0 - ADAM_B1 ** ADAM_STEP)
    v_hat = nv / (1.0 - ADAM_B2 ** ADAM_STEP)
    d_ref[...] = -ADAM_LR * (m_hat / (jnp.sqrt(v_hat) + ADAM_EPS) + ADAM_WD * w_ref[...])


def _adamw(w, g, m, v, name):
    r, c = w.shape
    tr = 128 if r % 128 == 0 else r

    def body(w_ref, g_ref, m_ref, v_ref, g_out_ref, d_ref, nm_ref, nv_ref):
        g_out_ref[...] = g_ref[...]
        _adamw_update(w_ref, g_ref, m_ref, v_ref, d_ref, nm_ref, nv_ref)

    spec = pl.BlockSpec((tr, c), lambda i: (i, 0))
    return pl.pallas_call(body, name=name, grid=(r // tr,), in_specs=[spec] * 4, out_specs=[spec] * 4,
                          out_shape=[SDS((r, c), f32)] * 4)(w, g, m, v)


def _adamw_small(ws, gs, ms, vs):
    n = len(ws)

    def body(*refs):
        for i in range(n):
            _adamw_update(*refs[i:7 * n:n])

    whole = pl.BlockSpec(memory_space=pltpu.VMEM)
    out = pl.pallas_call(body, name="adamw_small", in_specs=[whole] * (4 * n), out_specs=[whole] * (3 * n),
                         out_shape=[SDS(a.shape, f32) for a in ws] * 3)(*ws, *gs, *ms, *vs)
    return out[:n], out[n:2 * n], out[2 * n:]


def _pack(arrays, total_rows=None):
    parts = []
    rows = 0
    for a in arrays:
        flat = a.reshape(-1, LANES)
        pad = -flat.shape[0] % 8
        parts.append(jnp.pad(flat, ((0, pad), (0, 0))))
        rows += flat.shape[0] + pad
    if total_rows is not None:
        parts.append(jnp.zeros((total_rows - rows, LANES), arrays[0].dtype))
    return jnp.concatenate(parts, axis=0)


def _unpack(buf, shapes):
    out = []
    row = 0
    lead = buf.shape[:-2]
    for shape in shapes:
        size = 1
        for s in shape:
            size *= s
        rows = size // LANES
        out.append(buf[..., row:row + rows, :].reshape(lead + tuple(shape)))
        row += rows + (-rows % 8)
    return out


BIG = ("e_w_in", "e_w_out", "o_w_in", "o_w_out")
SHARDED_SMALL = {
    "e_pool_w": ((4, 64, 256), 1), "o_pre_norm": ((512,), 0), "o_sgu_norm_g": ((256,), 0), "o_sgu_norm_b": ((256,), 0),
    "o_conv_w": ((31, 256), 1), "o_conv_b": ((256,), 0), "o_conv_norm_g": ((256,), 0), "o_conv_norm_b": ((256,), 0),
    "o_post_norm": ((512,), 0),
}
REPLICATED_SMALL = {"e_pre_norm": (2048,), "e_pool_scale": (1024,), "e_post_norm": (2048,),
                    "o_sgu_w": (4, 128, 128), "o_sgu_b": (4, 128)}
SMALL_ORDER = ("e_pre_norm", "e_pool_w", "e_pool_scale", "e_post_norm", "o_pre_norm", "o_sgu_norm_g", "o_sgu_norm_b",
               "o_sgu_w", "o_sgu_b", "o_conv_w", "o_conv_b", "o_conv_norm_g", "o_conv_norm_b", "o_post_norm")
ALL_ORDER = ("e_pre_norm", "e_w_in", "e_pool_w", "e_pool_scale", "e_w_out", "e_post_norm", "o_pre_norm", "o_w_in",
             "o_sgu_norm_g", "o_sgu_norm_b", "o_sgu_w", "o_sgu_b", "o_conv_w", "o_conv_b", "o_conv_norm_g",
             "o_conv_norm_b", "o_w_out", "o_post_norm")


def _full_shape(name):
    shape, axis = SHARDED_SMALL[name]
    return tuple(s * N_CHIPS if i == axis else s for i, s in enumerate(shape))


def _from_chips(name, stacked):
    shape, axis = SHARDED_SMALL[name]
    return jnp.moveaxis(stacked, 0, axis).reshape(_full_shape(name))


def _my_shard(name, full, chip):
    shape, axis = SHARDED_SMALL[name]
    return lax.dynamic_slice_in_dim(full, chip * shape[axis], shape[axis], axis)


def kernel(x, e_pre_norm, e_w_in, e_pool_w, e_pool_scale, e_w_out, e_post_norm, o_pre_norm, o_w_in, o_sgu_norm_g, o_sgu_norm_b, o_sgu_w, o_sgu_b, o_conv_w, o_conv_b, o_conv_norm_g, o_conv_norm_b, o_w_out, o_post_norm, loss_target, m_e_pre_norm, m_e_w_in, m_e_pool_w, m_e_pool_scale, m_e_w_out, m_e_post_norm, m_o_pre_norm, m_o_w_in, m_o_sgu_norm_g, m_o_sgu_norm_b, m_o_sgu_w, m_o_sgu_b, m_o_conv_w, m_o_conv_b, m_o_conv_norm_g, m_o_conv_norm_b, m_o_w_out, m_o_post_norm, v_e_pre_norm, v_e_w_in, v_e_pool_w, v_e_pool_scale, v_e_w_out, v_e_post_norm, v_o_pre_norm, v_o_w_in, v_o_sgu_norm_g, v_o_sgu_norm_b, v_o_sgu_w, v_o_sgu_b, v_o_conv_w, v_o_conv_b, v_o_conv_norm_g, v_o_conv_norm_b, v_o_w_out, v_o_post_norm):
    w = dict(e_pre_norm=e_pre_norm, e_w_in=e_w_in, e_pool_w=e_pool_w, e_pool_scale=e_pool_scale, e_w_out=e_w_out,
             e_post_norm=e_post_norm, o_pre_norm=o_pre_norm, o_w_in=o_w_in, o_sgu_norm_g=o_sgu_norm_g,
             o_sgu_norm_b=o_sgu_norm_b, o_sgu_w=o_sgu_w, o_sgu_b=o_sgu_b, o_conv_w=o_conv_w, o_conv_b=o_conv_b,
             o_conv_norm_g=o_conv_norm_g, o_conv_norm_b=o_conv_norm_b, o_w_out=o_w_out, o_post_norm=o_post_norm)
    m = dict(e_pre_norm=m_e_pre_norm, e_w_in=m_e_w_in, e_pool_w=m_e_pool_w, e_pool_scale=m_e_pool_scale,
             e_w_out=m_e_w_out, e_post_norm=m_e_post_norm, o_pre_norm=m_o_pre_norm, o_w_in=m_o_w_in,
             o_sgu_norm_g=m_o_sgu_norm_g, o_sgu_norm_b=m_o_sgu_norm_b, o_sgu_w=m_o_sgu_w, o_sgu_b=m_o_sgu_b,
             o_conv_w=m_o_conv_w, o_conv_b=m_o_conv_b, o_conv_norm_g=m_o_conv_norm_g, o_conv_norm_b=m_o_conv_norm_b,
             o_w_out=m_o_w_out, o_post_norm=m_o_post_norm)
    v = dict(e_pre_norm=v_e_pre_norm, e_w_in=v_e_w_in, e_pool_w=v_e_pool_w, e_pool_scale=v_e_pool_scale,
             e_w_out=v_e_w_out, e_post_norm=v_e_post_norm, o_pre_norm=v_o_pre_norm, o_w_in=v_o_w_in,
             o_sgu_norm_g=v_o_sgu_norm_g, o_sgu_norm_b=v_o_sgu_norm_b, o_sgu_w=v_o_sgu_w, o_sgu_b=v_o_sgu_b,
             o_conv_w=v_o_conv_w, o_conv_b=v_o_conv_b, o_conv_norm_g=v_o_conv_norm_g, o_conv_norm_b=v_o_conv_norm_b,
             o_w_out=v_o_w_out, o_post_norm=v_o_post_norm)
    w, m, v = ({k: a[0] for k, a in d.items()} for d in (w, m, v))
    chip = 2 * lax.axis_index("x") + lax.axis_index("y")

    loss, grad_x, in_flight, small = _step(x[0], loss_target[0], w, chip)

    grads, delta, new_m, new_v = {}, {}, {}, {}
    after = grad_x
    for k in ("o_w_out", "o_w_in", "e_w_out", "e_w_in"):
        grads[k], delta[k], new_m[k], new_v[k] = _adamw(w[k], _land(in_flight, k, chip, after), m[k], v[k], f"adamw_{k}")
        after = delta[k]

    small_full_shapes = {k: (_full_shape(k) if k in SHARDED_SMALL else REPLICATED_SMALL[k]) for k in SMALL_ORDER}
    small_parts = _pack([small[k].reshape(small_full_shapes[k]) for k in SMALL_ORDER], total_rows=SMALL_GRAD_ROWS)
    small_parts = small_parts + 0.0 * after[0, 0]
    reduced = _reduce_scatter([small_parts.reshape(N_CHIPS, SMALL_GRAD_ROWS // N_CHIPS, LANES)], "small")
    small_sum = _all_gather(reduced, "gather_small_grads")[0].reshape(SMALL_GRAD_ROWS, LANES)
    for k, a in zip(SMALL_ORDER, _unpack(small_sum, [small_full_shapes[k] for k in SMALL_ORDER])):
        grads[k] = _my_shard(k, a, chip) if k in SHARDED_SMALL else a
    loss = lax.psum(loss[0, 0], ("x", "y", "c"))

    def rows_of(a):
        return a.reshape(-1, a.shape[-1])

    updates = _adamw_small(*[[rows_of(d[k]) for k in SMALL_ORDER] for d in (w, grads, m, v)])
    for d, arrays in zip((delta, new_m, new_v), updates):
        for k, a in zip(SMALL_ORDER, arrays):
            d[k] = a.reshape(w[k].shape)

    outs = [loss, grad_x[None]]
    for d in (grads, delta, new_m, new_v):
        outs += [d[k][None] for k in ALL_ORDER]
    return tuple(outs)
```

```python
import jax
import jax.numpy as jnp
from jax import lax
from jax.experimental import pallas as pl
from jax.experimental.pallas import tpu as pltpu

f32 = jnp.float32
bf16 = jnp.bfloat16
SDS = jax.ShapeDtypeStruct

SEQ = 2048
D_MODEL = 2048
EPS = 1e-6
NEG = -1e30
HEAD_DIM = 128
ROT_HALF = 16
ROPE_THETA = 500000.0
DILATIONS = (1, 4, 16)
SPAN = 128
N_HEADS = 8
HALF = 1024
POOL_CH = 256
CONV_K = 31
CONV_PAD = 32
CHUNK = 128
N_CHIPS = 4
LANES = 256
E_IN_PIECES = 3
SMALL_SHARD_ROWS = 352
ANY = pl.BlockSpec(memory_space=pl.ANY)
MESH = pl.DeviceIdType.MESH

ADAM_LR = 0.001
ADAM_B1 = 0.9
ADAM_B2 = 0.999
ADAM_EPS = 1e-08
ADAM_WD = 0.01
ADAM_STEP = 10


def _dot(a, b):
    return jnp.dot(a, b, preferred_element_type=f32)


def _dot_nt(a, b):
    return lax.dot_general(a, b, (((1,), (1,)), ((), ())), preferred_element_type=f32)


def _dot_tn(a, b):
    return lax.dot_general(a, b, (((0,), (0,)), ((), ())), preferred_element_type=f32)


def _sigmoid(x):
    return 1.0 / (1.0 + jnp.exp(-x))


def _silu_and_grad(x):
    s = _sigmoid(x)
    return x * s, s * (1.0 + x * (1.0 - s))


def _rms_fwd(x, g):
    r = lax.rsqrt(jnp.mean(x * x, axis=-1, keepdims=True) + EPS)
    return x * r * g


def _rms_bwd(x, g, dout):
    r = lax.rsqrt(jnp.mean(x * x, axis=-1, keepdims=True) + EPS)
    xh = x * r
    dg = jnp.sum(dout * xh, axis=0, keepdims=True)
    dxh = dout * g
    dx = r * (dxh - xh * jnp.mean(dxh * xh, axis=-1, keepdims=True))
    return dx, dg


def _ln_stats(x):
    mu = jnp.mean(x, axis=-1, keepdims=True)
    xc = x - mu
    rstd = lax.rsqrt(jnp.mean(xc * xc, axis=-1, keepdims=True) + EPS)
    return xc * rstd, rstd


def _ln_bwd(xh, rstd, g, dout):
    dg = jnp.sum(dout * xh, axis=0, keepdims=True)
    db = jnp.sum(dout, axis=0, keepdims=True)
    dxh = dout * g
    dx = rstd * (dxh - jnp.mean(dxh, axis=-1, keepdims=True) - xh * jnp.mean(dxh * xh, axis=-1, keepdims=True))
    return dx, dg, db


def _accumulate(ref, value, first):
    @pl.when(first)
    def _():
        ref[...] = value

    @pl.when(jnp.logical_not(first))
    def _():
        ref[...] += value


def _col_tile(ns):
    for t in (1024, 768, 512, 256):
        if ns % t == 0:
            return t
    raise ValueError(ns)


def _mm_nn(a, w, out_dtype, name, piece=0, pieces=1, into=None):
    m, k = a.shape
    j, _, ns = w.shape
    tm, tn = m, _col_tile(ns)
    nb = ns // tn

    def body(a_ref, w_ref, *rest):
        rest[-1][...] = _dot(a_ref[...], w_ref[...]).astype(rest[-1].dtype)

    return pl.pallas_call(
        body, name=name, grid=(j * nb, m // tm),
        in_specs=[pl.BlockSpec((tm, k), lambda n, i: (i, 0)),
                  pl.BlockSpec((None, k, tn), lambda n, i: (n // nb, 0, n % nb))] + ([] if into is None else [ANY]),
        out_specs=pl.BlockSpec((tm, tn), lambda n, i: (i, ((n // nb) * pieces + piece) * nb + n % nb)),
        out_shape=SDS((m, j * ns * pieces), out_dtype),
        input_output_aliases={} if into is None else {2: 0},
    )(a, w, *([] if into is None else [into]))


def _mm_nt(dz, ws, name, after):
    m, _ = dz.shape
    pieces = len(ws)
    j, k, ns = ws[0].shape
    tm, tk = 1024, 1024

    def body(dz_ref, *rest):
        w_refs, o_ref = rest[:pieces], rest[-1]
        total = _dot_nt(dz_ref[:, 0:ns], w_refs[0][...])
        for q in range(1, pieces):
            total = total + _dot_nt(dz_ref[:, q * ns:(q + 1) * ns], w_refs[q][...])
        _accumulate(o_ref, total, pl.program_id(2) == 0)

    return pl.pallas_call(
        body, name=name, grid=(m // tm, k // tk, j),
        in_specs=[pl.BlockSpec((tm, pieces * ns), lambda i, kk, r: (i, r))]
        + [pl.BlockSpec((None, tk, ns), lambda i, kk, r: (r, kk, 0))] * pieces + [ANY],
        out_specs=pl.BlockSpec((tm, tk), lambda i, kk, r: (i, kk)),
        out_shape=SDS((m, k), f32),
    )(dz, *ws, after)


def _mm_tn(a, dz, j, name):
    m, k = a.shape
    ns = dz.shape[1] // j
    tk, tn = 1024, _col_tile(ns)
    nb = ns // tn

    def body(a_ref, dz_ref, o_ref):
        o_ref[...] = _dot_tn(a_ref[...], dz_ref[...]).astype(o_ref.dtype)

    return pl.pallas_call(
        body, name=name, grid=(k // tk, j * nb),
        in_specs=[pl.BlockSpec((m, tk), lambda kk, n: (0, kk)),
                  pl.BlockSpec((m, tn), lambda kk, n: (0, n))],
        out_specs=pl.BlockSpec((None, tk, tn), lambda kk, n: (n // nb, kk, n % nb)),
        out_shape=SDS((j, k, ns), bf16),
    )(a, dz)


ROWS = 256


def _row_spec(width=D_MODEL, col=0):
    return pl.BlockSpec((ROWS, width), lambda i: (i, col))


def _vec_spec(width=D_MODEL):
    return pl.BlockSpec((1, width), lambda i: (0, 0))


def _pre_norm(x, g):
    def body(x_ref, g_ref, h_ref):
        h_ref[...] = _rms_fwd(x_ref[...], g_ref[...]).astype(bf16)

    return pl.pallas_call(
        body, name="pre_norm", grid=(SEQ // ROWS,), in_specs=[_row_spec(), _vec_spec()],
        out_specs=_row_spec(), out_shape=SDS((SEQ, D_MODEL), bf16))(x, g)


def _mid_norm(x, y, g_post, g_pre):
    def body(x_ref, y_ref, gpost_ref, gpre_ref, x1_ref, h1_ref):
        x1 = x_ref[...] + _rms_fwd(y_ref[...], gpost_ref[...])
        x1_ref[...] = x1
        h1_ref[...] = _rms_fwd(x1, gpre_ref[...]).astype(bf16)

    return pl.pallas_call(
        body, name="mid_norm", grid=(SEQ // ROWS,),
        in_specs=[_row_spec(), _row_spec(), _vec_spec(), _vec_spec()],
        out_specs=[_row_spec(), _row_spec()],
        out_shape=[SDS((SEQ, D_MODEL), f32), SDS((SEQ, D_MODEL), bf16)])(x, y, g_post, g_pre)


def _final_norm_loss(x1, y, g_post, target):
    def body(x1_ref, y_ref, g_ref, t_ref, loss_ref, dx2_ref, dy_ref, dg_ref):
        first = pl.program_id(0) == 0
        y = y_ref[...]
        g = g_ref[...]
        err = x1_ref[...] + _rms_fwd(y, g) - t_ref[...]
        sq = jnp.sum(jnp.sum(err * err, axis=1, keepdims=True), axis=0, keepdims=True)
        _accumulate(loss_ref, sq * (0.5 / D_MODEL), first)
        dx2 = err * (1.0 / D_MODEL)
        dx2_ref[...] = dx2
        dy, dg = _rms_bwd(y, g, dx2)
        dy_ref[...] = dy.astype(bf16)
        _accumulate(dg_ref, dg, first)

    return pl.pallas_call(
        body, name="final_norm_loss", grid=(SEQ // ROWS,),
        in_specs=[_row_spec(), _row_spec(), _vec_spec(), _row_spec()],
        out_specs=[pl.BlockSpec((1, 1), lambda i: (0, 0)), _row_spec(), _row_spec(), _vec_spec()],
        out_shape=[SDS((1, 1), f32), SDS((SEQ, D_MODEL), f32), SDS((SEQ, D_MODEL), bf16), SDS((1, D_MODEL), f32)],
    )(x1, y, g_post, target)


def _mid_norm_bwd(dx2, dh1, x1, y0, g_pre, g_post):
    def body(dx2_ref, dh1_ref, x1_ref, y0_ref, gpre_ref, gpost_ref, dx1_ref, dy0_ref, dgpre_ref, dgpost_ref):
        first = pl.program_id(0) == 0
        d_in, dgpre = _rms_bwd(x1_ref[...], gpre_ref[...], dh1_ref[...])
        dx1 = dx2_ref[...] + d_in
        dx1_ref[...] = dx1
        dy0, dgpost = _rms_bwd(y0_ref[...], gpost_ref[...], dx1)
        dy0_ref[...] = dy0.astype(bf16)
        _accumulate(dgpre_ref, dgpre, first)
        _accumulate(dgpost_ref, dgpost, first)

    return pl.pallas_call(
        body, name="mid_norm_bwd", grid=(SEQ // ROWS,),
        in_specs=[_row_spec(), _row_spec(), _row_spec(), _row_spec(), _vec_spec(), _vec_spec()],
        out_specs=[_row_spec(), _row_spec(), _vec_spec(), _vec_spec()],
        out_shape=[SDS((SEQ, D_MODEL), f32), SDS((SEQ, D_MODEL), bf16), SDS((1, D_MODEL), f32), SDS((1, D_MODEL), f32)],
    )(dx2, dh1, x1, y0, g_pre, g_post)


def _pre_norm_bwd(dx1, dh0, x, g):
    def body(dx1_ref, dh0_ref, x_ref, g_ref, dx_ref, dg_ref):
        d_in, dg = _rms_bwd(x_ref[...], g_ref[...], dh0_ref[...])
        dx_ref[...] = dx1_ref[...] + d_in
        _accumulate(dg_ref, dg, pl.program_id(0) == 0)

    return pl.pallas_call(
        body, name="pre_norm_bwd", grid=(SEQ // ROWS,),
        in_specs=[_row_spec(), _row_spec(), _row_spec(), _vec_spec()],
        out_specs=[_row_spec(), _vec_spec()],
        out_shape=[SDS((SEQ, D_MODEL), f32), SDS((1, D_MODEL), f32)])(dx1, dh0, x, g)


def _pool_count(g):
    row = lax.broadcasted_iota(jnp.int32, (SEQ, 1), 0)
    width = jnp.left_shift(2, g)
    return row, width, jnp.minimum(row + 1, width).astype(f32)


def _trailing_sum(x, row, width):
    s = x
    for k in (1, 2, 4, 8):
        shifted = jnp.where(row >= k, pltpu.roll(s, k, 0), 0.0)
        s = jnp.where(width > k, s + shifted, s)
    return s


def _leading_sum(x, row, width):
    s = x
    for k in (1, 2, 4, 8):
        shifted = jnp.where(row < SEQ - k, pltpu.roll(s, SEQ - k, 0), 0.0)
        s = jnp.where(width > k, s + shifted, s)
    return s


def _pool_specs():
    a_in = pl.BlockSpec((SEQ, POOL_CH), lambda g: (0, g))
    a_gate = pl.BlockSpec((SEQ, POOL_CH), lambda g: (0, 4 + g))
    w = pl.BlockSpec((None, POOL_CH, POOL_CH), lambda g: (g, 0, 0))
    scale = pl.BlockSpec((1, POOL_CH), lambda g: (0, g))
    return a_in, a_gate, w, scale


def _pool_fwd(z0, pool_w, pool_scale):
    def body(a_ref, gate_ref, w_ref, scale_ref, ya_ref):
        row, width, count = _pool_count(pl.program_id(0))
        a = a_ref[...]
        pooled = _trailing_sum(a, row, width) / count - a
        mixed = _dot(pooled.astype(bf16), w_ref[...]) * scale_ref[...]
        gate = gate_ref[...]
        ya_ref[...] = (mixed * gate * _sigmoid(gate)).astype(bf16)

    return pl.pallas_call(
        body, name="pool_fwd", grid=(4,), in_specs=list(_pool_specs()),
        out_specs=pl.BlockSpec((SEQ, POOL_CH), lambda g: (0, g)),
        out_shape=SDS((SEQ, HALF), bf16))(z0, z0, pool_w, pool_scale)


def _pool_bwd(z0, dcat, pool_w, pool_scale):
    def body(a_ref, gate_ref, w_ref, scale_ref, dya_ref, da_ref, dgate_ref, dw_ref, dscale_ref):
        row, width, count = _pool_count(pl.program_id(0))
        a = a_ref[...]
        pooled = (_trailing_sum(a, row, width) / count - a).astype(bf16)
        w = w_ref[...]
        scale = scale_ref[...]
        mixed = _dot(pooled, w)
        silu, dsilu = _silu_and_grad(gate_ref[...])
        dya = dya_ref[...]
        dgate_ref[...] = (dya * mixed * scale * dsilu).astype(bf16)
        dms = dya * silu
        dscale_ref[...] = jnp.sum(dms * mixed, axis=0, keepdims=True)
        dmixed = (dms * scale).astype(bf16)
        dw_ref[...] = _dot_tn(pooled, dmixed)
        dpooled = _dot_nt(dmixed, w)
        da_ref[...] = (_leading_sum(dpooled / count, row, width) - dpooled).astype(bf16)

    a_in, a_gate, w, scale = _pool_specs()
    col = pl.BlockSpec((SEQ, POOL_CH), lambda g: (0, g))
    return pl.pallas_call(
        body, name="pool_bwd", grid=(4,), in_specs=[a_in, a_gate, w, scale, col],
        out_specs=[col, col, w, scale],
        out_shape=[SDS((SEQ, HALF), bf16), SDS((SEQ, HALF), bf16), SDS((4, POOL_CH, POOL_CH), f32), SDS((1, HALF), f32)],
    )(z0, z0, pool_w, pool_scale, dcat)


Q_COL, K_COL, V_COL, BGATE_COL = 16, 40, 64, 88


def _rope_tables():
    pos = jnp.arange(SEQ, dtype=f32)
    inv_freq = jnp.power(ROPE_THETA, -jnp.arange(0, 2 * ROT_HALF, 2, dtype=f32) / (2 * ROT_HALF))
    ang = pos[:, None] * inv_freq[None, :]
    cos, sin = jnp.cos(ang), jnp.sin(ang)
    zeros = jnp.zeros((SEQ, HEAD_DIM - 2 * ROT_HALF), f32)
    cos_t = jnp.concatenate([cos, cos, zeros + 1.0], axis=1)
    sin_t = jnp.concatenate([sin, sin, zeros], axis=1)
    j = jnp.arange(HEAD_DIM)[:, None]
    i = jnp.arange(HEAD_DIM)[None, :]
    rot = jnp.where((i < ROT_HALF) & (j == i + ROT_HALF), -1.0, 0.0) + jnp.where(
        (i >= ROT_HALF) & (i < 2 * ROT_HALF) & (j == i - ROT_HALF), 1.0, 0.0)
    return cos_t, sin_t, rot.astype(bf16), rot.T.astype(bf16)


def _exact_dot(t, m):
    hi = t.astype(bf16)
    lo = (t - hi.astype(f32)).astype(bf16)
    return _dot(hi, m) + _dot(lo, m)


def _rope(t, cos_t, sin_t, rot):
    return t * cos_t + _exact_dot(t, rot) * sin_t


def _rope_transposed(d, cos_t, sin_t, rot_t):
    return d * cos_t + _exact_dot(d * sin_t, rot_t)


ROW_CHUNK = 256


def _chunks(fn):
    def step(i, carry):
        fn(pl.multiple_of(i * ROW_CHUNK, ROW_CHUNK))
        return carry

    lax.fori_loop(0, SEQ // ROW_CHUNK, step, 0, unroll=2)


def _pieces(dilation):
    length = SEQ // dilation
    n = min(length, ROW_CHUNK)
    return [(r, l0, n) for r in range(dilation) for l0 in range(0, length, n)]


def _by_residue(dst_ref, src_ref, dilation, dtype):
    length = SEQ // dilation
    for r, l0, n in _pieces(dilation):
        src = src_ref[l0:l0 + n, :] if dilation == 1 else src_ref[pl.ds(r + dilation * l0, n, stride=dilation), :]
        start = r * length + l0
        dst_ref[start:start + n, :] = src.astype(dtype)


def _by_position(dst_ref, src_ref, dilation):
    length = SEQ // dilation
    for r, l0, n in _pieces(dilation):
        src = src_ref[r * length + l0:r * length + l0 + n, :]
        if dilation == 1:
            dst_ref[l0:l0 + n, :] = src
        else:
            dst_ref[pl.ds(r + dilation * l0, n, stride=dilation), :] = src


def _attn_masks():
    qi = lax.broadcasted_iota(jnp.int32, (SPAN, 2 * SPAN), 0)
    kj = lax.broadcasted_iota(jnp.int32, (SPAN, 2 * SPAN), 1)
    window = ((kj < SPAN) & (kj >= qi)) | ((kj >= SPAN) & (kj - SPAN <= qi))
    own = lax.broadcasted_iota(jnp.int32, (SPAN, SPAN), 1) <= lax.broadcasted_iota(jnp.int32, (SPAN, SPAN), 0)
    return window, own


def _attn_blocks(dilation):
    per_residue = SEQ // dilation // SPAN
    blocks = [(c, c % per_residue != 0) for c in range(SEQ // SPAN)]
    return [blocks[i:i + 4] for i in range(0, len(blocks), 4)]


def _block_keys(c, has_prev):
    return slice((c - 1) * SPAN if has_prev else c * SPAN, (c + 1) * SPAN)


def _head_spec(col):
    return pl.BlockSpec((SEQ, HEAD_DIM), lambda h: (0, col + h))


def _table_spec():
    return pl.BlockSpec((SEQ, HEAD_DIM), lambda h: (0, 0))


def _attn_fwd(z0, tables):
    scale = HEAD_DIM ** -0.5

    def body(*refs):
        qkv = refs[0:9]
        bg_ref, cos_ref, sin_ref, rot_ref = refs[9:13]
        yb_ref, att_ref, lse_ref = refs[13:16]
        saved = refs[16:25]
        tmp_q, tmp_k, v_ones, o_res, l_res, o_nat, l_nat = refs[25:32]
        window_mask, own_mask = _attn_masks()
        rot = rot_ref[...]

        @pl.when(pl.program_id(0) == 0)
        def _():
            v_ones[:, HEAD_DIM:] = jnp.ones((SEQ, HEAD_DIM), bf16)

        for g, dilation in enumerate(DILATIONS):
            q_ref, k_ref, v_ref = qkv[3 * g:3 * g + 3]
            qd, kd, vd = saved[3 * g:3 * g + 3]

            def rope_rows(start, q_ref=q_ref, k_ref=k_ref):
                r = pl.ds(start, ROW_CHUNK)
                cos_t, sin_t = cos_ref[r, :], sin_ref[r, :]
                tmp_q[r, :] = _rope(q_ref[r, :], cos_t, sin_t, rot) * scale
                tmp_k[r, :] = _rope(k_ref[r, :], cos_t, sin_t, rot)

            _chunks(rope_rows)
            _by_residue(qd, tmp_q, dilation, bf16)
            _by_residue(kd, tmp_k, dilation, bf16)
            _by_residue(vd, v_ref, dilation, bf16)
            for l0 in range(0, SEQ, ROW_CHUNK):
                v_ones[l0:l0 + ROW_CHUNK, 0:HEAD_DIM] = vd[l0:l0 + ROW_CHUNK, :]

            for four in _attn_blocks(dilation):
                scores = [_dot_nt(qd[c * SPAN:(c + 1) * SPAN, :], kd[_block_keys(c, prev), :]) for c, prev in four]
                tops, probs = [], []
                for (c, prev), s in zip(four, scores):
                    s = jnp.where(window_mask if prev else own_mask, s, NEG)
                    tops.append(jnp.max(s, axis=1, keepdims=True))
                    probs.append(jnp.exp(s - tops[-1]).astype(bf16))
                sums = [_dot(p, v_ones[_block_keys(c, prev), :]) for (c, prev), p in zip(four, probs)]
                for (c, prev), m, o in zip(four, tops, sums):
                    den = o[:, HEAD_DIM:]
                    o_res[c * SPAN:(c + 1) * SPAN, :] = o[:, :HEAD_DIM] / den
                    l_res[c * SPAN:(c + 1) * SPAN, :] = m + jnp.log(den)

            if dilation > 1:
                _by_position(o_nat, o_res, dilation)
                _by_position(l_nat, l_res, dilation)
            o_g, l_g = (o_res, l_res) if dilation == 1 else (o_nat, l_nat)

            def merge(start, g=g, o_g=o_g, l_g=l_g):
                r = pl.ds(start, ROW_CHUNK)
                if g == 0:
                    att, total = o_g[r, :], l_g[r, :]
                else:
                    l_old, l_new = lse_ref[r, :], l_g[r, :]
                    top = jnp.maximum(l_old, l_new)
                    total = top + jnp.log(jnp.exp(l_old - top) + jnp.exp(l_new - top))
                    att = att_ref[r, :] * jnp.exp(l_old - total) + o_g[r, :] * jnp.exp(l_new - total)
                att_ref[r, :] = att
                lse_ref[r, :] = total
                if g == len(DILATIONS) - 1:
                    gate = bg_ref[r, :]
                    yb_ref[r, :] = (att * gate * _sigmoid(gate)).astype(bf16)

            _chunks(merge)

    in_specs = []
    for g in range(3):
        in_specs += [_head_spec(Q_COL + 8 * g), _head_spec(K_COL + 8 * g), _head_spec(V_COL + 8 * g)]
    in_specs += [_head_spec(BGATE_COL), _table_spec(), _table_spec(), pl.BlockSpec((HEAD_DIM, HEAD_DIM), lambda h: (0, 0))]
    out_spec = pl.BlockSpec((SEQ, HEAD_DIM), lambda h: (0, h))
    vm = lambda dt: pltpu.VMEM((SEQ, HEAD_DIM), dt)
    cos_t, sin_t, rot, _ = tables
    out = pl.pallas_call(
        body, name="attn_fwd", grid=(N_HEADS,), in_specs=in_specs, out_specs=[out_spec] * 12,
        out_shape=[SDS((SEQ, HALF), bf16), SDS((SEQ, HALF), f32), SDS((SEQ, HALF), f32)] + [SDS((SEQ, HALF), bf16)] * 9,
        scratch_shapes=[vm(f32), vm(f32), pltpu.VMEM((SEQ, 2 * HEAD_DIM), bf16), vm(f32), vm(f32), vm(f32), vm(f32)],
    )(*([z0] * 10), cos_t, sin_t, rot)
    return out[0], out[1], out[2], [tuple(out[3 + 3 * g:6 + 3 * g]) for g in range(3)]


def _attn_bwd_group(g, saved, z0, att, lse, dcat, tables):
    scale = HEAD_DIM ** -0.5
    dilation = DILATIONS[g]
    with_gate = g == 0

    def body(*refs):
        qd, kd, vd, bg_ref, att_ref, lse_ref, dyb_ref, cos_ref, sin_ref, rot_t_ref = refs[0:10]
        n_out = 4 if with_gate else 3
        dq_ref, dk_ref, dv_ref = refs[10:13]
        dod, ld, dd, tmp, aq, ak, av = refs[10 + n_out:17 + n_out]
        window_mask, own_mask = _attn_masks()
        rot_t = rot_t_ref[...]

        def gate_rows(start):
            r = pl.ds(start, ROW_CHUNK)
            silu, dsilu = _silu_and_grad(bg_ref[r, :])
            att_v = att_ref[r, :]
            dyb = dyb_ref[r, :]
            if with_gate:
                refs[13][r, :] = (dyb * att_v * dsilu).astype(bf16)
            datt = dyb * silu
            tmp[r, :] = datt
            aq[r, :] = jnp.broadcast_to(jnp.sum(datt * att_v, axis=1, keepdims=True), (ROW_CHUNK, HEAD_DIM))

        _chunks(gate_rows)
        _by_residue(dod, tmp, dilation, bf16)
        _by_residue(dd, aq, dilation, f32)
        _by_residue(ld, lse_ref, dilation, f32)

        for four in _attn_blocks(dilation):
            rows = [slice(c * SPAN, (c + 1) * SPAN) for c, _ in four]
            keys = [_block_keys(c, prev) for c, prev in four]
            scores = [_dot_nt(qd[r, :], kd[k, :]) for r, k in zip(rows, keys)]
            dprobs = [_dot_nt(dod[r, :], vd[k, :]) for r, k in zip(rows, keys)]
            probs, dscores = [], []
            for (c, prev), r, s, dp in zip(four, rows, scores, dprobs):
                lse_q, delta = ld[r, :], dd[r, :]
                if prev:
                    lse_q = jnp.concatenate([lse_q, lse_q], axis=1)
                    delta = jnp.concatenate([delta, delta], axis=1)
                p = jnp.where(window_mask if prev else own_mask, jnp.exp(s - lse_q), 0.0)
                probs.append(p.astype(bf16))
                dscores.append((p * (dp - delta)).astype(bf16))
            dvs = [_dot_tn(p, dod[r, :]) for p, r in zip(probs, rows)]
            dks = [_dot_tn(ds, qd[r, :]) for ds, r in zip(dscores, rows)]
            dqs = [_dot(ds, kd[k, :]) for ds, k in zip(dscores, keys)]
            for (c, prev), r, dv, dk, dq in zip(four, rows, dvs, dks, dqs):
                aq[r, :] = dq
                if prev:
                    before = slice((c - 1) * SPAN, c * SPAN)
                    av[before, :] += dv[0:SPAN]
                    ak[before, :] += dk[0:SPAN]
                    av[r, :] = dv[SPAN:]
                    ak[r, :] = dk[SPAN:]
                else:
                    av[r, :] = dv
                    ak[r, :] = dk

        def finish(out_ref, acc, factor, roped):
            if dilation > 1:
                _by_position(tmp, acc, dilation)
            src = acc if dilation == 1 else tmp

            def rows(start):
                r = pl.ds(start, ROW_CHUNK)
                d = src[r, :]
                if factor != 1.0:
                    d = d * factor
                if roped:
                    d = _rope_transposed(d, cos_ref[r, :], sin_ref[r, :], rot_t)
                out_ref[r, :] = d.astype(bf16)

            _chunks(rows)

        finish(dq_ref, aq, scale, True)
        finish(dk_ref, ak, 1.0, True)
        finish(dv_ref, av, 1.0, False)

    head = pl.BlockSpec((SEQ, HEAD_DIM), lambda h: (0, h))
    in_specs = [head, head, head, _head_spec(BGATE_COL), head, head, _head_spec(8), _table_spec(), _table_spec(),
                pl.BlockSpec((HEAD_DIM, HEAD_DIM), lambda h: (0, 0))]
    n_out = 4 if with_gate else 3
    vm = lambda dt: pltpu.VMEM((SEQ, HEAD_DIM), dt)
    cos_t, sin_t, _, rot_t = tables
    return pl.pallas_call(
        body, name=f"attn_bwd_g{g}", grid=(N_HEADS,), in_specs=in_specs, out_specs=[head] * n_out,
        out_shape=[SDS((SEQ, HALF), bf16)] * n_out,
        scratch_shapes=[vm(bf16), vm(f32), vm(f32), vm(f32), vm(f32), vm(f32), vm(f32)],
    )(*saved, z0, att, lse, dcat, cos_t, sin_t, rot_t)


def _sgu_specs():
    chunk = lambda col: pl.BlockSpec((CHUNK, HALF), lambda n: (n, col))
    vec = pl.BlockSpec((1, HALF), lambda n: (0, 0))
    w = pl.BlockSpec((4, CHUNK, CHUNK), lambda n: (0, 0, 0))
    bias = pl.BlockSpec((CHUNK, CHUNK), lambda n: (0, 0))
    return chunk, vec, w, bias


def _sgu_weights(w_ref):
    tril = lax.broadcasted_iota(jnp.int32, (CHUNK, CHUNK), 1) <= lax.broadcasted_iota(jnp.int32, (CHUNK, CHUNK), 0)
    return tril, [jnp.where(tril, w_ref[h], 0.0).astype(bf16) for h in range(4)]


def _sgu_fwd(z1, ln_g, ln_b, sgu_w, bias_t):
    def body(u_ref, v_ref, cg_ref, g_ref, b_ref, w_ref, bias_ref, yc_ref):
        _, ws = _sgu_weights(w_ref)
        xh, _ = _ln_stats(v_ref[...])
        vn = (xh * g_ref[...] + b_ref[...]).astype(bf16)
        for h in range(4):
            cols = slice(h * POOL_CH, (h + 1) * POOL_CH)
            s = _dot(ws[h], vn[:, cols]) + bias_ref[:, h:h + 1]
            gate = cg_ref[:, cols]
            yc_ref[:, cols] = (u_ref[:, cols] * s * gate * _sigmoid(gate)).astype(bf16)

    chunk, vec, w, bias = _sgu_specs()
    return pl.pallas_call(
        body, name="sgu_fwd", grid=(SEQ // CHUNK,),
        in_specs=[chunk(0), chunk(1), chunk(2), vec, vec, w, bias], out_specs=chunk(0),
        out_shape=SDS((SEQ, HALF), bf16))(z1, z1, z1, ln_g, ln_b, sgu_w, bias_t)


def _sgu_bwd(z1, dcat, ln_g, ln_b, sgu_w, bias_t):
    def body(u_ref, v_ref, cg_ref, dyc_ref, g_ref, b_ref, w_ref, bias_ref,
             du_ref, dv_ref, dcg_ref, dw_ref, dbias_ref, dg_ref, db_ref, dvn_ref):
        first = pl.program_id(0) == 0
        tril, ws = _sgu_weights(w_ref)
        xh, rstd = _ln_stats(v_ref[...])
        g = g_ref[...]
        vn = (xh * g + b_ref[...]).astype(bf16)

        @pl.when(first)
        def _():
            dbias_ref[...] = jnp.zeros((CHUNK, CHUNK), f32)

        for h in range(4):
            cols = slice(h * POOL_CH, (h + 1) * POOL_CH)
            vn_h = vn[:, cols]
            s = _dot(ws[h], vn_h) + bias_ref[:, h:h + 1]
            silu, dsilu = _silu_and_grad(cg_ref[:, cols])
            dyc = dyc_ref[:, cols]
            u = u_ref[:, cols]
            du_ref[:, cols] = (dyc * s * silu).astype(bf16)
            dcg_ref[:, cols] = (dyc * u * s * dsilu).astype(bf16)
            ds = dyc * u * silu
            dbias_ref[:, h:h + 1] += jnp.sum(ds, axis=1, keepdims=True)
            ds = ds.astype(bf16)
            _accumulate(dw_ref.at[h], jnp.where(tril, _dot_nt(ds, vn_h), 0.0), first)
            dvn_ref[:, cols] = _dot_tn(ws[h], ds)
        dv, dg, db = _ln_bwd(xh, rstd, g, dvn_ref[...])
        dv_ref[...] = dv.astype(bf16)
        _accumulate(dg_ref, dg, first)
        _accumulate(db_ref, db, first)

    chunk, vec, w, bias = _sgu_specs()
    return pl.pallas_call(
        body, name="sgu_bwd", grid=(SEQ // CHUNK,),
        in_specs=[chunk(0), chunk(1), chunk(2), chunk(0), vec, vec, w, bias],
        out_specs=[chunk(0), chunk(0), chunk(0), w, bias, vec, vec],
        out_shape=[SDS((SEQ, HALF), bf16)] * 3 + [SDS((4, CHUNK, CHUNK), f32), SDS((CHUNK, CHUNK), f32),
                                                   SDS((1, HALF), f32), SDS((1, HALF), f32)],
        scratch_shapes=[pltpu.VMEM((CHUNK, HALF), f32)],
    )(z1, z1, z1, dcat, ln_g, ln_b, sgu_w, bias_t)


CONV_TILE = 128
DVAL_COL, DGLU_COL = 12, 16


def _conv_specs():
    val = pl.BlockSpec((SEQ, POOL_CH), lambda j: (0, DVAL_COL + j))
    glu = pl.BlockSpec((SEQ, POOL_CH), lambda j: (0, DGLU_COL + j))
    w = pl.BlockSpec((CONV_K, POOL_CH), lambda j: (0, j))
    col = pl.BlockSpec((SEQ, POOL_CH), lambda j: (0, j))
    vec = pl.BlockSpec((1, POOL_CH), lambda j: (0, j))
    return val, glu, w, col, vec


def _conv_fwd(z1, conv_w, conv_b):
    def body(val_ref, glu_ref, w_ref, b_ref, out_ref, xpad):
        xpad[0:CONV_PAD, :] = jnp.zeros((CONV_PAD, POOL_CH), f32)
        xpad[CONV_PAD:, :] = val_ref[...] * _sigmoid(glu_ref[...])
        w = w_ref[...]
        bias = b_ref[...]

        def tile(i, carry):
            t0 = pl.multiple_of(i * CONV_TILE, CONV_TILE)
            window = xpad[pl.ds(t0, CONV_TILE + CONV_PAD), :]
            acc = jnp.broadcast_to(bias, (CONV_TILE, POOL_CH))
            for k in range(CONV_K):
                shift = CONV_PAD - (CONV_K - 1) + k
                acc = acc + w[k:k + 1, :] * pltpu.roll(window, CONV_TILE + CONV_PAD - shift, 0)[0:CONV_TILE]
            out_ref[pl.ds(t0, CONV_TILE), :] = acc
            return carry

        lax.fori_loop(0, SEQ // CONV_TILE, tile, 0)

    val, glu, w, col, vec = _conv_specs()
    return pl.pallas_call(
        body, name="conv_fwd", grid=(4,), in_specs=[val, glu, w, vec], out_specs=col,
        out_shape=SDS((SEQ, HALF), f32), scratch_shapes=[pltpu.VMEM((SEQ + CONV_PAD, POOL_CH), f32)],
    )(z1, z1, conv_w, conv_b)


def _conv_bwd(z1, dconv, conv_w):
    def body(val_ref, glu_ref, w_ref, dout_ref, dval_ref, dglu_ref, dw_ref, db_ref, xpad, dpad, dx_ref):
        val = val_ref[...]
        sig = _sigmoid(glu_ref[...])
        xpad[0:CONV_PAD, :] = jnp.zeros((CONV_PAD, POOL_CH), f32)
        xpad[CONV_PAD:, :] = val * sig
        dout = dout_ref[...]
        dpad[0:SEQ, :] = dout
        dpad[SEQ:, :] = jnp.zeros((CONV_PAD, POOL_CH), f32)
        db_ref[...] = jnp.sum(dout, axis=0, keepdims=True)
        dw_ref[...] = jnp.zeros((CONV_K, POOL_CH), f32)
        w = w_ref[...]

        def tile(i, carry):
            t0 = pl.multiple_of(i * CONV_TILE, CONV_TILE)
            x_win = xpad[pl.ds(t0, CONV_TILE + CONV_PAD), :]
            d_win = dpad[pl.ds(t0, CONV_TILE + CONV_PAD), :]
            d_own = d_win[0:CONV_TILE]
            acc = jnp.zeros((CONV_TILE, POOL_CH), f32)
            for k in range(CONV_K):
                shift = CONV_PAD - (CONV_K - 1) + k
                x_k = pltpu.roll(x_win, CONV_TILE + CONV_PAD - shift, 0)[0:CONV_TILE]
                dw_ref[k:k + 1, :] += jnp.sum(d_own * x_k, axis=0, keepdims=True)
                back = CONV_K - 1 - k
                d_k = d_own if back == 0 else pltpu.roll(d_win, CONV_TILE + CONV_PAD - back, 0)[0:CONV_TILE]
                acc = acc + w[k:k + 1, :] * d_k
            dx_ref[pl.ds(t0, CONV_TILE), :] = acc
            return carry

        lax.fori_loop(0, SEQ // CONV_TILE, tile, 0)
        dx = dx_ref[...]
        dval_ref[...] = (dx * sig).astype(bf16)
        dglu_ref[...] = (dx * val * sig * (1.0 - sig)).astype(bf16)

    val, glu, w, col, vec = _conv_specs()
    pad = pltpu.VMEM((SEQ + CONV_PAD, POOL_CH), f32)
    return pl.pallas_call(
        body, name="conv_bwd", grid=(4,), in_specs=[val, glu, w, col], out_specs=[col, col, w, vec],
        out_shape=[SDS((SEQ, HALF), bf16), SDS((SEQ, HALF), bf16), SDS((CONV_K, HALF), f32), SDS((1, HALF), f32)],
        scratch_shapes=[pad, pad, pltpu.VMEM((SEQ, POOL_CH), f32)],
    )(z1, z1, conv_w, dconv)


DGATE_COL = 5


def _conv_norm_fwd(conv, z1, g, b):
    def body(c_ref, gate_ref, g_ref, b_ref, yd_ref):
        xh, _ = _ln_stats(c_ref[...])
        n = xh * g_ref[...] + b_ref[...]
        gate = gate_ref[...]
        yd_ref[...] = (n * _sigmoid(n) * gate * _sigmoid(gate)).astype(bf16)

    return pl.pallas_call(
        body, name="conv_norm_fwd", grid=(SEQ // ROWS,),
        in_specs=[_row_spec(HALF), _row_spec(HALF, DGATE_COL), _vec_spec(HALF), _vec_spec(HALF)],
        out_specs=_row_spec(HALF), out_shape=SDS((SEQ, HALF), bf16))(conv, z1, g, b)


def _conv_norm_bwd(conv, z1, dcat, g, b):
    def body(c_ref, gate_ref, dyd_ref, g_ref, b_ref, dconv_ref, dgate_ref, dg_ref, db_ref):
        first = pl.program_id(0) == 0
        xh, rstd = _ln_stats(c_ref[...])
        g = g_ref[...]
        n_silu, n_dsilu = _silu_and_grad(xh * g + b_ref[...])
        gate_silu, gate_dsilu = _silu_and_grad(gate_ref[...])
        dyd = dyd_ref[...]
        dgate_ref[...] = (dyd * n_silu * gate_dsilu).astype(bf16)
        dconv, dg, db = _ln_bwd(xh, rstd, g, dyd * gate_silu * n_dsilu)
        dconv_ref[...] = dconv
        _accumulate(dg_ref, dg, first)
        _accumulate(db_ref, db, first)

    return pl.pallas_call(
        body, name="conv_norm_bwd", grid=(SEQ // ROWS,),
        in_specs=[_row_spec(HALF), _row_spec(HALF, DGATE_COL), _row_spec(HALF, 1), _vec_spec(HALF), _vec_spec(HALF)],
        out_specs=[_row_spec(HALF), _row_spec(HALF), _vec_spec(HALF), _vec_spec(HALF)],
        out_shape=[SDS((SEQ, HALF), f32), SDS((SEQ, HALF), bf16), SDS((1, HALF), f32), SDS((1, HALF), f32)],
    )(conv, z1, dcat, g, b)


def _step(x, target, w, chip):
    chip_vec = chip.astype(jnp.int32).reshape(1)
    sharded_names = list(SHARDED_SMALL)
    small_shard = _pack([w[k] for k in sharded_names], total_rows=SMALL_SHARD_ROWS)
    small_slot = lax.dynamic_update_slice(jnp.zeros((N_CHIPS, SMALL_SHARD_ROWS, LANES), f32), small_shard[None], (chip, 0, 0))
    first = [small_slot, _cast_into_slot(w["e_w_in"], chip_vec, "cast_e_w_in0", small_slot, 0, E_IN_PIECES)]
    sems, bufs, token = _gather_start(first, "gather_start_first")
    more = [_cast_into_slot(w["e_w_in"], chip_vec, f"cast_e_w_in{i}", token, i, E_IN_PIECES) for i in range(1, E_IN_PIECES)]
    more_sems, more_bufs, token = _gather_start(more, "gather_start_pieces")
    rest = [_cast_into_slot(w[k], chip_vec, f"cast_{k}", token) for k in BIG[1:]]
    rest_sems, rest_bufs, token = _gather_start(rest, "gather_start_rest")
    sems, bufs = sems + more_sems + rest_sems, bufs + more_bufs + rest_bufs
    tables = _rope_tables()

    def vec(k):
        return w[k].reshape(1, -1)

    h0 = _pre_norm(x, vec("e_pre_norm") + token[0, 0])
    after, z0, e_w_in = h0, None, []
    for i in range(E_IN_PIECES):
        group = slice(0, 2) if i == 0 else slice(1 + i, 2 + i)
        landed = _forward_halves(_gather_wait(bufs[group], sems[group], after, f"gather_wait_{i}"), f"forward_{i}")
        if i == 0:
            small_full = landed[0]
        e_w_in.append(landed[-1])
        z0 = _mm_nn(h0, landed[-1], f32, f"e_in{i}", i, E_IN_PIECES, z0)
        after = z0
    p = {k: _from_chips(k, a) for k, a in zip(sharded_names, _unpack(small_full, [SHARDED_SMALL[k][0] for k in sharded_names]))}
    for k in ("o_pre_norm", "o_sgu_norm_g", "o_sgu_norm_b", "o_conv_b", "o_conv_norm_g", "o_conv_norm_b", "o_post_norm"):
        p[k] = p[k].reshape(1, -1)
    pool_w_bf = p["e_pool_w"].astype(bf16)
    bias_t = jnp.pad(w["o_sgu_b"].T, ((0, 0), (0, CHUNK - 4)))

    ya = _pool_fwd(z0, pool_w_bf, vec("e_pool_scale"))
    yb, att, lse, qkv_by_residue = _attn_fwd(z0, tables)

    def arrived(index, after, name):
        one = slice(index, index + 1)
        return _forward_halves(_gather_wait(bufs[one], sems[one], after, f"gather_wait_{name}"), f"forward_{name}")[0]

    e_w_out = arrived(1 + E_IN_PIECES, att, "e_w_out").reshape(1, D_MODEL, D_MODEL)
    cat0 = jnp.concatenate([ya, yb], axis=1)
    y0 = _mm_nn(cat0, e_w_out, f32, "e_out")
    x1, h1 = _mid_norm(x, y0, vec("e_post_norm"), p["o_pre_norm"])
    o_w_in = arrived(2 + E_IN_PIECES, h1, "o_w_in")
    z1 = _mm_nn(h1, o_w_in, f32, "o_in")
    yc = _sgu_fwd(z1, p["o_sgu_norm_g"], p["o_sgu_norm_b"], w["o_sgu_w"], bias_t)
    conv = _conv_fwd(z1, p["o_conv_w"], p["o_conv_b"])
    yd = _conv_norm_fwd(conv, z1, p["o_conv_norm_g"], p["o_conv_norm_b"])
    o_w_out = arrived(3 + E_IN_PIECES, yd, "o_w_out").reshape(1, D_MODEL, D_MODEL)
    cat1 = jnp.concatenate([yc, yd], axis=1)
    y1 = _mm_nn(cat1, o_w_out, f32, "o_out")
    loss, dx2, dy1, g_o_post = _final_norm_loss(x1, y1, p["o_post_norm"], target)

    in_flight = {}

    def send_off(name, grad):
        sem, sums, land, tok = _scatter_start(_swap_add(grad, f"swap_add_{name}"), f"scatter_start_{name}")
        in_flight[name] = (sem, sums, land)
        return tok

    tok = send_off("o_w_out", _mm_tn(cat1, dy1, 1, "o_out_dw").reshape(N_CHIPS, HALF // 2, D_MODEL))
    dcat1 = _mm_nt(dy1, [o_w_out], "o_out_dx", tok)
    du, dv, dcg, g_sgu_w, g_bias_t, g_sgu_g, g_sgu_b = _sgu_bwd(
        z1, dcat1, p["o_sgu_norm_g"] + tok[0, 0], p["o_sgu_norm_b"], w["o_sgu_w"], bias_t)
    dconv, ddgate, g_cn_g, g_cn_b = _conv_norm_bwd(conv, z1, dcat1, p["o_conv_norm_g"], p["o_conv_norm_b"])
    ddval, ddglu, g_conv_w, g_conv_b = _conv_bwd(z1, dconv, p["o_conv_w"])
    dz1 = jnp.concatenate([du, dv, dcg, ddval, ddglu, ddgate], axis=1)
    tok = send_off("o_w_in", _mm_tn(h1, dz1, N_CHIPS, "o_in_dw"))
    dh1 = _mm_nt(dz1, [o_w_in], "o_in_dx", tok)
    dx1, dy0, g_o_pre, g_e_post = _mid_norm_bwd(dx2, dh1, x1, y0, p["o_pre_norm"] + tok[0, 0], vec("e_post_norm"))

    tok = send_off("e_w_out", _mm_tn(cat0, dy0, 1, "e_out_dw").reshape(N_CHIPS, HALF // 2, D_MODEL))
    dcat0 = _mm_nt(dy0, [e_w_out], "e_out_dx", tok)
    da, dagate, g_pool_w, g_pool_scale = _pool_bwd(z0, dcat0, pool_w_bf, vec("e_pool_scale") + tok[0, 0])
    dq0, dk0, dv0, dbgate = _attn_bwd_group(0, qkv_by_residue[0], z0, att, lse, dcat0, tables)
    dq1, dk1, dv1 = _attn_bwd_group(1, qkv_by_residue[1], z0, att, lse, dcat0, tables)
    dq2, dk2, dv2 = _attn_bwd_group(2, qkv_by_residue[2], z0, att, lse, dcat0, tables)
    dz0 = jnp.concatenate([da, dagate, dq0, dq1, dq2, dk0, dk1, dk2, dv0, dv1, dv2, dbgate], axis=1)
    tok = send_off("e_w_in", _mm_tn(h0, dz0, N_CHIPS, "e_in_dw"))
    dh0 = _mm_nt(dz0, e_w_in, "e_in_dx", tok)
    grad_x, g_e_pre = _pre_norm_bwd(dx1, dh0, x, vec("e_pre_norm") + tok[0, 0])

    small = {"e_pre_norm": g_e_pre, "e_pool_w": g_pool_w, "e_pool_scale": g_pool_scale, "e_post_norm": g_e_post,
             "o_pre_norm": g_o_pre, "o_sgu_norm_g": g_sgu_g, "o_sgu_norm_b": g_sgu_b, "o_sgu_w": g_sgu_w,
             "o_sgu_b": g_bias_t, "o_conv_w": g_conv_w, "o_conv_b": g_conv_b,
             "o_conv_norm_g": g_cn_g, "o_conv_norm_b": g_cn_b, "o_post_norm": g_o_post}
    return loss, grad_x, in_flight, small


def _land(in_flight, name, chip, after):
    sems, sums, land = in_flight[name]
    sums, land = _scatter_wait(sems, sums, land, after, f"scatter_wait_{name}")
    return _add_landed_join(sums, land, chip.astype(jnp.int32).reshape(1), f"add_landed_{name}")


def _place():
    x, y, c = lax.axis_index("x"), lax.axis_index("y"), lax.axis_index("c")
    others = [(1 - x, y), (x, 1 - y), (1 - x, 1 - y)]
    return x, y, c, 2 * x + y, others


SWAP_ROWS = 256
FORWARD_STAGE_BYTES = 4 << 20


def _swap_add(g, name):
    chips, r, c = g.shape
    half = r // 2
    rows_per_step = 2 * SWAP_ROWS if half % (2 * SWAP_ROWS) == 0 else SWAP_ROWS
    nb = half // rows_per_step
    steps = chips * nb

    def body(core_ref, mine_ref, theirs_ref, out_ref, landing, send_sems, recv_sems, free_sems):
        i = pl.program_id(0)
        x, y, core, _, _ = _place()
        sibling = (x, y, 1 - core)

        def send(slot):
            return pltpu.make_async_remote_copy(src_ref=theirs_ref, dst_ref=landing.at[slot], send_sem=send_sems.at[slot],
                                                recv_sem=recv_sems.at[slot], device_id=sibling, device_id_type=MESH)

        @pl.when(i < steps)
        def _():
            @pl.when(i >= 2)
            def _():
                pl.semaphore_wait(free_sems.at[i % 2], 1)

            send(i % 2).start()

        @pl.when(i >= 1)
        def _():
            landed = (i - 1) % 2
            send(landed).wait_recv()
            out_ref[...] = (mine_ref[...].astype(f32) + landing[landed].astype(f32)).astype(out_ref.dtype)

            @pl.when(i + 1 < steps)
            def _():
                pl.semaphore_signal(free_sems.at[landed], 1, device_id=sibling, device_id_type=MESH)

        @pl.when(i < steps)
        def _():
            send(i % 2).wait_send()

    def rows_of(b, h):
        return (2 * (b // nb) + h) * nb + b % nb

    block = (rows_per_step, c)
    grid_spec = pltpu.PrefetchScalarGridSpec(
        num_scalar_prefetch=1, grid=(steps + 1,),
        in_specs=[pl.BlockSpec(block, lambda i, core: (rows_of(jnp.maximum(i - 1, 0), core[0]), 0)),
                  pl.BlockSpec(block, lambda i, core: (rows_of(jnp.minimum(i, steps - 1), 1 - core[0]), 0))],
        out_specs=pl.BlockSpec(block, lambda i, core: (jnp.maximum(i - 1, 0), 0)),
        scratch_shapes=[pltpu.VMEM((2, rows_per_step, c), g.dtype), pltpu.SemaphoreType.DMA((2,)),
                        pltpu.SemaphoreType.DMA((2,)), pltpu.SemaphoreType.REGULAR((2,))])
    core = lax.axis_index("c").astype(jnp.int32).reshape(1)
    rows = g.reshape(chips * r, c)
    out = pl.pallas_call(body, name=name, grid_spec=grid_spec, out_shape=SDS((chips * half, c), g.dtype))(core, rows, rows)
    return out.reshape(chips, half, c)


HBM = pl.BlockSpec(memory_space=pltpu.HBM)
SEM = pl.BlockSpec(memory_space=pltpu.SEMAPHORE)
EFFECT = pltpu.SideEffectType.DATAFLOW_SIDE_EFFECTING


def _in_hbm(a):
    return pltpu.with_memory_space_constraint(a, pltpu.HBM)


def _cast_into_slot(w, chip, name, after, piece=0, pieces=1):
    r, c = w.shape
    c = c // pieces
    nb = r // SWAP_ROWS

    def body(chip_ref, w_ref, after_ref, o_ref):
        o_ref[...] = w_ref[...].astype(bf16)

    grid_spec = pltpu.PrefetchScalarGridSpec(
        num_scalar_prefetch=1, grid=(nb,),
        in_specs=[pl.BlockSpec((SWAP_ROWS, c), lambda i, chip: (i, piece)), ANY],
        out_specs=pl.BlockSpec((SWAP_ROWS, c), lambda i, chip: (chip[0] * nb + i, 0)))
    out = pl.pallas_call(body, name=name, grid_spec=grid_spec, out_shape=SDS((N_CHIPS * r, c), bf16))(chip, w, after)
    return out.reshape(N_CHIPS, r, c)


def _gather_start(bufs, name):
    n = len(bufs)

    def body(*refs):
        ins, sems, token = refs[:n], refs[n:3 * n], refs[4 * n]
        x, y, c, me, others = _place()
        for a in range(n):
            rows = ins[a].shape[1] // 2
            mine = ins[a].at[me, pl.ds(c * rows, rows), :]
            for k, (ox, oy) in enumerate(others):
                pltpu.make_async_remote_copy(src_ref=mine, dst_ref=mine, send_sem=sems[2 * a].at[k],
                                             recv_sem=sems[2 * a + 1].at[k], device_id=(ox, oy, c),
                                             device_id_type=MESH).start()
        token[...] = jnp.zeros_like(token)

    out = pl.pallas_call(
        body, name=name, in_specs=[HBM] * n,
        out_shape=(*[pltpu.SemaphoreType.DMA((3,))] * (2 * n), *[pltpu.HBM(b.shape, b.dtype) for b in bufs],
                   SDS((8, 128), f32)),
        out_specs=(*[SEM] * (2 * n), *[HBM] * n, pl.BlockSpec(memory_space=pltpu.VMEM)),
        input_output_aliases={a: 2 * n + a for a in range(n)},
        compiler_params=pltpu.CompilerParams(has_side_effects=EFFECT),
    )(*[_in_hbm(b) for b in bufs])
    return [(out[2 * a], out[2 * a + 1]) for a in range(n)], list(out[2 * n:3 * n]), out[3 * n]


def _gather_wait(bufs, sems, after, name):
    n = len(bufs)

    def body(*refs):
        ins, sem_refs = refs[:n], refs[n:3 * n]
        x, y, c, me, others = _place()
        for a in range(n):
            rows = ins[a].shape[1] // 2
            mine = ins[a].at[me, pl.ds(c * rows, rows), :]
            for k, (ox, oy) in enumerate(others):
                landed = ins[a].at[2 * ox + oy, pl.ds(c * rows, rows), :]
                copy = pltpu.make_async_remote_copy(src_ref=mine, dst_ref=landed, send_sem=sem_refs[2 * a].at[k],
                                                    recv_sem=sem_refs[2 * a + 1].at[k], device_id=(ox, oy, c),
                                                    device_id_type=MESH)
                copy.wait_send()
                copy.wait_recv()

    flat_sems = [s for pair in sems for s in pair]
    out = pl.pallas_call(
        body, name=name, in_specs=[HBM] * n + [SEM] * (2 * n) + [ANY],
        out_shape=tuple(pltpu.HBM(b.shape, b.dtype) for b in bufs), out_specs=tuple([HBM] * n),
        input_output_aliases={a: a for a in range(n)},
        compiler_params=pltpu.CompilerParams(has_side_effects=EFFECT),
    )(*bufs, *flat_sems, after)
    return list(out)


def _forward_halves(bufs, name):
    n = len(bufs)
    blocks = []
    for b in bufs:
        half = b.shape[1] // 2
        whole = half * b.shape[2] * b.dtype.itemsize <= FORWARD_STAGE_BYTES
        blocks.append((half, half if whole or half % SWAP_ROWS else SWAP_ROWS))
    work = [(a, k, b) for a in range(n) for k in range(3) for b in range(blocks[a][0] // blocks[a][1])]

    def body(*refs):
        outs, stages = refs[n:2 * n], refs[2 * n:3 * n]
        load_sems, send_sems, recv_sems = refs[3 * n:]
        x, y, c, me, others = _place()
        sibling = (x, y, 1 - c)

        def rows(item):
            a, k, b = item
            half, tr = blocks[a]
            ox, oy = others[k]
            return outs[a].at[2 * ox + oy, pl.ds(c * half + b * tr, tr), :]

        def load(s, item):
            return pltpu.make_async_copy(rows(item), stages[item[0]].at[s], load_sems.at[s])

        def send(s, item):
            return pltpu.make_async_remote_copy(src_ref=stages[item[0]].at[s], dst_ref=rows(item), send_sem=send_sems.at[s],
                                                recv_sem=recv_sems.at[item[0]], device_id=sibling, device_id_type=MESH)

        load(0, work[0]).start()
        for t, item in enumerate(work):
            s = t % 2
            load(s, item).wait()
            send(s, item).start()
            if t + 1 < len(work):
                if t >= 1:
                    send(1 - s, work[t - 1]).wait_send()
                load(1 - s, work[t + 1]).start()
        if len(work) > 1:
            send(len(work) % 2, work[-2]).wait_send()
        send((len(work) - 1) % 2, work[-1]).wait_send()
        for a in range(n):
            theirs = outs[a].at[pl.ds(0, 3), pl.ds(0, blocks[a][0]), :]
            pltpu.make_async_remote_copy(src_ref=theirs, dst_ref=theirs, send_sem=send_sems.at[0], recv_sem=recv_sems.at[a],
                                         device_id=sibling, device_id_type=MESH).wait_recv()

    out = pl.pallas_call(
        body, name=name, in_specs=[ANY] * n, out_specs=[ANY] * n, out_shape=[SDS(b.shape, b.dtype) for b in bufs],
        input_output_aliases={a: a for a in range(n)},
        scratch_shapes=[pltpu.VMEM((2, blocks[a][1], bufs[a].shape[2]), bufs[a].dtype) for a in range(n)]
        + [pltpu.SemaphoreType.DMA((2,)), pltpu.SemaphoreType.DMA((2,)), pltpu.SemaphoreType.DMA((n,))],
    )(*bufs)
    return list(out)


def _scatter_start(chip_sums, name):
    def body(a_ref, land_ref, send_sems, recv_sems, a_thru, land_thru, token):
        x, y, c, me, others = _place()
        for k, (ox, oy) in enumerate(others):
            pltpu.make_async_remote_copy(src_ref=a_ref.at[2 * ox + oy], dst_ref=land_ref.at[me], send_sem=send_sems.at[k],
                                         recv_sem=recv_sems.at[k], device_id=(ox, oy, c), device_id_type=MESH).start()
        token[...] = jnp.zeros_like(token)

    shape = pltpu.HBM(chip_sums.shape, chip_sums.dtype)
    send, recv, a_thru, land, token = pl.pallas_call(
        body, name=name, in_specs=[HBM, HBM],
        out_shape=(pltpu.SemaphoreType.DMA((3,)), pltpu.SemaphoreType.DMA((3,)), shape, shape, SDS((8, 128), f32)),
        out_specs=(SEM, SEM, HBM, HBM, pl.BlockSpec(memory_space=pltpu.VMEM)), input_output_aliases={0: 2, 1: 3},
        compiler_params=pltpu.CompilerParams(has_side_effects=EFFECT),
    )(_in_hbm(chip_sums), _in_hbm(lax.empty(chip_sums.shape, chip_sums.dtype)))
    return (send, recv), a_thru, land, token


def _scatter_wait(sems, chip_sums, land, after, name):
    def body(a_ref, land_ref, send_sems, recv_sems, after_ref, a_out, land_out):
        x, y, c, me, others = _place()
        for k, (ox, oy) in enumerate(others):
            copy = pltpu.make_async_remote_copy(
                src_ref=a_ref.at[2 * ox + oy], dst_ref=land_ref.at[2 * ox + oy], send_sem=send_sems.at[k],
                recv_sem=recv_sems.at[k], device_id=(ox, oy, c), device_id_type=MESH)
            copy.wait_send()
            copy.wait_recv()

    shape = pltpu.HBM(chip_sums.shape, chip_sums.dtype)
    return pl.pallas_call(
        body, name=name, in_specs=[HBM, HBM, SEM, SEM, ANY], out_shape=(shape, shape), out_specs=(HBM, HBM),
        input_output_aliases={0: 0, 1: 1}, compiler_params=pltpu.CompilerParams(has_side_effects=EFFECT),
    )(chip_sums, land, sems[0], sems[1], after)


def _add_landed_join(chip_sums, land, chip, name):
    chips, rh, c = chip_sums.shape
    nb = rh // SWAP_ROWS

    def body(chip_ref, own_ref, l1_ref, l2_ref, l3_ref, out_hbm, buf, send_sems, recv_sem, local_sems):
        i = pl.program_id(0)
        slot = i % 2
        x, y, core, _, _ = _place()
        sibling = (x, y, 1 - core)

        def copies(s, step):
            rows = pl.ds(pl.multiple_of((core * nb + step) * SWAP_ROWS, SWAP_ROWS), SWAP_ROWS)
            keep = pltpu.make_async_copy(buf.at[s], out_hbm.at[rows, :], local_sems.at[s])
            give = pltpu.make_async_remote_copy(src_ref=buf.at[s], dst_ref=out_hbm.at[rows, :], send_sem=send_sems.at[s],
                                                recv_sem=recv_sem.at[0], device_id=sibling, device_id_type=MESH)
            return keep, give

        def drain(s, step):
            keep, give = copies(s, step)
            keep.wait()
            give.wait_send()

        @pl.when(i >= 2)
        def _():
            drain(slot, i - 2)

        buf[slot] = ((own_ref[...].astype(f32) + l1_ref[...].astype(f32)) + l2_ref[...].astype(f32)) + l3_ref[...].astype(f32)
        keep, give = copies(slot, i)
        keep.start()
        give.start()

        @pl.when(i == nb - 1)
        def _():
            drain(slot, i)
            if nb > 1:
                drain(1 - slot, i - 1)
            theirs = out_hbm.at[pl.ds((1 - core) * rh, rh), :]
            pltpu.make_async_remote_copy(src_ref=theirs, dst_ref=theirs, send_sem=send_sems.at[0], recv_sem=recv_sem.at[0],
                                         device_id=sibling, device_id_type=MESH).wait_recv()

    block = (SWAP_ROWS, c)
    from_slot = lambda d: pl.BlockSpec(block, lambda i, chip: (((chip[0] + d) % chips) * nb + i, 0))
    grid_spec = pltpu.PrefetchScalarGridSpec(
        num_scalar_prefetch=1, grid=(nb,), in_specs=[from_slot(0), from_slot(1), from_slot(2), from_slot(3)],
        out_specs=ANY,
        scratch_shapes=[pltpu.VMEM((2, SWAP_ROWS, c), f32), pltpu.SemaphoreType.DMA((2,)),
                        pltpu.SemaphoreType.DMA((1,)), pltpu.SemaphoreType.DMA((2,))])
    land_rows = land.reshape(chips * rh, c)
    return pl.pallas_call(body, name=name, grid_spec=grid_spec, out_shape=SDS((2 * rh, c), f32))(
        chip, chip_sums.reshape(chips * rh, c), land_rows, land_rows, land_rows)


def _adamw_update(w_ref, g_ref, m_ref, v_ref, d_ref, nm_ref, nv_ref):
    g = g_ref[...]
    nm = ADAM_B1 * m_ref[...] + (1.0 - ADAM_B1) * g
    nv = ADAM_B2 * v_ref[...] + (1.0 - ADAM_B2) * (g * g)
    nm_ref[...] = nm
    nv_ref[...] = nv
    m_hat = nm / (1.0 - ADAM_B1 ** ADAM_STEP)
    v_hat = nv / (1.0 - ADAM_B2 ** ADAM_STEP)
    d_ref[...] = -ADAM_LR * (m_hat / (jnp.sqrt(v_hat) + ADAM_EPS) + ADAM_WD * w_ref[...])


def _adamw(w, g, m, v, name):
    r, c = w.shape
    tr = 128 if r % 128 == 0 else r

    def body(w_ref, g_ref, m_ref, v_ref, g_out_ref, d_ref, nm_ref, nv_ref):
        g_out_ref[...] = g_ref[...]
        _adamw_update(w_ref, g_ref, m_ref, v_ref, d_ref, nm_ref, nv_ref)

    spec = pl.BlockSpec((tr, c), lambda i: (i, 0))
    return pl.pallas_call(body, name=name, grid=(r // tr,), in_specs=[spec] * 4, out_specs=[spec] * 4,
                          out_shape=[SDS((r, c), f32)] * 4)(w, g, m, v)


SMALL_PACKING = {
    "e_pre_norm": ((1, 2048), 8, (1, 2048)), "e_pool_w": ((1024, 256), 1024, (256, 256)),
    "e_pool_scale": ((1, 1024), 8, (1, 1024)), "e_post_norm": ((1, 2048), 8, (1, 2048)),
    "o_pre_norm": ((1, 2048), 8, (1, 512)), "o_sgu_norm_g": ((1, 1024), 8, (1, 256)),
    "o_sgu_norm_b": ((1, 1024), 8, (1, 256)), "o_sgu_w": ((512, 128), 512, (512, 128)),
    "o_sgu_b": ((128, 128), 8, (4, 128)), "o_conv_w": ((31, 1024), 128, (31, 256)), "o_conv_b": ((1, 1024), 8, (1, 256)),
    "o_conv_norm_g": ((1, 1024), 8, (1, 256)), "o_conv_norm_b": ((1, 1024), 8, (1, 256)),
    "o_post_norm": ((1, 2048), 8, (1, 512)),
}
SMALL_PACKED_ROWS = 1792
RELATIONS = [(fx, fy, fc) for fx in (0, 1) for fy in (0, 1) for fc in (0, 1)][1:]


def _small_finalize(grads, ws, ms, vs, after):
    names = list(SMALL_ORDER)
    n = len(names)
    half, piece = SMALL_PACKED_ROWS // 2, SMALL_PACKED_ROWS // 8
    first_row, row = {}, 0
    for k in names:
        first_row[k] = row
        row += SMALL_PACKING[k][1]

    def body(*refs):
        g_refs, w_refs, m_refs, v_refs = refs[0:n], refs[n:2 * n], refs[2 * n:3 * n], refs[3 * n:4 * n]
        outs = refs[4 * n + 1:8 * n + 1]
        pack, from_sibling, from_chips, total, send_a, recv_a, send_b, recv_b, send_c, recv_c = refs[8 * n + 1:]
        x, y, c, me, others = _place()

        for r0 in range(0, SMALL_PACKED_ROWS, piece):
            pack[r0:r0 + piece, :] = jnp.zeros((piece, LANES), f32)
        for k, g_ref in zip(names, g_refs):
            (rows, width), _, _ = SMALL_PACKING[k]
            r0 = first_row[k]
            if k == "o_sgu_b":
                pack[r0:r0 + 4, 0:CHUNK] = g_ref[...].T[0:4, :]
            elif width < LANES:
                pack[r0:r0 + rows, 0:width] = g_ref[...]
            else:
                for j in range(width // LANES):
                    dst = r0 + j * (1 if rows == 1 else 32)
                    pack[dst:dst + rows, :] = g_ref[:, j * LANES:(j + 1) * LANES]

        sibling = (x, y, 1 - c)
        swap = pltpu.make_async_remote_copy(
            src_ref=pack.at[pl.ds(pl.multiple_of((1 - c) * half, 8), half), :], dst_ref=from_sibling,
            send_sem=send_a.at[0], recv_sem=recv_a.at[0], device_id=sibling, device_id_type=MESH)
        swap.start()
        swap.wait()
        for j in range(4):
            rows = pl.ds(pl.multiple_of(c * half + j * piece, 8), piece)
            pack[rows, :] = pack[rows, :] + from_sibling[j * piece:(j + 1) * piece, :]

        def piece_of(chip):
            return pl.ds(pl.multiple_of(c * half + chip * piece, 8), piece)

        def to_chip(k):
            ox, oy = others[k]
            return pltpu.make_async_remote_copy(
                src_ref=pack.at[piece_of(2 * ox + oy), :], dst_ref=from_chips.at[me], send_sem=send_b.at[k],
                recv_sem=recv_b.at[k], device_id=(ox, oy, c), device_id_type=MESH)

        for k in range(3):
            to_chip(k).start()
        from_chips[me] = pack[piece_of(me), :]
        for k, (ox, oy) in enumerate(others):
            landed = from_chips.at[2 * ox + oy]
            pltpu.make_async_remote_copy(src_ref=landed, dst_ref=landed, send_sem=send_b.at[k], recv_sem=recv_b.at[k],
                                         device_id=(ox, oy, c), device_id_type=MESH).wait_recv()
        for k in range(3):
            to_chip(k).wait_send()
        mine = pl.ds(pl.multiple_of(c * half + me * piece, 8), piece)
        total[mine, :] = ((from_chips[0] + from_chips[1]) + from_chips[2]) + from_chips[3]

        def flip(v, f):
            return 1 - v if f else v

        def to_device(r):
            fx, fy, fc = RELATIONS[r]
            return pltpu.make_async_remote_copy(
                src_ref=total.at[mine, :], dst_ref=total.at[mine, :], send_sem=send_c.at[r], recv_sem=recv_c.at[r],
                device_id=(flip(x, fx), flip(y, fy), flip(c, fc)), device_id_type=MESH)

        for r in range(len(RELATIONS)):
            to_device(r).start()
        for r, (fx, fy, fc) in enumerate(RELATIONS):
            theirs = pl.ds(pl.multiple_of(flip(c, fc) * half + (2 * flip(x, fx) + flip(y, fy)) * piece, 8), piece)
            pltpu.make_async_remote_copy(
                src_ref=total.at[theirs, :], dst_ref=total.at[theirs, :], send_sem=send_c.at[r], recv_sem=recv_c.at[r],
                device_id=(flip(x, fx), flip(y, fy), flip(c, fc)), device_id_type=MESH).wait_recv()
        for r in range(len(RELATIONS)):
            to_device(r).wait_send()

        def of_chip(candidates):
            value = candidates[0]
            for j in range(1, N_CHIPS):
                value = jnp.where(me == j, candidates[j], value)
            return value

        for i, k in enumerate(names):
            (rows, width), _, (local_rows, local_width) = SMALL_PACKING[k]
            r0 = first_row[k]
            if k == "o_sgu_b":
                g = total[r0:r0 + 4, 0:CHUNK]
            elif k == "e_pool_w":
                for grp in range(4):
                    src = pl.ds(pl.multiple_of(r0 + grp * POOL_CH + me * 64, 8), 64)
                    dst = slice(grp * 64, (grp + 1) * 64)
                    _adamw_rows(total[src, :], i, dst, w_refs, m_refs, v_refs, outs, n)
                continue
            elif k == "o_conv_w":
                g = total[pl.ds(pl.multiple_of(r0 + me * 32, 8), 32), :][0:CONV_K]
            elif width < LANES:
                g = total[r0:r0 + rows, 0:width]
            else:
                lanes = [total[r0 + j:r0 + j + 1, :] for j in range(width // LANES)]
                per_chip = local_width // LANES
                if local_width == width:
                    g = jnp.concatenate(lanes, axis=1)
                elif per_chip == 1:
                    g = of_chip(lanes)
                else:
                    g = of_chip([jnp.concatenate(lanes[j * per_chip:(j + 1) * per_chip], axis=1) for j in range(N_CHIPS)])
            _adamw_rows(g, i, slice(None), w_refs, m_refs, v_refs, outs, n)

    shard_shapes = [SMALL_PACKING[k][2] for k in names]
    whole = pl.BlockSpec(memory_space=pltpu.VMEM)
    out = pl.pallas_call(
        body, name="small_finalize", in_specs=[whole] * (4 * n) + [ANY], out_specs=[whole] * (4 * n),
        out_shape=[SDS(s, f32) for s in shard_shapes] * 4,
        scratch_shapes=[pltpu.VMEM((SMALL_PACKED_ROWS, LANES), f32), pltpu.VMEM((half, LANES), f32),
                        pltpu.VMEM((N_CHIPS, piece, LANES), f32), pltpu.VMEM((SMALL_PACKED_ROWS, LANES), f32),
                        pltpu.SemaphoreType.DMA((1,)), pltpu.SemaphoreType.DMA((1,)), pltpu.SemaphoreType.DMA((3,)),
                        pltpu.SemaphoreType.DMA((3,)), pltpu.SemaphoreType.DMA((7,)), pltpu.SemaphoreType.DMA((7,))],
    )(*grads, *ws, *ms, *vs, after)
    return out[:n], out[n:2 * n], out[2 * n:3 * n], out[3 * n:]


def _adamw_rows(g, i, rows, w_refs, m_refs, v_refs, outs, n):
    w, m, v = w_refs[i][rows, :], m_refs[i][rows, :], v_refs[i][rows, :]
    nm = ADAM_B1 * m + (1.0 - ADAM_B1) * g
    nv = ADAM_B2 * v + (1.0 - ADAM_B2) * (g * g)
    m_hat = nm / (1.0 - ADAM_B1 ** ADAM_STEP)
    v_hat = nv / (1.0 - ADAM_B2 ** ADAM_STEP)
    outs[i][rows, :] = g
    outs[n + i][rows, :] = -ADAM_LR * (m_hat / (jnp.sqrt(v_hat) + ADAM_EPS) + ADAM_WD * w)
    outs[2 * n + i][rows, :] = nm
    outs[3 * n + i][rows, :] = nv


def _pack(arrays, total_rows=None):
    parts = []
    rows = 0
    for a in arrays:
        flat = a.reshape(-1, LANES)
        pad = -flat.shape[0] % 8
        parts.append(jnp.pad(flat, ((0, pad), (0, 0))))
        rows += flat.shape[0] + pad
    if total_rows is not None:
        parts.append(jnp.zeros((total_rows - rows, LANES), arrays[0].dtype))
    return jnp.concatenate(parts, axis=0)


def _unpack(buf, shapes):
    out = []
    row = 0
    lead = buf.shape[:-2]
    for shape in shapes:
        size = 1
        for s in shape:
            size *= s
        rows = size // LANES
        out.append(buf[..., row:row + rows, :].reshape(lead + tuple(shape)))
        row += rows + (-rows % 8)
    return out


BIG = ("e_w_in", "e_w_out", "o_w_in", "o_w_out")
SHARDED_SMALL = {
    "e_pool_w": ((4, 64, 256), 1), "o_pre_norm": ((512,), 0), "o_sgu_norm_g": ((256,), 0), "o_sgu_norm_b": ((256,), 0),
    "o_conv_w": ((31, 256), 1), "o_conv_b": ((256,), 0), "o_conv_norm_g": ((256,), 0), "o_conv_norm_b": ((256,), 0),
    "o_post_norm": ((512,), 0),
}
SMALL_ORDER = ("e_pre_norm", "e_pool_w", "e_pool_scale", "e_post_norm", "o_pre_norm", "o_sgu_norm_g", "o_sgu_norm_b",
               "o_sgu_w", "o_sgu_b", "o_conv_w", "o_conv_b", "o_conv_norm_g", "o_conv_norm_b", "o_post_norm")
ALL_ORDER = ("e_pre_norm", "e_w_in", "e_pool_w", "e_pool_scale", "e_w_out", "e_post_norm", "o_pre_norm", "o_w_in",
             "o_sgu_norm_g", "o_sgu_norm_b", "o_sgu_w", "o_sgu_b", "o_conv_w", "o_conv_b", "o_conv_norm_g",
             "o_conv_norm_b", "o_w_out", "o_post_norm")


def _full_shape(name):
    shape, axis = SHARDED_SMALL[name]
    return tuple(s * N_CHIPS if i == axis else s for i, s in enumerate(shape))


def _from_chips(name, stacked):
    shape, axis = SHARDED_SMALL[name]
    return jnp.moveaxis(stacked, 0, axis).reshape(_full_shape(name))


def kernel(x, e_pre_norm, e_w_in, e_pool_w, e_pool_scale, e_w_out, e_post_norm, o_pre_norm, o_w_in, o_sgu_norm_g, o_sgu_norm_b, o_sgu_w, o_sgu_b, o_conv_w, o_conv_b, o_conv_norm_g, o_conv_norm_b, o_w_out, o_post_norm, loss_target, m_e_pre_norm, m_e_w_in, m_e_pool_w, m_e_pool_scale, m_e_w_out, m_e_post_norm, m_o_pre_norm, m_o_w_in, m_o_sgu_norm_g, m_o_sgu_norm_b, m_o_sgu_w, m_o_sgu_b, m_o_conv_w, m_o_conv_b, m_o_conv_norm_g, m_o_conv_norm_b, m_o_w_out, m_o_post_norm, v_e_pre_norm, v_e_w_in, v_e_pool_w, v_e_pool_scale, v_e_w_out, v_e_post_norm, v_o_pre_norm, v_o_w_in, v_o_sgu_norm_g, v_o_sgu_norm_b, v_o_sgu_w, v_o_sgu_b, v_o_conv_w, v_o_conv_b, v_o_conv_norm_g, v_o_conv_norm_b, v_o_w_out, v_o_post_norm):
    w = dict(e_pre_norm=e_pre_norm, e_w_in=e_w_in, e_pool_w=e_pool_w, e_pool_scale=e_pool_scale, e_w_out=e_w_out,
             e_post_norm=e_post_norm, o_pre_norm=o_pre_norm, o_w_in=o_w_in, o_sgu_norm_g=o_sgu_norm_g,
             o_sgu_norm_b=o_sgu_norm_b, o_sgu_w=o_sgu_w, o_sgu_b=o_sgu_b, o_conv_w=o_conv_w, o_conv_b=o_conv_b,
             o_conv_norm_g=o_conv_norm_g, o_conv_norm_b=o_conv_norm_b, o_w_out=o_w_out, o_post_norm=o_post_norm)
    m = dict(e_pre_norm=m_e_pre_norm, e_w_in=m_e_w_in, e_pool_w=m_e_pool_w, e_pool_scale=m_e_pool_scale,
             e_w_out=m_e_w_out, e_post_norm=m_e_post_norm, o_pre_norm=m_o_pre_norm, o_w_in=m_o_w_in,
             o_sgu_norm_g=m_o_sgu_norm_g, o_sgu_norm_b=m_o_sgu_norm_b, o_sgu_w=m_o_sgu_w, o_sgu_b=m_o_sgu_b,
             o_conv_w=m_o_conv_w, o_conv_b=m_o_conv_b, o_conv_norm_g=m_o_conv_norm_g, o_conv_norm_b=m_o_conv_norm_b,
             o_w_out=m_o_w_out, o_post_norm=m_o_post_norm)
    v = dict(e_pre_norm=v_e_pre_norm, e_w_in=v_e_w_in, e_pool_w=v_e_pool_w, e_pool_scale=v_e_pool_scale,
             e_w_out=v_e_w_out, e_post_norm=v_e_post_norm, o_pre_norm=v_o_pre_norm, o_w_in=v_o_w_in,
             o_sgu_norm_g=v_o_sgu_norm_g, o_sgu_norm_b=v_o_sgu_norm_b, o_sgu_w=v_o_sgu_w, o_sgu_b=v_o_sgu_b,
             o_conv_w=v_o_conv_w, o_conv_b=v_o_conv_b, o_conv_norm_g=v_o_conv_norm_g, o_conv_norm_b=v_o_conv_norm_b,
             o_w_out=v_o_w_out, o_post_norm=v_o_post_norm)
    w, m, v = ({k: a[0] for k, a in d.items()} for d in (w, m, v))
    chip = 2 * lax.axis_index("x") + lax.axis_index("y")

    loss, grad_x, in_flight, small = _step(x[0], loss_target[0], w, chip)

    grads, delta, new_m, new_v = {}, {}, {}, {}
    after = grad_x
    for k in ("o_w_out", "o_w_in", "e_w_out", "e_w_in"):
        grads[k], delta[k], new_m[k], new_v[k] = _adamw(w[k], _land(in_flight, k, chip, after), m[k], v[k], f"adamw_{k}")
        after = delta[k]

    def rows_of(a):
        return a.reshape(-1, a.shape[-1])

    small_grads = [small[k].reshape(SMALL_PACKING[k][0]) for k in SMALL_ORDER]
    updates = _small_finalize(small_grads, *[[rows_of(d[k]) for k in SMALL_ORDER] for d in (w, m, v)], after)
    for d, arrays in zip((grads, delta, new_m, new_v), updates):
        for k, a in zip(SMALL_ORDER, arrays):
            d[k] = a.reshape(w[k].shape)
    loss = lax.psum(loss[0, 0], ("x", "y", "c"))

    outs = [loss, grad_x[None]]
    for d in (grads, delta, new_m, new_v):
        outs += [d[k][None] for k in ALL_ORDER]
    return tuple(outs)
```

```python
import jax
import jax.numpy as jnp
from jax import lax
from jax.experimental import pallas as pl
from jax.experimental.pallas import tpu as pltpu

f32 = jnp.float32
bf16 = jnp.bfloat16
SDS = jax.ShapeDtypeStruct

SEQ = 2048
D_MODEL = 2048
EPS = 1e-6
NEG = -1e30
HEAD_DIM = 128
ROT_HALF = 16
ROPE_THETA = 500000.0
DILATIONS = (1, 4, 16)
SPAN = 128
N_HEADS = 8
HALF = 1024
POOL_CH = 256
CONV_K = 31
CONV_PAD = 32
CHUNK = 128
N_CHIPS = 4
LANES = 256
E_IN_PIECES = 3
SMALL_SHARD_ROWS = 352
ANY = pl.BlockSpec(memory_space=pl.ANY)
MESH = pl.DeviceIdType.MESH

ADAM_LR = 0.001
ADAM_B1 = 0.9
ADAM_B2 = 0.999
ADAM_EPS = 1e-08
ADAM_WD = 0.01
ADAM_STEP = 10


def _dot(a, b):
    return jnp.dot(a, b, preferred_element_type=f32)


def _dot_nt(a, b):
    return lax.dot_general(a, b, (((1,), (1,)), ((), ())), preferred_element_type=f32)


def _dot_tn(a, b):
    return lax.dot_general(a, b, (((0,), (0,)), ((), ())), preferred_element_type=f32)


def _sigmoid(x):
    return 1.0 / (1.0 + jnp.exp(-x))


def _silu_and_grad(x):
    s = _sigmoid(x)
    return x * s, s * (1.0 + x * (1.0 - s))


def _rms_fwd(x, g):
    r = lax.rsqrt(jnp.mean(x * x, axis=-1, keepdims=True) + EPS)
    return x * r * g


def _rms_bwd(x, g, dout):
    r = lax.rsqrt(jnp.mean(x * x, axis=-1, keepdims=True) + EPS)
    xh = x * r
    dg = jnp.sum(dout * xh, axis=0, keepdims=True)
    dxh = dout * g
    dx = r * (dxh - xh * jnp.mean(dxh * xh, axis=-1, keepdims=True))
    return dx, dg


def _ln_stats(x):
    mu = jnp.mean(x, axis=-1, keepdims=True)
    xc = x - mu
    rstd = lax.rsqrt(jnp.mean(xc * xc, axis=-1, keepdims=True) + EPS)
    return xc * rstd, rstd


def _ln_bwd(xh, rstd, g, dout):
    dg = jnp.sum(dout * xh, axis=0, keepdims=True)
    db = jnp.sum(dout, axis=0, keepdims=True)
    dxh = dout * g
    dx = rstd * (dxh - jnp.mean(dxh, axis=-1, keepdims=True) - xh * jnp.mean(dxh * xh, axis=-1, keepdims=True))
    return dx, dg, db


def _accumulate(ref, value, first):
    @pl.when(first)
    def _():
        ref[...] = value

    @pl.when(jnp.logical_not(first))
    def _():
        ref[...] += value


def _col_tile(ns):
    for t in (1024, 768, 512, 256):
        if ns % t == 0:
            return t
    raise ValueError(ns)


def _mm_nn(a, w, out_dtype, name, piece=0, pieces=1, into=None):
    m, k = a.shape
    j, _, ns = w.shape
    tm, tn = m, _col_tile(ns)
    nb = ns // tn

    def body(a_ref, w_ref, *rest):
        rest[-1][...] = _dot(a_ref[...], w_ref[...]).astype(rest[-1].dtype)

    return pl.pallas_call(
        body, name=name, grid=(j * nb, m // tm),
        in_specs=[pl.BlockSpec((tm, k), lambda n, i: (i, 0)),
                  pl.BlockSpec((None, k, tn), lambda n, i: (n // nb, 0, n % nb))] + ([] if into is None else [ANY]),
        out_specs=pl.BlockSpec((tm, tn), lambda n, i: (i, ((n // nb) * pieces + piece) * nb + n % nb)),
        out_shape=SDS((m, j * ns * pieces), out_dtype),
        input_output_aliases={} if into is None else {2: 0},
    )(a, w, *([] if into is None else [into]))


def _mm_nt(dz, ws, name, after):
    m, _ = dz.shape
    pieces = len(ws)
    j, k, ns = ws[0].shape
    tm, tk = 1024, 1024

    def body(dz_ref, *rest):
        w_refs, o_ref = rest[:pieces], rest[-1]
        total = _dot_nt(dz_ref[:, 0:ns], w_refs[0][...])
        for q in range(1, pieces):
            total = total + _dot_nt(dz_ref[:, q * ns:(q + 1) * ns], w_refs[q][...])
        _accumulate(o_ref, total, pl.program_id(2) == 0)

    return pl.pallas_call(
        body, name=name, grid=(m // tm, k // tk, j),
        in_specs=[pl.BlockSpec((tm, pieces * ns), lambda i, kk, r: (i, r))]
        + [pl.BlockSpec((None, tk, ns), lambda i, kk, r: (r, kk, 0))] * pieces + [ANY],
        out_specs=pl.BlockSpec((tm, tk), lambda i, kk, r: (i, kk)),
        out_shape=SDS((m, k), f32),
    )(dz, *ws, after)


def _mm_tn(a, dz, j, name):
    m, k = a.shape
    ns = dz.shape[1] // j
    tk, tn = 1024, _col_tile(ns)
    nb = ns // tn

    def body(a_ref, dz_ref, o_ref):
        o_ref[...] = _dot_tn(a_ref[...], dz_ref[...]).astype(o_ref.dtype)

    return pl.pallas_call(
        body, name=name, grid=(k // tk, j * nb),
        in_specs=[pl.BlockSpec((m, tk), lambda kk, n: (0, kk)),
                  pl.BlockSpec((m, tn), lambda kk, n: (0, n))],
        out_specs=pl.BlockSpec((None, tk, tn), lambda kk, n: (n // nb, kk, n % nb)),
        out_shape=SDS((j, k, ns), bf16),
    )(a, dz)


ROWS = 256


def _row_spec(width=D_MODEL, col=0):
    return pl.BlockSpec((ROWS, width), lambda i: (i, col))


def _vec_spec(width=D_MODEL):
    return pl.BlockSpec((1, width), lambda i: (0, 0))


def _pre_norm(x, g):
    def body(x_ref, g_ref, h_ref):
        h_ref[...] = _rms_fwd(x_ref[...], g_ref[...]).astype(bf16)

    return pl.pallas_call(
        body, name="pre_norm", grid=(SEQ // ROWS,), in_specs=[_row_spec(), _vec_spec()],
        out_specs=_row_spec(), out_shape=SDS((SEQ, D_MODEL), bf16))(x, g)


def _mid_norm(x, y, g_post, g_pre):
    def body(x_ref, y_ref, gpost_ref, gpre_ref, x1_ref, h1_ref):
        x1 = x_ref[...] + _rms_fwd(y_ref[...], gpost_ref[...])
        x1_ref[...] = x1
        h1_ref[...] = _rms_fwd(x1, gpre_ref[...]).astype(bf16)

    return pl.pallas_call(
        body, name="mid_norm", grid=(SEQ // ROWS,),
        in_specs=[_row_spec(), _row_spec(), _vec_spec(), _vec_spec()],
        out_specs=[_row_spec(), _row_spec()],
        out_shape=[SDS((SEQ, D_MODEL), f32), SDS((SEQ, D_MODEL), bf16)])(x, y, g_post, g_pre)


def _final_norm_loss(x1, y, g_post, target):
    def body(x1_ref, y_ref, g_ref, t_ref, loss_ref, dx2_ref, dy_ref, dg_ref):
        first = pl.program_id(0) == 0
        y = y_ref[...]
        g = g_ref[...]
        err = x1_ref[...] + _rms_fwd(y, g) - t_ref[...]
        sq = jnp.sum(jnp.sum(err * err, axis=1, keepdims=True), axis=0, keepdims=True)
        _accumulate(loss_ref, sq * (0.5 / D_MODEL), first)
        dx2 = err * (1.0 / D_MODEL)
        dx2_ref[...] = dx2
        dy, dg = _rms_bwd(y, g, dx2)
        dy_ref[...] = dy.astype(bf16)
        _accumulate(dg_ref, dg, first)

    return pl.pallas_call(
        body, name="final_norm_loss", grid=(SEQ // ROWS,),
        in_specs=[_row_spec(), _row_spec(), _vec_spec(), _row_spec()],
        out_specs=[pl.BlockSpec((1, 1), lambda i: (0, 0)), _row_spec(), _row_spec(), _vec_spec()],
        out_shape=[SDS((1, 1), f32), SDS((SEQ, D_MODEL), f32), SDS((SEQ, D_MODEL), bf16), SDS((1, D_MODEL), f32)],
    )(x1, y, g_post, target)


def _mid_norm_bwd(dx2, dh1, x1, y0, g_pre, g_post):
    def body(dx2_ref, dh1_ref, x1_ref, y0_ref, gpre_ref, gpost_ref, dx1_ref, dy0_ref, dgpre_ref, dgpost_ref):
        first = pl.program_id(0) == 0
        d_in, dgpre = _rms_bwd(x1_ref[...], gpre_ref[...], dh1_ref[...])
        dx1 = dx2_ref[...] + d_in
        dx1_ref[...] = dx1
        dy0, dgpost = _rms_bwd(y0_ref[...], gpost_ref[...], dx1)
        dy0_ref[...] = dy0.astype(bf16)
        _accumulate(dgpre_ref, dgpre, first)
        _accumulate(dgpost_ref, dgpost, first)

    return pl.pallas_call(
        body, name="mid_norm_bwd", grid=(SEQ // ROWS,),
        in_specs=[_row_spec(), _row_spec(), _row_spec(), _row_spec(), _vec_spec(), _vec_spec()],
        out_specs=[_row_spec(), _row_spec(), _vec_spec(), _vec_spec()],
        out_shape=[SDS((SEQ, D_MODEL), f32), SDS((SEQ, D_MODEL), bf16), SDS((1, D_MODEL), f32), SDS((1, D_MODEL), f32)],
    )(dx2, dh1, x1, y0, g_pre, g_post)


def _pre_norm_bwd(dx1, dh0, x, g):
    def body(dx1_ref, dh0_ref, x_ref, g_ref, dx_ref, dg_ref):
        d_in, dg = _rms_bwd(x_ref[...], g_ref[...], dh0_ref[...])
        dx_ref[...] = dx1_ref[...] + d_in
        _accumulate(dg_ref, dg, pl.program_id(0) == 0)

    return pl.pallas_call(
        body, name="pre_norm_bwd", grid=(SEQ // ROWS,),
        in_specs=[_row_spec(), _row_spec(), _row_spec(), _vec_spec()],
        out_specs=[_row_spec(), _vec_spec()],
        out_shape=[SDS((SEQ, D_MODEL), f32), SDS((1, D_MODEL), f32)])(dx1, dh0, x, g)


def _pool_count(g):
    row = lax.broadcasted_iota(jnp.int32, (SEQ, 1), 0)
    width = jnp.left_shift(2, g)
    return row, width, jnp.minimum(row + 1, width).astype(f32)


def _trailing_sum(x, row, width):
    s = x
    for k in (1, 2, 4, 8):
        shifted = jnp.where(row >= k, pltpu.roll(s, k, 0), 0.0)
        s = jnp.where(width > k, s + shifted, s)
    return s


def _leading_sum(x, row, width):
    s = x
    for k in (1, 2, 4, 8):
        shifted = jnp.where(row < SEQ - k, pltpu.roll(s, SEQ - k, 0), 0.0)
        s = jnp.where(width > k, s + shifted, s)
    return s


def _pool_specs():
    a_in = pl.BlockSpec((SEQ, POOL_CH), lambda g: (0, g))
    a_gate = pl.BlockSpec((SEQ, POOL_CH), lambda g: (0, 4 + g))
    w = pl.BlockSpec((None, POOL_CH, POOL_CH), lambda g: (g, 0, 0))
    scale = pl.BlockSpec((1, POOL_CH), lambda g: (0, g))
    return a_in, a_gate, w, scale


def _pool_fwd(z0, pool_w, pool_scale):
    def body(a_ref, gate_ref, w_ref, scale_ref, ya_ref):
        row, width, count = _pool_count(pl.program_id(0))
        a = a_ref[...]
        pooled = _trailing_sum(a, row, width) / count - a
        mixed = _dot(pooled.astype(bf16), w_ref[...]) * scale_ref[...]
        gate = gate_ref[...]
        ya_ref[...] = (mixed * gate * _sigmoid(gate)).astype(bf16)

    return pl.pallas_call(
        body, name="pool_fwd", grid=(4,), in_specs=list(_pool_specs()),
        out_specs=pl.BlockSpec((SEQ, POOL_CH), lambda g: (0, g)),
        out_shape=SDS((SEQ, HALF), bf16))(z0, z0, pool_w, pool_scale)


def _pool_bwd(z0, dcat, pool_w, pool_scale):
    def body(a_ref, gate_ref, w_ref, scale_ref, dya_ref, da_ref, dgate_ref, dw_ref, dscale_ref):
        row, width, count = _pool_count(pl.program_id(0))
        a = a_ref[...]
        pooled = (_trailing_sum(a, row, width) / count - a).astype(bf16)
        w = w_ref[...]
        scale = scale_ref[...]
        mixed = _dot(pooled, w)
        silu, dsilu = _silu_and_grad(gate_ref[...])
        dya = dya_ref[...]
        dgate_ref[...] = (dya * mixed * scale * dsilu).astype(bf16)
        dms = dya * silu
        dscale_ref[...] = jnp.sum(dms * mixed, axis=0, keepdims=True)
        dmixed = (dms * scale).astype(bf16)
        dw_ref[...] = _dot_tn(pooled, dmixed)
        dpooled = _dot_nt(dmixed, w)
        da_ref[...] = (_leading_sum(dpooled / count, row, width) - dpooled).astype(bf16)

    a_in, a_gate, w, scale = _pool_specs()
    col = pl.BlockSpec((SEQ, POOL_CH), lambda g: (0, g))
    return pl.pallas_call(
        body, name="pool_bwd", grid=(4,), in_specs=[a_in, a_gate, w, scale, col],
        out_specs=[col, col, w, scale],
        out_shape=[SDS((SEQ, HALF), bf16), SDS((SEQ, HALF), bf16), SDS((4, POOL_CH, POOL_CH), f32), SDS((1, HALF), f32)],
    )(z0, z0, pool_w, pool_scale, dcat)


Q_COL, K_COL, V_COL, BGATE_COL = 16, 40, 64, 88


def _rope_tables():
    pos = jnp.arange(SEQ, dtype=f32)
    inv_freq = jnp.power(ROPE_THETA, -jnp.arange(0, 2 * ROT_HALF, 2, dtype=f32) / (2 * ROT_HALF))
    ang = pos[:, None] * inv_freq[None, :]
    cos, sin = jnp.cos(ang), jnp.sin(ang)
    zeros = jnp.zeros((SEQ, HEAD_DIM - 2 * ROT_HALF), f32)
    cos_t = jnp.concatenate([cos, cos, zeros + 1.0], axis=1)
    sin_t = jnp.concatenate([sin, sin, zeros], axis=1)
    j = jnp.arange(HEAD_DIM)[:, None]
    i = jnp.arange(HEAD_DIM)[None, :]
    rot = jnp.where((i < ROT_HALF) & (j == i + ROT_HALF), -1.0, 0.0) + jnp.where(
        (i >= ROT_HALF) & (i < 2 * ROT_HALF) & (j == i - ROT_HALF), 1.0, 0.0)
    return cos_t, sin_t, rot.astype(bf16), rot.T.astype(bf16)


def _exact_dot(t, m):
    hi = t.astype(bf16)
    lo = (t - hi.astype(f32)).astype(bf16)
    return _dot(hi, m) + _dot(lo, m)


def _rope(t, cos_t, sin_t, rot):
    return t * cos_t + _exact_dot(t, rot) * sin_t


def _rope_transposed(d, cos_t, sin_t, rot_t):
    return d * cos_t + _exact_dot(d * sin_t, rot_t)


ROW_CHUNK = 256


def _chunks(fn):
    def step(i, carry):
        fn(pl.multiple_of(i * ROW_CHUNK, ROW_CHUNK))
        return carry

    lax.fori_loop(0, SEQ // ROW_CHUNK, step, 0, unroll=2)


def _pieces(dilation):
    length = SEQ // dilation
    n = min(length, ROW_CHUNK)
    return [(r, l0, n) for r in range(dilation) for l0 in range(0, length, n)]


def _by_residue(dst_ref, src_ref, dilation, dtype):
    length = SEQ // dilation
    for r, l0, n in _pieces(dilation):
        src = src_ref[l0:l0 + n, :] if dilation == 1 else src_ref[pl.ds(r + dilation * l0, n, stride=dilation), :]
        start = r * length + l0
        dst_ref[start:start + n, :] = src.astype(dtype)


def _by_position(dst_ref, src_ref, dilation):
    length = SEQ // dilation
    for r, l0, n in _pieces(dilation):
        src = src_ref[r * length + l0:r * length + l0 + n, :]
        if dilation == 1:
            dst_ref[l0:l0 + n, :] = src
        else:
            dst_ref[pl.ds(r + dilation * l0, n, stride=dilation), :] = src


def _attn_masks():
    qi = lax.broadcasted_iota(jnp.int32, (SPAN, 2 * SPAN), 0)
    kj = lax.broadcasted_iota(jnp.int32, (SPAN, 2 * SPAN), 1)
    window = ((kj < SPAN) & (kj >= qi)) | ((kj >= SPAN) & (kj - SPAN <= qi))
    own = lax.broadcasted_iota(jnp.int32, (SPAN, SPAN), 1) <= lax.broadcasted_iota(jnp.int32, (SPAN, SPAN), 0)
    return window, own


def _attn_blocks(dilation):
    per_residue = SEQ // dilation // SPAN
    blocks = [(c, c % per_residue != 0) for c in range(SEQ // SPAN)]
    return [blocks[i:i + 4] for i in range(0, len(blocks), 4)]


def _block_keys(c, has_prev):
    return slice((c - 1) * SPAN if has_prev else c * SPAN, (c + 1) * SPAN)


def _head_spec(col):
    return pl.BlockSpec((SEQ, HEAD_DIM), lambda h: (0, col + h))


def _table_spec():
    return pl.BlockSpec((SEQ, HEAD_DIM), lambda h: (0, 0))


def _attn_fwd(z0, tables):
    scale = HEAD_DIM ** -0.5

    def body(*refs):
        qkv = refs[0:9]
        bg_ref, cos_ref, sin_ref, rot_ref = refs[9:13]
        yb_ref, att_ref, lse_ref = refs[13:16]
        saved = refs[16:25]
        tmp_q, tmp_k, v_ones, o_res, l_res, o_nat, l_nat = refs[25:32]
        window_mask, own_mask = _attn_masks()
        rot = rot_ref[...]

        @pl.when(pl.program_id(0) == 0)
        def _():
            v_ones[:, HEAD_DIM:] = jnp.ones((SEQ, HEAD_DIM), bf16)

        for g, dilation in enumerate(DILATIONS):
            q_ref, k_ref, v_ref = qkv[3 * g:3 * g + 3]
            qd, kd, vd = saved[3 * g:3 * g + 3]

            def rope_rows(start, q_ref=q_ref, k_ref=k_ref):
                r = pl.ds(start, ROW_CHUNK)
                cos_t, sin_t = cos_ref[r, :], sin_ref[r, :]
                tmp_q[r, :] = _rope(q_ref[r, :], cos_t, sin_t, rot) * scale
                tmp_k[r, :] = _rope(k_ref[r, :], cos_t, sin_t, rot)

            _chunks(rope_rows)
            _by_residue(qd, tmp_q, dilation, bf16)
            _by_residue(kd, tmp_k, dilation, bf16)
            _by_residue(vd, v_ref, dilation, bf16)
            for l0 in range(0, SEQ, ROW_CHUNK):
                v_ones[l0:l0 + ROW_CHUNK, 0:HEAD_DIM] = vd[l0:l0 + ROW_CHUNK, :]

            for four in _attn_blocks(dilation):
                scores = [_dot_nt(qd[c * SPAN:(c + 1) * SPAN, :], kd[_block_keys(c, prev), :]) for c, prev in four]
                tops, probs = [], []
                for (c, prev), s in zip(four, scores):
                    s = jnp.where(window_mask if prev else own_mask, s, NEG)
                    tops.append(jnp.max(s, axis=1, keepdims=True))
                    probs.append(jnp.exp(s - tops[-1]).astype(bf16))
                sums = [_dot(p, v_ones[_block_keys(c, prev), :]) for (c, prev), p in zip(four, probs)]
                for (c, prev), m, o in zip(four, tops, sums):
                    den = o[:, HEAD_DIM:]
                    o_res[c * SPAN:(c + 1) * SPAN, :] = o[:, :HEAD_DIM] / den
                    l_res[c * SPAN:(c + 1) * SPAN, :] = m + jnp.log(den)

            if dilation > 1:
                _by_position(o_nat, o_res, dilation)
                _by_position(l_nat, l_res, dilation)
            o_g, l_g = (o_res, l_res) if dilation == 1 else (o_nat, l_nat)

            def merge(start, g=g, o_g=o_g, l_g=l_g):
                r = pl.ds(start, ROW_CHUNK)
                if g == 0:
                    att, total = o_g[r, :], l_g[r, :]
                else:
                    l_old, l_new = lse_ref[r, :], l_g[r, :]
                    top = jnp.maximum(l_old, l_new)
                    total = top + jnp.log(jnp.exp(l_old - top) + jnp.exp(l_new - top))
                    att = att_ref[r, :] * jnp.exp(l_old - total) + o_g[r, :] * jnp.exp(l_new - total)
                att_ref[r, :] = att
                lse_ref[r, :] = total
                if g == len(DILATIONS) - 1:
                    gate = bg_ref[r, :]
                    yb_ref[r, :] = (att * gate * _sigmoid(gate)).astype(bf16)

            _chunks(merge)

    in_specs = []
    for g in range(3):
        in_specs += [_head_spec(Q_COL + 8 * g), _head_spec(K_COL + 8 * g), _head_spec(V_COL + 8 * g)]
    in_specs += [_head_spec(BGATE_COL), _table_spec(), _table_spec(), pl.BlockSpec((HEAD_DIM, HEAD_DIM), lambda h: (0, 0))]
    out_spec = pl.BlockSpec((SEQ, HEAD_DIM), lambda h: (0, h))
    vm = lambda dt: pltpu.VMEM((SEQ, HEAD_DIM), dt)
    cos_t, sin_t, rot, _ = tables
    out = pl.pallas_call(
        body, name="attn_fwd", grid=(N_HEADS,), in_specs=in_specs, out_specs=[out_spec] * 12,
        out_shape=[SDS((SEQ, HALF), bf16), SDS((SEQ, HALF), f32), SDS((SEQ, HALF), f32)] + [SDS((SEQ, HALF), bf16)] * 9,
        scratch_shapes=[vm(f32), vm(f32), pltpu.VMEM((SEQ, 2 * HEAD_DIM), bf16), vm(f32), vm(f32), vm(f32), vm(f32)],
    )(*([z0] * 10), cos_t, sin_t, rot)
    return out[0], out[1], out[2], [tuple(out[3 + 3 * g:6 + 3 * g]) for g in range(3)]


def _attn_bwd_group(g, saved, z0, att, lse, dcat, tables):
    scale = HEAD_DIM ** -0.5
    dilation = DILATIONS[g]
    with_gate = g == 0

    def body(*refs):
        qd, kd, vd, bg_ref, att_ref, lse_ref, dyb_ref, cos_ref, sin_ref, rot_t_ref = refs[0:10]
        n_out = 4 if with_gate else 3
        dq_ref, dk_ref, dv_ref = refs[10:13]
        dod, ld, dd, tmp, aq, ak, av = refs[10 + n_out:17 + n_out]
        window_mask, own_mask = _attn_masks()
        rot_t = rot_t_ref[...]

        def gate_rows(start):
            r = pl.ds(start, ROW_CHUNK)
            silu, dsilu = _silu_and_grad(bg_ref[r, :])
            att_v = att_ref[r, :]
            dyb = dyb_ref[r, :]
            if with_gate:
                refs[13][r, :] = (dyb * att_v * dsilu).astype(bf16)
            datt = dyb * silu
            tmp[r, :] = datt
            aq[r, :] = jnp.broadcast_to(jnp.sum(datt * att_v, axis=1, keepdims=True), (ROW_CHUNK, HEAD_DIM))

        _chunks(gate_rows)
        _by_residue(dod, tmp, dilation, bf16)
        _by_residue(dd, aq, dilation, f32)
        _by_residue(ld, lse_ref, dilation, f32)

        for four in _attn_blocks(dilation):
            rows = [slice(c * SPAN, (c + 1) * SPAN) for c, _ in four]
            keys = [_block_keys(c, prev) for c, prev in four]
            scores = [_dot_nt(qd[r, :], kd[k, :]) for r, k in zip(rows, keys)]
            dprobs = [_dot_nt(dod[r, :], vd[k, :]) for r, k in zip(rows, keys)]
            probs, dscores = [], []
            for (c, prev), r, s, dp in zip(four, rows, scores, dprobs):
                lse_q, delta = ld[r, :], dd[r, :]
                if prev:
                    lse_q = jnp.concatenate([lse_q, lse_q], axis=1)
                    delta = jnp.concatenate([delta, delta], axis=1)
                p = jnp.where(window_mask if prev else own_mask, jnp.exp(s - lse_q), 0.0)
                probs.append(p.astype(bf16))
                dscores.append((p * (dp - delta)).astype(bf16))
            dvs = [_dot_tn(p, dod[r, :]) for p, r in zip(probs, rows)]
            dks = [_dot_tn(ds, qd[r, :]) for ds, r in zip(dscores, rows)]
            dqs = [_dot(ds, kd[k, :]) for ds, k in zip(dscores, keys)]
            for (c, prev), r, dv, dk, dq in zip(four, rows, dvs, dks, dqs):
                aq[r, :] = dq
                if prev:
                    before = slice((c - 1) * SPAN, c * SPAN)
                    av[before, :] += dv[0:SPAN]
                    ak[before, :] += dk[0:SPAN]
                    av[r, :] = dv[SPAN:]
                    ak[r, :] = dk[SPAN:]
                else:
                    av[r, :] = dv
                    ak[r, :] = dk

        def finish(out_ref, acc, factor, roped):
            if dilation > 1:
                _by_position(tmp, acc, dilation)
            src = acc if dilation == 1 else tmp

            def rows(start):
                r = pl.ds(start, ROW_CHUNK)
                d = src[r, :]
                if factor != 1.0:
                    d = d * factor
                if roped:
                    d = _rope_transposed(d, cos_ref[r, :], sin_ref[r, :], rot_t)
                out_ref[r, :] = d.astype(bf16)

            _chunks(rows)

        finish(dq_ref, aq, scale, True)
        finish(dk_ref, ak, 1.0, True)
        finish(dv_ref, av, 1.0, False)

    head = pl.BlockSpec((SEQ, HEAD_DIM), lambda h: (0, h))
    in_specs = [head, head, head, _head_spec(BGATE_COL), head, head, _head_spec(8), _table_spec(), _table_spec(),
                pl.BlockSpec((HEAD_DIM, HEAD_DIM), lambda h: (0, 0))]
    n_out = 4 if with_gate else 3
    vm = lambda dt: pltpu.VMEM((SEQ, HEAD_DIM), dt)
    cos_t, sin_t, _, rot_t = tables
    return pl.pallas_call(
        body, name=f"attn_bwd_g{g}", grid=(N_HEADS,), in_specs=in_specs, out_specs=[head] * n_out,
        out_shape=[SDS((SEQ, HALF), bf16)] * n_out,
        scratch_shapes=[vm(bf16), vm(f32), vm(f32), vm(f32), vm(f32), vm(f32), vm(f32)],
    )(*saved, z0, att, lse, dcat, cos_t, sin_t, rot_t)


def _sgu_specs():
    chunk = lambda col: pl.BlockSpec((CHUNK, HALF), lambda n: (n, col))
    vec = pl.BlockSpec((1, HALF), lambda n: (0, 0))
    w = pl.BlockSpec((4, CHUNK, CHUNK), lambda n: (0, 0, 0))
    bias = pl.BlockSpec((CHUNK, CHUNK), lambda n: (0, 0))
    return chunk, vec, w, bias


def _sgu_weights(w_ref):
    tril = lax.broadcasted_iota(jnp.int32, (CHUNK, CHUNK), 1) <= lax.broadcasted_iota(jnp.int32, (CHUNK, CHUNK), 0)
    return tril, [jnp.where(tril, w_ref[h], 0.0).astype(bf16) for h in range(4)]


def _sgu_fwd(z1, ln_g, ln_b, sgu_w, bias_t):
    def body(u_ref, v_ref, cg_ref, g_ref, b_ref, w_ref, bias_ref, yc_ref):
        _, ws = _sgu_weights(w_ref)
        xh, _ = _ln_stats(v_ref[...])
        vn = (xh * g_ref[...] + b_ref[...]).astype(bf16)
        for h in range(4):
            cols = slice(h * POOL_CH, (h + 1) * POOL_CH)
            s = _dot(ws[h], vn[:, cols]) + bias_ref[:, h:h + 1]
            gate = cg_ref[:, cols]
            yc_ref[:, cols] = (u_ref[:, cols] * s * gate * _sigmoid(gate)).astype(bf16)

    chunk, vec, w, bias = _sgu_specs()
    return pl.pallas_call(
        body, name="sgu_fwd", grid=(SEQ // CHUNK,),
        in_specs=[chunk(0), chunk(1), chunk(2), vec, vec, w, bias], out_specs=chunk(0),
        out_shape=SDS((SEQ, HALF), bf16))(z1, z1, z1, ln_g, ln_b, sgu_w, bias_t)


def _sgu_bwd(z1, dcat, ln_g, ln_b, sgu_w, bias_t):
    def body(u_ref, v_ref, cg_ref, dyc_ref, g_ref, b_ref, w_ref, bias_ref,
             du_ref, dv_ref, dcg_ref, dw_ref, dbias_ref, dg_ref, db_ref, dvn_ref):
        first = pl.program_id(0) == 0
        tril, ws = _sgu_weights(w_ref)
        xh, rstd = _ln_stats(v_ref[...])
        g = g_ref[...]
        vn = (xh * g + b_ref[...]).astype(bf16)

        @pl.when(first)
        def _():
            dbias_ref[...] = jnp.zeros((CHUNK, CHUNK), f32)

        for h in range(4):
            cols = slice(h * POOL_CH, (h + 1) * POOL_CH)
            vn_h = vn[:, cols]
            s = _dot(ws[h], vn_h) + bias_ref[:, h:h + 1]
            silu, dsilu = _silu_and_grad(cg_ref[:, cols])
            dyc = dyc_ref[:, cols]
            u = u_ref[:, cols]
            du_ref[:, cols] = (dyc * s * silu).astype(bf16)
            dcg_ref[:, cols] = (dyc * u * s * dsilu).astype(bf16)
            ds = dyc * u * silu
            dbias_ref[:, h:h + 1] += jnp.sum(ds, axis=1, keepdims=True)
            ds = ds.astype(bf16)
            _accumulate(dw_ref.at[h], jnp.where(tril, _dot_nt(ds, vn_h), 0.0), first)
            dvn_ref[:, cols] = _dot_tn(ws[h], ds)
        dv, dg, db = _ln_bwd(xh, rstd, g, dvn_ref[...])
        dv_ref[...] = dv.astype(bf16)
        _accumulate(dg_ref, dg, first)
        _accumulate(db_ref, db, first)

    chunk, vec, w, bias = _sgu_specs()
    return pl.pallas_call(
        body, name="sgu_bwd", grid=(SEQ // CHUNK,),
        in_specs=[chunk(0), chunk(1), chunk(2), chunk(0), vec, vec, w, bias],
        out_specs=[chunk(0), chunk(0), chunk(0), w, bias, vec, vec],
        out_shape=[SDS((SEQ, HALF), bf16)] * 3 + [SDS((4, CHUNK, CHUNK), f32), SDS((CHUNK, CHUNK), f32),
                                                   SDS((1, HALF), f32), SDS((1, HALF), f32)],
        scratch_shapes=[pltpu.VMEM((CHUNK, HALF), f32)],
    )(z1, z1, z1, dcat, ln_g, ln_b, sgu_w, bias_t)


CONV_TILE = 128
DVAL_COL, DGLU_COL = 12, 16


def _conv_specs():
    val = pl.BlockSpec((SEQ, POOL_CH), lambda j: (0, DVAL_COL + j))
    glu = pl.BlockSpec((SEQ, POOL_CH), lambda j: (0, DGLU_COL + j))
    w = pl.BlockSpec((CONV_K, POOL_CH), lambda j: (0, j))
    col = pl.BlockSpec((SEQ, POOL_CH), lambda j: (0, j))
    vec = pl.BlockSpec((1, POOL_CH), lambda j: (0, j))
    return val, glu, w, col, vec


def _conv_fwd(z1, conv_w, conv_b):
    def body(val_ref, glu_ref, w_ref, b_ref, out_ref, xpad):
        xpad[0:CONV_PAD, :] = jnp.zeros((CONV_PAD, POOL_CH), f32)
        xpad[CONV_PAD:, :] = val_ref[...] * _sigmoid(glu_ref[...])
        w = w_ref[...]
        bias = b_ref[...]

        def tile(i, carry):
            t0 = pl.multiple_of(i * CONV_TILE, CONV_TILE)
            window = xpad[pl.ds(t0, CONV_TILE + CONV_PAD), :]
            acc = jnp.broadcast_to(bias, (CONV_TILE, POOL_CH))
            for k in range(CONV_K):
                shift = CONV_PAD - (CONV_K - 1) + k
                acc = acc + w[k:k + 1, :] * pltpu.roll(window, CONV_TILE + CONV_PAD - shift, 0)[0:CONV_TILE]
            out_ref[pl.ds(t0, CONV_TILE), :] = acc
            return carry

        lax.fori_loop(0, SEQ // CONV_TILE, tile, 0)

    val, glu, w, col, vec = _conv_specs()
    return pl.pallas_call(
        body, name="conv_fwd", grid=(4,), in_specs=[val, glu, w, vec], out_specs=col,
        out_shape=SDS((SEQ, HALF), f32), scratch_shapes=[pltpu.VMEM((SEQ + CONV_PAD, POOL_CH), f32)],
    )(z1, z1, conv_w, conv_b)


def _conv_bwd(z1, dconv, conv_w):
    def body(val_ref, glu_ref, w_ref, dout_ref, dval_ref, dglu_ref, dw_ref, db_ref, xpad, dpad, dx_ref):
        val = val_ref[...]
        sig = _sigmoid(glu_ref[...])
        xpad[0:CONV_PAD, :] = jnp.zeros((CONV_PAD, POOL_CH), f32)
        xpad[CONV_PAD:, :] = val * sig
        dout = dout_ref[...]
        dpad[0:SEQ, :] = dout
        dpad[SEQ:, :] = jnp.zeros((CONV_PAD, POOL_CH), f32)
        db_ref[...] = jnp.sum(dout, axis=0, keepdims=True)
        dw_ref[...] = jnp.zeros((CONV_K, POOL_CH), f32)
        w = w_ref[...]

        def tile(i, carry):
            t0 = pl.multiple_of(i * CONV_TILE, CONV_TILE)
            x_win = xpad[pl.ds(t0, CONV_TILE + CONV_PAD), :]
            d_win = dpad[pl.ds(t0, CONV_TILE + CONV_PAD), :]
            d_own = d_win[0:CONV_TILE]
            acc = jnp.zeros((CONV_TILE, POOL_CH), f32)
            for k in range(CONV_K):
                shift = CONV_PAD - (CONV_K - 1) + k
                x_k = pltpu.roll(x_win, CONV_TILE + CONV_PAD - shift, 0)[0:CONV_TILE]
                dw_ref[k:k + 1, :] += jnp.sum(d_own * x_k, axis=0, keepdims=True)
                back = CONV_K - 1 - k
                d_k = d_own if back == 0 else pltpu.roll(d_win, CONV_TILE + CONV_PAD - back, 0)[0:CONV_TILE]
                acc = acc + w[k:k + 1, :] * d_k
            dx_ref[pl.ds(t0, CONV_TILE), :] = acc
            return carry

        lax.fori_loop(0, SEQ // CONV_TILE, tile, 0)
        dx = dx_ref[...]
        dval_ref[...] = (dx * sig).astype(bf16)
        dglu_ref[...] = (dx * val * sig * (1.0 - sig)).astype(bf16)

    val, glu, w, col, vec = _conv_specs()
    pad = pltpu.VMEM((SEQ + CONV_PAD, POOL_CH), f32)
    return pl.pallas_call(
        body, name="conv_bwd", grid=(4,), in_specs=[val, glu, w, col], out_specs=[col, col, w, vec],
        out_shape=[SDS((SEQ, HALF), bf16), SDS((SEQ, HALF), bf16), SDS((CONV_K, HALF), f32), SDS((1, HALF), f32)],
        scratch_shapes=[pad, pad, pltpu.VMEM((SEQ, POOL_CH), f32)],
    )(z1, z1, conv_w, dconv)


DGATE_COL = 5


def _conv_norm_fwd(conv, z1, g, b):
    def body(c_ref, gate_ref, g_ref, b_ref, yd_ref):
        xh, _ = _ln_stats(c_ref[...])
        n = xh * g_ref[...] + b_ref[...]
        gate = gate_ref[...]
        yd_ref[...] = (n * _sigmoid(n) * gate * _sigmoid(gate)).astype(bf16)

    return pl.pallas_call(
        body, name="conv_norm_fwd", grid=(SEQ // ROWS,),
        in_specs=[_row_spec(HALF), _row_spec(HALF, DGATE_COL), _vec_spec(HALF), _vec_spec(HALF)],
        out_specs=_row_spec(HALF), out_shape=SDS((SEQ, HALF), bf16))(conv, z1, g, b)


def _conv_norm_bwd(conv, z1, dcat, g, b):
    def body(c_ref, gate_ref, dyd_ref, g_ref, b_ref, dconv_ref, dgate_ref, dg_ref, db_ref):
        first = pl.program_id(0) == 0
        xh, rstd = _ln_stats(c_ref[...])
        g = g_ref[...]
        n_silu, n_dsilu = _silu_and_grad(xh * g + b_ref[...])
        gate_silu, gate_dsilu = _silu_and_grad(gate_ref[...])
        dyd = dyd_ref[...]
        dgate_ref[...] = (dyd * n_silu * gate_dsilu).astype(bf16)
        dconv, dg, db = _ln_bwd(xh, rstd, g, dyd * gate_silu * n_dsilu)
        dconv_ref[...] = dconv
        _accumulate(dg_ref, dg, first)
        _accumulate(db_ref, db, first)

    return pl.pallas_call(
        body, name="conv_norm_bwd", grid=(SEQ // ROWS,),
        in_specs=[_row_spec(HALF), _row_spec(HALF, DGATE_COL), _row_spec(HALF, 1), _vec_spec(HALF), _vec_spec(HALF)],
        out_specs=[_row_spec(HALF), _row_spec(HALF), _vec_spec(HALF), _vec_spec(HALF)],
        out_shape=[SDS((SEQ, HALF), f32), SDS((SEQ, HALF), bf16), SDS((1, HALF), f32), SDS((1, HALF), f32)],
    )(conv, z1, dcat, g, b)


def _step(x, target, w, chip):
    chip_vec = chip.astype(jnp.int32).reshape(1)
    sharded_names = list(SHARDED_SMALL)
    small_shard = _pack([w[k] for k in sharded_names], total_rows=SMALL_SHARD_ROWS)
    small_slot = lax.dynamic_update_slice(jnp.zeros((N_CHIPS, SMALL_SHARD_ROWS, LANES), f32), small_shard[None], (chip, 0, 0))
    first = [small_slot, _cast_into_slot(w["e_w_in"], chip_vec, "cast_e_w_in0", small_slot, 0, E_IN_PIECES)]
    sems, bufs, token = _gather_start(first, "gather_start_first")
    more = [_cast_into_slot(w["e_w_in"], chip_vec, f"cast_e_w_in{i}", token, i, E_IN_PIECES) for i in range(1, E_IN_PIECES)]
    more_sems, more_bufs, token = _gather_start(more, "gather_start_pieces")
    rest = [_cast_into_slot(w[k], chip_vec, f"cast_{k}", token) for k in BIG[1:]]
    rest_sems, rest_bufs, token = _gather_start(rest, "gather_start_rest")
    sems, bufs = sems + more_sems + rest_sems, bufs + more_bufs + rest_bufs
    tables = _rope_tables()

    def vec(k):
        return w[k].reshape(1, -1)

    h0 = _pre_norm(x, vec("e_pre_norm") + token[0, 0])
    after, z0, e_w_in = h0, None, []
    for i in range(E_IN_PIECES):
        group = slice(0, 2) if i == 0 else slice(1 + i, 2 + i)
        landed = _forward_halves(_gather_wait(bufs[group], sems[group], after, f"gather_wait_{i}"), f"forward_{i}")
        if i == 0:
            small_full = landed[0]
        e_w_in.append(landed[-1])
        z0 = _mm_nn(h0, landed[-1], f32, f"e_in{i}", i, E_IN_PIECES, z0)
        after = z0
    p = {k: _from_chips(k, a) for k, a in zip(sharded_names, _unpack(small_full, [SHARDED_SMALL[k][0] for k in sharded_names]))}
    for k in ("o_pre_norm", "o_sgu_norm_g", "o_sgu_norm_b", "o_conv_b", "o_conv_norm_g", "o_conv_norm_b", "o_post_norm"):
        p[k] = p[k].reshape(1, -1)
    pool_w_bf = p["e_pool_w"].astype(bf16)
    bias_t = jnp.pad(w["o_sgu_b"].T, ((0, 0), (0, CHUNK - 4)))

    ya = _pool_fwd(z0, pool_w_bf, vec("e_pool_scale"))
    yb, att, lse, qkv_by_residue = _attn_fwd(z0, tables)

    def arrived(index, after, name):
        one = slice(index, index + 1)
        return _forward_halves(_gather_wait(bufs[one], sems[one], after, f"gather_wait_{name}"), f"forward_{name}")[0]

    e_w_out = arrived(1 + E_IN_PIECES, att, "e_w_out").reshape(1, D_MODEL, D_MODEL)
    cat0 = jnp.concatenate([ya, yb], axis=1)
    y0 = _mm_nn(cat0, e_w_out, f32, "e_out")
    x1, h1 = _mid_norm(x, y0, vec("e_post_norm"), p["o_pre_norm"])
    o_w_in = arrived(2 + E_IN_PIECES, h1, "o_w_in")
    z1 = _mm_nn(h1, o_w_in, f32, "o_in")
    yc = _sgu_fwd(z1, p["o_sgu_norm_g"], p["o_sgu_norm_b"], w["o_sgu_w"], bias_t)
    conv = _conv_fwd(z1, p["o_conv_w"], p["o_conv_b"])
    yd = _conv_norm_fwd(conv, z1, p["o_conv_norm_g"], p["o_conv_norm_b"])
    o_w_out = arrived(3 + E_IN_PIECES, yd, "o_w_out").reshape(1, D_MODEL, D_MODEL)
    cat1 = jnp.concatenate([yc, yd], axis=1)
    y1 = _mm_nn(cat1, o_w_out, f32, "o_out")
    loss, dx2, dy1, g_o_post = _final_norm_loss(x1, y1, p["o_post_norm"], target)

    in_flight = {}

    def send_off(name, grad):
        sem, sums, land, tok = _scatter_start(_swap_add(grad, f"swap_add_{name}"), f"scatter_start_{name}")
        in_flight[name] = (sem, sums, land)
        return tok

    tok = send_off("o_w_out", _mm_tn(cat1, dy1, 1, "o_out_dw").reshape(N_CHIPS, HALF // 2, D_MODEL))
    dcat1 = _mm_nt(dy1, [o_w_out], "o_out_dx", tok)
    du, dv, dcg, g_sgu_w, g_bias_t, g_sgu_g, g_sgu_b = _sgu_bwd(
        z1, dcat1, p["o_sgu_norm_g"] + tok[0, 0], p["o_sgu_norm_b"], w["o_sgu_w"], bias_t)
    dconv, ddgate, g_cn_g, g_cn_b = _conv_norm_bwd(conv, z1, dcat1, p["o_conv_norm_g"], p["o_conv_norm_b"])
    ddval, ddglu, g_conv_w, g_conv_b = _conv_bwd(z1, dconv, p["o_conv_w"])
    dz1 = jnp.concatenate([du, dv, dcg, ddval, ddglu, ddgate], axis=1)
    tok = send_off("o_w_in", _mm_tn(h1, dz1, N_CHIPS, "o_in_dw"))
    dh1 = _mm_nt(dz1, [o_w_in], "o_in_dx", tok)
    dx1, dy0, g_o_pre, g_e_post = _mid_norm_bwd(dx2, dh1, x1, y0, p["o_pre_norm"] + tok[0, 0], vec("e_post_norm"))

    tok = send_off("e_w_out", _mm_tn(cat0, dy0, 1, "e_out_dw").reshape(N_CHIPS, HALF // 2, D_MODEL))
    dcat0 = _mm_nt(dy0, [e_w_out], "e_out_dx", tok)
    da, dagate, g_pool_w, g_pool_scale = _pool_bwd(z0, dcat0, pool_w_bf, vec("e_pool_scale") + tok[0, 0])
    dq0, dk0, dv0, dbgate = _attn_bwd_group(0, qkv_by_residue[0], z0, att, lse, dcat0, tables)
    dq1, dk1, dv1 = _attn_bwd_group(1, qkv_by_residue[1], z0, att, lse, dcat0, tables)
    dq2, dk2, dv2 = _attn_bwd_group(2, qkv_by_residue[2], z0, att, lse, dcat0, tables)
    dz0 = jnp.concatenate([da, dagate, dq0, dq1, dq2, dk0, dk1, dk2, dv0, dv1, dv2, dbgate], axis=1)
    tok = send_off("e_w_in", _mm_tn(h0, dz0, N_CHIPS, "e_in_dw"))
    dh0 = _mm_nt(dz0, e_w_in, "e_in_dx", tok)
    grad_x, g_e_pre = _pre_norm_bwd(dx1, dh0, x, vec("e_pre_norm") + tok[0, 0])

    small = {"e_pre_norm": g_e_pre, "e_pool_w": g_pool_w, "e_pool_scale": g_pool_scale, "e_post_norm": g_e_post,
             "o_pre_norm": g_o_pre, "o_sgu_norm_g": g_sgu_g, "o_sgu_norm_b": g_sgu_b, "o_sgu_w": g_sgu_w,
             "o_sgu_b": g_bias_t, "o_conv_w": g_conv_w, "o_conv_b": g_conv_b,
             "o_conv_norm_g": g_cn_g, "o_conv_norm_b": g_cn_b, "o_post_norm": g_o_post}
    return loss, grad_x, in_flight, small


def _land(in_flight, name, chip, after):
    sems, sums, land = in_flight[name]
    sums, land = _scatter_wait(sems, sums, land, after, f"scatter_wait_{name}")
    return _add_landed_join(sums, land, chip.astype(jnp.int32).reshape(1), f"add_landed_{name}")


def _place():
    x, y, c = lax.axis_index("x"), lax.axis_index("y"), lax.axis_index("c")
    others = [(1 - x, y), (x, 1 - y), (1 - x, 1 - y)]
    return x, y, c, 2 * x + y, others


SWAP_ROWS = 256
FORWARD_STAGE_BYTES = 4 << 20


def _swap_add(g, name):
    chips, r, c = g.shape
    half = r // 2
    rows_per_step = 2 * SWAP_ROWS if half % (2 * SWAP_ROWS) == 0 else SWAP_ROWS
    nb = half // rows_per_step
    steps = chips * nb

    def body(core_ref, mine_ref, theirs_ref, out_ref, landing, send_sems, recv_sems, free_sems):
        i = pl.program_id(0)
        x, y, core, _, _ = _place()
        sibling = (x, y, 1 - core)

        def send(slot):
            return pltpu.make_async_remote_copy(src_ref=theirs_ref, dst_ref=landing.at[slot], send_sem=send_sems.at[slot],
                                                recv_sem=recv_sems.at[slot], device_id=sibling, device_id_type=MESH)

        @pl.when(i < steps)
        def _():
            @pl.when(i >= 2)
            def _():
                pl.semaphore_wait(free_sems.at[i % 2], 1)

            send(i % 2).start()

        @pl.when(i >= 1)
        def _():
            landed = (i - 1) % 2
            send(landed).wait_recv()
            out_ref[...] = (mine_ref[...].astype(f32) + landing[landed].astype(f32)).astype(out_ref.dtype)

            @pl.when(i + 1 < steps)
            def _():
                pl.semaphore_signal(free_sems.at[landed], 1, device_id=sibling, device_id_type=MESH)

        @pl.when(i < steps)
        def _():
            send(i % 2).wait_send()

    def rows_of(b, h):
        return (2 * (b // nb) + h) * nb + b % nb

    block = (rows_per_step, c)
    grid_spec = pltpu.PrefetchScalarGridSpec(
        num_scalar_prefetch=1, grid=(steps + 1,),
        in_specs=[pl.BlockSpec(block, lambda i, core: (rows_of(jnp.maximum(i - 1, 0), core[0]), 0)),
                  pl.BlockSpec(block, lambda i, core: (rows_of(jnp.minimum(i, steps - 1), 1 - core[0]), 0))],
        out_specs=pl.BlockSpec(block, lambda i, core: (jnp.maximum(i - 1, 0), 0)),
        scratch_shapes=[pltpu.VMEM((2, rows_per_step, c), g.dtype), pltpu.SemaphoreType.DMA((2,)),
                        pltpu.SemaphoreType.DMA((2,)), pltpu.SemaphoreType.REGULAR((2,))])
    core = lax.axis_index("c").astype(jnp.int32).reshape(1)
    rows = g.reshape(chips * r, c)
    out = pl.pallas_call(body, name=name, grid_spec=grid_spec, out_shape=SDS((chips * half, c), g.dtype))(core, rows, rows)
    return out.reshape(chips, half, c)


HBM = pl.BlockSpec(memory_space=pltpu.HBM)
SEM = pl.BlockSpec(memory_space=pltpu.SEMAPHORE)
EFFECT = pltpu.SideEffectType.DATAFLOW_SIDE_EFFECTING


def _in_hbm(a):
    return pltpu.with_memory_space_constraint(a, pltpu.HBM)


def _cast_into_slot(w, chip, name, after, piece=0, pieces=1):
    r, c = w.shape
    c = c // pieces
    nb = r // SWAP_ROWS

    def body(chip_ref, w_ref, after_ref, o_ref):
        o_ref[...] = w_ref[...].astype(bf16)

    grid_spec = pltpu.PrefetchScalarGridSpec(
        num_scalar_prefetch=1, grid=(nb,),
        in_specs=[pl.BlockSpec((SWAP_ROWS, c), lambda i, chip: (i, piece)), ANY],
        out_specs=pl.BlockSpec((SWAP_ROWS, c), lambda i, chip: (chip[0] * nb + i, 0)))
    out = pl.pallas_call(body, name=name, grid_spec=grid_spec, out_shape=SDS((N_CHIPS * r, c), bf16))(chip, w, after)
    return out.reshape(N_CHIPS, r, c)


def _gather_start(bufs, name):
    n = len(bufs)

    def body(*refs):
        ins, sems, token = refs[:n], refs[n:3 * n], refs[4 * n]
        x, y, c, me, others = _place()
        for a in range(n):
            rows = ins[a].shape[1] // 2
            mine = ins[a].at[me, pl.ds(c * rows, rows), :]
            for k, (ox, oy) in enumerate(others):
                pltpu.make_async_remote_copy(src_ref=mine, dst_ref=mine, send_sem=sems[2 * a].at[k],
                                             recv_sem=sems[2 * a + 1].at[k], device_id=(ox, oy, c),
                                             device_id_type=MESH).start()
        token[...] = jnp.zeros_like(token)

    out = pl.pallas_call(
        body, name=name, in_specs=[HBM] * n,
        out_shape=(*[pltpu.SemaphoreType.DMA((3,))] * (2 * n), *[pltpu.HBM(b.shape, b.dtype) for b in bufs],
                   SDS((8, 128), f32)),
        out_specs=(*[SEM] * (2 * n), *[HBM] * n, pl.BlockSpec(memory_space=pltpu.VMEM)),
        input_output_aliases={a: 2 * n + a for a in range(n)},
        compiler_params=pltpu.CompilerParams(has_side_effects=EFFECT),
    )(*[_in_hbm(b) for b in bufs])
    return [(out[2 * a], out[2 * a + 1]) for a in range(n)], list(out[2 * n:3 * n]), out[3 * n]


def _gather_wait(bufs, sems, after, name):
    n = len(bufs)

    def body(*refs):
        ins, sem_refs = refs[:n], refs[n:3 * n]
        x, y, c, me, others = _place()
        for a in range(n):
            rows = ins[a].shape[1] // 2
            mine = ins[a].at[me, pl.ds(c * rows, rows), :]
            for k, (ox, oy) in enumerate(others):
                landed = ins[a].at[2 * ox + oy, pl.ds(c * rows, rows), :]
                copy = pltpu.make_async_remote_copy(src_ref=mine, dst_ref=landed, send_sem=sem_refs[2 * a].at[k],
                                                    recv_sem=sem_refs[2 * a + 1].at[k], device_id=(ox, oy, c),
                                                    device_id_type=MESH)
                copy.wait_send()
                copy.wait_recv()

    flat_sems = [s for pair in sems for s in pair]
    out = pl.pallas_call(
        body, name=name, in_specs=[HBM] * n + [SEM] * (2 * n) + [ANY],
        out_shape=tuple(pltpu.HBM(b.shape, b.dtype) for b in bufs), out_specs=tuple([HBM] * n),
        input_output_aliases={a: a for a in range(n)},
        compiler_params=pltpu.CompilerParams(has_side_effects=EFFECT),
    )(*bufs, *flat_sems, after)
    return list(out)


def _forward_halves(bufs, name):
    n = len(bufs)
    blocks = []
    for b in bufs:
        half = b.shape[1] // 2
        whole = half * b.shape[2] * b.dtype.itemsize <= FORWARD_STAGE_BYTES
        blocks.append((half, half if whole or half % SWAP_ROWS else SWAP_ROWS))
    work = [(a, k, b) for a in range(n) for k in range(3) for b in range(blocks[a][0] // blocks[a][1])]

    def body(*refs):
        outs, stages = refs[n:2 * n], refs[2 * n:3 * n]
        load_sems, send_sems, recv_sems = refs[3 * n:]
        x, y, c, me, others = _place()
        sibling = (x, y, 1 - c)

        def rows(item):
            a, k, b = item
            half, tr = blocks[a]
            ox, oy = others[k]
            return outs[a].at[2 * ox + oy, pl.ds(c * half + b * tr, tr), :]

        def load(s, item):
            return pltpu.make_async_copy(rows(item), stages[item[0]].at[s], load_sems.at[s])

        def send(s, item):
            return pltpu.make_async_remote_copy(src_ref=stages[item[0]].at[s], dst_ref=rows(item), send_sem=send_sems.at[s],
                                                recv_sem=recv_sems.at[item[0]], device_id=sibling, device_id_type=MESH)

        load(0, work[0]).start()
        for t, item in enumerate(work):
            s = t % 2
            load(s, item).wait()
            send(s, item).start()
            if t + 1 < len(work):
                if t >= 1:
                    send(1 - s, work[t - 1]).wait_send()
                load(1 - s, work[t + 1]).start()
        if len(work) > 1:
            send(len(work) % 2, work[-2]).wait_send()
        send((len(work) - 1) % 2, work[-1]).wait_send()
        for a in range(n):
            theirs = outs[a].at[pl.ds(0, 3), pl.ds(0, blocks[a][0]), :]
            pltpu.make_async_remote_copy(src_ref=theirs, dst_ref=theirs, send_sem=send_sems.at[0], recv_sem=recv_sems.at[a],
                                         device_id=sibling, device_id_type=MESH).wait_recv()

    out = pl.pallas_call(
        body, name=name, in_specs=[ANY] * n, out_specs=[ANY] * n, out_shape=[SDS(b.shape, b.dtype) for b in bufs],
        input_output_aliases={a: a for a in range(n)},
        scratch_shapes=[pltpu.VMEM((2, blocks[a][1], bufs[a].shape[2]), bufs[a].dtype) for a in range(n)]
        + [pltpu.SemaphoreType.DMA((2,)), pltpu.SemaphoreType.DMA((2,)), pltpu.SemaphoreType.DMA((n,))],
    )(*bufs)
    return list(out)


def _scatter_start(chip_sums, name):
    def body(a_ref, land_ref, send_sems, recv_sems, a_thru, land_thru, token):
        x, y, c, me, others = _place()
        for k, (ox, oy) in enumerate(others):
            pltpu.make_async_remote_copy(src_ref=a_ref.at[2 * ox + oy], dst_ref=land_ref.at[me], send_sem=send_sems.at[k],
                                         recv_sem=recv_sems.at[k], device_id=(ox, oy, c), device_id_type=MESH).start()
        token[...] = jnp.zeros_like(token)

    shape = pltpu.HBM(chip_sums.shape, chip_sums.dtype)
    send, recv, a_thru, land, token = pl.pallas_call(
        body, name=name, in_specs=[HBM, HBM],
        out_shape=(pltpu.SemaphoreType.DMA((3,)), pltpu.SemaphoreType.DMA((3,)), shape, shape, SDS((8, 128), f32)),
        out_specs=(SEM, SEM, HBM, HBM, pl.BlockSpec(memory_space=pltpu.VMEM)), input_output_aliases={0: 2, 1: 3},
        compiler_params=pltpu.CompilerParams(has_side_effects=EFFECT),
    )(_in_hbm(chip_sums), _in_hbm(lax.empty(chip_sums.shape, chip_sums.dtype)))
    return (send, recv), a_thru, land, token


def _scatter_wait(sems, chip_sums, land, after, name):
    def body(a_ref, land_ref, send_sems, recv_sems, after_ref, a_out, land_out):
        x, y, c, me, others = _place()
        for k, (ox, oy) in enumerate(others):
            copy = pltpu.make_async_remote_copy(
                src_ref=a_ref.at[2 * ox + oy], dst_ref=land_ref.at[2 * ox + oy], send_sem=send_sems.at[k],
                recv_sem=recv_sems.at[k], device_id=(ox, oy, c), device_id_type=MESH)
            copy.wait_send()
            copy.wait_recv()

    shape = pltpu.HBM(chip_sums.shape, chip_sums.dtype)
    return pl.pallas_call(
        body, name=name, in_specs=[HBM, HBM, SEM, SEM, ANY], out_shape=(shape, shape), out_specs=(HBM, HBM),
        input_output_aliases={0: 0, 1: 1}, compiler_params=pltpu.CompilerParams(has_side_effects=EFFECT),
    )(chip_sums, land, sems[0], sems[1], after)


def _add_landed_join(chip_sums, land, chip, name):
    chips, rh, c = chip_sums.shape
    nb = rh // SWAP_ROWS

    def body(chip_ref, own_ref, l1_ref, l2_ref, l3_ref, out_hbm, buf, send_sems, recv_sem, local_sems):
        i = pl.program_id(0)
        slot = i % 2
        x, y, core, _, _ = _place()
        sibling = (x, y, 1 - core)

        def copies(s, step):
            rows = pl.ds(pl.multiple_of((core * nb + step) * SWAP_ROWS, SWAP_ROWS), SWAP_ROWS)
            keep = pltpu.make_async_copy(buf.at[s], out_hbm.at[rows, :], local_sems.at[s])
            give = pltpu.make_async_remote_copy(src_ref=buf.at[s], dst_ref=out_hbm.at[rows, :], send_sem=send_sems.at[s],
                                                recv_sem=recv_sem.at[0], device_id=sibling, device_id_type=MESH)
            return keep, give

        def drain(s, step):
            keep, give = copies(s, step)
            keep.wait()
            give.wait_send()

        @pl.when(i >= 2)
        def _():
            drain(slot, i - 2)

        buf[slot] = ((own_ref[...].astype(f32) + l1_ref[...].astype(f32)) + l2_ref[...].astype(f32)) + l3_ref[...].astype(f32)
        keep, give = copies(slot, i)
        keep.start()
        give.start()

        @pl.when(i == nb - 1)
        def _():
            drain(slot, i)
            if nb > 1:
                drain(1 - slot, i - 1)
            theirs = out_hbm.at[pl.ds((1 - core) * rh, rh), :]
            pltpu.make_async_remote_copy(src_ref=theirs, dst_ref=theirs, send_sem=send_sems.at[0], recv_sem=recv_sem.at[0],
                                         device_id=sibling, device_id_type=MESH).wait_recv()

    block = (SWAP_ROWS, c)
    from_slot = lambda d: pl.BlockSpec(block, lambda i, chip: (((chip[0] + d) % chips) * nb + i, 0))
    grid_spec = pltpu.PrefetchScalarGridSpec(
        num_scalar_prefetch=1, grid=(nb,), in_specs=[from_slot(0), from_slot(1), from_slot(2), from_slot(3)],
        out_specs=ANY,
        scratch_shapes=[pltpu.VMEM((2, SWAP_ROWS, c), f32), pltpu.SemaphoreType.DMA((2,)),
                        pltpu.SemaphoreType.DMA((1,)), pltpu.SemaphoreType.DMA((2,))])
    land_rows = land.reshape(chips * rh, c)
    return pl.pallas_call(body, name=name, grid_spec=grid_spec, out_shape=SDS((2 * rh, c), f32))(
        chip, chip_sums.reshape(chips * rh, c), land_rows, land_rows, land_rows)


def _adamw_update(w_ref, g_ref, m_ref, v_ref, d_ref, nm_ref, nv_ref):
    g = g_ref[...]
    nm = ADAM_B1 * m_ref[...] + (1.0 - ADAM_B1) * g
    nv = ADAM_B2 * v_ref[...] + (1.0 - ADAM_B2) * (g * g)
    nm_ref[...] = nm
    nv_ref[...] = nv
    m_hat = nm / (1.0 - ADAM_B1 ** ADAM_STEP)
    v_hat = nv / (1.0 - ADAM_B2 ** ADAM_STEP)
    d_ref[...] = -ADAM_LR * (m_hat / (jnp.sqrt(v_hat) + ADAM_EPS) + ADAM_WD * w_ref[...])


def _adamw(w, g, m, v, name):
    r, c = w.shape
    tr = 128 if r % 128 == 0 else r

    def body(w_ref, g_ref, m_ref, v_ref, g_out_ref, d_ref, nm_ref, nv_ref):
        g_out_ref[...] = g_ref[...]
        _adamw_update(w_ref, g_ref, m_ref, v_ref, d_ref, nm_ref, nv_ref)

    spec = pl.BlockSpec((tr, c), lambda i: (i, 0))
    return pl.pallas_call(body, name=name, grid=(r // tr,), in_specs=[spec] * 4, out_specs=[spec] * 4,
                          out_shape=[SDS((r, c), f32)] * 4)(w, g, m, v)


SMALL_PACKING = {
    "e_pre_norm": ((1, 2048), 8, (1, 2048)), "e_pool_w": ((1024, 256), 1024, (256, 256)),
    "e_pool_scale": ((1, 1024), 8, (1, 1024)), "e_post_norm": ((1, 2048), 8, (1, 2048)),
    "o_pre_norm": ((1, 2048), 8, (1, 512)), "o_sgu_norm_g": ((1, 1024), 8, (1, 256)),
    "o_sgu_norm_b": ((1, 1024), 8, (1, 256)), "o_sgu_w": ((512, 128), 512, (512, 128)),
    "o_sgu_b": ((128, 128), 8, (4, 128)), "o_conv_w": ((31, 1024), 128, (31, 256)), "o_conv_b": ((1, 1024), 8, (1, 256)),
    "o_conv_norm_g": ((1, 1024), 8, (1, 256)), "o_conv_norm_b": ((1, 1024), 8, (1, 256)),
    "o_post_norm": ((1, 2048), 8, (1, 512)),
}
SMALL_PACKED_ROWS = 1792
RELATIONS = [(fx, fy, fc) for fx in (0, 1) for fy in (0, 1) for fc in (0, 1)][1:]


def _small_finalize(grads, ws, ms, vs, after):
    names = list(SMALL_ORDER)
    n = len(names)
    half, piece = SMALL_PACKED_ROWS // 2, SMALL_PACKED_ROWS // 8
    first_row, row = {}, 0
    for k in names:
        first_row[k] = row
        row += SMALL_PACKING[k][1]

    def body(*refs):
        g_refs, total = refs[0:n], refs[n + 1]
        pack, from_sibling, from_chips, send_a, recv_a, send_b, recv_b, send_c, recv_c = refs[n + 2:]
        x, y, c, me, others = _place()

        for r0 in range(0, SMALL_PACKED_ROWS, piece):
            pack[r0:r0 + piece, :] = jnp.zeros((piece, LANES), f32)
        for k, g_ref in zip(names, g_refs):
            (rows, width), _, _ = SMALL_PACKING[k]
            r0 = first_row[k]
            if k == "o_sgu_b":
                pack[r0:r0 + 4, 0:CHUNK] = g_ref[...].T[0:4, :]
            elif width < LANES:
                pack[r0:r0 + rows, 0:width] = g_ref[...]
            else:
                for j in range(width // LANES):
                    dst = r0 + j * (1 if rows == 1 else 32)
                    pack[dst:dst + rows, :] = g_ref[:, j * LANES:(j + 1) * LANES]

        sibling = (x, y, 1 - c)
        swap = pltpu.make_async_remote_copy(
            src_ref=pack.at[pl.ds(pl.multiple_of((1 - c) * half, 8), half), :], dst_ref=from_sibling,
            send_sem=send_a.at[0], recv_sem=recv_a.at[0], device_id=sibling, device_id_type=MESH)
        swap.start()
        swap.wait()
        for j in range(4):
            rows = pl.ds(pl.multiple_of(c * half + j * piece, 8), piece)
            pack[rows, :] = pack[rows, :] + from_sibling[j * piece:(j + 1) * piece, :]

        def piece_of(chip):
            return pl.ds(pl.multiple_of(c * half + chip * piece, 8), piece)

        def to_chip(k):
            ox, oy = others[k]
            return pltpu.make_async_remote_copy(
                src_ref=pack.at[piece_of(2 * ox + oy), :], dst_ref=from_chips.at[me], send_sem=send_b.at[k],
                recv_sem=recv_b.at[k], device_id=(ox, oy, c), device_id_type=MESH)

        for k in range(3):
            to_chip(k).start()
        from_chips[me] = pack[piece_of(me), :]
        for k, (ox, oy) in enumerate(others):
            landed = from_chips.at[2 * ox + oy]
            pltpu.make_async_remote_copy(src_ref=landed, dst_ref=landed, send_sem=send_b.at[k], recv_sem=recv_b.at[k],
                                         device_id=(ox, oy, c), device_id_type=MESH).wait_recv()
        for k in range(3):
            to_chip(k).wait_send()
        mine = pl.ds(pl.multiple_of(c * half + me * piece, 8), piece)
        total[mine, :] = ((from_chips[0] + from_chips[1]) + from_chips[2]) + from_chips[3]

        def flip(v, f):
            return 1 - v if f else v

        def to_device(r):
            fx, fy, fc = RELATIONS[r]
            return pltpu.make_async_remote_copy(
                src_ref=total.at[mine, :], dst_ref=total.at[mine, :], send_sem=send_c.at[r], recv_sem=recv_c.at[r],
                device_id=(flip(x, fx), flip(y, fy), flip(c, fc)), device_id_type=MESH)

        for r in range(len(RELATIONS)):
            to_device(r).start()
        for r, (fx, fy, fc) in enumerate(RELATIONS):
            theirs = pl.ds(pl.multiple_of(flip(c, fc) * half + (2 * flip(x, fx) + flip(y, fy)) * piece, 8), piece)
            pltpu.make_async_remote_copy(
                src_ref=total.at[theirs, :], dst_ref=total.at[theirs, :], send_sem=send_c.at[r], recv_sem=recv_c.at[r],
                device_id=(flip(x, fx), flip(y, fy), flip(c, fc)), device_id_type=MESH).wait_recv()
        for r in range(len(RELATIONS)):
            to_device(r).wait_send()

    whole = pl.BlockSpec(memory_space=pltpu.VMEM)
    total = pl.pallas_call(
        body, name="small_allreduce", in_specs=[whole] * n + [ANY], out_specs=whole,
        out_shape=SDS((SMALL_PACKED_ROWS, LANES), f32),
        scratch_shapes=[pltpu.VMEM((SMALL_PACKED_ROWS, LANES), f32), pltpu.VMEM((half, LANES), f32),
                        pltpu.VMEM((N_CHIPS, piece, LANES), f32),
                        pltpu.SemaphoreType.DMA((1,)), pltpu.SemaphoreType.DMA((1,)), pltpu.SemaphoreType.DMA((3,)),
                        pltpu.SemaphoreType.DMA((3,)), pltpu.SemaphoreType.DMA((7,)), pltpu.SemaphoreType.DMA((7,))],
    )(*grads, after)

    def update(*refs):
        total = refs[0]
        w_refs, m_refs, v_refs = refs[1:n + 1], refs[n + 1:2 * n + 1], refs[2 * n + 1:3 * n + 1]
        outs = refs[3 * n + 1:]
        me = 2 * lax.axis_index("x") + lax.axis_index("y")

        def of_chip(candidates):
            value = candidates[0]
            for j in range(1, N_CHIPS):
                value = jnp.where(me == j, candidates[j], value)
            return value

        for i, k in enumerate(names):
            (rows, width), _, (local_rows, local_width) = SMALL_PACKING[k]
            r0 = first_row[k]
            if k == "o_sgu_b":
                g = total[r0:r0 + 4, 0:CHUNK]
            elif k == "e_pool_w":
                for grp in range(4):
                    src = pl.ds(pl.multiple_of(r0 + grp * POOL_CH + me * 64, 8), 64)
                    dst = slice(grp * 64, (grp + 1) * 64)
                    _adamw_rows(total[src, :], i, dst, w_refs, m_refs, v_refs, outs, n)
                continue
            elif k == "o_conv_w":
                g = total[pl.ds(pl.multiple_of(r0 + me * 32, 8), 32), :][0:CONV_K]
            elif width < LANES:
                g = total[r0:r0 + rows, 0:width]
            else:
                lanes = [total[r0 + j:r0 + j + 1, :] for j in range(width // LANES)]
                per_chip = local_width // LANES
                if local_width == width:
                    g = jnp.concatenate(lanes, axis=1)
                elif per_chip == 1:
                    g = of_chip(lanes)
                else:
                    g = of_chip([jnp.concatenate(lanes[j * per_chip:(j + 1) * per_chip], axis=1) for j in range(N_CHIPS)])
            _adamw_rows(g, i, slice(None), w_refs, m_refs, v_refs, outs, n)

    shard_shapes = [SMALL_PACKING[k][2] for k in names]
    out = pl.pallas_call(update, name="small_update", in_specs=[whole] * (3 * n + 1), out_specs=[whole] * (4 * n),
                         out_shape=[SDS(s, f32) for s in shard_shapes] * 4)(total, *ws, *ms, *vs)
    return out[:n], out[n:2 * n], out[2 * n:3 * n], out[3 * n:]


def _adamw_rows(g, i, rows, w_refs, m_refs, v_refs, outs, n):
    w, m, v = w_refs[i][rows, :], m_refs[i][rows, :], v_refs[i][rows, :]
    nm = ADAM_B1 * m + (1.0 - ADAM_B1) * g
    nv = ADAM_B2 * v + (1.0 - ADAM_B2) * (g * g)
    m_hat = nm / (1.0 - ADAM_B1 ** ADAM_STEP)
    v_hat = nv / (1.0 - ADAM_B2 ** ADAM_STEP)
    outs[i][rows, :] = g
    outs[n + i][rows, :] = -ADAM_LR * (m_hat / (jnp.sqrt(v_hat) + ADAM_EPS) + ADAM_WD * w)
    outs[2 * n + i][rows, :] = nm
    outs[3 * n + i][rows, :] = nv


def _pack(arrays, total_rows=None):
    parts = []
    rows = 0
    for a in arrays:
        flat = a.reshape(-1, LANES)
        pad = -flat.shape[0] % 8
        parts.append(jnp.pad(flat, ((0, pad), (0, 0))))
        rows += flat.shape[0] + pad
    if total_rows is not None:
        parts.append(jnp.zeros((total_rows - rows, LANES), arrays[0].dtype))
    return jnp.concatenate(parts, axis=0)


def _unpack(buf, shapes):
    out = []
    row = 0
    lead = buf.shape[:-2]
    for shape in shapes:
        size = 1
        for s in shape:
            size *= s
        rows = size // LANES
        out.append(buf[..., row:row + rows, :].reshape(lead + tuple(shape)))
        row += rows + (-rows % 8)
    return out


BIG = ("e_w_in", "e_w_out", "o_w_in", "o_w_out")
SHARDED_SMALL = {
    "e_pool_w": ((4, 64, 256), 1), "o_pre_norm": ((512,), 0), "o_sgu_norm_g": ((256,), 0), "o_sgu_norm_b": ((256,), 0),
    "o_conv_w": ((31, 256), 1), "o_conv_b": ((256,), 0), "o_conv_norm_g": ((256,), 0), "o_conv_norm_b": ((256,), 0),
    "o_post_norm": ((512,), 0),
}
SMALL_ORDER = ("e_pre_norm", "e_pool_w", "e_pool_scale", "e_post_norm", "o_pre_norm", "o_sgu_norm_g", "o_sgu_norm_b",
               "o_sgu_w", "o_sgu_b", "o_conv_w", "o_conv_b", "o_conv_norm_g", "o_conv_norm_b", "o_post_norm")
ALL_ORDER = ("e_pre_norm", "e_w_in", "e_pool_w", "e_pool_scale", "e_w_out", "e_post_norm", "o_pre_norm", "o_w_in",
             "o_sgu_norm_g", "o_sgu_norm_b", "o_sgu_w", "o_sgu_b", "o_conv_w", "o_conv_b", "o_conv_norm_g",
             "o_conv_norm_b", "o_w_out", "o_post_norm")


def _full_shape(name):
    shape, axis = SHARDED_SMALL[name]
    return tuple(s * N_CHIPS if i == axis else s for i, s in enumerate(shape))


def _from_chips(name, stacked):
    shape, axis = SHARDED_SMALL[name]
    return jnp.moveaxis(stacked, 0, axis).reshape(_full_shape(name))


def kernel(x, e_pre_norm, e_w_in, e_pool_w, e_pool_scale, e_w_out, e_post_norm, o_pre_norm, o_w_in, o_sgu_norm_g, o_sgu_norm_b, o_sgu_w, o_sgu_b, o_conv_w, o_conv_b, o_conv_norm_g, o_conv_norm_b, o_w_out, o_post_norm, loss_target, m_e_pre_norm, m_e_w_in, m_e_pool_w, m_e_pool_scale, m_e_w_out, m_e_post_norm, m_o_pre_norm, m_o_w_in, m_o_sgu_norm_g, m_o_sgu_norm_b, m_o_sgu_w, m_o_sgu_b, m_o_conv_w, m_o_conv_b, m_o_conv_norm_g, m_o_conv_norm_b, m_o_w_out, m_o_post_norm, v_e_pre_norm, v_e_w_in, v_e_pool_w, v_e_pool_scale, v_e_w_out, v_e_post_norm, v_o_pre_norm, v_o_w_in, v_o_sgu_norm_g, v_o_sgu_norm_b, v_o_sgu_w, v_o_sgu_b, v_o_conv_w, v_o_conv_b, v_o_conv_norm_g, v_o_conv_norm_b, v_o_w_out, v_o_post_norm):
    w = dict(e_pre_norm=e_pre_norm, e_w_in=e_w_in, e_pool_w=e_pool_w, e_pool_scale=e_pool_scale, e_w_out=e_w_out,
             e_post_norm=e_post_norm, o_pre_norm=o_pre_norm, o_w_in=o_w_in, o_sgu_norm_g=o_sgu_norm_g,
             o_sgu_norm_b=o_sgu_norm_b, o_sgu_w=o_sgu_w, o_sgu_b=o_sgu_b, o_conv_w=o_conv_w, o_conv_b=o_conv_b,
             o_conv_norm_g=o_conv_norm_g, o_conv_norm_b=o_conv_norm_b, o_w_out=o_w_out, o_post_norm=o_post_norm)
    m = dict(e_pre_norm=m_e_pre_norm, e_w_in=m_e_w_in, e_pool_w=m_e_pool_w, e_pool_scale=m_e_pool_scale,
             e_w_out=m_e_w_out, e_post_norm=m_e_post_norm, o_pre_norm=m_o_pre_norm, o_w_in=m_o_w_in,
             o_sgu_norm_g=m_o_sgu_norm_g, o_sgu_norm_b=m_o_sgu_norm_b, o_sgu_w=m_o_sgu_w, o_sgu_b=m_o_sgu_b,
             o_conv_w=m_o_conv_w, o_conv_b=m_o_conv_b, o_conv_norm_g=m_o_conv_norm_g, o_conv_norm_b=m_o_conv_norm_b,
             o_w_out=m_o_w_out, o_post_norm=m_o_post_norm)
    v = dict(e_pre_norm=v_e_pre_norm, e_w_in=v_e_w_in, e_pool_w=v_e_pool_w, e_pool_scale=v_e_pool_scale,
             e_w_out=v_e_w_out, e_post_norm=v_e_post_norm, o_pre_norm=v_o_pre_norm, o_w_in=v_o_w_in,
             o_sgu_norm_g=v_o_sgu_norm_g, o_sgu_norm_b=v_o_sgu_norm_b, o_sgu_w=v_o_sgu_w, o_sgu_b=v_o_sgu_b,
             o_conv_w=v_o_conv_w, o_conv_b=v_o_conv_b, o_conv_norm_g=v_o_conv_norm_g, o_conv_norm_b=v_o_conv_norm_b,
             o_w_out=v_o_w_out, o_post_norm=v_o_post_norm)
    w, m, v = ({k: a[0] for k, a in d.items()} for d in (w, m, v))
    chip = 2 * lax.axis_index("x") + lax.axis_index("y")

    loss, grad_x, in_flight, small = _step(x[0], loss_target[0], w, chip)

    grads, delta, new_m, new_v = {}, {}, {}, {}
    after = grad_x
    for k in ("o_w_out", "o_w_in", "e_w_out", "e_w_in"):
        grads[k], delta[k], new_m[k], new_v[k] = _adamw(w[k], _land(in_flight, k, chip, after), m[k], v[k], f"adamw_{k}")
        after = delta[k]

    def rows_of(a):
        return a.reshape(-1, a.shape[-1])

    small_grads = [small[k].reshape(SMALL_PACKING[k][0]) for k in SMALL_ORDER]
    updates = _small_finalize(small_grads, *[[rows_of(d[k]) for k in SMALL_ORDER] for d in (w, m, v)], after)
    for d, arrays in zip((grads, delta, new_m, new_v), updates):
        for k, a in zip(SMALL_ORDER, arrays):
            d[k] = a.reshape(w[k].shape)
    loss = lax.psum(loss[0, 0], ("x", "y", "c"))

    outs = [loss, grad_x[None]]
    for d in (grads, delta, new_m, new_v):
        outs += [d[k][None] for k in ALL_ORDER]
    return tuple(outs)
```

```python
import jax
import jax.numpy as jnp
from jax import lax
from jax.experimental import pallas as pl
from jax.experimental.pallas import tpu as pltpu

f32 = jnp.float32
bf16 = jnp.bfloat16
SDS = jax.ShapeDtypeStruct

SEQ = 2048
D_MODEL = 2048
EPS = 1e-6
NEG = -1e30
HEAD_DIM = 128
ROT_HALF = 16
ROPE_THETA = 500000.0
DILATIONS = (1, 4, 16)
SPAN = 128
N_HEADS = 8
HALF = 1024
POOL_CH = 256
CONV_K = 31
CONV_PAD = 32
CHUNK = 128
N_CHIPS = 4
LANES = 256
E_IN_PIECES = 3
SMALL_SHARD_ROWS = 352
ANY = pl.BlockSpec(memory_space=pl.ANY)
MESH = pl.DeviceIdType.MESH

ADAM_LR = 0.001
ADAM_B1 = 0.9
ADAM_B2 = 0.999
ADAM_EPS = 1e-08
ADAM_WD = 0.01
ADAM_STEP = 10


def _dot(a, b):
    return jnp.dot(a, b, preferred_element_type=f32)


def _dot_nt(a, b):
    return lax.dot_general(a, b, (((1,), (1,)), ((), ())), preferred_element_type=f32)


def _dot_tn(a, b):
    return lax.dot_general(a, b, (((0,), (0,)), ((), ())), preferred_element_type=f32)


def _sigmoid(x):
    return 1.0 / (1.0 + jnp.exp(-x))


def _silu_and_grad(x):
    s = _sigmoid(x)
    return x * s, s * (1.0 + x * (1.0 - s))


def _rms_fwd(x, g):
    r = lax.rsqrt(jnp.mean(x * x, axis=-1, keepdims=True) + EPS)
    return x * r * g


def _rms_bwd(x, g, dout):
    r = lax.rsqrt(jnp.mean(x * x, axis=-1, keepdims=True) + EPS)
    xh = x * r
    dg = jnp.sum(dout * xh, axis=0, keepdims=True)
    dxh = dout * g
    dx = r * (dxh - xh * jnp.mean(dxh * xh, axis=-1, keepdims=True))
    return dx, dg


def _ln_stats(x):
    mu = jnp.mean(x, axis=-1, keepdims=True)
    xc = x - mu
    rstd = lax.rsqrt(jnp.mean(xc * xc, axis=-1, keepdims=True) + EPS)
    return xc * rstd, rstd


def _ln_bwd(xh, rstd, g, dout):
    dg = jnp.sum(dout * xh, axis=0, keepdims=True)
    db = jnp.sum(dout, axis=0, keepdims=True)
    dxh = dout * g
    dx = rstd * (dxh - jnp.mean(dxh, axis=-1, keepdims=True) - xh * jnp.mean(dxh * xh, axis=-1, keepdims=True))
    return dx, dg, db


def _accumulate(ref, value, first):
    @pl.when(first)
    def _():
        ref[...] = value

    @pl.when(jnp.logical_not(first))
    def _():
        ref[...] += value


def _col_tile(ns):
    for t in (1024, 768, 512, 256):
        if ns % t == 0:
            return t
    raise ValueError(ns)


def _mm_nn(a, w, out_dtype, name, piece=0, pieces=1, into=None):
    m, k = a.shape
    j, _, ns = w.shape
    tm, tn = m, _col_tile(ns)
    nb = ns // tn

    def body(a_ref, w_ref, *rest):
        rest[-1][...] = _dot(a_ref[...], w_ref[...]).astype(rest[-1].dtype)

    return pl.pallas_call(
        body, name=name, grid=(j * nb, m // tm),
        in_specs=[pl.BlockSpec((tm, k), lambda n, i: (i, 0)),
                  pl.BlockSpec((None, k, tn), lambda n, i: (n // nb, 0, n % nb))] + ([] if into is None else [ANY]),
        out_specs=pl.BlockSpec((tm, tn), lambda n, i: (i, ((n // nb) * pieces + piece) * nb + n % nb)),
        out_shape=SDS((m, j * ns * pieces), out_dtype),
        input_output_aliases={} if into is None else {2: 0},
    )(a, w, *([] if into is None else [into]))


def _mm_nt(dz, ws, name, after):
    m, _ = dz.shape
    pieces = len(ws)
    j, k, ns = ws[0].shape
    tm, tk = 1024, 1024

    def body(dz_ref, *rest):
        w_refs, o_ref = rest[:pieces], rest[-1]
        total = _dot_nt(dz_ref[:, 0:ns], w_refs[0][...])
        for q in range(1, pieces):
            total = total + _dot_nt(dz_ref[:, q * ns:(q + 1) * ns], w_refs[q][...])
        _accumulate(o_ref, total, pl.program_id(2) == 0)

    return pl.pallas_call(
        body, name=name, grid=(m // tm, k // tk, j),
        in_specs=[pl.BlockSpec((tm, pieces * ns), lambda i, kk, r: (i, r))]
        + [pl.BlockSpec((None, tk, ns), lambda i, kk, r: (r, kk, 0))] * pieces + [ANY],
        out_specs=pl.BlockSpec((tm, tk), lambda i, kk, r: (i, kk)),
        out_shape=SDS((m, k), f32),
    )(dz, *ws, after)


def _mm_tn(a, dz, j, name):
    m, k = a.shape
    ns = dz.shape[1] // j
    tk, tn = 1024, _col_tile(ns)
    nb = ns // tn

    def body(a_ref, dz_ref, o_ref):
        o_ref[...] = _dot_tn(a_ref[...], dz_ref[...]).astype(o_ref.dtype)

    return pl.pallas_call(
        body, name=name, grid=(k // tk, j * nb),
        in_specs=[pl.BlockSpec((m, tk), lambda kk, n: (0, kk)),
                  pl.BlockSpec((m, tn), lambda kk, n: (0, n))],
        out_specs=pl.BlockSpec((None, tk, tn), lambda kk, n: (n // nb, kk, n % nb)),
        out_shape=SDS((j, k, ns), bf16),
    )(a, dz)


ROWS = 256


def _row_spec(width=D_MODEL, col=0):
    return pl.BlockSpec((ROWS, width), lambda i: (i, col))


def _vec_spec(width=D_MODEL):
    return pl.BlockSpec((1, width), lambda i: (0, 0))


def _pre_norm(x, g):
    def body(x_ref, g_ref, h_ref):
        h_ref[...] = _rms_fwd(x_ref[...], g_ref[...]).astype(bf16)

    return pl.pallas_call(
        body, name="pre_norm", grid=(SEQ // ROWS,), in_specs=[_row_spec(), _vec_spec()],
        out_specs=_row_spec(), out_shape=SDS((SEQ, D_MODEL), bf16))(x, g)


def _mid_norm(x, y, g_post, g_pre):
    def body(x_ref, y_ref, gpost_ref, gpre_ref, x1_ref, h1_ref):
        x1 = x_ref[...] + _rms_fwd(y_ref[...], gpost_ref[...])
        x1_ref[...] = x1
        h1_ref[...] = _rms_fwd(x1, gpre_ref[...]).astype(bf16)

    return pl.pallas_call(
        body, name="mid_norm", grid=(SEQ // ROWS,),
        in_specs=[_row_spec(), _row_spec(), _vec_spec(), _vec_spec()],
        out_specs=[_row_spec(), _row_spec()],
        out_shape=[SDS((SEQ, D_MODEL), f32), SDS((SEQ, D_MODEL), bf16)])(x, y, g_post, g_pre)


def _final_norm_loss(x1, y, g_post, target):
    def body(x1_ref, y_ref, g_ref, t_ref, loss_ref, dx2_ref, dy_ref, dg_ref):
        first = pl.program_id(0) == 0
        y = y_ref[...]
        g = g_ref[...]
        err = x1_ref[...] + _rms_fwd(y, g) - t_ref[...]
        sq = jnp.sum(jnp.sum(err * err, axis=1, keepdims=True), axis=0, keepdims=True)
        _accumulate(loss_ref, sq * (0.5 / D_MODEL), first)
        dx2 = err * (1.0 / D_MODEL)
        dx2_ref[...] = dx2
        dy, dg = _rms_bwd(y, g, dx2)
        dy_ref[...] = dy.astype(bf16)
        _accumulate(dg_ref, dg, first)

    return pl.pallas_call(
        body, name="final_norm_loss", grid=(SEQ // ROWS,),
        in_specs=[_row_spec(), _row_spec(), _vec_spec(), _row_spec()],
        out_specs=[pl.BlockSpec((1, 1), lambda i: (0, 0)), _row_spec(), _row_spec(), _vec_spec()],
        out_shape=[SDS((1, 1), f32), SDS((SEQ, D_MODEL), f32), SDS((SEQ, D_MODEL), bf16), SDS((1, D_MODEL), f32)],
    )(x1, y, g_post, target)


def _mid_norm_bwd(dx2, dh1, x1, y0, g_pre, g_post):
    def body(dx2_ref, dh1_ref, x1_ref, y0_ref, gpre_ref, gpost_ref, dx1_ref, dy0_ref, dgpre_ref, dgpost_ref):
        first = pl.program_id(0) == 0
        d_in, dgpre = _rms_bwd(x1_ref[...], gpre_ref[...], dh1_ref[...])
        dx1 = dx2_ref[...] + d_in
        dx1_ref[...] = dx1
        dy0, dgpost = _rms_bwd(y0_ref[...], gpost_ref[...], dx1)
        dy0_ref[...] = dy0.astype(bf16)
        _accumulate(dgpre_ref, dgpre, first)
        _accumulate(dgpost_ref, dgpost, first)

    return pl.pallas_call(
        body, name="mid_norm_bwd", grid=(SEQ // ROWS,),
        in_specs=[_row_spec(), _row_spec(), _row_spec(), _row_spec(), _vec_spec(), _vec_spec()],
        out_specs=[_row_spec(), _row_spec(), _vec_spec(), _vec_spec()],
        out_shape=[SDS((SEQ, D_MODEL), f32), SDS((SEQ, D_MODEL), bf16), SDS((1, D_MODEL), f32), SDS((1, D_MODEL), f32)],
    )(dx2, dh1, x1, y0, g_pre, g_post)


def _pre_norm_bwd(dx1, dh0, x, g):
    def body(dx1_ref, dh0_ref, x_ref, g_ref, dx_ref, dg_ref):
        d_in, dg = _rms_bwd(x_ref[...], g_ref[...], dh0_ref[...])
        dx_ref[...] = dx1_ref[...] + d_in
        _accumulate(dg_ref, dg, pl.program_id(0) == 0)

    return pl.pallas_call(
        body, name="pre_norm_bwd", grid=(SEQ // ROWS,),
        in_specs=[_row_spec(), _row_spec(), _row_spec(), _vec_spec()],
        out_specs=[_row_spec(), _vec_spec()],
        out_shape=[SDS((SEQ, D_MODEL), f32), SDS((1, D_MODEL), f32)])(dx1, dh0, x, g)


def _pool_count(g):
    row = lax.broadcasted_iota(jnp.int32, (SEQ, 1), 0)
    width = jnp.left_shift(2, g)
    return row, width, jnp.minimum(row + 1, width).astype(f32)


def _trailing_sum(x, row, width):
    s = x
    for k in (1, 2, 4, 8):
        shifted = jnp.where(row >= k, pltpu.roll(s, k, 0), 0.0)
        s = jnp.where(width > k, s + shifted, s)
    return s


def _leading_sum(x, row, width):
    s = x
    for k in (1, 2, 4, 8):
        shifted = jnp.where(row < SEQ - k, pltpu.roll(s, SEQ - k, 0), 0.0)
        s = jnp.where(width > k, s + shifted, s)
    return s


def _pool_specs():
    a_in = pl.BlockSpec((SEQ, POOL_CH), lambda g: (0, g))
    a_gate = pl.BlockSpec((SEQ, POOL_CH), lambda g: (0, 4 + g))
    w = pl.BlockSpec((None, POOL_CH, POOL_CH), lambda g: (g, 0, 0))
    scale = pl.BlockSpec((1, POOL_CH), lambda g: (0, g))
    return a_in, a_gate, w, scale


def _pool_fwd(z0, pool_w, pool_scale):
    def body(a_ref, gate_ref, w_ref, scale_ref, ya_ref):
        row, width, count = _pool_count(pl.program_id(0))
        a = a_ref[...]
        pooled = _trailing_sum(a, row, width) / count - a
        mixed = _dot(pooled.astype(bf16), w_ref[...]) * scale_ref[...]
        gate = gate_ref[...]
        ya_ref[...] = (mixed * gate * _sigmoid(gate)).astype(bf16)

    return pl.pallas_call(
        body, name="pool_fwd", grid=(4,), in_specs=list(_pool_specs()),
        out_specs=pl.BlockSpec((SEQ, POOL_CH), lambda g: (0, g)),
        out_shape=SDS((SEQ, HALF), bf16))(z0, z0, pool_w, pool_scale)


def _pool_bwd(z0, dcat, pool_w, pool_scale):
    def body(a_ref, gate_ref, w_ref, scale_ref, dya_ref, da_ref, dgate_ref, dw_ref, dscale_ref):
        row, width, count = _pool_count(pl.program_id(0))
        a = a_ref[...]
        pooled = (_trailing_sum(a, row, width) / count - a).astype(bf16)
        w = w_ref[...]
        scale = scale_ref[...]
        mixed = _dot(pooled, w)
        silu, dsilu = _silu_and_grad(gate_ref[...])
        dya = dya_ref[...]
        dgate_ref[...] = (dya * mixed * scale * dsilu).astype(bf16)
        dms = dya * silu
        dscale_ref[...] = jnp.sum(dms * mixed, axis=0, keepdims=True)
        dmixed = (dms * scale).astype(bf16)
        dw_ref[...] = _dot_tn(pooled, dmixed)
        dpooled = _dot_nt(dmixed, w)
        da_ref[...] = (_leading_sum(dpooled / count, row, width) - dpooled).astype(bf16)

    a_in, a_gate, w, scale = _pool_specs()
    col = pl.BlockSpec((SEQ, POOL_CH), lambda g: (0, g))
    return pl.pallas_call(
        body, name="pool_bwd", grid=(4,), in_specs=[a_in, a_gate, w, scale, col],
        out_specs=[col, col, w, scale],
        out_shape=[SDS((SEQ, HALF), bf16), SDS((SEQ, HALF), bf16), SDS((4, POOL_CH, POOL_CH), f32), SDS((1, HALF), f32)],
    )(z0, z0, pool_w, pool_scale, dcat)


Q_COL, K_COL, V_COL, BGATE_COL = 16, 40, 64, 88


def _rope_tables():
    pos = jnp.arange(SEQ, dtype=f32)
    inv_freq = jnp.power(ROPE_THETA, -jnp.arange(0, 2 * ROT_HALF, 2, dtype=f32) / (2 * ROT_HALF))
    ang = pos[:, None] * inv_freq[None, :]
    cos, sin = jnp.cos(ang), jnp.sin(ang)
    zeros = jnp.zeros((SEQ, HEAD_DIM - 2 * ROT_HALF), f32)
    cos_t = jnp.concatenate([cos, cos, zeros + 1.0], axis=1)
    sin_t = jnp.concatenate([sin, sin, zeros], axis=1)
    j = jnp.arange(HEAD_DIM)[:, None]
    i = jnp.arange(HEAD_DIM)[None, :]
    rot = jnp.where((i < ROT_HALF) & (j == i + ROT_HALF), -1.0, 0.0) + jnp.where(
        (i >= ROT_HALF) & (i < 2 * ROT_HALF) & (j == i - ROT_HALF), 1.0, 0.0)
    return cos_t, sin_t, rot.astype(bf16), rot.T.astype(bf16)


def _exact_dot(t, m):
    hi = t.astype(bf16)
    lo = (t - hi.astype(f32)).astype(bf16)
    return _dot(hi, m) + _dot(lo, m)


def _rope(t, cos_t, sin_t, rot):
    return t * cos_t + _exact_dot(t, rot) * sin_t


def _rope_transposed(d, cos_t, sin_t, rot_t):
    return d * cos_t + _exact_dot(d * sin_t, rot_t)


ROW_CHUNK = 256


def _chunks(fn):
    def step(i, carry):
        fn(pl.multiple_of(i * ROW_CHUNK, ROW_CHUNK))
        return carry

    lax.fori_loop(0, SEQ // ROW_CHUNK, step, 0, unroll=2)


def _pieces(dilation):
    length = SEQ // dilation
    n = min(length, ROW_CHUNK)
    return [(r, l0, n) for r in range(dilation) for l0 in range(0, length, n)]


def _by_residue(dst_ref, src_ref, dilation, dtype):
    length = SEQ // dilation
    for r, l0, n in _pieces(dilation):
        src = src_ref[l0:l0 + n, :] if dilation == 1 else src_ref[pl.ds(r + dilation * l0, n, stride=dilation), :]
        start = r * length + l0
        dst_ref[start:start + n, :] = src.astype(dtype)


def _by_position(dst_ref, src_ref, dilation):
    length = SEQ // dilation
    for r, l0, n in _pieces(dilation):
        src = src_ref[r * length + l0:r * length + l0 + n, :]
        if dilation == 1:
            dst_ref[l0:l0 + n, :] = src
        else:
            dst_ref[pl.ds(r + dilation * l0, n, stride=dilation), :] = src


def _attn_masks():
    qi = lax.broadcasted_iota(jnp.int32, (SPAN, 2 * SPAN), 0)
    kj = lax.broadcasted_iota(jnp.int32, (SPAN, 2 * SPAN), 1)
    window = ((kj < SPAN) & (kj >= qi)) | ((kj >= SPAN) & (kj - SPAN <= qi))
    own = lax.broadcasted_iota(jnp.int32, (SPAN, SPAN), 1) <= lax.broadcasted_iota(jnp.int32, (SPAN, SPAN), 0)
    return window, own


def _attn_blocks(dilation):
    per_residue = SEQ // dilation // SPAN
    blocks = [(c, c % per_residue != 0) for c in range(SEQ // SPAN)]
    return [blocks[i:i + 4] for i in range(0, len(blocks), 4)]


def _block_keys(c, has_prev):
    return slice((c - 1) * SPAN if has_prev else c * SPAN, (c + 1) * SPAN)


def _head_spec(col):
    return pl.BlockSpec((SEQ, HEAD_DIM), lambda h: (0, col + h))


def _table_spec():
    return pl.BlockSpec((SEQ, HEAD_DIM), lambda h: (0, 0))


def _attn_fwd(z0, tables):
    scale = HEAD_DIM ** -0.5

    def body(*refs):
        qkv = refs[0:9]
        bg_ref, cos_ref, sin_ref, rot_ref = refs[9:13]
        yb_ref, att_ref, lse_ref = refs[13:16]
        saved = refs[16:25]
        tmp_q, tmp_k, v_ones, o_res, l_res, o_nat, l_nat = refs[25:32]
        window_mask, own_mask = _attn_masks()
        rot = rot_ref[...]

        @pl.when(pl.program_id(0) == 0)
        def _():
            v_ones[:, HEAD_DIM:] = jnp.ones((SEQ, HEAD_DIM), bf16)

        for g, dilation in enumerate(DILATIONS):
            q_ref, k_ref, v_ref = qkv[3 * g:3 * g + 3]
            qd, kd, vd = saved[3 * g:3 * g + 3]

            def rope_rows(start, q_ref=q_ref, k_ref=k_ref):
                r = pl.ds(start, ROW_CHUNK)
                cos_t, sin_t = cos_ref[r, :], sin_ref[r, :]
                tmp_q[r, :] = _rope(q_ref[r, :], cos_t, sin_t, rot) * scale
                tmp_k[r, :] = _rope(k_ref[r, :], cos_t, sin_t, rot)

            _chunks(rope_rows)
            _by_residue(qd, tmp_q, dilation, bf16)
            _by_residue(kd, tmp_k, dilation, bf16)
            _by_residue(vd, v_ref, dilation, bf16)
            for l0 in range(0, SEQ, ROW_CHUNK):
                v_ones[l0:l0 + ROW_CHUNK, 0:HEAD_DIM] = vd[l0:l0 + ROW_CHUNK, :]

            for four in _attn_blocks(dilation):
                scores = [_dot_nt(qd[c * SPAN:(c + 1) * SPAN, :], kd[_block_keys(c, prev), :]) for c, prev in four]
                tops, probs = [], []
                for (c, prev), s in zip(four, scores):
                    s = jnp.where(window_mask if prev else own_mask, s, NEG)
                    tops.append(jnp.max(s, axis=1, keepdims=True))
                    probs.append(jnp.exp(s - tops[-1]).astype(bf16))
                sums = [_dot(p, v_ones[_block_keys(c, prev), :]) for (c, prev), p in zip(four, probs)]
                for (c, prev), m, o in zip(four, tops, sums):
                    den = o[:, HEAD_DIM:]
                    o_res[c * SPAN:(c + 1) * SPAN, :] = o[:, :HEAD_DIM] / den
                    l_res[c * SPAN:(c + 1) * SPAN, :] = m + jnp.log(den)

            if dilation > 1:
                _by_position(o_nat, o_res, dilation)
                _by_position(l_nat, l_res, dilation)
            o_g, l_g = (o_res, l_res) if dilation == 1 else (o_nat, l_nat)

            def merge(start, g=g, o_g=o_g, l_g=l_g):
                r = pl.ds(start, ROW_CHUNK)
                if g == 0:
                    att, total = o_g[r, :], l_g[r, :]
                else:
                    l_old, l_new = lse_ref[r, :], l_g[r, :]
                    top = jnp.maximum(l_old, l_new)
                    total = top + jnp.log(jnp.exp(l_old - top) + jnp.exp(l_new - top))
                    att = att_ref[r, :] * jnp.exp(l_old - total) + o_g[r, :] * jnp.exp(l_new - total)
                att_ref[r, :] = att
                lse_ref[r, :] = total
                if g == len(DILATIONS) - 1:
                    gate = bg_ref[r, :]
                    yb_ref[r, :] = (att * gate * _sigmoid(gate)).astype(bf16)

            _chunks(merge)

    in_specs = []
    for g in range(3):
        in_specs += [_head_spec(Q_COL + 8 * g), _head_spec(K_COL + 8 * g), _head_spec(V_COL + 8 * g)]
    in_specs += [_head_spec(BGATE_COL), _table_spec(), _table_spec(), pl.BlockSpec((HEAD_DIM, HEAD_DIM), lambda h: (0, 0))]
    out_spec = pl.BlockSpec((SEQ, HEAD_DIM), lambda h: (0, h))
    vm = lambda dt: pltpu.VMEM((SEQ, HEAD_DIM), dt)
    cos_t, sin_t, rot, _ = tables
    out = pl.pallas_call(
        body, name="attn_fwd", grid=(N_HEADS,), in_specs=in_specs, out_specs=[out_spec] * 12,
        out_shape=[SDS((SEQ, HALF), bf16), SDS((SEQ, HALF), f32), SDS((SEQ, HALF), f32)] + [SDS((SEQ, HALF), bf16)] * 9,
        scratch_shapes=[vm(f32), vm(f32), pltpu.VMEM((SEQ, 2 * HEAD_DIM), bf16), vm(f32), vm(f32), vm(f32), vm(f32)],
    )(*([z0] * 10), cos_t, sin_t, rot)
    return out[0], out[1], out[2], [tuple(out[3 + 3 * g:6 + 3 * g]) for g in range(3)]


def _attn_bwd_group(g, saved, z0, att, lse, dcat, tables):
    scale = HEAD_DIM ** -0.5
    dilation = DILATIONS[g]
    with_gate = g == 0

    def body(*refs):
        qd, kd, vd, bg_ref, att_ref, lse_ref, dyb_ref, cos_ref, sin_ref, rot_t_ref = refs[0:10]
        n_out = 4 if with_gate else 3
        dq_ref, dk_ref, dv_ref = refs[10:13]
        dod, ld, dd, tmp, aq, ak, av = refs[10 + n_out:17 + n_out]
        window_mask, own_mask = _attn_masks()
        rot_t = rot_t_ref[...]

        def gate_rows(start):
            r = pl.ds(start, ROW_CHUNK)
            silu, dsilu = _silu_and_grad(bg_ref[r, :])
            att_v = att_ref[r, :]
            dyb = dyb_ref[r, :]
            if with_gate:
                refs[13][r, :] = (dyb * att_v * dsilu).astype(bf16)
            datt = dyb * silu
            tmp[r, :] = datt
            aq[r, :] = jnp.broadcast_to(jnp.sum(datt * att_v, axis=1, keepdims=True), (ROW_CHUNK, HEAD_DIM))

        _chunks(gate_rows)
        _by_residue(dod, tmp, dilation, bf16)
        _by_residue(dd, aq, dilation, f32)
        _by_residue(ld, lse_ref, dilation, f32)

        for four in _attn_blocks(dilation):
            rows = [slice(c * SPAN, (c + 1) * SPAN) for c, _ in four]
            keys = [_block_keys(c, prev) for c, prev in four]
            scores = [_dot_nt(qd[r, :], kd[k, :]) for r, k in zip(rows, keys)]
            dprobs = [_dot_nt(dod[r, :], vd[k, :]) for r, k in zip(rows, keys)]
            probs, dscores = [], []
            for (c, prev), r, s, dp in zip(four, rows, scores, dprobs):
                lse_q, delta = ld[r, :], dd[r, :]
                if prev:
                    lse_q = jnp.concatenate([lse_q, lse_q], axis=1)
                    delta = jnp.concatenate([delta, delta], axis=1)
                p = jnp.where(window_mask if prev else own_mask, jnp.exp(s - lse_q), 0.0)
                probs.append(p.astype(bf16))
                dscores.append((p * (dp - delta)).astype(bf16))
            dvs = [_dot_tn(p, dod[r, :]) for p, r in zip(probs, rows)]
            dks = [_dot_tn(ds, qd[r, :]) for ds, r in zip(dscores, rows)]
            dqs = [_dot(ds, kd[k, :]) for ds, k in zip(dscores, keys)]
            for (c, prev), r, dv, dk, dq in zip(four, rows, dvs, dks, dqs):
                aq[r, :] = dq
                if prev:
                    before = slice((c - 1) * SPAN, c * SPAN)
                    av[before, :] += dv[0:SPAN]
                    ak[before, :] += dk[0:SPAN]
                    av[r, :] = dv[SPAN:]
                    ak[r, :] = dk[SPAN:]
                else:
                    av[r, :] = dv
                    ak[r, :] = dk

        def finish(out_ref, acc, factor, roped):
            if dilation > 1:
                _by_position(tmp, acc, dilation)
            src = acc if dilation == 1 else tmp

            def rows(start):
                r = pl.ds(start, ROW_CHUNK)
                d = src[r, :]
                if factor != 1.0:
                    d = d * factor
                if roped:
                    d = _rope_transposed(d, cos_ref[r, :], sin_ref[r, :], rot_t)
                out_ref[r, :] = d.astype(bf16)

            _chunks(rows)

        finish(dq_ref, aq, scale, True)
        finish(dk_ref, ak, 1.0, True)
        finish(dv_ref, av, 1.0, False)

    head = pl.BlockSpec((SEQ, HEAD_DIM), lambda h: (0, h))
    in_specs = [head, head, head, _head_spec(BGATE_COL), head, head, _head_spec(8), _table_spec(), _table_spec(),
                pl.BlockSpec((HEAD_DIM, HEAD_DIM), lambda h: (0, 0))]
    n_out = 4 if with_gate else 3
    vm = lambda dt: pltpu.VMEM((SEQ, HEAD_DIM), dt)
    cos_t, sin_t, _, rot_t = tables
    return pl.pallas_call(
        body, name=f"attn_bwd_g{g}", grid=(N_HEADS,), in_specs=in_specs, out_specs=[head] * n_out,
        out_shape=[SDS((SEQ, HALF), bf16)] * n_out,
        scratch_shapes=[vm(bf16), vm(f32), vm(f32), vm(f32), vm(f32), vm(f32), vm(f32)],
    )(*saved, z0, att, lse, dcat, cos_t, sin_t, rot_t)


def _sgu_specs():
    chunk = lambda col: pl.BlockSpec((CHUNK, HALF), lambda n: (n, col))
    vec = pl.BlockSpec((1, HALF), lambda n: (0, 0))
    w = pl.BlockSpec((4, CHUNK, CHUNK), lambda n: (0, 0, 0))
    bias = pl.BlockSpec((CHUNK, CHUNK), lambda n: (0, 0))
    return chunk, vec, w, bias


def _sgu_weights(w_ref):
    tril = lax.broadcasted_iota(jnp.int32, (CHUNK, CHUNK), 1) <= lax.broadcasted_iota(jnp.int32, (CHUNK, CHUNK), 0)
    return tril, [jnp.where(tril, w_ref[h], 0.0).astype(bf16) for h in range(4)]


def _sgu_fwd(z1, ln_g, ln_b, sgu_w, bias_t):
    def body(u_ref, v_ref, cg_ref, g_ref, b_ref, w_ref, bias_ref, yc_ref):
        _, ws = _sgu_weights(w_ref)
        xh, _ = _ln_stats(v_ref[...])
        vn = (xh * g_ref[...] + b_ref[...]).astype(bf16)
        for h in range(4):
            cols = slice(h * POOL_CH, (h + 1) * POOL_CH)
            s = _dot(ws[h], vn[:, cols]) + bias_ref[:, h:h + 1]
            gate = cg_ref[:, cols]
            yc_ref[:, cols] = (u_ref[:, cols] * s * gate * _sigmoid(gate)).astype(bf16)

    chunk, vec, w, bias = _sgu_specs()
    return pl.pallas_call(
        body, name="sgu_fwd", grid=(SEQ // CHUNK,),
        in_specs=[chunk(0), chunk(1), chunk(2), vec, vec, w, bias], out_specs=chunk(0),
        out_shape=SDS((SEQ, HALF), bf16))(z1, z1, z1, ln_g, ln_b, sgu_w, bias_t)


def _sgu_bwd(z1, dcat, ln_g, ln_b, sgu_w, bias_t):
    def body(u_ref, v_ref, cg_ref, dyc_ref, g_ref, b_ref, w_ref, bias_ref,
             du_ref, dv_ref, dcg_ref, dw_ref, dbias_ref, dg_ref, db_ref, dvn_ref):
        first = pl.program_id(0) == 0
        tril, ws = _sgu_weights(w_ref)
        xh, rstd = _ln_stats(v_ref[...])
        g = g_ref[...]
        vn = (xh * g + b_ref[...]).astype(bf16)

        @pl.when(first)
        def _():
            dbias_ref[...] = jnp.zeros((CHUNK, CHUNK), f32)

        for h in range(4):
            cols = slice(h * POOL_CH, (h + 1) * POOL_CH)
            vn_h = vn[:, cols]
            s = _dot(ws[h], vn_h) + bias_ref[:, h:h + 1]
            silu, dsilu = _silu_and_grad(cg_ref[:, cols])
            dyc = dyc_ref[:, cols]
            u = u_ref[:, cols]
            du_ref[:, cols] = (dyc * s * silu).astype(bf16)
            dcg_ref[:, cols] = (dyc * u * s * dsilu).astype(bf16)
            ds = dyc * u * silu
            dbias_ref[:, h:h + 1] += jnp.sum(ds, axis=1, keepdims=True)
            ds = ds.astype(bf16)
            _accumulate(dw_ref.at[h], jnp.where(tril, _dot_nt(ds, vn_h), 0.0), first)
            dvn_ref[:, cols] = _dot_tn(ws[h], ds)
        dv, dg, db = _ln_bwd(xh, rstd, g, dvn_ref[...])
        dv_ref[...] = dv.astype(bf16)
        _accumulate(dg_ref, dg, first)
        _accumulate(db_ref, db, first)

    chunk, vec, w, bias = _sgu_specs()
    return pl.pallas_call(
        body, name="sgu_bwd", grid=(SEQ // CHUNK,),
        in_specs=[chunk(0), chunk(1), chunk(2), chunk(0), vec, vec, w, bias],
        out_specs=[chunk(0), chunk(0), chunk(0), w, bias, vec, vec],
        out_shape=[SDS((SEQ, HALF), bf16)] * 3 + [SDS((4, CHUNK, CHUNK), f32), SDS((CHUNK, CHUNK), f32),
                                                   SDS((1, HALF), f32), SDS((1, HALF), f32)],
        scratch_shapes=[pltpu.VMEM((CHUNK, HALF), f32)],
    )(z1, z1, z1, dcat, ln_g, ln_b, sgu_w, bias_t)


CONV_TILE = 128
DVAL_COL, DGLU_COL = 12, 16


def _conv_specs():
    val = pl.BlockSpec((SEQ, POOL_CH), lambda j: (0, DVAL_COL + j))
    glu = pl.BlockSpec((SEQ, POOL_CH), lambda j: (0, DGLU_COL + j))
    w = pl.BlockSpec((CONV_K, POOL_CH), lambda j: (0, j))
    col = pl.BlockSpec((SEQ, POOL_CH), lambda j: (0, j))
    vec = pl.BlockSpec((1, POOL_CH), lambda j: (0, j))
    return val, glu, w, col, vec


def _conv_fwd(z1, conv_w, conv_b):
    def body(val_ref, glu_ref, w_ref, b_ref, out_ref, xpad):
        xpad[0:CONV_PAD, :] = jnp.zeros((CONV_PAD, POOL_CH), f32)
        xpad[CONV_PAD:, :] = val_ref[...] * _sigmoid(glu_ref[...])
        w = w_ref[...]
        bias = b_ref[...]

        def tile(i, carry):
            t0 = pl.multiple_of(i * CONV_TILE, CONV_TILE)
            window = xpad[pl.ds(t0, CONV_TILE + CONV_PAD), :]
            acc = jnp.broadcast_to(bias, (CONV_TILE, POOL_CH))
            for k in range(CONV_K):
                shift = CONV_PAD - (CONV_K - 1) + k
                acc = acc + w[k:k + 1, :] * pltpu.roll(window, CONV_TILE + CONV_PAD - shift, 0)[0:CONV_TILE]
            out_ref[pl.ds(t0, CONV_TILE), :] = acc
            return carry

        lax.fori_loop(0, SEQ // CONV_TILE, tile, 0)

    val, glu, w, col, vec = _conv_specs()
    return pl.pallas_call(
        body, name="conv_fwd", grid=(4,), in_specs=[val, glu, w, vec], out_specs=col,
        out_shape=SDS((SEQ, HALF), f32), scratch_shapes=[pltpu.VMEM((SEQ + CONV_PAD, POOL_CH), f32)],
    )(z1, z1, conv_w, conv_b)


def _conv_bwd(z1, dconv, conv_w):
    def body(val_ref, glu_ref, w_ref, dout_ref, dval_ref, dglu_ref, dw_ref, db_ref, xpad, dpad, dx_ref):
        val = val_ref[...]
        sig = _sigmoid(glu_ref[...])
        xpad[0:CONV_PAD, :] = jnp.zeros((CONV_PAD, POOL_CH), f32)
        xpad[CONV_PAD:, :] = val * sig
        dout = dout_ref[...]
        dpad[0:SEQ, :] = dout
        dpad[SEQ:, :] = jnp.zeros((CONV_PAD, POOL_CH), f32)
        db_ref[...] = jnp.sum(dout, axis=0, keepdims=True)
        dw_ref[...] = jnp.zeros((CONV_K, POOL_CH), f32)
        w = w_ref[...]

        def tile(i, carry):
            t0 = pl.multiple_of(i * CONV_TILE, CONV_TILE)
            x_win = xpad[pl.ds(t0, CONV_TILE + CONV_PAD), :]
            d_win = dpad[pl.ds(t0, CONV_TILE + CONV_PAD), :]
            d_own = d_win[0:CONV_TILE]
            acc = jnp.zeros((CONV_TILE, POOL_CH), f32)
            for k in range(CONV_K):
                shift = CONV_PAD - (CONV_K - 1) + k
                x_k = pltpu.roll(x_win, CONV_TILE + CONV_PAD - shift, 0)[0:CONV_TILE]
                dw_ref[k:k + 1, :] += jnp.sum(d_own * x_k, axis=0, keepdims=True)
                back = CONV_K - 1 - k
                d_k = d_own if back == 0 else pltpu.roll(d_win, CONV_TILE + CONV_PAD - back, 0)[0:CONV_TILE]
                acc = acc + w[k:k + 1, :] * d_k
            dx_ref[pl.ds(t0, CONV_TILE), :] = acc
            return carry

        lax.fori_loop(0, SEQ // CONV_TILE, tile, 0)
        dx = dx_ref[...]
        dval_ref[...] = (dx * sig).astype(bf16)
        dglu_ref[...] = (dx * val * sig * (1.0 - sig)).astype(bf16)

    val, glu, w, col, vec = _conv_specs()
    pad = pltpu.VMEM((SEQ + CONV_PAD, POOL_CH), f32)
    return pl.pallas_call(
        body, name="conv_bwd", grid=(4,), in_specs=[val, glu, w, col], out_specs=[col, col, w, vec],
        out_shape=[SDS((SEQ, HALF), bf16), SDS((SEQ, HALF), bf16), SDS((CONV_K, HALF), f32), SDS((1, HALF), f32)],
        scratch_shapes=[pad, pad, pltpu.VMEM((SEQ, POOL_CH), f32)],
    )(z1, z1, conv_w, dconv)


DGATE_COL = 5


def _conv_norm_fwd(conv, z1, g, b):
    def body(c_ref, gate_ref, g_ref, b_ref, yd_ref):
        xh, _ = _ln_stats(c_ref[...])
        n = xh * g_ref[...] + b_ref[...]
        gate = gate_ref[...]
        yd_ref[...] = (n * _sigmoid(n) * gate * _sigmoid(gate)).astype(bf16)

    return pl.pallas_call(
        body, name="conv_norm_fwd", grid=(SEQ // ROWS,),
        in_specs=[_row_spec(HALF), _row_spec(HALF, DGATE_COL), _vec_spec(HALF), _vec_spec(HALF)],
        out_specs=_row_spec(HALF), out_shape=SDS((SEQ, HALF), bf16))(conv, z1, g, b)


def _conv_norm_bwd(conv, z1, dcat, g, b):
    def body(c_ref, gate_ref, dyd_ref, g_ref, b_ref, dconv_ref, dgate_ref, dg_ref, db_ref):
        first = pl.program_id(0) == 0
        xh, rstd = _ln_stats(c_ref[...])
        g = g_ref[...]
        n_silu, n_dsilu = _silu_and_grad(xh * g + b_ref[...])
        gate_silu, gate_dsilu = _silu_and_grad(gate_ref[...])
        dyd = dyd_ref[...]
        dgate_ref[...] = (dyd * n_silu * gate_dsilu).astype(bf16)
        dconv, dg, db = _ln_bwd(xh, rstd, g, dyd * gate_silu * n_dsilu)
        dconv_ref[...] = dconv
        _accumulate(dg_ref, dg, first)
        _accumulate(db_ref, db, first)

    return pl.pallas_call(
        body, name="conv_norm_bwd", grid=(SEQ // ROWS,),
        in_specs=[_row_spec(HALF), _row_spec(HALF, DGATE_COL), _row_spec(HALF, 1), _vec_spec(HALF), _vec_spec(HALF)],
        out_specs=[_row_spec(HALF), _row_spec(HALF), _vec_spec(HALF), _vec_spec(HALF)],
        out_shape=[SDS((SEQ, HALF), f32), SDS((SEQ, HALF), bf16), SDS((1, HALF), f32), SDS((1, HALF), f32)],
    )(conv, z1, dcat, g, b)


def _step(x, target, w, chip):
    chip_vec = chip.astype(jnp.int32).reshape(1)
    sharded_names = list(SHARDED_SMALL)
    first = [_cast_into_slot(w["e_w_in"], chip_vec, "cast_e_w_in0", w["e_pre_norm"], 0, E_IN_PIECES)]
    sems, bufs, token = _gather_start(first, "gather_start_first")
    small_shard = _pack([w[k] for k in sharded_names], total_rows=SMALL_SHARD_ROWS) + 0.0 * token[0, 0]
    small_slot = lax.dynamic_update_slice(jnp.zeros((N_CHIPS, SMALL_SHARD_ROWS, LANES), f32), small_shard[None], (chip, 0, 0))
    more = [small_slot]
    more += [_cast_into_slot(w["e_w_in"], chip_vec, f"cast_e_w_in{i}", token, i, E_IN_PIECES) for i in range(1, E_IN_PIECES)]
    more_sems, more_bufs, token = _gather_start(more, "gather_start_pieces")
    rest = [_cast_into_slot(w[k], chip_vec, f"cast_{k}", token) for k in BIG[1:]]
    rest_sems, rest_bufs, token = _gather_start(rest, "gather_start_rest")
    sems, bufs = sems + more_sems + rest_sems, bufs + more_bufs + rest_bufs
    tables = _rope_tables()

    def vec(k):
        return w[k].reshape(1, -1)

    h0 = _pre_norm(x, vec("e_pre_norm") + token[0, 0])
    after, z0, e_w_in = h0, None, []
    for i in range(E_IN_PIECES):
        group = slice(0, 1) if i == 0 else slice(1, 3) if i == 1 else slice(i + 1, i + 2)
        landed = _forward_halves(_gather_wait(bufs[group], sems[group], after, f"gather_wait_{i}"), f"forward_{i}")
        if i == 1:
            small_full = landed[0]
        e_w_in.append(landed[-1])
        z0 = _mm_nn(h0, landed[-1], f32, f"e_in{i}", i, E_IN_PIECES, z0)
        after = z0
    p = {k: _from_chips(k, a) for k, a in zip(sharded_names, _unpack(small_full, [SHARDED_SMALL[k][0] for k in sharded_names]))}
    for k in ("o_pre_norm", "o_sgu_norm_g", "o_sgu_norm_b", "o_conv_b", "o_conv_norm_g", "o_conv_norm_b", "o_post_norm"):
        p[k] = p[k].reshape(1, -1)
    pool_w_bf = p["e_pool_w"].astype(bf16)
    bias_t = jnp.pad(w["o_sgu_b"].T, ((0, 0), (0, CHUNK - 4)))

    ya = _pool_fwd(z0, pool_w_bf, vec("e_pool_scale"))
    yb, att, lse, qkv_by_residue = _attn_fwd(z0, tables)

    def arrived(index, after, name):
        one = slice(index, index + 1)
        return _forward_halves(_gather_wait(bufs[one], sems[one], after, f"gather_wait_{name}"), f"forward_{name}")[0]

    e_w_out = arrived(1 + E_IN_PIECES, att, "e_w_out").reshape(1, D_MODEL, D_MODEL)
    cat0 = jnp.concatenate([ya, yb], axis=1)
    y0 = _mm_nn(cat0, e_w_out, f32, "e_out")
    x1, h1 = _mid_norm(x, y0, vec("e_post_norm"), p["o_pre_norm"])
    o_w_in = arrived(2 + E_IN_PIECES, h1, "o_w_in")
    z1 = _mm_nn(h1, o_w_in, f32, "o_in")
    yc = _sgu_fwd(z1, p["o_sgu_norm_g"], p["o_sgu_norm_b"], w["o_sgu_w"], bias_t)
    conv = _conv_fwd(z1, p["o_conv_w"], p["o_conv_b"])
    yd = _conv_norm_fwd(conv, z1, p["o_conv_norm_g"], p["o_conv_norm_b"])
    o_w_out = arrived(3 + E_IN_PIECES, yd, "o_w_out").reshape(1, D_MODEL, D_MODEL)
    cat1 = jnp.concatenate([yc, yd], axis=1)
    y1 = _mm_nn(cat1, o_w_out, f32, "o_out")
    loss, dx2, dy1, g_o_post = _final_norm_loss(x1, y1, p["o_post_norm"], target)

    in_flight = {}

    def send_off(name, grad):
        sem, sums, land, tok = _scatter_start(_swap_add(grad, f"swap_add_{name}"), f"scatter_start_{name}")
        in_flight[name] = (sem, sums, land)
        return tok

    tok = send_off("o_w_out", _mm_tn(cat1, dy1, 1, "o_out_dw").reshape(N_CHIPS, HALF // 2, D_MODEL))
    dcat1 = _mm_nt(dy1, [o_w_out], "o_out_dx", tok)
    du, dv, dcg, g_sgu_w, g_bias_t, g_sgu_g, g_sgu_b = _sgu_bwd(
        z1, dcat1, p["o_sgu_norm_g"] + tok[0, 0], p["o_sgu_norm_b"], w["o_sgu_w"], bias_t)
    dconv, ddgate, g_cn_g, g_cn_b = _conv_norm_bwd(conv, z1, dcat1, p["o_conv_norm_g"], p["o_conv_norm_b"])
    ddval, ddglu, g_conv_w, g_conv_b = _conv_bwd(z1, dconv, p["o_conv_w"])
    dz1 = jnp.concatenate([du, dv, dcg, ddval, ddglu, ddgate], axis=1)
    tok = send_off("o_w_in", _mm_tn(h1, dz1, N_CHIPS, "o_in_dw"))
    dh1 = _mm_nt(dz1, [o_w_in], "o_in_dx", tok)
    dx1, dy0, g_o_pre, g_e_post = _mid_norm_bwd(dx2, dh1, x1, y0, p["o_pre_norm"] + tok[0, 0], vec("e_post_norm"))

    tok = send_off("e_w_out", _mm_tn(cat0, dy0, 1, "e_out_dw").reshape(N_CHIPS, HALF // 2, D_MODEL))
    dcat0 = _mm_nt(dy0, [e_w_out], "e_out_dx", tok)
    da, dagate, g_pool_w, g_pool_scale = _pool_bwd(z0, dcat0, pool_w_bf, vec("e_pool_scale") + tok[0, 0])
    dq0, dk0, dv0, dbgate = _attn_bwd_group(0, qkv_by_residue[0], z0, att, lse, dcat0, tables)
    dq1, dk1, dv1 = _attn_bwd_group(1, qkv_by_residue[1], z0, att, lse, dcat0, tables)
    dq2, dk2, dv2 = _attn_bwd_group(2, qkv_by_residue[2], z0, att, lse, dcat0, tables)
    dz0 = jnp.concatenate([da, dagate, dq0, dq1, dq2, dk0, dk1, dk2, dv0, dv1, dv2, dbgate], axis=1)
    tok = send_off("e_w_in", _mm_tn(h0, dz0, N_CHIPS, "e_in_dw"))
    dh0 = _mm_nt(dz0, e_w_in, "e_in_dx", tok)
    grad_x, g_e_pre = _pre_norm_bwd(dx1, dh0, x, vec("e_pre_norm") + tok[0, 0])

    small = {"e_pre_norm": g_e_pre, "e_pool_w": g_pool_w, "e_pool_scale": g_pool_scale, "e_post_norm": g_e_post,
             "o_pre_norm": g_o_pre, "o_sgu_norm_g": g_sgu_g, "o_sgu_norm_b": g_sgu_b, "o_sgu_w": g_sgu_w,
             "o_sgu_b": g_bias_t, "o_conv_w": g_conv_w, "o_conv_b": g_conv_b,
             "o_conv_norm_g": g_cn_g, "o_conv_norm_b": g_cn_b, "o_post_norm": g_o_post}
    return loss, grad_x, in_flight, small


def _land(in_flight, name, chip, after):
    sems, sums, land = in_flight[name]
    sums, land = _scatter_wait(sems, sums, land, after, f"scatter_wait_{name}")
    return _add_landed_join(sums, land, chip.astype(jnp.int32).reshape(1), f"add_landed_{name}")


def _place():
    x, y, c = lax.axis_index("x"), lax.axis_index("y"), lax.axis_index("c")
    others = [(1 - x, y), (x, 1 - y), (1 - x, 1 - y)]
    return x, y, c, 2 * x + y, others


SWAP_ROWS = 256
FORWARD_STAGE_BYTES = 4 << 20


def _swap_add(g, name):
    chips, r, c = g.shape
    half = r // 2
    rows_per_step = 2 * SWAP_ROWS if half % (2 * SWAP_ROWS) == 0 else SWAP_ROWS
    nb = half // rows_per_step
    steps = chips * nb

    def body(core_ref, mine_ref, theirs_ref, out_ref, landing, send_sems, recv_sems, free_sems):
        i = pl.program_id(0)
        x, y, core, _, _ = _place()
        sibling = (x, y, 1 - core)

        def send(slot):
            return pltpu.make_async_remote_copy(src_ref=theirs_ref, dst_ref=landing.at[slot], send_sem=send_sems.at[slot],
                                                recv_sem=recv_sems.at[slot], device_id=sibling, device_id_type=MESH)

        @pl.when(i < steps)
        def _():
            @pl.when(i >= 2)
            def _():
                pl.semaphore_wait(free_sems.at[i % 2], 1)

            send(i % 2).start()

        @pl.when(i >= 1)
        def _():
            landed = (i - 1) % 2
            send(landed).wait_recv()
            out_ref[...] = (mine_ref[...].astype(f32) + landing[landed].astype(f32)).astype(out_ref.dtype)

            @pl.when(i + 1 < steps)
            def _():
                pl.semaphore_signal(free_sems.at[landed], 1, device_id=sibling, device_id_type=MESH)

        @pl.when(i < steps)
        def _():
            send(i % 2).wait_send()

    def rows_of(b, h):
        return (2 * (b // nb) + h) * nb + b % nb

    block = (rows_per_step, c)
    grid_spec = pltpu.PrefetchScalarGridSpec(
        num_scalar_prefetch=1, grid=(steps + 1,),
        in_specs=[pl.BlockSpec(block, lambda i, core: (rows_of(jnp.maximum(i - 1, 0), core[0]), 0)),
                  pl.BlockSpec(block, lambda i, core: (rows_of(jnp.minimum(i, steps - 1), 1 - core[0]), 0))],
        out_specs=pl.BlockSpec(block, lambda i, core: (jnp.maximum(i - 1, 0), 0)),
        scratch_shapes=[pltpu.VMEM((2, rows_per_step, c), g.dtype), pltpu.SemaphoreType.DMA((2,)),
                        pltpu.SemaphoreType.DMA((2,)), pltpu.SemaphoreType.REGULAR((2,))])
    core = lax.axis_index("c").astype(jnp.int32).reshape(1)
    rows = g.reshape(chips * r, c)
    out = pl.pallas_call(body, name=name, grid_spec=grid_spec, out_shape=SDS((chips * half, c), g.dtype))(core, rows, rows)
    return out.reshape(chips, half, c)


HBM = pl.BlockSpec(memory_space=pltpu.HBM)
SEM = pl.BlockSpec(memory_space=pltpu.SEMAPHORE)
EFFECT = pltpu.SideEffectType.DATAFLOW_SIDE_EFFECTING


def _in_hbm(a):
    return pltpu.with_memory_space_constraint(a, pltpu.HBM)


def _cast_into_slot(w, chip, name, after, piece=0, pieces=1):
    r, c = w.shape
    c = c // pieces
    nb = r // SWAP_ROWS

    def body(chip_ref, w_ref, after_ref, o_ref):
        o_ref[...] = w_ref[...].astype(bf16)

    grid_spec = pltpu.PrefetchScalarGridSpec(
        num_scalar_prefetch=1, grid=(nb,),
        in_specs=[pl.BlockSpec((SWAP_ROWS, c), lambda i, chip: (i, piece)), ANY],
        out_specs=pl.BlockSpec((SWAP_ROWS, c), lambda i, chip: (chip[0] * nb + i, 0)))
    out = pl.pallas_call(body, name=name, grid_spec=grid_spec, out_shape=SDS((N_CHIPS * r, c), bf16))(chip, w, after)
    return out.reshape(N_CHIPS, r, c)


def _gather_start(bufs, name):
    n = len(bufs)

    def body(*refs):
        ins, sems, token = refs[:n], refs[n:3 * n], refs[4 * n]
        x, y, c, me, others = _place()
        for a in range(n):
            rows = ins[a].shape[1] // 2
            mine = ins[a].at[me, pl.ds(c * rows, rows), :]
            for k, (ox, oy) in enumerate(others):
                pltpu.make_async_remote_copy(src_ref=mine, dst_ref=mine, send_sem=sems[2 * a].at[k],
                                             recv_sem=sems[2 * a + 1].at[k], device_id=(ox, oy, c),
                                             device_id_type=MESH).start()
        token[...] = jnp.zeros_like(token)

    out = pl.pallas_call(
        body, name=name, in_specs=[HBM] * n,
        out_shape=(*[pltpu.SemaphoreType.DMA((3,))] * (2 * n), *[pltpu.HBM(b.shape, b.dtype) for b in bufs],
                   SDS((8, 128), f32)),
        out_specs=(*[SEM] * (2 * n), *[HBM] * n, pl.BlockSpec(memory_space=pltpu.VMEM)),
        input_output_aliases={a: 2 * n + a for a in range(n)},
        compiler_params=pltpu.CompilerParams(has_side_effects=EFFECT),
    )(*[_in_hbm(b) for b in bufs])
    return [(out[2 * a], out[2 * a + 1]) for a in range(n)], list(out[2 * n:3 * n]), out[3 * n]


def _gather_wait(bufs, sems, after, name):
    n = len(bufs)

    def body(*refs):
        ins, sem_refs = refs[:n], refs[n:3 * n]
        x, y, c, me, others = _place()
        for a in range(n):
            rows = ins[a].shape[1] // 2
            mine = ins[a].at[me, pl.ds(c * rows, rows), :]
            for k, (ox, oy) in enumerate(others):
                landed = ins[a].at[2 * ox + oy, pl.ds(c * rows, rows), :]
                copy = pltpu.make_async_remote_copy(src_ref=mine, dst_ref=landed, send_sem=sem_refs[2 * a].at[k],
                                                    recv_sem=sem_refs[2 * a + 1].at[k], device_id=(ox, oy, c),
                                                    device_id_type=MESH)
                copy.wait_send()
                copy.wait_recv()

    flat_sems = [s for pair in sems for s in pair]
    out = pl.pallas_call(
        body, name=name, in_specs=[HBM] * n + [SEM] * (2 * n) + [ANY],
        out_shape=tuple(pltpu.HBM(b.shape, b.dtype) for b in bufs), out_specs=tuple([HBM] * n),
        input_output_aliases={a: a for a in range(n)},
        compiler_params=pltpu.CompilerParams(has_side_effects=EFFECT),
    )(*bufs, *flat_sems, after)
    return list(out)


def _forward_halves(bufs, name):
    n = len(bufs)
    blocks = []
    for b in bufs:
        half = b.shape[1] // 2
        whole = half * b.shape[2] * b.dtype.itemsize <= FORWARD_STAGE_BYTES
        blocks.append((half, half if whole or half % SWAP_ROWS else SWAP_ROWS))
    work = [(a, k, b) for a in range(n) for k in range(3) for b in range(blocks[a][0] // blocks[a][1])]

    def body(*refs):
        outs, stages = refs[n:2 * n], refs[2 * n:3 * n]
        load_sems, send_sems, recv_sems = refs[3 * n:]
        x, y, c, me, others = _place()
        sibling = (x, y, 1 - c)

        def rows(item):
            a, k, b = item
            half, tr = blocks[a]
            ox, oy = others[k]
            return outs[a].at[2 * ox + oy, pl.ds(c * half + b * tr, tr), :]

        def load(s, item):
            return pltpu.make_async_copy(rows(item), stages[item[0]].at[s], load_sems.at[s])

        def send(s, item):
            return pltpu.make_async_remote_copy(src_ref=stages[item[0]].at[s], dst_ref=rows(item), send_sem=send_sems.at[s],
                                                recv_sem=recv_sems.at[item[0]], device_id=sibling, device_id_type=MESH)

        load(0, work[0]).start()
        for t, item in enumerate(work):
            s = t % 2
            load(s, item).wait()
            send(s, item).start()
            if t + 1 < len(work):
                if t >= 1:
                    send(1 - s, work[t - 1]).wait_send()
                load(1 - s, work[t + 1]).start()
        if len(work) > 1:
            send(len(work) % 2, work[-2]).wait_send()
        send((len(work) - 1) % 2, work[-1]).wait_send()
        for a in range(n):
            theirs = outs[a].at[pl.ds(0, 3), pl.ds(0, blocks[a][0]), :]
            pltpu.make_async_remote_copy(src_ref=theirs, dst_ref=theirs, send_sem=send_sems.at[0], recv_sem=recv_sems.at[a],
                                         device_id=sibling, device_id_type=MESH).wait_recv()

    out = pl.pallas_call(
        body, name=name, in_specs=[ANY] * n, out_specs=[ANY] * n, out_shape=[SDS(b.shape, b.dtype) for b in bufs],
        input_output_aliases={a: a for a in range(n)},
        scratch_shapes=[pltpu.VMEM((2, blocks[a][1], bufs[a].shape[2]), bufs[a].dtype) for a in range(n)]
        + [pltpu.SemaphoreType.DMA((2,)), pltpu.SemaphoreType.DMA((2,)), pltpu.SemaphoreType.DMA((n,))],
    )(*bufs)
    return list(out)


def _scatter_start(chip_sums, name):
    def body(a_ref, land_ref, send_sems, recv_sems, a_thru, land_thru, token):
        x, y, c, me, others = _place()
        for k, (ox, oy) in enumerate(others):
            pltpu.make_async_remote_copy(src_ref=a_ref.at[2 * ox + oy], dst_ref=land_ref.at[me], send_sem=send_sems.at[k],
                                         recv_sem=recv_sems.at[k], device_id=(ox, oy, c), device_id_type=MESH).start()
        token[...] = jnp.zeros_like(token)

    shape = pltpu.HBM(chip_sums.shape, chip_sums.dtype)
    send, recv, a_thru, land, token = pl.pallas_call(
        body, name=name, in_specs=[HBM, HBM],
        out_shape=(pltpu.SemaphoreType.DMA((3,)), pltpu.SemaphoreType.DMA((3,)), shape, shape, SDS((8, 128), f32)),
        out_specs=(SEM, SEM, HBM, HBM, pl.BlockSpec(memory_space=pltpu.VMEM)), input_output_aliases={0: 2, 1: 3},
        compiler_params=pltpu.CompilerParams(has_side_effects=EFFECT),
    )(_in_hbm(chip_sums), _in_hbm(lax.empty(chip_sums.shape, chip_sums.dtype)))
    return (send, recv), a_thru, land, token


def _scatter_wait(sems, chip_sums, land, after, name):
    def body(a_ref, land_ref, send_sems, recv_sems, after_ref, a_out, land_out):
        x, y, c, me, others = _place()
        for k, (ox, oy) in enumerate(others):
            copy = pltpu.make_async_remote_copy(
                src_ref=a_ref.at[2 * ox + oy], dst_ref=land_ref.at[2 * ox + oy], send_sem=send_sems.at[k],
                recv_sem=recv_sems.at[k], device_id=(ox, oy, c), device_id_type=MESH)
            copy.wait_send()
            copy.wait_recv()

    shape = pltpu.HBM(chip_sums.shape, chip_sums.dtype)
    return pl.pallas_call(
        body, name=name, in_specs=[HBM, HBM, SEM, SEM, ANY], out_shape=(shape, shape), out_specs=(HBM, HBM),
        input_output_aliases={0: 0, 1: 1}, compiler_params=pltpu.CompilerParams(has_side_effects=EFFECT),
    )(chip_sums, land, sems[0], sems[1], after)


def _add_landed_join(chip_sums, land, chip, name):
    chips, rh, c = chip_sums.shape
    nb = rh // SWAP_ROWS

    def body(chip_ref, own_ref, l1_ref, l2_ref, l3_ref, out_hbm, buf, send_sems, recv_sem, local_sems):
        i = pl.program_id(0)
        slot = i % 2
        x, y, core, _, _ = _place()
        sibling = (x, y, 1 - core)

        def copies(s, step):
            rows = pl.ds(pl.multiple_of((core * nb + step) * SWAP_ROWS, SWAP_ROWS), SWAP_ROWS)
            keep = pltpu.make_async_copy(buf.at[s], out_hbm.at[rows, :], local_sems.at[s])
            give = pltpu.make_async_remote_copy(src_ref=buf.at[s], dst_ref=out_hbm.at[rows, :], send_sem=send_sems.at[s],
                                                recv_sem=recv_sem.at[0], device_id=sibling, device_id_type=MESH)
            return keep, give

        def drain(s, step):
            keep, give = copies(s, step)
            keep.wait()
            give.wait_send()

        @pl.when(i >= 2)
        def _():
            drain(slot, i - 2)

        buf[slot] = ((own_ref[...].astype(f32) + l1_ref[...].astype(f32)) + l2_ref[...].astype(f32)) + l3_ref[...].astype(f32)
        keep, give = copies(slot, i)
        keep.start()
        give.start()

        @pl.when(i == nb - 1)
        def _():
            drain(slot, i)
            if nb > 1:
                drain(1 - slot, i - 1)
            theirs = out_hbm.at[pl.ds((1 - core) * rh, rh), :]
            pltpu.make_async_remote_copy(src_ref=theirs, dst_ref=theirs, send_sem=send_sems.at[0], recv_sem=recv_sem.at[0],
                                         device_id=sibling, device_id_type=MESH).wait_recv()

    block = (SWAP_ROWS, c)
    from_slot = lambda d: pl.BlockSpec(block, lambda i, chip: (((chip[0] + d) % chips) * nb + i, 0))
    grid_spec = pltpu.PrefetchScalarGridSpec(
        num_scalar_prefetch=1, grid=(nb,), in_specs=[from_slot(0), from_slot(1), from_slot(2), from_slot(3)],
        out_specs=ANY,
        scratch_shapes=[pltpu.VMEM((2, SWAP_ROWS, c), f32), pltpu.SemaphoreType.DMA((2,)),
                        pltpu.SemaphoreType.DMA((1,)), pltpu.SemaphoreType.DMA((2,))])
    land_rows = land.reshape(chips * rh, c)
    return pl.pallas_call(body, name=name, grid_spec=grid_spec, out_shape=SDS((2 * rh, c), f32))(
        chip, chip_sums.reshape(chips * rh, c), land_rows, land_rows, land_rows)


def _adamw_update(w_ref, g_ref, m_ref, v_ref, d_ref, nm_ref, nv_ref):
    g = g_ref[...]
    nm = ADAM_B1 * m_ref[...] + (1.0 - ADAM_B1) * g
    nv = ADAM_B2 * v_ref[...] + (1.0 - ADAM_B2) * (g * g)
    nm_ref[...] = nm
    nv_ref[...] = nv
    m_hat = nm / (1.0 - ADAM_B1 ** ADAM_STEP)
    v_hat = nv / (1.0 - ADAM_B2 ** ADAM_STEP)
    d_ref[...] = -ADAM_LR * (m_hat / (jnp.sqrt(v_hat) + ADAM_EPS) + ADAM_WD * w_ref[...])


def _adamw(w, g, m, v, name):
    r, c = w.shape
    tr = 128 if r % 128 == 0 else r

    def body(w_ref, g_ref, m_ref, v_ref, g_out_ref, d_ref, nm_ref, nv_ref):
        g_out_ref[...] = g_ref[...]
        _adamw_update(w_ref, g_ref, m_ref, v_ref, d_ref, nm_ref, nv_ref)

    spec = pl.BlockSpec((tr, c), lambda i: (i, 0))
    return pl.pallas_call(body, name=name, grid=(r // tr,), in_specs=[spec] * 4, out_specs=[spec] * 4,
                          out_shape=[SDS((r, c), f32)] * 4)(w, g, m, v)


SMALL_PACKING = {
    "e_pre_norm": ((1, 2048), 8, (1, 2048)), "e_pool_w": ((1024, 256), 1024, (256, 256)),
    "e_pool_scale": ((1, 1024), 8, (1, 1024)), "e_post_norm": ((1, 2048), 8, (1, 2048)),
    "o_pre_norm": ((1, 2048), 8, (1, 512)), "o_sgu_norm_g": ((1, 1024), 8, (1, 256)),
    "o_sgu_norm_b": ((1, 1024), 8, (1, 256)), "o_sgu_w": ((512, 128), 512, (512, 128)),
    "o_sgu_b": ((128, 128), 8, (4, 128)), "o_conv_w": ((31, 1024), 128, (31, 256)), "o_conv_b": ((1, 1024), 8, (1, 256)),
    "o_conv_norm_g": ((1, 1024), 8, (1, 256)), "o_conv_norm_b": ((1, 1024), 8, (1, 256)),
    "o_post_norm": ((1, 2048), 8, (1, 512)),
}
SMALL_PACKED_ROWS = 1792


def _small_finalize(grads, ws, ms, vs, after):
    names = list(SMALL_ORDER)
    n = len(names)
    half, piece = SMALL_PACKED_ROWS // 2, SMALL_PACKED_ROWS // 8
    first_row, row = {}, 0
    for k in names:
        first_row[k] = row
        row += SMALL_PACKING[k][1]

    def body(*refs):
        g_refs, total = refs[0:n], refs[n + 1]
        pack, from_sibling, from_chips, send_a, recv_a, send_b, recv_b, send_c, recv_c, send_d, recv_d = refs[n + 2:]
        x, y, c, me, others = _place()

        for r0 in range(0, SMALL_PACKED_ROWS, piece):
            pack[r0:r0 + piece, :] = jnp.zeros((piece, LANES), f32)
        for k, g_ref in zip(names, g_refs):
            (rows, width), _, _ = SMALL_PACKING[k]
            r0 = first_row[k]
            if k == "o_sgu_b":
                pack[r0:r0 + 4, 0:CHUNK] = g_ref[...].T[0:4, :]
            elif width < LANES:
                pack[r0:r0 + rows, 0:width] = g_ref[...]
            else:
                for j in range(width // LANES):
                    dst = r0 + j * (1 if rows == 1 else 32)
                    pack[dst:dst + rows, :] = g_ref[:, j * LANES:(j + 1) * LANES]

        sibling = (x, y, 1 - c)
        swap = pltpu.make_async_remote_copy(
            src_ref=pack.at[pl.ds(pl.multiple_of((1 - c) * half, 8), half), :], dst_ref=from_sibling,
            send_sem=send_a.at[0], recv_sem=recv_a.at[0], device_id=sibling, device_id_type=MESH)
        swap.start()
        swap.wait()
        for j in range(4):
            rows = pl.ds(pl.multiple_of(c * half + j * piece, 8), piece)
            pack[rows, :] = pack[rows, :] + from_sibling[j * piece:(j + 1) * piece, :]

        def piece_of(chip):
            return pl.ds(pl.multiple_of(c * half + chip * piece, 8), piece)

        def to_chip(k):
            ox, oy = others[k]
            return pltpu.make_async_remote_copy(
                src_ref=pack.at[piece_of(2 * ox + oy), :], dst_ref=from_chips.at[me], send_sem=send_b.at[k],
                recv_sem=recv_b.at[k], device_id=(ox, oy, c), device_id_type=MESH)

        for k in range(3):
            to_chip(k).start()
        from_chips[me] = pack[piece_of(me), :]
        for k, (ox, oy) in enumerate(others):
            landed = from_chips.at[2 * ox + oy]
            pltpu.make_async_remote_copy(src_ref=landed, dst_ref=landed, send_sem=send_b.at[k], recv_sem=recv_b.at[k],
                                         device_id=(ox, oy, c), device_id_type=MESH).wait_recv()
        for k in range(3):
            to_chip(k).wait_send()
        mine = pl.ds(pl.multiple_of(c * half + me * piece, 8), piece)
        total[mine, :] = ((from_chips[0] + from_chips[1]) + from_chips[2]) + from_chips[3]

        def to_same_core(k):
            ox, oy = others[k]
            return pltpu.make_async_remote_copy(
                src_ref=total.at[mine, :], dst_ref=total.at[mine, :], send_sem=send_c.at[k], recv_sem=recv_c.at[k],
                device_id=(ox, oy, c), device_id_type=MESH)

        for k in range(3):
            to_same_core(k).start()
        for k, (ox, oy) in enumerate(others):
            theirs = total.at[piece_of(2 * ox + oy), :]
            pltpu.make_async_remote_copy(src_ref=theirs, dst_ref=theirs, send_sem=send_c.at[k], recv_sem=recv_c.at[k],
                                         device_id=(ox, oy, c), device_id_type=MESH).wait_recv()
        for k in range(3):
            to_same_core(k).wait_send()
        my_half = total.at[pl.ds(pl.multiple_of(c * half, 8), half), :]
        join = pltpu.make_async_remote_copy(src_ref=my_half, dst_ref=my_half, send_sem=send_d.at[0], recv_sem=recv_d.at[0],
                                            device_id=sibling, device_id_type=MESH)
        join.start()
        their_half = total.at[pl.ds(pl.multiple_of((1 - c) * half, 8), half), :]
        pltpu.make_async_remote_copy(src_ref=their_half, dst_ref=their_half, send_sem=send_d.at[0], recv_sem=recv_d.at[0],
                                     device_id=sibling, device_id_type=MESH).wait_recv()
        join.wait_send()

    whole = pl.BlockSpec(memory_space=pltpu.VMEM)
    total = pl.pallas_call(
        body, name="small_allreduce", in_specs=[whole] * n + [ANY], out_specs=whole,
        out_shape=SDS((SMALL_PACKED_ROWS, LANES), f32),
        scratch_shapes=[pltpu.VMEM((SMALL_PACKED_ROWS, LANES), f32), pltpu.VMEM((half, LANES), f32),
                        pltpu.VMEM((N_CHIPS, piece, LANES), f32),
                        pltpu.SemaphoreType.DMA((1,)), pltpu.SemaphoreType.DMA((1,)), pltpu.SemaphoreType.DMA((3,)),
                        pltpu.SemaphoreType.DMA((3,)), pltpu.SemaphoreType.DMA((3,)), pltpu.SemaphoreType.DMA((3,)),
                        pltpu.SemaphoreType.DMA((1,)), pltpu.SemaphoreType.DMA((1,))],
    )(*grads, after)

    def update(*refs):
        total = refs[0]
        w_refs, m_refs, v_refs = refs[1:n + 1], refs[n + 1:2 * n + 1], refs[2 * n + 1:3 * n + 1]
        outs = refs[3 * n + 1:]
        me = 2 * lax.axis_index("x") + lax.axis_index("y")

        def of_chip(candidates):
            value = candidates[0]
            for j in range(1, N_CHIPS):
                value = jnp.where(me == j, candidates[j], value)
            return value

        for i, k in enumerate(names):
            (rows, width), _, (local_rows, local_width) = SMALL_PACKING[k]
            r0 = first_row[k]
            if k == "o_sgu_b":
                g = total[r0:r0 + 4, 0:CHUNK]
            elif k == "e_pool_w":
                for grp in range(4):
                    src = pl.ds(pl.multiple_of(r0 + grp * POOL_CH + me * 64, 8), 64)
                    dst = slice(grp * 64, (grp + 1) * 64)
                    _adamw_rows(total[src, :], i, dst, w_refs, m_refs, v_refs, outs, n)
                continue
            elif k == "o_conv_w":
                g = total[pl.ds(pl.multiple_of(r0 + me * 32, 8), 32), :][0:CONV_K]
            elif width < LANES:
                g = total[r0:r0 + rows, 0:width]
            else:
                lanes = [total[r0 + j:r0 + j + 1, :] for j in range(width // LANES)]
                per_chip = local_width // LANES
                if local_width == width:
                    g = jnp.concatenate(lanes, axis=1)
                elif per_chip == 1:
                    g = of_chip(lanes)
                else:
                    g = of_chip([jnp.concatenate(lanes[j * per_chip:(j + 1) * per_chip], axis=1) for j in range(N_CHIPS)])
            _adamw_rows(g, i, slice(None), w_refs, m_refs, v_refs, outs, n)

    shard_shapes = [SMALL_PACKING[k][2] for k in names]
    out = pl.pallas_call(update, name="small_update", in_specs=[whole] * (3 * n + 1), out_specs=[whole] * (4 * n),
                         out_shape=[SDS(s, f32) for s in shard_shapes] * 4)(total, *ws, *ms, *vs)
    return out[:n], out[n:2 * n], out[2 * n:3 * n], out[3 * n:]


def _adamw_rows(g, i, rows, w_refs, m_refs, v_refs, outs, n):
    w, m, v = w_refs[i][rows, :], m_refs[i][rows, :], v_refs[i][rows, :]
    nm = ADAM_B1 * m + (1.0 - ADAM_B1) * g
    nv = ADAM_B2 * v + (1.0 - ADAM_B2) * (g * g)
    m_hat = nm / (1.0 - ADAM_B1 ** ADAM_STEP)
    v_hat = nv / (1.0 - ADAM_B2 ** ADAM_STEP)
    outs[i][rows, :] = g
    outs[n + i][rows, :] = -ADAM_LR * (m_hat / (jnp.sqrt(v_hat) + ADAM_EPS) + ADAM_WD * w)
    outs[2 * n + i][rows, :] = nm
    outs[3 * n + i][rows, :] = nv


def _pack(arrays, total_rows=None):
    parts = []
    rows = 0
    for a in arrays:
        flat = a.reshape(-1, LANES)
        pad = -flat.shape[0] % 8
        parts.append(jnp.pad(flat, ((0, pad), (0, 0))))
        rows += flat.shape[0] + pad
    if total_rows is not None:
        parts.append(jnp.zeros((total_rows - rows, LANES), arrays[0].dtype))
    return jnp.concatenate(parts, axis=0)


def _unpack(buf, shapes):
    out = []
    row = 0
    lead = buf.shape[:-2]
    for shape in shapes:
        size = 1
        for s in shape:
            size *= s
        rows = size // LANES
        out.append(buf[..., row:row + rows, :].reshape(lead + tuple(shape)))
        row += rows + (-rows % 8)
    return out


BIG = ("e_w_in", "e_w_out", "o_w_in", "o_w_out")
SHARDED_SMALL = {
    "e_pool_w": ((4, 64, 256), 1), "o_pre_norm": ((512,), 0), "o_sgu_norm_g": ((256,), 0), "o_sgu_norm_b": ((256,), 0),
    "o_conv_w": ((31, 256), 1), "o_conv_b": ((256,), 0), "o_conv_norm_g": ((256,), 0), "o_conv_norm_b": ((256,), 0),
    "o_post_norm": ((512,), 0),
}
SMALL_ORDER = ("e_pre_norm", "e_pool_w", "e_pool_scale", "e_post_norm", "o_pre_norm", "o_sgu_norm_g", "o_sgu_norm_b",
               "o_sgu_w", "o_sgu_b", "o_conv_w", "o_conv_b", "o_conv_norm_g", "o_conv_norm_b", "o_post_norm")
ALL_ORDER = ("e_pre_norm", "e_w_in", "e_pool_w", "e_pool_scale", "e_w_out", "e_post_norm", "o_pre_norm", "o_w_in",
             "o_sgu_norm_g", "o_sgu_norm_b", "o_sgu_w", "o_sgu_b", "o_conv_w", "o_conv_b", "o_conv_norm_g",
             "o_conv_norm_b", "o_w_out", "o_post_norm")


def _full_shape(name):
    shape, axis = SHARDED_SMALL[name]
    return tuple(s * N_CHIPS if i == axis else s for i, s in enumerate(shape))


def _from_chips(name, stacked):
    shape, axis = SHARDED_SMALL[name]
    return jnp.moveaxis(stacked, 0, axis).reshape(_full_shape(name))


def kernel(x, e_pre_norm, e_w_in, e_pool_w, e_pool_scale, e_w_out, e_post_norm, o_pre_norm, o_w_in, o_sgu_norm_g, o_sgu_norm_b, o_sgu_w, o_sgu_b, o_conv_w, o_conv_b, o_conv_norm_g, o_conv_norm_b, o_w_out, o_post_norm, loss_target, m_e_pre_norm, m_e_w_in, m_e_pool_w, m_e_pool_scale, m_e_w_out, m_e_post_norm, m_o_pre_norm, m_o_w_in, m_o_sgu_norm_g, m_o_sgu_norm_b, m_o_sgu_w, m_o_sgu_b, m_o_conv_w, m_o_conv_b, m_o_conv_norm_g, m_o_conv_norm_b, m_o_w_out, m_o_post_norm, v_e_pre_norm, v_e_w_in, v_e_pool_w, v_e_pool_scale, v_e_w_out, v_e_post_norm, v_o_pre_norm, v_o_w_in, v_o_sgu_norm_g, v_o_sgu_norm_b, v_o_sgu_w, v_o_sgu_b, v_o_conv_w, v_o_conv_b, v_o_conv_norm_g, v_o_conv_norm_b, v_o_w_out, v_o_post_norm):
    w = dict(e_pre_norm=e_pre_norm, e_w_in=e_w_in, e_pool_w=e_pool_w, e_pool_scale=e_pool_scale, e_w_out=e_w_out,
             e_post_norm=e_post_norm, o_pre_norm=o_pre_norm, o_w_in=o_w_in, o_sgu_norm_g=o_sgu_norm_g,
             o_sgu_norm_b=o_sgu_norm_b, o_sgu_w=o_sgu_w, o_sgu_b=o_sgu_b, o_conv_w=o_conv_w, o_conv_b=o_conv_b,
             o_conv_norm_g=o_conv_norm_g, o_conv_norm_b=o_conv_norm_b, o_w_out=o_w_out, o_post_norm=o_post_norm)
    m = dict(e_pre_norm=m_e_pre_norm, e_w_in=m_e_w_in, e_pool_w=m_e_pool_w, e_pool_scale=m_e_pool_scale,
             e_w_out=m_e_w_out, e_post_norm=m_e_post_norm, o_pre_norm=m_o_pre_norm, o_w_in=m_o_w_in,
             o_sgu_norm_g=m_o_sgu_norm_g, o_sgu_norm_b=m_o_sgu_norm_b, o_sgu_w=m_o_sgu_w, o_sgu_b=m_o_sgu_b,
             o_conv_w=m_o_conv_w, o_conv_b=m_o_conv_b, o_conv_norm_g=m_o_conv_norm_g, o_conv_norm_b=m_o_conv_norm_b,
             o_w_out=m_o_w_out, o_post_norm=m_o_post_norm)
    v = dict(e_pre_norm=v_e_pre_norm, e_w_in=v_e_w_in, e_pool_w=v_e_pool_w, e_pool_scale=v_e_pool_scale,
             e_w_out=v_e_w_out, e_post_norm=v_e_post_norm, o_pre_norm=v_o_pre_norm, o_w_in=v_o_w_in,
             o_sgu_norm_g=v_o_sgu_norm_g, o_sgu_norm_b=v_o_sgu_norm_b, o_sgu_w=v_o_sgu_w, o_sgu_b=v_o_sgu_b,
             o_conv_w=v_o_conv_w, o_conv_b=v_o_conv_b, o_conv_norm_g=v_o_conv_norm_g, o_conv_norm_b=v_o_conv_norm_b,
             o_w_out=v_o_w_out, o_post_norm=v_o_post_norm)
    w, m, v = ({k: a[0] for k, a in d.items()} for d in (w, m, v))
    chip = 2 * lax.axis_index("x") + lax.axis_index("y")

    loss, grad_x, in_flight, small = _step(x[0], loss_target[0], w, chip)

    grads, delta, new_m, new_v = {}, {}, {}, {}
    after = grad_x
    for k in ("o_w_out", "o_w_in", "e_w_out", "e_w_in"):
        grads[k], delta[k], new_m[k], new_v[k] = _adamw(w[k], _land(in_flight, k, chip, after), m[k], v[k], f"adamw_{k}")
        after = delta[k]

    def rows_of(a):
        return a.reshape(-1, a.shape[-1])

    small_grads = [small[k].reshape(SMALL_PACKING[k][0]) for k in SMALL_ORDER]
    updates = _small_finalize(small_grads, *[[rows_of(d[k]) for k in SMALL_ORDER] for d in (w, m, v)], after)
    for d, arrays in zip((grads, delta, new_m, new_v), updates):
        for k, a in zip(SMALL_ORDER, arrays):
            d[k] = a.reshape(w[k].shape)
    loss = lax.psum(loss[0, 0], ("x", "y", "c"))

    outs = [loss, grad_x[None]]
    for d in (grads, delta, new_m, new_v):
        outs += [d[k][None] for k in ALL_ORDER]
    return tuple(outs)
```

```python
import jax
import jax.numpy as jnp
from jax import lax
from jax.experimental import pallas as pl
from jax.experimental.pallas import tpu as pltpu

f32 = jnp.float32
bf16 = jnp.bfloat16
SDS = jax.ShapeDtypeStruct

SEQ = 2048
D_MODEL = 2048
EPS = 1e-6
NEG = -1e30
HEAD_DIM = 128
ROT_HALF = 16
ROPE_THETA = 500000.0
DILATIONS = (1, 4, 16)
SPAN = 128
N_HEADS = 8
HALF = 1024
POOL_CH = 256
CONV_K = 31
CONV_PAD = 32
CHUNK = 128
N_CHIPS = 4
LANES = 256
E_IN_PIECES = 3
SMALL_SHARD_ROWS = 352
ANY = pl.BlockSpec(memory_space=pl.ANY)
MESH = pl.DeviceIdType.MESH

ADAM_LR = 0.001
ADAM_B1 = 0.9
ADAM_B2 = 0.999
ADAM_EPS = 1e-08
ADAM_WD = 0.01
ADAM_STEP = 10


def _dot(a, b):
    return jnp.dot(a, b, preferred_element_type=f32)


def _dot_nt(a, b):
    return lax.dot_general(a, b, (((1,), (1,)), ((), ())), preferred_element_type=f32)


def _dot_tn(a, b):
    return lax.dot_general(a, b, (((0,), (0,)), ((), ())), preferred_element_type=f32)


def _sigmoid(x):
    return 1.0 / (1.0 + jnp.exp(-x))


def _silu_and_grad(x):
    s = _sigmoid(x)
    return x * s, s * (1.0 + x * (1.0 - s))


def _rms_fwd(x, g):
    r = lax.rsqrt(jnp.mean(x * x, axis=-1, keepdims=True) + EPS)
    return x * r * g


def _rms_bwd(x, g, dout):
    r = lax.rsqrt(jnp.mean(x * x, axis=-1, keepdims=True) + EPS)
    xh = x * r
    dg = jnp.sum(dout * xh, axis=0, keepdims=True)
    dxh = dout * g
    dx = r * (dxh - xh * jnp.mean(dxh * xh, axis=-1, keepdims=True))
    return dx, dg


def _ln_stats(x):
    mu = jnp.mean(x, axis=-1, keepdims=True)
    xc = x - mu
    rstd = lax.rsqrt(jnp.mean(xc * xc, axis=-1, keepdims=True) + EPS)
    return xc * rstd, rstd


def _ln_bwd(xh, rstd, g, dout):
    dg = jnp.sum(dout * xh, axis=0, keepdims=True)
    db = jnp.sum(dout, axis=0, keepdims=True)
    dxh = dout * g
    dx = rstd * (dxh - jnp.mean(dxh, axis=-1, keepdims=True) - xh * jnp.mean(dxh * xh, axis=-1, keepdims=True))
    return dx, dg, db


def _accumulate(ref, value, first):
    @pl.when(first)
    def _():
        ref[...] = value

    @pl.when(jnp.logical_not(first))
    def _():
        ref[...] += value


def _col_tile(ns):
    for t in (1024, 768, 512, 256):
        if ns % t == 0:
            return t
    raise ValueError(ns)


def _mm_nn(a, w, out_dtype, name, piece=0, pieces=1, into=None):
    m, k = a.shape
    j, _, ns = w.shape
    tm, tn = m, _col_tile(ns)
    nb = ns // tn

    def body(a_ref, w_ref, *rest):
        rest[-1][...] = _dot(a_ref[...], w_ref[...]).astype(rest[-1].dtype)

    return pl.pallas_call(
        body, name=name, grid=(j * nb, m // tm),
        in_specs=[pl.BlockSpec((tm, k), lambda n, i: (i, 0)),
                  pl.BlockSpec((None, k, tn), lambda n, i: (n // nb, 0, n % nb))] + ([] if into is None else [ANY]),
        out_specs=pl.BlockSpec((tm, tn), lambda n, i: (i, ((n // nb) * pieces + piece) * nb + n % nb)),
        out_shape=SDS((m, j * ns * pieces), out_dtype),
        input_output_aliases={} if into is None else {2: 0},
    )(a, w, *([] if into is None else [into]))


def _mm_nt(dz, ws, name, after):
    m, _ = dz.shape
    pieces = len(ws)
    j, k, ns = ws[0].shape
    tm, tk = 1024, 1024

    def body(dz_ref, *rest):
        w_refs, o_ref = rest[:pieces], rest[-1]
        total = _dot_nt(dz_ref[:, 0:ns], w_refs[0][...])
        for q in range(1, pieces):
            total = total + _dot_nt(dz_ref[:, q * ns:(q + 1) * ns], w_refs[q][...])
        _accumulate(o_ref, total, pl.program_id(2) == 0)

    return pl.pallas_call(
        body, name=name, grid=(m // tm, k // tk, j),
        in_specs=[pl.BlockSpec((tm, pieces * ns), lambda i, kk, r: (i, r))]
        + [pl.BlockSpec((None, tk, ns), lambda i, kk, r: (r, kk, 0))] * pieces + [ANY],
        out_specs=pl.BlockSpec((tm, tk), lambda i, kk, r: (i, kk)),
        out_shape=SDS((m, k), f32),
    )(dz, *ws, after)


def _mm_tn(a, dz, j, name):
    m, k = a.shape
    ns = dz.shape[1] // j
    tk, tn = 1024, _col_tile(ns)
    nb = ns // tn

    def body(a_ref, dz_ref, o_ref):
        o_ref[...] = _dot_tn(a_ref[...], dz_ref[...]).astype(o_ref.dtype)

    return pl.pallas_call(
        body, name=name, grid=(k // tk, j * nb),
        in_specs=[pl.BlockSpec((m, tk), lambda kk, n: (0, kk)),
                  pl.BlockSpec((m, tn), lambda kk, n: (0, n))],
        out_specs=pl.BlockSpec((None, tk, tn), lambda kk, n: (n // nb, kk, n % nb)),
        out_shape=SDS((j, k, ns), bf16),
    )(a, dz)


ROWS = 256


def _row_spec(width=D_MODEL, col=0):
    return pl.BlockSpec((ROWS, width), lambda i: (i, col))


def _vec_spec(width=D_MODEL):
    return pl.BlockSpec((1, width), lambda i: (0, 0))


def _pre_norm(x, g):
    def body(x_ref, g_ref, h_ref):
        h_ref[...] = _rms_fwd(x_ref[...], g_ref[...]).astype(bf16)

    return pl.pallas_call(
        body, name="pre_norm", grid=(SEQ // ROWS,), in_specs=[_row_spec(), _vec_spec()],
        out_specs=_row_spec(), out_shape=SDS((SEQ, D_MODEL), bf16))(x, g)


def _mid_norm(x, y, g_post, g_pre):
    def body(x_ref, y_ref, gpost_ref, gpre_ref, x1_ref, h1_ref):
        x1 = x_ref[...] + _rms_fwd(y_ref[...], gpost_ref[...])
        x1_ref[...] = x1
        h1_ref[...] = _rms_fwd(x1, gpre_ref[...]).astype(bf16)

    return pl.pallas_call(
        body, name="mid_norm", grid=(SEQ // ROWS,),
        in_specs=[_row_spec(), _row_spec(), _vec_spec(), _vec_spec()],
        out_specs=[_row_spec(), _row_spec()],
        out_shape=[SDS((SEQ, D_MODEL), f32), SDS((SEQ, D_MODEL), bf16)])(x, y, g_post, g_pre)


def _final_norm_loss(x1, y, g_post, target):
    def body(x1_ref, y_ref, g_ref, t_ref, loss_ref, dx2_ref, dy_ref, dg_ref):
        first = pl.program_id(0) == 0
        y = y_ref[...]
        g = g_ref[...]
        err = x1_ref[...] + _rms_fwd(y, g) - t_ref[...]
        sq = jnp.sum(jnp.sum(err * err, axis=1, keepdims=True), axis=0, keepdims=True)
        _accumulate(loss_ref, sq * (0.5 / D_MODEL), first)
        dx2 = err * (1.0 / D_MODEL)
        dx2_ref[...] = dx2
        dy, dg = _rms_bwd(y, g, dx2)
        dy_ref[...] = dy.astype(bf16)
        _accumulate(dg_ref, dg, first)

    return pl.pallas_call(
        body, name="final_norm_loss", grid=(SEQ // ROWS,),
        in_specs=[_row_spec(), _row_spec(), _vec_spec(), _row_spec()],
        out_specs=[pl.BlockSpec((1, 1), lambda i: (0, 0)), _row_spec(), _row_spec(), _vec_spec()],
        out_shape=[SDS((1, 1), f32), SDS((SEQ, D_MODEL), f32), SDS((SEQ, D_MODEL), bf16), SDS((1, D_MODEL), f32)],
    )(x1, y, g_post, target)


def _mid_norm_bwd(dx2, dh1, x1, y0, g_pre, g_post):
    def body(dx2_ref, dh1_ref, x1_ref, y0_ref, gpre_ref, gpost_ref, dx1_ref, dy0_ref, dgpre_ref, dgpost_ref):
        first = pl.program_id(0) == 0
        d_in, dgpre = _rms_bwd(x1_ref[...], gpre_ref[...], dh1_ref[...])
        dx1 = dx2_ref[...] + d_in
        dx1_ref[...] = dx1
        dy0, dgpost = _rms_bwd(y0_ref[...], gpost_ref[...], dx1)
        dy0_ref[...] = dy0.astype(bf16)
        _accumulate(dgpre_ref, dgpre, first)
        _accumulate(dgpost_ref, dgpost, first)

    return pl.pallas_call(
        body, name="mid_norm_bwd", grid=(SEQ // ROWS,),
        in_specs=[_row_spec(), _row_spec(), _row_spec(), _row_spec(), _vec_spec(), _vec_spec()],
        out_specs=[_row_spec(), _row_spec(), _vec_spec(), _vec_spec()],
        out_shape=[SDS((SEQ, D_MODEL), f32), SDS((SEQ, D_MODEL), bf16), SDS((1, D_MODEL), f32), SDS((1, D_MODEL), f32)],
    )(dx2, dh1, x1, y0, g_pre, g_post)


def _pre_norm_bwd(dx1, dh0, x, g):
    def body(dx1_ref, dh0_ref, x_ref, g_ref, dx_ref, dg_ref):
        d_in, dg = _rms_bwd(x_ref[...], g_ref[...], dh0_ref[...])
        dx_ref[...] = dx1_ref[...] + d_in
        _accumulate(dg_ref, dg, pl.program_id(0) == 0)

    return pl.pallas_call(
        body, name="pre_norm_bwd", grid=(SEQ // ROWS,),
        in_specs=[_row_spec(), _row_spec(), _row_spec(), _vec_spec()],
        out_specs=[_row_spec(), _vec_spec()],
        out_shape=[SDS((SEQ, D_MODEL), f32), SDS((1, D_MODEL), f32)])(dx1, dh0, x, g)


def _pool_count(g):
    row = lax.broadcasted_iota(jnp.int32, (SEQ, 1), 0)
    width = jnp.left_shift(2, g)
    return row, width, jnp.minimum(row + 1, width).astype(f32)


def _trailing_sum(x, row, width):
    s = x
    for k in (1, 2, 4, 8):
        shifted = jnp.where(row >= k, pltpu.roll(s, k, 0), 0.0)
        s = jnp.where(width > k, s + shifted, s)
    return s


def _leading_sum(x, row, width):
    s = x
    for k in (1, 2, 4, 8):
        shifted = jnp.where(row < SEQ - k, pltpu.roll(s, SEQ - k, 0), 0.0)
        s = jnp.where(width > k, s + shifted, s)
    return s


def _pool_specs():
    a_in = pl.BlockSpec((SEQ, POOL_CH), lambda g: (0, g))
    a_gate = pl.BlockSpec((SEQ, POOL_CH), lambda g: (0, 4 + g))
    w = pl.BlockSpec((None, POOL_CH, POOL_CH), lambda g: (g, 0, 0))
    scale = pl.BlockSpec((1, POOL_CH), lambda g: (0, g))
    return a_in, a_gate, w, scale


def _pool_fwd(z0, pool_w, pool_scale):
    def body(a_ref, gate_ref, w_ref, scale_ref, ya_ref):
        row, width, count = _pool_count(pl.program_id(0))
        a = a_ref[...]
        pooled = _trailing_sum(a, row, width) / count - a
        mixed = _dot(pooled.astype(bf16), w_ref[...]) * scale_ref[...]
        gate = gate_ref[...]
        ya_ref[...] = (mixed * gate * _sigmoid(gate)).astype(bf16)

    return pl.pallas_call(
        body, name="pool_fwd", grid=(4,), in_specs=list(_pool_specs()),
        out_specs=pl.BlockSpec((SEQ, POOL_CH), lambda g: (0, g)),
        out_shape=SDS((SEQ, 2 * HALF), bf16))(z0, z0, pool_w, pool_scale)


def _pool_bwd(z0, dcat, pool_w, pool_scale):
    def body(a_ref, gate_ref, w_ref, scale_ref, dya_ref, da_ref, dgate_ref, dw_ref, dscale_ref):
        row, width, count = _pool_count(pl.program_id(0))
        a = a_ref[...]
        pooled = (_trailing_sum(a, row, width) / count - a).astype(bf16)
        w = w_ref[...]
        scale = scale_ref[...]
        mixed = _dot(pooled, w)
        silu, dsilu = _silu_and_grad(gate_ref[...])
        dya = dya_ref[...]
        dgate_ref[...] = (dya * mixed * scale * dsilu).astype(bf16)
        dms = dya * silu
        dscale_ref[...] = jnp.sum(dms * mixed, axis=0, keepdims=True)
        dmixed = (dms * scale).astype(bf16)
        dw_ref[...] = _dot_tn(pooled, dmixed)
        dpooled = _dot_nt(dmixed, w)
        da_ref[...] = (_leading_sum(dpooled / count, row, width) - dpooled).astype(bf16)

    a_in, a_gate, w, scale = _pool_specs()
    col = pl.BlockSpec((SEQ, POOL_CH), lambda g: (0, g))
    return pl.pallas_call(
        body, name="pool_bwd", grid=(4,), in_specs=[a_in, a_gate, w, scale, col],
        out_specs=[col, col, w, scale],
        out_shape=[SDS((SEQ, HALF), bf16), SDS((SEQ, HALF), bf16), SDS((4, POOL_CH, POOL_CH), f32), SDS((1, HALF), f32)],
    )(z0, z0, pool_w, pool_scale, dcat)


Q_COL, K_COL, V_COL, BGATE_COL = 16, 40, 64, 88


def _rope_tables():
    pos = jnp.arange(SEQ, dtype=f32)
    inv_freq = jnp.power(ROPE_THETA, -jnp.arange(0, 2 * ROT_HALF, 2, dtype=f32) / (2 * ROT_HALF))
    ang = pos[:, None] * inv_freq[None, :]
    cos, sin = jnp.cos(ang), jnp.sin(ang)
    zeros = jnp.zeros((SEQ, HEAD_DIM - 2 * ROT_HALF), f32)
    cos_t = jnp.concatenate([cos, cos, zeros + 1.0], axis=1)
    sin_t = jnp.concatenate([sin, sin, zeros], axis=1)
    j = jnp.arange(HEAD_DIM)[:, None]
    i = jnp.arange(HEAD_DIM)[None, :]
    rot = jnp.where((i < ROT_HALF) & (j == i + ROT_HALF), -1.0, 0.0) + jnp.where(
        (i >= ROT_HALF) & (i < 2 * ROT_HALF) & (j == i - ROT_HALF), 1.0, 0.0)
    return cos_t, sin_t, rot.astype(bf16), rot.T.astype(bf16)


def _exact_dot(t, m):
    hi = t.astype(bf16)
    lo = (t - hi.astype(f32)).astype(bf16)
    return _dot(hi, m) + _dot(lo, m)


def _rope(t, cos_t, sin_t, rot):
    return t * cos_t + _exact_dot(t, rot) * sin_t


def _rope_transposed(d, cos_t, sin_t, rot_t):
    return d * cos_t + _exact_dot(d * sin_t, rot_t)


ROW_CHUNK = 256


def _chunks(fn):
    def step(i, carry):
        fn(pl.multiple_of(i * ROW_CHUNK, ROW_CHUNK))
        return carry

    lax.fori_loop(0, SEQ // ROW_CHUNK, step, 0, unroll=2)


def _pieces(dilation):
    length = SEQ // dilation
    n = min(length, ROW_CHUNK)
    return [(r, l0, n) for r in range(dilation) for l0 in range(0, length, n)]


def _by_residue(dst_ref, src_ref, dilation, dtype):
    length = SEQ // dilation
    for r, l0, n in _pieces(dilation):
        src = src_ref[l0:l0 + n, :] if dilation == 1 else src_ref[pl.ds(r + dilation * l0, n, stride=dilation), :]
        start = r * length + l0
        dst_ref[start:start + n, :] = src.astype(dtype)


def _by_position(dst_ref, src_ref, dilation):
    length = SEQ // dilation
    for r, l0, n in _pieces(dilation):
        src = src_ref[r * length + l0:r * length + l0 + n, :]
        if dilation == 1:
            dst_ref[l0:l0 + n, :] = src
        else:
            dst_ref[pl.ds(r + dilation * l0, n, stride=dilation), :] = src


def _attn_masks():
    qi = lax.broadcasted_iota(jnp.int32, (SPAN, 2 * SPAN), 0)
    kj = lax.broadcasted_iota(jnp.int32, (SPAN, 2 * SPAN), 1)
    window = ((kj < SPAN) & (kj >= qi)) | ((kj >= SPAN) & (kj - SPAN <= qi))
    own = lax.broadcasted_iota(jnp.int32, (SPAN, SPAN), 1) <= lax.broadcasted_iota(jnp.int32, (SPAN, SPAN), 0)
    return window, own


def _attn_blocks(dilation):
    per_residue = SEQ // dilation // SPAN
    blocks = [(c, c % per_residue != 0) for c in range(SEQ // SPAN)]
    return [blocks[i:i + 4] for i in range(0, len(blocks), 4)]


def _block_keys(c, has_prev):
    return slice((c - 1) * SPAN if has_prev else c * SPAN, (c + 1) * SPAN)


def _head_spec(col):
    return pl.BlockSpec((SEQ, HEAD_DIM), lambda h: (0, col + h))


def _table_spec():
    return pl.BlockSpec((SEQ, HEAD_DIM), lambda h: (0, 0))


def _attn_fwd(z0, tables, mixed):
    scale = HEAD_DIM ** -0.5

    def body(*refs):
        qkv = refs[0:9]
        bg_ref, cos_ref, sin_ref, rot_ref = refs[9:13]
        yb_ref, att_ref, lse_ref = refs[14:17]
        saved = refs[17:26]
        tmp_q, tmp_k, v_ones, o_res, l_res, o_nat, l_nat = refs[26:33]
        window_mask, own_mask = _attn_masks()
        rot = rot_ref[...]

        @pl.when(pl.program_id(0) == 0)
        def _():
            v_ones[:, HEAD_DIM:] = jnp.ones((SEQ, HEAD_DIM), bf16)

        for g, dilation in enumerate(DILATIONS):
            q_ref, k_ref, v_ref = qkv[3 * g:3 * g + 3]
            qd, kd, vd = saved[3 * g:3 * g + 3]

            def rope_rows(start, q_ref=q_ref, k_ref=k_ref):
                r = pl.ds(start, ROW_CHUNK)
                cos_t, sin_t = cos_ref[r, :], sin_ref[r, :]
                tmp_q[r, :] = _rope(q_ref[r, :], cos_t, sin_t, rot) * scale
                tmp_k[r, :] = _rope(k_ref[r, :], cos_t, sin_t, rot)

            _chunks(rope_rows)
            _by_residue(qd, tmp_q, dilation, bf16)
            _by_residue(kd, tmp_k, dilation, bf16)
            _by_residue(vd, v_ref, dilation, bf16)
            for l0 in range(0, SEQ, ROW_CHUNK):
                v_ones[l0:l0 + ROW_CHUNK, 0:HEAD_DIM] = vd[l0:l0 + ROW_CHUNK, :]

            for four in _attn_blocks(dilation):
                scores = [_dot_nt(qd[c * SPAN:(c + 1) * SPAN, :], kd[_block_keys(c, prev), :]) for c, prev in four]
                tops, probs = [], []
                for (c, prev), s in zip(four, scores):
                    s = jnp.where(window_mask if prev else own_mask, s, NEG)
                    tops.append(jnp.max(s, axis=1, keepdims=True))
                    probs.append(jnp.exp(s - tops[-1]).astype(bf16))
                sums = [_dot(p, v_ones[_block_keys(c, prev), :]) for (c, prev), p in zip(four, probs)]
                for (c, prev), m, o in zip(four, tops, sums):
                    den = o[:, HEAD_DIM:]
                    o_res[c * SPAN:(c + 1) * SPAN, :] = o[:, :HEAD_DIM] / den
                    l_res[c * SPAN:(c + 1) * SPAN, :] = m + jnp.log(den)

            if dilation > 1:
                _by_position(o_nat, o_res, dilation)
                _by_position(l_nat, l_res, dilation)
            o_g, l_g = (o_res, l_res) if dilation == 1 else (o_nat, l_nat)

            def merge(start, g=g, o_g=o_g, l_g=l_g):
                r = pl.ds(start, ROW_CHUNK)
                if g == 0:
                    att, total = o_g[r, :], l_g[r, :]
                else:
                    l_old, l_new = lse_ref[r, :], l_g[r, :]
                    top = jnp.maximum(l_old, l_new)
                    total = top + jnp.log(jnp.exp(l_old - top) + jnp.exp(l_new - top))
                    att = att_ref[r, :] * jnp.exp(l_old - total) + o_g[r, :] * jnp.exp(l_new - total)
                att_ref[r, :] = att
                lse_ref[r, :] = total
                if g == len(DILATIONS) - 1:
                    gate = bg_ref[r, :]
                    yb_ref[r, :] = (att * gate * _sigmoid(gate)).astype(bf16)

            _chunks(merge)

    in_specs = []
    for g in range(3):
        in_specs += [_head_spec(Q_COL + 8 * g), _head_spec(K_COL + 8 * g), _head_spec(V_COL + 8 * g)]
    in_specs += [_head_spec(BGATE_COL), _table_spec(), _table_spec(), pl.BlockSpec((HEAD_DIM, HEAD_DIM), lambda h: (0, 0)), ANY]
    out_spec = pl.BlockSpec((SEQ, HEAD_DIM), lambda h: (0, h))
    right_half = pl.BlockSpec((SEQ, HEAD_DIM), lambda h: (0, N_HEADS + h))
    vm = lambda dt: pltpu.VMEM((SEQ, HEAD_DIM), dt)
    cos_t, sin_t, rot, _ = tables
    out = pl.pallas_call(
        body, name="attn_fwd", grid=(N_HEADS,), in_specs=in_specs, out_specs=[right_half] + [out_spec] * 11,
        out_shape=[SDS((SEQ, 2 * HALF), bf16), SDS((SEQ, HALF), f32), SDS((SEQ, HALF), f32)] + [SDS((SEQ, HALF), bf16)] * 9,
        scratch_shapes=[vm(f32), vm(f32), pltpu.VMEM((SEQ, 2 * HEAD_DIM), bf16), vm(f32), vm(f32), vm(f32), vm(f32)],
        input_output_aliases={13: 0},
    )(*([z0] * 10), cos_t, sin_t, rot, mixed)
    return out[0], out[1], out[2], [tuple(out[3 + 3 * g:6 + 3 * g]) for g in range(3)]


def _attn_bwd_group(g, saved, z0, att, lse, dcat, tables):
    scale = HEAD_DIM ** -0.5
    dilation = DILATIONS[g]
    with_gate = g == 0

    def body(*refs):
        qd, kd, vd, bg_ref, att_ref, lse_ref, dyb_ref, cos_ref, sin_ref, rot_t_ref = refs[0:10]
        n_out = 4 if with_gate else 3
        dq_ref, dk_ref, dv_ref = refs[10:13]
        dod, ld, dd, tmp, aq, ak, av = refs[10 + n_out:17 + n_out]
        window_mask, own_mask = _attn_masks()
        rot_t = rot_t_ref[...]

        def gate_rows(start):
            r = pl.ds(start, ROW_CHUNK)
            silu, dsilu = _silu_and_grad(bg_ref[r, :])
            att_v = att_ref[r, :]
            dyb = dyb_ref[r, :]
            if with_gate:
                refs[13][r, :] = (dyb * att_v * dsilu).astype(bf16)
            datt = dyb * silu
            tmp[r, :] = datt
            aq[r, :] = jnp.broadcast_to(jnp.sum(datt * att_v, axis=1, keepdims=True), (ROW_CHUNK, HEAD_DIM))

        _chunks(gate_rows)
        _by_residue(dod, tmp, dilation, bf16)
        _by_residue(dd, aq, dilation, f32)
        _by_residue(ld, lse_ref, dilation, f32)

        for four in _attn_blocks(dilation):
            rows = [slice(c * SPAN, (c + 1) * SPAN) for c, _ in four]
            keys = [_block_keys(c, prev) for c, prev in four]
            scores = [_dot_nt(qd[r, :], kd[k, :]) for r, k in zip(rows, keys)]
            dprobs = [_dot_nt(dod[r, :], vd[k, :]) for r, k in zip(rows, keys)]
            probs, dscores = [], []
            for (c, prev), r, s, dp in zip(four, rows, scores, dprobs):
                lse_q, delta = ld[r, :], dd[r, :]
                if prev:
                    lse_q = jnp.concatenate([lse_q, lse_q], axis=1)
                    delta = jnp.concatenate([delta, delta], axis=1)
                p = jnp.where(window_mask if prev else own_mask, jnp.exp(s - lse_q), 0.0)
                probs.append(p.astype(bf16))
                dscores.append((p * (dp - delta)).astype(bf16))
            dvs = [_dot_tn(p, dod[r, :]) for p, r in zip(probs, rows)]
            dks = [_dot_tn(ds, qd[r, :]) for ds, r in zip(dscores, rows)]
            dqs = [_dot(ds, kd[k, :]) for ds, k in zip(dscores, keys)]
            for (c, prev), r, dv, dk, dq in zip(four, rows, dvs, dks, dqs):
                aq[r, :] = dq
                if prev:
                    before = slice((c - 1) * SPAN, c * SPAN)
                    av[before, :] += dv[0:SPAN]
                    ak[before, :] += dk[0:SPAN]
                    av[r, :] = dv[SPAN:]
                    ak[r, :] = dk[SPAN:]
                else:
                    av[r, :] = dv
                    ak[r, :] = dk

        def finish(out_ref, acc, factor, roped):
            if dilation > 1:
                _by_position(tmp, acc, dilation)
            src = acc if dilation == 1 else tmp

            def rows(start):
                r = pl.ds(start, ROW_CHUNK)
                d = src[r, :]
                if factor != 1.0:
                    d = d * factor
                if roped:
                    d = _rope_transposed(d, cos_ref[r, :], sin_ref[r, :], rot_t)
                out_ref[r, :] = d.astype(bf16)

            _chunks(rows)

        finish(dq_ref, aq, scale, True)
        finish(dk_ref, ak, 1.0, True)
        finish(dv_ref, av, 1.0, False)

    head = pl.BlockSpec((SEQ, HEAD_DIM), lambda h: (0, h))
    in_specs = [head, head, head, _head_spec(BGATE_COL), head, head, _head_spec(8), _table_spec(), _table_spec(),
                pl.BlockSpec((HEAD_DIM, HEAD_DIM), lambda h: (0, 0))]
    n_out = 4 if with_gate else 3
    vm = lambda dt: pltpu.VMEM((SEQ, HEAD_DIM), dt)
    cos_t, sin_t, _, rot_t = tables
    return pl.pallas_call(
        body, name=f"attn_bwd_g{g}", grid=(N_HEADS,), in_specs=in_specs, out_specs=[head] * n_out,
        out_shape=[SDS((SEQ, HALF), bf16)] * n_out,
        scratch_shapes=[vm(bf16), vm(f32), vm(f32), vm(f32), vm(f32), vm(f32), vm(f32)],
    )(*saved, z0, att, lse, dcat, cos_t, sin_t, rot_t)


def _sgu_specs():
    chunk = lambda col: pl.BlockSpec((CHUNK, HALF), lambda n: (n, col))
    vec = pl.BlockSpec((1, HALF), lambda n: (0, 0))
    w = pl.BlockSpec((4, CHUNK, CHUNK), lambda n: (0, 0, 0))
    bias = pl.BlockSpec((CHUNK, CHUNK), lambda n: (0, 0))
    return chunk, vec, w, bias


def _sgu_weights(w_ref):
    tril = lax.broadcasted_iota(jnp.int32, (CHUNK, CHUNK), 1) <= lax.broadcasted_iota(jnp.int32, (CHUNK, CHUNK), 0)
    return tril, [jnp.where(tril, w_ref[h], 0.0).astype(bf16) for h in range(4)]


def _sgu_fwd(z1, ln_g, ln_b, sgu_w, bias_t):
    def body(u_ref, v_ref, cg_ref, g_ref, b_ref, w_ref, bias_ref, yc_ref):
        _, ws = _sgu_weights(w_ref)
        xh, _ = _ln_stats(v_ref[...])
        vn = (xh * g_ref[...] + b_ref[...]).astype(bf16)
        for h in range(4):
            cols = slice(h * POOL_CH, (h + 1) * POOL_CH)
            s = _dot(ws[h], vn[:, cols]) + bias_ref[:, h:h + 1]
            gate = cg_ref[:, cols]
            yc_ref[:, cols] = (u_ref[:, cols] * s * gate * _sigmoid(gate)).astype(bf16)

    chunk, vec, w, bias = _sgu_specs()
    return pl.pallas_call(
        body, name="sgu_fwd", grid=(SEQ // CHUNK,),
        in_specs=[chunk(0), chunk(1), chunk(2), vec, vec, w, bias], out_specs=chunk(0),
        out_shape=SDS((SEQ, 2 * HALF), bf16))(z1, z1, z1, ln_g, ln_b, sgu_w, bias_t)


def _sgu_bwd(z1, dcat, ln_g, ln_b, sgu_w, bias_t):
    def body(u_ref, v_ref, cg_ref, dyc_ref, g_ref, b_ref, w_ref, bias_ref,
             du_ref, dv_ref, dcg_ref, dw_ref, dbias_ref, dg_ref, db_ref, dvn_ref):
        first = pl.program_id(0) == 0
        tril, ws = _sgu_weights(w_ref)
        xh, rstd = _ln_stats(v_ref[...])
        g = g_ref[...]
        vn = (xh * g + b_ref[...]).astype(bf16)

        @pl.when(first)
        def _():
            dbias_ref[...] = jnp.zeros((CHUNK, CHUNK), f32)

        for h in range(4):
            cols = slice(h * POOL_CH, (h + 1) * POOL_CH)
            vn_h = vn[:, cols]
            s = _dot(ws[h], vn_h) + bias_ref[:, h:h + 1]
            silu, dsilu = _silu_and_grad(cg_ref[:, cols])
            dyc = dyc_ref[:, cols]
            u = u_ref[:, cols]
            du_ref[:, cols] = (dyc * s * silu).astype(bf16)
            dcg_ref[:, cols] = (dyc * u * s * dsilu).astype(bf16)
            ds = dyc * u * silu
            dbias_ref[:, h:h + 1] += jnp.sum(ds, axis=1, keepdims=True)
            ds = ds.astype(bf16)
            _accumulate(dw_ref.at[h], jnp.where(tril, _dot_nt(ds, vn_h), 0.0), first)
            dvn_ref[:, cols] = _dot_tn(ws[h], ds)
        dv, dg, db = _ln_bwd(xh, rstd, g, dvn_ref[...])
        dv_ref[...] = dv.astype(bf16)
        _accumulate(dg_ref, dg, first)
        _accumulate(db_ref, db, first)

    chunk, vec, w, bias = _sgu_specs()
    return pl.pallas_call(
        body, name="sgu_bwd", grid=(SEQ // CHUNK,),
        in_specs=[chunk(0), chunk(1), chunk(2), chunk(0), vec, vec, w, bias],
        out_specs=[chunk(0), chunk(0), chunk(0), w, bias, vec, vec],
        out_shape=[SDS((SEQ, HALF), bf16)] * 3 + [SDS((4, CHUNK, CHUNK), f32), SDS((CHUNK, CHUNK), f32),
                                                   SDS((1, HALF), f32), SDS((1, HALF), f32)],
        scratch_shapes=[pltpu.VMEM((CHUNK, HALF), f32)],
    )(z1, z1, z1, dcat, ln_g, ln_b, sgu_w, bias_t)


CONV_TILE = 128
DVAL_COL, DGLU_COL = 12, 16


def _conv_specs():
    val = pl.BlockSpec((SEQ, POOL_CH), lambda j: (0, DVAL_COL + j))
    glu = pl.BlockSpec((SEQ, POOL_CH), lambda j: (0, DGLU_COL + j))
    w = pl.BlockSpec((CONV_K, POOL_CH), lambda j: (0, j))
    col = pl.BlockSpec((SEQ, POOL_CH), lambda j: (0, j))
    vec = pl.BlockSpec((1, POOL_CH), lambda j: (0, j))
    return val, glu, w, col, vec


def _conv_fwd(z1, conv_w, conv_b):
    def body(val_ref, glu_ref, w_ref, b_ref, out_ref, xpad):
        xpad[0:CONV_PAD, :] = jnp.zeros((CONV_PAD, POOL_CH), f32)
        xpad[CONV_PAD:, :] = val_ref[...] * _sigmoid(glu_ref[...])
        w = w_ref[...]
        bias = b_ref[...]

        def tile(i, carry):
            t0 = pl.multiple_of(i * CONV_TILE, CONV_TILE)
            window = xpad[pl.ds(t0, CONV_TILE + CONV_PAD), :]
            acc = jnp.broadcast_to(bias, (CONV_TILE, POOL_CH))
            for k in range(CONV_K):
                shift = CONV_PAD - (CONV_K - 1) + k
                acc = acc + w[k:k + 1, :] * pltpu.roll(window, CONV_TILE + CONV_PAD - shift, 0)[0:CONV_TILE]
            out_ref[pl.ds(t0, CONV_TILE), :] = acc
            return carry

        lax.fori_loop(0, SEQ // CONV_TILE, tile, 0)

    val, glu, w, col, vec = _conv_specs()
    return pl.pallas_call(
        body, name="conv_fwd", grid=(4,), in_specs=[val, glu, w, vec], out_specs=col,
        out_shape=SDS((SEQ, HALF), f32), scratch_shapes=[pltpu.VMEM((SEQ + CONV_PAD, POOL_CH), f32)],
    )(z1, z1, conv_w, conv_b)


def _conv_bwd(z1, dconv, conv_w):
    def body(val_ref, glu_ref, w_ref, dout_ref, dval_ref, dglu_ref, dw_ref, db_ref, xpad, dpad, dx_ref):
        val = val_ref[...]
        sig = _sigmoid(glu_ref[...])
        xpad[0:CONV_PAD, :] = jnp.zeros((CONV_PAD, POOL_CH), f32)
        xpad[CONV_PAD:, :] = val * sig
        dout = dout_ref[...]
        dpad[0:SEQ, :] = dout
        dpad[SEQ:, :] = jnp.zeros((CONV_PAD, POOL_CH), f32)
        db_ref[...] = jnp.sum(dout, axis=0, keepdims=True)
        dw_ref[...] = jnp.zeros((CONV_K, POOL_CH), f32)
        w = w_ref[...]

        def tile(i, carry):
            t0 = pl.multiple_of(i * CONV_TILE, CONV_TILE)
            x_win = xpad[pl.ds(t0, CONV_TILE + CONV_PAD), :]
            d_win = dpad[pl.ds(t0, CONV_TILE + CONV_PAD), :]
            d_own = d_win[0:CONV_TILE]
            acc = jnp.zeros((CONV_TILE, POOL_CH), f32)
            for k in range(CONV_K):
                shift = CONV_PAD - (CONV_K - 1) + k
                x_k = pltpu.roll(x_win, CONV_TILE + CONV_PAD - shift, 0)[0:CONV_TILE]
                dw_ref[k:k + 1, :] += jnp.sum(d_own * x_k, axis=0, keepdims=True)
                back = CONV_K - 1 - k
                d_k = d_own if back == 0 else pltpu.roll(d_win, CONV_TILE + CONV_PAD - back, 0)[0:CONV_TILE]
                acc = acc + w[k:k + 1, :] * d_k
            dx_ref[pl.ds(t0, CONV_TILE), :] = acc
            return carry

        lax.fori_loop(0, SEQ // CONV_TILE, tile, 0)
        dx = dx_ref[...]
        dval_ref[...] = (dx * sig).astype(bf16)
        dglu_ref[...] = (dx * val * sig * (1.0 - sig)).astype(bf16)

    val, glu, w, col, vec = _conv_specs()
    pad = pltpu.VMEM((SEQ + CONV_PAD, POOL_CH), f32)
    return pl.pallas_call(
        body, name="conv_bwd", grid=(4,), in_specs=[val, glu, w, col], out_specs=[col, col, w, vec],
        out_shape=[SDS((SEQ, HALF), bf16), SDS((SEQ, HALF), bf16), SDS((CONV_K, HALF), f32), SDS((1, HALF), f32)],
        scratch_shapes=[pad, pad, pltpu.VMEM((SEQ, POOL_CH), f32)],
    )(z1, z1, conv_w, dconv)


DGATE_COL = 5


def _conv_norm_fwd(conv, z1, g, b, mixed):
    def body(c_ref, gate_ref, g_ref, b_ref, mixed_ref, yd_ref):
        xh, _ = _ln_stats(c_ref[...])
        n = xh * g_ref[...] + b_ref[...]
        gate = gate_ref[...]
        yd_ref[...] = (n * _sigmoid(n) * gate * _sigmoid(gate)).astype(bf16)

    return pl.pallas_call(
        body, name="conv_norm_fwd", grid=(SEQ // ROWS,),
        in_specs=[_row_spec(HALF), _row_spec(HALF, DGATE_COL), _vec_spec(HALF), _vec_spec(HALF), ANY],
        out_specs=_row_spec(HALF, 1), out_shape=SDS((SEQ, 2 * HALF), bf16), input_output_aliases={4: 0},
    )(conv, z1, g, b, mixed)


def _conv_norm_bwd(conv, z1, dcat, g, b):
    def body(c_ref, gate_ref, dyd_ref, g_ref, b_ref, dconv_ref, dgate_ref, dg_ref, db_ref):
        first = pl.program_id(0) == 0
        xh, rstd = _ln_stats(c_ref[...])
        g = g_ref[...]
        n_silu, n_dsilu = _silu_and_grad(xh * g + b_ref[...])
        gate_silu, gate_dsilu = _silu_and_grad(gate_ref[...])
        dyd = dyd_ref[...]
        dgate_ref[...] = (dyd * n_silu * gate_dsilu).astype(bf16)
        dconv, dg, db = _ln_bwd(xh, rstd, g, dyd * gate_silu * n_dsilu)
        dconv_ref[...] = dconv
        _accumulate(dg_ref, dg, first)
        _accumulate(db_ref, db, first)

    return pl.pallas_call(
        body, name="conv_norm_bwd", grid=(SEQ // ROWS,),
        in_specs=[_row_spec(HALF), _row_spec(HALF, DGATE_COL), _row_spec(HALF, 1), _vec_spec(HALF), _vec_spec(HALF)],
        out_specs=[_row_spec(HALF), _row_spec(HALF), _vec_spec(HALF), _vec_spec(HALF)],
        out_shape=[SDS((SEQ, HALF), f32), SDS((SEQ, HALF), bf16), SDS((1, HALF), f32), SDS((1, HALF), f32)],
    )(conv, z1, dcat, g, b)


def _step(x, target, w, chip):
    chip_vec = chip.astype(jnp.int32).reshape(1)
    sharded_names = list(SHARDED_SMALL)
    first = [_cast_into_slot(w["e_w_in"], chip_vec, "cast_e_w_in0", w["e_pre_norm"], 0, E_IN_PIECES)]
    sems, bufs, token = _gather_start(first, "gather_start_first")
    small_shard = _pack([w[k] for k in sharded_names], total_rows=SMALL_SHARD_ROWS) + 0.0 * token[0, 0]
    small_slot = lax.dynamic_update_slice(jnp.zeros((N_CHIPS, SMALL_SHARD_ROWS, LANES), f32), small_shard[None], (chip, 0, 0))
    more = [small_slot]
    more += [_cast_into_slot(w["e_w_in"], chip_vec, f"cast_e_w_in{i}", token, i, E_IN_PIECES) for i in range(1, E_IN_PIECES)]
    more_sems, more_bufs, token = _gather_start(more, "gather_start_pieces")
    rest = [_cast_into_slot(w[k], chip_vec, f"cast_{k}", token) for k in BIG[1:]]
    rest_sems, rest_bufs, token = _gather_start(rest, "gather_start_rest")
    sems, bufs = sems + more_sems + rest_sems, bufs + more_bufs + rest_bufs
    tables = _rope_tables()

    def vec(k):
        return w[k].reshape(1, -1)

    h0 = _pre_norm(x, vec("e_pre_norm") + token[0, 0])
    after, z0, e_w_in = h0, None, []
    for i in range(E_IN_PIECES):
        group = slice(0, 1) if i == 0 else slice(1, 3) if i == 1 else slice(i + 1, i + 2)
        landed = _forward_halves(_gather_wait(bufs[group], sems[group], after, f"gather_wait_{i}"), f"forward_{i}")
        if i == 1:
            small_full = landed[0]
        e_w_in.append(landed[-1])
        z0 = _mm_nn(h0, landed[-1], f32, f"e_in{i}", i, E_IN_PIECES, z0)
        after = z0
    p = {k: _from_chips(k, a) for k, a in zip(sharded_names, _unpack(small_full, [SHARDED_SMALL[k][0] for k in sharded_names]))}
    for k in ("o_pre_norm", "o_sgu_norm_g", "o_sgu_norm_b", "o_conv_b", "o_conv_norm_g", "o_conv_norm_b", "o_post_norm"):
        p[k] = p[k].reshape(1, -1)
    pool_w_bf = p["e_pool_w"].astype(bf16)
    bias_t = jnp.pad(w["o_sgu_b"].T, ((0, 0), (0, CHUNK - 4)))

    cat0, att, lse, qkv_by_residue = _attn_fwd(z0, tables, _pool_fwd(z0, pool_w_bf, vec("e_pool_scale")))

    def arrived(index, after, name):
        one = slice(index, index + 1)
        return _forward_halves(_gather_wait(bufs[one], sems[one], after, f"gather_wait_{name}"), f"forward_{name}")[0]

    e_w_out = arrived(1 + E_IN_PIECES, att, "e_w_out").reshape(1, D_MODEL, D_MODEL)
    y0 = _mm_nn(cat0, e_w_out, f32, "e_out")
    x1, h1 = _mid_norm(x, y0, vec("e_post_norm"), p["o_pre_norm"])
    o_w_in = arrived(2 + E_IN_PIECES, h1, "o_w_in")
    z1 = _mm_nn(h1, o_w_in, f32, "o_in")
    yc = _sgu_fwd(z1, p["o_sgu_norm_g"], p["o_sgu_norm_b"], w["o_sgu_w"], bias_t)
    conv = _conv_fwd(z1, p["o_conv_w"], p["o_conv_b"])
    cat1 = _conv_norm_fwd(conv, z1, p["o_conv_norm_g"], p["o_conv_norm_b"], yc)
    o_w_out = arrived(3 + E_IN_PIECES, cat1, "o_w_out").reshape(1, D_MODEL, D_MODEL)
    y1 = _mm_nn(cat1, o_w_out, f32, "o_out")
    loss, dx2, dy1, g_o_post = _final_norm_loss(x1, y1, p["o_post_norm"], target)

    in_flight = {}

    def send_off(name, grad):
        sem, sums, land, tok = _scatter_start(_swap_add(grad, f"swap_add_{name}"), f"scatter_start_{name}")
        in_flight[name] = (sem, sums, land)
        return tok

    tok = send_off("o_w_out", _mm_tn(cat1, dy1, 1, "o_out_dw").reshape(N_CHIPS, HALF // 2, D_MODEL))
    dcat1 = _mm_nt(dy1, [o_w_out], "o_out_dx", tok)
    du, dv, dcg, g_sgu_w, g_bias_t, g_sgu_g, g_sgu_b = _sgu_bwd(
        z1, dcat1, p["o_sgu_norm_g"] + tok[0, 0], p["o_sgu_norm_b"], w["o_sgu_w"], bias_t)
    dconv, ddgate, g_cn_g, g_cn_b = _conv_norm_bwd(conv, z1, dcat1, p["o_conv_norm_g"], p["o_conv_norm_b"])
    ddval, ddglu, g_conv_w, g_conv_b = _conv_bwd(z1, dconv, p["o_conv_w"])
    dz1 = jnp.concatenate([du, dv, dcg, ddval, ddglu, ddgate], axis=1)
    tok = send_off("o_w_in", _mm_tn(h1, dz1, N_CHIPS, "o_in_dw"))
    dh1 = _mm_nt(dz1, [o_w_in], "o_in_dx", tok)
    dx1, dy0, g_o_pre, g_e_post = _mid_norm_bwd(dx2, dh1, x1, y0, p["o_pre_norm"] + tok[0, 0], vec("e_post_norm"))

    tok = send_off("e_w_out", _mm_tn(cat0, dy0, 1, "e_out_dw").reshape(N_CHIPS, HALF // 2, D_MODEL))
    dcat0 = _mm_nt(dy0, [e_w_out], "e_out_dx", tok)
    da, dagate, g_pool_w, g_pool_scale = _pool_bwd(z0, dcat0, pool_w_bf, vec("e_pool_scale") + tok[0, 0])
    dq0, dk0, dv0, dbgate = _attn_bwd_group(0, qkv_by_residue[0], z0, att, lse, dcat0, tables)
    dq1, dk1, dv1 = _attn_bwd_group(1, qkv_by_residue[1], z0, att, lse, dcat0, tables)
    dq2, dk2, dv2 = _attn_bwd_group(2, qkv_by_residue[2], z0, att, lse, dcat0, tables)
    dz0 = jnp.concatenate([da, dagate, dq0, dq1, dq2, dk0, dk1, dk2, dv0, dv1, dv2, dbgate], axis=1)
    tok = send_off("e_w_in", _mm_tn(h0, dz0, N_CHIPS, "e_in_dw"))
    dh0 = _mm_nt(dz0, e_w_in, "e_in_dx", tok)
    grad_x, g_e_pre = _pre_norm_bwd(dx1, dh0, x, vec("e_pre_norm") + tok[0, 0])

    small = {"e_pre_norm": g_e_pre, "e_pool_w": g_pool_w, "e_pool_scale": g_pool_scale, "e_post_norm": g_e_post,
             "o_pre_norm": g_o_pre, "o_sgu_norm_g": g_sgu_g, "o_sgu_norm_b": g_sgu_b, "o_sgu_w": g_sgu_w,
             "o_sgu_b": g_bias_t, "o_conv_w": g_conv_w, "o_conv_b": g_conv_b,
             "o_conv_norm_g": g_cn_g, "o_conv_norm_b": g_cn_b, "o_post_norm": g_o_post}
    return loss, grad_x, in_flight, small


def _land(in_flight, name, chip, after):
    sems, sums, land = in_flight[name]
    sums, land = _scatter_wait(sems, sums, land, after, f"scatter_wait_{name}")
    return _add_landed_join(sums, land, chip.astype(jnp.int32).reshape(1), f"add_landed_{name}")


def _place():
    x, y, c = lax.axis_index("x"), lax.axis_index("y"), lax.axis_index("c")
    others = [(1 - x, y), (x, 1 - y), (1 - x, 1 - y)]
    return x, y, c, 2 * x + y, others


SWAP_ROWS = 256
FORWARD_STAGE_BYTES = 4 << 20


def _swap_add(g, name):
    chips, r, c = g.shape
    half = r // 2
    rows_per_step = 2 * SWAP_ROWS if half % (2 * SWAP_ROWS) == 0 else SWAP_ROWS
    nb = half // rows_per_step
    steps = chips * nb

    def body(core_ref, mine_ref, theirs_ref, out_ref, landing, send_sems, recv_sems, free_sems):
        i = pl.program_id(0)
        x, y, core, _, _ = _place()
        sibling = (x, y, 1 - core)

        def send(slot):
            return pltpu.make_async_remote_copy(src_ref=theirs_ref, dst_ref=landing.at[slot], send_sem=send_sems.at[slot],
                                                recv_sem=recv_sems.at[slot], device_id=sibling, device_id_type=MESH)

        @pl.when(i < steps)
        def _():
            @pl.when(i >= 2)
            def _():
                pl.semaphore_wait(free_sems.at[i % 2], 1)

            send(i % 2).start()

        @pl.when(i >= 1)
        def _():
            landed = (i - 1) % 2
            send(landed).wait_recv()
            out_ref[...] = (mine_ref[...].astype(f32) + landing[landed].astype(f32)).astype(out_ref.dtype)

            @pl.when(i + 1 < steps)
            def _():
                pl.semaphore_signal(free_sems.at[landed], 1, device_id=sibling, device_id_type=MESH)

        @pl.when(i < steps)
        def _():
            send(i % 2).wait_send()

    def rows_of(b, h):
        return (2 * (b // nb) + h) * nb + b % nb

    block = (rows_per_step, c)
    grid_spec = pltpu.PrefetchScalarGridSpec(
        num_scalar_prefetch=1, grid=(steps + 1,),
        in_specs=[pl.BlockSpec(block, lambda i, core: (rows_of(jnp.maximum(i - 1, 0), core[0]), 0)),
                  pl.BlockSpec(block, lambda i, core: (rows_of(jnp.minimum(i, steps - 1), 1 - core[0]), 0))],
        out_specs=pl.BlockSpec(block, lambda i, core: (jnp.maximum(i - 1, 0), 0)),
        scratch_shapes=[pltpu.VMEM((2, rows_per_step, c), g.dtype), pltpu.SemaphoreType.DMA((2,)),
                        pltpu.SemaphoreType.DMA((2,)), pltpu.SemaphoreType.REGULAR((2,))])
    core = lax.axis_index("c").astype(jnp.int32).reshape(1)
    rows = g.reshape(chips * r, c)
    out = pl.pallas_call(body, name=name, grid_spec=grid_spec, out_shape=SDS((chips * half, c), g.dtype))(core, rows, rows)
    return out.reshape(chips, half, c)


HBM = pl.BlockSpec(memory_space=pltpu.HBM)
SEM = pl.BlockSpec(memory_space=pltpu.SEMAPHORE)
EFFECT = pltpu.SideEffectType.DATAFLOW_SIDE_EFFECTING


def _in_hbm(a):
    return pltpu.with_memory_space_constraint(a, pltpu.HBM)


def _cast_into_slot(w, chip, name, after, piece=0, pieces=1):
    r, c = w.shape
    c = c // pieces
    nb = r // SWAP_ROWS

    def body(chip_ref, w_ref, after_ref, o_ref):
        o_ref[...] = w_ref[...].astype(bf16)

    grid_spec = pltpu.PrefetchScalarGridSpec(
        num_scalar_prefetch=1, grid=(nb,),
        in_specs=[pl.BlockSpec((SWAP_ROWS, c), lambda i, chip: (i, piece)), ANY],
        out_specs=pl.BlockSpec((SWAP_ROWS, c), lambda i, chip: (chip[0] * nb + i, 0)))
    out = pl.pallas_call(body, name=name, grid_spec=grid_spec, out_shape=SDS((N_CHIPS * r, c), bf16))(chip, w, after)
    return out.reshape(N_CHIPS, r, c)


def _gather_start(bufs, name):
    n = len(bufs)

    def body(*refs):
        ins, sems, token = refs[:n], refs[n:3 * n], refs[4 * n]
        x, y, c, me, others = _place()
        for a in range(n):
            rows = ins[a].shape[1] // 2
            mine = ins[a].at[me, pl.ds(c * rows, rows), :]
            for k, (ox, oy) in enumerate(others):
                pltpu.make_async_remote_copy(src_ref=mine, dst_ref=mine, send_sem=sems[2 * a].at[k],
                                             recv_sem=sems[2 * a + 1].at[k], device_id=(ox, oy, c),
                                             device_id_type=MESH).start()
        token[...] = jnp.zeros_like(token)

    out = pl.pallas_call(
        body, name=name, in_specs=[HBM] * n,
        out_shape=(*[pltpu.SemaphoreType.DMA((3,))] * (2 * n), *[pltpu.HBM(b.shape, b.dtype) for b in bufs],
                   SDS((8, 128), f32)),
        out_specs=(*[SEM] * (2 * n), *[HBM] * n, pl.BlockSpec(memory_space=pltpu.VMEM)),
        input_output_aliases={a: 2 * n + a for a in range(n)},
        compiler_params=pltpu.CompilerParams(has_side_effects=EFFECT),
    )(*[_in_hbm(b) for b in bufs])
    return [(out[2 * a], out[2 * a + 1]) for a in range(n)], list(out[2 * n:3 * n]), out[3 * n]


def _gather_wait(bufs, sems, after, name):
    n = len(bufs)

    def body(*refs):
        ins, sem_refs = refs[:n], refs[n:3 * n]
        x, y, c, me, others = _place()
        for a in range(n):
            rows = ins[a].shape[1] // 2
            mine = ins[a].at[me, pl.ds(c * rows, rows), :]
            for k, (ox, oy) in enumerate(others):
                landed = ins[a].at[2 * ox + oy, pl.ds(c * rows, rows), :]
                copy = pltpu.make_async_remote_copy(src_ref=mine, dst_ref=landed, send_sem=sem_refs[2 * a].at[k],
                                                    recv_sem=sem_refs[2 * a + 1].at[k], device_id=(ox, oy, c),
                                                    device_id_type=MESH)
                copy.wait_send()
                copy.wait_recv()

    flat_sems = [s for pair in sems for s in pair]
    out = pl.pallas_call(
        body, name=name, in_specs=[HBM] * n + [SEM] * (2 * n) + [ANY],
        out_shape=tuple(pltpu.HBM(b.shape, b.dtype) for b in bufs), out_specs=tuple([HBM] * n),
        input_output_aliases={a: a for a in range(n)},
        compiler_params=pltpu.CompilerParams(has_side_effects=EFFECT),
    )(*bufs, *flat_sems, after)
    return list(out)


def _forward_halves(bufs, name):
    n = len(bufs)
    blocks = []
    for b in bufs:
        half = b.shape[1] // 2
        whole = half * b.shape[2] * b.dtype.itemsize <= FORWARD_STAGE_BYTES
        blocks.append((half, half if whole or half % SWAP_ROWS else SWAP_ROWS))
    work = [(a, k, b) for a in range(n) for k in range(3) for b in range(blocks[a][0] // blocks[a][1])]

    def body(*refs):
        outs, stages = refs[n:2 * n], refs[2 * n:3 * n]
        load_sems, send_sems, recv_sems = refs[3 * n:]
        x, y, c, me, others = _place()
        sibling = (x, y, 1 - c)

        def rows(item):
            a, k, b = item
            half, tr = blocks[a]
            ox, oy = others[k]
            return outs[a].at[2 * ox + oy, pl.ds(c * half + b * tr, tr), :]

        def load(s, item):
            return pltpu.make_async_copy(rows(item), stages[item[0]].at[s], load_sems.at[s])

        def send(s, item):
            return pltpu.make_async_remote_copy(src_ref=stages[item[0]].at[s], dst_ref=rows(item), send_sem=send_sems.at[s],
                                                recv_sem=recv_sems.at[item[0]], device_id=sibling, device_id_type=MESH)

        load(0, work[0]).start()
        for t, item in enumerate(work):
            s = t % 2
            load(s, item).wait()
            send(s, item).start()
            if t + 1 < len(work):
                if t >= 1:
                    send(1 - s, work[t - 1]).wait_send()
                load(1 - s, work[t + 1]).start()
        if len(work) > 1:
            send(len(work) % 2, work[-2]).wait_send()
        send((len(work) - 1) % 2, work[-1]).wait_send()
        for a in range(n):
            theirs = outs[a].at[pl.ds(0, 3), pl.ds(0, blocks[a][0]), :]
            pltpu.make_async_remote_copy(src_ref=theirs, dst_ref=theirs, send_sem=send_sems.at[0], recv_sem=recv_sems.at[a],
                                         device_id=sibling, device_id_type=MESH).wait_recv()

    out = pl.pallas_call(
        body, name=name, in_specs=[ANY] * n, out_specs=[ANY] * n, out_shape=[SDS(b.shape, b.dtype) for b in bufs],
        input_output_aliases={a: a for a in range(n)},
        scratch_shapes=[pltpu.VMEM((2, blocks[a][1], bufs[a].shape[2]), bufs[a].dtype) for a in range(n)]
        + [pltpu.SemaphoreType.DMA((2,)), pltpu.SemaphoreType.DMA((2,)), pltpu.SemaphoreType.DMA((n,))],
    )(*bufs)
    return list(out)


def _scatter_start(chip_sums, name):
    def body(a_ref, land_ref, send_sems, recv_sems, a_thru, land_thru, token):
        x, y, c, me, others = _place()
        for k, (ox, oy) in enumerate(others):
            pltpu.make_async_remote_copy(src_ref=a_ref.at[2 * ox + oy], dst_ref=land_ref.at[me], send_sem=send_sems.at[k],
                                         recv_sem=recv_sems.at[k], device_id=(ox, oy, c), device_id_type=MESH).start()
        token[...] = jnp.zeros_like(token)

    shape = pltpu.HBM(chip_sums.shape, chip_sums.dtype)
    send, recv, a_thru, land, token = pl.pallas_call(
        body, name=name, in_specs=[HBM, HBM],
        out_shape=(pltpu.SemaphoreType.DMA((3,)), pltpu.SemaphoreType.DMA((3,)), shape, shape, SDS((8, 128), f32)),
        out_specs=(SEM, SEM, HBM, HBM, pl.BlockSpec(memory_space=pltpu.VMEM)), input_output_aliases={0: 2, 1: 3},
        compiler_params=pltpu.CompilerParams(has_side_effects=EFFECT),
    )(_in_hbm(chip_sums), _in_hbm(lax.empty(chip_sums.shape, chip_sums.dtype)))
    return (send, recv), a_thru, land, token


def _scatter_wait(sems, chip_sums, land, after, name):
    def body(a_ref, land_ref, send_sems, recv_sems, after_ref, a_out, land_out):
        x, y, c, me, others = _place()
        for k, (ox, oy) in enumerate(others):
            copy = pltpu.make_async_remote_copy(
                src_ref=a_ref.at[2 * ox + oy], dst_ref=land_ref.at[2 * ox + oy], send_sem=send_sems.at[k],
                recv_sem=recv_sems.at[k], device_id=(ox, oy, c), device_id_type=MESH)
            copy.wait_send()
            copy.wait_recv()

    shape = pltpu.HBM(chip_sums.shape, chip_sums.dtype)
    return pl.pallas_call(
        body, name=name, in_specs=[HBM, HBM, SEM, SEM, ANY], out_shape=(shape, shape), out_specs=(HBM, HBM),
        input_output_aliases={0: 0, 1: 1}, compiler_params=pltpu.CompilerParams(has_side_effects=EFFECT),
    )(chip_sums, land, sems[0], sems[1], after)


def _add_landed_join(chip_sums, land, chip, name):
    chips, rh, c = chip_sums.shape
    nb = rh // SWAP_ROWS

    def body(chip_ref, own_ref, l1_ref, l2_ref, l3_ref, out_hbm, buf, send_sems, recv_sem, local_sems):
        i = pl.program_id(0)
        slot = i % 2
        x, y, core, _, _ = _place()
        sibling = (x, y, 1 - core)

        def copies(s, step):
            rows = pl.ds(pl.multiple_of((core * nb + step) * SWAP_ROWS, SWAP_ROWS), SWAP_ROWS)
            keep = pltpu.make_async_copy(buf.at[s], out_hbm.at[rows, :], local_sems.at[s])
            give = pltpu.make_async_remote_copy(src_ref=buf.at[s], dst_ref=out_hbm.at[rows, :], send_sem=send_sems.at[s],
                                                recv_sem=recv_sem.at[0], device_id=sibling, device_id_type=MESH)
            return keep, give

        def drain(s, step):
            keep, give = copies(s, step)
            keep.wait()
            give.wait_send()

        @pl.when(i >= 2)
        def _():
            drain(slot, i - 2)

        buf[slot] = ((own_ref[...].astype(f32) + l1_ref[...].astype(f32)) + l2_ref[...].astype(f32)) + l3_ref[...].astype(f32)
        keep, give = copies(slot, i)
        keep.start()
        give.start()

        @pl.when(i == nb - 1)
        def _():
            drain(slot, i)
            if nb > 1:
                drain(1 - slot, i - 1)
            theirs = out_hbm.at[pl.ds((1 - core) * rh, rh), :]
            pltpu.make_async_remote_copy(src_ref=theirs, dst_ref=theirs, send_sem=send_sems.at[0], recv_sem=recv_sem.at[0],
                                         device_id=sibling, device_id_type=MESH).wait_recv()

    block = (SWAP_ROWS, c)
    from_slot = lambda d: pl.BlockSpec(block, lambda i, chip: (((chip[0] + d) % chips) * nb + i, 0))
    grid_spec = pltpu.PrefetchScalarGridSpec(
        num_scalar_prefetch=1, grid=(nb,), in_specs=[from_slot(0), from_slot(1), from_slot(2), from_slot(3)],
        out_specs=ANY,
        scratch_shapes=[pltpu.VMEM((2, SWAP_ROWS, c), f32), pltpu.SemaphoreType.DMA((2,)),
                        pltpu.SemaphoreType.DMA((1,)), pltpu.SemaphoreType.DMA((2,))])
    land_rows = land.reshape(chips * rh, c)
    return pl.pallas_call(body, name=name, grid_spec=grid_spec, out_shape=SDS((2 * rh, c), f32))(
        chip, chip_sums.reshape(chips * rh, c), land_rows, land_rows, land_rows)


def _adamw_update(w_ref, g_ref, m_ref, v_ref, d_ref, nm_ref, nv_ref):
    g = g_ref[...]
    nm = ADAM_B1 * m_ref[...] + (1.0 - ADAM_B1) * g
    nv = ADAM_B2 * v_ref[...] + (1.0 - ADAM_B2) * (g * g)
    nm_ref[...] = nm
    nv_ref[...] = nv
    m_hat = nm / (1.0 - ADAM_B1 ** ADAM_STEP)
    v_hat = nv / (1.0 - ADAM_B2 ** ADAM_STEP)
    d_ref[...] = -ADAM_LR * (m_hat / (jnp.sqrt(v_hat) + ADAM_EPS) + ADAM_WD * w_ref[...])


def _adamw(w, g, m, v, name):
    r, c = w.shape
    tr = 128 if r % 128 == 0 else r

    def body(w_ref, g_ref, m_ref, v_ref, g_out_ref, d_ref, nm_ref, nv_ref):
        g_out_ref[...] = g_ref[...]
        _adamw_update(w_ref, g_ref, m_ref, v_ref, d_ref, nm_ref, nv_ref)

    spec = pl.BlockSpec((tr, c), lambda i: (i, 0))
    return pl.pallas_call(body, name=name, grid=(r // tr,), in_specs=[spec] * 4, out_specs=[spec] * 4,
                          out_shape=[SDS((r, c), f32)] * 4)(w, g, m, v)


SMALL_PACKING = {
    "e_pre_norm": ((1, 2048), 8, (1, 2048)), "e_pool_w": ((1024, 256), 1024, (256, 256)),
    "e_pool_scale": ((1, 1024), 8, (1, 1024)), "e_post_norm": ((1, 2048), 8, (1, 2048)),
    "o_pre_norm": ((1, 2048), 8, (1, 512)), "o_sgu_norm_g": ((1, 1024), 8, (1, 256)),
    "o_sgu_norm_b": ((1, 1024), 8, (1, 256)), "o_sgu_w": ((512, 128), 512, (512, 128)),
    "o_sgu_b": ((128, 128), 8, (4, 128)), "o_conv_w": ((31, 1024), 128, (31, 256)), "o_conv_b": ((1, 1024), 8, (1, 256)),
    "o_conv_norm_g": ((1, 1024), 8, (1, 256)), "o_conv_norm_b": ((1, 1024), 8, (1, 256)),
    "o_post_norm": ((1, 2048), 8, (1, 512)),
}
SMALL_PACKED_ROWS = 1792


def _small_finalize(grads, ws, ms, vs, after):
    names = list(SMALL_ORDER)
    n = len(names)
    half, piece = SMALL_PACKED_ROWS // 2, SMALL_PACKED_ROWS // 8
    first_row, row = {}, 0
    for k in names:
        first_row[k] = row
        row += SMALL_PACKING[k][1]

    def body(*refs):
        g_refs, total = refs[0:n], refs[n + 1]
        pack, from_sibling, from_chips, send_a, recv_a, send_b, recv_b, send_c, recv_c, send_d, recv_d = refs[n + 2:]
        x, y, c, me, others = _place()

        for r0 in range(0, SMALL_PACKED_ROWS, piece):
            pack[r0:r0 + piece, :] = jnp.zeros((piece, LANES), f32)
        for k, g_ref in zip(names, g_refs):
            (rows, width), _, _ = SMALL_PACKING[k]
            r0 = first_row[k]
            if k == "o_sgu_b":
                pack[r0:r0 + 4, 0:CHUNK] = g_ref[...].T[0:4, :]
            elif width < LANES:
                pack[r0:r0 + rows, 0:width] = g_ref[...]
            else:
                for j in range(width // LANES):
                    dst = r0 + j * (1 if rows == 1 else 32)
                    pack[dst:dst + rows, :] = g_ref[:, j * LANES:(j + 1) * LANES]

        sibling = (x, y, 1 - c)
        swap = pltpu.make_async_remote_copy(
            src_ref=pack.at[pl.ds(pl.multiple_of((1 - c) * half, 8), half), :], dst_ref=from_sibling,
            send_sem=send_a.at[0], recv_sem=recv_a.at[0], device_id=sibling, device_id_type=MESH)
        swap.start()
        swap.wait()
        for j in range(4):
            rows = pl.ds(pl.multiple_of(c * half + j * piece, 8), piece)
            pack[rows, :] = pack[rows, :] + from_sibling[j * piece:(j + 1) * piece, :]

        def piece_of(chip):
            return pl.ds(pl.multiple_of(c * half + chip * piece, 8), piece)

        def to_chip(k):
            ox, oy = others[k]
            return pltpu.make_async_remote_copy(
                src_ref=pack.at[piece_of(2 * ox + oy), :], dst_ref=from_chips.at[me], send_sem=send_b.at[k],
                recv_sem=recv_b.at[k], device_id=(ox, oy, c), device_id_type=MESH)

        for k in range(3):
            to_chip(k).start()
        from_chips[me] = pack[piece_of(me), :]
        for k, (ox, oy) in enumerate(others):
            landed = from_chips.at[2 * ox + oy]
            pltpu.make_async_remote_copy(src_ref=landed, dst_ref=landed, send_sem=send_b.at[k], recv_sem=recv_b.at[k],
                                         device_id=(ox, oy, c), device_id_type=MESH).wait_recv()
        for k in range(3):
            to_chip(k).wait_send()
        mine = pl.ds(pl.multiple_of(c * half + me * piece, 8), piece)
        total[mine, :] = ((from_chips[0] + from_chips[1]) + from_chips[2]) + from_chips[3]

        def to_same_core(k):
            ox, oy = others[k]
            return pltpu.make_async_remote_copy(
                src_ref=total.at[mine, :], dst_ref=total.at[mine, :], send_sem=send_c.at[k], recv_sem=recv_c.at[k],
                device_id=(ox, oy, c), device_id_type=MESH)

        for k in range(3):
            to_same_core(k).start()
        for k, (ox, oy) in enumerate(others):
            theirs = total.at[piece_of(2 * ox + oy), :]
            pltpu.make_async_remote_copy(src_ref=theirs, dst_ref=theirs, send_sem=send_c.at[k], recv_sem=recv_c.at[k],
                                         device_id=(ox, oy, c), device_id_type=MESH).wait_recv()
        for k in range(3):
            to_same_core(k).wait_send()
        my_half = total.at[pl.ds(pl.multiple_of(c * half, 8), half), :]
        join = pltpu.make_async_remote_copy(src_ref=my_half, dst_ref=my_half, send_sem=send_d.at[0], recv_sem=recv_d.at[0],
                                            device_id=sibling, device_id_type=MESH)
        join.start()
        their_half = total.at[pl.ds(pl.multiple_of((1 - c) * half, 8), half), :]
        pltpu.make_async_remote_copy(src_ref=their_half, dst_ref=their_half, send_sem=send_d.at[0], recv_sem=recv_d.at[0],
                                     device_id=sibling, device_id_type=MESH).wait_recv()
        join.wait_send()

    whole = pl.BlockSpec(memory_space=pltpu.VMEM)
    total = pl.pallas_call(
        body, name="small_allreduce", in_specs=[whole] * n + [ANY], out_specs=whole,
        out_shape=SDS((SMALL_PACKED_ROWS, LANES), f32),
        scratch_shapes=[pltpu.VMEM((SMALL_PACKED_ROWS, LANES), f32), pltpu.VMEM((half, LANES), f32),
                        pltpu.VMEM((N_CHIPS, piece, LANES), f32),
                        pltpu.SemaphoreType.DMA((1,)), pltpu.SemaphoreType.DMA((1,)), pltpu.SemaphoreType.DMA((3,)),
                        pltpu.SemaphoreType.DMA((3,)), pltpu.SemaphoreType.DMA((3,)), pltpu.SemaphoreType.DMA((3,)),
                        pltpu.SemaphoreType.DMA((1,)), pltpu.SemaphoreType.DMA((1,))],
    )(*grads, after)

    def update(*refs):
        total = refs[0]
        w_refs, m_refs, v_refs = refs[1:n + 1], refs[n + 1:2 * n + 1], refs[2 * n + 1:3 * n + 1]
        outs = refs[3 * n + 1:]
        me = 2 * lax.axis_index("x") + lax.axis_index("y")

        def of_chip(candidates):
            value = candidates[0]
            for j in range(1, N_CHIPS):
                value = jnp.where(me == j, candidates[j], value)
            return value

        for i, k in enumerate(names):
            (rows, width), _, (local_rows, local_width) = SMALL_PACKING[k]
            r0 = first_row[k]
            if k == "o_sgu_b":
                g = total[r0:r0 + 4, 0:CHUNK]
            elif k == "e_pool_w":
                for grp in range(4):
                    src = pl.ds(pl.multiple_of(r0 + grp * POOL_CH + me * 64, 8), 64)
                    dst = slice(grp * 64, (grp + 1) * 64)
                    _adamw_rows(total[src, :], i, dst, w_refs, m_refs, v_refs, outs, n)
                continue
            elif k == "o_conv_w":
                g = total[pl.ds(pl.multiple_of(r0 + me * 32, 8), 32), :][0:CONV_K]
            elif width < LANES:
                g = total[r0:r0 + rows, 0:width]
            else:
                lanes = [total[r0 + j:r0 + j + 1, :] for j in range(width // LANES)]
                per_chip = local_width // LANES
                if local_width == width:
                    g = jnp.concatenate(lanes, axis=1)
                elif per_chip == 1:
                    g = of_chip(lanes)
                else:
                    g = of_chip([jnp.concatenate(lanes[j * per_chip:(j + 1) * per_chip], axis=1) for j in range(N_CHIPS)])
            _adamw_rows(g, i, slice(None), w_refs, m_refs, v_refs, outs, n)

    shard_shapes = [SMALL_PACKING[k][2] for k in names]
    out = pl.pallas_call(update, name="small_update", in_specs=[whole] * (3 * n + 1), out_specs=[whole] * (4 * n),
                         out_shape=[SDS(s, f32) for s in shard_shapes] * 4)(total, *ws, *ms, *vs)
    return out[:n], out[n:2 * n], out[2 * n:3 * n], out[3 * n:]


def _adamw_rows(g, i, rows, w_refs, m_refs, v_refs, outs, n):
    w, m, v = w_refs[i][rows, :], m_refs[i][rows, :], v_refs[i][rows, :]
    nm = ADAM_B1 * m + (1.0 - ADAM_B1) * g
    nv = ADAM_B2 * v + (1.0 - ADAM_B2) * (g * g)
    m_hat = nm / (1.0 - ADAM_B1 ** ADAM_STEP)
    v_hat = nv / (1.0 - ADAM_B2 ** ADAM_STEP)
    outs[i][rows, :] = g
    outs[n + i][rows, :] = -ADAM_LR * (m_hat / (jnp.sqrt(v_hat) + ADAM_EPS) + ADAM_WD * w)
    outs[2 * n + i][rows, :] = nm
    outs[3 * n + i][rows, :] = nv


def _pack(arrays, total_rows=None):
    parts = []
    rows = 0
    for a in arrays:
        flat = a.reshape(-1, LANES)
        pad = -flat.shape[0] % 8
        parts.append(jnp.pad(flat, ((0, pad), (0, 0))))
        rows += flat.shape[0] + pad
    if total_rows is not None:
        parts.append(jnp.zeros((total_rows - rows, LANES), arrays[0].dtype))
    return jnp.concatenate(parts, axis=0)


def _unpack(buf, shapes):
    out = []
    row = 0
    lead = buf.shape[:-2]
    for shape in shapes:
        size = 1
        for s in shape:
            size *= s
        rows = size // LANES
        out.append(buf[..., row:row + rows, :].reshape(lead + tuple(shape)))
        row += rows + (-rows % 8)
    return out


BIG = ("e_w_in", "e_w_out", "o_w_in", "o_w_out")
SHARDED_SMALL = {
    "e_pool_w": ((4, 64, 256), 1), "o_pre_norm": ((512,), 0), "o_sgu_norm_g": ((256,), 0), "o_sgu_norm_b": ((256,), 0),
    "o_conv_w": ((31, 256), 1), "o_conv_b": ((256,), 0), "o_conv_norm_g": ((256,), 0), "o_conv_norm_b": ((256,), 0),
    "o_post_norm": ((512,), 0),
}
SMALL_ORDER = ("e_pre_norm", "e_pool_w", "e_pool_scale", "e_post_norm", "o_pre_norm", "o_sgu_norm_g", "o_sgu_norm_b",
               "o_sgu_w", "o_sgu_b", "o_conv_w", "o_conv_b", "o_conv_norm_g", "o_conv_norm_b", "o_post_norm")
ALL_ORDER = ("e_pre_norm", "e_w_in", "e_pool_w", "e_pool_scale", "e_w_out", "e_post_norm", "o_pre_norm", "o_w_in",
             "o_sgu_norm_g", "o_sgu_norm_b", "o_sgu_w", "o_sgu_b", "o_conv_w", "o_conv_b", "o_conv_norm_g",
             "o_conv_norm_b", "o_w_out", "o_post_norm")


def _full_shape(name):
    shape, axis = SHARDED_SMALL[name]
    return tuple(s * N_CHIPS if i == axis else s for i, s in enumerate(shape))


def _from_chips(name, stacked):
    shape, axis = SHARDED_SMALL[name]
    return jnp.moveaxis(stacked, 0, axis).reshape(_full_shape(name))


def kernel(x, e_pre_norm, e_w_in, e_pool_w, e_pool_scale, e_w_out, e_post_norm, o_pre_norm, o_w_in, o_sgu_norm_g, o_sgu_norm_b, o_sgu_w, o_sgu_b, o_conv_w, o_conv_b, o_conv_norm_g, o_conv_norm_b, o_w_out, o_post_norm, loss_target, m_e_pre_norm, m_e_w_in, m_e_pool_w, m_e_pool_scale, m_e_w_out, m_e_post_norm, m_o_pre_norm, m_o_w_in, m_o_sgu_norm_g, m_o_sgu_norm_b, m_o_sgu_w, m_o_sgu_b, m_o_conv_w, m_o_conv_b, m_o_conv_norm_g, m_o_conv_norm_b, m_o_w_out, m_o_post_norm, v_e_pre_norm, v_e_w_in, v_e_pool_w, v_e_pool_scale, v_e_w_out, v_e_post_norm, v_o_pre_norm, v_o_w_in, v_o_sgu_norm_g, v_o_sgu_norm_b, v_o_sgu_w, v_o_sgu_b, v_o_conv_w, v_o_conv_b, v_o_conv_norm_g, v_o_conv_norm_b, v_o_w_out, v_o_post_norm):
    w = dict(e_pre_norm=e_pre_norm, e_w_in=e_w_in, e_pool_w=e_pool_w, e_pool_scale=e_pool_scale, e_w_out=e_w_out,
             e_post_norm=e_post_norm, o_pre_norm=o_pre_norm, o_w_in=o_w_in, o_sgu_norm_g=o_sgu_norm_g,
             o_sgu_norm_b=o_sgu_norm_b, o_sgu_w=o_sgu_w, o_sgu_b=o_sgu_b, o_conv_w=o_conv_w, o_conv_b=o_conv_b,
             o_conv_norm_g=o_conv_norm_g, o_conv_norm_b=o_conv_norm_b, o_w_out=o_w_out, o_post_norm=o_post_norm)
    m = dict(e_pre_norm=m_e_pre_norm, e_w_in=m_e_w_in, e_pool_w=m_e_pool_w, e_pool_scale=m_e_pool_scale,
             e_w_out=m_e_w_out, e_post_norm=m_e_post_norm, o_pre_norm=m_o_pre_norm, o_w_in=m_o_w_in,
             o_sgu_norm_g=m_o_sgu_norm_g, o_sgu_norm_b=m_o_sgu_norm_b, o_sgu_w=m_o_sgu_w, o_sgu_b=m_o_sgu_b,
             o_conv_w=m_o_conv_w, o_conv_b=m_o_conv_b, o_conv_norm_g=m_o_conv_norm_g, o_conv_norm_b=m_o_conv_norm_b,
             o_w_out=m_o_w_out, o_post_norm=m_o_post_norm)
    v = dict(e_pre_norm=v_e_pre_norm, e_w_in=v_e_w_in, e_pool_w=v_e_pool_w, e_pool_scale=v_e_pool_scale,
             e_w_out=v_e_w_out, e_post_norm=v_e_post_norm, o_pre_norm=v_o_pre_norm, o_w_in=v_o_w_in,
             o_sgu_norm_g=v_o_sgu_norm_g, o_sgu_norm_b=v_o_sgu_norm_b, o_sgu_w=v_o_sgu_w, o_sgu_b=v_o_sgu_b,
             o_conv_w=v_o_conv_w, o_conv_b=v_o_conv_b, o_conv_norm_g=v_o_conv_norm_g, o_conv_norm_b=v_o_conv_norm_b,
             o_w_out=v_o_w_out, o_post_norm=v_o_post_norm)
    w, m, v = ({k: a[0] for k, a in d.items()} for d in (w, m, v))
    chip = 2 * lax.axis_index("x") + lax.axis_index("y")

    loss, grad_x, in_flight, small = _step(x[0], loss_target[0], w, chip)

    grads, delta, new_m, new_v = {}, {}, {}, {}
    after = grad_x
    for k in ("o_w_out", "o_w_in", "e_w_out", "e_w_in"):
        grads[k], delta[k], new_m[k], new_v[k] = _adamw(w[k], _land(in_flight, k, chip, after), m[k], v[k], f"adamw_{k}")
        after = delta[k]

    def rows_of(a):
        return a.reshape(-1, a.shape[-1])

    small_grads = [small[k].reshape(SMALL_PACKING[k][0]) for k in SMALL_ORDER]
    updates = _small_finalize(small_grads, *[[rows_of(d[k]) for k in SMALL_ORDER] for d in (w, m, v)], after)
    for d, arrays in zip((grads, delta, new_m, new_v), updates):
        for k, a in zip(SMALL_ORDER, arrays):
            d[k] = a.reshape(w[k].shape)
    loss = lax.psum(loss[0, 0], ("x", "y", "c"))

    outs = [loss, grad_x[None]]
    for d in (grads, delta, new_m, new_v):
        outs += [d[k][None] for k in ALL_ORDER]
    return tuple(outs)
```

```python
import jax
import jax.numpy as jnp
from jax import lax
from jax.experimental import pallas as pl
from jax.experimental.pallas import tpu as pltpu

f32 = jnp.float32
bf16 = jnp.bfloat16
SDS = jax.ShapeDtypeStruct

SEQ = 2048
D_MODEL = 2048
EPS = 1e-6
NEG = -1e30
HEAD_DIM = 128
ROT_HALF = 16
ROPE_THETA = 500000.0
DILATIONS = (1, 4, 16)
SPAN = 128
N_HEADS = 8
HALF = 1024
POOL_CH = 256
CONV_K = 31
CONV_PAD = 32
CHUNK = 128
N_CHIPS = 4
LANES = 256
E_IN_PIECES = 3
SMALL_SHARD_ROWS = 352
ANY = pl.BlockSpec(memory_space=pl.ANY)
MESH = pl.DeviceIdType.MESH

ADAM_LR = 0.001
ADAM_B1 = 0.9
ADAM_B2 = 0.999
ADAM_EPS = 1e-08
ADAM_WD = 0.01
ADAM_STEP = 10


def _dot(a, b):
    return jnp.dot(a, b, preferred_element_type=f32)


def _dot_nt(a, b):
    return lax.dot_general(a, b, (((1,), (1,)), ((), ())), preferred_element_type=f32)


def _dot_tn(a, b):
    return lax.dot_general(a, b, (((0,), (0,)), ((), ())), preferred_element_type=f32)


def _sigmoid(x):
    return 1.0 / (1.0 + jnp.exp(-x))


def _silu_and_grad(x):
    s = _sigmoid(x)
    return x * s, s * (1.0 + x * (1.0 - s))


def _rms_fwd(x, g):
    r = lax.rsqrt(jnp.mean(x * x, axis=-1, keepdims=True) + EPS)
    return x * r * g


def _rms_bwd(x, g, dout):
    r = lax.rsqrt(jnp.mean(x * x, axis=-1, keepdims=True) + EPS)
    xh = x * r
    dg = jnp.sum(dout * xh, axis=0, keepdims=True)
    dxh = dout * g
    dx = r * (dxh - xh * jnp.mean(dxh * xh, axis=-1, keepdims=True))
    return dx, dg


def _ln_stats(x):
    mu = jnp.mean(x, axis=-1, keepdims=True)
    xc = x - mu
    rstd = lax.rsqrt(jnp.mean(xc * xc, axis=-1, keepdims=True) + EPS)
    return xc * rstd, rstd


def _ln_bwd(xh, rstd, g, dout):
    dg = jnp.sum(dout * xh, axis=0, keepdims=True)
    db = jnp.sum(dout, axis=0, keepdims=True)
    dxh = dout * g
    dx = rstd * (dxh - jnp.mean(dxh, axis=-1, keepdims=True) - xh * jnp.mean(dxh * xh, axis=-1, keepdims=True))
    return dx, dg, db


def _accumulate(ref, value, first):
    @pl.when(first)
    def _():
        ref[...] = value

    @pl.when(jnp.logical_not(first))
    def _():
        ref[...] += value


def _col_tile(ns):
    for t in (1024, 768, 512, 256):
        if ns % t == 0:
            return t
    raise ValueError(ns)


def _mm_nn(a, w, out_dtype, name, piece=0, pieces=1, into=None):
    m, k = a.shape
    j, _, ns = w.shape
    tm, tn = m, _col_tile(ns)
    nb = ns // tn

    def body(a_ref, w_ref, *rest):
        rest[-1][...] = _dot(a_ref[...], w_ref[...]).astype(rest[-1].dtype)

    return pl.pallas_call(
        body, name=name, grid=(j * nb, m // tm),
        in_specs=[pl.BlockSpec((tm, k), lambda n, i: (i, 0)),
                  pl.BlockSpec((None, k, tn), lambda n, i: (n // nb, 0, n % nb))] + ([] if into is None else [ANY]),
        out_specs=pl.BlockSpec((tm, tn), lambda n, i: (i, ((n // nb) * pieces + piece) * nb + n % nb)),
        out_shape=SDS((m, j * ns * pieces), out_dtype),
        input_output_aliases={} if into is None else {2: 0},
    )(a, w, *([] if into is None else [into]))


def _mm_nt(dz, ws, name, after):
    m, _ = dz.shape
    pieces = len(ws)
    j, k, ns = ws[0].shape
    tm, tk = 1024, 1024

    def body(dz_ref, *rest):
        w_refs, o_ref = rest[:pieces], rest[-1]
        total = _dot_nt(dz_ref[:, 0:ns], w_refs[0][...])
        for q in range(1, pieces):
            total = total + _dot_nt(dz_ref[:, q * ns:(q + 1) * ns], w_refs[q][...])
        _accumulate(o_ref, total, pl.program_id(2) == 0)

    return pl.pallas_call(
        body, name=name, grid=(m // tm, k // tk, j),
        in_specs=[pl.BlockSpec((tm, pieces * ns), lambda i, kk, r: (i, r))]
        + [pl.BlockSpec((None, tk, ns), lambda i, kk, r: (r, kk, 0))] * pieces + [ANY],
        out_specs=pl.BlockSpec((tm, tk), lambda i, kk, r: (i, kk)),
        out_shape=SDS((m, k), f32),
    )(dz, *ws, after)


def _mm_tn(a, dz, j, name):
    m, k = a.shape
    ns = dz.shape[1] // j
    tk, tn = 1024, _col_tile(ns)
    nb = ns // tn

    def body(a_ref, dz_ref, o_ref):
        o_ref[...] = _dot_tn(a_ref[...], dz_ref[...]).astype(o_ref.dtype)

    return pl.pallas_call(
        body, name=name, grid=(k // tk, j * nb),
        in_specs=[pl.BlockSpec((m, tk), lambda kk, n: (0, kk)),
                  pl.BlockSpec((m, tn), lambda kk, n: (0, n))],
        out_specs=pl.BlockSpec((None, tk, tn), lambda kk, n: (n // nb, kk, n % nb)),
        out_shape=SDS((j, k, ns), bf16),
    )(a, dz)


ROWS = 256


def _row_spec(width=D_MODEL, col=0):
    return pl.BlockSpec((ROWS, width), lambda i: (i, col))


def _vec_spec(width=D_MODEL):
    return pl.BlockSpec((1, width), lambda i: (0, 0))


def _pre_norm(x, g):
    def body(x_ref, g_ref, h_ref):
        h_ref[...] = _rms_fwd(x_ref[...], g_ref[...]).astype(bf16)

    return pl.pallas_call(
        body, name="pre_norm", grid=(SEQ // ROWS,), in_specs=[_row_spec(), _vec_spec()],
        out_specs=_row_spec(), out_shape=SDS((SEQ, D_MODEL), bf16))(x, g)


def _mid_norm(x, y, g_post, g_pre):
    def body(x_ref, y_ref, gpost_ref, gpre_ref, x1_ref, h1_ref):
        x1 = x_ref[...] + _rms_fwd(y_ref[...], gpost_ref[...])
        x1_ref[...] = x1
        h1_ref[...] = _rms_fwd(x1, gpre_ref[...]).astype(bf16)

    return pl.pallas_call(
        body, name="mid_norm", grid=(SEQ // ROWS,),
        in_specs=[_row_spec(), _row_spec(), _vec_spec(), _vec_spec()],
        out_specs=[_row_spec(), _row_spec()],
        out_shape=[SDS((SEQ, D_MODEL), f32), SDS((SEQ, D_MODEL), bf16)])(x, y, g_post, g_pre)


def _final_norm_loss(x1, y, g_post, target):
    def body(x1_ref, y_ref, g_ref, t_ref, loss_ref, dx2_ref, dy_ref, dg_ref):
        first = pl.program_id(0) == 0
        y = y_ref[...]
        g = g_ref[...]
        err = x1_ref[...] + _rms_fwd(y, g) - t_ref[...]
        sq = jnp.sum(jnp.sum(err * err, axis=1, keepdims=True), axis=0, keepdims=True)
        _accumulate(loss_ref, sq * (0.5 / D_MODEL), first)
        dx2 = err * (1.0 / D_MODEL)
        dx2_ref[...] = dx2
        dy, dg = _rms_bwd(y, g, dx2)
        dy_ref[...] = dy.astype(bf16)
        _accumulate(dg_ref, dg, first)

    return pl.pallas_call(
        body, name="final_norm_loss", grid=(SEQ // ROWS,),
        in_specs=[_row_spec(), _row_spec(), _vec_spec(), _row_spec()],
        out_specs=[pl.BlockSpec((1, 1), lambda i: (0, 0)), _row_spec(), _row_spec(), _vec_spec()],
        out_shape=[SDS((1, 1), f32), SDS((SEQ, D_MODEL), f32), SDS((SEQ, D_MODEL), bf16), SDS((1, D_MODEL), f32)],
    )(x1, y, g_post, target)


def _mid_norm_bwd(dx2, dh1, x1, y0, g_pre, g_post):
    def body(dx2_ref, dh1_ref, x1_ref, y0_ref, gpre_ref, gpost_ref, dx1_ref, dy0_ref, dgpre_ref, dgpost_ref):
        first = pl.program_id(0) == 0
        d_in, dgpre = _rms_bwd(x1_ref[...], gpre_ref[...], dh1_ref[...])
        dx1 = dx2_ref[...] + d_in
        dx1_ref[...] = dx1
        dy0, dgpost = _rms_bwd(y0_ref[...], gpost_ref[...], dx1)
        dy0_ref[...] = dy0.astype(bf16)
        _accumulate(dgpre_ref, dgpre, first)
        _accumulate(dgpost_ref, dgpost, first)

    return pl.pallas_call(
        body, name="mid_norm_bwd", grid=(SEQ // ROWS,),
        in_specs=[_row_spec(), _row_spec(), _row_spec(), _row_spec(), _vec_spec(), _vec_spec()],
        out_specs=[_row_spec(), _row_spec(), _vec_spec(), _vec_spec()],
        out_shape=[SDS((SEQ, D_MODEL), f32), SDS((SEQ, D_MODEL), bf16), SDS((1, D_MODEL), f32), SDS((1, D_MODEL), f32)],
    )(dx2, dh1, x1, y0, g_pre, g_post)


def _pre_norm_bwd(dx1, dh0, x, g):
    def body(dx1_ref, dh0_ref, x_ref, g_ref, dx_ref, dg_ref):
        d_in, dg = _rms_bwd(x_ref[...], g_ref[...], dh0_ref[...])
        dx_ref[...] = dx1_ref[...] + d_in
        _accumulate(dg_ref, dg, pl.program_id(0) == 0)

    return pl.pallas_call(
        body, name="pre_norm_bwd", grid=(SEQ // ROWS,),
        in_specs=[_row_spec(), _row_spec(), _row_spec(), _vec_spec()],
        out_specs=[_row_spec(), _vec_spec()],
        out_shape=[SDS((SEQ, D_MODEL), f32), SDS((1, D_MODEL), f32)])(dx1, dh0, x, g)


def _pool_count(g):
    row = lax.broadcasted_iota(jnp.int32, (SEQ, 1), 0)
    width = jnp.left_shift(2, g)
    return row, width, jnp.minimum(row + 1, width).astype(f32)


def _trailing_sum(x, row, width):
    s = x
    for k in (1, 2, 4, 8):
        shifted = jnp.where(row >= k, pltpu.roll(s, k, 0), 0.0)
        s = jnp.where(width > k, s + shifted, s)
    return s


def _leading_sum(x, row, width):
    s = x
    for k in (1, 2, 4, 8):
        shifted = jnp.where(row < SEQ - k, pltpu.roll(s, SEQ - k, 0), 0.0)
        s = jnp.where(width > k, s + shifted, s)
    return s


def _pool_specs():
    a_in = pl.BlockSpec((SEQ, POOL_CH), lambda g: (0, g))
    a_gate = pl.BlockSpec((SEQ, POOL_CH), lambda g: (0, 4 + g))
    w = pl.BlockSpec((None, POOL_CH, POOL_CH), lambda g: (g, 0, 0))
    scale = pl.BlockSpec((1, POOL_CH), lambda g: (0, g))
    return a_in, a_gate, w, scale


def _pool_fwd(z0, pool_w, pool_scale):
    def body(a_ref, gate_ref, w_ref, scale_ref, ya_ref):
        row, width, count = _pool_count(pl.program_id(0))
        a = a_ref[...]
        pooled = _trailing_sum(a, row, width) / count - a
        mixed = _dot(pooled.astype(bf16), w_ref[...]) * scale_ref[...]
        gate = gate_ref[...]
        ya_ref[...] = (mixed * gate * _sigmoid(gate)).astype(bf16)

    return pl.pallas_call(
        body, name="pool_fwd", grid=(4,), in_specs=list(_pool_specs()),
        out_specs=pl.BlockSpec((SEQ, POOL_CH), lambda g: (0, g)),
        out_shape=SDS((SEQ, 2 * HALF), bf16))(z0, z0, pool_w, pool_scale)


def _pool_bwd(z0, dcat, pool_w, pool_scale):
    def body(a_ref, gate_ref, w_ref, scale_ref, dya_ref, da_ref, dgate_ref, dw_ref, dscale_ref):
        row, width, count = _pool_count(pl.program_id(0))
        a = a_ref[...]
        pooled = (_trailing_sum(a, row, width) / count - a).astype(bf16)
        w = w_ref[...]
        scale = scale_ref[...]
        mixed = _dot(pooled, w)
        silu, dsilu = _silu_and_grad(gate_ref[...])
        dya = dya_ref[...]
        dgate_ref[...] = (dya * mixed * scale * dsilu).astype(bf16)
        dms = dya * silu
        dscale_ref[...] = jnp.sum(dms * mixed, axis=0, keepdims=True)
        dmixed = (dms * scale).astype(bf16)
        dw_ref[...] = _dot_tn(pooled, dmixed)
        dpooled = _dot_nt(dmixed, w)
        da_ref[...] = (_leading_sum(dpooled / count, row, width) - dpooled).astype(bf16)

    a_in, a_gate, w, scale = _pool_specs()
    col = pl.BlockSpec((SEQ, POOL_CH), lambda g: (0, g))
    return pl.pallas_call(
        body, name="pool_bwd", grid=(4,), in_specs=[a_in, a_gate, w, scale, col],
        out_specs=[col, col, w, scale],
        out_shape=[SDS((SEQ, HALF), bf16), SDS((SEQ, HALF), bf16), SDS((4, POOL_CH, POOL_CH), f32), SDS((1, HALF), f32)],
    )(z0, z0, pool_w, pool_scale, dcat)


Q_COL, K_COL, V_COL, BGATE_COL = 16, 40, 64, 88


def _rope_tables():
    pos = jnp.arange(SEQ, dtype=f32)
    inv_freq = jnp.power(ROPE_THETA, -jnp.arange(0, 2 * ROT_HALF, 2, dtype=f32) / (2 * ROT_HALF))
    ang = pos[:, None] * inv_freq[None, :]
    cos, sin = jnp.cos(ang), jnp.sin(ang)
    zeros = jnp.zeros((SEQ, HEAD_DIM - 2 * ROT_HALF), f32)
    cos_t = jnp.concatenate([cos, cos, zeros + 1.0], axis=1)
    sin_t = jnp.concatenate([sin, sin, zeros], axis=1)
    j = jnp.arange(HEAD_DIM)[:, None]
    i = jnp.arange(HEAD_DIM)[None, :]
    rot = jnp.where((i < ROT_HALF) & (j == i + ROT_HALF), -1.0, 0.0) + jnp.where(
        (i >= ROT_HALF) & (i < 2 * ROT_HALF) & (j == i - ROT_HALF), 1.0, 0.0)
    return cos_t, sin_t, rot.astype(bf16), rot.T.astype(bf16)


def _exact_dot(t, m):
    hi = t.astype(bf16)
    lo = (t - hi.astype(f32)).astype(bf16)
    return _dot(hi, m) + _dot(lo, m)


def _rope(t, cos_t, sin_t, rot):
    return t * cos_t + _exact_dot(t, rot) * sin_t


def _rope_transposed(d, cos_t, sin_t, rot_t):
    return d * cos_t + _exact_dot(d * sin_t, rot_t)


ROW_CHUNK = 256


def _chunks(fn):
    def step(i, carry):
        fn(pl.multiple_of(i * ROW_CHUNK, ROW_CHUNK))
        return carry

    lax.fori_loop(0, SEQ // ROW_CHUNK, step, 0, unroll=2)


def _pieces(dilation):
    length = SEQ // dilation
    n = min(length, ROW_CHUNK)
    return [(r, l0, n) for r in range(dilation) for l0 in range(0, length, n)]


def _by_residue(dst_ref, src_ref, dilation, dtype):
    length = SEQ // dilation
    for r, l0, n in _pieces(dilation):
        src = src_ref[l0:l0 + n, :] if dilation == 1 else src_ref[pl.ds(r + dilation * l0, n, stride=dilation), :]
        start = r * length + l0
        dst_ref[start:start + n, :] = src.astype(dtype)


def _by_position(dst_ref, src_ref, dilation):
    length = SEQ // dilation
    for r, l0, n in _pieces(dilation):
        src = src_ref[r * length + l0:r * length + l0 + n, :]
        if dilation == 1:
            dst_ref[l0:l0 + n, :] = src
        else:
            dst_ref[pl.ds(r + dilation * l0, n, stride=dilation), :] = src


def _attn_masks():
    qi = lax.broadcasted_iota(jnp.int32, (SPAN, 2 * SPAN), 0)
    kj = lax.broadcasted_iota(jnp.int32, (SPAN, 2 * SPAN), 1)
    window = ((kj < SPAN) & (kj >= qi)) | ((kj >= SPAN) & (kj - SPAN <= qi))
    own = lax.broadcasted_iota(jnp.int32, (SPAN, SPAN), 1) <= lax.broadcasted_iota(jnp.int32, (SPAN, SPAN), 0)
    return window, own


def _attn_blocks(dilation):
    per_residue = SEQ // dilation // SPAN
    blocks = [(c, c % per_residue != 0) for c in range(SEQ // SPAN)]
    return [blocks[i:i + 4] for i in range(0, len(blocks), 4)]


def _block_keys(c, has_prev):
    return slice((c - 1) * SPAN if has_prev else c * SPAN, (c + 1) * SPAN)


def _head_spec(col):
    return pl.BlockSpec((SEQ, HEAD_DIM), lambda h: (0, col + h))


def _table_spec():
    return pl.BlockSpec((SEQ, HEAD_DIM), lambda h: (0, 0))


def _attn_fwd(z0, tables, mixed):
    scale = HEAD_DIM ** -0.5

    def body(*refs):
        qkv = refs[0:9]
        bg_ref, cos_ref, sin_ref, rot_ref = refs[9:13]
        yb_ref, att_ref, lse_ref = refs[14:17]
        saved = refs[17:26]
        tmp_q, tmp_k, v_ones, o_res, l_res, o_nat, l_nat = refs[26:33]
        window_mask, own_mask = _attn_masks()
        rot = rot_ref[...]

        @pl.when(pl.program_id(0) == 0)
        def _():
            v_ones[:, HEAD_DIM:] = jnp.ones((SEQ, HEAD_DIM), bf16)

        for g, dilation in enumerate(DILATIONS):
            q_ref, k_ref, v_ref = qkv[3 * g:3 * g + 3]
            qd, kd, vd = saved[3 * g:3 * g + 3]

            def rope_rows(start, q_ref=q_ref, k_ref=k_ref):
                r = pl.ds(start, ROW_CHUNK)
                cos_t, sin_t = cos_ref[r, :], sin_ref[r, :]
                tmp_q[r, :] = _rope(q_ref[r, :], cos_t, sin_t, rot) * scale
                tmp_k[r, :] = _rope(k_ref[r, :], cos_t, sin_t, rot)

            _chunks(rope_rows)
            _by_residue(qd, tmp_q, dilation, bf16)
            _by_residue(kd, tmp_k, dilation, bf16)
            _by_residue(vd, v_ref, dilation, bf16)
            for l0 in range(0, SEQ, ROW_CHUNK):
                v_ones[l0:l0 + ROW_CHUNK, 0:HEAD_DIM] = vd[l0:l0 + ROW_CHUNK, :]

            for four in _attn_blocks(dilation):
                scores = [_dot_nt(qd[c * SPAN:(c + 1) * SPAN, :], kd[_block_keys(c, prev), :]) for c, prev in four]
                tops, probs = [], []
                for (c, prev), s in zip(four, scores):
                    s = jnp.where(window_mask if prev else own_mask, s, NEG)
                    tops.append(jnp.max(s, axis=1, keepdims=True))
                    probs.append(jnp.exp(s - tops[-1]).astype(bf16))
                sums = [_dot(p, v_ones[_block_keys(c, prev), :]) for (c, prev), p in zip(four, probs)]
                for (c, prev), m, o in zip(four, tops, sums):
                    den = o[:, HEAD_DIM:]
                    o_res[c * SPAN:(c + 1) * SPAN, :] = o[:, :HEAD_DIM] / den
                    l_res[c * SPAN:(c + 1) * SPAN, :] = m + jnp.log(den)

            if dilation > 1:
                _by_position(o_nat, o_res, dilation)
                _by_position(l_nat, l_res, dilation)
            o_g, l_g = (o_res, l_res) if dilation == 1 else (o_nat, l_nat)

            def merge(start, g=g, o_g=o_g, l_g=l_g):
                r = pl.ds(start, ROW_CHUNK)
                if g == 0:
                    att, total = o_g[r, :], l_g[r, :]
                else:
                    l_old, l_new = lse_ref[r, :], l_g[r, :]
                    top = jnp.maximum(l_old, l_new)
                    total = top + jnp.log(jnp.exp(l_old - top) + jnp.exp(l_new - top))
                    att = att_ref[r, :] * jnp.exp(l_old - total) + o_g[r, :] * jnp.exp(l_new - total)
                att_ref[r, :] = att
                lse_ref[r, :] = total
                if g == len(DILATIONS) - 1:
                    gate = bg_ref[r, :]
                    yb_ref[r, :] = (att * gate * _sigmoid(gate)).astype(bf16)

            _chunks(merge)

    in_specs = []
    for g in range(3):
        in_specs += [_head_spec(Q_COL + 8 * g), _head_spec(K_COL + 8 * g), _head_spec(V_COL + 8 * g)]
    in_specs += [_head_spec(BGATE_COL), _table_spec(), _table_spec(), pl.BlockSpec((HEAD_DIM, HEAD_DIM), lambda h: (0, 0)), ANY]
    out_spec = pl.BlockSpec((SEQ, HEAD_DIM), lambda h: (0, h))
    right_half = pl.BlockSpec((SEQ, HEAD_DIM), lambda h: (0, N_HEADS + h))
    vm = lambda dt: pltpu.VMEM((SEQ, HEAD_DIM), dt)
    cos_t, sin_t, rot, _ = tables
    out = pl.pallas_call(
        body, name="attn_fwd", grid=(N_HEADS,), in_specs=in_specs, out_specs=[right_half] + [out_spec] * 11,
        out_shape=[SDS((SEQ, 2 * HALF), bf16), SDS((SEQ, HALF), f32), SDS((SEQ, HALF), f32)] + [SDS((SEQ, HALF), bf16)] * 9,
        scratch_shapes=[vm(f32), vm(f32), pltpu.VMEM((SEQ, 2 * HEAD_DIM), bf16), vm(f32), vm(f32), vm(f32), vm(f32)],
        input_output_aliases={13: 0},
    )(*([z0] * 10), cos_t, sin_t, rot, mixed)
    return out[0], out[1], out[2], [tuple(out[3 + 3 * g:6 + 3 * g]) for g in range(3)]


def _attn_bwd_group(g, saved, z0, att, lse, dcat, tables):
    scale = HEAD_DIM ** -0.5
    dilation = DILATIONS[g]
    with_gate = g == 0

    def body(*refs):
        qd, kd, vd, bg_ref, att_ref, lse_ref, dyb_ref, cos_ref, sin_ref, rot_t_ref = refs[0:10]
        n_out = 4 if with_gate else 3
        dq_ref, dk_ref, dv_ref = refs[10:13]
        dod, ld, dd, tmp, aq, ak, av = refs[10 + n_out:17 + n_out]
        window_mask, own_mask = _attn_masks()
        rot_t = rot_t_ref[...]

        def gate_rows(start):
            r = pl.ds(start, ROW_CHUNK)
            silu, dsilu = _silu_and_grad(bg_ref[r, :])
            att_v = att_ref[r, :]
            dyb = dyb_ref[r, :]
            if with_gate:
                refs[13][r, :] = (dyb * att_v * dsilu).astype(bf16)
            datt = dyb * silu
            tmp[r, :] = datt
            aq[r, :] = jnp.broadcast_to(jnp.sum(datt * att_v, axis=1, keepdims=True), (ROW_CHUNK, HEAD_DIM))

        _chunks(gate_rows)
        _by_residue(dod, tmp, dilation, bf16)
        _by_residue(dd, aq, dilation, f32)
        _by_residue(ld, lse_ref, dilation, f32)

        for four in _attn_blocks(dilation):
            rows = [slice(c * SPAN, (c + 1) * SPAN) for c, _ in four]
            keys = [_block_keys(c, prev) for c, prev in four]
            scores = [_dot_nt(qd[r, :], kd[k, :]) for r, k in zip(rows, keys)]
            dprobs = [_dot_nt(dod[r, :], vd[k, :]) for r, k in zip(rows, keys)]
            probs, dscores = [], []
            for (c, prev), r, s, dp in zip(four, rows, scores, dprobs):
                lse_q, delta = ld[r, :], dd[r, :]
                if prev:
                    lse_q = jnp.concatenate([lse_q, lse_q], axis=1)
                    delta = jnp.concatenate([delta, delta], axis=1)
                p = jnp.where(window_mask if prev else own_mask, jnp.exp(s - lse_q), 0.0)
                probs.append(p.astype(bf16))
                dscores.append((p * (dp - delta)).astype(bf16))
            dvs = [_dot_tn(p, dod[r, :]) for p, r in zip(probs, rows)]
            dks = [_dot_tn(ds, qd[r, :]) for ds, r in zip(dscores, rows)]
            dqs = [_dot(ds, kd[k, :]) for ds, k in zip(dscores, keys)]
            for (c, prev), r, dv, dk, dq in zip(four, rows, dvs, dks, dqs):
                aq[r, :] = dq
                if prev:
                    before = slice((c - 1) * SPAN, c * SPAN)
                    av[before, :] += dv[0:SPAN]
                    ak[before, :] += dk[0:SPAN]
                    av[r, :] = dv[SPAN:]
                    ak[r, :] = dk[SPAN:]
                else:
                    av[r, :] = dv
                    ak[r, :] = dk

        def finish(out_ref, acc, factor, roped):
            if dilation > 1:
                _by_position(tmp, acc, dilation)
            src = acc if dilation == 1 else tmp

            def rows(start):
                r = pl.ds(start, ROW_CHUNK)
                d = src[r, :]
                if factor != 1.0:
                    d = d * factor
                if roped:
                    d = _rope_transposed(d, cos_ref[r, :], sin_ref[r, :], rot_t)
                out_ref[r, :] = d.astype(bf16)

            _chunks(rows)

        finish(dq_ref, aq, scale, True)
        finish(dk_ref, ak, 1.0, True)
        finish(dv_ref, av, 1.0, False)

    head = pl.BlockSpec((SEQ, HEAD_DIM), lambda h: (0, h))
    in_specs = [head, head, head, _head_spec(BGATE_COL), head, head, _head_spec(8), _table_spec(), _table_spec(),
                pl.BlockSpec((HEAD_DIM, HEAD_DIM), lambda h: (0, 0))]
    n_out = 4 if with_gate else 3
    vm = lambda dt: pltpu.VMEM((SEQ, HEAD_DIM), dt)
    cos_t, sin_t, _, rot_t = tables
    return pl.pallas_call(
        body, name=f"attn_bwd_g{g}", grid=(N_HEADS,), in_specs=in_specs, out_specs=[head] * n_out,
        out_shape=[SDS((SEQ, HALF), bf16)] * n_out,
        scratch_shapes=[vm(bf16), vm(f32), vm(f32), vm(f32), vm(f32), vm(f32), vm(f32)],
    )(*saved, z0, att, lse, dcat, cos_t, sin_t, rot_t)


def _sgu_specs():
    chunk = lambda col: pl.BlockSpec((CHUNK, HALF), lambda n: (n, col))
    vec = pl.BlockSpec((1, HALF), lambda n: (0, 0))
    w = pl.BlockSpec((4, CHUNK, CHUNK), lambda n: (0, 0, 0))
    bias = pl.BlockSpec((CHUNK, CHUNK), lambda n: (0, 0))
    return chunk, vec, w, bias


def _sgu_weights(w_ref):
    tril = lax.broadcasted_iota(jnp.int32, (CHUNK, CHUNK), 1) <= lax.broadcasted_iota(jnp.int32, (CHUNK, CHUNK), 0)
    return tril, [jnp.where(tril, w_ref[h], 0.0).astype(bf16) for h in range(4)]


def _sgu_fwd(z1, ln_g, ln_b, sgu_w, bias_t):
    def body(u_ref, v_ref, cg_ref, g_ref, b_ref, w_ref, bias_ref, yc_ref):
        _, ws = _sgu_weights(w_ref)
        xh, _ = _ln_stats(v_ref[...])
        vn = (xh * g_ref[...] + b_ref[...]).astype(bf16)
        for h in range(4):
            cols = slice(h * POOL_CH, (h + 1) * POOL_CH)
            s = _dot(ws[h], vn[:, cols]) + bias_ref[:, h:h + 1]
            gate = cg_ref[:, cols]
            yc_ref[:, cols] = (u_ref[:, cols] * s * gate * _sigmoid(gate)).astype(bf16)

    chunk, vec, w, bias = _sgu_specs()
    return pl.pallas_call(
        body, name="sgu_fwd", grid=(SEQ // CHUNK,),
        in_specs=[chunk(0), chunk(1), chunk(2), vec, vec, w, bias], out_specs=chunk(0),
        out_shape=SDS((SEQ, 2 * HALF), bf16))(z1, z1, z1, ln_g, ln_b, sgu_w, bias_t)


def _sgu_bwd(z1, dcat, ln_g, ln_b, sgu_w, bias_t):
    def body(u_ref, v_ref, cg_ref, dyc_ref, g_ref, b_ref, w_ref, bias_ref,
             du_ref, dv_ref, dcg_ref, dw_ref, dbias_ref, dg_ref, db_ref, dvn_ref):
        first = pl.program_id(0) == 0
        tril, ws = _sgu_weights(w_ref)
        xh, rstd = _ln_stats(v_ref[...])
        g = g_ref[...]
        vn = (xh * g + b_ref[...]).astype(bf16)

        @pl.when(first)
        def _():
            dbias_ref[...] = jnp.zeros((CHUNK, CHUNK), f32)

        for h in range(4):
            cols = slice(h * POOL_CH, (h + 1) * POOL_CH)
            vn_h = vn[:, cols]
            s = _dot(ws[h], vn_h) + bias_ref[:, h:h + 1]
            silu, dsilu = _silu_and_grad(cg_ref[:, cols])
            dyc = dyc_ref[:, cols]
            u = u_ref[:, cols]
            du_ref[:, cols] = (dyc * s * silu).astype(bf16)
            dcg_ref[:, cols] = (dyc * u * s * dsilu).astype(bf16)
            ds = dyc * u * silu
            dbias_ref[:, h:h + 1] += jnp.sum(ds, axis=1, keepdims=True)
            ds = ds.astype(bf16)
            _accumulate(dw_ref.at[h], jnp.where(tril, _dot_nt(ds, vn_h), 0.0), first)
            dvn_ref[:, cols] = _dot_tn(ws[h], ds)
        dv, dg, db = _ln_bwd(xh, rstd, g, dvn_ref[...])
        dv_ref[...] = dv.astype(bf16)
        _accumulate(dg_ref, dg, first)
        _accumulate(db_ref, db, first)

    chunk, vec, w, bias = _sgu_specs()
    return pl.pallas_call(
        body, name="sgu_bwd", grid=(SEQ // CHUNK,),
        in_specs=[chunk(0), chunk(1), chunk(2), chunk(0), vec, vec, w, bias],
        out_specs=[chunk(0), chunk(0), chunk(0), w, bias, vec, vec],
        out_shape=[SDS((SEQ, HALF), bf16)] * 3 + [SDS((4, CHUNK, CHUNK), f32), SDS((CHUNK, CHUNK), f32),
                                                   SDS((1, HALF), f32), SDS((1, HALF), f32)],
        scratch_shapes=[pltpu.VMEM((CHUNK, HALF), f32)],
    )(z1, z1, z1, dcat, ln_g, ln_b, sgu_w, bias_t)


CONV_TILE = 128
DVAL_COL, DGLU_COL = 12, 16


def _conv_specs():
    val = pl.BlockSpec((SEQ, POOL_CH), lambda j: (0, DVAL_COL + j))
    glu = pl.BlockSpec((SEQ, POOL_CH), lambda j: (0, DGLU_COL + j))
    w = pl.BlockSpec((CONV_K, POOL_CH), lambda j: (0, j))
    col = pl.BlockSpec((SEQ, POOL_CH), lambda j: (0, j))
    vec = pl.BlockSpec((1, POOL_CH), lambda j: (0, j))
    return val, glu, w, col, vec


def _conv_fwd(z1, conv_w, conv_b):
    def body(val_ref, glu_ref, w_ref, b_ref, out_ref, xpad):
        xpad[0:CONV_PAD, :] = jnp.zeros((CONV_PAD, POOL_CH), f32)
        xpad[CONV_PAD:, :] = val_ref[...] * _sigmoid(glu_ref[...])
        w = w_ref[...]
        bias = b_ref[...]

        def tile(i, carry):
            t0 = pl.multiple_of(i * CONV_TILE, CONV_TILE)
            window = xpad[pl.ds(t0, CONV_TILE + CONV_PAD), :]
            acc = jnp.broadcast_to(bias, (CONV_TILE, POOL_CH))
            for k in range(CONV_K):
                shift = CONV_PAD - (CONV_K - 1) + k
                acc = acc + w[k:k + 1, :] * pltpu.roll(window, CONV_TILE + CONV_PAD - shift, 0)[0:CONV_TILE]
            out_ref[pl.ds(t0, CONV_TILE), :] = acc
            return carry

        lax.fori_loop(0, SEQ // CONV_TILE, tile, 0)

    val, glu, w, col, vec = _conv_specs()
    return pl.pallas_call(
        body, name="conv_fwd", grid=(4,), in_specs=[val, glu, w, vec], out_specs=col,
        out_shape=SDS((SEQ, HALF), f32), scratch_shapes=[pltpu.VMEM((SEQ + CONV_PAD, POOL_CH), f32)],
    )(z1, z1, conv_w, conv_b)


def _conv_bwd(z1, dconv, conv_w):
    def body(val_ref, glu_ref, w_ref, dout_ref, dval_ref, dglu_ref, dw_ref, db_ref, xpad, dpad, dx_ref):
        val = val_ref[...]
        sig = _sigmoid(glu_ref[...])
        xpad[0:CONV_PAD, :] = jnp.zeros((CONV_PAD, POOL_CH), f32)
        xpad[CONV_PAD:, :] = val * sig
        dout = dout_ref[...]
        dpad[0:SEQ, :] = dout
        dpad[SEQ:, :] = jnp.zeros((CONV_PAD, POOL_CH), f32)
        db_ref[...] = jnp.sum(dout, axis=0, keepdims=True)
        dw_ref[...] = jnp.zeros((CONV_K, POOL_CH), f32)
        w = w_ref[...]

        def tile(i, carry):
            t0 = pl.multiple_of(i * CONV_TILE, CONV_TILE)
            x_win = xpad[pl.ds(t0, CONV_TILE + CONV_PAD), :]
            d_win = dpad[pl.ds(t0, CONV_TILE + CONV_PAD), :]
            d_own = d_win[0:CONV_TILE]
            acc = jnp.zeros((CONV_TILE, POOL_CH), f32)
            for k in range(CONV_K):
                shift = CONV_PAD - (CONV_K - 1) + k
                x_k = pltpu.roll(x_win, CONV_TILE + CONV_PAD - shift, 0)[0:CONV_TILE]
                dw_ref[k:k + 1, :] += jnp.sum(d_own * x_k, axis=0, keepdims=True)
                back = CONV_K - 1 - k
                d_k = d_own if back == 0 else pltpu.roll(d_win, CONV_TILE + CONV_PAD - back, 0)[0:CONV_TILE]
                acc = acc + w[k:k + 1, :] * d_k
            dx_ref[pl.ds(t0, CONV_TILE), :] = acc
            return carry

        lax.fori_loop(0, SEQ // CONV_TILE, tile, 0)
        dx = dx_ref[...]
        dval_ref[...] = (dx * sig).astype(bf16)
        dglu_ref[...] = (dx * val * sig * (1.0 - sig)).astype(bf16)

    val, glu, w, col, vec = _conv_specs()
    pad = pltpu.VMEM((SEQ + CONV_PAD, POOL_CH), f32)
    return pl.pallas_call(
        body, name="conv_bwd", grid=(4,), in_specs=[val, glu, w, col], out_specs=[col, col, w, vec],
        out_shape=[SDS((SEQ, HALF), bf16), SDS((SEQ, HALF), bf16), SDS((CONV_K, HALF), f32), SDS((1, HALF), f32)],
        scratch_shapes=[pad, pad, pltpu.VMEM((SEQ, POOL_CH), f32)],
    )(z1, z1, conv_w, dconv)


DGATE_COL = 5


def _conv_norm_fwd(conv, z1, g, b, mixed):
    def body(c_ref, gate_ref, g_ref, b_ref, mixed_ref, yd_ref):
        xh, _ = _ln_stats(c_ref[...])
        n = xh * g_ref[...] + b_ref[...]
        gate = gate_ref[...]
        yd_ref[...] = (n * _sigmoid(n) * gate * _sigmoid(gate)).astype(bf16)

    return pl.pallas_call(
        body, name="conv_norm_fwd", grid=(SEQ // ROWS,),
        in_specs=[_row_spec(HALF), _row_spec(HALF, DGATE_COL), _vec_spec(HALF), _vec_spec(HALF), ANY],
        out_specs=_row_spec(HALF, 1), out_shape=SDS((SEQ, 2 * HALF), bf16), input_output_aliases={4: 0},
    )(conv, z1, g, b, mixed)


def _conv_norm_bwd(conv, z1, dcat, g, b):
    def body(c_ref, gate_ref, dyd_ref, g_ref, b_ref, dconv_ref, dgate_ref, dg_ref, db_ref):
        first = pl.program_id(0) == 0
        xh, rstd = _ln_stats(c_ref[...])
        g = g_ref[...]
        n_silu, n_dsilu = _silu_and_grad(xh * g + b_ref[...])
        gate_silu, gate_dsilu = _silu_and_grad(gate_ref[...])
        dyd = dyd_ref[...]
        dgate_ref[...] = (dyd * n_silu * gate_dsilu).astype(bf16)
        dconv, dg, db = _ln_bwd(xh, rstd, g, dyd * gate_silu * n_dsilu)
        dconv_ref[...] = dconv
        _accumulate(dg_ref, dg, first)
        _accumulate(db_ref, db, first)

    return pl.pallas_call(
        body, name="conv_norm_bwd", grid=(SEQ // ROWS,),
        in_specs=[_row_spec(HALF), _row_spec(HALF, DGATE_COL), _row_spec(HALF, 1), _vec_spec(HALF), _vec_spec(HALF)],
        out_specs=[_row_spec(HALF), _row_spec(HALF), _vec_spec(HALF), _vec_spec(HALF)],
        out_shape=[SDS((SEQ, HALF), f32), SDS((SEQ, HALF), bf16), SDS((1, HALF), f32), SDS((1, HALF), f32)],
    )(conv, z1, dcat, g, b)


def _step(x, target, w, chip):
    chip_vec = chip.astype(jnp.int32).reshape(1)
    sharded_names = list(SHARDED_SMALL)
    first = [_cast_into_slot(w["e_w_in"], chip_vec, "cast_e_w_in0", w["e_pre_norm"], 0, E_IN_PIECES)]
    sems, bufs, token = _gather_start(first, "gather_start_first")
    small_shard = _pack([w[k] for k in sharded_names], total_rows=SMALL_SHARD_ROWS) + 0.0 * token[0, 0]
    small_slot = lax.dynamic_update_slice(jnp.zeros((N_CHIPS, SMALL_SHARD_ROWS, LANES), f32), small_shard[None], (chip, 0, 0))
    more = [small_slot]
    more += [_cast_into_slot(w["e_w_in"], chip_vec, f"cast_e_w_in{i}", token, i, E_IN_PIECES) for i in range(1, E_IN_PIECES)]
    more_sems, more_bufs, token = _gather_start(more, "gather_start_pieces")
    rest = [_cast_into_slot(w[k], chip_vec, f"cast_{k}", token) for k in BIG[1:]]
    rest_sems, rest_bufs, token = _gather_start(rest, "gather_start_rest")
    sems, bufs = sems + more_sems + rest_sems, bufs + more_bufs + rest_bufs
    tables = _rope_tables()

    def vec(k):
        return w[k].reshape(1, -1)

    h0 = _pre_norm(x, vec("e_pre_norm") + token[0, 0])
    after, z0, e_w_in = h0, None, []
    for i in range(E_IN_PIECES):
        group = slice(0, 1) if i == 0 else slice(1, 3) if i == 1 else slice(i + 1, i + 2)
        landed = _forward_halves(_gather_wait(bufs[group], sems[group], after, f"gather_wait_{i}"), f"forward_{i}")
        if i == 1:
            small_full = landed[0]
        e_w_in.append(landed[-1])
        z0 = _mm_nn(h0, landed[-1], f32, f"e_in{i}", i, E_IN_PIECES, z0)
        after = z0
    p = {k: _from_chips(k, a) for k, a in zip(sharded_names, _unpack(small_full, [SHARDED_SMALL[k][0] for k in sharded_names]))}
    for k in ("o_pre_norm", "o_sgu_norm_g", "o_sgu_norm_b", "o_conv_b", "o_conv_norm_g", "o_conv_norm_b", "o_post_norm"):
        p[k] = p[k].reshape(1, -1)
    pool_w_bf = p["e_pool_w"].astype(bf16)
    bias_t = jnp.pad(w["o_sgu_b"].T, ((0, 0), (0, CHUNK - 4)))

    cat0, att, lse, qkv_by_residue = _attn_fwd(z0, tables, _pool_fwd(z0, pool_w_bf, vec("e_pool_scale")))

    def arrived(index, after, name):
        one = slice(index, index + 1)
        return _forward_halves(_gather_wait(bufs[one], sems[one], after, f"gather_wait_{name}"), f"forward_{name}")[0]

    e_w_out = arrived(1 + E_IN_PIECES, att, "e_w_out").reshape(1, D_MODEL, D_MODEL)
    y0 = _mm_nn(cat0, e_w_out, f32, "e_out")
    x1, h1 = _mid_norm(x, y0, vec("e_post_norm"), p["o_pre_norm"])
    o_w_in = arrived(2 + E_IN_PIECES, h1, "o_w_in")
    z1 = _mm_nn(h1, o_w_in, f32, "o_in")
    yc = _sgu_fwd(z1, p["o_sgu_norm_g"], p["o_sgu_norm_b"], w["o_sgu_w"], bias_t)
    conv = _conv_fwd(z1, p["o_conv_w"], p["o_conv_b"])
    cat1 = _conv_norm_fwd(conv, z1, p["o_conv_norm_g"], p["o_conv_norm_b"], yc)
    o_w_out = arrived(3 + E_IN_PIECES, cat1, "o_w_out").reshape(1, D_MODEL, D_MODEL)
    y1 = _mm_nn(cat1, o_w_out, f32, "o_out")
    loss, dx2, dy1, g_o_post = _final_norm_loss(x1, y1, p["o_post_norm"], target)

    in_flight = {}

    def send_off(name, grad):
        sem, sums, land, tok = _scatter_start(_swap_add(grad, f"swap_add_{name}"), f"scatter_start_{name}")
        in_flight[name] = (sem, sums, land)
        return tok

    tok = send_off("o_w_out", _mm_tn(cat1, dy1, 1, "o_out_dw").reshape(N_CHIPS, HALF // 2, D_MODEL))
    dcat1 = _mm_nt(dy1, [o_w_out], "o_out_dx", tok)
    du, dv, dcg, g_sgu_w, g_bias_t, g_sgu_g, g_sgu_b = _sgu_bwd(
        z1, dcat1, p["o_sgu_norm_g"] + tok[0, 0], p["o_sgu_norm_b"], w["o_sgu_w"], bias_t)
    dconv, ddgate, g_cn_g, g_cn_b = _conv_norm_bwd(conv, z1, dcat1, p["o_conv_norm_g"], p["o_conv_norm_b"])
    ddval, ddglu, g_conv_w, g_conv_b = _conv_bwd(z1, dconv, p["o_conv_w"])
    dz1 = jnp.concatenate([du, dv, dcg, ddval, ddglu, ddgate], axis=1)
    tok = send_off("o_w_in", _mm_tn(h1, dz1, N_CHIPS, "o_in_dw"))
    dh1 = _mm_nt(dz1, [o_w_in], "o_in_dx", tok)
    dx1, dy0, g_o_pre, g_e_post = _mid_norm_bwd(dx2, dh1, x1, y0, p["o_pre_norm"] + tok[0, 0], vec("e_post_norm"))

    tok = send_off("e_w_out", _mm_tn(cat0, dy0, 1, "e_out_dw").reshape(N_CHIPS, HALF // 2, D_MODEL))
    dcat0 = _mm_nt(dy0, [e_w_out], "e_out_dx", tok)
    da, dagate, g_pool_w, g_pool_scale = _pool_bwd(z0, dcat0, pool_w_bf, vec("e_pool_scale") + tok[0, 0])
    dq0, dk0, dv0, dbgate = _attn_bwd_group(0, qkv_by_residue[0], z0, att, lse, dcat0, tables)
    dq1, dk1, dv1 = _attn_bwd_group(1, qkv_by_residue[1], z0, att, lse, dcat0, tables)
    dq2, dk2, dv2 = _attn_bwd_group(2, qkv_by_residue[2], z0, att, lse, dcat0, tables)
    dz0 = jnp.concatenate([da, dagate, dq0, dq1, dq2, dk0, dk1, dk2, dv0, dv1, dv2, dbgate], axis=1)
    tok = send_off("e_w_in", _mm_tn(h0, dz0, N_CHIPS, "e_in_dw"))
    dh0 = _mm_nt(dz0, e_w_in, "e_in_dx", tok)
    grad_x, g_e_pre = _pre_norm_bwd(dx1, dh0, x, vec("e_pre_norm") + tok[0, 0])

    small = {"e_pre_norm": g_e_pre, "e_pool_w": g_pool_w, "e_pool_scale": g_pool_scale, "e_post_norm": g_e_post,
             "o_pre_norm": g_o_pre, "o_sgu_norm_g": g_sgu_g, "o_sgu_norm_b": g_sgu_b, "o_sgu_w": g_sgu_w,
             "o_sgu_b": g_bias_t, "o_conv_w": g_conv_w, "o_conv_b": g_conv_b,
             "o_conv_norm_g": g_cn_g, "o_conv_norm_b": g_cn_b, "o_post_norm": g_o_post}
    return loss, grad_x, in_flight, small


def _land(in_flight, name, chip, after):
    sems, sums, land = in_flight[name]
    sums, land = _scatter_wait(sems, sums, land, after, f"scatter_wait_{name}")
    return _add_landed_join(sums, land, chip.astype(jnp.int32).reshape(1), f"add_landed_{name}")


def _place():
    x, y, c = lax.axis_index("x"), lax.axis_index("y"), lax.axis_index("c")
    others = [(1 - x, y), (x, 1 - y), (1 - x, 1 - y)]
    return x, y, c, 2 * x + y, others


SWAP_ROWS = 256
FORWARD_STAGE_BYTES = 4 << 20


def _swap_add(g, name):
    chips, r, c = g.shape
    half = r // 2
    rows_per_step = 2 * SWAP_ROWS if half % (2 * SWAP_ROWS) == 0 else SWAP_ROWS
    nb = half // rows_per_step
    steps = chips * nb

    def body(core_ref, mine_ref, theirs_ref, out_ref, landing, send_sems, recv_sems, free_sems):
        i = pl.program_id(0)
        x, y, core, _, _ = _place()
        sibling = (x, y, 1 - core)

        def send(slot):
            return pltpu.make_async_remote_copy(src_ref=theirs_ref, dst_ref=landing.at[slot], send_sem=send_sems.at[slot],
                                                recv_sem=recv_sems.at[slot], device_id=sibling, device_id_type=MESH)

        @pl.when(i < steps)
        def _():
            @pl.when(i >= 2)
            def _():
                pl.semaphore_wait(free_sems.at[i % 2], 1)

            send(i % 2).start()

        @pl.when(i >= 1)
        def _():
            landed = (i - 1) % 2
            send(landed).wait_recv()
            out_ref[...] = (mine_ref[...].astype(f32) + landing[landed].astype(f32)).astype(out_ref.dtype)

            @pl.when(i + 1 < steps)
            def _():
                pl.semaphore_signal(free_sems.at[landed], 1, device_id=sibling, device_id_type=MESH)

        @pl.when(i < steps)
        def _():
            send(i % 2).wait_send()

    def rows_of(b, h):
        return (2 * (b // nb) + h) * nb + b % nb

    block = (rows_per_step, c)
    grid_spec = pltpu.PrefetchScalarGridSpec(
        num_scalar_prefetch=1, grid=(steps + 1,),
        in_specs=[pl.BlockSpec(block, lambda i, core: (rows_of(jnp.maximum(i - 1, 0), core[0]), 0)),
                  pl.BlockSpec(block, lambda i, core: (rows_of(jnp.minimum(i, steps - 1), 1 - core[0]), 0))],
        out_specs=pl.BlockSpec(block, lambda i, core: (jnp.maximum(i - 1, 0), 0)),
        scratch_shapes=[pltpu.VMEM((2, rows_per_step, c), g.dtype), pltpu.SemaphoreType.DMA((2,)),
                        pltpu.SemaphoreType.DMA((2,)), pltpu.SemaphoreType.REGULAR((2,))])
    core = lax.axis_index("c").astype(jnp.int32).reshape(1)
    rows = g.reshape(chips * r, c)
    out = pl.pallas_call(body, name=name, grid_spec=grid_spec, out_shape=SDS((chips * half, c), g.dtype))(core, rows, rows)
    return out.reshape(chips, half, c)


HBM = pl.BlockSpec(memory_space=pltpu.HBM)
SEM = pl.BlockSpec(memory_space=pltpu.SEMAPHORE)
EFFECT = pltpu.SideEffectType.DATAFLOW_SIDE_EFFECTING


def _in_hbm(a):
    return pltpu.with_memory_space_constraint(a, pltpu.HBM)


def _cast_into_slot(w, chip, name, after, piece=0, pieces=1):
    r, c = w.shape
    c = c // pieces
    nb = r // SWAP_ROWS

    def body(chip_ref, w_ref, after_ref, o_ref):
        o_ref[...] = w_ref[...].astype(bf16)

    grid_spec = pltpu.PrefetchScalarGridSpec(
        num_scalar_prefetch=1, grid=(nb,),
        in_specs=[pl.BlockSpec((SWAP_ROWS, c), lambda i, chip: (i, piece)), ANY],
        out_specs=pl.BlockSpec((SWAP_ROWS, c), lambda i, chip: (chip[0] * nb + i, 0)))
    out = pl.pallas_call(body, name=name, grid_spec=grid_spec, out_shape=SDS((N_CHIPS * r, c), bf16))(chip, w, after)
    return out.reshape(N_CHIPS, r, c)


def _gather_start(bufs, name):
    n = len(bufs)

    def body(*refs):
        ins, sems, token = refs[:n], refs[n:3 * n], refs[4 * n]
        x, y, c, me, others = _place()
        for a in range(n):
            rows = ins[a].shape[1] // 2
            mine = ins[a].at[me, pl.ds(c * rows, rows), :]
            for k, (ox, oy) in enumerate(others):
                pltpu.make_async_remote_copy(src_ref=mine, dst_ref=mine, send_sem=sems[2 * a].at[k],
                                             recv_sem=sems[2 * a + 1].at[k], device_id=(ox, oy, c),
                                             device_id_type=MESH).start()
        token[...] = jnp.zeros_like(token)

    out = pl.pallas_call(
        body, name=name, in_specs=[HBM] * n,
        out_shape=(*[pltpu.SemaphoreType.DMA((3,))] * (2 * n), *[pltpu.HBM(b.shape, b.dtype) for b in bufs],
                   SDS((8, 128), f32)),
        out_specs=(*[SEM] * (2 * n), *[HBM] * n, pl.BlockSpec(memory_space=pltpu.VMEM)),
        input_output_aliases={a: 2 * n + a for a in range(n)},
        compiler_params=pltpu.CompilerParams(has_side_effects=EFFECT),
    )(*[_in_hbm(b) for b in bufs])
    return [(out[2 * a], out[2 * a + 1]) for a in range(n)], list(out[2 * n:3 * n]), out[3 * n]


def _gather_wait(bufs, sems, after, name):
    n = len(bufs)

    def body(*refs):
        ins, sem_refs = refs[:n], refs[n:3 * n]
        x, y, c, me, others = _place()
        for a in range(n):
            rows = ins[a].shape[1] // 2
            mine = ins[a].at[me, pl.ds(c * rows, rows), :]
            for k, (ox, oy) in enumerate(others):
                landed = ins[a].at[2 * ox + oy, pl.ds(c * rows, rows), :]
                copy = pltpu.make_async_remote_copy(src_ref=mine, dst_ref=landed, send_sem=sem_refs[2 * a].at[k],
                                                    recv_sem=sem_refs[2 * a + 1].at[k], device_id=(ox, oy, c),
                                                    device_id_type=MESH)
                copy.wait_send()
                copy.wait_recv()

    flat_sems = [s for pair in sems for s in pair]
    out = pl.pallas_call(
        body, name=name, in_specs=[HBM] * n + [SEM] * (2 * n) + [ANY],
        out_shape=tuple(pltpu.HBM(b.shape, b.dtype) for b in bufs), out_specs=tuple([HBM] * n),
        input_output_aliases={a: a for a in range(n)},
        compiler_params=pltpu.CompilerParams(has_side_effects=EFFECT),
    )(*bufs, *flat_sems, after)
    return list(out)


def _forward_halves(bufs, name):
    n = len(bufs)
    blocks = []
    for b in bufs:
        half = b.shape[1] // 2
        whole = half * b.shape[2] * b.dtype.itemsize <= FORWARD_STAGE_BYTES
        blocks.append((half, half if whole or half % SWAP_ROWS else SWAP_ROWS))
    work = [(a, k, b) for a in range(n) for k in range(3) for b in range(blocks[a][0] // blocks[a][1])]

    def body(*refs):
        outs, stages = refs[n:2 * n], refs[2 * n:3 * n]
        load_sems, send_sems, recv_sems = refs[3 * n:]
        x, y, c, me, others = _place()
        sibling = (x, y, 1 - c)

        def rows(item):
            a, k, b = item
            half, tr = blocks[a]
            ox, oy = others[k]
            return outs[a].at[2 * ox + oy, pl.ds(c * half + b * tr, tr), :]

        def load(s, item):
            return pltpu.make_async_copy(rows(item), stages[item[0]].at[s], load_sems.at[s])

        def send(s, item):
            return pltpu.make_async_remote_copy(src_ref=stages[item[0]].at[s], dst_ref=rows(item), send_sem=send_sems.at[s],
                                                recv_sem=recv_sems.at[item[0]], device_id=sibling, device_id_type=MESH)

        load(0, work[0]).start()
        for t, item in enumerate(work):
            s = t % 2
            load(s, item).wait()
            send(s, item).start()
            if t + 1 < len(work):
                if t >= 1:
                    send(1 - s, work[t - 1]).wait_send()
                load(1 - s, work[t + 1]).start()
        if len(work) > 1:
            send(len(work) % 2, work[-2]).wait_send()
        send((len(work) - 1) % 2, work[-1]).wait_send()
        for a in range(n):
            theirs = outs[a].at[pl.ds(0, 3), pl.ds(0, blocks[a][0]), :]
            pltpu.make_async_remote_copy(src_ref=theirs, dst_ref=theirs, send_sem=send_sems.at[0], recv_sem=recv_sems.at[a],
                                         device_id=sibling, device_id_type=MESH).wait_recv()

    out = pl.pallas_call(
        body, name=name, in_specs=[ANY] * n, out_specs=[ANY] * n, out_shape=[SDS(b.shape, b.dtype) for b in bufs],
        input_output_aliases={a: a for a in range(n)},
        scratch_shapes=[pltpu.VMEM((2, blocks[a][1], bufs[a].shape[2]), bufs[a].dtype) for a in range(n)]
        + [pltpu.SemaphoreType.DMA((2,)), pltpu.SemaphoreType.DMA((2,)), pltpu.SemaphoreType.DMA((n,))],
    )(*bufs)
    return list(out)


def _scatter_start(chip_sums, name):
    def body(a_ref, land_ref, send_sems, recv_sems, a_thru, land_thru, token):
        x, y, c, me, others = _place()
        for k, (ox, oy) in enumerate(others):
            pltpu.make_async_remote_copy(src_ref=a_ref.at[2 * ox + oy], dst_ref=land_ref.at[me], send_sem=send_sems.at[k],
                                         recv_sem=recv_sems.at[k], device_id=(ox, oy, c), device_id_type=MESH).start()
        token[...] = jnp.zeros_like(token)

    shape = pltpu.HBM(chip_sums.shape, chip_sums.dtype)
    send, recv, a_thru, land, token = pl.pallas_call(
        body, name=name, in_specs=[HBM, HBM],
        out_shape=(pltpu.SemaphoreType.DMA((3,)), pltpu.SemaphoreType.DMA((3,)), shape, shape, SDS((8, 128), f32)),
        out_specs=(SEM, SEM, HBM, HBM, pl.BlockSpec(memory_space=pltpu.VMEM)), input_output_aliases={0: 2, 1: 3},
        compiler_params=pltpu.CompilerParams(has_side_effects=EFFECT),
    )(_in_hbm(chip_sums), _in_hbm(lax.empty(chip_sums.shape, chip_sums.dtype)))
    return (send, recv), a_thru, land, token


def _scatter_wait(sems, chip_sums, land, after, name):
    def body(a_ref, land_ref, send_sems, recv_sems, after_ref, a_out, land_out):
        x, y, c, me, others = _place()
        for k, (ox, oy) in enumerate(others):
            copy = pltpu.make_async_remote_copy(
                src_ref=a_ref.at[2 * ox + oy], dst_ref=land_ref.at[2 * ox + oy], send_sem=send_sems.at[k],
                recv_sem=recv_sems.at[k], device_id=(ox, oy, c), device_id_type=MESH)
            copy.wait_send()
            copy.wait_recv()

    shape = pltpu.HBM(chip_sums.shape, chip_sums.dtype)
    return pl.pallas_call(
        body, name=name, in_specs=[HBM, HBM, SEM, SEM, ANY], out_shape=(shape, shape), out_specs=(HBM, HBM),
        input_output_aliases={0: 0, 1: 1}, compiler_params=pltpu.CompilerParams(has_side_effects=EFFECT),
    )(chip_sums, land, sems[0], sems[1], after)


def _add_landed_join(chip_sums, land, chip, name):
    chips, rh, c = chip_sums.shape
    nb = rh // SWAP_ROWS

    def body(chip_ref, own_ref, l1_ref, l2_ref, l3_ref, out_hbm, buf, send_sems, recv_sem, local_sems):
        i = pl.program_id(0)
        slot = i % 2
        x, y, core, _, _ = _place()
        sibling = (x, y, 1 - core)

        def copies(s, step):
            rows = pl.ds(pl.multiple_of((core * nb + step) * SWAP_ROWS, SWAP_ROWS), SWAP_ROWS)
            keep = pltpu.make_async_copy(buf.at[s], out_hbm.at[rows, :], local_sems.at[s])
            give = pltpu.make_async_remote_copy(src_ref=buf.at[s], dst_ref=out_hbm.at[rows, :], send_sem=send_sems.at[s],
                                                recv_sem=recv_sem.at[0], device_id=sibling, device_id_type=MESH)
            return keep, give

        def drain(s, step):
            keep, give = copies(s, step)
            keep.wait()
            give.wait_send()

        @pl.when(i >= 2)
        def _():
            drain(slot, i - 2)

        buf[slot] = ((own_ref[...].astype(f32) + l1_ref[...].astype(f32)) + l2_ref[...].astype(f32)) + l3_ref[...].astype(f32)
        keep, give = copies(slot, i)
        keep.start()
        give.start()

        @pl.when(i == nb - 1)
        def _():
            drain(slot, i)
            if nb > 1:
                drain(1 - slot, i - 1)
            theirs = out_hbm.at[pl.ds((1 - core) * rh, rh), :]
            pltpu.make_async_remote_copy(src_ref=theirs, dst_ref=theirs, send_sem=send_sems.at[0], recv_sem=recv_sem.at[0],
                                         device_id=sibling, device_id_type=MESH).wait_recv()

    block = (SWAP_ROWS, c)
    from_slot = lambda d: pl.BlockSpec(block, lambda i, chip: (((chip[0] + d) % chips) * nb + i, 0))
    grid_spec = pltpu.PrefetchScalarGridSpec(
        num_scalar_prefetch=1, grid=(nb,), in_specs=[from_slot(0), from_slot(1), from_slot(2), from_slot(3)],
        out_specs=ANY,
        scratch_shapes=[pltpu.VMEM((2, SWAP_ROWS, c), f32), pltpu.SemaphoreType.DMA((2,)),
                        pltpu.SemaphoreType.DMA((1,)), pltpu.SemaphoreType.DMA((2,))])
    land_rows = land.reshape(chips * rh, c)
    return pl.pallas_call(body, name=name, grid_spec=grid_spec, out_shape=SDS((2 * rh, c), f32))(
        chip, chip_sums.reshape(chips * rh, c), land_rows, land_rows, land_rows)


def _adamw_update(w_ref, g_ref, m_ref, v_ref, d_ref, nm_ref, nv_ref):
    g = g_ref[...]
    nm = ADAM_B1 * m_ref[...] + (1.0 - ADAM_B1) * g
    nv = ADAM_B2 * v_ref[...] + (1.0 - ADAM_B2) * (g * g)
    nm_ref[...] = nm
    nv_ref[...] = nv
    m_hat = nm / (1.0 - ADAM_B1 ** ADAM_STEP)
    v_hat = nv / (1.0 - ADAM_B2 ** ADAM_STEP)
    d_ref[...] = -ADAM_LR * (m_hat / (jnp.sqrt(v_hat) + ADAM_EPS) + ADAM_WD * w_ref[...])


def _adamw(w, g, m, v, name):
    r, c = w.shape
    tr = 128 if r % 128 == 0 else r

    def body(w_ref, g_ref, m_ref, v_ref, g_out_ref, d_ref, nm_ref, nv_ref):
        g_out_ref[...] = g_ref[...]
        _adamw_update(w_ref, g_ref, m_ref, v_ref, d_ref, nm_ref, nv_ref)

    spec = pl.BlockSpec((tr, c), lambda i: (i, 0))
    return pl.pallas_call(body, name=name, grid=(r // tr,), in_specs=[spec] * 4, out_specs=[spec] * 4,
                          out_shape=[SDS((r, c), f32)] * 4)(w, g, m, v)


SMALL_PACKING = {
    "e_pre_norm": ((1, 2048), 8, (1, 2048)), "e_pool_w": ((1024, 256), 1024, (256, 256)),
    "e_pool_scale": ((1, 1024), 8, (1, 1024)), "e_post_norm": ((1, 2048), 8, (1, 2048)),
    "o_pre_norm": ((1, 2048), 8, (1, 512)), "o_sgu_norm_g": ((1, 1024), 8, (1, 256)),
    "o_sgu_norm_b": ((1, 1024), 8, (1, 256)), "o_sgu_w": ((512, 128), 512, (512, 128)),
    "o_sgu_b": ((128, 128), 8, (4, 128)), "o_conv_w": ((31, 1024), 128, (31, 256)), "o_conv_b": ((1, 1024), 8, (1, 256)),
    "o_conv_norm_g": ((1, 1024), 8, (1, 256)), "o_conv_norm_b": ((1, 1024), 8, (1, 256)),
    "o_post_norm": ((1, 2048), 8, (1, 512)),
}
SMALL_PACKED_ROWS = 1792


def _small_finalize(grads, ws, ms, vs, after):
    names = list(SMALL_ORDER)
    n = len(names)
    half, piece = SMALL_PACKED_ROWS // 2, SMALL_PACKED_ROWS // 8
    first_row, row = {}, 0
    for k in names:
        first_row[k] = row
        row += SMALL_PACKING[k][1]

    def body(*refs):
        g_refs, total = refs[0:n], refs[n + 1]
        pack, from_sibling, from_chips, send_a, recv_a, send_b, recv_b, send_c, recv_c, send_d, recv_d = refs[n + 2:]
        x, y, c, me, others = _place()

        for r0 in range(0, SMALL_PACKED_ROWS, piece):
            pack[r0:r0 + piece, :] = jnp.zeros((piece, LANES), f32)
        for k, g_ref in zip(names, g_refs):
            (rows, width), _, _ = SMALL_PACKING[k]
            r0 = first_row[k]
            if k == "o_sgu_b":
                pack[r0:r0 + 4, 0:CHUNK] = g_ref[...].T[0:4, :]
            elif width < LANES:
                pack[r0:r0 + rows, 0:width] = g_ref[...]
            else:
                for j in range(width // LANES):
                    dst = r0 + j * (1 if rows == 1 else 32)
                    pack[dst:dst + rows, :] = g_ref[:, j * LANES:(j + 1) * LANES]

        sibling = (x, y, 1 - c)
        swap = pltpu.make_async_remote_copy(
            src_ref=pack.at[pl.ds(pl.multiple_of((1 - c) * half, 8), half), :], dst_ref=from_sibling,
            send_sem=send_a.at[0], recv_sem=recv_a.at[0], device_id=sibling, device_id_type=MESH)
        swap.start()
        swap.wait()
        for j in range(4):
            rows = pl.ds(pl.multiple_of(c * half + j * piece, 8), piece)
            pack[rows, :] = pack[rows, :] + from_sibling[j * piece:(j + 1) * piece, :]

        def piece_of(chip):
            return pl.ds(pl.multiple_of(c * half + chip * piece, 8), piece)

        def to_chip(k):
            ox, oy = others[k]
            return pltpu.make_async_remote_copy(
                src_ref=pack.at[piece_of(2 * ox + oy), :], dst_ref=from_chips.at[me], send_sem=send_b.at[k],
                recv_sem=recv_b.at[k], device_id=(ox, oy, c), device_id_type=MESH)

        for k in range(3):
            to_chip(k).start()
        from_chips[me] = pack[piece_of(me), :]
        for k, (ox, oy) in enumerate(others):
            landed = from_chips.at[2 * ox + oy]
            pltpu.make_async_remote_copy(src_ref=landed, dst_ref=landed, send_sem=send_b.at[k], recv_sem=recv_b.at[k],
                                         device_id=(ox, oy, c), device_id_type=MESH).wait_recv()
        for k in range(3):
            to_chip(k).wait_send()
        mine = pl.ds(pl.multiple_of(c * half + me * piece, 8), piece)
        total[mine, :] = ((from_chips[0] + from_chips[1]) + from_chips[2]) + from_chips[3]

        def to_same_core(k):
            ox, oy = others[k]
            return pltpu.make_async_remote_copy(
                src_ref=total.at[mine, :], dst_ref=total.at[mine, :], send_sem=send_c.at[k], recv_sem=recv_c.at[k],
                device_id=(ox, oy, c), device_id_type=MESH)

        for k in range(3):
            to_same_core(k).start()
        for k, (ox, oy) in enumerate(others):
            theirs = total.at[piece_of(2 * ox + oy), :]
            pltpu.make_async_remote_copy(src_ref=theirs, dst_ref=theirs, send_sem=send_c.at[k], recv_sem=recv_c.at[k],
                                         device_id=(ox, oy, c), device_id_type=MESH).wait_recv()
        for k in range(3):
            to_same_core(k).wait_send()
        my_half = total.at[pl.ds(pl.multiple_of(c * half, 8), half), :]
        join = pltpu.make_async_remote_copy(src_ref=my_half, dst_ref=my_half, send_sem=send_d.at[0], recv_sem=recv_d.at[0],
                                            device_id=sibling, device_id_type=MESH)
        join.start()
        their_half = total.at[pl.ds(pl.multiple_of((1 - c) * half, 8), half), :]
        pltpu.make_async_remote_copy(src_ref=their_half, dst_ref=their_half, send_sem=send_d.at[0], recv_sem=recv_d.at[0],
                                     device_id=sibling, device_id_type=MESH).wait_recv()
        join.wait_send()

    whole = pl.BlockSpec(memory_space=pltpu.VMEM)
    total = pl.pallas_call(
        body, name="small_allreduce", in_specs=[whole] * n + [ANY], out_specs=whole,
        out_shape=SDS((SMALL_PACKED_ROWS, LANES), f32),
        scratch_shapes=[pltpu.VMEM((SMALL_PACKED_ROWS, LANES), f32), pltpu.VMEM((half, LANES), f32),
                        pltpu.VMEM((N_CHIPS, piece, LANES), f32),
                        pltpu.SemaphoreType.DMA((1,)), pltpu.SemaphoreType.DMA((1,)), pltpu.SemaphoreType.DMA((3,)),
                        pltpu.SemaphoreType.DMA((3,)), pltpu.SemaphoreType.DMA((3,)), pltpu.SemaphoreType.DMA((3,)),
                        pltpu.SemaphoreType.DMA((1,)), pltpu.SemaphoreType.DMA((1,))],
    )(*grads, after)

    def update(*refs):
        total = refs[0]
        w_refs, m_refs, v_refs = refs[1:n + 1], refs[n + 1:2 * n + 1], refs[2 * n + 1:3 * n + 1]
        outs = refs[3 * n + 1:]
        me = 2 * lax.axis_index("x") + lax.axis_index("y")

        def of_chip(candidates):
            value = candidates[0]
            for j in range(1, N_CHIPS):
                value = jnp.where(me == j, candidates[j], value)
            return value

        for i, k in enumerate(names):
            (rows, width), _, (local_rows, local_width) = SMALL_PACKING[k]
            r0 = first_row[k]
            if k == "o_sgu_b":
                g = total[r0:r0 + 4, 0:CHUNK]
            elif k == "e_pool_w":
                for grp in range(4):
                    src = pl.ds(pl.multiple_of(r0 + grp * POOL_CH + me * 64, 8), 64)
                    dst = slice(grp * 64, (grp + 1) * 64)
                    _adamw_rows(total[src, :], i, dst, w_refs, m_refs, v_refs, outs, n)
                continue
            elif k == "o_conv_w":
                g = total[pl.ds(pl.multiple_of(r0 + me * 32, 8), 32), :][0:CONV_K]
            elif width < LANES:
                g = total[r0:r0 + rows, 0:width]
            else:
                lanes = [total[r0 + j:r0 + j + 1, :] for j in range(width // LANES)]
                per_chip = local_width // LANES
                if local_width == width:
                    g = jnp.concatenate(lanes, axis=1)
                elif per_chip == 1:
                    g = of_chip(lanes)
                else:
                    g = of_chip([jnp.concatenate(lanes[j * per_chip:(j + 1) * per_chip], axis=1) for j in range(N_CHIPS)])
            _adamw_rows(g, i, slice(None), w_refs, m_refs, v_refs, outs, n)

    shard_shapes = [SMALL_PACKING[k][2] for k in names]
    out = pl.pallas_call(update, name="small_update", in_specs=[whole] * (3 * n + 1), out_specs=[whole] * (4 * n),
                         out_shape=[SDS(s, f32) for s in shard_shapes] * 4)(total, *ws, *ms, *vs)
    return out[:n], out[n:2 * n], out[2 * n:3 * n], out[3 * n:]


def _adamw_rows(g, i, rows, w_refs, m_refs, v_refs, outs, n):
    w, m, v = w_refs[i][rows, :], m_refs[i][rows, :], v_refs[i][rows, :]
    nm = ADAM_B1 * m + (1.0 - ADAM_B1) * g
    nv = ADAM_B2 * v + (1.0 - ADAM_B2) * (g * g)
    m_hat = nm / (1.0 - ADAM_B1 ** ADAM_STEP)
    v_hat = nv / (1.0 - ADAM_B2 ** ADAM_STEP)
    outs[i][rows, :] = g
    outs[n + i][rows, :] = -ADAM_LR * (m_hat / (jnp.sqrt(v_hat) + ADAM_EPS) + ADAM_WD * w)
    outs[2 * n + i][rows, :] = nm
    outs[3 * n + i][rows, :] = nv


def _pack(arrays, total_rows=None):
    parts = []
    rows = 0
    for a in arrays:
        flat = a.reshape(-1, LANES)
        pad = -flat.shape[0] % 8
        parts.append(jnp.pad(flat, ((0, pad), (0, 0))))
        rows += flat.shape[0] + pad
    if total_rows is not None:
        parts.append(jnp.zeros((total_rows - rows, LANES), arrays[0].dtype))
    return jnp.concatenate(parts, axis=0)


def _unpack(buf, shapes):
    out = []
    row = 0
    lead = buf.shape[:-2]
    for shape in shapes:
        size = 1
        for s in shape:
            size *= s
        rows = size // LANES
        out.append(buf[..., row:row + rows, :].reshape(lead + tuple(shape)))
        row += rows + (-rows % 8)
    return out


BIG = ("e_w_in", "e_w_out", "o_w_in", "o_w_out")
SHARDED_SMALL = {
    "e_pool_w": ((4, 64, 256), 1), "o_pre_norm": ((512,), 0), "o_sgu_norm_g": ((256,), 0), "o_sgu_norm_b": ((256,), 0),
    "o_conv_w": ((31, 256), 1), "o_conv_b": ((256,), 0), "o_conv_norm_g": ((256,), 0), "o_conv_norm_b": ((256,), 0),
    "o_post_norm": ((512,), 0),
}
SMALL_ORDER = ("e_pre_norm", "e_pool_w", "e_pool_scale", "e_post_norm", "o_pre_norm", "o_sgu_norm_g", "o_sgu_norm_b",
               "o_sgu_w", "o_sgu_b", "o_conv_w", "o_conv_b", "o_conv_norm_g", "o_conv_norm_b", "o_post_norm")
ALL_ORDER = ("e_pre_norm", "e_w_in", "e_pool_w", "e_pool_scale", "e_w_out", "e_post_norm", "o_pre_norm", "o_w_in",
             "o_sgu_norm_g", "o_sgu_norm_b", "o_sgu_w", "o_sgu_b", "o_conv_w", "o_conv_b", "o_conv_norm_g",
             "o_conv_norm_b", "o_w_out", "o_post_norm")


def _full_shape(name):
    shape, axis = SHARDED_SMALL[name]
    return tuple(s * N_CHIPS if i == axis else s for i, s in enumerate(shape))


def _from_chips(name, stacked):
    shape, axis = SHARDED_SMALL[name]
    return jnp.moveaxis(stacked, 0, axis).reshape(_full_shape(name))


def kernel(x, e_pre_norm, e_w_in, e_pool_w, e_pool_scale, e_w_out, e_post_norm, o_pre_norm, o_w_in, o_sgu_norm_g, o_sgu_norm_b, o_sgu_w, o_sgu_b, o_conv_w, o_conv_b, o_conv_norm_g, o_conv_norm_b, o_w_out, o_post_norm, loss_target, m_e_pre_norm, m_e_w_in, m_e_pool_w, m_e_pool_scale, m_e_w_out, m_e_post_norm, m_o_pre_norm, m_o_w_in, m_o_sgu_norm_g, m_o_sgu_norm_b, m_o_sgu_w, m_o_sgu_b, m_o_conv_w, m_o_conv_b, m_o_conv_norm_g, m_o_conv_norm_b, m_o_w_out, m_o_post_norm, v_e_pre_norm, v_e_w_in, v_e_pool_w, v_e_pool_scale, v_e_w_out, v_e_post_norm, v_o_pre_norm, v_o_w_in, v_o_sgu_norm_g, v_o_sgu_norm_b, v_o_sgu_w, v_o_sgu_b, v_o_conv_w, v_o_conv_b, v_o_conv_norm_g, v_o_conv_norm_b, v_o_w_out, v_o_post_norm):
    w = dict(e_pre_norm=e_pre_norm, e_w_in=e_w_in, e_pool_w=e_pool_w, e_pool_scale=e_pool_scale, e_w_out=e_w_out,
             e_post_norm=e_post_norm, o_pre_norm=o_pre_norm, o_w_in=o_w_in, o_sgu_norm_g=o_sgu_norm_g,
             o_sgu_norm_b=o_sgu_norm_b, o_sgu_w=o_sgu_w, o_sgu_b=o_sgu_b, o_conv_w=o_conv_w, o_conv_b=o_conv_b,
             o_conv_norm_g=o_conv_norm_g, o_conv_norm_b=o_conv_norm_b, o_w_out=o_w_out, o_post_norm=o_post_norm)
    m = dict(e_pre_norm=m_e_pre_norm, e_w_in=m_e_w_in, e_pool_w=m_e_pool_w, e_pool_scale=m_e_pool_scale,
             e_w_out=m_e_w_out, e_post_norm=m_e_post_norm, o_pre_norm=m_o_pre_norm, o_w_in=m_o_w_in,
             o_sgu_norm_g=m_o_sgu_norm_g, o_sgu_norm_b=m_o_sgu_norm_b, o_sgu_w=m_o_sgu_w, o_sgu_b=m_o_sgu_b,
             o_conv_w=m_o_conv_w, o_conv_b=m_o_conv_b, o_conv_norm_g=m_o_conv_norm_g, o_conv_norm_b=m_o_conv_norm_b,
             o_w_out=m_o_w_out, o_post_norm=m_o_post_norm)
    v = dict(e_pre_norm=v_e_pre_norm, e_w_in=v_e_w_in, e_pool_w=v_e_pool_w, e_pool_scale=v_e_pool_scale,
             e_w_out=v_e_w_out, e_post_norm=v_e_post_norm, o_pre_norm=v_o_pre_norm, o_w_in=v_o_w_in,
             o_sgu_norm_g=v_o_sgu_norm_g, o_sgu_norm_b=v_o_sgu_norm_b, o_sgu_w=v_o_sgu_w, o_sgu_b=v_o_sgu_b,
             o_conv_w=v_o_conv_w, o_conv_b=v_o_conv_b, o_conv_norm_g=v_o_conv_norm_g, o_conv_norm_b=v_o_conv_norm_b,
             o_w_out=v_o_w_out, o_post_norm=v_o_post_norm)
    w, m, v = ({k: a[0] for k, a in d.items()} for d in (w, m, v))
    chip = 2 * lax.axis_index("x") + lax.axis_index("y")

    loss, grad_x, in_flight, small = _step(x[0], loss_target[0], w, chip)

    grads, delta, new_m, new_v = {}, {}, {}, {}

    def rows_of(a):
        return a.reshape(-1, a.shape[-1])

    after = grad_x
    for k in ("o_w_out", "o_w_in", "e_w_out", "small", "e_w_in"):
        if k == "small":
            small_grads = [small[name].reshape(SMALL_PACKING[name][0]) for name in SMALL_ORDER]
            updates = _small_finalize(small_grads, *[[rows_of(d[name]) for name in SMALL_ORDER] for d in (w, m, v)], after)
            for d, arrays in zip((grads, delta, new_m, new_v), updates):
                for name, a in zip(SMALL_ORDER, arrays):
                    d[name] = a.reshape(w[name].shape)
            after = updates[1][0]
            continue
        grads[k], delta[k], new_m[k], new_v[k] = _adamw(w[k], _land(in_flight, k, chip, after), m[k], v[k], f"adamw_{k}")
        after = delta[k]
    loss = lax.psum(loss[0, 0], ("x", "y", "c"))

    outs = [loss, grad_x[None]]
    for d in (grads, delta, new_m, new_v):
        outs += [d[k][None] for k in ALL_ORDER]
    return tuple(outs)
```

```python
import jax
import jax.numpy as jnp
from jax import lax
from jax.experimental import pallas as pl
from jax.experimental.pallas import tpu as pltpu

f32 = jnp.float32
bf16 = jnp.bfloat16
SDS = jax.ShapeDtypeStruct

SEQ = 2048
D_MODEL = 2048
EPS = 1e-6
NEG = -1e30
HEAD_DIM = 128
ROT_HALF = 16
ROPE_THETA = 500000.0
DILATIONS = (1, 4, 16)
SPAN = 128
N_HEADS = 8
HALF = 1024
POOL_CH = 256
CONV_K = 31
CONV_PAD = 32
CHUNK = 128
N_CHIPS = 4
LANES = 256
E_IN_PIECES = 3
SMALL_SHARD_ROWS = 352
ANY = pl.BlockSpec(memory_space=pl.ANY)
MESH = pl.DeviceIdType.MESH

ADAM_LR = 0.001
ADAM_B1 = 0.9
ADAM_B2 = 0.999
ADAM_EPS = 1e-08
ADAM_WD = 0.01
ADAM_STEP = 10


def _dot(a, b):
    return jnp.dot(a, b, preferred_element_type=f32)


def _dot_nt(a, b):
    return lax.dot_general(a, b, (((1,), (1,)), ((), ())), preferred_element_type=f32)


def _dot_tn(a, b):
    return lax.dot_general(a, b, (((0,), (0,)), ((), ())), preferred_element_type=f32)


def _sigmoid(x):
    return 1.0 / (1.0 + jnp.exp(-x))


def _silu_and_grad(x):
    s = _sigmoid(x)
    return x * s, s * (1.0 + x * (1.0 - s))


def _rms_fwd(x, g):
    r = lax.rsqrt(jnp.mean(x * x, axis=-1, keepdims=True) + EPS)
    return x * r * g


def _rms_bwd(x, g, dout):
    r = lax.rsqrt(jnp.mean(x * x, axis=-1, keepdims=True) + EPS)
    xh = x * r
    dg = jnp.sum(dout * xh, axis=0, keepdims=True)
    dxh = dout * g
    dx = r * (dxh - xh * jnp.mean(dxh * xh, axis=-1, keepdims=True))
    return dx, dg


def _ln_stats(x):
    mu = jnp.mean(x, axis=-1, keepdims=True)
    xc = x - mu
    rstd = lax.rsqrt(jnp.mean(xc * xc, axis=-1, keepdims=True) + EPS)
    return xc * rstd, rstd


def _ln_bwd(xh, rstd, g, dout):
    dg = jnp.sum(dout * xh, axis=0, keepdims=True)
    db = jnp.sum(dout, axis=0, keepdims=True)
    dxh = dout * g
    dx = rstd * (dxh - jnp.mean(dxh, axis=-1, keepdims=True) - xh * jnp.mean(dxh * xh, axis=-1, keepdims=True))
    return dx, dg, db


def _accumulate(ref, value, first):
    @pl.when(first)
    def _():
        ref[...] = value

    @pl.when(jnp.logical_not(first))
    def _():
        ref[...] += value


def _write_behind(step, steps, tiles, sems, window):
    slot = step % 2

    def copies(s, at):
        return [pltpu.make_async_copy(tile.at[s], window(t, at), sems.at[2 * t + s]) for t, tile in enumerate(tiles)]

    @pl.when(step >= 2)
    def _():
        for cp in copies(slot, step - 2):
            cp.wait()

    def full():
        for cp in copies(slot, step):
            cp.start()

        @pl.when(step == steps - 1)
        def _():
            for cp in copies(slot, step):
                cp.wait()
            if steps > 1:
                for cp in copies(1 - slot, step - 1):
                    cp.wait()

    return [tile.at[slot] for tile in tiles], full


def _columns(ref, first, width):
    return ref.at[:, pl.ds(pl.multiple_of(first, 128), width)]


def _col_tile(ns):
    for t in (1024, 768, 512, 256):
        if ns % t == 0:
            return t
    raise ValueError(ns)


def _mm_nn(a, w, out_dtype, name, piece=0, pieces=1, into=None):
    m, k = a.shape
    j, _, ns = w.shape
    tm, tn = m, _col_tile(ns)
    nb = ns // tn

    def body(a_ref, w_ref, *rest):
        rest[-1][...] = _dot(a_ref[...], w_ref[...]).astype(rest[-1].dtype)

    return pl.pallas_call(
        body, name=name, grid=(j * nb, m // tm),
        in_specs=[pl.BlockSpec((tm, k), lambda n, i: (i, 0)),
                  pl.BlockSpec((None, k, tn), lambda n, i: (n // nb, 0, n % nb))] + ([] if into is None else [ANY]),
        out_specs=pl.BlockSpec((tm, tn), lambda n, i: (i, ((n // nb) * pieces + piece) * nb + n % nb)),
        out_shape=SDS((m, j * ns * pieces), out_dtype),
        input_output_aliases={} if into is None else {2: 0},
    )(a, w, *([] if into is None else [into]))


def _mm_nt(dz, ws, name, after):
    m, _ = dz.shape
    pieces = len(ws)
    j, k, ns = ws[0].shape
    tm, tk = 1024, 1024

    def body(dz_ref, *rest):
        w_refs, o_ref = rest[:pieces], rest[-1]
        total = _dot_nt(dz_ref[:, 0:ns], w_refs[0][...])
        for q in range(1, pieces):
            total = total + _dot_nt(dz_ref[:, q * ns:(q + 1) * ns], w_refs[q][...])
        _accumulate(o_ref, total, pl.program_id(2) == 0)

    return pl.pallas_call(
        body, name=name, grid=(m // tm, k // tk, j),
        in_specs=[pl.BlockSpec((tm, pieces * ns), lambda i, kk, r: (i, r))]
        + [pl.BlockSpec((None, tk, ns), lambda i, kk, r: (r, kk, 0))] * pieces + [ANY],
        out_specs=pl.BlockSpec((tm, tk), lambda i, kk, r: (i, kk)),
        out_shape=SDS((m, k), f32),
    )(dz, *ws, after)


def _mm_tn(a, dz, j, name):
    m, k = a.shape
    ns = dz.shape[1] // j
    tk, tn = 1024, _col_tile(ns)
    nb = ns // tn

    def body(a_ref, dz_ref, o_ref):
        o_ref[...] = _dot_tn(a_ref[...], dz_ref[...]).astype(o_ref.dtype)

    return pl.pallas_call(
        body, name=name, grid=(k // tk, j * nb),
        in_specs=[pl.BlockSpec((m, tk), lambda kk, n: (0, kk)),
                  pl.BlockSpec((m, tn), lambda kk, n: (0, n))],
        out_specs=pl.BlockSpec((None, tk, tn), lambda kk, n: (n // nb, kk, n % nb)),
        out_shape=SDS((j, k, ns), bf16),
    )(a, dz)


ROWS = 256


def _row_spec(width=D_MODEL, col=0):
    return pl.BlockSpec((ROWS, width), lambda i: (i, col))


def _vec_spec(width=D_MODEL):
    return pl.BlockSpec((1, width), lambda i: (0, 0))


def _pre_norm(x, g):
    def body(x_ref, g_ref, h_ref):
        h_ref[...] = _rms_fwd(x_ref[...], g_ref[...]).astype(bf16)

    return pl.pallas_call(
        body, name="pre_norm", grid=(SEQ // ROWS,), in_specs=[_row_spec(), _vec_spec()],
        out_specs=_row_spec(), out_shape=SDS((SEQ, D_MODEL), bf16))(x, g)


def _mid_norm(x, y, g_post, g_pre):
    def body(x_ref, y_ref, gpost_ref, gpre_ref, x1_ref, h1_ref):
        x1 = x_ref[...] + _rms_fwd(y_ref[...], gpost_ref[...])
        x1_ref[...] = x1
        h1_ref[...] = _rms_fwd(x1, gpre_ref[...]).astype(bf16)

    return pl.pallas_call(
        body, name="mid_norm", grid=(SEQ // ROWS,),
        in_specs=[_row_spec(), _row_spec(), _vec_spec(), _vec_spec()],
        out_specs=[_row_spec(), _row_spec()],
        out_shape=[SDS((SEQ, D_MODEL), f32), SDS((SEQ, D_MODEL), bf16)])(x, y, g_post, g_pre)


def _final_norm_loss(x1, y, g_post, target):
    def body(x1_ref, y_ref, g_ref, t_ref, loss_ref, dx2_ref, dy_ref, dg_ref):
        first = pl.program_id(0) == 0
        y = y_ref[...]
        g = g_ref[...]
        err = x1_ref[...] + _rms_fwd(y, g) - t_ref[...]
        sq = jnp.sum(jnp.sum(err * err, axis=1, keepdims=True), axis=0, keepdims=True)
        _accumulate(loss_ref, sq * (0.5 / D_MODEL), first)
        dx2 = err * (1.0 / D_MODEL)
        dx2_ref[...] = dx2
        dy, dg = _rms_bwd(y, g, dx2)
        dy_ref[...] = dy.astype(bf16)
        _accumulate(dg_ref, dg, first)

    return pl.pallas_call(
        body, name="final_norm_loss", grid=(SEQ // ROWS,),
        in_specs=[_row_spec(), _row_spec(), _vec_spec(), _row_spec()],
        out_specs=[pl.BlockSpec((1, 1), lambda i: (0, 0)), _row_spec(), _row_spec(), _vec_spec()],
        out_shape=[SDS((1, 1), f32), SDS((SEQ, D_MODEL), f32), SDS((SEQ, D_MODEL), bf16), SDS((1, D_MODEL), f32)],
    )(x1, y, g_post, target)


def _mid_norm_bwd(dx2, dh1, x1, y0, g_pre, g_post):
    def body(dx2_ref, dh1_ref, x1_ref, y0_ref, gpre_ref, gpost_ref, dx1_ref, dy0_ref, dgpre_ref, dgpost_ref):
        first = pl.program_id(0) == 0
        d_in, dgpre = _rms_bwd(x1_ref[...], gpre_ref[...], dh1_ref[...])
        dx1 = dx2_ref[...] + d_in
        dx1_ref[...] = dx1
        dy0, dgpost = _rms_bwd(y0_ref[...], gpost_ref[...], dx1)
        dy0_ref[...] = dy0.astype(bf16)
        _accumulate(dgpre_ref, dgpre, first)
        _accumulate(dgpost_ref, dgpost, first)

    return pl.pallas_call(
        body, name="mid_norm_bwd", grid=(SEQ // ROWS,),
        in_specs=[_row_spec(), _row_spec(), _row_spec(), _row_spec(), _vec_spec(), _vec_spec()],
        out_specs=[_row_spec(), _row_spec(), _vec_spec(), _vec_spec()],
        out_shape=[SDS((SEQ, D_MODEL), f32), SDS((SEQ, D_MODEL), bf16), SDS((1, D_MODEL), f32), SDS((1, D_MODEL), f32)],
    )(dx2, dh1, x1, y0, g_pre, g_post)


def _pre_norm_bwd(dx1, dh0, x, g):
    def body(dx1_ref, dh0_ref, x_ref, g_ref, dx_ref, dg_ref):
        d_in, dg = _rms_bwd(x_ref[...], g_ref[...], dh0_ref[...])
        dx_ref[...] = dx1_ref[...] + d_in
        _accumulate(dg_ref, dg, pl.program_id(0) == 0)

    return pl.pallas_call(
        body, name="pre_norm_bwd", grid=(SEQ // ROWS,),
        in_specs=[_row_spec(), _row_spec(), _row_spec(), _vec_spec()],
        out_specs=[_row_spec(), _vec_spec()],
        out_shape=[SDS((SEQ, D_MODEL), f32), SDS((1, D_MODEL), f32)])(dx1, dh0, x, g)


def _pool_count(g):
    row = lax.broadcasted_iota(jnp.int32, (SEQ, 1), 0)
    width = jnp.left_shift(2, g)
    return row, width, jnp.minimum(row + 1, width).astype(f32)


def _trailing_sum(x, row, width):
    s = x
    for k in (1, 2, 4, 8):
        shifted = jnp.where(row >= k, pltpu.roll(s, k, 0), 0.0)
        s = jnp.where(width > k, s + shifted, s)
    return s


def _leading_sum(x, row, width):
    s = x
    for k in (1, 2, 4, 8):
        shifted = jnp.where(row < SEQ - k, pltpu.roll(s, SEQ - k, 0), 0.0)
        s = jnp.where(width > k, s + shifted, s)
    return s


def _pool_specs():
    a_in = pl.BlockSpec((SEQ, POOL_CH), lambda g: (0, g))
    a_gate = pl.BlockSpec((SEQ, POOL_CH), lambda g: (0, 4 + g))
    w = pl.BlockSpec((None, POOL_CH, POOL_CH), lambda g: (g, 0, 0))
    scale = pl.BlockSpec((1, POOL_CH), lambda g: (0, g))
    return a_in, a_gate, w, scale


def _pool_fwd(z0, pool_w, pool_scale):
    def body(a_ref, gate_ref, w_ref, scale_ref, ya_ref):
        row, width, count = _pool_count(pl.program_id(0))
        a = a_ref[...]
        pooled = _trailing_sum(a, row, width) / count - a
        mixed = _dot(pooled.astype(bf16), w_ref[...]) * scale_ref[...]
        gate = gate_ref[...]
        ya_ref[...] = (mixed * gate * _sigmoid(gate)).astype(bf16)

    return pl.pallas_call(
        body, name="pool_fwd", grid=(4,), in_specs=list(_pool_specs()),
        out_specs=pl.BlockSpec((SEQ, POOL_CH), lambda g: (0, g)),
        out_shape=SDS((SEQ, 2 * HALF), bf16))(z0, z0, pool_w, pool_scale)


def _pool_bwd(z0, dcat, pool_w, pool_scale):
    def body(a_ref, gate_ref, w_ref, scale_ref, dya_ref, dz_ref, dw_ref, dscale_ref, da_tiles, dgate_tiles, sems):
        g = pl.program_id(0)
        (da_ref, dgate_ref), full = _write_behind(
            g, 4, [da_tiles, dgate_tiles], sems, lambda t, at: _columns(dz_ref, t * HALF + at * POOL_CH, POOL_CH))
        row, width, count = _pool_count(g)
        a = a_ref[...]
        pooled = (_trailing_sum(a, row, width) / count - a).astype(bf16)
        w = w_ref[...]
        scale = scale_ref[...]
        mixed = _dot(pooled, w)
        silu, dsilu = _silu_and_grad(gate_ref[...])
        dya = dya_ref[...]
        dgate_ref[...] = (dya * mixed * scale * dsilu).astype(bf16)
        dms = dya * silu
        dscale_ref[...] = jnp.sum(dms * mixed, axis=0, keepdims=True)
        dmixed = (dms * scale).astype(bf16)
        dw_ref[...] = _dot_tn(pooled, dmixed)
        dpooled = _dot_nt(dmixed, w)
        da_ref[...] = (_leading_sum(dpooled / count, row, width) - dpooled).astype(bf16)
        full()

    a_in, a_gate, w, scale = _pool_specs()
    col = pl.BlockSpec((SEQ, POOL_CH), lambda g: (0, g))
    tiles = pltpu.VMEM((2, SEQ, POOL_CH), bf16)
    return pl.pallas_call(
        body, name="pool_bwd", grid=(4,), in_specs=[a_in, a_gate, w, scale, col],
        out_specs=[ANY, w, scale],
        out_shape=[SDS((SEQ, 6 * D_MODEL), bf16), SDS((4, POOL_CH, POOL_CH), f32), SDS((1, HALF), f32)],
        scratch_shapes=[tiles, tiles, pltpu.SemaphoreType.DMA((4,))],
    )(z0, z0, pool_w, pool_scale, dcat)


Q_COL, K_COL, V_COL, BGATE_COL = 16, 40, 64, 88


def _rope_tables():
    pos = jnp.arange(SEQ, dtype=f32)
    inv_freq = jnp.power(ROPE_THETA, -jnp.arange(0, 2 * ROT_HALF, 2, dtype=f32) / (2 * ROT_HALF))
    ang = pos[:, None] * inv_freq[None, :]
    cos, sin = jnp.cos(ang), jnp.sin(ang)
    zeros = jnp.zeros((SEQ, HEAD_DIM - 2 * ROT_HALF), f32)
    cos_t = jnp.concatenate([cos, cos, zeros + 1.0], axis=1)
    sin_t = jnp.concatenate([sin, sin, zeros], axis=1)
    j = jnp.arange(HEAD_DIM)[:, None]
    i = jnp.arange(HEAD_DIM)[None, :]
    rot = jnp.where((i < ROT_HALF) & (j == i + ROT_HALF), -1.0, 0.0) + jnp.where(
        (i >= ROT_HALF) & (i < 2 * ROT_HALF) & (j == i - ROT_HALF), 1.0, 0.0)
    return cos_t, sin_t, rot.astype(bf16), rot.T.astype(bf16)


def _exact_dot(t, m):
    hi = t.astype(bf16)
    lo = (t - hi.astype(f32)).astype(bf16)
    return _dot(hi, m) + _dot(lo, m)


def _rope(t, cos_t, sin_t, rot):
    return t * cos_t + _exact_dot(t, rot) * sin_t


def _rope_transposed(d, cos_t, sin_t, rot_t):
    return d * cos_t + _exact_dot(d * sin_t, rot_t)


ROW_CHUNK = 256


def _chunks(fn):
    def step(i, carry):
        fn(pl.multiple_of(i * ROW_CHUNK, ROW_CHUNK))
        return carry

    lax.fori_loop(0, SEQ // ROW_CHUNK, step, 0, unroll=2)


def _pieces(dilation):
    length = SEQ // dilation
    n = min(length, ROW_CHUNK)
    return [(r, l0, n) for r in range(dilation) for l0 in range(0, length, n)]


def _by_residue(dst_ref, src_ref, dilation, dtype):
    length = SEQ // dilation
    for r, l0, n in _pieces(dilation):
        src = src_ref[l0:l0 + n, :] if dilation == 1 else src_ref[pl.ds(r + dilation * l0, n, stride=dilation), :]
        start = r * length + l0
        dst_ref[start:start + n, :] = src.astype(dtype)


def _by_position(dst_ref, src_ref, dilation):
    length = SEQ // dilation
    for r, l0, n in _pieces(dilation):
        src = src_ref[r * length + l0:r * length + l0 + n, :]
        if dilation == 1:
            dst_ref[l0:l0 + n, :] = src
        else:
            dst_ref[pl.ds(r + dilation * l0, n, stride=dilation), :] = src


def _attn_masks():
    qi = lax.broadcasted_iota(jnp.int32, (SPAN, 2 * SPAN), 0)
    kj = lax.broadcasted_iota(jnp.int32, (SPAN, 2 * SPAN), 1)
    window = ((kj < SPAN) & (kj >= qi)) | ((kj >= SPAN) & (kj - SPAN <= qi))
    own = lax.broadcasted_iota(jnp.int32, (SPAN, SPAN), 1) <= lax.broadcasted_iota(jnp.int32, (SPAN, SPAN), 0)
    return window, own


def _attn_blocks(dilation):
    per_residue = SEQ // dilation // SPAN
    blocks = [(c, c % per_residue != 0) for c in range(SEQ // SPAN)]
    return [blocks[i:i + 4] for i in range(0, len(blocks), 4)]


def _block_keys(c, has_prev):
    return slice((c - 1) * SPAN if has_prev else c * SPAN, (c + 1) * SPAN)


def _head_spec(col):
    return pl.BlockSpec((SEQ, HEAD_DIM), lambda h: (0, col + h))


def _table_spec():
    return pl.BlockSpec((SEQ, HEAD_DIM), lambda h: (0, 0))


def _attn_fwd(z0, tables, mixed):
    scale = HEAD_DIM ** -0.5

    def body(*refs):
        qkv = refs[0:9]
        bg_ref, cos_ref, sin_ref, rot_ref = refs[9:13]
        yb_ref, att_ref, lse_ref = refs[14:17]
        saved = refs[17:26]
        tmp_q, tmp_k, v_ones, o_res, l_res, o_nat, l_nat = refs[26:33]
        window_mask, own_mask = _attn_masks()
        rot = rot_ref[...]

        @pl.when(pl.program_id(0) == 0)
        def _():
            v_ones[:, HEAD_DIM:] = jnp.ones((SEQ, HEAD_DIM), bf16)

        for g, dilation in enumerate(DILATIONS):
            q_ref, k_ref, v_ref = qkv[3 * g:3 * g + 3]
            qd, kd, vd = saved[3 * g:3 * g + 3]

            def rope_rows(start, q_ref=q_ref, k_ref=k_ref):
                r = pl.ds(start, ROW_CHUNK)
                cos_t, sin_t = cos_ref[r, :], sin_ref[r, :]
                tmp_q[r, :] = _rope(q_ref[r, :], cos_t, sin_t, rot) * scale
                tmp_k[r, :] = _rope(k_ref[r, :], cos_t, sin_t, rot)

            _chunks(rope_rows)
            _by_residue(qd, tmp_q, dilation, bf16)
            _by_residue(kd, tmp_k, dilation, bf16)
            _by_residue(vd, v_ref, dilation, bf16)
            for l0 in range(0, SEQ, ROW_CHUNK):
                v_ones[l0:l0 + ROW_CHUNK, 0:HEAD_DIM] = vd[l0:l0 + ROW_CHUNK, :]

            for four in _attn_blocks(dilation):
                scores = [_dot_nt(qd[c * SPAN:(c + 1) * SPAN, :], kd[_block_keys(c, prev), :]) for c, prev in four]
                tops, probs = [], []
                for (c, prev), s in zip(four, scores):
                    s = jnp.where(window_mask if prev else own_mask, s, NEG)
                    tops.append(jnp.max(s, axis=1, keepdims=True))
                    probs.append(jnp.exp(s - tops[-1]).astype(bf16))
                sums = [_dot(p, v_ones[_block_keys(c, prev), :]) for (c, prev), p in zip(four, probs)]
                for (c, prev), m, o in zip(four, tops, sums):
                    den = o[:, HEAD_DIM:]
                    o_res[c * SPAN:(c + 1) * SPAN, :] = o[:, :HEAD_DIM] / den
                    l_res[c * SPAN:(c + 1) * SPAN, :] = m + jnp.log(den)

            if dilation > 1:
                _by_position(o_nat, o_res, dilation)
                _by_position(l_nat, l_res, dilation)
            o_g, l_g = (o_res, l_res) if dilation == 1 else (o_nat, l_nat)

            def merge(start, g=g, o_g=o_g, l_g=l_g):
                r = pl.ds(start, ROW_CHUNK)
                if g == 0:
                    att, total = o_g[r, :], l_g[r, :]
                else:
                    l_old, l_new = lse_ref[r, :], l_g[r, :]
                    top = jnp.maximum(l_old, l_new)
                    total = top + jnp.log(jnp.exp(l_old - top) + jnp.exp(l_new - top))
                    att = att_ref[r, :] * jnp.exp(l_old - total) + o_g[r, :] * jnp.exp(l_new - total)
                att_ref[r, :] = att
                lse_ref[r, :] = total
                if g == len(DILATIONS) - 1:
                    gate = bg_ref[r, :]
                    yb_ref[r, :] = (att * gate * _sigmoid(gate)).astype(bf16)

            _chunks(merge)

    in_specs = []
    for g in range(3):
        in_specs += [_head_spec(Q_COL + 8 * g), _head_spec(K_COL + 8 * g), _head_spec(V_COL + 8 * g)]
    in_specs += [_head_spec(BGATE_COL), _table_spec(), _table_spec(), pl.BlockSpec((HEAD_DIM, HEAD_DIM), lambda h: (0, 0)), ANY]
    out_spec = pl.BlockSpec((SEQ, HEAD_DIM), lambda h: (0, h))
    right_half = pl.BlockSpec((SEQ, HEAD_DIM), lambda h: (0, N_HEADS + h))
    vm = lambda dt: pltpu.VMEM((SEQ, HEAD_DIM), dt)
    cos_t, sin_t, rot, _ = tables
    out = pl.pallas_call(
        body, name="attn_fwd", grid=(N_HEADS,), in_specs=in_specs, out_specs=[right_half] + [out_spec] * 11,
        out_shape=[SDS((SEQ, 2 * HALF), bf16), SDS((SEQ, HALF), f32), SDS((SEQ, HALF), f32)] + [SDS((SEQ, HALF), bf16)] * 9,
        scratch_shapes=[vm(f32), vm(f32), pltpu.VMEM((SEQ, 2 * HEAD_DIM), bf16), vm(f32), vm(f32), vm(f32), vm(f32)],
        input_output_aliases={13: 0},
    )(*([z0] * 10), cos_t, sin_t, rot, mixed)
    return out[0], out[1], out[2], [tuple(out[3 + 3 * g:6 + 3 * g]) for g in range(3)]


def _attn_bwd_group(g, saved, z0, att, lse, dcat, tables, dz):
    scale = HEAD_DIM ** -0.5
    dilation = DILATIONS[g]
    with_gate = g == 0
    n_out = 4 if with_gate else 3
    first_col = (Q_COL + 8 * g, K_COL + 8 * g, V_COL + 8 * g, BGATE_COL)

    def body(*refs):
        qd, kd, vd, bg_ref, att_ref, lse_ref, dyb_ref, cos_ref, sin_ref, rot_t_ref = refs[0:10]
        dz_ref = refs[11]
        dod, ld, dd, tmp, aq, ak, av = refs[12:19]
        views, full = _write_behind(pl.program_id(0), N_HEADS, refs[19:19 + n_out], refs[19 + n_out],
                                    lambda t, at: _columns(dz_ref, (first_col[t] + at) * HEAD_DIM, HEAD_DIM))
        dq_ref, dk_ref, dv_ref = views[0:3]
        window_mask, own_mask = _attn_masks()
        rot_t = rot_t_ref[...]

        def gate_rows(start):
            r = pl.ds(start, ROW_CHUNK)
            silu, dsilu = _silu_and_grad(bg_ref[r, :])
            att_v = att_ref[r, :]
            dyb = dyb_ref[r, :]
            if with_gate:
                views[3][r, :] = (dyb * att_v * dsilu).astype(bf16)
            datt = dyb * silu
            tmp[r, :] = datt
            aq[r, :] = jnp.broadcast_to(jnp.sum(datt * att_v, axis=1, keepdims=True), (ROW_CHUNK, HEAD_DIM))

        _chunks(gate_rows)
        _by_residue(dod, tmp, dilation, bf16)
        _by_residue(dd, aq, dilation, f32)
        _by_residue(ld, lse_ref, dilation, f32)

        for four in _attn_blocks(dilation):
            rows = [slice(c * SPAN, (c + 1) * SPAN) for c, _ in four]
            keys = [_block_keys(c, prev) for c, prev in four]
            scores = [_dot_nt(qd[r, :], kd[k, :]) for r, k in zip(rows, keys)]
            dprobs = [_dot_nt(dod[r, :], vd[k, :]) for r, k in zip(rows, keys)]
            probs, dscores = [], []
            for (c, prev), r, s, dp in zip(four, rows, scores, dprobs):
                lse_q, delta = ld[r, :], dd[r, :]
                if prev:
                    lse_q = jnp.concatenate([lse_q, lse_q], axis=1)
                    delta = jnp.concatenate([delta, delta], axis=1)
                p = jnp.where(window_mask if prev else own_mask, jnp.exp(s - lse_q), 0.0)
                probs.append(p.astype(bf16))
                dscores.append((p * (dp - delta)).astype(bf16))
            dvs = [_dot_tn(p, dod[r, :]) for p, r in zip(probs, rows)]
            dks = [_dot_tn(ds, qd[r, :]) for ds, r in zip(dscores, rows)]
            dqs = [_dot(ds, kd[k, :]) for ds, k in zip(dscores, keys)]
            for (c, prev), r, dv, dk, dq in zip(four, rows, dvs, dks, dqs):
                aq[r, :] = dq
                if prev:
                    before = slice((c - 1) * SPAN, c * SPAN)
                    av[before, :] += dv[0:SPAN]
                    ak[before, :] += dk[0:SPAN]
                    av[r, :] = dv[SPAN:]
                    ak[r, :] = dk[SPAN:]
                else:
                    av[r, :] = dv
                    ak[r, :] = dk

        def finish(out_ref, acc, factor, roped):
            if dilation > 1:
                _by_position(tmp, acc, dilation)
            src = acc if dilation == 1 else tmp

            def rows(start):
                r = pl.ds(start, ROW_CHUNK)
                d = src[r, :]
                if factor != 1.0:
                    d = d * factor
                if roped:
                    d = _rope_transposed(d, cos_ref[r, :], sin_ref[r, :], rot_t)
                out_ref[r, :] = d.astype(bf16)

            _chunks(rows)

        finish(dq_ref, aq, scale, True)
        finish(dk_ref, ak, 1.0, True)
        finish(dv_ref, av, 1.0, False)
        full()

    head = pl.BlockSpec((SEQ, HEAD_DIM), lambda h: (0, h))
    in_specs = [head, head, head, _head_spec(BGATE_COL), head, head, _head_spec(8), _table_spec(), _table_spec(),
                pl.BlockSpec((HEAD_DIM, HEAD_DIM), lambda h: (0, 0)), ANY]
    vm = lambda dt: pltpu.VMEM((SEQ, HEAD_DIM), dt)
    cos_t, sin_t, _, rot_t = tables
    return pl.pallas_call(
        body, name=f"attn_bwd_g{g}", grid=(N_HEADS,), in_specs=in_specs, out_specs=ANY,
        out_shape=SDS(dz.shape, dz.dtype), input_output_aliases={10: 0},
        scratch_shapes=[vm(bf16), vm(f32), vm(f32), vm(f32), vm(f32), vm(f32), vm(f32)]
        + [pltpu.VMEM((2, SEQ, HEAD_DIM), bf16)] * n_out + [pltpu.SemaphoreType.DMA((2 * n_out,))],
    )(*saved, z0, att, lse, dcat, cos_t, sin_t, rot_t, dz)


def _sgu_specs():
    chunk = lambda col: pl.BlockSpec((CHUNK, HALF), lambda n: (n, col))
    vec = pl.BlockSpec((1, HALF), lambda n: (0, 0))
    w = pl.BlockSpec((4, CHUNK, CHUNK), lambda n: (0, 0, 0))
    bias = pl.BlockSpec((CHUNK, CHUNK), lambda n: (0, 0))
    return chunk, vec, w, bias


def _sgu_weights(w_ref):
    tril = lax.broadcasted_iota(jnp.int32, (CHUNK, CHUNK), 1) <= lax.broadcasted_iota(jnp.int32, (CHUNK, CHUNK), 0)
    return tril, [jnp.where(tril, w_ref[h], 0.0).astype(bf16) for h in range(4)]


def _sgu_fwd(z1, ln_g, ln_b, sgu_w, bias_t):
    def body(u_ref, v_ref, cg_ref, g_ref, b_ref, w_ref, bias_ref, yc_ref):
        _, ws = _sgu_weights(w_ref)
        xh, _ = _ln_stats(v_ref[...])
        vn = (xh * g_ref[...] + b_ref[...]).astype(bf16)
        for h in range(4):
            cols = slice(h * POOL_CH, (h + 1) * POOL_CH)
            s = _dot(ws[h], vn[:, cols]) + bias_ref[:, h:h + 1]
            gate = cg_ref[:, cols]
            yc_ref[:, cols] = (u_ref[:, cols] * s * gate * _sigmoid(gate)).astype(bf16)

    chunk, vec, w, bias = _sgu_specs()
    return pl.pallas_call(
        body, name="sgu_fwd", grid=(SEQ // CHUNK,),
        in_specs=[chunk(0), chunk(1), chunk(2), vec, vec, w, bias], out_specs=chunk(0),
        out_shape=SDS((SEQ, 2 * HALF), bf16))(z1, z1, z1, ln_g, ln_b, sgu_w, bias_t)


def _sgu_bwd(z1, dcat, ln_g, ln_b, sgu_w, bias_t):
    def body(u_ref, v_ref, cg_ref, dyc_ref, g_ref, b_ref, w_ref, bias_ref,
             dz_ref, dw_ref, dbias_ref, dg_ref, db_ref, dvn_ref, du_tiles, dv_tiles, dcg_tiles, sems):
        n = pl.program_id(0)
        (du_ref, dv_ref, dcg_ref), full = _write_behind(
            n, SEQ // CHUNK, [du_tiles, dv_tiles, dcg_tiles], sems,
            lambda t, at: dz_ref.at[pl.ds(pl.multiple_of(at * CHUNK, CHUNK), CHUNK), t * HALF:(t + 1) * HALF])
        first = n == 0
        tril, ws = _sgu_weights(w_ref)
        xh, rstd = _ln_stats(v_ref[...])
        g = g_ref[...]
        vn = (xh * g + b_ref[...]).astype(bf16)

        @pl.when(first)
        def _():
            dbias_ref[...] = jnp.zeros((CHUNK, CHUNK), f32)

        for h in range(4):
            cols = slice(h * POOL_CH, (h + 1) * POOL_CH)
            vn_h = vn[:, cols]
            s = _dot(ws[h], vn_h) + bias_ref[:, h:h + 1]
            silu, dsilu = _silu_and_grad(cg_ref[:, cols])
            dyc = dyc_ref[:, cols]
            u = u_ref[:, cols]
            du_ref[:, cols] = (dyc * s * silu).astype(bf16)
            dcg_ref[:, cols] = (dyc * u * s * dsilu).astype(bf16)
            ds = dyc * u * silu
            dbias_ref[:, h:h + 1] += jnp.sum(ds, axis=1, keepdims=True)
            ds = ds.astype(bf16)
            _accumulate(dw_ref.at[h], jnp.where(tril, _dot_nt(ds, vn_h), 0.0), first)
            dvn_ref[:, cols] = _dot_tn(ws[h], ds)
        dv, dg, db = _ln_bwd(xh, rstd, g, dvn_ref[...])
        dv_ref[...] = dv.astype(bf16)
        _accumulate(dg_ref, dg, first)
        _accumulate(db_ref, db, first)
        full()

    chunk, vec, w, bias = _sgu_specs()
    tiles = pltpu.VMEM((2, CHUNK, HALF), bf16)
    return pl.pallas_call(
        body, name="sgu_bwd", grid=(SEQ // CHUNK,),
        in_specs=[chunk(0), chunk(1), chunk(2), chunk(0), vec, vec, w, bias],
        out_specs=[ANY, w, bias, vec, vec],
        out_shape=[SDS((SEQ, 3 * D_MODEL), bf16), SDS((4, CHUNK, CHUNK), f32), SDS((CHUNK, CHUNK), f32),
                   SDS((1, HALF), f32), SDS((1, HALF), f32)],
        scratch_shapes=[pltpu.VMEM((CHUNK, HALF), f32), tiles, tiles, tiles, pltpu.SemaphoreType.DMA((6,))],
    )(z1, z1, z1, dcat, ln_g, ln_b, sgu_w, bias_t)


CONV_TILE = 128
DVAL_COL, DGLU_COL = 12, 16


def _conv_specs():
    val = pl.BlockSpec((SEQ, POOL_CH), lambda j: (0, DVAL_COL + j))
    glu = pl.BlockSpec((SEQ, POOL_CH), lambda j: (0, DGLU_COL + j))
    w = pl.BlockSpec((CONV_K, POOL_CH), lambda j: (0, j))
    col = pl.BlockSpec((SEQ, POOL_CH), lambda j: (0, j))
    vec = pl.BlockSpec((1, POOL_CH), lambda j: (0, j))
    return val, glu, w, col, vec


def _conv_fwd(z1, conv_w, conv_b):
    def body(val_ref, glu_ref, w_ref, b_ref, out_ref, xpad):
        xpad[0:CONV_PAD, :] = jnp.zeros((CONV_PAD, POOL_CH), f32)
        xpad[CONV_PAD:, :] = val_ref[...] * _sigmoid(glu_ref[...])
        w = w_ref[...]
        bias = b_ref[...]

        def tile(i, carry):
            t0 = pl.multiple_of(i * CONV_TILE, CONV_TILE)
            window = xpad[pl.ds(t0, CONV_TILE + CONV_PAD), :]
            acc = jnp.broadcast_to(bias, (CONV_TILE, POOL_CH))
            for k in range(CONV_K):
                shift = CONV_PAD - (CONV_K - 1) + k
                acc = acc + w[k:k + 1, :] * pltpu.roll(window, CONV_TILE + CONV_PAD - shift, 0)[0:CONV_TILE]
            out_ref[pl.ds(t0, CONV_TILE), :] = acc
            return carry

        lax.fori_loop(0, SEQ // CONV_TILE, tile, 0)

    val, glu, w, col, vec = _conv_specs()
    return pl.pallas_call(
        body, name="conv_fwd", grid=(4,), in_specs=[val, glu, w, vec], out_specs=col,
        out_shape=SDS((SEQ, HALF), f32), scratch_shapes=[pltpu.VMEM((SEQ + CONV_PAD, POOL_CH), f32)],
    )(z1, z1, conv_w, conv_b)


def _conv_bwd(z1, dconv, conv_w, dz):
    def body(val_ref, glu_ref, w_ref, dout_ref, dz_in, dz_ref, dw_ref, db_ref, xpad, dpad, dx_ref, dval_tiles, dglu_tiles, sems):
        j = pl.program_id(0)
        (dval_ref, dglu_ref), full = _write_behind(
            j, 4, [dval_tiles, dglu_tiles], sems,
            lambda t, at: _columns(dz_ref, ((DVAL_COL, DGLU_COL)[t] + at) * POOL_CH, POOL_CH))
        val = val_ref[...]
        sig = _sigmoid(glu_ref[...])
        xpad[0:CONV_PAD, :] = jnp.zeros((CONV_PAD, POOL_CH), f32)
        xpad[CONV_PAD:, :] = val * sig
        dout = dout_ref[...]
        dpad[0:SEQ, :] = dout
        dpad[SEQ:, :] = jnp.zeros((CONV_PAD, POOL_CH), f32)
        db_ref[...] = jnp.sum(dout, axis=0, keepdims=True)
        dw_ref[...] = jnp.zeros((CONV_K, POOL_CH), f32)
        w = w_ref[...]

        def tile(i, carry):
            t0 = pl.multiple_of(i * CONV_TILE, CONV_TILE)
            x_win = xpad[pl.ds(t0, CONV_TILE + CONV_PAD), :]
            d_win = dpad[pl.ds(t0, CONV_TILE + CONV_PAD), :]
            d_own = d_win[0:CONV_TILE]
            acc = jnp.zeros((CONV_TILE, POOL_CH), f32)
            for k in range(CONV_K):
                shift = CONV_PAD - (CONV_K - 1) + k
                x_k = pltpu.roll(x_win, CONV_TILE + CONV_PAD - shift, 0)[0:CONV_TILE]
                dw_ref[k:k + 1, :] += jnp.sum(d_own * x_k, axis=0, keepdims=True)
                back = CONV_K - 1 - k
                d_k = d_own if back == 0 else pltpu.roll(d_win, CONV_TILE + CONV_PAD - back, 0)[0:CONV_TILE]
                acc = acc + w[k:k + 1, :] * d_k
            dx_ref[pl.ds(t0, CONV_TILE), :] = acc
            return carry

        lax.fori_loop(0, SEQ // CONV_TILE, tile, 0)
        dx = dx_ref[...]
        dval_ref[...] = (dx * sig).astype(bf16)
        dglu_ref[...] = (dx * val * sig * (1.0 - sig)).astype(bf16)
        full()

    val, glu, w, col, vec = _conv_specs()
    pad = pltpu.VMEM((SEQ + CONV_PAD, POOL_CH), f32)
    tiles = pltpu.VMEM((2, SEQ, POOL_CH), bf16)
    return pl.pallas_call(
        body, name="conv_bwd", grid=(4,), in_specs=[val, glu, w, col, ANY], out_specs=[ANY, w, vec],
        out_shape=[SDS(dz.shape, dz.dtype), SDS((CONV_K, HALF), f32), SDS((1, HALF), f32)],
        input_output_aliases={4: 0},
        scratch_shapes=[pad, pad, pltpu.VMEM((SEQ, POOL_CH), f32), tiles, tiles, pltpu.SemaphoreType.DMA((4,))],
    )(z1, z1, conv_w, dconv, dz)


DGATE_COL = 5


def _conv_norm_fwd(conv, z1, g, b, mixed):
    def body(c_ref, gate_ref, g_ref, b_ref, mixed_ref, yd_ref):
        xh, _ = _ln_stats(c_ref[...])
        n = xh * g_ref[...] + b_ref[...]
        gate = gate_ref[...]
        yd_ref[...] = (n * _sigmoid(n) * gate * _sigmoid(gate)).astype(bf16)

    return pl.pallas_call(
        body, name="conv_norm_fwd", grid=(SEQ // ROWS,),
        in_specs=[_row_spec(HALF), _row_spec(HALF, DGATE_COL), _vec_spec(HALF), _vec_spec(HALF), ANY],
        out_specs=_row_spec(HALF, 1), out_shape=SDS((SEQ, 2 * HALF), bf16), input_output_aliases={4: 0},
    )(conv, z1, g, b, mixed)


def _conv_norm_bwd(conv, z1, dcat, g, b, dz):
    def body(c_ref, gate_ref, dyd_ref, g_ref, b_ref, dz_ref, dconv_ref, dgate_ref, dg_ref, db_ref):
        first = pl.program_id(0) == 0
        xh, rstd = _ln_stats(c_ref[...])
        g = g_ref[...]
        n_silu, n_dsilu = _silu_and_grad(xh * g + b_ref[...])
        gate_silu, gate_dsilu = _silu_and_grad(gate_ref[...])
        dyd = dyd_ref[...]
        dgate_ref[...] = (dyd * n_silu * gate_dsilu).astype(bf16)
        dconv, dg, db = _ln_bwd(xh, rstd, g, dyd * gate_silu * n_dsilu)
        dconv_ref[...] = dconv
        _accumulate(dg_ref, dg, first)
        _accumulate(db_ref, db, first)

    return pl.pallas_call(
        body, name="conv_norm_bwd", grid=(SEQ // ROWS,),
        in_specs=[_row_spec(HALF), _row_spec(HALF, DGATE_COL), _row_spec(HALF, 1), _vec_spec(HALF), _vec_spec(HALF), ANY],
        out_specs=[_row_spec(HALF), _row_spec(HALF, DGATE_COL), _vec_spec(HALF), _vec_spec(HALF)],
        out_shape=[SDS((SEQ, HALF), f32), SDS(dz.shape, dz.dtype), SDS((1, HALF), f32), SDS((1, HALF), f32)],
        input_output_aliases={5: 1},
    )(conv, z1, dcat, g, b, dz)


def _step(x, target, w, chip):
    chip_vec = chip.astype(jnp.int32).reshape(1)
    sharded_names = list(SHARDED_SMALL)
    first = [_cast_into_slot(w["e_w_in"], chip_vec, "cast_e_w_in0", w["e_pre_norm"], 0, E_IN_PIECES)]
    sems, bufs, token = _gather_start(first, "gather_start_first")
    small_shard = _pack([w[k] for k in sharded_names], total_rows=SMALL_SHARD_ROWS) + 0.0 * token[0, 0]
    small_slot = lax.dynamic_update_slice(jnp.zeros((N_CHIPS, SMALL_SHARD_ROWS, LANES), f32), small_shard[None], (chip, 0, 0))
    more = [small_slot]
    more += [_cast_into_slot(w["e_w_in"], chip_vec, f"cast_e_w_in{i}", token, i, E_IN_PIECES) for i in range(1, E_IN_PIECES)]
    more_sems, more_bufs, token = _gather_start(more, "gather_start_pieces")
    rest = [_cast_into_slot(w[k], chip_vec, f"cast_{k}", token) for k in BIG[1:]]
    rest_sems, rest_bufs, token = _gather_start(rest, "gather_start_rest")
    sems, bufs = sems + more_sems + rest_sems, bufs + more_bufs + rest_bufs
    tables = _rope_tables()

    def vec(k):
        return w[k].reshape(1, -1)

    h0 = _pre_norm(x, vec("e_pre_norm") + token[0, 0])
    after, z0, e_w_in = h0, None, []
    for i in range(E_IN_PIECES):
        group = slice(0, 1) if i == 0 else slice(1, 3) if i == 1 else slice(i + 1, i + 2)
        landed = _forward_halves(_gather_wait(bufs[group], sems[group], after, f"gather_wait_{i}"), f"forward_{i}")
        if i == 1:
            small_full = landed[0]
        e_w_in.append(landed[-1])
        z0 = _mm_nn(h0, landed[-1], f32, f"e_in{i}", i, E_IN_PIECES, z0)
        after = z0
    p = {k: _from_chips(k, a) for k, a in zip(sharded_names, _unpack(small_full, [SHARDED_SMALL[k][0] for k in sharded_names]))}
    for k in ("o_pre_norm", "o_sgu_norm_g", "o_sgu_norm_b", "o_conv_b", "o_conv_norm_g", "o_conv_norm_b", "o_post_norm"):
        p[k] = p[k].reshape(1, -1)
    pool_w_bf = p["e_pool_w"].astype(bf16)
    bias_t = jnp.pad(w["o_sgu_b"].T, ((0, 0), (0, CHUNK - 4)))

    cat0, att, lse, qkv_by_residue = _attn_fwd(z0, tables, _pool_fwd(z0, pool_w_bf, vec("e_pool_scale")))

    def arrived(index, after, name):
        one = slice(index, index + 1)
        return _forward_halves(_gather_wait(bufs[one], sems[one], after, f"gather_wait_{name}"), f"forward_{name}")[0]

    e_w_out = arrived(1 + E_IN_PIECES, att, "e_w_out").reshape(1, D_MODEL, D_MODEL)
    y0 = _mm_nn(cat0, e_w_out, f32, "e_out")
    x1, h1 = _mid_norm(x, y0, vec("e_post_norm"), p["o_pre_norm"])
    o_w_in = arrived(2 + E_IN_PIECES, h1, "o_w_in")
    z1 = _mm_nn(h1, o_w_in, f32, "o_in")
    yc = _sgu_fwd(z1, p["o_sgu_norm_g"], p["o_sgu_norm_b"], w["o_sgu_w"], bias_t)
    conv = _conv_fwd(z1, p["o_conv_w"], p["o_conv_b"])
    cat1 = _conv_norm_fwd(conv, z1, p["o_conv_norm_g"], p["o_conv_norm_b"], yc)
    o_w_out = arrived(3 + E_IN_PIECES, cat1, "o_w_out").reshape(1, D_MODEL, D_MODEL)
    y1 = _mm_nn(cat1, o_w_out, f32, "o_out")
    loss, dx2, dy1, g_o_post = _final_norm_loss(x1, y1, p["o_post_norm"], target)

    in_flight = {}

    def send_off(name, grad):
        sem, sums, land, tok = _scatter_start(_swap_add(grad, f"swap_add_{name}"), f"scatter_start_{name}")
        in_flight[name] = (sem, sums, land)
        return tok

    tok = send_off("o_w_out", _mm_tn(cat1, dy1, 1, "o_out_dw").reshape(N_CHIPS, HALF // 2, D_MODEL))
    dcat1 = _mm_nt(dy1, [o_w_out], "o_out_dx", tok)
    dz1, g_sgu_w, g_bias_t, g_sgu_g, g_sgu_b = _sgu_bwd(
        z1, dcat1, p["o_sgu_norm_g"] + tok[0, 0], p["o_sgu_norm_b"], w["o_sgu_w"], bias_t)
    dconv, dz1, g_cn_g, g_cn_b = _conv_norm_bwd(conv, z1, dcat1, p["o_conv_norm_g"], p["o_conv_norm_b"], dz1)
    dz1, g_conv_w, g_conv_b = _conv_bwd(z1, dconv, p["o_conv_w"], dz1)
    tok = send_off("o_w_in", _mm_tn(h1, dz1, N_CHIPS, "o_in_dw"))
    dh1 = _mm_nt(dz1, [o_w_in], "o_in_dx", tok)
    dx1, dy0, g_o_pre, g_e_post = _mid_norm_bwd(dx2, dh1, x1, y0, p["o_pre_norm"] + tok[0, 0], vec("e_post_norm"))

    tok = send_off("e_w_out", _mm_tn(cat0, dy0, 1, "e_out_dw").reshape(N_CHIPS, HALF // 2, D_MODEL))
    dcat0 = _mm_nt(dy0, [e_w_out], "e_out_dx", tok)
    dz0, g_pool_w, g_pool_scale = _pool_bwd(z0, dcat0, pool_w_bf, vec("e_pool_scale") + tok[0, 0])
    for g in range(len(DILATIONS)):
        dz0 = _attn_bwd_group(g, qkv_by_residue[g], z0, att, lse, dcat0, tables, dz0)
    tok = send_off("e_w_in", _mm_tn(h0, dz0, N_CHIPS, "e_in_dw"))
    dh0 = _mm_nt(dz0, e_w_in, "e_in_dx", tok)
    grad_x, g_e_pre = _pre_norm_bwd(dx1, dh0, x, vec("e_pre_norm") + tok[0, 0])

    small = {"e_pre_norm": g_e_pre, "e_pool_w": g_pool_w, "e_pool_scale": g_pool_scale, "e_post_norm": g_e_post,
             "o_pre_norm": g_o_pre, "o_sgu_norm_g": g_sgu_g, "o_sgu_norm_b": g_sgu_b, "o_sgu_w": g_sgu_w,
             "o_sgu_b": g_bias_t, "o_conv_w": g_conv_w, "o_conv_b": g_conv_b,
             "o_conv_norm_g": g_cn_g, "o_conv_norm_b": g_cn_b, "o_post_norm": g_o_post}
    return loss, grad_x, in_flight, small


def _land(in_flight, name, chip, after):
    sems, sums, land = in_flight[name]
    sums, land = _scatter_wait(sems, sums, land, after, f"scatter_wait_{name}")
    return _add_landed_join(sums, land, chip.astype(jnp.int32).reshape(1), f"add_landed_{name}")


def _place():
    x, y, c = lax.axis_index("x"), lax.axis_index("y"), lax.axis_index("c")
    others = [(1 - x, y), (x, 1 - y), (1 - x, 1 - y)]
    return x, y, c, 2 * x + y, others


SWAP_ROWS = 256
FORWARD_STAGE_BYTES = 4 << 20


def _swap_add(g, name):
    chips, r, c = g.shape
    half = r // 2
    rows_per_step = 2 * SWAP_ROWS if half % (2 * SWAP_ROWS) == 0 else SWAP_ROWS
    nb = half // rows_per_step
    steps = chips * nb

    def body(core_ref, mine_ref, theirs_ref, out_ref, landing, send_sems, recv_sems, free_sems):
        i = pl.program_id(0)
        x, y, core, _, _ = _place()
        sibling = (x, y, 1 - core)

        def send(slot):
            return pltpu.make_async_remote_copy(src_ref=theirs_ref, dst_ref=landing.at[slot], send_sem=send_sems.at[slot],
                                                recv_sem=recv_sems.at[slot], device_id=sibling, device_id_type=MESH)

        @pl.when(i < steps)
        def _():
            @pl.when(i >= 2)
            def _():
                pl.semaphore_wait(free_sems.at[i % 2], 1)

            send(i % 2).start()

        @pl.when(i >= 1)
        def _():
            landed = (i - 1) % 2
            send(landed).wait_recv()
            out_ref[...] = (mine_ref[...].astype(f32) + landing[landed].astype(f32)).astype(out_ref.dtype)

            @pl.when(i + 1 < steps)
            def _():
                pl.semaphore_signal(free_sems.at[landed], 1, device_id=sibling, device_id_type=MESH)

        @pl.when(i < steps)
        def _():
            send(i % 2).wait_send()

    def rows_of(b, h):
        return (2 * (b // nb) + h) * nb + b % nb

    block = (rows_per_step, c)
    grid_spec = pltpu.PrefetchScalarGridSpec(
        num_scalar_prefetch=1, grid=(steps + 1,),
        in_specs=[pl.BlockSpec(block, lambda i, core: (rows_of(jnp.maximum(i - 1, 0), core[0]), 0)),
                  pl.BlockSpec(block, lambda i, core: (rows_of(jnp.minimum(i, steps - 1), 1 - core[0]), 0))],
        out_specs=pl.BlockSpec(block, lambda i, core: (jnp.maximum(i - 1, 0), 0)),
        scratch_shapes=[pltpu.VMEM((2, rows_per_step, c), g.dtype), pltpu.SemaphoreType.DMA((2,)),
                        pltpu.SemaphoreType.DMA((2,)), pltpu.SemaphoreType.REGULAR((2,))])
    core = lax.axis_index("c").astype(jnp.int32).reshape(1)
    rows = g.reshape(chips * r, c)
    out = pl.pallas_call(body, name=name, grid_spec=grid_spec, out_shape=SDS((chips * half, c), g.dtype))(core, rows, rows)
    return out.reshape(chips, half, c)


HBM = pl.BlockSpec(memory_space=pltpu.HBM)
SEM = pl.BlockSpec(memory_space=pltpu.SEMAPHORE)
EFFECT = pltpu.SideEffectType.DATAFLOW_SIDE_EFFECTING


def _in_hbm(a):
    return pltpu.with_memory_space_constraint(a, pltpu.HBM)


def _cast_into_slot(w, chip, name, after, piece=0, pieces=1):
    r, c = w.shape
    c = c // pieces
    nb = r // SWAP_ROWS

    def body(chip_ref, w_ref, after_ref, o_ref):
        o_ref[...] = w_ref[...].astype(bf16)

    grid_spec = pltpu.PrefetchScalarGridSpec(
        num_scalar_prefetch=1, grid=(nb,),
        in_specs=[pl.BlockSpec((SWAP_ROWS, c), lambda i, chip: (i, piece)), ANY],
        out_specs=pl.BlockSpec((SWAP_ROWS, c), lambda i, chip: (chip[0] * nb + i, 0)))
    out = pl.pallas_call(body, name=name, grid_spec=grid_spec, out_shape=SDS((N_CHIPS * r, c), bf16))(chip, w, after)
    return out.reshape(N_CHIPS, r, c)


def _gather_start(bufs, name):
    n = len(bufs)

    def body(*refs):
        ins, sems, token = refs[:n], refs[n:3 * n], refs[4 * n]
        x, y, c, me, others = _place()
        for a in range(n):
            rows = ins[a].shape[1] // 2
            mine = ins[a].at[me, pl.ds(c * rows, rows), :]
            for k, (ox, oy) in enumerate(others):
                pltpu.make_async_remote_copy(src_ref=mine, dst_ref=mine, send_sem=sems[2 * a].at[k],
                                             recv_sem=sems[2 * a + 1].at[k], device_id=(ox, oy, c),
                                             device_id_type=MESH).start()
        token[...] = jnp.zeros_like(token)

    out = pl.pallas_call(
        body, name=name, in_specs=[HBM] * n,
        out_shape=(*[pltpu.SemaphoreType.DMA((3,))] * (2 * n), *[pltpu.HBM(b.shape, b.dtype) for b in bufs],
                   SDS((8, 128), f32)),
        out_specs=(*[SEM] * (2 * n), *[HBM] * n, pl.BlockSpec(memory_space=pltpu.VMEM)),
        input_output_aliases={a: 2 * n + a for a in range(n)},
        compiler_params=pltpu.CompilerParams(has_side_effects=EFFECT),
    )(*[_in_hbm(b) for b in bufs])
    return [(out[2 * a], out[2 * a + 1]) for a in range(n)], list(out[2 * n:3 * n]), out[3 * n]


def _gather_wait(bufs, sems, after, name):
    n = len(bufs)

    def body(*refs):
        ins, sem_refs = refs[:n], refs[n:3 * n]
        x, y, c, me, others = _place()
        for a in range(n):
            rows = ins[a].shape[1] // 2
            mine = ins[a].at[me, pl.ds(c * rows, rows), :]
            for k, (ox, oy) in enumerate(others):
                landed = ins[a].at[2 * ox + oy, pl.ds(c * rows, rows), :]
                copy = pltpu.make_async_remote_copy(src_ref=mine, dst_ref=landed, send_sem=sem_refs[2 * a].at[k],
                                                    recv_sem=sem_refs[2 * a + 1].at[k], device_id=(ox, oy, c),
                                                    device_id_type=MESH)
                copy.wait_send()
                copy.wait_recv()

    flat_sems = [s for pair in sems for s in pair]
    out = pl.pallas_call(
        body, name=name, in_specs=[HBM] * n + [SEM] * (2 * n) + [ANY],
        out_shape=tuple(pltpu.HBM(b.shape, b.dtype) for b in bufs), out_specs=tuple([HBM] * n),
        input_output_aliases={a: a for a in range(n)},
        compiler_params=pltpu.CompilerParams(has_side_effects=EFFECT),
    )(*bufs, *flat_sems, after)
    return list(out)


def _forward_halves(bufs, name):
    n = len(bufs)
    blocks = []
    for b in bufs:
        half = b.shape[1] // 2
        whole = half * b.shape[2] * b.dtype.itemsize <= FORWARD_STAGE_BYTES
        blocks.append((half, half if whole or half % SWAP_ROWS else SWAP_ROWS))
    work = [(a, k, b) for a in range(n) for k in range(3) for b in range(blocks[a][0] // blocks[a][1])]

    def body(*refs):
        outs, stages = refs[n:2 * n], refs[2 * n:3 * n]
        load_sems, send_sems, recv_sems = refs[3 * n:]
        x, y, c, me, others = _place()
        sibling = (x, y, 1 - c)

        def rows(item):
            a, k, b = item
            half, tr = blocks[a]
            ox, oy = others[k]
            return outs[a].at[2 * ox + oy, pl.ds(c * half + b * tr, tr), :]

        def load(s, item):
            return pltpu.make_async_copy(rows(item), stages[item[0]].at[s], load_sems.at[s])

        def send(s, item):
            return pltpu.make_async_remote_copy(src_ref=stages[item[0]].at[s], dst_ref=rows(item), send_sem=send_sems.at[s],
                                                recv_sem=recv_sems.at[item[0]], device_id=sibling, device_id_type=MESH)

        load(0, work[0]).start()
        for t, item in enumerate(work):
            s = t % 2
            load(s, item).wait()
            send(s, item).start()
            if t + 1 < len(work):
                if t >= 1:
                    send(1 - s, work[t - 1]).wait_send()
                load(1 - s, work[t + 1]).start()
        if len(work) > 1:
            send(len(work) % 2, work[-2]).wait_send()
        send((len(work) - 1) % 2, work[-1]).wait_send()
        for a in range(n):
            theirs = outs[a].at[pl.ds(0, 3), pl.ds(0, blocks[a][0]), :]
            pltpu.make_async_remote_copy(src_ref=theirs, dst_ref=theirs, send_sem=send_sems.at[0], recv_sem=recv_sems.at[a],
                                         device_id=sibling, device_id_type=MESH).wait_recv()

    out = pl.pallas_call(
        body, name=name, in_specs=[ANY] * n, out_specs=[ANY] * n, out_shape=[SDS(b.shape, b.dtype) for b in bufs],
        input_output_aliases={a: a for a in range(n)},
        scratch_shapes=[pltpu.VMEM((2, blocks[a][1], bufs[a].shape[2]), bufs[a].dtype) for a in range(n)]
        + [pltpu.SemaphoreType.DMA((2,)), pltpu.SemaphoreType.DMA((2,)), pltpu.SemaphoreType.DMA((n,))],
    )(*bufs)
    return list(out)


def _scatter_start(chip_sums, name):
    def body(a_ref, land_ref, send_sems, recv_sems, a_thru, land_thru, token):
        x, y, c, me, others = _place()
        for k, (ox, oy) in enumerate(others):
            pltpu.make_async_remote_copy(src_ref=a_ref.at[2 * ox + oy], dst_ref=land_ref.at[me], send_sem=send_sems.at[k],
                                         recv_sem=recv_sems.at[k], device_id=(ox, oy, c), device_id_type=MESH).start()
        token[...] = jnp.zeros_like(token)

    shape = pltpu.HBM(chip_sums.shape, chip_sums.dtype)
    send, recv, a_thru, land, token = pl.pallas_call(
        body, name=name, in_specs=[HBM, HBM],
        out_shape=(pltpu.SemaphoreType.DMA((3,)), pltpu.SemaphoreType.DMA((3,)), shape, shape, SDS((8, 128), f32)),
        out_specs=(SEM, SEM, HBM, HBM, pl.BlockSpec(memory_space=pltpu.VMEM)), input_output_aliases={0: 2, 1: 3},
        compiler_params=pltpu.CompilerParams(has_side_effects=EFFECT),
    )(_in_hbm(chip_sums), _in_hbm(lax.empty(chip_sums.shape, chip_sums.dtype)))
    return (send, recv), a_thru, land, token


def _scatter_wait(sems, chip_sums, land, after, name):
    def body(a_ref, land_ref, send_sems, recv_sems, after_ref, a_out, land_out):
        x, y, c, me, others = _place()
        for k, (ox, oy) in enumerate(others):
            copy = pltpu.make_async_remote_copy(
                src_ref=a_ref.at[2 * ox + oy], dst_ref=land_ref.at[2 * ox + oy], send_sem=send_sems.at[k],
                recv_sem=recv_sems.at[k], device_id=(ox, oy, c), device_id_type=MESH)
            copy.wait_send()
            copy.wait_recv()

    shape = pltpu.HBM(chip_sums.shape, chip_sums.dtype)
    return pl.pallas_call(
        body, name=name, in_specs=[HBM, HBM, SEM, SEM, ANY], out_shape=(shape, shape), out_specs=(HBM, HBM),
        input_output_aliases={0: 0, 1: 1}, compiler_params=pltpu.CompilerParams(has_side_effects=EFFECT),
    )(chip_sums, land, sems[0], sems[1], after)


def _add_landed_join(chip_sums, land, chip, name):
    chips, rh, c = chip_sums.shape
    nb = rh // SWAP_ROWS

    def body(chip_ref, own_ref, l1_ref, l2_ref, l3_ref, out_hbm, buf, send_sems, recv_sem, local_sems):
        i = pl.program_id(0)
        slot = i % 2
        x, y, core, _, _ = _place()
        sibling = (x, y, 1 - core)

        def copies(s, step):
            rows = pl.ds(pl.multiple_of((core * nb + step) * SWAP_ROWS, SWAP_ROWS), SWAP_ROWS)
            keep = pltpu.make_async_copy(buf.at[s], out_hbm.at[rows, :], local_sems.at[s])
            give = pltpu.make_async_remote_copy(src_ref=buf.at[s], dst_ref=out_hbm.at[rows, :], send_sem=send_sems.at[s],
                                                recv_sem=recv_sem.at[0], device_id=sibling, device_id_type=MESH)
            return keep, give

        def drain(s, step):
            keep, give = copies(s, step)
            keep.wait()
            give.wait_send()

        @pl.when(i >= 2)
        def _():
            drain(slot, i - 2)

        buf[slot] = ((own_ref[...].astype(f32) + l1_ref[...].astype(f32)) + l2_ref[...].astype(f32)) + l3_ref[...].astype(f32)
        keep, give = copies(slot, i)
        keep.start()
        give.start()

        @pl.when(i == nb - 1)
        def _():
            drain(slot, i)
            if nb > 1:
                drain(1 - slot, i - 1)
            theirs = out_hbm.at[pl.ds((1 - core) * rh, rh), :]
            pltpu.make_async_remote_copy(src_ref=theirs, dst_ref=theirs, send_sem=send_sems.at[0], recv_sem=recv_sem.at[0],
                                         device_id=sibling, device_id_type=MESH).wait_recv()

    block = (SWAP_ROWS, c)
    from_slot = lambda d: pl.BlockSpec(block, lambda i, chip: (((chip[0] + d) % chips) * nb + i, 0))
    grid_spec = pltpu.PrefetchScalarGridSpec(
        num_scalar_prefetch=1, grid=(nb,), in_specs=[from_slot(0), from_slot(1), from_slot(2), from_slot(3)],
        out_specs=ANY,
        scratch_shapes=[pltpu.VMEM((2, SWAP_ROWS, c), f32), pltpu.SemaphoreType.DMA((2,)),
                        pltpu.SemaphoreType.DMA((1,)), pltpu.SemaphoreType.DMA((2,))])
    land_rows = land.reshape(chips * rh, c)
    return pl.pallas_call(body, name=name, grid_spec=grid_spec, out_shape=SDS((2 * rh, c), f32))(
        chip, chip_sums.reshape(chips * rh, c), land_rows, land_rows, land_rows)


def _adamw_update(w_ref, g_ref, m_ref, v_ref, d_ref, nm_ref, nv_ref):
    g = g_ref[...]
    nm = ADAM_B1 * m_ref[...] + (1.0 - ADAM_B1) * g
    nv = ADAM_B2 * v_ref[...] + (1.0 - ADAM_B2) * (g * g)
    nm_ref[...] = nm
    nv_ref[...] = nv
    m_hat = nm / (1.0 - ADAM_B1 ** ADAM_STEP)
    v_hat = nv / (1.0 - ADAM_B2 ** ADAM_STEP)
    d_ref[...] = -ADAM_LR * (m_hat / (jnp.sqrt(v_hat) + ADAM_EPS) + ADAM_WD * w_ref[...])


def _adamw(w, g, m, v, name):
    r, c = w.shape
    tr = 128 if r % 128 == 0 else r

    def body(w_ref, g_ref, m_ref, v_ref, g_out_ref, d_ref, nm_ref, nv_ref):
        g_out_ref[...] = g_ref[...]
        _adamw_update(w_ref, g_ref, m_ref, v_ref, d_ref, nm_ref, nv_ref)

    spec = pl.BlockSpec((tr, c), lambda i: (i, 0))
    return pl.pallas_call(body, name=name, grid=(r // tr,), in_specs=[spec] * 4, out_specs=[spec] * 4,
                          out_shape=[SDS((r, c), f32)] * 4)(w, g, m, v)


SMALL_PACKING = {
    "e_pre_norm": ((1, 2048), 8, (1, 2048)), "e_pool_w": ((1024, 256), 1024, (256, 256)),
    "e_pool_scale": ((1, 1024), 8, (1, 1024)), "e_post_norm": ((1, 2048), 8, (1, 2048)),
    "o_pre_norm": ((1, 2048), 8, (1, 512)), "o_sgu_norm_g": ((1, 1024), 8, (1, 256)),
    "o_sgu_norm_b": ((1, 1024), 8, (1, 256)), "o_sgu_w": ((512, 128), 512, (512, 128)),
    "o_sgu_b": ((128, 128), 8, (4, 128)), "o_conv_w": ((31, 1024), 128, (31, 256)), "o_conv_b": ((1, 1024), 8, (1, 256)),
    "o_conv_norm_g": ((1, 1024), 8, (1, 256)), "o_conv_norm_b": ((1, 1024), 8, (1, 256)),
    "o_post_norm": ((1, 2048), 8, (1, 512)),
}
SMALL_PACKED_ROWS = 1792


def _small_finalize(grads, ws, ms, vs, after):
    names = list(SMALL_ORDER)
    n = len(names)
    half, piece = SMALL_PACKED_ROWS // 2, SMALL_PACKED_ROWS // 8
    first_row, row = {}, 0
    for k in names:
        first_row[k] = row
        row += SMALL_PACKING[k][1]

    def body(*refs):
        g_refs, total = refs[0:n], refs[n + 1]
        pack, from_sibling, from_chips, send_a, recv_a, send_b, recv_b, send_c, recv_c, send_d, recv_d = refs[n + 2:]
        x, y, c, me, others = _place()

        for r0 in range(0, SMALL_PACKED_ROWS, piece):
            pack[r0:r0 + piece, :] = jnp.zeros((piece, LANES), f32)
        for k, g_ref in zip(names, g_refs):
            (rows, width), _, _ = SMALL_PACKING[k]
            r0 = first_row[k]
            if k == "o_sgu_b":
                pack[r0:r0 + 4, 0:CHUNK] = g_ref[...].T[0:4, :]
            elif width < LANES:
                pack[r0:r0 + rows, 0:width] = g_ref[...]
            else:
                for j in range(width // LANES):
                    dst = r0 + j * (1 if rows == 1 else 32)
                    pack[dst:dst + rows, :] = g_ref[:, j * LANES:(j + 1) * LANES]

        sibling = (x, y, 1 - c)
        swap = pltpu.make_async_remote_copy(
            src_ref=pack.at[pl.ds(pl.multiple_of((1 - c) * half, 8), half), :], dst_ref=from_sibling,
            send_sem=send_a.at[0], recv_sem=recv_a.at[0], device_id=sibling, device_id_type=MESH)
        swap.start()
        swap.wait()
        for j in range(4):
            rows = pl.ds(pl.multiple_of(c * half + j * piece, 8), piece)
            pack[rows, :] = pack[rows, :] + from_sibling[j * piece:(j + 1) * piece, :]

        def piece_of(chip):
            return pl.ds(pl.multiple_of(c * half + chip * piece, 8), piece)

        def to_chip(k):
            ox, oy = others[k]
            return pltpu.make_async_remote_copy(
                src_ref=pack.at[piece_of(2 * ox + oy), :], dst_ref=from_chips.at[me], send_sem=send_b.at[k],
                recv_sem=recv_b.at[k], device_id=(ox, oy, c), device_id_type=MESH)

        for k in range(3):
            to_chip(k).start()
        from_chips[me] = pack[piece_of(me), :]
        for k, (ox, oy) in enumerate(others):
            landed = from_chips.at[2 * ox + oy]
            pltpu.make_async_remote_copy(src_ref=landed, dst_ref=landed, send_sem=send_b.at[k], recv_sem=recv_b.at[k],
                                         device_id=(ox, oy, c), device_id_type=MESH).wait_recv()
        for k in range(3):
            to_chip(k).wait_send()
        mine = pl.ds(pl.multiple_of(c * half + me * piece, 8), piece)
        total[mine, :] = ((from_chips[0] + from_chips[1]) + from_chips[2]) + from_chips[3]

        def to_same_core(k):
            ox, oy = others[k]
            return pltpu.make_async_remote_copy(
                src_ref=total.at[mine, :], dst_ref=total.at[mine, :], send_sem=send_c.at[k], recv_sem=recv_c.at[k],
                device_id=(ox, oy, c), device_id_type=MESH)

        for k in range(3):
            to_same_core(k).start()
        for k, (ox, oy) in enumerate(others):
            theirs = total.at[piece_of(2 * ox + oy), :]
            pltpu.make_async_remote_copy(src_ref=theirs, dst_ref=theirs, send_sem=send_c.at[k], recv_sem=recv_c.at[k],
                                         device_id=(ox, oy, c), device_id_type=MESH).wait_recv()
        for k in range(3):
            to_same_core(k).wait_send()
        my_half = total.at[pl.ds(pl.multiple_of(c * half, 8), half), :]
        join = pltpu.make_async_remote_copy(src_ref=my_half, dst_ref=my_half, send_sem=send_d.at[0], recv_sem=recv_d.at[0],
                                            device_id=sibling, device_id_type=MESH)
        join.start()
        their_half = total.at[pl.ds(pl.multiple_of((1 - c) * half, 8), half), :]
        pltpu.make_async_remote_copy(src_ref=their_half, dst_ref=their_half, send_sem=send_d.at[0], recv_sem=recv_d.at[0],
                                     device_id=sibling, device_id_type=MESH).wait_recv()
        join.wait_send()

    whole = pl.BlockSpec(memory_space=pltpu.VMEM)
    total = pl.pallas_call(
        body, name="small_allreduce", in_specs=[whole] * n + [ANY], out_specs=whole,
        out_shape=SDS((SMALL_PACKED_ROWS, LANES), f32),
        scratch_shapes=[pltpu.VMEM((SMALL_PACKED_ROWS, LANES), f32), pltpu.VMEM((half, LANES), f32),
                        pltpu.VMEM((N_CHIPS, piece, LANES), f32),
                        pltpu.SemaphoreType.DMA((1,)), pltpu.SemaphoreType.DMA((1,)), pltpu.SemaphoreType.DMA((3,)),
                        pltpu.SemaphoreType.DMA((3,)), pltpu.SemaphoreType.DMA((3,)), pltpu.SemaphoreType.DMA((3,)),
                        pltpu.SemaphoreType.DMA((1,)), pltpu.SemaphoreType.DMA((1,))],
    )(*grads, after)

    def update(*refs):
        total = refs[0]
        w_refs, m_refs, v_refs = refs[1:n + 1], refs[n + 1:2 * n + 1], refs[2 * n + 1:3 * n + 1]
        outs = refs[3 * n + 1:]
        me = 2 * lax.axis_index("x") + lax.axis_index("y")

        def of_chip(candidates):
            value = candidates[0]
            for j in range(1, N_CHIPS):
                value = jnp.where(me == j, candidates[j], value)
            return value

        for i, k in enumerate(names):
            (rows, width), _, (local_rows, local_width) = SMALL_PACKING[k]
            r0 = first_row[k]
            if k == "o_sgu_b":
                g = total[r0:r0 + 4, 0:CHUNK]
            elif k == "e_pool_w":
                for grp in range(4):
                    src = pl.ds(pl.multiple_of(r0 + grp * POOL_CH + me * 64, 8), 64)
                    dst = slice(grp * 64, (grp + 1) * 64)
                    _adamw_rows(total[src, :], i, dst, w_refs, m_refs, v_refs, outs, n)
                continue
            elif k == "o_conv_w":
                g = total[pl.ds(pl.multiple_of(r0 + me * 32, 8), 32), :][0:CONV_K]
            elif width < LANES:
                g = total[r0:r0 + rows, 0:width]
            else:
                lanes = [total[r0 + j:r0 + j + 1, :] for j in range(width // LANES)]
                per_chip = local_width // LANES
                if local_width == width:
                    g = jnp.concatenate(lanes, axis=1)
                elif per_chip == 1:
                    g = of_chip(lanes)
                else:
                    g = of_chip([jnp.concatenate(lanes[j * per_chip:(j + 1) * per_chip], axis=1) for j in range(N_CHIPS)])
            _adamw_rows(g, i, slice(None), w_refs, m_refs, v_refs, outs, n)

    shard_shapes = [SMALL_PACKING[k][2] for k in names]
    out = pl.pallas_call(update, name="small_update", in_specs=[whole] * (3 * n + 1), out_specs=[whole] * (4 * n),
                         out_shape=[SDS(s, f32) for s in shard_shapes] * 4)(total, *ws, *ms, *vs)
    return out[:n], out[n:2 * n], out[2 * n:3 * n], out[3 * n:]


def _adamw_rows(g, i, rows, w_refs, m_refs, v_refs, outs, n):
    w, m, v = w_refs[i][rows, :], m_refs[i][rows, :], v_refs[i][rows, :]
    nm = ADAM_B1 * m + (1.0 - ADAM_B1) * g
    nv = ADAM_B2 * v + (1.0 - ADAM_B2) * (g * g)
    m_hat = nm / (1.0 - ADAM_B1 ** ADAM_STEP)
    v_hat = nv / (1.0 - ADAM_B2 ** ADAM_STEP)
    outs[i][rows, :] = g
    outs[n + i][rows, :] = -ADAM_LR * (m_hat / (jnp.sqrt(v_hat) + ADAM_EPS) + ADAM_WD * w)
    outs[2 * n + i][rows, :] = nm
    outs[3 * n + i][rows, :] = nv


def _pack(arrays, total_rows=None):
    parts = []
    rows = 0
    for a in arrays:
        flat = a.reshape(-1, LANES)
        pad = -flat.shape[0] % 8
        parts.append(jnp.pad(flat, ((0, pad), (0, 0))))
        rows += flat.shape[0] + pad
    if total_rows is not None:
        parts.append(jnp.zeros((total_rows - rows, LANES), arrays[0].dtype))
    return jnp.concatenate(parts, axis=0)


def _unpack(buf, shapes):
    out = []
    row = 0
    lead = buf.shape[:-2]
    for shape in shapes:
        size = 1
        for s in shape:
            size *= s
        rows = size // LANES
        out.append(buf[..., row:row + rows, :].reshape(lead + tuple(shape)))
        row += rows + (-rows % 8)
    return out


BIG = ("e_w_in", "e_w_out", "o_w_in", "o_w_out")
SHARDED_SMALL = {
    "e_pool_w": ((4, 64, 256), 1), "o_pre_norm": ((512,), 0), "o_sgu_norm_g": ((256,), 0), "o_sgu_norm_b": ((256,), 0),
    "o_conv_w": ((31, 256), 1), "o_conv_b": ((256,), 0), "o_conv_norm_g": ((256,), 0), "o_conv_norm_b": ((256,), 0),
    "o_post_norm": ((512,), 0),
}
SMALL_ORDER = ("e_pre_norm", "e_pool_w", "e_pool_scale", "e_post_norm", "o_pre_norm", "o_sgu_norm_g", "o_sgu_norm_b",
               "o_sgu_w", "o_sgu_b", "o_conv_w", "o_conv_b", "o_conv_norm_g", "o_conv_norm_b", "o_post_norm")
ALL_ORDER = ("e_pre_norm", "e_w_in", "e_pool_w", "e_pool_scale", "e_w_out", "e_post_norm", "o_pre_norm", "o_w_in",
             "o_sgu_norm_g", "o_sgu_norm_b", "o_sgu_w", "o_sgu_b", "o_conv_w", "o_conv_b", "o_conv_norm_g",
             "o_conv_norm_b", "o_w_out", "o_post_norm")


def _full_shape(name):
    shape, axis = SHARDED_SMALL[name]
    return tuple(s * N_CHIPS if i == axis else s for i, s in enumerate(shape))


def _from_chips(name, stacked):
    shape, axis = SHARDED_SMALL[name]
    return jnp.moveaxis(stacked, 0, axis).reshape(_full_shape(name))


def kernel(x, e_pre_norm, e_w_in, e_pool_w, e_pool_scale, e_w_out, e_post_norm, o_pre_norm, o_w_in, o_sgu_norm_g, o_sgu_norm_b, o_sgu_w, o_sgu_b, o_conv_w, o_conv_b, o_conv_norm_g, o_conv_norm_b, o_w_out, o_post_norm, loss_target, m_e_pre_norm, m_e_w_in, m_e_pool_w, m_e_pool_scale, m_e_w_out, m_e_post_norm, m_o_pre_norm, m_o_w_in, m_o_sgu_norm_g, m_o_sgu_norm_b, m_o_sgu_w, m_o_sgu_b, m_o_conv_w, m_o_conv_b, m_o_conv_norm_g, m_o_conv_norm_b, m_o_w_out, m_o_post_norm, v_e_pre_norm, v_e_w_in, v_e_pool_w, v_e_pool_scale, v_e_w_out, v_e_post_norm, v_o_pre_norm, v_o_w_in, v_o_sgu_norm_g, v_o_sgu_norm_b, v_o_sgu_w, v_o_sgu_b, v_o_conv_w, v_o_conv_b, v_o_conv_norm_g, v_o_conv_norm_b, v_o_w_out, v_o_post_norm):
    w = dict(e_pre_norm=e_pre_norm, e_w_in=e_w_in, e_pool_w=e_pool_w, e_pool_scale=e_pool_scale, e_w_out=e_w_out,
             e_post_norm=e_post_norm, o_pre_norm=o_pre_norm, o_w_in=o_w_in, o_sgu_norm_g=o_sgu_norm_g,
             o_sgu_norm_b=o_sgu_norm_b, o_sgu_w=o_sgu_w, o_sgu_b=o_sgu_b, o_conv_w=o_conv_w, o_conv_b=o_conv_b,
             o_conv_norm_g=o_conv_norm_g, o_conv_norm_b=o_conv_norm_b, o_w_out=o_w_out, o_post_norm=o_post_norm)
    m = dict(e_pre_norm=m_e_pre_norm, e_w_in=m_e_w_in, e_pool_w=m_e_pool_w, e_pool_scale=m_e_pool_scale,
             e_w_out=m_e_w_out, e_post_norm=m_e_post_norm, o_pre_norm=m_o_pre_norm, o_w_in=m_o_w_in,
             o_sgu_norm_g=m_o_sgu_norm_g, o_sgu_norm_b=m_o_sgu_norm_b, o_sgu_w=m_o_sgu_w, o_sgu_b=m_o_sgu_b,
             o_conv_w=m_o_conv_w, o_conv_b=m_o_conv_b, o_conv_norm_g=m_o_conv_norm_g, o_conv_norm_b=m_o_conv_norm_b,
             o_w_out=m_o_w_out, o_post_norm=m_o_post_norm)
    v = dict(e_pre_norm=v_e_pre_norm, e_w_in=v_e_w_in, e_pool_w=v_e_pool_w, e_pool_scale=v_e_pool_scale,
             e_w_out=v_e_w_out, e_post_norm=v_e_post_norm, o_pre_norm=v_o_pre_norm, o_w_in=v_o_w_in,
             o_sgu_norm_g=v_o_sgu_norm_g, o_sgu_norm_b=v_o_sgu_norm_b, o_sgu_w=v_o_sgu_w, o_sgu_b=v_o_sgu_b,
             o_conv_w=v_o_conv_w, o_conv_b=v_o_conv_b, o_conv_norm_g=v_o_conv_norm_g, o_conv_norm_b=v_o_conv_norm_b,
             o_w_out=v_o_w_out, o_post_norm=v_o_post_norm)
    w, m, v = ({k: a[0] for k, a in d.items()} for d in (w, m, v))
    chip = 2 * lax.axis_index("x") + lax.axis_index("y")

    loss, grad_x, in_flight, small = _step(x[0], loss_target[0], w, chip)

    grads, delta, new_m, new_v = {}, {}, {}, {}

    def rows_of(a):
        return a.reshape(-1, a.shape[-1])

    after = grad_x
    for k in ("o_w_out", "o_w_in", "e_w_out", "small", "e_w_in"):
        if k == "small":
            small_grads = [small[name].reshape(SMALL_PACKING[name][0]) for name in SMALL_ORDER]
            updates = _small_finalize(small_grads, *[[rows_of(d[name]) for name in SMALL_ORDER] for d in (w, m, v)], after)
            for d, arrays in zip((grads, delta, new_m, new_v), updates):
                for name, a in zip(SMALL_ORDER, arrays):
                    d[name] = a.reshape(w[name].shape)
            after = updates[1][0]
            continue
        grads[k], delta[k], new_m[k], new_v[k] = _adamw(w[k], _land(in_flight, k, chip, after), m[k], v[k], f"adamw_{k}")
        after = delta[k]
    loss = lax.psum(loss[0, 0], ("x", "y", "c"))

    outs = [loss, grad_x[None]]
    for d in (grads, delta, new_m, new_v):
        outs += [d[k][None] for k in ALL_ORDER]
    return tuple(outs)
```

```python
import jax
import jax.numpy as jnp
from jax import lax
from jax.experimental import pallas as pl
from jax.experimental.pallas import tpu as pltpu

f32 = jnp.float32
bf16 = jnp.bfloat16
SDS = jax.ShapeDtypeStruct

SEQ = 2048
D_MODEL = 2048
EPS = 1e-6
NEG = -1e30
HEAD_DIM = 128
ROT_HALF = 16
ROPE_THETA = 500000.0
DILATIONS = (1, 4, 16)
SPAN = 128
N_HEADS = 8
HALF = 1024
POOL_CH = 256
CONV_K = 31
CONV_PAD = 32
CHUNK = 128
N_CHIPS = 4
LANES = 256
E_IN_PIECES = 3
SMALL_SHARD_ROWS = 352
ANY = pl.BlockSpec(memory_space=pl.ANY)
MESH = pl.DeviceIdType.MESH

ADAM_LR = 0.001
ADAM_B1 = 0.9
ADAM_B2 = 0.999
ADAM_EPS = 1e-08
ADAM_WD = 0.01
ADAM_STEP = 10


def _dot(a, b):
    return jnp.dot(a, b, preferred_element_type=f32)


def _dot_nt(a, b):
    return lax.dot_general(a, b, (((1,), (1,)), ((), ())), preferred_element_type=f32)


def _dot_tn(a, b):
    return lax.dot_general(a, b, (((0,), (0,)), ((), ())), preferred_element_type=f32)


def _sigmoid(x):
    return 1.0 / (1.0 + jnp.exp(-x))


def _silu_and_grad(x):
    s = _sigmoid(x)
    return x * s, s * (1.0 + x * (1.0 - s))


def _rms_fwd(x, g):
    r = lax.rsqrt(jnp.mean(x * x, axis=-1, keepdims=True) + EPS)
    return x * r * g


def _rms_bwd(x, g, dout):
    r = lax.rsqrt(jnp.mean(x * x, axis=-1, keepdims=True) + EPS)
    xh = x * r
    dg = jnp.sum(dout * xh, axis=0, keepdims=True)
    dxh = dout * g
    dx = r * (dxh - xh * jnp.mean(dxh * xh, axis=-1, keepdims=True))
    return dx, dg


def _ln_stats(x):
    mu = jnp.mean(x, axis=-1, keepdims=True)
    xc = x - mu
    rstd = lax.rsqrt(jnp.mean(xc * xc, axis=-1, keepdims=True) + EPS)
    return xc * rstd, rstd


def _ln_bwd(xh, rstd, g, dout):
    dg = jnp.sum(dout * xh, axis=0, keepdims=True)
    db = jnp.sum(dout, axis=0, keepdims=True)
    dxh = dout * g
    dx = rstd * (dxh - jnp.mean(dxh, axis=-1, keepdims=True) - xh * jnp.mean(dxh * xh, axis=-1, keepdims=True))
    return dx, dg, db


def _accumulate(ref, value, first):
    @pl.when(first)
    def _():
        ref[...] = value

    @pl.when(jnp.logical_not(first))
    def _():
        ref[...] += value


def _write_behind(step, steps, tiles, sems, window):
    slot = step % 2

    def copies(s, at):
        return [pltpu.make_async_copy(tile.at[s], window(t, at), sems.at[2 * t + s]) for t, tile in enumerate(tiles)]

    @pl.when(step >= 2)
    def _():
        for cp in copies(slot, step - 2):
            cp.wait()

    def full():
        for cp in copies(slot, step):
            cp.start()

        @pl.when(step == steps - 1)
        def _():
            for cp in copies(slot, step):
                cp.wait()
            if steps > 1:
                for cp in copies(1 - slot, step - 1):
                    cp.wait()

    return [tile.at[slot] for tile in tiles], full


def _columns(ref, first, width):
    return ref.at[:, pl.ds(pl.multiple_of(first, 128), width)]


def _col_tile(ns):
    for t in (1024, 768, 512, 256):
        if ns % t == 0:
            return t
    raise ValueError(ns)


def _mm_nn(a, w, out_dtype, name, piece=0, pieces=1, into=None):
    m, k = a.shape
    j, _, ns = w.shape
    tm, tn = m, _col_tile(ns)
    nb = ns // tn

    def body(a_ref, w_ref, *rest):
        rest[-1][...] = _dot(a_ref[...], w_ref[...]).astype(rest[-1].dtype)

    return pl.pallas_call(
        body, name=name, grid=(j * nb, m // tm),
        in_specs=[pl.BlockSpec((tm, k), lambda n, i: (i, 0)),
                  pl.BlockSpec((None, k, tn), lambda n, i: (n // nb, 0, n % nb))] + ([] if into is None else [ANY]),
        out_specs=pl.BlockSpec((tm, tn), lambda n, i: (i, ((n // nb) * pieces + piece) * nb + n % nb)),
        out_shape=SDS((m, j * ns * pieces), out_dtype),
        input_output_aliases={} if into is None else {2: 0},
    )(a, w, *([] if into is None else [into]))


def _mm_nt(dz, ws, name, after):
    m, _ = dz.shape
    pieces = len(ws)
    j, k, ns = ws[0].shape
    tm, tk = 1024, 1024

    def body(dz_ref, *rest):
        w_refs, o_ref = rest[:pieces], rest[pieces + 1]
        total = _dot_nt(dz_ref[:, 0:ns], w_refs[0][...])
        for q in range(1, pieces):
            total = total + _dot_nt(dz_ref[:, q * ns:(q + 1) * ns], w_refs[q][...])
        if j == 1:
            o_ref[...] = total.astype(bf16)
            return
        sum_ref = rest[pieces + 2]
        r = pl.program_id(2)
        _accumulate(sum_ref, total, r == 0)

        @pl.when(r == j - 1)
        def _():
            o_ref[...] = sum_ref[...].astype(bf16)

    return pl.pallas_call(
        body, name=name, grid=(m // tm, k // tk, j),
        in_specs=[pl.BlockSpec((tm, pieces * ns), lambda i, kk, r: (i, r))]
        + [pl.BlockSpec((None, tk, ns), lambda i, kk, r: (r, kk, 0))] * pieces + [ANY],
        out_specs=pl.BlockSpec((tm, tk), lambda i, kk, r: (i, kk)),
        out_shape=SDS((m, k), bf16), scratch_shapes=[] if j == 1 else [pltpu.VMEM((tm, tk), f32)],
    )(dz, *ws, after)


def _mm_tn(a, dz, j, name):
    m, k = a.shape
    ns = dz.shape[1] // j
    tk, tn = 1024, _col_tile(ns)
    nb = ns // tn

    def body(a_ref, dz_ref, o_ref):
        o_ref[...] = _dot_tn(a_ref[...], dz_ref[...]).astype(o_ref.dtype)

    return pl.pallas_call(
        body, name=name, grid=(k // tk, j * nb),
        in_specs=[pl.BlockSpec((m, tk), lambda kk, n: (0, kk)),
                  pl.BlockSpec((m, tn), lambda kk, n: (0, n))],
        out_specs=pl.BlockSpec((None, tk, tn), lambda kk, n: (n // nb, kk, n % nb)),
        out_shape=SDS((j, k, ns), bf16),
    )(a, dz)


ROWS = 256


def _row_spec(width=D_MODEL, col=0):
    return pl.BlockSpec((ROWS, width), lambda i: (i, col))


def _vec_spec(width=D_MODEL):
    return pl.BlockSpec((1, width), lambda i: (0, 0))


def _pre_norm(x, g):
    def body(x_ref, g_ref, h_ref):
        h_ref[...] = _rms_fwd(x_ref[...], g_ref[...]).astype(bf16)

    return pl.pallas_call(
        body, name="pre_norm", grid=(SEQ // ROWS,), in_specs=[_row_spec(), _vec_spec()],
        out_specs=_row_spec(), out_shape=SDS((SEQ, D_MODEL), bf16))(x, g)


def _mid_norm(x, y, g_post, g_pre):
    def body(x_ref, y_ref, gpost_ref, gpre_ref, x1_ref, h1_ref):
        x1 = x_ref[...] + _rms_fwd(y_ref[...], gpost_ref[...])
        x1_ref[...] = x1
        h1_ref[...] = _rms_fwd(x1, gpre_ref[...]).astype(bf16)

    return pl.pallas_call(
        body, name="mid_norm", grid=(SEQ // ROWS,),
        in_specs=[_row_spec(), _row_spec(), _vec_spec(), _vec_spec()],
        out_specs=[_row_spec(), _row_spec()],
        out_shape=[SDS((SEQ, D_MODEL), f32), SDS((SEQ, D_MODEL), bf16)])(x, y, g_post, g_pre)


def _final_norm_loss(x1, y, g_post, target):
    def body(x1_ref, y_ref, g_ref, t_ref, loss_ref, dx2_ref, dy_ref, dg_ref):
        first = pl.program_id(0) == 0
        y = y_ref[...]
        g = g_ref[...]
        err = x1_ref[...] + _rms_fwd(y, g) - t_ref[...]
        sq = jnp.sum(jnp.sum(err * err, axis=1, keepdims=True), axis=0, keepdims=True)
        _accumulate(loss_ref, sq * (0.5 / D_MODEL), first)
        dx2 = err * (1.0 / D_MODEL)
        dx2_ref[...] = dx2
        dy, dg = _rms_bwd(y, g, dx2)
        dy_ref[...] = dy.astype(bf16)
        _accumulate(dg_ref, dg, first)

    return pl.pallas_call(
        body, name="final_norm_loss", grid=(SEQ // ROWS,),
        in_specs=[_row_spec(), _row_spec(), _vec_spec(), _row_spec()],
        out_specs=[pl.BlockSpec((1, 1), lambda i: (0, 0)), _row_spec(), _row_spec(), _vec_spec()],
        out_shape=[SDS((1, 1), f32), SDS((SEQ, D_MODEL), f32), SDS((SEQ, D_MODEL), bf16), SDS((1, D_MODEL), f32)],
    )(x1, y, g_post, target)


def _mid_norm_bwd(dx2, dh1, x1, y0, g_pre, g_post):
    def body(dx2_ref, dh1_ref, x1_ref, y0_ref, gpre_ref, gpost_ref, dx1_ref, dy0_ref, dgpre_ref, dgpost_ref):
        first = pl.program_id(0) == 0
        d_in, dgpre = _rms_bwd(x1_ref[...], gpre_ref[...], dh1_ref[...])
        dx1 = dx2_ref[...] + d_in
        dx1_ref[...] = dx1
        dy0, dgpost = _rms_bwd(y0_ref[...], gpost_ref[...], dx1)
        dy0_ref[...] = dy0.astype(bf16)
        _accumulate(dgpre_ref, dgpre, first)
        _accumulate(dgpost_ref, dgpost, first)

    return pl.pallas_call(
        body, name="mid_norm_bwd", grid=(SEQ // ROWS,),
        in_specs=[_row_spec(), _row_spec(), _row_spec(), _row_spec(), _vec_spec(), _vec_spec()],
        out_specs=[_row_spec(), _row_spec(), _vec_spec(), _vec_spec()],
        out_shape=[SDS((SEQ, D_MODEL), f32), SDS((SEQ, D_MODEL), bf16), SDS((1, D_MODEL), f32), SDS((1, D_MODEL), f32)],
    )(dx2, dh1, x1, y0, g_pre, g_post)


def _pre_norm_bwd(dx1, dh0, x, g):
    def body(dx1_ref, dh0_ref, x_ref, g_ref, dx_ref, dg_ref):
        d_in, dg = _rms_bwd(x_ref[...], g_ref[...], dh0_ref[...])
        dx_ref[...] = dx1_ref[...] + d_in
        _accumulate(dg_ref, dg, pl.program_id(0) == 0)

    return pl.pallas_call(
        body, name="pre_norm_bwd", grid=(SEQ // ROWS,),
        in_specs=[_row_spec(), _row_spec(), _row_spec(), _vec_spec()],
        out_specs=[_row_spec(), _vec_spec()],
        out_shape=[SDS((SEQ, D_MODEL), f32), SDS((1, D_MODEL), f32)])(dx1, dh0, x, g)


def _pool_count(g):
    row = lax.broadcasted_iota(jnp.int32, (SEQ, 1), 0)
    width = jnp.left_shift(2, g)
    return row, width, jnp.minimum(row + 1, width).astype(f32)


def _trailing_sum(x, row, width):
    s = x
    for k in (1, 2, 4, 8):
        shifted = jnp.where(row >= k, pltpu.roll(s, k, 0), 0.0)
        s = jnp.where(width > k, s + shifted, s)
    return s


def _leading_sum(x, row, width):
    s = x
    for k in (1, 2, 4, 8):
        shifted = jnp.where(row < SEQ - k, pltpu.roll(s, SEQ - k, 0), 0.0)
        s = jnp.where(width > k, s + shifted, s)
    return s


def _pool_specs():
    a_in = pl.BlockSpec((SEQ, POOL_CH), lambda g: (0, g))
    a_gate = pl.BlockSpec((SEQ, POOL_CH), lambda g: (0, 4 + g))
    w = pl.BlockSpec((None, POOL_CH, POOL_CH), lambda g: (g, 0, 0))
    scale = pl.BlockSpec((1, POOL_CH), lambda g: (0, g))
    return a_in, a_gate, w, scale


def _pool_fwd(z0, pool_w, pool_scale):
    def body(a_ref, gate_ref, w_ref, scale_ref, ya_ref):
        row, width, count = _pool_count(pl.program_id(0))
        a = a_ref[...]
        pooled = _trailing_sum(a, row, width) / count - a
        mixed = _dot(pooled.astype(bf16), w_ref[...]) * scale_ref[...]
        gate = gate_ref[...]
        ya_ref[...] = (mixed * gate * _sigmoid(gate)).astype(bf16)

    return pl.pallas_call(
        body, name="pool_fwd", grid=(4,), in_specs=list(_pool_specs()),
        out_specs=pl.BlockSpec((SEQ, POOL_CH), lambda g: (0, g)),
        out_shape=SDS((SEQ, 2 * HALF), bf16))(z0, z0, pool_w, pool_scale)


def _pool_bwd(z0, dcat, pool_w, pool_scale):
    def body(a_ref, gate_ref, w_ref, scale_ref, dya_ref, dz_ref, dw_ref, dscale_ref, da_tiles, dgate_tiles, sems):
        g = pl.program_id(0)
        (da_ref, dgate_ref), full = _write_behind(
            g, 4, [da_tiles, dgate_tiles], sems, lambda t, at: _columns(dz_ref, t * HALF + at * POOL_CH, POOL_CH))
        row, width, count = _pool_count(g)
        a = a_ref[...]
        pooled = (_trailing_sum(a, row, width) / count - a).astype(bf16)
        w = w_ref[...]
        scale = scale_ref[...]
        mixed = _dot(pooled, w)
        silu, dsilu = _silu_and_grad(gate_ref[...])
        dya = dya_ref[...]
        dgate_ref[...] = (dya * mixed * scale * dsilu).astype(bf16)
        dms = dya * silu
        dscale_ref[...] = jnp.sum(dms * mixed, axis=0, keepdims=True)
        dmixed = (dms * scale).astype(bf16)
        dw_ref[...] = _dot_tn(pooled, dmixed)
        dpooled = _dot_nt(dmixed, w)
        da_ref[...] = (_leading_sum(dpooled / count, row, width) - dpooled).astype(bf16)
        full()

    a_in, a_gate, w, scale = _pool_specs()
    col = pl.BlockSpec((SEQ, POOL_CH), lambda g: (0, g))
    tiles = pltpu.VMEM((2, SEQ, POOL_CH), bf16)
    return pl.pallas_call(
        body, name="pool_bwd", grid=(4,), in_specs=[a_in, a_gate, w, scale, col],
        out_specs=[ANY, w, scale],
        out_shape=[SDS((SEQ, 6 * D_MODEL), bf16), SDS((4, POOL_CH, POOL_CH), f32), SDS((1, HALF), f32)],
        scratch_shapes=[tiles, tiles, pltpu.SemaphoreType.DMA((4,))],
    )(z0, z0, pool_w, pool_scale, dcat)


Q_COL, K_COL, V_COL, BGATE_COL = 16, 40, 64, 88


def _rope_tables():
    pos = jnp.arange(SEQ, dtype=f32)
    inv_freq = jnp.power(ROPE_THETA, -jnp.arange(0, 2 * ROT_HALF, 2, dtype=f32) / (2 * ROT_HALF))
    ang = pos[:, None] * inv_freq[None, :]
    cos, sin = jnp.cos(ang), jnp.sin(ang)
    zeros = jnp.zeros((SEQ, HEAD_DIM - 2 * ROT_HALF), f32)
    cos_t = jnp.concatenate([cos, cos, zeros + 1.0], axis=1)
    sin_t = jnp.concatenate([sin, sin, zeros], axis=1)
    j = jnp.arange(HEAD_DIM)[:, None]
    i = jnp.arange(HEAD_DIM)[None, :]
    rot = jnp.where((i < ROT_HALF) & (j == i + ROT_HALF), -1.0, 0.0) + jnp.where(
        (i >= ROT_HALF) & (i < 2 * ROT_HALF) & (j == i - ROT_HALF), 1.0, 0.0)
    return cos_t, sin_t, rot.astype(bf16), rot.T.astype(bf16)


def _exact_dot(t, m):
    hi = t.astype(bf16)
    lo = (t - hi.astype(f32)).astype(bf16)
    return _dot(hi, m) + _dot(lo, m)


def _rope(t, cos_t, sin_t, rot):
    return t * cos_t + _exact_dot(t, rot) * sin_t


def _rope_transposed(d, cos_t, sin_t, rot_t):
    return d * cos_t + _exact_dot(d * sin_t, rot_t)


ROW_CHUNK = 256


def _chunks(fn):
    def step(i, carry):
        fn(pl.multiple_of(i * ROW_CHUNK, ROW_CHUNK))
        return carry

    lax.fori_loop(0, SEQ // ROW_CHUNK, step, 0, unroll=2)


def _pieces(dilation):
    length = SEQ // dilation
    n = min(length, ROW_CHUNK)
    return [(r, l0, n) for r in range(dilation) for l0 in range(0, length, n)]


def _by_residue(dst_ref, src_ref, dilation, dtype):
    length = SEQ // dilation
    for r, l0, n in _pieces(dilation):
        src = src_ref[l0:l0 + n, :] if dilation == 1 else src_ref[pl.ds(r + dilation * l0, n, stride=dilation), :]
        start = r * length + l0
        dst_ref[start:start + n, :] = src.astype(dtype)


def _by_position(dst_ref, src_ref, dilation):
    length = SEQ // dilation
    for r, l0, n in _pieces(dilation):
        src = src_ref[r * length + l0:r * length + l0 + n, :]
        if dilation == 1:
            dst_ref[l0:l0 + n, :] = src
        else:
            dst_ref[pl.ds(r + dilation * l0, n, stride=dilation), :] = src


def _attn_masks():
    qi = lax.broadcasted_iota(jnp.int32, (SPAN, 2 * SPAN), 0)
    kj = lax.broadcasted_iota(jnp.int32, (SPAN, 2 * SPAN), 1)
    window = ((kj < SPAN) & (kj >= qi)) | ((kj >= SPAN) & (kj - SPAN <= qi))
    own = lax.broadcasted_iota(jnp.int32, (SPAN, SPAN), 1) <= lax.broadcasted_iota(jnp.int32, (SPAN, SPAN), 0)
    return window, own


def _attn_blocks(dilation):
    per_residue = SEQ // dilation // SPAN
    blocks = [(c, c % per_residue != 0) for c in range(SEQ // SPAN)]
    return [blocks[i:i + 4] for i in range(0, len(blocks), 4)]


def _block_keys(c, has_prev):
    return slice((c - 1) * SPAN if has_prev else c * SPAN, (c + 1) * SPAN)


def _head_spec(col):
    return pl.BlockSpec((SEQ, HEAD_DIM), lambda h: (0, col + h))


def _table_spec():
    return pl.BlockSpec((SEQ, HEAD_DIM), lambda h: (0, 0))


def _attn_fwd(z0, tables, mixed):
    scale = HEAD_DIM ** -0.5

    def body(*refs):
        qkv = refs[0:9]
        bg_ref, cos_ref, sin_ref, rot_ref = refs[9:13]
        yb_ref, att_ref, lse_ref = refs[14:17]
        saved = refs[17:26]
        tmp_q, tmp_k, v_ones, o_res, l_res, o_nat, l_nat = refs[26:33]
        window_mask, own_mask = _attn_masks()
        rot = rot_ref[...]

        @pl.when(pl.program_id(0) == 0)
        def _():
            v_ones[:, HEAD_DIM:] = jnp.ones((SEQ, HEAD_DIM), bf16)

        for g, dilation in enumerate(DILATIONS):
            q_ref, k_ref, v_ref = qkv[3 * g:3 * g + 3]
            qd, kd, vd = saved[3 * g:3 * g + 3]

            def rope_rows(start, q_ref=q_ref, k_ref=k_ref):
                r = pl.ds(start, ROW_CHUNK)
                cos_t, sin_t = cos_ref[r, :], sin_ref[r, :]
                tmp_q[r, :] = _rope(q_ref[r, :], cos_t, sin_t, rot) * scale
                tmp_k[r, :] = _rope(k_ref[r, :], cos_t, sin_t, rot)

            _chunks(rope_rows)
            _by_residue(qd, tmp_q, dilation, bf16)
            _by_residue(kd, tmp_k, dilation, bf16)
            _by_residue(vd, v_ref, dilation, bf16)
            for l0 in range(0, SEQ, ROW_CHUNK):
                v_ones[l0:l0 + ROW_CHUNK, 0:HEAD_DIM] = vd[l0:l0 + ROW_CHUNK, :]

            for four in _attn_blocks(dilation):
                scores = [_dot_nt(qd[c * SPAN:(c + 1) * SPAN, :], kd[_block_keys(c, prev), :]) for c, prev in four]
                tops, probs = [], []
                for (c, prev), s in zip(four, scores):
                    s = jnp.where(window_mask if prev else own_mask, s, NEG)
                    tops.append(jnp.max(s, axis=1, keepdims=True))
                    probs.append(jnp.exp(s - tops[-1]).astype(bf16))
                sums = [_dot(p, v_ones[_block_keys(c, prev), :]) for (c, prev), p in zip(four, probs)]
                for (c, prev), m, o in zip(four, tops, sums):
                    den = o[:, HEAD_DIM:]
                    o_res[c * SPAN:(c + 1) * SPAN, :] = o[:, :HEAD_DIM] / den
                    l_res[c * SPAN:(c + 1) * SPAN, :] = m + jnp.log(den)

            if dilation > 1:
                _by_position(o_nat, o_res, dilation)
                _by_position(l_nat, l_res, dilation)
            o_g, l_g = (o_res, l_res) if dilation == 1 else (o_nat, l_nat)

            def merge(start, g=g, o_g=o_g, l_g=l_g):
                r = pl.ds(start, ROW_CHUNK)
                if g == 0:
                    att, total = o_g[r, :], l_g[r, :]
                else:
                    l_old, l_new = lse_ref[r, :], l_g[r, :]
                    top = jnp.maximum(l_old, l_new)
                    total = top + jnp.log(jnp.exp(l_old - top) + jnp.exp(l_new - top))
                    att = att_ref[r, :] * jnp.exp(l_old - total) + o_g[r, :] * jnp.exp(l_new - total)
                att_ref[r, :] = att
                lse_ref[r, :] = total
                if g == len(DILATIONS) - 1:
                    gate = bg_ref[r, :]
                    yb_ref[r, :] = (att * gate * _sigmoid(gate)).astype(bf16)

            _chunks(merge)

    in_specs = []
    for g in range(3):
        in_specs += [_head_spec(Q_COL + 8 * g), _head_spec(K_COL + 8 * g), _head_spec(V_COL + 8 * g)]
    in_specs += [_head_spec(BGATE_COL), _table_spec(), _table_spec(), pl.BlockSpec((HEAD_DIM, HEAD_DIM), lambda h: (0, 0)), ANY]
    out_spec = pl.BlockSpec((SEQ, HEAD_DIM), lambda h: (0, h))
    right_half = pl.BlockSpec((SEQ, HEAD_DIM), lambda h: (0, N_HEADS + h))
    vm = lambda dt: pltpu.VMEM((SEQ, HEAD_DIM), dt)
    cos_t, sin_t, rot, _ = tables
    out = pl.pallas_call(
        body, name="attn_fwd", grid=(N_HEADS,), in_specs=in_specs, out_specs=[right_half] + [out_spec] * 11,
        out_shape=[SDS((SEQ, 2 * HALF), bf16), SDS((SEQ, HALF), f32), SDS((SEQ, HALF), f32)] + [SDS((SEQ, HALF), bf16)] * 9,
        scratch_shapes=[vm(f32), vm(f32), pltpu.VMEM((SEQ, 2 * HEAD_DIM), bf16), vm(f32), vm(f32), vm(f32), vm(f32)],
        input_output_aliases={13: 0},
    )(*([z0] * 10), cos_t, sin_t, rot, mixed)
    return out[0], out[1], out[2], [tuple(out[3 + 3 * g:6 + 3 * g]) for g in range(3)]


def _attn_bwd_group(g, saved, z0, att, lse, dcat, tables, dz):
    scale = HEAD_DIM ** -0.5
    dilation = DILATIONS[g]
    with_gate = g == 0
    n_out = 4 if with_gate else 3
    first_col = (Q_COL + 8 * g, K_COL + 8 * g, V_COL + 8 * g, BGATE_COL)

    def body(*refs):
        qd, kd, vd, bg_ref, att_ref, lse_ref, dyb_ref, cos_ref, sin_ref, rot_t_ref = refs[0:10]
        dz_ref = refs[11]
        dod, ld, dd, tmp, aq, ak, av = refs[12:19]
        views, full = _write_behind(pl.program_id(0), N_HEADS, refs[19:19 + n_out], refs[19 + n_out],
                                    lambda t, at: _columns(dz_ref, (first_col[t] + at) * HEAD_DIM, HEAD_DIM))
        dq_ref, dk_ref, dv_ref = views[0:3]
        window_mask, own_mask = _attn_masks()
        rot_t = rot_t_ref[...]

        def gate_rows(start):
            r = pl.ds(start, ROW_CHUNK)
            silu, dsilu = _silu_and_grad(bg_ref[r, :])
            att_v = att_ref[r, :]
            dyb = dyb_ref[r, :]
            if with_gate:
                views[3][r, :] = (dyb * att_v * dsilu).astype(bf16)
            datt = dyb * silu
            tmp[r, :] = datt
            aq[r, :] = jnp.broadcast_to(jnp.sum(datt * att_v, axis=1, keepdims=True), (ROW_CHUNK, HEAD_DIM))

        _chunks(gate_rows)
        _by_residue(dod, tmp, dilation, bf16)
        _by_residue(dd, aq, dilation, f32)
        _by_residue(ld, lse_ref, dilation, f32)

        for four in _attn_blocks(dilation):
            rows = [slice(c * SPAN, (c + 1) * SPAN) for c, _ in four]
            keys = [_block_keys(c, prev) for c, prev in four]
            scores = [_dot_nt(qd[r, :], kd[k, :]) for r, k in zip(rows, keys)]
            dprobs = [_dot_nt(dod[r, :], vd[k, :]) for r, k in zip(rows, keys)]
            probs, dscores = [], []
            for (c, prev), r, s, dp in zip(four, rows, scores, dprobs):
                lse_q, delta = ld[r, :], dd[r, :]
                if prev:
                    lse_q = jnp.concatenate([lse_q, lse_q], axis=1)
                    delta = jnp.concatenate([delta, delta], axis=1)
                p = jnp.where(window_mask if prev else own_mask, jnp.exp(s - lse_q), 0.0)
                probs.append(p.astype(bf16))
                dscores.append((p * (dp - delta)).astype(bf16))
            dvs = [_dot_tn(p, dod[r, :]) for p, r in zip(probs, rows)]
            dks = [_dot_tn(ds, qd[r, :]) for ds, r in zip(dscores, rows)]
            dqs = [_dot(ds, kd[k, :]) for ds, k in zip(dscores, keys)]
            for (c, prev), r, dv, dk, dq in zip(four, rows, dvs, dks, dqs):
                aq[r, :] = dq
                if prev:
                    before = slice((c - 1) * SPAN, c * SPAN)
                    av[before, :] += dv[0:SPAN]
                    ak[before, :] += dk[0:SPAN]
                    av[r, :] = dv[SPAN:]
                    ak[r, :] = dk[SPAN:]
                else:
                    av[r, :] = dv
                    ak[r, :] = dk

        def finish(out_ref, acc, factor, roped):
            if dilation > 1:
                _by_position(tmp, acc, dilation)
            src = acc if dilation == 1 else tmp

            def rows(start):
                r = pl.ds(start, ROW_CHUNK)
                d = src[r, :]
                if factor != 1.0:
                    d = d * factor
                if roped:
                    d = _rope_transposed(d, cos_ref[r, :], sin_ref[r, :], rot_t)
                out_ref[r, :] = d.astype(bf16)

            _chunks(rows)

        finish(dq_ref, aq, scale, True)
        finish(dk_ref, ak, 1.0, True)
        finish(dv_ref, av, 1.0, False)
        full()

    head = pl.BlockSpec((SEQ, HEAD_DIM), lambda h: (0, h))
    in_specs = [head, head, head, _head_spec(BGATE_COL), head, head, _head_spec(8), _table_spec(), _table_spec(),
                pl.BlockSpec((HEAD_DIM, HEAD_DIM), lambda h: (0, 0)), ANY]
    vm = lambda dt: pltpu.VMEM((SEQ, HEAD_DIM), dt)
    cos_t, sin_t, _, rot_t = tables
    return pl.pallas_call(
        body, name=f"attn_bwd_g{g}", grid=(N_HEADS,), in_specs=in_specs, out_specs=ANY,
        out_shape=SDS(dz.shape, dz.dtype), input_output_aliases={10: 0},
        scratch_shapes=[vm(bf16), vm(f32), vm(f32), vm(f32), vm(f32), vm(f32), vm(f32)]
        + [pltpu.VMEM((2, SEQ, HEAD_DIM), bf16)] * n_out + [pltpu.SemaphoreType.DMA((2 * n_out,))],
    )(*saved, z0, att, lse, dcat, cos_t, sin_t, rot_t, dz)


def _sgu_specs():
    chunk = lambda col: pl.BlockSpec((CHUNK, HALF), lambda n: (n, col))
    vec = pl.BlockSpec((1, HALF), lambda n: (0, 0))
    w = pl.BlockSpec((4, CHUNK, CHUNK), lambda n: (0, 0, 0))
    bias = pl.BlockSpec((CHUNK, CHUNK), lambda n: (0, 0))
    return chunk, vec, w, bias


def _sgu_weights(w_ref):
    tril = lax.broadcasted_iota(jnp.int32, (CHUNK, CHUNK), 1) <= lax.broadcasted_iota(jnp.int32, (CHUNK, CHUNK), 0)
    return tril, [jnp.where(tril, w_ref[h], 0.0).astype(bf16) for h in range(4)]


def _sgu_fwd(z1, ln_g, ln_b, sgu_w, bias_t):
    def body(u_ref, v_ref, cg_ref, g_ref, b_ref, w_ref, bias_ref, yc_ref):
        _, ws = _sgu_weights(w_ref)
        xh, _ = _ln_stats(v_ref[...])
        vn = (xh * g_ref[...] + b_ref[...]).astype(bf16)
        for h in range(4):
            cols = slice(h * POOL_CH, (h + 1) * POOL_CH)
            s = _dot(ws[h], vn[:, cols]) + bias_ref[:, h:h + 1]
            gate = cg_ref[:, cols]
            yc_ref[:, cols] = (u_ref[:, cols] * s * gate * _sigmoid(gate)).astype(bf16)

    chunk, vec, w, bias = _sgu_specs()
    return pl.pallas_call(
        body, name="sgu_fwd", grid=(SEQ // CHUNK,),
        in_specs=[chunk(0), chunk(1), chunk(2), vec, vec, w, bias], out_specs=chunk(0),
        out_shape=SDS((SEQ, 2 * HALF), bf16))(z1, z1, z1, ln_g, ln_b, sgu_w, bias_t)


def _sgu_bwd(z1, dcat, ln_g, ln_b, sgu_w, bias_t):
    def body(u_ref, v_ref, cg_ref, dyc_ref, g_ref, b_ref, w_ref, bias_ref,
             dz_ref, dw_ref, dbias_ref, dg_ref, db_ref, dvn_ref, du_tiles, dv_tiles, dcg_tiles, sems):
        n = pl.program_id(0)
        (du_ref, dv_ref, dcg_ref), full = _write_behind(
            n, SEQ // CHUNK, [du_tiles, dv_tiles, dcg_tiles], sems,
            lambda t, at: dz_ref.at[pl.ds(pl.multiple_of(at * CHUNK, CHUNK), CHUNK), t * HALF:(t + 1) * HALF])
        first = n == 0
        tril, ws = _sgu_weights(w_ref)
        xh, rstd = _ln_stats(v_ref[...])
        g = g_ref[...]
        vn = (xh * g + b_ref[...]).astype(bf16)

        @pl.when(first)
        def _():
            dbias_ref[...] = jnp.zeros((CHUNK, CHUNK), f32)

        for h in range(4):
            cols = slice(h * POOL_CH, (h + 1) * POOL_CH)
            vn_h = vn[:, cols]
            s = _dot(ws[h], vn_h) + bias_ref[:, h:h + 1]
            silu, dsilu = _silu_and_grad(cg_ref[:, cols])
            dyc = dyc_ref[:, cols]
            u = u_ref[:, cols]
            du_ref[:, cols] = (dyc * s * silu).astype(bf16)
            dcg_ref[:, cols] = (dyc * u * s * dsilu).astype(bf16)
            ds = dyc * u * silu
            dbias_ref[:, h:h + 1] += jnp.sum(ds, axis=1, keepdims=True)
            ds = ds.astype(bf16)
            _accumulate(dw_ref.at[h], jnp.where(tril, _dot_nt(ds, vn_h), 0.0), first)
            dvn_ref[:, cols] = _dot_tn(ws[h], ds)
        dv, dg, db = _ln_bwd(xh, rstd, g, dvn_ref[...])
        dv_ref[...] = dv.astype(bf16)
        _accumulate(dg_ref, dg, first)
        _accumulate(db_ref, db, first)
        full()

    chunk, vec, w, bias = _sgu_specs()
    tiles = pltpu.VMEM((2, CHUNK, HALF), bf16)
    return pl.pallas_call(
        body, name="sgu_bwd", grid=(SEQ // CHUNK,),
        in_specs=[chunk(0), chunk(1), chunk(2), chunk(0), vec, vec, w, bias],
        out_specs=[ANY, w, bias, vec, vec],
        out_shape=[SDS((SEQ, 3 * D_MODEL), bf16), SDS((4, CHUNK, CHUNK), f32), SDS((CHUNK, CHUNK), f32),
                   SDS((1, HALF), f32), SDS((1, HALF), f32)],
        scratch_shapes=[pltpu.VMEM((CHUNK, HALF), f32), tiles, tiles, tiles, pltpu.SemaphoreType.DMA((6,))],
    )(z1, z1, z1, dcat, ln_g, ln_b, sgu_w, bias_t)


CONV_TILE = 128
DVAL_COL, DGLU_COL = 12, 16


def _conv_specs():
    val = pl.BlockSpec((SEQ, POOL_CH), lambda j: (0, DVAL_COL + j))
    glu = pl.BlockSpec((SEQ, POOL_CH), lambda j: (0, DGLU_COL + j))
    w = pl.BlockSpec((CONV_K, POOL_CH), lambda j: (0, j))
    col = pl.BlockSpec((SEQ, POOL_CH), lambda j: (0, j))
    vec = pl.BlockSpec((1, POOL_CH), lambda j: (0, j))
    return val, glu, w, col, vec


def _conv_fwd(z1, conv_w, conv_b):
    def body(val_ref, glu_ref, w_ref, b_ref, out_ref, xpad):
        xpad[0:CONV_PAD, :] = jnp.zeros((CONV_PAD, POOL_CH), f32)
        xpad[CONV_PAD:, :] = val_ref[...] * _sigmoid(glu_ref[...])
        w = w_ref[...]
        bias = b_ref[...]

        def tile(i, carry):
            t0 = pl.multiple_of(i * CONV_TILE, CONV_TILE)
            window = xpad[pl.ds(t0, CONV_TILE + CONV_PAD), :]
            acc = jnp.broadcast_to(bias, (CONV_TILE, POOL_CH))
            for k in range(CONV_K):
                shift = CONV_PAD - (CONV_K - 1) + k
                acc = acc + w[k:k + 1, :] * pltpu.roll(window, CONV_TILE + CONV_PAD - shift, 0)[0:CONV_TILE]
            out_ref[pl.ds(t0, CONV_TILE), :] = acc
            return carry

        lax.fori_loop(0, SEQ // CONV_TILE, tile, 0)

    val, glu, w, col, vec = _conv_specs()
    return pl.pallas_call(
        body, name="conv_fwd", grid=(4,), in_specs=[val, glu, w, vec], out_specs=col,
        out_shape=SDS((SEQ, HALF), f32), scratch_shapes=[pltpu.VMEM((SEQ + CONV_PAD, POOL_CH), f32)],
    )(z1, z1, conv_w, conv_b)


def _conv_bwd(z1, dconv, conv_w, dz):
    def body(val_ref, glu_ref, w_ref, dout_ref, dz_in, dz_ref, dw_ref, db_ref, xpad, dpad, dx_ref, dval_tiles, dglu_tiles, sems):
        j = pl.program_id(0)
        (dval_ref, dglu_ref), full = _write_behind(
            j, 4, [dval_tiles, dglu_tiles], sems,
            lambda t, at: _columns(dz_ref, ((DVAL_COL, DGLU_COL)[t] + at) * POOL_CH, POOL_CH))
        val = val_ref[...]
        sig = _sigmoid(glu_ref[...])
        xpad[0:CONV_PAD, :] = jnp.zeros((CONV_PAD, POOL_CH), f32)
        xpad[CONV_PAD:, :] = val * sig
        dout = dout_ref[...]
        dpad[0:SEQ, :] = dout
        dpad[SEQ:, :] = jnp.zeros((CONV_PAD, POOL_CH), f32)
        db_ref[...] = jnp.sum(dout, axis=0, keepdims=True)
        dw_ref[...] = jnp.zeros((CONV_K, POOL_CH), f32)
        w = w_ref[...]

        def tile(i, carry):
            t0 = pl.multiple_of(i * CONV_TILE, CONV_TILE)
            x_win = xpad[pl.ds(t0, CONV_TILE + CONV_PAD), :]
            d_win = dpad[pl.ds(t0, CONV_TILE + CONV_PAD), :]
            d_own = d_win[0:CONV_TILE]
            acc = jnp.zeros((CONV_TILE, POOL_CH), f32)
            for k in range(CONV_K):
                shift = CONV_PAD - (CONV_K - 1) + k
                x_k = pltpu.roll(x_win, CONV_TILE + CONV_PAD - shift, 0)[0:CONV_TILE]
                dw_ref[k:k + 1, :] += jnp.sum(d_own * x_k, axis=0, keepdims=True)
                back = CONV_K - 1 - k
                d_k = d_own if back == 0 else pltpu.roll(d_win, CONV_TILE + CONV_PAD - back, 0)[0:CONV_TILE]
                acc = acc + w[k:k + 1, :] * d_k
            dx_ref[pl.ds(t0, CONV_TILE), :] = acc
            return carry

        lax.fori_loop(0, SEQ // CONV_TILE, tile, 0)
        dx = dx_ref[...]
        dval_ref[...] = (dx * sig).astype(bf16)
        dglu_ref[...] = (dx * val * sig * (1.0 - sig)).astype(bf16)
        full()

    val, glu, w, col, vec = _conv_specs()
    pad = pltpu.VMEM((SEQ + CONV_PAD, POOL_CH), f32)
    tiles = pltpu.VMEM((2, SEQ, POOL_CH), bf16)
    return pl.pallas_call(
        body, name="conv_bwd", grid=(4,), in_specs=[val, glu, w, col, ANY], out_specs=[ANY, w, vec],
        out_shape=[SDS(dz.shape, dz.dtype), SDS((CONV_K, HALF), f32), SDS((1, HALF), f32)],
        input_output_aliases={4: 0},
        scratch_shapes=[pad, pad, pltpu.VMEM((SEQ, POOL_CH), f32), tiles, tiles, pltpu.SemaphoreType.DMA((4,))],
    )(z1, z1, conv_w, dconv, dz)


DGATE_COL = 5


def _conv_norm_fwd(conv, z1, g, b, mixed):
    def body(c_ref, gate_ref, g_ref, b_ref, mixed_ref, yd_ref):
        xh, _ = _ln_stats(c_ref[...])
        n = xh * g_ref[...] + b_ref[...]
        gate = gate_ref[...]
        yd_ref[...] = (n * _sigmoid(n) * gate * _sigmoid(gate)).astype(bf16)

    return pl.pallas_call(
        body, name="conv_norm_fwd", grid=(SEQ // ROWS,),
        in_specs=[_row_spec(HALF), _row_spec(HALF, DGATE_COL), _vec_spec(HALF), _vec_spec(HALF), ANY],
        out_specs=_row_spec(HALF, 1), out_shape=SDS((SEQ, 2 * HALF), bf16), input_output_aliases={4: 0},
    )(conv, z1, g, b, mixed)


def _conv_norm_bwd(conv, z1, dcat, g, b, dz):
    def body(c_ref, gate_ref, dyd_ref, g_ref, b_ref, dz_ref, dconv_ref, dgate_ref, dg_ref, db_ref):
        first = pl.program_id(0) == 0
        xh, rstd = _ln_stats(c_ref[...])
        g = g_ref[...]
        n_silu, n_dsilu = _silu_and_grad(xh * g + b_ref[...])
        gate_silu, gate_dsilu = _silu_and_grad(gate_ref[...])
        dyd = dyd_ref[...]
        dgate_ref[...] = (dyd * n_silu * gate_dsilu).astype(bf16)
        dconv, dg, db = _ln_bwd(xh, rstd, g, dyd * gate_silu * n_dsilu)
        dconv_ref[...] = dconv
        _accumulate(dg_ref, dg, first)
        _accumulate(db_ref, db, first)

    return pl.pallas_call(
        body, name="conv_norm_bwd", grid=(SEQ // ROWS,),
        in_specs=[_row_spec(HALF), _row_spec(HALF, DGATE_COL), _row_spec(HALF, 1), _vec_spec(HALF), _vec_spec(HALF), ANY],
        out_specs=[_row_spec(HALF), _row_spec(HALF, DGATE_COL), _vec_spec(HALF), _vec_spec(HALF)],
        out_shape=[SDS((SEQ, HALF), f32), SDS(dz.shape, dz.dtype), SDS((1, HALF), f32), SDS((1, HALF), f32)],
        input_output_aliases={5: 1},
    )(conv, z1, dcat, g, b, dz)


def _step(x, target, w, chip):
    chip_vec = chip.astype(jnp.int32).reshape(1)
    sharded_names = list(SHARDED_SMALL)
    first = [_cast_into_slot(w["e_w_in"], chip_vec, "cast_e_w_in0", w["e_pre_norm"], 0, E_IN_PIECES)]
    sems, bufs, token = _gather_start(first, "gather_start_first")
    small_shard = _pack([w[k] for k in sharded_names], total_rows=SMALL_SHARD_ROWS) + 0.0 * token[0, 0]
    small_slot = lax.dynamic_update_slice(jnp.zeros((N_CHIPS, SMALL_SHARD_ROWS, LANES), f32), small_shard[None], (chip, 0, 0))
    more = [small_slot]
    more += [_cast_into_slot(w["e_w_in"], chip_vec, f"cast_e_w_in{i}", token, i, E_IN_PIECES) for i in range(1, E_IN_PIECES)]
    more_sems, more_bufs, token = _gather_start(more, "gather_start_pieces")
    rest = [_cast_into_slot(w[k], chip_vec, f"cast_{k}", token) for k in BIG[1:]]
    rest_sems, rest_bufs, token = _gather_start(rest, "gather_start_rest")
    sems, bufs = sems + more_sems + rest_sems, bufs + more_bufs + rest_bufs
    tables = _rope_tables()

    def vec(k):
        return w[k].reshape(1, -1)

    h0 = _pre_norm(x, vec("e_pre_norm") + token[0, 0])
    after, z0, e_w_in = h0, None, []
    for i in range(E_IN_PIECES):
        group = slice(0, 1) if i == 0 else slice(1, 3) if i == 1 else slice(i + 1, i + 2)
        landed = _forward_halves(_gather_wait(bufs[group], sems[group], after, f"gather_wait_{i}"), f"forward_{i}")
        if i == 1:
            small_full = landed[0]
        e_w_in.append(landed[-1])
        z0 = _mm_nn(h0, landed[-1], f32, f"e_in{i}", i, E_IN_PIECES, z0)
        after = z0
    p = {k: _from_chips(k, a) for k, a in zip(sharded_names, _unpack(small_full, [SHARDED_SMALL[k][0] for k in sharded_names]))}
    for k in ("o_pre_norm", "o_sgu_norm_g", "o_sgu_norm_b", "o_conv_b", "o_conv_norm_g", "o_conv_norm_b", "o_post_norm"):
        p[k] = p[k].reshape(1, -1)
    pool_w_bf = p["e_pool_w"].astype(bf16)
    bias_t = jnp.pad(w["o_sgu_b"].T, ((0, 0), (0, CHUNK - 4)))

    cat0, att, lse, qkv_by_residue = _attn_fwd(z0, tables, _pool_fwd(z0, pool_w_bf, vec("e_pool_scale")))

    def arrived(index, after, name):
        one = slice(index, index + 1)
        return _forward_halves(_gather_wait(bufs[one], sems[one], after, f"gather_wait_{name}"), f"forward_{name}")[0]

    e_w_out = arrived(1 + E_IN_PIECES, att, "e_w_out").reshape(1, D_MODEL, D_MODEL)
    y0 = _mm_nn(cat0, e_w_out, f32, "e_out")
    x1, h1 = _mid_norm(x, y0, vec("e_post_norm"), p["o_pre_norm"])
    o_w_in = arrived(2 + E_IN_PIECES, h1, "o_w_in")
    z1 = _mm_nn(h1, o_w_in, f32, "o_in")
    yc = _sgu_fwd(z1, p["o_sgu_norm_g"], p["o_sgu_norm_b"], w["o_sgu_w"], bias_t)
    conv = _conv_fwd(z1, p["o_conv_w"], p["o_conv_b"])
    cat1 = _conv_norm_fwd(conv, z1, p["o_conv_norm_g"], p["o_conv_norm_b"], yc)
    o_w_out = arrived(3 + E_IN_PIECES, cat1, "o_w_out").reshape(1, D_MODEL, D_MODEL)
    y1 = _mm_nn(cat1, o_w_out, f32, "o_out")
    loss, dx2, dy1, g_o_post = _final_norm_loss(x1, y1, p["o_post_norm"], target)

    in_flight = {}

    def send_off(name, grad):
        sem, sums, land, tok = _scatter_start(_swap_add(grad, f"swap_add_{name}"), f"scatter_start_{name}")
        in_flight[name] = (sem, sums, land)
        return tok

    tok = send_off("o_w_out", _mm_tn(cat1, dy1, 1, "o_out_dw").reshape(N_CHIPS, HALF // 2, D_MODEL))
    dcat1 = _mm_nt(dy1, [o_w_out], "o_out_dx", tok)
    dz1, g_sgu_w, g_bias_t, g_sgu_g, g_sgu_b = _sgu_bwd(
        z1, dcat1, p["o_sgu_norm_g"] + tok[0, 0], p["o_sgu_norm_b"], w["o_sgu_w"], bias_t)
    dconv, dz1, g_cn_g, g_cn_b = _conv_norm_bwd(conv, z1, dcat1, p["o_conv_norm_g"], p["o_conv_norm_b"], dz1)
    dz1, g_conv_w, g_conv_b = _conv_bwd(z1, dconv, p["o_conv_w"], dz1)
    tok = send_off("o_w_in", _mm_tn(h1, dz1, N_CHIPS, "o_in_dw"))
    dh1 = _mm_nt(dz1, [o_w_in], "o_in_dx", tok)
    dx1, dy0, g_o_pre, g_e_post = _mid_norm_bwd(dx2, dh1, x1, y0, p["o_pre_norm"] + tok[0, 0], vec("e_post_norm"))

    tok = send_off("e_w_out", _mm_tn(cat0, dy0, 1, "e_out_dw").reshape(N_CHIPS, HALF // 2, D_MODEL))
    dcat0 = _mm_nt(dy0, [e_w_out], "e_out_dx", tok)
    dz0, g_pool_w, g_pool_scale = _pool_bwd(z0, dcat0, pool_w_bf, vec("e_pool_scale") + tok[0, 0])
    for g in range(len(DILATIONS)):
        dz0 = _attn_bwd_group(g, qkv_by_residue[g], z0, att, lse, dcat0, tables, dz0)
    tok = send_off("e_w_in", _mm_tn(h0, dz0, N_CHIPS, "e_in_dw"))
    dh0 = _mm_nt(dz0, e_w_in, "e_in_dx", tok)
    grad_x, g_e_pre = _pre_norm_bwd(dx1, dh0, x, vec("e_pre_norm") + tok[0, 0])

    small = {"e_pre_norm": g_e_pre, "e_pool_w": g_pool_w, "e_pool_scale": g_pool_scale, "e_post_norm": g_e_post,
             "o_pre_norm": g_o_pre, "o_sgu_norm_g": g_sgu_g, "o_sgu_norm_b": g_sgu_b, "o_sgu_w": g_sgu_w,
             "o_sgu_b": g_bias_t, "o_conv_w": g_conv_w, "o_conv_b": g_conv_b,
             "o_conv_norm_g": g_cn_g, "o_conv_norm_b": g_cn_b, "o_post_norm": g_o_post}
    return loss, grad_x, in_flight, small


def _land(in_flight, name, chip, after):
    sems, sums, land = in_flight[name]
    sums, land = _scatter_wait(sems, sums, land, after, f"scatter_wait_{name}")
    return _add_landed_join(sums, land, chip.astype(jnp.int32).reshape(1), f"add_landed_{name}")


def _place():
    x, y, c = lax.axis_index("x"), lax.axis_index("y"), lax.axis_index("c")
    others = [(1 - x, y), (x, 1 - y), (1 - x, 1 - y)]
    return x, y, c, 2 * x + y, others


SWAP_ROWS = 256
FORWARD_STAGE_BYTES = 4 << 20


def _swap_add(g, name):
    chips, r, c = g.shape
    half = r // 2
    rows_per_step = 2 * SWAP_ROWS if half % (2 * SWAP_ROWS) == 0 else SWAP_ROWS
    nb = half // rows_per_step
    steps = chips * nb

    def body(core_ref, mine_ref, theirs_ref, out_ref, landing, send_sems, recv_sems, free_sems):
        i = pl.program_id(0)
        x, y, core, _, _ = _place()
        sibling = (x, y, 1 - core)

        def send(slot):
            return pltpu.make_async_remote_copy(src_ref=theirs_ref, dst_ref=landing.at[slot], send_sem=send_sems.at[slot],
                                                recv_sem=recv_sems.at[slot], device_id=sibling, device_id_type=MESH)

        @pl.when(i < steps)
        def _():
            @pl.when(i >= 2)
            def _():
                pl.semaphore_wait(free_sems.at[i % 2], 1)

            send(i % 2).start()

        @pl.when(i >= 1)
        def _():
            landed = (i - 1) % 2
            send(landed).wait_recv()
            out_ref[...] = (mine_ref[...].astype(f32) + landing[landed].astype(f32)).astype(out_ref.dtype)

            @pl.when(i + 1 < steps)
            def _():
                pl.semaphore_signal(free_sems.at[landed], 1, device_id=sibling, device_id_type=MESH)

        @pl.when(i < steps)
        def _():
            send(i % 2).wait_send()

    def rows_of(b, h):
        return (2 * (b // nb) + h) * nb + b % nb

    block = (rows_per_step, c)
    grid_spec = pltpu.PrefetchScalarGridSpec(
        num_scalar_prefetch=1, grid=(steps + 1,),
        in_specs=[pl.BlockSpec(block, lambda i, core: (rows_of(jnp.maximum(i - 1, 0), core[0]), 0)),
                  pl.BlockSpec(block, lambda i, core: (rows_of(jnp.minimum(i, steps - 1), 1 - core[0]), 0))],
        out_specs=pl.BlockSpec(block, lambda i, core: (jnp.maximum(i - 1, 0), 0)),
        scratch_shapes=[pltpu.VMEM((2, rows_per_step, c), g.dtype), pltpu.SemaphoreType.DMA((2,)),
                        pltpu.SemaphoreType.DMA((2,)), pltpu.SemaphoreType.REGULAR((2,))])
    core = lax.axis_index("c").astype(jnp.int32).reshape(1)
    rows = g.reshape(chips * r, c)
    out = pl.pallas_call(body, name=name, grid_spec=grid_spec, out_shape=SDS((chips * half, c), g.dtype))(core, rows, rows)
    return out.reshape(chips, half, c)


HBM = pl.BlockSpec(memory_space=pltpu.HBM)
SEM = pl.BlockSpec(memory_space=pltpu.SEMAPHORE)
EFFECT = pltpu.SideEffectType.DATAFLOW_SIDE_EFFECTING


def _in_hbm(a):
    return pltpu.with_memory_space_constraint(a, pltpu.HBM)


def _cast_into_slot(w, chip, name, after, piece=0, pieces=1):
    r, c = w.shape
    c = c // pieces
    nb = r // SWAP_ROWS

    def body(chip_ref, w_ref, after_ref, o_ref):
        o_ref[...] = w_ref[...].astype(bf16)

    grid_spec = pltpu.PrefetchScalarGridSpec(
        num_scalar_prefetch=1, grid=(nb,),
        in_specs=[pl.BlockSpec((SWAP_ROWS, c), lambda i, chip: (i, piece)), ANY],
        out_specs=pl.BlockSpec((SWAP_ROWS, c), lambda i, chip: (chip[0] * nb + i, 0)))
    out = pl.pallas_call(body, name=name, grid_spec=grid_spec, out_shape=SDS((N_CHIPS * r, c), bf16))(chip, w, after)
    return out.reshape(N_CHIPS, r, c)


def _gather_start(bufs, name):
    n = len(bufs)

    def body(*refs):
        ins, sems, token = refs[:n], refs[n:3 * n], refs[4 * n]
        x, y, c, me, others = _place()
        for a in range(n):
            rows = ins[a].shape[1] // 2
            mine = ins[a].at[me, pl.ds(c * rows, rows), :]
            for k, (ox, oy) in enumerate(others):
                pltpu.make_async_remote_copy(src_ref=mine, dst_ref=mine, send_sem=sems[2 * a].at[k],
                                             recv_sem=sems[2 * a + 1].at[k], device_id=(ox, oy, c),
                                             device_id_type=MESH).start()
        token[...] = jnp.zeros_like(token)

    out = pl.pallas_call(
        body, name=name, in_specs=[HBM] * n,
        out_shape=(*[pltpu.SemaphoreType.DMA((3,))] * (2 * n), *[pltpu.HBM(b.shape, b.dtype) for b in bufs],
                   SDS((8, 128), f32)),
        out_specs=(*[SEM] * (2 * n), *[HBM] * n, pl.BlockSpec(memory_space=pltpu.VMEM)),
        input_output_aliases={a: 2 * n + a for a in range(n)},
        compiler_params=pltpu.CompilerParams(has_side_effects=EFFECT),
    )(*[_in_hbm(b) for b in bufs])
    return [(out[2 * a], out[2 * a + 1]) for a in range(n)], list(out[2 * n:3 * n]), out[3 * n]


def _gather_wait(bufs, sems, after, name):
    n = len(bufs)

    def body(*refs):
        ins, sem_refs = refs[:n], refs[n:3 * n]
        x, y, c, me, others = _place()
        for a in range(n):
            rows = ins[a].shape[1] // 2
            mine = ins[a].at[me, pl.ds(c * rows, rows), :]
            for k, (ox, oy) in enumerate(others):
                landed = ins[a].at[2 * ox + oy, pl.ds(c * rows, rows), :]
                copy = pltpu.make_async_remote_copy(src_ref=mine, dst_ref=landed, send_sem=sem_refs[2 * a].at[k],
                                                    recv_sem=sem_refs[2 * a + 1].at[k], device_id=(ox, oy, c),
                                                    device_id_type=MESH)
                copy.wait_send()
                copy.wait_recv()

    flat_sems = [s for pair in sems for s in pair]
    out = pl.pallas_call(
        body, name=name, in_specs=[HBM] * n + [SEM] * (2 * n) + [ANY],
        out_shape=tuple(pltpu.HBM(b.shape, b.dtype) for b in bufs), out_specs=tuple([HBM] * n),
        input_output_aliases={a: a for a in range(n)},
        compiler_params=pltpu.CompilerParams(has_side_effects=EFFECT),
    )(*bufs, *flat_sems, after)
    return list(out)


def _forward_halves(bufs, name):
    n = len(bufs)
    blocks = []
    for b in bufs:
        half = b.shape[1] // 2
        whole = half * b.shape[2] * b.dtype.itemsize <= FORWARD_STAGE_BYTES
        blocks.append((half, half if whole or half % SWAP_ROWS else SWAP_ROWS))
    work = [(a, k, b) for a in range(n) for k in range(3) for b in range(blocks[a][0] // blocks[a][1])]

    def body(*refs):
        outs, stages = refs[n:2 * n], refs[2 * n:3 * n]
        load_sems, send_sems, recv_sems = refs[3 * n:]
        x, y, c, me, others = _place()
        sibling = (x, y, 1 - c)

        def rows(item):
            a, k, b = item
            half, tr = blocks[a]
            ox, oy = others[k]
            return outs[a].at[2 * ox + oy, pl.ds(c * half + b * tr, tr), :]

        def load(s, item):
            return pltpu.make_async_copy(rows(item), stages[item[0]].at[s], load_sems.at[s])

        def send(s, item):
            return pltpu.make_async_remote_copy(src_ref=stages[item[0]].at[s], dst_ref=rows(item), send_sem=send_sems.at[s],
                                                recv_sem=recv_sems.at[item[0]], device_id=sibling, device_id_type=MESH)

        load(0, work[0]).start()
        for t, item in enumerate(work):
            s = t % 2
            load(s, item).wait()
            send(s, item).start()
            if t + 1 < len(work):
                if t >= 1:
                    send(1 - s, work[t - 1]).wait_send()
                load(1 - s, work[t + 1]).start()
        if len(work) > 1:
            send(len(work) % 2, work[-2]).wait_send()
        send((len(work) - 1) % 2, work[-1]).wait_send()
        for a in range(n):
            theirs = outs[a].at[pl.ds(0, 3), pl.ds(0, blocks[a][0]), :]
            pltpu.make_async_remote_copy(src_ref=theirs, dst_ref=theirs, send_sem=send_sems.at[0], recv_sem=recv_sems.at[a],
                                         device_id=sibling, device_id_type=MESH).wait_recv()

    out = pl.pallas_call(
        body, name=name, in_specs=[ANY] * n, out_specs=[ANY] * n, out_shape=[SDS(b.shape, b.dtype) for b in bufs],
        input_output_aliases={a: a for a in range(n)},
        scratch_shapes=[pltpu.VMEM((2, blocks[a][1], bufs[a].shape[2]), bufs[a].dtype) for a in range(n)]
        + [pltpu.SemaphoreType.DMA((2,)), pltpu.SemaphoreType.DMA((2,)), pltpu.SemaphoreType.DMA((n,))],
    )(*bufs)
    return list(out)


def _scatter_start(chip_sums, name):
    def body(a_ref, land_ref, send_sems, recv_sems, a_thru, land_thru, token):
        x, y, c, me, others = _place()
        for k, (ox, oy) in enumerate(others):
            pltpu.make_async_remote_copy(src_ref=a_ref.at[2 * ox + oy], dst_ref=land_ref.at[me], send_sem=send_sems.at[k],
                                         recv_sem=recv_sems.at[k], device_id=(ox, oy, c), device_id_type=MESH).start()
        token[...] = jnp.zeros_like(token)

    shape = pltpu.HBM(chip_sums.shape, chip_sums.dtype)
    send, recv, a_thru, land, token = pl.pallas_call(
        body, name=name, in_specs=[HBM, HBM],
        out_shape=(pltpu.SemaphoreType.DMA((3,)), pltpu.SemaphoreType.DMA((3,)), shape, shape, SDS((8, 128), f32)),
        out_specs=(SEM, SEM, HBM, HBM, pl.BlockSpec(memory_space=pltpu.VMEM)), input_output_aliases={0: 2, 1: 3},
        compiler_params=pltpu.CompilerParams(has_side_effects=EFFECT),
    )(_in_hbm(chip_sums), _in_hbm(lax.empty(chip_sums.shape, chip_sums.dtype)))
    return (send, recv), a_thru, land, token


def _scatter_wait(sems, chip_sums, land, after, name):
    def body(a_ref, land_ref, send_sems, recv_sems, after_ref, a_out, land_out):
        x, y, c, me, others = _place()
        for k, (ox, oy) in enumerate(others):
            copy = pltpu.make_async_remote_copy(
                src_ref=a_ref.at[2 * ox + oy], dst_ref=land_ref.at[2 * ox + oy], send_sem=send_sems.at[k],
                recv_sem=recv_sems.at[k], device_id=(ox, oy, c), device_id_type=MESH)
            copy.wait_send()
            copy.wait_recv()

    shape = pltpu.HBM(chip_sums.shape, chip_sums.dtype)
    return pl.pallas_call(
        body, name=name, in_specs=[HBM, HBM, SEM, SEM, ANY], out_shape=(shape, shape), out_specs=(HBM, HBM),
        input_output_aliases={0: 0, 1: 1}, compiler_params=pltpu.CompilerParams(has_side_effects=EFFECT),
    )(chip_sums, land, sems[0], sems[1], after)


def _add_landed_join(chip_sums, land, chip, name):
    chips, rh, c = chip_sums.shape
    nb = rh // SWAP_ROWS

    def body(chip_ref, own_ref, l1_ref, l2_ref, l3_ref, out_hbm, buf, send_sems, recv_sem, local_sems):
        i = pl.program_id(0)
        slot = i % 2
        x, y, core, _, _ = _place()
        sibling = (x, y, 1 - core)

        def copies(s, step):
            rows = pl.ds(pl.multiple_of((core * nb + step) * SWAP_ROWS, SWAP_ROWS), SWAP_ROWS)
            keep = pltpu.make_async_copy(buf.at[s], out_hbm.at[rows, :], local_sems.at[s])
            give = pltpu.make_async_remote_copy(src_ref=buf.at[s], dst_ref=out_hbm.at[rows, :], send_sem=send_sems.at[s],
                                                recv_sem=recv_sem.at[0], device_id=sibling, device_id_type=MESH)
            return keep, give

        def drain(s, step):
            keep, give = copies(s, step)
            keep.wait()
            give.wait_send()

        @pl.when(i >= 2)
        def _():
            drain(slot, i - 2)

        buf[slot] = ((own_ref[...].astype(f32) + l1_ref[...].astype(f32)) + l2_ref[...].astype(f32)) + l3_ref[...].astype(f32)
        keep, give = copies(slot, i)
        keep.start()
        give.start()

        @pl.when(i == nb - 1)
        def _():
            drain(slot, i)
            if nb > 1:
                drain(1 - slot, i - 1)
            theirs = out_hbm.at[pl.ds((1 - core) * rh, rh), :]
            pltpu.make_async_remote_copy(src_ref=theirs, dst_ref=theirs, send_sem=send_sems.at[0], recv_sem=recv_sem.at[0],
                                         device_id=sibling, device_id_type=MESH).wait_recv()

    block = (SWAP_ROWS, c)
    from_slot = lambda d: pl.BlockSpec(block, lambda i, chip: (((chip[0] + d) % chips) * nb + i, 0))
    grid_spec = pltpu.PrefetchScalarGridSpec(
        num_scalar_prefetch=1, grid=(nb,), in_specs=[from_slot(0), from_slot(1), from_slot(2), from_slot(3)],
        out_specs=ANY,
        scratch_shapes=[pltpu.VMEM((2, SWAP_ROWS, c), f32), pltpu.SemaphoreType.DMA((2,)),
                        pltpu.SemaphoreType.DMA((1,)), pltpu.SemaphoreType.DMA((2,))])
    land_rows = land.reshape(chips * rh, c)
    return pl.pallas_call(body, name=name, grid_spec=grid_spec, out_shape=SDS((2 * rh, c), f32))(
        chip, chip_sums.reshape(chips * rh, c), land_rows, land_rows, land_rows)


def _adamw_update(w_ref, g_ref, m_ref, v_ref, d_ref, nm_ref, nv_ref):
    g = g_ref[...]
    nm = ADAM_B1 * m_ref[...] + (1.0 - ADAM_B1) * g
    nv = ADAM_B2 * v_ref[...] + (1.0 - ADAM_B2) * (g * g)
    nm_ref[...] = nm
    nv_ref[...] = nv
    m_hat = nm / (1.0 - ADAM_B1 ** ADAM_STEP)
    v_hat = nv / (1.0 - ADAM_B2 ** ADAM_STEP)
    d_ref[...] = -ADAM_LR * (m_hat / (jnp.sqrt(v_hat) + ADAM_EPS) + ADAM_WD * w_ref[...])


def _adamw(w, g, m, v, name):
    r, c = w.shape
    tr = 128 if r % 128 == 0 else r

    def body(w_ref, g_ref, m_ref, v_ref, g_out_ref, d_ref, nm_ref, nv_ref):
        g_out_ref[...] = g_ref[...]
        _adamw_update(w_ref, g_ref, m_ref, v_ref, d_ref, nm_ref, nv_ref)

    spec = pl.BlockSpec((tr, c), lambda i: (i, 0))
    return pl.pallas_call(body, name=name, grid=(r // tr,), in_specs=[spec] * 4, out_specs=[spec] * 4,
                          out_shape=[SDS((r, c), f32)] * 4)(w, g, m, v)


SMALL_PACKING = {
    "e_pre_norm": ((1, 2048), 8, (1, 2048)), "e_pool_w": ((1024, 256), 1024, (256, 256)),
    "e_pool_scale": ((1, 1024), 8, (1, 1024)), "e_post_norm": ((1, 2048), 8, (1, 2048)),
    "o_pre_norm": ((1, 2048), 8, (1, 512)), "o_sgu_norm_g": ((1, 1024), 8, (1, 256)),
    "o_sgu_norm_b": ((1, 1024), 8, (1, 256)), "o_sgu_w": ((512, 128), 512, (512, 128)),
    "o_sgu_b": ((128, 128), 8, (4, 128)), "o_conv_w": ((31, 1024), 128, (31, 256)), "o_conv_b": ((1, 1024), 8, (1, 256)),
    "o_conv_norm_g": ((1, 1024), 8, (1, 256)), "o_conv_norm_b": ((1, 1024), 8, (1, 256)),
    "o_post_norm": ((1, 2048), 8, (1, 512)),
}
SMALL_PACKED_ROWS = 1792


def _small_finalize(grads, ws, ms, vs, after):
    names = list(SMALL_ORDER)
    n = len(names)
    half, piece = SMALL_PACKED_ROWS // 2, SMALL_PACKED_ROWS // 8
    first_row, row = {}, 0
    for k in names:
        first_row[k] = row
        row += SMALL_PACKING[k][1]

    def body(*refs):
        g_refs, total = refs[0:n], refs[n + 1]
        pack, from_sibling, from_chips, send_a, recv_a, send_b, recv_b, send_c, recv_c, send_d, recv_d = refs[n + 2:]
        x, y, c, me, others = _place()

        for r0 in range(0, SMALL_PACKED_ROWS, piece):
            pack[r0:r0 + piece, :] = jnp.zeros((piece, LANES), f32)
        for k, g_ref in zip(names, g_refs):
            (rows, width), _, _ = SMALL_PACKING[k]
            r0 = first_row[k]
            if k == "o_sgu_b":
                pack[r0:r0 + 4, 0:CHUNK] = g_ref[...].T[0:4, :]
            elif width < LANES:
                pack[r0:r0 + rows, 0:width] = g_ref[...]
            else:
                for j in range(width // LANES):
                    dst = r0 + j * (1 if rows == 1 else 32)
                    pack[dst:dst + rows, :] = g_ref[:, j * LANES:(j + 1) * LANES]

        sibling = (x, y, 1 - c)
        swap = pltpu.make_async_remote_copy(
            src_ref=pack.at[pl.ds(pl.multiple_of((1 - c) * half, 8), half), :], dst_ref=from_sibling,
            send_sem=send_a.at[0], recv_sem=recv_a.at[0], device_id=sibling, device_id_type=MESH)
        swap.start()
        swap.wait()
        for j in range(4):
            rows = pl.ds(pl.multiple_of(c * half + j * piece, 8), piece)
            pack[rows, :] = pack[rows, :] + from_sibling[j * piece:(j + 1) * piece, :]

        def piece_of(chip):
            return pl.ds(pl.multiple_of(c * half + chip * piece, 8), piece)

        def to_chip(k):
            ox, oy = others[k]
            return pltpu.make_async_remote_copy(
                src_ref=pack.at[piece_of(2 * ox + oy), :], dst_ref=from_chips.at[me], send_sem=send_b.at[k],
                recv_sem=recv_b.at[k], device_id=(ox, oy, c), device_id_type=MESH)

        for k in range(3):
            to_chip(k).start()
        from_chips[me] = pack[piece_of(me), :]
        for k, (ox, oy) in enumerate(others):
            landed = from_chips.at[2 * ox + oy]
            pltpu.make_async_remote_copy(src_ref=landed, dst_ref=landed, send_sem=send_b.at[k], recv_sem=recv_b.at[k],
                                         device_id=(ox, oy, c), device_id_type=MESH).wait_recv()
        for k in range(3):
            to_chip(k).wait_send()
        mine = pl.ds(pl.multiple_of(c * half + me * piece, 8), piece)
        total[mine, :] = ((from_chips[0] + from_chips[1]) + from_chips[2]) + from_chips[3]

        def to_same_core(k):
            ox, oy = others[k]
            return pltpu.make_async_remote_copy(
                src_ref=total.at[mine, :], dst_ref=total.at[mine, :], send_sem=send_c.at[k], recv_sem=recv_c.at[k],
                device_id=(ox, oy, c), device_id_type=MESH)

        for k in range(3):
            to_same_core(k).start()
        for k, (ox, oy) in enumerate(others):
            theirs = total.at[piece_of(2 * ox + oy), :]
            pltpu.make_async_remote_copy(src_ref=theirs, dst_ref=theirs, send_sem=send_c.at[k], recv_sem=recv_c.at[k],
                                         device_id=(ox, oy, c), device_id_type=MESH).wait_recv()
        for k in range(3):
            to_same_core(k).wait_send()
        my_half = total.at[pl.ds(pl.multiple_of(c * half, 8), half), :]
        join = pltpu.make_async_remote_copy(src_ref=my_half, dst_ref=my_half, send_sem=send_d.at[0], recv_sem=recv_d.at[0],
                                            device_id=sibling, device_id_type=MESH)
        join.start()
        their_half = total.at[pl.ds(pl.multiple_of((1 - c) * half, 8), half), :]
        pltpu.make_async_remote_copy(src_ref=their_half, dst_ref=their_half, send_sem=send_d.at[0], recv_sem=recv_d.at[0],
                                     device_id=sibling, device_id_type=MESH).wait_recv()
        join.wait_send()

    whole = pl.BlockSpec(memory_space=pltpu.VMEM)
    total = pl.pallas_call(
        body, name="small_allreduce", in_specs=[whole] * n + [ANY], out_specs=whole,
        out_shape=SDS((SMALL_PACKED_ROWS, LANES), f32),
        scratch_shapes=[pltpu.VMEM((SMALL_PACKED_ROWS, LANES), f32), pltpu.VMEM((half, LANES), f32),
                        pltpu.VMEM((N_CHIPS, piece, LANES), f32),
                        pltpu.SemaphoreType.DMA((1,)), pltpu.SemaphoreType.DMA((1,)), pltpu.SemaphoreType.DMA((3,)),
                        pltpu.SemaphoreType.DMA((3,)), pltpu.SemaphoreType.DMA((3,)), pltpu.SemaphoreType.DMA((3,)),
                        pltpu.SemaphoreType.DMA((1,)), pltpu.SemaphoreType.DMA((1,))],
    )(*grads, after)

    def update(*refs):
        total = refs[0]
        w_refs, m_refs, v_refs = refs[1:n + 1], refs[n + 1:2 * n + 1], refs[2 * n + 1:3 * n + 1]
        outs = refs[3 * n + 1:]
        me = 2 * lax.axis_index("x") + lax.axis_index("y")

        def of_chip(candidates):
            value = candidates[0]
            for j in range(1, N_CHIPS):
                value = jnp.where(me == j, candidates[j], value)
            return value

        for i, k in enumerate(names):
            (rows, width), _, (local_rows, local_width) = SMALL_PACKING[k]
            r0 = first_row[k]
            if k == "o_sgu_b":
                g = total[r0:r0 + 4, 0:CHUNK]
            elif k == "e_pool_w":
                for grp in range(4):
                    src = pl.ds(pl.multiple_of(r0 + grp * POOL_CH + me * 64, 8), 64)
                    dst = slice(grp * 64, (grp + 1) * 64)
                    _adamw_rows(total[src, :], i, dst, w_refs, m_refs, v_refs, outs, n)
                continue
            elif k == "o_conv_w":
                g = total[pl.ds(pl.multiple_of(r0 + me * 32, 8), 32), :][0:CONV_K]
            elif width < LANES:
                g = total[r0:r0 + rows, 0:width]
            else:
                lanes = [total[r0 + j:r0 + j + 1, :] for j in range(width // LANES)]
                per_chip = local_width // LANES
                if local_width == width:
                    g = jnp.concatenate(lanes, axis=1)
                elif per_chip == 1:
                    g = of_chip(lanes)
                else:
                    g = of_chip([jnp.concatenate(lanes[j * per_chip:(j + 1) * per_chip], axis=1) for j in range(N_CHIPS)])
            _adamw_rows(g, i, slice(None), w_refs, m_refs, v_refs, outs, n)

    shard_shapes = [SMALL_PACKING[k][2] for k in names]
    out = pl.pallas_call(update, name="small_update", in_specs=[whole] * (3 * n + 1), out_specs=[whole] * (4 * n),
                         out_shape=[SDS(s, f32) for s in shard_shapes] * 4)(total, *ws, *ms, *vs)
    return out[:n], out[n:2 * n], out[2 * n:3 * n], out[3 * n:]


def _adamw_rows(g, i, rows, w_refs, m_refs, v_refs, outs, n):
    w, m, v = w_refs[i][rows, :], m_refs[i][rows, :], v_refs[i][rows, :]
    nm = ADAM_B1 * m + (1.0 - ADAM_B1) * g
    nv = ADAM_B2 * v + (1.0 - ADAM_B2) * (g * g)
    m_hat = nm / (1.0 - ADAM_B1 ** ADAM_STEP)
    v_hat = nv / (1.0 - ADAM_B2 ** ADAM_STEP)
    outs[i][rows, :] = g
    outs[n + i][rows, :] = -ADAM_LR * (m_hat / (jnp.sqrt(v_hat) + ADAM_EPS) + ADAM_WD * w)
    outs[2 * n + i][rows, :] = nm
    outs[3 * n + i][rows, :] = nv


def _pack(arrays, total_rows=None):
    parts = []
    rows = 0
    for a in arrays:
        flat = a.reshape(-1, LANES)
        pad = -flat.shape[0] % 8
        parts.append(jnp.pad(flat, ((0, pad), (0, 0))))
        rows += flat.shape[0] + pad
    if total_rows is not None:
        parts.append(jnp.zeros((total_rows - rows, LANES), arrays[0].dtype))
    return jnp.concatenate(parts, axis=0)


def _unpack(buf, shapes):
    out = []
    row = 0
    lead = buf.shape[:-2]
    for shape in shapes:
        size = 1
        for s in shape:
            size *= s
        rows = size // LANES
        out.append(buf[..., row:row + rows, :].reshape(lead + tuple(shape)))
        row += rows + (-rows % 8)
    return out


BIG = ("e_w_in", "e_w_out", "o_w_in", "o_w_out")
SHARDED_SMALL = {
    "e_pool_w": ((4, 64, 256), 1), "o_pre_norm": ((512,), 0), "o_sgu_norm_g": ((256,), 0), "o_sgu_norm_b": ((256,), 0),
    "o_conv_w": ((31, 256), 1), "o_conv_b": ((256,), 0), "o_conv_norm_g": ((256,), 0), "o_conv_norm_b": ((256,), 0),
    "o_post_norm": ((512,), 0),
}
SMALL_ORDER = ("e_pre_norm", "e_pool_w", "e_pool_scale", "e_post_norm", "o_pre_norm", "o_sgu_norm_g", "o_sgu_norm_b",
               "o_sgu_w", "o_sgu_b", "o_conv_w", "o_conv_b", "o_conv_norm_g", "o_conv_norm_b", "o_post_norm")
ALL_ORDER = ("e_pre_norm", "e_w_in", "e_pool_w", "e_pool_scale", "e_w_out", "e_post_norm", "o_pre_norm", "o_w_in",
             "o_sgu_norm_g", "o_sgu_norm_b", "o_sgu_w", "o_sgu_b", "o_conv_w", "o_conv_b", "o_conv_norm_g",
             "o_conv_norm_b", "o_w_out", "o_post_norm")


def _full_shape(name):
    shape, axis = SHARDED_SMALL[name]
    return tuple(s * N_CHIPS if i == axis else s for i, s in enumerate(shape))


def _from_chips(name, stacked):
    shape, axis = SHARDED_SMALL[name]
    return jnp.moveaxis(stacked, 0, axis).reshape(_full_shape(name))


def kernel(x, e_pre_norm, e_w_in, e_pool_w, e_pool_scale, e_w_out, e_post_norm, o_pre_norm, o_w_in, o_sgu_norm_g, o_sgu_norm_b, o_sgu_w, o_sgu_b, o_conv_w, o_conv_b, o_conv_norm_g, o_conv_norm_b, o_w_out, o_post_norm, loss_target, m_e_pre_norm, m_e_w_in, m_e_pool_w, m_e_pool_scale, m_e_w_out, m_e_post_norm, m_o_pre_norm, m_o_w_in, m_o_sgu_norm_g, m_o_sgu_norm_b, m_o_sgu_w, m_o_sgu_b, m_o_conv_w, m_o_conv_b, m_o_conv_norm_g, m_o_conv_norm_b, m_o_w_out, m_o_post_norm, v_e_pre_norm, v_e_w_in, v_e_pool_w, v_e_pool_scale, v_e_w_out, v_e_post_norm, v_o_pre_norm, v_o_w_in, v_o_sgu_norm_g, v_o_sgu_norm_b, v_o_sgu_w, v_o_sgu_b, v_o_conv_w, v_o_conv_b, v_o_conv_norm_g, v_o_conv_norm_b, v_o_w_out, v_o_post_norm):
    w = dict(e_pre_norm=e_pre_norm, e_w_in=e_w_in, e_pool_w=e_pool_w, e_pool_scale=e_pool_scale, e_w_out=e_w_out,
             e_post_norm=e_post_norm, o_pre_norm=o_pre_norm, o_w_in=o_w_in, o_sgu_norm_g=o_sgu_norm_g,
             o_sgu_norm_b=o_sgu_norm_b, o_sgu_w=o_sgu_w, o_sgu_b=o_sgu_b, o_conv_w=o_conv_w, o_conv_b=o_conv_b,
             o_conv_norm_g=o_conv_norm_g, o_conv_norm_b=o_conv_norm_b, o_w_out=o_w_out, o_post_norm=o_post_norm)
    m = dict(e_pre_norm=m_e_pre_norm, e_w_in=m_e_w_in, e_pool_w=m_e_pool_w, e_pool_scale=m_e_pool_scale,
             e_w_out=m_e_w_out, e_post_norm=m_e_post_norm, o_pre_norm=m_o_pre_norm, o_w_in=m_o_w_in,
             o_sgu_norm_g=m_o_sgu_norm_g, o_sgu_norm_b=m_o_sgu_norm_b, o_sgu_w=m_o_sgu_w, o_sgu_b=m_o_sgu_b,
             o_conv_w=m_o_conv_w, o_conv_b=m_o_conv_b, o_conv_norm_g=m_o_conv_norm_g, o_conv_norm_b=m_o_conv_norm_b,
             o_w_out=m_o_w_out, o_post_norm=m_o_post_norm)
    v = dict(e_pre_norm=v_e_pre_norm, e_w_in=v_e_w_in, e_pool_w=v_e_pool_w, e_pool_scale=v_e_pool_scale,
             e_w_out=v_e_w_out, e_post_norm=v_e_post_norm, o_pre_norm=v_o_pre_norm, o_w_in=v_o_w_in,
             o_sgu_norm_g=v_o_sgu_norm_g, o_sgu_norm_b=v_o_sgu_norm_b, o_sgu_w=v_o_sgu_w, o_sgu_b=v_o_sgu_b,
             o_conv_w=v_o_conv_w, o_conv_b=v_o_conv_b, o_conv_norm_g=v_o_conv_norm_g, o_conv_norm_b=v_o_conv_norm_b,
             o_w_out=v_o_w_out, o_post_norm=v_o_post_norm)
    w, m, v = ({k: a[0] for k, a in d.items()} for d in (w, m, v))
    chip = 2 * lax.axis_index("x") + lax.axis_index("y")

    loss, grad_x, in_flight, small = _step(x[0], loss_target[0], w, chip)

    grads, delta, new_m, new_v = {}, {}, {}, {}

    def rows_of(a):
        return a.reshape(-1, a.shape[-1])

    after = grad_x
    for k in ("o_w_out", "o_w_in", "e_w_out", "small", "e_w_in"):
        if k == "small":
            small_grads = [small[name].reshape(SMALL_PACKING[name][0]) for name in SMALL_ORDER]
            updates = _small_finalize(small_grads, *[[rows_of(d[name]) for name in SMALL_ORDER] for d in (w, m, v)], after)
            for d, arrays in zip((grads, delta, new_m, new_v), updates):
                for name, a in zip(SMALL_ORDER, arrays):
                    d[name] = a.reshape(w[name].shape)
            after = updates[1][0]
            continue
        grads[k], delta[k], new_m[k], new_v[k] = _adamw(w[k], _land(in_flight, k, chip, after), m[k], v[k], f"adamw_{k}")
        after = delta[k]
    loss = lax.psum(loss[0, 0], ("x", "y", "c"))

    outs = [loss, grad_x[None]]
    for d in (grads, delta, new_m, new_v):
        outs += [d[k][None] for k in ALL_ORDER]
    return tuple(outs)
```

```python
import jax
import jax.numpy as jnp
from jax import lax
from jax.experimental import pallas as pl
from jax.experimental.pallas import tpu as pltpu

f32 = jnp.float32
bf16 = jnp.bfloat16
SDS = jax.ShapeDtypeStruct

SEQ = 2048
D_MODEL = 2048
EPS = 1e-6
NEG = -1e30
HEAD_DIM = 128
ROT_HALF = 16
ROPE_THETA = 500000.0
DILATIONS = (1, 4, 16)
SPAN = 128
N_HEADS = 8
HALF = 1024
POOL_CH = 256
CONV_K = 31
CONV_PAD = 32
CHUNK = 128
N_CHIPS = 4
LANES = 256
E_IN_PIECES = 3
SMALL_SHARD_ROWS = 352
ANY = pl.BlockSpec(memory_space=pl.ANY)
MESH = pl.DeviceIdType.MESH

ADAM_LR = 0.001
ADAM_B1 = 0.9
ADAM_B2 = 0.999
ADAM_EPS = 1e-08
ADAM_WD = 0.01
ADAM_STEP = 10


def _dot(a, b):
    return jnp.dot(a, b, preferred_element_type=f32)


def _dot_nt(a, b):
    return lax.dot_general(a, b, (((1,), (1,)), ((), ())), preferred_element_type=f32)


def _dot_tn(a, b):
    return lax.dot_general(a, b, (((0,), (0,)), ((), ())), preferred_element_type=f32)


def _sigmoid(x):
    return 1.0 / (1.0 + jnp.exp(-x))


def _silu_and_grad(x):
    s = _sigmoid(x)
    return x * s, s * (1.0 + x * (1.0 - s))


def _rms_fwd(x, g):
    r = lax.rsqrt(jnp.mean(x * x, axis=-1, keepdims=True) + EPS)
    return x * r * g


def _rms_bwd(x, g, dout):
    r = lax.rsqrt(jnp.mean(x * x, axis=-1, keepdims=True) + EPS)
    xh = x * r
    dg = jnp.sum(dout * xh, axis=0, keepdims=True)
    dxh = dout * g
    dx = r * (dxh - xh * jnp.mean(dxh * xh, axis=-1, keepdims=True))
    return dx, dg


def _ln_stats(x):
    mu = jnp.mean(x, axis=-1, keepdims=True)
    xc = x - mu
    rstd = lax.rsqrt(jnp.mean(xc * xc, axis=-1, keepdims=True) + EPS)
    return xc * rstd, rstd


def _ln_bwd(xh, rstd, g, dout):
    dg = jnp.sum(dout * xh, axis=0, keepdims=True)
    db = jnp.sum(dout, axis=0, keepdims=True)
    dxh = dout * g
    dx = rstd * (dxh - jnp.mean(dxh, axis=-1, keepdims=True) - xh * jnp.mean(dxh * xh, axis=-1, keepdims=True))
    return dx, dg, db


def _accumulate(ref, value, first):
    @pl.when(first)
    def _():
        ref[...] = value

    @pl.when(jnp.logical_not(first))
    def _():
        ref[...] += value


def _write_behind(step, steps, tiles, sems, window):
    slot = step % 2

    def copies(s, at):
        return [pltpu.make_async_copy(tile.at[s], window(t, at), sems.at[2 * t + s]) for t, tile in enumerate(tiles)]

    @pl.when(step >= 2)
    def _():
        for cp in copies(slot, step - 2):
            cp.wait()

    def full():
        for cp in copies(slot, step):
            cp.start()

        @pl.when(step == steps - 1)
        def _():
            for cp in copies(slot, step):
                cp.wait()
            if steps > 1:
                for cp in copies(1 - slot, step - 1):
                    cp.wait()

    return [tile.at[slot] for tile in tiles], full


def _columns(ref, first, width):
    return ref.at[:, pl.ds(pl.multiple_of(first, 128), width)]


def _col_tile(ns):
    for t in (1024, 768, 512, 256):
        if ns % t == 0:
            return t
    raise ValueError(ns)


def _mm_nn(a, w, out_dtype, name, piece=0, pieces=1, into=None):
    m, k = a.shape
    j, _, ns = w.shape
    tm, tn = m, _col_tile(ns)
    nb = ns // tn

    def body(a_ref, w_ref, *rest):
        rest[-1][...] = _dot(a_ref[...], w_ref[...]).astype(rest[-1].dtype)

    return pl.pallas_call(
        body, name=name, grid=(j * nb, m // tm),
        in_specs=[pl.BlockSpec((tm, k), lambda n, i: (i, 0)),
                  pl.BlockSpec((None, k, tn), lambda n, i: (n // nb, 0, n % nb))] + ([] if into is None else [ANY]),
        out_specs=pl.BlockSpec((tm, tn), lambda n, i: (i, ((n // nb) * pieces + piece) * nb + n % nb)),
        out_shape=SDS((m, j * ns * pieces), out_dtype),
        input_output_aliases={} if into is None else {2: 0},
    )(a, w, *([] if into is None else [into]))


def _mm_nt(dz, ws, name, after):
    m, _ = dz.shape
    pieces = len(ws)
    j, k, ns = ws[0].shape
    tm, tk = 1024, 1024

    def body(dz_ref, *rest):
        w_refs, o_ref = rest[:pieces], rest[pieces + 1]
        total = _dot_nt(dz_ref[:, 0:ns], w_refs[0][...])
        for q in range(1, pieces):
            total = total + _dot_nt(dz_ref[:, q * ns:(q + 1) * ns], w_refs[q][...])
        if j == 1:
            o_ref[...] = total.astype(bf16)
            return
        sum_ref = rest[pieces + 2]
        r = pl.program_id(2)
        _accumulate(sum_ref, total, r == 0)

        @pl.when(r == j - 1)
        def _():
            o_ref[...] = sum_ref[...].astype(bf16)

    return pl.pallas_call(
        body, name=name, grid=(m // tm, k // tk, j),
        in_specs=[pl.BlockSpec((tm, pieces * ns), lambda i, kk, r: (i, r))]
        + [pl.BlockSpec((None, tk, ns), lambda i, kk, r: (r, kk, 0))] * pieces + [ANY],
        out_specs=pl.BlockSpec((tm, tk), lambda i, kk, r: (i, kk)),
        out_shape=SDS((m, k), bf16), scratch_shapes=[] if j == 1 else [pltpu.VMEM((tm, tk), f32)],
    )(dz, *ws, after)


def _mm_tn(a, dz, j, name):
    m, k = a.shape
    ns = dz.shape[1] // j
    tk, tn = 1024, _col_tile(ns)
    nb = ns // tn

    def body(a_ref, dz_ref, o_ref):
        o_ref[...] = _dot_tn(a_ref[...], dz_ref[...]).astype(o_ref.dtype)

    return pl.pallas_call(
        body, name=name, grid=(k // tk, j * nb),
        in_specs=[pl.BlockSpec((m, tk), lambda kk, n: (0, kk)),
                  pl.BlockSpec((m, tn), lambda kk, n: (0, n))],
        out_specs=pl.BlockSpec((None, tk, tn), lambda kk, n: (n // nb, kk, n % nb)),
        out_shape=SDS((j, k, ns), bf16),
    )(a, dz)


ROWS = 512


def _row_spec(width=D_MODEL, col=0):
    return pl.BlockSpec((ROWS, width), lambda i: (i, col))


def _vec_spec(width=D_MODEL):
    return pl.BlockSpec((1, width), lambda i: (0, 0))


def _pre_norm(x, g):
    def body(x_ref, g_ref, h_ref):
        h_ref[...] = _rms_fwd(x_ref[...], g_ref[...]).astype(bf16)

    return pl.pallas_call(
        body, name="pre_norm", grid=(SEQ // ROWS,), in_specs=[_row_spec(), _vec_spec()],
        out_specs=_row_spec(), out_shape=SDS((SEQ, D_MODEL), bf16))(x, g)


def _mid_norm(x, y, g_post, g_pre):
    def body(x_ref, y_ref, gpost_ref, gpre_ref, x1_ref, h1_ref):
        x1 = x_ref[...] + _rms_fwd(y_ref[...], gpost_ref[...])
        x1_ref[...] = x1
        h1_ref[...] = _rms_fwd(x1, gpre_ref[...]).astype(bf16)

    return pl.pallas_call(
        body, name="mid_norm", grid=(SEQ // ROWS,),
        in_specs=[_row_spec(), _row_spec(), _vec_spec(), _vec_spec()],
        out_specs=[_row_spec(), _row_spec()],
        out_shape=[SDS((SEQ, D_MODEL), f32), SDS((SEQ, D_MODEL), bf16)])(x, y, g_post, g_pre)


def _final_norm_loss(x1, y, g_post, target):
    def body(x1_ref, y_ref, g_ref, t_ref, loss_ref, dx2_ref, dy_ref, dg_ref):
        first = pl.program_id(0) == 0
        y = y_ref[...]
        g = g_ref[...]
        err = x1_ref[...] + _rms_fwd(y, g) - t_ref[...]
        sq = jnp.sum(jnp.sum(err * err, axis=1, keepdims=True), axis=0, keepdims=True)
        _accumulate(loss_ref, sq * (0.5 / D_MODEL), first)
        dx2 = err * (1.0 / D_MODEL)
        dx2_ref[...] = dx2
        dy, dg = _rms_bwd(y, g, dx2)
        dy_ref[...] = dy.astype(bf16)
        _accumulate(dg_ref, dg, first)

    return pl.pallas_call(
        body, name="final_norm_loss", grid=(SEQ // ROWS,),
        in_specs=[_row_spec(), _row_spec(), _vec_spec(), _row_spec()],
        out_specs=[pl.BlockSpec((1, 1), lambda i: (0, 0)), _row_spec(), _row_spec(), _vec_spec()],
        out_shape=[SDS((1, 1), f32), SDS((SEQ, D_MODEL), f32), SDS((SEQ, D_MODEL), bf16), SDS((1, D_MODEL), f32)],
    )(x1, y, g_post, target)


def _mid_norm_bwd(dx2, dh1, x1, y0, g_pre, g_post):
    def body(dx2_ref, dh1_ref, x1_ref, y0_ref, gpre_ref, gpost_ref, dx1_ref, dy0_ref, dgpre_ref, dgpost_ref):
        first = pl.program_id(0) == 0
        d_in, dgpre = _rms_bwd(x1_ref[...], gpre_ref[...], dh1_ref[...])
        dx1 = dx2_ref[...] + d_in
        dx1_ref[...] = dx1
        dy0, dgpost = _rms_bwd(y0_ref[...], gpost_ref[...], dx1)
        dy0_ref[...] = dy0.astype(bf16)
        _accumulate(dgpre_ref, dgpre, first)
        _accumulate(dgpost_ref, dgpost, first)

    return pl.pallas_call(
        body, name="mid_norm_bwd", grid=(SEQ // ROWS,),
        in_specs=[_row_spec(), _row_spec(), _row_spec(), _row_spec(), _vec_spec(), _vec_spec()],
        out_specs=[_row_spec(), _row_spec(), _vec_spec(), _vec_spec()],
        out_shape=[SDS((SEQ, D_MODEL), f32), SDS((SEQ, D_MODEL), bf16), SDS((1, D_MODEL), f32), SDS((1, D_MODEL), f32)],
    )(dx2, dh1, x1, y0, g_pre, g_post)


def _pre_norm_bwd(dx1, dh0, x, g):
    def body(dx1_ref, dh0_ref, x_ref, g_ref, dx_ref, dg_ref):
        d_in, dg = _rms_bwd(x_ref[...], g_ref[...], dh0_ref[...])
        dx_ref[...] = dx1_ref[...] + d_in
        _accumulate(dg_ref, dg, pl.program_id(0) == 0)

    return pl.pallas_call(
        body, name="pre_norm_bwd", grid=(SEQ // ROWS,),
        in_specs=[_row_spec(), _row_spec(), _row_spec(), _vec_spec()],
        out_specs=[_row_spec(), _vec_spec()],
        out_shape=[SDS((SEQ, D_MODEL), f32), SDS((1, D_MODEL), f32)])(dx1, dh0, x, g)


def _pool_count(g):
    row = lax.broadcasted_iota(jnp.int32, (SEQ, 1), 0)
    width = jnp.left_shift(2, g)
    return row, width, jnp.minimum(row + 1, width).astype(f32)


def _trailing_sum(x, row, width):
    s = x
    for k in (1, 2, 4, 8):
        shifted = jnp.where(row >= k, pltpu.roll(s, k, 0), 0.0)
        s = jnp.where(width > k, s + shifted, s)
    return s


def _leading_sum(x, row, width):
    s = x
    for k in (1, 2, 4, 8):
        shifted = jnp.where(row < SEQ - k, pltpu.roll(s, SEQ - k, 0), 0.0)
        s = jnp.where(width > k, s + shifted, s)
    return s


def _pool_specs():
    a_in = pl.BlockSpec((SEQ, POOL_CH), lambda g: (0, g))
    a_gate = pl.BlockSpec((SEQ, POOL_CH), lambda g: (0, 4 + g))
    w = pl.BlockSpec((None, POOL_CH, POOL_CH), lambda g: (g, 0, 0))
    scale = pl.BlockSpec((1, POOL_CH), lambda g: (0, g))
    return a_in, a_gate, w, scale


def _pool_fwd(z0, pool_w, pool_scale):
    def body(a_ref, gate_ref, w_ref, scale_ref, ya_ref):
        row, width, count = _pool_count(pl.program_id(0))
        a = a_ref[...]
        pooled = _trailing_sum(a, row, width) / count - a
        mixed = _dot(pooled.astype(bf16), w_ref[...]) * scale_ref[...]
        gate = gate_ref[...]
        ya_ref[...] = (mixed * gate * _sigmoid(gate)).astype(bf16)

    return pl.pallas_call(
        body, name="pool_fwd", grid=(4,), in_specs=list(_pool_specs()),
        out_specs=pl.BlockSpec((SEQ, POOL_CH), lambda g: (0, g)),
        out_shape=SDS((SEQ, 2 * HALF), bf16))(z0, z0, pool_w, pool_scale)


def _pool_bwd(z0, dcat, pool_w, pool_scale):
    def body(a_ref, gate_ref, w_ref, scale_ref, dya_ref, dz_ref, dw_ref, dscale_ref, da_tiles, dgate_tiles, sems):
        g = pl.program_id(0)
        (da_ref, dgate_ref), full = _write_behind(
            g, 4, [da_tiles, dgate_tiles], sems, lambda t, at: _columns(dz_ref, t * HALF + at * POOL_CH, POOL_CH))
        row, width, count = _pool_count(g)
        a = a_ref[...]
        pooled = (_trailing_sum(a, row, width) / count - a).astype(bf16)
        w = w_ref[...]
        scale = scale_ref[...]
        mixed = _dot(pooled, w)
        silu, dsilu = _silu_and_grad(gate_ref[...])
        dya = dya_ref[...]
        dgate_ref[...] = (dya * mixed * scale * dsilu).astype(bf16)
        dms = dya * silu
        dscale_ref[...] = jnp.sum(dms * mixed, axis=0, keepdims=True)
        dmixed = (dms * scale).astype(bf16)
        dw_ref[...] = _dot_tn(pooled, dmixed)
        dpooled = _dot_nt(dmixed, w)
        da_ref[...] = (_leading_sum(dpooled / count, row, width) - dpooled).astype(bf16)
        full()

    a_in, a_gate, w, scale = _pool_specs()
    col = pl.BlockSpec((SEQ, POOL_CH), lambda g: (0, g))
    tiles = pltpu.VMEM((2, SEQ, POOL_CH), bf16)
    return pl.pallas_call(
        body, name="pool_bwd", grid=(4,), in_specs=[a_in, a_gate, w, scale, col],
        out_specs=[ANY, w, scale],
        out_shape=[SDS((SEQ, 6 * D_MODEL), bf16), SDS((4, POOL_CH, POOL_CH), f32), SDS((1, HALF), f32)],
        scratch_shapes=[tiles, tiles, pltpu.SemaphoreType.DMA((4,))],
    )(z0, z0, pool_w, pool_scale, dcat)


Q_COL, K_COL, V_COL, BGATE_COL = 16, 40, 64, 88


def _rope_tables():
    pos = jnp.arange(SEQ, dtype=f32)
    inv_freq = jnp.power(ROPE_THETA, -jnp.arange(0, 2 * ROT_HALF, 2, dtype=f32) / (2 * ROT_HALF))
    ang = pos[:, None] * inv_freq[None, :]
    cos, sin = jnp.cos(ang), jnp.sin(ang)
    zeros = jnp.zeros((SEQ, HEAD_DIM - 2 * ROT_HALF), f32)
    cos_t = jnp.concatenate([cos, cos, zeros + 1.0], axis=1)
    sin_t = jnp.concatenate([sin, sin, zeros], axis=1)
    j = jnp.arange(HEAD_DIM)[:, None]
    i = jnp.arange(HEAD_DIM)[None, :]
    rot = jnp.where((i < ROT_HALF) & (j == i + ROT_HALF), -1.0, 0.0) + jnp.where(
        (i >= ROT_HALF) & (i < 2 * ROT_HALF) & (j == i - ROT_HALF), 1.0, 0.0)
    return cos_t, sin_t, rot.astype(bf16), rot.T.astype(bf16)


def _exact_dot(t, m):
    hi = t.astype(bf16)
    lo = (t - hi.astype(f32)).astype(bf16)
    return _dot(hi, m) + _dot(lo, m)


def _rope(t, cos_t, sin_t, rot):
    return t * cos_t + _exact_dot(t, rot) * sin_t


def _rope_transposed(d, cos_t, sin_t, rot_t):
    return d * cos_t + _exact_dot(d * sin_t, rot_t)


ROW_CHUNK = 256


def _chunks(fn):
    def step(i, carry):
        fn(pl.multiple_of(i * ROW_CHUNK, ROW_CHUNK))
        return carry

    lax.fori_loop(0, SEQ // ROW_CHUNK, step, 0, unroll=4)


def _pieces(dilation):
    length = SEQ // dilation
    n = min(length, ROW_CHUNK)
    return [(r, l0, n) for r in range(dilation) for l0 in range(0, length, n)]


def _by_residue(dst_ref, src_ref, dilation, dtype):
    length = SEQ // dilation
    for r, l0, n in _pieces(dilation):
        src = src_ref[l0:l0 + n, :] if dilation == 1 else src_ref[pl.ds(r + dilation * l0, n, stride=dilation), :]
        start = r * length + l0
        dst_ref[start:start + n, :] = src.astype(dtype)


def _by_position(dst_ref, src_ref, dilation):
    length = SEQ // dilation
    for r, l0, n in _pieces(dilation):
        src = src_ref[r * length + l0:r * length + l0 + n, :]
        if dilation == 1:
            dst_ref[l0:l0 + n, :] = src
        else:
            dst_ref[pl.ds(r + dilation * l0, n, stride=dilation), :] = src


def _attn_masks():
    qi = lax.broadcasted_iota(jnp.int32, (SPAN, 2 * SPAN), 0)
    kj = lax.broadcasted_iota(jnp.int32, (SPAN, 2 * SPAN), 1)
    window = ((kj < SPAN) & (kj >= qi)) | ((kj >= SPAN) & (kj - SPAN <= qi))
    own = lax.broadcasted_iota(jnp.int32, (SPAN, SPAN), 1) <= lax.broadcasted_iota(jnp.int32, (SPAN, SPAN), 0)
    return window, own


def _attn_blocks(dilation):
    per_residue = SEQ // dilation // SPAN
    blocks = [(c, c % per_residue != 0) for c in range(SEQ // SPAN)]
    return [blocks[i:i + 4] for i in range(0, len(blocks), 4)]


def _block_keys(c, has_prev):
    return slice((c - 1) * SPAN if has_prev else c * SPAN, (c + 1) * SPAN)


def _head_spec(col):
    return pl.BlockSpec((SEQ, HEAD_DIM), lambda h: (0, col + h))


def _table_spec():
    return pl.BlockSpec((SEQ, HEAD_DIM), lambda h: (0, 0))


def _attn_fwd(z0, tables, mixed):
    scale = HEAD_DIM ** -0.5

    def body(*refs):
        qkv = refs[0:9]
        bg_ref, cos_ref, sin_ref, rot_ref = refs[9:13]
        yb_ref, att_ref, lse_ref = refs[14:17]
        saved = refs[17:26]
        tmp_q, tmp_k, v_ones, o_res, l_res, o_nat, l_nat = refs[26:33]
        window_mask, own_mask = _attn_masks()
        rot = rot_ref[...]

        @pl.when(pl.program_id(0) == 0)
        def _():
            v_ones[:, HEAD_DIM:] = jnp.ones((SEQ, HEAD_DIM), bf16)

        for g, dilation in enumerate(DILATIONS):
            q_ref, k_ref, v_ref = qkv[3 * g:3 * g + 3]
            qd, kd, vd = saved[3 * g:3 * g + 3]

            def rope_rows(start, q_ref=q_ref, k_ref=k_ref):
                r = pl.ds(start, ROW_CHUNK)
                cos_t, sin_t = cos_ref[r, :], sin_ref[r, :]
                tmp_q[r, :] = _rope(q_ref[r, :], cos_t, sin_t, rot) * scale
                tmp_k[r, :] = _rope(k_ref[r, :], cos_t, sin_t, rot)

            _chunks(rope_rows)
            _by_residue(qd, tmp_q, dilation, bf16)
            _by_residue(kd, tmp_k, dilation, bf16)
            _by_residue(vd, v_ref, dilation, bf16)
            for l0 in range(0, SEQ, ROW_CHUNK):
                v_ones[l0:l0 + ROW_CHUNK, 0:HEAD_DIM] = vd[l0:l0 + ROW_CHUNK, :]

            for four in _attn_blocks(dilation):
                scores = [_dot_nt(qd[c * SPAN:(c + 1) * SPAN, :], kd[_block_keys(c, prev), :]) for c, prev in four]
                tops, probs = [], []
                for (c, prev), s in zip(four, scores):
                    s = jnp.where(window_mask if prev else own_mask, s, NEG)
                    tops.append(jnp.max(s, axis=1, keepdims=True))
                    probs.append(jnp.exp(s - tops[-1]).astype(bf16))
                sums = [_dot(p, v_ones[_block_keys(c, prev), :]) for (c, prev), p in zip(four, probs)]
                for (c, prev), m, o in zip(four, tops, sums):
                    den = o[:, HEAD_DIM:]
                    o_res[c * SPAN:(c + 1) * SPAN, :] = o[:, :HEAD_DIM] / den
                    l_res[c * SPAN:(c + 1) * SPAN, :] = m + jnp.log(den)

            if dilation > 1:
                _by_position(o_nat, o_res, dilation)
                _by_position(l_nat, l_res, dilation)
            o_g, l_g = (o_res, l_res) if dilation == 1 else (o_nat, l_nat)

            def merge(start, g=g, o_g=o_g, l_g=l_g):
                r = pl.ds(start, ROW_CHUNK)
                if g == 0:
                    att, total = o_g[r, :], l_g[r, :]
                else:
                    l_old, l_new = lse_ref[r, :], l_g[r, :]
                    top = jnp.maximum(l_old, l_new)
                    total = top + jnp.log(jnp.exp(l_old - top) + jnp.exp(l_new - top))
                    att = att_ref[r, :] * jnp.exp(l_old - total) + o_g[r, :] * jnp.exp(l_new - total)
                att_ref[r, :] = att
                lse_ref[r, :] = total
                if g == len(DILATIONS) - 1:
                    gate = bg_ref[r, :]
                    yb_ref[r, :] = (att * gate * _sigmoid(gate)).astype(bf16)

            _chunks(merge)

    in_specs = []
    for g in range(3):
        in_specs += [_head_spec(Q_COL + 8 * g), _head_spec(K_COL + 8 * g), _head_spec(V_COL + 8 * g)]
    in_specs += [_head_spec(BGATE_COL), _table_spec(), _table_spec(), pl.BlockSpec((HEAD_DIM, HEAD_DIM), lambda h: (0, 0)), ANY]
    out_spec = pl.BlockSpec((SEQ, HEAD_DIM), lambda h: (0, h))
    right_half = pl.BlockSpec((SEQ, HEAD_DIM), lambda h: (0, N_HEADS + h))
    vm = lambda dt: pltpu.VMEM((SEQ, HEAD_DIM), dt)
    cos_t, sin_t, rot, _ = tables
    out = pl.pallas_call(
        body, name="attn_fwd", grid=(N_HEADS,), in_specs=in_specs, out_specs=[right_half] + [out_spec] * 11,
        out_shape=[SDS((SEQ, 2 * HALF), bf16), SDS((SEQ, HALF), f32), SDS((SEQ, HALF), f32)] + [SDS((SEQ, HALF), bf16)] * 9,
        scratch_shapes=[vm(f32), vm(f32), pltpu.VMEM((SEQ, 2 * HEAD_DIM), bf16), vm(f32), vm(f32), vm(f32), vm(f32)],
        input_output_aliases={13: 0},
    )(*([z0] * 10), cos_t, sin_t, rot, mixed)
    return out[0], out[1], out[2], [tuple(out[3 + 3 * g:6 + 3 * g]) for g in range(3)]


def _attn_bwd_group(g, saved, z0, att, lse, dcat, tables, dz):
    scale = HEAD_DIM ** -0.5
    dilation = DILATIONS[g]
    with_gate = g == 0
    n_out = 4 if with_gate else 3
    first_col = (Q_COL + 8 * g, K_COL + 8 * g, V_COL + 8 * g, BGATE_COL)

    def body(*refs):
        qd, kd, vd, bg_ref, att_ref, lse_ref, dyb_ref, cos_ref, sin_ref, rot_t_ref = refs[0:10]
        dz_ref = refs[11]
        dod, ld, dd, tmp, aq, ak, av = refs[12:19]
        views, full = _write_behind(pl.program_id(0), N_HEADS, refs[19:19 + n_out], refs[19 + n_out],
                                    lambda t, at: _columns(dz_ref, (first_col[t] + at) * HEAD_DIM, HEAD_DIM))
        dq_ref, dk_ref, dv_ref = views[0:3]
        window_mask, own_mask = _attn_masks()
        rot_t = rot_t_ref[...]

        def gate_rows(start):
            r = pl.ds(start, ROW_CHUNK)
            silu, dsilu = _silu_and_grad(bg_ref[r, :])
            att_v = att_ref[r, :]
            dyb = dyb_ref[r, :]
            if with_gate:
                views[3][r, :] = (dyb * att_v * dsilu).astype(bf16)
            datt = dyb * silu
            tmp[r, :] = datt
            aq[r, :] = jnp.broadcast_to(jnp.sum(datt * att_v, axis=1, keepdims=True), (ROW_CHUNK, HEAD_DIM))

        _chunks(gate_rows)
        _by_residue(dod, tmp, dilation, bf16)
        _by_residue(dd, aq, dilation, f32)
        _by_residue(ld, lse_ref, dilation, f32)

        for four in _attn_blocks(dilation):
            rows = [slice(c * SPAN, (c + 1) * SPAN) for c, _ in four]
            keys = [_block_keys(c, prev) for c, prev in four]
            scores = [_dot_nt(qd[r, :], kd[k, :]) for r, k in zip(rows, keys)]
            dprobs = [_dot_nt(dod[r, :], vd[k, :]) for r, k in zip(rows, keys)]
            probs, dscores = [], []
            for (c, prev), r, s, dp in zip(four, rows, scores, dprobs):
                lse_q, delta = ld[r, :], dd[r, :]
                if prev:
                    lse_q = jnp.concatenate([lse_q, lse_q], axis=1)
                    delta = jnp.concatenate([delta, delta], axis=1)
                p = jnp.where(window_mask if prev else own_mask, jnp.exp(s - lse_q), 0.0)
                probs.append(p.astype(bf16))
                dscores.append((p * (dp - delta)).astype(bf16))
            dvs = [_dot_tn(p, dod[r, :]) for p, r in zip(probs, rows)]
            dks = [_dot_tn(ds, qd[r, :]) for ds, r in zip(dscores, rows)]
            dqs = [_dot(ds, kd[k, :]) for ds, k in zip(dscores, keys)]
            for (c, prev), r, dv, dk, dq in zip(four, rows, dvs, dks, dqs):
                aq[r, :] = dq
                if prev:
                    before = slice((c - 1) * SPAN, c * SPAN)
                    av[before, :] += dv[0:SPAN]
                    ak[before, :] += dk[0:SPAN]
                    av[r, :] = dv[SPAN:]
                    ak[r, :] = dk[SPAN:]
                else:
                    av[r, :] = dv
                    ak[r, :] = dk

        def finish(out_ref, acc, factor, roped):
            if dilation > 1:
                _by_position(tmp, acc, dilation)
            src = acc if dilation == 1 else tmp

            def rows(start):
                r = pl.ds(start, ROW_CHUNK)
                d = src[r, :]
                if factor != 1.0:
                    d = d * factor
                if roped:
                    d = _rope_transposed(d, cos_ref[r, :], sin_ref[r, :], rot_t)
                out_ref[r, :] = d.astype(bf16)

            _chunks(rows)

        finish(dq_ref, aq, scale, True)
        finish(dk_ref, ak, 1.0, True)
        finish(dv_ref, av, 1.0, False)
        full()

    head = pl.BlockSpec((SEQ, HEAD_DIM), lambda h: (0, h))
    in_specs = [head, head, head, _head_spec(BGATE_COL), head, head, _head_spec(8), _table_spec(), _table_spec(),
                pl.BlockSpec((HEAD_DIM, HEAD_DIM), lambda h: (0, 0)), ANY]
    vm = lambda dt: pltpu.VMEM((SEQ, HEAD_DIM), dt)
    cos_t, sin_t, _, rot_t = tables
    return pl.pallas_call(
        body, name=f"attn_bwd_g{g}", grid=(N_HEADS,), in_specs=in_specs, out_specs=ANY,
        out_shape=SDS(dz.shape, dz.dtype), input_output_aliases={10: 0},
        scratch_shapes=[vm(bf16), vm(f32), vm(f32), vm(f32), vm(f32), vm(f32), vm(f32)]
        + [pltpu.VMEM((2, SEQ, HEAD_DIM), bf16)] * n_out + [pltpu.SemaphoreType.DMA((2 * n_out,))],
    )(*saved, z0, att, lse, dcat, cos_t, sin_t, rot_t, dz)


def _sgu_specs():
    chunk = lambda col: pl.BlockSpec((CHUNK, HALF), lambda n: (n, col))
    vec = pl.BlockSpec((1, HALF), lambda n: (0, 0))
    w = pl.BlockSpec((4, CHUNK, CHUNK), lambda n: (0, 0, 0))
    bias = pl.BlockSpec((CHUNK, CHUNK), lambda n: (0, 0))
    return chunk, vec, w, bias


def _sgu_weights(w_ref):
    tril = lax.broadcasted_iota(jnp.int32, (CHUNK, CHUNK), 1) <= lax.broadcasted_iota(jnp.int32, (CHUNK, CHUNK), 0)
    return tril, [jnp.where(tril, w_ref[h], 0.0).astype(bf16) for h in range(4)]


def _sgu_fwd(z1, ln_g, ln_b, sgu_w, bias_t):
    def body(u_ref, v_ref, cg_ref, g_ref, b_ref, w_ref, bias_ref, yc_ref):
        _, ws = _sgu_weights(w_ref)
        xh, _ = _ln_stats(v_ref[...])
        vn = (xh * g_ref[...] + b_ref[...]).astype(bf16)
        for h in range(4):
            cols = slice(h * POOL_CH, (h + 1) * POOL_CH)
            s = _dot(ws[h], vn[:, cols]) + bias_ref[:, h:h + 1]
            gate = cg_ref[:, cols]
            yc_ref[:, cols] = (u_ref[:, cols] * s * gate * _sigmoid(gate)).astype(bf16)

    chunk, vec, w, bias = _sgu_specs()
    return pl.pallas_call(
        body, name="sgu_fwd", grid=(SEQ // CHUNK,),
        in_specs=[chunk(0), chunk(1), chunk(2), vec, vec, w, bias], out_specs=chunk(0),
        out_shape=SDS((SEQ, 2 * HALF), bf16))(z1, z1, z1, ln_g, ln_b, sgu_w, bias_t)


def _sgu_bwd(z1, dcat, ln_g, ln_b, sgu_w, bias_t):
    def body(u_ref, v_ref, cg_ref, dyc_ref, g_ref, b_ref, w_ref, bias_ref,
             dz_ref, dw_ref, dbias_ref, dg_ref, db_ref, dvn_ref, du_tiles, dv_tiles, dcg_tiles, sems):
        n = pl.program_id(0)
        (du_ref, dv_ref, dcg_ref), full = _write_behind(
            n, SEQ // CHUNK, [du_tiles, dv_tiles, dcg_tiles], sems,
            lambda t, at: dz_ref.at[pl.ds(pl.multiple_of(at * CHUNK, CHUNK), CHUNK), t * HALF:(t + 1) * HALF])
        first = n == 0
        tril, ws = _sgu_weights(w_ref)
        xh, rstd = _ln_stats(v_ref[...])
        g = g_ref[...]
        vn = (xh * g + b_ref[...]).astype(bf16)

        @pl.when(first)
        def _():
            dbias_ref[...] = jnp.zeros((CHUNK, CHUNK), f32)

        for h in range(4):
            cols = slice(h * POOL_CH, (h + 1) * POOL_CH)
            vn_h = vn[:, cols]
            s = _dot(ws[h], vn_h) + bias_ref[:, h:h + 1]
            silu, dsilu = _silu_and_grad(cg_ref[:, cols])
            dyc = dyc_ref[:, cols]
            u = u_ref[:, cols]
            du_ref[:, cols] = (dyc * s * silu).astype(bf16)
            dcg_ref[:, cols] = (dyc * u * s * dsilu).astype(bf16)
            ds = dyc * u * silu
            dbias_ref[:, h:h + 1] += jnp.sum(ds, axis=1, keepdims=True)
            ds = ds.astype(bf16)
            _accumulate(dw_ref.at[h], jnp.where(tril, _dot_nt(ds, vn_h), 0.0), first)
            dvn_ref[:, cols] = _dot_tn(ws[h], ds)
        dv, dg, db = _ln_bwd(xh, rstd, g, dvn_ref[...])
        dv_ref[...] = dv.astype(bf16)
        _accumulate(dg_ref, dg, first)
        _accumulate(db_ref, db, first)
        full()

    chunk, vec, w, bias = _sgu_specs()
    tiles = pltpu.VMEM((2, CHUNK, HALF), bf16)
    return pl.pallas_call(
        body, name="sgu_bwd", grid=(SEQ // CHUNK,),
        in_specs=[chunk(0), chunk(1), chunk(2), chunk(0), vec, vec, w, bias],
        out_specs=[ANY, w, bias, vec, vec],
        out_shape=[SDS((SEQ, 3 * D_MODEL), bf16), SDS((4, CHUNK, CHUNK), f32), SDS((CHUNK, CHUNK), f32),
                   SDS((1, HALF), f32), SDS((1, HALF), f32)],
        scratch_shapes=[pltpu.VMEM((CHUNK, HALF), f32), tiles, tiles, tiles, pltpu.SemaphoreType.DMA((6,))],
    )(z1, z1, z1, dcat, ln_g, ln_b, sgu_w, bias_t)


CONV_TILE = 128
DVAL_COL, DGLU_COL = 12, 16


def _conv_specs():
    val = pl.BlockSpec((SEQ, POOL_CH), lambda j: (0, DVAL_COL + j))
    glu = pl.BlockSpec((SEQ, POOL_CH), lambda j: (0, DGLU_COL + j))
    w = pl.BlockSpec((CONV_K, POOL_CH), lambda j: (0, j))
    col = pl.BlockSpec((SEQ, POOL_CH), lambda j: (0, j))
    vec = pl.BlockSpec((1, POOL_CH), lambda j: (0, j))
    return val, glu, w, col, vec


def _conv_fwd(z1, conv_w, conv_b):
    def body(val_ref, glu_ref, w_ref, b_ref, out_ref, xpad):
        xpad[0:CONV_PAD, :] = jnp.zeros((CONV_PAD, POOL_CH), f32)
        xpad[CONV_PAD:, :] = val_ref[...] * _sigmoid(glu_ref[...])
        w = w_ref[...]
        bias = b_ref[...]

        def tile(i, carry):
            t0 = pl.multiple_of(i * CONV_TILE, CONV_TILE)
            window = xpad[pl.ds(t0, CONV_TILE + CONV_PAD), :]
            acc = jnp.broadcast_to(bias, (CONV_TILE, POOL_CH))
            for k in range(CONV_K):
                shift = CONV_PAD - (CONV_K - 1) + k
                acc = acc + w[k:k + 1, :] * pltpu.roll(window, CONV_TILE + CONV_PAD - shift, 0)[0:CONV_TILE]
            out_ref[pl.ds(t0, CONV_TILE), :] = acc
            return carry

        lax.fori_loop(0, SEQ // CONV_TILE, tile, 0)

    val, glu, w, col, vec = _conv_specs()
    return pl.pallas_call(
        body, name="conv_fwd", grid=(4,), in_specs=[val, glu, w, vec], out_specs=col,
        out_shape=SDS((SEQ, HALF), f32), scratch_shapes=[pltpu.VMEM((SEQ + CONV_PAD, POOL_CH), f32)],
    )(z1, z1, conv_w, conv_b)


def _conv_bwd(z1, dconv, conv_w, dz):
    def body(val_ref, glu_ref, w_ref, dout_ref, dz_in, dz_ref, dw_ref, db_ref, xpad, dpad, dx_ref, dval_tiles, dglu_tiles, sems):
        j = pl.program_id(0)
        (dval_ref, dglu_ref), full = _write_behind(
            j, 4, [dval_tiles, dglu_tiles], sems,
            lambda t, at: _columns(dz_ref, ((DVAL_COL, DGLU_COL)[t] + at) * POOL_CH, POOL_CH))
        val = val_ref[...]
        sig = _sigmoid(glu_ref[...])
        xpad[0:CONV_PAD, :] = jnp.zeros((CONV_PAD, POOL_CH), f32)
        xpad[CONV_PAD:, :] = val * sig
        dout = dout_ref[...]
        dpad[0:SEQ, :] = dout
        dpad[SEQ:, :] = jnp.zeros((CONV_PAD, POOL_CH), f32)
        db_ref[...] = jnp.sum(dout, axis=0, keepdims=True)
        dw_ref[...] = jnp.zeros((CONV_K, POOL_CH), f32)
        w = w_ref[...]

        def tile(i, carry):
            t0 = pl.multiple_of(i * CONV_TILE, CONV_TILE)
            x_win = xpad[pl.ds(t0, CONV_TILE + CONV_PAD), :]
            d_win = dpad[pl.ds(t0, CONV_TILE + CONV_PAD), :]
            d_own = d_win[0:CONV_TILE]
            acc = jnp.zeros((CONV_TILE, POOL_CH), f32)
            for k in range(CONV_K):
                shift = CONV_PAD - (CONV_K - 1) + k
                x_k = pltpu.roll(x_win, CONV_TILE + CONV_PAD - shift, 0)[0:CONV_TILE]
                dw_ref[k:k + 1, :] += jnp.sum(d_own * x_k, axis=0, keepdims=True)
                back = CONV_K - 1 - k
                d_k = d_own if back == 0 else pltpu.roll(d_win, CONV_TILE + CONV_PAD - back, 0)[0:CONV_TILE]
                acc = acc + w[k:k + 1, :] * d_k
            dx_ref[pl.ds(t0, CONV_TILE), :] = acc
            return carry

        lax.fori_loop(0, SEQ // CONV_TILE, tile, 0)
        dx = dx_ref[...]
        dval_ref[...] = (dx * sig).astype(bf16)
        dglu_ref[...] = (dx * val * sig * (1.0 - sig)).astype(bf16)
        full()

    val, glu, w, col, vec = _conv_specs()
    pad = pltpu.VMEM((SEQ + CONV_PAD, POOL_CH), f32)
    tiles = pltpu.VMEM((2, SEQ, POOL_CH), bf16)
    return pl.pallas_call(
        body, name="conv_bwd", grid=(4,), in_specs=[val, glu, w, col, ANY], out_specs=[ANY, w, vec],
        out_shape=[SDS(dz.shape, dz.dtype), SDS((CONV_K, HALF), f32), SDS((1, HALF), f32)],
        input_output_aliases={4: 0},
        scratch_shapes=[pad, pad, pltpu.VMEM((SEQ, POOL_CH), f32), tiles, tiles, pltpu.SemaphoreType.DMA((4,))],
    )(z1, z1, conv_w, dconv, dz)


DGATE_COL = 5


def _conv_norm_fwd(conv, z1, g, b, mixed):
    def body(c_ref, gate_ref, g_ref, b_ref, mixed_ref, yd_ref):
        xh, _ = _ln_stats(c_ref[...])
        n = xh * g_ref[...] + b_ref[...]
        gate = gate_ref[...]
        yd_ref[...] = (n * _sigmoid(n) * gate * _sigmoid(gate)).astype(bf16)

    return pl.pallas_call(
        body, name="conv_norm_fwd", grid=(SEQ // ROWS,),
        in_specs=[_row_spec(HALF), _row_spec(HALF, DGATE_COL), _vec_spec(HALF), _vec_spec(HALF), ANY],
        out_specs=_row_spec(HALF, 1), out_shape=SDS((SEQ, 2 * HALF), bf16), input_output_aliases={4: 0},
    )(conv, z1, g, b, mixed)


def _conv_norm_bwd(conv, z1, dcat, g, b, dz):
    def body(c_ref, gate_ref, dyd_ref, g_ref, b_ref, dz_ref, dconv_ref, dgate_ref, dg_ref, db_ref):
        first = pl.program_id(0) == 0
        xh, rstd = _ln_stats(c_ref[...])
        g = g_ref[...]
        n_silu, n_dsilu = _silu_and_grad(xh * g + b_ref[...])
        gate_silu, gate_dsilu = _silu_and_grad(gate_ref[...])
        dyd = dyd_ref[...]
        dgate_ref[...] = (dyd * n_silu * gate_dsilu).astype(bf16)
        dconv, dg, db = _ln_bwd(xh, rstd, g, dyd * gate_silu * n_dsilu)
        dconv_ref[...] = dconv
        _accumulate(dg_ref, dg, first)
        _accumulate(db_ref, db, first)

    return pl.pallas_call(
        body, name="conv_norm_bwd", grid=(SEQ // ROWS,),
        in_specs=[_row_spec(HALF), _row_spec(HALF, DGATE_COL), _row_spec(HALF, 1), _vec_spec(HALF), _vec_spec(HALF), ANY],
        out_specs=[_row_spec(HALF), _row_spec(HALF, DGATE_COL), _vec_spec(HALF), _vec_spec(HALF)],
        out_shape=[SDS((SEQ, HALF), f32), SDS(dz.shape, dz.dtype), SDS((1, HALF), f32), SDS((1, HALF), f32)],
        input_output_aliases={5: 1},
    )(conv, z1, dcat, g, b, dz)


def _step(x, target, w, chip):
    chip_vec = chip.astype(jnp.int32).reshape(1)
    sharded_names = list(SHARDED_SMALL)
    first = [_cast_into_slot(w["e_w_in"], chip_vec, "cast_e_w_in0", w["e_pre_norm"], 0, E_IN_PIECES)]
    sems, bufs, token = _gather_start(first, "gather_start_first")
    small_shard = _pack([w[k] for k in sharded_names], total_rows=SMALL_SHARD_ROWS) + 0.0 * token[0, 0]
    small_slot = lax.dynamic_update_slice(jnp.zeros((N_CHIPS, SMALL_SHARD_ROWS, LANES), f32), small_shard[None], (chip, 0, 0))
    more = [small_slot]
    more += [_cast_into_slot(w["e_w_in"], chip_vec, f"cast_e_w_in{i}", token, i, E_IN_PIECES) for i in range(1, E_IN_PIECES)]
    more_sems, more_bufs, token = _gather_start(more, "gather_start_pieces")
    rest = [_cast_into_slot(w[k], chip_vec, f"cast_{k}", token) for k in BIG[1:]]
    rest_sems, rest_bufs, token = _gather_start(rest, "gather_start_rest")
    sems, bufs = sems + more_sems + rest_sems, bufs + more_bufs + rest_bufs
    tables = _rope_tables()

    def vec(k):
        return w[k].reshape(1, -1)

    h0 = _pre_norm(x, vec("e_pre_norm") + token[0, 0])
    after, z0, e_w_in = h0, None, []
    for i in range(E_IN_PIECES):
        group = slice(0, 1) if i == 0 else slice(1, 3) if i == 1 else slice(i + 1, i + 2)
        landed = _forward_halves(_gather_wait(bufs[group], sems[group], after, f"gather_wait_{i}"), f"forward_{i}")
        if i == 1:
            small_full = landed[0]
        e_w_in.append(landed[-1])
        z0 = _mm_nn(h0, landed[-1], f32, f"e_in{i}", i, E_IN_PIECES, z0)
        after = z0
    p = {k: _from_chips(k, a) for k, a in zip(sharded_names, _unpack(small_full, [SHARDED_SMALL[k][0] for k in sharded_names]))}
    for k in ("o_pre_norm", "o_sgu_norm_g", "o_sgu_norm_b", "o_conv_b", "o_conv_norm_g", "o_conv_norm_b", "o_post_norm"):
        p[k] = p[k].reshape(1, -1)
    pool_w_bf = p["e_pool_w"].astype(bf16)
    bias_t = jnp.pad(w["o_sgu_b"].T, ((0, 0), (0, CHUNK - 4)))

    cat0, att, lse, qkv_by_residue = _attn_fwd(z0, tables, _pool_fwd(z0, pool_w_bf, vec("e_pool_scale")))

    def arrived(index, after, name):
        one = slice(index, index + 1)
        return _forward_halves(_gather_wait(bufs[one], sems[one], after, f"gather_wait_{name}"), f"forward_{name}")[0]

    e_w_out = arrived(1 + E_IN_PIECES, att, "e_w_out").reshape(1, D_MODEL, D_MODEL)
    y0 = _mm_nn(cat0, e_w_out, f32, "e_out")
    x1, h1 = _mid_norm(x, y0, vec("e_post_norm"), p["o_pre_norm"])
    o_w_in = arrived(2 + E_IN_PIECES, h1, "o_w_in")
    z1 = _mm_nn(h1, o_w_in, f32, "o_in")
    yc = _sgu_fwd(z1, p["o_sgu_norm_g"], p["o_sgu_norm_b"], w["o_sgu_w"], bias_t)
    conv = _conv_fwd(z1, p["o_conv_w"], p["o_conv_b"])
    cat1 = _conv_norm_fwd(conv, z1, p["o_conv_norm_g"], p["o_conv_norm_b"], yc)
    o_w_out = arrived(3 + E_IN_PIECES, cat1, "o_w_out").reshape(1, D_MODEL, D_MODEL)
    y1 = _mm_nn(cat1, o_w_out, f32, "o_out")
    loss, dx2, dy1, g_o_post = _final_norm_loss(x1, y1, p["o_post_norm"], target)

    in_flight = {}

    def send_off(name, grad):
        sem, sums, land, tok = _scatter_start(_swap_add(grad, f"swap_add_{name}"), f"scatter_start_{name}")
        in_flight[name] = (sem, sums, land)
        return tok

    tok = send_off("o_w_out", _mm_tn(cat1, dy1, 1, "o_out_dw").reshape(N_CHIPS, HALF // 2, D_MODEL))
    dcat1 = _mm_nt(dy1, [o_w_out], "o_out_dx", tok)
    dz1, g_sgu_w, g_bias_t, g_sgu_g, g_sgu_b = _sgu_bwd(
        z1, dcat1, p["o_sgu_norm_g"] + tok[0, 0], p["o_sgu_norm_b"], w["o_sgu_w"], bias_t)
    dconv, dz1, g_cn_g, g_cn_b = _conv_norm_bwd(conv, z1, dcat1, p["o_conv_norm_g"], p["o_conv_norm_b"], dz1)
    dz1, g_conv_w, g_conv_b = _conv_bwd(z1, dconv, p["o_conv_w"], dz1)
    tok = send_off("o_w_in", _mm_tn(h1, dz1, N_CHIPS, "o_in_dw"))
    dh1 = _mm_nt(dz1, [o_w_in], "o_in_dx", tok)
    dx1, dy0, g_o_pre, g_e_post = _mid_norm_bwd(dx2, dh1, x1, y0, p["o_pre_norm"] + tok[0, 0], vec("e_post_norm"))

    tok = send_off("e_w_out", _mm_tn(cat0, dy0, 1, "e_out_dw").reshape(N_CHIPS, HALF // 2, D_MODEL))
    dcat0 = _mm_nt(dy0, [e_w_out], "e_out_dx", tok)
    dz0, g_pool_w, g_pool_scale = _pool_bwd(z0, dcat0, pool_w_bf, vec("e_pool_scale") + tok[0, 0])
    for g in range(len(DILATIONS)):
        dz0 = _attn_bwd_group(g, qkv_by_residue[g], z0, att, lse, dcat0, tables, dz0)
    tok = send_off("e_w_in", _mm_tn(h0, dz0, N_CHIPS, "e_in_dw"))
    dh0 = _mm_nt(dz0, e_w_in, "e_in_dx", tok)
    grad_x, g_e_pre = _pre_norm_bwd(dx1, dh0, x, vec("e_pre_norm") + tok[0, 0])

    small = {"e_pre_norm": g_e_pre, "e_pool_w": g_pool_w, "e_pool_scale": g_pool_scale, "e_post_norm": g_e_post,
             "o_pre_norm": g_o_pre, "o_sgu_norm_g": g_sgu_g, "o_sgu_norm_b": g_sgu_b, "o_sgu_w": g_sgu_w,
             "o_sgu_b": g_bias_t, "o_conv_w": g_conv_w, "o_conv_b": g_conv_b,
             "o_conv_norm_g": g_cn_g, "o_conv_norm_b": g_cn_b, "o_post_norm": g_o_post}
    return loss, grad_x, in_flight, small


def _land(in_flight, name, chip, after):
    sems, sums, land = in_flight[name]
    sums, land = _scatter_wait(sems, sums, land, after, f"scatter_wait_{name}")
    return _add_landed_join(sums, land, chip.astype(jnp.int32).reshape(1), f"add_landed_{name}")


def _place():
    x, y, c = lax.axis_index("x"), lax.axis_index("y"), lax.axis_index("c")
    others = [(1 - x, y), (x, 1 - y), (1 - x, 1 - y)]
    return x, y, c, 2 * x + y, others


SWAP_ROWS = 256
FORWARD_STAGE_BYTES = 4 << 20


def _swap_add(g, name):
    chips, r, c = g.shape
    half = r // 2
    rows_per_step = 2 * SWAP_ROWS if half % (2 * SWAP_ROWS) == 0 else SWAP_ROWS
    nb = half // rows_per_step
    steps = chips * nb

    def body(core_ref, mine_ref, theirs_ref, out_ref, landing, send_sems, recv_sems, free_sems):
        i = pl.program_id(0)
        x, y, core, _, _ = _place()
        sibling = (x, y, 1 - core)

        def send(slot):
            return pltpu.make_async_remote_copy(src_ref=theirs_ref, dst_ref=landing.at[slot], send_sem=send_sems.at[slot],
                                                recv_sem=recv_sems.at[slot], device_id=sibling, device_id_type=MESH)

        @pl.when(i < steps)
        def _():
            @pl.when(i >= 2)
            def _():
                pl.semaphore_wait(free_sems.at[i % 2], 1)

            send(i % 2).start()

        @pl.when(i >= 1)
        def _():
            landed = (i - 1) % 2
            send(landed).wait_recv()
            out_ref[...] = (mine_ref[...].astype(f32) + landing[landed].astype(f32)).astype(out_ref.dtype)

            @pl.when(i + 1 < steps)
            def _():
                pl.semaphore_signal(free_sems.at[landed], 1, device_id=sibling, device_id_type=MESH)

        @pl.when(i < steps)
        def _():
            send(i % 2).wait_send()

    def rows_of(b, h):
        return (2 * (b // nb) + h) * nb + b % nb

    block = (rows_per_step, c)
    grid_spec = pltpu.PrefetchScalarGridSpec(
        num_scalar_prefetch=1, grid=(steps + 1,),
        in_specs=[pl.BlockSpec(block, lambda i, core: (rows_of(jnp.maximum(i - 1, 0), core[0]), 0)),
                  pl.BlockSpec(block, lambda i, core: (rows_of(jnp.minimum(i, steps - 1), 1 - core[0]), 0))],
        out_specs=pl.BlockSpec(block, lambda i, core: (jnp.maximum(i - 1, 0), 0)),
        scratch_shapes=[pltpu.VMEM((2, rows_per_step, c), g.dtype), pltpu.SemaphoreType.DMA((2,)),
                        pltpu.SemaphoreType.DMA((2,)), pltpu.SemaphoreType.REGULAR((2,))])
    core = lax.axis_index("c").astype(jnp.int32).reshape(1)
    rows = g.reshape(chips * r, c)
    out = pl.pallas_call(body, name=name, grid_spec=grid_spec, out_shape=SDS((chips * half, c), g.dtype))(core, rows, rows)
    return out.reshape(chips, half, c)


HBM = pl.BlockSpec(memory_space=pltpu.HBM)
SEM = pl.BlockSpec(memory_space=pltpu.SEMAPHORE)
EFFECT = pltpu.SideEffectType.DATAFLOW_SIDE_EFFECTING


def _in_hbm(a):
    return pltpu.with_memory_space_constraint(a, pltpu.HBM)


def _cast_into_slot(w, chip, name, after, piece=0, pieces=1):
    r, c = w.shape
    c = c // pieces
    nb = r // SWAP_ROWS

    def body(chip_ref, w_ref, after_ref, o_ref):
        o_ref[...] = w_ref[...].astype(bf16)

    grid_spec = pltpu.PrefetchScalarGridSpec(
        num_scalar_prefetch=1, grid=(nb,),
        in_specs=[pl.BlockSpec((SWAP_ROWS, c), lambda i, chip: (i, piece)), ANY],
        out_specs=pl.BlockSpec((SWAP_ROWS, c), lambda i, chip: (chip[0] * nb + i, 0)))
    out = pl.pallas_call(body, name=name, grid_spec=grid_spec, out_shape=SDS((N_CHIPS * r, c), bf16))(chip, w, after)
    return out.reshape(N_CHIPS, r, c)


def _gather_start(bufs, name):
    n = len(bufs)

    def body(*refs):
        ins, sems, token = refs[:n], refs[n:3 * n], refs[4 * n]
        x, y, c, me, others = _place()
        for a in range(n):
            rows = ins[a].shape[1] // 2
            mine = ins[a].at[me, pl.ds(c * rows, rows), :]
            for k, (ox, oy) in enumerate(others):
                pltpu.make_async_remote_copy(src_ref=mine, dst_ref=mine, send_sem=sems[2 * a].at[k],
                                             recv_sem=sems[2 * a + 1].at[k], device_id=(ox, oy, c),
                                             device_id_type=MESH).start()
        token[...] = jnp.zeros_like(token)

    out = pl.pallas_call(
        body, name=name, in_specs=[HBM] * n,
        out_shape=(*[pltpu.SemaphoreType.DMA((3,))] * (2 * n), *[pltpu.HBM(b.shape, b.dtype) for b in bufs],
                   SDS((8, 128), f32)),
        out_specs=(*[SEM] * (2 * n), *[HBM] * n, pl.BlockSpec(memory_space=pltpu.VMEM)),
        input_output_aliases={a: 2 * n + a for a in range(n)},
        compiler_params=pltpu.CompilerParams(has_side_effects=EFFECT),
    )(*[_in_hbm(b) for b in bufs])
    return [(out[2 * a], out[2 * a + 1]) for a in range(n)], list(out[2 * n:3 * n]), out[3 * n]


def _gather_wait(bufs, sems, after, name):
    n = len(bufs)

    def body(*refs):
        ins, sem_refs = refs[:n], refs[n:3 * n]
        x, y, c, me, others = _place()
        for a in range(n):
            rows = ins[a].shape[1] // 2
            mine = ins[a].at[me, pl.ds(c * rows, rows), :]
            for k, (ox, oy) in enumerate(others):
                landed = ins[a].at[2 * ox + oy, pl.ds(c * rows, rows), :]
                copy = pltpu.make_async_remote_copy(src_ref=mine, dst_ref=landed, send_sem=sem_refs[2 * a].at[k],
                                                    recv_sem=sem_refs[2 * a + 1].at[k], device_id=(ox, oy, c),
                                                    device_id_type=MESH)
                copy.wait_send()
                copy.wait_recv()

    flat_sems = [s for pair in sems for s in pair]
    out = pl.pallas_call(
        body, name=name, in_specs=[HBM] * n + [SEM] * (2 * n) + [ANY],
        out_shape=tuple(pltpu.HBM(b.shape, b.dtype) for b in bufs), out_specs=tuple([HBM] * n),
        input_output_aliases={a: a for a in range(n)},
        compiler_params=pltpu.CompilerParams(has_side_effects=EFFECT),
    )(*bufs, *flat_sems, after)
    return list(out)


def _forward_halves(bufs, name):
    n = len(bufs)
    blocks = []
    for b in bufs:
        half = b.shape[1] // 2
        whole = half * b.shape[2] * b.dtype.itemsize <= FORWARD_STAGE_BYTES
        blocks.append((half, half if whole or half % SWAP_ROWS else SWAP_ROWS))
    work = [(a, k, b) for a in range(n) for k in range(3) for b in range(blocks[a][0] // blocks[a][1])]

    def body(*refs):
        outs, stages = refs[n:2 * n], refs[2 * n:3 * n]
        load_sems, send_sems, recv_sems = refs[3 * n:]
        x, y, c, me, others = _place()
        sibling = (x, y, 1 - c)

        def rows(item):
            a, k, b = item
            half, tr = blocks[a]
            ox, oy = others[k]
            return outs[a].at[2 * ox + oy, pl.ds(c * half + b * tr, tr), :]

        def load(s, item):
            return pltpu.make_async_copy(rows(item), stages[item[0]].at[s], load_sems.at[s])

        def send(s, item):
            return pltpu.make_async_remote_copy(src_ref=stages[item[0]].at[s], dst_ref=rows(item), send_sem=send_sems.at[s],
                                                recv_sem=recv_sems.at[item[0]], device_id=sibling, device_id_type=MESH)

        load(0, work[0]).start()
        for t, item in enumerate(work):
            s = t % 2
            load(s, item).wait()
            send(s, item).start()
            if t + 1 < len(work):
                if t >= 1:
                    send(1 - s, work[t - 1]).wait_send()
                load(1 - s, work[t + 1]).start()
        if len(work) > 1:
            send(len(work) % 2, work[-2]).wait_send()
        send((len(work) - 1) % 2, work[-1]).wait_send()
        for a in range(n):
            theirs = outs[a].at[pl.ds(0, 3), pl.ds(0, blocks[a][0]), :]
            pltpu.make_async_remote_copy(src_ref=theirs, dst_ref=theirs, send_sem=send_sems.at[0], recv_sem=recv_sems.at[a],
                                         device_id=sibling, device_id_type=MESH).wait_recv()

    out = pl.pallas_call(
        body, name=name, in_specs=[ANY] * n, out_specs=[ANY] * n, out_shape=[SDS(b.shape, b.dtype) for b in bufs],
        input_output_aliases={a: a for a in range(n)},
        scratch_shapes=[pltpu.VMEM((2, blocks[a][1], bufs[a].shape[2]), bufs[a].dtype) for a in range(n)]
        + [pltpu.SemaphoreType.DMA((2,)), pltpu.SemaphoreType.DMA((2,)), pltpu.SemaphoreType.DMA((n,))],
    )(*bufs)
    return list(out)


def _scatter_start(chip_sums, name):
    def body(a_ref, land_ref, send_sems, recv_sems, a_thru, land_thru, token):
        x, y, c, me, others = _place()
        for k, (ox, oy) in enumerate(others):
            pltpu.make_async_remote_copy(src_ref=a_ref.at[2 * ox + oy], dst_ref=land_ref.at[me], send_sem=send_sems.at[k],
                                         recv_sem=recv_sems.at[k], device_id=(ox, oy, c), device_id_type=MESH).start()
        token[...] = jnp.zeros_like(token)

    shape = pltpu.HBM(chip_sums.shape, chip_sums.dtype)
    send, recv, a_thru, land, token = pl.pallas_call(
        body, name=name, in_specs=[HBM, HBM],
        out_shape=(pltpu.SemaphoreType.DMA((3,)), pltpu.SemaphoreType.DMA((3,)), shape, shape, SDS((8, 128), f32)),
        out_specs=(SEM, SEM, HBM, HBM, pl.BlockSpec(memory_space=pltpu.VMEM)), input_output_aliases={0: 2, 1: 3},
        compiler_params=pltpu.CompilerParams(has_side_effects=EFFECT),
    )(_in_hbm(chip_sums), _in_hbm(lax.empty(chip_sums.shape, chip_sums.dtype)))
    return (send, recv), a_thru, land, token


def _scatter_wait(sems, chip_sums, land, after, name):
    def body(a_ref, land_ref, send_sems, recv_sems, after_ref, a_out, land_out):
        x, y, c, me, others = _place()
        for k, (ox, oy) in enumerate(others):
            copy = pltpu.make_async_remote_copy(
                src_ref=a_ref.at[2 * ox + oy], dst_ref=land_ref.at[2 * ox + oy], send_sem=send_sems.at[k],
                recv_sem=recv_sems.at[k], device_id=(ox, oy, c), device_id_type=MESH)
            copy.wait_send()
            copy.wait_recv()

    shape = pltpu.HBM(chip_sums.shape, chip_sums.dtype)
    return pl.pallas_call(
        body, name=name, in_specs=[HBM, HBM, SEM, SEM, ANY], out_shape=(shape, shape), out_specs=(HBM, HBM),
        input_output_aliases={0: 0, 1: 1}, compiler_params=pltpu.CompilerParams(has_side_effects=EFFECT),
    )(chip_sums, land, sems[0], sems[1], after)


def _add_landed_join(chip_sums, land, chip, name):
    chips, rh, c = chip_sums.shape
    nb = rh // SWAP_ROWS

    def body(chip_ref, own_ref, l1_ref, l2_ref, l3_ref, out_hbm, buf, send_sems, recv_sem, local_sems):
        i = pl.program_id(0)
        slot = i % 2
        x, y, core, _, _ = _place()
        sibling = (x, y, 1 - core)

        def copies(s, step):
            rows = pl.ds(pl.multiple_of((core * nb + step) * SWAP_ROWS, SWAP_ROWS), SWAP_ROWS)
            keep = pltpu.make_async_copy(buf.at[s], out_hbm.at[rows, :], local_sems.at[s])
            give = pltpu.make_async_remote_copy(src_ref=buf.at[s], dst_ref=out_hbm.at[rows, :], send_sem=send_sems.at[s],
                                                recv_sem=recv_sem.at[0], device_id=sibling, device_id_type=MESH)
            return keep, give

        def drain(s, step):
            keep, give = copies(s, step)
            keep.wait()
            give.wait_send()

        @pl.when(i >= 2)
        def _():
            drain(slot, i - 2)

        buf[slot] = ((own_ref[...].astype(f32) + l1_ref[...].astype(f32)) + l2_ref[...].astype(f32)) + l3_ref[...].astype(f32)
        keep, give = copies(slot, i)
        keep.start()
        give.start()

        @pl.when(i == nb - 1)
        def _():
            drain(slot, i)
            if nb > 1:
                drain(1 - slot, i - 1)
            theirs = out_hbm.at[pl.ds((1 - core) * rh, rh), :]
            pltpu.make_async_remote_copy(src_ref=theirs, dst_ref=theirs, send_sem=send_sems.at[0], recv_sem=recv_sem.at[0],
                                         device_id=sibling, device_id_type=MESH).wait_recv()

    block = (SWAP_ROWS, c)
    from_slot = lambda d: pl.BlockSpec(block, lambda i, chip: (((chip[0] + d) % chips) * nb + i, 0))
    grid_spec = pltpu.PrefetchScalarGridSpec(
        num_scalar_prefetch=1, grid=(nb,), in_specs=[from_slot(0), from_slot(1), from_slot(2), from_slot(3)],
        out_specs=ANY,
        scratch_shapes=[pltpu.VMEM((2, SWAP_ROWS, c), f32), pltpu.SemaphoreType.DMA((2,)),
                        pltpu.SemaphoreType.DMA((1,)), pltpu.SemaphoreType.DMA((2,))])
    land_rows = land.reshape(chips * rh, c)
    return pl.pallas_call(body, name=name, grid_spec=grid_spec, out_shape=SDS((2 * rh, c), f32))(
        chip, chip_sums.reshape(chips * rh, c), land_rows, land_rows, land_rows)


def _adamw_update(w_ref, g_ref, m_ref, v_ref, d_ref, nm_ref, nv_ref):
    g = g_ref[...]
    nm = ADAM_B1 * m_ref[...] + (1.0 - ADAM_B1) * g
    nv = ADAM_B2 * v_ref[...] + (1.0 - ADAM_B2) * (g * g)
    nm_ref[...] = nm
    nv_ref[...] = nv
    m_hat = nm / (1.0 - ADAM_B1 ** ADAM_STEP)
    v_hat = nv / (1.0 - ADAM_B2 ** ADAM_STEP)
    d_ref[...] = -ADAM_LR * (m_hat / (jnp.sqrt(v_hat) + ADAM_EPS) + ADAM_WD * w_ref[...])


def _adamw(w, g, m, v, name):
    r, c = w.shape
    tr = 128 if r % 128 == 0 else r

    def body(w_ref, g_ref, m_ref, v_ref, g_out_ref, d_ref, nm_ref, nv_ref):
        g_out_ref[...] = g_ref[...]
        _adamw_update(w_ref, g_ref, m_ref, v_ref, d_ref, nm_ref, nv_ref)

    spec = pl.BlockSpec((tr, c), lambda i: (i, 0))
    return pl.pallas_call(body, name=name, grid=(r // tr,), in_specs=[spec] * 4, out_specs=[spec] * 4,
                          out_shape=[SDS((r, c), f32)] * 4)(w, g, m, v)


SMALL_PACKING = {
    "e_pre_norm": ((1, 2048), 8, (1, 2048)), "e_pool_w": ((1024, 256), 1024, (256, 256)),
    "e_pool_scale": ((1, 1024), 8, (1, 1024)), "e_post_norm": ((1, 2048), 8, (1, 2048)),
    "o_pre_norm": ((1, 2048), 8, (1, 512)), "o_sgu_norm_g": ((1, 1024), 8, (1, 256)),
    "o_sgu_norm_b": ((1, 1024), 8, (1, 256)), "o_sgu_w": ((512, 128), 512, (512, 128)),
    "o_sgu_b": ((128, 128), 8, (4, 128)), "o_conv_w": ((31, 1024), 128, (31, 256)), "o_conv_b": ((1, 1024), 8, (1, 256)),
    "o_conv_norm_g": ((1, 1024), 8, (1, 256)), "o_conv_norm_b": ((1, 1024), 8, (1, 256)),
    "o_post_norm": ((1, 2048), 8, (1, 512)),
}
SMALL_PACKED_ROWS = 1792


def _small_finalize(grads, ws, ms, vs, after):
    names = list(SMALL_ORDER)
    n = len(names)
    half, piece = SMALL_PACKED_ROWS // 2, SMALL_PACKED_ROWS // 8
    first_row, row = {}, 0
    for k in names:
        first_row[k] = row
        row += SMALL_PACKING[k][1]

    def body(*refs):
        g_refs, total = refs[0:n], refs[n + 1]
        pack, from_sibling, from_chips, send_a, recv_a, send_b, recv_b, send_c, recv_c, send_d, recv_d = refs[n + 2:]
        x, y, c, me, others = _place()

        for r0 in range(0, SMALL_PACKED_ROWS, piece):
            pack[r0:r0 + piece, :] = jnp.zeros((piece, LANES), f32)
        for k, g_ref in zip(names, g_refs):
            (rows, width), _, _ = SMALL_PACKING[k]
            r0 = first_row[k]
            if k == "o_sgu_b":
                pack[r0:r0 + 4, 0:CHUNK] = g_ref[...].T[0:4, :]
            elif width < LANES:
                pack[r0:r0 + rows, 0:width] = g_ref[...]
            else:
                for j in range(width // LANES):
                    dst = r0 + j * (1 if rows == 1 else 32)
                    pack[dst:dst + rows, :] = g_ref[:, j * LANES:(j + 1) * LANES]

        sibling = (x, y, 1 - c)
        swap = pltpu.make_async_remote_copy(
            src_ref=pack.at[pl.ds(pl.multiple_of((1 - c) * half, 8), half), :], dst_ref=from_sibling,
            send_sem=send_a.at[0], recv_sem=recv_a.at[0], device_id=sibling, device_id_type=MESH)
        swap.start()
        swap.wait()
        for j in range(4):
            rows = pl.ds(pl.multiple_of(c * half + j * piece, 8), piece)
            pack[rows, :] = pack[rows, :] + from_sibling[j * piece:(j + 1) * piece, :]

        def piece_of(chip):
            return pl.ds(pl.multiple_of(c * half + chip * piece, 8), piece)

        def to_chip(k):
            ox, oy = others[k]
            return pltpu.make_async_remote_copy(
                src_ref=pack.at[piece_of(2 * ox + oy), :], dst_ref=from_chips.at[me], send_sem=send_b.at[k],
                recv_sem=recv_b.at[k], device_id=(ox, oy, c), device_id_type=MESH)

        for k in range(3):
            to_chip(k).start()
        from_chips[me] = pack[piece_of(me), :]
        for k, (ox, oy) in enumerate(others):
            landed = from_chips.at[2 * ox + oy]
            pltpu.make_async_remote_copy(src_ref=landed, dst_ref=landed, send_sem=send_b.at[k], recv_sem=recv_b.at[k],
                                         device_id=(ox, oy, c), device_id_type=MESH).wait_recv()
        for k in range(3):
            to_chip(k).wait_send()
        mine = pl.ds(pl.multiple_of(c * half + me * piece, 8), piece)
        total[mine, :] = ((from_chips[0] + from_chips[1]) + from_chips[2]) + from_chips[3]

        def to_same_core(k):
            ox, oy = others[k]
            return pltpu.make_async_remote_copy(
                src_ref=total.at[mine, :], dst_ref=total.at[mine, :], send_sem=send_c.at[k], recv_sem=recv_c.at[k],
                device_id=(ox, oy, c), device_id_type=MESH)

        for k in range(3):
            to_same_core(k).start()
        for k, (ox, oy) in enumerate(others):
            theirs = total.at[piece_of(2 * ox + oy), :]
            pltpu.make_async_remote_copy(src_ref=theirs, dst_ref=theirs, send_sem=send_c.at[k], recv_sem=recv_c.at[k],
                                         device_id=(ox, oy, c), device_id_type=MESH).wait_recv()
        for k in range(3):
            to_same_core(k).wait_send()
        my_half = total.at[pl.ds(pl.multiple_of(c * half, 8), half), :]
        join = pltpu.make_async_remote_copy(src_ref=my_half, dst_ref=my_half, send_sem=send_d.at[0], recv_sem=recv_d.at[0],
                                            device_id=sibling, device_id_type=MESH)
        join.start()
        their_half = total.at[pl.ds(pl.multiple_of((1 - c) * half, 8), half), :]
        pltpu.make_async_remote_copy(src_ref=their_half, dst_ref=their_half, send_sem=send_d.at[0], recv_sem=recv_d.at[0],
                                     device_id=sibling, device_id_type=MESH).wait_recv()
        join.wait_send()

    whole = pl.BlockSpec(memory_space=pltpu.VMEM)
    total = pl.pallas_call(
        body, name="small_allreduce", in_specs=[whole] * n + [ANY], out_specs=whole,
        out_shape=SDS((SMALL_PACKED_ROWS, LANES), f32),
        scratch_shapes=[pltpu.VMEM((SMALL_PACKED_ROWS, LANES), f32), pltpu.VMEM((half, LANES), f32),
                        pltpu.VMEM((N_CHIPS, piece, LANES), f32),
                        pltpu.SemaphoreType.DMA((1,)), pltpu.SemaphoreType.DMA((1,)), pltpu.SemaphoreType.DMA((3,)),
                        pltpu.SemaphoreType.DMA((3,)), pltpu.SemaphoreType.DMA((3,)), pltpu.SemaphoreType.DMA((3,)),
                        pltpu.SemaphoreType.DMA((1,)), pltpu.SemaphoreType.DMA((1,))],
    )(*grads, after)

    def update(*refs):
        total = refs[0]
        w_refs, m_refs, v_refs = refs[1:n + 1], refs[n + 1:2 * n + 1], refs[2 * n + 1:3 * n + 1]
        outs = refs[3 * n + 1:]
        me = 2 * lax.axis_index("x") + lax.axis_index("y")

        def of_chip(candidates):
            value = candidates[0]
            for j in range(1, N_CHIPS):
                value = jnp.where(me == j, candidates[j], value)
            return value

        for i, k in enumerate(names):
            (rows, width), _, (local_rows, local_width) = SMALL_PACKING[k]
            r0 = first_row[k]
            if k == "o_sgu_b":
                g = total[r0:r0 + 4, 0:CHUNK]
            elif k == "e_pool_w":
                for grp in range(4):
                    src = pl.ds(pl.multiple_of(r0 + grp * POOL_CH + me * 64, 8), 64)
                    dst = slice(grp * 64, (grp + 1) * 64)
                    _adamw_rows(total[src, :], i, dst, w_refs, m_refs, v_refs, outs, n)
                continue
            elif k == "o_conv_w":
                g = total[pl.ds(pl.multiple_of(r0 + me * 32, 8), 32), :][0:CONV_K]
            elif width < LANES:
                g = total[r0:r0 + rows, 0:width]
            else:
                lanes = [total[r0 + j:r0 + j + 1, :] for j in range(width // LANES)]
                per_chip = local_width // LANES
                if local_width == width:
                    g = jnp.concatenate(lanes, axis=1)
                elif per_chip == 1:
                    g = of_chip(lanes)
                else:
                    g = of_chip([jnp.concatenate(lanes[j * per_chip:(j + 1) * per_chip], axis=1) for j in range(N_CHIPS)])
            _adamw_rows(g, i, slice(None), w_refs, m_refs, v_refs, outs, n)

    shard_shapes = [SMALL_PACKING[k][2] for k in names]
    out = pl.pallas_call(update, name="small_update", in_specs=[whole] * (3 * n + 1), out_specs=[whole] * (4 * n),
                         out_shape=[SDS(s, f32) for s in shard_shapes] * 4)(total, *ws, *ms, *vs)
    return out[:n], out[n:2 * n], out[2 * n:3 * n], out[3 * n:]


def _adamw_rows(g, i, rows, w_refs, m_refs, v_refs, outs, n):
    w, m, v = w_refs[i][rows, :], m_refs[i][rows, :], v_refs[i][rows, :]
    nm = ADAM_B1 * m + (1.0 - ADAM_B1) * g
    nv = ADAM_B2 * v + (1.0 - ADAM_B2) * (g * g)
    m_hat = nm / (1.0 - ADAM_B1 ** ADAM_STEP)
    v_hat = nv / (1.0 - ADAM_B2 ** ADAM_STEP)
    outs[i][rows, :] = g
    outs[n + i][rows, :] = -ADAM_LR * (m_hat / (jnp.sqrt(v_hat) + ADAM_EPS) + ADAM_WD * w)
    outs[2 * n + i][rows, :] = nm
    outs[3 * n + i][rows, :] = nv


def _pack(arrays, total_rows=None):
    parts = []
    rows = 0
    for a in arrays:
        flat = a.reshape(-1, LANES)
        pad = -flat.shape[0] % 8
        parts.append(jnp.pad(flat, ((0, pad), (0, 0))))
        rows += flat.shape[0] + pad
    if total_rows is not None:
        parts.append(jnp.zeros((total_rows - rows, LANES), arrays[0].dtype))
    return jnp.concatenate(parts, axis=0)


def _unpack(buf, shapes):
    out = []
    row = 0
    lead = buf.shape[:-2]
    for shape in shapes:
        size = 1
        for s in shape:
            size *= s
        rows = size // LANES
        out.append(buf[..., row:row + rows, :].reshape(lead + tuple(shape)))
        row += rows + (-rows % 8)
    return out


BIG = ("e_w_in", "e_w_out", "o_w_in", "o_w_out")
SHARDED_SMALL = {
    "e_pool_w": ((4, 64, 256), 1), "o_pre_norm": ((512,), 0), "o_sgu_norm_g": ((256,), 0), "o_sgu_norm_b": ((256,), 0),
    "o_conv_w": ((31, 256), 1), "o_conv_b": ((256,), 0), "o_conv_norm_g": ((256,), 0), "o_conv_norm_b": ((256,), 0),
    "o_post_norm": ((512,), 0),
}
SMALL_ORDER = ("e_pre_norm", "e_pool_w", "e_pool_scale", "e_post_norm", "o_pre_norm", "o_sgu_norm_g", "o_sgu_norm_b",
               "o_sgu_w", "o_sgu_b", "o_conv_w", "o_conv_b", "o_conv_norm_g", "o_conv_norm_b", "o_post_norm")
ALL_ORDER = ("e_pre_norm", "e_w_in", "e_pool_w", "e_pool_scale", "e_w_out", "e_post_norm", "o_pre_norm", "o_w_in",
             "o_sgu_norm_g", "o_sgu_norm_b", "o_sgu_w", "o_sgu_b", "o_conv_w", "o_conv_b", "o_conv_norm_g",
             "o_conv_norm_b", "o_w_out", "o_post_norm")


def _full_shape(name):
    shape, axis = SHARDED_SMALL[name]
    return tuple(s * N_CHIPS if i == axis else s for i, s in enumerate(shape))


def _from_chips(name, stacked):
    shape, axis = SHARDED_SMALL[name]
    return jnp.moveaxis(stacked, 0, axis).reshape(_full_shape(name))


def kernel(x, e_pre_norm, e_w_in, e_pool_w, e_pool_scale, e_w_out, e_post_norm, o_pre_norm, o_w_in, o_sgu_norm_g, o_sgu_norm_b, o_sgu_w, o_sgu_b, o_conv_w, o_conv_b, o_conv_norm_g, o_conv_norm_b, o_w_out, o_post_norm, loss_target, m_e_pre_norm, m_e_w_in, m_e_pool_w, m_e_pool_scale, m_e_w_out, m_e_post_norm, m_o_pre_norm, m_o_w_in, m_o_sgu_norm_g, m_o_sgu_norm_b, m_o_sgu_w, m_o_sgu_b, m_o_conv_w, m_o_conv_b, m_o_conv_norm_g, m_o_conv_norm_b, m_o_w_out, m_o_post_norm, v_e_pre_norm, v_e_w_in, v_e_pool_w, v_e_pool_scale, v_e_w_out, v_e_post_norm, v_o_pre_norm, v_o_w_in, v_o_sgu_norm_g, v_o_sgu_norm_b, v_o_sgu_w, v_o_sgu_b, v_o_conv_w, v_o_conv_b, v_o_conv_norm_g, v_o_conv_norm_b, v_o_w_out, v_o_post_norm):
    w = dict(e_pre_norm=e_pre_norm, e_w_in=e_w_in, e_pool_w=e_pool_w, e_pool_scale=e_pool_scale, e_w_out=e_w_out,
             e_post_norm=e_post_norm, o_pre_norm=o_pre_norm, o_w_in=o_w_in, o_sgu_norm_g=o_sgu_norm_g,
             o_sgu_norm_b=o_sgu_norm_b, o_sgu_w=o_sgu_w, o_sgu_b=o_sgu_b, o_conv_w=o_conv_w, o_conv_b=o_conv_b,
             o_conv_norm_g=o_conv_norm_g, o_conv_norm_b=o_conv_norm_b, o_w_out=o_w_out, o_post_norm=o_post_norm)
    m = dict(e_pre_norm=m_e_pre_norm, e_w_in=m_e_w_in, e_pool_w=m_e_pool_w, e_pool_scale=m_e_pool_scale,
             e_w_out=m_e_w_out, e_post_norm=m_e_post_norm, o_pre_norm=m_o_pre_norm, o_w_in=m_o_w_in,
             o_sgu_norm_g=m_o_sgu_norm_g, o_sgu_norm_b=m_o_sgu_norm_b, o_sgu_w=m_o_sgu_w, o_sgu_b=m_o_sgu_b,
             o_conv_w=m_o_conv_w, o_conv_b=m_o_conv_b, o_conv_norm_g=m_o_conv_norm_g, o_conv_norm_b=m_o_conv_norm_b,
             o_w_out=m_o_w_out, o_post_norm=m_o_post_norm)
    v = dict(e_pre_norm=v_e_pre_norm, e_w_in=v_e_w_in, e_pool_w=v_e_pool_w, e_pool_scale=v_e_pool_scale,
             e_w_out=v_e_w_out, e_post_norm=v_e_post_norm, o_pre_norm=v_o_pre_norm, o_w_in=v_o_w_in,
             o_sgu_norm_g=v_o_sgu_norm_g, o_sgu_norm_b=v_o_sgu_norm_b, o_sgu_w=v_o_sgu_w, o_sgu_b=v_o_sgu_b,
             o_conv_w=v_o_conv_w, o_conv_b=v_o_conv_b, o_conv_norm_g=v_o_conv_norm_g, o_conv_norm_b=v_o_conv_norm_b,
             o_w_out=v_o_w_out, o_post_norm=v_o_post_norm)
    w, m, v = ({k: a[0] for k, a in d.items()} for d in (w, m, v))
    chip = 2 * lax.axis_index("x") + lax.axis_index("y")

    loss, grad_x, in_flight, small = _step(x[0], loss_target[0], w, chip)

    grads, delta, new_m, new_v = {}, {}, {}, {}

    def rows_of(a):
        return a.reshape(-1, a.shape[-1])

    after = grad_x
    for k in ("o_w_out", "o_w_in", "e_w_out", "small", "e_w_in"):
        if k == "small":
            small_grads = [small[name].reshape(SMALL_PACKING[name][0]) for name in SMALL_ORDER]
            updates = _small_finalize(small_grads, *[[rows_of(d[name]) for name in SMALL_ORDER] for d in (w, m, v)], after)
            for d, arrays in zip((grads, delta, new_m, new_v), updates):
                for name, a in zip(SMALL_ORDER, arrays):
                    d[name] = a.reshape(w[name].shape)
            after = updates[1][0]
            continue
        grads[k], delta[k], new_m[k], new_v[k] = _adamw(w[k], _land(in_flight, k, chip, after), m[k], v[k], f"adamw_{k}")
        after = delta[k]
    loss = lax.psum(loss[0, 0], ("x", "y", "c"))

    outs = [loss, grad_x[None]]
    for d in (grads, delta, new_m, new_v):
        outs += [d[k][None] for k in ALL_ORDER]
    return tuple(outs)
```

```python
import jax
import jax.numpy as jnp
from jax import lax
from jax.experimental import pallas as pl
from jax.experimental.pallas import tpu as pltpu

f32 = jnp.float32
bf16 = jnp.bfloat16
SDS = jax.ShapeDtypeStruct

SEQ = 2048
D_MODEL = 2048
EPS = 1e-6
NEG = -1e30
HEAD_DIM = 128
ROT_HALF = 16
ROPE_THETA = 500000.0
DILATIONS = (1, 4, 16)
SPAN = 128
N_HEADS = 8
HALF = 1024
POOL_CH = 256
CONV_K = 31
CONV_PAD = 32
CHUNK = 128
N_CHIPS = 4
LANES = 256
E_IN_PIECES = 3
SMALL_SHARD_ROWS = 352
ANY = pl.BlockSpec(memory_space=pl.ANY)
MESH = pl.DeviceIdType.MESH

ADAM_LR = 0.001
ADAM_B1 = 0.9
ADAM_B2 = 0.999
ADAM_EPS = 1e-08
ADAM_WD = 0.01
ADAM_STEP = 10


def _dot(a, b):
    return jnp.dot(a, b, preferred_element_type=f32)


def _dot_nt(a, b):
    return lax.dot_general(a, b, (((1,), (1,)), ((), ())), preferred_element_type=f32)


def _dot_tn(a, b):
    return lax.dot_general(a, b, (((0,), (0,)), ((), ())), preferred_element_type=f32)


def _sigmoid(x):
    return 1.0 / (1.0 + jnp.exp(-x))


def _silu_and_grad(x):
    s = _sigmoid(x)
    return x * s, s * (1.0 + x * (1.0 - s))


def _rms_fwd(x, g):
    r = lax.rsqrt(jnp.mean(x * x, axis=-1, keepdims=True) + EPS)
    return x * r * g


def _rms_bwd(x, g, dout):
    r = lax.rsqrt(jnp.mean(x * x, axis=-1, keepdims=True) + EPS)
    xh = x * r
    dg = jnp.sum(dout * xh, axis=0, keepdims=True)
    dxh = dout * g
    dx = r * (dxh - xh * jnp.mean(dxh * xh, axis=-1, keepdims=True))
    return dx, dg


def _ln_stats(x):
    mu = jnp.mean(x, axis=-1, keepdims=True)
    xc = x - mu
    rstd = lax.rsqrt(jnp.mean(xc * xc, axis=-1, keepdims=True) + EPS)
    return xc * rstd, rstd


def _ln_bwd(xh, rstd, g, dout):
    dg = jnp.sum(dout * xh, axis=0, keepdims=True)
    db = jnp.sum(dout, axis=0, keepdims=True)
    dxh = dout * g
    dx = rstd * (dxh - jnp.mean(dxh, axis=-1, keepdims=True) - xh * jnp.mean(dxh * xh, axis=-1, keepdims=True))
    return dx, dg, db


def _accumulate(ref, value, first):
    @pl.when(first)
    def _():
        ref[...] = value

    @pl.when(jnp.logical_not(first))
    def _():
        ref[...] += value


def _write_behind(step, steps, tiles, sems, window):
    slot = step % 2

    def copies(s, at):
        return [pltpu.make_async_copy(tile.at[s], window(t, at), sems.at[2 * t + s]) for t, tile in enumerate(tiles)]

    @pl.when(step >= 2)
    def _():
        for cp in copies(slot, step - 2):
            cp.wait()

    def full():
        for cp in copies(slot, step):
            cp.start()

        @pl.when(step == steps - 1)
        def _():
            for cp in copies(slot, step):
                cp.wait()
            if steps > 1:
                for cp in copies(1 - slot, step - 1):
                    cp.wait()

    return [tile.at[slot] for tile in tiles], full


def _columns(ref, first, width):
    return ref.at[:, pl.ds(pl.multiple_of(first, 128), width)]


def _col_tile(ns):
    for t in (1024, 768, 512, 256):
        if ns % t == 0:
            return t
    raise ValueError(ns)


def _mm_nn(a, w, out_dtype, name, piece=0, pieces=1, into=None):
    m, k = a.shape
    j, _, ns = w.shape
    tm, tn = m, _col_tile(ns)
    nb = ns // tn

    def body(a_ref, w_ref, *rest):
        rest[-1][...] = _dot(a_ref[...], w_ref[...]).astype(rest[-1].dtype)

    return pl.pallas_call(
        body, name=name, grid=(j * nb, m // tm),
        in_specs=[pl.BlockSpec((tm, k), lambda n, i: (i, 0)),
                  pl.BlockSpec((None, k, tn), lambda n, i: (n // nb, 0, n % nb))] + ([] if into is None else [ANY]),
        out_specs=pl.BlockSpec((tm, tn), lambda n, i: (i, ((n // nb) * pieces + piece) * nb + n % nb)),
        out_shape=SDS((m, j * ns * pieces), out_dtype),
        input_output_aliases={} if into is None else {2: 0},
    )(a, w, *([] if into is None else [into]))


def _mm_nt(dz, ws, name, after):
    m, _ = dz.shape
    pieces = len(ws)
    j, k, ns = ws[0].shape
    tm, tk = 1024, 1024

    def body(dz_ref, *rest):
        w_refs, o_ref = rest[:pieces], rest[pieces + 1]
        total = _dot_nt(dz_ref[:, 0:ns], w_refs[0][...])
        for q in range(1, pieces):
            total = total + _dot_nt(dz_ref[:, q * ns:(q + 1) * ns], w_refs[q][...])
        if j == 1:
            o_ref[...] = total.astype(bf16)
            return
        sum_ref = rest[pieces + 2]
        r = pl.program_id(2)
        _accumulate(sum_ref, total, r == 0)

        @pl.when(r == j - 1)
        def _():
            o_ref[...] = sum_ref[...].astype(bf16)

    return pl.pallas_call(
        body, name=name, grid=(m // tm, k // tk, j),
        in_specs=[pl.BlockSpec((tm, pieces * ns), lambda i, kk, r: (i, r))]
        + [pl.BlockSpec((None, tk, ns), lambda i, kk, r: (r, kk, 0))] * pieces + [ANY],
        out_specs=pl.BlockSpec((tm, tk), lambda i, kk, r: (i, kk)),
        out_shape=SDS((m, k), bf16), scratch_shapes=[] if j == 1 else [pltpu.VMEM((tm, tk), f32)],
    )(dz, *ws, after)


def _mm_tn(a, dz, j, name):
    m, k = a.shape
    ns = dz.shape[1] // j
    tk, tn = 1024, _col_tile(ns)
    nb = ns // tn

    def body(a_ref, dz_ref, o_ref):
        o_ref[...] = _dot_tn(a_ref[...], dz_ref[...]).astype(o_ref.dtype)

    return pl.pallas_call(
        body, name=name, grid=(k // tk, j * nb),
        in_specs=[pl.BlockSpec((m, tk), lambda kk, n: (0, kk)),
                  pl.BlockSpec((m, tn), lambda kk, n: (0, n))],
        out_specs=pl.BlockSpec((None, tk, tn), lambda kk, n: (n // nb, kk, n % nb)),
        out_shape=SDS((j, k, ns), bf16),
    )(a, dz)


ROWS = 512


def _row_spec(width=D_MODEL, col=0):
    return pl.BlockSpec((ROWS, width), lambda i: (i, col))


def _vec_spec(width=D_MODEL):
    return pl.BlockSpec((1, width), lambda i: (0, 0))


def _pre_norm(x, g):
    def body(x_ref, g_ref, h_ref):
        h_ref[...] = _rms_fwd(x_ref[...], g_ref[...]).astype(bf16)

    return pl.pallas_call(
        body, name="pre_norm", grid=(SEQ // ROWS,), in_specs=[_row_spec(), _vec_spec()],
        out_specs=_row_spec(), out_shape=SDS((SEQ, D_MODEL), bf16))(x, g)


def _mid_norm(x, y, g_post, g_pre):
    def body(x_ref, y_ref, gpost_ref, gpre_ref, x1_ref, h1_ref):
        x1 = x_ref[...] + _rms_fwd(y_ref[...], gpost_ref[...])
        x1_ref[...] = x1
        h1_ref[...] = _rms_fwd(x1, gpre_ref[...]).astype(bf16)

    return pl.pallas_call(
        body, name="mid_norm", grid=(SEQ // ROWS,),
        in_specs=[_row_spec(), _row_spec(), _vec_spec(), _vec_spec()],
        out_specs=[_row_spec(), _row_spec()],
        out_shape=[SDS((SEQ, D_MODEL), f32), SDS((SEQ, D_MODEL), bf16)])(x, y, g_post, g_pre)


def _final_norm_loss(x1, y, g_post, target):
    def body(x1_ref, y_ref, g_ref, t_ref, loss_ref, dx2_ref, dy_ref, dg_ref):
        first = pl.program_id(0) == 0
        y = y_ref[...]
        g = g_ref[...]
        err = x1_ref[...] + _rms_fwd(y, g) - t_ref[...]
        sq = jnp.sum(jnp.sum(err * err, axis=1, keepdims=True), axis=0, keepdims=True)
        _accumulate(loss_ref, sq * (0.5 / D_MODEL), first)
        dx2 = err * (1.0 / D_MODEL)
        dx2_ref[...] = dx2
        dy, dg = _rms_bwd(y, g, dx2)
        dy_ref[...] = dy.astype(bf16)
        _accumulate(dg_ref, dg, first)

    return pl.pallas_call(
        body, name="final_norm_loss", grid=(SEQ // ROWS,),
        in_specs=[_row_spec(), _row_spec(), _vec_spec(), _row_spec()],
        out_specs=[pl.BlockSpec((1, 1), lambda i: (0, 0)), _row_spec(), _row_spec(), _vec_spec()],
        out_shape=[SDS((1, 1), f32), SDS((SEQ, D_MODEL), f32), SDS((SEQ, D_MODEL), bf16), SDS((1, D_MODEL), f32)],
    )(x1, y, g_post, target)


def _mid_norm_bwd(dx2, dh1, x1, y0, g_pre, g_post):
    def body(dx2_ref, dh1_ref, x1_ref, y0_ref, gpre_ref, gpost_ref, dx1_ref, dy0_ref, dgpre_ref, dgpost_ref):
        first = pl.program_id(0) == 0
        d_in, dgpre = _rms_bwd(x1_ref[...], gpre_ref[...], dh1_ref[...])
        dx1 = dx2_ref[...] + d_in
        dx1_ref[...] = dx1
        dy0, dgpost = _rms_bwd(y0_ref[...], gpost_ref[...], dx1)
        dy0_ref[...] = dy0.astype(bf16)
        _accumulate(dgpre_ref, dgpre, first)
        _accumulate(dgpost_ref, dgpost, first)

    return pl.pallas_call(
        body, name="mid_norm_bwd", grid=(SEQ // ROWS,),
        in_specs=[_row_spec(), _row_spec(), _row_spec(), _row_spec(), _vec_spec(), _vec_spec()],
        out_specs=[_row_spec(), _row_spec(), _vec_spec(), _vec_spec()],
        out_shape=[SDS((SEQ, D_MODEL), f32), SDS((SEQ, D_MODEL), bf16), SDS((1, D_MODEL), f32), SDS((1, D_MODEL), f32)],
    )(dx2, dh1, x1, y0, g_pre, g_post)


def _pre_norm_bwd(dx1, dh0, x, g):
    def body(dx1_ref, dh0_ref, x_ref, g_ref, dx_ref, dg_ref):
        d_in, dg = _rms_bwd(x_ref[...], g_ref[...], dh0_ref[...])
        dx_ref[...] = dx1_ref[...] + d_in
        _accumulate(dg_ref, dg, pl.program_id(0) == 0)

    return pl.pallas_call(
        body, name="pre_norm_bwd", grid=(SEQ // ROWS,),
        in_specs=[_row_spec(), _row_spec(), _row_spec(), _vec_spec()],
        out_specs=[_row_spec(), _vec_spec()],
        out_shape=[SDS((SEQ, D_MODEL), f32), SDS((1, D_MODEL), f32)])(dx1, dh0, x, g)


def _pool_count(g):
    row = lax.broadcasted_iota(jnp.int32, (SEQ, 1), 0)
    width = jnp.left_shift(2, g)
    return row, width, jnp.minimum(row + 1, width).astype(f32)


def _trailing_sum(x, row, width):
    s = x
    for k in (1, 2, 4, 8):
        shifted = jnp.where(row >= k, pltpu.roll(s, k, 0), 0.0)
        s = jnp.where(width > k, s + shifted, s)
    return s


def _leading_sum(x, row, width):
    s = x
    for k in (1, 2, 4, 8):
        shifted = jnp.where(row < SEQ - k, pltpu.roll(s, SEQ - k, 0), 0.0)
        s = jnp.where(width > k, s + shifted, s)
    return s


def _pool_specs():
    a_in = pl.BlockSpec((SEQ, POOL_CH), lambda g: (0, g))
    a_gate = pl.BlockSpec((SEQ, POOL_CH), lambda g: (0, 4 + g))
    w = pl.BlockSpec((None, POOL_CH, POOL_CH), lambda g: (g, 0, 0))
    scale = pl.BlockSpec((1, POOL_CH), lambda g: (0, g))
    return a_in, a_gate, w, scale


def _pool_fwd(z0, pool_w, pool_scale):
    def body(a_ref, gate_ref, w_ref, scale_ref, ya_ref):
        row, width, count = _pool_count(pl.program_id(0))
        a = a_ref[...]
        pooled = _trailing_sum(a, row, width) / count - a
        mixed = _dot(pooled.astype(bf16), w_ref[...]) * scale_ref[...]
        gate = gate_ref[...]
        ya_ref[...] = (mixed * gate * _sigmoid(gate)).astype(bf16)

    return pl.pallas_call(
        body, name="pool_fwd", grid=(4,), in_specs=list(_pool_specs()),
        out_specs=pl.BlockSpec((SEQ, POOL_CH), lambda g: (0, g)),
        out_shape=SDS((SEQ, 2 * HALF), bf16))(z0, z0, pool_w, pool_scale)


def _pool_bwd(z0, dcat, pool_w, pool_scale):
    def body(a_ref, gate_ref, w_ref, scale_ref, dya_ref, dz_ref, dw_ref, dscale_ref, da_tiles, dgate_tiles, sems):
        g = pl.program_id(0)
        (da_ref, dgate_ref), full = _write_behind(
            g, 4, [da_tiles, dgate_tiles], sems, lambda t, at: _columns(dz_ref, t * HALF + at * POOL_CH, POOL_CH))
        row, width, count = _pool_count(g)
        a = a_ref[...]
        pooled = (_trailing_sum(a, row, width) / count - a).astype(bf16)
        w = w_ref[...]
        scale = scale_ref[...]
        mixed = _dot(pooled, w)
        silu, dsilu = _silu_and_grad(gate_ref[...])
        dya = dya_ref[...]
        dgate_ref[...] = (dya * mixed * scale * dsilu).astype(bf16)
        dms = dya * silu
        dscale_ref[...] = jnp.sum(dms * mixed, axis=0, keepdims=True)
        dmixed = (dms * scale).astype(bf16)
        dw_ref[...] = _dot_tn(pooled, dmixed)
        dpooled = _dot_nt(dmixed, w)
        da_ref[...] = (_leading_sum(dpooled / count, row, width) - dpooled).astype(bf16)
        full()

    a_in, a_gate, w, scale = _pool_specs()
    col = pl.BlockSpec((SEQ, POOL_CH), lambda g: (0, g))
    tiles = pltpu.VMEM((2, SEQ, POOL_CH), bf16)
    return pl.pallas_call(
        body, name="pool_bwd", grid=(4,), in_specs=[a_in, a_gate, w, scale, col],
        out_specs=[ANY, w, scale],
        out_shape=[SDS((SEQ, 6 * D_MODEL), bf16), SDS((4, POOL_CH, POOL_CH), f32), SDS((1, HALF), f32)],
        scratch_shapes=[tiles, tiles, pltpu.SemaphoreType.DMA((4,))],
    )(z0, z0, pool_w, pool_scale, dcat)


Q_COL, K_COL, V_COL, BGATE_COL = 16, 40, 64, 88


def _rope_tables():
    pos = jnp.arange(SEQ, dtype=f32)
    inv_freq = jnp.power(ROPE_THETA, -jnp.arange(0, 2 * ROT_HALF, 2, dtype=f32) / (2 * ROT_HALF))
    ang = pos[:, None] * inv_freq[None, :]
    cos, sin = jnp.cos(ang), jnp.sin(ang)
    zeros = jnp.zeros((SEQ, HEAD_DIM - 2 * ROT_HALF), f32)
    cos_t = jnp.concatenate([cos, cos, zeros + 1.0], axis=1)
    sin_t = jnp.concatenate([sin, sin, zeros], axis=1)
    j = jnp.arange(HEAD_DIM)[:, None]
    i = jnp.arange(HEAD_DIM)[None, :]
    rot = jnp.where((i < ROT_HALF) & (j == i + ROT_HALF), -1.0, 0.0) + jnp.where(
        (i >= ROT_HALF) & (i < 2 * ROT_HALF) & (j == i - ROT_HALF), 1.0, 0.0)
    return cos_t, sin_t, rot.astype(bf16), rot.T.astype(bf16)


def _exact_dot(t, m):
    hi = t.astype(bf16)
    lo = (t - hi.astype(f32)).astype(bf16)
    return _dot(hi, m) + _dot(lo, m)


def _rope(t, cos_t, sin_t, rot):
    return t * cos_t + _exact_dot(t, rot) * sin_t


def _rope_transposed(d, cos_t, sin_t, rot_t):
    return d * cos_t + _exact_dot(d * sin_t, rot_t)


ROW_CHUNK = 256
BLOCKS_TOGETHER = 8


def _chunks(fn):
    for start in range(0, SEQ, ROW_CHUNK):
        fn(start)


def _pieces(dilation):
    length = SEQ // dilation
    n = min(length, ROW_CHUNK)
    return [(r, l0, n) for r in range(dilation) for l0 in range(0, length, n)]


def _by_residue(dst_ref, src_ref, dilation, dtype):
    length = SEQ // dilation
    for r, l0, n in _pieces(dilation):
        src = src_ref[l0:l0 + n, :] if dilation == 1 else src_ref[pl.ds(r + dilation * l0, n, stride=dilation), :]
        start = r * length + l0
        dst_ref[start:start + n, :] = src.astype(dtype)


def _by_position(dst_ref, src_ref, dilation):
    length = SEQ // dilation
    for r, l0, n in _pieces(dilation):
        src = src_ref[r * length + l0:r * length + l0 + n, :]
        if dilation == 1:
            dst_ref[l0:l0 + n, :] = src
        else:
            dst_ref[pl.ds(r + dilation * l0, n, stride=dilation), :] = src


def _attn_masks():
    qi = lax.broadcasted_iota(jnp.int32, (SPAN, 2 * SPAN), 0)
    kj = lax.broadcasted_iota(jnp.int32, (SPAN, 2 * SPAN), 1)
    window = ((kj < SPAN) & (kj >= qi)) | ((kj >= SPAN) & (kj - SPAN <= qi))
    own = lax.broadcasted_iota(jnp.int32, (SPAN, SPAN), 1) <= lax.broadcasted_iota(jnp.int32, (SPAN, SPAN), 0)
    return window, own


def _attn_blocks(dilation):
    per_residue = SEQ // dilation // SPAN
    blocks = [(c, c % per_residue != 0) for c in range(SEQ // SPAN)]
    return [blocks[i:i + BLOCKS_TOGETHER] for i in range(0, len(blocks), BLOCKS_TOGETHER)]


def _block_keys(c, has_prev):
    return slice((c - 1) * SPAN if has_prev else c * SPAN, (c + 1) * SPAN)


def _head_spec(col):
    return pl.BlockSpec((SEQ, HEAD_DIM), lambda h: (0, col + h))


def _table_spec():
    return pl.BlockSpec((SEQ, HEAD_DIM), lambda h: (0, 0))


def _attn_fwd(z0, tables, mixed):
    scale = HEAD_DIM ** -0.5

    def body(*refs):
        qkv = refs[0:9]
        bg_ref, cos_ref, sin_ref, rot_ref = refs[9:13]
        yb_ref, att_ref, lse_ref = refs[14:17]
        saved = refs[17:26]
        tmp_q, tmp_k, v_ones, o_res, l_res, o_nat, l_nat = refs[26:33]
        window_mask, own_mask = _attn_masks()
        rot = rot_ref[...]

        @pl.when(pl.program_id(0) == 0)
        def _():
            v_ones[:, HEAD_DIM:] = jnp.ones((SEQ, HEAD_DIM), bf16)

        for g, dilation in enumerate(DILATIONS):
            q_ref, k_ref, v_ref = qkv[3 * g:3 * g + 3]
            qd, kd, vd = saved[3 * g:3 * g + 3]

            def rope_rows(start, q_ref=q_ref, k_ref=k_ref):
                r = pl.ds(start, ROW_CHUNK)
                cos_t, sin_t = cos_ref[r, :], sin_ref[r, :]
                tmp_q[r, :] = _rope(q_ref[r, :], cos_t, sin_t, rot) * scale
                tmp_k[r, :] = _rope(k_ref[r, :], cos_t, sin_t, rot)

            _chunks(rope_rows)
            _by_residue(qd, tmp_q, dilation, bf16)
            _by_residue(kd, tmp_k, dilation, bf16)
            _by_residue(vd, v_ref, dilation, bf16)
            for l0 in range(0, SEQ, ROW_CHUNK):
                v_ones[l0:l0 + ROW_CHUNK, 0:HEAD_DIM] = vd[l0:l0 + ROW_CHUNK, :]

            for group in _attn_blocks(dilation):
                scores = [_dot_nt(qd[c * SPAN:(c + 1) * SPAN, :], kd[_block_keys(c, prev), :]) for c, prev in group]
                tops, probs = [], []
                for (c, prev), s in zip(group, scores):
                    s = jnp.where(window_mask if prev else own_mask, s, NEG)
                    tops.append(jnp.max(s, axis=1, keepdims=True))
                    probs.append(jnp.exp(s - tops[-1]).astype(bf16))
                sums = [_dot(p, v_ones[_block_keys(c, prev), :]) for (c, prev), p in zip(group, probs)]
                for (c, prev), m, o in zip(group, tops, sums):
                    den = o[:, HEAD_DIM:]
                    o_res[c * SPAN:(c + 1) * SPAN, :] = o[:, :HEAD_DIM] / den
                    l_res[c * SPAN:(c + 1) * SPAN, :] = m + jnp.log(den)

            if dilation > 1:
                _by_position(o_nat, o_res, dilation)
                _by_position(l_nat, l_res, dilation)
            o_g, l_g = (o_res, l_res) if dilation == 1 else (o_nat, l_nat)

            def merge(start, g=g, o_g=o_g, l_g=l_g):
                r = pl.ds(start, ROW_CHUNK)
                if g == 0:
                    att, total = o_g[r, :], l_g[r, :]
                else:
                    l_old, l_new = lse_ref[r, :], l_g[r, :]
                    top = jnp.maximum(l_old, l_new)
                    total = top + jnp.log(jnp.exp(l_old - top) + jnp.exp(l_new - top))
                    att = att_ref[r, :] * jnp.exp(l_old - total) + o_g[r, :] * jnp.exp(l_new - total)
                att_ref[r, :] = att
                lse_ref[r, :] = total
                if g == len(DILATIONS) - 1:
                    gate = bg_ref[r, :]
                    yb_ref[r, :] = (att * gate * _sigmoid(gate)).astype(bf16)

            _chunks(merge)

    in_specs = []
    for g in range(3):
        in_specs += [_head_spec(Q_COL + 8 * g), _head_spec(K_COL + 8 * g), _head_spec(V_COL + 8 * g)]
    in_specs += [_head_spec(BGATE_COL), _table_spec(), _table_spec(), pl.BlockSpec((HEAD_DIM, HEAD_DIM), lambda h: (0, 0)), ANY]
    out_spec = pl.BlockSpec((SEQ, HEAD_DIM), lambda h: (0, h))
    right_half = pl.BlockSpec((SEQ, HEAD_DIM), lambda h: (0, N_HEADS + h))
    vm = lambda dt: pltpu.VMEM((SEQ, HEAD_DIM), dt)
    cos_t, sin_t, rot, _ = tables
    out = pl.pallas_call(
        body, name="attn_fwd", grid=(N_HEADS,), in_specs=in_specs, out_specs=[right_half] + [out_spec] * 11,
        out_shape=[SDS((SEQ, 2 * HALF), bf16), SDS((SEQ, HALF), f32), SDS((SEQ, HALF), f32)] + [SDS((SEQ, HALF), bf16)] * 9,
        scratch_shapes=[vm(f32), vm(f32), pltpu.VMEM((SEQ, 2 * HEAD_DIM), bf16), vm(f32), vm(f32), vm(f32), vm(f32)],
        input_output_aliases={13: 0},
    )(*([z0] * 10), cos_t, sin_t, rot, mixed)
    return out[0], out[1], out[2], [tuple(out[3 + 3 * g:6 + 3 * g]) for g in range(3)]


def _attn_bwd_group(g, saved, z0, att, lse, dcat, tables, dz):
    scale = HEAD_DIM ** -0.5
    dilation = DILATIONS[g]
    with_gate = g == 0
    n_out = 4 if with_gate else 3
    first_col = (Q_COL + 8 * g, K_COL + 8 * g, V_COL + 8 * g, BGATE_COL)

    def body(*refs):
        qd, kd, vd, bg_ref, att_ref, lse_ref, dyb_ref, cos_ref, sin_ref, rot_t_ref = refs[0:10]
        dz_ref = refs[11]
        dod, ld, dd, tmp, aq, ak, av = refs[12:19]
        views, full = _write_behind(pl.program_id(0), N_HEADS, refs[19:19 + n_out], refs[19 + n_out],
                                    lambda t, at: _columns(dz_ref, (first_col[t] + at) * HEAD_DIM, HEAD_DIM))
        dq_ref, dk_ref, dv_ref = views[0:3]
        window_mask, own_mask = _attn_masks()
        rot_t = rot_t_ref[...]

        def gate_rows(start):
            r = pl.ds(start, ROW_CHUNK)
            silu, dsilu = _silu_and_grad(bg_ref[r, :])
            att_v = att_ref[r, :]
            dyb = dyb_ref[r, :]
            if with_gate:
                views[3][r, :] = (dyb * att_v * dsilu).astype(bf16)
            datt = dyb * silu
            tmp[r, :] = datt
            aq[r, :] = jnp.broadcast_to(jnp.sum(datt * att_v, axis=1, keepdims=True), (ROW_CHUNK, HEAD_DIM))

        _chunks(gate_rows)
        _by_residue(dod, tmp, dilation, bf16)
        _by_residue(dd, aq, dilation, f32)
        _by_residue(ld, lse_ref, dilation, f32)

        for group in _attn_blocks(dilation):
            rows = [slice(c * SPAN, (c + 1) * SPAN) for c, _ in group]
            keys = [_block_keys(c, prev) for c, prev in group]
            scores = [_dot_nt(qd[r, :], kd[k, :]) for r, k in zip(rows, keys)]
            dprobs = [_dot_nt(dod[r, :], vd[k, :]) for r, k in zip(rows, keys)]
            probs, dscores = [], []
            for (c, prev), r, s, dp in zip(group, rows, scores, dprobs):
                lse_q, delta = ld[r, :], dd[r, :]
                if prev:
                    lse_q = jnp.concatenate([lse_q, lse_q], axis=1)
                    delta = jnp.concatenate([delta, delta], axis=1)
                p = jnp.where(window_mask if prev else own_mask, jnp.exp(s - lse_q), 0.0)
                probs.append(p.astype(bf16))
                dscores.append((p * (dp - delta)).astype(bf16))
            dvs = [_dot_tn(p, dod[r, :]) for p, r in zip(probs, rows)]
            dks = [_dot_tn(ds, qd[r, :]) for ds, r in zip(dscores, rows)]
            dqs = [_dot(ds, kd[k, :]) for ds, k in zip(dscores, keys)]
            for (c, prev), r, dv, dk, dq in zip(group, rows, dvs, dks, dqs):
                aq[r, :] = dq
                if prev:
                    before = slice((c - 1) * SPAN, c * SPAN)
                    av[before, :] += dv[0:SPAN]
                    ak[before, :] += dk[0:SPAN]
                    av[r, :] = dv[SPAN:]
                    ak[r, :] = dk[SPAN:]
                else:
                    av[r, :] = dv
                    ak[r, :] = dk

        def finish(out_ref, acc, factor, roped):
            if dilation > 1:
                _by_position(tmp, acc, dilation)
            src = acc if dilation == 1 else tmp

            def rows(start):
                r = pl.ds(start, ROW_CHUNK)
                d = src[r, :]
                if factor != 1.0:
                    d = d * factor
                if roped:
                    d = _rope_transposed(d, cos_ref[r, :], sin_ref[r, :], rot_t)
                out_ref[r, :] = d.astype(bf16)

            _chunks(rows)

        finish(dq_ref, aq, scale, True)
        finish(dk_ref, ak, 1.0, True)
        finish(dv_ref, av, 1.0, False)
        full()

    head = pl.BlockSpec((SEQ, HEAD_DIM), lambda h: (0, h))
    in_specs = [head, head, head, _head_spec(BGATE_COL), head, head, _head_spec(8), _table_spec(), _table_spec(),
                pl.BlockSpec((HEAD_DIM, HEAD_DIM), lambda h: (0, 0)), ANY]
    vm = lambda dt: pltpu.VMEM((SEQ, HEAD_DIM), dt)
    cos_t, sin_t, _, rot_t = tables
    return pl.pallas_call(
        body, name=f"attn_bwd_g{g}", grid=(N_HEADS,), in_specs=in_specs, out_specs=ANY,
        out_shape=SDS(dz.shape, dz.dtype), input_output_aliases={10: 0},
        scratch_shapes=[vm(bf16), vm(f32), vm(f32), vm(f32), vm(f32), vm(f32), vm(f32)]
        + [pltpu.VMEM((2, SEQ, HEAD_DIM), bf16)] * n_out + [pltpu.SemaphoreType.DMA((2 * n_out,))],
    )(*saved, z0, att, lse, dcat, cos_t, sin_t, rot_t, dz)


def _sgu_specs():
    chunk = lambda col: pl.BlockSpec((CHUNK, HALF), lambda n: (n, col))
    vec = pl.BlockSpec((1, HALF), lambda n: (0, 0))
    w = pl.BlockSpec((4, CHUNK, CHUNK), lambda n: (0, 0, 0))
    bias = pl.BlockSpec((CHUNK, CHUNK), lambda n: (0, 0))
    return chunk, vec, w, bias


def _sgu_weights(w_ref):
    tril = lax.broadcasted_iota(jnp.int32, (CHUNK, CHUNK), 1) <= lax.broadcasted_iota(jnp.int32, (CHUNK, CHUNK), 0)
    return tril, [jnp.where(tril, w_ref[h], 0.0).astype(bf16) for h in range(4)]


def _sgu_fwd(z1, ln_g, ln_b, sgu_w, bias_t):
    def body(u_ref, v_ref, cg_ref, g_ref, b_ref, w_ref, bias_ref, yc_ref):
        _, ws = _sgu_weights(w_ref)
        xh, _ = _ln_stats(v_ref[...])
        vn = (xh * g_ref[...] + b_ref[...]).astype(bf16)
        for h in range(4):
            cols = slice(h * POOL_CH, (h + 1) * POOL_CH)
            s = _dot(ws[h], vn[:, cols]) + bias_ref[:, h:h + 1]
            gate = cg_ref[:, cols]
            yc_ref[:, cols] = (u_ref[:, cols] * s * gate * _sigmoid(gate)).astype(bf16)

    chunk, vec, w, bias = _sgu_specs()
    return pl.pallas_call(
        body, name="sgu_fwd", grid=(SEQ // CHUNK,),
        in_specs=[chunk(0), chunk(1), chunk(2), vec, vec, w, bias], out_specs=chunk(0),
        out_shape=SDS((SEQ, 2 * HALF), bf16))(z1, z1, z1, ln_g, ln_b, sgu_w, bias_t)


def _sgu_bwd(z1, dcat, ln_g, ln_b, sgu_w, bias_t):
    def body(u_ref, v_ref, cg_ref, dyc_ref, g_ref, b_ref, w_ref, bias_ref,
             dz_ref, dw_ref, dbias_ref, dg_ref, db_ref, dvn_ref, du_tiles, dv_tiles, dcg_tiles, sems):
        n = pl.program_id(0)
        (du_ref, dv_ref, dcg_ref), full = _write_behind(
            n, SEQ // CHUNK, [du_tiles, dv_tiles, dcg_tiles], sems,
            lambda t, at: dz_ref.at[pl.ds(pl.multiple_of(at * CHUNK, CHUNK), CHUNK), t * HALF:(t + 1) * HALF])
        first = n == 0
        tril, ws = _sgu_weights(w_ref)
        xh, rstd = _ln_stats(v_ref[...])
        g = g_ref[...]
        vn = (xh * g + b_ref[...]).astype(bf16)

        @pl.when(first)
        def _():
            dbias_ref[...] = jnp.zeros((CHUNK, CHUNK), f32)

        for h in range(4):
            cols = slice(h * POOL_CH, (h + 1) * POOL_CH)
            vn_h = vn[:, cols]
            s = _dot(ws[h], vn_h) + bias_ref[:, h:h + 1]
            silu, dsilu = _silu_and_grad(cg_ref[:, cols])
            dyc = dyc_ref[:, cols]
            u = u_ref[:, cols]
            du_ref[:, cols] = (dyc * s * silu).astype(bf16)
            dcg_ref[:, cols] = (dyc * u * s * dsilu).astype(bf16)
            ds = dyc * u * silu
            dbias_ref[:, h:h + 1] += jnp.sum(ds, axis=1, keepdims=True)
            ds = ds.astype(bf16)
            _accumulate(dw_ref.at[h], jnp.where(tril, _dot_nt(ds, vn_h), 0.0), first)
            dvn_ref[:, cols] = _dot_tn(ws[h], ds)
        dv, dg, db = _ln_bwd(xh, rstd, g, dvn_ref[...])
        dv_ref[...] = dv.astype(bf16)
        _accumulate(dg_ref, dg, first)
        _accumulate(db_ref, db, first)
        full()

    chunk, vec, w, bias = _sgu_specs()
    tiles = pltpu.VMEM((2, CHUNK, HALF), bf16)
    return pl.pallas_call(
        body, name="sgu_bwd", grid=(SEQ // CHUNK,),
        in_specs=[chunk(0), chunk(1), chunk(2), chunk(0), vec, vec, w, bias],
        out_specs=[ANY, w, bias, vec, vec],
        out_shape=[SDS((SEQ, 3 * D_MODEL), bf16), SDS((4, CHUNK, CHUNK), f32), SDS((CHUNK, CHUNK), f32),
                   SDS((1, HALF), f32), SDS((1, HALF), f32)],
        scratch_shapes=[pltpu.VMEM((CHUNK, HALF), f32), tiles, tiles, tiles, pltpu.SemaphoreType.DMA((6,))],
    )(z1, z1, z1, dcat, ln_g, ln_b, sgu_w, bias_t)


CONV_TILE = 128
DVAL_COL, DGLU_COL = 12, 16


def _conv_specs():
    val = pl.BlockSpec((SEQ, POOL_CH), lambda j: (0, DVAL_COL + j))
    glu = pl.BlockSpec((SEQ, POOL_CH), lambda j: (0, DGLU_COL + j))
    w = pl.BlockSpec((CONV_K, POOL_CH), lambda j: (0, j))
    col = pl.BlockSpec((SEQ, POOL_CH), lambda j: (0, j))
    vec = pl.BlockSpec((1, POOL_CH), lambda j: (0, j))
    return val, glu, w, col, vec


def _conv_fwd(z1, conv_w, conv_b):
    def body(val_ref, glu_ref, w_ref, b_ref, out_ref, xpad):
        xpad[0:CONV_PAD, :] = jnp.zeros((CONV_PAD, POOL_CH), f32)
        xpad[CONV_PAD:, :] = val_ref[...] * _sigmoid(glu_ref[...])
        w = w_ref[...]
        bias = b_ref[...]

        def tile(i, carry):
            t0 = pl.multiple_of(i * CONV_TILE, CONV_TILE)
            window = xpad[pl.ds(t0, CONV_TILE + CONV_PAD), :]
            acc = jnp.broadcast_to(bias, (CONV_TILE, POOL_CH))
            for k in range(CONV_K):
                shift = CONV_PAD - (CONV_K - 1) + k
                acc = acc + w[k:k + 1, :] * pltpu.roll(window, CONV_TILE + CONV_PAD - shift, 0)[0:CONV_TILE]
            out_ref[pl.ds(t0, CONV_TILE), :] = acc
            return carry

        lax.fori_loop(0, SEQ // CONV_TILE, tile, 0)

    val, glu, w, col, vec = _conv_specs()
    return pl.pallas_call(
        body, name="conv_fwd", grid=(4,), in_specs=[val, glu, w, vec], out_specs=col,
        out_shape=SDS((SEQ, HALF), f32), scratch_shapes=[pltpu.VMEM((SEQ + CONV_PAD, POOL_CH), f32)],
    )(z1, z1, conv_w, conv_b)


def _conv_bwd(z1, dconv, conv_w, dz):
    def body(val_ref, glu_ref, w_ref, dout_ref, dz_in, dz_ref, dw_ref, db_ref, xpad, dpad, dx_ref, dval_tiles, dglu_tiles, sems):
        j = pl.program_id(0)
        (dval_ref, dglu_ref), full = _write_behind(
            j, 4, [dval_tiles, dglu_tiles], sems,
            lambda t, at: _columns(dz_ref, ((DVAL_COL, DGLU_COL)[t] + at) * POOL_CH, POOL_CH))
        val = val_ref[...]
        sig = _sigmoid(glu_ref[...])
        xpad[0:CONV_PAD, :] = jnp.zeros((CONV_PAD, POOL_CH), f32)
        xpad[CONV_PAD:, :] = val * sig
        dout = dout_ref[...]
        dpad[0:SEQ, :] = dout
        dpad[SEQ:, :] = jnp.zeros((CONV_PAD, POOL_CH), f32)
        db_ref[...] = jnp.sum(dout, axis=0, keepdims=True)
        dw_ref[...] = jnp.zeros((CONV_K, POOL_CH), f32)
        w = w_ref[...]

        def tile(i, carry):
            t0 = pl.multiple_of(i * CONV_TILE, CONV_TILE)
            x_win = xpad[pl.ds(t0, CONV_TILE + CONV_PAD), :]
            d_win = dpad[pl.ds(t0, CONV_TILE + CONV_PAD), :]
            d_own = d_win[0:CONV_TILE]
            acc = jnp.zeros((CONV_TILE, POOL_CH), f32)
            for k in range(CONV_K):
                shift = CONV_PAD - (CONV_K - 1) + k
                x_k = pltpu.roll(x_win, CONV_TILE + CONV_PAD - shift, 0)[0:CONV_TILE]
                dw_ref[k:k + 1, :] += jnp.sum(d_own * x_k, axis=0, keepdims=True)
                back = CONV_K - 1 - k
                d_k = d_own if back == 0 else pltpu.roll(d_win, CONV_TILE + CONV_PAD - back, 0)[0:CONV_TILE]
                acc = acc + w[k:k + 1, :] * d_k
            dx_ref[pl.ds(t0, CONV_TILE), :] = acc
            return carry

        lax.fori_loop(0, SEQ // CONV_TILE, tile, 0)
        dx = dx_ref[...]
        dval_ref[...] = (dx * sig).astype(bf16)
        dglu_ref[...] = (dx * val * sig * (1.0 - sig)).astype(bf16)
        full()

    val, glu, w, col, vec = _conv_specs()
    pad = pltpu.VMEM((SEQ + CONV_PAD, POOL_CH), f32)
    tiles = pltpu.VMEM((2, SEQ, POOL_CH), bf16)
    return pl.pallas_call(
        body, name="conv_bwd", grid=(4,), in_specs=[val, glu, w, col, ANY], out_specs=[ANY, w, vec],
        out_shape=[SDS(dz.shape, dz.dtype), SDS((CONV_K, HALF), f32), SDS((1, HALF), f32)],
        input_output_aliases={4: 0},
        scratch_shapes=[pad, pad, pltpu.VMEM((SEQ, POOL_CH), f32), tiles, tiles, pltpu.SemaphoreType.DMA((4,))],
    )(z1, z1, conv_w, dconv, dz)


DGATE_COL = 5


def _conv_norm_fwd(conv, z1, g, b, mixed):
    def body(c_ref, gate_ref, g_ref, b_ref, mixed_ref, yd_ref):
        xh, _ = _ln_stats(c_ref[...])
        n = xh * g_ref[...] + b_ref[...]
        gate = gate_ref[...]
        yd_ref[...] = (n * _sigmoid(n) * gate * _sigmoid(gate)).astype(bf16)

    return pl.pallas_call(
        body, name="conv_norm_fwd", grid=(SEQ // ROWS,),
        in_specs=[_row_spec(HALF), _row_spec(HALF, DGATE_COL), _vec_spec(HALF), _vec_spec(HALF), ANY],
        out_specs=_row_spec(HALF, 1), out_shape=SDS((SEQ, 2 * HALF), bf16), input_output_aliases={4: 0},
    )(conv, z1, g, b, mixed)


def _conv_norm_bwd(conv, z1, dcat, g, b, dz):
    def body(c_ref, gate_ref, dyd_ref, g_ref, b_ref, dz_ref, dconv_ref, dgate_ref, dg_ref, db_ref):
        first = pl.program_id(0) == 0
        xh, rstd = _ln_stats(c_ref[...])
        g = g_ref[...]
        n_silu, n_dsilu = _silu_and_grad(xh * g + b_ref[...])
        gate_silu, gate_dsilu = _silu_and_grad(gate_ref[...])
        dyd = dyd_ref[...]
        dgate_ref[...] = (dyd * n_silu * gate_dsilu).astype(bf16)
        dconv, dg, db = _ln_bwd(xh, rstd, g, dyd * gate_silu * n_dsilu)
        dconv_ref[...] = dconv
        _accumulate(dg_ref, dg, first)
        _accumulate(db_ref, db, first)

    return pl.pallas_call(
        body, name="conv_norm_bwd", grid=(SEQ // ROWS,),
        in_specs=[_row_spec(HALF), _row_spec(HALF, DGATE_COL), _row_spec(HALF, 1), _vec_spec(HALF), _vec_spec(HALF), ANY],
        out_specs=[_row_spec(HALF), _row_spec(HALF, DGATE_COL), _vec_spec(HALF), _vec_spec(HALF)],
        out_shape=[SDS((SEQ, HALF), f32), SDS(dz.shape, dz.dtype), SDS((1, HALF), f32), SDS((1, HALF), f32)],
        input_output_aliases={5: 1},
    )(conv, z1, dcat, g, b, dz)


def _step(x, target, w, chip):
    chip_vec = chip.astype(jnp.int32).reshape(1)
    sharded_names = list(SHARDED_SMALL)
    first = [_cast_into_slot(w["e_w_in"], chip_vec, "cast_e_w_in0", w["e_pre_norm"], 0, E_IN_PIECES)]
    sems, bufs, token = _gather_start(first, "gather_start_first")
    small_shard = _pack([w[k] for k in sharded_names], total_rows=SMALL_SHARD_ROWS) + 0.0 * token[0, 0]
    small_slot = lax.dynamic_update_slice(jnp.zeros((N_CHIPS, SMALL_SHARD_ROWS, LANES), f32), small_shard[None], (chip, 0, 0))
    more = [small_slot]
    more += [_cast_into_slot(w["e_w_in"], chip_vec, f"cast_e_w_in{i}", token, i, E_IN_PIECES) for i in range(1, E_IN_PIECES)]
    more_sems, more_bufs, token = _gather_start(more, "gather_start_pieces")
    rest = [_cast_into_slot(w[k], chip_vec, f"cast_{k}", token) for k in BIG[1:]]
    rest_sems, rest_bufs, token = _gather_start(rest, "gather_start_rest")
    sems, bufs = sems + more_sems + rest_sems, bufs + more_bufs + rest_bufs
    tables = _rope_tables()

    def vec(k):
        return w[k].reshape(1, -1)

    h0 = _pre_norm(x, vec("e_pre_norm") + token[0, 0])
    after, z0, e_w_in = h0, None, []
    for i in range(E_IN_PIECES):
        group = slice(0, 1) if i == 0 else slice(1, 3) if i == 1 else slice(i + 1, i + 2)
        landed = _forward_halves(_gather_wait(bufs[group], sems[group], after, f"gather_wait_{i}"), f"forward_{i}")
        if i == 1:
            small_full = landed[0]
        e_w_in.append(landed[-1])
        z0 = _mm_nn(h0, landed[-1], f32, f"e_in{i}", i, E_IN_PIECES, z0)
        after = z0
    p = {k: _from_chips(k, a) for k, a in zip(sharded_names, _unpack(small_full, [SHARDED_SMALL[k][0] for k in sharded_names]))}
    for k in ("o_pre_norm", "o_sgu_norm_g", "o_sgu_norm_b", "o_conv_b", "o_conv_norm_g", "o_conv_norm_b", "o_post_norm"):
        p[k] = p[k].reshape(1, -1)
    pool_w_bf = p["e_pool_w"].astype(bf16)
    bias_t = jnp.pad(w["o_sgu_b"].T, ((0, 0), (0, CHUNK - 4)))

    cat0, att, lse, qkv_by_residue = _attn_fwd(z0, tables, _pool_fwd(z0, pool_w_bf, vec("e_pool_scale")))

    def arrived(index, after, name):
        one = slice(index, index + 1)
        return _forward_halves(_gather_wait(bufs[one], sems[one], after, f"gather_wait_{name}"), f"forward_{name}")[0]

    e_w_out = arrived(1 + E_IN_PIECES, att, "e_w_out").reshape(1, D_MODEL, D_MODEL)
    y0 = _mm_nn(cat0, e_w_out, f32, "e_out")
    x1, h1 = _mid_norm(x, y0, vec("e_post_norm"), p["o_pre_norm"])
    o_w_in = arrived(2 + E_IN_PIECES, h1, "o_w_in")
    z1 = _mm_nn(h1, o_w_in, f32, "o_in")
    yc = _sgu_fwd(z1, p["o_sgu_norm_g"], p["o_sgu_norm_b"], w["o_sgu_w"], bias_t)
    conv = _conv_fwd(z1, p["o_conv_w"], p["o_conv_b"])
    cat1 = _conv_norm_fwd(conv, z1, p["o_conv_norm_g"], p["o_conv_norm_b"], yc)
    o_w_out = arrived(3 + E_IN_PIECES, cat1, "o_w_out").reshape(1, D_MODEL, D_MODEL)
    y1 = _mm_nn(cat1, o_w_out, f32, "o_out")
    loss, dx2, dy1, g_o_post = _final_norm_loss(x1, y1, p["o_post_norm"], target)

    in_flight = {}

    def send_off(name, grad):
        sem, sums, land, tok = _scatter_start(_swap_add(grad, f"swap_add_{name}"), f"scatter_start_{name}")
        in_flight[name] = (sem, sums, land)
        return tok

    tok = send_off("o_w_out", _mm_tn(cat1, dy1, 1, "o_out_dw").reshape(N_CHIPS, HALF // 2, D_MODEL))
    dcat1 = _mm_nt(dy1, [o_w_out], "o_out_dx", tok)
    dz1, g_sgu_w, g_bias_t, g_sgu_g, g_sgu_b = _sgu_bwd(
        z1, dcat1, p["o_sgu_norm_g"] + tok[0, 0], p["o_sgu_norm_b"], w["o_sgu_w"], bias_t)
    dconv, dz1, g_cn_g, g_cn_b = _conv_norm_bwd(conv, z1, dcat1, p["o_conv_norm_g"], p["o_conv_norm_b"], dz1)
    dz1, g_conv_w, g_conv_b = _conv_bwd(z1, dconv, p["o_conv_w"], dz1)
    tok = send_off("o_w_in", _mm_tn(h1, dz1, N_CHIPS, "o_in_dw"))
    dh1 = _mm_nt(dz1, [o_w_in], "o_in_dx", tok)
    dx1, dy0, g_o_pre, g_e_post = _mid_norm_bwd(dx2, dh1, x1, y0, p["o_pre_norm"] + tok[0, 0], vec("e_post_norm"))

    tok = send_off("e_w_out", _mm_tn(cat0, dy0, 1, "e_out_dw").reshape(N_CHIPS, HALF // 2, D_MODEL))
    dcat0 = _mm_nt(dy0, [e_w_out], "e_out_dx", tok)
    dz0, g_pool_w, g_pool_scale = _pool_bwd(z0, dcat0, pool_w_bf, vec("e_pool_scale") + tok[0, 0])
    for g in range(len(DILATIONS)):
        dz0 = _attn_bwd_group(g, qkv_by_residue[g], z0, att, lse, dcat0, tables, dz0)
    tok = send_off("e_w_in", _mm_tn(h0, dz0, N_CHIPS, "e_in_dw"))
    dh0 = _mm_nt(dz0, e_w_in, "e_in_dx", tok)
    grad_x, g_e_pre = _pre_norm_bwd(dx1, dh0, x, vec("e_pre_norm") + tok[0, 0])

    small = {"e_pre_norm": g_e_pre, "e_pool_w": g_pool_w, "e_pool_scale": g_pool_scale, "e_post_norm": g_e_post,
             "o_pre_norm": g_o_pre, "o_sgu_norm_g": g_sgu_g, "o_sgu_norm_b": g_sgu_b, "o_sgu_w": g_sgu_w,
             "o_sgu_b": g_bias_t, "o_conv_w": g_conv_w, "o_conv_b": g_conv_b,
             "o_conv_norm_g": g_cn_g, "o_conv_norm_b": g_cn_b, "o_post_norm": g_o_post}
    return loss, grad_x, in_flight, small


def _land(in_flight, name, chip, after):
    sems, sums, land = in_flight[name]
    sums, land = _scatter_wait(sems, sums, land, after, f"scatter_wait_{name}")
    return _add_landed_join(sums, land, chip.astype(jnp.int32).reshape(1), f"add_landed_{name}")


def _place():
    x, y, c = lax.axis_index("x"), lax.axis_index("y"), lax.axis_index("c")
    others = [(1 - x, y), (x, 1 - y), (1 - x, 1 - y)]
    return x, y, c, 2 * x + y, others


SWAP_ROWS = 256
FORWARD_STAGE_BYTES = 4 << 20


def _swap_add(g, name):
    chips, r, c = g.shape
    half = r // 2
    rows_per_step = 2 * SWAP_ROWS if half % (2 * SWAP_ROWS) == 0 else SWAP_ROWS
    nb = half // rows_per_step
    steps = chips * nb

    def body(core_ref, mine_ref, theirs_ref, out_ref, landing, send_sems, recv_sems, free_sems):
        i = pl.program_id(0)
        x, y, core, _, _ = _place()
        sibling = (x, y, 1 - core)

        def send(slot):
            return pltpu.make_async_remote_copy(src_ref=theirs_ref, dst_ref=landing.at[slot], send_sem=send_sems.at[slot],
                                                recv_sem=recv_sems.at[slot], device_id=sibling, device_id_type=MESH)

        @pl.when(i < steps)
        def _():
            @pl.when(i >= 2)
            def _():
                pl.semaphore_wait(free_sems.at[i % 2], 1)

            send(i % 2).start()

        @pl.when(i >= 1)
        def _():
            landed = (i - 1) % 2
            send(landed).wait_recv()
            out_ref[...] = (mine_ref[...].astype(f32) + landing[landed].astype(f32)).astype(out_ref.dtype)

            @pl.when(i + 1 < steps)
            def _():
                pl.semaphore_signal(free_sems.at[landed], 1, device_id=sibling, device_id_type=MESH)

        @pl.when(i < steps)
        def _():
            send(i % 2).wait_send()

    def rows_of(b, h):
        return (2 * (b // nb) + h) * nb + b % nb

    block = (rows_per_step, c)
    grid_spec = pltpu.PrefetchScalarGridSpec(
        num_scalar_prefetch=1, grid=(steps + 1,),
        in_specs=[pl.BlockSpec(block, lambda i, core: (rows_of(jnp.maximum(i - 1, 0), core[0]), 0)),
                  pl.BlockSpec(block, lambda i, core: (rows_of(jnp.minimum(i, steps - 1), 1 - core[0]), 0))],
        out_specs=pl.BlockSpec(block, lambda i, core: (jnp.maximum(i - 1, 0), 0)),
        scratch_shapes=[pltpu.VMEM((2, rows_per_step, c), g.dtype), pltpu.SemaphoreType.DMA((2,)),
                        pltpu.SemaphoreType.DMA((2,)), pltpu.SemaphoreType.REGULAR((2,))])
    core = lax.axis_index("c").astype(jnp.int32).reshape(1)
    rows = g.reshape(chips * r, c)
    out = pl.pallas_call(body, name=name, grid_spec=grid_spec, out_shape=SDS((chips * half, c), g.dtype))(core, rows, rows)
    return out.reshape(chips, half, c)


HBM = pl.BlockSpec(memory_space=pltpu.HBM)
SEM = pl.BlockSpec(memory_space=pltpu.SEMAPHORE)
EFFECT = pltpu.SideEffectType.DATAFLOW_SIDE_EFFECTING


def _in_hbm(a):
    return pltpu.with_memory_space_constraint(a, pltpu.HBM)


def _cast_into_slot(w, chip, name, after, piece=0, pieces=1):
    r, c = w.shape
    c = c // pieces
    nb = r // SWAP_ROWS

    def body(chip_ref, w_ref, after_ref, o_ref):
        o_ref[...] = w_ref[...].astype(bf16)

    grid_spec = pltpu.PrefetchScalarGridSpec(
        num_scalar_prefetch=1, grid=(nb,),
        in_specs=[pl.BlockSpec((SWAP_ROWS, c), lambda i, chip: (i, piece)), ANY],
        out_specs=pl.BlockSpec((SWAP_ROWS, c), lambda i, chip: (chip[0] * nb + i, 0)))
    out = pl.pallas_call(body, name=name, grid_spec=grid_spec, out_shape=SDS((N_CHIPS * r, c), bf16))(chip, w, after)
    return out.reshape(N_CHIPS, r, c)


def _gather_start(bufs, name):
    n = len(bufs)

    def body(*refs):
        ins, sems, token = refs[:n], refs[n:3 * n], refs[4 * n]
        x, y, c, me, others = _place()
        for a in range(n):
            rows = ins[a].shape[1] // 2
            mine = ins[a].at[me, pl.ds(c * rows, rows), :]
            for k, (ox, oy) in enumerate(others):
                pltpu.make_async_remote_copy(src_ref=mine, dst_ref=mine, send_sem=sems[2 * a].at[k],
                                             recv_sem=sems[2 * a + 1].at[k], device_id=(ox, oy, c),
                                             device_id_type=MESH).start()
        token[...] = jnp.zeros_like(token)

    out = pl.pallas_call(
        body, name=name, in_specs=[HBM] * n,
        out_shape=(*[pltpu.SemaphoreType.DMA((3,))] * (2 * n), *[pltpu.HBM(b.shape, b.dtype) for b in bufs],
                   SDS((8, 128), f32)),
        out_specs=(*[SEM] * (2 * n), *[HBM] * n, pl.BlockSpec(memory_space=pltpu.VMEM)),
        input_output_aliases={a: 2 * n + a for a in range(n)},
        compiler_params=pltpu.CompilerParams(has_side_effects=EFFECT),
    )(*[_in_hbm(b) for b in bufs])
    return [(out[2 * a], out[2 * a + 1]) for a in range(n)], list(out[2 * n:3 * n]), out[3 * n]


def _gather_wait(bufs, sems, after, name):
    n = len(bufs)

    def body(*refs):
        ins, sem_refs = refs[:n], refs[n:3 * n]
        x, y, c, me, others = _place()
        for a in range(n):
            rows = ins[a].shape[1] // 2
            mine = ins[a].at[me, pl.ds(c * rows, rows), :]
            for k, (ox, oy) in enumerate(others):
                landed = ins[a].at[2 * ox + oy, pl.ds(c * rows, rows), :]
                copy = pltpu.make_async_remote_copy(src_ref=mine, dst_ref=landed, send_sem=sem_refs[2 * a].at[k],
                                                    recv_sem=sem_refs[2 * a + 1].at[k], device_id=(ox, oy, c),
                                                    device_id_type=MESH)
                copy.wait_send()
                copy.wait_recv()

    flat_sems = [s for pair in sems for s in pair]
    out = pl.pallas_call(
        body, name=name, in_specs=[HBM] * n + [SEM] * (2 * n) + [ANY],
        out_shape=tuple(pltpu.HBM(b.shape, b.dtype) for b in bufs), out_specs=tuple([HBM] * n),
        input_output_aliases={a: a for a in range(n)},
        compiler_params=pltpu.CompilerParams(has_side_effects=EFFECT),
    )(*bufs, *flat_sems, after)
    return list(out)


def _forward_halves(bufs, name):
    n = len(bufs)
    blocks = []
    for b in bufs:
        half = b.shape[1] // 2
        whole = half * b.shape[2] * b.dtype.itemsize <= FORWARD_STAGE_BYTES
        blocks.append((half, half if whole or half % SWAP_ROWS else SWAP_ROWS))
    work = [(a, k, b) for a in range(n) for k in range(3) for b in range(blocks[a][0] // blocks[a][1])]

    def body(*refs):
        outs, stages = refs[n:2 * n], refs[2 * n:3 * n]
        load_sems, send_sems, recv_sems = refs[3 * n:]
        x, y, c, me, others = _place()
        sibling = (x, y, 1 - c)

        def rows(item):
            a, k, b = item
            half, tr = blocks[a]
            ox, oy = others[k]
            return outs[a].at[2 * ox + oy, pl.ds(c * half + b * tr, tr), :]

        def load(s, item):
            return pltpu.make_async_copy(rows(item), stages[item[0]].at[s], load_sems.at[s])

        def send(s, item):
            return pltpu.make_async_remote_copy(src_ref=stages[item[0]].at[s], dst_ref=rows(item), send_sem=send_sems.at[s],
                                                recv_sem=recv_sems.at[item[0]], device_id=sibling, device_id_type=MESH)

        load(0, work[0]).start()
        for t, item in enumerate(work):
            s = t % 2
            load(s, item).wait()
            send(s, item).start()
            if t + 1 < len(work):
                if t >= 1:
                    send(1 - s, work[t - 1]).wait_send()
                load(1 - s, work[t + 1]).start()
        if len(work) > 1:
            send(len(work) % 2, work[-2]).wait_send()
        send((len(work) - 1) % 2, work[-1]).wait_send()
        for a in range(n):
            theirs = outs[a].at[pl.ds(0, 3), pl.ds(0, blocks[a][0]), :]
            pltpu.make_async_remote_copy(src_ref=theirs, dst_ref=theirs, send_sem=send_sems.at[0], recv_sem=recv_sems.at[a],
                                         device_id=sibling, device_id_type=MESH).wait_recv()

    out = pl.pallas_call(
        body, name=name, in_specs=[ANY] * n, out_specs=[ANY] * n, out_shape=[SDS(b.shape, b.dtype) for b in bufs],
        input_output_aliases={a: a for a in range(n)},
        scratch_shapes=[pltpu.VMEM((2, blocks[a][1], bufs[a].shape[2]), bufs[a].dtype) for a in range(n)]
        + [pltpu.SemaphoreType.DMA((2,)), pltpu.SemaphoreType.DMA((2,)), pltpu.SemaphoreType.DMA((n,))],
    )(*bufs)
    return list(out)


def _scatter_start(chip_sums, name):
    def body(a_ref, land_ref, send_sems, recv_sems, a_thru, land_thru, token):
        x, y, c, me, others = _place()
        for k, (ox, oy) in enumerate(others):
            pltpu.make_async_remote_copy(src_ref=a_ref.at[2 * ox + oy], dst_ref=land_ref.at[me], send_sem=send_sems.at[k],
                                         recv_sem=recv_sems.at[k], device_id=(ox, oy, c), device_id_type=MESH).start()
        token[...] = jnp.zeros_like(token)

    shape = pltpu.HBM(chip_sums.shape, chip_sums.dtype)
    send, recv, a_thru, land, token = pl.pallas_call(
        body, name=name, in_specs=[HBM, HBM],
        out_shape=(pltpu.SemaphoreType.DMA((3,)), pltpu.SemaphoreType.DMA((3,)), shape, shape, SDS((8, 128), f32)),
        out_specs=(SEM, SEM, HBM, HBM, pl.BlockSpec(memory_space=pltpu.VMEM)), input_output_aliases={0: 2, 1: 3},
        compiler_params=pltpu.CompilerParams(has_side_effects=EFFECT),
    )(_in_hbm(chip_sums), _in_hbm(lax.empty(chip_sums.shape, chip_sums.dtype)))
    return (send, recv), a_thru, land, token


def _scatter_wait(sems, chip_sums, land, after, name):
    def body(a_ref, land_ref, send_sems, recv_sems, after_ref, a_out, land_out):
        x, y, c, me, others = _place()
        for k, (ox, oy) in enumerate(others):
            copy = pltpu.make_async_remote_copy(
                src_ref=a_ref.at[2 * ox + oy], dst_ref=land_ref.at[2 * ox + oy], send_sem=send_sems.at[k],
                recv_sem=recv_sems.at[k], device_id=(ox, oy, c), device_id_type=MESH)
            copy.wait_send()
            copy.wait_recv()

    shape = pltpu.HBM(chip_sums.shape, chip_sums.dtype)
    return pl.pallas_call(
        body, name=name, in_specs=[HBM, HBM, SEM, SEM, ANY], out_shape=(shape, shape), out_specs=(HBM, HBM),
        input_output_aliases={0: 0, 1: 1}, compiler_params=pltpu.CompilerParams(has_side_effects=EFFECT),
    )(chip_sums, land, sems[0], sems[1], after)


def _add_landed_join(chip_sums, land, chip, name):
    chips, rh, c = chip_sums.shape
    nb = rh // SWAP_ROWS

    def body(chip_ref, own_ref, l1_ref, l2_ref, l3_ref, out_hbm, buf, send_sems, recv_sem, local_sems):
        i = pl.program_id(0)
        slot = i % 2
        x, y, core, _, _ = _place()
        sibling = (x, y, 1 - core)

        def copies(s, step):
            rows = pl.ds(pl.multiple_of((core * nb + step) * SWAP_ROWS, SWAP_ROWS), SWAP_ROWS)
            keep = pltpu.make_async_copy(buf.at[s], out_hbm.at[rows, :], local_sems.at[s])
            give = pltpu.make_async_remote_copy(src_ref=buf.at[s], dst_ref=out_hbm.at[rows, :], send_sem=send_sems.at[s],
                                                recv_sem=recv_sem.at[0], device_id=sibling, device_id_type=MESH)
            return keep, give

        def drain(s, step):
            keep, give = copies(s, step)
            keep.wait()
            give.wait_send()

        @pl.when(i >= 2)
        def _():
            drain(slot, i - 2)

        buf[slot] = ((own_ref[...].astype(f32) + l1_ref[...].astype(f32)) + l2_ref[...].astype(f32)) + l3_ref[...].astype(f32)
        keep, give = copies(slot, i)
        keep.start()
        give.start()

        @pl.when(i == nb - 1)
        def _():
            drain(slot, i)
            if nb > 1:
                drain(1 - slot, i - 1)
            theirs = out_hbm.at[pl.ds((1 - core) * rh, rh), :]
            pltpu.make_async_remote_copy(src_ref=theirs, dst_ref=theirs, send_sem=send_sems.at[0], recv_sem=recv_sem.at[0],
                                         device_id=sibling, device_id_type=MESH).wait_recv()

    block = (SWAP_ROWS, c)
    from_slot = lambda d: pl.BlockSpec(block, lambda i, chip: (((chip[0] + d) % chips) * nb + i, 0))
    grid_spec = pltpu.PrefetchScalarGridSpec(
        num_scalar_prefetch=1, grid=(nb,), in_specs=[from_slot(0), from_slot(1), from_slot(2), from_slot(3)],
        out_specs=ANY,
        scratch_shapes=[pltpu.VMEM((2, SWAP_ROWS, c), f32), pltpu.SemaphoreType.DMA((2,)),
                        pltpu.SemaphoreType.DMA((1,)), pltpu.SemaphoreType.DMA((2,))])
    land_rows = land.reshape(chips * rh, c)
    return pl.pallas_call(body, name=name, grid_spec=grid_spec, out_shape=SDS((2 * rh, c), f32))(
        chip, chip_sums.reshape(chips * rh, c), land_rows, land_rows, land_rows)


def _adamw_update(w_ref, g_ref, m_ref, v_ref, d_ref, nm_ref, nv_ref):
    g = g_ref[...]
    nm = ADAM_B1 * m_ref[...] + (1.0 - ADAM_B1) * g
    nv = ADAM_B2 * v_ref[...] + (1.0 - ADAM_B2) * (g * g)
    nm_ref[...] = nm
    nv_ref[...] = nv
    m_hat = nm / (1.0 - ADAM_B1 ** ADAM_STEP)
    v_hat = nv / (1.0 - ADAM_B2 ** ADAM_STEP)
    d_ref[...] = -ADAM_LR * (m_hat / (jnp.sqrt(v_hat) + ADAM_EPS) + ADAM_WD * w_ref[...])


def _adamw(w, g, m, v, name):
    r, c = w.shape
    tr = 128 if r % 128 == 0 else r

    def body(w_ref, g_ref, m_ref, v_ref, g_out_ref, d_ref, nm_ref, nv_ref):
        g_out_ref[...] = g_ref[...]
        _adamw_update(w_ref, g_ref, m_ref, v_ref, d_ref, nm_ref, nv_ref)

    spec = pl.BlockSpec((tr, c), lambda i: (i, 0))
    return pl.pallas_call(body, name=name, grid=(r // tr,), in_specs=[spec] * 4, out_specs=[spec] * 4,
                          out_shape=[SDS((r, c), f32)] * 4)(w, g, m, v)


SMALL_PACKING = {
    "e_pre_norm": ((1, 2048), 8, (1, 2048)), "e_pool_w": ((1024, 256), 1024, (256, 256)),
    "e_pool_scale": ((1, 1024), 8, (1, 1024)), "e_post_norm": ((1, 2048), 8, (1, 2048)),
    "o_pre_norm": ((1, 2048), 8, (1, 512)), "o_sgu_norm_g": ((1, 1024), 8, (1, 256)),
    "o_sgu_norm_b": ((1, 1024), 8, (1, 256)), "o_sgu_w": ((512, 128), 512, (512, 128)),
    "o_sgu_b": ((128, 128), 8, (4, 128)), "o_conv_w": ((31, 1024), 128, (31, 256)), "o_conv_b": ((1, 1024), 8, (1, 256)),
    "o_conv_norm_g": ((1, 1024), 8, (1, 256)), "o_conv_norm_b": ((1, 1024), 8, (1, 256)),
    "o_post_norm": ((1, 2048), 8, (1, 512)),
}
SMALL_PACKED_ROWS = 1792


def _small_finalize(grads, ws, ms, vs, after):
    names = list(SMALL_ORDER)
    n = len(names)
    half, piece = SMALL_PACKED_ROWS // 2, SMALL_PACKED_ROWS // 8
    first_row, row = {}, 0
    for k in names:
        first_row[k] = row
        row += SMALL_PACKING[k][1]

    def body(*refs):
        g_refs, total = refs[0:n], refs[n + 1]
        pack, from_sibling, from_chips, send_a, recv_a, send_b, recv_b, send_c, recv_c, send_d, recv_d = refs[n + 2:]
        x, y, c, me, others = _place()

        for r0 in range(0, SMALL_PACKED_ROWS, piece):
            pack[r0:r0 + piece, :] = jnp.zeros((piece, LANES), f32)
        for k, g_ref in zip(names, g_refs):
            (rows, width), _, _ = SMALL_PACKING[k]
            r0 = first_row[k]
            if k == "o_sgu_b":
                pack[r0:r0 + 4, 0:CHUNK] = g_ref[...].T[0:4, :]
            elif width < LANES:
                pack[r0:r0 + rows, 0:width] = g_ref[...]
            else:
                for j in range(width // LANES):
                    dst = r0 + j * (1 if rows == 1 else 32)
                    pack[dst:dst + rows, :] = g_ref[:, j * LANES:(j + 1) * LANES]

        sibling = (x, y, 1 - c)
        swap = pltpu.make_async_remote_copy(
            src_ref=pack.at[pl.ds(pl.multiple_of((1 - c) * half, 8), half), :], dst_ref=from_sibling,
            send_sem=send_a.at[0], recv_sem=recv_a.at[0], device_id=sibling, device_id_type=MESH)
        swap.start()
        swap.wait()
        for j in range(4):
            rows = pl.ds(pl.multiple_of(c * half + j * piece, 8), piece)
            pack[rows, :] = pack[rows, :] + from_sibling[j * piece:(j + 1) * piece, :]

        def piece_of(chip):
            return pl.ds(pl.multiple_of(c * half + chip * piece, 8), piece)

        def to_chip(k):
            ox, oy = others[k]
            return pltpu.make_async_remote_copy(
                src_ref=pack.at[piece_of(2 * ox + oy), :], dst_ref=from_chips.at[me], send_sem=send_b.at[k],
                recv_sem=recv_b.at[k], device_id=(ox, oy, c), device_id_type=MESH)

        for k in range(3):
            to_chip(k).start()
        from_chips[me] = pack[piece_of(me), :]
        for k, (ox, oy) in enumerate(others):
            landed = from_chips.at[2 * ox + oy]
            pltpu.make_async_remote_copy(src_ref=landed, dst_ref=landed, send_sem=send_b.at[k], recv_sem=recv_b.at[k],
                                         device_id=(ox, oy, c), device_id_type=MESH).wait_recv()
        for k in range(3):
            to_chip(k).wait_send()
        mine = pl.ds(pl.multiple_of(c * half + me * piece, 8), piece)
        total[mine, :] = ((from_chips[0] + from_chips[1]) + from_chips[2]) + from_chips[3]

        def to_same_core(k):
            ox, oy = others[k]
            return pltpu.make_async_remote_copy(
                src_ref=total.at[mine, :], dst_ref=total.at[mine, :], send_sem=send_c.at[k], recv_sem=recv_c.at[k],
                device_id=(ox, oy, c), device_id_type=MESH)

        for k in range(3):
            to_same_core(k).start()
        for k, (ox, oy) in enumerate(others):
            theirs = total.at[piece_of(2 * ox + oy), :]
            pltpu.make_async_remote_copy(src_ref=theirs, dst_ref=theirs, send_sem=send_c.at[k], recv_sem=recv_c.at[k],
                                         device_id=(ox, oy, c), device_id_type=MESH).wait_recv()
        for k in range(3):
            to_same_core(k).wait_send()
        my_half = total.at[pl.ds(pl.multiple_of(c * half, 8), half), :]
        join = pltpu.make_async_remote_copy(src_ref=my_half, dst_ref=my_half, send_sem=send_d.at[0], recv_sem=recv_d.at[0],
                                            device_id=sibling, device_id_type=MESH)
        join.start()
        their_half = total.at[pl.ds(pl.multiple_of((1 - c) * half, 8), half), :]
        pltpu.make_async_remote_copy(src_ref=their_half, dst_ref=their_half, send_sem=send_d.at[0], recv_sem=recv_d.at[0],
                                     device_id=sibling, device_id_type=MESH).wait_recv()
        join.wait_send()

    whole = pl.BlockSpec(memory_space=pltpu.VMEM)
    total = pl.pallas_call(
        body, name="small_allreduce", in_specs=[whole] * n + [ANY], out_specs=whole,
        out_shape=SDS((SMALL_PACKED_ROWS, LANES), f32),
        scratch_shapes=[pltpu.VMEM((SMALL_PACKED_ROWS, LANES), f32), pltpu.VMEM((half, LANES), f32),
                        pltpu.VMEM((N_CHIPS, piece, LANES), f32),
                        pltpu.SemaphoreType.DMA((1,)), pltpu.SemaphoreType.DMA((1,)), pltpu.SemaphoreType.DMA((3,)),
                        pltpu.SemaphoreType.DMA((3,)), pltpu.SemaphoreType.DMA((3,)), pltpu.SemaphoreType.DMA((3,)),
                        pltpu.SemaphoreType.DMA((1,)), pltpu.SemaphoreType.DMA((1,))],
    )(*grads, after)

    def update(*refs):
        total = refs[0]
        w_refs, m_refs, v_refs = refs[1:n + 1], refs[n + 1:2 * n + 1], refs[2 * n + 1:3 * n + 1]
        outs = refs[3 * n + 1:]
        me = 2 * lax.axis_index("x") + lax.axis_index("y")

        def of_chip(candidates):
            value = candidates[0]
            for j in range(1, N_CHIPS):
                value = jnp.where(me == j, candidates[j], value)
            return value

        for i, k in enumerate(names):
            (rows, width), _, (local_rows, local_width) = SMALL_PACKING[k]
            r0 = first_row[k]
            if k == "o_sgu_b":
                g = total[r0:r0 + 4, 0:CHUNK]
            elif k == "e_pool_w":
                for grp in range(4):
                    src = pl.ds(pl.multiple_of(r0 + grp * POOL_CH + me * 64, 8), 64)
                    dst = slice(grp * 64, (grp + 1) * 64)
                    _adamw_rows(total[src, :], i, dst, w_refs, m_refs, v_refs, outs, n)
                continue
            elif k == "o_conv_w":
                g = total[pl.ds(pl.multiple_of(r0 + me * 32, 8), 32), :][0:CONV_K]
            elif width < LANES:
                g = total[r0:r0 + rows, 0:width]
            else:
                lanes = [total[r0 + j:r0 + j + 1, :] for j in range(width // LANES)]
                per_chip = local_width // LANES
                if local_width == width:
                    g = jnp.concatenate(lanes, axis=1)
                elif per_chip == 1:
                    g = of_chip(lanes)
                else:
                    g = of_chip([jnp.concatenate(lanes[j * per_chip:(j + 1) * per_chip], axis=1) for j in range(N_CHIPS)])
            _adamw_rows(g, i, slice(None), w_refs, m_refs, v_refs, outs, n)

    shard_shapes = [SMALL_PACKING[k][2] for k in names]
    out = pl.pallas_call(update, name="small_update", in_specs=[whole] * (3 * n + 1), out_specs=[whole] * (4 * n),
                         out_shape=[SDS(s, f32) for s in shard_shapes] * 4)(total, *ws, *ms, *vs)
    return out[:n], out[n:2 * n], out[2 * n:3 * n], out[3 * n:]


def _adamw_rows(g, i, rows, w_refs, m_refs, v_refs, outs, n):
    w, m, v = w_refs[i][rows, :], m_refs[i][rows, :], v_refs[i][rows, :]
    nm = ADAM_B1 * m + (1.0 - ADAM_B1) * g
    nv = ADAM_B2 * v + (1.0 - ADAM_B2) * (g * g)
    m_hat = nm / (1.0 - ADAM_B1 ** ADAM_STEP)
    v_hat = nv / (1.0 - ADAM_B2 ** ADAM_STEP)
    outs[i][rows, :] = g
    outs[n + i][rows, :] = -ADAM_LR * (m_hat / (jnp.sqrt(v_hat) + ADAM_EPS) + ADAM_WD * w)
    outs[2 * n + i][rows, :] = nm
    outs[3 * n + i][rows, :] = nv


def _pack(arrays, total_rows=None):
    parts = []
    rows = 0
    for a in arrays:
        flat = a.reshape(-1, LANES)
        pad = -flat.shape[0] % 8
        parts.append(jnp.pad(flat, ((0, pad), (0, 0))))
        rows += flat.shape[0] + pad
    if total_rows is not None:
        parts.append(jnp.zeros((total_rows - rows, LANES), arrays[0].dtype))
    return jnp.concatenate(parts, axis=0)


def _unpack(buf, shapes):
    out = []
    row = 0
    lead = buf.shape[:-2]
    for shape in shapes:
        size = 1
        for s in shape:
            size *= s
        rows = size // LANES
        out.append(buf[..., row:row + rows, :].reshape(lead + tuple(shape)))
        row += rows + (-rows % 8)
    return out


BIG = ("e_w_in", "e_w_out", "o_w_in", "o_w_out")
SHARDED_SMALL = {
    "e_pool_w": ((4, 64, 256), 1), "o_pre_norm": ((512,), 0), "o_sgu_norm_g": ((256,), 0), "o_sgu_norm_b": ((256,), 0),
    "o_conv_w": ((31, 256), 1), "o_conv_b": ((256,), 0), "o_conv_norm_g": ((256,), 0), "o_conv_norm_b": ((256,), 0),
    "o_post_norm": ((512,), 0),
}
SMALL_ORDER = ("e_pre_norm", "e_pool_w", "e_pool_scale", "e_post_norm", "o_pre_norm", "o_sgu_norm_g", "o_sgu_norm_b",
               "o_sgu_w", "o_sgu_b", "o_conv_w", "o_conv_b", "o_conv_norm_g", "o_conv_norm_b", "o_post_norm")
ALL_ORDER = ("e_pre_norm", "e_w_in", "e_pool_w", "e_pool_scale", "e_w_out", "e_post_norm", "o_pre_norm", "o_w_in",
             "o_sgu_norm_g", "o_sgu_norm_b", "o_sgu_w", "o_sgu_b", "o_conv_w", "o_conv_b", "o_conv_norm_g",
             "o_conv_norm_b", "o_w_out", "o_post_norm")


def _full_shape(name):
    shape, axis = SHARDED_SMALL[name]
    return tuple(s * N_CHIPS if i == axis else s for i, s in enumerate(shape))


def _from_chips(name, stacked):
    shape, axis = SHARDED_SMALL[name]
    return jnp.moveaxis(stacked, 0, axis).reshape(_full_shape(name))


def kernel(x, e_pre_norm, e_w_in, e_pool_w, e_pool_scale, e_w_out, e_post_norm, o_pre_norm, o_w_in, o_sgu_norm_g, o_sgu_norm_b, o_sgu_w, o_sgu_b, o_conv_w, o_conv_b, o_conv_norm_g, o_conv_norm_b, o_w_out, o_post_norm, loss_target, m_e_pre_norm, m_e_w_in, m_e_pool_w, m_e_pool_scale, m_e_w_out, m_e_post_norm, m_o_pre_norm, m_o_w_in, m_o_sgu_norm_g, m_o_sgu_norm_b, m_o_sgu_w, m_o_sgu_b, m_o_conv_w, m_o_conv_b, m_o_conv_norm_g, m_o_conv_norm_b, m_o_w_out, m_o_post_norm, v_e_pre_norm, v_e_w_in, v_e_pool_w, v_e_pool_scale, v_e_w_out, v_e_post_norm, v_o_pre_norm, v_o_w_in, v_o_sgu_norm_g, v_o_sgu_norm_b, v_o_sgu_w, v_o_sgu_b, v_o_conv_w, v_o_conv_b, v_o_conv_norm_g, v_o_conv_norm_b, v_o_w_out, v_o_post_norm):
    w = dict(e_pre_norm=e_pre_norm, e_w_in=e_w_in, e_pool_w=e_pool_w, e_pool_scale=e_pool_scale, e_w_out=e_w_out,
             e_post_norm=e_post_norm, o_pre_norm=o_pre_norm, o_w_in=o_w_in, o_sgu_norm_g=o_sgu_norm_g,
             o_sgu_norm_b=o_sgu_norm_b, o_sgu_w=o_sgu_w, o_sgu_b=o_sgu_b, o_conv_w=o_conv_w, o_conv_b=o_conv_b,
             o_conv_norm_g=o_conv_norm_g, o_conv_norm_b=o_conv_norm_b, o_w_out=o_w_out, o_post_norm=o_post_norm)
    m = dict(e_pre_norm=m_e_pre_norm, e_w_in=m_e_w_in, e_pool_w=m_e_pool_w, e_pool_scale=m_e_pool_scale,
             e_w_out=m_e_w_out, e_post_norm=m_e_post_norm, o_pre_norm=m_o_pre_norm, o_w_in=m_o_w_in,
             o_sgu_norm_g=m_o_sgu_norm_g, o_sgu_norm_b=m_o_sgu_norm_b, o_sgu_w=m_o_sgu_w, o_sgu_b=m_o_sgu_b,
             o_conv_w=m_o_conv_w, o_conv_b=m_o_conv_b, o_conv_norm_g=m_o_conv_norm_g, o_conv_norm_b=m_o_conv_norm_b,
             o_w_out=m_o_w_out, o_post_norm=m_o_post_norm)
    v = dict(e_pre_norm=v_e_pre_norm, e_w_in=v_e_w_in, e_pool_w=v_e_pool_w, e_pool_scale=v_e_pool_scale,
             e_w_out=v_e_w_out, e_post_norm=v_e_post_norm, o_pre_norm=v_o_pre_norm, o_w_in=v_o_w_in,
             o_sgu_norm_g=v_o_sgu_norm_g, o_sgu_norm_b=v_o_sgu_norm_b, o_sgu_w=v_o_sgu_w, o_sgu_b=v_o_sgu_b,
             o_conv_w=v_o_conv_w, o_conv_b=v_o_conv_b, o_conv_norm_g=v_o_conv_norm_g, o_conv_norm_b=v_o_conv_norm_b,
             o_w_out=v_o_w_out, o_post_norm=v_o_post_norm)
    w, m, v = ({k: a[0] for k, a in d.items()} for d in (w, m, v))
    chip = 2 * lax.axis_index("x") + lax.axis_index("y")

    loss, grad_x, in_flight, small = _step(x[0], loss_target[0], w, chip)

    grads, delta, new_m, new_v = {}, {}, {}, {}

    def rows_of(a):
        return a.reshape(-1, a.shape[-1])

    after = grad_x
    for k in ("o_w_out", "o_w_in", "e_w_out", "small", "e_w_in"):
        if k == "small":
            small_grads = [small[name].reshape(SMALL_PACKING[name][0]) for name in SMALL_ORDER]
            updates = _small_finalize(small_grads, *[[rows_of(d[name]) for name in SMALL_ORDER] for d in (w, m, v)], after)
            for d, arrays in zip((grads, delta, new_m, new_v), updates):
                for name, a in zip(SMALL_ORDER, arrays):
                    d[name] = a.reshape(w[name].shape)
            after = updates[1][0]
            continue
        grads[k], delta[k], new_m[k], new_v[k] = _adamw(w[k], _land(in_flight, k, chip, after), m[k], v[k], f"adamw_{k}")
        after = delta[k]
    loss = lax.psum(loss[0, 0], ("x", "y", "c"))

    outs = [loss, grad_x[None]]
    for d in (grads, delta, new_m, new_v):
        outs += [d[k][None] for k in ALL_ORDER]
    return tuple(outs)
```

```python
import jax
import jax.numpy as jnp
from jax import lax
from jax.experimental import pallas as pl
from jax.experimental.pallas import tpu as pltpu

f32 = jnp.float32
bf16 = jnp.bfloat16
SDS = jax.ShapeDtypeStruct

SEQ = 2048
D_MODEL = 2048
EPS = 1e-6
NEG = -1e30
HEAD_DIM = 128
ROT_HALF = 16
ROPE_THETA = 500000.0
DILATIONS = (1, 4, 16)
SPAN = 128
N_HEADS = 8
HALF = 1024
POOL_CH = 256
CONV_K = 31
CONV_PAD = 32
CHUNK = 128
N_CHIPS = 4
LANES = 256
E_IN_PIECES = 3
SMALL_SHARD_ROWS = 352
ANY = pl.BlockSpec(memory_space=pl.ANY)
MESH = pl.DeviceIdType.MESH

ADAM_LR = 0.001
ADAM_B1 = 0.9
ADAM_B2 = 0.999
ADAM_EPS = 1e-08
ADAM_WD = 0.01
ADAM_STEP = 10


def _dot(a, b):
    return jnp.dot(a, b, preferred_element_type=f32)


def _dot_nt(a, b):
    return lax.dot_general(a, b, (((1,), (1,)), ((), ())), preferred_element_type=f32)


def _dot_tn(a, b):
    return lax.dot_general(a, b, (((0,), (0,)), ((), ())), preferred_element_type=f32)


def _sigmoid(x):
    return 1.0 / (1.0 + jnp.exp(-x))


def _silu_and_grad(x):
    s = _sigmoid(x)
    return x * s, s * (1.0 + x * (1.0 - s))


def _rms_fwd(x, g):
    r = lax.rsqrt(jnp.mean(x * x, axis=-1, keepdims=True) + EPS)
    return x * r * g


def _rms_bwd(x, g, dout):
    r = lax.rsqrt(jnp.mean(x * x, axis=-1, keepdims=True) + EPS)
    xh = x * r
    dg = jnp.sum(dout * xh, axis=0, keepdims=True)
    dxh = dout * g
    dx = r * (dxh - xh * jnp.mean(dxh * xh, axis=-1, keepdims=True))
    return dx, dg


def _ln_stats(x):
    mu = jnp.mean(x, axis=-1, keepdims=True)
    xc = x - mu
    rstd = lax.rsqrt(jnp.mean(xc * xc, axis=-1, keepdims=True) + EPS)
    return xc * rstd, rstd


def _ln_bwd(xh, rstd, g, dout):
    dg = jnp.sum(dout * xh, axis=0, keepdims=True)
    db = jnp.sum(dout, axis=0, keepdims=True)
    dxh = dout * g
    dx = rstd * (dxh - jnp.mean(dxh, axis=-1, keepdims=True) - xh * jnp.mean(dxh * xh, axis=-1, keepdims=True))
    return dx, dg, db


def _accumulate(ref, value, first):
    @pl.when(first)
    def _():
        ref[...] = value

    @pl.when(jnp.logical_not(first))
    def _():
        ref[...] += value


def _write_behind(step, steps, tiles, sems, window):
    slot = step % 2

    def copies(s, at):
        return [pltpu.make_async_copy(tile.at[s], window(t, at), sems.at[2 * t + s]) for t, tile in enumerate(tiles)]

    @pl.when(step >= 2)
    def _():
        for cp in copies(slot, step - 2):
            cp.wait()

    def full():
        for cp in copies(slot, step):
            cp.start()

        @pl.when(step == steps - 1)
        def _():
            for cp in copies(slot, step):
                cp.wait()
            if steps > 1:
                for cp in copies(1 - slot, step - 1):
                    cp.wait()

    return [tile.at[slot] for tile in tiles], full


def _columns(ref, first, width):
    return ref.at[:, pl.ds(pl.multiple_of(first, 128), width)]


def _col_tile(ns):
    for t in (1024, 768, 512, 256):
        if ns % t == 0:
            return t
    raise ValueError(ns)


def _mm_nn(a, w, out_dtype, name, piece=0, pieces=1, into=None):
    m, k = a.shape
    j, _, ns = w.shape
    tm, tn = m, _col_tile(ns)
    nb = ns // tn

    def body(a_ref, w_ref, *rest):
        rest[-1][...] = _dot(a_ref[...], w_ref[...]).astype(rest[-1].dtype)

    return pl.pallas_call(
        body, name=name, grid=(j * nb, m // tm),
        in_specs=[pl.BlockSpec((tm, k), lambda n, i: (i, 0)),
                  pl.BlockSpec((None, k, tn), lambda n, i: (n // nb, 0, n % nb))] + ([] if into is None else [ANY]),
        out_specs=pl.BlockSpec((tm, tn), lambda n, i: (i, ((n // nb) * pieces + piece) * nb + n % nb)),
        out_shape=SDS((m, j * ns * pieces), out_dtype),
        input_output_aliases={} if into is None else {2: 0},
    )(a, w, *([] if into is None else [into]))


def _mm_nt(dz, ws, name, after):
    m, _ = dz.shape
    pieces = len(ws)
    j, k, ns = ws[0].shape
    tm, tk = 1024, 1024

    def body(dz_ref, *rest):
        w_refs, o_ref = rest[:pieces], rest[pieces + 1]
        total = _dot_nt(dz_ref[:, 0:ns], w_refs[0][...])
        for q in range(1, pieces):
            total = total + _dot_nt(dz_ref[:, q * ns:(q + 1) * ns], w_refs[q][...])
        if j == 1:
            o_ref[...] = total.astype(bf16)
            return
        sum_ref = rest[pieces + 2]
        r = pl.program_id(2)
        _accumulate(sum_ref, total, r == 0)

        @pl.when(r == j - 1)
        def _():
            o_ref[...] = sum_ref[...].astype(bf16)

    return pl.pallas_call(
        body, name=name, grid=(m // tm, k // tk, j),
        in_specs=[pl.BlockSpec((tm, pieces * ns), lambda i, kk, r: (i, r))]
        + [pl.BlockSpec((None, tk, ns), lambda i, kk, r: (r, kk, 0))] * pieces + [ANY],
        out_specs=pl.BlockSpec((tm, tk), lambda i, kk, r: (i, kk)),
        out_shape=SDS((m, k), bf16), scratch_shapes=[] if j == 1 else [pltpu.VMEM((tm, tk), f32)],
    )(dz, *ws, after)


def _mm_tn(a, dz, j, name):
    m, k = a.shape
    ns = dz.shape[1] // j
    tk, tn = 1024, _col_tile(ns)
    nb = ns // tn

    def body(a_ref, dz_ref, o_ref):
        o_ref[...] = _dot_tn(a_ref[...], dz_ref[...]).astype(o_ref.dtype)

    return pl.pallas_call(
        body, name=name, grid=(k // tk, j * nb),
        in_specs=[pl.BlockSpec((m, tk), lambda kk, n: (0, kk)),
                  pl.BlockSpec((m, tn), lambda kk, n: (0, n))],
        out_specs=pl.BlockSpec((None, tk, tn), lambda kk, n: (n // nb, kk, n % nb)),
        out_shape=SDS((j, k, ns), bf16),
    )(a, dz)


ROWS = 512


def _row_spec(width=D_MODEL, col=0):
    return pl.BlockSpec((ROWS, width), lambda i: (i, col))


def _vec_spec(width=D_MODEL):
    return pl.BlockSpec((1, width), lambda i: (0, 0))


def _pre_norm(x, g):
    def body(x_ref, g_ref, h_ref):
        h_ref[...] = _rms_fwd(x_ref[...], g_ref[...]).astype(bf16)

    return pl.pallas_call(
        body, name="pre_norm", grid=(SEQ // ROWS,), in_specs=[_row_spec(), _vec_spec()],
        out_specs=_row_spec(), out_shape=SDS((SEQ, D_MODEL), bf16))(x, g)


def _mid_norm(x, y, g_post, g_pre):
    def body(x_ref, y_ref, gpost_ref, gpre_ref, x1_ref, h1_ref):
        x1 = x_ref[...] + _rms_fwd(y_ref[...], gpost_ref[...])
        x1_ref[...] = x1
        h1_ref[...] = _rms_fwd(x1, gpre_ref[...]).astype(bf16)

    return pl.pallas_call(
        body, name="mid_norm", grid=(SEQ // ROWS,),
        in_specs=[_row_spec(), _row_spec(), _vec_spec(), _vec_spec()],
        out_specs=[_row_spec(), _row_spec()],
        out_shape=[SDS((SEQ, D_MODEL), f32), SDS((SEQ, D_MODEL), bf16)])(x, y, g_post, g_pre)


def _final_norm_loss(x1, y, g_post, target):
    def body(x1_ref, y_ref, g_ref, t_ref, loss_ref, dx2_ref, dy_ref, dg_ref):
        first = pl.program_id(0) == 0
        y = y_ref[...]
        g = g_ref[...]
        err = x1_ref[...] + _rms_fwd(y, g) - t_ref[...]
        sq = jnp.sum(jnp.sum(err * err, axis=1, keepdims=True), axis=0, keepdims=True)
        _accumulate(loss_ref, sq * (0.5 / D_MODEL), first)
        dx2 = err * (1.0 / D_MODEL)
        dx2_ref[...] = dx2
        dy, dg = _rms_bwd(y, g, dx2)
        dy_ref[...] = dy.astype(bf16)
        _accumulate(dg_ref, dg, first)

    return pl.pallas_call(
        body, name="final_norm_loss", grid=(SEQ // ROWS,),
        in_specs=[_row_spec(), _row_spec(), _vec_spec(), _row_spec()],
        out_specs=[pl.BlockSpec((1, 1), lambda i: (0, 0)), _row_spec(), _row_spec(), _vec_spec()],
        out_shape=[SDS((1, 1), f32), SDS((SEQ, D_MODEL), f32), SDS((SEQ, D_MODEL), bf16), SDS((1, D_MODEL), f32)],
    )(x1, y, g_post, target)


def _mid_norm_bwd(dx2, dh1, x1, y0, g_pre, g_post):
    def body(dx2_ref, dh1_ref, x1_ref, y0_ref, gpre_ref, gpost_ref, dx1_ref, dy0_ref, dgpre_ref, dgpost_ref):
        first = pl.program_id(0) == 0
        d_in, dgpre = _rms_bwd(x1_ref[...], gpre_ref[...], dh1_ref[...])
        dx1 = dx2_ref[...] + d_in
        dx1_ref[...] = dx1
        dy0, dgpost = _rms_bwd(y0_ref[...], gpost_ref[...], dx1)
        dy0_ref[...] = dy0.astype(bf16)
        _accumulate(dgpre_ref, dgpre, first)
        _accumulate(dgpost_ref, dgpost, first)

    return pl.pallas_call(
        body, name="mid_norm_bwd", grid=(SEQ // ROWS,),
        in_specs=[_row_spec(), _row_spec(), _row_spec(), _row_spec(), _vec_spec(), _vec_spec()],
        out_specs=[_row_spec(), _row_spec(), _vec_spec(), _vec_spec()],
        out_shape=[SDS((SEQ, D_MODEL), f32), SDS((SEQ, D_MODEL), bf16), SDS((1, D_MODEL), f32), SDS((1, D_MODEL), f32)],
    )(dx2, dh1, x1, y0, g_pre, g_post)


def _pre_norm_bwd(dx1, dh0, x, g):
    def body(dx1_ref, dh0_ref, x_ref, g_ref, dx_ref, dg_ref):
        d_in, dg = _rms_bwd(x_ref[...], g_ref[...], dh0_ref[...])
        dx_ref[...] = dx1_ref[...] + d_in
        _accumulate(dg_ref, dg, pl.program_id(0) == 0)

    return pl.pallas_call(
        body, name="pre_norm_bwd", grid=(SEQ // ROWS,),
        in_specs=[_row_spec(), _row_spec(), _row_spec(), _vec_spec()],
        out_specs=[_row_spec(), _vec_spec()],
        out_shape=[SDS((SEQ, D_MODEL), f32), SDS((1, D_MODEL), f32)])(dx1, dh0, x, g)


def _pool_count(g):
    row = lax.broadcasted_iota(jnp.int32, (SEQ, 1), 0)
    width = jnp.left_shift(2, g)
    return row, width, jnp.minimum(row + 1, width).astype(f32)


def _trailing_sum(x, row, width):
    s = x
    for k in (1, 2, 4, 8):
        shifted = jnp.where(row >= k, pltpu.roll(s, k, 0), 0.0)
        s = jnp.where(width > k, s + shifted, s)
    return s


def _leading_sum(x, row, width):
    s = x
    for k in (1, 2, 4, 8):
        shifted = jnp.where(row < SEQ - k, pltpu.roll(s, SEQ - k, 0), 0.0)
        s = jnp.where(width > k, s + shifted, s)
    return s


def _pool_specs():
    a_in = pl.BlockSpec((SEQ, POOL_CH), lambda g: (0, g))
    a_gate = pl.BlockSpec((SEQ, POOL_CH), lambda g: (0, 4 + g))
    w = pl.BlockSpec((None, POOL_CH, POOL_CH), lambda g: (g, 0, 0))
    scale = pl.BlockSpec((1, POOL_CH), lambda g: (0, g))
    return a_in, a_gate, w, scale


def _pool_fwd(z0, pool_w, pool_scale):
    def body(a_ref, gate_ref, w_ref, scale_ref, ya_ref):
        row, width, count = _pool_count(pl.program_id(0))
        a = a_ref[...]
        pooled = _trailing_sum(a, row, width) / count - a
        mixed = _dot(pooled.astype(bf16), w_ref[...]) * scale_ref[...]
        gate = gate_ref[...]
        ya_ref[...] = (mixed * gate * _sigmoid(gate)).astype(bf16)

    return pl.pallas_call(
        body, name="pool_fwd", grid=(4,), in_specs=list(_pool_specs()),
        out_specs=pl.BlockSpec((SEQ, POOL_CH), lambda g: (0, g)),
        out_shape=SDS((SEQ, 2 * HALF), bf16))(z0, z0, pool_w, pool_scale)


def _pool_bwd(z0, dcat, pool_w, pool_scale):
    def body(a_ref, gate_ref, w_ref, scale_ref, dya_ref, dz_ref, dw_ref, dscale_ref, da_tiles, dgate_tiles, sems):
        g = pl.program_id(0)
        (da_ref, dgate_ref), full = _write_behind(
            g, 4, [da_tiles, dgate_tiles], sems, lambda t, at: _columns(dz_ref, t * HALF + at * POOL_CH, POOL_CH))
        row, width, count = _pool_count(g)
        a = a_ref[...]
        pooled = (_trailing_sum(a, row, width) / count - a).astype(bf16)
        w = w_ref[...]
        scale = scale_ref[...]
        mixed = _dot(pooled, w)
        silu, dsilu = _silu_and_grad(gate_ref[...])
        dya = dya_ref[...]
        dgate_ref[...] = (dya * mixed * scale * dsilu).astype(bf16)
        dms = dya * silu
        dscale_ref[...] = jnp.sum(dms * mixed, axis=0, keepdims=True)
        dmixed = (dms * scale).astype(bf16)
        dw_ref[...] = _dot_tn(pooled, dmixed)
        dpooled = _dot_nt(dmixed, w)
        da_ref[...] = (_leading_sum(dpooled / count, row, width) - dpooled).astype(bf16)
        full()

    a_in, a_gate, w, scale = _pool_specs()
    col = pl.BlockSpec((SEQ, POOL_CH), lambda g: (0, g))
    tiles = pltpu.VMEM((2, SEQ, POOL_CH), bf16)
    return pl.pallas_call(
        body, name="pool_bwd", grid=(4,), in_specs=[a_in, a_gate, w, scale, col],
        out_specs=[ANY, w, scale],
        out_shape=[SDS((SEQ, 6 * D_MODEL), bf16), SDS((4, POOL_CH, POOL_CH), f32), SDS((1, HALF), f32)],
        scratch_shapes=[tiles, tiles, pltpu.SemaphoreType.DMA((4,))],
    )(z0, z0, pool_w, pool_scale, dcat)


Q_COL, K_COL, V_COL, BGATE_COL = 16, 40, 64, 88


def _rope_tables():
    pos = jnp.arange(SEQ, dtype=f32)
    inv_freq = jnp.power(ROPE_THETA, -jnp.arange(0, 2 * ROT_HALF, 2, dtype=f32) / (2 * ROT_HALF))
    ang = pos[:, None] * inv_freq[None, :]
    cos, sin = jnp.cos(ang), jnp.sin(ang)
    zeros = jnp.zeros((SEQ, HEAD_DIM - 2 * ROT_HALF), f32)
    cos_t = jnp.concatenate([cos, cos, zeros + 1.0], axis=1)
    sin_t = jnp.concatenate([sin, sin, zeros], axis=1)
    j = jnp.arange(HEAD_DIM)[:, None]
    i = jnp.arange(HEAD_DIM)[None, :]
    rot = jnp.where((i < ROT_HALF) & (j == i + ROT_HALF), -1.0, 0.0) + jnp.where(
        (i >= ROT_HALF) & (i < 2 * ROT_HALF) & (j == i - ROT_HALF), 1.0, 0.0)
    return cos_t, sin_t, rot.astype(bf16), rot.T.astype(bf16)


def _exact_dot(t, m):
    hi = t.astype(bf16)
    lo = (t - hi.astype(f32)).astype(bf16)
    return _dot(hi, m) + _dot(lo, m)


def _rope(t, cos_t, sin_t, rot):
    return t * cos_t + _exact_dot(t, rot) * sin_t


def _rope_transposed(d, cos_t, sin_t, rot_t):
    return d * cos_t + _exact_dot(d * sin_t, rot_t)


ROW_CHUNK = 256
BLOCKS_TOGETHER = 16


def _chunks(fn):
    for start in range(0, SEQ, ROW_CHUNK):
        fn(start)


def _pieces(dilation):
    length = SEQ // dilation
    n = min(length, ROW_CHUNK)
    return [(r, l0, n) for r in range(dilation) for l0 in range(0, length, n)]


def _by_residue(dst_ref, src_ref, dilation, dtype):
    length = SEQ // dilation
    for r, l0, n in _pieces(dilation):
        src = src_ref[l0:l0 + n, :] if dilation == 1 else src_ref[pl.ds(r + dilation * l0, n, stride=dilation), :]
        start = r * length + l0
        dst_ref[start:start + n, :] = src.astype(dtype)


def _by_position(dst_ref, src_ref, dilation):
    length = SEQ // dilation
    for r, l0, n in _pieces(dilation):
        src = src_ref[r * length + l0:r * length + l0 + n, :]
        if dilation == 1:
            dst_ref[l0:l0 + n, :] = src
        else:
            dst_ref[pl.ds(r + dilation * l0, n, stride=dilation), :] = src


def _attn_masks():
    qi = lax.broadcasted_iota(jnp.int32, (SPAN, 2 * SPAN), 0)
    kj = lax.broadcasted_iota(jnp.int32, (SPAN, 2 * SPAN), 1)
    window = ((kj < SPAN) & (kj >= qi)) | ((kj >= SPAN) & (kj - SPAN <= qi))
    own = lax.broadcasted_iota(jnp.int32, (SPAN, SPAN), 1) <= lax.broadcasted_iota(jnp.int32, (SPAN, SPAN), 0)
    return window, own


def _attn_blocks(dilation):
    per_residue = SEQ // dilation // SPAN
    blocks = [(c, c % per_residue != 0) for c in range(SEQ // SPAN)]
    return [blocks[i:i + BLOCKS_TOGETHER] for i in range(0, len(blocks), BLOCKS_TOGETHER)]


def _block_keys(c, has_prev):
    return slice((c - 1) * SPAN if has_prev else c * SPAN, (c + 1) * SPAN)


def _head_spec(col):
    return pl.BlockSpec((SEQ, HEAD_DIM), lambda h: (0, col + h))


def _table_spec():
    return pl.BlockSpec((SEQ, HEAD_DIM), lambda h: (0, 0))


def _attn_fwd(z0, tables, mixed):
    scale = HEAD_DIM ** -0.5

    def body(*refs):
        qkv = refs[0:9]
        bg_ref, cos_ref, sin_ref, rot_ref = refs[9:13]
        yb_ref, att_ref, lse_ref = refs[14:17]
        saved = refs[17:26]
        tmp_q, tmp_k, v_ones, o_res, l_res, o_nat, l_nat = refs[26:33]
        window_mask, own_mask = _attn_masks()
        rot = rot_ref[...]

        @pl.when(pl.program_id(0) == 0)
        def _():
            v_ones[:, HEAD_DIM:] = jnp.ones((SEQ, HEAD_DIM), bf16)

        for g, dilation in enumerate(DILATIONS):
            q_ref, k_ref, v_ref = qkv[3 * g:3 * g + 3]
            qd, kd, vd = saved[3 * g:3 * g + 3]

            def rope_rows(start, q_ref=q_ref, k_ref=k_ref):
                r = pl.ds(start, ROW_CHUNK)
                cos_t, sin_t = cos_ref[r, :], sin_ref[r, :]
                tmp_q[r, :] = _rope(q_ref[r, :], cos_t, sin_t, rot) * scale
                tmp_k[r, :] = _rope(k_ref[r, :], cos_t, sin_t, rot)

            _chunks(rope_rows)
            _by_residue(qd, tmp_q, dilation, bf16)
            _by_residue(kd, tmp_k, dilation, bf16)
            _by_residue(vd, v_ref, dilation, bf16)
            for l0 in range(0, SEQ, ROW_CHUNK):
                v_ones[l0:l0 + ROW_CHUNK, 0:HEAD_DIM] = vd[l0:l0 + ROW_CHUNK, :]

            for group in _attn_blocks(dilation):
                scores = [_dot_nt(qd[c * SPAN:(c + 1) * SPAN, :], kd[_block_keys(c, prev), :]) for c, prev in group]
                tops, probs = [], []
                for (c, prev), s in zip(group, scores):
                    s = jnp.where(window_mask if prev else own_mask, s, NEG)
                    tops.append(jnp.max(s, axis=1, keepdims=True))
                    probs.append(jnp.exp(s - tops[-1]).astype(bf16))
                sums = [_dot(p, v_ones[_block_keys(c, prev), :]) for (c, prev), p in zip(group, probs)]
                for (c, prev), m, o in zip(group, tops, sums):
                    den = o[:, HEAD_DIM:]
                    o_res[c * SPAN:(c + 1) * SPAN, :] = o[:, :HEAD_DIM] / den
                    l_res[c * SPAN:(c + 1) * SPAN, :] = m + jnp.log(den)

            if dilation > 1:
                _by_position(o_nat, o_res, dilation)
                _by_position(l_nat, l_res, dilation)
            o_g, l_g = (o_res, l_res) if dilation == 1 else (o_nat, l_nat)

            def merge(start, g=g, o_g=o_g, l_g=l_g):
                r = pl.ds(start, ROW_CHUNK)
                if g == 0:
                    att, total = o_g[r, :], l_g[r, :]
                else:
                    l_old, l_new = lse_ref[r, :], l_g[r, :]
                    top = jnp.maximum(l_old, l_new)
                    total = top + jnp.log(jnp.exp(l_old - top) + jnp.exp(l_new - top))
                    att = att_ref[r, :] * jnp.exp(l_old - total) + o_g[r, :] * jnp.exp(l_new - total)
                att_ref[r, :] = att
                lse_ref[r, :] = total
                if g == len(DILATIONS) - 1:
                    gate = bg_ref[r, :]
                    yb_ref[r, :] = (att * gate * _sigmoid(gate)).astype(bf16)

            _chunks(merge)

    in_specs = []
    for g in range(3):
        in_specs += [_head_spec(Q_COL + 8 * g), _head_spec(K_COL + 8 * g), _head_spec(V_COL + 8 * g)]
    in_specs += [_head_spec(BGATE_COL), _table_spec(), _table_spec(), pl.BlockSpec((HEAD_DIM, HEAD_DIM), lambda h: (0, 0)), ANY]
    out_spec = pl.BlockSpec((SEQ, HEAD_DIM), lambda h: (0, h))
    right_half = pl.BlockSpec((SEQ, HEAD_DIM), lambda h: (0, N_HEADS + h))
    vm = lambda dt: pltpu.VMEM((SEQ, HEAD_DIM), dt)
    cos_t, sin_t, rot, _ = tables
    out = pl.pallas_call(
        body, name="attn_fwd", grid=(N_HEADS,), in_specs=in_specs, out_specs=[right_half] + [out_spec] * 11,
        out_shape=[SDS((SEQ, 2 * HALF), bf16), SDS((SEQ, HALF), f32), SDS((SEQ, HALF), f32)] + [SDS((SEQ, HALF), bf16)] * 9,
        scratch_shapes=[vm(f32), vm(f32), pltpu.VMEM((SEQ, 2 * HEAD_DIM), bf16), vm(f32), vm(f32), vm(f32), vm(f32)],
        input_output_aliases={13: 0},
    )(*([z0] * 10), cos_t, sin_t, rot, mixed)
    return out[0], out[1], out[2], [tuple(out[3 + 3 * g:6 + 3 * g]) for g in range(3)]


def _attn_bwd_group(g, saved, z0, att, lse, dcat, tables, dz):
    scale = HEAD_DIM ** -0.5
    dilation = DILATIONS[g]
    with_gate = g == 0
    n_out = 4 if with_gate else 3
    first_col = (Q_COL + 8 * g, K_COL + 8 * g, V_COL + 8 * g, BGATE_COL)

    def body(*refs):
        qd, kd, vd, bg_ref, att_ref, lse_ref, dyb_ref, cos_ref, sin_ref, rot_t_ref = refs[0:10]
        dz_ref = refs[11]
        dod, ld, dd, tmp, aq, ak, av = refs[12:19]
        views, full = _write_behind(pl.program_id(0), N_HEADS, refs[19:19 + n_out], refs[19 + n_out],
                                    lambda t, at: _columns(dz_ref, (first_col[t] + at) * HEAD_DIM, HEAD_DIM))
        dq_ref, dk_ref, dv_ref = views[0:3]
        window_mask, own_mask = _attn_masks()
        rot_t = rot_t_ref[...]

        def gate_rows(start):
            r = pl.ds(start, ROW_CHUNK)
            silu, dsilu = _silu_and_grad(bg_ref[r, :])
            att_v = att_ref[r, :]
            dyb = dyb_ref[r, :]
            if with_gate:
                views[3][r, :] = (dyb * att_v * dsilu).astype(bf16)
            datt = dyb * silu
            tmp[r, :] = datt
            aq[r, :] = jnp.broadcast_to(jnp.sum(datt * att_v, axis=1, keepdims=True), (ROW_CHUNK, HEAD_DIM))

        _chunks(gate_rows)
        _by_residue(dod, tmp, dilation, bf16)
        _by_residue(dd, aq, dilation, f32)
        _by_residue(ld, lse_ref, dilation, f32)

        for group in _attn_blocks(dilation):
            rows = [slice(c * SPAN, (c + 1) * SPAN) for c, _ in group]
            keys = [_block_keys(c, prev) for c, prev in group]
            scores = [_dot_nt(qd[r, :], kd[k, :]) for r, k in zip(rows, keys)]
            dprobs = [_dot_nt(dod[r, :], vd[k, :]) for r, k in zip(rows, keys)]
            probs, dscores = [], []
            for (c, prev), r, s, dp in zip(group, rows, scores, dprobs):
                lse_q, delta = ld[r, :], dd[r, :]
                if prev:
                    lse_q = jnp.concatenate([lse_q, lse_q], axis=1)
                    delta = jnp.concatenate([delta, delta], axis=1)
                p = jnp.where(window_mask if prev else own_mask, jnp.exp(s - lse_q), 0.0)
                probs.append(p.astype(bf16))
                dscores.append((p * (dp - delta)).astype(bf16))
            dvs = [_dot_tn(p, dod[r, :]) for p, r in zip(probs, rows)]
            dks = [_dot_tn(ds, qd[r, :]) for ds, r in zip(dscores, rows)]
            dqs = [_dot(ds, kd[k, :]) for ds, k in zip(dscores, keys)]
            for (c, prev), r, dv, dk, dq in zip(group, rows, dvs, dks, dqs):
                aq[r, :] = dq
                if prev:
                    before = slice((c - 1) * SPAN, c * SPAN)
                    av[before, :] += dv[0:SPAN]
                    ak[before, :] += dk[0:SPAN]
                    av[r, :] = dv[SPAN:]
                    ak[r, :] = dk[SPAN:]
                else:
                    av[r, :] = dv
                    ak[r, :] = dk

        def finish(out_ref, acc, factor, roped):
            if dilation > 1:
                _by_position(tmp, acc, dilation)
            src = acc if dilation == 1 else tmp

            def rows(start):
                r = pl.ds(start, ROW_CHUNK)
                d = src[r, :]
                if factor != 1.0:
                    d = d * factor
                if roped:
                    d = _rope_transposed(d, cos_ref[r, :], sin_ref[r, :], rot_t)
                out_ref[r, :] = d.astype(bf16)

            _chunks(rows)

        finish(dq_ref, aq, scale, True)
        finish(dk_ref, ak, 1.0, True)
        finish(dv_ref, av, 1.0, False)
        full()

    head = pl.BlockSpec((SEQ, HEAD_DIM), lambda h: (0, h))
    in_specs = [head, head, head, _head_spec(BGATE_COL), head, head, _head_spec(8), _table_spec(), _table_spec(),
                pl.BlockSpec((HEAD_DIM, HEAD_DIM), lambda h: (0, 0)), ANY]
    vm = lambda dt: pltpu.VMEM((SEQ, HEAD_DIM), dt)
    cos_t, sin_t, _, rot_t = tables
    return pl.pallas_call(
        body, name=f"attn_bwd_g{g}", grid=(N_HEADS,), in_specs=in_specs, out_specs=ANY,
        out_shape=SDS(dz.shape, dz.dtype), input_output_aliases={10: 0},
        scratch_shapes=[vm(bf16), vm(f32), vm(f32), vm(f32), vm(f32), vm(f32), vm(f32)]
        + [pltpu.VMEM((2, SEQ, HEAD_DIM), bf16)] * n_out + [pltpu.SemaphoreType.DMA((2 * n_out,))],
    )(*saved, z0, att, lse, dcat, cos_t, sin_t, rot_t, dz)


def _sgu_specs():
    chunk = lambda col: pl.BlockSpec((CHUNK, HALF), lambda n: (n, col))
    vec = pl.BlockSpec((1, HALF), lambda n: (0, 0))
    w = pl.BlockSpec((4, CHUNK, CHUNK), lambda n: (0, 0, 0))
    bias = pl.BlockSpec((CHUNK, CHUNK), lambda n: (0, 0))
    return chunk, vec, w, bias


def _sgu_weights(w_ref):
    tril = lax.broadcasted_iota(jnp.int32, (CHUNK, CHUNK), 1) <= lax.broadcasted_iota(jnp.int32, (CHUNK, CHUNK), 0)
    return tril, [jnp.where(tril, w_ref[h], 0.0).astype(bf16) for h in range(4)]


def _sgu_fwd(z1, ln_g, ln_b, sgu_w, bias_t):
    def body(u_ref, v_ref, cg_ref, g_ref, b_ref, w_ref, bias_ref, yc_ref):
        _, ws = _sgu_weights(w_ref)
        xh, _ = _ln_stats(v_ref[...])
        vn = (xh * g_ref[...] + b_ref[...]).astype(bf16)
        for h in range(4):
            cols = slice(h * POOL_CH, (h + 1) * POOL_CH)
            s = _dot(ws[h], vn[:, cols]) + bias_ref[:, h:h + 1]
            gate = cg_ref[:, cols]
            yc_ref[:, cols] = (u_ref[:, cols] * s * gate * _sigmoid(gate)).astype(bf16)

    chunk, vec, w, bias = _sgu_specs()
    return pl.pallas_call(
        body, name="sgu_fwd", grid=(SEQ // CHUNK,),
        in_specs=[chunk(0), chunk(1), chunk(2), vec, vec, w, bias], out_specs=chunk(0),
        out_shape=SDS((SEQ, 2 * HALF), bf16))(z1, z1, z1, ln_g, ln_b, sgu_w, bias_t)


def _sgu_bwd(z1, dcat, ln_g, ln_b, sgu_w, bias_t):
    def body(u_ref, v_ref, cg_ref, dyc_ref, g_ref, b_ref, w_ref, bias_ref,
             dz_ref, dw_ref, dbias_ref, dg_ref, db_ref, dvn_ref, du_tiles, dv_tiles, dcg_tiles, sems):
        n = pl.program_id(0)
        (du_ref, dv_ref, dcg_ref), full = _write_behind(
            n, SEQ // CHUNK, [du_tiles, dv_tiles, dcg_tiles], sems,
            lambda t, at: dz_ref.at[pl.ds(pl.multiple_of(at * CHUNK, CHUNK), CHUNK), t * HALF:(t + 1) * HALF])
        first = n == 0
        tril, ws = _sgu_weights(w_ref)
        xh, rstd = _ln_stats(v_ref[...])
        g = g_ref[...]
        vn = (xh * g + b_ref[...]).astype(bf16)

        @pl.when(first)
        def _():
            dbias_ref[...] = jnp.zeros((CHUNK, CHUNK), f32)

        for h in range(4):
            cols = slice(h * POOL_CH, (h + 1) * POOL_CH)
            vn_h = vn[:, cols]
            s = _dot(ws[h], vn_h) + bias_ref[:, h:h + 1]
            silu, dsilu = _silu_and_grad(cg_ref[:, cols])
            dyc = dyc_ref[:, cols]
            u = u_ref[:, cols]
            du_ref[:, cols] = (dyc * s * silu).astype(bf16)
            dcg_ref[:, cols] = (dyc * u * s * dsilu).astype(bf16)
            ds = dyc * u * silu
            dbias_ref[:, h:h + 1] += jnp.sum(ds, axis=1, keepdims=True)
            ds = ds.astype(bf16)
            _accumulate(dw_ref.at[h], jnp.where(tril, _dot_nt(ds, vn_h), 0.0), first)
            dvn_ref[:, cols] = _dot_tn(ws[h], ds)
        dv, dg, db = _ln_bwd(xh, rstd, g, dvn_ref[...])
        dv_ref[...] = dv.astype(bf16)
        _accumulate(dg_ref, dg, first)
        _accumulate(db_ref, db, first)
        full()

    chunk, vec, w, bias = _sgu_specs()
    tiles = pltpu.VMEM((2, CHUNK, HALF), bf16)
    return pl.pallas_call(
        body, name="sgu_bwd", grid=(SEQ // CHUNK,),
        in_specs=[chunk(0), chunk(1), chunk(2), chunk(0), vec, vec, w, bias],
        out_specs=[ANY, w, bias, vec, vec],
        out_shape=[SDS((SEQ, 3 * D_MODEL), bf16), SDS((4, CHUNK, CHUNK), f32), SDS((CHUNK, CHUNK), f32),
                   SDS((1, HALF), f32), SDS((1, HALF), f32)],
        scratch_shapes=[pltpu.VMEM((CHUNK, HALF), f32), tiles, tiles, tiles, pltpu.SemaphoreType.DMA((6,))],
    )(z1, z1, z1, dcat, ln_g, ln_b, sgu_w, bias_t)


CONV_TILE = 128
DVAL_COL, DGLU_COL = 12, 16


def _conv_specs():
    val = pl.BlockSpec((SEQ, POOL_CH), lambda j: (0, DVAL_COL + j))
    glu = pl.BlockSpec((SEQ, POOL_CH), lambda j: (0, DGLU_COL + j))
    w = pl.BlockSpec((CONV_K, POOL_CH), lambda j: (0, j))
    col = pl.BlockSpec((SEQ, POOL_CH), lambda j: (0, j))
    vec = pl.BlockSpec((1, POOL_CH), lambda j: (0, j))
    return val, glu, w, col, vec


def _conv_fwd(z1, conv_w, conv_b):
    def body(val_ref, glu_ref, w_ref, b_ref, out_ref, xpad):
        xpad[0:CONV_PAD, :] = jnp.zeros((CONV_PAD, POOL_CH), f32)
        xpad[CONV_PAD:, :] = val_ref[...] * _sigmoid(glu_ref[...])
        w = w_ref[...]
        bias = b_ref[...]

        def tile(i, carry):
            t0 = pl.multiple_of(i * CONV_TILE, CONV_TILE)
            window = xpad[pl.ds(t0, CONV_TILE + CONV_PAD), :]
            acc = jnp.broadcast_to(bias, (CONV_TILE, POOL_CH))
            for k in range(CONV_K):
                shift = CONV_PAD - (CONV_K - 1) + k
                acc = acc + w[k:k + 1, :] * pltpu.roll(window, CONV_TILE + CONV_PAD - shift, 0)[0:CONV_TILE]
            out_ref[pl.ds(t0, CONV_TILE), :] = acc
            return carry

        lax.fori_loop(0, SEQ // CONV_TILE, tile, 0)

    val, glu, w, col, vec = _conv_specs()
    return pl.pallas_call(
        body, name="conv_fwd", grid=(4,), in_specs=[val, glu, w, vec], out_specs=col,
        out_shape=SDS((SEQ, HALF), f32), scratch_shapes=[pltpu.VMEM((SEQ + CONV_PAD, POOL_CH), f32)],
    )(z1, z1, conv_w, conv_b)


def _conv_bwd(z1, dconv, conv_w, dz):
    def body(val_ref, glu_ref, w_ref, dout_ref, dz_in, dz_ref, dw_ref, db_ref, xpad, dpad, dx_ref, dval_tiles, dglu_tiles, sems):
        j = pl.program_id(0)
        (dval_ref, dglu_ref), full = _write_behind(
            j, 4, [dval_tiles, dglu_tiles], sems,
            lambda t, at: _columns(dz_ref, ((DVAL_COL, DGLU_COL)[t] + at) * POOL_CH, POOL_CH))
        val = val_ref[...]
        sig = _sigmoid(glu_ref[...])
        xpad[0:CONV_PAD, :] = jnp.zeros((CONV_PAD, POOL_CH), f32)
        xpad[CONV_PAD:, :] = val * sig
        dout = dout_ref[...]
        dpad[0:SEQ, :] = dout
        dpad[SEQ:, :] = jnp.zeros((CONV_PAD, POOL_CH), f32)
        db_ref[...] = jnp.sum(dout, axis=0, keepdims=True)
        dw_ref[...] = jnp.zeros((CONV_K, POOL_CH), f32)
        w = w_ref[...]

        def tile(i, carry):
            t0 = pl.multiple_of(i * CONV_TILE, CONV_TILE)
            x_win = xpad[pl.ds(t0, CONV_TILE + CONV_PAD), :]
            d_win = dpad[pl.ds(t0, CONV_TILE + CONV_PAD), :]
            d_own = d_win[0:CONV_TILE]
            acc = jnp.zeros((CONV_TILE, POOL_CH), f32)
            for k in range(CONV_K):
                shift = CONV_PAD - (CONV_K - 1) + k
                x_k = pltpu.roll(x_win, CONV_TILE + CONV_PAD - shift, 0)[0:CONV_TILE]
                dw_ref[k:k + 1, :] += jnp.sum(d_own * x_k, axis=0, keepdims=True)
                back = CONV_K - 1 - k
                d_k = d_own if back == 0 else pltpu.roll(d_win, CONV_TILE + CONV_PAD - back, 0)[0:CONV_TILE]
                acc = acc + w[k:k + 1, :] * d_k
            dx_ref[pl.ds(t0, CONV_TILE), :] = acc
            return carry

        lax.fori_loop(0, SEQ // CONV_TILE, tile, 0)
        dx = dx_ref[...]
        dval_ref[...] = (dx * sig).astype(bf16)
        dglu_ref[...] = (dx * val * sig * (1.0 - sig)).astype(bf16)
        full()

    val, glu, w, col, vec = _conv_specs()
    pad = pltpu.VMEM((SEQ + CONV_PAD, POOL_CH), f32)
    tiles = pltpu.VMEM((2, SEQ, POOL_CH), bf16)
    return pl.pallas_call(
        body, name="conv_bwd", grid=(4,), in_specs=[val, glu, w, col, ANY], out_specs=[ANY, w, vec],
        out_shape=[SDS(dz.shape, dz.dtype), SDS((CONV_K, HALF), f32), SDS((1, HALF), f32)],
        input_output_aliases={4: 0},
        scratch_shapes=[pad, pad, pltpu.VMEM((SEQ, POOL_CH), f32), tiles, tiles, pltpu.SemaphoreType.DMA((4,))],
    )(z1, z1, conv_w, dconv, dz)


DGATE_COL = 5


def _conv_norm_fwd(conv, z1, g, b, mixed):
    def body(c_ref, gate_ref, g_ref, b_ref, mixed_ref, yd_ref):
        xh, _ = _ln_stats(c_ref[...])
        n = xh * g_ref[...] + b_ref[...]
        gate = gate_ref[...]
        yd_ref[...] = (n * _sigmoid(n) * gate * _sigmoid(gate)).astype(bf16)

    return pl.pallas_call(
        body, name="conv_norm_fwd", grid=(SEQ // ROWS,),
        in_specs=[_row_spec(HALF), _row_spec(HALF, DGATE_COL), _vec_spec(HALF), _vec_spec(HALF), ANY],
        out_specs=_row_spec(HALF, 1), out_shape=SDS((SEQ, 2 * HALF), bf16), input_output_aliases={4: 0},
    )(conv, z1, g, b, mixed)


def _conv_norm_bwd(conv, z1, dcat, g, b, dz):
    def body(c_ref, gate_ref, dyd_ref, g_ref, b_ref, dz_ref, dconv_ref, dgate_ref, dg_ref, db_ref):
        first = pl.program_id(0) == 0
        xh, rstd = _ln_stats(c_ref[...])
        g = g_ref[...]
        n_silu, n_dsilu = _silu_and_grad(xh * g + b_ref[...])
        gate_silu, gate_dsilu = _silu_and_grad(gate_ref[...])
        dyd = dyd_ref[...]
        dgate_ref[...] = (dyd * n_silu * gate_dsilu).astype(bf16)
        dconv, dg, db = _ln_bwd(xh, rstd, g, dyd * gate_silu * n_dsilu)
        dconv_ref[...] = dconv
        _accumulate(dg_ref, dg, first)
        _accumulate(db_ref, db, first)

    return pl.pallas_call(
        body, name="conv_norm_bwd", grid=(SEQ // ROWS,),
        in_specs=[_row_spec(HALF), _row_spec(HALF, DGATE_COL), _row_spec(HALF, 1), _vec_spec(HALF), _vec_spec(HALF), ANY],
        out_specs=[_row_spec(HALF), _row_spec(HALF, DGATE_COL), _vec_spec(HALF), _vec_spec(HALF)],
        out_shape=[SDS((SEQ, HALF), f32), SDS(dz.shape, dz.dtype), SDS((1, HALF), f32), SDS((1, HALF), f32)],
        input_output_aliases={5: 1},
    )(conv, z1, dcat, g, b, dz)


def _step(x, target, w, chip):
    chip_vec = chip.astype(jnp.int32).reshape(1)
    sharded_names = list(SHARDED_SMALL)
    first = [_cast_into_slot(w["e_w_in"], chip_vec, "cast_e_w_in0", w["e_pre_norm"], 0, E_IN_PIECES)]
    sems, bufs, token = _gather_start(first, "gather_start_first")
    small_shard = _pack([w[k] for k in sharded_names], total_rows=SMALL_SHARD_ROWS) + 0.0 * token[0, 0]
    small_slot = lax.dynamic_update_slice(jnp.zeros((N_CHIPS, SMALL_SHARD_ROWS, LANES), f32), small_shard[None], (chip, 0, 0))
    more = [small_slot]
    more += [_cast_into_slot(w["e_w_in"], chip_vec, f"cast_e_w_in{i}", token, i, E_IN_PIECES) for i in range(1, E_IN_PIECES)]
    more_sems, more_bufs, token = _gather_start(more, "gather_start_pieces")
    rest = [_cast_into_slot(w[k], chip_vec, f"cast_{k}", token) for k in BIG[1:]]
    rest_sems, rest_bufs, token = _gather_start(rest, "gather_start_rest")
    sems, bufs = sems + more_sems + rest_sems, bufs + more_bufs + rest_bufs
    tables = _rope_tables()

    def vec(k):
        return w[k].reshape(1, -1)

    h0 = _pre_norm(x, vec("e_pre_norm") + token[0, 0])
    after, z0, e_w_in = h0, None, []
    for i in range(E_IN_PIECES):
        group = slice(0, 1) if i == 0 else slice(1, 3) if i == 1 else slice(i + 1, i + 2)
        landed = _forward_halves(_gather_wait(bufs[group], sems[group], after, f"gather_wait_{i}"), f"forward_{i}")
        if i == 1:
            small_full = landed[0]
        e_w_in.append(landed[-1])
        z0 = _mm_nn(h0, landed[-1], f32, f"e_in{i}", i, E_IN_PIECES, z0)
        after = z0
    p = {k: _from_chips(k, a) for k, a in zip(sharded_names, _unpack(small_full, [SHARDED_SMALL[k][0] for k in sharded_names]))}
    for k in ("o_pre_norm", "o_sgu_norm_g", "o_sgu_norm_b", "o_conv_b", "o_conv_norm_g", "o_conv_norm_b", "o_post_norm"):
        p[k] = p[k].reshape(1, -1)
    pool_w_bf = p["e_pool_w"].astype(bf16)
    bias_t = jnp.pad(w["o_sgu_b"].T, ((0, 0), (0, CHUNK - 4)))

    cat0, att, lse, qkv_by_residue = _attn_fwd(z0, tables, _pool_fwd(z0, pool_w_bf, vec("e_pool_scale")))

    def arrived(index, after, name):
        one = slice(index, index + 1)
        return _forward_halves(_gather_wait(bufs[one], sems[one], after, f"gather_wait_{name}"), f"forward_{name}")[0]

    e_w_out = arrived(1 + E_IN_PIECES, att, "e_w_out").reshape(1, D_MODEL, D_MODEL)
    y0 = _mm_nn(cat0, e_w_out, f32, "e_out")
    x1, h1 = _mid_norm(x, y0, vec("e_post_norm"), p["o_pre_norm"])
    o_w_in = arrived(2 + E_IN_PIECES, h1, "o_w_in")
    z1 = _mm_nn(h1, o_w_in, f32, "o_in")
    yc = _sgu_fwd(z1, p["o_sgu_norm_g"], p["o_sgu_norm_b"], w["o_sgu_w"], bias_t)
    conv = _conv_fwd(z1, p["o_conv_w"], p["o_conv_b"])
    cat1 = _conv_norm_fwd(conv, z1, p["o_conv_norm_g"], p["o_conv_norm_b"], yc)
    o_w_out = arrived(3 + E_IN_PIECES, cat1, "o_w_out").reshape(1, D_MODEL, D_MODEL)
    y1 = _mm_nn(cat1, o_w_out, f32, "o_out")
    loss, dx2, dy1, g_o_post = _final_norm_loss(x1, y1, p["o_post_norm"], target)

    in_flight = {}

    def send_off(name, grad):
        sem, sums, land, tok = _scatter_start(_swap_add(grad, f"swap_add_{name}"), f"scatter_start_{name}")
        in_flight[name] = (sem, sums, land)
        return tok

    tok = send_off("o_w_out", _mm_tn(cat1, dy1, 1, "o_out_dw").reshape(N_CHIPS, HALF // 2, D_MODEL))
    dcat1 = _mm_nt(dy1, [o_w_out], "o_out_dx", tok)
    dz1, g_sgu_w, g_bias_t, g_sgu_g, g_sgu_b = _sgu_bwd(
        z1, dcat1, p["o_sgu_norm_g"] + tok[0, 0], p["o_sgu_norm_b"], w["o_sgu_w"], bias_t)
    dconv, dz1, g_cn_g, g_cn_b = _conv_norm_bwd(conv, z1, dcat1, p["o_conv_norm_g"], p["o_conv_norm_b"], dz1)
    dz1, g_conv_w, g_conv_b = _conv_bwd(z1, dconv, p["o_conv_w"], dz1)
    tok = send_off("o_w_in", _mm_tn(h1, dz1, N_CHIPS, "o_in_dw"))
    dh1 = _mm_nt(dz1, [o_w_in], "o_in_dx", tok)
    dx1, dy0, g_o_pre, g_e_post = _mid_norm_bwd(dx2, dh1, x1, y0, p["o_pre_norm"] + tok[0, 0], vec("e_post_norm"))

    tok = send_off("e_w_out", _mm_tn(cat0, dy0, 1, "e_out_dw").reshape(N_CHIPS, HALF // 2, D_MODEL))
    dcat0 = _mm_nt(dy0, [e_w_out], "e_out_dx", tok)
    dz0, g_pool_w, g_pool_scale = _pool_bwd(z0, dcat0, pool_w_bf, vec("e_pool_scale") + tok[0, 0])
    for g in range(len(DILATIONS)):
        dz0 = _attn_bwd_group(g, qkv_by_residue[g], z0, att, lse, dcat0, tables, dz0)
    tok = send_off("e_w_in", _mm_tn(h0, dz0, N_CHIPS, "e_in_dw"))
    dh0 = _mm_nt(dz0, e_w_in, "e_in_dx", tok)
    grad_x, g_e_pre = _pre_norm_bwd(dx1, dh0, x, vec("e_pre_norm") + tok[0, 0])

    small = {"e_pre_norm": g_e_pre, "e_pool_w": g_pool_w, "e_pool_scale": g_pool_scale, "e_post_norm": g_e_post,
             "o_pre_norm": g_o_pre, "o_sgu_norm_g": g_sgu_g, "o_sgu_norm_b": g_sgu_b, "o_sgu_w": g_sgu_w,
             "o_sgu_b": g_bias_t, "o_conv_w": g_conv_w, "o_conv_b": g_conv_b,
             "o_conv_norm_g": g_cn_g, "o_conv_norm_b": g_cn_b, "o_post_norm": g_o_post}
    return loss, grad_x, in_flight, small


def _land(in_flight, name, chip, after):
    sems, sums, land = in_flight[name]
    sums, land = _scatter_wait(sems, sums, land, after, f"scatter_wait_{name}")
    return _add_landed_join(sums, land, chip.astype(jnp.int32).reshape(1), f"add_landed_{name}")


def _place():
    x, y, c = lax.axis_index("x"), lax.axis_index("y"), lax.axis_index("c")
    others = [(1 - x, y), (x, 1 - y), (1 - x, 1 - y)]
    return x, y, c, 2 * x + y, others


SWAP_ROWS = 256
FORWARD_STAGE_BYTES = 4 << 20


def _swap_add(g, name):
    chips, r, c = g.shape
    half = r // 2
    rows_per_step = 2 * SWAP_ROWS if half % (2 * SWAP_ROWS) == 0 else SWAP_ROWS
    nb = half // rows_per_step
    steps = chips * nb

    def body(core_ref, mine_ref, theirs_ref, out_ref, landing, send_sems, recv_sems, free_sems):
        i = pl.program_id(0)
        x, y, core, _, _ = _place()
        sibling = (x, y, 1 - core)

        def send(slot):
            return pltpu.make_async_remote_copy(src_ref=theirs_ref, dst_ref=landing.at[slot], send_sem=send_sems.at[slot],
                                                recv_sem=recv_sems.at[slot], device_id=sibling, device_id_type=MESH)

        @pl.when(i < steps)
        def _():
            @pl.when(i >= 2)
            def _():
                pl.semaphore_wait(free_sems.at[i % 2], 1)

            send(i % 2).start()

        @pl.when(i >= 1)
        def _():
            landed = (i - 1) % 2
            send(landed).wait_recv()
            out_ref[...] = (mine_ref[...].astype(f32) + landing[landed].astype(f32)).astype(out_ref.dtype)

            @pl.when(i + 1 < steps)
            def _():
                pl.semaphore_signal(free_sems.at[landed], 1, device_id=sibling, device_id_type=MESH)

        @pl.when(i < steps)
        def _():
            send(i % 2).wait_send()

    def rows_of(b, h):
        return (2 * (b // nb) + h) * nb + b % nb

    block = (rows_per_step, c)
    grid_spec = pltpu.PrefetchScalarGridSpec(
        num_scalar_prefetch=1, grid=(steps + 1,),
        in_specs=[pl.BlockSpec(block, lambda i, core: (rows_of(jnp.maximum(i - 1, 0), core[0]), 0)),
                  pl.BlockSpec(block, lambda i, core: (rows_of(jnp.minimum(i, steps - 1), 1 - core[0]), 0))],
        out_specs=pl.BlockSpec(block, lambda i, core: (jnp.maximum(i - 1, 0), 0)),
        scratch_shapes=[pltpu.VMEM((2, rows_per_step, c), g.dtype), pltpu.SemaphoreType.DMA((2,)),
                        pltpu.SemaphoreType.DMA((2,)), pltpu.SemaphoreType.REGULAR((2,))])
    core = lax.axis_index("c").astype(jnp.int32).reshape(1)
    rows = g.reshape(chips * r, c)
    out = pl.pallas_call(body, name=name, grid_spec=grid_spec, out_shape=SDS((chips * half, c), g.dtype))(core, rows, rows)
    return out.reshape(chips, half, c)


HBM = pl.BlockSpec(memory_space=pltpu.HBM)
SEM = pl.BlockSpec(memory_space=pltpu.SEMAPHORE)
EFFECT = pltpu.SideEffectType.DATAFLOW_SIDE_EFFECTING


def _in_hbm(a):
    return pltpu.with_memory_space_constraint(a, pltpu.HBM)


def _cast_into_slot(w, chip, name, after, piece=0, pieces=1):
    r, c = w.shape
    c = c // pieces
    nb = r // SWAP_ROWS

    def body(chip_ref, w_ref, after_ref, o_ref):
        o_ref[...] = w_ref[...].astype(bf16)

    grid_spec = pltpu.PrefetchScalarGridSpec(
        num_scalar_prefetch=1, grid=(nb,),
        in_specs=[pl.BlockSpec((SWAP_ROWS, c), lambda i, chip: (i, piece)), ANY],
        out_specs=pl.BlockSpec((SWAP_ROWS, c), lambda i, chip: (chip[0] * nb + i, 0)))
    out = pl.pallas_call(body, name=name, grid_spec=grid_spec, out_shape=SDS((N_CHIPS * r, c), bf16))(chip, w, after)
    return out.reshape(N_CHIPS, r, c)


def _gather_start(bufs, name):
    n = len(bufs)

    def body(*refs):
        ins, sems, token = refs[:n], refs[n:3 * n], refs[4 * n]
        x, y, c, me, others = _place()
        for a in range(n):
            rows = ins[a].shape[1] // 2
            mine = ins[a].at[me, pl.ds(c * rows, rows), :]
            for k, (ox, oy) in enumerate(others):
                pltpu.make_async_remote_copy(src_ref=mine, dst_ref=mine, send_sem=sems[2 * a].at[k],
                                             recv_sem=sems[2 * a + 1].at[k], device_id=(ox, oy, c),
                                             device_id_type=MESH).start()
        token[...] = jnp.zeros_like(token)

    out = pl.pallas_call(
        body, name=name, in_specs=[HBM] * n,
        out_shape=(*[pltpu.SemaphoreType.DMA((3,))] * (2 * n), *[pltpu.HBM(b.shape, b.dtype) for b in bufs],
                   SDS((8, 128), f32)),
        out_specs=(*[SEM] * (2 * n), *[HBM] * n, pl.BlockSpec(memory_space=pltpu.VMEM)),
        input_output_aliases={a: 2 * n + a for a in range(n)},
        compiler_params=pltpu.CompilerParams(has_side_effects=EFFECT),
    )(*[_in_hbm(b) for b in bufs])
    return [(out[2 * a], out[2 * a + 1]) for a in range(n)], list(out[2 * n:3 * n]), out[3 * n]


def _gather_wait(bufs, sems, after, name):
    n = len(bufs)

    def body(*refs):
        ins, sem_refs = refs[:n], refs[n:3 * n]
        x, y, c, me, others = _place()
        for a in range(n):
            rows = ins[a].shape[1] // 2
            mine = ins[a].at[me, pl.ds(c * rows, rows), :]
            for k, (ox, oy) in enumerate(others):
                landed = ins[a].at[2 * ox + oy, pl.ds(c * rows, rows), :]
                copy = pltpu.make_async_remote_copy(src_ref=mine, dst_ref=landed, send_sem=sem_refs[2 * a].at[k],
                                                    recv_sem=sem_refs[2 * a + 1].at[k], device_id=(ox, oy, c),
                                                    device_id_type=MESH)
                copy.wait_send()
                copy.wait_recv()

    flat_sems = [s for pair in sems for s in pair]
    out = pl.pallas_call(
        body, name=name, in_specs=[HBM] * n + [SEM] * (2 * n) + [ANY],
        out_shape=tuple(pltpu.HBM(b.shape, b.dtype) for b in bufs), out_specs=tuple([HBM] * n),
        input_output_aliases={a: a for a in range(n)},
        compiler_params=pltpu.CompilerParams(has_side_effects=EFFECT),
    )(*bufs, *flat_sems, after)
    return list(out)


def _forward_halves(bufs, name):
    n = len(bufs)
    blocks = []
    for b in bufs:
        half = b.shape[1] // 2
        whole = half * b.shape[2] * b.dtype.itemsize <= FORWARD_STAGE_BYTES
        blocks.append((half, half if whole or half % SWAP_ROWS else SWAP_ROWS))
    work = [(a, k, b) for a in range(n) for k in range(3) for b in range(blocks[a][0] // blocks[a][1])]

    def body(*refs):
        outs, stages = refs[n:2 * n], refs[2 * n:3 * n]
        load_sems, send_sems, recv_sems = refs[3 * n:]
        x, y, c, me, others = _place()
        sibling = (x, y, 1 - c)

        def rows(item):
            a, k, b = item
            half, tr = blocks[a]
            ox, oy = others[k]
            return outs[a].at[2 * ox + oy, pl.ds(c * half + b * tr, tr), :]

        def load(s, item):
            return pltpu.make_async_copy(rows(item), stages[item[0]].at[s], load_sems.at[s])

        def send(s, item):
            return pltpu.make_async_remote_copy(src_ref=stages[item[0]].at[s], dst_ref=rows(item), send_sem=send_sems.at[s],
                                                recv_sem=recv_sems.at[item[0]], device_id=sibling, device_id_type=MESH)

        load(0, work[0]).start()
        for t, item in enumerate(work):
            s = t % 2
            load(s, item).wait()
            send(s, item).start()
            if t + 1 < len(work):
                if t >= 1:
                    send(1 - s, work[t - 1]).wait_send()
                load(1 - s, work[t + 1]).start()
        if len(work) > 1:
            send(len(work) % 2, work[-2]).wait_send()
        send((len(work) - 1) % 2, work[-1]).wait_send()
        for a in range(n):
            theirs = outs[a].at[pl.ds(0, 3), pl.ds(0, blocks[a][0]), :]
            pltpu.make_async_remote_copy(src_ref=theirs, dst_ref=theirs, send_sem=send_sems.at[0], recv_sem=recv_sems.at[a],
                                         device_id=sibling, device_id_type=MESH).wait_recv()

    out = pl.pallas_call(
        body, name=name, in_specs=[ANY] * n, out_specs=[ANY] * n, out_shape=[SDS(b.shape, b.dtype) for b in bufs],
        input_output_aliases={a: a for a in range(n)},
        scratch_shapes=[pltpu.VMEM((2, blocks[a][1], bufs[a].shape[2]), bufs[a].dtype) for a in range(n)]
        + [pltpu.SemaphoreType.DMA((2,)), pltpu.SemaphoreType.DMA((2,)), pltpu.SemaphoreType.DMA((n,))],
    )(*bufs)
    return list(out)


def _scatter_start(chip_sums, name):
    def body(a_ref, land_ref, send_sems, recv_sems, a_thru, land_thru, token):
        x, y, c, me, others = _place()
        for k, (ox, oy) in enumerate(others):
            pltpu.make_async_remote_copy(src_ref=a_ref.at[2 * ox + oy], dst_ref=land_ref.at[me], send_sem=send_sems.at[k],
                                         recv_sem=recv_sems.at[k], device_id=(ox, oy, c), device_id_type=MESH).start()
        token[...] = jnp.zeros_like(token)

    shape = pltpu.HBM(chip_sums.shape, chip_sums.dtype)
    send, recv, a_thru, land, token = pl.pallas_call(
        body, name=name, in_specs=[HBM, HBM],
        out_shape=(pltpu.SemaphoreType.DMA((3,)), pltpu.SemaphoreType.DMA((3,)), shape, shape, SDS((8, 128), f32)),
        out_specs=(SEM, SEM, HBM, HBM, pl.BlockSpec(memory_space=pltpu.VMEM)), input_output_aliases={0: 2, 1: 3},
        compiler_params=pltpu.CompilerParams(has_side_effects=EFFECT),
    )(_in_hbm(chip_sums), _in_hbm(lax.empty(chip_sums.shape, chip_sums.dtype)))
    return (send, recv), a_thru, land, token


def _scatter_wait(sems, chip_sums, land, after, name):
    def body(a_ref, land_ref, send_sems, recv_sems, after_ref, a_out, land_out):
        x, y, c, me, others = _place()
        for k, (ox, oy) in enumerate(others):
            copy = pltpu.make_async_remote_copy(
                src_ref=a_ref.at[2 * ox + oy], dst_ref=land_ref.at[2 * ox + oy], send_sem=send_sems.at[k],
                recv_sem=recv_sems.at[k], device_id=(ox, oy, c), device_id_type=MESH)
            copy.wait_send()
            copy.wait_recv()

    shape = pltpu.HBM(chip_sums.shape, chip_sums.dtype)
    return pl.pallas_call(
        body, name=name, in_specs=[HBM, HBM, SEM, SEM, ANY], out_shape=(shape, shape), out_specs=(HBM, HBM),
        input_output_aliases={0: 0, 1: 1}, compiler_params=pltpu.CompilerParams(has_side_effects=EFFECT),
    )(chip_sums, land, sems[0], sems[1], after)


def _add_landed_join(chip_sums, land, chip, name):
    chips, rh, c = chip_sums.shape
    nb = rh // SWAP_ROWS

    def body(chip_ref, own_ref, l1_ref, l2_ref, l3_ref, out_hbm, buf, send_sems, recv_sem, local_sems):
        i = pl.program_id(0)
        slot = i % 2
        x, y, core, _, _ = _place()
        sibling = (x, y, 1 - core)

        def copies(s, step):
            rows = pl.ds(pl.multiple_of((core * nb + step) * SWAP_ROWS, SWAP_ROWS), SWAP_ROWS)
            keep = pltpu.make_async_copy(buf.at[s], out_hbm.at[rows, :], local_sems.at[s])
            give = pltpu.make_async_remote_copy(src_ref=buf.at[s], dst_ref=out_hbm.at[rows, :], send_sem=send_sems.at[s],
                                                recv_sem=recv_sem.at[0], device_id=sibling, device_id_type=MESH)
            return keep, give

        def drain(s, step):
            keep, give = copies(s, step)
            keep.wait()
            give.wait_send()

        @pl.when(i >= 2)
        def _():
            drain(slot, i - 2)

        buf[slot] = ((own_ref[...].astype(f32) + l1_ref[...].astype(f32)) + l2_ref[...].astype(f32)) + l3_ref[...].astype(f32)
        keep, give = copies(slot, i)
        keep.start()
        give.start()

        @pl.when(i == nb - 1)
        def _():
            drain(slot, i)
            if nb > 1:
                drain(1 - slot, i - 1)
            theirs = out_hbm.at[pl.ds((1 - core) * rh, rh), :]
            pltpu.make_async_remote_copy(src_ref=theirs, dst_ref=theirs, send_sem=send_sems.at[0], recv_sem=recv_sem.at[0],
                                         device_id=sibling, device_id_type=MESH).wait_recv()

    block = (SWAP_ROWS, c)
    from_slot = lambda d: pl.BlockSpec(block, lambda i, chip: (((chip[0] + d) % chips) * nb + i, 0))
    grid_spec = pltpu.PrefetchScalarGridSpec(
        num_scalar_prefetch=1, grid=(nb,), in_specs=[from_slot(0), from_slot(1), from_slot(2), from_slot(3)],
        out_specs=ANY,
        scratch_shapes=[pltpu.VMEM((2, SWAP_ROWS, c), f32), pltpu.SemaphoreType.DMA((2,)),
                        pltpu.SemaphoreType.DMA((1,)), pltpu.SemaphoreType.DMA((2,))])
    land_rows = land.reshape(chips * rh, c)
    return pl.pallas_call(body, name=name, grid_spec=grid_spec, out_shape=SDS((2 * rh, c), f32))(
        chip, chip_sums.reshape(chips * rh, c), land_rows, land_rows, land_rows)


def _adamw_update(w_ref, g_ref, m_ref, v_ref, d_ref, nm_ref, nv_ref):
    g = g_ref[...]
    nm = ADAM_B1 * m_ref[...] + (1.0 - ADAM_B1) * g
    nv = ADAM_B2 * v_ref[...] + (1.0 - ADAM_B2) * (g * g)
    nm_ref[...] = nm
    nv_ref[...] = nv
    m_hat = nm / (1.0 - ADAM_B1 ** ADAM_STEP)
    v_hat = nv / (1.0 - ADAM_B2 ** ADAM_STEP)
    d_ref[...] = -ADAM_LR * (m_hat / (jnp.sqrt(v_hat) + ADAM_EPS) + ADAM_WD * w_ref[...])


def _adamw(w, g, m, v, name):
    r, c = w.shape
    tr = 128 if r % 128 == 0 else r

    def body(w_ref, g_ref, m_ref, v_ref, g_out_ref, d_ref, nm_ref, nv_ref):
        g_out_ref[...] = g_ref[...]
        _adamw_update(w_ref, g_ref, m_ref, v_ref, d_ref, nm_ref, nv_ref)

    spec = pl.BlockSpec((tr, c), lambda i: (i, 0))
    return pl.pallas_call(body, name=name, grid=(r // tr,), in_specs=[spec] * 4, out_specs=[spec] * 4,
                          out_shape=[SDS((r, c), f32)] * 4)(w, g, m, v)


SMALL_PACKING = {
    "e_pre_norm": ((1, 2048), 8, (1, 2048)), "e_pool_w": ((1024, 256), 1024, (256, 256)),
    "e_pool_scale": ((1, 1024), 8, (1, 1024)), "e_post_norm": ((1, 2048), 8, (1, 2048)),
    "o_pre_norm": ((1, 2048), 8, (1, 512)), "o_sgu_norm_g": ((1, 1024), 8, (1, 256)),
    "o_sgu_norm_b": ((1, 1024), 8, (1, 256)), "o_sgu_w": ((512, 128), 512, (512, 128)),
    "o_sgu_b": ((128, 128), 8, (4, 128)), "o_conv_w": ((31, 1024), 128, (31, 256)), "o_conv_b": ((1, 1024), 8, (1, 256)),
    "o_conv_norm_g": ((1, 1024), 8, (1, 256)), "o_conv_norm_b": ((1, 1024), 8, (1, 256)),
    "o_post_norm": ((1, 2048), 8, (1, 512)),
}
SMALL_PACKED_ROWS = 1792


def _small_finalize(grads, ws, ms, vs, after):
    names = list(SMALL_ORDER)
    n = len(names)
    half, piece = SMALL_PACKED_ROWS // 2, SMALL_PACKED_ROWS // 8
    first_row, row = {}, 0
    for k in names:
        first_row[k] = row
        row += SMALL_PACKING[k][1]

    def body(*refs):
        g_refs, total = refs[0:n], refs[n + 1]
        pack, from_sibling, from_chips, send_a, recv_a, send_b, recv_b, send_c, recv_c, send_d, recv_d = refs[n + 2:]
        x, y, c, me, others = _place()

        for r0 in range(0, SMALL_PACKED_ROWS, piece):
            pack[r0:r0 + piece, :] = jnp.zeros((piece, LANES), f32)
        for k, g_ref in zip(names, g_refs):
            (rows, width), _, _ = SMALL_PACKING[k]
            r0 = first_row[k]
            if k == "o_sgu_b":
                pack[r0:r0 + 4, 0:CHUNK] = g_ref[...].T[0:4, :]
            elif width < LANES:
                pack[r0:r0 + rows, 0:width] = g_ref[...]
            else:
                for j in range(width // LANES):
                    dst = r0 + j * (1 if rows == 1 else 32)
                    pack[dst:dst + rows, :] = g_ref[:, j * LANES:(j + 1) * LANES]

        sibling = (x, y, 1 - c)
        swap = pltpu.make_async_remote_copy(
            src_ref=pack.at[pl.ds(pl.multiple_of((1 - c) * half, 8), half), :], dst_ref=from_sibling,
            send_sem=send_a.at[0], recv_sem=recv_a.at[0], device_id=sibling, device_id_type=MESH)
        swap.start()
        swap.wait()
        for j in range(4):
            rows = pl.ds(pl.multiple_of(c * half + j * piece, 8), piece)
            pack[rows, :] = pack[rows, :] + from_sibling[j * piece:(j + 1) * piece, :]

        def piece_of(chip):
            return pl.ds(pl.multiple_of(c * half + chip * piece, 8), piece)

        def to_chip(k):
            ox, oy = others[k]
            return pltpu.make_async_remote_copy(
                src_ref=pack.at[piece_of(2 * ox + oy), :], dst_ref=from_chips.at[me], send_sem=send_b.at[k],
                recv_sem=recv_b.at[k], device_id=(ox, oy, c), device_id_type=MESH)

        for k in range(3):
            to_chip(k).start()
        from_chips[me] = pack[piece_of(me), :]
        for k, (ox, oy) in enumerate(others):
            landed = from_chips.at[2 * ox + oy]
            pltpu.make_async_remote_copy(src_ref=landed, dst_ref=landed, send_sem=send_b.at[k], recv_sem=recv_b.at[k],
                                         device_id=(ox, oy, c), device_id_type=MESH).wait_recv()
        for k in range(3):
            to_chip(k).wait_send()
        mine = pl.ds(pl.multiple_of(c * half + me * piece, 8), piece)
        total[mine, :] = ((from_chips[0] + from_chips[1]) + from_chips[2]) + from_chips[3]

        def to_same_core(k):
            ox, oy = others[k]
            return pltpu.make_async_remote_copy(
                src_ref=total.at[mine, :], dst_ref=total.at[mine, :], send_sem=send_c.at[k], recv_sem=recv_c.at[k],
                device_id=(ox, oy, c), device_id_type=MESH)

        for k in range(3):
            to_same_core(k).start()
        for k, (ox, oy) in enumerate(others):
            theirs = total.at[piece_of(2 * ox + oy), :]
            pltpu.make_async_remote_copy(src_ref=theirs, dst_ref=theirs, send_sem=send_c.at[k], recv_sem=recv_c.at[k],
                                         device_id=(ox, oy, c), device_id_type=MESH).wait_recv()
        for k in range(3):
            to_same_core(k).wait_send()
        my_half = total.at[pl.ds(pl.multiple_of(c * half, 8), half), :]
        join = pltpu.make_async_remote_copy(src_ref=my_half, dst_ref=my_half, send_sem=send_d.at[0], recv_sem=recv_d.at[0],
                                            device_id=sibling, device_id_type=MESH)
        join.start()
        their_half = total.at[pl.ds(pl.multiple_of((1 - c) * half, 8), half), :]
        pltpu.make_async_remote_copy(src_ref=their_half, dst_ref=their_half, send_sem=send_d.at[0], recv_sem=recv_d.at[0],
                                     device_id=sibling, device_id_type=MESH).wait_recv()
        join.wait_send()

    whole = pl.BlockSpec(memory_space=pltpu.VMEM)
    total = pl.pallas_call(
        body, name="small_allreduce", in_specs=[whole] * n + [ANY], out_specs=whole,
        out_shape=SDS((SMALL_PACKED_ROWS, LANES), f32),
        scratch_shapes=[pltpu.VMEM((SMALL_PACKED_ROWS, LANES), f32), pltpu.VMEM((half, LANES), f32),
                        pltpu.VMEM((N_CHIPS, piece, LANES), f32),
                        pltpu.SemaphoreType.DMA((1,)), pltpu.SemaphoreType.DMA((1,)), pltpu.SemaphoreType.DMA((3,)),
                        pltpu.SemaphoreType.DMA((3,)), pltpu.SemaphoreType.DMA((3,)), pltpu.SemaphoreType.DMA((3,)),
                        pltpu.SemaphoreType.DMA((1,)), pltpu.SemaphoreType.DMA((1,))],
    )(*grads, after)

    def update(*refs):
        total = refs[0]
        w_refs, m_refs, v_refs = refs[1:n + 1], refs[n + 1:2 * n + 1], refs[2 * n + 1:3 * n + 1]
        outs = refs[3 * n + 1:]
        me = 2 * lax.axis_index("x") + lax.axis_index("y")

        def of_chip(candidates):
            value = candidates[0]
            for j in range(1, N_CHIPS):
                value = jnp.where(me == j, candidates[j], value)
            return value

        for i, k in enumerate(names):
            (rows, width), _, (local_rows, local_width) = SMALL_PACKING[k]
            r0 = first_row[k]
            if k == "o_sgu_b":
                g = total[r0:r0 + 4, 0:CHUNK]
            elif k == "e_pool_w":
                for grp in range(4):
                    src = pl.ds(pl.multiple_of(r0 + grp * POOL_CH + me * 64, 8), 64)
                    dst = slice(grp * 64, (grp + 1) * 64)
                    _adamw_rows(total[src, :], i, dst, w_refs, m_refs, v_refs, outs, n)
                continue
            elif k == "o_conv_w":
                g = total[pl.ds(pl.multiple_of(r0 + me * 32, 8), 32), :][0:CONV_K]
            elif width < LANES:
                g = total[r0:r0 + rows, 0:width]
            else:
                lanes = [total[r0 + j:r0 + j + 1, :] for j in range(width // LANES)]
                per_chip = local_width // LANES
                if local_width == width:
                    g = jnp.concatenate(lanes, axis=1)
                elif per_chip == 1:
                    g = of_chip(lanes)
                else:
                    g = of_chip([jnp.concatenate(lanes[j * per_chip:(j + 1) * per_chip], axis=1) for j in range(N_CHIPS)])
            _adamw_rows(g, i, slice(None), w_refs, m_refs, v_refs, outs, n)

    shard_shapes = [SMALL_PACKING[k][2] for k in names]
    out = pl.pallas_call(update, name="small_update", in_specs=[whole] * (3 * n + 1), out_specs=[whole] * (4 * n),
                         out_shape=[SDS(s, f32) for s in shard_shapes] * 4)(total, *ws, *ms, *vs)
    return out[:n], out[n:2 * n], out[2 * n:3 * n], out[3 * n:]


def _adamw_rows(g, i, rows, w_refs, m_refs, v_refs, outs, n):
    w, m, v = w_refs[i][rows, :], m_refs[i][rows, :], v_refs[i][rows, :]
    nm = ADAM_B1 * m + (1.0 - ADAM_B1) * g
    nv = ADAM_B2 * v + (1.0 - ADAM_B2) * (g * g)
    m_hat = nm / (1.0 - ADAM_B1 ** ADAM_STEP)
    v_hat = nv / (1.0 - ADAM_B2 ** ADAM_STEP)
    outs[i][rows, :] = g
    outs[n + i][rows, :] = -ADAM_LR * (m_hat / (jnp.sqrt(v_hat) + ADAM_EPS) + ADAM_WD * w)
    outs[2 * n + i][rows, :] = nm
    outs[3 * n + i][rows, :] = nv


def _pack(arrays, total_rows=None):
    parts = []
    rows = 0
    for a in arrays:
        flat = a.reshape(-1, LANES)
        pad = -flat.shape[0] % 8
        parts.append(jnp.pad(flat, ((0, pad), (0, 0))))
        rows += flat.shape[0] + pad
    if total_rows is not None:
        parts.append(jnp.zeros((total_rows - rows, LANES), arrays[0].dtype))
    return jnp.concatenate(parts, axis=0)


def _unpack(buf, shapes):
    out = []
    row = 0
    lead = buf.shape[:-2]
    for shape in shapes:
        size = 1
        for s in shape:
            size *= s
        rows = size // LANES
        out.append(buf[..., row:row + rows, :].reshape(lead + tuple(shape)))
        row += rows + (-rows % 8)
    return out


BIG = ("e_w_in", "e_w_out", "o_w_in", "o_w_out")
SHARDED_SMALL = {
    "e_pool_w": ((4, 64, 256), 1), "o_pre_norm": ((512,), 0), "o_sgu_norm_g": ((256,), 0), "o_sgu_norm_b": ((256,), 0),
    "o_conv_w": ((31, 256), 1), "o_conv_b": ((256,), 0), "o_conv_norm_g": ((256,), 0), "o_conv_norm_b": ((256,), 0),
    "o_post_norm": ((512,), 0),
}
SMALL_ORDER = ("e_pre_norm", "e_pool_w", "e_pool_scale", "e_post_norm", "o_pre_norm", "o_sgu_norm_g", "o_sgu_norm_b",
               "o_sgu_w", "o_sgu_b", "o_conv_w", "o_conv_b", "o_conv_norm_g", "o_conv_norm_b", "o_post_norm")
ALL_ORDER = ("e_pre_norm", "e_w_in", "e_pool_w", "e_pool_scale", "e_w_out", "e_post_norm", "o_pre_norm", "o_w_in",
             "o_sgu_norm_g", "o_sgu_norm_b", "o_sgu_w", "o_sgu_b", "o_conv_w", "o_conv_b", "o_conv_norm_g",
             "o_conv_norm_b", "o_w_out", "o_post_norm")


def _full_shape(name):
    shape, axis = SHARDED_SMALL[name]
    return tuple(s * N_CHIPS if i == axis else s for i, s in enumerate(shape))


def _from_chips(name, stacked):
    shape, axis = SHARDED_SMALL[name]
    return jnp.moveaxis(stacked, 0, axis).reshape(_full_shape(name))


def kernel(x, e_pre_norm, e_w_in, e_pool_w, e_pool_scale, e_w_out, e_post_norm, o_pre_norm, o_w_in, o_sgu_norm_g, o_sgu_norm_b, o_sgu_w, o_sgu_b, o_conv_w, o_conv_b, o_conv_norm_g, o_conv_norm_b, o_w_out, o_post_norm, loss_target, m_e_pre_norm, m_e_w_in, m_e_pool_w, m_e_pool_scale, m_e_w_out, m_e_post_norm, m_o_pre_norm, m_o_w_in, m_o_sgu_norm_g, m_o_sgu_norm_b, m_o_sgu_w, m_o_sgu_b, m_o_conv_w, m_o_conv_b, m_o_conv_norm_g, m_o_conv_norm_b, m_o_w_out, m_o_post_norm, v_e_pre_norm, v_e_w_in, v_e_pool_w, v_e_pool_scale, v_e_w_out, v_e_post_norm, v_o_pre_norm, v_o_w_in, v_o_sgu_norm_g, v_o_sgu_norm_b, v_o_sgu_w, v_o_sgu_b, v_o_conv_w, v_o_conv_b, v_o_conv_norm_g, v_o_conv_norm_b, v_o_w_out, v_o_post_norm):
    w = dict(e_pre_norm=e_pre_norm, e_w_in=e_w_in, e_pool_w=e_pool_w, e_pool_scale=e_pool_scale, e_w_out=e_w_out,
             e_post_norm=e_post_norm, o_pre_norm=o_pre_norm, o_w_in=o_w_in, o_sgu_norm_g=o_sgu_norm_g,
             o_sgu_norm_b=o_sgu_norm_b, o_sgu_w=o_sgu_w, o_sgu_b=o_sgu_b, o_conv_w=o_conv_w, o_conv_b=o_conv_b,
             o_conv_norm_g=o_conv_norm_g, o_conv_norm_b=o_conv_norm_b, o_w_out=o_w_out, o_post_norm=o_post_norm)
    m = dict(e_pre_norm=m_e_pre_norm, e_w_in=m_e_w_in, e_pool_w=m_e_pool_w, e_pool_scale=m_e_pool_scale,
             e_w_out=m_e_w_out, e_post_norm=m_e_post_norm, o_pre_norm=m_o_pre_norm, o_w_in=m_o_w_in,
             o_sgu_norm_g=m_o_sgu_norm_g, o_sgu_norm_b=m_o_sgu_norm_b, o_sgu_w=m_o_sgu_w, o_sgu_b=m_o_sgu_b,
             o_conv_w=m_o_conv_w, o_conv_b=m_o_conv_b, o_conv_norm_g=m_o_conv_norm_g, o_conv_norm_b=m_o_conv_norm_b,
             o_w_out=m_o_w_out, o_post_norm=m_o_post_norm)
    v = dict(e_pre_norm=v_e_pre_norm, e_w_in=v_e_w_in, e_pool_w=v_e_pool_w, e_pool_scale=v_e_pool_scale,
             e_w_out=v_e_w_out, e_post_norm=v_e_post_norm, o_pre_norm=v_o_pre_norm, o_w_in=v_o_w_in,
             o_sgu_norm_g=v_o_sgu_norm_g, o_sgu_norm_b=v_o_sgu_norm_b, o_sgu_w=v_o_sgu_w, o_sgu_b=v_o_sgu_b,
             o_conv_w=v_o_conv_w, o_conv_b=v_o_conv_b, o_conv_norm_g=v_o_conv_norm_g, o_conv_norm_b=v_o_conv_norm_b,
             o_w_out=v_o_w_out, o_post_norm=v_o_post_norm)
    w, m, v = ({k: a[0] for k, a in d.items()} for d in (w, m, v))
    chip = 2 * lax.axis_index("x") + lax.axis_index("y")

    loss, grad_x, in_flight, small = _step(x[0], loss_target[0], w, chip)

    grads, delta, new_m, new_v = {}, {}, {}, {}

    def rows_of(a):
        return a.reshape(-1, a.shape[-1])

    after = grad_x
    for k in ("o_w_out", "o_w_in", "e_w_out", "small", "e_w_in"):
        if k == "small":
            small_grads = [small[name].reshape(SMALL_PACKING[name][0]) for name in SMALL_ORDER]
            updates = _small_finalize(small_grads, *[[rows_of(d[name]) for name in SMALL_ORDER] for d in (w, m, v)], after)
            for d, arrays in zip((grads, delta, new_m, new_v), updates):
                for name, a in zip(SMALL_ORDER, arrays):
                    d[name] = a.reshape(w[name].shape)
            after = updates[1][0]
            continue
        grads[k], delta[k], new_m[k], new_v[k] = _adamw(w[k], _land(in_flight, k, chip, after), m[k], v[k], f"adamw_{k}")
        after = delta[k]
    loss = lax.psum(loss[0, 0], ("x", "y", "c"))

    outs = [loss, grad_x[None]]
    for d in (grads, delta, new_m, new_v):
        outs += [d[k][None] for k in ALL_ORDER]
    return tuple(outs)
```

```python
import jax
import jax.numpy as jnp
from jax import lax
from jax.experimental import pallas as pl
from jax.experimental.pallas import tpu as pltpu

f32 = jnp.float32
bf16 = jnp.bfloat16
SDS = jax.ShapeDtypeStruct

SEQ = 2048
D_MODEL = 2048
EPS = 1e-6
NEG = -1e30
HEAD_DIM = 128
ROT_HALF = 16
ROPE_THETA = 500000.0
DILATIONS = (1, 4, 16)
SPAN = 128
N_HEADS = 8
HALF = 1024
POOL_CH = 256
CONV_K = 31
CONV_PAD = 32
CHUNK = 128
N_CHIPS = 4
LANES = 256
E_IN_PIECES = 3
SMALL_SHARD_ROWS = 352
ANY = pl.BlockSpec(memory_space=pl.ANY)
MESH = pl.DeviceIdType.MESH

ADAM_LR = 0.001
ADAM_B1 = 0.9
ADAM_B2 = 0.999
ADAM_EPS = 1e-08
ADAM_WD = 0.01
ADAM_STEP = 10


def _dot(a, b):
    return jnp.dot(a, b, preferred_element_type=f32)


def _dot_nt(a, b):
    return lax.dot_general(a, b, (((1,), (1,)), ((), ())), preferred_element_type=f32)


def _dot_tn(a, b):
    return lax.dot_general(a, b, (((0,), (0,)), ((), ())), preferred_element_type=f32)


def _sigmoid(x):
    return 1.0 / (1.0 + jnp.exp(-x))


def _silu_and_grad(x):
    s = _sigmoid(x)
    return x * s, s * (1.0 + x * (1.0 - s))


def _rms_fwd(x, g):
    r = lax.rsqrt(jnp.mean(x * x, axis=-1, keepdims=True) + EPS)
    return x * r * g


def _rms_bwd(x, g, dout):
    r = lax.rsqrt(jnp.mean(x * x, axis=-1, keepdims=True) + EPS)
    xh = x * r
    dg = jnp.sum(dout * xh, axis=0, keepdims=True)
    dxh = dout * g
    dx = r * (dxh - xh * jnp.mean(dxh * xh, axis=-1, keepdims=True))
    return dx, dg


def _ln_stats(x):
    mu = jnp.mean(x, axis=-1, keepdims=True)
    xc = x - mu
    rstd = lax.rsqrt(jnp.mean(xc * xc, axis=-1, keepdims=True) + EPS)
    return xc * rstd, rstd


def _ln_bwd(xh, rstd, g, dout):
    dg = jnp.sum(dout * xh, axis=0, keepdims=True)
    db = jnp.sum(dout, axis=0, keepdims=True)
    dxh = dout * g
    dx = rstd * (dxh - jnp.mean(dxh, axis=-1, keepdims=True) - xh * jnp.mean(dxh * xh, axis=-1, keepdims=True))
    return dx, dg, db


def _accumulate(ref, value, first):
    @pl.when(first)
    def _():
        ref[...] = value

    @pl.when(jnp.logical_not(first))
    def _():
        ref[...] += value


def _write_behind(step, steps, tiles, sems, window):
    slot = step % 2

    def copies(s, at):
        return [pltpu.make_async_copy(tile.at[s], window(t, at), sems.at[2 * t + s]) for t, tile in enumerate(tiles)]

    @pl.when(step >= 2)
    def _():
        for cp in copies(slot, step - 2):
            cp.wait()

    def full():
        for cp in copies(slot, step):
            cp.start()

        @pl.when(step == steps - 1)
        def _():
            for cp in copies(slot, step):
                cp.wait()
            if steps > 1:
                for cp in copies(1 - slot, step - 1):
                    cp.wait()

    return [tile.at[slot] for tile in tiles], full


def _columns(ref, first, width):
    return ref.at[:, pl.ds(pl.multiple_of(first, 128), width)]


def _col_tile(ns):
    for t in (1024, 768, 512, 256):
        if ns % t == 0:
            return t
    raise ValueError(ns)


def _mm_nn(a, w, out_dtype, name, piece=0, pieces=1, into=None):
    m, k = a.shape
    j, _, ns = w.shape
    tm, tn = m, _col_tile(ns)
    nb = ns // tn

    def body(a_ref, w_ref, *rest):
        rest[-1][...] = _dot(a_ref[...], w_ref[...]).astype(rest[-1].dtype)

    return pl.pallas_call(
        body, name=name, grid=(j * nb, m // tm),
        in_specs=[pl.BlockSpec((tm, k), lambda n, i: (i, 0)),
                  pl.BlockSpec((None, k, tn), lambda n, i: (n // nb, 0, n % nb))] + ([] if into is None else [ANY]),
        out_specs=pl.BlockSpec((tm, tn), lambda n, i: (i, ((n // nb) * pieces + piece) * nb + n % nb)),
        out_shape=SDS((m, j * ns * pieces), out_dtype),
        input_output_aliases={} if into is None else {2: 0},
    )(a, w, *([] if into is None else [into]))


def _mm_nt(dz, ws, name, after):
    m, _ = dz.shape
    pieces = len(ws)
    j, k, ns = ws[0].shape
    tm, tk = 1024, 1024

    def body(dz_ref, *rest):
        w_refs, o_ref = rest[:pieces], rest[pieces + 1]
        total = _dot_nt(dz_ref[:, 0:ns], w_refs[0][...])
        for q in range(1, pieces):
            total = total + _dot_nt(dz_ref[:, q * ns:(q + 1) * ns], w_refs[q][...])
        if j == 1:
            o_ref[...] = total.astype(bf16)
            return
        sum_ref = rest[pieces + 2]
        r = pl.program_id(2)
        _accumulate(sum_ref, total, r == 0)

        @pl.when(r == j - 1)
        def _():
            o_ref[...] = sum_ref[...].astype(bf16)

    return pl.pallas_call(
        body, name=name, grid=(m // tm, k // tk, j),
        in_specs=[pl.BlockSpec((tm, pieces * ns), lambda i, kk, r: (i, r))]
        + [pl.BlockSpec((None, tk, ns), lambda i, kk, r: (r, kk, 0))] * pieces + [ANY],
        out_specs=pl.BlockSpec((tm, tk), lambda i, kk, r: (i, kk)),
        out_shape=SDS((m, k), bf16), scratch_shapes=[] if j == 1 else [pltpu.VMEM((tm, tk), f32)],
    )(dz, *ws, after)


def _mm_tn(a, dz, j, name):
    m, k = a.shape
    ns = dz.shape[1] // j
    tk, tn = 1024, _col_tile(ns)
    nb = ns // tn

    def body(a_ref, dz_ref, o_ref):
        o_ref[...] = _dot_tn(a_ref[...], dz_ref[...]).astype(o_ref.dtype)

    return pl.pallas_call(
        body, name=name, grid=(k // tk, j * nb),
        in_specs=[pl.BlockSpec((m, tk), lambda kk, n: (0, kk)),
                  pl.BlockSpec((m, tn), lambda kk, n: (0, n))],
        out_specs=pl.BlockSpec((None, tk, tn), lambda kk, n: (n // nb, kk, n % nb)),
        out_shape=SDS((j, k, ns), bf16),
    )(a, dz)


ROWS = 512


def _row_spec(width=D_MODEL, col=0):
    return pl.BlockSpec((ROWS, width), lambda i: (i, col))


def _vec_spec(width=D_MODEL):
    return pl.BlockSpec((1, width), lambda i: (0, 0))


def _pre_norm(x, g):
    def body(x_ref, g_ref, h_ref):
        h_ref[...] = _rms_fwd(x_ref[...], g_ref[...]).astype(bf16)

    return pl.pallas_call(
        body, name="pre_norm", grid=(SEQ // ROWS,), in_specs=[_row_spec(), _vec_spec()],
        out_specs=_row_spec(), out_shape=SDS((SEQ, D_MODEL), bf16))(x, g)


def _mid_norm(x, y, g_post, g_pre):
    def body(x_ref, y_ref, gpost_ref, gpre_ref, x1_ref, h1_ref):
        x1 = x_ref[...] + _rms_fwd(y_ref[...], gpost_ref[...])
        x1_ref[...] = x1
        h1_ref[...] = _rms_fwd(x1, gpre_ref[...]).astype(bf16)

    return pl.pallas_call(
        body, name="mid_norm", grid=(SEQ // ROWS,),
        in_specs=[_row_spec(), _row_spec(), _vec_spec(), _vec_spec()],
        out_specs=[_row_spec(), _row_spec()],
        out_shape=[SDS((SEQ, D_MODEL), f32), SDS((SEQ, D_MODEL), bf16)])(x, y, g_post, g_pre)


def _final_norm_loss(x1, y, g_post, target):
    def body(x1_ref, y_ref, g_ref, t_ref, loss_ref, dx2_ref, dy_ref, dg_ref):
        first = pl.program_id(0) == 0
        y = y_ref[...]
        g = g_ref[...]
        err = x1_ref[...] + _rms_fwd(y, g) - t_ref[...]
        sq = jnp.sum(jnp.sum(err * err, axis=1, keepdims=True), axis=0, keepdims=True)
        _accumulate(loss_ref, sq * (0.5 / D_MODEL), first)
        dx2 = err * (1.0 / D_MODEL)
        dx2_ref[...] = dx2
        dy, dg = _rms_bwd(y, g, dx2)
        dy_ref[...] = dy.astype(bf16)
        _accumulate(dg_ref, dg, first)

    return pl.pallas_call(
        body, name="final_norm_loss", grid=(SEQ // ROWS,),
        in_specs=[_row_spec(), _row_spec(), _vec_spec(), _row_spec()],
        out_specs=[pl.BlockSpec((1, 1), lambda i: (0, 0)), _row_spec(), _row_spec(), _vec_spec()],
        out_shape=[SDS((1, 1), f32), SDS((SEQ, D_MODEL), f32), SDS((SEQ, D_MODEL), bf16), SDS((1, D_MODEL), f32)],
    )(x1, y, g_post, target)


def _mid_norm_bwd(dx2, dh1, x1, y0, g_pre, g_post):
    def body(dx2_ref, dh1_ref, x1_ref, y0_ref, gpre_ref, gpost_ref, dx1_ref, dy0_ref, dgpre_ref, dgpost_ref):
        first = pl.program_id(0) == 0
        d_in, dgpre = _rms_bwd(x1_ref[...], gpre_ref[...], dh1_ref[...])
        dx1 = dx2_ref[...] + d_in
        dx1_ref[...] = dx1
        dy0, dgpost = _rms_bwd(y0_ref[...], gpost_ref[...], dx1)
        dy0_ref[...] = dy0.astype(bf16)
        _accumulate(dgpre_ref, dgpre, first)
        _accumulate(dgpost_ref, dgpost, first)

    return pl.pallas_call(
        body, name="mid_norm_bwd", grid=(SEQ // ROWS,),
        in_specs=[_row_spec(), _row_spec(), _row_spec(), _row_spec(), _vec_spec(), _vec_spec()],
        out_specs=[_row_spec(), _row_spec(), _vec_spec(), _vec_spec()],
        out_shape=[SDS((SEQ, D_MODEL), f32), SDS((SEQ, D_MODEL), bf16), SDS((1, D_MODEL), f32), SDS((1, D_MODEL), f32)],
    )(dx2, dh1, x1, y0, g_pre, g_post)


def _pre_norm_bwd(dx1, dh0, x, g):
    def body(dx1_ref, dh0_ref, x_ref, g_ref, dx_ref, dg_ref):
        d_in, dg = _rms_bwd(x_ref[...], g_ref[...], dh0_ref[...])
        dx_ref[...] = dx1_ref[...] + d_in
        _accumulate(dg_ref, dg, pl.program_id(0) == 0)

    return pl.pallas_call(
        body, name="pre_norm_bwd", grid=(SEQ // ROWS,),
        in_specs=[_row_spec(), _row_spec(), _row_spec(), _vec_spec()],
        out_specs=[_row_spec(), _vec_spec()],
        out_shape=[SDS((SEQ, D_MODEL), f32), SDS((1, D_MODEL), f32)])(dx1, dh0, x, g)


def _pool_count(g):
    row = lax.broadcasted_iota(jnp.int32, (SEQ, 1), 0)
    width = jnp.left_shift(2, g)
    return row, width, jnp.minimum(row + 1, width).astype(f32)


def _trailing_sum(x, row, width):
    s = x
    for k in (1, 2, 4, 8):
        shifted = jnp.where(row >= k, pltpu.roll(s, k, 0), 0.0)
        s = jnp.where(width > k, s + shifted, s)
    return s


def _leading_sum(x, row, width):
    s = x
    for k in (1, 2, 4, 8):
        shifted = jnp.where(row < SEQ - k, pltpu.roll(s, SEQ - k, 0), 0.0)
        s = jnp.where(width > k, s + shifted, s)
    return s


def _pool_specs():
    a_in = pl.BlockSpec((SEQ, POOL_CH), lambda g: (0, g))
    a_gate = pl.BlockSpec((SEQ, POOL_CH), lambda g: (0, 4 + g))
    w = pl.BlockSpec((None, POOL_CH, POOL_CH), lambda g: (g, 0, 0))
    scale = pl.BlockSpec((1, POOL_CH), lambda g: (0, g))
    return a_in, a_gate, w, scale


def _pool_fwd(z0, pool_w, pool_scale):
    def body(a_ref, gate_ref, w_ref, scale_ref, ya_ref):
        row, width, count = _pool_count(pl.program_id(0))
        a = a_ref[...]
        pooled = _trailing_sum(a, row, width) / count - a
        mixed = _dot(pooled.astype(bf16), w_ref[...]) * scale_ref[...]
        gate = gate_ref[...]
        ya_ref[...] = (mixed * gate * _sigmoid(gate)).astype(bf16)

    return pl.pallas_call(
        body, name="pool_fwd", grid=(4,), in_specs=list(_pool_specs()),
        out_specs=pl.BlockSpec((SEQ, POOL_CH), lambda g: (0, g)),
        out_shape=SDS((SEQ, 2 * HALF), bf16))(z0, z0, pool_w, pool_scale)


def _pool_bwd(z0, dcat, pool_w, pool_scale):
    def body(a_ref, gate_ref, w_ref, scale_ref, dya_ref, dz_ref, dw_ref, dscale_ref, da_tiles, dgate_tiles, sems):
        g = pl.program_id(0)
        (da_ref, dgate_ref), full = _write_behind(
            g, 4, [da_tiles, dgate_tiles], sems, lambda t, at: _columns(dz_ref, t * HALF + at * POOL_CH, POOL_CH))
        row, width, count = _pool_count(g)
        a = a_ref[...]
        pooled = (_trailing_sum(a, row, width) / count - a).astype(bf16)
        w = w_ref[...]
        scale = scale_ref[...]
        mixed = _dot(pooled, w)
        silu, dsilu = _silu_and_grad(gate_ref[...])
        dya = dya_ref[...]
        dgate_ref[...] = (dya * mixed * scale * dsilu).astype(bf16)
        dms = dya * silu
        dscale_ref[...] = jnp.sum(dms * mixed, axis=0, keepdims=True)
        dmixed = (dms * scale).astype(bf16)
        dw_ref[...] = _dot_tn(pooled, dmixed)
        dpooled = _dot_nt(dmixed, w)
        da_ref[...] = (_leading_sum(dpooled / count, row, width) - dpooled).astype(bf16)
        full()

    a_in, a_gate, w, scale = _pool_specs()
    col = pl.BlockSpec((SEQ, POOL_CH), lambda g: (0, g))
    tiles = pltpu.VMEM((2, SEQ, POOL_CH), bf16)
    return pl.pallas_call(
        body, name="pool_bwd", grid=(4,), in_specs=[a_in, a_gate, w, scale, col],
        out_specs=[ANY, w, scale],
        out_shape=[SDS((SEQ, 6 * D_MODEL), bf16), SDS((4, POOL_CH, POOL_CH), f32), SDS((1, HALF), f32)],
        scratch_shapes=[tiles, tiles, pltpu.SemaphoreType.DMA((4,))],
    )(z0, z0, pool_w, pool_scale, dcat)


Q_COL, K_COL, V_COL, BGATE_COL = 16, 40, 64, 88


def _rope_tables():
    pos = jnp.arange(SEQ, dtype=f32)
    inv_freq = jnp.power(ROPE_THETA, -jnp.arange(0, 2 * ROT_HALF, 2, dtype=f32) / (2 * ROT_HALF))
    ang = pos[:, None] * inv_freq[None, :]
    cos, sin = jnp.cos(ang), jnp.sin(ang)
    zeros = jnp.zeros((SEQ, HEAD_DIM - 2 * ROT_HALF), f32)
    cos_t = jnp.concatenate([cos, cos, zeros + 1.0], axis=1)
    sin_t = jnp.concatenate([sin, sin, zeros], axis=1)
    j = jnp.arange(HEAD_DIM)[:, None]
    i = jnp.arange(HEAD_DIM)[None, :]
    rot = jnp.where((i < ROT_HALF) & (j == i + ROT_HALF), -1.0, 0.0) + jnp.where(
        (i >= ROT_HALF) & (i < 2 * ROT_HALF) & (j == i - ROT_HALF), 1.0, 0.0)
    return cos_t, sin_t, rot.astype(bf16), rot.T.astype(bf16)


def _exact_dot(t, m):
    hi = t.astype(bf16)
    lo = (t - hi.astype(f32)).astype(bf16)
    return _dot(hi, m) + _dot(lo, m)


def _rope(t, cos_t, sin_t, rot):
    return t * cos_t + _exact_dot(t, rot) * sin_t


def _rope_transposed(d, cos_t, sin_t, rot_t):
    return d * cos_t + _exact_dot(d * sin_t, rot_t)


ROW_CHUNK = 256
BLOCKS_TOGETHER = 8


def _chunks(fn):
    for start in range(0, SEQ, ROW_CHUNK):
        fn(start)


def _pieces(dilation):
    length = SEQ // dilation
    n = min(length, ROW_CHUNK)
    return [(r, l0, n) for r in range(dilation) for l0 in range(0, length, n)]


def _by_residue(dst_ref, src_ref, dilation, dtype):
    length = SEQ // dilation
    for r, l0, n in _pieces(dilation):
        src = src_ref[l0:l0 + n, :] if dilation == 1 else src_ref[pl.ds(r + dilation * l0, n, stride=dilation), :]
        start = r * length + l0
        dst_ref[start:start + n, :] = src.astype(dtype)


def _by_position(dst_ref, src_ref, dilation):
    length = SEQ // dilation
    for r, l0, n in _pieces(dilation):
        src = src_ref[r * length + l0:r * length + l0 + n, :]
        if dilation == 1:
            dst_ref[l0:l0 + n, :] = src
        else:
            dst_ref[pl.ds(r + dilation * l0, n, stride=dilation), :] = src


def _attn_masks():
    qi = lax.broadcasted_iota(jnp.int32, (SPAN, 2 * SPAN), 0)
    kj = lax.broadcasted_iota(jnp.int32, (SPAN, 2 * SPAN), 1)
    window = ((kj < SPAN) & (kj >= qi)) | ((kj >= SPAN) & (kj - SPAN <= qi))
    own = lax.broadcasted_iota(jnp.int32, (SPAN, SPAN), 1) <= lax.broadcasted_iota(jnp.int32, (SPAN, SPAN), 0)
    return window, own


def _attn_blocks(dilation):
    per_residue = SEQ // dilation // SPAN
    blocks = [(c, c % per_residue != 0) for c in range(SEQ // SPAN)]
    return [blocks[i:i + BLOCKS_TOGETHER] for i in range(0, len(blocks), BLOCKS_TOGETHER)]


def _block_keys(c, has_prev):
    return slice((c - 1) * SPAN if has_prev else c * SPAN, (c + 1) * SPAN)


def _head_spec(col):
    return pl.BlockSpec((SEQ, HEAD_DIM), lambda h: (0, col + h))


def _table_spec():
    return pl.BlockSpec((SEQ, HEAD_DIM), lambda h: (0, 0))


def _attn_fwd(z0, tables, mixed):
    scale = HEAD_DIM ** -0.5

    def body(*refs):
        qkv = refs[0:9]
        bg_ref, cos_ref, sin_ref, rot_ref = refs[9:13]
        yb_ref, att_ref, lse_ref = refs[14:17]
        saved = refs[17:26]
        tmp_q, tmp_k, v_ones, o_res, l_res, o_nat, l_nat = refs[26:33]
        window_mask, own_mask = _attn_masks()
        rot = rot_ref[...]

        @pl.when(pl.program_id(0) == 0)
        def _():
            v_ones[:, HEAD_DIM:] = jnp.ones((SEQ, HEAD_DIM), bf16)

        for g, dilation in enumerate(DILATIONS):
            q_ref, k_ref, v_ref = qkv[3 * g:3 * g + 3]
            qd, kd, vd = saved[3 * g:3 * g + 3]

            def rope_rows(start, q_ref=q_ref, k_ref=k_ref):
                r = pl.ds(start, ROW_CHUNK)
                cos_t, sin_t = cos_ref[r, :], sin_ref[r, :]
                tmp_q[r, :] = _rope(q_ref[r, :], cos_t, sin_t, rot) * scale
                tmp_k[r, :] = _rope(k_ref[r, :], cos_t, sin_t, rot)

            _chunks(rope_rows)
            _by_residue(qd, tmp_q, dilation, bf16)
            _by_residue(kd, tmp_k, dilation, bf16)
            _by_residue(vd, v_ref, dilation, bf16)
            for l0 in range(0, SEQ, ROW_CHUNK):
                v_ones[l0:l0 + ROW_CHUNK, 0:HEAD_DIM] = vd[l0:l0 + ROW_CHUNK, :]

            for group in _attn_blocks(dilation):
                scores = [_dot_nt(qd[c * SPAN:(c + 1) * SPAN, :], kd[_block_keys(c, prev), :]) for c, prev in group]
                tops, probs = [], []
                for (c, prev), s in zip(group, scores):
                    s = jnp.where(window_mask if prev else own_mask, s, NEG)
                    tops.append(jnp.max(s, axis=1, keepdims=True))
                    probs.append(jnp.exp(s - tops[-1]).astype(bf16))
                sums = [_dot(p, v_ones[_block_keys(c, prev), :]) for (c, prev), p in zip(group, probs)]
                for (c, prev), m, o in zip(group, tops, sums):
                    den = o[:, HEAD_DIM:]
                    o_res[c * SPAN:(c + 1) * SPAN, :] = o[:, :HEAD_DIM] / den
                    l_res[c * SPAN:(c + 1) * SPAN, :] = m + jnp.log(den)

            if dilation > 1:
                _by_position(o_nat, o_res, dilation)
                _by_position(l_nat, l_res, dilation)
            o_g, l_g = (o_res, l_res) if dilation == 1 else (o_nat, l_nat)

            def merge(start, g=g, o_g=o_g, l_g=l_g):
                r = pl.ds(start, ROW_CHUNK)
                if g == 0:
                    att, total = o_g[r, :], l_g[r, :]
                else:
                    l_old, l_new = lse_ref[r, :], l_g[r, :]
                    top = jnp.maximum(l_old, l_new)
                    total = top + jnp.log(jnp.exp(l_old - top) + jnp.exp(l_new - top))
                    att = att_ref[r, :] * jnp.exp(l_old - total) + o_g[r, :] * jnp.exp(l_new - total)
                att_ref[r, :] = att
                lse_ref[r, :] = total
                if g == len(DILATIONS) - 1:
                    gate = bg_ref[r, :]
                    yb_ref[r, :] = (att * gate * _sigmoid(gate)).astype(bf16)

            _chunks(merge)

    in_specs = []
    for g in range(3):
        in_specs += [_head_spec(Q_COL + 8 * g), _head_spec(K_COL + 8 * g), _head_spec(V_COL + 8 * g)]
    in_specs += [_head_spec(BGATE_COL), _table_spec(), _table_spec(), pl.BlockSpec((HEAD_DIM, HEAD_DIM), lambda h: (0, 0)), ANY]
    out_spec = pl.BlockSpec((SEQ, HEAD_DIM), lambda h: (0, h))
    right_half = pl.BlockSpec((SEQ, HEAD_DIM), lambda h: (0, N_HEADS + h))
    vm = lambda dt: pltpu.VMEM((SEQ, HEAD_DIM), dt)
    cos_t, sin_t, rot, _ = tables
    out = pl.pallas_call(
        body, name="attn_fwd", grid=(N_HEADS,), in_specs=in_specs, out_specs=[right_half] + [out_spec] * 11,
        out_shape=[SDS((SEQ, 2 * HALF), bf16), SDS((SEQ, HALF), f32), SDS((SEQ, HALF), f32)] + [SDS((SEQ, HALF), bf16)] * 9,
        scratch_shapes=[vm(f32), vm(f32), pltpu.VMEM((SEQ, 2 * HEAD_DIM), bf16), vm(f32), vm(f32), vm(f32), vm(f32)],
        input_output_aliases={13: 0},
    )(*([z0] * 10), cos_t, sin_t, rot, mixed)
    return out[0], out[1], out[2], [tuple(out[3 + 3 * g:6 + 3 * g]) for g in range(3)]


def _attn_bwd_group(g, saved, z0, att, lse, dcat, tables, dz):
    scale = HEAD_DIM ** -0.5
    dilation = DILATIONS[g]
    with_gate = g == 0
    n_out = 4 if with_gate else 3
    first_col = (Q_COL + 8 * g, K_COL + 8 * g, V_COL + 8 * g, BGATE_COL)

    def body(*refs):
        qd, kd, vd, bg_ref, att_ref, lse_ref, dyb_ref, cos_ref, sin_ref, rot_t_ref = refs[0:10]
        dz_ref = refs[11]
        dod, ld, dd, tmp, aq, ak, av = refs[12:19]
        views, full = _write_behind(pl.program_id(0), N_HEADS, refs[19:19 + n_out], refs[19 + n_out],
                                    lambda t, at: _columns(dz_ref, (first_col[t] + at) * HEAD_DIM, HEAD_DIM))
        dq_ref, dk_ref, dv_ref = views[0:3]
        window_mask, own_mask = _attn_masks()
        rot_t = rot_t_ref[...]

        def gate_rows(start):
            r = pl.ds(start, ROW_CHUNK)
            silu, dsilu = _silu_and_grad(bg_ref[r, :])
            att_v = att_ref[r, :]
            dyb = dyb_ref[r, :]
            if with_gate:
                views[3][r, :] = (dyb * att_v * dsilu).astype(bf16)
            datt = dyb * silu
            tmp[r, :] = datt
            aq[r, :] = jnp.broadcast_to(jnp.sum(datt * att_v, axis=1, keepdims=True), (ROW_CHUNK, HEAD_DIM))

        _chunks(gate_rows)
        _by_residue(dod, tmp, dilation, bf16)
        _by_residue(dd, aq, dilation, f32)
        _by_residue(ld, lse_ref, dilation, f32)

        for group in _attn_blocks(dilation):
            rows = [slice(c * SPAN, (c + 1) * SPAN) for c, _ in group]
            keys = [_block_keys(c, prev) for c, prev in group]
            scores = [_dot_nt(qd[r, :], kd[k, :]) for r, k in zip(rows, keys)]
            dprobs = [_dot_nt(dod[r, :], vd[k, :]) for r, k in zip(rows, keys)]
            probs, dscores = [], []
            for (c, prev), r, s, dp in zip(group, rows, scores, dprobs):
                lse_q, delta = ld[r, :], dd[r, :]
                if prev:
                    lse_q = jnp.concatenate([lse_q, lse_q], axis=1)
                    delta = jnp.concatenate([delta, delta], axis=1)
                p = jnp.where(window_mask if prev else own_mask, jnp.exp(s - lse_q), 0.0)
                probs.append(p.astype(bf16))
                dscores.append((p * (dp - delta)).astype(bf16))
            dvs = [_dot_tn(p, dod[r, :]) for p, r in zip(probs, rows)]
            dks = [_dot_tn(ds, qd[r, :]) for ds, r in zip(dscores, rows)]
            dqs = [_dot(ds, kd[k, :]) for ds, k in zip(dscores, keys)]
            for (c, prev), r, dv, dk, dq in zip(group, rows, dvs, dks, dqs):
                aq[r, :] = dq
                if prev:
                    before = slice((c - 1) * SPAN, c * SPAN)
                    av[before, :] += dv[0:SPAN]
                    ak[before, :] += dk[0:SPAN]
                    av[r, :] = dv[SPAN:]
                    ak[r, :] = dk[SPAN:]
                else:
                    av[r, :] = dv
                    ak[r, :] = dk

        def finish(out_ref, acc, factor, roped):
            if dilation > 1:
                _by_position(tmp, acc, dilation)
            src = acc if dilation == 1 else tmp

            def rows(start):
                r = pl.ds(start, ROW_CHUNK)
                d = src[r, :]
                if factor != 1.0:
                    d = d * factor
                if roped:
                    d = _rope_transposed(d, cos_ref[r, :], sin_ref[r, :], rot_t)
                out_ref[r, :] = d.astype(bf16)

            _chunks(rows)

        finish(dq_ref, aq, scale, True)
        finish(dk_ref, ak, 1.0, True)
        finish(dv_ref, av, 1.0, False)
        full()

    head = pl.BlockSpec((SEQ, HEAD_DIM), lambda h: (0, h))
    in_specs = [head, head, head, _head_spec(BGATE_COL), head, head, _head_spec(8), _table_spec(), _table_spec(),
                pl.BlockSpec((HEAD_DIM, HEAD_DIM), lambda h: (0, 0)), ANY]
    vm = lambda dt: pltpu.VMEM((SEQ, HEAD_DIM), dt)
    cos_t, sin_t, _, rot_t = tables
    return pl.pallas_call(
        body, name=f"attn_bwd_g{g}", grid=(N_HEADS,), in_specs=in_specs, out_specs=ANY,
        out_shape=SDS(dz.shape, dz.dtype), input_output_aliases={10: 0},
        scratch_shapes=[vm(bf16), vm(f32), vm(f32), vm(f32), vm(f32), vm(f32), vm(f32)]
        + [pltpu.VMEM((2, SEQ, HEAD_DIM), bf16)] * n_out + [pltpu.SemaphoreType.DMA((2 * n_out,))],
    )(*saved, z0, att, lse, dcat, cos_t, sin_t, rot_t, dz)


def _sgu_specs():
    chunk = lambda col: pl.BlockSpec((CHUNK, HALF), lambda n: (n, col))
    vec = pl.BlockSpec((1, HALF), lambda n: (0, 0))
    w = pl.BlockSpec((4, CHUNK, CHUNK), lambda n: (0, 0, 0))
    bias = pl.BlockSpec((CHUNK, CHUNK), lambda n: (0, 0))
    return chunk, vec, w, bias


def _sgu_weights(w_ref):
    tril = lax.broadcasted_iota(jnp.int32, (CHUNK, CHUNK), 1) <= lax.broadcasted_iota(jnp.int32, (CHUNK, CHUNK), 0)
    return tril, [jnp.where(tril, w_ref[h], 0.0).astype(bf16) for h in range(4)]


def _sgu_fwd(z1, ln_g, ln_b, sgu_w, bias_t):
    def body(u_ref, v_ref, cg_ref, g_ref, b_ref, w_ref, bias_ref, yc_ref):
        _, ws = _sgu_weights(w_ref)
        xh, _ = _ln_stats(v_ref[...])
        vn = (xh * g_ref[...] + b_ref[...]).astype(bf16)
        for h in range(4):
            cols = slice(h * POOL_CH, (h + 1) * POOL_CH)
            s = _dot(ws[h], vn[:, cols]) + bias_ref[:, h:h + 1]
            gate = cg_ref[:, cols]
            yc_ref[:, cols] = (u_ref[:, cols] * s * gate * _sigmoid(gate)).astype(bf16)

    chunk, vec, w, bias = _sgu_specs()
    return pl.pallas_call(
        body, name="sgu_fwd", grid=(SEQ // CHUNK,),
        in_specs=[chunk(0), chunk(1), chunk(2), vec, vec, w, bias], out_specs=chunk(0),
        out_shape=SDS((SEQ, 2 * HALF), bf16))(z1, z1, z1, ln_g, ln_b, sgu_w, bias_t)


def _sgu_bwd(z1, dcat, ln_g, ln_b, sgu_w, bias_t):
    def body(u_ref, v_ref, cg_ref, dyc_ref, g_ref, b_ref, w_ref, bias_ref,
             dz_ref, dw_ref, dbias_ref, dg_ref, db_ref, dvn_ref, du_tiles, dv_tiles, dcg_tiles, sems):
        n = pl.program_id(0)
        (du_ref, dv_ref, dcg_ref), full = _write_behind(
            n, SEQ // CHUNK, [du_tiles, dv_tiles, dcg_tiles], sems,
            lambda t, at: dz_ref.at[pl.ds(pl.multiple_of(at * CHUNK, CHUNK), CHUNK), t * HALF:(t + 1) * HALF])
        first = n == 0
        tril, ws = _sgu_weights(w_ref)
        xh, rstd = _ln_stats(v_ref[...])
        g = g_ref[...]
        vn = (xh * g + b_ref[...]).astype(bf16)

        @pl.when(first)
        def _():
            dbias_ref[...] = jnp.zeros((CHUNK, CHUNK), f32)

        for h in range(4):
            cols = slice(h * POOL_CH, (h + 1) * POOL_CH)
            vn_h = vn[:, cols]
            s = _dot(ws[h], vn_h) + bias_ref[:, h:h + 1]
            silu, dsilu = _silu_and_grad(cg_ref[:, cols])
            dyc = dyc_ref[:, cols]
            u = u_ref[:, cols]
            du_ref[:, cols] = (dyc * s * silu).astype(bf16)
            dcg_ref[:, cols] = (dyc * u * s * dsilu).astype(bf16)
            ds = dyc * u * silu
            dbias_ref[:, h:h + 1] += jnp.sum(ds, axis=1, keepdims=True)
            ds = ds.astype(bf16)
            _accumulate(dw_ref.at[h], jnp.where(tril, _dot_nt(ds, vn_h), 0.0), first)
            dvn_ref[:, cols] = _dot_tn(ws[h], ds)
        dv, dg, db = _ln_bwd(xh, rstd, g, dvn_ref[...])
        dv_ref[...] = dv.astype(bf16)
        _accumulate(dg_ref, dg, first)
        _accumulate(db_ref, db, first)
        full()

    chunk, vec, w, bias = _sgu_specs()
    tiles = pltpu.VMEM((2, CHUNK, HALF), bf16)
    return pl.pallas_call(
        body, name="sgu_bwd", grid=(SEQ // CHUNK,),
        in_specs=[chunk(0), chunk(1), chunk(2), chunk(0), vec, vec, w, bias],
        out_specs=[ANY, w, bias, vec, vec],
        out_shape=[SDS((SEQ, 3 * D_MODEL), bf16), SDS((4, CHUNK, CHUNK), f32), SDS((CHUNK, CHUNK), f32),
                   SDS((1, HALF), f32), SDS((1, HALF), f32)],
        scratch_shapes=[pltpu.VMEM((CHUNK, HALF), f32), tiles, tiles, tiles, pltpu.SemaphoreType.DMA((6,))],
    )(z1, z1, z1, dcat, ln_g, ln_b, sgu_w, bias_t)


CONV_TILE = 128
DVAL_COL, DGLU_COL = 12, 16


def _conv_specs():
    val = pl.BlockSpec((SEQ, POOL_CH), lambda j: (0, DVAL_COL + j))
    glu = pl.BlockSpec((SEQ, POOL_CH), lambda j: (0, DGLU_COL + j))
    w = pl.BlockSpec((CONV_K, POOL_CH), lambda j: (0, j))
    col = pl.BlockSpec((SEQ, POOL_CH), lambda j: (0, j))
    vec = pl.BlockSpec((1, POOL_CH), lambda j: (0, j))
    return val, glu, w, col, vec


def _conv_fwd(z1, conv_w, conv_b):
    def body(val_ref, glu_ref, w_ref, b_ref, out_ref, xpad):
        xpad[0:CONV_PAD, :] = jnp.zeros((CONV_PAD, POOL_CH), f32)
        xpad[CONV_PAD:, :] = val_ref[...] * _sigmoid(glu_ref[...])
        w = w_ref[...]
        bias = b_ref[...]

        def tile(i, carry):
            t0 = pl.multiple_of(i * CONV_TILE, CONV_TILE)
            window = xpad[pl.ds(t0, CONV_TILE + CONV_PAD), :]
            acc = jnp.broadcast_to(bias, (CONV_TILE, POOL_CH))
            for k in range(CONV_K):
                shift = CONV_PAD - (CONV_K - 1) + k
                acc = acc + w[k:k + 1, :] * pltpu.roll(window, CONV_TILE + CONV_PAD - shift, 0)[0:CONV_TILE]
            out_ref[pl.ds(t0, CONV_TILE), :] = acc
            return carry

        lax.fori_loop(0, SEQ // CONV_TILE, tile, 0, unroll=2)

    val, glu, w, col, vec = _conv_specs()
    return pl.pallas_call(
        body, name="conv_fwd", grid=(4,), in_specs=[val, glu, w, vec], out_specs=col,
        out_shape=SDS((SEQ, HALF), f32), scratch_shapes=[pltpu.VMEM((SEQ + CONV_PAD, POOL_CH), f32)],
    )(z1, z1, conv_w, conv_b)


def _conv_bwd(z1, dconv, conv_w, dz):
    def body(val_ref, glu_ref, w_ref, dout_ref, dz_in, dz_ref, dw_ref, db_ref, xpad, dpad, dx_ref, dval_tiles, dglu_tiles, sems):
        j = pl.program_id(0)
        (dval_ref, dglu_ref), full = _write_behind(
            j, 4, [dval_tiles, dglu_tiles], sems,
            lambda t, at: _columns(dz_ref, ((DVAL_COL, DGLU_COL)[t] + at) * POOL_CH, POOL_CH))
        val = val_ref[...]
        sig = _sigmoid(glu_ref[...])
        xpad[0:CONV_PAD, :] = jnp.zeros((CONV_PAD, POOL_CH), f32)
        xpad[CONV_PAD:, :] = val * sig
        dout = dout_ref[...]
        dpad[0:SEQ, :] = dout
        dpad[SEQ:, :] = jnp.zeros((CONV_PAD, POOL_CH), f32)
        db_ref[...] = jnp.sum(dout, axis=0, keepdims=True)
        dw_ref[...] = jnp.zeros((CONV_K, POOL_CH), f32)
        w = w_ref[...]

        def tile(i, carry):
            t0 = pl.multiple_of(i * CONV_TILE, CONV_TILE)
            x_win = xpad[pl.ds(t0, CONV_TILE + CONV_PAD), :]
            d_win = dpad[pl.ds(t0, CONV_TILE + CONV_PAD), :]
            d_own = d_win[0:CONV_TILE]
            acc = jnp.zeros((CONV_TILE, POOL_CH), f32)
            for k in range(CONV_K):
                shift = CONV_PAD - (CONV_K - 1) + k
                x_k = pltpu.roll(x_win, CONV_TILE + CONV_PAD - shift, 0)[0:CONV_TILE]
                dw_ref[k:k + 1, :] += jnp.sum(d_own * x_k, axis=0, keepdims=True)
                back = CONV_K - 1 - k
                d_k = d_own if back == 0 else pltpu.roll(d_win, CONV_TILE + CONV_PAD - back, 0)[0:CONV_TILE]
                acc = acc + w[k:k + 1, :] * d_k
            dx_ref[pl.ds(t0, CONV_TILE), :] = acc
            return carry

        lax.fori_loop(0, SEQ // CONV_TILE, tile, 0, unroll=2)
        dx = dx_ref[...]
        dval_ref[...] = (dx * sig).astype(bf16)
        dglu_ref[...] = (dx * val * sig * (1.0 - sig)).astype(bf16)
        full()

    val, glu, w, col, vec = _conv_specs()
    pad = pltpu.VMEM((SEQ + CONV_PAD, POOL_CH), f32)
    tiles = pltpu.VMEM((2, SEQ, POOL_CH), bf16)
    return pl.pallas_call(
        body, name="conv_bwd", grid=(4,), in_specs=[val, glu, w, col, ANY], out_specs=[ANY, w, vec],
        out_shape=[SDS(dz.shape, dz.dtype), SDS((CONV_K, HALF), f32), SDS((1, HALF), f32)],
        input_output_aliases={4: 0},
        scratch_shapes=[pad, pad, pltpu.VMEM((SEQ, POOL_CH), f32), tiles, tiles, pltpu.SemaphoreType.DMA((4,))],
    )(z1, z1, conv_w, dconv, dz)


DGATE_COL = 5


def _conv_norm_fwd(conv, z1, g, b, mixed):
    def body(c_ref, gate_ref, g_ref, b_ref, mixed_ref, yd_ref):
        xh, _ = _ln_stats(c_ref[...])
        n = xh * g_ref[...] + b_ref[...]
        gate = gate_ref[...]
        yd_ref[...] = (n * _sigmoid(n) * gate * _sigmoid(gate)).astype(bf16)

    return pl.pallas_call(
        body, name="conv_norm_fwd", grid=(SEQ // ROWS,),
        in_specs=[_row_spec(HALF), _row_spec(HALF, DGATE_COL), _vec_spec(HALF), _vec_spec(HALF), ANY],
        out_specs=_row_spec(HALF, 1), out_shape=SDS((SEQ, 2 * HALF), bf16), input_output_aliases={4: 0},
    )(conv, z1, g, b, mixed)


def _conv_norm_bwd(conv, z1, dcat, g, b, dz):
    def body(c_ref, gate_ref, dyd_ref, g_ref, b_ref, dz_ref, dconv_ref, dgate_ref, dg_ref, db_ref):
        first = pl.program_id(0) == 0
        xh, rstd = _ln_stats(c_ref[...])
        g = g_ref[...]
        n_silu, n_dsilu = _silu_and_grad(xh * g + b_ref[...])
        gate_silu, gate_dsilu = _silu_and_grad(gate_ref[...])
        dyd = dyd_ref[...]
        dgate_ref[...] = (dyd * n_silu * gate_dsilu).astype(bf16)
        dconv, dg, db = _ln_bwd(xh, rstd, g, dyd * gate_silu * n_dsilu)
        dconv_ref[...] = dconv
        _accumulate(dg_ref, dg, first)
        _accumulate(db_ref, db, first)

    return pl.pallas_call(
        body, name="conv_norm_bwd", grid=(SEQ // ROWS,),
        in_specs=[_row_spec(HALF), _row_spec(HALF, DGATE_COL), _row_spec(HALF, 1), _vec_spec(HALF), _vec_spec(HALF), ANY],
        out_specs=[_row_spec(HALF), _row_spec(HALF, DGATE_COL), _vec_spec(HALF), _vec_spec(HALF)],
        out_shape=[SDS((SEQ, HALF), f32), SDS(dz.shape, dz.dtype), SDS((1, HALF), f32), SDS((1, HALF), f32)],
        input_output_aliases={5: 1},
    )(conv, z1, dcat, g, b, dz)


def _step(x, target, w, chip):
    chip_vec = chip.astype(jnp.int32).reshape(1)
    sharded_names = list(SHARDED_SMALL)
    first = [_cast_into_slot(w["e_w_in"], chip_vec, "cast_e_w_in0", w["e_pre_norm"], 0, E_IN_PIECES)]
    sems, bufs, token = _gather_start(first, "gather_start_first")
    small_shard = _pack([w[k] for k in sharded_names], total_rows=SMALL_SHARD_ROWS) + 0.0 * token[0, 0]
    small_slot = lax.dynamic_update_slice(jnp.zeros((N_CHIPS, SMALL_SHARD_ROWS, LANES), f32), small_shard[None], (chip, 0, 0))
    more = [small_slot]
    more += [_cast_into_slot(w["e_w_in"], chip_vec, f"cast_e_w_in{i}", token, i, E_IN_PIECES) for i in range(1, E_IN_PIECES)]
    more_sems, more_bufs, token = _gather_start(more, "gather_start_pieces")
    rest = [_cast_into_slot(w[k], chip_vec, f"cast_{k}", token) for k in BIG[1:]]
    rest_sems, rest_bufs, token = _gather_start(rest, "gather_start_rest")
    sems, bufs = sems + more_sems + rest_sems, bufs + more_bufs + rest_bufs
    tables = _rope_tables()

    def vec(k):
        return w[k].reshape(1, -1)

    h0 = _pre_norm(x, vec("e_pre_norm") + token[0, 0])
    after, z0, e_w_in = h0, None, []
    for i in range(E_IN_PIECES):
        group = slice(0, 1) if i == 0 else slice(1, 3) if i == 1 else slice(i + 1, i + 2)
        landed = _forward_halves(_gather_wait(bufs[group], sems[group], after, f"gather_wait_{i}"), f"forward_{i}")
        if i == 1:
            small_full = landed[0]
        e_w_in.append(landed[-1])
        z0 = _mm_nn(h0, landed[-1], f32, f"e_in{i}", i, E_IN_PIECES, z0)
        after = z0
    p = {k: _from_chips(k, a) for k, a in zip(sharded_names, _unpack(small_full, [SHARDED_SMALL[k][0] for k in sharded_names]))}
    for k in ("o_pre_norm", "o_sgu_norm_g", "o_sgu_norm_b", "o_conv_b", "o_conv_norm_g", "o_conv_norm_b", "o_post_norm"):
        p[k] = p[k].reshape(1, -1)
    pool_w_bf = p["e_pool_w"].astype(bf16)
    bias_t = jnp.pad(w["o_sgu_b"].T, ((0, 0), (0, CHUNK - 4)))

    cat0, att, lse, qkv_by_residue = _attn_fwd(z0, tables, _pool_fwd(z0, pool_w_bf, vec("e_pool_scale")))

    def arrived(index, after, name):
        one = slice(index, index + 1)
        return _forward_halves(_gather_wait(bufs[one], sems[one], after, f"gather_wait_{name}"), f"forward_{name}")[0]

    e_w_out = arrived(1 + E_IN_PIECES, att, "e_w_out").reshape(1, D_MODEL, D_MODEL)
    y0 = _mm_nn(cat0, e_w_out, f32, "e_out")
    x1, h1 = _mid_norm(x, y0, vec("e_post_norm"), p["o_pre_norm"])
    o_w_in = arrived(2 + E_IN_PIECES, h1, "o_w_in")
    z1 = _mm_nn(h1, o_w_in, f32, "o_in")
    yc = _sgu_fwd(z1, p["o_sgu_norm_g"], p["o_sgu_norm_b"], w["o_sgu_w"], bias_t)
    conv = _conv_fwd(z1, p["o_conv_w"], p["o_conv_b"])
    cat1 = _conv_norm_fwd(conv, z1, p["o_conv_norm_g"], p["o_conv_norm_b"], yc)
    o_w_out = arrived(3 + E_IN_PIECES, cat1, "o_w_out").reshape(1, D_MODEL, D_MODEL)
    y1 = _mm_nn(cat1, o_w_out, f32, "o_out")
    loss, dx2, dy1, g_o_post = _final_norm_loss(x1, y1, p["o_post_norm"], target)

    in_flight = {}

    def send_off(name, grad):
        sem, sums, land, tok = _scatter_start(_swap_add(grad, f"swap_add_{name}"), f"scatter_start_{name}")
        in_flight[name] = (sem, sums, land)
        return tok

    tok = send_off("o_w_out", _mm_tn(cat1, dy1, 1, "o_out_dw").reshape(N_CHIPS, HALF // 2, D_MODEL))
    dcat1 = _mm_nt(dy1, [o_w_out], "o_out_dx", tok)
    dz1, g_sgu_w, g_bias_t, g_sgu_g, g_sgu_b = _sgu_bwd(
        z1, dcat1, p["o_sgu_norm_g"] + tok[0, 0], p["o_sgu_norm_b"], w["o_sgu_w"], bias_t)
    dconv, dz1, g_cn_g, g_cn_b = _conv_norm_bwd(conv, z1, dcat1, p["o_conv_norm_g"], p["o_conv_norm_b"], dz1)
    dz1, g_conv_w, g_conv_b = _conv_bwd(z1, dconv, p["o_conv_w"], dz1)
    tok = send_off("o_w_in", _mm_tn(h1, dz1, N_CHIPS, "o_in_dw"))
    dh1 = _mm_nt(dz1, [o_w_in], "o_in_dx", tok)
    dx1, dy0, g_o_pre, g_e_post = _mid_norm_bwd(dx2, dh1, x1, y0, p["o_pre_norm"] + tok[0, 0], vec("e_post_norm"))

    tok = send_off("e_w_out", _mm_tn(cat0, dy0, 1, "e_out_dw").reshape(N_CHIPS, HALF // 2, D_MODEL))
    dcat0 = _mm_nt(dy0, [e_w_out], "e_out_dx", tok)
    dz0, g_pool_w, g_pool_scale = _pool_bwd(z0, dcat0, pool_w_bf, vec("e_pool_scale") + tok[0, 0])
    for g in range(len(DILATIONS)):
        dz0 = _attn_bwd_group(g, qkv_by_residue[g], z0, att, lse, dcat0, tables, dz0)
    tok = send_off("e_w_in", _mm_tn(h0, dz0, N_CHIPS, "e_in_dw"))
    dh0 = _mm_nt(dz0, e_w_in, "e_in_dx", tok)
    grad_x, g_e_pre = _pre_norm_bwd(dx1, dh0, x, vec("e_pre_norm") + tok[0, 0])

    small = {"e_pre_norm": g_e_pre, "e_pool_w": g_pool_w, "e_pool_scale": g_pool_scale, "e_post_norm": g_e_post,
             "o_pre_norm": g_o_pre, "o_sgu_norm_g": g_sgu_g, "o_sgu_norm_b": g_sgu_b, "o_sgu_w": g_sgu_w,
             "o_sgu_b": g_bias_t, "o_conv_w": g_conv_w, "o_conv_b": g_conv_b,
             "o_conv_norm_g": g_cn_g, "o_conv_norm_b": g_cn_b, "o_post_norm": g_o_post}
    return loss, grad_x, in_flight, small


def _land(in_flight, name, chip, after):
    sems, sums, land = in_flight[name]
    sums, land = _scatter_wait(sems, sums, land, after, f"scatter_wait_{name}")
    return _add_landed_join(sums, land, chip.astype(jnp.int32).reshape(1), f"add_landed_{name}")


def _place():
    x, y, c = lax.axis_index("x"), lax.axis_index("y"), lax.axis_index("c")
    others = [(1 - x, y), (x, 1 - y), (1 - x, 1 - y)]
    return x, y, c, 2 * x + y, others


SWAP_ROWS = 256
FORWARD_STAGE_BYTES = 4 << 20


def _swap_add(g, name):
    chips, r, c = g.shape
    half = r // 2
    rows_per_step = 2 * SWAP_ROWS if half % (2 * SWAP_ROWS) == 0 else SWAP_ROWS
    nb = half // rows_per_step
    steps = chips * nb

    def body(core_ref, mine_ref, theirs_ref, out_ref, landing, send_sems, recv_sems, free_sems):
        i = pl.program_id(0)
        x, y, core, _, _ = _place()
        sibling = (x, y, 1 - core)

        def send(slot):
            return pltpu.make_async_remote_copy(src_ref=theirs_ref, dst_ref=landing.at[slot], send_sem=send_sems.at[slot],
                                                recv_sem=recv_sems.at[slot], device_id=sibling, device_id_type=MESH)

        @pl.when(i < steps)
        def _():
            @pl.when(i >= 2)
            def _():
                pl.semaphore_wait(free_sems.at[i % 2], 1)

            send(i % 2).start()

        @pl.when(i >= 1)
        def _():
            landed = (i - 1) % 2
            send(landed).wait_recv()
            out_ref[...] = (mine_ref[...].astype(f32) + landing[landed].astype(f32)).astype(out_ref.dtype)

            @pl.when(i + 1 < steps)
            def _():
                pl.semaphore_signal(free_sems.at[landed], 1, device_id=sibling, device_id_type=MESH)

        @pl.when(i < steps)
        def _():
            send(i % 2).wait_send()

    def rows_of(b, h):
        return (2 * (b // nb) + h) * nb + b % nb

    block = (rows_per_step, c)
    grid_spec = pltpu.PrefetchScalarGridSpec(
        num_scalar_prefetch=1, grid=(steps + 1,),
        in_specs=[pl.BlockSpec(block, lambda i, core: (rows_of(jnp.maximum(i - 1, 0), core[0]), 0)),
                  pl.BlockSpec(block, lambda i, core: (rows_of(jnp.minimum(i, steps - 1), 1 - core[0]), 0))],
        out_specs=pl.BlockSpec(block, lambda i, core: (jnp.maximum(i - 1, 0), 0)),
        scratch_shapes=[pltpu.VMEM((2, rows_per_step, c), g.dtype), pltpu.SemaphoreType.DMA((2,)),
                        pltpu.SemaphoreType.DMA((2,)), pltpu.SemaphoreType.REGULAR((2,))])
    core = lax.axis_index("c").astype(jnp.int32).reshape(1)
    rows = g.reshape(chips * r, c)
    out = pl.pallas_call(body, name=name, grid_spec=grid_spec, out_shape=SDS((chips * half, c), g.dtype))(core, rows, rows)
    return out.reshape(chips, half, c)


HBM = pl.BlockSpec(memory_space=pltpu.HBM)
SEM = pl.BlockSpec(memory_space=pltpu.SEMAPHORE)
EFFECT = pltpu.SideEffectType.DATAFLOW_SIDE_EFFECTING


def _in_hbm(a):
    return pltpu.with_memory_space_constraint(a, pltpu.HBM)


def _cast_into_slot(w, chip, name, after, piece=0, pieces=1):
    r, c = w.shape
    c = c // pieces
    nb = r // SWAP_ROWS

    def body(chip_ref, w_ref, after_ref, o_ref):
        o_ref[...] = w_ref[...].astype(bf16)

    grid_spec = pltpu.PrefetchScalarGridSpec(
        num_scalar_prefetch=1, grid=(nb,),
        in_specs=[pl.BlockSpec((SWAP_ROWS, c), lambda i, chip: (i, piece)), ANY],
        out_specs=pl.BlockSpec((SWAP_ROWS, c), lambda i, chip: (chip[0] * nb + i, 0)))
    out = pl.pallas_call(body, name=name, grid_spec=grid_spec, out_shape=SDS((N_CHIPS * r, c), bf16))(chip, w, after)
    return out.reshape(N_CHIPS, r, c)


def _gather_start(bufs, name):
    n = len(bufs)

    def body(*refs):
        ins, sems, token = refs[:n], refs[n:3 * n], refs[4 * n]
        x, y, c, me, others = _place()
        for a in range(n):
            rows = ins[a].shape[1] // 2
            mine = ins[a].at[me, pl.ds(c * rows, rows), :]
            for k, (ox, oy) in enumerate(others):
                pltpu.make_async_remote_copy(src_ref=mine, dst_ref=mine, send_sem=sems[2 * a].at[k],
                                             recv_sem=sems[2 * a + 1].at[k], device_id=(ox, oy, c),
                                             device_id_type=MESH).start()
        token[...] = jnp.zeros_like(token)

    out = pl.pallas_call(
        body, name=name, in_specs=[HBM] * n,
        out_shape=(*[pltpu.SemaphoreType.DMA((3,))] * (2 * n), *[pltpu.HBM(b.shape, b.dtype) for b in bufs],
                   SDS((8, 128), f32)),
        out_specs=(*[SEM] * (2 * n), *[HBM] * n, pl.BlockSpec(memory_space=pltpu.VMEM)),
        input_output_aliases={a: 2 * n + a for a in range(n)},
        compiler_params=pltpu.CompilerParams(has_side_effects=EFFECT),
    )(*[_in_hbm(b) for b in bufs])
    return [(out[2 * a], out[2 * a + 1]) for a in range(n)], list(out[2 * n:3 * n]), out[3 * n]


def _gather_wait(bufs, sems, after, name):
    n = len(bufs)

    def body(*refs):
        ins, sem_refs = refs[:n], refs[n:3 * n]
        x, y, c, me, others = _place()
        for a in range(n):
            rows = ins[a].shape[1] // 2
            mine = ins[a].at[me, pl.ds(c * rows, rows), :]
            for k, (ox, oy) in enumerate(others):
                landed = ins[a].at[2 * ox + oy, pl.ds(c * rows, rows), :]
                copy = pltpu.make_async_remote_copy(src_ref=mine, dst_ref=landed, send_sem=sem_refs[2 * a].at[k],
                                                    recv_sem=sem_refs[2 * a + 1].at[k], device_id=(ox, oy, c),
                                                    device_id_type=MESH)
                copy.wait_send()
                copy.wait_recv()

    flat_sems = [s for pair in sems for s in pair]
    out = pl.pallas_call(
        body, name=name, in_specs=[HBM] * n + [SEM] * (2 * n) + [ANY],
        out_shape=tuple(pltpu.HBM(b.shape, b.dtype) for b in bufs), out_specs=tuple([HBM] * n),
        input_output_aliases={a: a for a in range(n)},
        compiler_params=pltpu.CompilerParams(has_side_effects=EFFECT),
    )(*bufs, *flat_sems, after)
    return list(out)


def _forward_halves(bufs, name):
    n = len(bufs)
    blocks = []
    for b in bufs:
        half = b.shape[1] // 2
        whole = half * b.shape[2] * b.dtype.itemsize <= FORWARD_STAGE_BYTES
        blocks.append((half, half if whole or half % SWAP_ROWS else SWAP_ROWS))
    work = [(a, k, b) for a in range(n) for k in range(3) for b in range(blocks[a][0] // blocks[a][1])]

    def body(*refs):
        outs, stages = refs[n:2 * n], refs[2 * n:3 * n]
        load_sems, send_sems, recv_sems = refs[3 * n:]
        x, y, c, me, others = _place()
        sibling = (x, y, 1 - c)

        def rows(item):
            a, k, b = item
            half, tr = blocks[a]
            ox, oy = others[k]
            return outs[a].at[2 * ox + oy, pl.ds(c * half + b * tr, tr), :]

        def load(s, item):
            return pltpu.make_async_copy(rows(item), stages[item[0]].at[s], load_sems.at[s])

        def send(s, item):
            return pltpu.make_async_remote_copy(src_ref=stages[item[0]].at[s], dst_ref=rows(item), send_sem=send_sems.at[s],
                                                recv_sem=recv_sems.at[item[0]], device_id=sibling, device_id_type=MESH)

        load(0, work[0]).start()
        for t, item in enumerate(work):
            s = t % 2
            load(s, item).wait()
            send(s, item).start()
            if t + 1 < len(work):
                if t >= 1:
                    send(1 - s, work[t - 1]).wait_send()
                load(1 - s, work[t + 1]).start()
        if len(work) > 1:
            send(len(work) % 2, work[-2]).wait_send()
        send((len(work) - 1) % 2, work[-1]).wait_send()
        for a in range(n):
            theirs = outs[a].at[pl.ds(0, 3), pl.ds(0, blocks[a][0]), :]
            pltpu.make_async_remote_copy(src_ref=theirs, dst_ref=theirs, send_sem=send_sems.at[0], recv_sem=recv_sems.at[a],
                                         device_id=sibling, device_id_type=MESH).wait_recv()

    out = pl.pallas_call(
        body, name=name, in_specs=[ANY] * n, out_specs=[ANY] * n, out_shape=[SDS(b.shape, b.dtype) for b in bufs],
        input_output_aliases={a: a for a in range(n)},
        scratch_shapes=[pltpu.VMEM((2, blocks[a][1], bufs[a].shape[2]), bufs[a].dtype) for a in range(n)]
        + [pltpu.SemaphoreType.DMA((2,)), pltpu.SemaphoreType.DMA((2,)), pltpu.SemaphoreType.DMA((n,))],
    )(*bufs)
    return list(out)


def _scatter_start(chip_sums, name):
    def body(a_ref, land_ref, send_sems, recv_sems, a_thru, land_thru, token):
        x, y, c, me, others = _place()
        for k, (ox, oy) in enumerate(others):
            pltpu.make_async_remote_copy(src_ref=a_ref.at[2 * ox + oy], dst_ref=land_ref.at[me], send_sem=send_sems.at[k],
                                         recv_sem=recv_sems.at[k], device_id=(ox, oy, c), device_id_type=MESH).start()
        token[...] = jnp.zeros_like(token)

    shape = pltpu.HBM(chip_sums.shape, chip_sums.dtype)
    send, recv, a_thru, land, token = pl.pallas_call(
        body, name=name, in_specs=[HBM, HBM],
        out_shape=(pltpu.SemaphoreType.DMA((3,)), pltpu.SemaphoreType.DMA((3,)), shape, shape, SDS((8, 128), f32)),
        out_specs=(SEM, SEM, HBM, HBM, pl.BlockSpec(memory_space=pltpu.VMEM)), input_output_aliases={0: 2, 1: 3},
        compiler_params=pltpu.CompilerParams(has_side_effects=EFFECT),
    )(_in_hbm(chip_sums), _in_hbm(lax.empty(chip_sums.shape, chip_sums.dtype)))
    return (send, recv), a_thru, land, token


def _scatter_wait(sems, chip_sums, land, after, name):
    def body(a_ref, land_ref, send_sems, recv_sems, after_ref, a_out, land_out):
        x, y, c, me, others = _place()
        for k, (ox, oy) in enumerate(others):
            copy = pltpu.make_async_remote_copy(
                src_ref=a_ref.at[2 * ox + oy], dst_ref=land_ref.at[2 * ox + oy], send_sem=send_sems.at[k],
                recv_sem=recv_sems.at[k], device_id=(ox, oy, c), device_id_type=MESH)
            copy.wait_send()
            copy.wait_recv()

    shape = pltpu.HBM(chip_sums.shape, chip_sums.dtype)
    return pl.pallas_call(
        body, name=name, in_specs=[HBM, HBM, SEM, SEM, ANY], out_shape=(shape, shape), out_specs=(HBM, HBM),
        input_output_aliases={0: 0, 1: 1}, compiler_params=pltpu.CompilerParams(has_side_effects=EFFECT),
    )(chip_sums, land, sems[0], sems[1], after)


def _add_landed_join(chip_sums, land, chip, name):
    chips, rh, c = chip_sums.shape
    nb = rh // SWAP_ROWS

    def body(chip_ref, own_ref, l1_ref, l2_ref, l3_ref, out_hbm, buf, send_sems, recv_sem, local_sems):
        i = pl.program_id(0)
        slot = i % 2
        x, y, core, _, _ = _place()
        sibling = (x, y, 1 - core)

        def copies(s, step):
            rows = pl.ds(pl.multiple_of((core * nb + step) * SWAP_ROWS, SWAP_ROWS), SWAP_ROWS)
            keep = pltpu.make_async_copy(buf.at[s], out_hbm.at[rows, :], local_sems.at[s])
            give = pltpu.make_async_remote_copy(src_ref=buf.at[s], dst_ref=out_hbm.at[rows, :], send_sem=send_sems.at[s],
                                                recv_sem=recv_sem.at[0], device_id=sibling, device_id_type=MESH)
            return keep, give

        def drain(s, step):
            keep, give = copies(s, step)
            keep.wait()
            give.wait_send()

        @pl.when(i >= 2)
        def _():
            drain(slot, i - 2)

        buf[slot] = ((own_ref[...].astype(f32) + l1_ref[...].astype(f32)) + l2_ref[...].astype(f32)) + l3_ref[...].astype(f32)
        keep, give = copies(slot, i)
        keep.start()
        give.start()

        @pl.when(i == nb - 1)
        def _():
            drain(slot, i)
            if nb > 1:
                drain(1 - slot, i - 1)
            theirs = out_hbm.at[pl.ds((1 - core) * rh, rh), :]
            pltpu.make_async_remote_copy(src_ref=theirs, dst_ref=theirs, send_sem=send_sems.at[0], recv_sem=recv_sem.at[0],
                                         device_id=sibling, device_id_type=MESH).wait_recv()

    block = (SWAP_ROWS, c)
    from_slot = lambda d: pl.BlockSpec(block, lambda i, chip: (((chip[0] + d) % chips) * nb + i, 0))
    grid_spec = pltpu.PrefetchScalarGridSpec(
        num_scalar_prefetch=1, grid=(nb,), in_specs=[from_slot(0), from_slot(1), from_slot(2), from_slot(3)],
        out_specs=ANY,
        scratch_shapes=[pltpu.VMEM((2, SWAP_ROWS, c), f32), pltpu.SemaphoreType.DMA((2,)),
                        pltpu.SemaphoreType.DMA((1,)), pltpu.SemaphoreType.DMA((2,))])
    land_rows = land.reshape(chips * rh, c)
    return pl.pallas_call(body, name=name, grid_spec=grid_spec, out_shape=SDS((2 * rh, c), f32))(
        chip, chip_sums.reshape(chips * rh, c), land_rows, land_rows, land_rows)


def _adamw_update(w_ref, g_ref, m_ref, v_ref, d_ref, nm_ref, nv_ref):
    g = g_ref[...]
    nm = ADAM_B1 * m_ref[...] + (1.0 - ADAM_B1) * g
    nv = ADAM_B2 * v_ref[...] + (1.0 - ADAM_B2) * (g * g)
    nm_ref[...] = nm
    nv_ref[...] = nv
    m_hat = nm / (1.0 - ADAM_B1 ** ADAM_STEP)
    v_hat = nv / (1.0 - ADAM_B2 ** ADAM_STEP)
    d_ref[...] = -ADAM_LR * (m_hat / (jnp.sqrt(v_hat) + ADAM_EPS) + ADAM_WD * w_ref[...])


def _adamw(w, g, m, v, name):
    r, c = w.shape
    tr = 128 if r % 128 == 0 else r

    def body(w_ref, g_ref, m_ref, v_ref, g_out_ref, d_ref, nm_ref, nv_ref):
        g_out_ref[...] = g_ref[...]
        _adamw_update(w_ref, g_ref, m_ref, v_ref, d_ref, nm_ref, nv_ref)

    spec = pl.BlockSpec((tr, c), lambda i: (i, 0))
    return pl.pallas_call(body, name=name, grid=(r // tr,), in_specs=[spec] * 4, out_specs=[spec] * 4,
                          out_shape=[SDS((r, c), f32)] * 4)(w, g, m, v)


SMALL_PACKING = {
    "e_pre_norm": ((1, 2048), 8, (1, 2048)), "e_pool_w": ((1024, 256), 1024, (256, 256)),
    "e_pool_scale": ((1, 1024), 8, (1, 1024)), "e_post_norm": ((1, 2048), 8, (1, 2048)),
    "o_pre_norm": ((1, 2048), 8, (1, 512)), "o_sgu_norm_g": ((1, 1024), 8, (1, 256)),
    "o_sgu_norm_b": ((1, 1024), 8, (1, 256)), "o_sgu_w": ((512, 128), 512, (512, 128)),
    "o_sgu_b": ((128, 128), 8, (4, 128)), "o_conv_w": ((31, 1024), 128, (31, 256)), "o_conv_b": ((1, 1024), 8, (1, 256)),
    "o_conv_norm_g": ((1, 1024), 8, (1, 256)), "o_conv_norm_b": ((1, 1024), 8, (1, 256)),
    "o_post_norm": ((1, 2048), 8, (1, 512)),
}
SMALL_PACKED_ROWS = 1792


def _small_finalize(grads, ws, ms, vs, after):
    names = list(SMALL_ORDER)
    n = len(names)
    half, piece = SMALL_PACKED_ROWS // 2, SMALL_PACKED_ROWS // 8
    first_row, row = {}, 0
    for k in names:
        first_row[k] = row
        row += SMALL_PACKING[k][1]

    def body(*refs):
        g_refs, total = refs[0:n], refs[n + 1]
        pack, from_sibling, from_chips, send_a, recv_a, send_b, recv_b, send_c, recv_c, send_d, recv_d = refs[n + 2:]
        x, y, c, me, others = _place()

        for r0 in range(0, SMALL_PACKED_ROWS, piece):
            pack[r0:r0 + piece, :] = jnp.zeros((piece, LANES), f32)
        for k, g_ref in zip(names, g_refs):
            (rows, width), _, _ = SMALL_PACKING[k]
            r0 = first_row[k]
            if k == "o_sgu_b":
                pack[r0:r0 + 4, 0:CHUNK] = g_ref[...].T[0:4, :]
            elif width < LANES:
                pack[r0:r0 + rows, 0:width] = g_ref[...]
            else:
                for j in range(width // LANES):
                    dst = r0 + j * (1 if rows == 1 else 32)
                    pack[dst:dst + rows, :] = g_ref[:, j * LANES:(j + 1) * LANES]

        sibling = (x, y, 1 - c)
        swap = pltpu.make_async_remote_copy(
            src_ref=pack.at[pl.ds(pl.multiple_of((1 - c) * half, 8), half), :], dst_ref=from_sibling,
            send_sem=send_a.at[0], recv_sem=recv_a.at[0], device_id=sibling, device_id_type=MESH)
        swap.start()
        swap.wait()
        for j in range(4):
            rows = pl.ds(pl.multiple_of(c * half + j * piece, 8), piece)
            pack[rows, :] = pack[rows, :] + from_sibling[j * piece:(j + 1) * piece, :]

        def piece_of(chip):
            return pl.ds(pl.multiple_of(c * half + chip * piece, 8), piece)

        def to_chip(k):
            ox, oy = others[k]
            return pltpu.make_async_remote_copy(
                src_ref=pack.at[piece_of(2 * ox + oy), :], dst_ref=from_chips.at[me], send_sem=send_b.at[k],
                recv_sem=recv_b.at[k], device_id=(ox, oy, c), device_id_type=MESH)

        for k in range(3):
            to_chip(k).start()
        from_chips[me] = pack[piece_of(me), :]
        for k, (ox, oy) in enumerate(others):
            landed = from_chips.at[2 * ox + oy]
            pltpu.make_async_remote_copy(src_ref=landed, dst_ref=landed, send_sem=send_b.at[k], recv_sem=recv_b.at[k],
                                         device_id=(ox, oy, c), device_id_type=MESH).wait_recv()
        for k in range(3):
            to_chip(k).wait_send()
        mine = pl.ds(pl.multiple_of(c * half + me * piece, 8), piece)
        total[mine, :] = ((from_chips[0] + from_chips[1]) + from_chips[2]) + from_chips[3]

        def to_same_core(k):
            ox, oy = others[k]
            return pltpu.make_async_remote_copy(
                src_ref=total.at[mine, :], dst_ref=total.at[mine, :], send_sem=send_c.at[k], recv_sem=recv_c.at[k],
                device_id=(ox, oy, c), device_id_type=MESH)

        for k in range(3):
            to_same_core(k).start()
        for k, (ox, oy) in enumerate(others):
            theirs = total.at[piece_of(2 * ox + oy), :]
            pltpu.make_async_remote_copy(src_ref=theirs, dst_ref=theirs, send_sem=send_c.at[k], recv_sem=recv_c.at[k],
                                         device_id=(ox, oy, c), device_id_type=MESH).wait_recv()
        for k in range(3):
            to_same_core(k).wait_send()
        my_half = total.at[pl.ds(pl.multiple_of(c * half, 8), half), :]
        join = pltpu.make_async_remote_copy(src_ref=my_half, dst_ref=my_half, send_sem=send_d.at[0], recv_sem=recv_d.at[0],
                                            device_id=sibling, device_id_type=MESH)
        join.start()
        their_half = total.at[pl.ds(pl.multiple_of((1 - c) * half, 8), half), :]
        pltpu.make_async_remote_copy(src_ref=their_half, dst_ref=their_half, send_sem=send_d.at[0], recv_sem=recv_d.at[0],
                                     device_id=sibling, device_id_type=MESH).wait_recv()
        join.wait_send()

    whole = pl.BlockSpec(memory_space=pltpu.VMEM)
    total = pl.pallas_call(
        body, name="small_allreduce", in_specs=[whole] * n + [ANY], out_specs=whole,
        out_shape=SDS((SMALL_PACKED_ROWS, LANES), f32),
        scratch_shapes=[pltpu.VMEM((SMALL_PACKED_ROWS, LANES), f32), pltpu.VMEM((half, LANES), f32),
                        pltpu.VMEM((N_CHIPS, piece, LANES), f32),
                        pltpu.SemaphoreType.DMA((1,)), pltpu.SemaphoreType.DMA((1,)), pltpu.SemaphoreType.DMA((3,)),
                        pltpu.SemaphoreType.DMA((3,)), pltpu.SemaphoreType.DMA((3,)), pltpu.SemaphoreType.DMA((3,)),
                        pltpu.SemaphoreType.DMA((1,)), pltpu.SemaphoreType.DMA((1,))],
    )(*grads, after)

    def update(*refs):
        total = refs[0]
        w_refs, m_refs, v_refs = refs[1:n + 1], refs[n + 1:2 * n + 1], refs[2 * n + 1:3 * n + 1]
        outs = refs[3 * n + 1:]
        me = 2 * lax.axis_index("x") + lax.axis_index("y")

        def of_chip(candidates):
            value = candidates[0]
            for j in range(1, N_CHIPS):
                value = jnp.where(me == j, candidates[j], value)
            return value

        for i, k in enumerate(names):
            (rows, width), _, (local_rows, local_width) = SMALL_PACKING[k]
            r0 = first_row[k]
            if k == "o_sgu_b":
                g = total[r0:r0 + 4, 0:CHUNK]
            elif k == "e_pool_w":
                for grp in range(4):
                    src = pl.ds(pl.multiple_of(r0 + grp * POOL_CH + me * 64, 8), 64)
                    dst = slice(grp * 64, (grp + 1) * 64)
                    _adamw_rows(total[src, :], i, dst, w_refs, m_refs, v_refs, outs, n)
                continue
            elif k == "o_conv_w":
                g = total[pl.ds(pl.multiple_of(r0 + me * 32, 8), 32), :][0:CONV_K]
            elif width < LANES:
                g = total[r0:r0 + rows, 0:width]
            else:
                lanes = [total[r0 + j:r0 + j + 1, :] for j in range(width // LANES)]
                per_chip = local_width // LANES
                if local_width == width:
                    g = jnp.concatenate(lanes, axis=1)
                elif per_chip == 1:
                    g = of_chip(lanes)
                else:
                    g = of_chip([jnp.concatenate(lanes[j * per_chip:(j + 1) * per_chip], axis=1) for j in range(N_CHIPS)])
            _adamw_rows(g, i, slice(None), w_refs, m_refs, v_refs, outs, n)

    shard_shapes = [SMALL_PACKING[k][2] for k in names]
    out = pl.pallas_call(update, name="small_update", in_specs=[whole] * (3 * n + 1), out_specs=[whole] * (4 * n),
                         out_shape=[SDS(s, f32) for s in shard_shapes] * 4)(total, *ws, *ms, *vs)
    return out[:n], out[n:2 * n], out[2 * n:3 * n], out[3 * n:]


def _adamw_rows(g, i, rows, w_refs, m_refs, v_refs, outs, n):
    w, m, v = w_refs[i][rows, :], m_refs[i][rows, :], v_refs[i][rows, :]
    nm = ADAM_B1 * m + (1.0 - ADAM_B1) * g
    nv = ADAM_B2 * v + (1.0 - ADAM_B2) * (g * g)
    m_hat = nm / (1.0 - ADAM_B1 ** ADAM_STEP)
    v_hat = nv / (1.0 - ADAM_B2 ** ADAM_STEP)
    outs[i][rows, :] = g
    outs[n + i][rows, :] = -ADAM_LR * (m_hat / (jnp.sqrt(v_hat) + ADAM_EPS) + ADAM_WD * w)
    outs[2 * n + i][rows, :] = nm
    outs[3 * n + i][rows, :] = nv


def _pack(arrays, total_rows=None):
    parts = []
    rows = 0
    for a in arrays:
        flat = a.reshape(-1, LANES)
        pad = -flat.shape[0] % 8
        parts.append(jnp.pad(flat, ((0, pad), (0, 0))))
        rows += flat.shape[0] + pad
    if total_rows is not None:
        parts.append(jnp.zeros((total_rows - rows, LANES), arrays[0].dtype))
    return jnp.concatenate(parts, axis=0)


def _unpack(buf, shapes):
    out = []
    row = 0
    lead = buf.shape[:-2]
    for shape in shapes:
        size = 1
        for s in shape:
            size *= s
        rows = size // LANES
        out.append(buf[..., row:row + rows, :].reshape(lead + tuple(shape)))
        row += rows + (-rows % 8)
    return out


BIG = ("e_w_in", "e_w_out", "o_w_in", "o_w_out")
SHARDED_SMALL = {
    "e_pool_w": ((4, 64, 256), 1), "o_pre_norm": ((512,), 0), "o_sgu_norm_g": ((256,), 0), "o_sgu_norm_b": ((256,), 0),
    "o_conv_w": ((31, 256), 1), "o_conv_b": ((256,), 0), "o_conv_norm_g": ((256,), 0), "o_conv_norm_b": ((256,), 0),
    "o_post_norm": ((512,), 0),
}
SMALL_ORDER = ("e_pre_norm", "e_pool_w", "e_pool_scale", "e_post_norm", "o_pre_norm", "o_sgu_norm_g", "o_sgu_norm_b",
               "o_sgu_w", "o_sgu_b", "o_conv_w", "o_conv_b", "o_conv_norm_g", "o_conv_norm_b", "o_post_norm")
ALL_ORDER = ("e_pre_norm", "e_w_in", "e_pool_w", "e_pool_scale", "e_w_out", "e_post_norm", "o_pre_norm", "o_w_in",
             "o_sgu_norm_g", "o_sgu_norm_b", "o_sgu_w", "o_sgu_b", "o_conv_w", "o_conv_b", "o_conv_norm_g",
             "o_conv_norm_b", "o_w_out", "o_post_norm")


def _full_shape(name):
    shape, axis = SHARDED_SMALL[name]
    return tuple(s * N_CHIPS if i == axis else s for i, s in enumerate(shape))


def _from_chips(name, stacked):
    shape, axis = SHARDED_SMALL[name]
    return jnp.moveaxis(stacked, 0, axis).reshape(_full_shape(name))


def kernel(x, e_pre_norm, e_w_in, e_pool_w, e_pool_scale, e_w_out, e_post_norm, o_pre_norm, o_w_in, o_sgu_norm_g, o_sgu_norm_b, o_sgu_w, o_sgu_b, o_conv_w, o_conv_b, o_conv_norm_g, o_conv_norm_b, o_w_out, o_post_norm, loss_target, m_e_pre_norm, m_e_w_in, m_e_pool_w, m_e_pool_scale, m_e_w_out, m_e_post_norm, m_o_pre_norm, m_o_w_in, m_o_sgu_norm_g, m_o_sgu_norm_b, m_o_sgu_w, m_o_sgu_b, m_o_conv_w, m_o_conv_b, m_o_conv_norm_g, m_o_conv_norm_b, m_o_w_out, m_o_post_norm, v_e_pre_norm, v_e_w_in, v_e_pool_w, v_e_pool_scale, v_e_w_out, v_e_post_norm, v_o_pre_norm, v_o_w_in, v_o_sgu_norm_g, v_o_sgu_norm_b, v_o_sgu_w, v_o_sgu_b, v_o_conv_w, v_o_conv_b, v_o_conv_norm_g, v_o_conv_norm_b, v_o_w_out, v_o_post_norm):
    w = dict(e_pre_norm=e_pre_norm, e_w_in=e_w_in, e_pool_w=e_pool_w, e_pool_scale=e_pool_scale, e_w_out=e_w_out,
             e_post_norm=e_post_norm, o_pre_norm=o_pre_norm, o_w_in=o_w_in, o_sgu_norm_g=o_sgu_norm_g,
             o_sgu_norm_b=o_sgu_norm_b, o_sgu_w=o_sgu_w, o_sgu_b=o_sgu_b, o_conv_w=o_conv_w, o_conv_b=o_conv_b,
             o_conv_norm_g=o_conv_norm_g, o_conv_norm_b=o_conv_norm_b, o_w_out=o_w_out, o_post_norm=o_post_norm)
    m = dict(e_pre_norm=m_e_pre_norm, e_w_in=m_e_w_in, e_pool_w=m_e_pool_w, e_pool_scale=m_e_pool_scale,
             e_w_out=m_e_w_out, e_post_norm=m_e_post_norm, o_pre_norm=m_o_pre_norm, o_w_in=m_o_w_in,
             o_sgu_norm_g=m_o_sgu_norm_g, o_sgu_norm_b=m_o_sgu_norm_b, o_sgu_w=m_o_sgu_w, o_sgu_b=m_o_sgu_b,
             o_conv_w=m_o_conv_w, o_conv_b=m_o_conv_b, o_conv_norm_g=m_o_conv_norm_g, o_conv_norm_b=m_o_conv_norm_b,
             o_w_out=m_o_w_out, o_post_norm=m_o_post_norm)
    v = dict(e_pre_norm=v_e_pre_norm, e_w_in=v_e_w_in, e_pool_w=v_e_pool_w, e_pool_scale=v_e_pool_scale,
             e_w_out=v_e_w_out, e_post_norm=v_e_post_norm, o_pre_norm=v_o_pre_norm, o_w_in=v_o_w_in,
             o_sgu_norm_g=v_o_sgu_norm_g, o_sgu_norm_b=v_o_sgu_norm_b, o_sgu_w=v_o_sgu_w, o_sgu_b=v_o_sgu_b,
             o_conv_w=v_o_conv_w, o_conv_b=v_o_conv_b, o_conv_norm_g=v_o_conv_norm_g, o_conv_norm_b=v_o_conv_norm_b,
             o_w_out=v_o_w_out, o_post_norm=v_o_post_norm)
    w, m, v = ({k: a[0] for k, a in d.items()} for d in (w, m, v))
    chip = 2 * lax.axis_index("x") + lax.axis_index("y")

    loss, grad_x, in_flight, small = _step(x[0], loss_target[0], w, chip)

    grads, delta, new_m, new_v = {}, {}, {}, {}

    def rows_of(a):
        return a.reshape(-1, a.shape[-1])

    after = grad_x
    for k in ("o_w_out", "o_w_in", "e_w_out", "small", "e_w_in"):
        if k == "small":
            small_grads = [small[name].reshape(SMALL_PACKING[name][0]) for name in SMALL_ORDER]
            updates = _small_finalize(small_grads, *[[rows_of(d[name]) for name in SMALL_ORDER] for d in (w, m, v)], after)
            for d, arrays in zip((grads, delta, new_m, new_v), updates):
                for name, a in zip(SMALL_ORDER, arrays):
                    d[name] = a.reshape(w[name].shape)
            after = updates[1][0]
            continue
        grads[k], delta[k], new_m[k], new_v[k] = _adamw(w[k], _land(in_flight, k, chip, after), m[k], v[k], f"adamw_{k}")
        after = delta[k]
    loss = lax.psum(loss[0, 0], ("x", "y", "c"))

    outs = [loss, grad_x[None]]
    for d in (grads, delta, new_m, new_v):
        outs += [d[k][None] for k in ALL_ORDER]
    return tuple(outs)
```

```python
import jax
import jax.numpy as jnp
from jax import lax
from jax.experimental import pallas as pl
from jax.experimental.pallas import tpu as pltpu

f32 = jnp.float32
bf16 = jnp.bfloat16
SDS = jax.ShapeDtypeStruct

SEQ = 2048
D_MODEL = 2048
EPS = 1e-6
NEG = -1e30
HEAD_DIM = 128
ROT_HALF = 16
ROPE_THETA = 500000.0
DILATIONS = (1, 4, 16)
SPAN = 128
N_HEADS = 8
HALF = 1024
POOL_CH = 256
CONV_K = 31
CONV_PAD = 32
CHUNK = 128
N_CHIPS = 4
LANES = 256
E_IN_PIECES = 3
SMALL_SHARD_ROWS = 352
ANY = pl.BlockSpec(memory_space=pl.ANY)
MESH = pl.DeviceIdType.MESH

ADAM_LR = 0.001
ADAM_B1 = 0.9
ADAM_B2 = 0.999
ADAM_EPS = 1e-08
ADAM_WD = 0.01
ADAM_STEP = 10


def _dot(a, b):
    return jnp.dot(a, b, preferred_element_type=f32)


def _dot_nt(a, b):
    return lax.dot_general(a, b, (((1,), (1,)), ((), ())), preferred_element_type=f32)


def _dot_tn(a, b):
    return lax.dot_general(a, b, (((0,), (0,)), ((), ())), preferred_element_type=f32)


def _sigmoid(x):
    return 1.0 / (1.0 + jnp.exp(-x))


def _silu_and_grad(x):
    s = _sigmoid(x)
    return x * s, s * (1.0 + x * (1.0 - s))


def _rms_fwd(x, g):
    r = lax.rsqrt(jnp.mean(x * x, axis=-1, keepdims=True) + EPS)
    return x * r * g


def _rms_bwd(x, g, dout):
    r = lax.rsqrt(jnp.mean(x * x, axis=-1, keepdims=True) + EPS)
    xh = x * r
    dg = jnp.sum(dout * xh, axis=0, keepdims=True)
    dxh = dout * g
    dx = r * (dxh - xh * jnp.mean(dxh * xh, axis=-1, keepdims=True))
    return dx, dg


def _ln_stats(x):
    mu = jnp.mean(x, axis=-1, keepdims=True)
    xc = x - mu
    rstd = lax.rsqrt(jnp.mean(xc * xc, axis=-1, keepdims=True) + EPS)
    return xc * rstd, rstd


def _ln_bwd(xh, rstd, g, dout):
    dg = jnp.sum(dout * xh, axis=0, keepdims=True)
    db = jnp.sum(dout, axis=0, keepdims=True)
    dxh = dout * g
    dx = rstd * (dxh - jnp.mean(dxh, axis=-1, keepdims=True) - xh * jnp.mean(dxh * xh, axis=-1, keepdims=True))
    return dx, dg, db


def _accumulate(ref, value, first):
    @pl.when(first)
    def _():
        ref[...] = value

    @pl.when(jnp.logical_not(first))
    def _():
        ref[...] += value


def _write_behind(step, steps, tiles, sems, window):
    slot = step % 2

    def copies(s, at):
        return [pltpu.make_async_copy(tile.at[s], window(t, at), sems.at[2 * t + s]) for t, tile in enumerate(tiles)]

    @pl.when(step >= 2)
    def _():
        for cp in copies(slot, step - 2):
            cp.wait()

    def full():
        for cp in copies(slot, step):
            cp.start()

        @pl.when(step == steps - 1)
        def _():
            for cp in copies(slot, step):
                cp.wait()
            if steps > 1:
                for cp in copies(1 - slot, step - 1):
                    cp.wait()

    return [tile.at[slot] for tile in tiles], full


def _columns(ref, first, width):
    return ref.at[:, pl.ds(pl.multiple_of(first, 128), width)]


def _col_tile(ns):
    for t in (1024, 768, 512, 256):
        if ns % t == 0:
            return t
    raise ValueError(ns)


def _mm_nn(a, w, out_dtype, name, piece=0, pieces=1, into=None):
    m, k = a.shape
    j, _, ns = w.shape
    tm, tn = m, _col_tile(ns)
    nb = ns // tn

    def body(a_ref, w_ref, *rest):
        rest[-1][...] = _dot(a_ref[...], w_ref[...]).astype(rest[-1].dtype)

    return pl.pallas_call(
        body, name=name, grid=(j * nb, m // tm),
        in_specs=[pl.BlockSpec((tm, k), lambda n, i: (i, 0)),
                  pl.BlockSpec((None, k, tn), lambda n, i: (n // nb, 0, n % nb))] + ([] if into is None else [ANY]),
        out_specs=pl.BlockSpec((tm, tn), lambda n, i: (i, ((n // nb) * pieces + piece) * nb + n % nb)),
        out_shape=SDS((m, j * ns * pieces), out_dtype),
        input_output_aliases={} if into is None else {2: 0},
    )(a, w, *([] if into is None else [into]))


def _mm_nt(dz, ws, name, after):
    m, _ = dz.shape
    pieces = len(ws)
    j, k, ns = ws[0].shape
    tm, tk = 1024, 1024

    def body(dz_ref, *rest):
        w_refs, o_ref = rest[:pieces], rest[pieces + 1]
        total = _dot_nt(dz_ref[:, 0:ns], w_refs[0][...])
        for q in range(1, pieces):
            total = total + _dot_nt(dz_ref[:, q * ns:(q + 1) * ns], w_refs[q][...])
        if j == 1:
            o_ref[...] = total.astype(bf16)
            return
        sum_ref = rest[pieces + 2]
        r = pl.program_id(2)
        _accumulate(sum_ref, total, r == 0)

        @pl.when(r == j - 1)
        def _():
            o_ref[...] = sum_ref[...].astype(bf16)

    return pl.pallas_call(
        body, name=name, grid=(m // tm, k // tk, j),
        in_specs=[pl.BlockSpec((tm, pieces * ns), lambda i, kk, r: (i, r))]
        + [pl.BlockSpec((None, tk, ns), lambda i, kk, r: (r, kk, 0))] * pieces + [ANY],
        out_specs=pl.BlockSpec((tm, tk), lambda i, kk, r: (i, kk)),
        out_shape=SDS((m, k), bf16), scratch_shapes=[] if j == 1 else [pltpu.VMEM((tm, tk), f32)],
    )(dz, *ws, after)


def _mm_tn(a, dz, j, name):
    m, k = a.shape
    ns = dz.shape[1] // j
    tk, tn = 1024, _col_tile(ns)
    nb = ns // tn

    def body(a_ref, dz_ref, o_ref):
        o_ref[...] = _dot_tn(a_ref[...], dz_ref[...]).astype(o_ref.dtype)

    return pl.pallas_call(
        body, name=name, grid=(k // tk, j * nb),
        in_specs=[pl.BlockSpec((m, tk), lambda kk, n: (0, kk)),
                  pl.BlockSpec((m, tn), lambda kk, n: (0, n))],
        out_specs=pl.BlockSpec((None, tk, tn), lambda kk, n: (n // nb, kk, n % nb)),
        out_shape=SDS((j, k, ns), bf16),
    )(a, dz)


ROWS = 512


def _row_spec(width=D_MODEL, col=0):
    return pl.BlockSpec((ROWS, width), lambda i: (i, col))


def _vec_spec(width=D_MODEL):
    return pl.BlockSpec((1, width), lambda i: (0, 0))


def _pre_norm(x, g):
    def body(x_ref, g_ref, h_ref):
        h_ref[...] = _rms_fwd(x_ref[...], g_ref[...]).astype(bf16)

    return pl.pallas_call(
        body, name="pre_norm", grid=(SEQ // ROWS,), in_specs=[_row_spec(), _vec_spec()],
        out_specs=_row_spec(), out_shape=SDS((SEQ, D_MODEL), bf16))(x, g)


def _mid_norm(x, y, g_post, g_pre):
    def body(x_ref, y_ref, gpost_ref, gpre_ref, x1_ref, h1_ref):
        x1 = x_ref[...] + _rms_fwd(y_ref[...], gpost_ref[...])
        x1_ref[...] = x1
        h1_ref[...] = _rms_fwd(x1, gpre_ref[...]).astype(bf16)

    return pl.pallas_call(
        body, name="mid_norm", grid=(SEQ // ROWS,),
        in_specs=[_row_spec(), _row_spec(), _vec_spec(), _vec_spec()],
        out_specs=[_row_spec(), _row_spec()],
        out_shape=[SDS((SEQ, D_MODEL), f32), SDS((SEQ, D_MODEL), bf16)])(x, y, g_post, g_pre)


def _final_norm_loss(x1, y, g_post, target):
    def body(x1_ref, y_ref, g_ref, t_ref, loss_ref, dx2_ref, dy_ref, dg_ref):
        first = pl.program_id(0) == 0
        y = y_ref[...]
        g = g_ref[...]
        err = x1_ref[...] + _rms_fwd(y, g) - t_ref[...]
        sq = jnp.sum(jnp.sum(err * err, axis=1, keepdims=True), axis=0, keepdims=True)
        _accumulate(loss_ref, sq * (0.5 / D_MODEL), first)
        dx2 = err * (1.0 / D_MODEL)
        dx2_ref[...] = dx2
        dy, dg = _rms_bwd(y, g, dx2)
        dy_ref[...] = dy.astype(bf16)
        _accumulate(dg_ref, dg, first)

    return pl.pallas_call(
        body, name="final_norm_loss", grid=(SEQ // ROWS,),
        in_specs=[_row_spec(), _row_spec(), _vec_spec(), _row_spec()],
        out_specs=[pl.BlockSpec((1, 1), lambda i: (0, 0)), _row_spec(), _row_spec(), _vec_spec()],
        out_shape=[SDS((1, 1), f32), SDS((SEQ, D_MODEL), f32), SDS((SEQ, D_MODEL), bf16), SDS((1, D_MODEL), f32)],
    )(x1, y, g_post, target)


def _mid_norm_bwd(dx2, dh1, x1, y0, g_pre, g_post):
    def body(dx2_ref, dh1_ref, x1_ref, y0_ref, gpre_ref, gpost_ref, dx1_ref, dy0_ref, dgpre_ref, dgpost_ref):
        first = pl.program_id(0) == 0
        d_in, dgpre = _rms_bwd(x1_ref[...], gpre_ref[...], dh1_ref[...])
        dx1 = dx2_ref[...] + d_in
        dx1_ref[...] = dx1
        dy0, dgpost = _rms_bwd(y0_ref[...], gpost_ref[...], dx1)
        dy0_ref[...] = dy0.astype(bf16)
        _accumulate(dgpre_ref, dgpre, first)
        _accumulate(dgpost_ref, dgpost, first)

    return pl.pallas_call(
        body, name="mid_norm_bwd", grid=(SEQ // ROWS,),
        in_specs=[_row_spec(), _row_spec(), _row_spec(), _row_spec(), _vec_spec(), _vec_spec()],
        out_specs=[_row_spec(), _row_spec(), _vec_spec(), _vec_spec()],
        out_shape=[SDS((SEQ, D_MODEL), f32), SDS((SEQ, D_MODEL), bf16), SDS((1, D_MODEL), f32), SDS((1, D_MODEL), f32)],
    )(dx2, dh1, x1, y0, g_pre, g_post)


def _pre_norm_bwd(dx1, dh0, x, g):
    def body(dx1_ref, dh0_ref, x_ref, g_ref, dx_ref, dg_ref):
        d_in, dg = _rms_bwd(x_ref[...], g_ref[...], dh0_ref[...])
        dx_ref[...] = dx1_ref[...] + d_in
        _accumulate(dg_ref, dg, pl.program_id(0) == 0)

    return pl.pallas_call(
        body, name="pre_norm_bwd", grid=(SEQ // ROWS,),
        in_specs=[_row_spec(), _row_spec(), _row_spec(), _vec_spec()],
        out_specs=[_row_spec(), _vec_spec()],
        out_shape=[SDS((SEQ, D_MODEL), f32), SDS((1, D_MODEL), f32)])(dx1, dh0, x, g)


def _pool_count(g):
    row = lax.broadcasted_iota(jnp.int32, (SEQ, 1), 0)
    width = jnp.left_shift(2, g)
    return row, width, jnp.minimum(row + 1, width).astype(f32)


def _trailing_sum(x, row, width):
    s = x
    for k in (1, 2, 4, 8):
        shifted = jnp.where(row >= k, pltpu.roll(s, k, 0), 0.0)
        s = jnp.where(width > k, s + shifted, s)
    return s


def _leading_sum(x, row, width):
    s = x
    for k in (1, 2, 4, 8):
        shifted = jnp.where(row < SEQ - k, pltpu.roll(s, SEQ - k, 0), 0.0)
        s = jnp.where(width > k, s + shifted, s)
    return s


def _pool_specs():
    a_in = pl.BlockSpec((SEQ, POOL_CH), lambda g: (0, g))
    a_gate = pl.BlockSpec((SEQ, POOL_CH), lambda g: (0, 4 + g))
    w = pl.BlockSpec((None, POOL_CH, POOL_CH), lambda g: (g, 0, 0))
    scale = pl.BlockSpec((1, POOL_CH), lambda g: (0, g))
    return a_in, a_gate, w, scale


def _pool_fwd(z0, pool_w, pool_scale):
    def body(a_ref, gate_ref, w_ref, scale_ref, ya_ref):
        row, width, count = _pool_count(pl.program_id(0))
        a = a_ref[...]
        pooled = _trailing_sum(a, row, width) / count - a
        mixed = _dot(pooled.astype(bf16), w_ref[...]) * scale_ref[...]
        gate = gate_ref[...]
        ya_ref[...] = (mixed * gate * _sigmoid(gate)).astype(bf16)

    return pl.pallas_call(
        body, name="pool_fwd", grid=(4,), in_specs=list(_pool_specs()),
        out_specs=pl.BlockSpec((SEQ, POOL_CH), lambda g: (0, g)),
        out_shape=SDS((SEQ, 2 * HALF), bf16))(z0, z0, pool_w, pool_scale)


def _pool_bwd(z0, dcat, pool_w, pool_scale):
    def body(a_ref, gate_ref, w_ref, scale_ref, dya_ref, dz_ref, dw_ref, dscale_ref, da_tiles, dgate_tiles, sems):
        g = pl.program_id(0)
        (da_ref, dgate_ref), full = _write_behind(
            g, 4, [da_tiles, dgate_tiles], sems, lambda t, at: _columns(dz_ref, t * HALF + at * POOL_CH, POOL_CH))
        row, width, count = _pool_count(g)
        a = a_ref[...]
        pooled = (_trailing_sum(a, row, width) / count - a).astype(bf16)
        w = w_ref[...]
        scale = scale_ref[...]
        mixed = _dot(pooled, w)
        silu, dsilu = _silu_and_grad(gate_ref[...])
        dya = dya_ref[...]
        dgate_ref[...] = (dya * mixed * scale * dsilu).astype(bf16)
        dms = dya * silu
        dscale_ref[...] = jnp.sum(dms * mixed, axis=0, keepdims=True)
        dmixed = (dms * scale).astype(bf16)
        dw_ref[...] = _dot_tn(pooled, dmixed)
        dpooled = _dot_nt(dmixed, w)
        da_ref[...] = (_leading_sum(dpooled / count, row, width) - dpooled).astype(bf16)
        full()

    a_in, a_gate, w, scale = _pool_specs()
    col = pl.BlockSpec((SEQ, POOL_CH), lambda g: (0, g))
    tiles = pltpu.VMEM((2, SEQ, POOL_CH), bf16)
    return pl.pallas_call(
        body, name="pool_bwd", grid=(4,), in_specs=[a_in, a_gate, w, scale, col],
        out_specs=[ANY, w, scale],
        out_shape=[SDS((SEQ, 6 * D_MODEL), bf16), SDS((4, POOL_CH, POOL_CH), f32), SDS((1, HALF), f32)],
        scratch_shapes=[tiles, tiles, pltpu.SemaphoreType.DMA((4,))],
    )(z0, z0, pool_w, pool_scale, dcat)


Q_COL, K_COL, V_COL, BGATE_COL = 16, 40, 64, 88


def _rope_tables():
    pos = jnp.arange(SEQ, dtype=f32)
    inv_freq = jnp.power(ROPE_THETA, -jnp.arange(0, 2 * ROT_HALF, 2, dtype=f32) / (2 * ROT_HALF))
    ang = pos[:, None] * inv_freq[None, :]
    cos, sin = jnp.cos(ang), jnp.sin(ang)
    zeros = jnp.zeros((SEQ, HEAD_DIM - 2 * ROT_HALF), f32)
    cos_t = jnp.concatenate([cos, cos, zeros + 1.0], axis=1)
    sin_t = jnp.concatenate([sin, sin, zeros], axis=1)
    j = jnp.arange(HEAD_DIM)[:, None]
    i = jnp.arange(HEAD_DIM)[None, :]
    rot = jnp.where((i < ROT_HALF) & (j == i + ROT_HALF), -1.0, 0.0) + jnp.where(
        (i >= ROT_HALF) & (i < 2 * ROT_HALF) & (j == i - ROT_HALF), 1.0, 0.0)
    return cos_t, sin_t, rot.astype(bf16), rot.T.astype(bf16)


def _exact_dot(t, m):
    hi = t.astype(bf16)
    lo = (t - hi.astype(f32)).astype(bf16)
    return _dot(hi, m) + _dot(lo, m)


def _rope(t, cos_t, sin_t, rot):
    return t * cos_t + _exact_dot(t, rot) * sin_t


def _rope_transposed(d, cos_t, sin_t, rot_t):
    return d * cos_t + _exact_dot(d * sin_t, rot_t)


ROW_CHUNK = 256
BLOCKS_TOGETHER = 8


def _chunks(fn):
    for start in range(0, SEQ, ROW_CHUNK):
        fn(start)


def _pieces(dilation):
    length = SEQ // dilation
    n = min(length, ROW_CHUNK)
    return [(r, l0, n) for r in range(dilation) for l0 in range(0, length, n)]


def _by_residue(dst_ref, src_ref, dilation, dtype):
    length = SEQ // dilation
    for r, l0, n in _pieces(dilation):
        src = src_ref[l0:l0 + n, :] if dilation == 1 else src_ref[pl.ds(r + dilation * l0, n, stride=dilation), :]
        start = r * length + l0
        dst_ref[start:start + n, :] = src.astype(dtype)


def _by_position(dst_ref, src_ref, dilation):
    length = SEQ // dilation
    for r, l0, n in _pieces(dilation):
        src = src_ref[r * length + l0:r * length + l0 + n, :]
        if dilation == 1:
            dst_ref[l0:l0 + n, :] = src
        else:
            dst_ref[pl.ds(r + dilation * l0, n, stride=dilation), :] = src


def _attn_masks():
    qi = lax.broadcasted_iota(jnp.int32, (SPAN, 2 * SPAN), 0)
    kj = lax.broadcasted_iota(jnp.int32, (SPAN, 2 * SPAN), 1)
    window = ((kj < SPAN) & (kj >= qi)) | ((kj >= SPAN) & (kj - SPAN <= qi))
    own = lax.broadcasted_iota(jnp.int32, (SPAN, SPAN), 1) <= lax.broadcasted_iota(jnp.int32, (SPAN, SPAN), 0)
    return window, own


def _attn_blocks(dilation):
    per_residue = SEQ // dilation // SPAN
    blocks = [(c, c % per_residue != 0) for c in range(SEQ // SPAN)]
    return [blocks[i:i + BLOCKS_TOGETHER] for i in range(0, len(blocks), BLOCKS_TOGETHER)]


def _block_keys(c, has_prev):
    return slice((c - 1) * SPAN if has_prev else c * SPAN, (c + 1) * SPAN)


def _head_spec(col):
    return pl.BlockSpec((SEQ, HEAD_DIM), lambda h: (0, col + h))


def _table_spec():
    return pl.BlockSpec((SEQ, HEAD_DIM), lambda h: (0, 0))


def _attn_fwd(z0, tables, mixed):
    scale = HEAD_DIM ** -0.5

    def body(*refs):
        qkv = refs[0:9]
        bg_ref, cos_ref, sin_ref, rot_ref = refs[9:13]
        yb_ref, att_ref, lse_ref = refs[14:17]
        saved = refs[17:26]
        tmp_q, tmp_k, v_ones, o_res, l_res, o_nat, l_nat = refs[26:33]
        window_mask, own_mask = _attn_masks()
        rot = rot_ref[...]

        @pl.when(pl.program_id(0) == 0)
        def _():
            v_ones[:, HEAD_DIM:] = jnp.ones((SEQ, HEAD_DIM), bf16)

        for g, dilation in enumerate(DILATIONS):
            q_ref, k_ref, v_ref = qkv[3 * g:3 * g + 3]
            qd, kd, vd = saved[3 * g:3 * g + 3]

            def rope_rows(start, q_ref=q_ref, k_ref=k_ref):
                r = pl.ds(start, ROW_CHUNK)
                cos_t, sin_t = cos_ref[r, :], sin_ref[r, :]
                tmp_q[r, :] = _rope(q_ref[r, :], cos_t, sin_t, rot) * scale
                tmp_k[r, :] = _rope(k_ref[r, :], cos_t, sin_t, rot)

            _chunks(rope_rows)
            _by_residue(qd, tmp_q, dilation, bf16)
            _by_residue(kd, tmp_k, dilation, bf16)
            _by_residue(vd, v_ref, dilation, bf16)
            for l0 in range(0, SEQ, ROW_CHUNK):
                v_ones[l0:l0 + ROW_CHUNK, 0:HEAD_DIM] = vd[l0:l0 + ROW_CHUNK, :]

            for group in _attn_blocks(dilation):
                scores = [_dot_nt(qd[c * SPAN:(c + 1) * SPAN, :], kd[_block_keys(c, prev), :]) for c, prev in group]
                tops, probs = [], []
                for (c, prev), s in zip(group, scores):
                    s = jnp.where(window_mask if prev else own_mask, s, NEG)
                    tops.append(jnp.max(s, axis=1, keepdims=True))
                    probs.append(jnp.exp(s - tops[-1]).astype(bf16))
                sums = [_dot(p, v_ones[_block_keys(c, prev), :]) for (c, prev), p in zip(group, probs)]
                for (c, prev), m, o in zip(group, tops, sums):
                    den = o[:, HEAD_DIM:]
                    o_res[c * SPAN:(c + 1) * SPAN, :] = o[:, :HEAD_DIM] / den
                    l_res[c * SPAN:(c + 1) * SPAN, :] = m + jnp.log(den)

            if dilation > 1:
                _by_position(o_nat, o_res, dilation)
                _by_position(l_nat, l_res, dilation)
            o_g, l_g = (o_res, l_res) if dilation == 1 else (o_nat, l_nat)

            def merge(start, g=g, o_g=o_g, l_g=l_g):
                r = pl.ds(start, ROW_CHUNK)
                if g == 0:
                    att, total = o_g[r, :], l_g[r, :]
                else:
                    l_old, l_new = lse_ref[r, :], l_g[r, :]
                    top = jnp.maximum(l_old, l_new)
                    total = top + jnp.log(jnp.exp(l_old - top) + jnp.exp(l_new - top))
                    att = att_ref[r, :] * jnp.exp(l_old - total) + o_g[r, :] * jnp.exp(l_new - total)
                att_ref[r, :] = att
                lse_ref[r, :] = total
                if g == len(DILATIONS) - 1:
                    gate = bg_ref[r, :]
                    yb_ref[r, :] = (att * gate * _sigmoid(gate)).astype(bf16)

            _chunks(merge)

    in_specs = []
    for g in range(3):
        in_specs += [_head_spec(Q_COL + 8 * g), _head_spec(K_COL + 8 * g), _head_spec(V_COL + 8 * g)]
    in_specs += [_head_spec(BGATE_COL), _table_spec(), _table_spec(), pl.BlockSpec((HEAD_DIM, HEAD_DIM), lambda h: (0, 0)), ANY]
    out_spec = pl.BlockSpec((SEQ, HEAD_DIM), lambda h: (0, h))
    right_half = pl.BlockSpec((SEQ, HEAD_DIM), lambda h: (0, N_HEADS + h))
    vm = lambda dt: pltpu.VMEM((SEQ, HEAD_DIM), dt)
    cos_t, sin_t, rot, _ = tables
    out = pl.pallas_call(
        body, name="attn_fwd", grid=(N_HEADS,), in_specs=in_specs, out_specs=[right_half] + [out_spec] * 11,
        out_shape=[SDS((SEQ, 2 * HALF), bf16), SDS((SEQ, HALF), f32), SDS((SEQ, HALF), f32)] + [SDS((SEQ, HALF), bf16)] * 9,
        scratch_shapes=[vm(f32), vm(f32), pltpu.VMEM((SEQ, 2 * HEAD_DIM), bf16), vm(f32), vm(f32), vm(f32), vm(f32)],
        input_output_aliases={13: 0},
    )(*([z0] * 10), cos_t, sin_t, rot, mixed)
    return out[0], out[1], out[2], [tuple(out[3 + 3 * g:6 + 3 * g]) for g in range(3)]


def _attn_bwd_group(g, saved, z0, att, lse, dcat, tables, dz):
    scale = HEAD_DIM ** -0.5
    dilation = DILATIONS[g]
    with_gate = g == 0
    n_out = 4 if with_gate else 3
    first_col = (Q_COL + 8 * g, K_COL + 8 * g, V_COL + 8 * g, BGATE_COL)

    def body(*refs):
        qd, kd, vd, bg_ref, att_ref, lse_ref, dyb_ref, cos_ref, sin_ref, rot_t_ref = refs[0:10]
        dz_ref = refs[11]
        dod, ld, dd, tmp, aq, ak, av = refs[12:19]
        views, full = _write_behind(pl.program_id(0), N_HEADS, refs[19:19 + n_out], refs[19 + n_out],
                                    lambda t, at: _columns(dz_ref, (first_col[t] + at) * HEAD_DIM, HEAD_DIM))
        dq_ref, dk_ref, dv_ref = views[0:3]
        window_mask, own_mask = _attn_masks()
        rot_t = rot_t_ref[...]

        def gate_rows(start):
            r = pl.ds(start, ROW_CHUNK)
            silu, dsilu = _silu_and_grad(bg_ref[r, :])
            att_v = att_ref[r, :]
            dyb = dyb_ref[r, :]
            if with_gate:
                views[3][r, :] = (dyb * att_v * dsilu).astype(bf16)
            datt = dyb * silu
            tmp[r, :] = datt
            aq[r, :] = jnp.broadcast_to(jnp.sum(datt * att_v, axis=1, keepdims=True), (ROW_CHUNK, HEAD_DIM))

        _chunks(gate_rows)
        _by_residue(dod, tmp, dilation, bf16)
        _by_residue(dd, aq, dilation, f32)
        _by_residue(ld, lse_ref, dilation, f32)

        for group in _attn_blocks(dilation):
            rows = [slice(c * SPAN, (c + 1) * SPAN) for c, _ in group]
            keys = [_block_keys(c, prev) for c, prev in group]
            scores = [_dot_nt(qd[r, :], kd[k, :]) for r, k in zip(rows, keys)]
            dprobs = [_dot_nt(dod[r, :], vd[k, :]) for r, k in zip(rows, keys)]
            probs, dscores = [], []
            for (c, prev), r, s, dp in zip(group, rows, scores, dprobs):
                lse_q, delta = ld[r, :], dd[r, :]
                if prev:
                    lse_q = jnp.concatenate([lse_q, lse_q], axis=1)
                    delta = jnp.concatenate([delta, delta], axis=1)
                p = jnp.where(window_mask if prev else own_mask, jnp.exp(s - lse_q), 0.0)
                probs.append(p.astype(bf16))
                dscores.append((p * (dp - delta)).astype(bf16))
            dvs = [_dot_tn(p, dod[r, :]) for p, r in zip(probs, rows)]
            dks = [_dot_tn(ds, qd[r, :]) for ds, r in zip(dscores, rows)]
            dqs = [_dot(ds, kd[k, :]) for ds, k in zip(dscores, keys)]
            for (c, prev), r, dv, dk, dq in zip(group, rows, dvs, dks, dqs):
                aq[r, :] = dq
                if prev:
                    before = slice((c - 1) * SPAN, c * SPAN)
                    av[before, :] += dv[0:SPAN]
                    ak[before, :] += dk[0:SPAN]
                    av[r, :] = dv[SPAN:]
                    ak[r, :] = dk[SPAN:]
                else:
                    av[r, :] = dv
                    ak[r, :] = dk

        def finish(out_ref, acc, factor, roped):
            if dilation > 1:
                _by_position(tmp, acc, dilation)
            src = acc if dilation == 1 else tmp

            def rows(start):
                r = pl.ds(start, ROW_CHUNK)
                d = src[r, :]
                if factor != 1.0:
                    d = d * factor
                if roped:
                    d = _rope_transposed(d, cos_ref[r, :], sin_ref[r, :], rot_t)
                out_ref[r, :] = d.astype(bf16)

            _chunks(rows)

        finish(dq_ref, aq, scale, True)
        finish(dk_ref, ak, 1.0, True)
        finish(dv_ref, av, 1.0, False)
        full()

    head = pl.BlockSpec((SEQ, HEAD_DIM), lambda h: (0, h))
    in_specs = [head, head, head, _head_spec(BGATE_COL), head, head, _head_spec(8), _table_spec(), _table_spec(),
                pl.BlockSpec((HEAD_DIM, HEAD_DIM), lambda h: (0, 0)), ANY]
    vm = lambda dt: pltpu.VMEM((SEQ, HEAD_DIM), dt)
    cos_t, sin_t, _, rot_t = tables
    return pl.pallas_call(
        body, name=f"attn_bwd_g{g}", grid=(N_HEADS,), in_specs=in_specs, out_specs=ANY,
        out_shape=SDS(dz.shape, dz.dtype), input_output_aliases={10: 0},
        scratch_shapes=[vm(bf16), vm(f32), vm(f32), vm(f32), vm(f32), vm(f32), vm(f32)]
        + [pltpu.VMEM((2, SEQ, HEAD_DIM), bf16)] * n_out + [pltpu.SemaphoreType.DMA((2 * n_out,))],
    )(*saved, z0, att, lse, dcat, cos_t, sin_t, rot_t, dz)


def _sgu_specs():
    chunk = lambda col: pl.BlockSpec((CHUNK, HALF), lambda n: (n, col))
    vec = pl.BlockSpec((1, HALF), lambda n: (0, 0))
    w = pl.BlockSpec((4, CHUNK, CHUNK), lambda n: (0, 0, 0))
    bias = pl.BlockSpec((CHUNK, CHUNK), lambda n: (0, 0))
    return chunk, vec, w, bias


def _sgu_weights(w_ref):
    tril = lax.broadcasted_iota(jnp.int32, (CHUNK, CHUNK), 1) <= lax.broadcasted_iota(jnp.int32, (CHUNK, CHUNK), 0)
    return tril, [jnp.where(tril, w_ref[h], 0.0).astype(bf16) for h in range(4)]


def _sgu_fwd(z1, ln_g, ln_b, sgu_w, bias_t):
    def body(u_ref, v_ref, cg_ref, g_ref, b_ref, w_ref, bias_ref, yc_ref):
        _, ws = _sgu_weights(w_ref)
        xh, _ = _ln_stats(v_ref[...])
        vn = (xh * g_ref[...] + b_ref[...]).astype(bf16)
        for h in range(4):
            cols = slice(h * POOL_CH, (h + 1) * POOL_CH)
            s = _dot(ws[h], vn[:, cols]) + bias_ref[:, h:h + 1]
            gate = cg_ref[:, cols]
            yc_ref[:, cols] = (u_ref[:, cols] * s * gate * _sigmoid(gate)).astype(bf16)

    chunk, vec, w, bias = _sgu_specs()
    return pl.pallas_call(
        body, name="sgu_fwd", grid=(SEQ // CHUNK,),
        in_specs=[chunk(0), chunk(1), chunk(2), vec, vec, w, bias], out_specs=chunk(0),
        out_shape=SDS((SEQ, 2 * HALF), bf16))(z1, z1, z1, ln_g, ln_b, sgu_w, bias_t)


def _sgu_bwd(z1, dcat, ln_g, ln_b, sgu_w, bias_t):
    def body(u_ref, v_ref, cg_ref, dyc_ref, g_ref, b_ref, w_ref, bias_ref,
             dz_ref, dw_ref, dbias_ref, dg_ref, db_ref, dvn_ref, du_tiles, dv_tiles, dcg_tiles, sems):
        n = pl.program_id(0)
        (du_ref, dv_ref, dcg_ref), full = _write_behind(
            n, SEQ // CHUNK, [du_tiles, dv_tiles, dcg_tiles], sems,
            lambda t, at: dz_ref.at[pl.ds(pl.multiple_of(at * CHUNK, CHUNK), CHUNK), t * HALF:(t + 1) * HALF])
        first = n == 0
        tril, ws = _sgu_weights(w_ref)
        xh, rstd = _ln_stats(v_ref[...])
        g = g_ref[...]
        vn = (xh * g + b_ref[...]).astype(bf16)

        @pl.when(first)
        def _():
            dbias_ref[...] = jnp.zeros((CHUNK, CHUNK), f32)

        for h in range(4):
            cols = slice(h * POOL_CH, (h + 1) * POOL_CH)
            vn_h = vn[:, cols]
            s = _dot(ws[h], vn_h) + bias_ref[:, h:h + 1]
            silu, dsilu = _silu_and_grad(cg_ref[:, cols])
            dyc = dyc_ref[:, cols]
            u = u_ref[:, cols]
            du_ref[:, cols] = (dyc * s * silu).astype(bf16)
            dcg_ref[:, cols] = (dyc * u * s * dsilu).astype(bf16)
            ds = dyc * u * silu
            dbias_ref[:, h:h + 1] += jnp.sum(ds, axis=1, keepdims=True)
            ds = ds.astype(bf16)
            _accumulate(dw_ref.at[h], jnp.where(tril, _dot_nt(ds, vn_h), 0.0), first)
            dvn_ref[:, cols] = _dot_tn(ws[h], ds)
        dv, dg, db = _ln_bwd(xh, rstd, g, dvn_ref[...])
        dv_ref[...] = dv.astype(bf16)
        _accumulate(dg_ref, dg, first)
        _accumulate(db_ref, db, first)
        full()

    chunk, vec, w, bias = _sgu_specs()
    tiles = pltpu.VMEM((2, CHUNK, HALF), bf16)
    return pl.pallas_call(
        body, name="sgu_bwd", grid=(SEQ // CHUNK,),
        in_specs=[chunk(0), chunk(1), chunk(2), chunk(0), vec, vec, w, bias],
        out_specs=[ANY, w, bias, vec, vec],
        out_shape=[SDS((SEQ, 3 * D_MODEL), bf16), SDS((4, CHUNK, CHUNK), f32), SDS((CHUNK, CHUNK), f32),
                   SDS((1, HALF), f32), SDS((1, HALF), f32)],
        scratch_shapes=[pltpu.VMEM((CHUNK, HALF), f32), tiles, tiles, tiles, pltpu.SemaphoreType.DMA((6,))],
    )(z1, z1, z1, dcat, ln_g, ln_b, sgu_w, bias_t)


CONV_TILE = 128
CONV_CH = 128
CONV_BLOCKS = HALF // CONV_CH
DVAL_COL, DGLU_COL = 24, 32


def _conv_specs():
    val = pl.BlockSpec((SEQ, CONV_CH), lambda j: (0, DVAL_COL + j))
    glu = pl.BlockSpec((SEQ, CONV_CH), lambda j: (0, DGLU_COL + j))
    w = pl.BlockSpec((CONV_K, CONV_CH), lambda j: (0, j))
    col = pl.BlockSpec((SEQ, CONV_CH), lambda j: (0, j))
    vec = pl.BlockSpec((1, CONV_CH), lambda j: (0, j))
    return val, glu, w, col, vec


def _conv_fwd(z1, conv_w, conv_b):
    def body(val_ref, glu_ref, w_ref, b_ref, out_ref, xpad):
        xpad[0:CONV_PAD, :] = jnp.zeros((CONV_PAD, CONV_CH), f32)
        xpad[CONV_PAD:, :] = val_ref[...] * _sigmoid(glu_ref[...])
        w = w_ref[...]
        bias = b_ref[...]

        def tile(i, carry):
            t0 = pl.multiple_of(i * CONV_TILE, CONV_TILE)
            window = xpad[pl.ds(t0, CONV_TILE + CONV_PAD), :]
            acc = jnp.broadcast_to(bias, (CONV_TILE, CONV_CH))
            for k in range(CONV_K):
                shift = CONV_PAD - (CONV_K - 1) + k
                acc = acc + w[k:k + 1, :] * pltpu.roll(window, CONV_TILE + CONV_PAD - shift, 0)[0:CONV_TILE]
            out_ref[pl.ds(t0, CONV_TILE), :] = acc
            return carry

        lax.fori_loop(0, SEQ // CONV_TILE, tile, 0)

    val, glu, w, col, vec = _conv_specs()
    return pl.pallas_call(
        body, name="conv_fwd", grid=(CONV_BLOCKS,), in_specs=[val, glu, w, vec], out_specs=col,
        out_shape=SDS((SEQ, HALF), f32), scratch_shapes=[pltpu.VMEM((SEQ + CONV_PAD, CONV_CH), f32)],
    )(z1, z1, conv_w, conv_b)


def _conv_bwd(z1, dconv, conv_w, dz):
    def body(val_ref, glu_ref, w_ref, dout_ref, dz_in, dz_ref, dw_ref, db_ref, xpad, dpad, dx_ref, dval_tiles, dglu_tiles, sems):
        j = pl.program_id(0)
        (dval_ref, dglu_ref), full = _write_behind(
            j, CONV_BLOCKS, [dval_tiles, dglu_tiles], sems,
            lambda t, at: _columns(dz_ref, ((DVAL_COL, DGLU_COL)[t] + at) * CONV_CH, CONV_CH))
        val = val_ref[...]
        sig = _sigmoid(glu_ref[...])
        xpad[0:CONV_PAD, :] = jnp.zeros((CONV_PAD, CONV_CH), f32)
        xpad[CONV_PAD:, :] = val * sig
        dout = dout_ref[...]
        dpad[0:SEQ, :] = dout
        dpad[SEQ:, :] = jnp.zeros((CONV_PAD, CONV_CH), f32)
        db_ref[...] = jnp.sum(dout, axis=0, keepdims=True)
        dw_ref[...] = jnp.zeros((CONV_K, CONV_CH), f32)
        w = w_ref[...]

        def tile(i, carry):
            t0 = pl.multiple_of(i * CONV_TILE, CONV_TILE)
            x_win = xpad[pl.ds(t0, CONV_TILE + CONV_PAD), :]
            d_win = dpad[pl.ds(t0, CONV_TILE + CONV_PAD), :]
            d_own = d_win[0:CONV_TILE]
            acc = jnp.zeros((CONV_TILE, CONV_CH), f32)
            for k in range(CONV_K):
                shift = CONV_PAD - (CONV_K - 1) + k
                x_k = pltpu.roll(x_win, CONV_TILE + CONV_PAD - shift, 0)[0:CONV_TILE]
                dw_ref[k:k + 1, :] += jnp.sum(d_own * x_k, axis=0, keepdims=True)
                back = CONV_K - 1 - k
                d_k = d_own if back == 0 else pltpu.roll(d_win, CONV_TILE + CONV_PAD - back, 0)[0:CONV_TILE]
                acc = acc + w[k:k + 1, :] * d_k
            dx_ref[pl.ds(t0, CONV_TILE), :] = acc
            return carry

        lax.fori_loop(0, SEQ // CONV_TILE, tile, 0)
        dx = dx_ref[...]
        dval_ref[...] = (dx * sig).astype(bf16)
        dglu_ref[...] = (dx * val * sig * (1.0 - sig)).astype(bf16)
        full()

    val, glu, w, col, vec = _conv_specs()
    pad = pltpu.VMEM((SEQ + CONV_PAD, CONV_CH), f32)
    tiles = pltpu.VMEM((2, SEQ, CONV_CH), bf16)
    return pl.pallas_call(
        body, name="conv_bwd", grid=(CONV_BLOCKS,), in_specs=[val, glu, w, col, ANY], out_specs=[ANY, w, vec],
        out_shape=[SDS(dz.shape, dz.dtype), SDS((CONV_K, HALF), f32), SDS((1, HALF), f32)],
        input_output_aliases={4: 0},
        scratch_shapes=[pad, pad, pltpu.VMEM((SEQ, CONV_CH), f32), tiles, tiles, pltpu.SemaphoreType.DMA((4,))],
    )(z1, z1, conv_w, dconv, dz)


DGATE_COL = 5


def _conv_norm_fwd(conv, z1, g, b, mixed):
    def body(c_ref, gate_ref, g_ref, b_ref, mixed_ref, yd_ref):
        xh, _ = _ln_stats(c_ref[...])
        n = xh * g_ref[...] + b_ref[...]
        gate = gate_ref[...]
        yd_ref[...] = (n * _sigmoid(n) * gate * _sigmoid(gate)).astype(bf16)

    return pl.pallas_call(
        body, name="conv_norm_fwd", grid=(SEQ // ROWS,),
        in_specs=[_row_spec(HALF), _row_spec(HALF, DGATE_COL), _vec_spec(HALF), _vec_spec(HALF), ANY],
        out_specs=_row_spec(HALF, 1), out_shape=SDS((SEQ, 2 * HALF), bf16), input_output_aliases={4: 0},
    )(conv, z1, g, b, mixed)


def _conv_norm_bwd(conv, z1, dcat, g, b, dz):
    def body(c_ref, gate_ref, dyd_ref, g_ref, b_ref, dz_ref, dconv_ref, dgate_ref, dg_ref, db_ref):
        first = pl.program_id(0) == 0
        xh, rstd = _ln_stats(c_ref[...])
        g = g_ref[...]
        n_silu, n_dsilu = _silu_and_grad(xh * g + b_ref[...])
        gate_silu, gate_dsilu = _silu_and_grad(gate_ref[...])
        dyd = dyd_ref[...]
        dgate_ref[...] = (dyd * n_silu * gate_dsilu).astype(bf16)
        dconv, dg, db = _ln_bwd(xh, rstd, g, dyd * gate_silu * n_dsilu)
        dconv_ref[...] = dconv
        _accumulate(dg_ref, dg, first)
        _accumulate(db_ref, db, first)

    return pl.pallas_call(
        body, name="conv_norm_bwd", grid=(SEQ // ROWS,),
        in_specs=[_row_spec(HALF), _row_spec(HALF, DGATE_COL), _row_spec(HALF, 1), _vec_spec(HALF), _vec_spec(HALF), ANY],
        out_specs=[_row_spec(HALF), _row_spec(HALF, DGATE_COL), _vec_spec(HALF), _vec_spec(HALF)],
        out_shape=[SDS((SEQ, HALF), f32), SDS(dz.shape, dz.dtype), SDS((1, HALF), f32), SDS((1, HALF), f32)],
        input_output_aliases={5: 1},
    )(conv, z1, dcat, g, b, dz)


def _step(x, target, w, chip):
    chip_vec = chip.astype(jnp.int32).reshape(1)
    sharded_names = list(SHARDED_SMALL)
    first = [_cast_into_slot(w["e_w_in"], chip_vec, "cast_e_w_in0", w["e_pre_norm"], 0, E_IN_PIECES)]
    sems, bufs, token = _gather_start(first, "gather_start_first")
    small_shard = _pack([w[k] for k in sharded_names], total_rows=SMALL_SHARD_ROWS) + 0.0 * token[0, 0]
    small_slot = lax.dynamic_update_slice(jnp.zeros((N_CHIPS, SMALL_SHARD_ROWS, LANES), f32), small_shard[None], (chip, 0, 0))
    more = [small_slot]
    more += [_cast_into_slot(w["e_w_in"], chip_vec, f"cast_e_w_in{i}", token, i, E_IN_PIECES) for i in range(1, E_IN_PIECES)]
    more_sems, more_bufs, token = _gather_start(more, "gather_start_pieces")
    rest = [_cast_into_slot(w[k], chip_vec, f"cast_{k}", token) for k in BIG[1:]]
    rest_sems, rest_bufs, token = _gather_start(rest, "gather_start_rest")
    sems, bufs = sems + more_sems + rest_sems, bufs + more_bufs + rest_bufs
    tables = _rope_tables()

    def vec(k):
        return w[k].reshape(1, -1)

    h0 = _pre_norm(x, vec("e_pre_norm") + token[0, 0])
    after, z0, e_w_in = h0, None, []
    for i in range(E_IN_PIECES):
        group = slice(0, 1) if i == 0 else slice(1, 3) if i == 1 else slice(i + 1, i + 2)
        landed = _forward_halves(_gather_wait(bufs[group], sems[group], after, f"gather_wait_{i}"), f"forward_{i}")
        if i == 1:
            small_full = landed[0]
        e_w_in.append(landed[-1])
        z0 = _mm_nn(h0, landed[-1], f32, f"e_in{i}", i, E_IN_PIECES, z0)
        after = z0
    p = {k: _from_chips(k, a) for k, a in zip(sharded_names, _unpack(small_full, [SHARDED_SMALL[k][0] for k in sharded_names]))}
    for k in ("o_pre_norm", "o_sgu_norm_g", "o_sgu_norm_b", "o_conv_b", "o_conv_norm_g", "o_conv_norm_b", "o_post_norm"):
        p[k] = p[k].reshape(1, -1)
    pool_w_bf = p["e_pool_w"].astype(bf16)
    bias_t = jnp.pad(w["o_sgu_b"].T, ((0, 0), (0, CHUNK - 4)))

    cat0, att, lse, qkv_by_residue = _attn_fwd(z0, tables, _pool_fwd(z0, pool_w_bf, vec("e_pool_scale")))

    def arrived(index, after, name):
        one = slice(index, index + 1)
        return _forward_halves(_gather_wait(bufs[one], sems[one], after, f"gather_wait_{name}"), f"forward_{name}")[0]

    e_w_out = arrived(1 + E_IN_PIECES, att, "e_w_out").reshape(1, D_MODEL, D_MODEL)
    y0 = _mm_nn(cat0, e_w_out, f32, "e_out")
    x1, h1 = _mid_norm(x, y0, vec("e_post_norm"), p["o_pre_norm"])
    o_w_in = arrived(2 + E_IN_PIECES, h1, "o_w_in")
    z1 = _mm_nn(h1, o_w_in, f32, "o_in")
    yc = _sgu_fwd(z1, p["o_sgu_norm_g"], p["o_sgu_norm_b"], w["o_sgu_w"], bias_t)
    conv = _conv_fwd(z1, p["o_conv_w"], p["o_conv_b"])
    cat1 = _conv_norm_fwd(conv, z1, p["o_conv_norm_g"], p["o_conv_norm_b"], yc)
    o_w_out = arrived(3 + E_IN_PIECES, cat1, "o_w_out").reshape(1, D_MODEL, D_MODEL)
    y1 = _mm_nn(cat1, o_w_out, f32, "o_out")
    loss, dx2, dy1, g_o_post = _final_norm_loss(x1, y1, p["o_post_norm"], target)

    in_flight = {}

    def send_off(name, grad):
        sem, sums, land, tok = _scatter_start(_swap_add(grad, f"swap_add_{name}"), f"scatter_start_{name}")
        in_flight[name] = (sem, sums, land)
        return tok

    tok = send_off("o_w_out", _mm_tn(cat1, dy1, 1, "o_out_dw").reshape(N_CHIPS, HALF // 2, D_MODEL))
    dcat1 = _mm_nt(dy1, [o_w_out], "o_out_dx", tok)
    dz1, g_sgu_w, g_bias_t, g_sgu_g, g_sgu_b = _sgu_bwd(
        z1, dcat1, p["o_sgu_norm_g"] + tok[0, 0], p["o_sgu_norm_b"], w["o_sgu_w"], bias_t)
    dconv, dz1, g_cn_g, g_cn_b = _conv_norm_bwd(conv, z1, dcat1, p["o_conv_norm_g"], p["o_conv_norm_b"], dz1)
    dz1, g_conv_w, g_conv_b = _conv_bwd(z1, dconv, p["o_conv_w"], dz1)
    tok = send_off("o_w_in", _mm_tn(h1, dz1, N_CHIPS, "o_in_dw"))
    dh1 = _mm_nt(dz1, [o_w_in], "o_in_dx", tok)
    dx1, dy0, g_o_pre, g_e_post = _mid_norm_bwd(dx2, dh1, x1, y0, p["o_pre_norm"] + tok[0, 0], vec("e_post_norm"))

    tok = send_off("e_w_out", _mm_tn(cat0, dy0, 1, "e_out_dw").reshape(N_CHIPS, HALF // 2, D_MODEL))
    dcat0 = _mm_nt(dy0, [e_w_out], "e_out_dx", tok)
    dz0, g_pool_w, g_pool_scale = _pool_bwd(z0, dcat0, pool_w_bf, vec("e_pool_scale") + tok[0, 0])
    for g in range(len(DILATIONS)):
        dz0 = _attn_bwd_group(g, qkv_by_residue[g], z0, att, lse, dcat0, tables, dz0)
    tok = send_off("e_w_in", _mm_tn(h0, dz0, N_CHIPS, "e_in_dw"))
    dh0 = _mm_nt(dz0, e_w_in, "e_in_dx", tok)
    grad_x, g_e_pre = _pre_norm_bwd(dx1, dh0, x, vec("e_pre_norm") + tok[0, 0])

    small = {"e_pre_norm": g_e_pre, "e_pool_w": g_pool_w, "e_pool_scale": g_pool_scale, "e_post_norm": g_e_post,
             "o_pre_norm": g_o_pre, "o_sgu_norm_g": g_sgu_g, "o_sgu_norm_b": g_sgu_b, "o_sgu_w": g_sgu_w,
             "o_sgu_b": g_bias_t, "o_conv_w": g_conv_w, "o_conv_b": g_conv_b,
             "o_conv_norm_g": g_cn_g, "o_conv_norm_b": g_cn_b, "o_post_norm": g_o_post}
    return loss, grad_x, in_flight, small


def _land(in_flight, name, chip, after):
    sems, sums, land = in_flight[name]
    sums, land = _scatter_wait(sems, sums, land, after, f"scatter_wait_{name}")
    return _add_landed_join(sums, land, chip.astype(jnp.int32).reshape(1), f"add_landed_{name}")


def _place():
    x, y, c = lax.axis_index("x"), lax.axis_index("y"), lax.axis_index("c")
    others = [(1 - x, y), (x, 1 - y), (1 - x, 1 - y)]
    return x, y, c, 2 * x + y, others


SWAP_ROWS = 256
FORWARD_STAGE_BYTES = 4 << 20


def _swap_add(g, name):
    chips, r, c = g.shape
    half = r // 2
    rows_per_step = 2 * SWAP_ROWS if half % (2 * SWAP_ROWS) == 0 else SWAP_ROWS
    nb = half // rows_per_step
    steps = chips * nb

    def body(core_ref, mine_ref, theirs_ref, out_ref, landing, send_sems, recv_sems, free_sems):
        i = pl.program_id(0)
        x, y, core, _, _ = _place()
        sibling = (x, y, 1 - core)

        def send(slot):
            return pltpu.make_async_remote_copy(src_ref=theirs_ref, dst_ref=landing.at[slot], send_sem=send_sems.at[slot],
                                                recv_sem=recv_sems.at[slot], device_id=sibling, device_id_type=MESH)

        @pl.when(i < steps)
        def _():
            @pl.when(i >= 2)
            def _():
                pl.semaphore_wait(free_sems.at[i % 2], 1)

            send(i % 2).start()

        @pl.when(i >= 1)
        def _():
            landed = (i - 1) % 2
            send(landed).wait_recv()
            out_ref[...] = (mine_ref[...].astype(f32) + landing[landed].astype(f32)).astype(out_ref.dtype)

            @pl.when(i + 1 < steps)
            def _():
                pl.semaphore_signal(free_sems.at[landed], 1, device_id=sibling, device_id_type=MESH)

        @pl.when(i < steps)
        def _():
            send(i % 2).wait_send()

    def rows_of(b, h):
        return (2 * (b // nb) + h) * nb + b % nb

    block = (rows_per_step, c)
    grid_spec = pltpu.PrefetchScalarGridSpec(
        num_scalar_prefetch=1, grid=(steps + 1,),
        in_specs=[pl.BlockSpec(block, lambda i, core: (rows_of(jnp.maximum(i - 1, 0), core[0]), 0)),
                  pl.BlockSpec(block, lambda i, core: (rows_of(jnp.minimum(i, steps - 1), 1 - core[0]), 0))],
        out_specs=pl.BlockSpec(block, lambda i, core: (jnp.maximum(i - 1, 0), 0)),
        scratch_shapes=[pltpu.VMEM((2, rows_per_step, c), g.dtype), pltpu.SemaphoreType.DMA((2,)),
                        pltpu.SemaphoreType.DMA((2,)), pltpu.SemaphoreType.REGULAR((2,))])
    core = lax.axis_index("c").astype(jnp.int32).reshape(1)
    rows = g.reshape(chips * r, c)
    out = pl.pallas_call(body, name=name, grid_spec=grid_spec, out_shape=SDS((chips * half, c), g.dtype))(core, rows, rows)
    return out.reshape(chips, half, c)


HBM = pl.BlockSpec(memory_space=pltpu.HBM)
SEM = pl.BlockSpec(memory_space=pltpu.SEMAPHORE)
EFFECT = pltpu.SideEffectType.DATAFLOW_SIDE_EFFECTING


def _in_hbm(a):
    return pltpu.with_memory_space_constraint(a, pltpu.HBM)


def _cast_into_slot(w, chip, name, after, piece=0, pieces=1):
    r, c = w.shape
    c = c // pieces
    nb = r // SWAP_ROWS

    def body(chip_ref, w_ref, after_ref, o_ref):
        o_ref[...] = w_ref[...].astype(bf16)

    grid_spec = pltpu.PrefetchScalarGridSpec(
        num_scalar_prefetch=1, grid=(nb,),
        in_specs=[pl.BlockSpec((SWAP_ROWS, c), lambda i, chip: (i, piece)), ANY],
        out_specs=pl.BlockSpec((SWAP_ROWS, c), lambda i, chip: (chip[0] * nb + i, 0)))
    out = pl.pallas_call(body, name=name, grid_spec=grid_spec, out_shape=SDS((N_CHIPS * r, c), bf16))(chip, w, after)
    return out.reshape(N_CHIPS, r, c)


def _gather_start(bufs, name):
    n = len(bufs)

    def body(*refs):
        ins, sems, token = refs[:n], refs[n:3 * n], refs[4 * n]
        x, y, c, me, others = _place()
        for a in range(n):
            rows = ins[a].shape[1] // 2
            mine = ins[a].at[me, pl.ds(c * rows, rows), :]
            for k, (ox, oy) in enumerate(others):
                pltpu.make_async_remote_copy(src_ref=mine, dst_ref=mine, send_sem=sems[2 * a].at[k],
                                             recv_sem=sems[2 * a + 1].at[k], device_id=(ox, oy, c),
                                             device_id_type=MESH).start()
        token[...] = jnp.zeros_like(token)

    out = pl.pallas_call(
        body, name=name, in_specs=[HBM] * n,
        out_shape=(*[pltpu.SemaphoreType.DMA((3,))] * (2 * n), *[pltpu.HBM(b.shape, b.dtype) for b in bufs],
                   SDS((8, 128), f32)),
        out_specs=(*[SEM] * (2 * n), *[HBM] * n, pl.BlockSpec(memory_space=pltpu.VMEM)),
        input_output_aliases={a: 2 * n + a for a in range(n)},
        compiler_params=pltpu.CompilerParams(has_side_effects=EFFECT),
    )(*[_in_hbm(b) for b in bufs])
    return [(out[2 * a], out[2 * a + 1]) for a in range(n)], list(out[2 * n:3 * n]), out[3 * n]


def _gather_wait(bufs, sems, after, name):
    n = len(bufs)

    def body(*refs):
        ins, sem_refs = refs[:n], refs[n:3 * n]
        x, y, c, me, others = _place()
        for a in range(n):
            rows = ins[a].shape[1] // 2
            mine = ins[a].at[me, pl.ds(c * rows, rows), :]
            for k, (ox, oy) in enumerate(others):
                landed = ins[a].at[2 * ox + oy, pl.ds(c * rows, rows), :]
                copy = pltpu.make_async_remote_copy(src_ref=mine, dst_ref=landed, send_sem=sem_refs[2 * a].at[k],
                                                    recv_sem=sem_refs[2 * a + 1].at[k], device_id=(ox, oy, c),
                                                    device_id_type=MESH)
                copy.wait_send()
                copy.wait_recv()

    flat_sems = [s for pair in sems for s in pair]
    out = pl.pallas_call(
        body, name=name, in_specs=[HBM] * n + [SEM] * (2 * n) + [ANY],
        out_shape=tuple(pltpu.HBM(b.shape, b.dtype) for b in bufs), out_specs=tuple([HBM] * n),
        input_output_aliases={a: a for a in range(n)},
        compiler_params=pltpu.CompilerParams(has_side_effects=EFFECT),
    )(*bufs, *flat_sems, after)
    return list(out)


def _forward_halves(bufs, name):
    n = len(bufs)
    blocks = []
    for b in bufs:
        half = b.shape[1] // 2
        whole = half * b.shape[2] * b.dtype.itemsize <= FORWARD_STAGE_BYTES
        blocks.append((half, half if whole or half % SWAP_ROWS else SWAP_ROWS))
    work = [(a, k, b) for a in range(n) for k in range(3) for b in range(blocks[a][0] // blocks[a][1])]

    def body(*refs):
        outs, stages = refs[n:2 * n], refs[2 * n:3 * n]
        load_sems, send_sems, recv_sems = refs[3 * n:]
        x, y, c, me, others = _place()
        sibling = (x, y, 1 - c)

        def rows(item):
            a, k, b = item
            half, tr = blocks[a]
            ox, oy = others[k]
            return outs[a].at[2 * ox + oy, pl.ds(c * half + b * tr, tr), :]

        def load(s, item):
            return pltpu.make_async_copy(rows(item), stages[item[0]].at[s], load_sems.at[s])

        def send(s, item):
            return pltpu.make_async_remote_copy(src_ref=stages[item[0]].at[s], dst_ref=rows(item), send_sem=send_sems.at[s],
                                                recv_sem=recv_sems.at[item[0]], device_id=sibling, device_id_type=MESH)

        load(0, work[0]).start()
        for t, item in enumerate(work):
            s = t % 2
            load(s, item).wait()
            send(s, item).start()
            if t + 1 < len(work):
                if t >= 1:
                    send(1 - s, work[t - 1]).wait_send()
                load(1 - s, work[t + 1]).start()
        if len(work) > 1:
            send(len(work) % 2, work[-2]).wait_send()
        send((len(work) - 1) % 2, work[-1]).wait_send()
        for a in range(n):
            theirs = outs[a].at[pl.ds(0, 3), pl.ds(0, blocks[a][0]), :]
            pltpu.make_async_remote_copy(src_ref=theirs, dst_ref=theirs, send_sem=send_sems.at[0], recv_sem=recv_sems.at[a],
                                         device_id=sibling, device_id_type=MESH).wait_recv()

    out = pl.pallas_call(
        body, name=name, in_specs=[ANY] * n, out_specs=[ANY] * n, out_shape=[SDS(b.shape, b.dtype) for b in bufs],
        input_output_aliases={a: a for a in range(n)},
        scratch_shapes=[pltpu.VMEM((2, blocks[a][1], bufs[a].shape[2]), bufs[a].dtype) for a in range(n)]
        + [pltpu.SemaphoreType.DMA((2,)), pltpu.SemaphoreType.DMA((2,)), pltpu.SemaphoreType.DMA((n,))],
    )(*bufs)
    return list(out)


def _scatter_start(chip_sums, name):
    def body(a_ref, land_ref, send_sems, recv_sems, a_thru, land_thru, token):
        x, y, c, me, others = _place()
        for k, (ox, oy) in enumerate(others):
            pltpu.make_async_remote_copy(src_ref=a_ref.at[2 * ox + oy], dst_ref=land_ref.at[me], send_sem=send_sems.at[k],
                                         recv_sem=recv_sems.at[k], device_id=(ox, oy, c), device_id_type=MESH).start()
        token[...] = jnp.zeros_like(token)

    shape = pltpu.HBM(chip_sums.shape, chip_sums.dtype)
    send, recv, a_thru, land, token = pl.pallas_call(
        body, name=name, in_specs=[HBM, HBM],
        out_shape=(pltpu.SemaphoreType.DMA((3,)), pltpu.SemaphoreType.DMA((3,)), shape, shape, SDS((8, 128), f32)),
        out_specs=(SEM, SEM, HBM, HBM, pl.BlockSpec(memory_space=pltpu.VMEM)), input_output_aliases={0: 2, 1: 3},
        compiler_params=pltpu.CompilerParams(has_side_effects=EFFECT),
    )(_in_hbm(chip_sums), _in_hbm(lax.empty(chip_sums.shape, chip_sums.dtype)))
    return (send, recv), a_thru, land, token


def _scatter_wait(sems, chip_sums, land, after, name):
    def body(a_ref, land_ref, send_sems, recv_sems, after_ref, a_out, land_out):
        x, y, c, me, others = _place()
        for k, (ox, oy) in enumerate(others):
            copy = pltpu.make_async_remote_copy(
                src_ref=a_ref.at[2 * ox + oy], dst_ref=land_ref.at[2 * ox + oy], send_sem=send_sems.at[k],
                recv_sem=recv_sems.at[k], device_id=(ox, oy, c), device_id_type=MESH)
            copy.wait_send()
            copy.wait_recv()

    shape = pltpu.HBM(chip_sums.shape, chip_sums.dtype)
    return pl.pallas_call(
        body, name=name, in_specs=[HBM, HBM, SEM, SEM, ANY], out_shape=(shape, shape), out_specs=(HBM, HBM),
        input_output_aliases={0: 0, 1: 1}, compiler_params=pltpu.CompilerParams(has_side_effects=EFFECT),
    )(chip_sums, land, sems[0], sems[1], after)


def _add_landed_join(chip_sums, land, chip, name):
    chips, rh, c = chip_sums.shape
    nb = rh // SWAP_ROWS

    def body(chip_ref, own_ref, l1_ref, l2_ref, l3_ref, out_hbm, buf, send_sems, recv_sem, local_sems):
        i = pl.program_id(0)
        slot = i % 2
        x, y, core, _, _ = _place()
        sibling = (x, y, 1 - core)

        def copies(s, step):
            rows = pl.ds(pl.multiple_of((core * nb + step) * SWAP_ROWS, SWAP_ROWS), SWAP_ROWS)
            keep = pltpu.make_async_copy(buf.at[s], out_hbm.at[rows, :], local_sems.at[s])
            give = pltpu.make_async_remote_copy(src_ref=buf.at[s], dst_ref=out_hbm.at[rows, :], send_sem=send_sems.at[s],
                                                recv_sem=recv_sem.at[0], device_id=sibling, device_id_type=MESH)
            return keep, give

        def drain(s, step):
            keep, give = copies(s, step)
            keep.wait()
            give.wait_send()

        @pl.when(i >= 2)
        def _():
            drain(slot, i - 2)

        buf[slot] = ((own_ref[...].astype(f32) + l1_ref[...].astype(f32)) + l2_ref[...].astype(f32)) + l3_ref[...].astype(f32)
        keep, give = copies(slot, i)
        keep.start()
        give.start()

        @pl.when(i == nb - 1)
        def _():
            drain(slot, i)
            if nb > 1:
                drain(1 - slot, i - 1)
            theirs = out_hbm.at[pl.ds((1 - core) * rh, rh), :]
            pltpu.make_async_remote_copy(src_ref=theirs, dst_ref=theirs, send_sem=send_sems.at[0], recv_sem=recv_sem.at[0],
                                         device_id=sibling, device_id_type=MESH).wait_recv()

    block = (SWAP_ROWS, c)
    from_slot = lambda d: pl.BlockSpec(block, lambda i, chip: (((chip[0] + d) % chips) * nb + i, 0))
    grid_spec = pltpu.PrefetchScalarGridSpec(
        num_scalar_prefetch=1, grid=(nb,), in_specs=[from_slot(0), from_slot(1), from_slot(2), from_slot(3)],
        out_specs=ANY,
        scratch_shapes=[pltpu.VMEM((2, SWAP_ROWS, c), f32), pltpu.SemaphoreType.DMA((2,)),
                        pltpu.SemaphoreType.DMA((1,)), pltpu.SemaphoreType.DMA((2,))])
    land_rows = land.reshape(chips * rh, c)
    return pl.pallas_call(body, name=name, grid_spec=grid_spec, out_shape=SDS((2 * rh, c), f32))(
        chip, chip_sums.reshape(chips * rh, c), land_rows, land_rows, land_rows)


def _adamw_update(w_ref, g_ref, m_ref, v_ref, d_ref, nm_ref, nv_ref):
    g = g_ref[...]
    nm = ADAM_B1 * m_ref[...] + (1.0 - ADAM_B1) * g
    nv = ADAM_B2 * v_ref[...] + (1.0 - ADAM_B2) * (g * g)
    nm_ref[...] = nm
    nv_ref[...] = nv
    m_hat = nm / (1.0 - ADAM_B1 ** ADAM_STEP)
    v_hat = nv / (1.0 - ADAM_B2 ** ADAM_STEP)
    d_ref[...] = -ADAM_LR * (m_hat / (jnp.sqrt(v_hat) + ADAM_EPS) + ADAM_WD * w_ref[...])


def _adamw(w, g, m, v, name):
    r, c = w.shape
    tr = 128 if r % 128 == 0 else r

    def body(w_ref, g_ref, m_ref, v_ref, g_out_ref, d_ref, nm_ref, nv_ref):
        g_out_ref[...] = g_ref[...]
        _adamw_update(w_ref, g_ref, m_ref, v_ref, d_ref, nm_ref, nv_ref)

    spec = pl.BlockSpec((tr, c), lambda i: (i, 0))
    return pl.pallas_call(body, name=name, grid=(r // tr,), in_specs=[spec] * 4, out_specs=[spec] * 4,
                          out_shape=[SDS((r, c), f32)] * 4)(w, g, m, v)


SMALL_PACKING = {
    "e_pre_norm": ((1, 2048), 8, (1, 2048)), "e_pool_w": ((1024, 256), 1024, (256, 256)),
    "e_pool_scale": ((1, 1024), 8, (1, 1024)), "e_post_norm": ((1, 2048), 8, (1, 2048)),
    "o_pre_norm": ((1, 2048), 8, (1, 512)), "o_sgu_norm_g": ((1, 1024), 8, (1, 256)),
    "o_sgu_norm_b": ((1, 1024), 8, (1, 256)), "o_sgu_w": ((512, 128), 512, (512, 128)),
    "o_sgu_b": ((128, 128), 8, (4, 128)), "o_conv_w": ((31, 1024), 128, (31, 256)), "o_conv_b": ((1, 1024), 8, (1, 256)),
    "o_conv_norm_g": ((1, 1024), 8, (1, 256)), "o_conv_norm_b": ((1, 1024), 8, (1, 256)),
    "o_post_norm": ((1, 2048), 8, (1, 512)),
}
SMALL_PACKED_ROWS = 1792


def _small_finalize(grads, ws, ms, vs, after):
    names = list(SMALL_ORDER)
    n = len(names)
    half, piece = SMALL_PACKED_ROWS // 2, SMALL_PACKED_ROWS // 8
    first_row, row = {}, 0
    for k in names:
        first_row[k] = row
        row += SMALL_PACKING[k][1]

    def body(*refs):
        g_refs, total = refs[0:n], refs[n + 1]
        pack, from_sibling, from_chips, send_a, recv_a, send_b, recv_b, send_c, recv_c, send_d, recv_d = refs[n + 2:]
        x, y, c, me, others = _place()

        for r0 in range(0, SMALL_PACKED_ROWS, piece):
            pack[r0:r0 + piece, :] = jnp.zeros((piece, LANES), f32)
        for k, g_ref in zip(names, g_refs):
            (rows, width), _, _ = SMALL_PACKING[k]
            r0 = first_row[k]
            if k == "o_sgu_b":
                pack[r0:r0 + 4, 0:CHUNK] = g_ref[...].T[0:4, :]
            elif width < LANES:
                pack[r0:r0 + rows, 0:width] = g_ref[...]
            else:
                for j in range(width // LANES):
                    dst = r0 + j * (1 if rows == 1 else 32)
                    pack[dst:dst + rows, :] = g_ref[:, j * LANES:(j + 1) * LANES]

        sibling = (x, y, 1 - c)
        swap = pltpu.make_async_remote_copy(
            src_ref=pack.at[pl.ds(pl.multiple_of((1 - c) * half, 8), half), :], dst_ref=from_sibling,
            send_sem=send_a.at[0], recv_sem=recv_a.at[0], device_id=sibling, device_id_type=MESH)
        swap.start()
        swap.wait()
        for j in range(4):
            rows = pl.ds(pl.multiple_of(c * half + j * piece, 8), piece)
            pack[rows, :] = pack[rows, :] + from_sibling[j * piece:(j + 1) * piece, :]

        def piece_of(chip):
            return pl.ds(pl.multiple_of(c * half + chip * piece, 8), piece)

        def to_chip(k):
            ox, oy = others[k]
            return pltpu.make_async_remote_copy(
                src_ref=pack.at[piece_of(2 * ox + oy), :], dst_ref=from_chips.at[me], send_sem=send_b.at[k],
                recv_sem=recv_b.at[k], device_id=(ox, oy, c), device_id_type=MESH)

        for k in range(3):
            to_chip(k).start()
        from_chips[me] = pack[piece_of(me), :]
        for k, (ox, oy) in enumerate(others):
            landed = from_chips.at[2 * ox + oy]
            pltpu.make_async_remote_copy(src_ref=landed, dst_ref=landed, send_sem=send_b.at[k], recv_sem=recv_b.at[k],
                                         device_id=(ox, oy, c), device_id_type=MESH).wait_recv()
        for k in range(3):
            to_chip(k).wait_send()
        mine = pl.ds(pl.multiple_of(c * half + me * piece, 8), piece)
        total[mine, :] = ((from_chips[0] + from_chips[1]) + from_chips[2]) + from_chips[3]

        def to_same_core(k):
            ox, oy = others[k]
            return pltpu.make_async_remote_copy(
                src_ref=total.at[mine, :], dst_ref=total.at[mine, :], send_sem=send_c.at[k], recv_sem=recv_c.at[k],
                device_id=(ox, oy, c), device_id_type=MESH)

        for k in range(3):
            to_same_core(k).start()
        for k, (ox, oy) in enumerate(others):
            theirs = total.at[piece_of(2 * ox + oy), :]
            pltpu.make_async_remote_copy(src_ref=theirs, dst_ref=theirs, send_sem=send_c.at[k], recv_sem=recv_c.at[k],
                                         device_id=(ox, oy, c), device_id_type=MESH).wait_recv()
        for k in range(3):
            to_same_core(k).wait_send()
        my_half = total.at[pl.ds(pl.multiple_of(c * half, 8), half), :]
        join = pltpu.make_async_remote_copy(src_ref=my_half, dst_ref=my_half, send_sem=send_d.at[0], recv_sem=recv_d.at[0],
                                            device_id=sibling, device_id_type=MESH)
        join.start()
        their_half = total.at[pl.ds(pl.multiple_of((1 - c) * half, 8), half), :]
        pltpu.make_async_remote_copy(src_ref=their_half, dst_ref=their_half, send_sem=send_d.at[0], recv_sem=recv_d.at[0],
                                     device_id=sibling, device_id_type=MESH).wait_recv()
        join.wait_send()

    whole = pl.BlockSpec(memory_space=pltpu.VMEM)
    total = pl.pallas_call(
        body, name="small_allreduce", in_specs=[whole] * n + [ANY], out_specs=whole,
        out_shape=SDS((SMALL_PACKED_ROWS, LANES), f32),
        scratch_shapes=[pltpu.VMEM((SMALL_PACKED_ROWS, LANES), f32), pltpu.VMEM((half, LANES), f32),
                        pltpu.VMEM((N_CHIPS, piece, LANES), f32),
                        pltpu.SemaphoreType.DMA((1,)), pltpu.SemaphoreType.DMA((1,)), pltpu.SemaphoreType.DMA((3,)),
                        pltpu.SemaphoreType.DMA((3,)), pltpu.SemaphoreType.DMA((3,)), pltpu.SemaphoreType.DMA((3,)),
                        pltpu.SemaphoreType.DMA((1,)), pltpu.SemaphoreType.DMA((1,))],
    )(*grads, after)

    def update(*refs):
        total = refs[0]
        w_refs, m_refs, v_refs = refs[1:n + 1], refs[n + 1:2 * n + 1], refs[2 * n + 1:3 * n + 1]
        outs = refs[3 * n + 1:]
        me = 2 * lax.axis_index("x") + lax.axis_index("y")

        def of_chip(candidates):
            value = candidates[0]
            for j in range(1, N_CHIPS):
                value = jnp.where(me == j, candidates[j], value)
            return value

        for i, k in enumerate(names):
            (rows, width), _, (local_rows, local_width) = SMALL_PACKING[k]
            r0 = first_row[k]
            if k == "o_sgu_b":
                g = total[r0:r0 + 4, 0:CHUNK]
            elif k == "e_pool_w":
                for grp in range(4):
                    src = pl.ds(pl.multiple_of(r0 + grp * POOL_CH + me * 64, 8), 64)
                    dst = slice(grp * 64, (grp + 1) * 64)
                    _adamw_rows(total[src, :], i, dst, w_refs, m_refs, v_refs, outs, n)
                continue
            elif k == "o_conv_w":
                g = total[pl.ds(pl.multiple_of(r0 + me * 32, 8), 32), :][0:CONV_K]
            elif width < LANES:
                g = total[r0:r0 + rows, 0:width]
            else:
                lanes = [total[r0 + j:r0 + j + 1, :] for j in range(width // LANES)]
                per_chip = local_width // LANES
                if local_width == width:
                    g = jnp.concatenate(lanes, axis=1)
                elif per_chip == 1:
                    g = of_chip(lanes)
                else:
                    g = of_chip([jnp.concatenate(lanes[j * per_chip:(j + 1) * per_chip], axis=1) for j in range(N_CHIPS)])
            _adamw_rows(g, i, slice(None), w_refs, m_refs, v_refs, outs, n)

    shard_shapes = [SMALL_PACKING[k][2] for k in names]
    out = pl.pallas_call(update, name="small_update", in_specs=[whole] * (3 * n + 1), out_specs=[whole] * (4 * n),
                         out_shape=[SDS(s, f32) for s in shard_shapes] * 4)(total, *ws, *ms, *vs)
    return out[:n], out[n:2 * n], out[2 * n:3 * n], out[3 * n:]


def _adamw_rows(g, i, rows, w_refs, m_refs, v_refs, outs, n):
    w, m, v = w_refs[i][rows, :], m_refs[i][rows, :], v_refs[i][rows, :]
    nm = ADAM_B1 * m + (1.0 - ADAM_B1) * g
    nv = ADAM_B2 * v + (1.0 - ADAM_B2) * (g * g)
    m_hat = nm / (1.0 - ADAM_B1 ** ADAM_STEP)
    v_hat = nv / (1.0 - ADAM_B2 ** ADAM_STEP)
    outs[i][rows, :] = g
    outs[n + i][rows, :] = -ADAM_LR * (m_hat / (jnp.sqrt(v_hat) + ADAM_EPS) + ADAM_WD * w)
    outs[2 * n + i][rows, :] = nm
    outs[3 * n + i][rows, :] = nv


def _pack(arrays, total_rows=None):
    parts = []
    rows = 0
    for a in arrays:
        flat = a.reshape(-1, LANES)
        pad = -flat.shape[0] % 8
        parts.append(jnp.pad(flat, ((0, pad), (0, 0))))
        rows += flat.shape[0] + pad
    if total_rows is not None:
        parts.append(jnp.zeros((total_rows - rows, LANES), arrays[0].dtype))
    return jnp.concatenate(parts, axis=0)


def _unpack(buf, shapes):
    out = []
    row = 0
    lead = buf.shape[:-2]
    for shape in shapes:
        size = 1
        for s in shape:
            size *= s
        rows = size // LANES
        out.append(buf[..., row:row + rows, :].reshape(lead + tuple(shape)))
        row += rows + (-rows % 8)
    return out


BIG = ("e_w_in", "e_w_out", "o_w_in", "o_w_out")
SHARDED_SMALL = {
    "e_pool_w": ((4, 64, 256), 1), "o_pre_norm": ((512,), 0), "o_sgu_norm_g": ((256,), 0), "o_sgu_norm_b": ((256,), 0),
    "o_conv_w": ((31, 256), 1), "o_conv_b": ((256,), 0), "o_conv_norm_g": ((256,), 0), "o_conv_norm_b": ((256,), 0),
    "o_post_norm": ((512,), 0),
}
SMALL_ORDER = ("e_pre_norm", "e_pool_w", "e_pool_scale", "e_post_norm", "o_pre_norm", "o_sgu_norm_g", "o_sgu_norm_b",
               "o_sgu_w", "o_sgu_b", "o_conv_w", "o_conv_b", "o_conv_norm_g", "o_conv_norm_b", "o_post_norm")
ALL_ORDER = ("e_pre_norm", "e_w_in", "e_pool_w", "e_pool_scale", "e_w_out", "e_post_norm", "o_pre_norm", "o_w_in",
             "o_sgu_norm_g", "o_sgu_norm_b", "o_sgu_w", "o_sgu_b", "o_conv_w", "o_conv_b", "o_conv_norm_g",
             "o_conv_norm_b", "o_w_out", "o_post_norm")


def _full_shape(name):
    shape, axis = SHARDED_SMALL[name]
    return tuple(s * N_CHIPS if i == axis else s for i, s in enumerate(shape))


def _from_chips(name, stacked):
    shape, axis = SHARDED_SMALL[name]
    return jnp.moveaxis(stacked, 0, axis).reshape(_full_shape(name))


def kernel(x, e_pre_norm, e_w_in, e_pool_w, e_pool_scale, e_w_out, e_post_norm, o_pre_norm, o_w_in, o_sgu_norm_g, o_sgu_norm_b, o_sgu_w, o_sgu_b, o_conv_w, o_conv_b, o_conv_norm_g, o_conv_norm_b, o_w_out, o_post_norm, loss_target, m_e_pre_norm, m_e_w_in, m_e_pool_w, m_e_pool_scale, m_e_w_out, m_e_post_norm, m_o_pre_norm, m_o_w_in, m_o_sgu_norm_g, m_o_sgu_norm_b, m_o_sgu_w, m_o_sgu_b, m_o_conv_w, m_o_conv_b, m_o_conv_norm_g, m_o_conv_norm_b, m_o_w_out, m_o_post_norm, v_e_pre_norm, v_e_w_in, v_e_pool_w, v_e_pool_scale, v_e_w_out, v_e_post_norm, v_o_pre_norm, v_o_w_in, v_o_sgu_norm_g, v_o_sgu_norm_b, v_o_sgu_w, v_o_sgu_b, v_o_conv_w, v_o_conv_b, v_o_conv_norm_g, v_o_conv_norm_b, v_o_w_out, v_o_post_norm):
    w = dict(e_pre_norm=e_pre_norm, e_w_in=e_w_in, e_pool_w=e_pool_w, e_pool_scale=e_pool_scale, e_w_out=e_w_out,
             e_post_norm=e_post_norm, o_pre_norm=o_pre_norm, o_w_in=o_w_in, o_sgu_norm_g=o_sgu_norm_g,
             o_sgu_norm_b=o_sgu_norm_b, o_sgu_w=o_sgu_w, o_sgu_b=o_sgu_b, o_conv_w=o_conv_w, o_conv_b=o_conv_b,
             o_conv_norm_g=o_conv_norm_g, o_conv_norm_b=o_conv_norm_b, o_w_out=o_w_out, o_post_norm=o_post_norm)
    m = dict(e_pre_norm=m_e_pre_norm, e_w_in=m_e_w_in, e_pool_w=m_e_pool_w, e_pool_scale=m_e_pool_scale,
             e_w_out=m_e_w_out, e_post_norm=m_e_post_norm, o_pre_norm=m_o_pre_norm, o_w_in=m_o_w_in,
             o_sgu_norm_g=m_o_sgu_norm_g, o_sgu_norm_b=m_o_sgu_norm_b, o_sgu_w=m_o_sgu_w, o_sgu_b=m_o_sgu_b,
             o_conv_w=m_o_conv_w, o_conv_b=m_o_conv_b, o_conv_norm_g=m_o_conv_norm_g, o_conv_norm_b=m_o_conv_norm_b,
             o_w_out=m_o_w_out, o_post_norm=m_o_post_norm)
    v = dict(e_pre_norm=v_e_pre_norm, e_w_in=v_e_w_in, e_pool_w=v_e_pool_w, e_pool_scale=v_e_pool_scale,
             e_w_out=v_e_w_out, e_post_norm=v_e_post_norm, o_pre_norm=v_o_pre_norm, o_w_in=v_o_w_in,
             o_sgu_norm_g=v_o_sgu_norm_g, o_sgu_norm_b=v_o_sgu_norm_b, o_sgu_w=v_o_sgu_w, o_sgu_b=v_o_sgu_b,
             o_conv_w=v_o_conv_w, o_conv_b=v_o_conv_b, o_conv_norm_g=v_o_conv_norm_g, o_conv_norm_b=v_o_conv_norm_b,
             o_w_out=v_o_w_out, o_post_norm=v_o_post_norm)
    w, m, v = ({k: a[0] for k, a in d.items()} for d in (w, m, v))
    chip = 2 * lax.axis_index("x") + lax.axis_index("y")

    loss, grad_x, in_flight, small = _step(x[0], loss_target[0], w, chip)

    grads, delta, new_m, new_v = {}, {}, {}, {}

    def rows_of(a):
        return a.reshape(-1, a.shape[-1])

    after = grad_x
    for k in ("o_w_out", "o_w_in", "e_w_out", "small", "e_w_in"):
        if k == "small":
            small_grads = [small[name].reshape(SMALL_PACKING[name][0]) for name in SMALL_ORDER]
            updates = _small_finalize(small_grads, *[[rows_of(d[name]) for name in SMALL_ORDER] for d in (w, m, v)], after)
            for d, arrays in zip((grads, delta, new_m, new_v), updates):
                for name, a in zip(SMALL_ORDER, arrays):
                    d[name] = a.reshape(w[name].shape)
            after = updates[1][0]
            continue
        grads[k], delta[k], new_m[k], new_v[k] = _adamw(w[k], _land(in_flight, k, chip, after), m[k], v[k], f"adamw_{k}")
        after = delta[k]
    loss = lax.psum(loss[0, 0], ("x", "y", "c"))

    outs = [loss, grad_x[None]]
    for d in (grads, delta, new_m, new_v):
        outs += [d[k][None] for k in ALL_ORDER]
    return tuple(outs)
```
